```python
import math
import jax, jax.numpy as jnp
from jax import lax
import numpy as np

D_MODEL = 1024
BATCH = 8
SEQ = 4096
DEPTH = 1

N_MEM = 256
D_FF = 2816
D_POOL = 512
POOL_WINDOWS = (2, 4, 8, 16)
N_POOL_GROUPS = len(POOL_WINDOWS)
POOL_GROUP = D_POOL // N_POOL_GROUPS
D_SSM = 256
SSM_GROUP = 16
N_SSM_GROUPS = D_SSM // SSM_GROUP
SSM_STATE = 64
N_XHEADS = 4
XHEAD_DIM = D_MODEL // N_XHEADS
D_IN = D_POOL + D_SSM + 2 * D_MODEL
EPS = 1e-6

kernel_name = "hybrid_pool_s5_gated_encoder_layer"


def _rms_norm(v, g):
    vf = v.astype(jnp.float32)
    r = lax.rsqrt(jnp.mean(vf * vf, axis=-1, keepdims=True) + EPS)
    return (vf * r).astype(v.dtype) * g


def _swiglu(u, w_gate, w_up, w_down):
    return (jax.nn.silu(u @ w_gate) * (u @ w_up)) @ w_down


def _centred_pool_minus_self(v, window):
    L = v.shape[1]
    left = window // 2
    right = window - 1 - left
    c = jnp.concatenate([jnp.zeros_like(v[:, :1]), jnp.cumsum(v, axis=1)], axis=1)
    t = jnp.arange(L)
    lo = jnp.clip(t - left, 0, L)
    hi = jnp.clip(t + right + 1, 0, L)
    s = jnp.take(c, hi, axis=1) - jnp.take(c, lo, axis=1)
    cnt = (hi - lo).astype(jnp.float32)[None, :, None]
    return s / cnt - v


def _pool_mixer(p, pool_w, pool_scale):
    B_, L, _ = p.shape
    pf = p.astype(jnp.float32)
    groups = [
        _centred_pool_minus_self(pf[..., g * POOL_GROUP:(g + 1) * POOL_GROUP], w)
        for g, w in enumerate(POOL_WINDOWS)
    ]
    pooled = jnp.stack(groups, axis=2).astype(p.dtype)
    mixed = jnp.einsum('blgc,gcd->blgd', pooled, pool_w)
    return mixed.reshape(B_, L, D_POOL) * pool_scale


def _complex_linear_combine(e1, e2):
    a1r, a1i, b1r, b1i = e1
    a2r, a2i, b2r, b2i = e2
    ar = a2r * a1r - a2i * a1i
    ai = a2r * a1i + a2i * a1r
    br = a2r * b1r - a2i * b1i + b2r
    bi = a2r * b1i + a2i * b1r + b2i
    return (ar, ai, br, bi)


def _s5_bidirectional(s, a_re, a_im, log_dt, b_re, b_im, c_re, c_im, d_skip):
    B_, L, _ = s.shape
    uf = s.astype(jnp.float32).reshape(B_, L, N_SSM_GROUPS, SSM_GROUP)
    y = uf * d_skip.astype(jnp.float32).reshape(N_SSM_GROUPS, SSM_GROUP)
    for direction in range(2):
        ar = a_re[direction].astype(jnp.float32)
        ai = a_im[direction].astype(jnp.float32)
        dt = jnp.exp(log_dt[direction].astype(jnp.float32))[:, None]
        mag = jnp.exp(dt * ar)
        ang = dt * ai
        abr = mag * jnp.cos(ang)
        abi = mag * jnp.sin(ang)
        den = ar * ar + ai * ai
        nr = abr - 1.0
        qr = (nr * ar + abi * ai) / den
        qi = (abi * ar - nr * ai) / den
        br = b_re[direction].astype(jnp.float32)
        bi = b_im[direction].astype(jnp.float32)
        bbr = qr[..., None] * br - qi[..., None] * bi
        bbi = qr[..., None] * bi + qi[..., None] * br
        bur = jnp.einsum('gph,blgh->blgp', bbr, uf)
        bui = jnp.einsum('gph,blgh->blgp', bbi, uf)
        a_r = jnp.broadcast_to(abr, bur.shape)
        a_i = jnp.broadcast_to(abi, bur.shape)
        _, _, xr, xi = lax.associative_scan(
            _complex_linear_combine, (a_r, a_i, bur, bui), axis=1, reverse=(direction == 1))
        cr = c_re[direction].astype(jnp.float32)
        ci = c_im[direction].astype(jnp.float32)
        y = y + jnp.einsum('ghp,blgp->blgh', cr, xr) - jnp.einsum('ghp,blgp->blgh', ci, xi)
    return y.reshape(B_, L, D_SSM).astype(s.dtype)


def _cross_attention(u, mem_n, w_q, w_kv, w_xo):
    B_, L, _ = u.shape
    M = mem_n.shape[1]
    q = (u @ w_q).reshape(B_, L, N_XHEADS, XHEAD_DIM)
    kv = mem_n @ w_kv
    k = kv[..., :D_MODEL].reshape(B_, M, N_XHEADS, XHEAD_DIM)
    v = kv[..., D_MODEL:].reshape(B_, M, N_XHEADS, XHEAD_DIM)
    scores = jnp.einsum('blhd,bmhd->bhlm', q.astype(jnp.float32), k.astype(jnp.float32)) / math.sqrt(XHEAD_DIM)
    probs = jax.nn.softmax(scores, axis=-1).astype(u.dtype)
    o = jnp.einsum('bhlm,bmhd->blhd', probs, v).reshape(B_, L, D_MODEL)
    return o @ w_xo


def _fwd_setup_inputs(seed: int = 0) -> dict:
    key = jax.random.key(seed)
    ks = iter(jax.random.split(key, 48))
    f32 = jnp.float32

    def nrm(shape, scale):
        return jax.random.normal(next(ks), shape, f32) * scale

    def gain(shape):
        return 1.0 + 0.02 * jax.random.normal(next(ks), shape, f32)

    L_ = DEPTH
    G, P, H = N_SSM_GROUPS, SSM_STATE, SSM_GROUP
    inp = {}
    inp['x'] = jax.random.normal(next(ks), (BATCH, SEQ, D_MODEL), f32)
    inp['mem'] = jax.random.normal(next(ks), (BATCH, N_MEM, D_MODEL), f32)
    inp['ffn1_norm'] = gain((L_, D_MODEL))
    inp['ffn1_w_gate'] = nrm((L_, D_MODEL, D_FF), D_MODEL ** -0.5)
    inp['ffn1_w_up'] = nrm((L_, D_MODEL, D_FF), D_MODEL ** -0.5)
    inp['ffn1_w_down'] = nrm((L_, D_FF, D_MODEL), D_FF ** -0.5)
    inp['mix_norm'] = gain((L_, D_MODEL))
    inp['w_in'] = nrm((L_, D_MODEL, D_IN), D_MODEL ** -0.5)
    inp['pool_w'] = nrm((L_, N_POOL_GROUPS, POOL_GROUP, POOL_GROUP), POOL_GROUP ** -0.5)
    inp['pool_scale'] = gain((L_, D_POOL))
    inp['w_pool_proj'] = nrm((L_, D_POOL, D_MODEL), D_POOL ** -0.5)
    a_re = -0.5 + 0.01 * jax.random.normal(next(ks), (L_, 2, G, P), f32)
    a_im = math.pi * jnp.arange(P, dtype=f32) + 0.01 * jax.random.normal(next(ks), (L_, 2, G, P), f32)
    inp['ssm_a_re'] = a_re
    inp['ssm_a_im'] = a_im
    inp['ssm_log_dt'] = jax.random.uniform(next(ks), (L_, 2, G), f32, math.log(1e-3), math.log(1e-1))
    inp['ssm_b_re'] = nrm((L_, 2, G, P, H), (2.0 * H) ** -0.5)
    inp['ssm_b_im'] = nrm((L_, 2, G, P, H), (2.0 * H) ** -0.5)
    inp['ssm_c_re'] = nrm((L_, 2, G, H, P), (2.0 * P) ** -0.5)
    inp['ssm_c_im'] = nrm((L_, 2, G, H, P), (2.0 * P) ** -0.5)
    inp['ssm_d'] = nrm((L_, D_SSM), 1.0)
    inp['w_glu_val'] = nrm((L_, D_SSM, D_MODEL), D_SSM ** -0.5)
    inp['w_glu_gate'] = nrm((L_, D_SSM, D_MODEL), D_SSM ** -0.5)
    inp['w_mix_out'] = nrm((L_, D_MODEL, D_MODEL), D_MODEL ** -0.5)
    inp['xattn_norm'] = gain((L_, D_MODEL))
    inp['mem_norm'] = gain((L_, D_MODEL))
    inp['w_q'] = nrm((L_, D_MODEL, D_MODEL), D_MODEL ** -0.5)
    inp['w_kv'] = nrm((L_, D_MODEL, 2 * D_MODEL), D_MODEL ** -0.5)
    inp['w_xo'] = nrm((L_, D_MODEL, D_MODEL), D_MODEL ** -0.5)
    inp['ffn2_norm'] = gain((L_, D_MODEL))
    inp['ffn2_w_gate'] = nrm((L_, D_MODEL, D_FF), D_MODEL ** -0.5)
    inp['ffn2_w_up'] = nrm((L_, D_MODEL, D_FF), D_MODEL ** -0.5)
    inp['ffn2_w_down'] = nrm((L_, D_FF, D_MODEL), D_FF ** -0.5)
    inp['final_norm'] = gain((D_MODEL,))
    return inp


def _fwd_reference(x, mem, ffn1_norm, ffn1_w_gate, ffn1_w_up, ffn1_w_down,
              mix_norm, w_in, pool_w, pool_scale, w_pool_proj,
              ssm_a_re, ssm_a_im, ssm_log_dt, ssm_b_re, ssm_b_im, ssm_c_re, ssm_c_im, ssm_d,
              w_glu_val, w_glu_gate, w_mix_out,
              xattn_norm, mem_norm, w_q, w_kv, w_xo,
              ffn2_norm, ffn2_w_gate, ffn2_w_up, ffn2_w_down, final_norm):
    h = x
    for l in range(DEPTH):
        h = h + 0.5 * _swiglu(_rms_norm(h, ffn1_norm[l]), ffn1_w_gate[l], ffn1_w_up[l], ffn1_w_down[l])

        u = _rms_norm(h, mix_norm[l])
        proj = u @ w_in[l]
        p = proj[..., :D_POOL]
        s = proj[..., D_POOL:D_POOL + D_SSM]
        g_pool = proj[..., D_POOL + D_SSM:D_POOL + D_SSM + D_MODEL]
        g_ssm = proj[..., D_POOL + D_SSM + D_MODEL:]

        z_pool = _pool_mixer(p, pool_w[l], pool_scale[l]) @ w_pool_proj[l]
        y_ssm = jax.nn.gelu(_s5_bidirectional(s, ssm_a_re[l], ssm_a_im[l], ssm_log_dt[l],
                                              ssm_b_re[l], ssm_b_im[l], ssm_c_re[l], ssm_c_im[l], ssm_d[l]))
        z_ssm = (y_ssm @ w_glu_val[l]) * jax.nn.sigmoid(y_ssm @ w_glu_gate[l])

        merged = jax.nn.sigmoid(g_pool) * z_pool + jax.nn.sigmoid(g_ssm) * z_ssm
        h = h + merged @ w_mix_out[l]

        h = h + _cross_attention(_rms_norm(h, xattn_norm[l]), _rms_norm(mem, mem_norm[l]),
                                 w_q[l], w_kv[l], w_xo[l])

        h = h + 0.5 * _swiglu(_rms_norm(h, ffn2_norm[l]), ffn2_w_gate[l], ffn2_w_up[l], ffn2_w_down[l])
    return _rms_norm(h, final_norm)


import jax as _jax
import jax.numpy as _jnp

TWIN_FORMAT = 'train_step'
FWD_PARAMS = ['x', 'mem', 'ffn1_norm', 'ffn1_w_gate', 'ffn1_w_up', 'ffn1_w_down', 'mix_norm', 'w_in', 'pool_w', 'pool_scale', 'w_pool_proj', 'ssm_a_re', 'ssm_a_im', 'ssm_log_dt', 'ssm_b_re', 'ssm_b_im', 'ssm_c_re', 'ssm_c_im', 'ssm_d', 'w_glu_val', 'w_glu_gate', 'w_mix_out', 'xattn_norm', 'mem_norm', 'w_q', 'w_kv', 'w_xo', 'ffn2_norm', 'ffn2_w_gate', 'ffn2_w_up', 'ffn2_w_down', 'final_norm']
TWIN_WEIGHTS = ['ffn1_norm', 'ffn1_w_gate', 'ffn1_w_up', 'ffn1_w_down', 'mix_norm', 'w_in', 'pool_w', 'pool_scale', 'w_pool_proj', 'ssm_a_re', 'ssm_a_im', 'ssm_log_dt', 'ssm_b_re', 'ssm_b_im', 'ssm_c_re', 'ssm_c_im', 'ssm_d', 'w_glu_val', 'w_glu_gate', 'w_mix_out', 'xattn_norm', 'mem_norm', 'w_q', 'w_kv', 'w_xo', 'ffn2_norm', 'ffn2_w_gate', 'ffn2_w_up', 'ffn2_w_down', 'final_norm']
TWIN_DIFF_INPUT = 'x'
TWIN_INPUTS = ['x', 'mem', 'ffn1_norm', 'ffn1_w_gate', 'ffn1_w_up', 'ffn1_w_down', 'mix_norm', 'w_in', 'pool_w', 'pool_scale', 'w_pool_proj', 'ssm_a_re', 'ssm_a_im', 'ssm_log_dt', 'ssm_b_re', 'ssm_b_im', 'ssm_c_re', 'ssm_c_im', 'ssm_d', 'w_glu_val', 'w_glu_gate', 'w_mix_out', 'xattn_norm', 'mem_norm', 'w_q', 'w_kv', 'w_xo', 'ffn2_norm', 'ffn2_w_gate', 'ffn2_w_up', 'ffn2_w_down', 'final_norm', 'loss_target', 'm_ffn1_norm', 'm_ffn1_w_gate', 'm_ffn1_w_up', 'm_ffn1_w_down', 'm_mix_norm', 'm_w_in', 'm_pool_w', 'm_pool_scale', 'm_w_pool_proj', 'm_ssm_a_re', 'm_ssm_a_im', 'm_ssm_log_dt', 'm_ssm_b_re', 'm_ssm_b_im', 'm_ssm_c_re', 'm_ssm_c_im', 'm_ssm_d', 'm_w_glu_val', 'm_w_glu_gate', 'm_w_mix_out', 'm_xattn_norm', 'm_mem_norm', 'm_w_q', 'm_w_kv', 'm_w_xo', 'm_ffn2_norm', 'm_ffn2_w_gate', 'm_ffn2_w_up', 'm_ffn2_w_down', 'm_final_norm', 'v_ffn1_norm', 'v_ffn1_w_gate', 'v_ffn1_w_up', 'v_ffn1_w_down', 'v_mix_norm', 'v_w_in', 'v_pool_w', 'v_pool_scale', 'v_w_pool_proj', 'v_ssm_a_re', 'v_ssm_a_im', 'v_ssm_log_dt', 'v_ssm_b_re', 'v_ssm_b_im', 'v_ssm_c_re', 'v_ssm_c_im', 'v_ssm_d', 'v_w_glu_val', 'v_w_glu_gate', 'v_w_mix_out', 'v_xattn_norm', 'v_mem_norm', 'v_w_q', 'v_w_kv', 'v_w_xo', 'v_ffn2_norm', 'v_ffn2_w_gate', 'v_ffn2_w_up', 'v_ffn2_w_down', 'v_final_norm']
TWIN_OUTPUTS = ['loss', 'grad_x', 'grad_ffn1_norm', 'grad_ffn1_w_gate', 'grad_ffn1_w_up', 'grad_ffn1_w_down', 'grad_mix_norm', 'grad_w_in', 'grad_pool_w', 'grad_pool_scale', 'grad_w_pool_proj', 'grad_ssm_a_re', 'grad_ssm_a_im', 'grad_ssm_log_dt', 'grad_ssm_b_re', 'grad_ssm_b_im', 'grad_ssm_c_re', 'grad_ssm_c_im', 'grad_ssm_d', 'grad_w_glu_val', 'grad_w_glu_gate', 'grad_w_mix_out', 'grad_xattn_norm', 'grad_mem_norm', 'grad_w_q', 'grad_w_kv', 'grad_w_xo', 'grad_ffn2_norm', 'grad_ffn2_w_gate', 'grad_ffn2_w_up', 'grad_ffn2_w_down', 'grad_final_norm', 'delta_ffn1_norm', 'delta_ffn1_w_gate', 'delta_ffn1_w_up', 'delta_ffn1_w_down', 'delta_mix_norm', 'delta_w_in', 'delta_pool_w', 'delta_pool_scale', 'delta_w_pool_proj', 'delta_ssm_a_re', 'delta_ssm_a_im', 'delta_ssm_log_dt', 'delta_ssm_b_re', 'delta_ssm_b_im', 'delta_ssm_c_re', 'delta_ssm_c_im', 'delta_ssm_d', 'delta_w_glu_val', 'delta_w_glu_gate', 'delta_w_mix_out', 'delta_xattn_norm', 'delta_mem_norm', 'delta_w_q', 'delta_w_kv', 'delta_w_xo', 'delta_ffn2_norm', 'delta_ffn2_w_gate', 'delta_ffn2_w_up', 'delta_ffn2_w_down', 'delta_final_norm', 'new_m_ffn1_norm', 'new_m_ffn1_w_gate', 'new_m_ffn1_w_up', 'new_m_ffn1_w_down', 'new_m_mix_norm', 'new_m_w_in', 'new_m_pool_w', 'new_m_pool_scale', 'new_m_w_pool_proj', 'new_m_ssm_a_re', 'new_m_ssm_a_im', 'new_m_ssm_log_dt', 'new_m_ssm_b_re', 'new_m_ssm_b_im', 'new_m_ssm_c_re', 'new_m_ssm_c_im', 'new_m_ssm_d', 'new_m_w_glu_val', 'new_m_w_glu_gate', 'new_m_w_mix_out', 'new_m_xattn_norm', 'new_m_mem_norm', 'new_m_w_q', 'new_m_w_kv', 'new_m_w_xo', 'new_m_ffn2_norm', 'new_m_ffn2_w_gate', 'new_m_ffn2_w_up', 'new_m_ffn2_w_down', 'new_m_final_norm', 'new_v_ffn1_norm', 'new_v_ffn1_w_gate', 'new_v_ffn1_w_up', 'new_v_ffn1_w_down', 'new_v_mix_norm', 'new_v_w_in', 'new_v_pool_w', 'new_v_pool_scale', 'new_v_w_pool_proj', 'new_v_ssm_a_re', 'new_v_ssm_a_im', 'new_v_ssm_log_dt', 'new_v_ssm_b_re', 'new_v_ssm_b_im', 'new_v_ssm_c_re', 'new_v_ssm_c_im', 'new_v_ssm_d', 'new_v_w_glu_val', 'new_v_w_glu_gate', 'new_v_w_mix_out', 'new_v_xattn_norm', 'new_v_mem_norm', 'new_v_w_q', 'new_v_w_kv', 'new_v_w_xo', 'new_v_ffn2_norm', 'new_v_ffn2_w_gate', 'new_v_ffn2_w_up', 'new_v_ffn2_w_down', 'new_v_final_norm']
TWIN_LEAF_KINDS = {'loss': 'loss', 'grad_x': 'grad_x', 'grad_ffn1_norm': 'grad_w', 'grad_ffn1_w_gate': 'grad_w', 'grad_ffn1_w_up': 'grad_w', 'grad_ffn1_w_down': 'grad_w', 'grad_mix_norm': 'grad_w', 'grad_w_in': 'grad_w', 'grad_pool_w': 'grad_w', 'grad_pool_scale': 'grad_w', 'grad_w_pool_proj': 'grad_w', 'grad_ssm_a_re': 'grad_w', 'grad_ssm_a_im': 'grad_w', 'grad_ssm_log_dt': 'grad_w', 'grad_ssm_b_re': 'grad_w', 'grad_ssm_b_im': 'grad_w', 'grad_ssm_c_re': 'grad_w', 'grad_ssm_c_im': 'grad_w', 'grad_ssm_d': 'grad_w', 'grad_w_glu_val': 'grad_w', 'grad_w_glu_gate': 'grad_w', 'grad_w_mix_out': 'grad_w', 'grad_xattn_norm': 'grad_w', 'grad_mem_norm': 'grad_w', 'grad_w_q': 'grad_w', 'grad_w_kv': 'grad_w', 'grad_w_xo': 'grad_w', 'grad_ffn2_norm': 'grad_w', 'grad_ffn2_w_gate': 'grad_w', 'grad_ffn2_w_up': 'grad_w', 'grad_ffn2_w_down': 'grad_w', 'grad_final_norm': 'grad_w', 'delta_ffn1_norm': 'delta_w', 'delta_ffn1_w_gate': 'delta_w', 'delta_ffn1_w_up': 'delta_w', 'delta_ffn1_w_down': 'delta_w', 'delta_mix_norm': 'delta_w', 'delta_w_in': 'delta_w', 'delta_pool_w': 'delta_w', 'delta_pool_scale': 'delta_w', 'delta_w_pool_proj': 'delta_w', 'delta_ssm_a_re': 'delta_w', 'delta_ssm_a_im': 'delta_w', 'delta_ssm_log_dt': 'delta_w', 'delta_ssm_b_re': 'delta_w', 'delta_ssm_b_im': 'delta_w', 'delta_ssm_c_re': 'delta_w', 'delta_ssm_c_im': 'delta_w', 'delta_ssm_d': 'delta_w', 'delta_w_glu_val': 'delta_w', 'delta_w_glu_gate': 'delta_w', 'delta_w_mix_out': 'delta_w', 'delta_xattn_norm': 'delta_w', 'delta_mem_norm': 'delta_w', 'delta_w_q': 'delta_w', 'delta_w_kv': 'delta_w', 'delta_w_xo': 'delta_w', 'delta_ffn2_norm': 'delta_w', 'delta_ffn2_w_gate': 'delta_w', 'delta_ffn2_w_up': 'delta_w', 'delta_ffn2_w_down': 'delta_w', 'delta_final_norm': 'delta_w', 'new_m_ffn1_norm': 'new_m', 'new_m_ffn1_w_gate': 'new_m', 'new_m_ffn1_w_up': 'new_m', 'new_m_ffn1_w_down': 'new_m', 'new_m_mix_norm': 'new_m', 'new_m_w_in': 'new_m', 'new_m_pool_w': 'new_m', 'new_m_pool_scale': 'new_m', 'new_m_w_pool_proj': 'new_m', 'new_m_ssm_a_re': 'new_m', 'new_m_ssm_a_im': 'new_m', 'new_m_ssm_log_dt': 'new_m', 'new_m_ssm_b_re': 'new_m', 'new_m_ssm_b_im': 'new_m', 'new_m_ssm_c_re': 'new_m', 'new_m_ssm_c_im': 'new_m', 'new_m_ssm_d': 'new_m', 'new_m_w_glu_val': 'new_m', 'new_m_w_glu_gate': 'new_m', 'new_m_w_mix_out': 'new_m', 'new_m_xattn_norm': 'new_m', 'new_m_mem_norm': 'new_m', 'new_m_w_q': 'new_m', 'new_m_w_kv': 'new_m', 'new_m_w_xo': 'new_m', 'new_m_ffn2_norm': 'new_m', 'new_m_ffn2_w_gate': 'new_m', 'new_m_ffn2_w_up': 'new_m', 'new_m_ffn2_w_down': 'new_m', 'new_m_final_norm': 'new_m', 'new_v_ffn1_norm': 'new_v', 'new_v_ffn1_w_gate': 'new_v', 'new_v_ffn1_w_up': 'new_v', 'new_v_ffn1_w_down': 'new_v', 'new_v_mix_norm': 'new_v', 'new_v_w_in': 'new_v', 'new_v_pool_w': 'new_v', 'new_v_pool_scale': 'new_v', 'new_v_w_pool_proj': 'new_v', 'new_v_ssm_a_re': 'new_v', 'new_v_ssm_a_im': 'new_v', 'new_v_ssm_log_dt': 'new_v', 'new_v_ssm_b_re': 'new_v', 'new_v_ssm_b_im': 'new_v', 'new_v_ssm_c_re': 'new_v', 'new_v_ssm_c_im': 'new_v', 'new_v_ssm_d': 'new_v', 'new_v_w_glu_val': 'new_v', 'new_v_w_glu_gate': 'new_v', 'new_v_w_mix_out': 'new_v', 'new_v_xattn_norm': 'new_v', 'new_v_mem_norm': 'new_v', 'new_v_w_q': 'new_v', 'new_v_w_kv': 'new_v', 'new_v_w_xo': 'new_v', 'new_v_ffn2_norm': 'new_v', 'new_v_ffn2_w_gate': 'new_v', 'new_v_ffn2_w_up': 'new_v', 'new_v_ffn2_w_down': 'new_v', 'new_v_final_norm': 'new_v'}


def _forward(args):
    return _fwd_reference(*[args[k] for k in FWD_PARAMS])


def _output_shape():
    out = _jax.eval_shape(lambda: _forward(_fwd_setup_inputs(0)))
    return out.shape, out.dtype

N_MICROBATCH = 1
ADAM_LR = 0.001
ADAM_B1 = 0.9
ADAM_B2 = 0.999
ADAM_EPS = 1e-08
ADAM_WD = 0.01
ADAM_STEP = 10
PER_EXAMPLE_BATCH_AXIS = {'x': 0, 'mem': 0, 'loss_target': 0}
SHARED_INPUTS = []
_WEIGHT_DTYPES = {'ffn1_norm': _jnp.float32, 'ffn1_w_gate': _jnp.float32, 'ffn1_w_up': _jnp.float32, 'ffn1_w_down': _jnp.float32, 'mix_norm': _jnp.float32, 'w_in': _jnp.float32, 'pool_w': _jnp.float32, 'pool_scale': _jnp.float32, 'w_pool_proj': _jnp.float32, 'ssm_a_re': _jnp.float32, 'ssm_a_im': _jnp.float32, 'ssm_log_dt': _jnp.float32, 'ssm_b_re': _jnp.float32, 'ssm_b_im': _jnp.float32, 'ssm_c_re': _jnp.float32, 'ssm_c_im': _jnp.float32, 'ssm_d': _jnp.float32, 'w_glu_val': _jnp.float32, 'w_glu_gate': _jnp.float32, 'w_mix_out': _jnp.float32, 'xattn_norm': _jnp.float32, 'mem_norm': _jnp.float32, 'w_q': _jnp.float32, 'w_kv': _jnp.float32, 'w_xo': _jnp.float32, 'ffn2_norm': _jnp.float32, 'ffn2_w_gate': _jnp.float32, 'ffn2_w_up': _jnp.float32, 'ffn2_w_down': _jnp.float32, 'final_norm': _jnp.float32}
MOMENT_SCALE = {'ffn1_norm': 7.785124e-02, 'ffn1_w_gate': 3.309312e-02, 'ffn1_w_up': 3.210060e-02, 'ffn1_w_down': 5.310671e-02, 'mix_norm': 9.083585e-02, 'w_in': 5.234799e-02, 'pool_w': 1.055933e-01, 'pool_scale': 1.138935e-01, 'w_pool_proj': 7.438316e-02, 'ssm_a_re': 2.787469e-03, 'ssm_a_im': 3.404608e-03, 'ssm_log_dt': 4.026050e+00, 'ssm_b_re': 2.112511e-03, 'ssm_b_im': 2.135030e-03, 'ssm_c_re': 4.349738e-03, 'ssm_c_im': 4.188195e-03, 'ssm_d': 6.351159e-02, 'w_glu_val': 3.012176e-02, 'w_glu_gate': 8.876288e-03, 'w_mix_out': 8.023321e-02, 'xattn_norm': 1.669474e-02, 'mem_norm': 2.384439e-02, 'w_q': 1.635833e-02, 'w_kv': 1.662187e-02, 'w_xo': 1.659220e-02, 'ffn2_norm': 6.533182e-02, 'ffn2_w_gate': 2.730474e-02, 'ffn2_w_up': 2.646254e-02, 'ffn2_w_down': 4.413238e-02, 'final_norm': 3.194131e+01}


def _to_microbatches(a, axis):
    t = _jnp.moveaxis(a, axis, 0)
    t = t.reshape((N_MICROBATCH, t.shape[0] // N_MICROBATCH) + t.shape[1:])
    return _jnp.moveaxis(t, 1, axis + 1)


def setup_inputs(seed: int = 0) -> dict:
    inp = _fwd_setup_inputs(seed)
    key = _jax.random.fold_in(_jax.random.key(seed), 7919)
    shape, _ = _output_shape()
    out = dict(inp)
    out["loss_target"] = _jax.random.normal(_jax.random.fold_in(key, 0), shape, _jnp.float32)
    for i, name in enumerate(TWIN_WEIGHTS):
        w = inp[name].astype(_jnp.float32)
        if MOMENT_SCALE is None:
            s = _jnp.sqrt(_jnp.mean(_jnp.square(w)) + 1e-30)
        else:
            s = MOMENT_SCALE[name]
        km, kv = _jax.random.split(_jax.random.fold_in(key, i + 1))
        out[name] = w
        out["m_" + name] = s * _jax.random.normal(km, w.shape, _jnp.float32)
        out["v_" + name] = (s * s) * _jax.random.uniform(kv, w.shape, _jnp.float32, 0.5, 1.5)
    if N_MICROBATCH > 1:
        for name, axis in PER_EXAMPLE_BATCH_AXIS.items():
            out[name] = _to_microbatches(out[name], axis)
    return {'x': out['x'], 'mem': out['mem'], 'ffn1_norm': out['ffn1_norm'], 'ffn1_w_gate': out['ffn1_w_gate'], 'ffn1_w_up': out['ffn1_w_up'], 'ffn1_w_down': out['ffn1_w_down'], 'mix_norm': out['mix_norm'], 'w_in': out['w_in'], 'pool_w': out['pool_w'], 'pool_scale': out['pool_scale'], 'w_pool_proj': out['w_pool_proj'], 'ssm_a_re': out['ssm_a_re'], 'ssm_a_im': out['ssm_a_im'], 'ssm_log_dt': out['ssm_log_dt'], 'ssm_b_re': out['ssm_b_re'], 'ssm_b_im': out['ssm_b_im'], 'ssm_c_re': out['ssm_c_re'], 'ssm_c_im': out['ssm_c_im'], 'ssm_d': out['ssm_d'], 'w_glu_val': out['w_glu_val'], 'w_glu_gate': out['w_glu_gate'], 'w_mix_out': out['w_mix_out'], 'xattn_norm': out['xattn_norm'], 'mem_norm': out['mem_norm'], 'w_q': out['w_q'], 'w_kv': out['w_kv'], 'w_xo': out['w_xo'], 'ffn2_norm': out['ffn2_norm'], 'ffn2_w_gate': out['ffn2_w_gate'], 'ffn2_w_up': out['ffn2_w_up'], 'ffn2_w_down': out['ffn2_w_down'], 'final_norm': out['final_norm'], 'loss_target': out['loss_target'], 'm_ffn1_norm': out['m_ffn1_norm'], 'm_ffn1_w_gate': out['m_ffn1_w_gate'], 'm_ffn1_w_up': out['m_ffn1_w_up'], 'm_ffn1_w_down': out['m_ffn1_w_down'], 'm_mix_norm': out['m_mix_norm'], 'm_w_in': out['m_w_in'], 'm_pool_w': out['m_pool_w'], 'm_pool_scale': out['m_pool_scale'], 'm_w_pool_proj': out['m_w_pool_proj'], 'm_ssm_a_re': out['m_ssm_a_re'], 'm_ssm_a_im': out['m_ssm_a_im'], 'm_ssm_log_dt': out['m_ssm_log_dt'], 'm_ssm_b_re': out['m_ssm_b_re'], 'm_ssm_b_im': out['m_ssm_b_im'], 'm_ssm_c_re': out['m_ssm_c_re'], 'm_ssm_c_im': out['m_ssm_c_im'], 'm_ssm_d': out['m_ssm_d'], 'm_w_glu_val': out['m_w_glu_val'], 'm_w_glu_gate': out['m_w_glu_gate'], 'm_w_mix_out': out['m_w_mix_out'], 'm_xattn_norm': out['m_xattn_norm'], 'm_mem_norm': out['m_mem_norm'], 'm_w_q': out['m_w_q'], 'm_w_kv': out['m_w_kv'], 'm_w_xo': out['m_w_xo'], 'm_ffn2_norm': out['m_ffn2_norm'], 'm_ffn2_w_gate': out['m_ffn2_w_gate'], 'm_ffn2_w_up': out['m_ffn2_w_up'], 'm_ffn2_w_down': out['m_ffn2_w_down'], 'm_final_norm': out['m_final_norm'], 'v_ffn1_norm': out['v_ffn1_norm'], 'v_ffn1_w_gate': out['v_ffn1_w_gate'], 'v_ffn1_w_up': out['v_ffn1_w_up'], 'v_ffn1_w_down': out['v_ffn1_w_down'], 'v_mix_norm': out['v_mix_norm'], 'v_w_in': out['v_w_in'], 'v_pool_w': out['v_pool_w'], 'v_pool_scale': out['v_pool_scale'], 'v_w_pool_proj': out['v_w_pool_proj'], 'v_ssm_a_re': out['v_ssm_a_re'], 'v_ssm_a_im': out['v_ssm_a_im'], 'v_ssm_log_dt': out['v_ssm_log_dt'], 'v_ssm_b_re': out['v_ssm_b_re'], 'v_ssm_b_im': out['v_ssm_b_im'], 'v_ssm_c_re': out['v_ssm_c_re'], 'v_ssm_c_im': out['v_ssm_c_im'], 'v_ssm_d': out['v_ssm_d'], 'v_w_glu_val': out['v_w_glu_val'], 'v_w_glu_gate': out['v_w_glu_gate'], 'v_w_mix_out': out['v_w_mix_out'], 'v_xattn_norm': out['v_xattn_norm'], 'v_mem_norm': out['v_mem_norm'], 'v_w_q': out['v_w_q'], 'v_w_kv': out['v_w_kv'], 'v_w_xo': out['v_w_xo'], 'v_ffn2_norm': out['v_ffn2_norm'], 'v_ffn2_w_gate': out['v_ffn2_w_gate'], 'v_ffn2_w_up': out['v_ffn2_w_up'], 'v_ffn2_w_down': out['v_ffn2_w_down'], 'v_final_norm': out['v_final_norm']}


def _loss(weights, diff, rest, loss_target):
    with _jax.named_scope("forward"):
        args = {**rest, TWIN_DIFF_INPUT: diff, **{k: w.astype(_WEIGHT_DTYPES[k]) for k, w in weights.items()}}
        y = _forward(args)
    with _jax.named_scope("loss_head"):
        err = _jnp.square(y.astype(_jnp.float32) - loss_target)
        return 0.5 * _jnp.sum(_jnp.mean(err, axis=-1)) if err.ndim else 0.5 * err


def _adamw(w, g, m, v):
    m = ADAM_B1 * m + (1.0 - ADAM_B1) * g
    v = ADAM_B2 * v + (1.0 - ADAM_B2) * _jnp.square(g)
    m_hat = m / (1.0 - ADAM_B1 ** ADAM_STEP)
    v_hat = v / (1.0 - ADAM_B2 ** ADAM_STEP)
    delta = -ADAM_LR * (m_hat / (_jnp.sqrt(v_hat) + ADAM_EPS) + ADAM_WD * w)
    return delta, m, v


def reference(x, mem, ffn1_norm, ffn1_w_gate, ffn1_w_up, ffn1_w_down, mix_norm, w_in, pool_w, pool_scale, w_pool_proj, ssm_a_re, ssm_a_im, ssm_log_dt, ssm_b_re, ssm_b_im, ssm_c_re, ssm_c_im, ssm_d, w_glu_val, w_glu_gate, w_mix_out, xattn_norm, mem_norm, w_q, w_kv, w_xo, ffn2_norm, ffn2_w_gate, ffn2_w_up, ffn2_w_down, final_norm, loss_target, m_ffn1_norm, m_ffn1_w_gate, m_ffn1_w_up, m_ffn1_w_down, m_mix_norm, m_w_in, m_pool_w, m_pool_scale, m_w_pool_proj, m_ssm_a_re, m_ssm_a_im, m_ssm_log_dt, m_ssm_b_re, m_ssm_b_im, m_ssm_c_re, m_ssm_c_im, m_ssm_d, m_w_glu_val, m_w_glu_gate, m_w_mix_out, m_xattn_norm, m_mem_norm, m_w_q, m_w_kv, m_w_xo, m_ffn2_norm, m_ffn2_w_gate, m_ffn2_w_up, m_ffn2_w_down, m_final_norm, v_ffn1_norm, v_ffn1_w_gate, v_ffn1_w_up, v_ffn1_w_down, v_mix_norm, v_w_in, v_pool_w, v_pool_scale, v_w_pool_proj, v_ssm_a_re, v_ssm_a_im, v_ssm_log_dt, v_ssm_b_re, v_ssm_b_im, v_ssm_c_re, v_ssm_c_im, v_ssm_d, v_w_glu_val, v_w_glu_gate, v_w_mix_out, v_xattn_norm, v_mem_norm, v_w_q, v_w_kv, v_w_xo, v_ffn2_norm, v_ffn2_w_gate, v_ffn2_w_up, v_ffn2_w_down, v_final_norm):
    given = dict(x=x, mem=mem, ffn1_norm=ffn1_norm, ffn1_w_gate=ffn1_w_gate, ffn1_w_up=ffn1_w_up, ffn1_w_down=ffn1_w_down, mix_norm=mix_norm, w_in=w_in, pool_w=pool_w, pool_scale=pool_scale, w_pool_proj=w_pool_proj, ssm_a_re=ssm_a_re, ssm_a_im=ssm_a_im, ssm_log_dt=ssm_log_dt, ssm_b_re=ssm_b_re, ssm_b_im=ssm_b_im, ssm_c_re=ssm_c_re, ssm_c_im=ssm_c_im, ssm_d=ssm_d, w_glu_val=w_glu_val, w_glu_gate=w_glu_gate, w_mix_out=w_mix_out, xattn_norm=xattn_norm, mem_norm=mem_norm, w_q=w_q, w_kv=w_kv, w_xo=w_xo, ffn2_norm=ffn2_norm, ffn2_w_gate=ffn2_w_gate, ffn2_w_up=ffn2_w_up, ffn2_w_down=ffn2_w_down, final_norm=final_norm, loss_target=loss_target, m_ffn1_norm=m_ffn1_norm, m_ffn1_w_gate=m_ffn1_w_gate, m_ffn1_w_up=m_ffn1_w_up, m_ffn1_w_down=m_ffn1_w_down, m_mix_norm=m_mix_norm, m_w_in=m_w_in, m_pool_w=m_pool_w, m_pool_scale=m_pool_scale, m_w_pool_proj=m_w_pool_proj, m_ssm_a_re=m_ssm_a_re, m_ssm_a_im=m_ssm_a_im, m_ssm_log_dt=m_ssm_log_dt, m_ssm_b_re=m_ssm_b_re, m_ssm_b_im=m_ssm_b_im, m_ssm_c_re=m_ssm_c_re, m_ssm_c_im=m_ssm_c_im, m_ssm_d=m_ssm_d, m_w_glu_val=m_w_glu_val, m_w_glu_gate=m_w_glu_gate, m_w_mix_out=m_w_mix_out, m_xattn_norm=m_xattn_norm, m_mem_norm=m_mem_norm, m_w_q=m_w_q, m_w_kv=m_w_kv, m_w_xo=m_w_xo, m_ffn2_norm=m_ffn2_norm, m_ffn2_w_gate=m_ffn2_w_gate, m_ffn2_w_up=m_ffn2_w_up, m_ffn2_w_down=m_ffn2_w_down, m_final_norm=m_final_norm, v_ffn1_norm=v_ffn1_norm, v_ffn1_w_gate=v_ffn1_w_gate, v_ffn1_w_up=v_ffn1_w_up, v_ffn1_w_down=v_ffn1_w_down, v_mix_norm=v_mix_norm, v_w_in=v_w_in, v_pool_w=v_pool_w, v_pool_scale=v_pool_scale, v_w_pool_proj=v_w_pool_proj, v_ssm_a_re=v_ssm_a_re, v_ssm_a_im=v_ssm_a_im, v_ssm_log_dt=v_ssm_log_dt, v_ssm_b_re=v_ssm_b_re, v_ssm_b_im=v_ssm_b_im, v_ssm_c_re=v_ssm_c_re, v_ssm_c_im=v_ssm_c_im, v_ssm_d=v_ssm_d, v_w_glu_val=v_w_glu_val, v_w_glu_gate=v_w_glu_gate, v_w_mix_out=v_w_mix_out, v_xattn_norm=v_xattn_norm, v_mem_norm=v_mem_norm, v_w_q=v_w_q, v_w_kv=v_w_kv, v_w_xo=v_w_xo, v_ffn2_norm=v_ffn2_norm, v_ffn2_w_gate=v_ffn2_w_gate, v_ffn2_w_up=v_ffn2_w_up, v_ffn2_w_down=v_ffn2_w_down, v_final_norm=v_final_norm)
    weights = {n: given[n] for n in TWIN_WEIGHTS}
    shared = {n: given[n] for n in SHARED_INPUTS}
    per_example = {n: given[n] for n in ['x', 'mem']}
    grad_fn = _jax.value_and_grad(_loss, argnums=(0, 1))

    def one_microbatch(ex, loss_target):
        ex = dict(ex)
        diff = ex.pop(TWIN_DIFF_INPUT)
        return grad_fn(weights, diff, {**shared, **ex}, loss_target)

    if N_MICROBATCH == 1:
        loss, (grad_w, grad_x) = one_microbatch(per_example, given["loss_target"])
    else:
        def body(carry, xs):
            loss_sum, grad_sum = carry
            l_k, (gw_k, gx_k) = one_microbatch(xs[0], xs[1])
            with _jax.named_scope("update"):
                return (loss_sum + l_k, _jax.tree.map(_jnp.add, grad_sum, gw_k)), gx_k

        init = (_jnp.zeros((), _jnp.float32), _jax.tree.map(_jnp.zeros_like, weights))
        (loss, grad_w), grad_x = _jax.lax.scan(body, init, (per_example, given["loss_target"]))
    with _jax.named_scope("update"):
        delta_w, new_m, new_v = {}, {}, {}
        for n in TWIN_WEIGHTS:
            delta_w[n], new_m[n], new_v[n] = _adamw(weights[n], grad_w[n], given["m_" + n], given["v_" + n])
    return (loss, grad_x, *[grad_w[n] for n in TWIN_WEIGHTS], *[delta_w[n] for n in TWIN_WEIGHTS],
            *[new_m[n] for n in TWIN_WEIGHTS], *[new_v[n] for n in TWIN_WEIGHTS])
```

```python
import functools
import math

import jax
import jax.numpy as jnp
from jax import lax
from jax.experimental import pallas as pl
from jax.experimental.pallas import tpu as pltpu

F32 = jnp.float32
BF16 = jnp.bfloat16
EPS = 1e-6
N_XHEADS = 4
POOL_WINDOWS = (2, 4, 8, 16)
ADAM_LR = 0.001
ADAM_B1 = 0.9
ADAM_B2 = 0.999
ADAM_EPS = 1e-08
ADAM_WD = 0.01
ADAM_STEP = 10
N_DEV = 8
VMEM_LIMIT_V7X = 48 * 1024 * 1024
LANES = 128
SUBLANES = 8
POOL_PAD = 16
MESH = pl.DeviceIdType.MESH
ANY = pl.BlockSpec(memory_space=pl.ANY)

_DIMS = {
    "nt": (((1,), (1,)), ((), ())),
    "nn": (((1,), (0,)), ((), ())),
    "tn": (((0,), (0,)), ((), ())),
}


def _pick(dim, pref, mult=LANES):
    if dim <= pref:
        return dim
    for t in range(pref - pref % mult, 0, -mult):
        if dim % t == 0:
            return t
    return dim


def _params(sem):
    return pltpu.CompilerParams(dimension_semantics=sem, vmem_limit_bytes=VMEM_LIMIT_V7X)


def _tile(tm, tn, coff=0):
    return pl.BlockSpec((tm, tn), lambda i, j: (i, j + coff))


def _rowvec(tn, coff=0):
    return pl.BlockSpec((1, tn), lambda i, j: (0, j + coff))


def _out(m, n, dtype):
    return jax.ShapeDtypeStruct((m, n), dtype)


def _mm(name, form, a_list, b_list, groups, m, n, tm, tn, extras, epilogue, outs):
    na, nb, ne = len(a_list), len(b_list), len(extras)

    def a_spec(a):
        if form == "tn":
            return pl.BlockSpec((a.shape[0], tm), lambda i, j: (0, i))
        return pl.BlockSpec((tm, a.shape[1]), lambda i, j: (i, 0))

    def b_spec(b):
        if form == "nt":
            return pl.BlockSpec((tn, b.shape[1]), lambda i, j: (j, 0))
        return pl.BlockSpec((b.shape[0], tn), lambda i, j: (0, j))

    def body(*refs):
        a_refs, b_refs = refs[:na], refs[na:na + nb]
        e_refs, o_refs = refs[na + nb:na + nb + ne], refs[na + nb + ne:]
        a_vals, b_vals, accs = {}, {}, []
        for group in groups:
            acc = None
            for ai, bi in group:
                if ai not in a_vals:
                    a_vals[ai] = a_refs[ai][...].astype(BF16)
                if bi not in b_vals:
                    b_vals[bi] = b_refs[bi][...].astype(BF16)
                d = lax.dot_general(a_vals[ai], b_vals[bi], _DIMS[form], preferred_element_type=F32)
                acc = d if acc is None else acc + d
            accs.append(acc)
        res = epilogue(accs, *[e[...] for e in e_refs])
        for o_ref, r in zip(o_refs, res):
            o_ref[...] = r.astype(o_ref.dtype)

    out_specs = [_tile(tm, tn) if s is None else s for _, s in outs]
    res = pl.pallas_call(
        body, name=name, grid=(m // tm, n // tn),
        in_specs=[a_spec(a) for a in a_list] + [b_spec(b) for b in b_list] + [s for _, s in extras],
        out_specs=out_specs, out_shape=[o for o, _ in outs],
        compiler_params=_params(("parallel", "parallel")),
    )(*a_list, *b_list, *[e for e, _ in extras])
    return res


def _mm1(name, form, a, b, m, n, tm, tn, dtype, scale=None):
    epi = (lambda accs: (accs[0],)) if scale is None else (lambda accs: (accs[0] * scale,))
    return _mm(name, form, [a], [b], [[(0, 0)]], m, n, tm, tn, [], epi, [(_out(m, n, dtype), None)])[0]


def _rms_fwd(name, h, g):
    t, d = h.shape
    tm = _pick(t, 512, SUBLANES)

    def body(h_ref, g_ref, n_ref):
        hv = h_ref[...]
        r = lax.rsqrt(jnp.mean(hv * hv, axis=-1, keepdims=True) + EPS)
        n_ref[...] = ((hv * r) * g_ref[...]).astype(BF16)

    return pl.pallas_call(
        body, name=name, grid=(t // tm,),
        in_specs=[pl.BlockSpec((tm, d), lambda i: (i, 0)), pl.BlockSpec((1, d), lambda i: (0, 0))],
        out_specs=pl.BlockSpec((tm, d), lambda i: (i, 0)), out_shape=_out(t, d, BF16),
        compiler_params=_params(("parallel",)),
    )(h, g)


def _rms_bwd(name, h, g, dn, dres=None):
    t, d = h.shape
    tm = _pick(t, 256, SUBLANES)
    need_dh = dres is not None

    def body(*refs):
        if need_dh:
            h_ref, g_ref, dn_ref, dres_ref, dh_ref, dhb_ref, dg_ref = refs
        else:
            h_ref, g_ref, dn_ref, dg_ref = refs
        hv = h_ref[...]
        r = lax.rsqrt(jnp.mean(hv * hv, axis=-1, keepdims=True) + EPS)
        nh = hv * r
        dnv = dn_ref[...].astype(F32)

        @pl.when(pl.program_id(0) == 0)
        def _():
            dg_ref[...] = jnp.zeros_like(dg_ref)

        dg_ref[...] += jnp.sum(dnv * nh, axis=0, keepdims=True)
        if need_dh:
            dng = dnv * g_ref[...]
            dh = dres_ref[...] + r * (dng - nh * jnp.mean(dng * nh, axis=-1, keepdims=True))
            dh_ref[...] = dh
            dhb_ref[...] = dh.astype(BF16)

    row = pl.BlockSpec((tm, d), lambda i: (i, 0))
    vec = pl.BlockSpec((1, d), lambda i: (0, 0))
    if need_dh:
        return pl.pallas_call(
            body, name=name, grid=(t // tm,), in_specs=[row, vec, row, row], out_specs=[row, row, vec],
            out_shape=[_out(t, d, F32), _out(t, d, BF16), _out(1, d, F32)], compiler_params=_params(("arbitrary",)),
        )(h, g, dn, dres)
    return pl.pallas_call(
        body, name=name, grid=(t // tm,), in_specs=[row, vec, row], out_specs=vec,
        out_shape=_out(1, d, F32), compiler_params=_params(("arbitrary",)),
    )(h, g, dn)


def _loss_head(h, g, tgt):
    t, d = h.shape
    tm = _pick(t, 256, SUBLANES)

    def body(h_ref, g_ref, t_ref, dh_ref, dhb_ref, dg_ref, loss_ref):
        hv = h_ref[...]
        r = lax.rsqrt(jnp.mean(hv * hv, axis=-1, keepdims=True) + EPS)
        nh = hv * r
        err = nh * g_ref[...] - t_ref[...]

        @pl.when(pl.program_id(0) == 0)
        def _():
            dg_ref[...] = jnp.zeros_like(dg_ref)
            loss_ref[...] = jnp.zeros_like(loss_ref)

        per_row = jnp.mean(err * err, axis=-1, keepdims=True)
        loss_ref[...] += 0.5 * jnp.sum(per_row, axis=0, keepdims=True)
        dy = err * (1.0 / d)
        dg_ref[...] += jnp.sum(dy * nh, axis=0, keepdims=True)
        dng = dy * g_ref[...]
        dh = r * (dng - nh * jnp.mean(dng * nh, axis=-1, keepdims=True))
        dh_ref[...] = dh
        dhb_ref[...] = dh.astype(BF16)

    row = pl.BlockSpec((tm, d), lambda i: (i, 0))
    vec = pl.BlockSpec((1, d), lambda i: (0, 0))
    return pl.pallas_call(
        body, name="loss_head", grid=(t // tm,), in_specs=[row, vec, row],
        out_specs=[row, row, vec, pl.BlockSpec((1, LANES), lambda i: (0, 0))],
        out_shape=[_out(t, d, F32), _out(t, d, BF16), _out(1, d, F32), _out(1, LANES, F32)],
        compiler_params=_params(("arbitrary",)),
    )(h, g, tgt)


def _ffn_fwd(tag, h, g, wg_t, wu_t, wd):
    t, d = h.shape
    f = wd.shape[0]
    n = _rms_fwd(tag + "_norm", h, g)
    tm, tn = _pick(t, 1024), _pick(f, 256)

    def up_epi(accs):
        a, b = accs
        return a, b, (a * jax.nn.sigmoid(a)) * b

    a, b, hid = _mm(tag + "_up", "nt", [n], [wg_t, wu_t], [[(0, 0)], [(0, 1)]], t, f, tm, tn, [], up_epi,
                    [(_out(t, f, BF16), None)] * 3)
    tm2, tn2 = _pick(t, 512), _pick(d, 256)
    h_out = _mm(tag + "_down", "nn", [hid], [wd], [[(0, 0)]], t, d, tm2, tn2, [(h, _tile(tm2, tn2))],
                lambda accs, hin: (hin + 0.5 * accs[0],), [(_out(t, d, F32), None)])[0]
    return h_out, (n, a, b, hid)


def _ffn_bwd(tag, h, g, wg_t, wu_t, wd, saved, dh, dh_bf):
    n, a, b, hid = saved
    t, d = h.shape
    f = wd.shape[0]
    tm, tn = _pick(t, 1024), _pick(f, 256)

    def hid_epi(accs, av, bv):
        dhid = 0.5 * accs[0]
        av, bv = av.astype(F32), bv.astype(F32)
        sig = jax.nn.sigmoid(av)
        da = dhid * bv * (sig * (1.0 + av * (1.0 - sig)))
        db = dhid * (av * sig)
        return da, db

    da, db = _mm(tag + "_bwd_hid", "nt", [dh_bf], [wd], [[(0, 0)]], t, f, tm, tn,
                 [(a, _tile(tm, tn)), (b, _tile(tm, tn))], hid_epi, [(_out(t, f, BF16), None)] * 2)
    tw = _pick(f, 256)
    d_wd = _mm1(tag + "_dwd", "tn", hid, dh_bf, f, d, tw, d, BF16, scale=0.5)
    d_wg = _mm1(tag + "_dwg", "tn", da, n, f, d, tw, d, BF16)
    d_wu = _mm1(tag + "_dwu", "tn", db, n, f, d, tw, d, BF16)
    tm2, tn2 = _pick(t, 512), _pick(d, 256)
    dn = _mm(tag + "_dn", "nn", [da, db], [wg_t, wu_t], [[(0, 0), (1, 1)]], t, d, tm2, tn2, [],
             lambda accs: (accs[0],), [(_out(t, d, F32), None)])[0]
    dh_in, dh_in_bf, dg = _rms_bwd(tag + "_norm_bwd", h, g, dn, dh)
    return dh_in, dh_in_bf, dg, d_wg, d_wu, d_wd


def _window_sum(win, offsets):
    n = win.shape[0]
    acc = None
    for j in offsets:
        term = win if j == 0 else pltpu.roll(win, (-j) % n, 0)
        acc = term if acc is None else acc + term
    return acc


def _pool_counts(r0, ch, c, left, right, t):
    pos = r0 + lax.broadcasted_iota(jnp.int32, (ch, c), 0)
    return (jnp.minimum(pos + right + 1, t) - jnp.maximum(pos - left, 0)).astype(F32)


def _pool_fwd(proj, pool_w_bf, pool_scale):
    t = proj.shape[0]
    ng, c, _ = pool_w_bf.shape
    ch = _pick(t, 256, SUBLANES)
    pad = POOL_PAD

    def body(p_ref, w_ref, s_ref, pooled_ref, pm_ref, buf):
        grp = pl.program_id(0)
        buf[pl.ds(0, pad), :] = jnp.zeros((pad, c), F32)
        buf[pl.ds(pad + t, pad), :] = jnp.zeros((pad, c), F32)

        def fill(ci, carry):
            r0 = pl.multiple_of(ci * ch, SUBLANES)
            buf[pl.ds(pl.multiple_of(r0 + pad, SUBLANES), ch), :] = p_ref[pl.ds(r0, ch), :]
            return carry

        lax.fori_loop(0, t // ch, fill, 0)
        for gi, w in enumerate(POOL_WINDOWS):
            left = w // 2
            right = w - 1 - left

            @pl.when(grp == gi)
            def _(left=left, right=right):
                def chunk(ci, carry):
                    r0 = pl.multiple_of(ci * ch, SUBLANES)
                    win = buf[pl.ds(r0, ch + 2 * pad), :]
                    s = _window_sum(win, range(-left, right + 1))[pad:pad + ch]
                    pooled = s / _pool_counts(r0, ch, c, left, right, t) - win[pad:pad + ch]
                    pooled_bf = pooled.astype(BF16)
                    mixed = jnp.dot(pooled_bf, w_ref[0], preferred_element_type=F32)
                    pooled_ref[pl.ds(r0, ch), :] = pooled_bf
                    pm_ref[pl.ds(r0, ch), :] = (mixed * s_ref[...]).astype(BF16)
                    return carry

                lax.fori_loop(0, t // ch, chunk, 0)

    col = pl.BlockSpec((t, c), lambda g: (0, g))
    return pl.pallas_call(
        body, name="pool_fwd", grid=(ng,),
        in_specs=[col, pl.BlockSpec((1, c, c), lambda g: (g, 0, 0)), pl.BlockSpec((1, c), lambda g: (0, g))],
        out_specs=[col, col], out_shape=[_out(t, ng * c, BF16), _out(t, ng * c, BF16)],
        scratch_shapes=[pltpu.VMEM((t + 2 * pad, c), F32)],
        compiler_params=_params(("parallel",)),
    )(proj, pool_w_bf, pool_scale)


def _pool_bwd(pooled, dpm, pool_w_bf, pool_scale):
    t = pooled.shape[0]
    ng, c, _ = pool_w_bf.shape
    ch = _pick(t, 256, SUBLANES)
    pad = POOL_PAD

    def body(pooled_ref, dpm_ref, w_ref, s_ref, dp_ref, dw_ref, ds_ref, buf, raw):
        grp = pl.program_id(0)
        buf[pl.ds(0, pad), :] = jnp.zeros((pad, c), F32)
        buf[pl.ds(pad + t, pad), :] = jnp.zeros((pad, c), F32)
        dw_ref[...] = jnp.zeros_like(dw_ref)
        ds_ref[...] = jnp.zeros_like(ds_ref)
        for gi, w in enumerate(POOL_WINDOWS):
            left = w // 2
            right = w - 1 - left

            @pl.when(grp == gi)
            def _(left=left, right=right):
                def first(ci, carry):
                    r0 = pl.multiple_of(ci * ch, SUBLANES)
                    pv = pooled_ref[pl.ds(r0, ch), :]
                    dpm_v = dpm_ref[pl.ds(r0, ch), :]
                    mixed = jnp.dot(pv, w_ref[0], preferred_element_type=F32)
                    ds_ref[...] += jnp.sum(dpm_v * mixed, axis=0, keepdims=True)
                    dmixed = (dpm_v * s_ref[...]).astype(BF16)
                    dw_ref[0] += lax.dot_general(pv, dmixed, _DIMS["tn"], preferred_element_type=F32)
                    dpooled = lax.dot_general(dmixed, w_ref[0], _DIMS["nt"], preferred_element_type=F32)
                    raw[pl.ds(r0, ch), :] = dpooled
                    buf[pl.ds(pl.multiple_of(r0 + pad, SUBLANES), ch), :] = (
                        dpooled / _pool_counts(r0, ch, c, left, right, t))
                    return carry

                lax.fori_loop(0, t // ch, first, 0)

                def second(ci, carry):
                    r0 = pl.multiple_of(ci * ch, SUBLANES)
                    win = buf[pl.ds(r0, ch + 2 * pad), :]
                    s = _window_sum(win, range(-right, left + 1))[pad:pad + ch]
                    dp_ref[pl.ds(r0, ch), :] = (s - raw[pl.ds(r0, ch), :]).astype(BF16)
                    return carry

                lax.fori_loop(0, t // ch, second, 0)

    col = pl.BlockSpec((t, c), lambda g: (0, g))
    return pl.pallas_call(
        body, name="pool_bwd", grid=(ng,),
        in_specs=[col, col, pl.BlockSpec((1, c, c), lambda g: (g, 0, 0)), pl.BlockSpec((1, c), lambda g: (0, g))],
        out_specs=[col, pl.BlockSpec((1, c, c), lambda g: (g, 0, 0)), pl.BlockSpec((1, c), lambda g: (0, g))],
        out_shape=[_out(t, ng * c, BF16), jax.ShapeDtypeStruct((ng, c, c), F32), _out(1, ng * c, F32)],
        scratch_shapes=[pltpu.VMEM((t + 2 * pad, c), F32), pltpu.VMEM((t, c), F32)],
        compiler_params=_params(("parallel",)),
    )(pooled, dpm, pool_w_bf, pool_scale)


def _discretise(a_re, a_im, log_dt, b_re, b_im):
    dt = jnp.exp(log_dt)
    mag = jnp.exp(dt * a_re)
    ang = dt * a_im
    abr = mag * jnp.cos(ang)
    abi = mag * jnp.sin(ang)
    den = a_re * a_re + a_im * a_im
    nr = abr - 1.0
    qr = (nr * a_re + abi * a_im) / den
    qi = (abi * a_re - nr * a_im) / den
    return abr, abi, qr * b_re - qi * b_im, qr * b_im + qi * b_re


def _ssm_disc(cols):
    n, hh = cols[3].shape

    def body(ar, ai, ld, br, bi, o1, o2, o3, o4):
        res = _discretise(ar[...], ai[...], ld[...], br[...], bi[...])
        for o, r in zip((o1, o2, o3, o4), res):
            o[...] = r

    return pl.pallas_call(
        body, name="ssm_disc",
        out_shape=[_out(n, 1, F32), _out(n, 1, F32), _out(n, hh, F32), _out(n, hh, F32)],
    )(*cols)


def _ssm_disc_bwd(cols, cots):
    n, hh = cols[3].shape

    def body(ar, ai, ld, br, bi, c1, c2, c3, c4, o1, o2, o3, o4, o5):
        _, vjp = jax.vjp(_discretise, ar[...], ai[...], ld[...], br[...], bi[...])
        res = vjp((c1[...], c2[...], c3[...], c4[...]))
        for o, r in zip((o1, o2, o3, o4, o5), res):
            o[...] = r

    return pl.pallas_call(
        body, name="ssm_disc_bwd",
        out_shape=[_out(n, 1, F32)] * 3 + [_out(n, hh, F32)] * 2,
    )(*cols, *cots)


def _rowsum(name, a):
    r, _ = a.shape

    def body(a_ref, o_ref):
        o_ref[...] = jnp.sum(a_ref[...], axis=-1, keepdims=True)

    return pl.pallas_call(body, name=name, out_shape=_out(r, 1, F32))(a)


def _cmul(pr, pi, qr, qi):
    return pr * qr - pi * qi, pr * qi + pi * qr


def _scan(name, u, ar, ai, reverse, conj):
    t, s2 = u.shape
    s = s2 // 2
    w = _pick(s, 512)
    tc = _pick(t, 512, SUBLANES)
    n_t, n_w = t // tc, s // w
    groups = tc // SUBLANES
    last = 0 if reverse else SUBLANES - 1

    def body(ar_ref, ai_ref, ur_ref, ui_ref, xr_ref, xi_ref, cr_ref, ci_ref):
        @pl.when(pl.program_id(1) == 0)
        def _():
            cr_ref[...] = jnp.zeros_like(cr_ref)
            ci_ref[...] = jnp.zeros_like(ci_ref)

        a1r = ar_ref[...]
        a1i = -ai_ref[...] if conj else ai_ref[...]
        a2r, a2i = _cmul(a1r, a1i, a1r, a1i)
        a4r, a4i = _cmul(a2r, a2i, a2r, a2i)
        row = lax.broadcasted_iota(jnp.int32, (SUBLANES, w), 0)
        pwr = jnp.zeros((SUBLANES, w), F32)
        pwi = jnp.zeros((SUBLANES, w), F32)
        cur_r, cur_i = a1r, a1i
        for k in range(SUBLANES):
            rk = SUBLANES - 1 - k if reverse else k
            pwr = jnp.where(row == rk, cur_r, pwr)
            pwi = jnp.where(row == rk, cur_i, pwi)
            cur_r, cur_i = _cmul(cur_r, cur_i, a1r, a1i)
        steps = ((1, a1r, a1i), (2, a2r, a2i), (4, a4r, a4i))

        def one(i, carry):
            g = groups - 1 - i if reverse else i
            r0 = pl.multiple_of(g * SUBLANES, SUBLANES)
            br = ur_ref[pl.ds(r0, SUBLANES), :]
            bi = ui_ref[pl.ds(r0, SUBLANES), :]
            for dist, pr, pi in steps:
                if reverse:
                    keep = row < SUBLANES - dist
                    shift = SUBLANES - dist
                else:
                    keep = row >= dist
                    shift = dist
                sr = jnp.where(keep, pltpu.roll(br, shift, 0), 0.0)
                si = jnp.where(keep, pltpu.roll(bi, shift, 0), 0.0)
                br, bi = br + pr * sr - pi * si, bi + pr * si + pi * sr
            cr = cr_ref[pl.ds(last, 1), :]
            ci = ci_ref[pl.ds(last, 1), :]
            xr = br + pwr * cr - pwi * ci
            xi = bi + pwr * ci + pwi * cr
            xr_ref[pl.ds(r0, SUBLANES), :] = xr
            xi_ref[pl.ds(r0, SUBLANES), :] = xi
            cr_ref[...] = xr
            ci_ref[...] = xi
            return carry

        lax.fori_loop(0, groups, one, 0)

    def tmap(k):
        return n_t - 1 - k if reverse else k

    re_blk = pl.BlockSpec((tc, w), lambda cb, k: (tmap(k), cb))
    im_blk = pl.BlockSpec((tc, w), lambda cb, k: (tmap(k), cb + n_w))
    a_blk = pl.BlockSpec((1, w), lambda cb, k: (0, cb))
    xr, xi = pl.pallas_call(
        body, name=name, grid=(n_w, n_t), in_specs=[a_blk, a_blk, re_blk, im_blk],
        out_specs=[pl.BlockSpec((tc, w), lambda cb, k: (tmap(k), cb))] * 2,
        out_shape=[_out(t, s, F32), _out(t, s, F32)],
        scratch_shapes=[pltpu.VMEM((SUBLANES, w), F32), pltpu.VMEM((SUBLANES, w), F32)],
        compiler_params=_params(("parallel", "arbitrary")),
    )(ar, ai, u, u)
    return xr, xi


def _ssm_da(name, lr, li, xr, xi, reverse):
    t, s = xr.shape
    w = _pick(s, 512)
    tc = _pick(t, 256, SUBLANES)
    n_t, n_w = t // tc, s // w

    def body(lr_ref, li_ref, xr_ref, xi_ref, dar_ref, dai_ref, pr_ref, pi_ref):
        @pl.when(pl.program_id(1) == 0)
        def _():
            pr_ref[...] = jnp.zeros_like(pr_ref)
            pi_ref[...] = jnp.zeros_like(pi_ref)
            dar_ref[...] = jnp.zeros_like(dar_ref)
            dai_ref[...] = jnp.zeros_like(dai_ref)

        row = lax.broadcasted_iota(jnp.int32, (tc, w), 0)
        xrv, xiv = xr_ref[...], xi_ref[...]
        if reverse:
            keep, shift, edge = row < tc - 1, tc - 1, 0
        else:
            keep, shift, edge = row >= 1, 1, tc - 1
        xsr = jnp.where(keep, pltpu.roll(xrv, shift, 0), pr_ref[pl.ds(0, 1), :])
        xsi = jnp.where(keep, pltpu.roll(xiv, shift, 0), pi_ref[pl.ds(0, 1), :])
        lrv, liv = lr_ref[...], li_ref[...]
        dar_ref[...] += jnp.sum(lrv * xsr + liv * xsi, axis=0, keepdims=True)
        dai_ref[...] += jnp.sum(liv * xsr - lrv * xsi, axis=0, keepdims=True)
        pr_ref[pl.ds(0, 1), :] = xr_ref[pl.ds(edge, 1), :]
        pi_ref[pl.ds(0, 1), :] = xi_ref[pl.ds(edge, 1), :]

    def tmap(k):
        return n_t - 1 - k if reverse else k

    blk = pl.BlockSpec((tc, w), lambda cb, k: (tmap(k), cb))
    vec = pl.BlockSpec((1, w), lambda cb, k: (0, cb))
    return pl.pallas_call(
        body, name=name, grid=(n_w, n_t), in_specs=[blk] * 4, out_specs=[vec, vec],
        out_shape=[_out(1, s, F32), _out(1, s, F32)],
        scratch_shapes=[pltpu.VMEM((SUBLANES, w), F32), pltpu.VMEM((SUBLANES, w), F32)],
        compiler_params=_params(("parallel", "arbitrary")),
    )(lr, li, xr, xi)


def _colsum_prod(name, a, b, b_coff=0):
    t, n = a.shape
    tm = _pick(t, 512, SUBLANES)

    def body(a_ref, b_ref, o_ref):
        @pl.when(pl.program_id(0) == 0)
        def _():
            o_ref[...] = jnp.zeros_like(o_ref)

        o_ref[...] += jnp.sum(a_ref[...].astype(F32) * b_ref[...].astype(F32), axis=0, keepdims=True)

    return pl.pallas_call(
        body, name=name, grid=(t // tm,),
        in_specs=[pl.BlockSpec((tm, n), lambda i: (i, 0)), pl.BlockSpec((tm, n), lambda i: (i, b_coff))],
        out_specs=pl.BlockSpec((1, n), lambda i: (0, 0)), out_shape=_out(1, n, F32),
        compiler_params=_params(("arbitrary",)),
    )(a, b)


def _bd_in(bb, g, p, hh):
    blk = bb.reshape(g, p, hh).transpose(0, 2, 1)
    eye = jnp.eye(g, dtype=bool)[:, None, :, None]
    return jnp.where(eye, blk[:, :, None, :], 0.0).reshape(g * hh, g * p)


def _bd_out(cc, g, p, hh):
    blk = cc.transpose(0, 2, 1)
    eye = jnp.eye(g, dtype=bool)[:, None, :, None]
    return jnp.where(eye, blk[:, :, None, :], 0.0).reshape(g * p, g * hh)


def _diag_in(dmat, g, p, hh):
    d4 = dmat.reshape(g, hh, g, p)
    diag = jnp.stack([d4[k, :, k, :] for k in range(g)], axis=0)
    return diag.transpose(0, 2, 1).reshape(g * p, hh)


def _diag_out(dmat, g, p, hh):
    d4 = dmat.reshape(g, p, g, hh)
    diag = jnp.stack([d4[k, :, k, :] for k in range(g)], axis=0)
    return diag.transpose(0, 2, 1)


def _softmax(qh, kh, scale):
    s = lax.dot_general(qh, kh, _DIMS["nt"], preferred_element_type=F32) * scale
    e = jnp.exp(s - jnp.max(s, axis=-1, keepdims=True))
    return e / jnp.sum(e, axis=-1, keepdims=True)


def _attn_fwd(q, kv):
    t, d = q.shape
    mm_ = kv.shape[0]
    hd = d // N_XHEADS
    scale = 1.0 / math.sqrt(hd)
    tm = _pick(t, 512, SUBLANES)

    def body(q_ref, kv_ref, o_ref):
        for h in range(N_XHEADS):
            sl = pl.ds(h * hd, hd)
            p = _softmax(q_ref[:, sl], kv_ref[:, sl], scale)
            o_ref[:, sl] = jnp.dot(p.astype(BF16), kv_ref[:, pl.ds(d + h * hd, hd)],
                                   preferred_element_type=F32).astype(BF16)

    return pl.pallas_call(
        body, name="attn_fwd", grid=(t // tm,),
        in_specs=[pl.BlockSpec((tm, d), lambda i: (i, 0)), pl.BlockSpec((mm_, 2 * d), lambda i: (0, 0))],
        out_specs=pl.BlockSpec((tm, d), lambda i: (i, 0)), out_shape=_out(t, d, BF16),
        compiler_params=_params(("parallel",)),
    )(q, kv)


def _attn_bwd(q, kv, do):
    t, d = q.shape
    mm_ = kv.shape[0]
    hd = d // N_XHEADS
    scale = 1.0 / math.sqrt(hd)
    tm = _pick(t, 512, SUBLANES)

    def body(q_ref, kv_ref, do_ref, dq_ref, dkv_ref):
        @pl.when(pl.program_id(0) == 0)
        def _():
            dkv_ref[...] = jnp.zeros_like(dkv_ref)

        for h in range(N_XHEADS):
            sl = pl.ds(h * hd, hd)
            vsl = pl.ds(d + h * hd, hd)
            qh, kh, doh = q_ref[:, sl], kv_ref[:, sl], do_ref[:, sl]
            p = _softmax(qh, kh, scale)
            dp = lax.dot_general(doh, kv_ref[:, vsl], _DIMS["nt"], preferred_element_type=F32)
            dkv_ref[:, vsl] += lax.dot_general(p.astype(BF16), doh, _DIMS["tn"], preferred_element_type=F32)
            ds = (p * (dp - jnp.sum(dp * p, axis=-1, keepdims=True)) * scale).astype(BF16)
            dq_ref[:, sl] = jnp.dot(ds, kh, preferred_element_type=F32).astype(BF16)
            dkv_ref[:, sl] += lax.dot_general(ds, qh, _DIMS["tn"], preferred_element_type=F32)

    row = pl.BlockSpec((tm, d), lambda i: (i, 0))
    full = pl.BlockSpec((mm_, 2 * d), lambda i: (0, 0))
    return pl.pallas_call(
        body, name="attn_bwd", grid=(t // tm,), in_specs=[row, full, row], out_specs=[row, full],
        out_shape=[_out(t, d, BF16), _out(mm_, 2 * d, F32)], compiler_params=_params(("arbitrary",)),
    )(q, kv, do)


def _ew(name, fn, ins, outs, rows_pref=256):
    r, c = ins[0].shape
    tr = _pick(r, rows_pref, SUBLANES)
    ni = len(ins)

    def body(*refs):
        res = fn(*[x[...] for x in refs[:ni]])
        for o_ref, v in zip(refs[ni:], res):
            o_ref[...] = v.astype(o_ref.dtype)

    blk = pl.BlockSpec((tr, c), lambda i: (i, 0))
    return pl.pallas_call(
        body, name=name, grid=(r // tr,), in_specs=[blk] * ni, out_specs=[blk] * len(outs),
        out_shape=[_out(r, c, dt) for dt in outs], compiler_params=_params(("parallel",)),
    )(*ins)


def _sum_slots(name, a, dtype):
    s, r, c = a.shape
    tr = _pick(r, 256, SUBLANES)

    def body(a_ref, o_ref):
        acc = a_ref[0].astype(F32)
        for k in range(1, s):
            acc = acc + a_ref[k].astype(F32)
        o_ref[...] = acc.astype(o_ref.dtype)

    return pl.pallas_call(
        body, name=name, grid=(r // tr,), in_specs=[pl.BlockSpec((s, tr, c), lambda i: (0, i, 0))],
        out_specs=pl.BlockSpec((tr, c), lambda i: (i, 0)), out_shape=_out(r, c, dtype),
        compiler_params=_params(("parallel",)),
    )(a)


def _adamw(name, w, g, m, v):
    bc1 = 1.0 - ADAM_B1 ** ADAM_STEP
    bc2 = 1.0 - ADAM_B2 ** ADAM_STEP

    def fn(wv, gv, mv, vv):
        m2 = ADAM_B1 * mv + (1.0 - ADAM_B1) * gv
        v2 = ADAM_B2 * vv + (1.0 - ADAM_B2) * (gv * gv)
        delta = -ADAM_LR * ((m2 / bc1) / (jnp.sqrt(v2 / bc2) + ADAM_EPS) + ADAM_WD * wv)
        return delta, m2, v2

    return _ew(name, fn, [w, g, m, v], [F32, F32, F32])


def _allgather(name, arrs):
    n = len(arrs)

    def body(*refs):
        ins, outs = refs[:n], refs[n:2 * n]
        send_sems, recv_sems, local_sems = refs[2 * n:]
        x, y, c = lax.axis_index("x"), lax.axis_index("y"), lax.axis_index("c")
        me, sibling = (x, y, c), (x, y, 1 - c)
        chips = [(1 - x, y), (x, 1 - y), (1 - x, 1 - y)]

        def rows(a, px, py, pc):
            r = ins[a].shape[0]
            return outs[a].at[pl.ds((4 * px + 2 * py + pc) * r, r), :]

        def copy(a, k, block, to, src=None):
            return pltpu.make_async_remote_copy(
                src_ref=rows(a, *block) if src is None else src, dst_ref=rows(a, *block),
                send_sem=send_sems.at[a, k], recv_sem=recv_sems.at[a, k], device_id=to, device_id_type=MESH)

        mine = [pltpu.make_async_copy(ins[a], rows(a, *me), local_sems.at[a]) for a in range(n)]
        for cp in mine:
            cp.start()
        first = []
        for a in range(n):
            first.append(copy(a, 0, me, sibling, src=ins[a]))
            first += [copy(a, 1 + j, me, (*chip, c), src=ins[a]) for j, chip in enumerate(chips)]
        for cp in first:
            cp.start()
        passed = []
        for j, chip in enumerate(chips):
            for a in range(n):
                copy(a, 1 + j, (*chip, c), me).wait_recv()
                cp = copy(a, 4 + j, (*chip, c), sibling)
                cp.start()
                passed.append(cp)
        for a in range(n):
            copy(a, 0, sibling, me).wait_recv()
            for j, chip in enumerate(chips):
                copy(a, 4 + j, (*chip, 1 - c), me).wait_recv()
        for cp in first + passed:
            cp.wait_send()
        for cp in mine:
            cp.wait()

    return pl.pallas_call(
        body, name=name, in_specs=[ANY] * n, out_specs=[ANY] * n,
        out_shape=[_out(N_DEV * a.shape[0], a.shape[1], a.dtype) for a in arrs],
        scratch_shapes=[pltpu.SemaphoreType.DMA((n, 7)), pltpu.SemaphoreType.DMA((n, 7)), pltpu.SemaphoreType.DMA((n,))],
    )(*arrs)


def _exchange_cores(g):
    _, r, c = g.shape

    def body(g_ref, own_ref, recv_ref, send_sems, recv_sems, local_sems):
        x, y, cc = lax.axis_index("x"), lax.axis_index("y"), lax.axis_index("c")
        local, remote = [], []
        for q in range(4):
            px, py = q // 2, q % 2
            local.append(pltpu.make_async_copy(g_ref.at[4 * px + 2 * py + cc], own_ref.at[q], local_sems.at[q]))
            remote.append(pltpu.make_async_remote_copy(
                src_ref=g_ref.at[4 * px + 2 * py + (1 - cc)], dst_ref=recv_ref.at[q],
                send_sem=send_sems.at[q], recv_sem=recv_sems.at[q], device_id=(x, y, 1 - cc), device_id_type=MESH))
        for cp in local + remote:
            cp.start()
        for cp in remote:
            cp.wait()
        for cp in local:
            cp.wait()

    return pl.pallas_call(
        body, name="grad_exchange_cores", in_specs=[ANY], out_specs=[ANY, ANY],
        out_shape=[jax.ShapeDtypeStruct((4, r, c), g.dtype)] * 2,
        scratch_shapes=[pltpu.SemaphoreType.DMA((4,)), pltpu.SemaphoreType.DMA((4,)), pltpu.SemaphoreType.DMA((4,))],
    )(g)


def _exchange_chips(p):
    _, r, c = p.shape

    def body(p_ref, out_ref, send_sems, recv_sems, local_sem):
        x, y, cc = lax.axis_index("x"), lax.axis_index("y"), lax.axis_index("c")
        own = pltpu.make_async_copy(p_ref.at[2 * x + y], out_ref.at[0], local_sem)
        own.start()
        remote = []
        for k in range(1, 4):
            px = 1 - x if k >> 1 else x
            py = 1 - y if k & 1 else y
            remote.append(pltpu.make_async_remote_copy(
                src_ref=p_ref.at[2 * px + py], dst_ref=out_ref.at[k],
                send_sem=send_sems.at[k - 1], recv_sem=recv_sems.at[k - 1], device_id=(px, py, cc),
                device_id_type=MESH))
        for cp in remote:
            cp.start()
        for cp in remote:
            cp.wait()
        own.wait()

    return pl.pallas_call(
        body, name="grad_exchange_chips", in_specs=[ANY], out_specs=ANY,
        out_shape=jax.ShapeDtypeStruct((4, r, c), p.dtype),
        scratch_shapes=[pltpu.SemaphoreType.DMA((3,)), pltpu.SemaphoreType.DMA((3,)), pltpu.SemaphoreType.DMA],
    )(p)


def _local_step(x, mem, tgt, wt, sm):
    t, d = x.shape
    n_mem = mem.shape[0]
    d_pool = sm["pool_scale"].shape[1]
    ng, pc = sm["pool_w"].shape[0], sm["pool_w"].shape[1]
    d_ssm = sm["ssm_d"].shape[1]
    _, sg, sp, sh = sm["ssm_b_re"].shape
    n_state = sg * sp
    gb, gs = {}, {}

    h1, ffn1_saved = _ffn_fwd("ffn1", x, sm["ffn1_norm"], wt["ffn1_w_gate"], wt["ffn1_w_up"], wt["ffn1_w_down"])
    u = _rms_fwd("mix_norm", h1, sm["mix_norm"])
    d_in = wt["w_in"].shape[0]
    tm, tn = _pick(t, 1024), _pick(d_in, 256)
    proj = _mm1("in_proj", "nt", u, wt["w_in"], t, d_in, tm, tn, F32)
    off_s = d_pool // d_ssm
    off_gp = (d_pool + d_ssm)
    off_gs = off_gp + d

    pool_w_bf = sm["pool_w"].astype(BF16)
    pooled, pm = _pool_fwd(proj, pool_w_bf, sm["pool_scale"])

    cols = [sm["ssm_a_re"].reshape(-1, 1), sm["ssm_a_im"].reshape(-1, 1),
            jnp.broadcast_to(sm["ssm_log_dt"][:, :, None], (2, sg, sp)).reshape(-1, 1),
            sm["ssm_b_re"].reshape(-1, sh), sm["ssm_b_im"].reshape(-1, sh)]
    abr, abi, bbr, bbi = _ssm_disc(cols)
    abr2, abi2 = abr.reshape(2, n_state), abi.reshape(2, n_state)
    bbr4, bbi4 = bbr.reshape(2, sg * sp, sh), bbi.reshape(2, sg * sp, sh)
    b_re = [_bd_in(bbr4[dr], sg, sp, sh).astype(BF16) for dr in range(2)]
    b_im = [_bd_in(bbi4[dr], sg, sp, sh).astype(BF16) for dr in range(2)]
    c_re = [_bd_out(sm["ssm_c_re"][dr], sg, sp, sh).astype(BF16) for dr in range(2)]
    c_im = [_bd_out(-sm["ssm_c_im"][dr], sg, sp, sh).astype(BF16) for dr in range(2)]
    tms = _pick(t, 512)
    s_bf = _ew("ssm_cast", lambda v: (v,), [proj[:, d_pool:d_pool + d_ssm]], [BF16])[0]
    xs = []
    for dr in range(2):
        u_d = _mm1(f"ssm_in{dr}", "nn", s_bf, jnp.concatenate([b_re[dr], b_im[dr]], axis=1), t, 2 * n_state, tms,
                   _pick(2 * n_state, 512), F32)
        xs.append(_scan(f"ssm_scan{dr}", u_d, abr2[dr:dr + 1], abi2[dr:dr + 1], reverse=(dr == 1), conj=False))
    tmy = _pick(t, 256)
    x_list = [xs[0][0], xs[0][1], xs[1][0], xs[1][1]]
    y = _mm("ssm_out", "nn", x_list, [c_re[0], c_im[0], c_re[1], c_im[1]], [[(k, k) for k in range(4)]], t, d_ssm, tmy,
            d_ssm, [(proj, _tile(tmy, d_ssm, off_s)), (sm["ssm_d"], _rowvec(d_ssm))],
            lambda accs, sv, dv: (sv * dv + accs[0],), [(_out(t, d_ssm, F32), None)])[0]
    ys = _ew("ssm_gelu", lambda v: (jax.nn.gelu(v),), [y], [BF16])[0]

    tmm, tnm = _pick(t, 512), _pick(d, 256)
    gp_spec = _tile(tmm, tnm, off_gp // tnm)
    gs_spec = _tile(tmm, tnm, off_gs // tnm)

    def merge_epi(accs, gpv, gsv):
        z_pool, val, gate = accs
        return (jax.nn.sigmoid(gpv) * z_pool + jax.nn.sigmoid(gsv) * (val * jax.nn.sigmoid(gate)),)

    merged = _mm("mix_merge", "nt", [pm, ys], [wt["w_pool_proj"], wt["w_glu_val"], wt["w_glu_gate"]],
                 [[(0, 0)], [(1, 1)], [(1, 2)]], t, d, tmm, tnm, [(proj, gp_spec), (proj, gs_spec)], merge_epi,
                 [(_out(t, d, BF16), None)])[0]
    res_epi = lambda accs, hin: (hin + accs[0],)
    h2 = _mm("mix_out", "nn", [merged], [wt["w_mix_out"]], [[(0, 0)]], t, d, tmm, tnm, [(h1, _tile(tmm, tnm))],
             res_epi, [(_out(t, d, F32), None)])[0]

    un = _rms_fwd("xattn_norm", h2, sm["xattn_norm"])
    mn = _rms_fwd("mem_norm", mem, sm["mem_norm"])
    q = _mm1("xattn_q", "nn", un, wt["w_q"], t, d, tmm, tnm, BF16)
    kv = _mm1("xattn_kv", "nt", mn, wt["w_kv"], n_mem, 2 * d, n_mem, _pick(2 * d, 512), BF16)
    o = _attn_fwd(q, kv)
    h3 = _mm("xattn_out", "nn", [o], [wt["w_xo"]], [[(0, 0)]], t, d, tmm, tnm, [(h2, _tile(tmm, tnm))],
             res_epi, [(_out(t, d, F32), None)])[0]

    h4, ffn2_saved = _ffn_fwd("ffn2", h3, sm["ffn2_norm"], wt["ffn2_w_gate"], wt["ffn2_w_up"], wt["ffn2_w_down"])

    dh4, dh4_bf, gs["final_norm"], loss = _loss_head(h4, sm["final_norm"], tgt)
    dh3, dh3_bf, gs["ffn2_norm"], gb["ffn2_w_gate"], gb["ffn2_w_up"], gb["ffn2_w_down"] = _ffn_bwd(
        "ffn2", h3, sm["ffn2_norm"], wt["ffn2_w_gate"], wt["ffn2_w_up"], wt["ffn2_w_down"], ffn2_saved, dh4, dh4_bf)

    tw = _pick(d, 256)
    do = _mm1("xattn_do", "nt", dh3_bf, wt["w_xo"], t, d, tmm, tnm, BF16)
    gb["w_xo"] = _mm1("xattn_dwxo", "tn", o, dh3_bf, d, d, tw, d, BF16)
    dq, dkv = _attn_bwd(q, kv, do)
    gb["w_q"] = _mm1("xattn_dwq", "tn", un, dq, d, d, tw, d, BF16)
    dun = _mm1("xattn_dun", "nt", dq, wt["w_q"], t, d, tmm, tnm, F32)
    dh2, dh2_bf, gs["xattn_norm"] = _rms_bwd("xattn_norm_bwd", h2, sm["xattn_norm"], dun, dh3)
    gb["w_kv"] = _mm1("xattn_dwkv", "tn", dkv, mn, 2 * d, d, _pick(2 * d, 512), d, BF16)
    dmn = _mm1("xattn_dmn", "nn", dkv, wt["w_kv"], n_mem, d, n_mem, tnm, F32)
    gs["mem_norm"] = _rms_bwd("mem_norm_bwd", mem, sm["mem_norm"], dmn)

    gb["w_mix_out"] = _mm1("mix_dwout", "tn", merged, dh2_bf, d, d, tw, d, BF16)

    def merge_bwd_epi(accs, gpv, gsv):
        dmerged, z_pool, val, gate = accs
        sp_, ss_, sg_ = jax.nn.sigmoid(gpv), jax.nn.sigmoid(gsv), jax.nn.sigmoid(gate)
        glu = val * sg_
        dz_pool = dmerged * sp_
        dg_pool = dmerged * z_pool * (sp_ * (1.0 - sp_))
        dz_ssm = dmerged * ss_
        dg_ssm = dmerged * glu * (ss_ * (1.0 - ss_))
        dval = dz_ssm * sg_
        dgate = dz_ssm * glu * (1.0 - sg_)
        return dz_pool, dg_pool, dg_ssm, dval, dgate

    dz_pool, dg_pool, dg_ssm, dval, dgate = _mm(
        "mix_merge_bwd", "nt", [dh2_bf, pm, ys], [wt["w_mix_out"], wt["w_pool_proj"], wt["w_glu_val"], wt["w_glu_gate"]],
        [[(0, 0)], [(1, 1)], [(2, 2)], [(2, 3)]], t, d, tmm, tnm, [(proj, gp_spec), (proj, gs_spec)], merge_bwd_epi,
        [(_out(t, d, BF16), None)] * 5)
    gb["w_pool_proj"] = _mm1("pool_dwproj", "tn", dz_pool, pm, d, d_pool, tw, d_pool, BF16)
    gb["w_glu_val"] = _mm1("glu_dwval", "tn", dval, ys, d, d_ssm, tw, d_ssm, BF16)
    gb["w_glu_gate"] = _mm1("glu_dwgate", "tn", dgate, ys, d, d_ssm, tw, d_ssm, BF16)

    def gelu_bwd_epi(accs, yv):
        _, vjp = jax.vjp(jax.nn.gelu, yv)
        return (vjp(accs[0])[0],)

    dy = _mm("glu_dy", "nn", [dval, dgate], [wt["w_glu_val"], wt["w_glu_gate"]], [[(0, 0), (1, 1)]], t, d_ssm, tmy, d_ssm,
             [(y, _tile(tmy, d_ssm))], gelu_bwd_epi, [(_out(t, d_ssm, F32), None)])[0]
    gs["ssm_d"] = _colsum_prod("ssm_dd", dy, proj, b_coff=off_s)
    dy_bf = _ew("ssm_dy_cast", lambda v: (v,), [dy], [BF16])[0]
    d_abr, d_abi, d_bbr, d_bbi, d_cre, d_cim, lams = [], [], [], [], [], [], []
    ts = _pick(n_state, 512)
    tc_ = _pick(n_state, 256)
    for dr in range(2):
        gx = _mm1(f"ssm_gx{dr}", "nt", dy_bf, jnp.concatenate([c_re[dr], c_im[dr]], axis=0), t, 2 * n_state, tms,
                  _pick(2 * n_state, 512), F32)
        lr, li = _scan(f"ssm_adj{dr}", gx, abr2[dr:dr + 1], abi2[dr:dr + 1], reverse=(dr == 0), conj=True)
        dar, dai = _ssm_da(f"ssm_da{dr}", lr, li, xs[dr][0], xs[dr][1], reverse=(dr == 1))
        d_abr.append(dar)
        d_abi.append(dai)
        lams += [lr, li]
        d_bbr.append(_diag_in(_mm1(f"ssm_dbre{dr}", "tn", s_bf, lr, d_ssm, n_state, d_ssm, ts, F32), sg, sp, sh))
        d_bbi.append(_diag_in(_mm1(f"ssm_dbim{dr}", "tn", s_bf, li, d_ssm, n_state, d_ssm, ts, F32), sg, sp, sh))
        d_cre.append(_diag_out(_mm1(f"ssm_dcre{dr}", "tn", xs[dr][0], dy_bf, n_state, d_ssm, tc_, d_ssm, F32), sg, sp, sh))
        d_cim.append(-_diag_out(_mm1(f"ssm_dcim{dr}", "tn", xs[dr][1], dy_bf, n_state, d_ssm, tc_, d_ssm, F32), sg, sp, sh))
    ds = _mm("ssm_ds", "nt", lams, [b_re[0], b_im[0], b_re[1], b_im[1]], [[(k, k) for k in range(4)]], t, d_ssm, tmy,
             d_ssm, [(dy, _tile(tmy, d_ssm)), (sm["ssm_d"], _rowvec(d_ssm))],
             lambda accs, dyv, dv: (dyv * dv + accs[0],), [(_out(t, d_ssm, BF16), None)])[0]
    cots = [jnp.concatenate(d_abr, axis=0).reshape(-1, 1), jnp.concatenate(d_abi, axis=0).reshape(-1, 1),
            jnp.concatenate(d_bbr, axis=0), jnp.concatenate(d_bbi, axis=0)]
    d_are, d_aim, d_ldt, d_bre, d_bim = _ssm_disc_bwd(cols, cots)
    gs["ssm_a_re"] = d_are.reshape(2, sg, sp)
    gs["ssm_a_im"] = d_aim.reshape(2, sg, sp)
    gs["ssm_log_dt"] = _rowsum("ssm_dlogdt", d_ldt.reshape(2 * sg, sp)).reshape(2, sg)
    gs["ssm_b_re"] = d_bre.reshape(2, sg, sp, sh)
    gs["ssm_b_im"] = d_bim.reshape(2, sg, sp, sh)
    gs["ssm_c_re"] = jnp.stack(d_cre, axis=0)
    gs["ssm_c_im"] = jnp.stack(d_cim, axis=0)

    dpm = _mm1("pool_dpm", "nn", dz_pool, wt["w_pool_proj"], t, d_pool, tmm, _pick(d_pool, 256), F32)
    dp, gs["pool_w"], gs["pool_scale"] = _pool_bwd(pooled, dpm, pool_w_bf, sm["pool_scale"])

    w_in = wt["w_in"]
    parts = [(dp, 0, d_pool), (ds, d_pool, d_ssm), (dg_pool, off_gp, d), (dg_ssm, off_gs, d)]
    w_in_parts = [w_in[o0:o0 + width] for _, o0, width in parts]
    du = _mm("in_proj_du", "nn", [p_[0] for p_ in parts], w_in_parts, [[(k, k) for k in range(4)]], t, d, tmm, tnm, [],
             lambda accs: (accs[0],), [(_out(t, d, F32), None)])[0]
    gb["w_in"] = jnp.concatenate(
        [_mm1(f"in_proj_dw{k}", "tn", p_[0], u, p_[2], d, _pick(p_[2], 256), d, BF16) for k, p_ in enumerate(parts)], axis=0)
    dh1, dh1_bf, gs["mix_norm"] = _rms_bwd("mix_norm_bwd", h1, sm["mix_norm"], du, dh2)

    dx, _, gs["ffn1_norm"], gb["ffn1_w_gate"], gb["ffn1_w_up"], gb["ffn1_w_down"] = _ffn_bwd(
        "ffn1", x, sm["ffn1_norm"], wt["ffn1_w_gate"], wt["ffn1_w_up"], wt["ffn1_w_down"], ffn1_saved, dh1, dh1_bf)
    return loss, dx, gb, gs


WEIGHTS = ["ffn1_norm", "ffn1_w_gate", "ffn1_w_up", "ffn1_w_down", "mix_norm", "w_in", "pool_w", "pool_scale",
           "w_pool_proj", "ssm_a_re", "ssm_a_im", "ssm_log_dt", "ssm_b_re", "ssm_b_im", "ssm_c_re", "ssm_c_im", "ssm_d",
           "w_glu_val", "w_glu_gate", "w_mix_out", "xattn_norm", "mem_norm", "w_q", "w_kv", "w_xo", "ffn2_norm",
           "ffn2_w_gate", "ffn2_w_up", "ffn2_w_down", "final_norm"]
COL_SHARDED = ["ffn1_w_gate", "ffn1_w_up", "w_in", "w_pool_proj", "w_glu_val", "w_glu_gate", "w_kv", "ffn2_w_gate",
               "ffn2_w_up"]
ROW_SHARDED = ["ffn1_w_down", "w_mix_out", "w_q", "w_xo", "ffn2_w_down"]
BIG = [n for n in WEIGHTS if n in COL_SHARDED or n in ROW_SHARDED]
SMALL = [n for n in WEIGHTS if n not in BIG]
PACK_ROWS = SUBLANES * LANES
GRAD_ROW_TILE = 256


def _to_rows(name, w, width):
    if name in COL_SHARDED:
        w = w.T
    return w.reshape(-1, width)


def _from_rows(name, rows, shard_shape):
    if name in COL_SHARDED:
        return rows.reshape(shard_shape[1], shard_shape[0]).T
    return rows.reshape(shard_shape)


def _pack_small(vals):
    flat = []
    for v in vals:
        f = v.reshape(-1)
        flat.append(jnp.pad(f, (0, (-f.shape[0]) % PACK_ROWS)))
    total = sum(f.shape[0] for f in flat)
    flat.append(jnp.zeros(((-total) % (GRAD_ROW_TILE * LANES),), F32))
    return jnp.concatenate(flat).reshape(-1, LANES)


def _unpack_small(packed, shapes):
    out, off = [], 0
    flat = packed.reshape(-1)
    for shp in shapes:
        size = math.prod(shp)
        out.append(flat[off:off + size].reshape(shp))
        off += size + (-size) % PACK_ROWS
    return out


def kernel(x, mem, ffn1_norm, ffn1_w_gate, ffn1_w_up, ffn1_w_down, mix_norm, w_in, pool_w, pool_scale, w_pool_proj, ssm_a_re, ssm_a_im, ssm_log_dt, ssm_b_re, ssm_b_im, ssm_c_re, ssm_c_im, ssm_d, w_glu_val, w_glu_gate, w_mix_out, xattn_norm, mem_norm, w_q, w_kv, w_xo, ffn2_norm, ffn2_w_gate, ffn2_w_up, ffn2_w_down, final_norm, loss_target, m_ffn1_norm, m_ffn1_w_gate, m_ffn1_w_up, m_ffn1_w_down, m_mix_norm, m_w_in, m_pool_w, m_pool_scale, m_w_pool_proj, m_ssm_a_re, m_ssm_a_im, m_ssm_log_dt, m_ssm_b_re, m_ssm_b_im, m_ssm_c_re, m_ssm_c_im, m_ssm_d, m_w_glu_val, m_w_glu_gate, m_w_mix_out, m_xattn_norm, m_mem_norm, m_w_q, m_w_kv, m_w_xo, m_ffn2_norm, m_ffn2_w_gate, m_ffn2_w_up, m_ffn2_w_down, m_final_norm, v_ffn1_norm, v_ffn1_w_gate, v_ffn1_w_up, v_ffn1_w_down, v_mix_norm, v_w_in, v_pool_w, v_pool_scale, v_w_pool_proj, v_ssm_a_re, v_ssm_a_im, v_ssm_log_dt, v_ssm_b_re, v_ssm_b_im, v_ssm_c_re, v_ssm_c_im, v_ssm_d, v_w_glu_val, v_w_glu_gate, v_w_mix_out, v_xattn_norm, v_mem_norm, v_w_q, v_w_kv, v_w_xo, v_ffn2_norm, v_ffn2_w_gate, v_ffn2_w_up, v_ffn2_w_down, v_final_norm):
    given = dict(locals())
    wts = {n: given[n] for n in WEIGHTS}
    moms = {n: (given["m_" + n], given["v_" + n]) for n in WEIGHTS}
    x2, mem2, tgt2 = x[0], mem[0], loss_target[0]
    d = x2.shape[1]

    shards = [_to_rows(n, wts[n][0], d).astype(BF16) for n in BIG]
    full = _allgather("weight_allgather", shards)
    wt = {}
    for n, f in zip(BIG, full):
        shard = wts[n][0].shape
        if n in COL_SHARDED:
            wt[n] = f.reshape(N_DEV * shard[1], shard[0])
        else:
            wt[n] = f.reshape(N_DEV * shard[0], shard[1])
    sm = {n: (wts[n].reshape(1, -1) if wts[n].ndim <= 2 else wts[n][0]) for n in SMALL}

    loss, dx, gb, gs = _local_step(x2, mem2, tgt2, wt, sm)

    blocks = [gb[n].reshape(N_DEV, -1, d) for n in BIG]
    rows = sum(b.shape[1] for b in blocks)
    pad_rows = (-rows) % GRAD_ROW_TILE
    if pad_rows:
        blocks.append(jnp.zeros((N_DEV, pad_rows, d), BF16))
    packed = jnp.concatenate(blocks, axis=1)
    r_tot = rows + pad_rows
    own, got = _exchange_cores(packed)
    pair = _ew("grad_pair_sum", lambda a, b: (a.astype(F32) + b.astype(F32),),
               [own.reshape(4 * r_tot, d), got.reshape(4 * r_tot, d)], [BF16])[0]
    quad = _exchange_chips(pair.reshape(4, r_tot, d))
    g_rows = _sum_slots("grad_chip_sum", quad, F32)

    small_vals = [gs[n] for n in SMALL] + [loss[:, :1]]
    gathered = _allgather("small_allgather", [_pack_small(small_vals)])[0]
    small_sum = _sum_slots("small_sum", gathered.reshape(N_DEV, -1, LANES), F32)

    out_g, out_d, out_m, out_v = {}, {}, {}, {}
    off = 0
    for n, b in zip(BIG, blocks):
        shard = wts[n].shape
        g_full = _from_rows(n, g_rows[off:off + b.shape[1]], shard[1:]).reshape(shard)
        off += b.shape[1]
        two_d = (-1, shard[-1])
        dl, m2, v2 = _adamw("adamw_" + n, wts[n].reshape(two_d), g_full.reshape(two_d), moms[n][0].reshape(two_d),
                            moms[n][1].reshape(two_d))
        out_g[n], out_d[n], out_m[n], out_v[n] = g_full, dl.reshape(shard), m2.reshape(shard), v2.reshape(shard)
    zero = jnp.zeros((1, 1), F32)
    shapes = [wts[n].shape for n in SMALL] + [(1, 1)]
    w_pack = _pack_small([wts[n] for n in SMALL] + [zero])
    m_pack = _pack_small([moms[n][0] for n in SMALL] + [zero])
    v_pack = _pack_small([moms[n][1] for n in SMALL] + [zero])
    dl, m2, v2 = _adamw("adamw_small", w_pack, small_sum, m_pack, v_pack)
    for dst, src in ((out_g, small_sum), (out_d, dl), (out_m, m2), (out_v, v2)):
        vals = _unpack_small(src, shapes)
        for n, val in zip(SMALL, vals):
            dst[n] = val
        if dst is out_g:
            total_loss = vals[-1].reshape(())

    return (total_loss, dx[None], *[out_g[n] for n in WEIGHTS], *[out_d[n] for n in WEIGHTS],
            *[out_m[n] for n in WEIGHTS], *[out_v[n] for n in WEIGHTS])
```

```python
import functools
import math

import jax
import jax.numpy as jnp
from jax import lax
from jax.experimental import pallas as pl
from jax.experimental.pallas import tpu as pltpu

F32 = jnp.float32
BF16 = jnp.bfloat16
EPS = 1e-6
N_XHEADS = 4
POOL_WINDOWS = (2, 4, 8, 16)
ADAM_LR = 0.001
ADAM_B1 = 0.9
ADAM_B2 = 0.999
ADAM_EPS = 1e-08
ADAM_WD = 0.01
ADAM_STEP = 10
N_DEV = 8
VMEM_LIMIT_V7X = 48 * 1024 * 1024
LANES = 128
SUBLANES = 8
POOL_PAD = 16
MESH = pl.DeviceIdType.MESH
ANY = pl.BlockSpec(memory_space=pl.ANY)

_DIMS = {
    "nt": (((1,), (1,)), ((), ())),
    "nn": (((1,), (0,)), ((), ())),
    "tn": (((0,), (0,)), ((), ())),
}


def _pick(dim, pref, mult=LANES):
    if dim <= pref:
        return dim
    for t in range(pref - pref % mult, 0, -mult):
        if dim % t == 0:
            return t
    return dim


def _params(sem):
    return pltpu.CompilerParams(dimension_semantics=sem, vmem_limit_bytes=VMEM_LIMIT_V7X)


def _tile(tm, tn, coff=0):
    return pl.BlockSpec((tm, tn), lambda i, j: (i, j + coff))


def _rowvec(tn, coff=0):
    return pl.BlockSpec((1, tn), lambda i, j: (0, j + coff))


def _out(m, n, dtype):
    return jax.ShapeDtypeStruct((m, n), dtype)


def _mm(name, form, a_list, b_list, groups, m, n, tm, tn, extras, epilogue, outs):
    na, nb, ne = len(a_list), len(b_list), len(extras)

    def a_spec(a):
        if form == "tn":
            return pl.BlockSpec((a.shape[0], tm), lambda i, j: (0, i))
        return pl.BlockSpec((tm, a.shape[1]), lambda i, j: (i, 0))

    def b_spec(b):
        if form == "nt":
            return pl.BlockSpec((tn, b.shape[1]), lambda i, j: (j, 0))
        return pl.BlockSpec((b.shape[0], tn), lambda i, j: (0, j))

    def body(*refs):
        a_refs, b_refs = refs[:na], refs[na:na + nb]
        e_refs, o_refs = refs[na + nb:na + nb + ne], refs[na + nb + ne:]
        a_vals, b_vals, accs = {}, {}, []
        for group in groups:
            acc = None
            for ai, bi in group:
                if ai not in a_vals:
                    a_vals[ai] = a_refs[ai][...].astype(BF16)
                if bi not in b_vals:
                    b_vals[bi] = b_refs[bi][...].astype(BF16)
                d = lax.dot_general(a_vals[ai], b_vals[bi], _DIMS[form], preferred_element_type=F32)
                acc = d if acc is None else acc + d
            accs.append(acc)
        res = epilogue(accs, *[e[...] for e in e_refs])
        for o_ref, r in zip(o_refs, res):
            o_ref[...] = r.astype(o_ref.dtype)

    out_specs = [_tile(tm, tn) if s is None else s for _, s in outs]
    res = pl.pallas_call(
        body, name=name, grid=(m // tm, n // tn),
        in_specs=[a_spec(a) for a in a_list] + [b_spec(b) for b in b_list] + [s for _, s in extras],
        out_specs=out_specs, out_shape=[o for o, _ in outs],
        compiler_params=_params(("parallel", "parallel")),
    )(*a_list, *b_list, *[e for e, _ in extras])
    return res


def _mm1(name, form, a, b, m, n, tm, tn, dtype, scale=None):
    epi = (lambda accs: (accs[0],)) if scale is None else (lambda accs: (accs[0] * scale,))
    return _mm(name, form, [a], [b], [[(0, 0)]], m, n, tm, tn, [], epi, [(_out(m, n, dtype), None)])[0]


def _rms_fwd(name, h, g):
    t, d = h.shape
    tm = _pick(t, 512, SUBLANES)

    def body(h_ref, g_ref, n_ref):
        hv = h_ref[...]
        r = lax.rsqrt(jnp.mean(hv * hv, axis=-1, keepdims=True) + EPS)
        n_ref[...] = ((hv * r) * g_ref[...]).astype(BF16)

    return pl.pallas_call(
        body, name=name, grid=(t // tm,),
        in_specs=[pl.BlockSpec((tm, d), lambda i: (i, 0)), pl.BlockSpec((1, d), lambda i: (0, 0))],
        out_specs=pl.BlockSpec((tm, d), lambda i: (i, 0)), out_shape=_out(t, d, BF16),
        compiler_params=_params(("parallel",)),
    )(h, g)


def _rms_bwd(name, h, g, dn, dres=None):
    t, d = h.shape
    tm = _pick(t, 256, SUBLANES)
    need_dh = dres is not None

    def body(*refs):
        if need_dh:
            h_ref, g_ref, dn_ref, dres_ref, dh_ref, dhb_ref, dg_ref = refs
        else:
            h_ref, g_ref, dn_ref, dg_ref = refs
        hv = h_ref[...]
        r = lax.rsqrt(jnp.mean(hv * hv, axis=-1, keepdims=True) + EPS)
        nh = hv * r
        dnv = dn_ref[...].astype(F32)

        @pl.when(pl.program_id(0) == 0)
        def _():
            dg_ref[...] = jnp.zeros_like(dg_ref)

        dg_ref[...] += jnp.sum(dnv * nh, axis=0, keepdims=True)
        if need_dh:
            dng = dnv * g_ref[...]
            dh = dres_ref[...] + r * (dng - nh * jnp.mean(dng * nh, axis=-1, keepdims=True))
            dh_ref[...] = dh
            dhb_ref[...] = dh.astype(BF16)

    row = pl.BlockSpec((tm, d), lambda i: (i, 0))
    vec = pl.BlockSpec((1, d), lambda i: (0, 0))
    if need_dh:
        return pl.pallas_call(
            body, name=name, grid=(t // tm,), in_specs=[row, vec, row, row], out_specs=[row, row, vec],
            out_shape=[_out(t, d, F32), _out(t, d, BF16), _out(1, d, F32)], compiler_params=_params(("arbitrary",)),
        )(h, g, dn, dres)
    return pl.pallas_call(
        body, name=name, grid=(t // tm,), in_specs=[row, vec, row], out_specs=vec,
        out_shape=_out(1, d, F32), compiler_params=_params(("arbitrary",)),
    )(h, g, dn)


def _loss_head(h, g, tgt):
    t, d = h.shape
    tm = _pick(t, 256, SUBLANES)

    def body(h_ref, g_ref, t_ref, dh_ref, dhb_ref, dg_ref, loss_ref):
        hv = h_ref[...]
        r = lax.rsqrt(jnp.mean(hv * hv, axis=-1, keepdims=True) + EPS)
        nh = hv * r
        err = nh * g_ref[...] - t_ref[...]

        @pl.when(pl.program_id(0) == 0)
        def _():
            dg_ref[...] = jnp.zeros_like(dg_ref)
            loss_ref[...] = jnp.zeros_like(loss_ref)

        per_row = jnp.mean(err * err, axis=-1, keepdims=True)
        loss_ref[...] += 0.5 * jnp.sum(per_row, axis=0, keepdims=True)
        dy = err * (1.0 / d)
        dg_ref[...] += jnp.sum(dy * nh, axis=0, keepdims=True)
        dng = dy * g_ref[...]
        dh = r * (dng - nh * jnp.mean(dng * nh, axis=-1, keepdims=True))
        dh_ref[...] = dh
        dhb_ref[...] = dh.astype(BF16)

    row = pl.BlockSpec((tm, d), lambda i: (i, 0))
    vec = pl.BlockSpec((1, d), lambda i: (0, 0))
    return pl.pallas_call(
        body, name="loss_head", grid=(t // tm,), in_specs=[row, vec, row],
        out_specs=[row, row, vec, pl.BlockSpec((1, LANES), lambda i: (0, 0))],
        out_shape=[_out(t, d, F32), _out(t, d, BF16), _out(1, d, F32), _out(1, LANES, F32)],
        compiler_params=_params(("arbitrary",)),
    )(h, g, tgt)


def _ffn_fwd(tag, h, g, wg_t, wu_t, wd):
    t, d = h.shape
    f = wd.shape[0]
    n = _rms_fwd(tag + "_norm", h, g)
    tm, tn = _pick(t, 1024), _pick(f, 256)

    def up_epi(accs):
        a, b = accs
        return a, b, (a * jax.nn.sigmoid(a)) * b

    a, b, hid = _mm(tag + "_up", "nt", [n], [wg_t, wu_t], [[(0, 0)], [(0, 1)]], t, f, tm, tn, [], up_epi,
                    [(_out(t, f, BF16), None)] * 3)
    tm2, tn2 = _pick(t, 512), _pick(d, 256)
    h_out = _mm(tag + "_down", "nn", [hid], [wd], [[(0, 0)]], t, d, tm2, tn2, [(h, _tile(tm2, tn2))],
                lambda accs, hin: (hin + 0.5 * accs[0],), [(_out(t, d, F32), None)])[0]
    return h_out, (n, a, b, hid)


def _ffn_bwd(tag, h, g, wg_t, wu_t, wd, saved, dh, dh_bf):
    n, a, b, hid = saved
    t, d = h.shape
    f = wd.shape[0]
    tm, tn = _pick(t, 1024), _pick(f, 256)

    def hid_epi(accs, av, bv):
        dhid = 0.5 * accs[0]
        av, bv = av.astype(F32), bv.astype(F32)
        sig = jax.nn.sigmoid(av)
        da = dhid * bv * (sig * (1.0 + av * (1.0 - sig)))
        db = dhid * (av * sig)
        return da, db

    da, db = _mm(tag + "_bwd_hid", "nt", [dh_bf], [wd], [[(0, 0)]], t, f, tm, tn,
                 [(a, _tile(tm, tn)), (b, _tile(tm, tn))], hid_epi, [(_out(t, f, BF16), None)] * 2)
    tw = _pick(f, 256)
    d_wd = _mm1(tag + "_dwd", "tn", hid, dh_bf, f, d, tw, d, BF16, scale=0.5)
    d_wg = _mm1(tag + "_dwg", "tn", da, n, f, d, tw, d, BF16)
    d_wu = _mm1(tag + "_dwu", "tn", db, n, f, d, tw, d, BF16)
    tm2, tn2 = _pick(t, 512), _pick(d, 256)
    dn = _mm(tag + "_dn", "nn", [da, db], [wg_t, wu_t], [[(0, 0), (1, 1)]], t, d, tm2, tn2, [],
             lambda accs: (accs[0],), [(_out(t, d, F32), None)])[0]
    dh_in, dh_in_bf, dg = _rms_bwd(tag + "_norm_bwd", h, g, dn, dh)
    return dh_in, dh_in_bf, dg, d_wg, d_wu, d_wd


def _window_sum(win, offsets):
    n = win.shape[0]
    acc = None
    for j in offsets:
        term = win if j == 0 else pltpu.roll(win, (-j) % n, 0)
        acc = term if acc is None else acc + term
    return acc


def _pool_counts(r0, ch, c, left, right, t):
    pos = r0 + lax.broadcasted_iota(jnp.int32, (ch, c), 0)
    return (jnp.minimum(pos + right + 1, t) - jnp.maximum(pos - left, 0)).astype(F32)


def _pool_fwd(proj, pool_w_bf, pool_scale):
    t = proj.shape[0]
    ng, c, _ = pool_w_bf.shape
    ch = _pick(t, 256, SUBLANES)
    pad = POOL_PAD

    def body(p_ref, w_ref, s_ref, pooled_ref, pm_ref, buf):
        grp = pl.program_id(0)
        buf[pl.ds(0, pad), :] = jnp.zeros((pad, c), F32)
        buf[pl.ds(pad + t, pad), :] = jnp.zeros((pad, c), F32)

        def fill(ci, carry):
            r0 = pl.multiple_of(ci * ch, SUBLANES)
            buf[pl.ds(pl.multiple_of(r0 + pad, SUBLANES), ch), :] = p_ref[pl.ds(r0, ch), :]
            return carry

        lax.fori_loop(0, t // ch, fill, 0)
        for gi, w in enumerate(POOL_WINDOWS):
            left = w // 2
            right = w - 1 - left

            @pl.when(grp == gi)
            def _(left=left, right=right):
                def chunk(ci, carry):
                    r0 = pl.multiple_of(ci * ch, SUBLANES)
                    win = buf[pl.ds(r0, ch + 2 * pad), :]
                    s = _window_sum(win, range(-left, right + 1))[pad:pad + ch]
                    pooled = s / _pool_counts(r0, ch, c, left, right, t) - win[pad:pad + ch]
                    pooled_bf = pooled.astype(BF16)
                    mixed = jnp.dot(pooled_bf, w_ref[0], preferred_element_type=F32)
                    pooled_ref[pl.ds(r0, ch), :] = pooled_bf
                    pm_ref[pl.ds(r0, ch), :] = (mixed * s_ref[...]).astype(BF16)
                    return carry

                lax.fori_loop(0, t // ch, chunk, 0)

    col = pl.BlockSpec((t, c), lambda g: (0, g))
    return pl.pallas_call(
        body, name="pool_fwd", grid=(ng,),
        in_specs=[col, pl.BlockSpec((1, c, c), lambda g: (g, 0, 0)), pl.BlockSpec((1, c), lambda g: (0, g))],
        out_specs=[col, col], out_shape=[_out(t, ng * c, BF16), _out(t, ng * c, BF16)],
        scratch_shapes=[pltpu.VMEM((t + 2 * pad, c), F32)],
        compiler_params=_params(("parallel",)),
    )(proj, pool_w_bf, pool_scale)


def _pool_bwd(pooled, dpm, pool_w_bf, pool_scale):
    t = pooled.shape[0]
    ng, c, _ = pool_w_bf.shape
    ch = _pick(t, 256, SUBLANES)
    pad = POOL_PAD

    def body(pooled_ref, dpm_ref, w_ref, s_ref, dp_ref, dw_ref, ds_ref, buf, raw):
        grp = pl.program_id(0)
        buf[pl.ds(0, pad), :] = jnp.zeros((pad, c), F32)
        buf[pl.ds(pad + t, pad), :] = jnp.zeros((pad, c), F32)
        dw_ref[...] = jnp.zeros_like(dw_ref)
        ds_ref[...] = jnp.zeros_like(ds_ref)
        for gi, w in enumerate(POOL_WINDOWS):
            left = w // 2
            right = w - 1 - left

            @pl.when(grp == gi)
            def _(left=left, right=right):
                def first(ci, carry):
                    r0 = pl.multiple_of(ci * ch, SUBLANES)
                    pv = pooled_ref[pl.ds(r0, ch), :]
                    dpm_v = dpm_ref[pl.ds(r0, ch), :]
                    mixed = jnp.dot(pv, w_ref[0], preferred_element_type=F32)
                    ds_ref[...] += jnp.sum(dpm_v * mixed, axis=0, keepdims=True)
                    dmixed = (dpm_v * s_ref[...]).astype(BF16)
                    dw_ref[0] += lax.dot_general(pv, dmixed, _DIMS["tn"], preferred_element_type=F32)
                    dpooled = lax.dot_general(dmixed, w_ref[0], _DIMS["nt"], preferred_element_type=F32)
                    raw[pl.ds(r0, ch), :] = dpooled
                    buf[pl.ds(pl.multiple_of(r0 + pad, SUBLANES), ch), :] = (
                        dpooled / _pool_counts(r0, ch, c, left, right, t))
                    return carry

                lax.fori_loop(0, t // ch, first, 0)

                def second(ci, carry):
                    r0 = pl.multiple_of(ci * ch, SUBLANES)
                    win = buf[pl.ds(r0, ch + 2 * pad), :]
                    s = _window_sum(win, range(-right, left + 1))[pad:pad + ch]
                    dp_ref[pl.ds(r0, ch), :] = (s - raw[pl.ds(r0, ch), :]).astype(BF16)
                    return carry

                lax.fori_loop(0, t // ch, second, 0)

    col = pl.BlockSpec((t, c), lambda g: (0, g))
    return pl.pallas_call(
        body, name="pool_bwd", grid=(ng,),
        in_specs=[col, col, pl.BlockSpec((1, c, c), lambda g: (g, 0, 0)), pl.BlockSpec((1, c), lambda g: (0, g))],
        out_specs=[col, pl.BlockSpec((1, c, c), lambda g: (g, 0, 0)), pl.BlockSpec((1, c), lambda g: (0, g))],
        out_shape=[_out(t, ng * c, BF16), jax.ShapeDtypeStruct((ng, c, c), F32), _out(1, ng * c, F32)],
        scratch_shapes=[pltpu.VMEM((t + 2 * pad, c), F32), pltpu.VMEM((t, c), F32)],
        compiler_params=_params(("parallel",)),
    )(pooled, dpm, pool_w_bf, pool_scale)


def _discretise(a_re, a_im, log_dt, b_re, b_im):
    dt = jnp.exp(log_dt)
    mag = jnp.exp(dt * a_re)
    ang = dt * a_im
    abr = mag * jnp.cos(ang)
    abi = mag * jnp.sin(ang)
    den = a_re * a_re + a_im * a_im
    nr = abr - 1.0
    qr = (nr * a_re + abi * a_im) / den
    qi = (abi * a_re - nr * a_im) / den
    return abr, abi, qr * b_re - qi * b_im, qr * b_im + qi * b_re


def _ssm_disc(cols):
    n, hh = cols[3].shape

    def body(ar, ai, ld, br, bi, o1, o2, o3, o4):
        res = _discretise(ar[...], ai[...], ld[...], br[...], bi[...])
        for o, r in zip((o1, o2, o3, o4), res):
            o[...] = r

    return pl.pallas_call(
        body, name="ssm_disc",
        out_shape=[_out(n, 1, F32), _out(n, 1, F32), _out(n, hh, F32), _out(n, hh, F32)],
    )(*cols)


def _ssm_disc_bwd(cols, cots):
    n, hh = cols[3].shape

    def body(ar, ai, ld, br, bi, c1, c2, c3, c4, o1, o2, o3, o4, o5):
        _, vjp = jax.vjp(_discretise, ar[...], ai[...], ld[...], br[...], bi[...])
        res = vjp((c1[...], c2[...], c3[...], c4[...]))
        for o, r in zip((o1, o2, o3, o4, o5), res):
            o[...] = r

    return pl.pallas_call(
        body, name="ssm_disc_bwd",
        out_shape=[_out(n, 1, F32)] * 3 + [_out(n, hh, F32)] * 2,
    )(*cols, *cots)


def _rowsum(name, a):
    r, _ = a.shape

    def body(a_ref, o_ref):
        o_ref[...] = jnp.sum(a_ref[...], axis=-1, keepdims=True)

    return pl.pallas_call(body, name=name, out_shape=_out(r, 1, F32))(a)


def _cmul(pr, pi, qr, qi):
    return pr * qr - pi * qi, pr * qi + pi * qr


def _scan(name, u, ar, ai, reverse, conj):
    t, s2 = u.shape
    s = s2 // 2
    w = _pick(s, 512)
    tc = _pick(t, 512, SUBLANES)
    n_t, n_w = t // tc, s // w
    groups = tc // SUBLANES
    last = 0 if reverse else SUBLANES - 1

    def body(ar_ref, ai_ref, ur_ref, ui_ref, xr_ref, xi_ref, cr_ref, ci_ref):
        @pl.when(pl.program_id(1) == 0)
        def _():
            cr_ref[...] = jnp.zeros_like(cr_ref)
            ci_ref[...] = jnp.zeros_like(ci_ref)

        a1r = ar_ref[...]
        a1i = -ai_ref[...] if conj else ai_ref[...]
        a2r, a2i = _cmul(a1r, a1i, a1r, a1i)
        a4r, a4i = _cmul(a2r, a2i, a2r, a2i)
        row = lax.broadcasted_iota(jnp.int32, (SUBLANES, w), 0)
        pwr = jnp.zeros((SUBLANES, w), F32)
        pwi = jnp.zeros((SUBLANES, w), F32)
        cur_r, cur_i = a1r, a1i
        for k in range(SUBLANES):
            rk = SUBLANES - 1 - k if reverse else k
            pwr = jnp.where(row == rk, cur_r, pwr)
            pwi = jnp.where(row == rk, cur_i, pwi)
            cur_r, cur_i = _cmul(cur_r, cur_i, a1r, a1i)
        steps = ((1, a1r, a1i), (2, a2r, a2i), (4, a4r, a4i))

        def one(i, carry):
            g = groups - 1 - i if reverse else i
            r0 = pl.multiple_of(g * SUBLANES, SUBLANES)
            br = ur_ref[pl.ds(r0, SUBLANES), :]
            bi = ui_ref[pl.ds(r0, SUBLANES), :]
            for dist, pr, pi in steps:
                if reverse:
                    keep = row < SUBLANES - dist
                    shift = SUBLANES - dist
                else:
                    keep = row >= dist
                    shift = dist
                sr = jnp.where(keep, pltpu.roll(br, shift, 0), 0.0)
                si = jnp.where(keep, pltpu.roll(bi, shift, 0), 0.0)
                br, bi = br + pr * sr - pi * si, bi + pr * si + pi * sr
            cr = cr_ref[pl.ds(last, 1), :]
            ci = ci_ref[pl.ds(last, 1), :]
            xr = br + pwr * cr - pwi * ci
            xi = bi + pwr * ci + pwi * cr
            xr_ref[pl.ds(r0, SUBLANES), :] = xr
            xi_ref[pl.ds(r0, SUBLANES), :] = xi
            cr_ref[...] = xr
            ci_ref[...] = xi
            return carry

        lax.fori_loop(0, groups, one, 0)

    def tmap(k):
        return n_t - 1 - k if reverse else k

    re_blk = pl.BlockSpec((tc, w), lambda cb, k: (tmap(k), cb))
    im_blk = pl.BlockSpec((tc, w), lambda cb, k: (tmap(k), cb + n_w))
    a_blk = pl.BlockSpec((1, w), lambda cb, k: (0, cb))
    xr, xi = pl.pallas_call(
        body, name=name, grid=(n_w, n_t), in_specs=[a_blk, a_blk, re_blk, im_blk],
        out_specs=[pl.BlockSpec((tc, w), lambda cb, k: (tmap(k), cb))] * 2,
        out_shape=[_out(t, s, F32), _out(t, s, F32)],
        scratch_shapes=[pltpu.VMEM((SUBLANES, w), F32), pltpu.VMEM((SUBLANES, w), F32)],
        compiler_params=_params(("parallel", "arbitrary")),
    )(ar, ai, u, u)
    return xr, xi


def _ssm_da(name, lr, li, xr, xi, reverse):
    t, s = xr.shape
    w = _pick(s, 512)
    tc = _pick(t, 256, SUBLANES)
    n_t, n_w = t // tc, s // w

    def body(lr_ref, li_ref, xr_ref, xi_ref, dar_ref, dai_ref, pr_ref, pi_ref):
        @pl.when(pl.program_id(1) == 0)
        def _():
            pr_ref[...] = jnp.zeros_like(pr_ref)
            pi_ref[...] = jnp.zeros_like(pi_ref)
            dar_ref[...] = jnp.zeros_like(dar_ref)
            dai_ref[...] = jnp.zeros_like(dai_ref)

        row = lax.broadcasted_iota(jnp.int32, (tc, w), 0)
        xrv, xiv = xr_ref[...], xi_ref[...]
        if reverse:
            keep, shift, edge = row < tc - 1, tc - 1, 0
        else:
            keep, shift, edge = row >= 1, 1, tc - 1
        xsr = jnp.where(keep, pltpu.roll(xrv, shift, 0), pr_ref[pl.ds(0, 1), :])
        xsi = jnp.where(keep, pltpu.roll(xiv, shift, 0), pi_ref[pl.ds(0, 1), :])
        lrv, liv = lr_ref[...], li_ref[...]
        dar_ref[...] += jnp.sum(lrv * xsr + liv * xsi, axis=0, keepdims=True)
        dai_ref[...] += jnp.sum(liv * xsr - lrv * xsi, axis=0, keepdims=True)
        pr_ref[pl.ds(0, 1), :] = xr_ref[pl.ds(edge, 1), :]
        pi_ref[pl.ds(0, 1), :] = xi_ref[pl.ds(edge, 1), :]

    def tmap(k):
        return n_t - 1 - k if reverse else k

    blk = pl.BlockSpec((tc, w), lambda cb, k: (tmap(k), cb))
    vec = pl.BlockSpec((1, w), lambda cb, k: (0, cb))
    return pl.pallas_call(
        body, name=name, grid=(n_w, n_t), in_specs=[blk] * 4, out_specs=[vec, vec],
        out_shape=[_out(1, s, F32), _out(1, s, F32)],
        scratch_shapes=[pltpu.VMEM((SUBLANES, w), F32), pltpu.VMEM((SUBLANES, w), F32)],
        compiler_params=_params(("parallel", "arbitrary")),
    )(lr, li, xr, xi)


def _colsum_prod(name, a, b, b_coff=0):
    t, n = a.shape
    tm = _pick(t, 512, SUBLANES)

    def body(a_ref, b_ref, o_ref):
        @pl.when(pl.program_id(0) == 0)
        def _():
            o_ref[...] = jnp.zeros_like(o_ref)

        o_ref[...] += jnp.sum(a_ref[...].astype(F32) * b_ref[...].astype(F32), axis=0, keepdims=True)

    return pl.pallas_call(
        body, name=name, grid=(t // tm,),
        in_specs=[pl.BlockSpec((tm, n), lambda i: (i, 0)), pl.BlockSpec((tm, n), lambda i: (i, b_coff))],
        out_specs=pl.BlockSpec((1, n), lambda i: (0, 0)), out_shape=_out(1, n, F32),
        compiler_params=_params(("arbitrary",)),
    )(a, b)


def _bd_in(bb, g, p, hh):
    blk = bb.reshape(g, p, hh).transpose(0, 2, 1)
    eye = jnp.eye(g, dtype=bool)[:, None, :, None]
    return jnp.where(eye, blk[:, :, None, :], 0.0).reshape(g * hh, g * p)


def _bd_out(cc, g, p, hh):
    blk = cc.transpose(0, 2, 1)
    eye = jnp.eye(g, dtype=bool)[:, None, :, None]
    return jnp.where(eye, blk[:, :, None, :], 0.0).reshape(g * p, g * hh)


def _diag_in(dmat, g, p, hh):
    d4 = dmat.reshape(g, hh, g, p)
    diag = jnp.stack([d4[k, :, k, :] for k in range(g)], axis=0)
    return diag.transpose(0, 2, 1).reshape(g * p, hh)


def _diag_out(dmat, g, p, hh):
    d4 = dmat.reshape(g, p, g, hh)
    diag = jnp.stack([d4[k, :, k, :] for k in range(g)], axis=0)
    return diag.transpose(0, 2, 1)


def _softmax(qh, kh, scale):
    s = lax.dot_general(qh, kh, _DIMS["nt"], preferred_element_type=F32) * scale
    e = jnp.exp(s - jnp.max(s, axis=-1, keepdims=True))
    return e / jnp.sum(e, axis=-1, keepdims=True)


def _attn_fwd(q, kv):
    t, d = q.shape
    mm_ = kv.shape[0]
    hd = d // N_XHEADS
    scale = 1.0 / math.sqrt(hd)
    tm = _pick(t, 512, SUBLANES)

    def body(q_ref, kv_ref, o_ref):
        for h in range(N_XHEADS):
            sl = pl.ds(h * hd, hd)
            p = _softmax(q_ref[:, sl], kv_ref[:, sl], scale)
            o_ref[:, sl] = jnp.dot(p.astype(BF16), kv_ref[:, pl.ds(d + h * hd, hd)],
                                   preferred_element_type=F32).astype(BF16)

    return pl.pallas_call(
        body, name="attn_fwd", grid=(t // tm,),
        in_specs=[pl.BlockSpec((tm, d), lambda i: (i, 0)), pl.BlockSpec((mm_, 2 * d), lambda i: (0, 0))],
        out_specs=pl.BlockSpec((tm, d), lambda i: (i, 0)), out_shape=_out(t, d, BF16),
        compiler_params=_params(("parallel",)),
    )(q, kv)


def _attn_bwd(q, kv, do):
    t, d = q.shape
    mm_ = kv.shape[0]
    hd = d // N_XHEADS
    scale = 1.0 / math.sqrt(hd)
    tm = _pick(t, 512, SUBLANES)

    def body(q_ref, kv_ref, do_ref, dq_ref, dkv_ref):
        @pl.when(pl.program_id(0) == 0)
        def _():
            dkv_ref[...] = jnp.zeros_like(dkv_ref)

        for h in range(N_XHEADS):
            sl = pl.ds(h * hd, hd)
            vsl = pl.ds(d + h * hd, hd)
            qh, kh, doh = q_ref[:, sl], kv_ref[:, sl], do_ref[:, sl]
            p = _softmax(qh, kh, scale)
            dp = lax.dot_general(doh, kv_ref[:, vsl], _DIMS["nt"], preferred_element_type=F32)
            dkv_ref[:, vsl] += lax.dot_general(p.astype(BF16), doh, _DIMS["tn"], preferred_element_type=F32)
            ds = (p * (dp - jnp.sum(dp * p, axis=-1, keepdims=True)) * scale).astype(BF16)
            dq_ref[:, sl] = jnp.dot(ds, kh, preferred_element_type=F32).astype(BF16)
            dkv_ref[:, sl] += lax.dot_general(ds, qh, _DIMS["tn"], preferred_element_type=F32)

    row = pl.BlockSpec((tm, d), lambda i: (i, 0))
    full = pl.BlockSpec((mm_, 2 * d), lambda i: (0, 0))
    return pl.pallas_call(
        body, name="attn_bwd", grid=(t // tm,), in_specs=[row, full, row], out_specs=[row, full],
        out_shape=[_out(t, d, BF16), _out(mm_, 2 * d, F32)], compiler_params=_params(("arbitrary",)),
    )(q, kv, do)


def _ew(name, fn, ins, outs, rows_pref=256):
    r, c = ins[0].shape
    tr = _pick(r, rows_pref, SUBLANES)
    ni = len(ins)

    def body(*refs):
        res = fn(*[x[...] for x in refs[:ni]])
        for o_ref, v in zip(refs[ni:], res):
            o_ref[...] = v.astype(o_ref.dtype)

    blk = pl.BlockSpec((tr, c), lambda i: (i, 0))
    return pl.pallas_call(
        body, name=name, grid=(r // tr,), in_specs=[blk] * ni, out_specs=[blk] * len(outs),
        out_shape=[_out(r, c, dt) for dt in outs], compiler_params=_params(("parallel",)),
    )(*ins)


def _sum_slots(name, a, dtype):
    s, r, c = a.shape
    tr = _pick(r, 256, SUBLANES)

    def body(a_ref, o_ref):
        acc = a_ref[0].astype(F32)
        for k in range(1, s):
            acc = acc + a_ref[k].astype(F32)
        o_ref[...] = acc.astype(o_ref.dtype)

    return pl.pallas_call(
        body, name=name, grid=(r // tr,), in_specs=[pl.BlockSpec((s, tr, c), lambda i: (0, i, 0))],
        out_specs=pl.BlockSpec((tr, c), lambda i: (i, 0)), out_shape=_out(r, c, dtype),
        compiler_params=_params(("parallel",)),
    )(a)


def _adamw(name, w, g, m, v):
    bc1 = 1.0 - ADAM_B1 ** ADAM_STEP
    bc2 = 1.0 - ADAM_B2 ** ADAM_STEP

    def fn(wv, gv, mv, vv):
        m2 = ADAM_B1 * mv + (1.0 - ADAM_B1) * gv
        v2 = ADAM_B2 * vv + (1.0 - ADAM_B2) * (gv * gv)
        delta = -ADAM_LR * ((m2 / bc1) / (jnp.sqrt(v2 / bc2) + ADAM_EPS) + ADAM_WD * wv)
        return delta, m2, v2

    return _ew(name, fn, [w, g, m, v], [F32, F32, F32])


def _allgather(name, arrs):
    n = len(arrs)

    def body(*refs):
        ins, outs = refs[:n], refs[n:2 * n]
        send_sems, recv_sems, local_sems = refs[2 * n:]
        x, y, c = lax.axis_index("x"), lax.axis_index("y"), lax.axis_index("c")
        me, sibling = (x, y, c), (x, y, 1 - c)
        chips = [(1 - x, y), (x, 1 - y), (1 - x, 1 - y)]

        def rows(a, px, py, pc):
            r = ins[a].shape[0]
            return outs[a].at[pl.ds((4 * px + 2 * py + pc) * r, r), :]

        def copy(a, k, block, to, src=None):
            return pltpu.make_async_remote_copy(
                src_ref=rows(a, *block) if src is None else src, dst_ref=rows(a, *block),
                send_sem=send_sems.at[a, k], recv_sem=recv_sems.at[a, k], device_id=to, device_id_type=MESH)

        mine = [pltpu.make_async_copy(ins[a], rows(a, *me), local_sems.at[a]) for a in range(n)]
        for cp in mine:
            cp.start()
        first = []
        for a in range(n):
            first.append(copy(a, 0, me, sibling, src=ins[a]))
            first += [copy(a, 1 + j, me, (*chip, c), src=ins[a]) for j, chip in enumerate(chips)]
        for cp in first:
            cp.start()
        passed = []
        for j, chip in enumerate(chips):
            for a in range(n):
                copy(a, 1 + j, (*chip, c), me).wait_recv()
                cp = copy(a, 4 + j, (*chip, c), sibling)
                cp.start()
                passed.append(cp)
        for a in range(n):
            copy(a, 0, sibling, me).wait_recv()
            for j, chip in enumerate(chips):
                copy(a, 4 + j, (*chip, 1 - c), me).wait_recv()
        for cp in first + passed:
            cp.wait_send()
        for cp in mine:
            cp.wait()

    return pl.pallas_call(
        body, name=name, in_specs=[ANY] * n, out_specs=[ANY] * n,
        out_shape=[_out(N_DEV * a.shape[0], a.shape[1], a.dtype) for a in arrs],
        scratch_shapes=[pltpu.SemaphoreType.DMA((n, 7)), pltpu.SemaphoreType.DMA((n, 7)), pltpu.SemaphoreType.DMA((n,))],
    )(*arrs)


def _exchange_cores(g):
    _, r, c = g.shape
    nck = r // GRAD_ROW_TILE

    def body(g_ref, recv_ref, send_sems, recv_sems):
        x, y, cc = lax.axis_index("x"), lax.axis_index("y"), lax.axis_index("c")
        copies = []
        for q in range(4):
            for k in range(nck):
                rows = pl.ds(k * GRAD_ROW_TILE, GRAD_ROW_TILE)
                copies.append(pltpu.make_async_remote_copy(
                    src_ref=g_ref.at[2 * q + (1 - cc), rows], dst_ref=recv_ref.at[q, rows],
                    send_sem=send_sems.at[q, k], recv_sem=recv_sems.at[q, k], device_id=(x, y, 1 - cc),
                    device_id_type=MESH))
        for cp in copies:
            cp.start()
        for cp in copies:
            cp.wait()

    return pl.pallas_call(
        body, name="grad_exchange_cores", in_specs=[ANY], out_specs=ANY,
        out_shape=jax.ShapeDtypeStruct((4, r, c), g.dtype),
        scratch_shapes=[pltpu.SemaphoreType.DMA((4, nck)), pltpu.SemaphoreType.DMA((4, nck))],
    )(g)


def _pair_sum(g, recv, core):
    _, r, c = g.shape
    tr = GRAD_ROW_TILE

    def body(core_ref, g_ref, r_ref, o_ref):
        o_ref[...] = (g_ref[...].astype(F32) + r_ref[...].astype(F32)).astype(o_ref.dtype)

    blk = pl.BlockSpec((None, tr, c), lambda q, i, core_ref: (q, i, 0))
    return pl.pallas_call(
        body, name="grad_pair_sum",
        grid_spec=pltpu.PrefetchScalarGridSpec(
            num_scalar_prefetch=1, grid=(4, r // tr),
            in_specs=[pl.BlockSpec((None, tr, c), lambda q, i, core_ref: (2 * q + core_ref[0], i, 0)), blk],
            out_specs=blk),
        out_shape=jax.ShapeDtypeStruct((4, r, c), g.dtype), compiler_params=_params(("parallel", "parallel")),
    )(core, g, recv)


def _exchange_chips(p):
    _, r, c = p.shape
    nck = r // GRAD_ROW_TILE

    def body(p_ref, out_ref, send_sems, recv_sems):
        x, y, cc = lax.axis_index("x"), lax.axis_index("y"), lax.axis_index("c")
        copies = []
        for k in range(1, 4):
            px = 1 - x if k >> 1 else x
            py = 1 - y if k & 1 else y
            for j in range(nck):
                rows = pl.ds(j * GRAD_ROW_TILE, GRAD_ROW_TILE)
                copies.append(pltpu.make_async_remote_copy(
                    src_ref=p_ref.at[2 * px + py, rows], dst_ref=out_ref.at[k - 1, rows],
                    send_sem=send_sems.at[k - 1, j], recv_sem=recv_sems.at[k - 1, j], device_id=(px, py, cc),
                    device_id_type=MESH))
        for cp in copies:
            cp.start()
        for cp in copies:
            cp.wait()

    return pl.pallas_call(
        body, name="grad_exchange_chips", in_specs=[ANY], out_specs=ANY,
        out_shape=jax.ShapeDtypeStruct((3, r, c), p.dtype),
        scratch_shapes=[pltpu.SemaphoreType.DMA((3, nck)), pltpu.SemaphoreType.DMA((3, nck))],
    )(p)


def _chip_sum(p, recv, chip):
    _, r, c = p.shape
    tr = GRAD_ROW_TILE

    def body(chip_ref, p_ref, r_ref, o_ref):
        acc = p_ref[...].astype(F32)
        for k in range(3):
            acc = acc + r_ref[k].astype(F32)
        o_ref[...] = acc

    return pl.pallas_call(
        body, name="grad_chip_sum",
        grid_spec=pltpu.PrefetchScalarGridSpec(
            num_scalar_prefetch=1, grid=(r // tr,),
            in_specs=[pl.BlockSpec((None, tr, c), lambda i, chip_ref: (chip_ref[0], i, 0)),
                      pl.BlockSpec((3, tr, c), lambda i, chip_ref: (0, i, 0))],
            out_specs=pl.BlockSpec((tr, c), lambda i, chip_ref: (i, 0))),
        out_shape=_out(r, c, F32), compiler_params=_params(("parallel",)),
    )(chip, p, recv)


def _local_step(x, mem, tgt, wt, sm):
    t, d = x.shape
    n_mem = mem.shape[0]
    d_pool = sm["pool_scale"].shape[1]
    ng, pc = sm["pool_w"].shape[0], sm["pool_w"].shape[1]
    d_ssm = sm["ssm_d"].shape[1]
    _, sg, sp, sh = sm["ssm_b_re"].shape
    n_state = sg * sp
    gb, gs = {}, {}

    h1, ffn1_saved = _ffn_fwd("ffn1", x, sm["ffn1_norm"], wt["ffn1_w_gate"], wt["ffn1_w_up"], wt["ffn1_w_down"])
    u = _rms_fwd("mix_norm", h1, sm["mix_norm"])
    d_in = wt["w_in"].shape[0]
    tm, tn = _pick(t, 1024), _pick(d_in, 256)
    proj = _mm1("in_proj", "nt", u, wt["w_in"], t, d_in, tm, tn, F32)
    off_s = d_pool // d_ssm
    off_gp = (d_pool + d_ssm)
    off_gs = off_gp + d

    pool_w_bf = sm["pool_w"].astype(BF16)
    pooled, pm = _pool_fwd(proj, pool_w_bf, sm["pool_scale"])

    cols = [sm["ssm_a_re"].reshape(-1, 1), sm["ssm_a_im"].reshape(-1, 1),
            jnp.broadcast_to(sm["ssm_log_dt"][:, :, None], (2, sg, sp)).reshape(-1, 1),
            sm["ssm_b_re"].reshape(-1, sh), sm["ssm_b_im"].reshape(-1, sh)]
    abr, abi, bbr, bbi = _ssm_disc(cols)
    abr2, abi2 = abr.reshape(2, n_state), abi.reshape(2, n_state)
    bbr4, bbi4 = bbr.reshape(2, sg * sp, sh), bbi.reshape(2, sg * sp, sh)
    b_re = [_bd_in(bbr4[dr], sg, sp, sh).astype(BF16) for dr in range(2)]
    b_im = [_bd_in(bbi4[dr], sg, sp, sh).astype(BF16) for dr in range(2)]
    c_re = [_bd_out(sm["ssm_c_re"][dr], sg, sp, sh).astype(BF16) for dr in range(2)]
    c_im = [_bd_out(-sm["ssm_c_im"][dr], sg, sp, sh).astype(BF16) for dr in range(2)]
    tms = _pick(t, 512)
    s_bf = _ew("ssm_cast", lambda v: (v,), [proj[:, d_pool:d_pool + d_ssm]], [BF16])[0]
    xs = []
    for dr in range(2):
        u_d = _mm1(f"ssm_in{dr}", "nn", s_bf, jnp.concatenate([b_re[dr], b_im[dr]], axis=1), t, 2 * n_state, tms,
                   _pick(2 * n_state, 512), F32)
        xs.append(_scan(f"ssm_scan{dr}", u_d, abr2[dr:dr + 1], abi2[dr:dr + 1], reverse=(dr == 1), conj=False))
    tmy = _pick(t, 256)
    x_list = [xs[0][0], xs[0][1], xs[1][0], xs[1][1]]
    y = _mm("ssm_out", "nn", x_list, [c_re[0], c_im[0], c_re[1], c_im[1]], [[(k, k) for k in range(4)]], t, d_ssm, tmy,
            d_ssm, [(proj, _tile(tmy, d_ssm, off_s)), (sm["ssm_d"], _rowvec(d_ssm))],
            lambda accs, sv, dv: (sv * dv + accs[0],), [(_out(t, d_ssm, F32), None)])[0]
    ys = _ew("ssm_gelu", lambda v: (jax.nn.gelu(v),), [y], [BF16])[0]

    tmm, tnm = _pick(t, 512), _pick(d, 256)
    gp_spec = _tile(tmm, tnm, off_gp // tnm)
    gs_spec = _tile(tmm, tnm, off_gs // tnm)

    def merge_epi(accs, gpv, gsv):
        z_pool, val, gate = accs
        return (jax.nn.sigmoid(gpv) * z_pool + jax.nn.sigmoid(gsv) * (val * jax.nn.sigmoid(gate)),)

    merged = _mm("mix_merge", "nt", [pm, ys], [wt["w_pool_proj"], wt["w_glu_val"], wt["w_glu_gate"]],
                 [[(0, 0)], [(1, 1)], [(1, 2)]], t, d, tmm, tnm, [(proj, gp_spec), (proj, gs_spec)], merge_epi,
                 [(_out(t, d, BF16), None)])[0]
    res_epi = lambda accs, hin: (hin + accs[0],)
    h2 = _mm("mix_out", "nn", [merged], [wt["w_mix_out"]], [[(0, 0)]], t, d, tmm, tnm, [(h1, _tile(tmm, tnm))],
             res_epi, [(_out(t, d, F32), None)])[0]

    un = _rms_fwd("xattn_norm", h2, sm["xattn_norm"])
    mn = _rms_fwd("mem_norm", mem, sm["mem_norm"])
    q = _mm1("xattn_q", "nn", un, wt["w_q"], t, d, tmm, tnm, BF16)
    kv = _mm1("xattn_kv", "nt", mn, wt["w_kv"], n_mem, 2 * d, n_mem, _pick(2 * d, 512), BF16)
    o = _attn_fwd(q, kv)
    h3 = _mm("xattn_out", "nn", [o], [wt["w_xo"]], [[(0, 0)]], t, d, tmm, tnm, [(h2, _tile(tmm, tnm))],
             res_epi, [(_out(t, d, F32), None)])[0]

    h4, ffn2_saved = _ffn_fwd("ffn2", h3, sm["ffn2_norm"], wt["ffn2_w_gate"], wt["ffn2_w_up"], wt["ffn2_w_down"])

    dh4, dh4_bf, gs["final_norm"], loss = _loss_head(h4, sm["final_norm"], tgt)
    dh3, dh3_bf, gs["ffn2_norm"], gb["ffn2_w_gate"], gb["ffn2_w_up"], gb["ffn2_w_down"] = _ffn_bwd(
        "ffn2", h3, sm["ffn2_norm"], wt["ffn2_w_gate"], wt["ffn2_w_up"], wt["ffn2_w_down"], ffn2_saved, dh4, dh4_bf)

    tw = _pick(d, 256)
    do = _mm1("xattn_do", "nt", dh3_bf, wt["w_xo"], t, d, tmm, tnm, BF16)
    gb["w_xo"] = _mm1("xattn_dwxo", "tn", o, dh3_bf, d, d, tw, d, BF16)
    dq, dkv = _attn_bwd(q, kv, do)
    gb["w_q"] = _mm1("xattn_dwq", "tn", un, dq, d, d, tw, d, BF16)
    dun = _mm1("xattn_dun", "nt", dq, wt["w_q"], t, d, tmm, tnm, F32)
    dh2, dh2_bf, gs["xattn_norm"] = _rms_bwd("xattn_norm_bwd", h2, sm["xattn_norm"], dun, dh3)
    gb["w_kv"] = _mm1("xattn_dwkv", "tn", dkv, mn, 2 * d, d, _pick(2 * d, 512), d, BF16)
    dmn = _mm1("xattn_dmn", "nn", dkv, wt["w_kv"], n_mem, d, n_mem, tnm, F32)
    gs["mem_norm"] = _rms_bwd("mem_norm_bwd", mem, sm["mem_norm"], dmn)

    gb["w_mix_out"] = _mm1("mix_dwout", "tn", merged, dh2_bf, d, d, tw, d, BF16)

    def merge_bwd_epi(accs, gpv, gsv):
        dmerged, z_pool, val, gate = accs
        sp_, ss_, sg_ = jax.nn.sigmoid(gpv), jax.nn.sigmoid(gsv), jax.nn.sigmoid(gate)
        glu = val * sg_
        dz_pool = dmerged * sp_
        dg_pool = dmerged * z_pool * (sp_ * (1.0 - sp_))
        dz_ssm = dmerged * ss_
        dg_ssm = dmerged * glu * (ss_ * (1.0 - ss_))
        dval = dz_ssm * sg_
        dgate = dz_ssm * glu * (1.0 - sg_)
        return dz_pool, dg_pool, dg_ssm, dval, dgate

    dz_pool, dg_pool, dg_ssm, dval, dgate = _mm(
        "mix_merge_bwd", "nt", [dh2_bf, pm, ys], [wt["w_mix_out"], wt["w_pool_proj"], wt["w_glu_val"], wt["w_glu_gate"]],
        [[(0, 0)], [(1, 1)], [(2, 2)], [(2, 3)]], t, d, tmm, tnm, [(proj, gp_spec), (proj, gs_spec)], merge_bwd_epi,
        [(_out(t, d, BF16), None)] * 5)
    gb["w_pool_proj"] = _mm1("pool_dwproj", "tn", dz_pool, pm, d, d_pool, tw, d_pool, BF16)
    gb["w_glu_val"] = _mm1("glu_dwval", "tn", dval, ys, d, d_ssm, tw, d_ssm, BF16)
    gb["w_glu_gate"] = _mm1("glu_dwgate", "tn", dgate, ys, d, d_ssm, tw, d_ssm, BF16)

    def gelu_bwd_epi(accs, yv):
        _, vjp = jax.vjp(jax.nn.gelu, yv)
        return (vjp(accs[0])[0],)

    dy = _mm("glu_dy", "nn", [dval, dgate], [wt["w_glu_val"], wt["w_glu_gate"]], [[(0, 0), (1, 1)]], t, d_ssm, tmy, d_ssm,
             [(y, _tile(tmy, d_ssm))], gelu_bwd_epi, [(_out(t, d_ssm, F32), None)])[0]
    gs["ssm_d"] = _colsum_prod("ssm_dd", dy, proj, b_coff=off_s)
    dy_bf = _ew("ssm_dy_cast", lambda v: (v,), [dy], [BF16])[0]
    d_abr, d_abi, d_bbr, d_bbi, d_cre, d_cim, lams = [], [], [], [], [], [], []
    ts = _pick(n_state, 512)
    tc_ = _pick(n_state, 256)
    for dr in range(2):
        gx = _mm1(f"ssm_gx{dr}", "nt", dy_bf, jnp.concatenate([c_re[dr], c_im[dr]], axis=0), t, 2 * n_state, tms,
                  _pick(2 * n_state, 512), F32)
        lr, li = _scan(f"ssm_adj{dr}", gx, abr2[dr:dr + 1], abi2[dr:dr + 1], reverse=(dr == 0), conj=True)
        dar, dai = _ssm_da(f"ssm_da{dr}", lr, li, xs[dr][0], xs[dr][1], reverse=(dr == 1))
        d_abr.append(dar)
        d_abi.append(dai)
        lams += [lr, li]
        d_bbr.append(_diag_in(_mm1(f"ssm_dbre{dr}", "tn", s_bf, lr, d_ssm, n_state, d_ssm, ts, F32), sg, sp, sh))
        d_bbi.append(_diag_in(_mm1(f"ssm_dbim{dr}", "tn", s_bf, li, d_ssm, n_state, d_ssm, ts, F32), sg, sp, sh))
        d_cre.append(_diag_out(_mm1(f"ssm_dcre{dr}", "tn", xs[dr][0], dy_bf, n_state, d_ssm, tc_, d_ssm, F32), sg, sp, sh))
        d_cim.append(-_diag_out(_mm1(f"ssm_dcim{dr}", "tn", xs[dr][1], dy_bf, n_state, d_ssm, tc_, d_ssm, F32), sg, sp, sh))
    ds = _mm("ssm_ds", "nt", lams, [b_re[0], b_im[0], b_re[1], b_im[1]], [[(k, k) for k in range(4)]], t, d_ssm, tmy,
             d_ssm, [(dy, _tile(tmy, d_ssm)), (sm["ssm_d"], _rowvec(d_ssm))],
             lambda accs, dyv, dv: (dyv * dv + accs[0],), [(_out(t, d_ssm, BF16), None)])[0]
    cots = [jnp.concatenate(d_abr, axis=0).reshape(-1, 1), jnp.concatenate(d_abi, axis=0).reshape(-1, 1),
            jnp.concatenate(d_bbr, axis=0), jnp.concatenate(d_bbi, axis=0)]
    d_are, d_aim, d_ldt, d_bre, d_bim = _ssm_disc_bwd(cols, cots)
    gs["ssm_a_re"] = d_are.reshape(2, sg, sp)
    gs["ssm_a_im"] = d_aim.reshape(2, sg, sp)
    gs["ssm_log_dt"] = _rowsum("ssm_dlogdt", d_ldt.reshape(2 * sg, sp)).reshape(2, sg)
    gs["ssm_b_re"] = d_bre.reshape(2, sg, sp, sh)
    gs["ssm_b_im"] = d_bim.reshape(2, sg, sp, sh)
    gs["ssm_c_re"] = jnp.stack(d_cre, axis=0)
    gs["ssm_c_im"] = jnp.stack(d_cim, axis=0)

    dpm = _mm1("pool_dpm", "nn", dz_pool, wt["w_pool_proj"], t, d_pool, tmm, _pick(d_pool, 256), F32)
    dp, gs["pool_w"], gs["pool_scale"] = _pool_bwd(pooled, dpm, pool_w_bf, sm["pool_scale"])

    w_in = wt["w_in"]
    parts = [(dp, 0, d_pool), (ds, d_pool, d_ssm), (dg_pool, off_gp, d), (dg_ssm, off_gs, d)]
    w_in_parts = [w_in[o0:o0 + width] for _, o0, width in parts]
    du = _mm("in_proj_du", "nn", [p_[0] for p_ in parts], w_in_parts, [[(k, k) for k in range(4)]], t, d, tmm, tnm, [],
             lambda accs: (accs[0],), [(_out(t, d, F32), None)])[0]
    gb["w_in"] = jnp.concatenate(
        [_mm1(f"in_proj_dw{k}", "tn", p_[0], u, p_[2], d, _pick(p_[2], 256), d, BF16) for k, p_ in enumerate(parts)], axis=0)
    dh1, dh1_bf, gs["mix_norm"] = _rms_bwd("mix_norm_bwd", h1, sm["mix_norm"], du, dh2)

    dx, _, gs["ffn1_norm"], gb["ffn1_w_gate"], gb["ffn1_w_up"], gb["ffn1_w_down"] = _ffn_bwd(
        "ffn1", x, sm["ffn1_norm"], wt["ffn1_w_gate"], wt["ffn1_w_up"], wt["ffn1_w_down"], ffn1_saved, dh1, dh1_bf)
    return loss, dx, gb, gs


WEIGHTS = ["ffn1_norm", "ffn1_w_gate", "ffn1_w_up", "ffn1_w_down", "mix_norm", "w_in", "pool_w", "pool_scale",
           "w_pool_proj", "ssm_a_re", "ssm_a_im", "ssm_log_dt", "ssm_b_re", "ssm_b_im", "ssm_c_re", "ssm_c_im", "ssm_d",
           "w_glu_val", "w_glu_gate", "w_mix_out", "xattn_norm", "mem_norm", "w_q", "w_kv", "w_xo", "ffn2_norm",
           "ffn2_w_gate", "ffn2_w_up", "ffn2_w_down", "final_norm"]
COL_SHARDED = ["ffn1_w_gate", "ffn1_w_up", "w_in", "w_pool_proj", "w_glu_val", "w_glu_gate", "w_kv", "ffn2_w_gate",
               "ffn2_w_up"]
ROW_SHARDED = ["ffn1_w_down", "w_mix_out", "w_q", "w_xo", "ffn2_w_down"]
BIG = [n for n in WEIGHTS if n in COL_SHARDED or n in ROW_SHARDED]
SMALL = [n for n in WEIGHTS if n not in BIG]
PACK_ROWS = SUBLANES * LANES
GRAD_ROW_TILE = 256


def _to_rows(name, w, width):
    if name in COL_SHARDED:
        w = w.T
    return w.reshape(-1, width)


def _from_rows(name, rows, shard_shape):
    if name in COL_SHARDED:
        return rows.reshape(shard_shape[1], shard_shape[0]).T
    return rows.reshape(shard_shape)


def _pack_small(vals):
    flat = []
    for v in vals:
        f = v.reshape(-1)
        flat.append(jnp.pad(f, (0, (-f.shape[0]) % PACK_ROWS)))
    total = sum(f.shape[0] for f in flat)
    flat.append(jnp.zeros(((-total) % (GRAD_ROW_TILE * LANES),), F32))
    return jnp.concatenate(flat).reshape(-1, LANES)


def _unpack_small(packed, shapes):
    out, off = [], 0
    flat = packed.reshape(-1)
    for shp in shapes:
        size = math.prod(shp)
        out.append(flat[off:off + size].reshape(shp))
        off += size + (-size) % PACK_ROWS
    return out


def kernel(x, mem, ffn1_norm, ffn1_w_gate, ffn1_w_up, ffn1_w_down, mix_norm, w_in, pool_w, pool_scale, w_pool_proj, ssm_a_re, ssm_a_im, ssm_log_dt, ssm_b_re, ssm_b_im, ssm_c_re, ssm_c_im, ssm_d, w_glu_val, w_glu_gate, w_mix_out, xattn_norm, mem_norm, w_q, w_kv, w_xo, ffn2_norm, ffn2_w_gate, ffn2_w_up, ffn2_w_down, final_norm, loss_target, m_ffn1_norm, m_ffn1_w_gate, m_ffn1_w_up, m_ffn1_w_down, m_mix_norm, m_w_in, m_pool_w, m_pool_scale, m_w_pool_proj, m_ssm_a_re, m_ssm_a_im, m_ssm_log_dt, m_ssm_b_re, m_ssm_b_im, m_ssm_c_re, m_ssm_c_im, m_ssm_d, m_w_glu_val, m_w_glu_gate, m_w_mix_out, m_xattn_norm, m_mem_norm, m_w_q, m_w_kv, m_w_xo, m_ffn2_norm, m_ffn2_w_gate, m_ffn2_w_up, m_ffn2_w_down, m_final_norm, v_ffn1_norm, v_ffn1_w_gate, v_ffn1_w_up, v_ffn1_w_down, v_mix_norm, v_w_in, v_pool_w, v_pool_scale, v_w_pool_proj, v_ssm_a_re, v_ssm_a_im, v_ssm_log_dt, v_ssm_b_re, v_ssm_b_im, v_ssm_c_re, v_ssm_c_im, v_ssm_d, v_w_glu_val, v_w_glu_gate, v_w_mix_out, v_xattn_norm, v_mem_norm, v_w_q, v_w_kv, v_w_xo, v_ffn2_norm, v_ffn2_w_gate, v_ffn2_w_up, v_ffn2_w_down, v_final_norm):
    given = dict(locals())
    wts = {n: given[n] for n in WEIGHTS}
    moms = {n: (given["m_" + n], given["v_" + n]) for n in WEIGHTS}
    x2, mem2, tgt2 = x[0], mem[0], loss_target[0]
    d = x2.shape[1]

    shards = [_to_rows(n, wts[n][0], d).astype(BF16) for n in BIG]
    full = _allgather("weight_allgather", shards)
    wt = {}
    for n, f in zip(BIG, full):
        shard = wts[n][0].shape
        if n in COL_SHARDED:
            wt[n] = f.reshape(N_DEV * shard[1], shard[0])
        else:
            wt[n] = f.reshape(N_DEV * shard[0], shard[1])
    sm = {n: (wts[n].reshape(1, -1) if wts[n].ndim <= 2 else wts[n][0]) for n in SMALL}

    loss, dx, gb, gs = _local_step(x2, mem2, tgt2, wt, sm)

    blocks = [gb[n].reshape(N_DEV, -1, d) for n in BIG]
    rows = sum(b.shape[1] for b in blocks)
    pad_rows = (-rows) % GRAD_ROW_TILE
    if pad_rows:
        blocks.append(jnp.zeros((N_DEV, pad_rows, d), BF16))
    packed = jnp.concatenate(blocks, axis=1)
    r_tot = rows + pad_rows
    core = lax.axis_index("c").astype(jnp.int32).reshape(1)
    chip = (2 * lax.axis_index("x") + lax.axis_index("y")).astype(jnp.int32).reshape(1)
    pair = _pair_sum(packed, _exchange_cores(packed), core)
    g_rows = _chip_sum(pair, _exchange_chips(pair), chip)

    small_vals = [gs[n] for n in SMALL] + [loss[:, :1]]
    gathered = _allgather("small_allgather", [_pack_small(small_vals)])[0]
    small_sum = _sum_slots("small_sum", gathered.reshape(N_DEV, -1, LANES), F32)

    out_g, out_d, out_m, out_v = {}, {}, {}, {}
    off = 0
    for n, b in zip(BIG, blocks):
        shard = wts[n].shape
        g_full = _from_rows(n, g_rows[off:off + b.shape[1]], shard[1:]).reshape(shard)
        off += b.shape[1]
        two_d = (-1, shard[-1])
        dl, m2, v2 = _adamw("adamw_" + n, wts[n].reshape(two_d), g_full.reshape(two_d), moms[n][0].reshape(two_d),
                            moms[n][1].reshape(two_d))
        out_g[n], out_d[n], out_m[n], out_v[n] = g_full, dl.reshape(shard), m2.reshape(shard), v2.reshape(shard)
    zero = jnp.zeros((1, 1), F32)
    shapes = [wts[n].shape for n in SMALL] + [(1, 1)]
    w_pack = _pack_small([wts[n] for n in SMALL] + [zero])
    m_pack = _pack_small([moms[n][0] for n in SMALL] + [zero])
    v_pack = _pack_small([moms[n][1] for n in SMALL] + [zero])
    dl, m2, v2 = _adamw("adamw_small", w_pack, small_sum, m_pack, v_pack)
    for dst, src in ((out_g, small_sum), (out_d, dl), (out_m, m2), (out_v, v2)):
        vals = _unpack_small(src, shapes)
        for n, val in zip(SMALL, vals):
            dst[n] = val
        if dst is out_g:
            total_loss = vals[-1].reshape(())

    return (total_loss, dx[None], *[out_g[n] for n in WEIGHTS], *[out_d[n] for n in WEIGHTS],
            *[out_m[n] for n in WEIGHTS], *[out_v[n] for n in WEIGHTS])
```

```python
import functools
import math

import jax
import jax.numpy as jnp
from jax import lax
from jax.experimental import pallas as pl
from jax.experimental.pallas import tpu as pltpu

F32 = jnp.float32
BF16 = jnp.bfloat16
EPS = 1e-6
N_XHEADS = 4
POOL_WINDOWS = (2, 4, 8, 16)
ADAM_LR = 0.001
ADAM_B1 = 0.9
ADAM_B2 = 0.999
ADAM_EPS = 1e-08
ADAM_WD = 0.01
ADAM_STEP = 10
N_DEV = 8
VMEM_LIMIT_V7X = 48 * 1024 * 1024
LANES = 128
SUBLANES = 8
POOL_PAD = 16
MESH = pl.DeviceIdType.MESH
ANY = pl.BlockSpec(memory_space=pl.ANY)
HBM = pl.BlockSpec(memory_space=pltpu.HBM)
SEM = pl.BlockSpec(memory_space=pltpu.SEMAPHORE)
SIDE_EFFECT = pltpu.SideEffectType.DATAFLOW_SIDE_EFFECTING

_DIMS = {
    "nt": (((1,), (1,)), ((), ())),
    "nn": (((1,), (0,)), ((), ())),
    "tn": (((0,), (0,)), ((), ())),
}


def _pick(dim, pref, mult=LANES):
    if dim <= pref:
        return dim
    for t in range(pref - pref % mult, 0, -mult):
        if dim % t == 0:
            return t
    return dim


def _params(sem):
    return pltpu.CompilerParams(dimension_semantics=sem, vmem_limit_bytes=VMEM_LIMIT_V7X)


def _tile(tm, tn, coff=0):
    return pl.BlockSpec((tm, tn), lambda i, j: (i, j + coff))


def _rowvec(tn, coff=0):
    return pl.BlockSpec((1, tn), lambda i, j: (0, j + coff))


def _out(m, n, dtype):
    return jax.ShapeDtypeStruct((m, n), dtype)


def _mm(name, form, a_list, b_list, groups, m, n, tm, tn, extras, epilogue, outs):
    na, nb, ne = len(a_list), len(b_list), len(extras)

    def a_spec(a):
        if form == "tn":
            return pl.BlockSpec((a.shape[0], tm), lambda i, j: (0, i))
        return pl.BlockSpec((tm, a.shape[1]), lambda i, j: (i, 0))

    def b_spec(b):
        if form == "nt":
            return pl.BlockSpec((tn, b.shape[1]), lambda i, j: (j, 0))
        return pl.BlockSpec((b.shape[0], tn), lambda i, j: (0, j))

    def body(*refs):
        a_refs, b_refs = refs[:na], refs[na:na + nb]
        e_refs, o_refs = refs[na + nb:na + nb + ne], refs[na + nb + ne:]
        a_vals, b_vals, accs = {}, {}, []
        for group in groups:
            acc = None
            for ai, bi in group:
                if ai not in a_vals:
                    a_vals[ai] = a_refs[ai][...].astype(BF16)
                if bi not in b_vals:
                    b_vals[bi] = b_refs[bi][...].astype(BF16)
                d = lax.dot_general(a_vals[ai], b_vals[bi], _DIMS[form], preferred_element_type=F32)
                acc = d if acc is None else acc + d
            accs.append(acc)
        res = epilogue(accs, *[e[...] for e in e_refs])
        for o_ref, r in zip(o_refs, res):
            o_ref[...] = r.astype(o_ref.dtype)

    out_specs = [_tile(tm, tn) if s is None else s for _, s in outs]
    res = pl.pallas_call(
        body, name=name, grid=(m // tm, n // tn),
        in_specs=[a_spec(a) for a in a_list] + [b_spec(b) for b in b_list] + [s for _, s in extras],
        out_specs=out_specs, out_shape=[o for o, _ in outs],
        compiler_params=_params(("parallel", "parallel")),
    )(*a_list, *b_list, *[e for e, _ in extras])
    return res


def _mm1(name, form, a, b, m, n, tm, tn, dtype, scale=None):
    epi = (lambda accs: (accs[0],)) if scale is None else (lambda accs: (accs[0] * scale,))
    return _mm(name, form, [a], [b], [[(0, 0)]], m, n, tm, tn, [], epi, [(_out(m, n, dtype), None)])[0]


def _rms_fwd(name, h, g):
    t, d = h.shape
    tm = _pick(t, 512, SUBLANES)

    def body(h_ref, g_ref, n_ref):
        hv = h_ref[...]
        r = lax.rsqrt(jnp.mean(hv * hv, axis=-1, keepdims=True) + EPS)
        n_ref[...] = ((hv * r) * g_ref[...]).astype(BF16)

    return pl.pallas_call(
        body, name=name, grid=(t // tm,),
        in_specs=[pl.BlockSpec((tm, d), lambda i: (i, 0)), pl.BlockSpec((1, d), lambda i: (0, 0))],
        out_specs=pl.BlockSpec((tm, d), lambda i: (i, 0)), out_shape=_out(t, d, BF16),
        compiler_params=_params(("parallel",)),
    )(h, g)


def _rms_bwd(name, h, g, dn, dres=None):
    t, d = h.shape
    tm = _pick(t, 256, SUBLANES)
    need_dh = dres is not None

    def body(*refs):
        if need_dh:
            h_ref, g_ref, dn_ref, dres_ref, dh_ref, dhb_ref, dg_ref = refs
        else:
            h_ref, g_ref, dn_ref, dg_ref = refs
        hv = h_ref[...]
        r = lax.rsqrt(jnp.mean(hv * hv, axis=-1, keepdims=True) + EPS)
        nh = hv * r
        dnv = dn_ref[...].astype(F32)

        @pl.when(pl.program_id(0) == 0)
        def _():
            dg_ref[...] = jnp.zeros_like(dg_ref)

        dg_ref[...] += jnp.sum(dnv * nh, axis=0, keepdims=True)
        if need_dh:
            dng = dnv * g_ref[...]
            dh = dres_ref[...] + r * (dng - nh * jnp.mean(dng * nh, axis=-1, keepdims=True))
            dh_ref[...] = dh
            dhb_ref[...] = dh.astype(BF16)

    row = pl.BlockSpec((tm, d), lambda i: (i, 0))
    vec = pl.BlockSpec((1, d), lambda i: (0, 0))
    if need_dh:
        return pl.pallas_call(
            body, name=name, grid=(t // tm,), in_specs=[row, vec, row, row], out_specs=[row, row, vec],
            out_shape=[_out(t, d, F32), _out(t, d, BF16), _out(1, d, F32)], compiler_params=_params(("arbitrary",)),
        )(h, g, dn, dres)
    return pl.pallas_call(
        body, name=name, grid=(t // tm,), in_specs=[row, vec, row], out_specs=vec,
        out_shape=_out(1, d, F32), compiler_params=_params(("arbitrary",)),
    )(h, g, dn)


def _loss_head(h, g, tgt):
    t, d = h.shape
    tm = _pick(t, 256, SUBLANES)

    def body(h_ref, g_ref, t_ref, dh_ref, dhb_ref, dg_ref, loss_ref):
        hv = h_ref[...]
        r = lax.rsqrt(jnp.mean(hv * hv, axis=-1, keepdims=True) + EPS)
        nh = hv * r
        err = nh * g_ref[...] - t_ref[...]

        @pl.when(pl.program_id(0) == 0)
        def _():
            dg_ref[...] = jnp.zeros_like(dg_ref)
            loss_ref[...] = jnp.zeros_like(loss_ref)

        per_row = jnp.mean(err * err, axis=-1, keepdims=True)
        loss_ref[...] += 0.5 * jnp.sum(per_row, axis=0, keepdims=True)
        dy = err * (1.0 / d)
        dg_ref[...] += jnp.sum(dy * nh, axis=0, keepdims=True)
        dng = dy * g_ref[...]
        dh = r * (dng - nh * jnp.mean(dng * nh, axis=-1, keepdims=True))
        dh_ref[...] = dh
        dhb_ref[...] = dh.astype(BF16)

    row = pl.BlockSpec((tm, d), lambda i: (i, 0))
    vec = pl.BlockSpec((1, d), lambda i: (0, 0))
    return pl.pallas_call(
        body, name="loss_head", grid=(t // tm,), in_specs=[row, vec, row],
        out_specs=[row, row, vec, pl.BlockSpec((1, LANES), lambda i: (0, 0))],
        out_shape=[_out(t, d, F32), _out(t, d, BF16), _out(1, d, F32), _out(1, LANES, F32)],
        compiler_params=_params(("arbitrary",)),
    )(h, g, tgt)


def _ffn_fwd(tag, h, g, wg_t, wu_t, wd):
    t, d = h.shape
    f = wd.shape[0]
    n = _rms_fwd(tag + "_norm", h, g)
    tm, tn = _pick(t, 1024), _pick(f, 256)

    def up_epi(accs):
        a, b = accs
        return a, b, (a * jax.nn.sigmoid(a)) * b

    a, b, hid = _mm(tag + "_up", "nt", [n], [wg_t, wu_t], [[(0, 0)], [(0, 1)]], t, f, tm, tn, [], up_epi,
                    [(_out(t, f, BF16), None)] * 3)
    tm2, tn2 = _pick(t, 512), _pick(d, 256)
    h_out = _mm(tag + "_down", "nn", [hid], [wd], [[(0, 0)]], t, d, tm2, tn2, [(h, _tile(tm2, tn2))],
                lambda accs, hin: (hin + 0.5 * accs[0],), [(_out(t, d, F32), None)])[0]
    return h_out, (n, a, b, hid)


def _ffn_bwd(tag, h, g, wg_t, wu_t, wd, saved, dh, dh_bf, weights_done=None):
    n, a, b, hid = saved
    t, d = h.shape
    f = wd.shape[0]
    tm, tn = _pick(t, 1024), _pick(f, 256)

    def hid_epi(accs, av, bv):
        dhid = 0.5 * accs[0]
        av, bv = av.astype(F32), bv.astype(F32)
        sig = jax.nn.sigmoid(av)
        da = dhid * bv * (sig * (1.0 + av * (1.0 - sig)))
        db = dhid * (av * sig)
        return da, db

    da, db = _mm(tag + "_bwd_hid", "nt", [dh_bf], [wd], [[(0, 0)]], t, f, tm, tn,
                 [(a, _tile(tm, tn)), (b, _tile(tm, tn))], hid_epi, [(_out(t, f, BF16), None)] * 2)
    tw = _pick(f, 256)
    d_wd = _mm1(tag + "_dwd", "tn", hid, dh_bf, f, d, tw, d, BF16, scale=0.5)
    d_wg = _mm1(tag + "_dwg", "tn", da, n, f, d, tw, d, BF16)
    d_wu = _mm1(tag + "_dwu", "tn", db, n, f, d, tw, d, BF16)
    if weights_done is not None:
        weights_done(d_wg, d_wu, d_wd)
    tm2, tn2 = _pick(t, 512), _pick(d, 256)
    dn = _mm(tag + "_dn", "nn", [da, db], [wg_t, wu_t], [[(0, 0), (1, 1)]], t, d, tm2, tn2, [],
             lambda accs: (accs[0],), [(_out(t, d, F32), None)])[0]
    dh_in, dh_in_bf, dg = _rms_bwd(tag + "_norm_bwd", h, g, dn, dh)
    return dh_in, dh_in_bf, dg, d_wg, d_wu, d_wd


def _window_sum(win, offsets):
    n = win.shape[0]
    acc = None
    for j in offsets:
        term = win if j == 0 else pltpu.roll(win, (-j) % n, 0)
        acc = term if acc is None else acc + term
    return acc


def _pool_counts(r0, ch, c, left, right, t):
    pos = r0 + lax.broadcasted_iota(jnp.int32, (ch, c), 0)
    return (jnp.minimum(pos + right + 1, t) - jnp.maximum(pos - left, 0)).astype(F32)


def _pool_fwd(proj, pool_w_bf, pool_scale):
    t = proj.shape[0]
    ng, c, _ = pool_w_bf.shape
    ch = _pick(t, 256, SUBLANES)
    pad = POOL_PAD

    def body(p_ref, w_ref, s_ref, pooled_ref, pm_ref, buf):
        grp = pl.program_id(0)
        buf[pl.ds(0, pad), :] = jnp.zeros((pad, c), F32)
        buf[pl.ds(pad + t, pad), :] = jnp.zeros((pad, c), F32)

        def fill(ci, carry):
            r0 = pl.multiple_of(ci * ch, SUBLANES)
            buf[pl.ds(pl.multiple_of(r0 + pad, SUBLANES), ch), :] = p_ref[pl.ds(r0, ch), :]
            return carry

        lax.fori_loop(0, t // ch, fill, 0)
        for gi, w in enumerate(POOL_WINDOWS):
            left = w // 2
            right = w - 1 - left

            @pl.when(grp == gi)
            def _(left=left, right=right):
                def chunk(ci, carry):
                    r0 = pl.multiple_of(ci * ch, SUBLANES)
                    win = buf[pl.ds(r0, ch + 2 * pad), :]
                    s = _window_sum(win, range(-left, right + 1))[pad:pad + ch]
                    pooled = s / _pool_counts(r0, ch, c, left, right, t) - win[pad:pad + ch]
                    pooled_bf = pooled.astype(BF16)
                    mixed = jnp.dot(pooled_bf, w_ref[0], preferred_element_type=F32)
                    pooled_ref[pl.ds(r0, ch), :] = pooled_bf
                    pm_ref[pl.ds(r0, ch), :] = (mixed * s_ref[...]).astype(BF16)
                    return carry

                lax.fori_loop(0, t // ch, chunk, 0)

    col = pl.BlockSpec((t, c), lambda g: (0, g))
    return pl.pallas_call(
        body, name="pool_fwd", grid=(ng,),
        in_specs=[col, pl.BlockSpec((1, c, c), lambda g: (g, 0, 0)), pl.BlockSpec((1, c), lambda g: (0, g))],
        out_specs=[col, col], out_shape=[_out(t, ng * c, BF16), _out(t, ng * c, BF16)],
        scratch_shapes=[pltpu.VMEM((t + 2 * pad, c), F32)],
        compiler_params=_params(("parallel",)),
    )(proj, pool_w_bf, pool_scale)


def _pool_bwd(pooled, dpm, pool_w_bf, pool_scale):
    t = pooled.shape[0]
    ng, c, _ = pool_w_bf.shape
    ch = _pick(t, 256, SUBLANES)
    pad = POOL_PAD

    def body(pooled_ref, dpm_ref, w_ref, s_ref, dp_ref, dw_ref, ds_ref, buf, raw):
        grp = pl.program_id(0)
        buf[pl.ds(0, pad), :] = jnp.zeros((pad, c), F32)
        buf[pl.ds(pad + t, pad), :] = jnp.zeros((pad, c), F32)
        dw_ref[...] = jnp.zeros_like(dw_ref)
        ds_ref[...] = jnp.zeros_like(ds_ref)
        for gi, w in enumerate(POOL_WINDOWS):
            left = w // 2
            right = w - 1 - left

            @pl.when(grp == gi)
            def _(left=left, right=right):
                def first(ci, carry):
                    r0 = pl.multiple_of(ci * ch, SUBLANES)
                    pv = pooled_ref[pl.ds(r0, ch), :]
                    dpm_v = dpm_ref[pl.ds(r0, ch), :]
                    mixed = jnp.dot(pv, w_ref[0], preferred_element_type=F32)
                    ds_ref[...] += jnp.sum(dpm_v * mixed, axis=0, keepdims=True)
                    dmixed = (dpm_v * s_ref[...]).astype(BF16)
                    dw_ref[0] += lax.dot_general(pv, dmixed, _DIMS["tn"], preferred_element_type=F32)
                    dpooled = lax.dot_general(dmixed, w_ref[0], _DIMS["nt"], preferred_element_type=F32)
                    raw[pl.ds(r0, ch), :] = dpooled
                    buf[pl.ds(pl.multiple_of(r0 + pad, SUBLANES), ch), :] = (
                        dpooled / _pool_counts(r0, ch, c, left, right, t))
                    return carry

                lax.fori_loop(0, t // ch, first, 0)

                def second(ci, carry):
                    r0 = pl.multiple_of(ci * ch, SUBLANES)
                    win = buf[pl.ds(r0, ch + 2 * pad), :]
                    s = _window_sum(win, range(-right, left + 1))[pad:pad + ch]
                    dp_ref[pl.ds(r0, ch), :] = (s - raw[pl.ds(r0, ch), :]).astype(BF16)
                    return carry

                lax.fori_loop(0, t // ch, second, 0)

    col = pl.BlockSpec((t, c), lambda g: (0, g))
    return pl.pallas_call(
        body, name="pool_bwd", grid=(ng,),
        in_specs=[col, col, pl.BlockSpec((1, c, c), lambda g: (g, 0, 0)), pl.BlockSpec((1, c), lambda g: (0, g))],
        out_specs=[col, pl.BlockSpec((1, c, c), lambda g: (g, 0, 0)), pl.BlockSpec((1, c), lambda g: (0, g))],
        out_shape=[_out(t, ng * c, BF16), jax.ShapeDtypeStruct((ng, c, c), F32), _out(1, ng * c, F32)],
        scratch_shapes=[pltpu.VMEM((t + 2 * pad, c), F32), pltpu.VMEM((t, c), F32)],
        compiler_params=_params(("parallel",)),
    )(pooled, dpm, pool_w_bf, pool_scale)


def _discretise(a_re, a_im, log_dt, b_re, b_im):
    dt = jnp.exp(log_dt)
    mag = jnp.exp(dt * a_re)
    ang = dt * a_im
    abr = mag * jnp.cos(ang)
    abi = mag * jnp.sin(ang)
    den = a_re * a_re + a_im * a_im
    nr = abr - 1.0
    qr = (nr * a_re + abi * a_im) / den
    qi = (abi * a_re - nr * a_im) / den
    return abr, abi, qr * b_re - qi * b_im, qr * b_im + qi * b_re


def _ssm_disc(cols):
    n, hh = cols[3].shape

    def body(ar, ai, ld, br, bi, o1, o2, o3, o4):
        res = _discretise(ar[...], ai[...], ld[...], br[...], bi[...])
        for o, r in zip((o1, o2, o3, o4), res):
            o[...] = r

    return pl.pallas_call(
        body, name="ssm_disc",
        out_shape=[_out(n, 1, F32), _out(n, 1, F32), _out(n, hh, F32), _out(n, hh, F32)],
    )(*cols)


def _ssm_disc_bwd(cols, cots):
    n, hh = cols[3].shape

    def body(ar, ai, ld, br, bi, c1, c2, c3, c4, o1, o2, o3, o4, o5):
        _, vjp = jax.vjp(_discretise, ar[...], ai[...], ld[...], br[...], bi[...])
        res = vjp((c1[...], c2[...], c3[...], c4[...]))
        for o, r in zip((o1, o2, o3, o4, o5), res):
            o[...] = r

    return pl.pallas_call(
        body, name="ssm_disc_bwd",
        out_shape=[_out(n, 1, F32)] * 3 + [_out(n, hh, F32)] * 2,
    )(*cols, *cots)


def _rowsum(name, a):
    r, _ = a.shape

    def body(a_ref, o_ref):
        o_ref[...] = jnp.sum(a_ref[...], axis=-1, keepdims=True)

    return pl.pallas_call(body, name=name, out_shape=_out(r, 1, F32))(a)


def _cmul(pr, pi, qr, qi):
    return pr * qr - pi * qi, pr * qi + pi * qr


def _scan(name, u, ar, ai, reverse, conj):
    t, s2 = u.shape
    s = s2 // 2
    w = _pick(s, 512)
    tc = _pick(t, 512, SUBLANES)
    n_t, n_w = t // tc, s // w
    groups = tc // SUBLANES
    last = 0 if reverse else SUBLANES - 1

    def body(ar_ref, ai_ref, ur_ref, ui_ref, xr_ref, xi_ref, cr_ref, ci_ref):
        @pl.when(pl.program_id(1) == 0)
        def _():
            cr_ref[...] = jnp.zeros_like(cr_ref)
            ci_ref[...] = jnp.zeros_like(ci_ref)

        a1r = ar_ref[...]
        a1i = -ai_ref[...] if conj else ai_ref[...]
        a2r, a2i = _cmul(a1r, a1i, a1r, a1i)
        a4r, a4i = _cmul(a2r, a2i, a2r, a2i)
        row = lax.broadcasted_iota(jnp.int32, (SUBLANES, w), 0)
        pwr = jnp.zeros((SUBLANES, w), F32)
        pwi = jnp.zeros((SUBLANES, w), F32)
        cur_r, cur_i = a1r, a1i
        for k in range(SUBLANES):
            rk = SUBLANES - 1 - k if reverse else k
            pwr = jnp.where(row == rk, cur_r, pwr)
            pwi = jnp.where(row == rk, cur_i, pwi)
            cur_r, cur_i = _cmul(cur_r, cur_i, a1r, a1i)
        steps = ((1, a1r, a1i), (2, a2r, a2i), (4, a4r, a4i))

        def one(i, carry):
            g = groups - 1 - i if reverse else i
            r0 = pl.multiple_of(g * SUBLANES, SUBLANES)
            br = ur_ref[pl.ds(r0, SUBLANES), :]
            bi = ui_ref[pl.ds(r0, SUBLANES), :]
            for dist, pr, pi in steps:
                if reverse:
                    keep = row < SUBLANES - dist
                    shift = SUBLANES - dist
                else:
                    keep = row >= dist
                    shift = dist
                sr = jnp.where(keep, pltpu.roll(br, shift, 0), 0.0)
                si = jnp.where(keep, pltpu.roll(bi, shift, 0), 0.0)
                br, bi = br + pr * sr - pi * si, bi + pr * si + pi * sr
            cr = cr_ref[pl.ds(last, 1), :]
            ci = ci_ref[pl.ds(last, 1), :]
            xr = br + pwr * cr - pwi * ci
            xi = bi + pwr * ci + pwi * cr
            xr_ref[pl.ds(r0, SUBLANES), :] = xr
            xi_ref[pl.ds(r0, SUBLANES), :] = xi
            cr_ref[...] = xr
            ci_ref[...] = xi
            return carry

        lax.fori_loop(0, groups, one, 0)

    def tmap(k):
        return n_t - 1 - k if reverse else k

    re_blk = pl.BlockSpec((tc, w), lambda cb, k: (tmap(k), cb))
    im_blk = pl.BlockSpec((tc, w), lambda cb, k: (tmap(k), cb + n_w))
    a_blk = pl.BlockSpec((1, w), lambda cb, k: (0, cb))
    xr, xi = pl.pallas_call(
        body, name=name, grid=(n_w, n_t), in_specs=[a_blk, a_blk, re_blk, im_blk],
        out_specs=[pl.BlockSpec((tc, w), lambda cb, k: (tmap(k), cb))] * 2,
        out_shape=[_out(t, s, F32), _out(t, s, F32)],
        scratch_shapes=[pltpu.VMEM((SUBLANES, w), F32), pltpu.VMEM((SUBLANES, w), F32)],
        compiler_params=_params(("parallel", "arbitrary")),
    )(ar, ai, u, u)
    return xr, xi


def _ssm_da(name, lr, li, xr, xi, reverse):
    t, s = xr.shape
    w = _pick(s, 512)
    tc = _pick(t, 256, SUBLANES)
    n_t, n_w = t // tc, s // w

    def body(lr_ref, li_ref, xr_ref, xi_ref, dar_ref, dai_ref, pr_ref, pi_ref):
        @pl.when(pl.program_id(1) == 0)
        def _():
            pr_ref[...] = jnp.zeros_like(pr_ref)
            pi_ref[...] = jnp.zeros_like(pi_ref)
            dar_ref[...] = jnp.zeros_like(dar_ref)
            dai_ref[...] = jnp.zeros_like(dai_ref)

        row = lax.broadcasted_iota(jnp.int32, (tc, w), 0)
        xrv, xiv = xr_ref[...], xi_ref[...]
        if reverse:
            keep, shift, edge = row < tc - 1, tc - 1, 0
        else:
            keep, shift, edge = row >= 1, 1, tc - 1
        xsr = jnp.where(keep, pltpu.roll(xrv, shift, 0), pr_ref[pl.ds(0, 1), :])
        xsi = jnp.where(keep, pltpu.roll(xiv, shift, 0), pi_ref[pl.ds(0, 1), :])
        lrv, liv = lr_ref[...], li_ref[...]
        dar_ref[...] += jnp.sum(lrv * xsr + liv * xsi, axis=0, keepdims=True)
        dai_ref[...] += jnp.sum(liv * xsr - lrv * xsi, axis=0, keepdims=True)
        pr_ref[pl.ds(0, 1), :] = xr_ref[pl.ds(edge, 1), :]
        pi_ref[pl.ds(0, 1), :] = xi_ref[pl.ds(edge, 1), :]

    def tmap(k):
        return n_t - 1 - k if reverse else k

    blk = pl.BlockSpec((tc, w), lambda cb, k: (tmap(k), cb))
    vec = pl.BlockSpec((1, w), lambda cb, k: (0, cb))
    return pl.pallas_call(
        body, name=name, grid=(n_w, n_t), in_specs=[blk] * 4, out_specs=[vec, vec],
        out_shape=[_out(1, s, F32), _out(1, s, F32)],
        scratch_shapes=[pltpu.VMEM((SUBLANES, w), F32), pltpu.VMEM((SUBLANES, w), F32)],
        compiler_params=_params(("parallel", "arbitrary")),
    )(lr, li, xr, xi)


def _colsum_prod(name, a, b, b_coff=0):
    t, n = a.shape
    tm = _pick(t, 512, SUBLANES)

    def body(a_ref, b_ref, o_ref):
        @pl.when(pl.program_id(0) == 0)
        def _():
            o_ref[...] = jnp.zeros_like(o_ref)

        o_ref[...] += jnp.sum(a_ref[...].astype(F32) * b_ref[...].astype(F32), axis=0, keepdims=True)

    return pl.pallas_call(
        body, name=name, grid=(t // tm,),
        in_specs=[pl.BlockSpec((tm, n), lambda i: (i, 0)), pl.BlockSpec((tm, n), lambda i: (i, b_coff))],
        out_specs=pl.BlockSpec((1, n), lambda i: (0, 0)), out_shape=_out(1, n, F32),
        compiler_params=_params(("arbitrary",)),
    )(a, b)


def _bd_in(bb, g, p, hh):
    blk = bb.reshape(g, p, hh).transpose(0, 2, 1)
    eye = jnp.eye(g, dtype=bool)[:, None, :, None]
    return jnp.where(eye, blk[:, :, None, :], 0.0).reshape(g * hh, g * p)


def _bd_out(cc, g, p, hh):
    blk = cc.transpose(0, 2, 1)
    eye = jnp.eye(g, dtype=bool)[:, None, :, None]
    return jnp.where(eye, blk[:, :, None, :], 0.0).reshape(g * p, g * hh)


def _diag_in(dmat, g, p, hh):
    eye = jnp.eye(g, dtype=bool)[:, None, :, None]
    diag = jnp.sum(jnp.where(eye, dmat.reshape(g, hh, g, p), 0.0), axis=2)
    return diag.transpose(0, 2, 1).reshape(g * p, hh)


def _diag_out(dmat, g, p, hh):
    eye = jnp.eye(g, dtype=bool)[:, None, :, None]
    diag = jnp.sum(jnp.where(eye, dmat.reshape(g, p, g, hh), 0.0), axis=2)
    return diag.transpose(0, 2, 1)


def _softmax(qh, kh, scale):
    s = lax.dot_general(qh, kh, _DIMS["nt"], preferred_element_type=F32) * scale
    e = jnp.exp(s - jnp.max(s, axis=-1, keepdims=True))
    return e / jnp.sum(e, axis=-1, keepdims=True)


def _attn_fwd(q, kv):
    t, d = q.shape
    mm_ = kv.shape[0]
    hd = d // N_XHEADS
    scale = 1.0 / math.sqrt(hd)
    tm = _pick(t, 512, SUBLANES)

    def body(q_ref, kv_ref, o_ref):
        for h in range(N_XHEADS):
            sl = pl.ds(h * hd, hd)
            p = _softmax(q_ref[:, sl], kv_ref[:, sl], scale)
            o_ref[:, sl] = jnp.dot(p.astype(BF16), kv_ref[:, pl.ds(d + h * hd, hd)],
                                   preferred_element_type=F32).astype(BF16)

    return pl.pallas_call(
        body, name="attn_fwd", grid=(t // tm,),
        in_specs=[pl.BlockSpec((tm, d), lambda i: (i, 0)), pl.BlockSpec((mm_, 2 * d), lambda i: (0, 0))],
        out_specs=pl.BlockSpec((tm, d), lambda i: (i, 0)), out_shape=_out(t, d, BF16),
        compiler_params=_params(("parallel",)),
    )(q, kv)


def _attn_bwd(q, kv, do):
    t, d = q.shape
    mm_ = kv.shape[0]
    hd = d // N_XHEADS
    scale = 1.0 / math.sqrt(hd)
    tm = _pick(t, 512, SUBLANES)

    def body(q_ref, kv_ref, do_ref, dq_ref, dkv_ref):
        @pl.when(pl.program_id(0) == 0)
        def _():
            dkv_ref[...] = jnp.zeros_like(dkv_ref)

        for h in range(N_XHEADS):
            sl = pl.ds(h * hd, hd)
            vsl = pl.ds(d + h * hd, hd)
            qh, kh, doh = q_ref[:, sl], kv_ref[:, sl], do_ref[:, sl]
            p = _softmax(qh, kh, scale)
            dp = lax.dot_general(doh, kv_ref[:, vsl], _DIMS["nt"], preferred_element_type=F32)
            dkv_ref[:, vsl] += lax.dot_general(p.astype(BF16), doh, _DIMS["tn"], preferred_element_type=F32)
            ds = (p * (dp - jnp.sum(dp * p, axis=-1, keepdims=True)) * scale).astype(BF16)
            dq_ref[:, sl] = jnp.dot(ds, kh, preferred_element_type=F32).astype(BF16)
            dkv_ref[:, sl] += lax.dot_general(ds, qh, _DIMS["tn"], preferred_element_type=F32)

    row = pl.BlockSpec((tm, d), lambda i: (i, 0))
    full = pl.BlockSpec((mm_, 2 * d), lambda i: (0, 0))
    return pl.pallas_call(
        body, name="attn_bwd", grid=(t // tm,), in_specs=[row, full, row], out_specs=[row, full],
        out_shape=[_out(t, d, BF16), _out(mm_, 2 * d, F32)], compiler_params=_params(("arbitrary",)),
    )(q, kv, do)


def _ew(name, fn, ins, outs, rows_pref=256):
    r, c = ins[0].shape
    tr = _pick(r, rows_pref, SUBLANES)
    ni = len(ins)

    def body(*refs):
        res = fn(*[x[...] for x in refs[:ni]])
        for o_ref, v in zip(refs[ni:], res):
            o_ref[...] = v.astype(o_ref.dtype)

    blk = pl.BlockSpec((tr, c), lambda i: (i, 0))
    return pl.pallas_call(
        body, name=name, grid=(r // tr,), in_specs=[blk] * ni, out_specs=[blk] * len(outs),
        out_shape=[_out(r, c, dt) for dt in outs], compiler_params=_params(("parallel",)),
    )(*ins)


def _sum_slots(name, a, dtype):
    s, r, c = a.shape
    tr = _pick(r, 256, SUBLANES)

    def body(a_ref, o_ref):
        acc = a_ref[0].astype(F32)
        for k in range(1, s):
            acc = acc + a_ref[k].astype(F32)
        o_ref[...] = acc.astype(o_ref.dtype)

    return pl.pallas_call(
        body, name=name, grid=(r // tr,), in_specs=[pl.BlockSpec((s, tr, c), lambda i: (0, i, 0))],
        out_specs=pl.BlockSpec((tr, c), lambda i: (i, 0)), out_shape=_out(r, c, dtype),
        compiler_params=_params(("parallel",)),
    )(a)


def _adamw(name, w, g, m, v):
    bc1 = 1.0 - ADAM_B1 ** ADAM_STEP
    bc2 = 1.0 - ADAM_B2 ** ADAM_STEP

    def fn(wv, gv, mv, vv):
        m2 = ADAM_B1 * mv + (1.0 - ADAM_B1) * gv
        v2 = ADAM_B2 * vv + (1.0 - ADAM_B2) * (gv * gv)
        delta = -ADAM_LR * ((m2 / bc1) / (jnp.sqrt(v2 / bc2) + ADAM_EPS) + ADAM_WD * wv)
        return delta, m2, v2

    return _ew(name, fn, [w, g, m, v], [F32, F32, F32])


def _allgather(name, arrs):
    n = len(arrs)

    def body(*refs):
        ins, outs = refs[:n], refs[n:2 * n]
        send_sems, recv_sems, local_sems = refs[2 * n:]
        x, y, c = lax.axis_index("x"), lax.axis_index("y"), lax.axis_index("c")
        me, sibling = (x, y, c), (x, y, 1 - c)
        chips = [(1 - x, y), (x, 1 - y), (1 - x, 1 - y)]

        def rows(a, px, py, pc):
            r = ins[a].shape[0]
            return outs[a].at[pl.ds((4 * px + 2 * py + pc) * r, r), :]

        def copy(a, k, block, to, src=None):
            return pltpu.make_async_remote_copy(
                src_ref=rows(a, *block) if src is None else src, dst_ref=rows(a, *block),
                send_sem=send_sems.at[a, k], recv_sem=recv_sems.at[a, k], device_id=to, device_id_type=MESH)

        mine = [pltpu.make_async_copy(ins[a], rows(a, *me), local_sems.at[a]) for a in range(n)]
        for cp in mine:
            cp.start()
        first = []
        for a in range(n):
            first.append(copy(a, 0, me, sibling, src=ins[a]))
            first += [copy(a, 1 + j, me, (*chip, c), src=ins[a]) for j, chip in enumerate(chips)]
        for cp in first:
            cp.start()
        passed = []
        for j, chip in enumerate(chips):
            for a in range(n):
                copy(a, 1 + j, (*chip, c), me).wait_recv()
                cp = copy(a, 4 + j, (*chip, c), sibling)
                cp.start()
                passed.append(cp)
        for a in range(n):
            copy(a, 0, sibling, me).wait_recv()
            for j, chip in enumerate(chips):
                copy(a, 4 + j, (*chip, 1 - c), me).wait_recv()
        for cp in first + passed:
            cp.wait_send()
        for cp in mine:
            cp.wait()

    return pl.pallas_call(
        body, name=name, in_specs=[ANY] * n, out_specs=[ANY] * n,
        out_shape=[_out(N_DEV * a.shape[0], a.shape[1], a.dtype) for a in arrs],
        scratch_shapes=[pltpu.SemaphoreType.DMA((n, 7)), pltpu.SemaphoreType.DMA((n, 7)), pltpu.SemaphoreType.DMA((n,))],
    )(*arrs)


def _exchange_cores(name, g):
    _, r, c = g.shape
    nck = r // GRAD_ROW_TILE

    def body(g_ref, recv_ref, send_sems, recv_sems):
        x, y, cc = lax.axis_index("x"), lax.axis_index("y"), lax.axis_index("c")
        copies = []
        for q in range(4):
            for k in range(nck):
                rows = pl.ds(k * GRAD_ROW_TILE, GRAD_ROW_TILE)
                copies.append(pltpu.make_async_remote_copy(
                    src_ref=g_ref.at[2 * q + (1 - cc), rows], dst_ref=recv_ref.at[q, rows],
                    send_sem=send_sems.at[q, k], recv_sem=recv_sems.at[q, k], device_id=(x, y, 1 - cc),
                    device_id_type=MESH))
        for cp in copies:
            cp.start()
        for cp in copies:
            cp.wait()

    return pl.pallas_call(
        body, name=name, in_specs=[ANY], out_specs=ANY,
        out_shape=jax.ShapeDtypeStruct((4, r, c), g.dtype),
        scratch_shapes=[pltpu.SemaphoreType.DMA((4, nck)), pltpu.SemaphoreType.DMA((4, nck))],
    )(g)


def _pair_sum(name, g, recv, core):
    _, r, c = g.shape
    tr = GRAD_ROW_TILE

    def body(core_ref, g_ref, r_ref, o_ref):
        o_ref[...] = (g_ref[...].astype(F32) + r_ref[...].astype(F32)).astype(o_ref.dtype)

    blk = pl.BlockSpec((None, tr, c), lambda q, i, core_ref: (q, i, 0))
    return pl.pallas_call(
        body, name=name,
        grid_spec=pltpu.PrefetchScalarGridSpec(
            num_scalar_prefetch=1, grid=(4, r // tr),
            in_specs=[pl.BlockSpec((None, tr, c), lambda q, i, core_ref: (2 * q + core_ref[0], i, 0)), blk],
            out_specs=blk),
        out_shape=jax.ShapeDtypeStruct((4, r, c), g.dtype), compiler_params=_params(("parallel", "parallel")),
    )(core, g, recv)


def _peer(k, x, y, c):
    return (1 - x if k & 4 else x, 1 - y if k & 2 else y, 1 - c if k & 1 else c)


def _split_start(name, srcs, land_shapes, n_remote, n_local, build):
    ns, nl = len(srcs), len(land_shapes)
    n_sem = 3 if n_local else 2

    def body(*refs):
        src_refs, land_refs = refs[:ns], refs[ns:ns + nl]
        sems = refs[ns + nl:ns + nl + n_sem]
        token = refs[-1]
        remote, local = build(src_refs, land_refs, *sems)
        for cp in local + remote:
            cp.start()
        token[...] = jnp.zeros_like(token)

    sem_shapes = [pltpu.SemaphoreType.DMA((n_remote,)), pltpu.SemaphoreType.DMA((n_remote,))]
    if n_local:
        sem_shapes.append(pltpu.SemaphoreType.DMA((n_local,)))
    bufs = [pltpu.with_memory_space_constraint(a, pltpu.HBM) for a in srcs]
    bufs += [pltpu.with_memory_space_constraint(lax.empty(s.shape, s.dtype), pltpu.HBM) for s in land_shapes]
    outs = pl.pallas_call(
        body, name=name,
        out_shape=sem_shapes + [pltpu.HBM(b.shape, b.dtype) for b in bufs] + [jax.ShapeDtypeStruct((SUBLANES, LANES), F32)],
        in_specs=[HBM] * (ns + nl),
        out_specs=[SEM] * n_sem + [HBM] * (ns + nl) + [pl.BlockSpec(memory_space=pltpu.VMEM)],
        input_output_aliases={i: n_sem + i for i in range(ns + nl)},
        compiler_params=pltpu.CompilerParams(has_side_effects=SIDE_EFFECT),
    )(*bufs)
    return dict(sems=list(outs[:n_sem]), bufs=list(outs[n_sem:n_sem + ns + nl]), token=outs[-1], build=build, ns=ns)


def _split_wait(name, started, after):
    ns, n_buf, n_sem = started["ns"], len(started["bufs"]), len(started["sems"])

    def body(*refs):
        src_refs, land_refs = refs[:ns], refs[ns:n_buf]
        sems = refs[n_buf:n_buf + n_sem]
        remote, local = started["build"](src_refs, land_refs, *sems)
        for cp in local:
            cp.wait()
        for cp in remote:
            cp.wait_send()
            cp.wait_recv()

    outs = pl.pallas_call(
        body, name=name, out_shape=[pltpu.HBM(b.shape, b.dtype) for b in started["bufs"]],
        in_specs=[HBM] * n_buf + [SEM] * n_sem + [ANY], out_specs=[HBM] * n_buf,
        input_output_aliases={i: i for i in range(n_buf)},
        compiler_params=pltpu.CompilerParams(has_side_effects=SIDE_EFFECT),
    )(*started["bufs"], *started["sems"], after)
    return list(outs[ns:])


def _gather_start(name, shards):
    m = len(shards)

    def build(src_refs, land_refs, send_sems, recv_sems, local_sems):
        x, y, c = lax.axis_index("x"), lax.axis_index("y"), lax.axis_index("c")
        remote, local = [], []
        for j in range(m):
            r = src_refs[j].shape[0]
            dst = land_refs[j].at[pl.ds((4 * x + 2 * y + c) * r, r), :]
            local.append(pltpu.make_async_copy(src_refs[j], dst, local_sems.at[j]))
            for k in range(1, N_DEV):
                remote.append(pltpu.make_async_remote_copy(
                    src_ref=src_refs[j], dst_ref=dst, send_sem=send_sems.at[7 * j + k - 1],
                    recv_sem=recv_sems.at[7 * j + k - 1], device_id=_peer(k, x, y, c), device_id_type=MESH))
        return remote, local

    lands = [jax.ShapeDtypeStruct((N_DEV * a.shape[0], a.shape[1]), a.dtype) for a in shards]
    return _split_start(name, shards, lands, 7 * m, m, build)


def _slots_start(name, a):
    def build(src_refs, land_refs, send_sems, recv_sems, local_sems):
        x, y, c = lax.axis_index("x"), lax.axis_index("y"), lax.axis_index("c")
        dst = land_refs[0].at[4 * x + 2 * y + c]
        local = [pltpu.make_async_copy(src_refs[0], dst, local_sems.at[0])]
        remote = [pltpu.make_async_remote_copy(
            src_ref=src_refs[0], dst_ref=dst, send_sem=send_sems.at[k - 1], recv_sem=recv_sems.at[k - 1],
            device_id=_peer(k, x, y, c), device_id_type=MESH) for k in range(1, N_DEV)]
        return remote, local

    return _split_start(name, [a], [jax.ShapeDtypeStruct((N_DEV,) + a.shape, a.dtype)], 7, 1, build)


def _chips_start(name, p):
    _, r, c = p.shape
    nck = r // GRAD_ROW_TILE

    def build(src_refs, land_refs, send_sems, recv_sems):
        x, y, cc = lax.axis_index("x"), lax.axis_index("y"), lax.axis_index("c")
        remote = []
        for k in range(1, 4):
            px = 1 - x if k >> 1 else x
            py = 1 - y if k & 1 else y
            for j in range(nck):
                rows = pl.ds(j * GRAD_ROW_TILE, GRAD_ROW_TILE)
                remote.append(pltpu.make_async_remote_copy(
                    src_ref=src_refs[0].at[2 * px + py, rows], dst_ref=land_refs[0].at[k - 1, rows],
                    send_sem=send_sems.at[(k - 1) * nck + j], recv_sem=recv_sems.at[(k - 1) * nck + j],
                    device_id=(px, py, cc), device_id_type=MESH))
        return remote, []

    return _split_start(name, [p], [jax.ShapeDtypeStruct((3, r, c), p.dtype)], 3 * nck, 0, build)


def _chip_sum(name, p, recv, chip):
    _, r, c = p.shape
    tr = GRAD_ROW_TILE

    def body(chip_ref, p_ref, r_ref, o_ref):
        acc = p_ref[...].astype(F32)
        for k in range(3):
            acc = acc + r_ref[k].astype(F32)
        o_ref[...] = acc

    return pl.pallas_call(
        body, name=name,
        grid_spec=pltpu.PrefetchScalarGridSpec(
            num_scalar_prefetch=1, grid=(r // tr,),
            in_specs=[pl.BlockSpec((None, tr, c), lambda i, chip_ref: (chip_ref[0], i, 0)),
                      pl.BlockSpec((3, tr, c), lambda i, chip_ref: (0, i, 0))],
            out_specs=pl.BlockSpec((tr, c), lambda i, chip_ref: (i, 0))),
        out_shape=_out(r, c, F32), compiler_params=_params(("parallel",)),
    )(chip, p, recv)


def _local_step(x, mem, tgt, wt, sm, ev=None):
    t, d = x.shape
    n_mem = mem.shape[0]
    d_pool = sm["pool_scale"].shape[1]
    ng, pc = sm["pool_w"].shape[0], sm["pool_w"].shape[1]
    d_ssm = sm["ssm_d"].shape[1]
    _, sg, sp, sh = sm["ssm_b_re"].shape
    n_state = sg * sp
    gb, gs = {}, {}

    def emit(name, **kw):
        if ev is not None:
            ev(name, **kw)

    h1, ffn1_saved = _ffn_fwd("ffn1", x, sm["ffn1_norm"], wt["ffn1_w_gate"], wt["ffn1_w_up"], wt["ffn1_w_down"])
    emit("ffn1_fwd_done", marker=h1)
    u = _rms_fwd("mix_norm", h1, sm["mix_norm"])
    d_in = wt["w_in"].shape[0]
    tm, tn = _pick(t, 1024), _pick(d_in, 256)
    proj = _mm1("in_proj", "nt", u, wt["w_in"], t, d_in, tm, tn, F32)
    off_s = d_pool // d_ssm
    off_gp = (d_pool + d_ssm)
    off_gs = off_gp + d

    pool_w_bf = sm["pool_w"].astype(BF16)
    pooled, pm = _pool_fwd(proj, pool_w_bf, sm["pool_scale"])

    cols = [sm["ssm_a_re"].reshape(-1, 1), sm["ssm_a_im"].reshape(-1, 1),
            jnp.broadcast_to(sm["ssm_log_dt"][:, :, None], (2, sg, sp)).reshape(-1, 1),
            sm["ssm_b_re"].reshape(-1, sh), sm["ssm_b_im"].reshape(-1, sh)]
    abr, abi, bbr, bbi = _ssm_disc(cols)
    abr2, abi2 = abr.reshape(2, n_state), abi.reshape(2, n_state)
    bbr4, bbi4 = bbr.reshape(2, sg * sp, sh), bbi.reshape(2, sg * sp, sh)
    b_re = [_bd_in(bbr4[dr], sg, sp, sh).astype(BF16) for dr in range(2)]
    b_im = [_bd_in(bbi4[dr], sg, sp, sh).astype(BF16) for dr in range(2)]
    c_re = [_bd_out(sm["ssm_c_re"][dr], sg, sp, sh).astype(BF16) for dr in range(2)]
    c_im = [_bd_out(-sm["ssm_c_im"][dr], sg, sp, sh).astype(BF16) for dr in range(2)]
    tms = _pick(t, 512)
    s_bf = _ew("ssm_cast", lambda v: (v,), [proj[:, d_pool:d_pool + d_ssm]], [BF16])[0]
    xs = []
    for dr in range(2):
        u_d = _mm1(f"ssm_in{dr}", "nn", s_bf, jnp.concatenate([b_re[dr], b_im[dr]], axis=1), t, 2 * n_state, tms,
                   _pick(2 * n_state, 512), F32)
        xs.append(_scan(f"ssm_scan{dr}", u_d, abr2[dr:dr + 1], abi2[dr:dr + 1], reverse=(dr == 1), conj=False))
    tmy = _pick(t, 256)
    x_list = [xs[0][0], xs[0][1], xs[1][0], xs[1][1]]
    y = _mm("ssm_out", "nn", x_list, [c_re[0], c_im[0], c_re[1], c_im[1]], [[(k, k) for k in range(4)]], t, d_ssm, tmy,
            d_ssm, [(proj, _tile(tmy, d_ssm, off_s)), (sm["ssm_d"], _rowvec(d_ssm))],
            lambda accs, sv, dv: (sv * dv + accs[0],), [(_out(t, d_ssm, F32), None)])[0]
    ys = _ew("ssm_gelu", lambda v: (jax.nn.gelu(v),), [y], [BF16])[0]
    emit("mix_in_done", marker=ys)

    tmm, tnm = _pick(t, 512), _pick(d, 256)
    gp_spec = _tile(tmm, tnm, off_gp // tnm)
    gs_spec = _tile(tmm, tnm, off_gs // tnm)

    def merge_epi(accs, gpv, gsv):
        z_pool, val, gate = accs
        return (jax.nn.sigmoid(gpv) * z_pool + jax.nn.sigmoid(gsv) * (val * jax.nn.sigmoid(gate)),)

    merged = _mm("mix_merge", "nt", [pm, ys], [wt["w_pool_proj"], wt["w_glu_val"], wt["w_glu_gate"]],
                 [[(0, 0)], [(1, 1)], [(1, 2)]], t, d, tmm, tnm, [(proj, gp_spec), (proj, gs_spec)], merge_epi,
                 [(_out(t, d, BF16), None)])[0]
    res_epi = lambda accs, hin: (hin + accs[0],)
    h2 = _mm("mix_out", "nn", [merged], [wt["w_mix_out"]], [[(0, 0)]], t, d, tmm, tnm, [(h1, _tile(tmm, tnm))],
             res_epi, [(_out(t, d, F32), None)])[0]

    un = _rms_fwd("xattn_norm", h2, sm["xattn_norm"])
    mn = _rms_fwd("mem_norm", mem, sm["mem_norm"])
    q = _mm1("xattn_q", "nn", un, wt["w_q"], t, d, tmm, tnm, BF16)
    kv = _mm1("xattn_kv", "nt", mn, wt["w_kv"], n_mem, 2 * d, n_mem, _pick(2 * d, 512), BF16)
    o = _attn_fwd(q, kv)
    h3 = _mm("xattn_out", "nn", [o], [wt["w_xo"]], [[(0, 0)]], t, d, tmm, tnm, [(h2, _tile(tmm, tnm))],
             res_epi, [(_out(t, d, F32), None)])[0]

    h4, ffn2_saved = _ffn_fwd("ffn2", h3, sm["ffn2_norm"], wt["ffn2_w_gate"], wt["ffn2_w_up"], wt["ffn2_w_down"])

    dh4, dh4_bf, gs["final_norm"], loss = _loss_head(h4, sm["final_norm"], tgt)
    dh3, dh3_bf, gs["ffn2_norm"], gb["ffn2_w_gate"], gb["ffn2_w_up"], gb["ffn2_w_down"] = _ffn_bwd(
        "ffn2", h3, sm["ffn2_norm"], wt["ffn2_w_gate"], wt["ffn2_w_up"], wt["ffn2_w_down"], ffn2_saved, dh4, dh4_bf)

    tw = _pick(d, 256)
    do = _mm1("xattn_do", "nt", dh3_bf, wt["w_xo"], t, d, tmm, tnm, BF16)
    gb["w_xo"] = _mm1("xattn_dwxo", "tn", o, dh3_bf, d, d, tw, d, BF16)
    dq, dkv = _attn_bwd(q, kv, do)
    gb["w_q"] = _mm1("xattn_dwq", "tn", un, dq, d, d, tw, d, BF16)
    dun = _mm1("xattn_dun", "nt", dq, wt["w_q"], t, d, tmm, tnm, F32)
    dh2, dh2_bf, gs["xattn_norm"] = _rms_bwd("xattn_norm_bwd", h2, sm["xattn_norm"], dun, dh3)
    gb["w_kv"] = _mm1("xattn_dwkv", "tn", dkv, mn, 2 * d, d, _pick(2 * d, 512), d, BF16)
    dmn = _mm1("xattn_dmn", "nn", dkv, wt["w_kv"], n_mem, d, n_mem, tnm, F32)
    gs["mem_norm"] = _rms_bwd("mem_norm_bwd", mem, sm["mem_norm"], dmn)

    gb["w_mix_out"] = _mm1("mix_dwout", "tn", merged, dh2_bf, d, d, tw, d, BF16)

    def merge_bwd_epi(accs, gpv, gsv):
        dmerged, z_pool, val, gate = accs
        sp_, ss_, sg_ = jax.nn.sigmoid(gpv), jax.nn.sigmoid(gsv), jax.nn.sigmoid(gate)
        glu = val * sg_
        dz_pool = dmerged * sp_
        dg_pool = dmerged * z_pool * (sp_ * (1.0 - sp_))
        dz_ssm = dmerged * ss_
        dg_ssm = dmerged * glu * (ss_ * (1.0 - ss_))
        dval = dz_ssm * sg_
        dgate = dz_ssm * glu * (1.0 - sg_)
        return dz_pool, dg_pool, dg_ssm, dval, dgate

    dz_pool, dg_pool, dg_ssm, dval, dgate = _mm(
        "mix_merge_bwd", "nt", [dh2_bf, pm, ys], [wt["w_mix_out"], wt["w_pool_proj"], wt["w_glu_val"], wt["w_glu_gate"]],
        [[(0, 0)], [(1, 1)], [(2, 2)], [(2, 3)]], t, d, tmm, tnm, [(proj, gp_spec), (proj, gs_spec)], merge_bwd_epi,
        [(_out(t, d, BF16), None)] * 5)
    gb["w_pool_proj"] = _mm1("pool_dwproj", "tn", dz_pool, pm, d, d_pool, tw, d_pool, BF16)
    gb["w_glu_val"] = _mm1("glu_dwval", "tn", dval, ys, d, d_ssm, tw, d_ssm, BF16)
    gb["w_glu_gate"] = _mm1("glu_dwgate", "tn", dgate, ys, d, d_ssm, tw, d_ssm, BF16)

    def gelu_bwd_epi(accs, yv):
        _, vjp = jax.vjp(jax.nn.gelu, yv)
        return (vjp(accs[0])[0],)

    dy = _mm("glu_dy", "nn", [dval, dgate], [wt["w_glu_val"], wt["w_glu_gate"]], [[(0, 0), (1, 1)]], t, d_ssm, tmy, d_ssm,
             [(y, _tile(tmy, d_ssm))], gelu_bwd_epi, [(_out(t, d_ssm, F32), None)])[0]
    gs["ssm_d"] = _colsum_prod("ssm_dd", dy, proj, b_coff=off_s)
    dy_bf = _ew("ssm_dy_cast", lambda v: (v,), [dy], [BF16])[0]
    d_abr, d_abi, d_bbr, d_bbi, d_cre, d_cim, lams = [], [], [], [], [], [], []
    ts = _pick(n_state, 512)
    tc_ = _pick(n_state, 256)
    for dr in range(2):
        gx = _mm1(f"ssm_gx{dr}", "nt", dy_bf, jnp.concatenate([c_re[dr], c_im[dr]], axis=0), t, 2 * n_state, tms,
                  _pick(2 * n_state, 512), F32)
        lr, li = _scan(f"ssm_adj{dr}", gx, abr2[dr:dr + 1], abi2[dr:dr + 1], reverse=(dr == 0), conj=True)
        dar, dai = _ssm_da(f"ssm_da{dr}", lr, li, xs[dr][0], xs[dr][1], reverse=(dr == 1))
        d_abr.append(dar)
        d_abi.append(dai)
        lams += [lr, li]
        d_bbr.append(_diag_in(_mm1(f"ssm_dbre{dr}", "tn", s_bf, lr, d_ssm, n_state, d_ssm, ts, F32), sg, sp, sh))
        d_bbi.append(_diag_in(_mm1(f"ssm_dbim{dr}", "tn", s_bf, li, d_ssm, n_state, d_ssm, ts, F32), sg, sp, sh))
        d_cre.append(_diag_out(_mm1(f"ssm_dcre{dr}", "tn", xs[dr][0], dy_bf, n_state, d_ssm, tc_, d_ssm, F32), sg, sp, sh))
        d_cim.append(-_diag_out(_mm1(f"ssm_dcim{dr}", "tn", xs[dr][1], dy_bf, n_state, d_ssm, tc_, d_ssm, F32), sg, sp, sh))
    ds = _mm("ssm_ds", "nt", lams, [b_re[0], b_im[0], b_re[1], b_im[1]], [[(k, k) for k in range(4)]], t, d_ssm, tmy,
             d_ssm, [(dy, _tile(tmy, d_ssm)), (sm["ssm_d"], _rowvec(d_ssm))],
             lambda accs, dyv, dv: (dyv * dv + accs[0],), [(_out(t, d_ssm, BF16), None)])[0]
    cots = [jnp.concatenate(d_abr, axis=0).reshape(-1, 1), jnp.concatenate(d_abi, axis=0).reshape(-1, 1),
            jnp.concatenate(d_bbr, axis=0), jnp.concatenate(d_bbi, axis=0)]
    d_are, d_aim, d_ldt, d_bre, d_bim = _ssm_disc_bwd(cols, cots)
    gs["ssm_a_re"] = d_are.reshape(2, sg, sp)
    gs["ssm_a_im"] = d_aim.reshape(2, sg, sp)
    gs["ssm_log_dt"] = _rowsum("ssm_dlogdt", d_ldt.reshape(2 * sg, sp)).reshape(2, sg)
    gs["ssm_b_re"] = d_bre.reshape(2, sg, sp, sh)
    gs["ssm_b_im"] = d_bim.reshape(2, sg, sp, sh)
    gs["ssm_c_re"] = jnp.stack(d_cre, axis=0)
    gs["ssm_c_im"] = jnp.stack(d_cim, axis=0)

    dpm = _mm1("pool_dpm", "nn", dz_pool, wt["w_pool_proj"], t, d_pool, tmm, _pick(d_pool, 256), F32)
    dp, gs["pool_w"], gs["pool_scale"] = _pool_bwd(pooled, dpm, pool_w_bf, sm["pool_scale"])

    w_in = wt["w_in"]
    parts = [(dp, 0, d_pool), (ds, d_pool, d_ssm), (dg_pool, off_gp, d), (dg_ssm, off_gs, d)]
    w_in_parts = [w_in[o0:o0 + width] for _, o0, width in parts]
    gb["w_in"] = jnp.concatenate(
        [_mm1(f"in_proj_dw{k}", "tn", p_[0], u, p_[2], d, _pick(p_[2], 256), d, BF16) for k, p_ in enumerate(parts)], axis=0)
    emit("grads_main", gb=gb)
    du = _mm("in_proj_du", "nn", [p_[0] for p_ in parts], w_in_parts, [[(k, k) for k in range(4)]], t, d, tmm, tnm, [],
             lambda accs: (accs[0],), [(_out(t, d, F32), None)])[0]
    dh1, dh1_bf, gs["mix_norm"] = _rms_bwd("mix_norm_bwd", h1, sm["mix_norm"], du, dh2)
    emit("small_early", gs=gs, loss=loss)

    def ffn1_weights_done(d_wg, d_wu, d_wd):
        gb["ffn1_w_gate"], gb["ffn1_w_up"], gb["ffn1_w_down"] = d_wg, d_wu, d_wd
        emit("grads_ffn1", gb=gb)

    dx, _, gs["ffn1_norm"], _, _, _ = _ffn_bwd(
        "ffn1", x, sm["ffn1_norm"], wt["ffn1_w_gate"], wt["ffn1_w_up"], wt["ffn1_w_down"], ffn1_saved, dh1, dh1_bf,
        weights_done=ffn1_weights_done)
    return loss, dx, gb, gs


WEIGHTS = ["ffn1_norm", "ffn1_w_gate", "ffn1_w_up", "ffn1_w_down", "mix_norm", "w_in", "pool_w", "pool_scale",
           "w_pool_proj", "ssm_a_re", "ssm_a_im", "ssm_log_dt", "ssm_b_re", "ssm_b_im", "ssm_c_re", "ssm_c_im", "ssm_d",
           "w_glu_val", "w_glu_gate", "w_mix_out", "xattn_norm", "mem_norm", "w_q", "w_kv", "w_xo", "ffn2_norm",
           "ffn2_w_gate", "ffn2_w_up", "ffn2_w_down", "final_norm"]
COL_SHARDED = ["ffn1_w_gate", "ffn1_w_up", "w_in", "w_pool_proj", "w_glu_val", "w_glu_gate", "w_kv", "ffn2_w_gate",
               "ffn2_w_up"]
ROW_SHARDED = ["ffn1_w_down", "w_mix_out", "w_q", "w_xo", "ffn2_w_down"]
BIG = [n for n in WEIGHTS if n in COL_SHARDED or n in ROW_SHARDED]
SMALL = [n for n in WEIGHTS if n not in BIG]
FFN1_BIG = ["ffn1_w_gate", "ffn1_w_up", "ffn1_w_down"]
MAIN_BIG = [n for n in BIG if n not in FFN1_BIG]
LATE_SMALL = "ffn1_norm"
EARLY_SMALL = [n for n in SMALL if n != LATE_SMALL]
PACK_ROWS = SUBLANES * LANES
GRAD_ROW_TILE = 256


def _to_rows(name, w, width):
    if name in COL_SHARDED:
        w = w.T
    return w.reshape(-1, width)


def _from_rows(name, rows, shard_shape):
    if name in COL_SHARDED:
        return rows.reshape(shard_shape[1], shard_shape[0]).T
    return rows.reshape(shard_shape)


def _pack_small(vals):
    flat = []
    for v in vals:
        f = v.reshape(-1)
        flat.append(jnp.pad(f, (0, (-f.shape[0]) % PACK_ROWS)))
    total = sum(f.shape[0] for f in flat)
    flat.append(jnp.zeros(((-total) % (GRAD_ROW_TILE * LANES),), F32))
    return jnp.concatenate(flat).reshape(-1, LANES)


def _unpack_small(packed, shapes):
    out, row = [], 0
    for shp in shapes:
        size = math.prod(shp)
        rows = -(-size // PACK_ROWS) * SUBLANES
        out.append(packed[row:row + rows].reshape(-1)[:size].reshape(shp))
        row += rows
    return out


def kernel(x, mem, ffn1_norm, ffn1_w_gate, ffn1_w_up, ffn1_w_down, mix_norm, w_in, pool_w, pool_scale, w_pool_proj, ssm_a_re, ssm_a_im, ssm_log_dt, ssm_b_re, ssm_b_im, ssm_c_re, ssm_c_im, ssm_d, w_glu_val, w_glu_gate, w_mix_out, xattn_norm, mem_norm, w_q, w_kv, w_xo, ffn2_norm, ffn2_w_gate, ffn2_w_up, ffn2_w_down, final_norm, loss_target, m_ffn1_norm, m_ffn1_w_gate, m_ffn1_w_up, m_ffn1_w_down, m_mix_norm, m_w_in, m_pool_w, m_pool_scale, m_w_pool_proj, m_ssm_a_re, m_ssm_a_im, m_ssm_log_dt, m_ssm_b_re, m_ssm_b_im, m_ssm_c_re, m_ssm_c_im, m_ssm_d, m_w_glu_val, m_w_glu_gate, m_w_mix_out, m_xattn_norm, m_mem_norm, m_w_q, m_w_kv, m_w_xo, m_ffn2_norm, m_ffn2_w_gate, m_ffn2_w_up, m_ffn2_w_down, m_final_norm, v_ffn1_norm, v_ffn1_w_gate, v_ffn1_w_up, v_ffn1_w_down, v_mix_norm, v_w_in, v_pool_w, v_pool_scale, v_w_pool_proj, v_ssm_a_re, v_ssm_a_im, v_ssm_log_dt, v_ssm_b_re, v_ssm_b_im, v_ssm_c_re, v_ssm_c_im, v_ssm_d, v_w_glu_val, v_w_glu_gate, v_w_mix_out, v_xattn_norm, v_mem_norm, v_w_q, v_w_kv, v_w_xo, v_ffn2_norm, v_ffn2_w_gate, v_ffn2_w_up, v_ffn2_w_down, v_final_norm):
    given = dict(locals())
    wts = {n: given[n] for n in WEIGHTS}
    moms = {n: (given["m_" + n], given["v_" + n]) for n in WEIGHTS}
    x2, mem2, tgt2 = x[0], mem[0], loss_target[0]
    d = x2.shape[1]
    core = lax.axis_index("c").astype(jnp.int32).reshape(1)
    chip = (2 * lax.axis_index("x") + lax.axis_index("y")).astype(jnp.int32).reshape(1)

    def full_form(n, f):
        shard = wts[n][0].shape
        return f.reshape(N_DEV * shard[1], shard[0]) if n in COL_SHARDED else f.reshape(N_DEV * shard[0], shard[1])

    shards = {n: _to_rows(n, wts[n][0], d).astype(BF16) for n in BIG}
    wt = {n: full_form(n, f) for n, f in zip(FFN1_BIG, _allgather("weight_allgather_ffn1", [shards[n] for n in FFN1_BIG]))}
    rest = [n for n in MAIN_BIG if n != "w_in"]
    gather_in = _gather_start("weight_gather_in_start", [shards["w_in"]])
    gather_rest = _gather_start("weight_gather_rest_start", [shards[n] for n in rest])
    sm = {n: (wts[n].reshape(1, -1) if wts[n].ndim <= 2 else wts[n][0]) for n in SMALL}
    sm["ffn1_norm"] = sm["ffn1_norm"] + (gather_in["token"][0, 0] + gather_rest["token"][0, 0])

    pending = {}

    def reduce_start(tag, names, gb):
        blocks = [gb[n].reshape(N_DEV, -1, d) for n in names]
        pad_rows = (-sum(b.shape[1] for b in blocks)) % GRAD_ROW_TILE
        packed = jnp.concatenate(blocks + ([jnp.zeros((N_DEV, pad_rows, d), BF16)] if pad_rows else []), axis=1)
        pair = _pair_sum("grad_pair_sum_" + tag, packed, _exchange_cores("grad_exchange_cores_" + tag, packed), core)
        pending[tag] = (pair, _chips_start("grad_exchange_chips_start_" + tag, pair), [b.shape[1] for b in blocks])

    def reduce_finish(tag, after):
        pair, started, rows = pending[tag]
        recv = _split_wait("grad_exchange_chips_wait_" + tag, started, after)[0]
        return _chip_sum("grad_chip_sum_" + tag, pair, recv, chip), rows

    def ev(name, gb=None, gs=None, loss=None, marker=None):
        if name == "ffn1_fwd_done":
            wt["w_in"] = full_form("w_in", _split_wait("weight_gather_in_wait", gather_in, marker)[0])
        elif name == "mix_in_done":
            for n, f in zip(rest, _split_wait("weight_gather_rest_wait", gather_rest, marker)):
                wt[n] = full_form(n, f)
        elif name == "grads_main":
            reduce_start("main", MAIN_BIG, gb)
        elif name == "small_early":
            pending["small"] = _slots_start("small_gather_start", _pack_small([gs[n] for n in EARLY_SMALL] + [loss[:, :1]]))
        elif name == "grads_ffn1":
            reduce_start("ffn1", FFN1_BIG, gb)

    _, dx, _, gs = _local_step(x2, mem2, tgt2, wt, sm, ev)

    out_g, out_d, out_m, out_v = {}, {}, {}, {}

    def update_big(names, g_rows, rows):
        off = 0
        for n, r in zip(names, rows):
            shard = wts[n].shape
            g_full = _from_rows(n, g_rows[off:off + r], shard[1:]).reshape(shard)
            off += r
            two_d = (-1, shard[-1])
            dl, m2, v2 = _adamw("adamw_" + n, wts[n].reshape(two_d), g_full.reshape(two_d), moms[n][0].reshape(two_d),
                                moms[n][1].reshape(two_d))
            out_g[n], out_d[n], out_m[n], out_v[n] = g_full, dl.reshape(shard), m2.reshape(shard), v2.reshape(shard)
        return dl

    last = update_big(MAIN_BIG, *reduce_finish("main", dx))

    small_sum = _sum_slots("small_sum", _split_wait("small_gather_wait", pending["small"], dx)[0], F32)
    late = _allgather("small_allgather_late", [gs[LATE_SMALL].reshape(-1, LANES)])[0]
    late_sum = _sum_slots("small_sum_late", late.reshape(N_DEV, -1, LANES), F32)
    zero = jnp.zeros((1, 1), F32)
    shapes = [wts[n].shape for n in EARLY_SMALL] + [(1, 1)]
    packs = [_pack_small([src[n] for n in EARLY_SMALL] + [zero])
             for src in (wts, {n: moms[n][0] for n in SMALL}, {n: moms[n][1] for n in SMALL})]
    dl, m2, v2 = _adamw("adamw_small", packs[0], small_sum, packs[1], packs[2])
    for dst, src in ((out_g, small_sum), (out_d, dl), (out_m, m2), (out_v, v2)):
        vals = _unpack_small(src, shapes)
        for n, val in zip(EARLY_SMALL, vals):
            dst[n] = val
        if dst is out_g:
            total_loss = vals[-1].reshape(())
    shp = wts[LATE_SMALL].shape
    dl, m2, v2 = _adamw("adamw_" + LATE_SMALL, wts[LATE_SMALL].reshape(-1, LANES), late_sum,
                        moms[LATE_SMALL][0].reshape(-1, LANES), moms[LATE_SMALL][1].reshape(-1, LANES))
    for dst, src in ((out_g, late_sum), (out_d, dl), (out_m, m2), (out_v, v2)):
        dst[LATE_SMALL] = src.reshape(shp)

    update_big(FFN1_BIG, *reduce_finish("ffn1", last))

    return (total_loss, dx[None], *[out_g[n] for n in WEIGHTS], *[out_d[n] for n in WEIGHTS],
            *[out_m[n] for n in WEIGHTS], *[out_v[n] for n in WEIGHTS])
```

```python
import functools
import math

import jax
import jax.numpy as jnp
from jax import lax
from jax.experimental import pallas as pl
from jax.experimental.pallas import tpu as pltpu

F32 = jnp.float32
BF16 = jnp.bfloat16
EPS = 1e-6
N_XHEADS = 4
POOL_WINDOWS = (2, 4, 8, 16)
ADAM_LR = 0.001
ADAM_B1 = 0.9
ADAM_B2 = 0.999
ADAM_EPS = 1e-08
ADAM_WD = 0.01
ADAM_STEP = 10
N_DEV = 8
VMEM_LIMIT_V7X = 48 * 1024 * 1024
LANES = 128
SUBLANES = 8
POOL_PAD = 16
MESH = pl.DeviceIdType.MESH
ANY = pl.BlockSpec(memory_space=pl.ANY)
HBM = pl.BlockSpec(memory_space=pltpu.HBM)
SEM = pl.BlockSpec(memory_space=pltpu.SEMAPHORE)
SIDE_EFFECT = pltpu.SideEffectType.DATAFLOW_SIDE_EFFECTING

_DIMS = {
    "nt": (((1,), (1,)), ((), ())),
    "nn": (((1,), (0,)), ((), ())),
    "tn": (((0,), (0,)), ((), ())),
}


def _pick(dim, pref, mult=LANES):
    if dim <= pref:
        return dim
    for t in range(pref - pref % mult, 0, -mult):
        if dim % t == 0:
            return t
    return dim


def _params(sem):
    return pltpu.CompilerParams(dimension_semantics=sem, vmem_limit_bytes=VMEM_LIMIT_V7X)


def _tile(tm, tn, coff=0):
    return pl.BlockSpec((tm, tn), lambda i, j: (i, j + coff))


def _rowvec(tn, coff=0):
    return pl.BlockSpec((1, tn), lambda i, j: (0, j + coff))


def _out(m, n, dtype):
    return jax.ShapeDtypeStruct((m, n), dtype)


def _mm(name, form, a_list, b_list, groups, m, n, tm, tn, extras, epilogue, outs, after=None):
    na, nb, ne = len(a_list), len(b_list), len(extras)
    pins = [] if after is None else [after]

    def a_spec(a):
        if form == "tn":
            return pl.BlockSpec((a.shape[0], tm), lambda i, j: (0, i))
        return pl.BlockSpec((tm, a.shape[1]), lambda i, j: (i, 0))

    def b_spec(b):
        if form == "nt":
            return pl.BlockSpec((tn, b.shape[1]), lambda i, j: (j, 0))
        return pl.BlockSpec((b.shape[0], tn), lambda i, j: (0, j))

    def body(*refs):
        a_refs, b_refs = refs[:na], refs[na:na + nb]
        e_refs, o_refs = refs[na + nb:na + nb + ne], refs[na + nb + ne + len(pins):]
        a_vals, b_vals, accs = {}, {}, []
        for group in groups:
            acc = None
            for ai, bi in group:
                if ai not in a_vals:
                    a_vals[ai] = a_refs[ai][...].astype(BF16)
                if bi not in b_vals:
                    b_vals[bi] = b_refs[bi][...].astype(BF16)
                d = lax.dot_general(a_vals[ai], b_vals[bi], _DIMS[form], preferred_element_type=F32)
                acc = d if acc is None else acc + d
            accs.append(acc)
        res = epilogue(accs, *[e[...] for e in e_refs])
        for o_ref, r in zip(o_refs, res):
            o_ref[...] = r.astype(o_ref.dtype)

    out_specs = [_tile(tm, tn) if s is None else s for _, s in outs]
    res = pl.pallas_call(
        body, name=name, grid=(m // tm, n // tn),
        in_specs=[a_spec(a) for a in a_list] + [b_spec(b) for b in b_list] + [s for _, s in extras] + [ANY] * len(pins),
        out_specs=out_specs, out_shape=[o for o, _ in outs],
        compiler_params=_params(("parallel", "parallel")),
    )(*a_list, *b_list, *[e for e, _ in extras], *pins)
    return res


def _mm1(name, form, a, b, m, n, tm, tn, dtype, scale=None):
    epi = (lambda accs: (accs[0],)) if scale is None else (lambda accs: (accs[0] * scale,))
    return _mm(name, form, [a], [b], [[(0, 0)]], m, n, tm, tn, [], epi, [(_out(m, n, dtype), None)])[0]


def _rms_fwd(name, h, g):
    t, d = h.shape
    tm = _pick(t, 512, SUBLANES)

    def body(h_ref, g_ref, n_ref):
        hv = h_ref[...]
        r = lax.rsqrt(jnp.mean(hv * hv, axis=-1, keepdims=True) + EPS)
        n_ref[...] = ((hv * r) * g_ref[...]).astype(BF16)

    return pl.pallas_call(
        body, name=name, grid=(t // tm,),
        in_specs=[pl.BlockSpec((tm, d), lambda i: (i, 0)), pl.BlockSpec((1, d), lambda i: (0, 0))],
        out_specs=pl.BlockSpec((tm, d), lambda i: (i, 0)), out_shape=_out(t, d, BF16),
        compiler_params=_params(("parallel",)),
    )(h, g)


def _rms_bwd(name, h, g, dn, dres=None):
    t, d = h.shape
    tm = _pick(t, 256, SUBLANES)
    need_dh = dres is not None

    def body(*refs):
        if need_dh:
            h_ref, g_ref, dn_ref, dres_ref, dh_ref, dhb_ref, dg_ref = refs
        else:
            h_ref, g_ref, dn_ref, dg_ref = refs
        hv = h_ref[...]
        r = lax.rsqrt(jnp.mean(hv * hv, axis=-1, keepdims=True) + EPS)
        nh = hv * r
        dnv = dn_ref[...].astype(F32)

        @pl.when(pl.program_id(0) == 0)
        def _():
            dg_ref[...] = jnp.zeros_like(dg_ref)

        dg_ref[...] += jnp.sum(dnv * nh, axis=0, keepdims=True)
        if need_dh:
            dng = dnv * g_ref[...]
            dh = dres_ref[...] + r * (dng - nh * jnp.mean(dng * nh, axis=-1, keepdims=True))
            dh_ref[...] = dh
            dhb_ref[...] = dh.astype(BF16)

    row = pl.BlockSpec((tm, d), lambda i: (i, 0))
    vec = pl.BlockSpec((1, d), lambda i: (0, 0))
    if need_dh:
        return pl.pallas_call(
            body, name=name, grid=(t // tm,), in_specs=[row, vec, row, row], out_specs=[row, row, vec],
            out_shape=[_out(t, d, F32), _out(t, d, BF16), _out(1, d, F32)], compiler_params=_params(("arbitrary",)),
        )(h, g, dn, dres)
    return pl.pallas_call(
        body, name=name, grid=(t // tm,), in_specs=[row, vec, row], out_specs=vec,
        out_shape=_out(1, d, F32), compiler_params=_params(("arbitrary",)),
    )(h, g, dn)


def _loss_head(h, g, tgt):
    t, d = h.shape
    tm = _pick(t, 256, SUBLANES)

    def body(h_ref, g_ref, t_ref, dh_ref, dhb_ref, dg_ref, loss_ref):
        hv = h_ref[...]
        r = lax.rsqrt(jnp.mean(hv * hv, axis=-1, keepdims=True) + EPS)
        nh = hv * r
        err = nh * g_ref[...] - t_ref[...]

        @pl.when(pl.program_id(0) == 0)
        def _():
            dg_ref[...] = jnp.zeros_like(dg_ref)
            loss_ref[...] = jnp.zeros_like(loss_ref)

        per_row = jnp.mean(err * err, axis=-1, keepdims=True)
        loss_ref[...] += 0.5 * jnp.sum(per_row, axis=0, keepdims=True)
        dy = err * (1.0 / d)
        dg_ref[...] += jnp.sum(dy * nh, axis=0, keepdims=True)
        dng = dy * g_ref[...]
        dh = r * (dng - nh * jnp.mean(dng * nh, axis=-1, keepdims=True))
        dh_ref[...] = dh
        dhb_ref[...] = dh.astype(BF16)

    row = pl.BlockSpec((tm, d), lambda i: (i, 0))
    vec = pl.BlockSpec((1, d), lambda i: (0, 0))
    return pl.pallas_call(
        body, name="loss_head", grid=(t // tm,), in_specs=[row, vec, row],
        out_specs=[row, row, vec, pl.BlockSpec((1, LANES), lambda i: (0, 0))],
        out_shape=[_out(t, d, F32), _out(t, d, BF16), _out(1, d, F32), _out(1, LANES, F32)],
        compiler_params=_params(("arbitrary",)),
    )(h, g, tgt)


def _ffn_fwd(tag, h, g, wg_t, wu_t, wd):
    t, d = h.shape
    f = wd.shape[0]
    n = _rms_fwd(tag + "_norm", h, g)
    tm, tn = _pick(t, 1024), _pick(f, 256)

    def up_epi(accs):
        a, b = accs
        return a, b, (a * jax.nn.sigmoid(a)) * b

    a, b, hid = _mm(tag + "_up", "nt", [n], [wg_t, wu_t], [[(0, 0)], [(0, 1)]], t, f, tm, tn, [], up_epi,
                    [(_out(t, f, BF16), None)] * 3)
    tm2, tn2 = _pick(t, 512), _pick(d, 256)
    h_out = _mm(tag + "_down", "nn", [hid], [wd], [[(0, 0)]], t, d, tm2, tn2, [(h, _tile(tm2, tn2))],
                lambda accs, hin: (hin + 0.5 * accs[0],), [(_out(t, d, F32), None)])[0]
    return h_out, (n, a, b, hid)


def _ffn_bwd(tag, h, g, wg_t, wu_t, wd, saved, dh, dh_bf, weights_done=None, after=None):
    n, a, b, hid = saved
    t, d = h.shape
    f = wd.shape[0]
    tm, tn = _pick(t, 1024), _pick(f, 256)

    def hid_epi(accs, av, bv):
        dhid = 0.5 * accs[0]
        av, bv = av.astype(F32), bv.astype(F32)
        sig = jax.nn.sigmoid(av)
        da = dhid * bv * (sig * (1.0 + av * (1.0 - sig)))
        db = dhid * (av * sig)
        return da, db

    da, db = _mm(tag + "_bwd_hid", "nt", [dh_bf], [wd], [[(0, 0)]], t, f, tm, tn,
                 [(a, _tile(tm, tn)), (b, _tile(tm, tn))], hid_epi, [(_out(t, f, BF16), None)] * 2, after=after)
    tw = _pick(f, 256)
    d_wd = _mm1(tag + "_dwd", "tn", hid, dh_bf, f, d, tw, d, BF16, scale=0.5)
    d_wg = _mm1(tag + "_dwg", "tn", da, n, f, d, tw, d, BF16)
    d_wu = _mm1(tag + "_dwu", "tn", db, n, f, d, tw, d, BF16)
    pin = weights_done(d_wg, d_wu, d_wd) if weights_done is not None else None
    tm2, tn2 = _pick(t, 512), _pick(d, 256)
    dn = _mm(tag + "_dn", "nn", [da, db], [wg_t, wu_t], [[(0, 0), (1, 1)]], t, d, tm2, tn2, [],
             lambda accs: (accs[0],), [(_out(t, d, F32), None)], after=pin)[0]
    dh_in, dh_in_bf, dg = _rms_bwd(tag + "_norm_bwd", h, g, dn, dh)
    return dh_in, dh_in_bf, dg, d_wg, d_wu, d_wd


def _window_sum(win, offsets):
    n = win.shape[0]
    acc = None
    for j in offsets:
        term = win if j == 0 else pltpu.roll(win, (-j) % n, 0)
        acc = term if acc is None else acc + term
    return acc


def _pool_counts(r0, ch, c, left, right, t):
    pos = r0 + lax.broadcasted_iota(jnp.int32, (ch, c), 0)
    return (jnp.minimum(pos + right + 1, t) - jnp.maximum(pos - left, 0)).astype(F32)


def _pool_fwd(proj, pool_w_bf, pool_scale):
    t = proj.shape[0]
    ng, c, _ = pool_w_bf.shape
    ch = _pick(t, 256, SUBLANES)
    pad = POOL_PAD

    def body(p_ref, w_ref, s_ref, pooled_ref, pm_ref, buf):
        grp = pl.program_id(0)
        buf[pl.ds(0, pad), :] = jnp.zeros((pad, c), F32)
        buf[pl.ds(pad + t, pad), :] = jnp.zeros((pad, c), F32)

        def fill(ci, carry):
            r0 = pl.multiple_of(ci * ch, SUBLANES)
            buf[pl.ds(pl.multiple_of(r0 + pad, SUBLANES), ch), :] = p_ref[pl.ds(r0, ch), :]
            return carry

        lax.fori_loop(0, t // ch, fill, 0)
        for gi, w in enumerate(POOL_WINDOWS):
            left = w // 2
            right = w - 1 - left

            @pl.when(grp == gi)
            def _(left=left, right=right):
                def chunk(ci, carry):
                    r0 = pl.multiple_of(ci * ch, SUBLANES)
                    win = buf[pl.ds(r0, ch + 2 * pad), :]
                    s = _window_sum(win, range(-left, right + 1))[pad:pad + ch]
                    pooled = s / _pool_counts(r0, ch, c, left, right, t) - win[pad:pad + ch]
                    pooled_bf = pooled.astype(BF16)
                    mixed = jnp.dot(pooled_bf, w_ref[0], preferred_element_type=F32)
                    pooled_ref[pl.ds(r0, ch), :] = pooled_bf
                    pm_ref[pl.ds(r0, ch), :] = (mixed * s_ref[...]).astype(BF16)
                    return carry

                lax.fori_loop(0, t // ch, chunk, 0)

    col = pl.BlockSpec((t, c), lambda g: (0, g))
    return pl.pallas_call(
        body, name="pool_fwd", grid=(ng,),
        in_specs=[col, pl.BlockSpec((1, c, c), lambda g: (g, 0, 0)), pl.BlockSpec((1, c), lambda g: (0, g))],
        out_specs=[col, col], out_shape=[_out(t, ng * c, BF16), _out(t, ng * c, BF16)],
        scratch_shapes=[pltpu.VMEM((t + 2 * pad, c), F32)],
        compiler_params=_params(("parallel",)),
    )(proj, pool_w_bf, pool_scale)


def _pool_bwd(pooled, dpm, pool_w_bf, pool_scale):
    t = pooled.shape[0]
    ng, c, _ = pool_w_bf.shape
    ch = _pick(t, 256, SUBLANES)
    pad = POOL_PAD

    def body(pooled_ref, dpm_ref, w_ref, s_ref, dp_ref, dw_ref, ds_ref, buf, raw):
        grp = pl.program_id(0)
        buf[pl.ds(0, pad), :] = jnp.zeros((pad, c), F32)
        buf[pl.ds(pad + t, pad), :] = jnp.zeros((pad, c), F32)
        dw_ref[...] = jnp.zeros_like(dw_ref)
        ds_ref[...] = jnp.zeros_like(ds_ref)
        for gi, w in enumerate(POOL_WINDOWS):
            left = w // 2
            right = w - 1 - left

            @pl.when(grp == gi)
            def _(left=left, right=right):
                def first(ci, carry):
                    r0 = pl.multiple_of(ci * ch, SUBLANES)
                    pv = pooled_ref[pl.ds(r0, ch), :]
                    dpm_v = dpm_ref[pl.ds(r0, ch), :]
                    mixed = jnp.dot(pv, w_ref[0], preferred_element_type=F32)
                    ds_ref[...] += jnp.sum(dpm_v * mixed, axis=0, keepdims=True)
                    dmixed = (dpm_v * s_ref[...]).astype(BF16)
                    dw_ref[0] += lax.dot_general(pv, dmixed, _DIMS["tn"], preferred_element_type=F32)
                    dpooled = lax.dot_general(dmixed, w_ref[0], _DIMS["nt"], preferred_element_type=F32)
                    raw[pl.ds(r0, ch), :] = dpooled
                    buf[pl.ds(pl.multiple_of(r0 + pad, SUBLANES), ch), :] = (
                        dpooled / _pool_counts(r0, ch, c, left, right, t))
                    return carry

                lax.fori_loop(0, t // ch, first, 0)

                def second(ci, carry):
                    r0 = pl.multiple_of(ci * ch, SUBLANES)
                    win = buf[pl.ds(r0, ch + 2 * pad), :]
                    s = _window_sum(win, range(-right, left + 1))[pad:pad + ch]
                    dp_ref[pl.ds(r0, ch), :] = (s - raw[pl.ds(r0, ch), :]).astype(BF16)
                    return carry

                lax.fori_loop(0, t // ch, second, 0)

    col = pl.BlockSpec((t, c), lambda g: (0, g))
    return pl.pallas_call(
        body, name="pool_bwd", grid=(ng,),
        in_specs=[col, col, pl.BlockSpec((1, c, c), lambda g: (g, 0, 0)), pl.BlockSpec((1, c), lambda g: (0, g))],
        out_specs=[col, pl.BlockSpec((1, c, c), lambda g: (g, 0, 0)), pl.BlockSpec((1, c), lambda g: (0, g))],
        out_shape=[_out(t, ng * c, BF16), jax.ShapeDtypeStruct((ng, c, c), F32), _out(1, ng * c, F32)],
        scratch_shapes=[pltpu.VMEM((t + 2 * pad, c), F32), pltpu.VMEM((t, c), F32)],
        compiler_params=_params(("parallel",)),
    )(pooled, dpm, pool_w_bf, pool_scale)


def _discretise(a_re, a_im, log_dt, b_re, b_im):
    dt = jnp.exp(log_dt)
    mag = jnp.exp(dt * a_re)
    ang = dt * a_im
    abr = mag * jnp.cos(ang)
    abi = mag * jnp.sin(ang)
    den = a_re * a_re + a_im * a_im
    nr = abr - 1.0
    qr = (nr * a_re + abi * a_im) / den
    qi = (abi * a_re - nr * a_im) / den
    return abr, abi, qr * b_re - qi * b_im, qr * b_im + qi * b_re


def _ssm_disc(cols):
    n, hh = cols[3].shape

    def body(ar, ai, ld, br, bi, o1, o2, o3, o4):
        res = _discretise(ar[...], ai[...], ld[...], br[...], bi[...])
        for o, r in zip((o1, o2, o3, o4), res):
            o[...] = r

    return pl.pallas_call(
        body, name="ssm_disc",
        out_shape=[_out(n, 1, F32), _out(n, 1, F32), _out(n, hh, F32), _out(n, hh, F32)],
    )(*cols)


def _ssm_disc_bwd(cols, cots):
    n, hh = cols[3].shape

    def body(ar, ai, ld, br, bi, c1, c2, c3, c4, o1, o2, o3, o4, o5):
        _, vjp = jax.vjp(_discretise, ar[...], ai[...], ld[...], br[...], bi[...])
        res = vjp((c1[...], c2[...], c3[...], c4[...]))
        for o, r in zip((o1, o2, o3, o4, o5), res):
            o[...] = r

    return pl.pallas_call(
        body, name="ssm_disc_bwd",
        out_shape=[_out(n, 1, F32)] * 3 + [_out(n, hh, F32)] * 2,
    )(*cols, *cots)


def _rowsum(name, a):
    r, _ = a.shape

    def body(a_ref, o_ref):
        o_ref[...] = jnp.sum(a_ref[...], axis=-1, keepdims=True)

    return pl.pallas_call(body, name=name, out_shape=_out(r, 1, F32))(a)


def _cmul(pr, pi, qr, qi):
    return pr * qr - pi * qi, pr * qi + pi * qr


def _scan(name, u, ar, ai, reverse, conj):
    t, s2 = u.shape
    s = s2 // 2
    w = _pick(s, 512)
    tc = _pick(t, 512, SUBLANES)
    n_t, n_w = t // tc, s // w
    groups = tc // SUBLANES
    last = 0 if reverse else SUBLANES - 1

    def body(ar_ref, ai_ref, ur_ref, ui_ref, xr_ref, xi_ref, cr_ref, ci_ref):
        @pl.when(pl.program_id(1) == 0)
        def _():
            cr_ref[...] = jnp.zeros_like(cr_ref)
            ci_ref[...] = jnp.zeros_like(ci_ref)

        a1r = ar_ref[...]
        a1i = -ai_ref[...] if conj else ai_ref[...]
        a2r, a2i = _cmul(a1r, a1i, a1r, a1i)
        a4r, a4i = _cmul(a2r, a2i, a2r, a2i)
        row = lax.broadcasted_iota(jnp.int32, (SUBLANES, w), 0)
        pwr = jnp.zeros((SUBLANES, w), F32)
        pwi = jnp.zeros((SUBLANES, w), F32)
        cur_r, cur_i = a1r, a1i
        for k in range(SUBLANES):
            rk = SUBLANES - 1 - k if reverse else k
            pwr = jnp.where(row == rk, cur_r, pwr)
            pwi = jnp.where(row == rk, cur_i, pwi)
            cur_r, cur_i = _cmul(cur_r, cur_i, a1r, a1i)
        steps = ((1, a1r, a1i), (2, a2r, a2i), (4, a4r, a4i))

        def one(i, carry):
            g = groups - 1 - i if reverse else i
            r0 = pl.multiple_of(g * SUBLANES, SUBLANES)
            br = ur_ref[pl.ds(r0, SUBLANES), :]
            bi = ui_ref[pl.ds(r0, SUBLANES), :]
            for dist, pr, pi in steps:
                if reverse:
                    keep = row < SUBLANES - dist
                    shift = SUBLANES - dist
                else:
                    keep = row >= dist
                    shift = dist
                sr = jnp.where(keep, pltpu.roll(br, shift, 0), 0.0)
                si = jnp.where(keep, pltpu.roll(bi, shift, 0), 0.0)
                br, bi = br + pr * sr - pi * si, bi + pr * si + pi * sr
            cr = cr_ref[pl.ds(last, 1), :]
            ci = ci_ref[pl.ds(last, 1), :]
            xr = br + pwr * cr - pwi * ci
            xi = bi + pwr * ci + pwi * cr
            xr_ref[pl.ds(r0, SUBLANES), :] = xr
            xi_ref[pl.ds(r0, SUBLANES), :] = xi
            cr_ref[...] = xr
            ci_ref[...] = xi
            return carry

        lax.fori_loop(0, groups, one, 0)

    def tmap(k):
        return n_t - 1 - k if reverse else k

    re_blk = pl.BlockSpec((tc, w), lambda cb, k: (tmap(k), cb))
    im_blk = pl.BlockSpec((tc, w), lambda cb, k: (tmap(k), cb + n_w))
    a_blk = pl.BlockSpec((1, w), lambda cb, k: (0, cb))
    xr, xi = pl.pallas_call(
        body, name=name, grid=(n_w, n_t), in_specs=[a_blk, a_blk, re_blk, im_blk],
        out_specs=[pl.BlockSpec((tc, w), lambda cb, k: (tmap(k), cb))] * 2,
        out_shape=[_out(t, s, F32), _out(t, s, F32)],
        scratch_shapes=[pltpu.VMEM((SUBLANES, w), F32), pltpu.VMEM((SUBLANES, w), F32)],
        compiler_params=_params(("parallel", "arbitrary")),
    )(ar, ai, u, u)
    return xr, xi


def _ssm_da(name, lr, li, xr, xi, reverse):
    t, s = xr.shape
    w = _pick(s, 512)
    tc = _pick(t, 256, SUBLANES)
    n_t, n_w = t // tc, s // w

    def body(lr_ref, li_ref, xr_ref, xi_ref, dar_ref, dai_ref, pr_ref, pi_ref):
        @pl.when(pl.program_id(1) == 0)
        def _():
            pr_ref[...] = jnp.zeros_like(pr_ref)
            pi_ref[...] = jnp.zeros_like(pi_ref)
            dar_ref[...] = jnp.zeros_like(dar_ref)
            dai_ref[...] = jnp.zeros_like(dai_ref)

        row = lax.broadcasted_iota(jnp.int32, (tc, w), 0)
        xrv, xiv = xr_ref[...], xi_ref[...]
        if reverse:
            keep, shift, edge = row < tc - 1, tc - 1, 0
        else:
            keep, shift, edge = row >= 1, 1, tc - 1
        xsr = jnp.where(keep, pltpu.roll(xrv, shift, 0), pr_ref[pl.ds(0, 1), :])
        xsi = jnp.where(keep, pltpu.roll(xiv, shift, 0), pi_ref[pl.ds(0, 1), :])
        lrv, liv = lr_ref[...], li_ref[...]
        dar_ref[...] += jnp.sum(lrv * xsr + liv * xsi, axis=0, keepdims=True)
        dai_ref[...] += jnp.sum(liv * xsr - lrv * xsi, axis=0, keepdims=True)
        pr_ref[pl.ds(0, 1), :] = xr_ref[pl.ds(edge, 1), :]
        pi_ref[pl.ds(0, 1), :] = xi_ref[pl.ds(edge, 1), :]

    def tmap(k):
        return n_t - 1 - k if reverse else k

    blk = pl.BlockSpec((tc, w), lambda cb, k: (tmap(k), cb))
    vec = pl.BlockSpec((1, w), lambda cb, k: (0, cb))
    return pl.pallas_call(
        body, name=name, grid=(n_w, n_t), in_specs=[blk] * 4, out_specs=[vec, vec],
        out_shape=[_out(1, s, F32), _out(1, s, F32)],
        scratch_shapes=[pltpu.VMEM((SUBLANES, w), F32), pltpu.VMEM((SUBLANES, w), F32)],
        compiler_params=_params(("parallel", "arbitrary")),
    )(lr, li, xr, xi)


def _colsum_prod(name, a, b, b_coff=0):
    t, n = a.shape
    tm = _pick(t, 512, SUBLANES)

    def body(a_ref, b_ref, o_ref):
        @pl.when(pl.program_id(0) == 0)
        def _():
            o_ref[...] = jnp.zeros_like(o_ref)

        o_ref[...] += jnp.sum(a_ref[...].astype(F32) * b_ref[...].astype(F32), axis=0, keepdims=True)

    return pl.pallas_call(
        body, name=name, grid=(t // tm,),
        in_specs=[pl.BlockSpec((tm, n), lambda i: (i, 0)), pl.BlockSpec((tm, n), lambda i: (i, b_coff))],
        out_specs=pl.BlockSpec((1, n), lambda i: (0, 0)), out_shape=_out(1, n, F32),
        compiler_params=_params(("arbitrary",)),
    )(a, b)


def _bd_in(bb, g, p, hh):
    blk = bb.reshape(g, p, hh).transpose(0, 2, 1)
    eye = jnp.eye(g, dtype=bool)[:, None, :, None]
    return jnp.where(eye, blk[:, :, None, :], 0.0).reshape(g * hh, g * p)


def _bd_out(cc, g, p, hh):
    blk = cc.transpose(0, 2, 1)
    eye = jnp.eye(g, dtype=bool)[:, None, :, None]
    return jnp.where(eye, blk[:, :, None, :], 0.0).reshape(g * p, g * hh)


def _diag_in(dmat, g, p, hh):
    eye = jnp.eye(g, dtype=bool)[:, None, :, None]
    diag = jnp.sum(jnp.where(eye, dmat.reshape(g, hh, g, p), 0.0), axis=2)
    return diag.transpose(0, 2, 1).reshape(g * p, hh)


def _diag_out(dmat, g, p, hh):
    eye = jnp.eye(g, dtype=bool)[:, None, :, None]
    diag = jnp.sum(jnp.where(eye, dmat.reshape(g, p, g, hh), 0.0), axis=2)
    return diag.transpose(0, 2, 1)


def _softmax(qh, kh, scale):
    s = lax.dot_general(qh, kh, _DIMS["nt"], preferred_element_type=F32) * scale
    e = jnp.exp(s - jnp.max(s, axis=-1, keepdims=True))
    return e / jnp.sum(e, axis=-1, keepdims=True)


def _attn_fwd(q, kv):
    t, d = q.shape
    mm_ = kv.shape[0]
    hd = d // N_XHEADS
    scale = 1.0 / math.sqrt(hd)
    tm = _pick(t, 512, SUBLANES)

    def body(q_ref, kv_ref, o_ref):
        for h in range(N_XHEADS):
            sl = pl.ds(h * hd, hd)
            p = _softmax(q_ref[:, sl], kv_ref[:, sl], scale)
            o_ref[:, sl] = jnp.dot(p.astype(BF16), kv_ref[:, pl.ds(d + h * hd, hd)],
                                   preferred_element_type=F32).astype(BF16)

    return pl.pallas_call(
        body, name="attn_fwd", grid=(t // tm,),
        in_specs=[pl.BlockSpec((tm, d), lambda i: (i, 0)), pl.BlockSpec((mm_, 2 * d), lambda i: (0, 0))],
        out_specs=pl.BlockSpec((tm, d), lambda i: (i, 0)), out_shape=_out(t, d, BF16),
        compiler_params=_params(("parallel",)),
    )(q, kv)


def _attn_bwd(q, kv, do):
    t, d = q.shape
    mm_ = kv.shape[0]
    hd = d // N_XHEADS
    scale = 1.0 / math.sqrt(hd)
    tm = _pick(t, 512, SUBLANES)

    def body(q_ref, kv_ref, do_ref, dq_ref, dkv_ref):
        @pl.when(pl.program_id(0) == 0)
        def _():
            dkv_ref[...] = jnp.zeros_like(dkv_ref)

        for h in range(N_XHEADS):
            sl = pl.ds(h * hd, hd)
            vsl = pl.ds(d + h * hd, hd)
            qh, kh, doh = q_ref[:, sl], kv_ref[:, sl], do_ref[:, sl]
            p = _softmax(qh, kh, scale)
            dp = lax.dot_general(doh, kv_ref[:, vsl], _DIMS["nt"], preferred_element_type=F32)
            dkv_ref[:, vsl] += lax.dot_general(p.astype(BF16), doh, _DIMS["tn"], preferred_element_type=F32)
            ds = (p * (dp - jnp.sum(dp * p, axis=-1, keepdims=True)) * scale).astype(BF16)
            dq_ref[:, sl] = jnp.dot(ds, kh, preferred_element_type=F32).astype(BF16)
            dkv_ref[:, sl] += lax.dot_general(ds, qh, _DIMS["tn"], preferred_element_type=F32)

    row = pl.BlockSpec((tm, d), lambda i: (i, 0))
    full = pl.BlockSpec((mm_, 2 * d), lambda i: (0, 0))
    return pl.pallas_call(
        body, name="attn_bwd", grid=(t // tm,), in_specs=[row, full, row], out_specs=[row, full],
        out_shape=[_out(t, d, BF16), _out(mm_, 2 * d, F32)], compiler_params=_params(("arbitrary",)),
    )(q, kv, do)


def _ew(name, fn, ins, outs, rows_pref=256):
    r, c = ins[0].shape
    tr = _pick(r, rows_pref, SUBLANES)
    ni = len(ins)

    def body(*refs):
        res = fn(*[x[...] for x in refs[:ni]])
        for o_ref, v in zip(refs[ni:], res):
            o_ref[...] = v.astype(o_ref.dtype)

    blk = pl.BlockSpec((tr, c), lambda i: (i, 0))
    return pl.pallas_call(
        body, name=name, grid=(r // tr,), in_specs=[blk] * ni, out_specs=[blk] * len(outs),
        out_shape=[_out(r, c, dt) for dt in outs], compiler_params=_params(("parallel",)),
    )(*ins)


def _sum_slots(name, a, dtype):
    s, r, c = a.shape
    tr = _pick(r, 256, SUBLANES)

    def body(a_ref, o_ref):
        acc = a_ref[0].astype(F32)
        for k in range(1, s):
            acc = acc + a_ref[k].astype(F32)
        o_ref[...] = acc.astype(o_ref.dtype)

    return pl.pallas_call(
        body, name=name, grid=(r // tr,), in_specs=[pl.BlockSpec((s, tr, c), lambda i: (0, i, 0))],
        out_specs=pl.BlockSpec((tr, c), lambda i: (i, 0)), out_shape=_out(r, c, dtype),
        compiler_params=_params(("parallel",)),
    )(a)


def _adamw(name, w, g, m, v):
    bc1 = 1.0 - ADAM_B1 ** ADAM_STEP
    bc2 = 1.0 - ADAM_B2 ** ADAM_STEP

    def fn(wv, gv, mv, vv):
        m2 = ADAM_B1 * mv + (1.0 - ADAM_B1) * gv
        v2 = ADAM_B2 * vv + (1.0 - ADAM_B2) * (gv * gv)
        delta = -ADAM_LR * ((m2 / bc1) / (jnp.sqrt(v2 / bc2) + ADAM_EPS) + ADAM_WD * wv)
        return delta, m2, v2

    return _ew(name, fn, [w, g, m, v], [F32, F32, F32])


def _allgather(name, arrs):
    n = len(arrs)

    def body(*refs):
        ins, outs = refs[:n], refs[n:2 * n]
        send_sems, recv_sems, local_sems = refs[2 * n:]
        x, y, c = lax.axis_index("x"), lax.axis_index("y"), lax.axis_index("c")
        me, sibling = (x, y, c), (x, y, 1 - c)
        chips = [(1 - x, y), (x, 1 - y), (1 - x, 1 - y)]

        def rows(a, px, py, pc):
            r = ins[a].shape[0]
            return outs[a].at[pl.ds((4 * px + 2 * py + pc) * r, r), :]

        def copy(a, k, block, to, src=None):
            return pltpu.make_async_remote_copy(
                src_ref=rows(a, *block) if src is None else src, dst_ref=rows(a, *block),
                send_sem=send_sems.at[a, k], recv_sem=recv_sems.at[a, k], device_id=to, device_id_type=MESH)

        mine = [pltpu.make_async_copy(ins[a], rows(a, *me), local_sems.at[a]) for a in range(n)]
        for cp in mine:
            cp.start()
        first = []
        for a in range(n):
            first.append(copy(a, 0, me, sibling, src=ins[a]))
            first += [copy(a, 1 + j, me, (*chip, c), src=ins[a]) for j, chip in enumerate(chips)]
        for cp in first:
            cp.start()
        passed = []
        for j, chip in enumerate(chips):
            for a in range(n):
                copy(a, 1 + j, (*chip, c), me).wait_recv()
                cp = copy(a, 4 + j, (*chip, c), sibling)
                cp.start()
                passed.append(cp)
        for a in range(n):
            copy(a, 0, sibling, me).wait_recv()
            for j, chip in enumerate(chips):
                copy(a, 4 + j, (*chip, 1 - c), me).wait_recv()
        for cp in first + passed:
            cp.wait_send()
        for cp in mine:
            cp.wait()

    return pl.pallas_call(
        body, name=name, in_specs=[ANY] * n, out_specs=[ANY] * n,
        out_shape=[_out(N_DEV * a.shape[0], a.shape[1], a.dtype) for a in arrs],
        scratch_shapes=[pltpu.SemaphoreType.DMA((n, 7)), pltpu.SemaphoreType.DMA((n, 7)), pltpu.SemaphoreType.DMA((n,))],
    )(*arrs)


def _exchange_cores(name, g):
    _, r, c = g.shape
    nck = r // GRAD_ROW_TILE

    def body(g_ref, recv_ref, send_sems, recv_sems):
        x, y, cc = lax.axis_index("x"), lax.axis_index("y"), lax.axis_index("c")
        copies = []
        for q in range(4):
            for k in range(nck):
                rows = pl.ds(k * GRAD_ROW_TILE, GRAD_ROW_TILE)
                copies.append(pltpu.make_async_remote_copy(
                    src_ref=g_ref.at[2 * q + (1 - cc), rows], dst_ref=recv_ref.at[q, rows],
                    send_sem=send_sems.at[q, k], recv_sem=recv_sems.at[q, k], device_id=(x, y, 1 - cc),
                    device_id_type=MESH))
        for cp in copies:
            cp.start()
        for cp in copies:
            cp.wait()

    return pl.pallas_call(
        body, name=name, in_specs=[ANY], out_specs=ANY,
        out_shape=jax.ShapeDtypeStruct((4, r, c), g.dtype),
        scratch_shapes=[pltpu.SemaphoreType.DMA((4, nck)), pltpu.SemaphoreType.DMA((4, nck))],
    )(g)


def _pair_sum(name, g, recv, core):
    _, r, c = g.shape
    tr = GRAD_ROW_TILE

    def body(core_ref, g_ref, r_ref, o_ref):
        o_ref[...] = (g_ref[...].astype(F32) + r_ref[...].astype(F32)).astype(o_ref.dtype)

    blk = pl.BlockSpec((None, tr, c), lambda q, i, core_ref: (q, i, 0))
    return pl.pallas_call(
        body, name=name,
        grid_spec=pltpu.PrefetchScalarGridSpec(
            num_scalar_prefetch=1, grid=(4, r // tr),
            in_specs=[pl.BlockSpec((None, tr, c), lambda q, i, core_ref: (2 * q + core_ref[0], i, 0)), blk],
            out_specs=blk),
        out_shape=jax.ShapeDtypeStruct((4, r, c), g.dtype), compiler_params=_params(("parallel", "parallel")),
    )(core, g, recv)


def _peer(k, x, y, c):
    return (1 - x if k & 4 else x, 1 - y if k & 2 else y, 1 - c if k & 1 else c)


def _split_start(name, srcs, land_shapes, n_remote, n_local, build, after=None):
    ns, nl = len(srcs), len(land_shapes)
    n_sem = 3 if n_local else 2
    pins = [] if after is None else [after]

    def body(*refs):
        src_refs, land_refs = refs[:ns], refs[ns:ns + nl]
        sems = refs[ns + nl + len(pins):ns + nl + len(pins) + n_sem]
        token = refs[-1]
        remote, local = build(src_refs, land_refs, *sems)
        for cp in local + remote:
            cp.start()
        token[...] = jnp.zeros_like(token)

    sem_shapes = [pltpu.SemaphoreType.DMA((n_remote,)), pltpu.SemaphoreType.DMA((n_remote,))]
    if n_local:
        sem_shapes.append(pltpu.SemaphoreType.DMA((n_local,)))
    bufs = [pltpu.with_memory_space_constraint(a, pltpu.HBM) for a in srcs]
    bufs += [pltpu.with_memory_space_constraint(lax.empty(s.shape, s.dtype), pltpu.HBM) for s in land_shapes]
    outs = pl.pallas_call(
        body, name=name,
        out_shape=sem_shapes + [pltpu.HBM(b.shape, b.dtype) for b in bufs] + [jax.ShapeDtypeStruct((SUBLANES, LANES), F32)],
        in_specs=[HBM] * (ns + nl) + [ANY] * len(pins),
        out_specs=[SEM] * n_sem + [HBM] * (ns + nl) + [pl.BlockSpec(memory_space=pltpu.VMEM)],
        input_output_aliases={i: n_sem + i for i in range(ns + nl)},
        compiler_params=pltpu.CompilerParams(has_side_effects=SIDE_EFFECT),
    )(*bufs, *pins)
    return dict(sems=list(outs[:n_sem]), bufs=list(outs[n_sem:n_sem + ns + nl]), token=outs[-1], build=build, ns=ns)


def _split_wait(name, started, after):
    ns, n_buf, n_sem = started["ns"], len(started["bufs"]), len(started["sems"])

    def body(*refs):
        src_refs, land_refs = refs[:ns], refs[ns:n_buf]
        sems = refs[n_buf:n_buf + n_sem]
        remote, local = started["build"](src_refs, land_refs, *sems)
        for cp in local:
            cp.wait()
        for cp in remote:
            cp.wait_send()
            cp.wait_recv()

    outs = pl.pallas_call(
        body, name=name, out_shape=[pltpu.HBM(b.shape, b.dtype) for b in started["bufs"]],
        in_specs=[HBM] * n_buf + [SEM] * n_sem + [ANY], out_specs=[HBM] * n_buf,
        input_output_aliases={i: i for i in range(n_buf)},
        compiler_params=pltpu.CompilerParams(has_side_effects=SIDE_EFFECT),
    )(*started["bufs"], *started["sems"], after)
    return list(outs[ns:])


def _gather_start(name, shards, after):
    m = len(shards)

    def build(src_refs, land_refs, send_sems, recv_sems, local_sems):
        x, y, c = lax.axis_index("x"), lax.axis_index("y"), lax.axis_index("c")
        remote, local = [], []
        for j in range(m):
            r = src_refs[j].shape[0]
            dst = land_refs[j].at[pl.ds((4 * x + 2 * y + c) * r, r), :]
            local.append(pltpu.make_async_copy(src_refs[j], dst, local_sems.at[j]))
            for k in range(1, N_DEV):
                remote.append(pltpu.make_async_remote_copy(
                    src_ref=src_refs[j], dst_ref=dst, send_sem=send_sems.at[7 * j + k - 1],
                    recv_sem=recv_sems.at[7 * j + k - 1], device_id=_peer(k, x, y, c), device_id_type=MESH))
        return remote, local

    lands = [jax.ShapeDtypeStruct((N_DEV * a.shape[0], a.shape[1]), a.dtype) for a in shards]
    return _split_start(name, shards, lands, 7 * m, m, build, after)


def _slots_start(name, a):
    def build(src_refs, land_refs, send_sems, recv_sems, local_sems):
        x, y, c = lax.axis_index("x"), lax.axis_index("y"), lax.axis_index("c")
        dst = land_refs[0].at[4 * x + 2 * y + c]
        local = [pltpu.make_async_copy(src_refs[0], dst, local_sems.at[0])]
        remote = [pltpu.make_async_remote_copy(
            src_ref=src_refs[0], dst_ref=dst, send_sem=send_sems.at[k - 1], recv_sem=recv_sems.at[k - 1],
            device_id=_peer(k, x, y, c), device_id_type=MESH) for k in range(1, N_DEV)]
        return remote, local

    return _split_start(name, [a], [jax.ShapeDtypeStruct((N_DEV,) + a.shape, a.dtype)], 7, 1, build)


def _chips_start(name, p):
    _, r, c = p.shape
    nck = r // GRAD_ROW_TILE

    def build(src_refs, land_refs, send_sems, recv_sems):
        x, y, cc = lax.axis_index("x"), lax.axis_index("y"), lax.axis_index("c")
        remote = []
        for k in range(1, 4):
            px = 1 - x if k >> 1 else x
            py = 1 - y if k & 1 else y
            for j in range(nck):
                rows = pl.ds(j * GRAD_ROW_TILE, GRAD_ROW_TILE)
                remote.append(pltpu.make_async_remote_copy(
                    src_ref=src_refs[0].at[2 * px + py, rows], dst_ref=land_refs[0].at[k - 1, rows],
                    send_sem=send_sems.at[(k - 1) * nck + j], recv_sem=recv_sems.at[(k - 1) * nck + j],
                    device_id=(px, py, cc), device_id_type=MESH))
        return remote, []

    return _split_start(name, [p], [jax.ShapeDtypeStruct((3, r, c), p.dtype)], 3 * nck, 0, build)


def _chip_sum(name, p, recv, chip):
    _, r, c = p.shape
    tr = GRAD_ROW_TILE

    def body(chip_ref, p_ref, r_ref, o_ref):
        acc = p_ref[...].astype(F32)
        for k in range(3):
            acc = acc + r_ref[k].astype(F32)
        o_ref[...] = acc

    return pl.pallas_call(
        body, name=name,
        grid_spec=pltpu.PrefetchScalarGridSpec(
            num_scalar_prefetch=1, grid=(r // tr,),
            in_specs=[pl.BlockSpec((None, tr, c), lambda i, chip_ref: (chip_ref[0], i, 0)),
                      pl.BlockSpec((3, tr, c), lambda i, chip_ref: (0, i, 0))],
            out_specs=pl.BlockSpec((tr, c), lambda i, chip_ref: (i, 0))),
        out_shape=_out(r, c, F32), compiler_params=_params(("parallel",)),
    )(chip, p, recv)


def _local_step(x, mem, tgt, wt, sm, ev=None):
    t, d = x.shape
    n_mem = mem.shape[0]
    d_pool = sm["pool_scale"].shape[1]
    ng, pc = sm["pool_w"].shape[0], sm["pool_w"].shape[1]
    d_ssm = sm["ssm_d"].shape[1]
    _, sg, sp, sh = sm["ssm_b_re"].shape
    n_state = sg * sp
    gb, gs = {}, {}

    def emit(name, **kw):
        return ev(name, **kw) if ev is not None else None

    h1, ffn1_saved = _ffn_fwd("ffn1", x, sm["ffn1_norm"], wt["ffn1_w_gate"], wt["ffn1_w_up"], wt["ffn1_w_down"])
    emit("ffn1_fwd_done", marker=h1)
    u = _rms_fwd("mix_norm", h1, sm["mix_norm"])
    d_in = wt["w_in"].shape[0]
    tm, tn = _pick(t, 1024), _pick(d_in, 256)
    proj = _mm1("in_proj", "nt", u, wt["w_in"], t, d_in, tm, tn, F32)
    off_s = d_pool // d_ssm
    off_gp = (d_pool + d_ssm)
    off_gs = off_gp + d

    pool_w_bf = sm["pool_w"].astype(BF16)
    pooled, pm = _pool_fwd(proj, pool_w_bf, sm["pool_scale"])

    cols = [sm["ssm_a_re"].reshape(-1, 1), sm["ssm_a_im"].reshape(-1, 1),
            jnp.broadcast_to(sm["ssm_log_dt"][:, :, None], (2, sg, sp)).reshape(-1, 1),
            sm["ssm_b_re"].reshape(-1, sh), sm["ssm_b_im"].reshape(-1, sh)]
    abr, abi, bbr, bbi = _ssm_disc(cols)
    abr2, abi2 = abr.reshape(2, n_state), abi.reshape(2, n_state)
    bbr4, bbi4 = bbr.reshape(2, sg * sp, sh), bbi.reshape(2, sg * sp, sh)
    b_re = [_bd_in(bbr4[dr], sg, sp, sh).astype(BF16) for dr in range(2)]
    b_im = [_bd_in(bbi4[dr], sg, sp, sh).astype(BF16) for dr in range(2)]
    c_re = [_bd_out(sm["ssm_c_re"][dr], sg, sp, sh).astype(BF16) for dr in range(2)]
    c_im = [_bd_out(-sm["ssm_c_im"][dr], sg, sp, sh).astype(BF16) for dr in range(2)]
    tms = _pick(t, 512)
    s_bf = _ew("ssm_cast", lambda v: (v,), [proj[:, d_pool:d_pool + d_ssm]], [BF16])[0]
    xs = []
    for dr in range(2):
        u_d = _mm1(f"ssm_in{dr}", "nn", s_bf, jnp.concatenate([b_re[dr], b_im[dr]], axis=1), t, 2 * n_state, tms,
                   _pick(2 * n_state, 512), F32)
        xs.append(_scan(f"ssm_scan{dr}", u_d, abr2[dr:dr + 1], abi2[dr:dr + 1], reverse=(dr == 1), conj=False))
    tmy = _pick(t, 256)
    x_list = [xs[0][0], xs[0][1], xs[1][0], xs[1][1]]
    y = _mm("ssm_out", "nn", x_list, [c_re[0], c_im[0], c_re[1], c_im[1]], [[(k, k) for k in range(4)]], t, d_ssm, tmy,
            d_ssm, [(proj, _tile(tmy, d_ssm, off_s)), (sm["ssm_d"], _rowvec(d_ssm))],
            lambda accs, sv, dv: (sv * dv + accs[0],), [(_out(t, d_ssm, F32), None)])[0]
    ys = _ew("ssm_gelu", lambda v: (jax.nn.gelu(v),), [y], [BF16])[0]
    emit("mix_in_done", marker=ys)

    tmm, tnm = _pick(t, 512), _pick(d, 256)
    gp_spec = _tile(tmm, tnm, off_gp // tnm)
    gs_spec = _tile(tmm, tnm, off_gs // tnm)

    def merge_epi(accs, gpv, gsv):
        z_pool, val, gate = accs
        return (jax.nn.sigmoid(gpv) * z_pool + jax.nn.sigmoid(gsv) * (val * jax.nn.sigmoid(gate)),)

    merged = _mm("mix_merge", "nt", [pm, ys], [wt["w_pool_proj"], wt["w_glu_val"], wt["w_glu_gate"]],
                 [[(0, 0)], [(1, 1)], [(1, 2)]], t, d, tmm, tnm, [(proj, gp_spec), (proj, gs_spec)], merge_epi,
                 [(_out(t, d, BF16), None)])[0]
    res_epi = lambda accs, hin: (hin + accs[0],)
    h2 = _mm("mix_out", "nn", [merged], [wt["w_mix_out"]], [[(0, 0)]], t, d, tmm, tnm, [(h1, _tile(tmm, tnm))],
             res_epi, [(_out(t, d, F32), None)])[0]

    un = _rms_fwd("xattn_norm", h2, sm["xattn_norm"])
    mn = _rms_fwd("mem_norm", mem, sm["mem_norm"])
    q = _mm1("xattn_q", "nn", un, wt["w_q"], t, d, tmm, tnm, BF16)
    kv = _mm1("xattn_kv", "nt", mn, wt["w_kv"], n_mem, 2 * d, n_mem, _pick(2 * d, 512), BF16)
    o = _attn_fwd(q, kv)
    h3 = _mm("xattn_out", "nn", [o], [wt["w_xo"]], [[(0, 0)]], t, d, tmm, tnm, [(h2, _tile(tmm, tnm))],
             res_epi, [(_out(t, d, F32), None)])[0]

    h4, ffn2_saved = _ffn_fwd("ffn2", h3, sm["ffn2_norm"], wt["ffn2_w_gate"], wt["ffn2_w_up"], wt["ffn2_w_down"])

    dh4, dh4_bf, gs["final_norm"], loss = _loss_head(h4, sm["final_norm"], tgt)
    dh3, dh3_bf, gs["ffn2_norm"], gb["ffn2_w_gate"], gb["ffn2_w_up"], gb["ffn2_w_down"] = _ffn_bwd(
        "ffn2", h3, sm["ffn2_norm"], wt["ffn2_w_gate"], wt["ffn2_w_up"], wt["ffn2_w_down"], ffn2_saved, dh4, dh4_bf)

    tw = _pick(d, 256)
    do = _mm1("xattn_do", "nt", dh3_bf, wt["w_xo"], t, d, tmm, tnm, BF16)
    gb["w_xo"] = _mm1("xattn_dwxo", "tn", o, dh3_bf, d, d, tw, d, BF16)
    dq, dkv = _attn_bwd(q, kv, do)
    gb["w_q"] = _mm1("xattn_dwq", "tn", un, dq, d, d, tw, d, BF16)
    dun = _mm1("xattn_dun", "nt", dq, wt["w_q"], t, d, tmm, tnm, F32)
    dh2, dh2_bf, gs["xattn_norm"] = _rms_bwd("xattn_norm_bwd", h2, sm["xattn_norm"], dun, dh3)
    gb["w_kv"] = _mm1("xattn_dwkv", "tn", dkv, mn, 2 * d, d, _pick(2 * d, 512), d, BF16)
    dmn = _mm1("xattn_dmn", "nn", dkv, wt["w_kv"], n_mem, d, n_mem, tnm, F32)
    gs["mem_norm"] = _rms_bwd("mem_norm_bwd", mem, sm["mem_norm"], dmn)

    gb["w_mix_out"] = _mm1("mix_dwout", "tn", merged, dh2_bf, d, d, tw, d, BF16)

    def merge_bwd_epi(accs, gpv, gsv):
        dmerged, z_pool, val, gate = accs
        sp_, ss_, sg_ = jax.nn.sigmoid(gpv), jax.nn.sigmoid(gsv), jax.nn.sigmoid(gate)
        glu = val * sg_
        dz_pool = dmerged * sp_
        dg_pool = dmerged * z_pool * (sp_ * (1.0 - sp_))
        dz_ssm = dmerged * ss_
        dg_ssm = dmerged * glu * (ss_ * (1.0 - ss_))
        dval = dz_ssm * sg_
        dgate = dz_ssm * glu * (1.0 - sg_)
        return dz_pool, dg_pool, dg_ssm, dval, dgate

    dz_pool, dg_pool, dg_ssm, dval, dgate = _mm(
        "mix_merge_bwd", "nt", [dh2_bf, pm, ys], [wt["w_mix_out"], wt["w_pool_proj"], wt["w_glu_val"], wt["w_glu_gate"]],
        [[(0, 0)], [(1, 1)], [(2, 2)], [(2, 3)]], t, d, tmm, tnm, [(proj, gp_spec), (proj, gs_spec)], merge_bwd_epi,
        [(_out(t, d, BF16), None)] * 5)
    gb["w_pool_proj"] = _mm1("pool_dwproj", "tn", dz_pool, pm, d, d_pool, tw, d_pool, BF16)
    gb["w_glu_val"] = _mm1("glu_dwval", "tn", dval, ys, d, d_ssm, tw, d_ssm, BF16)
    gb["w_glu_gate"] = _mm1("glu_dwgate", "tn", dgate, ys, d, d_ssm, tw, d_ssm, BF16)

    def gelu_bwd_epi(accs, yv):
        _, vjp = jax.vjp(jax.nn.gelu, yv)
        return (vjp(accs[0])[0],)

    dy = _mm("glu_dy", "nn", [dval, dgate], [wt["w_glu_val"], wt["w_glu_gate"]], [[(0, 0), (1, 1)]], t, d_ssm, tmy, d_ssm,
             [(y, _tile(tmy, d_ssm))], gelu_bwd_epi, [(_out(t, d_ssm, F32), None)])[0]
    gs["ssm_d"] = _colsum_prod("ssm_dd", dy, proj, b_coff=off_s)
    dy_bf = _ew("ssm_dy_cast", lambda v: (v,), [dy], [BF16])[0]
    d_abr, d_abi, d_bbr, d_bbi, d_cre, d_cim, lams = [], [], [], [], [], [], []
    ts = _pick(n_state, 512)
    tc_ = _pick(n_state, 256)
    for dr in range(2):
        gx = _mm1(f"ssm_gx{dr}", "nt", dy_bf, jnp.concatenate([c_re[dr], c_im[dr]], axis=0), t, 2 * n_state, tms,
                  _pick(2 * n_state, 512), F32)
        lr, li = _scan(f"ssm_adj{dr}", gx, abr2[dr:dr + 1], abi2[dr:dr + 1], reverse=(dr == 0), conj=True)
        dar, dai = _ssm_da(f"ssm_da{dr}", lr, li, xs[dr][0], xs[dr][1], reverse=(dr == 1))
        d_abr.append(dar)
        d_abi.append(dai)
        lams += [lr, li]
        d_bbr.append(_diag_in(_mm1(f"ssm_dbre{dr}", "tn", s_bf, lr, d_ssm, n_state, d_ssm, ts, F32), sg, sp, sh))
        d_bbi.append(_diag_in(_mm1(f"ssm_dbim{dr}", "tn", s_bf, li, d_ssm, n_state, d_ssm, ts, F32), sg, sp, sh))
        d_cre.append(_diag_out(_mm1(f"ssm_dcre{dr}", "tn", xs[dr][0], dy_bf, n_state, d_ssm, tc_, d_ssm, F32), sg, sp, sh))
        d_cim.append(-_diag_out(_mm1(f"ssm_dcim{dr}", "tn", xs[dr][1], dy_bf, n_state, d_ssm, tc_, d_ssm, F32), sg, sp, sh))
    ds = _mm("ssm_ds", "nt", lams, [b_re[0], b_im[0], b_re[1], b_im[1]], [[(k, k) for k in range(4)]], t, d_ssm, tmy,
             d_ssm, [(dy, _tile(tmy, d_ssm)), (sm["ssm_d"], _rowvec(d_ssm))],
             lambda accs, dyv, dv: (dyv * dv + accs[0],), [(_out(t, d_ssm, BF16), None)])[0]
    cots = [jnp.concatenate(d_abr, axis=0).reshape(-1, 1), jnp.concatenate(d_abi, axis=0).reshape(-1, 1),
            jnp.concatenate(d_bbr, axis=0), jnp.concatenate(d_bbi, axis=0)]
    d_are, d_aim, d_ldt, d_bre, d_bim = _ssm_disc_bwd(cols, cots)
    gs["ssm_a_re"] = d_are.reshape(2, sg, sp)
    gs["ssm_a_im"] = d_aim.reshape(2, sg, sp)
    gs["ssm_log_dt"] = _rowsum("ssm_dlogdt", d_ldt.reshape(2 * sg, sp)).reshape(2, sg)
    gs["ssm_b_re"] = d_bre.reshape(2, sg, sp, sh)
    gs["ssm_b_im"] = d_bim.reshape(2, sg, sp, sh)
    gs["ssm_c_re"] = jnp.stack(d_cre, axis=0)
    gs["ssm_c_im"] = jnp.stack(d_cim, axis=0)

    dpm = _mm1("pool_dpm", "nn", dz_pool, wt["w_pool_proj"], t, d_pool, tmm, _pick(d_pool, 256), F32)
    dp, gs["pool_w"], gs["pool_scale"] = _pool_bwd(pooled, dpm, pool_w_bf, sm["pool_scale"])

    w_in = wt["w_in"]
    parts = [(dp, 0, d_pool), (ds, d_pool, d_ssm), (dg_pool, off_gp, d), (dg_ssm, off_gs, d)]
    w_in_parts = [w_in[o0:o0 + width] for _, o0, width in parts]
    gb["w_in"] = jnp.concatenate(
        [_mm1(f"in_proj_dw{k}", "tn", p_[0], u, p_[2], d, _pick(p_[2], 256), d, BF16) for k, p_ in enumerate(parts)], axis=0)
    pin = emit("grads_main", gb=gb)
    du = _mm("in_proj_du", "nn", [p_[0] for p_ in parts], w_in_parts, [[(k, k) for k in range(4)]], t, d, tmm, tnm, [],
             lambda accs: (accs[0],), [(_out(t, d, F32), None)], after=pin)[0]
    dh1, dh1_bf, gs["mix_norm"] = _rms_bwd("mix_norm_bwd", h1, sm["mix_norm"], du, dh2)
    pin = emit("small_early", gs=gs, loss=loss)

    def ffn1_weights_done(d_wg, d_wu, d_wd):
        gb["ffn1_w_gate"], gb["ffn1_w_up"], gb["ffn1_w_down"] = d_wg, d_wu, d_wd
        return emit("grads_ffn1", gb=gb)

    dx, _, gs["ffn1_norm"], _, _, _ = _ffn_bwd(
        "ffn1", x, sm["ffn1_norm"], wt["ffn1_w_gate"], wt["ffn1_w_up"], wt["ffn1_w_down"], ffn1_saved, dh1, dh1_bf,
        weights_done=ffn1_weights_done, after=pin)
    return loss, dx, gb, gs


WEIGHTS = ["ffn1_norm", "ffn1_w_gate", "ffn1_w_up", "ffn1_w_down", "mix_norm", "w_in", "pool_w", "pool_scale",
           "w_pool_proj", "ssm_a_re", "ssm_a_im", "ssm_log_dt", "ssm_b_re", "ssm_b_im", "ssm_c_re", "ssm_c_im", "ssm_d",
           "w_glu_val", "w_glu_gate", "w_mix_out", "xattn_norm", "mem_norm", "w_q", "w_kv", "w_xo", "ffn2_norm",
           "ffn2_w_gate", "ffn2_w_up", "ffn2_w_down", "final_norm"]
COL_SHARDED = ["ffn1_w_gate", "ffn1_w_up", "w_in", "w_pool_proj", "w_glu_val", "w_glu_gate", "w_kv", "ffn2_w_gate",
               "ffn2_w_up"]
ROW_SHARDED = ["ffn1_w_down", "w_mix_out", "w_q", "w_xo", "ffn2_w_down"]
BIG = [n for n in WEIGHTS if n in COL_SHARDED or n in ROW_SHARDED]
SMALL = [n for n in WEIGHTS if n not in BIG]
FFN1_BIG = ["ffn1_w_gate", "ffn1_w_up", "ffn1_w_down"]
MAIN_BIG = [n for n in BIG if n not in FFN1_BIG]
LATE_SMALL = "ffn1_norm"
EARLY_SMALL = [n for n in SMALL if n != LATE_SMALL]
PACK_ROWS = SUBLANES * LANES
GRAD_ROW_TILE = 256


def _to_rows(name, w, width):
    if name in COL_SHARDED:
        w = w.T
    return w.reshape(-1, width)


def _from_rows(name, rows, shard_shape):
    if name in COL_SHARDED:
        return rows.reshape(shard_shape[1], shard_shape[0]).T
    return rows.reshape(shard_shape)


def _pack_small(vals):
    flat = []
    for v in vals:
        f = v.reshape(-1)
        flat.append(jnp.pad(f, (0, (-f.shape[0]) % PACK_ROWS)))
    total = sum(f.shape[0] for f in flat)
    flat.append(jnp.zeros(((-total) % (GRAD_ROW_TILE * LANES),), F32))
    return jnp.concatenate(flat).reshape(-1, LANES)


def _unpack_small(packed, shapes):
    out, row = [], 0
    for shp in shapes:
        size = math.prod(shp)
        rows = -(-size // PACK_ROWS) * SUBLANES
        out.append(packed[row:row + rows].reshape(-1)[:size].reshape(shp))
        row += rows
    return out


def kernel(x, mem, ffn1_norm, ffn1_w_gate, ffn1_w_up, ffn1_w_down, mix_norm, w_in, pool_w, pool_scale, w_pool_proj, ssm_a_re, ssm_a_im, ssm_log_dt, ssm_b_re, ssm_b_im, ssm_c_re, ssm_c_im, ssm_d, w_glu_val, w_glu_gate, w_mix_out, xattn_norm, mem_norm, w_q, w_kv, w_xo, ffn2_norm, ffn2_w_gate, ffn2_w_up, ffn2_w_down, final_norm, loss_target, m_ffn1_norm, m_ffn1_w_gate, m_ffn1_w_up, m_ffn1_w_down, m_mix_norm, m_w_in, m_pool_w, m_pool_scale, m_w_pool_proj, m_ssm_a_re, m_ssm_a_im, m_ssm_log_dt, m_ssm_b_re, m_ssm_b_im, m_ssm_c_re, m_ssm_c_im, m_ssm_d, m_w_glu_val, m_w_glu_gate, m_w_mix_out, m_xattn_norm, m_mem_norm, m_w_q, m_w_kv, m_w_xo, m_ffn2_norm, m_ffn2_w_gate, m_ffn2_w_up, m_ffn2_w_down, m_final_norm, v_ffn1_norm, v_ffn1_w_gate, v_ffn1_w_up, v_ffn1_w_down, v_mix_norm, v_w_in, v_pool_w, v_pool_scale, v_w_pool_proj, v_ssm_a_re, v_ssm_a_im, v_ssm_log_dt, v_ssm_b_re, v_ssm_b_im, v_ssm_c_re, v_ssm_c_im, v_ssm_d, v_w_glu_val, v_w_glu_gate, v_w_mix_out, v_xattn_norm, v_mem_norm, v_w_q, v_w_kv, v_w_xo, v_ffn2_norm, v_ffn2_w_gate, v_ffn2_w_up, v_ffn2_w_down, v_final_norm):
    given = dict(locals())
    wts = {n: given[n] for n in WEIGHTS}
    moms = {n: (given["m_" + n], given["v_" + n]) for n in WEIGHTS}
    x2, mem2, tgt2 = x[0], mem[0], loss_target[0]
    d = x2.shape[1]
    core = lax.axis_index("c").astype(jnp.int32).reshape(1)
    chip = (2 * lax.axis_index("x") + lax.axis_index("y")).astype(jnp.int32).reshape(1)

    def full_form(n, f):
        shard = wts[n][0].shape
        return f.reshape(N_DEV * shard[1], shard[0]) if n in COL_SHARDED else f.reshape(N_DEV * shard[0], shard[1])

    shards = {n: _to_rows(n, wts[n][0], d).astype(BF16) for n in BIG}
    wt = {n: full_form(n, f) for n, f in zip(FFN1_BIG, _allgather("weight_allgather_ffn1", [shards[n] for n in FFN1_BIG]))}
    rest = [n for n in MAIN_BIG if n != "w_in"]
    gather_in = _gather_start("weight_gather_in_start", [shards["w_in"]], wt[FFN1_BIG[0]])
    gather_rest = _gather_start("weight_gather_rest_start", [shards[n] for n in rest], gather_in["token"])
    sm = {n: (wts[n].reshape(1, -1) if wts[n].ndim <= 2 else wts[n][0]) for n in SMALL}
    sm["ffn1_norm"] = sm["ffn1_norm"] + (gather_in["token"][0, 0] + gather_rest["token"][0, 0])

    pending = {}

    def reduce_start(tag, names, gb):
        blocks = [gb[n].reshape(N_DEV, -1, d) for n in names]
        pad_rows = (-sum(b.shape[1] for b in blocks)) % GRAD_ROW_TILE
        packed = jnp.concatenate(blocks + ([jnp.zeros((N_DEV, pad_rows, d), BF16)] if pad_rows else []), axis=1)
        pair = _pair_sum("grad_pair_sum_" + tag, packed, _exchange_cores("grad_exchange_cores_" + tag, packed), core)
        pending[tag] = (pair, _chips_start("grad_exchange_chips_start_" + tag, pair), [b.shape[1] for b in blocks])
        return pending[tag][1]["token"]

    def reduce_finish(tag, after):
        pair, started, rows = pending[tag]
        recv = _split_wait("grad_exchange_chips_wait_" + tag, started, after)[0]
        return _chip_sum("grad_chip_sum_" + tag, pair, recv, chip), rows

    def ev(name, gb=None, gs=None, loss=None, marker=None):
        if name == "ffn1_fwd_done":
            wt["w_in"] = full_form("w_in", _split_wait("weight_gather_in_wait", gather_in, marker)[0])
        elif name == "mix_in_done":
            for n, f in zip(rest, _split_wait("weight_gather_rest_wait", gather_rest, marker)):
                wt[n] = full_form(n, f)
        elif name == "grads_main":
            return reduce_start("main", MAIN_BIG, gb)
        elif name == "small_early":
            pending["small"] = _slots_start("small_gather_start", _pack_small([gs[n] for n in EARLY_SMALL] + [loss[:, :1]]))
            return pending["small"]["token"]
        elif name == "grads_ffn1":
            return reduce_start("ffn1", FFN1_BIG, gb)
        return None

    _, dx, _, gs = _local_step(x2, mem2, tgt2, wt, sm, ev)

    out_g, out_d, out_m, out_v = {}, {}, {}, {}

    def update_big(names, g_rows, rows):
        off = 0
        for n, r in zip(names, rows):
            shard = wts[n].shape
            g_full = _from_rows(n, g_rows[off:off + r], shard[1:]).reshape(shard)
            off += r
            two_d = (-1, shard[-1])
            dl, m2, v2 = _adamw("adamw_" + n, wts[n].reshape(two_d), g_full.reshape(two_d), moms[n][0].reshape(two_d),
                                moms[n][1].reshape(two_d))
            out_g[n], out_d[n], out_m[n], out_v[n] = g_full, dl.reshape(shard), m2.reshape(shard), v2.reshape(shard)
        return dl

    last = update_big(MAIN_BIG, *reduce_finish("main", dx))

    small_sum = _sum_slots("small_sum", _split_wait("small_gather_wait", pending["small"], dx)[0], F32)
    late = _allgather("small_allgather_late", [gs[LATE_SMALL].reshape(-1, LANES)])[0]
    late_sum = _sum_slots("small_sum_late", late.reshape(N_DEV, -1, LANES), F32)
    zero = jnp.zeros((1, 1), F32)
    shapes = [wts[n].shape for n in EARLY_SMALL] + [(1, 1)]
    packs = [_pack_small([src[n] for n in EARLY_SMALL] + [zero])
             for src in (wts, {n: moms[n][0] for n in SMALL}, {n: moms[n][1] for n in SMALL})]
    dl, m2, v2 = _adamw("adamw_small", packs[0], small_sum, packs[1], packs[2])
    for dst, src in ((out_g, small_sum), (out_d, dl), (out_m, m2), (out_v, v2)):
        vals = _unpack_small(src, shapes)
        for n, val in zip(EARLY_SMALL, vals):
            dst[n] = val
        if dst is out_g:
            total_loss = vals[-1].reshape(())
    shp = wts[LATE_SMALL].shape
    dl, m2, v2 = _adamw("adamw_" + LATE_SMALL, wts[LATE_SMALL].reshape(-1, LANES), late_sum,
                        moms[LATE_SMALL][0].reshape(-1, LANES), moms[LATE_SMALL][1].reshape(-1, LANES))
    for dst, src in ((out_g, late_sum), (out_d, dl), (out_m, m2), (out_v, v2)):
        dst[LATE_SMALL] = src.reshape(shp)

    update_big(FFN1_BIG, *reduce_finish("ffn1", last))

    return (total_loss, dx[None], *[out_g[n] for n in WEIGHTS], *[out_d[n] for n in WEIGHTS],
            *[out_m[n] for n in WEIGHTS], *[out_v[n] for n in WEIGHTS])
```

```python
import functools
import math

import jax
import jax.numpy as jnp
from jax import lax
from jax.experimental import pallas as pl
from jax.experimental.pallas import tpu as pltpu

F32 = jnp.float32
BF16 = jnp.bfloat16
EPS = 1e-6
N_XHEADS = 4
POOL_WINDOWS = (2, 4, 8, 16)
ADAM_LR = 0.001
ADAM_B1 = 0.9
ADAM_B2 = 0.999
ADAM_EPS = 1e-08
ADAM_WD = 0.01
ADAM_STEP = 10
N_DEV = 8
VMEM_LIMIT_V7X = 48 * 1024 * 1024
LANES = 128
SUBLANES = 8
SUB_ROWS = 256
POOL_PAD = 16
MESH = pl.DeviceIdType.MESH
ANY = pl.BlockSpec(memory_space=pl.ANY)
HBM = pl.BlockSpec(memory_space=pltpu.HBM)
SEM = pl.BlockSpec(memory_space=pltpu.SEMAPHORE)
SIDE_EFFECT = pltpu.SideEffectType.DATAFLOW_SIDE_EFFECTING

_DIMS = {
    "nt": (((1,), (1,)), ((), ())),
    "nn": (((1,), (0,)), ((), ())),
    "tn": (((0,), (0,)), ((), ())),
}


def _pick(dim, pref, mult=LANES):
    if dim <= pref:
        return dim
    for t in range(pref - pref % mult, 0, -mult):
        if dim % t == 0:
            return t
    return dim


def _params(sem):
    return pltpu.CompilerParams(dimension_semantics=sem, vmem_limit_bytes=VMEM_LIMIT_V7X)


def _tile(tm, tn, coff=0):
    return pl.BlockSpec((tm, tn), lambda i, j: (i, j + coff))


def _rowvec(tn, coff=0):
    return pl.BlockSpec((1, tn), lambda i, j: (0, j + coff))


def _out(m, n, dtype):
    return jax.ShapeDtypeStruct((m, n), dtype)


def _mm(name, form, a_list, b_list, groups, m, n, tm, tn, extras, epilogue, outs, after=None, sub=SUB_ROWS):
    na, nb, ne = len(a_list), len(b_list), len(extras)
    pins = [] if after is None else [after]
    step = tm if (sub is None or form == "tn" or tm % sub) else sub

    def a_spec(a):
        if form == "tn":
            return pl.BlockSpec((a.shape[0], tm), lambda i, j: (0, i))
        return pl.BlockSpec((tm, a.shape[1]), lambda i, j: (i, 0))

    def b_spec(b):
        if form == "nt":
            return pl.BlockSpec((tn, b.shape[1]), lambda i, j: (j, 0))
        return pl.BlockSpec((b.shape[0], tn), lambda i, j: (0, j))

    def body(*refs):
        a_refs, b_refs = refs[:na], refs[na:na + nb]
        e_refs, o_refs = refs[na + nb:na + nb + ne], refs[na + nb + ne + len(pins):]
        b_vals = {}
        for s0 in range(0, tm, step):
            rows = slice(None) if step == tm else pl.ds(s0, step)
            a_vals, accs = {}, []
            for group in groups:
                acc = None
                for ai, bi in group:
                    if ai not in a_vals:
                        a_vals[ai] = (a_refs[ai][...] if form == "tn" else a_refs[ai][rows, :]).astype(BF16)
                    if bi not in b_vals:
                        b_vals[bi] = b_refs[bi][...].astype(BF16)
                    d = lax.dot_general(a_vals[ai], b_vals[bi], _DIMS[form], preferred_element_type=F32)
                    acc = d if acc is None else acc + d
                accs.append(acc)
            res = epilogue(accs, *[e[rows, :] if e.shape[0] == tm else e[...] for e in e_refs])
            for o_ref, r in zip(o_refs, res):
                o_ref[rows, :] = r.astype(o_ref.dtype)

    out_specs = [_tile(tm, tn) if s is None else s for _, s in outs]
    res = pl.pallas_call(
        body, name=name, grid=(m // tm, n // tn),
        in_specs=[a_spec(a) for a in a_list] + [b_spec(b) for b in b_list] + [s for _, s in extras] + [ANY] * len(pins),
        out_specs=out_specs, out_shape=[o for o, _ in outs],
        compiler_params=_params(("parallel", "parallel")),
    )(*a_list, *b_list, *[e for e, _ in extras], *pins)
    return res


def _mm1(name, form, a, b, m, n, tm, tn, dtype, scale=None):
    epi = (lambda accs: (accs[0],)) if scale is None else (lambda accs: (accs[0] * scale,))
    return _mm(name, form, [a], [b], [[(0, 0)]], m, n, tm, tn, [], epi, [(_out(m, n, dtype), None)])[0]


def _rms_fwd(name, h, g):
    t, d = h.shape
    tm = _pick(t, 512, SUBLANES)

    def body(h_ref, g_ref, n_ref):
        hv = h_ref[...]
        r = lax.rsqrt(jnp.mean(hv * hv, axis=-1, keepdims=True) + EPS)
        n_ref[...] = ((hv * r) * g_ref[...]).astype(BF16)

    return pl.pallas_call(
        body, name=name, grid=(t // tm,),
        in_specs=[pl.BlockSpec((tm, d), lambda i: (i, 0)), pl.BlockSpec((1, d), lambda i: (0, 0))],
        out_specs=pl.BlockSpec((tm, d), lambda i: (i, 0)), out_shape=_out(t, d, BF16),
        compiler_params=_params(("parallel",)),
    )(h, g)


def _rms_bwd(name, h, g, dn, dres=None):
    t, d = h.shape
    tm = _pick(t, 256, SUBLANES)
    need_dh = dres is not None

    def body(*refs):
        if need_dh:
            h_ref, g_ref, dn_ref, dres_ref, dh_ref, dhb_ref, dg_ref = refs
        else:
            h_ref, g_ref, dn_ref, dg_ref = refs
        hv = h_ref[...]
        r = lax.rsqrt(jnp.mean(hv * hv, axis=-1, keepdims=True) + EPS)
        nh = hv * r
        dnv = dn_ref[...].astype(F32)

        @pl.when(pl.program_id(0) == 0)
        def _():
            dg_ref[...] = jnp.zeros_like(dg_ref)

        dg_ref[...] += jnp.sum(dnv * nh, axis=0, keepdims=True)
        if need_dh:
            dng = dnv * g_ref[...]
            dh = dres_ref[...] + r * (dng - nh * jnp.mean(dng * nh, axis=-1, keepdims=True))
            dh_ref[...] = dh
            dhb_ref[...] = dh.astype(BF16)

    row = pl.BlockSpec((tm, d), lambda i: (i, 0))
    vec = pl.BlockSpec((1, d), lambda i: (0, 0))
    if need_dh:
        return pl.pallas_call(
            body, name=name, grid=(t // tm,), in_specs=[row, vec, row, row], out_specs=[row, row, vec],
            out_shape=[_out(t, d, F32), _out(t, d, BF16), _out(1, d, F32)], compiler_params=_params(("arbitrary",)),
        )(h, g, dn, dres)
    return pl.pallas_call(
        body, name=name, grid=(t // tm,), in_specs=[row, vec, row], out_specs=vec,
        out_shape=_out(1, d, F32), compiler_params=_params(("arbitrary",)),
    )(h, g, dn)


def _loss_head(h, g, tgt):
    t, d = h.shape
    tm = _pick(t, 256, SUBLANES)

    def body(h_ref, g_ref, t_ref, dh_ref, dhb_ref, dg_ref, loss_ref):
        hv = h_ref[...]
        r = lax.rsqrt(jnp.mean(hv * hv, axis=-1, keepdims=True) + EPS)
        nh = hv * r
        err = nh * g_ref[...] - t_ref[...]

        @pl.when(pl.program_id(0) == 0)
        def _():
            dg_ref[...] = jnp.zeros_like(dg_ref)
            loss_ref[...] = jnp.zeros_like(loss_ref)

        per_row = jnp.mean(err * err, axis=-1, keepdims=True)
        loss_ref[...] += 0.5 * jnp.sum(per_row, axis=0, keepdims=True)
        dy = err * (1.0 / d)
        dg_ref[...] += jnp.sum(dy * nh, axis=0, keepdims=True)
        dng = dy * g_ref[...]
        dh = r * (dng - nh * jnp.mean(dng * nh, axis=-1, keepdims=True))
        dh_ref[...] = dh
        dhb_ref[...] = dh.astype(BF16)

    row = pl.BlockSpec((tm, d), lambda i: (i, 0))
    vec = pl.BlockSpec((1, d), lambda i: (0, 0))
    return pl.pallas_call(
        body, name="loss_head", grid=(t // tm,), in_specs=[row, vec, row],
        out_specs=[row, row, vec, pl.BlockSpec((1, LANES), lambda i: (0, 0))],
        out_shape=[_out(t, d, F32), _out(t, d, BF16), _out(1, d, F32), _out(1, LANES, F32)],
        compiler_params=_params(("arbitrary",)),
    )(h, g, tgt)


def _ffn_fwd(tag, h, g, wg_t, wu_t, wd):
    t, d = h.shape
    f = wd.shape[0]
    n = _rms_fwd(tag + "_norm", h, g)
    tm, tn = _pick(t, 1024), _pick(f, 1408)

    def up_epi(accs):
        a, b = accs
        return a, b, (a * jax.nn.sigmoid(a)) * b

    a, b, hid = _mm(tag + "_up", "nt", [n], [wg_t, wu_t], [[(0, 0)], [(0, 1)]], t, f, tm, tn, [], up_epi,
                    [(_out(t, f, BF16), None)] * 3)
    tm2, tn2 = _pick(t, 1024), _pick(d, 512)
    h_out = _mm(tag + "_down", "nn", [hid], [wd], [[(0, 0)]], t, d, tm2, tn2, [(h, _tile(tm2, tn2))],
                lambda accs, hin: (hin + 0.5 * accs[0],), [(_out(t, d, F32), None)])[0]
    return h_out, (n, a, b, hid)


def _ffn_bwd(tag, h, g, wg_t, wu_t, wd, saved, dh, dh_bf, weights_done=None, after=None):
    n, a, b, hid = saved
    t, d = h.shape
    f = wd.shape[0]
    tm, tn = _pick(t, 1024), _pick(f, 1408)

    def hid_epi(accs, av, bv):
        dhid = 0.5 * accs[0]
        av, bv = av.astype(F32), bv.astype(F32)
        sig = jax.nn.sigmoid(av)
        da = dhid * bv * (sig * (1.0 + av * (1.0 - sig)))
        db = dhid * (av * sig)
        return da, db

    da, db = _mm(tag + "_bwd_hid", "nt", [dh_bf], [wd], [[(0, 0)]], t, f, tm, tn,
                 [(a, _tile(tm, tn)), (b, _tile(tm, tn))], hid_epi, [(_out(t, f, BF16), None)] * 2, after=after)
    tw = _pick(f, 256)
    d_wd = _mm1(tag + "_dwd", "tn", hid, dh_bf, f, d, tw, d, BF16, scale=0.5)
    d_wg = _mm1(tag + "_dwg", "tn", da, n, f, d, tw, d, BF16)
    d_wu = _mm1(tag + "_dwu", "tn", db, n, f, d, tw, d, BF16)
    pin = weights_done(d_wg, d_wu, d_wd) if weights_done is not None else None
    tm2, tn2 = _pick(t, 1024), _pick(d, 512)
    dn = _mm(tag + "_dn", "nn", [da, db], [wg_t, wu_t], [[(0, 0), (1, 1)]], t, d, tm2, tn2, [],
             lambda accs: (accs[0],), [(_out(t, d, F32), None)], after=pin)[0]
    dh_in, dh_in_bf, dg = _rms_bwd(tag + "_norm_bwd", h, g, dn, dh)
    return dh_in, dh_in_bf, dg, d_wg, d_wu, d_wd


def _window_sum(win, offsets):
    n = win.shape[0]
    acc = None
    for j in offsets:
        term = win if j == 0 else pltpu.roll(win, (-j) % n, 0)
        acc = term if acc is None else acc + term
    return acc


def _pool_counts(r0, ch, c, left, right, t):
    pos = r0 + lax.broadcasted_iota(jnp.int32, (ch, c), 0)
    return (jnp.minimum(pos + right + 1, t) - jnp.maximum(pos - left, 0)).astype(F32)


def _pool_fwd(proj, pool_w_bf, pool_scale):
    t = proj.shape[0]
    ng, c, _ = pool_w_bf.shape
    ch = _pick(t, 256, SUBLANES)
    pad = POOL_PAD

    def body(p_ref, w_ref, s_ref, pooled_ref, pm_ref, buf):
        grp = pl.program_id(0)
        buf[pl.ds(0, pad), :] = jnp.zeros((pad, c), F32)
        buf[pl.ds(pad + t, pad), :] = jnp.zeros((pad, c), F32)

        def fill(ci, carry):
            r0 = pl.multiple_of(ci * ch, SUBLANES)
            buf[pl.ds(pl.multiple_of(r0 + pad, SUBLANES), ch), :] = p_ref[pl.ds(r0, ch), :]
            return carry

        lax.fori_loop(0, t // ch, fill, 0)
        for gi, w in enumerate(POOL_WINDOWS):
            left = w // 2
            right = w - 1 - left

            @pl.when(grp == gi)
            def _(left=left, right=right):
                def chunk(ci, carry):
                    r0 = pl.multiple_of(ci * ch, SUBLANES)
                    win = buf[pl.ds(r0, ch + 2 * pad), :]
                    s = _window_sum(win, range(-left, right + 1))[pad:pad + ch]
                    pooled = s / _pool_counts(r0, ch, c, left, right, t) - win[pad:pad + ch]
                    pooled_bf = pooled.astype(BF16)
                    mixed = jnp.dot(pooled_bf, w_ref[0], preferred_element_type=F32)
                    pooled_ref[pl.ds(r0, ch), :] = pooled_bf
                    pm_ref[pl.ds(r0, ch), :] = (mixed * s_ref[...]).astype(BF16)
                    return carry

                lax.fori_loop(0, t // ch, chunk, 0)

    col = pl.BlockSpec((t, c), lambda g: (0, g))
    return pl.pallas_call(
        body, name="pool_fwd", grid=(ng,),
        in_specs=[col, pl.BlockSpec((1, c, c), lambda g: (g, 0, 0)), pl.BlockSpec((1, c), lambda g: (0, g))],
        out_specs=[col, col], out_shape=[_out(t, ng * c, BF16), _out(t, ng * c, BF16)],
        scratch_shapes=[pltpu.VMEM((t + 2 * pad, c), F32)],
        compiler_params=_params(("parallel",)),
    )(proj, pool_w_bf, pool_scale)


def _pool_bwd(pooled, dpm, pool_w_bf, pool_scale):
    t = pooled.shape[0]
    ng, c, _ = pool_w_bf.shape
    ch = _pick(t, 256, SUBLANES)
    pad = POOL_PAD

    def body(pooled_ref, dpm_ref, w_ref, s_ref, dp_ref, dw_ref, ds_ref, buf, raw):
        grp = pl.program_id(0)
        buf[pl.ds(0, pad), :] = jnp.zeros((pad, c), F32)
        buf[pl.ds(pad + t, pad), :] = jnp.zeros((pad, c), F32)
        dw_ref[...] = jnp.zeros_like(dw_ref)
        ds_ref[...] = jnp.zeros_like(ds_ref)
        for gi, w in enumerate(POOL_WINDOWS):
            left = w // 2
            right = w - 1 - left

            @pl.when(grp == gi)
            def _(left=left, right=right):
                def first(ci, carry):
                    r0 = pl.multiple_of(ci * ch, SUBLANES)
                    pv = pooled_ref[pl.ds(r0, ch), :]
                    dpm_v = dpm_ref[pl.ds(r0, ch), :]
                    mixed = jnp.dot(pv, w_ref[0], preferred_element_type=F32)
                    ds_ref[...] += jnp.sum(dpm_v * mixed, axis=0, keepdims=True)
                    dmixed = (dpm_v * s_ref[...]).astype(BF16)
                    dw_ref[0] += lax.dot_general(pv, dmixed, _DIMS["tn"], preferred_element_type=F32)
                    dpooled = lax.dot_general(dmixed, w_ref[0], _DIMS["nt"], preferred_element_type=F32)
                    raw[pl.ds(r0, ch), :] = dpooled
                    buf[pl.ds(pl.multiple_of(r0 + pad, SUBLANES), ch), :] = (
                        dpooled / _pool_counts(r0, ch, c, left, right, t))
                    return carry

                lax.fori_loop(0, t // ch, first, 0)

                def second(ci, carry):
                    r0 = pl.multiple_of(ci * ch, SUBLANES)
                    win = buf[pl.ds(r0, ch + 2 * pad), :]
                    s = _window_sum(win, range(-right, left + 1))[pad:pad + ch]
                    dp_ref[pl.ds(r0, ch), :] = (s - raw[pl.ds(r0, ch), :]).astype(BF16)
                    return carry

                lax.fori_loop(0, t // ch, second, 0)

    col = pl.BlockSpec((t, c), lambda g: (0, g))
    return pl.pallas_call(
        body, name="pool_bwd", grid=(ng,),
        in_specs=[col, col, pl.BlockSpec((1, c, c), lambda g: (g, 0, 0)), pl.BlockSpec((1, c), lambda g: (0, g))],
        out_specs=[col, pl.BlockSpec((1, c, c), lambda g: (g, 0, 0)), pl.BlockSpec((1, c), lambda g: (0, g))],
        out_shape=[_out(t, ng * c, BF16), jax.ShapeDtypeStruct((ng, c, c), F32), _out(1, ng * c, F32)],
        scratch_shapes=[pltpu.VMEM((t + 2 * pad, c), F32), pltpu.VMEM((t, c), F32)],
        compiler_params=_params(("parallel",)),
    )(pooled, dpm, pool_w_bf, pool_scale)


def _discretise(a_re, a_im, log_dt, b_re, b_im):
    dt = jnp.exp(log_dt)
    mag = jnp.exp(dt * a_re)
    ang = dt * a_im
    abr = mag * jnp.cos(ang)
    abi = mag * jnp.sin(ang)
    den = a_re * a_re + a_im * a_im
    nr = abr - 1.0
    qr = (nr * a_re + abi * a_im) / den
    qi = (abi * a_re - nr * a_im) / den
    return abr, abi, qr * b_re - qi * b_im, qr * b_im + qi * b_re


def _ssm_disc(cols):
    n, hh = cols[3].shape

    def body(ar, ai, ld, br, bi, o1, o2, o3, o4):
        res = _discretise(ar[...], ai[...], ld[...], br[...], bi[...])
        for o, r in zip((o1, o2, o3, o4), res):
            o[...] = r

    return pl.pallas_call(
        body, name="ssm_disc",
        out_shape=[_out(n, 1, F32), _out(n, 1, F32), _out(n, hh, F32), _out(n, hh, F32)],
    )(*cols)


def _ssm_disc_bwd(cols, cots):
    n, hh = cols[3].shape

    def body(ar, ai, ld, br, bi, c1, c2, c3, c4, o1, o2, o3, o4, o5):
        _, vjp = jax.vjp(_discretise, ar[...], ai[...], ld[...], br[...], bi[...])
        res = vjp((c1[...], c2[...], c3[...], c4[...]))
        for o, r in zip((o1, o2, o3, o4, o5), res):
            o[...] = r

    return pl.pallas_call(
        body, name="ssm_disc_bwd",
        out_shape=[_out(n, 1, F32)] * 3 + [_out(n, hh, F32)] * 2,
    )(*cols, *cots)


def _rowsum(name, a):
    r, _ = a.shape

    def body(a_ref, o_ref):
        o_ref[...] = jnp.sum(a_ref[...], axis=-1, keepdims=True)

    return pl.pallas_call(body, name=name, out_shape=_out(r, 1, F32))(a)


def _cmul(pr, pi, qr, qi):
    return pr * qr - pi * qi, pr * qi + pi * qr


def _scan(name, u, ar, ai, reverse, conj):
    t, s2 = u.shape
    s = s2 // 2
    w = _pick(s, 512)
    tc = _pick(t, 512, SUBLANES)
    n_t, n_w = t // tc, s // w
    groups = tc // SUBLANES
    last = 0 if reverse else SUBLANES - 1

    def body(ar_ref, ai_ref, ur_ref, ui_ref, xr_ref, xi_ref, cr_ref, ci_ref):
        @pl.when(pl.program_id(1) == 0)
        def _():
            cr_ref[...] = jnp.zeros_like(cr_ref)
            ci_ref[...] = jnp.zeros_like(ci_ref)

        a1r = ar_ref[...]
        a1i = -ai_ref[...] if conj else ai_ref[...]
        a2r, a2i = _cmul(a1r, a1i, a1r, a1i)
        a4r, a4i = _cmul(a2r, a2i, a2r, a2i)
        row = lax.broadcasted_iota(jnp.int32, (SUBLANES, w), 0)
        pwr = jnp.zeros((SUBLANES, w), F32)
        pwi = jnp.zeros((SUBLANES, w), F32)
        cur_r, cur_i = a1r, a1i
        for k in range(SUBLANES):
            rk = SUBLANES - 1 - k if reverse else k
            pwr = jnp.where(row == rk, cur_r, pwr)
            pwi = jnp.where(row == rk, cur_i, pwi)
            cur_r, cur_i = _cmul(cur_r, cur_i, a1r, a1i)
        steps = ((1, a1r, a1i), (2, a2r, a2i), (4, a4r, a4i))

        def one(i, carry):
            g = groups - 1 - i if reverse else i
            r0 = pl.multiple_of(g * SUBLANES, SUBLANES)
            br = ur_ref[pl.ds(r0, SUBLANES), :]
            bi = ui_ref[pl.ds(r0, SUBLANES), :]
            for dist, pr, pi in steps:
                if reverse:
                    keep = row < SUBLANES - dist
                    shift = SUBLANES - dist
                else:
                    keep = row >= dist
                    shift = dist
                sr = jnp.where(keep, pltpu.roll(br, shift, 0), 0.0)
                si = jnp.where(keep, pltpu.roll(bi, shift, 0), 0.0)
                br, bi = br + pr * sr - pi * si, bi + pr * si + pi * sr
            cr = cr_ref[pl.ds(last, 1), :]
            ci = ci_ref[pl.ds(last, 1), :]
            xr = br + pwr * cr - pwi * ci
            xi = bi + pwr * ci + pwi * cr
            xr_ref[pl.ds(r0, SUBLANES), :] = xr
            xi_ref[pl.ds(r0, SUBLANES), :] = xi
            cr_ref[...] = xr
            ci_ref[...] = xi
            return carry

        lax.fori_loop(0, groups, one, 0)

    def tmap(k):
        return n_t - 1 - k if reverse else k

    re_blk = pl.BlockSpec((tc, w), lambda cb, k: (tmap(k), cb))
    im_blk = pl.BlockSpec((tc, w), lambda cb, k: (tmap(k), cb + n_w))
    a_blk = pl.BlockSpec((1, w), lambda cb, k: (0, cb))
    xr, xi = pl.pallas_call(
        body, name=name, grid=(n_w, n_t), in_specs=[a_blk, a_blk, re_blk, im_blk],
        out_specs=[pl.BlockSpec((tc, w), lambda cb, k: (tmap(k), cb))] * 2,
        out_shape=[_out(t, s, F32), _out(t, s, F32)],
        scratch_shapes=[pltpu.VMEM((SUBLANES, w), F32), pltpu.VMEM((SUBLANES, w), F32)],
        compiler_params=_params(("parallel", "arbitrary")),
    )(ar, ai, u, u)
    return xr, xi


def _ssm_da(name, lr, li, xr, xi, reverse):
    t, s = xr.shape
    w = _pick(s, 512)
    tc = _pick(t, 256, SUBLANES)
    n_t, n_w = t // tc, s // w

    def body(lr_ref, li_ref, xr_ref, xi_ref, dar_ref, dai_ref, pr_ref, pi_ref):
        @pl.when(pl.program_id(1) == 0)
        def _():
            pr_ref[...] = jnp.zeros_like(pr_ref)
            pi_ref[...] = jnp.zeros_like(pi_ref)
            dar_ref[...] = jnp.zeros_like(dar_ref)
            dai_ref[...] = jnp.zeros_like(dai_ref)

        row = lax.broadcasted_iota(jnp.int32, (tc, w), 0)
        xrv, xiv = xr_ref[...], xi_ref[...]
        if reverse:
            keep, shift, edge = row < tc - 1, tc - 1, 0
        else:
            keep, shift, edge = row >= 1, 1, tc - 1
        xsr = jnp.where(keep, pltpu.roll(xrv, shift, 0), pr_ref[pl.ds(0, 1), :])
        xsi = jnp.where(keep, pltpu.roll(xiv, shift, 0), pi_ref[pl.ds(0, 1), :])
        lrv, liv = lr_ref[...], li_ref[...]
        dar_ref[...] += jnp.sum(lrv * xsr + liv * xsi, axis=0, keepdims=True)
        dai_ref[...] += jnp.sum(liv * xsr - lrv * xsi, axis=0, keepdims=True)
        pr_ref[pl.ds(0, 1), :] = xr_ref[pl.ds(edge, 1), :]
        pi_ref[pl.ds(0, 1), :] = xi_ref[pl.ds(edge, 1), :]

    def tmap(k):
        return n_t - 1 - k if reverse else k

    blk = pl.BlockSpec((tc, w), lambda cb, k: (tmap(k), cb))
    vec = pl.BlockSpec((1, w), lambda cb, k: (0, cb))
    return pl.pallas_call(
        body, name=name, grid=(n_w, n_t), in_specs=[blk] * 4, out_specs=[vec, vec],
        out_shape=[_out(1, s, F32), _out(1, s, F32)],
        scratch_shapes=[pltpu.VMEM((SUBLANES, w), F32), pltpu.VMEM((SUBLANES, w), F32)],
        compiler_params=_params(("parallel", "arbitrary")),
    )(lr, li, xr, xi)


def _colsum_prod(name, a, b, b_coff=0):
    t, n = a.shape
    tm = _pick(t, 512, SUBLANES)

    def body(a_ref, b_ref, o_ref):
        @pl.when(pl.program_id(0) == 0)
        def _():
            o_ref[...] = jnp.zeros_like(o_ref)

        o_ref[...] += jnp.sum(a_ref[...].astype(F32) * b_ref[...].astype(F32), axis=0, keepdims=True)

    return pl.pallas_call(
        body, name=name, grid=(t // tm,),
        in_specs=[pl.BlockSpec((tm, n), lambda i: (i, 0)), pl.BlockSpec((tm, n), lambda i: (i, b_coff))],
        out_specs=pl.BlockSpec((1, n), lambda i: (0, 0)), out_shape=_out(1, n, F32),
        compiler_params=_params(("arbitrary",)),
    )(a, b)


def _bd_in(bb, g, p, hh):
    blk = bb.reshape(g, p, hh).transpose(0, 2, 1)
    eye = jnp.eye(g, dtype=bool)[:, None, :, None]
    return jnp.where(eye, blk[:, :, None, :], 0.0).reshape(g * hh, g * p)


def _bd_out(cc, g, p, hh):
    blk = cc.transpose(0, 2, 1)
    eye = jnp.eye(g, dtype=bool)[:, None, :, None]
    return jnp.where(eye, blk[:, :, None, :], 0.0).reshape(g * p, g * hh)


def _diag_in(dmat, g, p, hh):
    eye = jnp.eye(g, dtype=bool)[:, None, :, None]
    diag = jnp.sum(jnp.where(eye, dmat.reshape(g, hh, g, p), 0.0), axis=2)
    return diag.transpose(0, 2, 1).reshape(g * p, hh)


def _diag_out(dmat, g, p, hh):
    eye = jnp.eye(g, dtype=bool)[:, None, :, None]
    diag = jnp.sum(jnp.where(eye, dmat.reshape(g, p, g, hh), 0.0), axis=2)
    return diag.transpose(0, 2, 1)


def _softmax(qh, kh, scale):
    s = lax.dot_general(qh, kh, _DIMS["nt"], preferred_element_type=F32) * scale
    e = jnp.exp(s - jnp.max(s, axis=-1, keepdims=True))
    return e / jnp.sum(e, axis=-1, keepdims=True)


def _attn_fwd(q, kv):
    t, d = q.shape
    mm_ = kv.shape[0]
    hd = d // N_XHEADS
    scale = 1.0 / math.sqrt(hd)
    tm = _pick(t, 512, SUBLANES)

    def body(q_ref, kv_ref, o_ref):
        for h in range(N_XHEADS):
            sl = pl.ds(h * hd, hd)
            p = _softmax(q_ref[:, sl], kv_ref[:, sl], scale)
            o_ref[:, sl] = jnp.dot(p.astype(BF16), kv_ref[:, pl.ds(d + h * hd, hd)],
                                   preferred_element_type=F32).astype(BF16)

    return pl.pallas_call(
        body, name="attn_fwd", grid=(t // tm,),
        in_specs=[pl.BlockSpec((tm, d), lambda i: (i, 0)), pl.BlockSpec((mm_, 2 * d), lambda i: (0, 0))],
        out_specs=pl.BlockSpec((tm, d), lambda i: (i, 0)), out_shape=_out(t, d, BF16),
        compiler_params=_params(("parallel",)),
    )(q, kv)


def _attn_bwd(q, kv, do):
    t, d = q.shape
    mm_ = kv.shape[0]
    hd = d // N_XHEADS
    scale = 1.0 / math.sqrt(hd)
    tm = _pick(t, 512, SUBLANES)

    def body(q_ref, kv_ref, do_ref, dq_ref, dkv_ref):
        @pl.when(pl.program_id(0) == 0)
        def _():
            dkv_ref[...] = jnp.zeros_like(dkv_ref)

        for h in range(N_XHEADS):
            sl = pl.ds(h * hd, hd)
            vsl = pl.ds(d + h * hd, hd)
            qh, kh, doh = q_ref[:, sl], kv_ref[:, sl], do_ref[:, sl]
            p = _softmax(qh, kh, scale)
            dp = lax.dot_general(doh, kv_ref[:, vsl], _DIMS["nt"], preferred_element_type=F32)
            dkv_ref[:, vsl] += lax.dot_general(p.astype(BF16), doh, _DIMS["tn"], preferred_element_type=F32)
            ds = (p * (dp - jnp.sum(dp * p, axis=-1, keepdims=True)) * scale).astype(BF16)
            dq_ref[:, sl] = jnp.dot(ds, kh, preferred_element_type=F32).astype(BF16)
            dkv_ref[:, sl] += lax.dot_general(ds, qh, _DIMS["tn"], preferred_element_type=F32)

    row = pl.BlockSpec((tm, d), lambda i: (i, 0))
    full = pl.BlockSpec((mm_, 2 * d), lambda i: (0, 0))
    return pl.pallas_call(
        body, name="attn_bwd", grid=(t // tm,), in_specs=[row, full, row], out_specs=[row, full],
        out_shape=[_out(t, d, BF16), _out(mm_, 2 * d, F32)], compiler_params=_params(("arbitrary",)),
    )(q, kv, do)


def _ew(name, fn, ins, outs, rows_pref=256):
    r, c = ins[0].shape
    tr = _pick(r, rows_pref, SUBLANES)
    ni = len(ins)

    def body(*refs):
        res = fn(*[x[...] for x in refs[:ni]])
        for o_ref, v in zip(refs[ni:], res):
            o_ref[...] = v.astype(o_ref.dtype)

    blk = pl.BlockSpec((tr, c), lambda i: (i, 0))
    return pl.pallas_call(
        body, name=name, grid=(r // tr,), in_specs=[blk] * ni, out_specs=[blk] * len(outs),
        out_shape=[_out(r, c, dt) for dt in outs], compiler_params=_params(("parallel",)),
    )(*ins)


def _sum_slots(name, a, dtype):
    s, r, c = a.shape
    tr = _pick(r, 256, SUBLANES)

    def body(a_ref, o_ref):
        acc = a_ref[0].astype(F32)
        for k in range(1, s):
            acc = acc + a_ref[k].astype(F32)
        o_ref[...] = acc.astype(o_ref.dtype)

    return pl.pallas_call(
        body, name=name, grid=(r // tr,), in_specs=[pl.BlockSpec((s, tr, c), lambda i: (0, i, 0))],
        out_specs=pl.BlockSpec((tr, c), lambda i: (i, 0)), out_shape=_out(r, c, dtype),
        compiler_params=_params(("parallel",)),
    )(a)


def _adamw(name, w, g, m, v):
    bc1 = 1.0 - ADAM_B1 ** ADAM_STEP
    bc2 = 1.0 - ADAM_B2 ** ADAM_STEP

    def fn(wv, gv, mv, vv):
        m2 = ADAM_B1 * mv + (1.0 - ADAM_B1) * gv
        v2 = ADAM_B2 * vv + (1.0 - ADAM_B2) * (gv * gv)
        delta = -ADAM_LR * ((m2 / bc1) / (jnp.sqrt(v2 / bc2) + ADAM_EPS) + ADAM_WD * wv)
        return delta, m2, v2

    return _ew(name, fn, [w, g, m, v], [F32, F32, F32])


def _allgather(name, arrs):
    n = len(arrs)

    def body(*refs):
        ins, outs = refs[:n], refs[n:2 * n]
        send_sems, recv_sems, local_sems = refs[2 * n:]
        x, y, c = lax.axis_index("x"), lax.axis_index("y"), lax.axis_index("c")
        me, sibling = (x, y, c), (x, y, 1 - c)
        chips = [(1 - x, y), (x, 1 - y), (1 - x, 1 - y)]

        def rows(a, px, py, pc):
            r = ins[a].shape[0]
            return outs[a].at[pl.ds((4 * px + 2 * py + pc) * r, r), :]

        def copy(a, k, block, to, src=None):
            return pltpu.make_async_remote_copy(
                src_ref=rows(a, *block) if src is None else src, dst_ref=rows(a, *block),
                send_sem=send_sems.at[a, k], recv_sem=recv_sems.at[a, k], device_id=to, device_id_type=MESH)

        mine = [pltpu.make_async_copy(ins[a], rows(a, *me), local_sems.at[a]) for a in range(n)]
        for cp in mine:
            cp.start()
        first = []
        for a in range(n):
            first.append(copy(a, 0, me, sibling, src=ins[a]))
            first += [copy(a, 1 + j, me, (*chip, c), src=ins[a]) for j, chip in enumerate(chips)]
        for cp in first:
            cp.start()
        passed = []
        for j, chip in enumerate(chips):
            for a in range(n):
                copy(a, 1 + j, (*chip, c), me).wait_recv()
                cp = copy(a, 4 + j, (*chip, c), sibling)
                cp.start()
                passed.append(cp)
        for a in range(n):
            copy(a, 0, sibling, me).wait_recv()
            for j, chip in enumerate(chips):
                copy(a, 4 + j, (*chip, 1 - c), me).wait_recv()
        for cp in first + passed:
            cp.wait_send()
        for cp in mine:
            cp.wait()

    return pl.pallas_call(
        body, name=name, in_specs=[ANY] * n, out_specs=[ANY] * n,
        out_shape=[_out(N_DEV * a.shape[0], a.shape[1], a.dtype) for a in arrs],
        scratch_shapes=[pltpu.SemaphoreType.DMA((n, 7)), pltpu.SemaphoreType.DMA((n, 7)), pltpu.SemaphoreType.DMA((n,))],
    )(*arrs)


def _exchange_cores(name, g):
    _, r, c = g.shape
    nck = r // GRAD_ROW_TILE

    def body(g_ref, recv_ref, send_sems, recv_sems):
        x, y, cc = lax.axis_index("x"), lax.axis_index("y"), lax.axis_index("c")
        copies = []
        for q in range(4):
            for k in range(nck):
                rows = pl.ds(k * GRAD_ROW_TILE, GRAD_ROW_TILE)
                copies.append(pltpu.make_async_remote_copy(
                    src_ref=g_ref.at[2 * q + (1 - cc), rows], dst_ref=recv_ref.at[q, rows],
                    send_sem=send_sems.at[q, k], recv_sem=recv_sems.at[q, k], device_id=(x, y, 1 - cc),
                    device_id_type=MESH))
        for cp in copies:
            cp.start()
        for cp in copies:
            cp.wait()

    return pl.pallas_call(
        body, name=name, in_specs=[ANY], out_specs=ANY,
        out_shape=jax.ShapeDtypeStruct((4, r, c), g.dtype),
        scratch_shapes=[pltpu.SemaphoreType.DMA((4, nck)), pltpu.SemaphoreType.DMA((4, nck))],
    )(g)


def _pair_sum(name, g, recv, core):
    _, r, c = g.shape
    tr = GRAD_ROW_TILE

    def body(core_ref, g_ref, r_ref, o_ref):
        o_ref[...] = (g_ref[...].astype(F32) + r_ref[...].astype(F32)).astype(o_ref.dtype)

    blk = pl.BlockSpec((None, tr, c), lambda q, i, core_ref: (q, i, 0))
    return pl.pallas_call(
        body, name=name,
        grid_spec=pltpu.PrefetchScalarGridSpec(
            num_scalar_prefetch=1, grid=(4, r // tr),
            in_specs=[pl.BlockSpec((None, tr, c), lambda q, i, core_ref: (2 * q + core_ref[0], i, 0)), blk],
            out_specs=blk),
        out_shape=jax.ShapeDtypeStruct((4, r, c), g.dtype), compiler_params=_params(("parallel", "parallel")),
    )(core, g, recv)


def _peer(k, x, y, c):
    return (1 - x if k & 4 else x, 1 - y if k & 2 else y, 1 - c if k & 1 else c)


def _split_start(name, srcs, land_shapes, n_remote, n_local, build, after=None):
    ns, nl = len(srcs), len(land_shapes)
    n_sem = 3 if n_local else 2
    pins = [] if after is None else [after]

    def body(*refs):
        src_refs, land_refs = refs[:ns], refs[ns:ns + nl]
        sems = refs[ns + nl + len(pins):ns + nl + len(pins) + n_sem]
        token = refs[-1]
        remote, local = build(src_refs, land_refs, *sems)
        for cp in local + remote:
            cp.start()
        token[...] = jnp.zeros_like(token)

    sem_shapes = [pltpu.SemaphoreType.DMA((n_remote,)), pltpu.SemaphoreType.DMA((n_remote,))]
    if n_local:
        sem_shapes.append(pltpu.SemaphoreType.DMA((n_local,)))
    bufs = [pltpu.with_memory_space_constraint(a, pltpu.HBM) for a in srcs]
    bufs += [pltpu.with_memory_space_constraint(lax.empty(s.shape, s.dtype), pltpu.HBM) for s in land_shapes]
    outs = pl.pallas_call(
        body, name=name,
        out_shape=sem_shapes + [pltpu.HBM(b.shape, b.dtype) for b in bufs] + [jax.ShapeDtypeStruct((SUBLANES, LANES), F32)],
        in_specs=[HBM] * (ns + nl) + [ANY] * len(pins),
        out_specs=[SEM] * n_sem + [HBM] * (ns + nl) + [pl.BlockSpec(memory_space=pltpu.VMEM)],
        input_output_aliases={i: n_sem + i for i in range(ns + nl)},
        compiler_params=pltpu.CompilerParams(has_side_effects=SIDE_EFFECT),
    )(*bufs, *pins)
    return dict(sems=list(outs[:n_sem]), bufs=list(outs[n_sem:n_sem + ns + nl]), token=outs[-1], build=build, ns=ns)


def _split_wait(name, started, after):
    ns, n_buf, n_sem = started["ns"], len(started["bufs"]), len(started["sems"])

    def body(*refs):
        src_refs, land_refs = refs[:ns], refs[ns:n_buf]
        sems = refs[n_buf:n_buf + n_sem]
        remote, local = started["build"](src_refs, land_refs, *sems)
        for cp in local:
            cp.wait()
        for cp in remote:
            cp.wait_send()
            cp.wait_recv()

    outs = pl.pallas_call(
        body, name=name, out_shape=[pltpu.HBM(b.shape, b.dtype) for b in started["bufs"]],
        in_specs=[HBM] * n_buf + [SEM] * n_sem + [ANY], out_specs=[HBM] * n_buf,
        input_output_aliases={i: i for i in range(n_buf)},
        compiler_params=pltpu.CompilerParams(has_side_effects=SIDE_EFFECT),
    )(*started["bufs"], *started["sems"], after)
    return list(outs[ns:])


def _gather_start(name, shards, after):
    m = len(shards)

    def build(src_refs, land_refs, send_sems, recv_sems, local_sems):
        x, y, c = lax.axis_index("x"), lax.axis_index("y"), lax.axis_index("c")
        remote, local = [], []
        for j in range(m):
            r = src_refs[j].shape[0]
            dst = land_refs[j].at[pl.ds((4 * x + 2 * y + c) * r, r), :]
            local.append(pltpu.make_async_copy(src_refs[j], dst, local_sems.at[j]))
            for k in range(1, N_DEV):
                remote.append(pltpu.make_async_remote_copy(
                    src_ref=src_refs[j], dst_ref=dst, send_sem=send_sems.at[7 * j + k - 1],
                    recv_sem=recv_sems.at[7 * j + k - 1], device_id=_peer(k, x, y, c), device_id_type=MESH))
        return remote, local

    lands = [jax.ShapeDtypeStruct((N_DEV * a.shape[0], a.shape[1]), a.dtype) for a in shards]
    return _split_start(name, shards, lands, 7 * m, m, build, after)


def _slots_start(name, a):
    def build(src_refs, land_refs, send_sems, recv_sems, local_sems):
        x, y, c = lax.axis_index("x"), lax.axis_index("y"), lax.axis_index("c")
        dst = land_refs[0].at[4 * x + 2 * y + c]
        local = [pltpu.make_async_copy(src_refs[0], dst, local_sems.at[0])]
        remote = [pltpu.make_async_remote_copy(
            src_ref=src_refs[0], dst_ref=dst, send_sem=send_sems.at[k - 1], recv_sem=recv_sems.at[k - 1],
            device_id=_peer(k, x, y, c), device_id_type=MESH) for k in range(1, N_DEV)]
        return remote, local

    return _split_start(name, [a], [jax.ShapeDtypeStruct((N_DEV,) + a.shape, a.dtype)], 7, 1, build)


def _chips_start(name, p):
    _, r, c = p.shape
    nck = r // GRAD_ROW_TILE

    def build(src_refs, land_refs, send_sems, recv_sems):
        x, y, cc = lax.axis_index("x"), lax.axis_index("y"), lax.axis_index("c")
        remote = []
        for k in range(1, 4):
            px = 1 - x if k >> 1 else x
            py = 1 - y if k & 1 else y
            for j in range(nck):
                rows = pl.ds(j * GRAD_ROW_TILE, GRAD_ROW_TILE)
                remote.append(pltpu.make_async_remote_copy(
                    src_ref=src_refs[0].at[2 * px + py, rows], dst_ref=land_refs[0].at[k - 1, rows],
                    send_sem=send_sems.at[(k - 1) * nck + j], recv_sem=recv_sems.at[(k - 1) * nck + j],
                    device_id=(px, py, cc), device_id_type=MESH))
        return remote, []

    return _split_start(name, [p], [jax.ShapeDtypeStruct((3, r, c), p.dtype)], 3 * nck, 0, build)


def _chip_sum(name, p, recv, chip):
    _, r, c = p.shape
    tr = GRAD_ROW_TILE

    def body(chip_ref, p_ref, r_ref, o_ref):
        acc = p_ref[...].astype(F32)
        for k in range(3):
            acc = acc + r_ref[k].astype(F32)
        o_ref[...] = acc

    return pl.pallas_call(
        body, name=name,
        grid_spec=pltpu.PrefetchScalarGridSpec(
            num_scalar_prefetch=1, grid=(r // tr,),
            in_specs=[pl.BlockSpec((None, tr, c), lambda i, chip_ref: (chip_ref[0], i, 0)),
                      pl.BlockSpec((3, tr, c), lambda i, chip_ref: (0, i, 0))],
            out_specs=pl.BlockSpec((tr, c), lambda i, chip_ref: (i, 0))),
        out_shape=_out(r, c, F32), compiler_params=_params(("parallel",)),
    )(chip, p, recv)


def _local_step(x, mem, tgt, wt, sm, ev=None):
    t, d = x.shape
    n_mem = mem.shape[0]
    d_pool = sm["pool_scale"].shape[1]
    ng, pc = sm["pool_w"].shape[0], sm["pool_w"].shape[1]
    d_ssm = sm["ssm_d"].shape[1]
    _, sg, sp, sh = sm["ssm_b_re"].shape
    n_state = sg * sp
    gb, gs = {}, {}

    def emit(name, **kw):
        return ev(name, **kw) if ev is not None else None

    h1, ffn1_saved = _ffn_fwd("ffn1", x, sm["ffn1_norm"], wt["ffn1_w_gate"], wt["ffn1_w_up"], wt["ffn1_w_down"])
    emit("ffn1_fwd_done", marker=h1)
    u = _rms_fwd("mix_norm", h1, sm["mix_norm"])
    d_in = wt["w_in"].shape[0]
    tm, tn = _pick(t, 1024), _pick(d_in, 1408)
    proj = _mm1("in_proj", "nt", u, wt["w_in"], t, d_in, tm, tn, F32)
    off_s = d_pool // d_ssm
    off_gp = (d_pool + d_ssm)
    off_gs = off_gp + d

    pool_w_bf = sm["pool_w"].astype(BF16)
    pooled, pm = _pool_fwd(proj, pool_w_bf, sm["pool_scale"])

    cols = [sm["ssm_a_re"].reshape(-1, 1), sm["ssm_a_im"].reshape(-1, 1),
            jnp.broadcast_to(sm["ssm_log_dt"][:, :, None], (2, sg, sp)).reshape(-1, 1),
            sm["ssm_b_re"].reshape(-1, sh), sm["ssm_b_im"].reshape(-1, sh)]
    abr, abi, bbr, bbi = _ssm_disc(cols)
    abr2, abi2 = abr.reshape(2, n_state), abi.reshape(2, n_state)
    bbr4, bbi4 = bbr.reshape(2, sg * sp, sh), bbi.reshape(2, sg * sp, sh)
    b_re = [_bd_in(bbr4[dr], sg, sp, sh).astype(BF16) for dr in range(2)]
    b_im = [_bd_in(bbi4[dr], sg, sp, sh).astype(BF16) for dr in range(2)]
    c_re = [_bd_out(sm["ssm_c_re"][dr], sg, sp, sh).astype(BF16) for dr in range(2)]
    c_im = [_bd_out(-sm["ssm_c_im"][dr], sg, sp, sh).astype(BF16) for dr in range(2)]
    tms = _pick(t, 512)
    s_bf = _ew("ssm_cast", lambda v: (v,), [proj[:, d_pool:d_pool + d_ssm]], [BF16])[0]
    xs = []
    for dr in range(2):
        u_d = _mm1(f"ssm_in{dr}", "nn", s_bf, jnp.concatenate([b_re[dr], b_im[dr]], axis=1), t, 2 * n_state, tms,
                   _pick(2 * n_state, 512), F32)
        xs.append(_scan(f"ssm_scan{dr}", u_d, abr2[dr:dr + 1], abi2[dr:dr + 1], reverse=(dr == 1), conj=False))
    tmy = _pick(t, 256)
    x_list = [xs[0][0], xs[0][1], xs[1][0], xs[1][1]]
    y = _mm("ssm_out", "nn", x_list, [c_re[0], c_im[0], c_re[1], c_im[1]], [[(k, k) for k in range(4)]], t, d_ssm, tmy,
            d_ssm, [(proj, _tile(tmy, d_ssm, off_s)), (sm["ssm_d"], _rowvec(d_ssm))],
            lambda accs, sv, dv: (sv * dv + accs[0],), [(_out(t, d_ssm, F32), None)])[0]
    ys = _ew("ssm_gelu", lambda v: (jax.nn.gelu(v),), [y], [BF16])[0]
    emit("mix_in_done", marker=ys)

    tmm, tnm, tnx = _pick(t, 1024), _pick(d, 256), _pick(d, 512)
    gp_spec = _tile(tmm, tnm, off_gp // tnm)
    gs_spec = _tile(tmm, tnm, off_gs // tnm)

    def merge_epi(accs, gpv, gsv):
        z_pool, val, gate = accs
        return (jax.nn.sigmoid(gpv) * z_pool + jax.nn.sigmoid(gsv) * (val * jax.nn.sigmoid(gate)),)

    merged = _mm("mix_merge", "nt", [pm, ys], [wt["w_pool_proj"], wt["w_glu_val"], wt["w_glu_gate"]],
                 [[(0, 0)], [(1, 1)], [(1, 2)]], t, d, tmm, tnm, [(proj, gp_spec), (proj, gs_spec)], merge_epi,
                 [(_out(t, d, BF16), None)])[0]
    res_epi = lambda accs, hin: (hin + accs[0],)
    h2 = _mm("mix_out", "nn", [merged], [wt["w_mix_out"]], [[(0, 0)]], t, d, tmm, tnx, [(h1, _tile(tmm, tnx))],
             res_epi, [(_out(t, d, F32), None)])[0]

    un = _rms_fwd("xattn_norm", h2, sm["xattn_norm"])
    mn = _rms_fwd("mem_norm", mem, sm["mem_norm"])
    q = _mm1("xattn_q", "nn", un, wt["w_q"], t, d, tmm, tnx, BF16)
    kv = _mm1("xattn_kv", "nt", mn, wt["w_kv"], n_mem, 2 * d, n_mem, _pick(2 * d, 512), BF16)
    o = _attn_fwd(q, kv)
    h3 = _mm("xattn_out", "nn", [o], [wt["w_xo"]], [[(0, 0)]], t, d, tmm, tnx, [(h2, _tile(tmm, tnx))],
             res_epi, [(_out(t, d, F32), None)])[0]

    h4, ffn2_saved = _ffn_fwd("ffn2", h3, sm["ffn2_norm"], wt["ffn2_w_gate"], wt["ffn2_w_up"], wt["ffn2_w_down"])

    dh4, dh4_bf, gs["final_norm"], loss = _loss_head(h4, sm["final_norm"], tgt)
    dh3, dh3_bf, gs["ffn2_norm"], gb["ffn2_w_gate"], gb["ffn2_w_up"], gb["ffn2_w_down"] = _ffn_bwd(
        "ffn2", h3, sm["ffn2_norm"], wt["ffn2_w_gate"], wt["ffn2_w_up"], wt["ffn2_w_down"], ffn2_saved, dh4, dh4_bf)

    tw = _pick(d, 256)
    do = _mm1("xattn_do", "nt", dh3_bf, wt["w_xo"], t, d, tmm, tnx, BF16)
    gb["w_xo"] = _mm1("xattn_dwxo", "tn", o, dh3_bf, d, d, tw, d, BF16)
    dq, dkv = _attn_bwd(q, kv, do)
    gb["w_q"] = _mm1("xattn_dwq", "tn", un, dq, d, d, tw, d, BF16)
    dun = _mm1("xattn_dun", "nt", dq, wt["w_q"], t, d, tmm, tnx, F32)
    dh2, dh2_bf, gs["xattn_norm"] = _rms_bwd("xattn_norm_bwd", h2, sm["xattn_norm"], dun, dh3)
    gb["w_kv"] = _mm1("xattn_dwkv", "tn", dkv, mn, 2 * d, d, _pick(2 * d, 512), d, BF16)
    dmn = _mm1("xattn_dmn", "nn", dkv, wt["w_kv"], n_mem, d, n_mem, tnx, F32)
    gs["mem_norm"] = _rms_bwd("mem_norm_bwd", mem, sm["mem_norm"], dmn)

    gb["w_mix_out"] = _mm1("mix_dwout", "tn", merged, dh2_bf, d, d, tw, d, BF16)

    def merge_bwd_epi(accs, gpv, gsv):
        dmerged, z_pool, val, gate = accs
        sp_, ss_, sg_ = jax.nn.sigmoid(gpv), jax.nn.sigmoid(gsv), jax.nn.sigmoid(gate)
        glu = val * sg_
        dz_pool = dmerged * sp_
        dg_pool = dmerged * z_pool * (sp_ * (1.0 - sp_))
        dz_ssm = dmerged * ss_
        dg_ssm = dmerged * glu * (ss_ * (1.0 - ss_))
        dval = dz_ssm * sg_
        dgate = dz_ssm * glu * (1.0 - sg_)
        return dz_pool, dg_pool, dg_ssm, dval, dgate

    dz_pool, dg_pool, dg_ssm, dval, dgate = _mm(
        "mix_merge_bwd", "nt", [dh2_bf, pm, ys], [wt["w_mix_out"], wt["w_pool_proj"], wt["w_glu_val"], wt["w_glu_gate"]],
        [[(0, 0)], [(1, 1)], [(2, 2)], [(2, 3)]], t, d, tmm, tnm, [(proj, gp_spec), (proj, gs_spec)], merge_bwd_epi,
        [(_out(t, d, BF16), None)] * 5)
    gb["w_pool_proj"] = _mm1("pool_dwproj", "tn", dz_pool, pm, d, d_pool, tw, d_pool, BF16)
    gb["w_glu_val"] = _mm1("glu_dwval", "tn", dval, ys, d, d_ssm, tw, d_ssm, BF16)
    gb["w_glu_gate"] = _mm1("glu_dwgate", "tn", dgate, ys, d, d_ssm, tw, d_ssm, BF16)

    def gelu_bwd_epi(accs, yv):
        _, vjp = jax.vjp(jax.nn.gelu, yv)
        return (vjp(accs[0])[0],)

    dy = _mm("glu_dy", "nn", [dval, dgate], [wt["w_glu_val"], wt["w_glu_gate"]], [[(0, 0), (1, 1)]], t, d_ssm, tmy, d_ssm,
             [(y, _tile(tmy, d_ssm))], gelu_bwd_epi, [(_out(t, d_ssm, F32), None)])[0]
    gs["ssm_d"] = _colsum_prod("ssm_dd", dy, proj, b_coff=off_s)
    dy_bf = _ew("ssm_dy_cast", lambda v: (v,), [dy], [BF16])[0]
    d_abr, d_abi, d_bbr, d_bbi, d_cre, d_cim, lams = [], [], [], [], [], [], []
    ts = _pick(n_state, 512)
    tc_ = _pick(n_state, 256)
    for dr in range(2):
        gx = _mm1(f"ssm_gx{dr}", "nt", dy_bf, jnp.concatenate([c_re[dr], c_im[dr]], axis=0), t, 2 * n_state, tms,
                  _pick(2 * n_state, 512), F32)
        lr, li = _scan(f"ssm_adj{dr}", gx, abr2[dr:dr + 1], abi2[dr:dr + 1], reverse=(dr == 0), conj=True)
        dar, dai = _ssm_da(f"ssm_da{dr}", lr, li, xs[dr][0], xs[dr][1], reverse=(dr == 1))
        d_abr.append(dar)
        d_abi.append(dai)
        lams += [lr, li]
        d_bbr.append(_diag_in(_mm1(f"ssm_dbre{dr}", "tn", s_bf, lr, d_ssm, n_state, d_ssm, ts, F32), sg, sp, sh))
        d_bbi.append(_diag_in(_mm1(f"ssm_dbim{dr}", "tn", s_bf, li, d_ssm, n_state, d_ssm, ts, F32), sg, sp, sh))
        d_cre.append(_diag_out(_mm1(f"ssm_dcre{dr}", "tn", xs[dr][0], dy_bf, n_state, d_ssm, tc_, d_ssm, F32), sg, sp, sh))
        d_cim.append(-_diag_out(_mm1(f"ssm_dcim{dr}", "tn", xs[dr][1], dy_bf, n_state, d_ssm, tc_, d_ssm, F32), sg, sp, sh))
    ds = _mm("ssm_ds", "nt", lams, [b_re[0], b_im[0], b_re[1], b_im[1]], [[(k, k) for k in range(4)]], t, d_ssm, tmy,
             d_ssm, [(dy, _tile(tmy, d_ssm)), (sm["ssm_d"], _rowvec(d_ssm))],
             lambda accs, dyv, dv: (dyv * dv + accs[0],), [(_out(t, d_ssm, BF16), None)])[0]
    cots = [jnp.concatenate(d_abr, axis=0).reshape(-1, 1), jnp.concatenate(d_abi, axis=0).reshape(-1, 1),
            jnp.concatenate(d_bbr, axis=0), jnp.concatenate(d_bbi, axis=0)]
    d_are, d_aim, d_ldt, d_bre, d_bim = _ssm_disc_bwd(cols, cots)
    gs["ssm_a_re"] = d_are.reshape(2, sg, sp)
    gs["ssm_a_im"] = d_aim.reshape(2, sg, sp)
    gs["ssm_log_dt"] = _rowsum("ssm_dlogdt", d_ldt.reshape(2 * sg, sp)).reshape(2, sg)
    gs["ssm_b_re"] = d_bre.reshape(2, sg, sp, sh)
    gs["ssm_b_im"] = d_bim.reshape(2, sg, sp, sh)
    gs["ssm_c_re"] = jnp.stack(d_cre, axis=0)
    gs["ssm_c_im"] = jnp.stack(d_cim, axis=0)

    dpm = _mm1("pool_dpm", "nn", dz_pool, wt["w_pool_proj"], t, d_pool, tmm, _pick(d_pool, 256), F32)
    dp, gs["pool_w"], gs["pool_scale"] = _pool_bwd(pooled, dpm, pool_w_bf, sm["pool_scale"])

    w_in = wt["w_in"]
    parts = [(dp, 0, d_pool), (ds, d_pool, d_ssm), (dg_pool, off_gp, d), (dg_ssm, off_gs, d)]
    w_in_parts = [w_in[o0:o0 + width] for _, o0, width in parts]
    gb["w_in"] = jnp.concatenate(
        [_mm1(f"in_proj_dw{k}", "tn", p_[0], u, p_[2], d, _pick(p_[2], 256), d, BF16) for k, p_ in enumerate(parts)], axis=0)
    pin = emit("grads_main", gb=gb)
    du = _mm("in_proj_du", "nn", [p_[0] for p_ in parts], w_in_parts, [[(k, k) for k in range(4)]], t, d, tmm, tnx, [],
             lambda accs: (accs[0],), [(_out(t, d, F32), None)], after=pin)[0]
    dh1, dh1_bf, gs["mix_norm"] = _rms_bwd("mix_norm_bwd", h1, sm["mix_norm"], du, dh2)
    pin = emit("small_early", gs=gs, loss=loss)

    def ffn1_weights_done(d_wg, d_wu, d_wd):
        gb["ffn1_w_gate"], gb["ffn1_w_up"], gb["ffn1_w_down"] = d_wg, d_wu, d_wd
        return emit("grads_ffn1", gb=gb)

    dx, _, gs["ffn1_norm"], _, _, _ = _ffn_bwd(
        "ffn1", x, sm["ffn1_norm"], wt["ffn1_w_gate"], wt["ffn1_w_up"], wt["ffn1_w_down"], ffn1_saved, dh1, dh1_bf,
        weights_done=ffn1_weights_done, after=pin)
    return loss, dx, gb, gs


WEIGHTS = ["ffn1_norm", "ffn1_w_gate", "ffn1_w_up", "ffn1_w_down", "mix_norm", "w_in", "pool_w", "pool_scale",
           "w_pool_proj", "ssm_a_re", "ssm_a_im", "ssm_log_dt", "ssm_b_re", "ssm_b_im", "ssm_c_re", "ssm_c_im", "ssm_d",
           "w_glu_val", "w_glu_gate", "w_mix_out", "xattn_norm", "mem_norm", "w_q", "w_kv", "w_xo", "ffn2_norm",
           "ffn2_w_gate", "ffn2_w_up", "ffn2_w_down", "final_norm"]
COL_SHARDED = ["ffn1_w_gate", "ffn1_w_up", "w_in", "w_pool_proj", "w_glu_val", "w_glu_gate", "w_kv", "ffn2_w_gate",
               "ffn2_w_up"]
ROW_SHARDED = ["ffn1_w_down", "w_mix_out", "w_q", "w_xo", "ffn2_w_down"]
BIG = [n for n in WEIGHTS if n in COL_SHARDED or n in ROW_SHARDED]
SMALL = [n for n in WEIGHTS if n not in BIG]
FFN1_BIG = ["ffn1_w_gate", "ffn1_w_up", "ffn1_w_down"]
MAIN_BIG = [n for n in BIG if n not in FFN1_BIG]
LATE_SMALL = "ffn1_norm"
EARLY_SMALL = [n for n in SMALL if n != LATE_SMALL]
PACK_ROWS = SUBLANES * LANES
GRAD_ROW_TILE = 256


def _to_rows(name, w, width):
    if name in COL_SHARDED:
        w = w.T
    return w.reshape(-1, width)


def _from_rows(name, rows, shard_shape):
    if name in COL_SHARDED:
        return rows.reshape(shard_shape[1], shard_shape[0]).T
    return rows.reshape(shard_shape)


def _pack_small(vals):
    flat = []
    for v in vals:
        f = v.reshape(-1)
        flat.append(jnp.pad(f, (0, (-f.shape[0]) % PACK_ROWS)))
    total = sum(f.shape[0] for f in flat)
    flat.append(jnp.zeros(((-total) % (GRAD_ROW_TILE * LANES),), F32))
    return jnp.concatenate(flat).reshape(-1, LANES)


def _unpack_small(packed, shapes):
    out, row = [], 0
    for shp in shapes:
        size = math.prod(shp)
        rows = -(-size // PACK_ROWS) * SUBLANES
        out.append(packed[row:row + rows].reshape(-1)[:size].reshape(shp))
        row += rows
    return out


def kernel(x, mem, ffn1_norm, ffn1_w_gate, ffn1_w_up, ffn1_w_down, mix_norm, w_in, pool_w, pool_scale, w_pool_proj, ssm_a_re, ssm_a_im, ssm_log_dt, ssm_b_re, ssm_b_im, ssm_c_re, ssm_c_im, ssm_d, w_glu_val, w_glu_gate, w_mix_out, xattn_norm, mem_norm, w_q, w_kv, w_xo, ffn2_norm, ffn2_w_gate, ffn2_w_up, ffn2_w_down, final_norm, loss_target, m_ffn1_norm, m_ffn1_w_gate, m_ffn1_w_up, m_ffn1_w_down, m_mix_norm, m_w_in, m_pool_w, m_pool_scale, m_w_pool_proj, m_ssm_a_re, m_ssm_a_im, m_ssm_log_dt, m_ssm_b_re, m_ssm_b_im, m_ssm_c_re, m_ssm_c_im, m_ssm_d, m_w_glu_val, m_w_glu_gate, m_w_mix_out, m_xattn_norm, m_mem_norm, m_w_q, m_w_kv, m_w_xo, m_ffn2_norm, m_ffn2_w_gate, m_ffn2_w_up, m_ffn2_w_down, m_final_norm, v_ffn1_norm, v_ffn1_w_gate, v_ffn1_w_up, v_ffn1_w_down, v_mix_norm, v_w_in, v_pool_w, v_pool_scale, v_w_pool_proj, v_ssm_a_re, v_ssm_a_im, v_ssm_log_dt, v_ssm_b_re, v_ssm_b_im, v_ssm_c_re, v_ssm_c_im, v_ssm_d, v_w_glu_val, v_w_glu_gate, v_w_mix_out, v_xattn_norm, v_mem_norm, v_w_q, v_w_kv, v_w_xo, v_ffn2_norm, v_ffn2_w_gate, v_ffn2_w_up, v_ffn2_w_down, v_final_norm):
    given = dict(locals())
    wts = {n: given[n] for n in WEIGHTS}
    moms = {n: (given["m_" + n], given["v_" + n]) for n in WEIGHTS}
    x2, mem2, tgt2 = x[0], mem[0], loss_target[0]
    d = x2.shape[1]
    core = lax.axis_index("c").astype(jnp.int32).reshape(1)
    chip = (2 * lax.axis_index("x") + lax.axis_index("y")).astype(jnp.int32).reshape(1)

    def full_form(n, f):
        shard = wts[n][0].shape
        return f.reshape(N_DEV * shard[1], shard[0]) if n in COL_SHARDED else f.reshape(N_DEV * shard[0], shard[1])

    shards = {n: _to_rows(n, wts[n][0], d).astype(BF16) for n in BIG}
    wt = {n: full_form(n, f) for n, f in zip(FFN1_BIG, _allgather("weight_allgather_ffn1", [shards[n] for n in FFN1_BIG]))}
    rest = [n for n in MAIN_BIG if n != "w_in"]
    gather_in = _gather_start("weight_gather_in_start", [shards["w_in"]], wt[FFN1_BIG[0]])
    gather_rest = _gather_start("weight_gather_rest_start", [shards[n] for n in rest], gather_in["token"])
    sm = {n: (wts[n].reshape(1, -1) if wts[n].ndim <= 2 else wts[n][0]) for n in SMALL}
    sm["ffn1_norm"] = sm["ffn1_norm"] + (gather_in["token"][0, 0] + gather_rest["token"][0, 0])

    pending = {}

    def reduce_start(tag, names, gb):
        blocks = [gb[n].reshape(N_DEV, -1, d) for n in names]
        pad_rows = (-sum(b.shape[1] for b in blocks)) % GRAD_ROW_TILE
        packed = jnp.concatenate(blocks + ([jnp.zeros((N_DEV, pad_rows, d), BF16)] if pad_rows else []), axis=1)
        pair = _pair_sum("grad_pair_sum_" + tag, packed, _exchange_cores("grad_exchange_cores_" + tag, packed), core)
        pending[tag] = (pair, _chips_start("grad_exchange_chips_start_" + tag, pair), [b.shape[1] for b in blocks])
        return pending[tag][1]["token"]

    def reduce_finish(tag, after):
        pair, started, rows = pending[tag]
        recv = _split_wait("grad_exchange_chips_wait_" + tag, started, after)[0]
        return _chip_sum("grad_chip_sum_" + tag, pair, recv, chip), rows

    def ev(name, gb=None, gs=None, loss=None, marker=None):
        if name == "ffn1_fwd_done":
            wt["w_in"] = full_form("w_in", _split_wait("weight_gather_in_wait", gather_in, marker)[0])
        elif name == "mix_in_done":
            for n, f in zip(rest, _split_wait("weight_gather_rest_wait", gather_rest, marker)):
                wt[n] = full_form(n, f)
        elif name == "grads_main":
            return reduce_start("main", MAIN_BIG, gb)
        elif name == "small_early":
            pending["small"] = _slots_start("small_gather_start", _pack_small([gs[n] for n in EARLY_SMALL] + [loss[:, :1]]))
            return pending["small"]["token"]
        elif name == "grads_ffn1":
            return reduce_start("ffn1", FFN1_BIG, gb)
        return None

    _, dx, _, gs = _local_step(x2, mem2, tgt2, wt, sm, ev)

    out_g, out_d, out_m, out_v = {}, {}, {}, {}

    def update_big(names, g_rows, rows):
        off = 0
        for n, r in zip(names, rows):
            shard = wts[n].shape
            g_full = _from_rows(n, g_rows[off:off + r], shard[1:]).reshape(shard)
            off += r
            two_d = (-1, shard[-1])
            dl, m2, v2 = _adamw("adamw_" + n, wts[n].reshape(two_d), g_full.reshape(two_d), moms[n][0].reshape(two_d),
                                moms[n][1].reshape(two_d))
            out_g[n], out_d[n], out_m[n], out_v[n] = g_full, dl.reshape(shard), m2.reshape(shard), v2.reshape(shard)
        return dl

    last = update_big(MAIN_BIG, *reduce_finish("main", dx))

    small_sum = _sum_slots("small_sum", _split_wait("small_gather_wait", pending["small"], dx)[0], F32)
    late = _allgather("small_allgather_late", [gs[LATE_SMALL].reshape(-1, LANES)])[0]
    late_sum = _sum_slots("small_sum_late", late.reshape(N_DEV, -1, LANES), F32)
    zero = jnp.zeros((1, 1), F32)
    shapes = [wts[n].shape for n in EARLY_SMALL] + [(1, 1)]
    packs = [_pack_small([src[n] for n in EARLY_SMALL] + [zero])
             for src in (wts, {n: moms[n][0] for n in SMALL}, {n: moms[n][1] for n in SMALL})]
    dl, m2, v2 = _adamw("adamw_small", packs[0], small_sum, packs[1], packs[2])
    for dst, src in ((out_g, small_sum), (out_d, dl), (out_m, m2), (out_v, v2)):
        vals = _unpack_small(src, shapes)
        for n, val in zip(EARLY_SMALL, vals):
            dst[n] = val
        if dst is out_g:
            total_loss = vals[-1].reshape(())
    shp = wts[LATE_SMALL].shape
    dl, m2, v2 = _adamw("adamw_" + LATE_SMALL, wts[LATE_SMALL].reshape(-1, LANES), late_sum,
                        moms[LATE_SMALL][0].reshape(-1, LANES), moms[LATE_SMALL][1].reshape(-1, LANES))
    for dst, src in ((out_g, late_sum), (out_d, dl), (out_m, m2), (out_v, v2)):
        dst[LATE_SMALL] = src.reshape(shp)

    update_big(FFN1_BIG, *reduce_finish("ffn1", last))

    return (total_loss, dx[None], *[out_g[n] for n in WEIGHTS], *[out_d[n] for n in WEIGHTS],
            *[out_m[n] for n in WEIGHTS], *[out_v[n] for n in WEIGHTS])
```

```python
import functools
import math

import jax
import jax.numpy as jnp
from jax import lax
from jax.experimental import pallas as pl
from jax.experimental.pallas import tpu as pltpu

F32 = jnp.float32
BF16 = jnp.bfloat16
EPS = 1e-6
N_XHEADS = 4
POOL_WINDOWS = (2, 4, 8, 16)
ADAM_LR = 0.001
ADAM_B1 = 0.9
ADAM_B2 = 0.999
ADAM_EPS = 1e-08
ADAM_WD = 0.01
ADAM_STEP = 10
N_DEV = 8
VMEM_LIMIT_V7X = 48 * 1024 * 1024
LANES = 128
SUBLANES = 8
SUB_ROWS = 256
POOL_PAD = 16
MESH = pl.DeviceIdType.MESH
ANY = pl.BlockSpec(memory_space=pl.ANY)
HBM = pl.BlockSpec(memory_space=pltpu.HBM)
SEM = pl.BlockSpec(memory_space=pltpu.SEMAPHORE)
SIDE_EFFECT = pltpu.SideEffectType.DATAFLOW_SIDE_EFFECTING

_DIMS = {
    "nt": (((1,), (1,)), ((), ())),
    "nn": (((1,), (0,)), ((), ())),
    "tn": (((0,), (0,)), ((), ())),
}


def _pick(dim, pref, mult=LANES):
    if dim <= pref:
        return dim
    for t in range(pref - pref % mult, 0, -mult):
        if dim % t == 0:
            return t
    return dim


def _params(sem):
    return pltpu.CompilerParams(dimension_semantics=sem, vmem_limit_bytes=VMEM_LIMIT_V7X)


def _tile(tm, tn, coff=0):
    return pl.BlockSpec((tm, tn), lambda i, j: (i, j + coff))


def _rowvec(tn, coff=0):
    return pl.BlockSpec((1, tn), lambda i, j: (0, j + coff))


def _out(m, n, dtype):
    return jax.ShapeDtypeStruct((m, n), dtype)


def _mm(name, form, a_list, b_list, groups, m, n, tm, tn, extras, epilogue, outs, after=None, sub=SUB_ROWS):
    na, nb, ne = len(a_list), len(b_list), len(extras)
    pins = [] if after is None else [after]
    step = tm if (sub is None or form == "tn" or tm % sub) else sub

    def a_spec(a):
        if form == "tn":
            return pl.BlockSpec((a.shape[0], tm), lambda i, j: (0, i))
        return pl.BlockSpec((tm, a.shape[1]), lambda i, j: (i, 0))

    def b_spec(b):
        if form == "nt":
            return pl.BlockSpec((tn, b.shape[1]), lambda i, j: (j, 0))
        return pl.BlockSpec((b.shape[0], tn), lambda i, j: (0, j))

    def body(*refs):
        a_refs, b_refs = refs[:na], refs[na:na + nb]
        e_refs, o_refs = refs[na + nb:na + nb + ne], refs[na + nb + ne + len(pins):]
        b_vals = {}
        for s0 in range(0, tm, step):
            rows = slice(None) if step == tm else pl.ds(s0, step)
            a_vals, accs = {}, []
            for group in groups:
                acc = None
                for ai, bi in group:
                    if ai not in a_vals:
                        a_vals[ai] = (a_refs[ai][...] if form == "tn" else a_refs[ai][rows, :]).astype(BF16)
                    if bi not in b_vals:
                        b_vals[bi] = b_refs[bi][...].astype(BF16)
                    d = lax.dot_general(a_vals[ai], b_vals[bi], _DIMS[form], preferred_element_type=F32)
                    acc = d if acc is None else acc + d
                accs.append(acc)
            res = epilogue(accs, *[e[rows, :] if e.shape[0] == tm else e[...] for e in e_refs])
            for o_ref, r in zip(o_refs, res):
                o_ref[rows, :] = r.astype(o_ref.dtype)

    out_specs = [_tile(tm, tn) if s is None else s for _, s in outs]
    res = pl.pallas_call(
        body, name=name, grid=(m // tm, n // tn),
        in_specs=[a_spec(a) for a in a_list] + [b_spec(b) for b in b_list] + [s for _, s in extras] + [ANY] * len(pins),
        out_specs=out_specs, out_shape=[o for o, _ in outs],
        compiler_params=_params(("parallel", "parallel")),
    )(*a_list, *b_list, *[e for e, _ in extras], *pins)
    return res


def _mm1(name, form, a, b, m, n, tm, tn, dtype, scale=None):
    epi = (lambda accs: (accs[0],)) if scale is None else (lambda accs: (accs[0] * scale,))
    return _mm(name, form, [a], [b], [[(0, 0)]], m, n, tm, tn, [], epi, [(_out(m, n, dtype), None)])[0]


def _rms_fwd(name, h, g):
    t, d = h.shape
    tm = _pick(t, 512, SUBLANES)

    def body(h_ref, g_ref, n_ref):
        hv = h_ref[...]
        r = lax.rsqrt(jnp.mean(hv * hv, axis=-1, keepdims=True) + EPS)
        n_ref[...] = ((hv * r) * g_ref[...]).astype(BF16)

    return pl.pallas_call(
        body, name=name, grid=(t // tm,),
        in_specs=[pl.BlockSpec((tm, d), lambda i: (i, 0)), pl.BlockSpec((1, d), lambda i: (0, 0))],
        out_specs=pl.BlockSpec((tm, d), lambda i: (i, 0)), out_shape=_out(t, d, BF16),
        compiler_params=_params(("parallel",)),
    )(h, g)


def _rms_bwd(name, h, g, dn, dres=None):
    t, d = h.shape
    tm = _pick(t, 512, SUBLANES)
    need_dh = dres is not None

    def body(*refs):
        if need_dh:
            h_ref, g_ref, dn_ref, dres_ref, dh_ref, dhb_ref, dg_ref = refs
        else:
            h_ref, g_ref, dn_ref, dg_ref = refs
        hv = h_ref[...]
        r = lax.rsqrt(jnp.mean(hv * hv, axis=-1, keepdims=True) + EPS)
        nh = hv * r
        dnv = dn_ref[...].astype(F32)

        @pl.when(pl.program_id(0) == 0)
        def _():
            dg_ref[...] = jnp.zeros_like(dg_ref)

        dg_ref[...] += jnp.sum(dnv * nh, axis=0, keepdims=True)
        if need_dh:
            dng = dnv * g_ref[...]
            dh = dres_ref[...] + r * (dng - nh * jnp.mean(dng * nh, axis=-1, keepdims=True))
            dh_ref[...] = dh
            dhb_ref[...] = dh.astype(BF16)

    row = pl.BlockSpec((tm, d), lambda i: (i, 0))
    vec = pl.BlockSpec((1, d), lambda i: (0, 0))
    if need_dh:
        return pl.pallas_call(
            body, name=name, grid=(t // tm,), in_specs=[row, vec, row, row], out_specs=[row, row, vec],
            out_shape=[_out(t, d, F32), _out(t, d, BF16), _out(1, d, F32)], compiler_params=_params(("arbitrary",)),
        )(h, g, dn, dres)
    return pl.pallas_call(
        body, name=name, grid=(t // tm,), in_specs=[row, vec, row], out_specs=vec,
        out_shape=_out(1, d, F32), compiler_params=_params(("arbitrary",)),
    )(h, g, dn)


def _loss_head(h, g, tgt):
    t, d = h.shape
    tm = _pick(t, 512, SUBLANES)

    def body(h_ref, g_ref, t_ref, dh_ref, dhb_ref, dg_ref, loss_ref):
        hv = h_ref[...]
        r = lax.rsqrt(jnp.mean(hv * hv, axis=-1, keepdims=True) + EPS)
        nh = hv * r
        err = nh * g_ref[...] - t_ref[...]

        @pl.when(pl.program_id(0) == 0)
        def _():
            dg_ref[...] = jnp.zeros_like(dg_ref)
            loss_ref[...] = jnp.zeros_like(loss_ref)

        per_row = jnp.mean(err * err, axis=-1, keepdims=True)
        loss_ref[...] += 0.5 * jnp.sum(per_row, axis=0, keepdims=True)
        dy = err * (1.0 / d)
        dg_ref[...] += jnp.sum(dy * nh, axis=0, keepdims=True)
        dng = dy * g_ref[...]
        dh = r * (dng - nh * jnp.mean(dng * nh, axis=-1, keepdims=True))
        dh_ref[...] = dh
        dhb_ref[...] = dh.astype(BF16)

    row = pl.BlockSpec((tm, d), lambda i: (i, 0))
    vec = pl.BlockSpec((1, d), lambda i: (0, 0))
    return pl.pallas_call(
        body, name="loss_head", grid=(t // tm,), in_specs=[row, vec, row],
        out_specs=[row, row, vec, pl.BlockSpec((1, LANES), lambda i: (0, 0))],
        out_shape=[_out(t, d, F32), _out(t, d, BF16), _out(1, d, F32), _out(1, LANES, F32)],
        compiler_params=_params(("arbitrary",)),
    )(h, g, tgt)


def _ffn_fwd(tag, h, g, wg_t, wu_t, wd):
    t, d = h.shape
    f = wd.shape[0]
    n = _rms_fwd(tag + "_norm", h, g)
    tm, tn = _pick(t, 1024), _pick(f, 1408)

    def up_epi(accs):
        a, b = accs
        return a, b, (a * jax.nn.sigmoid(a)) * b

    a, b, hid = _mm(tag + "_up", "nt", [n], [wg_t, wu_t], [[(0, 0)], [(0, 1)]], t, f, tm, tn, [], up_epi,
                    [(_out(t, f, BF16), None)] * 3)
    tm2, tn2 = _pick(t, 1024), _pick(d, 512)
    h_out = _mm(tag + "_down", "nn", [hid], [wd], [[(0, 0)]], t, d, tm2, tn2, [(h, _tile(tm2, tn2))],
                lambda accs, hin: (hin + 0.5 * accs[0],), [(_out(t, d, F32), None)])[0]
    return h_out, (n, a, b, hid)


def _ffn_bwd(tag, h, g, wg_t, wu_t, wd, saved, dh, dh_bf, weights_done=None, after=None):
    n, a, b, hid = saved
    t, d = h.shape
    f = wd.shape[0]
    tm, tn = _pick(t, 1024), _pick(f, 1408)

    def hid_epi(accs, av, bv):
        dhid = 0.5 * accs[0]
        av, bv = av.astype(F32), bv.astype(F32)
        sig = jax.nn.sigmoid(av)
        da = dhid * bv * (sig * (1.0 + av * (1.0 - sig)))
        db = dhid * (av * sig)
        return da, db

    da, db = _mm(tag + "_bwd_hid", "nt", [dh_bf], [wd], [[(0, 0)]], t, f, tm, tn,
                 [(a, _tile(tm, tn)), (b, _tile(tm, tn))], hid_epi, [(_out(t, f, BF16), None)] * 2, after=after)
    tw, tnw = _pick(f, 1408), _pick(d, 512)
    d_wd = _mm1(tag + "_dwd", "tn", hid, dh_bf, f, d, tw, tnw, BF16, scale=0.5)
    d_wg = _mm1(tag + "_dwg", "tn", da, n, f, d, tw, tnw, BF16)
    d_wu = _mm1(tag + "_dwu", "tn", db, n, f, d, tw, tnw, BF16)
    pin = weights_done(d_wg, d_wu, d_wd) if weights_done is not None else None
    tm2, tn2 = _pick(t, 1024), _pick(d, 512)
    dn = _mm(tag + "_dn", "nn", [da, db], [wg_t, wu_t], [[(0, 0), (1, 1)]], t, d, tm2, tn2, [],
             lambda accs: (accs[0],), [(_out(t, d, F32), None)], after=pin)[0]
    dh_in, dh_in_bf, dg = _rms_bwd(tag + "_norm_bwd", h, g, dn, dh)
    return dh_in, dh_in_bf, dg, d_wg, d_wu, d_wd


def _window_sum(win, offsets):
    n = win.shape[0]
    acc = None
    for j in offsets:
        term = win if j == 0 else pltpu.roll(win, (-j) % n, 0)
        acc = term if acc is None else acc + term
    return acc


def _pool_counts(r0, ch, c, left, right, t):
    pos = r0 + lax.broadcasted_iota(jnp.int32, (ch, c), 0)
    return (jnp.minimum(pos + right + 1, t) - jnp.maximum(pos - left, 0)).astype(F32)


def _pool_fwd(proj, pool_w_bf, pool_scale):
    t = proj.shape[0]
    ng, c, _ = pool_w_bf.shape
    ch = _pick(t, 256, SUBLANES)
    pad = POOL_PAD

    def body(p_ref, w_ref, s_ref, pooled_ref, pm_ref, buf):
        grp = pl.program_id(0)
        buf[pl.ds(0, pad), :] = jnp.zeros((pad, c), F32)
        buf[pl.ds(pad + t, pad), :] = jnp.zeros((pad, c), F32)

        def fill(ci, carry):
            r0 = pl.multiple_of(ci * ch, SUBLANES)
            buf[pl.ds(pl.multiple_of(r0 + pad, SUBLANES), ch), :] = p_ref[pl.ds(r0, ch), :]
            return carry

        lax.fori_loop(0, t // ch, fill, 0)
        for gi, w in enumerate(POOL_WINDOWS):
            left = w // 2
            right = w - 1 - left

            @pl.when(grp == gi)
            def _(left=left, right=right):
                def chunk(ci, carry):
                    r0 = pl.multiple_of(ci * ch, SUBLANES)
                    win = buf[pl.ds(r0, ch + 2 * pad), :]
                    s = _window_sum(win, range(-left, right + 1))[pad:pad + ch]
                    pooled = s / _pool_counts(r0, ch, c, left, right, t) - win[pad:pad + ch]
                    pooled_bf = pooled.astype(BF16)
                    mixed = jnp.dot(pooled_bf, w_ref[0], preferred_element_type=F32)
                    pooled_ref[pl.ds(r0, ch), :] = pooled_bf
                    pm_ref[pl.ds(r0, ch), :] = (mixed * s_ref[...]).astype(BF16)
                    return carry

                lax.fori_loop(0, t // ch, chunk, 0)

    col = pl.BlockSpec((t, c), lambda g: (0, g))
    return pl.pallas_call(
        body, name="pool_fwd", grid=(ng,),
        in_specs=[col, pl.BlockSpec((1, c, c), lambda g: (g, 0, 0)), pl.BlockSpec((1, c), lambda g: (0, g))],
        out_specs=[col, col], out_shape=[_out(t, ng * c, BF16), _out(t, ng * c, BF16)],
        scratch_shapes=[pltpu.VMEM((t + 2 * pad, c), F32)],
        compiler_params=_params(("parallel",)),
    )(proj, pool_w_bf, pool_scale)


def _pool_bwd(pooled, dpm, pool_w_bf, pool_scale):
    t = pooled.shape[0]
    ng, c, _ = pool_w_bf.shape
    ch = _pick(t, 256, SUBLANES)
    pad = POOL_PAD

    def body(pooled_ref, dpm_ref, w_ref, s_ref, dp_ref, dw_ref, ds_ref, buf, raw):
        grp = pl.program_id(0)
        buf[pl.ds(0, pad), :] = jnp.zeros((pad, c), F32)
        buf[pl.ds(pad + t, pad), :] = jnp.zeros((pad, c), F32)
        dw_ref[...] = jnp.zeros_like(dw_ref)
        ds_ref[...] = jnp.zeros_like(ds_ref)
        for gi, w in enumerate(POOL_WINDOWS):
            left = w // 2
            right = w - 1 - left

            @pl.when(grp == gi)
            def _(left=left, right=right):
                def first(ci, carry):
                    r0 = pl.multiple_of(ci * ch, SUBLANES)
                    pv = pooled_ref[pl.ds(r0, ch), :]
                    dpm_v = dpm_ref[pl.ds(r0, ch), :]
                    mixed = jnp.dot(pv, w_ref[0], preferred_element_type=F32)
                    ds_ref[...] += jnp.sum(dpm_v * mixed, axis=0, keepdims=True)
                    dmixed = (dpm_v * s_ref[...]).astype(BF16)
                    dw_ref[0] += lax.dot_general(pv, dmixed, _DIMS["tn"], preferred_element_type=F32)
                    dpooled = lax.dot_general(dmixed, w_ref[0], _DIMS["nt"], preferred_element_type=F32)
                    raw[pl.ds(r0, ch), :] = dpooled
                    buf[pl.ds(pl.multiple_of(r0 + pad, SUBLANES), ch), :] = (
                        dpooled / _pool_counts(r0, ch, c, left, right, t))
                    return carry

                lax.fori_loop(0, t // ch, first, 0)

                def second(ci, carry):
                    r0 = pl.multiple_of(ci * ch, SUBLANES)
                    win = buf[pl.ds(r0, ch + 2 * pad), :]
                    s = _window_sum(win, range(-right, left + 1))[pad:pad + ch]
                    dp_ref[pl.ds(r0, ch), :] = (s - raw[pl.ds(r0, ch), :]).astype(BF16)
                    return carry

                lax.fori_loop(0, t // ch, second, 0)

    col = pl.BlockSpec((t, c), lambda g: (0, g))
    return pl.pallas_call(
        body, name="pool_bwd", grid=(ng,),
        in_specs=[col, col, pl.BlockSpec((1, c, c), lambda g: (g, 0, 0)), pl.BlockSpec((1, c), lambda g: (0, g))],
        out_specs=[col, pl.BlockSpec((1, c, c), lambda g: (g, 0, 0)), pl.BlockSpec((1, c), lambda g: (0, g))],
        out_shape=[_out(t, ng * c, BF16), jax.ShapeDtypeStruct((ng, c, c), F32), _out(1, ng * c, F32)],
        scratch_shapes=[pltpu.VMEM((t + 2 * pad, c), F32), pltpu.VMEM((t, c), F32)],
        compiler_params=_params(("parallel",)),
    )(pooled, dpm, pool_w_bf, pool_scale)


def _discretise(a_re, a_im, log_dt, b_re, b_im):
    dt = jnp.exp(log_dt)
    mag = jnp.exp(dt * a_re)
    ang = dt * a_im
    abr = mag * jnp.cos(ang)
    abi = mag * jnp.sin(ang)
    den = a_re * a_re + a_im * a_im
    nr = abr - 1.0
    qr = (nr * a_re + abi * a_im) / den
    qi = (abi * a_re - nr * a_im) / den
    return abr, abi, qr * b_re - qi * b_im, qr * b_im + qi * b_re


def _ssm_disc(cols):
    n, hh = cols[3].shape

    def body(ar, ai, ld, br, bi, o1, o2, o3, o4):
        res = _discretise(ar[...], ai[...], ld[...], br[...], bi[...])
        for o, r in zip((o1, o2, o3, o4), res):
            o[...] = r

    return pl.pallas_call(
        body, name="ssm_disc",
        out_shape=[_out(n, 1, F32), _out(n, 1, F32), _out(n, hh, F32), _out(n, hh, F32)],
    )(*cols)


def _ssm_disc_bwd(cols, cots):
    n, hh = cols[3].shape

    def body(ar, ai, ld, br, bi, c1, c2, c3, c4, o1, o2, o3, o4, o5):
        _, vjp = jax.vjp(_discretise, ar[...], ai[...], ld[...], br[...], bi[...])
        res = vjp((c1[...], c2[...], c3[...], c4[...]))
        for o, r in zip((o1, o2, o3, o4, o5), res):
            o[...] = r

    return pl.pallas_call(
        body, name="ssm_disc_bwd",
        out_shape=[_out(n, 1, F32)] * 3 + [_out(n, hh, F32)] * 2,
    )(*cols, *cots)


def _rowsum(name, a):
    r, _ = a.shape

    def body(a_ref, o_ref):
        o_ref[...] = jnp.sum(a_ref[...], axis=-1, keepdims=True)

    return pl.pallas_call(body, name=name, out_shape=_out(r, 1, F32))(a)


def _cmul(pr, pi, qr, qi):
    return pr * qr - pi * qi, pr * qi + pi * qr


def _scan(name, u, ar, ai, reverse, conj):
    t, s2 = u.shape
    s = s2 // 2
    w = _pick(s, 512)
    tc = _pick(t, 512, SUBLANES)
    n_t, n_w = t // tc, s // w
    groups = tc // SUBLANES
    last = 0 if reverse else SUBLANES - 1

    def body(ar_ref, ai_ref, ur_ref, ui_ref, xr_ref, xi_ref, cr_ref, ci_ref):
        @pl.when(pl.program_id(1) == 0)
        def _():
            cr_ref[...] = jnp.zeros_like(cr_ref)
            ci_ref[...] = jnp.zeros_like(ci_ref)

        a1r = ar_ref[...]
        a1i = -ai_ref[...] if conj else ai_ref[...]
        a2r, a2i = _cmul(a1r, a1i, a1r, a1i)
        a4r, a4i = _cmul(a2r, a2i, a2r, a2i)
        row = lax.broadcasted_iota(jnp.int32, (SUBLANES, w), 0)
        pwr = jnp.zeros((SUBLANES, w), F32)
        pwi = jnp.zeros((SUBLANES, w), F32)
        cur_r, cur_i = a1r, a1i
        for k in range(SUBLANES):
            rk = SUBLANES - 1 - k if reverse else k
            pwr = jnp.where(row == rk, cur_r, pwr)
            pwi = jnp.where(row == rk, cur_i, pwi)
            cur_r, cur_i = _cmul(cur_r, cur_i, a1r, a1i)
        steps = ((1, a1r, a1i), (2, a2r, a2i), (4, a4r, a4i))

        def one(i, carry):
            g = groups - 1 - i if reverse else i
            r0 = pl.multiple_of(g * SUBLANES, SUBLANES)
            br = ur_ref[pl.ds(r0, SUBLANES), :]
            bi = ui_ref[pl.ds(r0, SUBLANES), :]
            for dist, pr, pi in steps:
                if reverse:
                    keep = row < SUBLANES - dist
                    shift = SUBLANES - dist
                else:
                    keep = row >= dist
                    shift = dist
                sr = jnp.where(keep, pltpu.roll(br, shift, 0), 0.0)
                si = jnp.where(keep, pltpu.roll(bi, shift, 0), 0.0)
                br, bi = br + pr * sr - pi * si, bi + pr * si + pi * sr
            cr = cr_ref[pl.ds(last, 1), :]
            ci = ci_ref[pl.ds(last, 1), :]
            xr = br + pwr * cr - pwi * ci
            xi = bi + pwr * ci + pwi * cr
            xr_ref[pl.ds(r0, SUBLANES), :] = xr
            xi_ref[pl.ds(r0, SUBLANES), :] = xi
            cr_ref[...] = xr
            ci_ref[...] = xi
            return carry

        lax.fori_loop(0, groups, one, 0)

    def tmap(k):
        return n_t - 1 - k if reverse else k

    re_blk = pl.BlockSpec((tc, w), lambda cb, k: (tmap(k), cb))
    im_blk = pl.BlockSpec((tc, w), lambda cb, k: (tmap(k), cb + n_w))
    a_blk = pl.BlockSpec((1, w), lambda cb, k: (0, cb))
    xr, xi = pl.pallas_call(
        body, name=name, grid=(n_w, n_t), in_specs=[a_blk, a_blk, re_blk, im_blk],
        out_specs=[pl.BlockSpec((tc, w), lambda cb, k: (tmap(k), cb))] * 2,
        out_shape=[_out(t, s, F32), _out(t, s, F32)],
        scratch_shapes=[pltpu.VMEM((SUBLANES, w), F32), pltpu.VMEM((SUBLANES, w), F32)],
        compiler_params=_params(("parallel", "arbitrary")),
    )(ar, ai, u, u)
    return xr, xi


def _ssm_da(name, lr, li, xr, xi, reverse):
    t, s = xr.shape
    w = _pick(s, 512)
    tc = _pick(t, 256, SUBLANES)
    n_t, n_w = t // tc, s // w

    def body(lr_ref, li_ref, xr_ref, xi_ref, dar_ref, dai_ref, pr_ref, pi_ref):
        @pl.when(pl.program_id(1) == 0)
        def _():
            pr_ref[...] = jnp.zeros_like(pr_ref)
            pi_ref[...] = jnp.zeros_like(pi_ref)
            dar_ref[...] = jnp.zeros_like(dar_ref)
            dai_ref[...] = jnp.zeros_like(dai_ref)

        row = lax.broadcasted_iota(jnp.int32, (tc, w), 0)
        xrv, xiv = xr_ref[...], xi_ref[...]
        if reverse:
            keep, shift, edge = row < tc - 1, tc - 1, 0
        else:
            keep, shift, edge = row >= 1, 1, tc - 1
        xsr = jnp.where(keep, pltpu.roll(xrv, shift, 0), pr_ref[pl.ds(0, 1), :])
        xsi = jnp.where(keep, pltpu.roll(xiv, shift, 0), pi_ref[pl.ds(0, 1), :])
        lrv, liv = lr_ref[...], li_ref[...]
        dar_ref[...] += jnp.sum(lrv * xsr + liv * xsi, axis=0, keepdims=True)
        dai_ref[...] += jnp.sum(liv * xsr - lrv * xsi, axis=0, keepdims=True)
        pr_ref[pl.ds(0, 1), :] = xr_ref[pl.ds(edge, 1), :]
        pi_ref[pl.ds(0, 1), :] = xi_ref[pl.ds(edge, 1), :]

    def tmap(k):
        return n_t - 1 - k if reverse else k

    blk = pl.BlockSpec((tc, w), lambda cb, k: (tmap(k), cb))
    vec = pl.BlockSpec((1, w), lambda cb, k: (0, cb))
    return pl.pallas_call(
        body, name=name, grid=(n_w, n_t), in_specs=[blk] * 4, out_specs=[vec, vec],
        out_shape=[_out(1, s, F32), _out(1, s, F32)],
        scratch_shapes=[pltpu.VMEM((SUBLANES, w), F32), pltpu.VMEM((SUBLANES, w), F32)],
        compiler_params=_params(("parallel", "arbitrary")),
    )(lr, li, xr, xi)


def _colsum_prod(name, a, b, b_coff=0):
    t, n = a.shape
    tm = _pick(t, 512, SUBLANES)

    def body(a_ref, b_ref, o_ref):
        @pl.when(pl.program_id(0) == 0)
        def _():
            o_ref[...] = jnp.zeros_like(o_ref)

        o_ref[...] += jnp.sum(a_ref[...].astype(F32) * b_ref[...].astype(F32), axis=0, keepdims=True)

    return pl.pallas_call(
        body, name=name, grid=(t // tm,),
        in_specs=[pl.BlockSpec((tm, n), lambda i: (i, 0)), pl.BlockSpec((tm, n), lambda i: (i, b_coff))],
        out_specs=pl.BlockSpec((1, n), lambda i: (0, 0)), out_shape=_out(1, n, F32),
        compiler_params=_params(("arbitrary",)),
    )(a, b)


def _bd_in(bb, g, p, hh):
    blk = bb.reshape(g, p, hh).transpose(0, 2, 1)
    eye = jnp.eye(g, dtype=bool)[:, None, :, None]
    return jnp.where(eye, blk[:, :, None, :], 0.0).reshape(g * hh, g * p)


def _bd_out(cc, g, p, hh):
    blk = cc.transpose(0, 2, 1)
    eye = jnp.eye(g, dtype=bool)[:, None, :, None]
    return jnp.where(eye, blk[:, :, None, :], 0.0).reshape(g * p, g * hh)


def _diag_in(dmat, g, p, hh):
    eye = jnp.eye(g, dtype=bool)[:, None, :, None]
    diag = jnp.sum(jnp.where(eye, dmat.reshape(g, hh, g, p), 0.0), axis=2)
    return diag.transpose(0, 2, 1).reshape(g * p, hh)


def _diag_out(dmat, g, p, hh):
    eye = jnp.eye(g, dtype=bool)[:, None, :, None]
    diag = jnp.sum(jnp.where(eye, dmat.reshape(g, p, g, hh), 0.0), axis=2)
    return diag.transpose(0, 2, 1)


def _softmax(qh, kh, scale):
    s = lax.dot_general(qh, kh, _DIMS["nt"], preferred_element_type=F32) * scale
    e = jnp.exp(s - jnp.max(s, axis=-1, keepdims=True))
    return e / jnp.sum(e, axis=-1, keepdims=True)


def _attn_fwd(q, kv):
    t, d = q.shape
    mm_ = kv.shape[0]
    hd = d // N_XHEADS
    scale = 1.0 / math.sqrt(hd)
    tm = _pick(t, 512, SUBLANES)

    def body(q_ref, kv_ref, o_ref):
        for h in range(N_XHEADS):
            sl = pl.ds(h * hd, hd)
            p = _softmax(q_ref[:, sl], kv_ref[:, sl], scale)
            o_ref[:, sl] = jnp.dot(p.astype(BF16), kv_ref[:, pl.ds(d + h * hd, hd)],
                                   preferred_element_type=F32).astype(BF16)

    return pl.pallas_call(
        body, name="attn_fwd", grid=(t // tm,),
        in_specs=[pl.BlockSpec((tm, d), lambda i: (i, 0)), pl.BlockSpec((mm_, 2 * d), lambda i: (0, 0))],
        out_specs=pl.BlockSpec((tm, d), lambda i: (i, 0)), out_shape=_out(t, d, BF16),
        compiler_params=_params(("parallel",)),
    )(q, kv)


def _attn_bwd(q, kv, do):
    t, d = q.shape
    mm_ = kv.shape[0]
    hd = d // N_XHEADS
    scale = 1.0 / math.sqrt(hd)
    tm = _pick(t, 512, SUBLANES)

    def body(q_ref, kv_ref, do_ref, dq_ref, dkv_ref):
        @pl.when(pl.program_id(0) == 0)
        def _():
            dkv_ref[...] = jnp.zeros_like(dkv_ref)

        for h in range(N_XHEADS):
            sl = pl.ds(h * hd, hd)
            vsl = pl.ds(d + h * hd, hd)
            qh, kh, doh = q_ref[:, sl], kv_ref[:, sl], do_ref[:, sl]
            p = _softmax(qh, kh, scale)
            dp = lax.dot_general(doh, kv_ref[:, vsl], _DIMS["nt"], preferred_element_type=F32)
            dkv_ref[:, vsl] += lax.dot_general(p.astype(BF16), doh, _DIMS["tn"], preferred_element_type=F32)
            ds = (p * (dp - jnp.sum(dp * p, axis=-1, keepdims=True)) * scale).astype(BF16)
            dq_ref[:, sl] = jnp.dot(ds, kh, preferred_element_type=F32).astype(BF16)
            dkv_ref[:, sl] += lax.dot_general(ds, qh, _DIMS["tn"], preferred_element_type=F32)

    row = pl.BlockSpec((tm, d), lambda i: (i, 0))
    full = pl.BlockSpec((mm_, 2 * d), lambda i: (0, 0))
    return pl.pallas_call(
        body, name="attn_bwd", grid=(t // tm,), in_specs=[row, full, row], out_specs=[row, full],
        out_shape=[_out(t, d, BF16), _out(mm_, 2 * d, F32)], compiler_params=_params(("arbitrary",)),
    )(q, kv, do)


def _ew(name, fn, ins, outs, rows_pref=256):
    r, c = ins[0].shape
    tr = _pick(r, rows_pref, SUBLANES)
    ni = len(ins)

    def body(*refs):
        res = fn(*[x[...] for x in refs[:ni]])
        for o_ref, v in zip(refs[ni:], res):
            o_ref[...] = v.astype(o_ref.dtype)

    blk = pl.BlockSpec((tr, c), lambda i: (i, 0))
    return pl.pallas_call(
        body, name=name, grid=(r // tr,), in_specs=[blk] * ni, out_specs=[blk] * len(outs),
        out_shape=[_out(r, c, dt) for dt in outs], compiler_params=_params(("parallel",)),
    )(*ins)


def _sum_slots(name, a, dtype):
    s, r, c = a.shape
    tr = _pick(r, 256, SUBLANES)

    def body(a_ref, o_ref):
        acc = a_ref[0].astype(F32)
        for k in range(1, s):
            acc = acc + a_ref[k].astype(F32)
        o_ref[...] = acc.astype(o_ref.dtype)

    return pl.pallas_call(
        body, name=name, grid=(r // tr,), in_specs=[pl.BlockSpec((s, tr, c), lambda i: (0, i, 0))],
        out_specs=pl.BlockSpec((tr, c), lambda i: (i, 0)), out_shape=_out(r, c, dtype),
        compiler_params=_params(("parallel",)),
    )(a)


def _adamw(name, w, g, m, v):
    bc1 = 1.0 - ADAM_B1 ** ADAM_STEP
    bc2 = 1.0 - ADAM_B2 ** ADAM_STEP

    def fn(wv, gv, mv, vv):
        m2 = ADAM_B1 * mv + (1.0 - ADAM_B1) * gv
        v2 = ADAM_B2 * vv + (1.0 - ADAM_B2) * (gv * gv)
        delta = -ADAM_LR * ((m2 / bc1) / (jnp.sqrt(v2 / bc2) + ADAM_EPS) + ADAM_WD * wv)
        return delta, m2, v2

    return _ew(name, fn, [w, g, m, v], [F32, F32, F32])


def _allgather(name, arrs):
    n = len(arrs)

    def body(*refs):
        ins, outs = refs[:n], refs[n:2 * n]
        send_sems, recv_sems, local_sems = refs[2 * n:]
        x, y, c = lax.axis_index("x"), lax.axis_index("y"), lax.axis_index("c")
        me, sibling = (x, y, c), (x, y, 1 - c)
        chips = [(1 - x, y), (x, 1 - y), (1 - x, 1 - y)]

        def rows(a, px, py, pc):
            r = ins[a].shape[0]
            return outs[a].at[pl.ds((4 * px + 2 * py + pc) * r, r), :]

        def copy(a, k, block, to, src=None):
            return pltpu.make_async_remote_copy(
                src_ref=rows(a, *block) if src is None else src, dst_ref=rows(a, *block),
                send_sem=send_sems.at[a, k], recv_sem=recv_sems.at[a, k], device_id=to, device_id_type=MESH)

        mine = [pltpu.make_async_copy(ins[a], rows(a, *me), local_sems.at[a]) for a in range(n)]
        for cp in mine:
            cp.start()
        first = []
        for a in range(n):
            first.append(copy(a, 0, me, sibling, src=ins[a]))
            first += [copy(a, 1 + j, me, (*chip, c), src=ins[a]) for j, chip in enumerate(chips)]
        for cp in first:
            cp.start()
        passed = []
        for j, chip in enumerate(chips):
            for a in range(n):
                copy(a, 1 + j, (*chip, c), me).wait_recv()
                cp = copy(a, 4 + j, (*chip, c), sibling)
                cp.start()
                passed.append(cp)
        for a in range(n):
            copy(a, 0, sibling, me).wait_recv()
            for j, chip in enumerate(chips):
                copy(a, 4 + j, (*chip, 1 - c), me).wait_recv()
        for cp in first + passed:
            cp.wait_send()
        for cp in mine:
            cp.wait()

    return pl.pallas_call(
        body, name=name, in_specs=[ANY] * n, out_specs=[ANY] * n,
        out_shape=[_out(N_DEV * a.shape[0], a.shape[1], a.dtype) for a in arrs],
        scratch_shapes=[pltpu.SemaphoreType.DMA((n, 7)), pltpu.SemaphoreType.DMA((n, 7)), pltpu.SemaphoreType.DMA((n,))],
    )(*arrs)


def _exchange_cores(name, g):
    _, r, c = g.shape
    nck = r // GRAD_ROW_TILE

    def body(g_ref, recv_ref, send_sems, recv_sems):
        x, y, cc = lax.axis_index("x"), lax.axis_index("y"), lax.axis_index("c")
        copies = []
        for q in range(4):
            for k in range(nck):
                rows = pl.ds(k * GRAD_ROW_TILE, GRAD_ROW_TILE)
                copies.append(pltpu.make_async_remote_copy(
                    src_ref=g_ref.at[2 * q + (1 - cc), rows], dst_ref=recv_ref.at[q, rows],
                    send_sem=send_sems.at[q, k], recv_sem=recv_sems.at[q, k], device_id=(x, y, 1 - cc),
                    device_id_type=MESH))
        for cp in copies:
            cp.start()
        for cp in copies:
            cp.wait()

    return pl.pallas_call(
        body, name=name, in_specs=[ANY], out_specs=ANY,
        out_shape=jax.ShapeDtypeStruct((4, r, c), g.dtype),
        scratch_shapes=[pltpu.SemaphoreType.DMA((4, nck)), pltpu.SemaphoreType.DMA((4, nck))],
    )(g)


def _pair_sum(name, g, recv, core):
    _, r, c = g.shape
    tr = GRAD_ROW_TILE

    def body(core_ref, g_ref, r_ref, o_ref):
        o_ref[...] = (g_ref[...].astype(F32) + r_ref[...].astype(F32)).astype(o_ref.dtype)

    blk = pl.BlockSpec((None, tr, c), lambda q, i, core_ref: (q, i, 0))
    return pl.pallas_call(
        body, name=name,
        grid_spec=pltpu.PrefetchScalarGridSpec(
            num_scalar_prefetch=1, grid=(4, r // tr),
            in_specs=[pl.BlockSpec((None, tr, c), lambda q, i, core_ref: (2 * q + core_ref[0], i, 0)), blk],
            out_specs=blk),
        out_shape=jax.ShapeDtypeStruct((4, r, c), g.dtype), compiler_params=_params(("parallel", "parallel")),
    )(core, g, recv)


def _peer(k, x, y, c):
    return (1 - x if k & 4 else x, 1 - y if k & 2 else y, 1 - c if k & 1 else c)


def _split_start(name, srcs, land_shapes, n_remote, n_local, build, after=None):
    ns, nl = len(srcs), len(land_shapes)
    n_sem = 3 if n_local else 2
    pins = [] if after is None else [after]

    def body(*refs):
        src_refs, land_refs = refs[:ns], refs[ns:ns + nl]
        sems = refs[ns + nl + len(pins):ns + nl + len(pins) + n_sem]
        token = refs[-1]
        remote, local = build(src_refs, land_refs, *sems)
        for cp in local + remote:
            cp.start()
        token[...] = jnp.zeros_like(token)

    sem_shapes = [pltpu.SemaphoreType.DMA((n_remote,)), pltpu.SemaphoreType.DMA((n_remote,))]
    if n_local:
        sem_shapes.append(pltpu.SemaphoreType.DMA((n_local,)))
    bufs = [pltpu.with_memory_space_constraint(a, pltpu.HBM) for a in srcs]
    bufs += [pltpu.with_memory_space_constraint(lax.empty(s.shape, s.dtype), pltpu.HBM) for s in land_shapes]
    outs = pl.pallas_call(
        body, name=name,
        out_shape=sem_shapes + [pltpu.HBM(b.shape, b.dtype) for b in bufs] + [jax.ShapeDtypeStruct((SUBLANES, LANES), F32)],
        in_specs=[HBM] * (ns + nl) + [ANY] * len(pins),
        out_specs=[SEM] * n_sem + [HBM] * (ns + nl) + [pl.BlockSpec(memory_space=pltpu.VMEM)],
        input_output_aliases={i: n_sem + i for i in range(ns + nl)},
        compiler_params=pltpu.CompilerParams(has_side_effects=SIDE_EFFECT),
    )(*bufs, *pins)
    return dict(sems=list(outs[:n_sem]), bufs=list(outs[n_sem:n_sem + ns + nl]), token=outs[-1], build=build, ns=ns)


def _split_wait(name, started, after):
    ns, n_buf, n_sem = started["ns"], len(started["bufs"]), len(started["sems"])

    def body(*refs):
        src_refs, land_refs = refs[:ns], refs[ns:n_buf]
        sems = refs[n_buf:n_buf + n_sem]
        remote, local = started["build"](src_refs, land_refs, *sems)
        for cp in local:
            cp.wait()
        for cp in remote:
            cp.wait_send()
            cp.wait_recv()

    outs = pl.pallas_call(
        body, name=name, out_shape=[pltpu.HBM(b.shape, b.dtype) for b in started["bufs"]],
        in_specs=[HBM] * n_buf + [SEM] * n_sem + [ANY], out_specs=[HBM] * n_buf,
        input_output_aliases={i: i for i in range(n_buf)},
        compiler_params=pltpu.CompilerParams(has_side_effects=SIDE_EFFECT),
    )(*started["bufs"], *started["sems"], after)
    return list(outs[ns:])


def _gather_start(name, shards, after):
    m = len(shards)

    def build(src_refs, land_refs, send_sems, recv_sems, local_sems):
        x, y, c = lax.axis_index("x"), lax.axis_index("y"), lax.axis_index("c")
        remote, local = [], []
        for j in range(m):
            r = src_refs[j].shape[0]
            dst = land_refs[j].at[pl.ds((4 * x + 2 * y + c) * r, r), :]
            local.append(pltpu.make_async_copy(src_refs[j], dst, local_sems.at[j]))
            for k in range(1, N_DEV):
                remote.append(pltpu.make_async_remote_copy(
                    src_ref=src_refs[j], dst_ref=dst, send_sem=send_sems.at[7 * j + k - 1],
                    recv_sem=recv_sems.at[7 * j + k - 1], device_id=_peer(k, x, y, c), device_id_type=MESH))
        return remote, local

    lands = [jax.ShapeDtypeStruct((N_DEV * a.shape[0], a.shape[1]), a.dtype) for a in shards]
    return _split_start(name, shards, lands, 7 * m, m, build, after)


def _slots_start(name, a):
    def build(src_refs, land_refs, send_sems, recv_sems, local_sems):
        x, y, c = lax.axis_index("x"), lax.axis_index("y"), lax.axis_index("c")
        dst = land_refs[0].at[4 * x + 2 * y + c]
        local = [pltpu.make_async_copy(src_refs[0], dst, local_sems.at[0])]
        remote = [pltpu.make_async_remote_copy(
            src_ref=src_refs[0], dst_ref=dst, send_sem=send_sems.at[k - 1], recv_sem=recv_sems.at[k - 1],
            device_id=_peer(k, x, y, c), device_id_type=MESH) for k in range(1, N_DEV)]
        return remote, local

    return _split_start(name, [a], [jax.ShapeDtypeStruct((N_DEV,) + a.shape, a.dtype)], 7, 1, build)


def _chips_start(name, p):
    _, r, c = p.shape
    nck = r // GRAD_ROW_TILE

    def build(src_refs, land_refs, send_sems, recv_sems):
        x, y, cc = lax.axis_index("x"), lax.axis_index("y"), lax.axis_index("c")
        remote = []
        for k in range(1, 4):
            px = 1 - x if k >> 1 else x
            py = 1 - y if k & 1 else y
            for j in range(nck):
                rows = pl.ds(j * GRAD_ROW_TILE, GRAD_ROW_TILE)
                remote.append(pltpu.make_async_remote_copy(
                    src_ref=src_refs[0].at[2 * px + py, rows], dst_ref=land_refs[0].at[k - 1, rows],
                    send_sem=send_sems.at[(k - 1) * nck + j], recv_sem=recv_sems.at[(k - 1) * nck + j],
                    device_id=(px, py, cc), device_id_type=MESH))
        return remote, []

    return _split_start(name, [p], [jax.ShapeDtypeStruct((3, r, c), p.dtype)], 3 * nck, 0, build)


def _chip_sum(name, p, recv, chip):
    _, r, c = p.shape
    tr = GRAD_ROW_TILE

    def body(chip_ref, p_ref, r_ref, o_ref):
        acc = p_ref[...].astype(F32)
        for k in range(3):
            acc = acc + r_ref[k].astype(F32)
        o_ref[...] = acc

    return pl.pallas_call(
        body, name=name,
        grid_spec=pltpu.PrefetchScalarGridSpec(
            num_scalar_prefetch=1, grid=(r // tr,),
            in_specs=[pl.BlockSpec((None, tr, c), lambda i, chip_ref: (chip_ref[0], i, 0)),
                      pl.BlockSpec((3, tr, c), lambda i, chip_ref: (0, i, 0))],
            out_specs=pl.BlockSpec((tr, c), lambda i, chip_ref: (i, 0))),
        out_shape=_out(r, c, F32), compiler_params=_params(("parallel",)),
    )(chip, p, recv)


def _local_step(x, mem, tgt, wt, sm, ev=None):
    t, d = x.shape
    n_mem = mem.shape[0]
    d_pool = sm["pool_scale"].shape[1]
    ng, pc = sm["pool_w"].shape[0], sm["pool_w"].shape[1]
    d_ssm = sm["ssm_d"].shape[1]
    _, sg, sp, sh = sm["ssm_b_re"].shape
    n_state = sg * sp
    gb, gs = {}, {}

    def emit(name, **kw):
        return ev(name, **kw) if ev is not None else None

    h1, ffn1_saved = _ffn_fwd("ffn1", x, sm["ffn1_norm"], wt["ffn1_w_gate"], wt["ffn1_w_up"], wt["ffn1_w_down"])
    emit("ffn1_fwd_done", marker=h1)
    u = _rms_fwd("mix_norm", h1, sm["mix_norm"])
    d_in = wt["w_in"].shape[0]
    tm, tn = _pick(t, 1024), _pick(d_in, 1408)
    proj = _mm1("in_proj", "nt", u, wt["w_in"], t, d_in, tm, tn, F32)
    off_s = d_pool // d_ssm
    off_gp = (d_pool + d_ssm)
    off_gs = off_gp + d

    pool_w_bf = sm["pool_w"].astype(BF16)
    pooled, pm = _pool_fwd(proj, pool_w_bf, sm["pool_scale"])

    cols = [sm["ssm_a_re"].reshape(-1, 1), sm["ssm_a_im"].reshape(-1, 1),
            jnp.broadcast_to(sm["ssm_log_dt"][:, :, None], (2, sg, sp)).reshape(-1, 1),
            sm["ssm_b_re"].reshape(-1, sh), sm["ssm_b_im"].reshape(-1, sh)]
    abr, abi, bbr, bbi = _ssm_disc(cols)
    abr2, abi2 = abr.reshape(2, n_state), abi.reshape(2, n_state)
    bbr4, bbi4 = bbr.reshape(2, sg * sp, sh), bbi.reshape(2, sg * sp, sh)
    b_re = [_bd_in(bbr4[dr], sg, sp, sh).astype(BF16) for dr in range(2)]
    b_im = [_bd_in(bbi4[dr], sg, sp, sh).astype(BF16) for dr in range(2)]
    c_re = [_bd_out(sm["ssm_c_re"][dr], sg, sp, sh).astype(BF16) for dr in range(2)]
    c_im = [_bd_out(-sm["ssm_c_im"][dr], sg, sp, sh).astype(BF16) for dr in range(2)]
    tms = _pick(t, 512)
    s_bf = _ew("ssm_cast", lambda v: (v,), [proj[:, d_pool:d_pool + d_ssm]], [BF16])[0]
    xs = []
    for dr in range(2):
        u_d = _mm1(f"ssm_in{dr}", "nn", s_bf, jnp.concatenate([b_re[dr], b_im[dr]], axis=1), t, 2 * n_state, tms,
                   _pick(2 * n_state, 512), F32)
        xs.append(_scan(f"ssm_scan{dr}", u_d, abr2[dr:dr + 1], abi2[dr:dr + 1], reverse=(dr == 1), conj=False))
    tmy = _pick(t, 256)
    x_list = [xs[0][0], xs[0][1], xs[1][0], xs[1][1]]
    y = _mm("ssm_out", "nn", x_list, [c_re[0], c_im[0], c_re[1], c_im[1]], [[(k, k) for k in range(4)]], t, d_ssm, tmy,
            d_ssm, [(proj, _tile(tmy, d_ssm, off_s)), (sm["ssm_d"], _rowvec(d_ssm))],
            lambda accs, sv, dv: (sv * dv + accs[0],), [(_out(t, d_ssm, F32), None)])[0]
    ys = _ew("ssm_gelu", lambda v: (jax.nn.gelu(v),), [y], [BF16])[0]
    emit("mix_in_done", marker=ys)

    tmm, tnm, tnx = _pick(t, 1024), _pick(d, 256), _pick(d, 512)
    gp_spec = _tile(tmm, tnm, off_gp // tnm)
    gs_spec = _tile(tmm, tnm, off_gs // tnm)

    def merge_epi(accs, gpv, gsv):
        z_pool, val, gate = accs
        return (jax.nn.sigmoid(gpv) * z_pool + jax.nn.sigmoid(gsv) * (val * jax.nn.sigmoid(gate)),)

    merged = _mm("mix_merge", "nt", [pm, ys], [wt["w_pool_proj"], wt["w_glu_val"], wt["w_glu_gate"]],
                 [[(0, 0)], [(1, 1)], [(1, 2)]], t, d, tmm, tnm, [(proj, gp_spec), (proj, gs_spec)], merge_epi,
                 [(_out(t, d, BF16), None)])[0]
    res_epi = lambda accs, hin: (hin + accs[0],)
    h2 = _mm("mix_out", "nn", [merged], [wt["w_mix_out"]], [[(0, 0)]], t, d, tmm, tnx, [(h1, _tile(tmm, tnx))],
             res_epi, [(_out(t, d, F32), None)])[0]

    un = _rms_fwd("xattn_norm", h2, sm["xattn_norm"])
    mn = _rms_fwd("mem_norm", mem, sm["mem_norm"])
    q = _mm1("xattn_q", "nn", un, wt["w_q"], t, d, tmm, tnx, BF16)
    kv = _mm1("xattn_kv", "nt", mn, wt["w_kv"], n_mem, 2 * d, n_mem, _pick(2 * d, 512), BF16)
    o = _attn_fwd(q, kv)
    h3 = _mm("xattn_out", "nn", [o], [wt["w_xo"]], [[(0, 0)]], t, d, tmm, tnx, [(h2, _tile(tmm, tnx))],
             res_epi, [(_out(t, d, F32), None)])[0]

    h4, ffn2_saved = _ffn_fwd("ffn2", h3, sm["ffn2_norm"], wt["ffn2_w_gate"], wt["ffn2_w_up"], wt["ffn2_w_down"])

    dh4, dh4_bf, gs["final_norm"], loss = _loss_head(h4, sm["final_norm"], tgt)
    dh3, dh3_bf, gs["ffn2_norm"], gb["ffn2_w_gate"], gb["ffn2_w_up"], gb["ffn2_w_down"] = _ffn_bwd(
        "ffn2", h3, sm["ffn2_norm"], wt["ffn2_w_gate"], wt["ffn2_w_up"], wt["ffn2_w_down"], ffn2_saved, dh4, dh4_bf)

    tw = _pick(d, 1024)
    do = _mm1("xattn_do", "nt", dh3_bf, wt["w_xo"], t, d, tmm, tnx, BF16)
    gb["w_xo"] = _mm1("xattn_dwxo", "tn", o, dh3_bf, d, d, tw, tnx, BF16)
    dq, dkv = _attn_bwd(q, kv, do)
    gb["w_q"] = _mm1("xattn_dwq", "tn", un, dq, d, d, tw, tnx, BF16)
    dun = _mm1("xattn_dun", "nt", dq, wt["w_q"], t, d, tmm, tnx, F32)
    dh2, dh2_bf, gs["xattn_norm"] = _rms_bwd("xattn_norm_bwd", h2, sm["xattn_norm"], dun, dh3)
    gb["w_kv"] = _mm1("xattn_dwkv", "tn", dkv, mn, 2 * d, d, _pick(2 * d, 512), d, BF16)
    dmn = _mm1("xattn_dmn", "nn", dkv, wt["w_kv"], n_mem, d, n_mem, tnx, F32)
    gs["mem_norm"] = _rms_bwd("mem_norm_bwd", mem, sm["mem_norm"], dmn)

    gb["w_mix_out"] = _mm1("mix_dwout", "tn", merged, dh2_bf, d, d, tw, tnx, BF16)

    def merge_bwd_epi(accs, gpv, gsv):
        dmerged, z_pool, val, gate = accs
        sp_, ss_, sg_ = jax.nn.sigmoid(gpv), jax.nn.sigmoid(gsv), jax.nn.sigmoid(gate)
        glu = val * sg_
        dz_pool = dmerged * sp_
        dg_pool = dmerged * z_pool * (sp_ * (1.0 - sp_))
        dz_ssm = dmerged * ss_
        dg_ssm = dmerged * glu * (ss_ * (1.0 - ss_))
        dval = dz_ssm * sg_
        dgate = dz_ssm * glu * (1.0 - sg_)
        return dz_pool, dg_pool, dg_ssm, dval, dgate

    dz_pool, dg_pool, dg_ssm, dval, dgate = _mm(
        "mix_merge_bwd", "nt", [dh2_bf, pm, ys], [wt["w_mix_out"], wt["w_pool_proj"], wt["w_glu_val"], wt["w_glu_gate"]],
        [[(0, 0)], [(1, 1)], [(2, 2)], [(2, 3)]], t, d, tmm, tnm, [(proj, gp_spec), (proj, gs_spec)], merge_bwd_epi,
        [(_out(t, d, BF16), None)] * 5)
    gb["w_pool_proj"] = _mm1("pool_dwproj", "tn", dz_pool, pm, d, d_pool, tw, d_pool, BF16)
    gb["w_glu_val"] = _mm1("glu_dwval", "tn", dval, ys, d, d_ssm, tw, d_ssm, BF16)
    gb["w_glu_gate"] = _mm1("glu_dwgate", "tn", dgate, ys, d, d_ssm, tw, d_ssm, BF16)

    def gelu_bwd_epi(accs, yv):
        _, vjp = jax.vjp(jax.nn.gelu, yv)
        return (vjp(accs[0])[0],)

    dy = _mm("glu_dy", "nn", [dval, dgate], [wt["w_glu_val"], wt["w_glu_gate"]], [[(0, 0), (1, 1)]], t, d_ssm, tmy, d_ssm,
             [(y, _tile(tmy, d_ssm))], gelu_bwd_epi, [(_out(t, d_ssm, F32), None)])[0]
    gs["ssm_d"] = _colsum_prod("ssm_dd", dy, proj, b_coff=off_s)
    dy_bf = _ew("ssm_dy_cast", lambda v: (v,), [dy], [BF16])[0]
    d_abr, d_abi, d_bbr, d_bbi, d_cre, d_cim, lams = [], [], [], [], [], [], []
    ts = _pick(n_state, 512)
    tc_ = _pick(n_state, 256)
    for dr in range(2):
        gx = _mm1(f"ssm_gx{dr}", "nt", dy_bf, jnp.concatenate([c_re[dr], c_im[dr]], axis=0), t, 2 * n_state, tms,
                  _pick(2 * n_state, 512), F32)
        lr, li = _scan(f"ssm_adj{dr}", gx, abr2[dr:dr + 1], abi2[dr:dr + 1], reverse=(dr == 0), conj=True)
        dar, dai = _ssm_da(f"ssm_da{dr}", lr, li, xs[dr][0], xs[dr][1], reverse=(dr == 1))
        d_abr.append(dar)
        d_abi.append(dai)
        lams += [lr, li]
        d_bbr.append(_diag_in(_mm1(f"ssm_dbre{dr}", "tn", s_bf, lr, d_ssm, n_state, d_ssm, ts, F32), sg, sp, sh))
        d_bbi.append(_diag_in(_mm1(f"ssm_dbim{dr}", "tn", s_bf, li, d_ssm, n_state, d_ssm, ts, F32), sg, sp, sh))
        d_cre.append(_diag_out(_mm1(f"ssm_dcre{dr}", "tn", xs[dr][0], dy_bf, n_state, d_ssm, tc_, d_ssm, F32), sg, sp, sh))
        d_cim.append(-_diag_out(_mm1(f"ssm_dcim{dr}", "tn", xs[dr][1], dy_bf, n_state, d_ssm, tc_, d_ssm, F32), sg, sp, sh))
    ds = _mm("ssm_ds", "nt", lams, [b_re[0], b_im[0], b_re[1], b_im[1]], [[(k, k) for k in range(4)]], t, d_ssm, tmy,
             d_ssm, [(dy, _tile(tmy, d_ssm)), (sm["ssm_d"], _rowvec(d_ssm))],
             lambda accs, dyv, dv: (dyv * dv + accs[0],), [(_out(t, d_ssm, BF16), None)])[0]
    cots = [jnp.concatenate(d_abr, axis=0).reshape(-1, 1), jnp.concatenate(d_abi, axis=0).reshape(-1, 1),
            jnp.concatenate(d_bbr, axis=0), jnp.concatenate(d_bbi, axis=0)]
    d_are, d_aim, d_ldt, d_bre, d_bim = _ssm_disc_bwd(cols, cots)
    gs["ssm_a_re"] = d_are.reshape(2, sg, sp)
    gs["ssm_a_im"] = d_aim.reshape(2, sg, sp)
    gs["ssm_log_dt"] = _rowsum("ssm_dlogdt", d_ldt.reshape(2 * sg, sp)).reshape(2, sg)
    gs["ssm_b_re"] = d_bre.reshape(2, sg, sp, sh)
    gs["ssm_b_im"] = d_bim.reshape(2, sg, sp, sh)
    gs["ssm_c_re"] = jnp.stack(d_cre, axis=0)
    gs["ssm_c_im"] = jnp.stack(d_cim, axis=0)

    dpm = _mm1("pool_dpm", "nn", dz_pool, wt["w_pool_proj"], t, d_pool, tmm, _pick(d_pool, 256), F32)
    dp, gs["pool_w"], gs["pool_scale"] = _pool_bwd(pooled, dpm, pool_w_bf, sm["pool_scale"])

    w_in = wt["w_in"]
    parts = [(dp, 0, d_pool), (ds, d_pool, d_ssm), (dg_pool, off_gp, d), (dg_ssm, off_gs, d)]
    w_in_parts = [w_in[o0:o0 + width] for _, o0, width in parts]
    gb["w_in"] = jnp.concatenate(
        [_mm1(f"in_proj_dw{k}", "tn", p_[0], u, p_[2], d, _pick(p_[2], 1024), tnx, BF16) for k, p_ in enumerate(parts)], axis=0)
    pin = emit("grads_main", gb=gb)
    du = _mm("in_proj_du", "nn", [p_[0] for p_ in parts], w_in_parts, [[(k, k) for k in range(4)]], t, d, tmm, tnx, [],
             lambda accs: (accs[0],), [(_out(t, d, F32), None)], after=pin)[0]
    dh1, dh1_bf, gs["mix_norm"] = _rms_bwd("mix_norm_bwd", h1, sm["mix_norm"], du, dh2)
    pin = emit("small_early", gs=gs, loss=loss)

    def ffn1_weights_done(d_wg, d_wu, d_wd):
        gb["ffn1_w_gate"], gb["ffn1_w_up"], gb["ffn1_w_down"] = d_wg, d_wu, d_wd
        return emit("grads_ffn1", gb=gb)

    dx, _, gs["ffn1_norm"], _, _, _ = _ffn_bwd(
        "ffn1", x, sm["ffn1_norm"], wt["ffn1_w_gate"], wt["ffn1_w_up"], wt["ffn1_w_down"], ffn1_saved, dh1, dh1_bf,
        weights_done=ffn1_weights_done, after=pin)
    return loss, dx, gb, gs


WEIGHTS = ["ffn1_norm", "ffn1_w_gate", "ffn1_w_up", "ffn1_w_down", "mix_norm", "w_in", "pool_w", "pool_scale",
           "w_pool_proj", "ssm_a_re", "ssm_a_im", "ssm_log_dt", "ssm_b_re", "ssm_b_im", "ssm_c_re", "ssm_c_im", "ssm_d",
           "w_glu_val", "w_glu_gate", "w_mix_out", "xattn_norm", "mem_norm", "w_q", "w_kv", "w_xo", "ffn2_norm",
           "ffn2_w_gate", "ffn2_w_up", "ffn2_w_down", "final_norm"]
COL_SHARDED = ["ffn1_w_gate", "ffn1_w_up", "w_in", "w_pool_proj", "w_glu_val", "w_glu_gate", "w_kv", "ffn2_w_gate",
               "ffn2_w_up"]
ROW_SHARDED = ["ffn1_w_down", "w_mix_out", "w_q", "w_xo", "ffn2_w_down"]
BIG = [n for n in WEIGHTS if n in COL_SHARDED or n in ROW_SHARDED]
SMALL = [n for n in WEIGHTS if n not in BIG]
FFN1_BIG = ["ffn1_w_gate", "ffn1_w_up", "ffn1_w_down"]
MAIN_BIG = [n for n in BIG if n not in FFN1_BIG]
LATE_SMALL = "ffn1_norm"
EARLY_SMALL = [n for n in SMALL if n != LATE_SMALL]
PACK_ROWS = SUBLANES * LANES
GRAD_ROW_TILE = 256


def _to_rows(name, w, width):
    if name in COL_SHARDED:
        w = w.T
    return w.reshape(-1, width)


def _from_rows(name, rows, shard_shape):
    if name in COL_SHARDED:
        return rows.reshape(shard_shape[1], shard_shape[0]).T
    return rows.reshape(shard_shape)


def _pack_small(vals):
    flat = []
    for v in vals:
        f = v.reshape(-1)
        flat.append(jnp.pad(f, (0, (-f.shape[0]) % PACK_ROWS)))
    total = sum(f.shape[0] for f in flat)
    flat.append(jnp.zeros(((-total) % (GRAD_ROW_TILE * LANES),), F32))
    return jnp.concatenate(flat).reshape(-1, LANES)


def _unpack_small(packed, shapes):
    out, row = [], 0
    for shp in shapes:
        size = math.prod(shp)
        rows = -(-size // PACK_ROWS) * SUBLANES
        out.append(packed[row:row + rows].reshape(-1)[:size].reshape(shp))
        row += rows
    return out


def kernel(x, mem, ffn1_norm, ffn1_w_gate, ffn1_w_up, ffn1_w_down, mix_norm, w_in, pool_w, pool_scale, w_pool_proj, ssm_a_re, ssm_a_im, ssm_log_dt, ssm_b_re, ssm_b_im, ssm_c_re, ssm_c_im, ssm_d, w_glu_val, w_glu_gate, w_mix_out, xattn_norm, mem_norm, w_q, w_kv, w_xo, ffn2_norm, ffn2_w_gate, ffn2_w_up, ffn2_w_down, final_norm, loss_target, m_ffn1_norm, m_ffn1_w_gate, m_ffn1_w_up, m_ffn1_w_down, m_mix_norm, m_w_in, m_pool_w, m_pool_scale, m_w_pool_proj, m_ssm_a_re, m_ssm_a_im, m_ssm_log_dt, m_ssm_b_re, m_ssm_b_im, m_ssm_c_re, m_ssm_c_im, m_ssm_d, m_w_glu_val, m_w_glu_gate, m_w_mix_out, m_xattn_norm, m_mem_norm, m_w_q, m_w_kv, m_w_xo, m_ffn2_norm, m_ffn2_w_gate, m_ffn2_w_up, m_ffn2_w_down, m_final_norm, v_ffn1_norm, v_ffn1_w_gate, v_ffn1_w_up, v_ffn1_w_down, v_mix_norm, v_w_in, v_pool_w, v_pool_scale, v_w_pool_proj, v_ssm_a_re, v_ssm_a_im, v_ssm_log_dt, v_ssm_b_re, v_ssm_b_im, v_ssm_c_re, v_ssm_c_im, v_ssm_d, v_w_glu_val, v_w_glu_gate, v_w_mix_out, v_xattn_norm, v_mem_norm, v_w_q, v_w_kv, v_w_xo, v_ffn2_norm, v_ffn2_w_gate, v_ffn2_w_up, v_ffn2_w_down, v_final_norm):
    given = dict(locals())
    wts = {n: given[n] for n in WEIGHTS}
    moms = {n: (given["m_" + n], given["v_" + n]) for n in WEIGHTS}
    x2, mem2, tgt2 = x[0], mem[0], loss_target[0]
    d = x2.shape[1]
    core = lax.axis_index("c").astype(jnp.int32).reshape(1)
    chip = (2 * lax.axis_index("x") + lax.axis_index("y")).astype(jnp.int32).reshape(1)

    def full_form(n, f):
        shard = wts[n][0].shape
        return f.reshape(N_DEV * shard[1], shard[0]) if n in COL_SHARDED else f.reshape(N_DEV * shard[0], shard[1])

    shards = {n: _to_rows(n, wts[n][0], d).astype(BF16) for n in BIG}
    wt = {n: full_form(n, f) for n, f in zip(FFN1_BIG, _allgather("weight_allgather_ffn1", [shards[n] for n in FFN1_BIG]))}
    rest = [n for n in MAIN_BIG if n != "w_in"]
    gather_in = _gather_start("weight_gather_in_start", [shards["w_in"]], wt[FFN1_BIG[0]])
    gather_rest = _gather_start("weight_gather_rest_start", [shards[n] for n in rest], gather_in["token"])
    sm = {n: (wts[n].reshape(1, -1) if wts[n].ndim <= 2 else wts[n][0]) for n in SMALL}
    sm["ffn1_norm"] = sm["ffn1_norm"] + (gather_in["token"][0, 0] + gather_rest["token"][0, 0])

    pending = {}

    def reduce_start(tag, names, gb):
        blocks = [gb[n].reshape(N_DEV, -1, d) for n in names]
        pad_rows = (-sum(b.shape[1] for b in blocks)) % GRAD_ROW_TILE
        packed = jnp.concatenate(blocks + ([jnp.zeros((N_DEV, pad_rows, d), BF16)] if pad_rows else []), axis=1)
        pair = _pair_sum("grad_pair_sum_" + tag, packed, _exchange_cores("grad_exchange_cores_" + tag, packed), core)
        pending[tag] = (pair, _chips_start("grad_exchange_chips_start_" + tag, pair), [b.shape[1] for b in blocks])
        return pending[tag][1]["token"]

    def reduce_finish(tag, after):
        pair, started, rows = pending[tag]
        recv = _split_wait("grad_exchange_chips_wait_" + tag, started, after)[0]
        return _chip_sum("grad_chip_sum_" + tag, pair, recv, chip), rows

    def ev(name, gb=None, gs=None, loss=None, marker=None):
        if name == "ffn1_fwd_done":
            wt["w_in"] = full_form("w_in", _split_wait("weight_gather_in_wait", gather_in, marker)[0])
        elif name == "mix_in_done":
            for n, f in zip(rest, _split_wait("weight_gather_rest_wait", gather_rest, marker)):
                wt[n] = full_form(n, f)
        elif name == "grads_main":
            return reduce_start("main", MAIN_BIG, gb)
        elif name == "small_early":
            pending["small"] = _slots_start("small_gather_start", _pack_small([gs[n] for n in EARLY_SMALL] + [loss[:, :1]]))
            return pending["small"]["token"]
        elif name == "grads_ffn1":
            return reduce_start("ffn1", FFN1_BIG, gb)
        return None

    _, dx, _, gs = _local_step(x2, mem2, tgt2, wt, sm, ev)

    out_g, out_d, out_m, out_v = {}, {}, {}, {}

    def update_big(names, g_rows, rows):
        off = 0
        for n, r in zip(names, rows):
            shard = wts[n].shape
            g_full = _from_rows(n, g_rows[off:off + r], shard[1:]).reshape(shard)
            off += r
            two_d = (-1, shard[-1])
            dl, m2, v2 = _adamw("adamw_" + n, wts[n].reshape(two_d), g_full.reshape(two_d), moms[n][0].reshape(two_d),
                                moms[n][1].reshape(two_d))
            out_g[n], out_d[n], out_m[n], out_v[n] = g_full, dl.reshape(shard), m2.reshape(shard), v2.reshape(shard)
        return dl

    last = update_big(MAIN_BIG, *reduce_finish("main", dx))

    small_sum = _sum_slots("small_sum", _split_wait("small_gather_wait", pending["small"], dx)[0], F32)
    late = _allgather("small_allgather_late", [gs[LATE_SMALL].reshape(-1, LANES)])[0]
    late_sum = _sum_slots("small_sum_late", late.reshape(N_DEV, -1, LANES), F32)
    zero = jnp.zeros((1, 1), F32)
    shapes = [wts[n].shape for n in EARLY_SMALL] + [(1, 1)]
    packs = [_pack_small([src[n] for n in EARLY_SMALL] + [zero])
             for src in (wts, {n: moms[n][0] for n in SMALL}, {n: moms[n][1] for n in SMALL})]
    dl, m2, v2 = _adamw("adamw_small", packs[0], small_sum, packs[1], packs[2])
    for dst, src in ((out_g, small_sum), (out_d, dl), (out_m, m2), (out_v, v2)):
        vals = _unpack_small(src, shapes)
        for n, val in zip(EARLY_SMALL, vals):
            dst[n] = val
        if dst is out_g:
            total_loss = vals[-1].reshape(())
    shp = wts[LATE_SMALL].shape
    dl, m2, v2 = _adamw("adamw_" + LATE_SMALL, wts[LATE_SMALL].reshape(-1, LANES), late_sum,
                        moms[LATE_SMALL][0].reshape(-1, LANES), moms[LATE_SMALL][1].reshape(-1, LANES))
    for dst, src in ((out_g, late_sum), (out_d, dl), (out_m, m2), (out_v, v2)):
        dst[LATE_SMALL] = src.reshape(shp)

    update_big(FFN1_BIG, *reduce_finish("ffn1", last))

    return (total_loss, dx[None], *[out_g[n] for n in WEIGHTS], *[out_d[n] for n in WEIGHTS],
            *[out_m[n] for n in WEIGHTS], *[out_v[n] for n in WEIGHTS])
```

```python
import functools
import math

import jax
import jax.numpy as jnp
from jax import lax
from jax.experimental import pallas as pl
from jax.experimental.pallas import tpu as pltpu

F32 = jnp.float32
BF16 = jnp.bfloat16
EPS = 1e-6
N_XHEADS = 4
POOL_WINDOWS = (2, 4, 8, 16)
ADAM_LR = 0.001
ADAM_B1 = 0.9
ADAM_B2 = 0.999
ADAM_EPS = 1e-08
ADAM_WD = 0.01
ADAM_STEP = 10
N_DEV = 8
VMEM_LIMIT_V7X = 48 * 1024 * 1024
LANES = 128
SUBLANES = 8
SUB_ROWS = 256
POOL_PAD = 16
MESH = pl.DeviceIdType.MESH
ANY = pl.BlockSpec(memory_space=pl.ANY)
HBM = pl.BlockSpec(memory_space=pltpu.HBM)
SEM = pl.BlockSpec(memory_space=pltpu.SEMAPHORE)
SIDE_EFFECT = pltpu.SideEffectType.DATAFLOW_SIDE_EFFECTING

_DIMS = {
    "nt": (((1,), (1,)), ((), ())),
    "nn": (((1,), (0,)), ((), ())),
    "tn": (((0,), (0,)), ((), ())),
}


def _pick(dim, pref, mult=LANES):
    if dim <= pref:
        return dim
    for t in range(pref - pref % mult, 0, -mult):
        if dim % t == 0:
            return t
    return dim


def _params(sem):
    return pltpu.CompilerParams(dimension_semantics=sem, vmem_limit_bytes=VMEM_LIMIT_V7X)


def _tile(tm, tn, coff=0):
    return pl.BlockSpec((tm, tn), lambda i, j: (i, j + coff))


def _rowvec(tn, coff=0):
    return pl.BlockSpec((1, tn), lambda i, j: (0, j + coff))


def _out(m, n, dtype):
    return jax.ShapeDtypeStruct((m, n), dtype)


def _mm(name, form, a_list, b_list, groups, m, n, tm, tn, extras, epilogue, outs, after=None, sub=SUB_ROWS):
    na, nb, ne = len(a_list), len(b_list), len(extras)
    pins = [] if after is None else [after]
    step = tm if (sub is None or form == "tn" or tm % sub) else sub

    def a_spec(a):
        if form == "tn":
            return pl.BlockSpec((a.shape[0], tm), lambda i, j: (0, i))
        return pl.BlockSpec((tm, a.shape[1]), lambda i, j: (i, 0))

    def b_spec(b):
        if form == "nt":
            return pl.BlockSpec((tn, b.shape[1]), lambda i, j: (j, 0))
        return pl.BlockSpec((b.shape[0], tn), lambda i, j: (0, j))

    def body(*refs):
        a_refs, b_refs = refs[:na], refs[na:na + nb]
        e_refs, o_refs = refs[na + nb:na + nb + ne], refs[na + nb + ne + len(pins):]
        b_vals = {}
        for s0 in range(0, tm, step):
            rows = slice(None) if step == tm else pl.ds(s0, step)
            a_vals, accs = {}, []
            for group in groups:
                acc = None
                for ai, bi in group:
                    if ai not in a_vals:
                        a_vals[ai] = (a_refs[ai][...] if form == "tn" else a_refs[ai][rows, :]).astype(BF16)
                    if bi not in b_vals:
                        b_vals[bi] = b_refs[bi][...].astype(BF16)
                    d = lax.dot_general(a_vals[ai], b_vals[bi], _DIMS[form], preferred_element_type=F32)
                    acc = d if acc is None else acc + d
                accs.append(acc)
            res = epilogue(accs, *[e[rows, :] if e.shape[0] == tm else e[...] for e in e_refs])
            for o_ref, r in zip(o_refs, res):
                o_ref[rows, :] = r.astype(o_ref.dtype)

    out_specs = [_tile(tm, tn) if s is None else s for _, s in outs]
    res = pl.pallas_call(
        body, name=name, grid=(m // tm, n // tn),
        in_specs=[a_spec(a) for a in a_list] + [b_spec(b) for b in b_list] + [s for _, s in extras] + [ANY] * len(pins),
        out_specs=out_specs, out_shape=[o for o, _ in outs],
        compiler_params=_params(("parallel", "parallel")),
    )(*a_list, *b_list, *[e for e, _ in extras], *pins)
    return res


def _mm1(name, form, a, b, m, n, tm, tn, dtype, scale=None):
    epi = (lambda accs: (accs[0],)) if scale is None else (lambda accs: (accs[0] * scale,))
    return _mm(name, form, [a], [b], [[(0, 0)]], m, n, tm, tn, [], epi, [(_out(m, n, dtype), None)])[0]


def _rms_fwd(name, h, g):
    t, d = h.shape
    tm = _pick(t, 512, SUBLANES)

    def body(h_ref, g_ref, n_ref):
        hv = h_ref[...]
        r = lax.rsqrt(jnp.mean(hv * hv, axis=-1, keepdims=True) + EPS)
        n_ref[...] = ((hv * r) * g_ref[...]).astype(BF16)

    return pl.pallas_call(
        body, name=name, grid=(t // tm,),
        in_specs=[pl.BlockSpec((tm, d), lambda i: (i, 0)), pl.BlockSpec((1, d), lambda i: (0, 0))],
        out_specs=pl.BlockSpec((tm, d), lambda i: (i, 0)), out_shape=_out(t, d, BF16),
        compiler_params=_params(("parallel",)),
    )(h, g)


def _rms_bwd(name, h, g, dn, dres=None):
    t, d = h.shape
    tm = _pick(t, 512, SUBLANES)
    need_dh = dres is not None

    def body(*refs):
        if need_dh:
            h_ref, g_ref, dn_ref, dres_ref, dh_ref, dhb_ref, dg_ref = refs
        else:
            h_ref, g_ref, dn_ref, dg_ref = refs
        hv = h_ref[...]
        r = lax.rsqrt(jnp.mean(hv * hv, axis=-1, keepdims=True) + EPS)
        nh = hv * r
        dnv = dn_ref[...].astype(F32)

        @pl.when(pl.program_id(0) == 0)
        def _():
            dg_ref[...] = jnp.zeros_like(dg_ref)

        dg_ref[...] += jnp.sum(dnv * nh, axis=0, keepdims=True)
        if need_dh:
            dng = dnv * g_ref[...]
            dh = dres_ref[...] + r * (dng - nh * jnp.mean(dng * nh, axis=-1, keepdims=True))
            dh_ref[...] = dh
            dhb_ref[...] = dh.astype(BF16)

    row = pl.BlockSpec((tm, d), lambda i: (i, 0))
    vec = pl.BlockSpec((1, d), lambda i: (0, 0))
    if need_dh:
        return pl.pallas_call(
            body, name=name, grid=(t // tm,), in_specs=[row, vec, row, row], out_specs=[row, row, vec],
            out_shape=[_out(t, d, F32), _out(t, d, BF16), _out(1, d, F32)], compiler_params=_params(("arbitrary",)),
        )(h, g, dn, dres)
    return pl.pallas_call(
        body, name=name, grid=(t // tm,), in_specs=[row, vec, row], out_specs=vec,
        out_shape=_out(1, d, F32), compiler_params=_params(("arbitrary",)),
    )(h, g, dn)


def _loss_head(h, g, tgt):
    t, d = h.shape
    tm = _pick(t, 512, SUBLANES)

    def body(h_ref, g_ref, t_ref, dh_ref, dhb_ref, dg_ref, loss_ref):
        hv = h_ref[...]
        r = lax.rsqrt(jnp.mean(hv * hv, axis=-1, keepdims=True) + EPS)
        nh = hv * r
        err = nh * g_ref[...] - t_ref[...]

        @pl.when(pl.program_id(0) == 0)
        def _():
            dg_ref[...] = jnp.zeros_like(dg_ref)
            loss_ref[...] = jnp.zeros_like(loss_ref)

        per_row = jnp.mean(err * err, axis=-1, keepdims=True)
        loss_ref[...] += 0.5 * jnp.sum(per_row, axis=0, keepdims=True)
        dy = err * (1.0 / d)
        dg_ref[...] += jnp.sum(dy * nh, axis=0, keepdims=True)
        dng = dy * g_ref[...]
        dh = r * (dng - nh * jnp.mean(dng * nh, axis=-1, keepdims=True))
        dh_ref[...] = dh
        dhb_ref[...] = dh.astype(BF16)

    row = pl.BlockSpec((tm, d), lambda i: (i, 0))
    vec = pl.BlockSpec((1, d), lambda i: (0, 0))
    return pl.pallas_call(
        body, name="loss_head", grid=(t // tm,), in_specs=[row, vec, row],
        out_specs=[row, row, vec, pl.BlockSpec((1, LANES), lambda i: (0, 0))],
        out_shape=[_out(t, d, F32), _out(t, d, BF16), _out(1, d, F32), _out(1, LANES, F32)],
        compiler_params=_params(("arbitrary",)),
    )(h, g, tgt)


def _ffn_fwd(tag, h, n, wg_t, wu_t, wd):
    t, d = h.shape
    f = wd.shape[0]
    tm, tn = _pick(t, 1024), _pick(f, 1408)

    def up_epi(accs):
        a, b = accs
        return a, b, (a * jax.nn.sigmoid(a)) * b

    a, b, hid = _mm(tag + "_up", "nt", [n], [wg_t, wu_t], [[(0, 0)], [(0, 1)]], t, f, tm, tn, [], up_epi,
                    [(_out(t, f, BF16), None)] * 3)
    tm2, tn2 = _pick(t, 1024), _pick(d, 512)
    h_out = _mm(tag + "_down", "nn", [hid], [wd], [[(0, 0)]], t, d, tm2, tn2, [(h, _tile(tm2, tn2))],
                lambda accs, hin: (hin + 0.5 * accs[0],), [(_out(t, d, F32), None)])[0]
    return h_out, (n, a, b, hid)


def _ffn_bwd(tag, h, g, wg_t, wu_t, wd, saved, dh, dh_bf, weights_done=None, after=None):
    n, a, b, hid = saved
    t, d = h.shape
    f = wd.shape[0]
    tm, tn = _pick(t, 1024), _pick(f, 1408)

    def hid_epi(accs, av, bv):
        dhid = 0.5 * accs[0]
        av, bv = av.astype(F32), bv.astype(F32)
        sig = jax.nn.sigmoid(av)
        da = dhid * bv * (sig * (1.0 + av * (1.0 - sig)))
        db = dhid * (av * sig)
        return da, db

    da, db = _mm(tag + "_bwd_hid", "nt", [dh_bf], [wd], [[(0, 0)]], t, f, tm, tn,
                 [(a, _tile(tm, tn)), (b, _tile(tm, tn))], hid_epi, [(_out(t, f, BF16), None)] * 2, after=after)
    tw, tnw = _pick(f, 1408), _pick(d, 512)
    d_wd = _mm1(tag + "_dwd", "tn", hid, dh_bf, f, d, tw, tnw, BF16, scale=0.5)
    d_wg = _mm1(tag + "_dwg", "tn", da, n, f, d, tw, tnw, BF16)
    d_wu = _mm1(tag + "_dwu", "tn", db, n, f, d, tw, tnw, BF16)
    pin = weights_done(d_wg, d_wu, d_wd) if weights_done is not None else None
    tm2, tn2 = _pick(t, 1024), _pick(d, 512)
    dn = _mm(tag + "_dn", "nn", [da, db], [wg_t, wu_t], [[(0, 0), (1, 1)]], t, d, tm2, tn2, [],
             lambda accs: (accs[0],), [(_out(t, d, F32), None)], after=pin)[0]
    dh_in, dh_in_bf, dg = _rms_bwd(tag + "_norm_bwd", h, g, dn, dh)
    return dh_in, dh_in_bf, dg, d_wg, d_wu, d_wd


def _window_sum(win, offsets):
    n = win.shape[0]
    acc = None
    for j in offsets:
        term = win if j == 0 else pltpu.roll(win, (-j) % n, 0)
        acc = term if acc is None else acc + term
    return acc


def _pool_counts(r0, ch, c, left, right, t):
    pos = r0 + lax.broadcasted_iota(jnp.int32, (ch, c), 0)
    return (jnp.minimum(pos + right + 1, t) - jnp.maximum(pos - left, 0)).astype(F32)


def _pool_fwd(proj, pool_w_bf, pool_scale):
    t = proj.shape[0]
    ng, c, _ = pool_w_bf.shape
    ch = _pick(t, 256, SUBLANES)
    pad = POOL_PAD

    def body(p_ref, w_ref, s_ref, pooled_ref, pm_ref, buf):
        grp = pl.program_id(0)
        buf[pl.ds(0, pad), :] = jnp.zeros((pad, c), F32)
        buf[pl.ds(pad + t, pad), :] = jnp.zeros((pad, c), F32)

        def fill(ci, carry):
            r0 = pl.multiple_of(ci * ch, SUBLANES)
            buf[pl.ds(pl.multiple_of(r0 + pad, SUBLANES), ch), :] = p_ref[pl.ds(r0, ch), :]
            return carry

        lax.fori_loop(0, t // ch, fill, 0)
        for gi, w in enumerate(POOL_WINDOWS):
            left = w // 2
            right = w - 1 - left

            @pl.when(grp == gi)
            def _(left=left, right=right):
                def chunk(ci, carry):
                    r0 = pl.multiple_of(ci * ch, SUBLANES)
                    win = buf[pl.ds(r0, ch + 2 * pad), :]
                    s = _window_sum(win, range(-left, right + 1))[pad:pad + ch]
                    pooled = s / _pool_counts(r0, ch, c, left, right, t) - win[pad:pad + ch]
                    pooled_bf = pooled.astype(BF16)
                    mixed = jnp.dot(pooled_bf, w_ref[0], preferred_element_type=F32)
                    pooled_ref[pl.ds(r0, ch), :] = pooled_bf
                    pm_ref[pl.ds(r0, ch), :] = (mixed * s_ref[...]).astype(BF16)
                    return carry

                lax.fori_loop(0, t // ch, chunk, 0)

    col = pl.BlockSpec((t, c), lambda g: (0, g))
    return pl.pallas_call(
        body, name="pool_fwd", grid=(ng,),
        in_specs=[col, pl.BlockSpec((1, c, c), lambda g: (g, 0, 0)), pl.BlockSpec((1, c), lambda g: (0, g))],
        out_specs=[col, col], out_shape=[_out(t, ng * c, BF16), _out(t, ng * c, BF16)],
        scratch_shapes=[pltpu.VMEM((t + 2 * pad, c), F32)],
        compiler_params=_params(("parallel",)),
    )(proj, pool_w_bf, pool_scale)


def _pool_bwd(pooled, dpm, pool_w_bf, pool_scale):
    t = pooled.shape[0]
    ng, c, _ = pool_w_bf.shape
    ch = _pick(t, 256, SUBLANES)
    pad = POOL_PAD

    def body(pooled_ref, dpm_ref, w_ref, s_ref, dp_ref, dw_ref, ds_ref, buf, raw):
        grp = pl.program_id(0)
        buf[pl.ds(0, pad), :] = jnp.zeros((pad, c), F32)
        buf[pl.ds(pad + t, pad), :] = jnp.zeros((pad, c), F32)
        dw_ref[...] = jnp.zeros_like(dw_ref)
        ds_ref[...] = jnp.zeros_like(ds_ref)
        for gi, w in enumerate(POOL_WINDOWS):
            left = w // 2
            right = w - 1 - left

            @pl.when(grp == gi)
            def _(left=left, right=right):
                def first(ci, carry):
                    r0 = pl.multiple_of(ci * ch, SUBLANES)
                    pv = pooled_ref[pl.ds(r0, ch), :]
                    dpm_v = dpm_ref[pl.ds(r0, ch), :]
                    mixed = jnp.dot(pv, w_ref[0], preferred_element_type=F32)
                    ds_ref[...] += jnp.sum(dpm_v * mixed, axis=0, keepdims=True)
                    dmixed = (dpm_v * s_ref[...]).astype(BF16)
                    dw_ref[0] += lax.dot_general(pv, dmixed, _DIMS["tn"], preferred_element_type=F32)
                    dpooled = lax.dot_general(dmixed, w_ref[0], _DIMS["nt"], preferred_element_type=F32)
                    raw[pl.ds(r0, ch), :] = dpooled
                    buf[pl.ds(pl.multiple_of(r0 + pad, SUBLANES), ch), :] = (
                        dpooled / _pool_counts(r0, ch, c, left, right, t))
                    return carry

                lax.fori_loop(0, t // ch, first, 0)

                def second(ci, carry):
                    r0 = pl.multiple_of(ci * ch, SUBLANES)
                    win = buf[pl.ds(r0, ch + 2 * pad), :]
                    s = _window_sum(win, range(-right, left + 1))[pad:pad + ch]
                    dp_ref[pl.ds(r0, ch), :] = (s - raw[pl.ds(r0, ch), :]).astype(BF16)
                    return carry

                lax.fori_loop(0, t // ch, second, 0)

    col = pl.BlockSpec((t, c), lambda g: (0, g))
    return pl.pallas_call(
        body, name="pool_bwd", grid=(ng,),
        in_specs=[col, col, pl.BlockSpec((1, c, c), lambda g: (g, 0, 0)), pl.BlockSpec((1, c), lambda g: (0, g))],
        out_specs=[col, pl.BlockSpec((1, c, c), lambda g: (g, 0, 0)), pl.BlockSpec((1, c), lambda g: (0, g))],
        out_shape=[_out(t, ng * c, BF16), jax.ShapeDtypeStruct((ng, c, c), F32), _out(1, ng * c, F32)],
        scratch_shapes=[pltpu.VMEM((t + 2 * pad, c), F32), pltpu.VMEM((t, c), F32)],
        compiler_params=_params(("parallel",)),
    )(pooled, dpm, pool_w_bf, pool_scale)


def _discretise(a_re, a_im, log_dt, b_re, b_im):
    dt = jnp.exp(log_dt)
    mag = jnp.exp(dt * a_re)
    ang = dt * a_im
    abr = mag * jnp.cos(ang)
    abi = mag * jnp.sin(ang)
    den = a_re * a_re + a_im * a_im
    nr = abr - 1.0
    qr = (nr * a_re + abi * a_im) / den
    qi = (abi * a_re - nr * a_im) / den
    return abr, abi, qr * b_re - qi * b_im, qr * b_im + qi * b_re


def _ssm_disc(cols):
    n, hh = cols[3].shape

    def body(ar, ai, ld, br, bi, o1, o2, o3, o4):
        res = _discretise(ar[...], ai[...], ld[...], br[...], bi[...])
        for o, r in zip((o1, o2, o3, o4), res):
            o[...] = r

    return pl.pallas_call(
        body, name="ssm_disc",
        out_shape=[_out(n, 1, F32), _out(n, 1, F32), _out(n, hh, F32), _out(n, hh, F32)],
    )(*cols)


def _ssm_disc_bwd(cols, cots):
    n, hh = cols[3].shape

    def body(ar, ai, ld, br, bi, c1, c2, c3, c4, o1, o2, o3, o4, o5):
        _, vjp = jax.vjp(_discretise, ar[...], ai[...], ld[...], br[...], bi[...])
        res = vjp((c1[...], c2[...], c3[...], c4[...]))
        for o, r in zip((o1, o2, o3, o4, o5), res):
            o[...] = r

    return pl.pallas_call(
        body, name="ssm_disc_bwd",
        out_shape=[_out(n, 1, F32)] * 3 + [_out(n, hh, F32)] * 2,
    )(*cols, *cots)


def _rowsum(name, a):
    r, _ = a.shape

    def body(a_ref, o_ref):
        o_ref[...] = jnp.sum(a_ref[...], axis=-1, keepdims=True)

    return pl.pallas_call(body, name=name, out_shape=_out(r, 1, F32))(a)


def _cmul(pr, pi, qr, qi):
    return pr * qr - pi * qi, pr * qi + pi * qr


def _scan(name, u, ar, ai, reverse, conj):
    t, s2 = u.shape
    s = s2 // 2
    w = _pick(s, 512)
    tc = _pick(t, 512, SUBLANES)
    n_t, n_w = t // tc, s // w
    groups = tc // SUBLANES
    last = 0 if reverse else SUBLANES - 1

    def body(ar_ref, ai_ref, ur_ref, ui_ref, xr_ref, xi_ref, cr_ref, ci_ref):
        @pl.when(pl.program_id(1) == 0)
        def _():
            cr_ref[...] = jnp.zeros_like(cr_ref)
            ci_ref[...] = jnp.zeros_like(ci_ref)

        a1r = ar_ref[...]
        a1i = -ai_ref[...] if conj else ai_ref[...]
        a2r, a2i = _cmul(a1r, a1i, a1r, a1i)
        a4r, a4i = _cmul(a2r, a2i, a2r, a2i)
        row = lax.broadcasted_iota(jnp.int32, (SUBLANES, w), 0)
        pwr = jnp.zeros((SUBLANES, w), F32)
        pwi = jnp.zeros((SUBLANES, w), F32)
        cur_r, cur_i = a1r, a1i
        for k in range(SUBLANES):
            rk = SUBLANES - 1 - k if reverse else k
            pwr = jnp.where(row == rk, cur_r, pwr)
            pwi = jnp.where(row == rk, cur_i, pwi)
            cur_r, cur_i = _cmul(cur_r, cur_i, a1r, a1i)
        steps = ((1, a1r, a1i), (2, a2r, a2i), (4, a4r, a4i))

        def one(i, carry):
            g = groups - 1 - i if reverse else i
            r0 = pl.multiple_of(g * SUBLANES, SUBLANES)
            br = ur_ref[pl.ds(r0, SUBLANES), :]
            bi = ui_ref[pl.ds(r0, SUBLANES), :]
            for dist, pr, pi in steps:
                if reverse:
                    keep = row < SUBLANES - dist
                    shift = SUBLANES - dist
                else:
                    keep = row >= dist
                    shift = dist
                sr = jnp.where(keep, pltpu.roll(br, shift, 0), 0.0)
                si = jnp.where(keep, pltpu.roll(bi, shift, 0), 0.0)
                br, bi = br + pr * sr - pi * si, bi + pr * si + pi * sr
            cr = cr_ref[pl.ds(last, 1), :]
            ci = ci_ref[pl.ds(last, 1), :]
            xr = br + pwr * cr - pwi * ci
            xi = bi + pwr * ci + pwi * cr
            xr_ref[pl.ds(r0, SUBLANES), :] = xr
            xi_ref[pl.ds(r0, SUBLANES), :] = xi
            cr_ref[...] = xr
            ci_ref[...] = xi
            return carry

        lax.fori_loop(0, groups, one, 0)

    def tmap(k):
        return n_t - 1 - k if reverse else k

    re_blk = pl.BlockSpec((tc, w), lambda cb, k: (tmap(k), cb))
    im_blk = pl.BlockSpec((tc, w), lambda cb, k: (tmap(k), cb + n_w))
    a_blk = pl.BlockSpec((1, w), lambda cb, k: (0, cb))
    xr, xi = pl.pallas_call(
        body, name=name, grid=(n_w, n_t), in_specs=[a_blk, a_blk, re_blk, im_blk],
        out_specs=[pl.BlockSpec((tc, w), lambda cb, k: (tmap(k), cb))] * 2,
        out_shape=[_out(t, s, F32), _out(t, s, F32)],
        scratch_shapes=[pltpu.VMEM((SUBLANES, w), F32), pltpu.VMEM((SUBLANES, w), F32)],
        compiler_params=_params(("parallel", "arbitrary")),
    )(ar, ai, u, u)
    return xr, xi


def _ssm_da(name, lr, li, xr, xi, reverse):
    t, s = xr.shape
    w = _pick(s, 512)
    tc = _pick(t, 256, SUBLANES)
    n_t, n_w = t // tc, s // w

    def body(lr_ref, li_ref, xr_ref, xi_ref, dar_ref, dai_ref, pr_ref, pi_ref):
        @pl.when(pl.program_id(1) == 0)
        def _():
            pr_ref[...] = jnp.zeros_like(pr_ref)
            pi_ref[...] = jnp.zeros_like(pi_ref)
            dar_ref[...] = jnp.zeros_like(dar_ref)
            dai_ref[...] = jnp.zeros_like(dai_ref)

        row = lax.broadcasted_iota(jnp.int32, (tc, w), 0)
        xrv, xiv = xr_ref[...], xi_ref[...]
        if reverse:
            keep, shift, edge = row < tc - 1, tc - 1, 0
        else:
            keep, shift, edge = row >= 1, 1, tc - 1
        xsr = jnp.where(keep, pltpu.roll(xrv, shift, 0), pr_ref[pl.ds(0, 1), :])
        xsi = jnp.where(keep, pltpu.roll(xiv, shift, 0), pi_ref[pl.ds(0, 1), :])
        lrv, liv = lr_ref[...], li_ref[...]
        dar_ref[...] += jnp.sum(lrv * xsr + liv * xsi, axis=0, keepdims=True)
        dai_ref[...] += jnp.sum(liv * xsr - lrv * xsi, axis=0, keepdims=True)
        pr_ref[pl.ds(0, 1), :] = xr_ref[pl.ds(edge, 1), :]
        pi_ref[pl.ds(0, 1), :] = xi_ref[pl.ds(edge, 1), :]

    def tmap(k):
        return n_t - 1 - k if reverse else k

    blk = pl.BlockSpec((tc, w), lambda cb, k: (tmap(k), cb))
    vec = pl.BlockSpec((1, w), lambda cb, k: (0, cb))
    return pl.pallas_call(
        body, name=name, grid=(n_w, n_t), in_specs=[blk] * 4, out_specs=[vec, vec],
        out_shape=[_out(1, s, F32), _out(1, s, F32)],
        scratch_shapes=[pltpu.VMEM((SUBLANES, w), F32), pltpu.VMEM((SUBLANES, w), F32)],
        compiler_params=_params(("parallel", "arbitrary")),
    )(lr, li, xr, xi)


def _colsum_prod(name, a, b, b_coff=0):
    t, n = a.shape
    tm = _pick(t, 512, SUBLANES)

    def body(a_ref, b_ref, o_ref):
        @pl.when(pl.program_id(0) == 0)
        def _():
            o_ref[...] = jnp.zeros_like(o_ref)

        o_ref[...] += jnp.sum(a_ref[...].astype(F32) * b_ref[...].astype(F32), axis=0, keepdims=True)

    return pl.pallas_call(
        body, name=name, grid=(t // tm,),
        in_specs=[pl.BlockSpec((tm, n), lambda i: (i, 0)), pl.BlockSpec((tm, n), lambda i: (i, b_coff))],
        out_specs=pl.BlockSpec((1, n), lambda i: (0, 0)), out_shape=_out(1, n, F32),
        compiler_params=_params(("arbitrary",)),
    )(a, b)


def _bd_in(bb, g, p, hh):
    blk = bb.reshape(g, p, hh).transpose(0, 2, 1)
    eye = jnp.eye(g, dtype=bool)[:, None, :, None]
    return jnp.where(eye, blk[:, :, None, :], 0.0).reshape(g * hh, g * p)


def _bd_out(cc, g, p, hh):
    blk = cc.transpose(0, 2, 1)
    eye = jnp.eye(g, dtype=bool)[:, None, :, None]
    return jnp.where(eye, blk[:, :, None, :], 0.0).reshape(g * p, g * hh)


def _diag_in(dmat, g, p, hh):
    eye = jnp.eye(g, dtype=bool)[:, None, :, None]
    diag = jnp.sum(jnp.where(eye, dmat.reshape(g, hh, g, p), 0.0), axis=2)
    return diag.transpose(0, 2, 1).reshape(g * p, hh)


def _diag_out(dmat, g, p, hh):
    eye = jnp.eye(g, dtype=bool)[:, None, :, None]
    diag = jnp.sum(jnp.where(eye, dmat.reshape(g, p, g, hh), 0.0), axis=2)
    return diag.transpose(0, 2, 1)


def _softmax(qh, kh, scale):
    s = lax.dot_general(qh, kh, _DIMS["nt"], preferred_element_type=F32) * scale
    e = jnp.exp(s - jnp.max(s, axis=-1, keepdims=True))
    return e / jnp.sum(e, axis=-1, keepdims=True)


def _attn_fwd(q, kv):
    t, d = q.shape
    mm_ = kv.shape[0]
    hd = d // N_XHEADS
    scale = 1.0 / math.sqrt(hd)
    tm = _pick(t, 512, SUBLANES)

    def body(q_ref, kv_ref, o_ref):
        for h in range(N_XHEADS):
            sl = pl.ds(h * hd, hd)
            p = _softmax(q_ref[:, sl], kv_ref[:, sl], scale)
            o_ref[:, sl] = jnp.dot(p.astype(BF16), kv_ref[:, pl.ds(d + h * hd, hd)],
                                   preferred_element_type=F32).astype(BF16)

    return pl.pallas_call(
        body, name="attn_fwd", grid=(t // tm,),
        in_specs=[pl.BlockSpec((tm, d), lambda i: (i, 0)), pl.BlockSpec((mm_, 2 * d), lambda i: (0, 0))],
        out_specs=pl.BlockSpec((tm, d), lambda i: (i, 0)), out_shape=_out(t, d, BF16),
        compiler_params=_params(("parallel",)),
    )(q, kv)


def _attn_bwd(q, kv, do):
    t, d = q.shape
    mm_ = kv.shape[0]
    hd = d // N_XHEADS
    scale = 1.0 / math.sqrt(hd)
    tm = _pick(t, 512, SUBLANES)

    def body(q_ref, kv_ref, do_ref, dq_ref, dkv_ref):
        @pl.when(pl.program_id(0) == 0)
        def _():
            dkv_ref[...] = jnp.zeros_like(dkv_ref)

        for h in range(N_XHEADS):
            sl = pl.ds(h * hd, hd)
            vsl = pl.ds(d + h * hd, hd)
            qh, kh, doh = q_ref[:, sl], kv_ref[:, sl], do_ref[:, sl]
            p = _softmax(qh, kh, scale)
            dp = lax.dot_general(doh, kv_ref[:, vsl], _DIMS["nt"], preferred_element_type=F32)
            dkv_ref[:, vsl] += lax.dot_general(p.astype(BF16), doh, _DIMS["tn"], preferred_element_type=F32)
            ds = (p * (dp - jnp.sum(dp * p, axis=-1, keepdims=True)) * scale).astype(BF16)
            dq_ref[:, sl] = jnp.dot(ds, kh, preferred_element_type=F32).astype(BF16)
            dkv_ref[:, sl] += lax.dot_general(ds, qh, _DIMS["tn"], preferred_element_type=F32)

    row = pl.BlockSpec((tm, d), lambda i: (i, 0))
    full = pl.BlockSpec((mm_, 2 * d), lambda i: (0, 0))
    return pl.pallas_call(
        body, name="attn_bwd", grid=(t // tm,), in_specs=[row, full, row], out_specs=[row, full],
        out_shape=[_out(t, d, BF16), _out(mm_, 2 * d, F32)], compiler_params=_params(("arbitrary",)),
    )(q, kv, do)


def _ew(name, fn, ins, outs, rows_pref=256):
    r, c = ins[0].shape
    tr = _pick(r, rows_pref, SUBLANES)
    ni = len(ins)

    def body(*refs):
        res = fn(*[x[...] for x in refs[:ni]])
        for o_ref, v in zip(refs[ni:], res):
            o_ref[...] = v.astype(o_ref.dtype)

    blk = pl.BlockSpec((tr, c), lambda i: (i, 0))
    return pl.pallas_call(
        body, name=name, grid=(r // tr,), in_specs=[blk] * ni, out_specs=[blk] * len(outs),
        out_shape=[_out(r, c, dt) for dt in outs], compiler_params=_params(("parallel",)),
    )(*ins)


def _sum_slots(name, a, dtype):
    s, r, c = a.shape
    tr = _pick(r, 256, SUBLANES)

    def body(a_ref, o_ref):
        acc = a_ref[0].astype(F32)
        for k in range(1, s):
            acc = acc + a_ref[k].astype(F32)
        o_ref[...] = acc.astype(o_ref.dtype)

    return pl.pallas_call(
        body, name=name, grid=(r // tr,), in_specs=[pl.BlockSpec((s, tr, c), lambda i: (0, i, 0))],
        out_specs=pl.BlockSpec((tr, c), lambda i: (i, 0)), out_shape=_out(r, c, dtype),
        compiler_params=_params(("parallel",)),
    )(a)


def _adamw(name, w, g, m, v):
    bc1 = 1.0 - ADAM_B1 ** ADAM_STEP
    bc2 = 1.0 - ADAM_B2 ** ADAM_STEP

    def fn(wv, gv, mv, vv):
        m2 = ADAM_B1 * mv + (1.0 - ADAM_B1) * gv
        v2 = ADAM_B2 * vv + (1.0 - ADAM_B2) * (gv * gv)
        delta = -ADAM_LR * ((m2 / bc1) / (jnp.sqrt(v2 / bc2) + ADAM_EPS) + ADAM_WD * wv)
        return delta, m2, v2

    return _ew(name, fn, [w, g, m, v], [F32, F32, F32])


def _allgather(name, arrs):
    n = len(arrs)

    def body(*refs):
        ins, outs = refs[:n], refs[n:2 * n]
        send_sems, recv_sems, local_sems = refs[2 * n:]
        x, y, c = lax.axis_index("x"), lax.axis_index("y"), lax.axis_index("c")
        me, sibling = (x, y, c), (x, y, 1 - c)
        chips = [(1 - x, y), (x, 1 - y), (1 - x, 1 - y)]

        def rows(a, px, py, pc):
            r = ins[a].shape[0]
            return outs[a].at[pl.ds((4 * px + 2 * py + pc) * r, r), :]

        def copy(a, k, block, to, src=None):
            return pltpu.make_async_remote_copy(
                src_ref=rows(a, *block) if src is None else src, dst_ref=rows(a, *block),
                send_sem=send_sems.at[a, k], recv_sem=recv_sems.at[a, k], device_id=to, device_id_type=MESH)

        mine = [pltpu.make_async_copy(ins[a], rows(a, *me), local_sems.at[a]) for a in range(n)]
        for cp in mine:
            cp.start()
        first = []
        for a in range(n):
            first.append(copy(a, 0, me, sibling, src=ins[a]))
            first += [copy(a, 1 + j, me, (*chip, c), src=ins[a]) for j, chip in enumerate(chips)]
        for cp in first:
            cp.start()
        passed = []
        for j, chip in enumerate(chips):
            for a in range(n):
                copy(a, 1 + j, (*chip, c), me).wait_recv()
                cp = copy(a, 4 + j, (*chip, c), sibling)
                cp.start()
                passed.append(cp)
        for a in range(n):
            copy(a, 0, sibling, me).wait_recv()
            for j, chip in enumerate(chips):
                copy(a, 4 + j, (*chip, 1 - c), me).wait_recv()
        for cp in first + passed:
            cp.wait_send()
        for cp in mine:
            cp.wait()

    return pl.pallas_call(
        body, name=name, in_specs=[ANY] * n, out_specs=[ANY] * n,
        out_shape=[_out(N_DEV * a.shape[0], a.shape[1], a.dtype) for a in arrs],
        scratch_shapes=[pltpu.SemaphoreType.DMA((n, 7)), pltpu.SemaphoreType.DMA((n, 7)), pltpu.SemaphoreType.DMA((n,))],
    )(*arrs)


def _exchange_cores(name, g):
    _, r, c = g.shape
    nck = r // GRAD_ROW_TILE

    def body(g_ref, recv_ref, send_sems, recv_sems):
        x, y, cc = lax.axis_index("x"), lax.axis_index("y"), lax.axis_index("c")
        copies = []
        for q in range(4):
            for k in range(nck):
                rows = pl.ds(k * GRAD_ROW_TILE, GRAD_ROW_TILE)
                copies.append(pltpu.make_async_remote_copy(
                    src_ref=g_ref.at[2 * q + (1 - cc), rows], dst_ref=recv_ref.at[q, rows],
                    send_sem=send_sems.at[q, k], recv_sem=recv_sems.at[q, k], device_id=(x, y, 1 - cc),
                    device_id_type=MESH))
        for cp in copies:
            cp.start()
        for cp in copies:
            cp.wait()

    return pl.pallas_call(
        body, name=name, in_specs=[ANY], out_specs=ANY,
        out_shape=jax.ShapeDtypeStruct((4, r, c), g.dtype),
        scratch_shapes=[pltpu.SemaphoreType.DMA((4, nck)), pltpu.SemaphoreType.DMA((4, nck))],
    )(g)


def _pair_sum(name, g, recv, core):
    _, r, c = g.shape
    tr = GRAD_ROW_TILE

    def body(core_ref, g_ref, r_ref, o_ref):
        o_ref[...] = (g_ref[...].astype(F32) + r_ref[...].astype(F32)).astype(o_ref.dtype)

    blk = pl.BlockSpec((None, tr, c), lambda q, i, core_ref: (q, i, 0))
    return pl.pallas_call(
        body, name=name,
        grid_spec=pltpu.PrefetchScalarGridSpec(
            num_scalar_prefetch=1, grid=(4, r // tr),
            in_specs=[pl.BlockSpec((None, tr, c), lambda q, i, core_ref: (2 * q + core_ref[0], i, 0)), blk],
            out_specs=blk),
        out_shape=jax.ShapeDtypeStruct((4, r, c), g.dtype), compiler_params=_params(("parallel", "parallel")),
    )(core, g, recv)


def _peer(k, x, y, c):
    return (1 - x if k & 4 else x, 1 - y if k & 2 else y, 1 - c if k & 1 else c)


def _split_start(name, srcs, land_shapes, n_remote, n_local, build, after=None):
    ns, nl = len(srcs), len(land_shapes)
    n_sem = 3 if n_local else 2
    pins = [] if after is None else [after]

    def body(*refs):
        src_refs, land_refs = refs[:ns], refs[ns:ns + nl]
        sems = refs[ns + nl + len(pins):ns + nl + len(pins) + n_sem]
        token = refs[-1]
        remote, local = build(src_refs, land_refs, *sems)
        for cp in local + remote:
            cp.start()
        token[...] = jnp.zeros_like(token)

    sem_shapes = [pltpu.SemaphoreType.DMA((n_remote,)), pltpu.SemaphoreType.DMA((n_remote,))]
    if n_local:
        sem_shapes.append(pltpu.SemaphoreType.DMA((n_local,)))
    bufs = [pltpu.with_memory_space_constraint(a, pltpu.HBM) for a in srcs]
    bufs += [pltpu.with_memory_space_constraint(lax.empty(s.shape, s.dtype), pltpu.HBM) for s in land_shapes]
    outs = pl.pallas_call(
        body, name=name,
        out_shape=sem_shapes + [pltpu.HBM(b.shape, b.dtype) for b in bufs] + [jax.ShapeDtypeStruct((SUBLANES, LANES), F32)],
        in_specs=[HBM] * (ns + nl) + [ANY] * len(pins),
        out_specs=[SEM] * n_sem + [HBM] * (ns + nl) + [pl.BlockSpec(memory_space=pltpu.VMEM)],
        input_output_aliases={i: n_sem + i for i in range(ns + nl)},
        compiler_params=pltpu.CompilerParams(has_side_effects=SIDE_EFFECT),
    )(*bufs, *pins)
    return dict(sems=list(outs[:n_sem]), bufs=list(outs[n_sem:n_sem + ns + nl]), token=outs[-1], build=build, ns=ns)


def _split_wait(name, started, after):
    ns, n_buf, n_sem = started["ns"], len(started["bufs"]), len(started["sems"])

    def body(*refs):
        src_refs, land_refs = refs[:ns], refs[ns:n_buf]
        sems = refs[n_buf:n_buf + n_sem]
        remote, local = started["build"](src_refs, land_refs, *sems)
        for cp in local:
            cp.wait()
        for cp in remote:
            cp.wait_send()
            cp.wait_recv()

    outs = pl.pallas_call(
        body, name=name, out_shape=[pltpu.HBM(b.shape, b.dtype) for b in started["bufs"]],
        in_specs=[HBM] * n_buf + [SEM] * n_sem + [ANY], out_specs=[HBM] * n_buf,
        input_output_aliases={i: i for i in range(n_buf)},
        compiler_params=pltpu.CompilerParams(has_side_effects=SIDE_EFFECT),
    )(*started["bufs"], *started["sems"], after)
    return list(outs[ns:])


def _gather_start(name, shards, after):
    m = len(shards)

    def build(src_refs, land_refs, send_sems, recv_sems, local_sems):
        x, y, c = lax.axis_index("x"), lax.axis_index("y"), lax.axis_index("c")
        remote, local = [], []
        for j in range(m):
            r = src_refs[j].shape[0]
            dst = land_refs[j].at[pl.ds((4 * x + 2 * y + c) * r, r), :]
            local.append(pltpu.make_async_copy(src_refs[j], dst, local_sems.at[j]))
            for k in range(1, N_DEV):
                remote.append(pltpu.make_async_remote_copy(
                    src_ref=src_refs[j], dst_ref=dst, send_sem=send_sems.at[7 * j + k - 1],
                    recv_sem=recv_sems.at[7 * j + k - 1], device_id=_peer(k, x, y, c), device_id_type=MESH))
        return remote, local

    lands = [jax.ShapeDtypeStruct((N_DEV * a.shape[0], a.shape[1]), a.dtype) for a in shards]
    return _split_start(name, shards, lands, 7 * m, m, build, after)


def _slots_start(name, a):
    def build(src_refs, land_refs, send_sems, recv_sems, local_sems):
        x, y, c = lax.axis_index("x"), lax.axis_index("y"), lax.axis_index("c")
        dst = land_refs[0].at[4 * x + 2 * y + c]
        local = [pltpu.make_async_copy(src_refs[0], dst, local_sems.at[0])]
        remote = [pltpu.make_async_remote_copy(
            src_ref=src_refs[0], dst_ref=dst, send_sem=send_sems.at[k - 1], recv_sem=recv_sems.at[k - 1],
            device_id=_peer(k, x, y, c), device_id_type=MESH) for k in range(1, N_DEV)]
        return remote, local

    return _split_start(name, [a], [jax.ShapeDtypeStruct((N_DEV,) + a.shape, a.dtype)], 7, 1, build)


def _chips_start(name, p):
    _, r, c = p.shape
    nck = r // GRAD_ROW_TILE

    def build(src_refs, land_refs, send_sems, recv_sems):
        x, y, cc = lax.axis_index("x"), lax.axis_index("y"), lax.axis_index("c")
        remote = []
        for k in range(1, 4):
            px = 1 - x if k >> 1 else x
            py = 1 - y if k & 1 else y
            for j in range(nck):
                rows = pl.ds(j * GRAD_ROW_TILE, GRAD_ROW_TILE)
                remote.append(pltpu.make_async_remote_copy(
                    src_ref=src_refs[0].at[2 * px + py, rows], dst_ref=land_refs[0].at[k - 1, rows],
                    send_sem=send_sems.at[(k - 1) * nck + j], recv_sem=recv_sems.at[(k - 1) * nck + j],
                    device_id=(px, py, cc), device_id_type=MESH))
        return remote, []

    return _split_start(name, [p], [jax.ShapeDtypeStruct((3, r, c), p.dtype)], 3 * nck, 0, build)


def _chip_sum(name, p, recv, chip):
    _, r, c = p.shape
    tr = GRAD_ROW_TILE

    def body(chip_ref, p_ref, r_ref, o_ref):
        acc = p_ref[...].astype(F32)
        for k in range(3):
            acc = acc + r_ref[k].astype(F32)
        o_ref[...] = acc

    return pl.pallas_call(
        body, name=name,
        grid_spec=pltpu.PrefetchScalarGridSpec(
            num_scalar_prefetch=1, grid=(r // tr,),
            in_specs=[pl.BlockSpec((None, tr, c), lambda i, chip_ref: (chip_ref[0], i, 0)),
                      pl.BlockSpec((3, tr, c), lambda i, chip_ref: (0, i, 0))],
            out_specs=pl.BlockSpec((tr, c), lambda i, chip_ref: (i, 0))),
        out_shape=_out(r, c, F32), compiler_params=_params(("parallel",)),
    )(chip, p, recv)


def _local_step(x, mem, tgt, wt, sm, ev=None):
    t, d = x.shape
    n_mem = mem.shape[0]
    d_pool = sm["pool_scale"].shape[1]
    ng, pc = sm["pool_w"].shape[0], sm["pool_w"].shape[1]
    d_ssm = sm["ssm_d"].shape[1]
    _, sg, sp, sh = sm["ssm_b_re"].shape
    n_state = sg * sp
    gb, gs = {}, {}

    def emit(name, **kw):
        return ev(name, **kw) if ev is not None else None

    n1 = _rms_fwd("ffn1_norm", x, sm["ffn1_norm"])
    emit("ffn1_norm_done", marker=n1)
    h1, ffn1_saved = _ffn_fwd("ffn1", x, n1, wt["ffn1_w_gate"], wt["ffn1_w_up"], wt["ffn1_w_down"])
    emit("ffn1_fwd_done", marker=h1)
    u = _rms_fwd("mix_norm", h1, sm["mix_norm"])
    d_in = wt["w_in"].shape[0]
    tm, tn = _pick(t, 1024), _pick(d_in, 1408)
    proj = _mm1("in_proj", "nt", u, wt["w_in"], t, d_in, tm, tn, F32)
    off_s = d_pool // d_ssm
    off_gp = (d_pool + d_ssm)
    off_gs = off_gp + d

    pool_w_bf = sm["pool_w"].astype(BF16)
    pooled, pm = _pool_fwd(proj, pool_w_bf, sm["pool_scale"])

    cols = [sm["ssm_a_re"].reshape(-1, 1), sm["ssm_a_im"].reshape(-1, 1),
            jnp.broadcast_to(sm["ssm_log_dt"][:, :, None], (2, sg, sp)).reshape(-1, 1),
            sm["ssm_b_re"].reshape(-1, sh), sm["ssm_b_im"].reshape(-1, sh)]
    abr, abi, bbr, bbi = _ssm_disc(cols)
    abr2, abi2 = abr.reshape(2, n_state), abi.reshape(2, n_state)
    bbr4, bbi4 = bbr.reshape(2, sg * sp, sh), bbi.reshape(2, sg * sp, sh)
    b_re = [_bd_in(bbr4[dr], sg, sp, sh).astype(BF16) for dr in range(2)]
    b_im = [_bd_in(bbi4[dr], sg, sp, sh).astype(BF16) for dr in range(2)]
    c_re = [_bd_out(sm["ssm_c_re"][dr], sg, sp, sh).astype(BF16) for dr in range(2)]
    c_im = [_bd_out(-sm["ssm_c_im"][dr], sg, sp, sh).astype(BF16) for dr in range(2)]
    tms = _pick(t, 512)
    s_bf = _ew("ssm_cast", lambda v: (v,), [proj[:, d_pool:d_pool + d_ssm]], [BF16])[0]
    xs = []
    for dr in range(2):
        u_d = _mm1(f"ssm_in{dr}", "nn", s_bf, jnp.concatenate([b_re[dr], b_im[dr]], axis=1), t, 2 * n_state, tms,
                   _pick(2 * n_state, 512), F32)
        xs.append(_scan(f"ssm_scan{dr}", u_d, abr2[dr:dr + 1], abi2[dr:dr + 1], reverse=(dr == 1), conj=False))
    tmy = _pick(t, 256)
    x_list = [xs[0][0], xs[0][1], xs[1][0], xs[1][1]]
    y = _mm("ssm_out", "nn", x_list, [c_re[0], c_im[0], c_re[1], c_im[1]], [[(k, k) for k in range(4)]], t, d_ssm, tmy,
            d_ssm, [(proj, _tile(tmy, d_ssm, off_s)), (sm["ssm_d"], _rowvec(d_ssm))],
            lambda accs, sv, dv: (sv * dv + accs[0],), [(_out(t, d_ssm, F32), None)])[0]
    ys = _ew("ssm_gelu", lambda v: (jax.nn.gelu(v),), [y], [BF16])[0]
    emit("mix_in_done", marker=ys)

    tmm, tnm, tnx = _pick(t, 1024), _pick(d, 256), _pick(d, 512)
    gp_spec = _tile(tmm, tnm, off_gp // tnm)
    gs_spec = _tile(tmm, tnm, off_gs // tnm)

    def merge_epi(accs, gpv, gsv):
        z_pool, val, gate = accs
        return (jax.nn.sigmoid(gpv) * z_pool + jax.nn.sigmoid(gsv) * (val * jax.nn.sigmoid(gate)),)

    merged = _mm("mix_merge", "nt", [pm, ys], [wt["w_pool_proj"], wt["w_glu_val"], wt["w_glu_gate"]],
                 [[(0, 0)], [(1, 1)], [(1, 2)]], t, d, tmm, tnm, [(proj, gp_spec), (proj, gs_spec)], merge_epi,
                 [(_out(t, d, BF16), None)])[0]
    res_epi = lambda accs, hin: (hin + accs[0],)
    h2 = _mm("mix_out", "nn", [merged], [wt["w_mix_out"]], [[(0, 0)]], t, d, tmm, tnx, [(h1, _tile(tmm, tnx))],
             res_epi, [(_out(t, d, F32), None)])[0]

    un = _rms_fwd("xattn_norm", h2, sm["xattn_norm"])
    mn = _rms_fwd("mem_norm", mem, sm["mem_norm"])
    q = _mm1("xattn_q", "nn", un, wt["w_q"], t, d, tmm, tnx, BF16)
    kv = _mm1("xattn_kv", "nt", mn, wt["w_kv"], n_mem, 2 * d, n_mem, _pick(2 * d, 512), BF16)
    o = _attn_fwd(q, kv)
    h3 = _mm("xattn_out", "nn", [o], [wt["w_xo"]], [[(0, 0)]], t, d, tmm, tnx, [(h2, _tile(tmm, tnx))],
             res_epi, [(_out(t, d, F32), None)])[0]

    n2 = _rms_fwd("ffn2_norm", h3, sm["ffn2_norm"])
    h4, ffn2_saved = _ffn_fwd("ffn2", h3, n2, wt["ffn2_w_gate"], wt["ffn2_w_up"], wt["ffn2_w_down"])

    dh4, dh4_bf, gs["final_norm"], loss = _loss_head(h4, sm["final_norm"], tgt)
    dh3, dh3_bf, gs["ffn2_norm"], gb["ffn2_w_gate"], gb["ffn2_w_up"], gb["ffn2_w_down"] = _ffn_bwd(
        "ffn2", h3, sm["ffn2_norm"], wt["ffn2_w_gate"], wt["ffn2_w_up"], wt["ffn2_w_down"], ffn2_saved, dh4, dh4_bf)

    tw = _pick(d, 1024)
    do = _mm1("xattn_do", "nt", dh3_bf, wt["w_xo"], t, d, tmm, tnx, BF16)
    gb["w_xo"] = _mm1("xattn_dwxo", "tn", o, dh3_bf, d, d, tw, tnx, BF16)
    dq, dkv = _attn_bwd(q, kv, do)
    gb["w_q"] = _mm1("xattn_dwq", "tn", un, dq, d, d, tw, tnx, BF16)
    dun = _mm1("xattn_dun", "nt", dq, wt["w_q"], t, d, tmm, tnx, F32)
    dh2, dh2_bf, gs["xattn_norm"] = _rms_bwd("xattn_norm_bwd", h2, sm["xattn_norm"], dun, dh3)
    gb["w_kv"] = _mm1("xattn_dwkv", "tn", dkv, mn, 2 * d, d, _pick(2 * d, 512), d, BF16)
    dmn = _mm1("xattn_dmn", "nn", dkv, wt["w_kv"], n_mem, d, n_mem, tnx, F32)
    gs["mem_norm"] = _rms_bwd("mem_norm_bwd", mem, sm["mem_norm"], dmn)

    gb["w_mix_out"] = _mm1("mix_dwout", "tn", merged, dh2_bf, d, d, tw, tnx, BF16)

    def merge_bwd_epi(accs, gpv, gsv):
        dmerged, z_pool, val, gate = accs
        sp_, ss_, sg_ = jax.nn.sigmoid(gpv), jax.nn.sigmoid(gsv), jax.nn.sigmoid(gate)
        glu = val * sg_
        dz_pool = dmerged * sp_
        dg_pool = dmerged * z_pool * (sp_ * (1.0 - sp_))
        dz_ssm = dmerged * ss_
        dg_ssm = dmerged * glu * (ss_ * (1.0 - ss_))
        dval = dz_ssm * sg_
        dgate = dz_ssm * glu * (1.0 - sg_)
        return dz_pool, dg_pool, dg_ssm, dval, dgate

    dz_pool, dg_pool, dg_ssm, dval, dgate = _mm(
        "mix_merge_bwd", "nt", [dh2_bf, pm, ys], [wt["w_mix_out"], wt["w_pool_proj"], wt["w_glu_val"], wt["w_glu_gate"]],
        [[(0, 0)], [(1, 1)], [(2, 2)], [(2, 3)]], t, d, tmm, tnm, [(proj, gp_spec), (proj, gs_spec)], merge_bwd_epi,
        [(_out(t, d, BF16), None)] * 5)
    gb["w_pool_proj"] = _mm1("pool_dwproj", "tn", dz_pool, pm, d, d_pool, tw, d_pool, BF16)
    gb["w_glu_val"] = _mm1("glu_dwval", "tn", dval, ys, d, d_ssm, tw, d_ssm, BF16)
    gb["w_glu_gate"] = _mm1("glu_dwgate", "tn", dgate, ys, d, d_ssm, tw, d_ssm, BF16)

    def gelu_bwd_epi(accs, yv):
        _, vjp = jax.vjp(jax.nn.gelu, yv)
        return (vjp(accs[0])[0],)

    dy = _mm("glu_dy", "nn", [dval, dgate], [wt["w_glu_val"], wt["w_glu_gate"]], [[(0, 0), (1, 1)]], t, d_ssm, tmy, d_ssm,
             [(y, _tile(tmy, d_ssm))], gelu_bwd_epi, [(_out(t, d_ssm, F32), None)])[0]
    gs["ssm_d"] = _colsum_prod("ssm_dd", dy, proj, b_coff=off_s)
    dy_bf = _ew("ssm_dy_cast", lambda v: (v,), [dy], [BF16])[0]
    d_abr, d_abi, d_bbr, d_bbi, d_cre, d_cim, lams = [], [], [], [], [], [], []
    ts = _pick(n_state, 512)
    tc_ = _pick(n_state, 256)
    for dr in range(2):
        gx = _mm1(f"ssm_gx{dr}", "nt", dy_bf, jnp.concatenate([c_re[dr], c_im[dr]], axis=0), t, 2 * n_state, tms,
                  _pick(2 * n_state, 512), F32)
        lr, li = _scan(f"ssm_adj{dr}", gx, abr2[dr:dr + 1], abi2[dr:dr + 1], reverse=(dr == 0), conj=True)
        dar, dai = _ssm_da(f"ssm_da{dr}", lr, li, xs[dr][0], xs[dr][1], reverse=(dr == 1))
        d_abr.append(dar)
        d_abi.append(dai)
        lams += [lr, li]
        d_bbr.append(_diag_in(_mm1(f"ssm_dbre{dr}", "tn", s_bf, lr, d_ssm, n_state, d_ssm, ts, F32), sg, sp, sh))
        d_bbi.append(_diag_in(_mm1(f"ssm_dbim{dr}", "tn", s_bf, li, d_ssm, n_state, d_ssm, ts, F32), sg, sp, sh))
        d_cre.append(_diag_out(_mm1(f"ssm_dcre{dr}", "tn", xs[dr][0], dy_bf, n_state, d_ssm, tc_, d_ssm, F32), sg, sp, sh))
        d_cim.append(-_diag_out(_mm1(f"ssm_dcim{dr}", "tn", xs[dr][1], dy_bf, n_state, d_ssm, tc_, d_ssm, F32), sg, sp, sh))
    ds = _mm("ssm_ds", "nt", lams, [b_re[0], b_im[0], b_re[1], b_im[1]], [[(k, k) for k in range(4)]], t, d_ssm, tmy,
             d_ssm, [(dy, _tile(tmy, d_ssm)), (sm["ssm_d"], _rowvec(d_ssm))],
             lambda accs, dyv, dv: (dyv * dv + accs[0],), [(_out(t, d_ssm, BF16), None)])[0]
    cots = [jnp.concatenate(d_abr, axis=0).reshape(-1, 1), jnp.concatenate(d_abi, axis=0).reshape(-1, 1),
            jnp.concatenate(d_bbr, axis=0), jnp.concatenate(d_bbi, axis=0)]
    d_are, d_aim, d_ldt, d_bre, d_bim = _ssm_disc_bwd(cols, cots)
    gs["ssm_a_re"] = d_are.reshape(2, sg, sp)
    gs["ssm_a_im"] = d_aim.reshape(2, sg, sp)
    gs["ssm_log_dt"] = _rowsum("ssm_dlogdt", d_ldt.reshape(2 * sg, sp)).reshape(2, sg)
    gs["ssm_b_re"] = d_bre.reshape(2, sg, sp, sh)
    gs["ssm_b_im"] = d_bim.reshape(2, sg, sp, sh)
    gs["ssm_c_re"] = jnp.stack(d_cre, axis=0)
    gs["ssm_c_im"] = jnp.stack(d_cim, axis=0)

    dpm = _mm1("pool_dpm", "nn", dz_pool, wt["w_pool_proj"], t, d_pool, tmm, _pick(d_pool, 256), F32)
    dp, gs["pool_w"], gs["pool_scale"] = _pool_bwd(pooled, dpm, pool_w_bf, sm["pool_scale"])

    w_in = wt["w_in"]
    parts = [(dp, 0, d_pool), (ds, d_pool, d_ssm), (dg_pool, off_gp, d), (dg_ssm, off_gs, d)]
    w_in_parts = [w_in[o0:o0 + width] for _, o0, width in parts]
    gb["w_in"] = jnp.concatenate(
        [_mm1(f"in_proj_dw{k}", "tn", p_[0], u, p_[2], d, _pick(p_[2], 1024), tnx, BF16) for k, p_ in enumerate(parts)], axis=0)
    pin = emit("grads_main", gb=gb)
    du = _mm("in_proj_du", "nn", [p_[0] for p_ in parts], w_in_parts, [[(k, k) for k in range(4)]], t, d, tmm, tnx, [],
             lambda accs: (accs[0],), [(_out(t, d, F32), None)], after=pin)[0]
    dh1, dh1_bf, gs["mix_norm"] = _rms_bwd("mix_norm_bwd", h1, sm["mix_norm"], du, dh2)
    pin = emit("small_early", gs=gs, loss=loss)

    def ffn1_weights_done(d_wg, d_wu, d_wd):
        gb["ffn1_w_gate"], gb["ffn1_w_up"], gb["ffn1_w_down"] = d_wg, d_wu, d_wd
        return emit("grads_ffn1", gb=gb)

    dx, _, gs["ffn1_norm"], _, _, _ = _ffn_bwd(
        "ffn1", x, sm["ffn1_norm"], wt["ffn1_w_gate"], wt["ffn1_w_up"], wt["ffn1_w_down"], ffn1_saved, dh1, dh1_bf,
        weights_done=ffn1_weights_done, after=pin)
    return loss, dx, gb, gs


WEIGHTS = ["ffn1_norm", "ffn1_w_gate", "ffn1_w_up", "ffn1_w_down", "mix_norm", "w_in", "pool_w", "pool_scale",
           "w_pool_proj", "ssm_a_re", "ssm_a_im", "ssm_log_dt", "ssm_b_re", "ssm_b_im", "ssm_c_re", "ssm_c_im", "ssm_d",
           "w_glu_val", "w_glu_gate", "w_mix_out", "xattn_norm", "mem_norm", "w_q", "w_kv", "w_xo", "ffn2_norm",
           "ffn2_w_gate", "ffn2_w_up", "ffn2_w_down", "final_norm"]
COL_SHARDED = ["ffn1_w_gate", "ffn1_w_up", "w_in", "w_pool_proj", "w_glu_val", "w_glu_gate", "w_kv", "ffn2_w_gate",
               "ffn2_w_up"]
ROW_SHARDED = ["ffn1_w_down", "w_mix_out", "w_q", "w_xo", "ffn2_w_down"]
BIG = [n for n in WEIGHTS if n in COL_SHARDED or n in ROW_SHARDED]
SMALL = [n for n in WEIGHTS if n not in BIG]
FFN1_BIG = ["ffn1_w_gate", "ffn1_w_up", "ffn1_w_down"]
MAIN_BIG = [n for n in BIG if n not in FFN1_BIG]
LATE_SMALL = "ffn1_norm"
EARLY_SMALL = [n for n in SMALL if n != LATE_SMALL]
PACK_ROWS = SUBLANES * LANES
GRAD_ROW_TILE = 256


def _to_rows(name, w, width):
    if name in COL_SHARDED:
        w = w.T
    return w.reshape(-1, width)


def _from_rows(name, rows, shard_shape):
    if name in COL_SHARDED:
        return rows.reshape(shard_shape[1], shard_shape[0]).T
    return rows.reshape(shard_shape)


def _pack_small(vals):
    flat = []
    for v in vals:
        f = v.reshape(-1)
        flat.append(jnp.pad(f, (0, (-f.shape[0]) % PACK_ROWS)))
    total = sum(f.shape[0] for f in flat)
    flat.append(jnp.zeros(((-total) % (GRAD_ROW_TILE * LANES),), F32))
    return jnp.concatenate(flat).reshape(-1, LANES)


def _unpack_small(packed, shapes):
    out, row = [], 0
    for shp in shapes:
        size = math.prod(shp)
        rows = -(-size // PACK_ROWS) * SUBLANES
        out.append(packed[row:row + rows].reshape(-1)[:size].reshape(shp))
        row += rows
    return out


def kernel(x, mem, ffn1_norm, ffn1_w_gate, ffn1_w_up, ffn1_w_down, mix_norm, w_in, pool_w, pool_scale, w_pool_proj, ssm_a_re, ssm_a_im, ssm_log_dt, ssm_b_re, ssm_b_im, ssm_c_re, ssm_c_im, ssm_d, w_glu_val, w_glu_gate, w_mix_out, xattn_norm, mem_norm, w_q, w_kv, w_xo, ffn2_norm, ffn2_w_gate, ffn2_w_up, ffn2_w_down, final_norm, loss_target, m_ffn1_norm, m_ffn1_w_gate, m_ffn1_w_up, m_ffn1_w_down, m_mix_norm, m_w_in, m_pool_w, m_pool_scale, m_w_pool_proj, m_ssm_a_re, m_ssm_a_im, m_ssm_log_dt, m_ssm_b_re, m_ssm_b_im, m_ssm_c_re, m_ssm_c_im, m_ssm_d, m_w_glu_val, m_w_glu_gate, m_w_mix_out, m_xattn_norm, m_mem_norm, m_w_q, m_w_kv, m_w_xo, m_ffn2_norm, m_ffn2_w_gate, m_ffn2_w_up, m_ffn2_w_down, m_final_norm, v_ffn1_norm, v_ffn1_w_gate, v_ffn1_w_up, v_ffn1_w_down, v_mix_norm, v_w_in, v_pool_w, v_pool_scale, v_w_pool_proj, v_ssm_a_re, v_ssm_a_im, v_ssm_log_dt, v_ssm_b_re, v_ssm_b_im, v_ssm_c_re, v_ssm_c_im, v_ssm_d, v_w_glu_val, v_w_glu_gate, v_w_mix_out, v_xattn_norm, v_mem_norm, v_w_q, v_w_kv, v_w_xo, v_ffn2_norm, v_ffn2_w_gate, v_ffn2_w_up, v_ffn2_w_down, v_final_norm):
    given = dict(locals())
    wts = {n: given[n] for n in WEIGHTS}
    moms = {n: (given["m_" + n], given["v_" + n]) for n in WEIGHTS}
    x2, mem2, tgt2 = x[0], mem[0], loss_target[0]
    d = x2.shape[1]
    core = lax.axis_index("c").astype(jnp.int32).reshape(1)
    chip = (2 * lax.axis_index("x") + lax.axis_index("y")).astype(jnp.int32).reshape(1)

    def full_form(n, f):
        shard = wts[n][0].shape
        return f.reshape(N_DEV * shard[1], shard[0]) if n in COL_SHARDED else f.reshape(N_DEV * shard[0], shard[1])

    shards = {n: _to_rows(n, wts[n][0], d).astype(BF16) for n in BIG}
    wt = {}
    rest = [n for n in MAIN_BIG if n != "w_in"]
    gather_ffn1 = _gather_start("weight_gather_ffn1_start", [shards[n] for n in FFN1_BIG], None)
    gather_in = _gather_start("weight_gather_in_start", [shards["w_in"]], gather_ffn1["token"])
    gather_rest = _gather_start("weight_gather_rest_start", [shards[n] for n in rest], gather_in["token"])
    sm = {n: (wts[n].reshape(1, -1) if wts[n].ndim <= 2 else wts[n][0]) for n in SMALL}
    sm["ffn1_norm"] = sm["ffn1_norm"] + gather_rest["token"][0, 0]

    pending = {}

    def reduce_start(tag, names, gb):
        blocks = [gb[n].reshape(N_DEV, -1, d) for n in names]
        pad_rows = (-sum(b.shape[1] for b in blocks)) % GRAD_ROW_TILE
        packed = jnp.concatenate(blocks + ([jnp.zeros((N_DEV, pad_rows, d), BF16)] if pad_rows else []), axis=1)
        pair = _pair_sum("grad_pair_sum_" + tag, packed, _exchange_cores("grad_exchange_cores_" + tag, packed), core)
        pending[tag] = (pair, _chips_start("grad_exchange_chips_start_" + tag, pair), [b.shape[1] for b in blocks])
        return pending[tag][1]["token"]

    def reduce_finish(tag, after):
        pair, started, rows = pending[tag]
        recv = _split_wait("grad_exchange_chips_wait_" + tag, started, after)[0]
        return _chip_sum("grad_chip_sum_" + tag, pair, recv, chip), rows

    def ev(name, gb=None, gs=None, loss=None, marker=None):
        if name == "ffn1_norm_done":
            for n, f in zip(FFN1_BIG, _split_wait("weight_gather_ffn1_wait", gather_ffn1, marker)):
                wt[n] = full_form(n, f)
        elif name == "ffn1_fwd_done":
            wt["w_in"] = full_form("w_in", _split_wait("weight_gather_in_wait", gather_in, marker)[0])
        elif name == "mix_in_done":
            for n, f in zip(rest, _split_wait("weight_gather_rest_wait", gather_rest, marker)):
                wt[n] = full_form(n, f)
        elif name == "grads_main":
            return reduce_start("main", MAIN_BIG, gb)
        elif name == "small_early":
            pending["small"] = _slots_start("small_gather_start", _pack_small([gs[n] for n in EARLY_SMALL] + [loss[:, :1]]))
            return pending["small"]["token"]
        elif name == "grads_ffn1":
            return reduce_start("ffn1", FFN1_BIG, gb)
        return None

    _, dx, _, gs = _local_step(x2, mem2, tgt2, wt, sm, ev)

    out_g, out_d, out_m, out_v = {}, {}, {}, {}

    def update(n, g_full):
        shape = wts[n].shape
        two_d = (-1, shape[-1])
        dl, m2, v2 = _adamw("adamw_" + n, wts[n].reshape(two_d), g_full.reshape(two_d), moms[n][0].reshape(two_d),
                            moms[n][1].reshape(two_d))
        out_g[n], out_d[n], out_m[n], out_v[n] = g_full, dl.reshape(shape), m2.reshape(shape), v2.reshape(shape)
        return dl

    def update_big(names, g_rows, rows):
        off = 0
        for n, r in zip(names, rows):
            shard = wts[n].shape
            dl = update(n, _from_rows(n, g_rows[off:off + r], shard[1:]).reshape(shard))
            off += r
        return dl

    last = update_big(MAIN_BIG, *reduce_finish("main", dx))

    small_sum = _sum_slots("small_sum", _split_wait("small_gather_wait", pending["small"], dx)[0], F32)
    late = _allgather("small_allgather_late", [gs[LATE_SMALL].reshape(-1, LANES)])[0]
    late_sum = _sum_slots("small_sum_late", late.reshape(N_DEV, -1, LANES), F32)
    vals = _unpack_small(small_sum, [wts[n].shape for n in EARLY_SMALL] + [(1, 1)])
    total_loss = vals[-1].reshape(())
    for n, g_full in zip(EARLY_SMALL + [LATE_SMALL], vals[:-1] + [late_sum.reshape(wts[LATE_SMALL].shape)]):
        update(n, g_full)

    update_big(FFN1_BIG, *reduce_finish("ffn1", last))

    return (total_loss, dx[None], *[out_g[n] for n in WEIGHTS], *[out_d[n] for n in WEIGHTS],
            *[out_m[n] for n in WEIGHTS], *[out_v[n] for n in WEIGHTS])
```

```python
import functools
import math

import jax
import jax.numpy as jnp
from jax import lax
from jax.experimental import pallas as pl
from jax.experimental.pallas import tpu as pltpu

F32 = jnp.float32
BF16 = jnp.bfloat16
EPS = 1e-6
N_XHEADS = 4
POOL_WINDOWS = (2, 4, 8, 16)
ADAM_LR = 0.001
ADAM_B1 = 0.9
ADAM_B2 = 0.999
ADAM_EPS = 1e-08
ADAM_WD = 0.01
ADAM_STEP = 10
N_DEV = 8
VMEM_LIMIT_V7X = 48 * 1024 * 1024
LANES = 128
SUBLANES = 8
SUB_ROWS = 256
POOL_PAD = 16
MESH = pl.DeviceIdType.MESH
ANY = pl.BlockSpec(memory_space=pl.ANY)
HBM = pl.BlockSpec(memory_space=pltpu.HBM)
SEM = pl.BlockSpec(memory_space=pltpu.SEMAPHORE)
SIDE_EFFECT = pltpu.SideEffectType.DATAFLOW_SIDE_EFFECTING

_DIMS = {
    "nt": (((1,), (1,)), ((), ())),
    "nn": (((1,), (0,)), ((), ())),
    "tn": (((0,), (0,)), ((), ())),
}


def _pick(dim, pref, mult=LANES):
    if dim <= pref:
        return dim
    for t in range(pref - pref % mult, 0, -mult):
        if dim % t == 0:
            return t
    return dim


def _params(sem):
    return pltpu.CompilerParams(dimension_semantics=sem, vmem_limit_bytes=VMEM_LIMIT_V7X)


def _tile(tm, tn, coff=0):
    return pl.BlockSpec((tm, tn), lambda i, j: (i, j + coff))


def _rowvec(tn, coff=0):
    return pl.BlockSpec((1, tn), lambda i, j: (0, j + coff))


def _out(m, n, dtype):
    return jax.ShapeDtypeStruct((m, n), dtype)


def _mm(name, form, a_list, b_list, groups, m, n, tm, tn, extras, epilogue, outs, after=None, sub=SUB_ROWS):
    na, nb, ne = len(a_list), len(b_list), len(extras)
    pins = [] if after is None else [after]
    step = tm if (sub is None or form == "tn" or tm % sub) else sub

    def a_spec(a):
        if form == "tn":
            return pl.BlockSpec((a.shape[0], tm), lambda i, j: (0, i))
        return pl.BlockSpec((tm, a.shape[1]), lambda i, j: (i, 0))

    def b_spec(b):
        if form == "nt":
            return pl.BlockSpec((tn, b.shape[1]), lambda i, j: (j, 0))
        return pl.BlockSpec((b.shape[0], tn), lambda i, j: (0, j))

    def body(*refs):
        a_refs, b_refs = refs[:na], refs[na:na + nb]
        e_refs, o_refs = refs[na + nb:na + nb + ne], refs[na + nb + ne + len(pins):]
        b_vals = {}
        for s0 in range(0, tm, step):
            rows = slice(None) if step == tm else pl.ds(s0, step)
            a_vals, accs = {}, []
            for group in groups:
                acc = None
                for ai, bi in group:
                    if ai not in a_vals:
                        a_vals[ai] = (a_refs[ai][...] if form == "tn" else a_refs[ai][rows, :]).astype(BF16)
                    if bi not in b_vals:
                        b_vals[bi] = b_refs[bi][...].astype(BF16)
                    d = lax.dot_general(a_vals[ai], b_vals[bi], _DIMS[form], preferred_element_type=F32)
                    acc = d if acc is None else acc + d
                accs.append(acc)
            res = epilogue(accs, *[e[rows, :] if e.shape[0] == tm else e[...] for e in e_refs])
            for o_ref, r in zip(o_refs, res):
                o_ref[rows, :] = r.astype(o_ref.dtype)

    out_specs = [_tile(tm, tn) if s is None else s for _, s in outs]
    res = pl.pallas_call(
        body, name=name, grid=(m // tm, n // tn),
        in_specs=[a_spec(a) for a in a_list] + [b_spec(b) for b in b_list] + [s for _, s in extras] + [ANY] * len(pins),
        out_specs=out_specs, out_shape=[o for o, _ in outs],
        compiler_params=_params(("parallel", "parallel")),
    )(*a_list, *b_list, *[e for e, _ in extras], *pins)
    return res


def _mm1(name, form, a, b, m, n, tm, tn, dtype, scale=None):
    epi = (lambda accs: (accs[0],)) if scale is None else (lambda accs: (accs[0] * scale,))
    return _mm(name, form, [a], [b], [[(0, 0)]], m, n, tm, tn, [], epi, [(_out(m, n, dtype), None)])[0]


def _rms_fwd(name, h, g):
    t, d = h.shape
    tm = _pick(t, 512, SUBLANES)

    def body(h_ref, g_ref, n_ref):
        hv = h_ref[...]
        r = lax.rsqrt(jnp.mean(hv * hv, axis=-1, keepdims=True) + EPS)
        n_ref[...] = ((hv * r) * g_ref[...]).astype(BF16)

    return pl.pallas_call(
        body, name=name, grid=(t // tm,),
        in_specs=[pl.BlockSpec((tm, d), lambda i: (i, 0)), pl.BlockSpec((1, d), lambda i: (0, 0))],
        out_specs=pl.BlockSpec((tm, d), lambda i: (i, 0)), out_shape=_out(t, d, BF16),
        compiler_params=_params(("parallel",)),
    )(h, g)


def _rms_bwd(name, h, g, dn, dres=None):
    t, d = h.shape
    tm = _pick(t, 512, SUBLANES)
    need_dh = dres is not None

    def body(*refs):
        if need_dh:
            h_ref, g_ref, dn_ref, dres_ref, dh_ref, dhb_ref, dg_ref = refs
        else:
            h_ref, g_ref, dn_ref, dg_ref = refs
        hv = h_ref[...]
        r = lax.rsqrt(jnp.mean(hv * hv, axis=-1, keepdims=True) + EPS)
        nh = hv * r
        dnv = dn_ref[...].astype(F32)

        @pl.when(pl.program_id(0) == 0)
        def _():
            dg_ref[...] = jnp.zeros_like(dg_ref)

        dg_ref[...] += jnp.sum(dnv * nh, axis=0, keepdims=True)
        if need_dh:
            dng = dnv * g_ref[...]
            dh = dres_ref[...] + r * (dng - nh * jnp.mean(dng * nh, axis=-1, keepdims=True))
            dh_ref[...] = dh
            dhb_ref[...] = dh.astype(BF16)

    row = pl.BlockSpec((tm, d), lambda i: (i, 0))
    vec = pl.BlockSpec((1, d), lambda i: (0, 0))
    if need_dh:
        return pl.pallas_call(
            body, name=name, grid=(t // tm,), in_specs=[row, vec, row, row], out_specs=[row, row, vec],
            out_shape=[_out(t, d, F32), _out(t, d, BF16), _out(1, d, F32)], compiler_params=_params(("arbitrary",)),
        )(h, g, dn, dres)
    return pl.pallas_call(
        body, name=name, grid=(t // tm,), in_specs=[row, vec, row], out_specs=vec,
        out_shape=_out(1, d, F32), compiler_params=_params(("arbitrary",)),
    )(h, g, dn)


def _loss_head(h, g, tgt):
    t, d = h.shape
    tm = _pick(t, 512, SUBLANES)

    def body(h_ref, g_ref, t_ref, dh_ref, dhb_ref, dg_ref, loss_ref):
        hv = h_ref[...]
        r = lax.rsqrt(jnp.mean(hv * hv, axis=-1, keepdims=True) + EPS)
        nh = hv * r
        err = nh * g_ref[...] - t_ref[...]

        @pl.when(pl.program_id(0) == 0)
        def _():
            dg_ref[...] = jnp.zeros_like(dg_ref)
            loss_ref[...] = jnp.zeros_like(loss_ref)

        per_row = jnp.mean(err * err, axis=-1, keepdims=True)
        loss_ref[...] += 0.5 * jnp.sum(per_row, axis=0, keepdims=True)
        dy = err * (1.0 / d)
        dg_ref[...] += jnp.sum(dy * nh, axis=0, keepdims=True)
        dng = dy * g_ref[...]
        dh = r * (dng - nh * jnp.mean(dng * nh, axis=-1, keepdims=True))
        dh_ref[...] = dh
        dhb_ref[...] = dh.astype(BF16)

    row = pl.BlockSpec((tm, d), lambda i: (i, 0))
    vec = pl.BlockSpec((1, d), lambda i: (0, 0))
    return pl.pallas_call(
        body, name="loss_head", grid=(t // tm,), in_specs=[row, vec, row],
        out_specs=[row, row, vec, pl.BlockSpec((1, LANES), lambda i: (0, 0))],
        out_shape=[_out(t, d, F32), _out(t, d, BF16), _out(1, d, F32), _out(1, LANES, F32)],
        compiler_params=_params(("arbitrary",)),
    )(h, g, tgt)


def _ffn_fwd(tag, h, n, wg_t, wu_t, wd):
    t, d = h.shape
    f = wd.shape[0]
    tm, tn = _pick(t, 1024), _pick(f, 1408)

    def up_epi(accs):
        a, b = accs
        return a, b, (a * jax.nn.sigmoid(a)) * b

    a, b, hid = _mm(tag + "_up", "nt", [n], [wg_t, wu_t], [[(0, 0)], [(0, 1)]], t, f, tm, tn, [], up_epi,
                    [(_out(t, f, BF16), None)] * 3)
    tm2, tn2 = _pick(t, 1024), _pick(d, 512)
    h_out = _mm(tag + "_down", "nn", [hid], [wd], [[(0, 0)]], t, d, tm2, tn2, [(h, _tile(tm2, tn2))],
                lambda accs, hin: (hin + 0.5 * accs[0],), [(_out(t, d, F32), None)])[0]
    return h_out, (n, a, b, hid)


def _ffn_bwd(tag, h, g, wg_t, wu_t, wd, saved, dh, dh_bf, weights_done=None, after=None):
    n, a, b, hid = saved
    t, d = h.shape
    f = wd.shape[0]
    tm, tn = _pick(t, 1024), _pick(f, 1408)

    def hid_epi(accs, av, bv):
        dhid = 0.5 * accs[0]
        av, bv = av.astype(F32), bv.astype(F32)
        sig = jax.nn.sigmoid(av)
        da = dhid * bv * (sig * (1.0 + av * (1.0 - sig)))
        db = dhid * (av * sig)
        return da, db

    da, db = _mm(tag + "_bwd_hid", "nt", [dh_bf], [wd], [[(0, 0)]], t, f, tm, tn,
                 [(a, _tile(tm, tn)), (b, _tile(tm, tn))], hid_epi, [(_out(t, f, BF16), None)] * 2, after=after)
    tw, tnw = _pick(f, 1408), _pick(d, 512)
    d_wd = _mm1(tag + "_dwd", "tn", hid, dh_bf, f, d, tw, tnw, BF16, scale=0.5)
    d_wg = _mm1(tag + "_dwg", "tn", da, n, f, d, tw, tnw, BF16)
    d_wu = _mm1(tag + "_dwu", "tn", db, n, f, d, tw, tnw, BF16)
    pin = weights_done(d_wg, d_wu, d_wd) if weights_done is not None else None
    tm2, tn2 = _pick(t, 1024), _pick(d, 512)
    dn = _mm(tag + "_dn", "nn", [da, db], [wg_t, wu_t], [[(0, 0), (1, 1)]], t, d, tm2, tn2, [],
             lambda accs: (accs[0],), [(_out(t, d, F32), None)], after=pin)[0]
    dh_in, dh_in_bf, dg = _rms_bwd(tag + "_norm_bwd", h, g, dn, dh)
    return dh_in, dh_in_bf, dg, d_wg, d_wu, d_wd


def _window_sum(win, offsets):
    n = win.shape[0]
    acc = None
    for j in offsets:
        term = win if j == 0 else pltpu.roll(win, (-j) % n, 0)
        acc = term if acc is None else acc + term
    return acc


def _pool_counts(r0, ch, c, left, right, t):
    pos = r0 + lax.broadcasted_iota(jnp.int32, (ch, c), 0)
    return (jnp.minimum(pos + right + 1, t) - jnp.maximum(pos - left, 0)).astype(F32)


def _pool_fwd(proj, pool_w_bf, pool_scale):
    t = proj.shape[0]
    ng, c, _ = pool_w_bf.shape
    ch = _pick(t, 256, SUBLANES)
    pad = POOL_PAD

    def body(p_ref, w_ref, s_ref, pooled_ref, pm_ref, buf):
        grp = pl.program_id(0)
        buf[pl.ds(0, pad), :] = jnp.zeros((pad, c), F32)
        buf[pl.ds(pad + t, pad), :] = jnp.zeros((pad, c), F32)

        def fill(ci, carry):
            r0 = pl.multiple_of(ci * ch, SUBLANES)
            buf[pl.ds(pl.multiple_of(r0 + pad, SUBLANES), ch), :] = p_ref[pl.ds(r0, ch), :]
            return carry

        lax.fori_loop(0, t // ch, fill, 0)
        for gi, w in enumerate(POOL_WINDOWS):
            left = w // 2
            right = w - 1 - left

            @pl.when(grp == gi)
            def _(left=left, right=right):
                def chunk(ci, carry):
                    r0 = pl.multiple_of(ci * ch, SUBLANES)
                    win = buf[pl.ds(r0, ch + 2 * pad), :]
                    s = _window_sum(win, range(-left, right + 1))[pad:pad + ch]
                    pooled = s / _pool_counts(r0, ch, c, left, right, t) - win[pad:pad + ch]
                    pooled_bf = pooled.astype(BF16)
                    mixed = jnp.dot(pooled_bf, w_ref[0], preferred_element_type=F32)
                    pooled_ref[pl.ds(r0, ch), :] = pooled_bf
                    pm_ref[pl.ds(r0, ch), :] = (mixed * s_ref[...]).astype(BF16)
                    return carry

                lax.fori_loop(0, t // ch, chunk, 0)

    col = pl.BlockSpec((t, c), lambda g: (0, g))
    return pl.pallas_call(
        body, name="pool_fwd", grid=(ng,),
        in_specs=[col, pl.BlockSpec((1, c, c), lambda g: (g, 0, 0)), pl.BlockSpec((1, c), lambda g: (0, g))],
        out_specs=[col, col], out_shape=[_out(t, ng * c, BF16), _out(t, ng * c, BF16)],
        scratch_shapes=[pltpu.VMEM((t + 2 * pad, c), F32)],
        compiler_params=_params(("parallel",)),
    )(proj, pool_w_bf, pool_scale)


def _pool_bwd(pooled, dpm, pool_w_bf, pool_scale):
    t = pooled.shape[0]
    ng, c, _ = pool_w_bf.shape
    ch = _pick(t, 256, SUBLANES)
    pad = POOL_PAD

    def body(pooled_ref, dpm_ref, w_ref, s_ref, dp_ref, dw_ref, ds_ref, buf, raw):
        grp = pl.program_id(0)
        buf[pl.ds(0, pad), :] = jnp.zeros((pad, c), F32)
        buf[pl.ds(pad + t, pad), :] = jnp.zeros((pad, c), F32)
        dw_ref[...] = jnp.zeros_like(dw_ref)
        ds_ref[...] = jnp.zeros_like(ds_ref)
        for gi, w in enumerate(POOL_WINDOWS):
            left = w // 2
            right = w - 1 - left

            @pl.when(grp == gi)
            def _(left=left, right=right):
                def first(ci, carry):
                    r0 = pl.multiple_of(ci * ch, SUBLANES)
                    pv = pooled_ref[pl.ds(r0, ch), :]
                    dpm_v = dpm_ref[pl.ds(r0, ch), :]
                    mixed = jnp.dot(pv, w_ref[0], preferred_element_type=F32)
                    ds_ref[...] += jnp.sum(dpm_v * mixed, axis=0, keepdims=True)
                    dmixed = (dpm_v * s_ref[...]).astype(BF16)
                    dw_ref[0] += lax.dot_general(pv, dmixed, _DIMS["tn"], preferred_element_type=F32)
                    dpooled = lax.dot_general(dmixed, w_ref[0], _DIMS["nt"], preferred_element_type=F32)
                    raw[pl.ds(r0, ch), :] = dpooled
                    buf[pl.ds(pl.multiple_of(r0 + pad, SUBLANES), ch), :] = (
                        dpooled / _pool_counts(r0, ch, c, left, right, t))
                    return carry

                lax.fori_loop(0, t // ch, first, 0)

                def second(ci, carry):
                    r0 = pl.multiple_of(ci * ch, SUBLANES)
                    win = buf[pl.ds(r0, ch + 2 * pad), :]
                    s = _window_sum(win, range(-right, left + 1))[pad:pad + ch]
                    dp_ref[pl.ds(r0, ch), :] = (s - raw[pl.ds(r0, ch), :]).astype(BF16)
                    return carry

                lax.fori_loop(0, t // ch, second, 0)

    col = pl.BlockSpec((t, c), lambda g: (0, g))
    return pl.pallas_call(
        body, name="pool_bwd", grid=(ng,),
        in_specs=[col, col, pl.BlockSpec((1, c, c), lambda g: (g, 0, 0)), pl.BlockSpec((1, c), lambda g: (0, g))],
        out_specs=[col, pl.BlockSpec((1, c, c), lambda g: (g, 0, 0)), pl.BlockSpec((1, c), lambda g: (0, g))],
        out_shape=[_out(t, ng * c, BF16), jax.ShapeDtypeStruct((ng, c, c), F32), _out(1, ng * c, F32)],
        scratch_shapes=[pltpu.VMEM((t + 2 * pad, c), F32), pltpu.VMEM((t, c), F32)],
        compiler_params=_params(("parallel",)),
    )(pooled, dpm, pool_w_bf, pool_scale)


def _discretise(a_re, a_im, log_dt, b_re, b_im):
    dt = jnp.exp(log_dt)
    mag = jnp.exp(dt * a_re)
    ang = dt * a_im
    abr = mag * jnp.cos(ang)
    abi = mag * jnp.sin(ang)
    den = a_re * a_re + a_im * a_im
    nr = abr - 1.0
    qr = (nr * a_re + abi * a_im) / den
    qi = (abi * a_re - nr * a_im) / den
    return abr, abi, qr * b_re - qi * b_im, qr * b_im + qi * b_re


def _ssm_disc(cols):
    n, hh = cols[3].shape

    def body(ar, ai, ld, br, bi, o1, o2, o3, o4):
        res = _discretise(ar[...], ai[...], ld[...], br[...], bi[...])
        for o, r in zip((o1, o2, o3, o4), res):
            o[...] = r

    return pl.pallas_call(
        body, name="ssm_disc",
        out_shape=[_out(n, 1, F32), _out(n, 1, F32), _out(n, hh, F32), _out(n, hh, F32)],
    )(*cols)


def _ssm_disc_bwd(cols, cots):
    n, hh = cols[3].shape

    def body(ar, ai, ld, br, bi, c1, c2, c3, c4, o1, o2, o3, o4, o5):
        _, vjp = jax.vjp(_discretise, ar[...], ai[...], ld[...], br[...], bi[...])
        res = vjp((c1[...], c2[...], c3[...], c4[...]))
        for o, r in zip((o1, o2, o3, o4, o5), res):
            o[...] = r

    return pl.pallas_call(
        body, name="ssm_disc_bwd",
        out_shape=[_out(n, 1, F32)] * 3 + [_out(n, hh, F32)] * 2,
    )(*cols, *cots)


def _rowsum(name, a):
    r, _ = a.shape

    def body(a_ref, o_ref):
        o_ref[...] = jnp.sum(a_ref[...], axis=-1, keepdims=True)

    return pl.pallas_call(body, name=name, out_shape=_out(r, 1, F32))(a)


def _cmul(pr, pi, qr, qi):
    return pr * qr - pi * qi, pr * qi + pi * qr


def _scan(name, u, ar, ai, reverse, conj):
    t, s2 = u.shape
    s = s2 // 2
    w = _pick(s, 512)
    tc = _pick(t, 512, SUBLANES)
    n_t, n_w = t // tc, s // w
    groups = tc // SUBLANES
    last = 0 if reverse else SUBLANES - 1

    def body(ar_ref, ai_ref, ur_ref, ui_ref, xr_ref, xi_ref, cr_ref, ci_ref):
        @pl.when(pl.program_id(1) == 0)
        def _():
            cr_ref[...] = jnp.zeros_like(cr_ref)
            ci_ref[...] = jnp.zeros_like(ci_ref)

        a1r = ar_ref[...]
        a1i = -ai_ref[...] if conj else ai_ref[...]
        a2r, a2i = _cmul(a1r, a1i, a1r, a1i)
        a4r, a4i = _cmul(a2r, a2i, a2r, a2i)
        row = lax.broadcasted_iota(jnp.int32, (SUBLANES, w), 0)
        pwr = jnp.zeros((SUBLANES, w), F32)
        pwi = jnp.zeros((SUBLANES, w), F32)
        cur_r, cur_i = a1r, a1i
        for k in range(SUBLANES):
            rk = SUBLANES - 1 - k if reverse else k
            pwr = jnp.where(row == rk, cur_r, pwr)
            pwi = jnp.where(row == rk, cur_i, pwi)
            cur_r, cur_i = _cmul(cur_r, cur_i, a1r, a1i)
        steps = ((1, a1r, a1i), (2, a2r, a2i), (4, a4r, a4i))

        def one(i, carry):
            g = groups - 1 - i if reverse else i
            r0 = pl.multiple_of(g * SUBLANES, SUBLANES)
            br = ur_ref[pl.ds(r0, SUBLANES), :]
            bi = ui_ref[pl.ds(r0, SUBLANES), :]
            for dist, pr, pi in steps:
                if reverse:
                    keep = row < SUBLANES - dist
                    shift = SUBLANES - dist
                else:
                    keep = row >= dist
                    shift = dist
                sr = jnp.where(keep, pltpu.roll(br, shift, 0), 0.0)
                si = jnp.where(keep, pltpu.roll(bi, shift, 0), 0.0)
                br, bi = br + pr * sr - pi * si, bi + pr * si + pi * sr
            cr = cr_ref[pl.ds(last, 1), :]
            ci = ci_ref[pl.ds(last, 1), :]
            xr = br + pwr * cr - pwi * ci
            xi = bi + pwr * ci + pwi * cr
            xr_ref[pl.ds(r0, SUBLANES), :] = xr
            xi_ref[pl.ds(r0, SUBLANES), :] = xi
            cr_ref[...] = xr
            ci_ref[...] = xi
            return carry

        lax.fori_loop(0, groups, one, 0)

    def tmap(k):
        return n_t - 1 - k if reverse else k

    re_blk = pl.BlockSpec((tc, w), lambda cb, k: (tmap(k), cb))
    im_blk = pl.BlockSpec((tc, w), lambda cb, k: (tmap(k), cb + n_w))
    a_blk = pl.BlockSpec((1, w), lambda cb, k: (0, cb))
    xr, xi = pl.pallas_call(
        body, name=name, grid=(n_w, n_t), in_specs=[a_blk, a_blk, re_blk, im_blk],
        out_specs=[pl.BlockSpec((tc, w), lambda cb, k: (tmap(k), cb))] * 2,
        out_shape=[_out(t, s, F32), _out(t, s, F32)],
        scratch_shapes=[pltpu.VMEM((SUBLANES, w), F32), pltpu.VMEM((SUBLANES, w), F32)],
        compiler_params=_params(("parallel", "arbitrary")),
    )(ar, ai, u, u)
    return xr, xi


def _ssm_da(name, lr, li, xr, xi, reverse):
    t, s = xr.shape
    w = _pick(s, 512)
    tc = _pick(t, 256, SUBLANES)
    n_t, n_w = t // tc, s // w

    def body(lr_ref, li_ref, xr_ref, xi_ref, dar_ref, dai_ref, pr_ref, pi_ref):
        @pl.when(pl.program_id(1) == 0)
        def _():
            pr_ref[...] = jnp.zeros_like(pr_ref)
            pi_ref[...] = jnp.zeros_like(pi_ref)
            dar_ref[...] = jnp.zeros_like(dar_ref)
            dai_ref[...] = jnp.zeros_like(dai_ref)

        row = lax.broadcasted_iota(jnp.int32, (tc, w), 0)
        xrv, xiv = xr_ref[...], xi_ref[...]
        if reverse:
            keep, shift, edge = row < tc - 1, tc - 1, 0
        else:
            keep, shift, edge = row >= 1, 1, tc - 1
        xsr = jnp.where(keep, pltpu.roll(xrv, shift, 0), pr_ref[pl.ds(0, 1), :])
        xsi = jnp.where(keep, pltpu.roll(xiv, shift, 0), pi_ref[pl.ds(0, 1), :])
        lrv, liv = lr_ref[...], li_ref[...]
        dar_ref[...] += jnp.sum(lrv * xsr + liv * xsi, axis=0, keepdims=True)
        dai_ref[...] += jnp.sum(liv * xsr - lrv * xsi, axis=0, keepdims=True)
        pr_ref[pl.ds(0, 1), :] = xr_ref[pl.ds(edge, 1), :]
        pi_ref[pl.ds(0, 1), :] = xi_ref[pl.ds(edge, 1), :]

    def tmap(k):
        return n_t - 1 - k if reverse else k

    blk = pl.BlockSpec((tc, w), lambda cb, k: (tmap(k), cb))
    vec = pl.BlockSpec((1, w), lambda cb, k: (0, cb))
    return pl.pallas_call(
        body, name=name, grid=(n_w, n_t), in_specs=[blk] * 4, out_specs=[vec, vec],
        out_shape=[_out(1, s, F32), _out(1, s, F32)],
        scratch_shapes=[pltpu.VMEM((SUBLANES, w), F32), pltpu.VMEM((SUBLANES, w), F32)],
        compiler_params=_params(("parallel", "arbitrary")),
    )(lr, li, xr, xi)


def _colsum_prod(name, a, b, b_coff=0):
    t, n = a.shape
    tm = _pick(t, 512, SUBLANES)

    def body(a_ref, b_ref, o_ref):
        @pl.when(pl.program_id(0) == 0)
        def _():
            o_ref[...] = jnp.zeros_like(o_ref)

        o_ref[...] += jnp.sum(a_ref[...].astype(F32) * b_ref[...].astype(F32), axis=0, keepdims=True)

    return pl.pallas_call(
        body, name=name, grid=(t // tm,),
        in_specs=[pl.BlockSpec((tm, n), lambda i: (i, 0)), pl.BlockSpec((tm, n), lambda i: (i, b_coff))],
        out_specs=pl.BlockSpec((1, n), lambda i: (0, 0)), out_shape=_out(1, n, F32),
        compiler_params=_params(("arbitrary",)),
    )(a, b)


def _bd_in(bb, g, p, hh):
    blk = bb.reshape(g, p, hh).transpose(0, 2, 1)
    eye = jnp.eye(g, dtype=bool)[:, None, :, None]
    return jnp.where(eye, blk[:, :, None, :], 0.0).reshape(g * hh, g * p)


def _bd_out(cc, g, p, hh):
    blk = cc.transpose(0, 2, 1)
    eye = jnp.eye(g, dtype=bool)[:, None, :, None]
    return jnp.where(eye, blk[:, :, None, :], 0.0).reshape(g * p, g * hh)


def _diag_in(dmat, g, p, hh):
    eye = jnp.eye(g, dtype=bool)[:, None, :, None]
    diag = jnp.sum(jnp.where(eye, dmat.reshape(g, hh, g, p), 0.0), axis=2)
    return diag.transpose(0, 2, 1).reshape(g * p, hh)


def _diag_out(dmat, g, p, hh):
    eye = jnp.eye(g, dtype=bool)[:, None, :, None]
    diag = jnp.sum(jnp.where(eye, dmat.reshape(g, p, g, hh), 0.0), axis=2)
    return diag.transpose(0, 2, 1)


def _softmax(qh, kh, scale):
    s = lax.dot_general(qh, kh, _DIMS["nt"], preferred_element_type=F32) * scale
    e = jnp.exp(s - jnp.max(s, axis=-1, keepdims=True))
    return e / jnp.sum(e, axis=-1, keepdims=True)


def _attn_fwd(q, kv):
    t, d = q.shape
    mm_ = kv.shape[0]
    hd = d // N_XHEADS
    scale = 1.0 / math.sqrt(hd)
    tm = _pick(t, 512, SUBLANES)

    def body(q_ref, kv_ref, o_ref):
        for h in range(N_XHEADS):
            sl = pl.ds(h * hd, hd)
            p = _softmax(q_ref[:, sl], kv_ref[:, sl], scale)
            o_ref[:, sl] = jnp.dot(p.astype(BF16), kv_ref[:, pl.ds(d + h * hd, hd)],
                                   preferred_element_type=F32).astype(BF16)

    return pl.pallas_call(
        body, name="attn_fwd", grid=(t // tm,),
        in_specs=[pl.BlockSpec((tm, d), lambda i: (i, 0)), pl.BlockSpec((mm_, 2 * d), lambda i: (0, 0))],
        out_specs=pl.BlockSpec((tm, d), lambda i: (i, 0)), out_shape=_out(t, d, BF16),
        compiler_params=_params(("parallel",)),
    )(q, kv)


def _attn_bwd(q, kv, do):
    t, d = q.shape
    mm_ = kv.shape[0]
    hd = d // N_XHEADS
    scale = 1.0 / math.sqrt(hd)
    tm = _pick(t, 512, SUBLANES)

    def body(q_ref, kv_ref, do_ref, dq_ref, dkv_ref):
        @pl.when(pl.program_id(0) == 0)
        def _():
            dkv_ref[...] = jnp.zeros_like(dkv_ref)

        for h in range(N_XHEADS):
            sl = pl.ds(h * hd, hd)
            vsl = pl.ds(d + h * hd, hd)
            qh, kh, doh = q_ref[:, sl], kv_ref[:, sl], do_ref[:, sl]
            p = _softmax(qh, kh, scale)
            dp = lax.dot_general(doh, kv_ref[:, vsl], _DIMS["nt"], preferred_element_type=F32)
            dkv_ref[:, vsl] += lax.dot_general(p.astype(BF16), doh, _DIMS["tn"], preferred_element_type=F32)
            ds = (p * (dp - jnp.sum(dp * p, axis=-1, keepdims=True)) * scale).astype(BF16)
            dq_ref[:, sl] = jnp.dot(ds, kh, preferred_element_type=F32).astype(BF16)
            dkv_ref[:, sl] += lax.dot_general(ds, qh, _DIMS["tn"], preferred_element_type=F32)

    row = pl.BlockSpec((tm, d), lambda i: (i, 0))
    full = pl.BlockSpec((mm_, 2 * d), lambda i: (0, 0))
    return pl.pallas_call(
        body, name="attn_bwd", grid=(t // tm,), in_specs=[row, full, row], out_specs=[row, full],
        out_shape=[_out(t, d, BF16), _out(mm_, 2 * d, F32)], compiler_params=_params(("arbitrary",)),
    )(q, kv, do)


def _ew(name, fn, ins, outs, rows_pref=256):
    r, c = ins[0].shape
    tr = _pick(r, rows_pref, SUBLANES)
    ni = len(ins)

    def body(*refs):
        res = fn(*[x[...] for x in refs[:ni]])
        for o_ref, v in zip(refs[ni:], res):
            o_ref[...] = v.astype(o_ref.dtype)

    blk = pl.BlockSpec((tr, c), lambda i: (i, 0))
    return pl.pallas_call(
        body, name=name, grid=(r // tr,), in_specs=[blk] * ni, out_specs=[blk] * len(outs),
        out_shape=[_out(r, c, dt) for dt in outs], compiler_params=_params(("parallel",)),
    )(*ins)


def _sum_slots(name, a, dtype):
    s, r, c = a.shape
    tr = _pick(r, 256, SUBLANES)

    def body(a_ref, o_ref):
        acc = a_ref[0].astype(F32)
        for k in range(1, s):
            acc = acc + a_ref[k].astype(F32)
        o_ref[...] = acc.astype(o_ref.dtype)

    return pl.pallas_call(
        body, name=name, grid=(r // tr,), in_specs=[pl.BlockSpec((s, tr, c), lambda i: (0, i, 0))],
        out_specs=pl.BlockSpec((tr, c), lambda i: (i, 0)), out_shape=_out(r, c, dtype),
        compiler_params=_params(("parallel",)),
    )(a)


def _adamw(name, w, g, m, v):
    bc1 = 1.0 - ADAM_B1 ** ADAM_STEP
    bc2 = 1.0 - ADAM_B2 ** ADAM_STEP

    def fn(wv, gv, mv, vv):
        m2 = ADAM_B1 * mv + (1.0 - ADAM_B1) * gv
        v2 = ADAM_B2 * vv + (1.0 - ADAM_B2) * (gv * gv)
        delta = -ADAM_LR * ((m2 / bc1) / (jnp.sqrt(v2 / bc2) + ADAM_EPS) + ADAM_WD * wv)
        return delta, m2, v2

    return _ew(name, fn, [w, g, m, v], [F32, F32, F32])


def _allgather(name, arrs):
    n = len(arrs)

    def body(*refs):
        ins, outs = refs[:n], refs[n:2 * n]
        send_sems, recv_sems, local_sems = refs[2 * n:]
        x, y, c = lax.axis_index("x"), lax.axis_index("y"), lax.axis_index("c")
        me, sibling = (x, y, c), (x, y, 1 - c)
        chips = [(1 - x, y), (x, 1 - y), (1 - x, 1 - y)]

        def rows(a, px, py, pc):
            r = ins[a].shape[0]
            return outs[a].at[pl.ds((4 * px + 2 * py + pc) * r, r), :]

        def copy(a, k, block, to, src=None):
            return pltpu.make_async_remote_copy(
                src_ref=rows(a, *block) if src is None else src, dst_ref=rows(a, *block),
                send_sem=send_sems.at[a, k], recv_sem=recv_sems.at[a, k], device_id=to, device_id_type=MESH)

        mine = [pltpu.make_async_copy(ins[a], rows(a, *me), local_sems.at[a]) for a in range(n)]
        for cp in mine:
            cp.start()
        first = []
        for a in range(n):
            first.append(copy(a, 0, me, sibling, src=ins[a]))
            first += [copy(a, 1 + j, me, (*chip, c), src=ins[a]) for j, chip in enumerate(chips)]
        for cp in first:
            cp.start()
        passed = []
        for j, chip in enumerate(chips):
            for a in range(n):
                copy(a, 1 + j, (*chip, c), me).wait_recv()
                cp = copy(a, 4 + j, (*chip, c), sibling)
                cp.start()
                passed.append(cp)
        for a in range(n):
            copy(a, 0, sibling, me).wait_recv()
            for j, chip in enumerate(chips):
                copy(a, 4 + j, (*chip, 1 - c), me).wait_recv()
        for cp in first + passed:
            cp.wait_send()
        for cp in mine:
            cp.wait()

    return pl.pallas_call(
        body, name=name, in_specs=[ANY] * n, out_specs=[ANY] * n,
        out_shape=[_out(N_DEV * a.shape[0], a.shape[1], a.dtype) for a in arrs],
        scratch_shapes=[pltpu.SemaphoreType.DMA((n, 7)), pltpu.SemaphoreType.DMA((n, 7)), pltpu.SemaphoreType.DMA((n,))],
    )(*arrs)


def _exchange_cores(name, g):
    _, r, c = g.shape
    nck = r // GRAD_ROW_TILE

    def body(g_ref, recv_ref, send_sems, recv_sems):
        x, y, cc = lax.axis_index("x"), lax.axis_index("y"), lax.axis_index("c")
        copies = []
        for q in range(4):
            for k in range(nck):
                rows = pl.ds(k * GRAD_ROW_TILE, GRAD_ROW_TILE)
                copies.append(pltpu.make_async_remote_copy(
                    src_ref=g_ref.at[2 * q + (1 - cc), rows], dst_ref=recv_ref.at[q, rows],
                    send_sem=send_sems.at[q, k], recv_sem=recv_sems.at[q, k], device_id=(x, y, 1 - cc),
                    device_id_type=MESH))
        for cp in copies:
            cp.start()
        for cp in copies:
            cp.wait()

    return pl.pallas_call(
        body, name=name, in_specs=[ANY], out_specs=ANY,
        out_shape=jax.ShapeDtypeStruct((4, r, c), g.dtype),
        scratch_shapes=[pltpu.SemaphoreType.DMA((4, nck)), pltpu.SemaphoreType.DMA((4, nck))],
    )(g)


def _pair_sum(name, g, recv, core):
    _, r, c = g.shape
    tr = GRAD_ROW_TILE

    def body(core_ref, g_ref, r_ref, o_ref):
        o_ref[...] = (g_ref[...].astype(F32) + r_ref[...].astype(F32)).astype(o_ref.dtype)

    blk = pl.BlockSpec((None, tr, c), lambda q, i, core_ref: (q, i, 0))
    return pl.pallas_call(
        body, name=name,
        grid_spec=pltpu.PrefetchScalarGridSpec(
            num_scalar_prefetch=1, grid=(4, r // tr),
            in_specs=[pl.BlockSpec((None, tr, c), lambda q, i, core_ref: (2 * q + core_ref[0], i, 0)), blk],
            out_specs=blk),
        out_shape=jax.ShapeDtypeStruct((4, r, c), g.dtype), compiler_params=_params(("parallel", "parallel")),
    )(core, g, recv)


def _peer(k, x, y, c):
    return (1 - x if k & 4 else x, 1 - y if k & 2 else y, 1 - c if k & 1 else c)


def _split_start(name, srcs, land_shapes, n_remote, n_local, build, after=None):
    ns, nl = len(srcs), len(land_shapes)
    n_sem = 3 if n_local else 2
    pins = [] if after is None else [after]

    def body(*refs):
        src_refs, land_refs = refs[:ns], refs[ns:ns + nl]
        sems = refs[ns + nl + len(pins):ns + nl + len(pins) + n_sem]
        token = refs[-1]
        remote, local = build(src_refs, land_refs, *sems)
        for cp in local + remote:
            cp.start()
        token[...] = jnp.zeros_like(token)

    sem_shapes = [pltpu.SemaphoreType.DMA((n_remote,)), pltpu.SemaphoreType.DMA((n_remote,))]
    if n_local:
        sem_shapes.append(pltpu.SemaphoreType.DMA((n_local,)))
    bufs = [pltpu.with_memory_space_constraint(a, pltpu.HBM) for a in srcs]
    bufs += [pltpu.with_memory_space_constraint(lax.empty(s.shape, s.dtype), pltpu.HBM) for s in land_shapes]
    outs = pl.pallas_call(
        body, name=name,
        out_shape=sem_shapes + [pltpu.HBM(b.shape, b.dtype) for b in bufs] + [jax.ShapeDtypeStruct((SUBLANES, LANES), F32)],
        in_specs=[HBM] * (ns + nl) + [ANY] * len(pins),
        out_specs=[SEM] * n_sem + [HBM] * (ns + nl) + [pl.BlockSpec(memory_space=pltpu.VMEM)],
        input_output_aliases={i: n_sem + i for i in range(ns + nl)},
        compiler_params=pltpu.CompilerParams(has_side_effects=SIDE_EFFECT),
    )(*bufs, *pins)
    return dict(sems=list(outs[:n_sem]), bufs=list(outs[n_sem:n_sem + ns + nl]), token=outs[-1], build=build, ns=ns)


def _split_wait(name, started, after):
    ns, n_buf, n_sem = started["ns"], len(started["bufs"]), len(started["sems"])

    def body(*refs):
        src_refs, land_refs = refs[:ns], refs[ns:n_buf]
        sems = refs[n_buf:n_buf + n_sem]
        remote, local = started["build"](src_refs, land_refs, *sems)
        for cp in local:
            cp.wait()
        for cp in remote:
            cp.wait_send()
            cp.wait_recv()

    outs = pl.pallas_call(
        body, name=name, out_shape=[pltpu.HBM(b.shape, b.dtype) for b in started["bufs"]],
        in_specs=[HBM] * n_buf + [SEM] * n_sem + [ANY], out_specs=[HBM] * n_buf,
        input_output_aliases={i: i for i in range(n_buf)},
        compiler_params=pltpu.CompilerParams(has_side_effects=SIDE_EFFECT),
    )(*started["bufs"], *started["sems"], after)
    return list(outs[ns:])


def _gather_start(name, shards, after):
    m = len(shards)

    def build(src_refs, land_refs, send_sems, recv_sems, local_sems):
        x, y, c = lax.axis_index("x"), lax.axis_index("y"), lax.axis_index("c")
        remote, local = [], []
        for j in range(m):
            r = src_refs[j].shape[0]
            dst = land_refs[j].at[pl.ds((4 * x + 2 * y + c) * r, r), :]
            local.append(pltpu.make_async_copy(src_refs[j], dst, local_sems.at[j]))
            for k in range(1, N_DEV):
                remote.append(pltpu.make_async_remote_copy(
                    src_ref=src_refs[j], dst_ref=dst, send_sem=send_sems.at[7 * j + k - 1],
                    recv_sem=recv_sems.at[7 * j + k - 1], device_id=_peer(k, x, y, c), device_id_type=MESH))
        return remote, local

    lands = [jax.ShapeDtypeStruct((N_DEV * a.shape[0], a.shape[1]), a.dtype) for a in shards]
    return _split_start(name, shards, lands, 7 * m, m, build, after)


def _slots_start(name, a):
    def build(src_refs, land_refs, send_sems, recv_sems, local_sems):
        x, y, c = lax.axis_index("x"), lax.axis_index("y"), lax.axis_index("c")
        dst = land_refs[0].at[4 * x + 2 * y + c]
        local = [pltpu.make_async_copy(src_refs[0], dst, local_sems.at[0])]
        remote = [pltpu.make_async_remote_copy(
            src_ref=src_refs[0], dst_ref=dst, send_sem=send_sems.at[k - 1], recv_sem=recv_sems.at[k - 1],
            device_id=_peer(k, x, y, c), device_id_type=MESH) for k in range(1, N_DEV)]
        return remote, local

    return _split_start(name, [a], [jax.ShapeDtypeStruct((N_DEV,) + a.shape, a.dtype)], 7, 1, build)


def _chips_start(name, p):
    _, r, c = p.shape
    nck = r // GRAD_ROW_TILE

    def build(src_refs, land_refs, send_sems, recv_sems):
        x, y, cc = lax.axis_index("x"), lax.axis_index("y"), lax.axis_index("c")
        remote = []
        for k in range(1, 4):
            px = 1 - x if k >> 1 else x
            py = 1 - y if k & 1 else y
            for j in range(nck):
                rows = pl.ds(j * GRAD_ROW_TILE, GRAD_ROW_TILE)
                remote.append(pltpu.make_async_remote_copy(
                    src_ref=src_refs[0].at[2 * px + py, rows], dst_ref=land_refs[0].at[k - 1, rows],
                    send_sem=send_sems.at[(k - 1) * nck + j], recv_sem=recv_sems.at[(k - 1) * nck + j],
                    device_id=(px, py, cc), device_id_type=MESH))
        return remote, []

    return _split_start(name, [p], [jax.ShapeDtypeStruct((3, r, c), p.dtype)], 3 * nck, 0, build)


def _chip_sum(name, p, recv, chip):
    _, r, c = p.shape
    tr = GRAD_ROW_TILE

    def body(chip_ref, p_ref, r_ref, o_ref):
        acc = p_ref[...].astype(F32)
        for k in range(3):
            acc = acc + r_ref[k].astype(F32)
        o_ref[...] = acc

    return pl.pallas_call(
        body, name=name,
        grid_spec=pltpu.PrefetchScalarGridSpec(
            num_scalar_prefetch=1, grid=(r // tr,),
            in_specs=[pl.BlockSpec((None, tr, c), lambda i, chip_ref: (chip_ref[0], i, 0)),
                      pl.BlockSpec((3, tr, c), lambda i, chip_ref: (0, i, 0))],
            out_specs=pl.BlockSpec((tr, c), lambda i, chip_ref: (i, 0))),
        out_shape=_out(r, c, F32), compiler_params=_params(("parallel",)),
    )(chip, p, recv)


def _local_step(x, mem, tgt, wt, sm, ev=None):
    t, d = x.shape
    n_mem = mem.shape[0]
    d_pool = sm["pool_scale"].shape[1]
    ng, pc = sm["pool_w"].shape[0], sm["pool_w"].shape[1]
    d_ssm = sm["ssm_d"].shape[1]
    _, sg, sp, sh = sm["ssm_b_re"].shape
    n_state = sg * sp
    gb, gs = {}, {}

    def emit(name, **kw):
        return ev(name, **kw) if ev is not None else None

    n1 = _rms_fwd("ffn1_norm", x, sm["ffn1_norm"])
    emit("ffn1_norm_done", marker=n1)
    h1, ffn1_saved = _ffn_fwd("ffn1", x, n1, wt["ffn1_w_gate"], wt["ffn1_w_up"], wt["ffn1_w_down"])
    emit("ffn1_fwd_done", marker=h1)
    u = _rms_fwd("mix_norm", h1, sm["mix_norm"])
    d_in = wt["w_in"].shape[0]
    tm, tn = _pick(t, 1024), _pick(d_in, 1408)
    proj = _mm1("in_proj", "nt", u, wt["w_in"], t, d_in, tm, tn, F32)
    off_s = d_pool // d_ssm
    off_gp = (d_pool + d_ssm)
    off_gs = off_gp + d

    pool_w_bf = sm["pool_w"].astype(BF16)
    pooled, pm = _pool_fwd(proj, pool_w_bf, sm["pool_scale"])

    cols = [sm["ssm_a_re"].reshape(-1, 1), sm["ssm_a_im"].reshape(-1, 1),
            jnp.broadcast_to(sm["ssm_log_dt"][:, :, None], (2, sg, sp)).reshape(-1, 1),
            sm["ssm_b_re"].reshape(-1, sh), sm["ssm_b_im"].reshape(-1, sh)]
    abr, abi, bbr, bbi = _ssm_disc(cols)
    abr2, abi2 = abr.reshape(2, n_state), abi.reshape(2, n_state)
    bbr4, bbi4 = bbr.reshape(2, sg * sp, sh), bbi.reshape(2, sg * sp, sh)
    b_re = [_bd_in(bbr4[dr], sg, sp, sh).astype(BF16) for dr in range(2)]
    b_im = [_bd_in(bbi4[dr], sg, sp, sh).astype(BF16) for dr in range(2)]
    c_re = [_bd_out(sm["ssm_c_re"][dr], sg, sp, sh).astype(BF16) for dr in range(2)]
    c_im = [_bd_out(-sm["ssm_c_im"][dr], sg, sp, sh).astype(BF16) for dr in range(2)]
    tms = _pick(t, 512)
    s_bf = _ew("ssm_cast", lambda v: (v,), [proj[:, d_pool:d_pool + d_ssm]], [BF16])[0]
    xs = []
    for dr in range(2):
        u_d = _mm1(f"ssm_in{dr}", "nn", s_bf, jnp.concatenate([b_re[dr], b_im[dr]], axis=1), t, 2 * n_state, tms,
                   _pick(2 * n_state, 512), F32)
        xs.append(_scan(f"ssm_scan{dr}", u_d, abr2[dr:dr + 1], abi2[dr:dr + 1], reverse=(dr == 1), conj=False))
    tmy = _pick(t, 256)
    x_list = [xs[0][0], xs[0][1], xs[1][0], xs[1][1]]
    y = _mm("ssm_out", "nn", x_list, [c_re[0], c_im[0], c_re[1], c_im[1]], [[(k, k) for k in range(4)]], t, d_ssm, tmy,
            d_ssm, [(proj, _tile(tmy, d_ssm, off_s)), (sm["ssm_d"], _rowvec(d_ssm))],
            lambda accs, sv, dv: (sv * dv + accs[0],), [(_out(t, d_ssm, F32), None)])[0]
    ys = _ew("ssm_gelu", lambda v: (jax.nn.gelu(v),), [y], [BF16])[0]
    emit("mix_in_done", marker=ys)

    tmm, tnm, tnx = _pick(t, 1024), _pick(d, 256), _pick(d, 512)
    gp_spec = _tile(tmm, tnm, off_gp // tnm)
    gs_spec = _tile(tmm, tnm, off_gs // tnm)

    def merge_epi(accs, gpv, gsv):
        z_pool, val, gate = accs
        return (jax.nn.sigmoid(gpv) * z_pool + jax.nn.sigmoid(gsv) * (val * jax.nn.sigmoid(gate)),)

    merged = _mm("mix_merge", "nt", [pm, ys], [wt["w_pool_proj"], wt["w_glu_val"], wt["w_glu_gate"]],
                 [[(0, 0)], [(1, 1)], [(1, 2)]], t, d, tmm, tnm, [(proj, gp_spec), (proj, gs_spec)], merge_epi,
                 [(_out(t, d, BF16), None)])[0]
    res_epi = lambda accs, hin: (hin + accs[0],)
    h2 = _mm("mix_out", "nn", [merged], [wt["w_mix_out"]], [[(0, 0)]], t, d, tmm, tnx, [(h1, _tile(tmm, tnx))],
             res_epi, [(_out(t, d, F32), None)])[0]

    un = _rms_fwd("xattn_norm", h2, sm["xattn_norm"])
    mn = _rms_fwd("mem_norm", mem, sm["mem_norm"])
    q = _mm1("xattn_q", "nn", un, wt["w_q"], t, d, tmm, tnx, BF16)
    kv = _mm1("xattn_kv", "nt", mn, wt["w_kv"], n_mem, 2 * d, n_mem, _pick(2 * d, 512), BF16)
    o = _attn_fwd(q, kv)
    h3 = _mm("xattn_out", "nn", [o], [wt["w_xo"]], [[(0, 0)]], t, d, tmm, tnx, [(h2, _tile(tmm, tnx))],
             res_epi, [(_out(t, d, F32), None)])[0]

    n2 = _rms_fwd("ffn2_norm", h3, sm["ffn2_norm"])
    h4, ffn2_saved = _ffn_fwd("ffn2", h3, n2, wt["ffn2_w_gate"], wt["ffn2_w_up"], wt["ffn2_w_down"])

    dh4, dh4_bf, gs["final_norm"], loss = _loss_head(h4, sm["final_norm"], tgt)
    dh3, dh3_bf, gs["ffn2_norm"], gb["ffn2_w_gate"], gb["ffn2_w_up"], gb["ffn2_w_down"] = _ffn_bwd(
        "ffn2", h3, sm["ffn2_norm"], wt["ffn2_w_gate"], wt["ffn2_w_up"], wt["ffn2_w_down"], ffn2_saved, dh4, dh4_bf)

    tw = _pick(d, 1024)
    do = _mm1("xattn_do", "nt", dh3_bf, wt["w_xo"], t, d, tmm, tnx, BF16)
    gb["w_xo"] = _mm1("xattn_dwxo", "tn", o, dh3_bf, d, d, tw, tnx, BF16)
    dq, dkv = _attn_bwd(q, kv, do)
    gb["w_q"] = _mm1("xattn_dwq", "tn", un, dq, d, d, tw, tnx, BF16)
    dun = _mm1("xattn_dun", "nt", dq, wt["w_q"], t, d, tmm, tnx, F32)
    dh2, dh2_bf, gs["xattn_norm"] = _rms_bwd("xattn_norm_bwd", h2, sm["xattn_norm"], dun, dh3)
    gb["w_kv"] = _mm1("xattn_dwkv", "tn", dkv, mn, 2 * d, d, _pick(2 * d, 512), d, BF16)
    dmn = _mm1("xattn_dmn", "nn", dkv, wt["w_kv"], n_mem, d, n_mem, tnx, F32)
    gs["mem_norm"] = _rms_bwd("mem_norm_bwd", mem, sm["mem_norm"], dmn)

    gb["w_mix_out"] = _mm1("mix_dwout", "tn", merged, dh2_bf, d, d, tw, tnx, BF16)

    def merge_bwd_epi(accs, gpv, gsv):
        dmerged, z_pool, val, gate = accs
        sp_, ss_, sg_ = jax.nn.sigmoid(gpv), jax.nn.sigmoid(gsv), jax.nn.sigmoid(gate)
        glu = val * sg_
        dz_pool = dmerged * sp_
        dg_pool = dmerged * z_pool * (sp_ * (1.0 - sp_))
        dz_ssm = dmerged * ss_
        dg_ssm = dmerged * glu * (ss_ * (1.0 - ss_))
        dval = dz_ssm * sg_
        dgate = dz_ssm * glu * (1.0 - sg_)
        return dz_pool, dg_pool, dg_ssm, dval, dgate

    dz_pool, dg_pool, dg_ssm, dval, dgate = _mm(
        "mix_merge_bwd", "nt", [dh2_bf, pm, ys], [wt["w_mix_out"], wt["w_pool_proj"], wt["w_glu_val"], wt["w_glu_gate"]],
        [[(0, 0)], [(1, 1)], [(2, 2)], [(2, 3)]], t, d, tmm, tnm, [(proj, gp_spec), (proj, gs_spec)], merge_bwd_epi,
        [(_out(t, d, BF16), None)] * 5)
    gb["w_pool_proj"] = _mm1("pool_dwproj", "tn", dz_pool, pm, d, d_pool, tw, d_pool, BF16)
    gb["w_glu_val"] = _mm1("glu_dwval", "tn", dval, ys, d, d_ssm, tw, d_ssm, BF16)
    gb["w_glu_gate"] = _mm1("glu_dwgate", "tn", dgate, ys, d, d_ssm, tw, d_ssm, BF16)

    def gelu_bwd_epi(accs, yv):
        _, vjp = jax.vjp(jax.nn.gelu, yv)
        return (vjp(accs[0])[0],)

    dy = _mm("glu_dy", "nn", [dval, dgate], [wt["w_glu_val"], wt["w_glu_gate"]], [[(0, 0), (1, 1)]], t, d_ssm, tmy, d_ssm,
             [(y, _tile(tmy, d_ssm))], gelu_bwd_epi, [(_out(t, d_ssm, F32), None)])[0]
    gs["ssm_d"] = _colsum_prod("ssm_dd", dy, proj, b_coff=off_s)
    dy_bf = _ew("ssm_dy_cast", lambda v: (v,), [dy], [BF16])[0]
    d_abr, d_abi, d_bbr, d_bbi, d_cre, d_cim, lams = [], [], [], [], [], [], []
    ts = _pick(n_state, 512)
    tc_ = _pick(n_state, 256)
    for dr in range(2):
        gx = _mm1(f"ssm_gx{dr}", "nt", dy_bf, jnp.concatenate([c_re[dr], c_im[dr]], axis=0), t, 2 * n_state, tms,
                  _pick(2 * n_state, 512), F32)
        lr, li = _scan(f"ssm_adj{dr}", gx, abr2[dr:dr + 1], abi2[dr:dr + 1], reverse=(dr == 0), conj=True)
        dar, dai = _ssm_da(f"ssm_da{dr}", lr, li, xs[dr][0], xs[dr][1], reverse=(dr == 1))
        d_abr.append(dar)
        d_abi.append(dai)
        lams += [lr, li]
        d_bbr.append(_diag_in(_mm1(f"ssm_dbre{dr}", "tn", s_bf, lr, d_ssm, n_state, d_ssm, ts, F32), sg, sp, sh))
        d_bbi.append(_diag_in(_mm1(f"ssm_dbim{dr}", "tn", s_bf, li, d_ssm, n_state, d_ssm, ts, F32), sg, sp, sh))
        d_cre.append(_diag_out(_mm1(f"ssm_dcre{dr}", "tn", xs[dr][0], dy_bf, n_state, d_ssm, tc_, d_ssm, F32), sg, sp, sh))
        d_cim.append(-_diag_out(_mm1(f"ssm_dcim{dr}", "tn", xs[dr][1], dy_bf, n_state, d_ssm, tc_, d_ssm, F32), sg, sp, sh))
    ds = _mm("ssm_ds", "nt", lams, [b_re[0], b_im[0], b_re[1], b_im[1]], [[(k, k) for k in range(4)]], t, d_ssm, tmy,
             d_ssm, [(dy, _tile(tmy, d_ssm)), (sm["ssm_d"], _rowvec(d_ssm))],
             lambda accs, dyv, dv: (dyv * dv + accs[0],), [(_out(t, d_ssm, BF16), None)])[0]
    cots = [jnp.concatenate(d_abr, axis=0).reshape(-1, 1), jnp.concatenate(d_abi, axis=0).reshape(-1, 1),
            jnp.concatenate(d_bbr, axis=0), jnp.concatenate(d_bbi, axis=0)]
    d_are, d_aim, d_ldt, d_bre, d_bim = _ssm_disc_bwd(cols, cots)
    gs["ssm_a_re"] = d_are.reshape(2, sg, sp)
    gs["ssm_a_im"] = d_aim.reshape(2, sg, sp)
    gs["ssm_log_dt"] = _rowsum("ssm_dlogdt", d_ldt.reshape(2 * sg, sp)).reshape(2, sg)
    gs["ssm_b_re"] = d_bre.reshape(2, sg, sp, sh)
    gs["ssm_b_im"] = d_bim.reshape(2, sg, sp, sh)
    gs["ssm_c_re"] = jnp.stack(d_cre, axis=0)
    gs["ssm_c_im"] = jnp.stack(d_cim, axis=0)

    dpm = _mm1("pool_dpm", "nn", dz_pool, wt["w_pool_proj"], t, d_pool, tmm, _pick(d_pool, 256), F32)
    dp, gs["pool_w"], gs["pool_scale"] = _pool_bwd(pooled, dpm, pool_w_bf, sm["pool_scale"])

    w_in = wt["w_in"]
    parts = [(dp, 0, d_pool), (ds, d_pool, d_ssm), (dg_pool, off_gp, d), (dg_ssm, off_gs, d)]
    w_in_parts = [w_in[o0:o0 + width] for _, o0, width in parts]
    gb["w_in"] = jnp.concatenate(
        [_mm1(f"in_proj_dw{k}", "tn", p_[0], u, p_[2], d, _pick(p_[2], 1024), tnx, BF16) for k, p_ in enumerate(parts)], axis=0)
    pin = emit("grads_main", gb=gb)
    du = _mm("in_proj_du", "nn", [p_[0] for p_ in parts], w_in_parts, [[(k, k) for k in range(4)]], t, d, tmm, tnx, [],
             lambda accs: (accs[0],), [(_out(t, d, F32), None)], after=pin)[0]
    dh1, dh1_bf, gs["mix_norm"] = _rms_bwd("mix_norm_bwd", h1, sm["mix_norm"], du, dh2)
    pin = emit("small_early", gs=gs, loss=loss)

    def ffn1_weights_done(d_wg, d_wu, d_wd):
        gb["ffn1_w_gate"], gb["ffn1_w_up"], gb["ffn1_w_down"] = d_wg, d_wu, d_wd
        return emit("grads_ffn1", gb=gb)

    dx, _, gs["ffn1_norm"], _, _, _ = _ffn_bwd(
        "ffn1", x, sm["ffn1_norm"], wt["ffn1_w_gate"], wt["ffn1_w_up"], wt["ffn1_w_down"], ffn1_saved, dh1, dh1_bf,
        weights_done=ffn1_weights_done, after=pin)
    return loss, dx, gb, gs


WEIGHTS = ["ffn1_norm", "ffn1_w_gate", "ffn1_w_up", "ffn1_w_down", "mix_norm", "w_in", "pool_w", "pool_scale",
           "w_pool_proj", "ssm_a_re", "ssm_a_im", "ssm_log_dt", "ssm_b_re", "ssm_b_im", "ssm_c_re", "ssm_c_im", "ssm_d",
           "w_glu_val", "w_glu_gate", "w_mix_out", "xattn_norm", "mem_norm", "w_q", "w_kv", "w_xo", "ffn2_norm",
           "ffn2_w_gate", "ffn2_w_up", "ffn2_w_down", "final_norm"]
COL_SHARDED = ["ffn1_w_gate", "ffn1_w_up", "w_in", "w_pool_proj", "w_glu_val", "w_glu_gate", "w_kv", "ffn2_w_gate",
               "ffn2_w_up"]
ROW_SHARDED = ["ffn1_w_down", "w_mix_out", "w_q", "w_xo", "ffn2_w_down"]
BIG = [n for n in WEIGHTS if n in COL_SHARDED or n in ROW_SHARDED]
SMALL = [n for n in WEIGHTS if n not in BIG]
FFN1_BIG = ["ffn1_w_gate", "ffn1_w_up", "ffn1_w_down"]
MAIN_BIG = [n for n in BIG if n not in FFN1_BIG]
LATE_SMALL = "ffn1_norm"
EARLY_SMALL = [n for n in SMALL if n != LATE_SMALL]
PACK_ROWS = SUBLANES * LANES
GRAD_ROW_TILE = 256


def _to_rows(name, w, width):
    if name in COL_SHARDED:
        w = w.T
    return w.reshape(-1, width)


def _from_rows(name, rows, shard_shape):
    if name in COL_SHARDED:
        return rows.reshape(shard_shape[1], shard_shape[0]).T
    return rows.reshape(shard_shape)


def _pack_small(vals):
    flat = []
    for v in vals:
        f = v.reshape(-1)
        flat.append(jnp.pad(f, (0, (-f.shape[0]) % PACK_ROWS)))
    total = sum(f.shape[0] for f in flat)
    flat.append(jnp.zeros(((-total) % (GRAD_ROW_TILE * LANES),), F32))
    return jnp.concatenate(flat).reshape(-1, LANES)


def _unpack_small(packed, shapes):
    out, row = [], 0
    for shp in shapes:
        size = math.prod(shp)
        rows = -(-size // PACK_ROWS) * SUBLANES
        out.append(packed[row:row + rows].reshape(-1)[:size].reshape(shp))
        row += rows
    return out


def kernel(x, mem, ffn1_norm, ffn1_w_gate, ffn1_w_up, ffn1_w_down, mix_norm, w_in, pool_w, pool_scale, w_pool_proj, ssm_a_re, ssm_a_im, ssm_log_dt, ssm_b_re, ssm_b_im, ssm_c_re, ssm_c_im, ssm_d, w_glu_val, w_glu_gate, w_mix_out, xattn_norm, mem_norm, w_q, w_kv, w_xo, ffn2_norm, ffn2_w_gate, ffn2_w_up, ffn2_w_down, final_norm, loss_target, m_ffn1_norm, m_ffn1_w_gate, m_ffn1_w_up, m_ffn1_w_down, m_mix_norm, m_w_in, m_pool_w, m_pool_scale, m_w_pool_proj, m_ssm_a_re, m_ssm_a_im, m_ssm_log_dt, m_ssm_b_re, m_ssm_b_im, m_ssm_c_re, m_ssm_c_im, m_ssm_d, m_w_glu_val, m_w_glu_gate, m_w_mix_out, m_xattn_norm, m_mem_norm, m_w_q, m_w_kv, m_w_xo, m_ffn2_norm, m_ffn2_w_gate, m_ffn2_w_up, m_ffn2_w_down, m_final_norm, v_ffn1_norm, v_ffn1_w_gate, v_ffn1_w_up, v_ffn1_w_down, v_mix_norm, v_w_in, v_pool_w, v_pool_scale, v_w_pool_proj, v_ssm_a_re, v_ssm_a_im, v_ssm_log_dt, v_ssm_b_re, v_ssm_b_im, v_ssm_c_re, v_ssm_c_im, v_ssm_d, v_w_glu_val, v_w_glu_gate, v_w_mix_out, v_xattn_norm, v_mem_norm, v_w_q, v_w_kv, v_w_xo, v_ffn2_norm, v_ffn2_w_gate, v_ffn2_w_up, v_ffn2_w_down, v_final_norm):
    given = dict(locals())
    wts = {n: given[n] for n in WEIGHTS}
    moms = {n: (given["m_" + n], given["v_" + n]) for n in WEIGHTS}
    x2, mem2, tgt2 = x[0], mem[0], loss_target[0]
    d = x2.shape[1]
    core = lax.axis_index("c").astype(jnp.int32).reshape(1)
    chip = (2 * lax.axis_index("x") + lax.axis_index("y")).astype(jnp.int32).reshape(1)

    def full_form(n, f):
        shard = wts[n][0].shape
        return f.reshape(N_DEV * shard[1], shard[0]) if n in COL_SHARDED else f.reshape(N_DEV * shard[0], shard[1])

    shards = {n: _to_rows(n, wts[n][0], d).astype(BF16) for n in BIG}
    wt = {n: full_form(n, f) for n, f in zip(FFN1_BIG, _allgather("weight_allgather_ffn1", [shards[n] for n in FFN1_BIG]))}
    rest = [n for n in MAIN_BIG if n != "w_in"]
    gather_in = _gather_start("weight_gather_in_start", [shards["w_in"]], wt[FFN1_BIG[0]])
    gather_rest = _gather_start("weight_gather_rest_start", [shards[n] for n in rest], gather_in["token"])
    sm = {n: (wts[n].reshape(1, -1) if wts[n].ndim <= 2 else wts[n][0]) for n in SMALL}
    sm["ffn1_norm"] = sm["ffn1_norm"] + gather_rest["token"][0, 0]

    pending = {}

    def reduce_start(tag, names, gb):
        blocks = [gb[n].reshape(N_DEV, -1, d) for n in names]
        pad_rows = (-sum(b.shape[1] for b in blocks)) % GRAD_ROW_TILE
        packed = jnp.concatenate(blocks + ([jnp.zeros((N_DEV, pad_rows, d), BF16)] if pad_rows else []), axis=1)
        pair = _pair_sum("grad_pair_sum_" + tag, packed, _exchange_cores("grad_exchange_cores_" + tag, packed), core)
        pending[tag] = (pair, _chips_start("grad_exchange_chips_start_" + tag, pair), [b.shape[1] for b in blocks])
        return pending[tag][1]["token"]

    def reduce_finish(tag, after):
        pair, started, rows = pending[tag]
        recv = _split_wait("grad_exchange_chips_wait_" + tag, started, after)[0]
        return _chip_sum("grad_chip_sum_" + tag, pair, recv, chip), rows

    def ev(name, gb=None, gs=None, loss=None, marker=None):
        if name == "ffn1_fwd_done":
            wt["w_in"] = full_form("w_in", _split_wait("weight_gather_in_wait", gather_in, marker)[0])
        elif name == "mix_in_done":
            for n, f in zip(rest, _split_wait("weight_gather_rest_wait", gather_rest, marker)):
                wt[n] = full_form(n, f)
        elif name == "grads_main":
            return reduce_start("main", MAIN_BIG, gb)
        elif name == "small_early":
            pending["small"] = _slots_start("small_gather_start", _pack_small([gs[n] for n in EARLY_SMALL] + [loss[:, :1]]))
            return pending["small"]["token"]
        elif name == "grads_ffn1":
            return reduce_start("ffn1", FFN1_BIG, gb)
        return None

    _, dx, _, gs = _local_step(x2, mem2, tgt2, wt, sm, ev)

    out_g, out_d, out_m, out_v = {}, {}, {}, {}

    def update(n, g_full):
        shape = wts[n].shape
        two_d = (-1, shape[-1])
        dl, m2, v2 = _adamw("adamw_" + n, wts[n].reshape(two_d), g_full.reshape(two_d), moms[n][0].reshape(two_d),
                            moms[n][1].reshape(two_d))
        out_g[n], out_d[n], out_m[n], out_v[n] = g_full, dl.reshape(shape), m2.reshape(shape), v2.reshape(shape)
        return dl

    def update_big(names, g_rows, rows):
        off = 0
        for n, r in zip(names, rows):
            shard = wts[n].shape
            dl = update(n, _from_rows(n, g_rows[off:off + r], shard[1:]).reshape(shard))
            off += r
        return dl

    last = update_big(MAIN_BIG, *reduce_finish("main", dx))

    small_sum = _sum_slots("small_sum", _split_wait("small_gather_wait", pending["small"], dx)[0], F32)
    late = _allgather("small_allgather_late", [gs[LATE_SMALL].reshape(-1, LANES)])[0]
    late_sum = _sum_slots("small_sum_late", late.reshape(N_DEV, -1, LANES), F32)
    vals = _unpack_small(small_sum, [wts[n].shape for n in EARLY_SMALL] + [(1, 1)])
    total_loss = vals[-1].reshape(())
    for n, g_full in zip(EARLY_SMALL + [LATE_SMALL], vals[:-1] + [late_sum.reshape(wts[LATE_SMALL].shape)]):
        update(n, g_full)

    update_big(FFN1_BIG, *reduce_finish("ffn1", last))

    return (total_loss, dx[None], *[out_g[n] for n in WEIGHTS], *[out_d[n] for n in WEIGHTS],
            *[out_m[n] for n in WEIGHTS], *[out_v[n] for n in WEIGHTS])
```

```python
import functools
import math

import jax
import jax.numpy as jnp
from jax import lax
from jax.experimental import pallas as pl
from jax.experimental.pallas import tpu as pltpu

F32 = jnp.float32
BF16 = jnp.bfloat16
EPS = 1e-6
N_XHEADS = 4
POOL_WINDOWS = (2, 4, 8, 16)
ADAM_LR = 0.001
ADAM_B1 = 0.9
ADAM_B2 = 0.999
ADAM_EPS = 1e-08
ADAM_WD = 0.01
ADAM_STEP = 10
N_DEV = 8
VMEM_LIMIT_V7X = 48 * 1024 * 1024
LANES = 128
SUBLANES = 8
SUB_ROWS = 256
POOL_PAD = 16
MESH = pl.DeviceIdType.MESH
ANY = pl.BlockSpec(memory_space=pl.ANY)
HBM = pl.BlockSpec(memory_space=pltpu.HBM)
SEM = pl.BlockSpec(memory_space=pltpu.SEMAPHORE)
SIDE_EFFECT = pltpu.SideEffectType.DATAFLOW_SIDE_EFFECTING

_DIMS = {
    "nt": (((1,), (1,)), ((), ())),
    "nn": (((1,), (0,)), ((), ())),
    "tn": (((0,), (0,)), ((), ())),
}


def _pick(dim, pref, mult=LANES):
    if dim <= pref:
        return dim
    for t in range(pref - pref % mult, 0, -mult):
        if dim % t == 0:
            return t
    return dim


def _params(sem):
    return pltpu.CompilerParams(dimension_semantics=sem, vmem_limit_bytes=VMEM_LIMIT_V7X)


def _tile(tm, tn, coff=0):
    return pl.BlockSpec((tm, tn), lambda i, j: (i, j + coff))


def _rowvec(tn, coff=0):
    return pl.BlockSpec((1, tn), lambda i, j: (0, j + coff))


def _out(m, n, dtype):
    return jax.ShapeDtypeStruct((m, n), dtype)


def _mm(name, form, a_list, b_list, groups, m, n, tm, tn, extras, epilogue, outs, after=None, sub=SUB_ROWS):
    na, nb, ne = len(a_list), len(b_list), len(extras)
    pins = [] if after is None else [after]
    step = tm if (sub is None or form == "tn" or tm % sub) else sub

    def a_spec(a):
        if form == "tn":
            return pl.BlockSpec((a.shape[0], tm), lambda i, j: (0, i))
        return pl.BlockSpec((tm, a.shape[1]), lambda i, j: (i, 0))

    def b_spec(b):
        if form == "nt":
            return pl.BlockSpec((tn, b.shape[1]), lambda i, j: (j, 0))
        return pl.BlockSpec((b.shape[0], tn), lambda i, j: (0, j))

    def body(*refs):
        a_refs, b_refs = refs[:na], refs[na:na + nb]
        e_refs, o_refs = refs[na + nb:na + nb + ne], refs[na + nb + ne + len(pins):]
        b_vals = {}
        for s0 in range(0, tm, step):
            rows = slice(None) if step == tm else pl.ds(s0, step)
            a_vals, accs = {}, []
            for group in groups:
                acc = None
                for ai, bi in group:
                    if ai not in a_vals:
                        a_vals[ai] = (a_refs[ai][...] if form == "tn" else a_refs[ai][rows, :]).astype(BF16)
                    if bi not in b_vals:
                        b_vals[bi] = b_refs[bi][...].astype(BF16)
                    d = lax.dot_general(a_vals[ai], b_vals[bi], _DIMS[form], preferred_element_type=F32)
                    acc = d if acc is None else acc + d
                accs.append(acc)
            res = epilogue(accs, *[e[rows, :] if e.shape[0] == tm else e[...] for e in e_refs])
            for o_ref, r in zip(o_refs, res):
                o_ref[rows, :] = r.astype(o_ref.dtype)

    out_specs = [_tile(tm, tn) if s is None else s for _, s in outs]
    res = pl.pallas_call(
        body, name=name, grid=(m // tm, n // tn),
        in_specs=[a_spec(a) for a in a_list] + [b_spec(b) for b in b_list] + [s for _, s in extras] + [ANY] * len(pins),
        out_specs=out_specs, out_shape=[o for o, _ in outs],
        compiler_params=_params(("parallel", "parallel")),
    )(*a_list, *b_list, *[e for e, _ in extras], *pins)
    return res


def _mm1(name, form, a, b, m, n, tm, tn, dtype, scale=None):
    epi = (lambda accs: (accs[0],)) if scale is None else (lambda accs: (accs[0] * scale,))
    return _mm(name, form, [a], [b], [[(0, 0)]], m, n, tm, tn, [], epi, [(_out(m, n, dtype), None)])[0]


def _rms_fwd(name, h, g):
    t, d = h.shape
    tm = _pick(t, 512, SUBLANES)

    def body(h_ref, g_ref, n_ref):
        hv = h_ref[...]
        r = lax.rsqrt(jnp.mean(hv * hv, axis=-1, keepdims=True) + EPS)
        n_ref[...] = ((hv * r) * g_ref[...]).astype(BF16)

    return pl.pallas_call(
        body, name=name, grid=(t // tm,),
        in_specs=[pl.BlockSpec((tm, d), lambda i: (i, 0)), pl.BlockSpec((1, d), lambda i: (0, 0))],
        out_specs=pl.BlockSpec((tm, d), lambda i: (i, 0)), out_shape=_out(t, d, BF16),
        compiler_params=_params(("parallel",)),
    )(h, g)


def _rms_bwd(name, h, g, dn, dres=None):
    t, d = h.shape
    tm = _pick(t, 512, SUBLANES)
    need_dh = dres is not None

    def body(*refs):
        if need_dh:
            h_ref, g_ref, dn_ref, dres_ref, dh_ref, dhb_ref, dg_ref = refs
        else:
            h_ref, g_ref, dn_ref, dg_ref = refs
        hv = h_ref[...]
        r = lax.rsqrt(jnp.mean(hv * hv, axis=-1, keepdims=True) + EPS)
        nh = hv * r
        dnv = dn_ref[...].astype(F32)

        @pl.when(pl.program_id(0) == 0)
        def _():
            dg_ref[...] = jnp.zeros_like(dg_ref)

        dg_ref[...] += jnp.sum(dnv * nh, axis=0, keepdims=True)
        if need_dh:
            dng = dnv * g_ref[...]
            dh = dres_ref[...] + r * (dng - nh * jnp.mean(dng * nh, axis=-1, keepdims=True))
            dh_ref[...] = dh
            dhb_ref[...] = dh.astype(BF16)

    row = pl.BlockSpec((tm, d), lambda i: (i, 0))
    vec = pl.BlockSpec((1, d), lambda i: (0, 0))
    if need_dh:
        return pl.pallas_call(
            body, name=name, grid=(t // tm,), in_specs=[row, vec, row, row], out_specs=[row, row, vec],
            out_shape=[_out(t, d, F32), _out(t, d, BF16), _out(1, d, F32)], compiler_params=_params(("arbitrary",)),
        )(h, g, dn, dres)
    return pl.pallas_call(
        body, name=name, grid=(t // tm,), in_specs=[row, vec, row], out_specs=vec,
        out_shape=_out(1, d, F32), compiler_params=_params(("arbitrary",)),
    )(h, g, dn)


def _loss_head(h, g, tgt):
    t, d = h.shape
    tm = _pick(t, 512, SUBLANES)

    def body(h_ref, g_ref, t_ref, dh_ref, dhb_ref, dg_ref, loss_ref):
        hv = h_ref[...]
        r = lax.rsqrt(jnp.mean(hv * hv, axis=-1, keepdims=True) + EPS)
        nh = hv * r
        err = nh * g_ref[...] - t_ref[...]

        @pl.when(pl.program_id(0) == 0)
        def _():
            dg_ref[...] = jnp.zeros_like(dg_ref)
            loss_ref[...] = jnp.zeros_like(loss_ref)

        per_row = jnp.mean(err * err, axis=-1, keepdims=True)
        loss_ref[...] += 0.5 * jnp.sum(per_row, axis=0, keepdims=True)
        dy = err * (1.0 / d)
        dg_ref[...] += jnp.sum(dy * nh, axis=0, keepdims=True)
        dng = dy * g_ref[...]
        dh = r * (dng - nh * jnp.mean(dng * nh, axis=-1, keepdims=True))
        dh_ref[...] = dh
        dhb_ref[...] = dh.astype(BF16)

    row = pl.BlockSpec((tm, d), lambda i: (i, 0))
    vec = pl.BlockSpec((1, d), lambda i: (0, 0))
    return pl.pallas_call(
        body, name="loss_head", grid=(t // tm,), in_specs=[row, vec, row],
        out_specs=[row, row, vec, pl.BlockSpec((1, LANES), lambda i: (0, 0))],
        out_shape=[_out(t, d, F32), _out(t, d, BF16), _out(1, d, F32), _out(1, LANES, F32)],
        compiler_params=_params(("arbitrary",)),
    )(h, g, tgt)


def _ffn_fwd(tag, h, n, wg_t, wu_t, wd):
    t, d = h.shape
    f = wd.shape[0]
    tm, tn = _pick(t, 1024), _pick(f, 1408)

    def up_epi(accs):
        a, b = accs
        return a, b, (a * jax.nn.sigmoid(a)) * b

    a, b, hid = _mm(tag + "_up", "nt", [n], [wg_t, wu_t], [[(0, 0)], [(0, 1)]], t, f, tm, tn, [], up_epi,
                    [(_out(t, f, BF16), None)] * 3)
    tm2, tn2 = _pick(t, 1024), _pick(d, 512)
    h_out = _mm(tag + "_down", "nn", [hid], [wd], [[(0, 0)]], t, d, tm2, tn2, [(h, _tile(tm2, tn2))],
                lambda accs, hin: (hin + 0.5 * accs[0],), [(_out(t, d, F32), None)])[0]
    return h_out, (n, a, b, hid)


def _ffn_bwd(tag, h, g, wg_t, wu_t, wd, saved, dh, dh_bf, weights_done=None, after=None):
    n, a, b, hid = saved
    t, d = h.shape
    f = wd.shape[0]
    tm, tn = _pick(t, 1024), _pick(f, 1408)

    def hid_epi(accs, av, bv):
        dhid = 0.5 * accs[0]
        av, bv = av.astype(F32), bv.astype(F32)
        sig = jax.nn.sigmoid(av)
        da = dhid * bv * (sig * (1.0 + av * (1.0 - sig)))
        db = dhid * (av * sig)
        return da, db

    da, db = _mm(tag + "_bwd_hid", "nt", [dh_bf], [wd], [[(0, 0)]], t, f, tm, tn,
                 [(a, _tile(tm, tn)), (b, _tile(tm, tn))], hid_epi, [(_out(t, f, BF16), None)] * 2, after=after)
    tw, tnw = _pick(f, 1408), _pick(d, 512)
    d_wd = _mm1(tag + "_dwd", "tn", hid, dh_bf, f, d, tw, tnw, BF16, scale=0.5)
    d_wg = _mm1(tag + "_dwg", "tn", da, n, f, d, tw, tnw, BF16)
    d_wu = _mm1(tag + "_dwu", "tn", db, n, f, d, tw, tnw, BF16)
    pin = weights_done(d_wg, d_wu, d_wd) if weights_done is not None else None
    tm2, tn2 = _pick(t, 1024), _pick(d, 512)
    dn = _mm(tag + "_dn", "nn", [da, db], [wg_t, wu_t], [[(0, 0), (1, 1)]], t, d, tm2, tn2, [],
             lambda accs: (accs[0],), [(_out(t, d, F32), None)], after=pin)[0]
    dh_in, dh_in_bf, dg = _rms_bwd(tag + "_norm_bwd", h, g, dn, dh)
    return dh_in, dh_in_bf, dg, d_wg, d_wu, d_wd


def _window_sum(win, offsets):
    n = win.shape[0]
    acc = None
    for j in offsets:
        term = win if j == 0 else pltpu.roll(win, (-j) % n, 0)
        acc = term if acc is None else acc + term
    return acc


def _pool_counts(r0, ch, c, left, right, t):
    pos = r0 + lax.broadcasted_iota(jnp.int32, (ch, c), 0)
    return (jnp.minimum(pos + right + 1, t) - jnp.maximum(pos - left, 0)).astype(F32)


def _pool_fwd(proj, pool_w_bf, pool_scale):
    t = proj.shape[0]
    ng, c, _ = pool_w_bf.shape
    ch = _pick(t, 256, SUBLANES)
    pad = POOL_PAD

    def body(p_ref, w_ref, s_ref, pooled_ref, pm_ref, buf):
        grp = pl.program_id(0)
        buf[pl.ds(0, pad), :] = jnp.zeros((pad, c), F32)
        buf[pl.ds(pad + t, pad), :] = jnp.zeros((pad, c), F32)

        def fill(ci, carry):
            r0 = pl.multiple_of(ci * ch, SUBLANES)
            buf[pl.ds(pl.multiple_of(r0 + pad, SUBLANES), ch), :] = p_ref[pl.ds(r0, ch), :]
            return carry

        lax.fori_loop(0, t // ch, fill, 0)
        for gi, w in enumerate(POOL_WINDOWS):
            left = w // 2
            right = w - 1 - left

            @pl.when(grp == gi)
            def _(left=left, right=right):
                def chunk(ci, carry):
                    r0 = pl.multiple_of(ci * ch, SUBLANES)
                    win = buf[pl.ds(r0, ch + 2 * pad), :]
                    s = _window_sum(win, range(-left, right + 1))[pad:pad + ch]
                    pooled = s / _pool_counts(r0, ch, c, left, right, t) - win[pad:pad + ch]
                    pooled_bf = pooled.astype(BF16)
                    mixed = jnp.dot(pooled_bf, w_ref[0], preferred_element_type=F32)
                    pooled_ref[pl.ds(r0, ch), :] = pooled_bf
                    pm_ref[pl.ds(r0, ch), :] = (mixed * s_ref[...]).astype(BF16)
                    return carry

                lax.fori_loop(0, t // ch, chunk, 0)

    col = pl.BlockSpec((t, c), lambda g: (0, g))
    return pl.pallas_call(
        body, name="pool_fwd", grid=(ng,),
        in_specs=[col, pl.BlockSpec((1, c, c), lambda g: (g, 0, 0)), pl.BlockSpec((1, c), lambda g: (0, g))],
        out_specs=[col, col], out_shape=[_out(t, ng * c, BF16), _out(t, ng * c, BF16)],
        scratch_shapes=[pltpu.VMEM((t + 2 * pad, c), F32)],
        compiler_params=_params(("parallel",)),
    )(proj, pool_w_bf, pool_scale)


def _pool_bwd(pooled, dpm, pool_w_bf, pool_scale):
    t = pooled.shape[0]
    ng, c, _ = pool_w_bf.shape
    ch = _pick(t, 256, SUBLANES)
    pad = POOL_PAD

    def body(pooled_ref, dpm_ref, w_ref, s_ref, dp_ref, dw_ref, ds_ref, buf, raw):
        grp = pl.program_id(0)
        buf[pl.ds(0, pad), :] = jnp.zeros((pad, c), F32)
        buf[pl.ds(pad + t, pad), :] = jnp.zeros((pad, c), F32)
        dw_ref[...] = jnp.zeros_like(dw_ref)
        ds_ref[...] = jnp.zeros_like(ds_ref)
        for gi, w in enumerate(POOL_WINDOWS):
            left = w // 2
            right = w - 1 - left

            @pl.when(grp == gi)
            def _(left=left, right=right):
                def first(ci, carry):
                    r0 = pl.multiple_of(ci * ch, SUBLANES)
                    pv = pooled_ref[pl.ds(r0, ch), :]
                    dpm_v = dpm_ref[pl.ds(r0, ch), :]
                    mixed = jnp.dot(pv, w_ref[0], preferred_element_type=F32)
                    ds_ref[...] += jnp.sum(dpm_v * mixed, axis=0, keepdims=True)
                    dmixed = (dpm_v * s_ref[...]).astype(BF16)
                    dw_ref[0] += lax.dot_general(pv, dmixed, _DIMS["tn"], preferred_element_type=F32)
                    dpooled = lax.dot_general(dmixed, w_ref[0], _DIMS["nt"], preferred_element_type=F32)
                    raw[pl.ds(r0, ch), :] = dpooled
                    buf[pl.ds(pl.multiple_of(r0 + pad, SUBLANES), ch), :] = (
                        dpooled / _pool_counts(r0, ch, c, left, right, t))
                    return carry

                lax.fori_loop(0, t // ch, first, 0)

                def second(ci, carry):
                    r0 = pl.multiple_of(ci * ch, SUBLANES)
                    win = buf[pl.ds(r0, ch + 2 * pad), :]
                    s = _window_sum(win, range(-right, left + 1))[pad:pad + ch]
                    dp_ref[pl.ds(r0, ch), :] = (s - raw[pl.ds(r0, ch), :]).astype(BF16)
                    return carry

                lax.fori_loop(0, t // ch, second, 0)

    col = pl.BlockSpec((t, c), lambda g: (0, g))
    return pl.pallas_call(
        body, name="pool_bwd", grid=(ng,),
        in_specs=[col, col, pl.BlockSpec((1, c, c), lambda g: (g, 0, 0)), pl.BlockSpec((1, c), lambda g: (0, g))],
        out_specs=[col, pl.BlockSpec((1, c, c), lambda g: (g, 0, 0)), pl.BlockSpec((1, c), lambda g: (0, g))],
        out_shape=[_out(t, ng * c, BF16), jax.ShapeDtypeStruct((ng, c, c), F32), _out(1, ng * c, F32)],
        scratch_shapes=[pltpu.VMEM((t + 2 * pad, c), F32), pltpu.VMEM((t, c), F32)],
        compiler_params=_params(("parallel",)),
    )(pooled, dpm, pool_w_bf, pool_scale)


def _discretise(a_re, a_im, log_dt, b_re, b_im):
    dt = jnp.exp(log_dt)
    mag = jnp.exp(dt * a_re)
    ang = dt * a_im
    abr = mag * jnp.cos(ang)
    abi = mag * jnp.sin(ang)
    den = a_re * a_re + a_im * a_im
    nr = abr - 1.0
    qr = (nr * a_re + abi * a_im) / den
    qi = (abi * a_re - nr * a_im) / den
    return abr, abi, qr * b_re - qi * b_im, qr * b_im + qi * b_re


def _ssm_disc(cols):
    n, hh = cols[3].shape

    def body(ar, ai, ld, br, bi, o1, o2, o3, o4):
        res = _discretise(ar[...], ai[...], ld[...], br[...], bi[...])
        for o, r in zip((o1, o2, o3, o4), res):
            o[...] = r

    return pl.pallas_call(
        body, name="ssm_disc",
        out_shape=[_out(n, 1, F32), _out(n, 1, F32), _out(n, hh, F32), _out(n, hh, F32)],
    )(*cols)


def _ssm_disc_bwd(cols, cots):
    n, hh = cols[3].shape

    def body(ar, ai, ld, br, bi, c1, c2, c3, c4, o1, o2, o3, o4, o5):
        _, vjp = jax.vjp(_discretise, ar[...], ai[...], ld[...], br[...], bi[...])
        res = vjp((c1[...], c2[...], c3[...], c4[...]))
        for o, r in zip((o1, o2, o3, o4, o5), res):
            o[...] = r

    return pl.pallas_call(
        body, name="ssm_disc_bwd",
        out_shape=[_out(n, 1, F32)] * 3 + [_out(n, hh, F32)] * 2,
    )(*cols, *cots)


def _rowsum(name, a):
    r, _ = a.shape

    def body(a_ref, o_ref):
        o_ref[...] = jnp.sum(a_ref[...], axis=-1, keepdims=True)

    return pl.pallas_call(body, name=name, out_shape=_out(r, 1, F32))(a)


def _cmul(pr, pi, qr, qi):
    return pr * qr - pi * qi, pr * qi + pi * qr


def _cpow(pr, pi, n):
    rr, ri = None, None
    while n:
        if n & 1:
            rr, ri = (pr, pi) if rr is None else _cmul(rr, ri, pr, pi)
        n >>= 1
        if n:
            pr, pi = _cmul(pr, pi, pr, pi)
    return rr, ri


def _segment_carry(er, ei, pr, pi, reverse):
    row = lax.broadcasted_iota(jnp.int32, er.shape, 0)
    cr, ci = jnp.zeros_like(er), jnp.zeros_like(ei)
    for _ in range(SUBLANES - 1):
        tr = er + pr * cr - pi * ci
        ti = ei + pr * ci + pi * cr
        if reverse:
            keep, shift = row < SUBLANES - 1, SUBLANES - 1
        else:
            keep, shift = row >= 1, 1
        cr = jnp.where(keep, pltpu.roll(tr, shift, 0), 0.0)
        ci = jnp.where(keep, pltpu.roll(ti, shift, 0), 0.0)
    return cr, ci


def _ssm_fwd(name, sp, b_re, b_im, c_re, c_im, ar, ai, reverse):
    t, c = sp.shape
    s = ar.shape[1]
    w = _pick(s, 512)
    ch = _pick(t, 256, SUBLANES)
    n_ch, gpc, steps = t // ch, ch // SUBLANES, t // SUBLANES

    def body(sp_ref, bre_ref, bim_ref, cre_ref, cim_ref, ar_ref, ai_ref, xr_ref, xi_ref, y_ref, ur, ui, xbr, xbi):
        a_r = jnp.broadcast_to(ar_ref[...], (SUBLANES, w))
        a_i = jnp.broadcast_to(ai_ref[...], (SUBLANES, w))

        @pl.when(pl.program_id(0) == 0)
        def _():
            y_ref[...] = jnp.zeros_like(y_ref)

        def sweep(h0, store):
            def chunk(k, h):
                ci = n_ch - 1 - k if reverse else k
                rows = pl.ds(pl.multiple_of(ci * ch, ch), ch)
                spv = sp_ref[rows, :].astype(BF16)
                ur[...] = jnp.dot(spv, bre_ref[...], preferred_element_type=F32)
                ui[...] = jnp.dot(spv, bim_ref[...], preferred_element_type=F32)

                def group(g, hh):
                    gi = gpc - 1 - g if reverse else g
                    r0 = pl.multiple_of(gi * SUBLANES, SUBLANES)
                    hr, hi = hh
                    nr = a_r * hr - a_i * hi + ur[pl.ds(r0, SUBLANES), :]
                    ni = a_r * hi + a_i * hr + ui[pl.ds(r0, SUBLANES), :]
                    if store:
                        xbr[pl.ds(r0, SUBLANES), :] = nr
                        xbi[pl.ds(r0, SUBLANES), :] = ni
                    return nr, ni

                h = lax.fori_loop(0, gpc, group, h)
                if store:
                    xr16, xi16 = xbr[...].astype(BF16), xbi[...].astype(BF16)
                    xr_ref[rows, :] = xr16
                    xi_ref[rows, :] = xi16
                    y_ref[rows, :] += (jnp.dot(xr16, cre_ref[...], preferred_element_type=F32)
                                       + jnp.dot(xi16, cim_ref[...], preferred_element_type=F32))
                return h

            return lax.fori_loop(0, n_ch, chunk, h0)

        zero = jnp.zeros((SUBLANES, w), F32)
        er, ei = sweep((zero, zero), False)
        pr, pi = _cpow(ar_ref[...], ai_ref[...], steps)
        sweep(_segment_carry(er, ei, pr, pi, reverse), True)

    col = lambda i: (0, i)
    return pl.pallas_call(
        body, name=name, grid=(s // w,),
        in_specs=[pl.BlockSpec((t, c), lambda i: (0, 0)), pl.BlockSpec((c, w), col), pl.BlockSpec((c, w), col),
                  pl.BlockSpec((w, c), lambda i: (i, 0)), pl.BlockSpec((w, c), lambda i: (i, 0)),
                  pl.BlockSpec((1, w), col), pl.BlockSpec((1, w), col)],
        out_specs=[pl.BlockSpec((t, w), col), pl.BlockSpec((t, w), col), pl.BlockSpec((t, c), lambda i: (0, 0))],
        out_shape=[_out(t, s, BF16), _out(t, s, BF16), _out(t, c, F32)],
        scratch_shapes=[pltpu.VMEM((ch, w), F32)] * 4,
        compiler_params=_params(("arbitrary",)),
    )(sp, b_re, b_im, c_re, c_im, ar, ai)


def _ssm_bwd(name, dyp, c_re, c_im, xr, xi, ar, ai, reverse):
    t, c = dyp.shape
    s = ar.shape[1]
    w = _pick(s, 512)
    ch = _pick(t, 256, SUBLANES)
    n_ch, gpc, steps = t // ch, ch // SUBLANES, t // SUBLANES
    back = not reverse
    edge = 2 * SUBLANES

    def body(dy_ref, cre_ref, cim_ref, xr_ref, xi_ref, ar_ref, ai_ref, lr_ref, li_ref, dar_ref, dai_ref,
             gr, gi_, lbr, lbi, xbr, xbi):
        a_r = jnp.broadcast_to(ar_ref[...], (SUBLANES, w))
        a_i = -jnp.broadcast_to(ai_ref[...], (SUBLANES, w))
        row = lax.broadcasted_iota(jnp.int32, (SUBLANES, w), 0)

        def neighbours(ci, x_ref, buf):
            rows = pl.ds(pl.multiple_of(ci * ch, ch), ch)
            if reverse:
                buf[pl.ds(0, ch), :] = x_ref[rows, :].astype(F32)
                nxt = x_ref[pl.ds(pl.multiple_of(jnp.minimum(ci + 1, n_ch - 1) * ch, ch), edge), :].astype(F32)[:SUBLANES]
                first = x_ref[pl.ds(0, edge), :].astype(F32)[:SUBLANES]
                wrap = jnp.where(row < SUBLANES - 1, pltpu.roll(first, SUBLANES - 1, 0), 0.0)
                buf[pl.ds(ch, SUBLANES), :] = jnp.where(ci == n_ch - 1, wrap, nxt)
            else:
                buf[pl.ds(SUBLANES, ch), :] = x_ref[rows, :].astype(F32)
                prv = x_ref[pl.ds(pl.multiple_of(jnp.maximum(ci * ch - edge, 0), edge), edge), :].astype(F32)[SUBLANES:]
                last = x_ref[pl.ds(t - edge, edge), :].astype(F32)[SUBLANES:]
                wrap = jnp.where(row >= 1, pltpu.roll(last, 1, 0), 0.0)
                buf[pl.ds(0, SUBLANES), :] = jnp.where(ci == 0, wrap, prv)

        def sweep(h0, store):
            def chunk(k, carry):
                ci = n_ch - 1 - k if back else k
                rows = pl.ds(pl.multiple_of(ci * ch, ch), ch)
                dyv = dy_ref[rows, :].astype(BF16)
                gr[...] = lax.dot_general(dyv, cre_ref[...], _DIMS["nt"], preferred_element_type=F32)
                gi_[...] = lax.dot_general(dyv, cim_ref[...], _DIMS["nt"], preferred_element_type=F32)
                if store:
                    neighbours(ci, xr_ref, xbr)
                    neighbours(ci, xi_ref, xbi)

                def group(g, cc):
                    gidx = gpc - 1 - g if back else g
                    r0 = pl.multiple_of(gidx * SUBLANES, SUBLANES)
                    hr, hi = cc[0], cc[1]
                    nr = a_r * hr - a_i * hi + gr[pl.ds(r0, SUBLANES), :]
                    ni = a_r * hi + a_i * hr + gi_[pl.ds(r0, SUBLANES), :]
                    if not store:
                        return nr, ni
                    lbr[pl.ds(r0, SUBLANES), :] = nr
                    lbi[pl.ds(r0, SUBLANES), :] = ni
                    x0 = pl.multiple_of(r0 + SUBLANES, SUBLANES) if reverse else r0
                    xpr, xpi = xbr[pl.ds(x0, SUBLANES), :], xbi[pl.ds(x0, SUBLANES), :]
                    return nr, ni, cc[2] + nr * xpr + ni * xpi, cc[3] + ni * xpr - nr * xpi

                carry = lax.fori_loop(0, gpc, group, carry)
                if store:
                    lr_ref[rows, :] = lbr[...].astype(BF16)
                    li_ref[rows, :] = lbi[...].astype(BF16)
                return carry

            return lax.fori_loop(0, n_ch, chunk, h0)

        zero = jnp.zeros((SUBLANES, w), F32)
        er, ei = sweep((zero, zero), False)
        pr, pi = _cpow(ar_ref[...], -ai_ref[...], steps)
        cr, ci0 = _segment_carry(er, ei, pr, pi, back)
        _, _, dar, dai = sweep((cr, ci0, zero, zero), True)
        dar_ref[...] = jnp.sum(dar, axis=0, keepdims=True)
        dai_ref[...] = jnp.sum(dai, axis=0, keepdims=True)

    col = lambda i: (0, i)
    return pl.pallas_call(
        body, name=name, grid=(s // w,),
        in_specs=[pl.BlockSpec((t, c), lambda i: (0, 0)), pl.BlockSpec((w, c), lambda i: (i, 0)),
                  pl.BlockSpec((w, c), lambda i: (i, 0)), pl.BlockSpec((t, w), col), pl.BlockSpec((t, w), col),
                  pl.BlockSpec((1, w), col), pl.BlockSpec((1, w), col)],
        out_specs=[pl.BlockSpec((t, w), col), pl.BlockSpec((t, w), col), pl.BlockSpec((1, w), col), pl.BlockSpec((1, w), col)],
        out_shape=[_out(t, s, BF16), _out(t, s, BF16), _out(1, s, F32), _out(1, s, F32)],
        scratch_shapes=[pltpu.VMEM((ch, w), F32)] * 4 + [pltpu.VMEM((ch + SUBLANES, w), F32)] * 2,
        compiler_params=_params(("parallel",)),
    )(dyp, c_re, c_im, xr, xi, ar, ai)


def _to_segments(a):
    t, c = a.shape
    return a.reshape(SUBLANES, t // SUBLANES, c).transpose(1, 0, 2).reshape(t, c)


def _from_segments(a):
    t, c = a.shape
    return a.reshape(t // SUBLANES, SUBLANES, c).transpose(1, 0, 2).reshape(t, c)


def _colsum_prod(name, a, b, b_coff=0):
    t, n = a.shape
    tm = _pick(t, 512, SUBLANES)

    def body(a_ref, b_ref, o_ref):
        @pl.when(pl.program_id(0) == 0)
        def _():
            o_ref[...] = jnp.zeros_like(o_ref)

        o_ref[...] += jnp.sum(a_ref[...].astype(F32) * b_ref[...].astype(F32), axis=0, keepdims=True)

    return pl.pallas_call(
        body, name=name, grid=(t // tm,),
        in_specs=[pl.BlockSpec((tm, n), lambda i: (i, 0)), pl.BlockSpec((tm, n), lambda i: (i, b_coff))],
        out_specs=pl.BlockSpec((1, n), lambda i: (0, 0)), out_shape=_out(1, n, F32),
        compiler_params=_params(("arbitrary",)),
    )(a, b)


def _bd_in(bb, g, p, hh):
    blk = bb.reshape(g, p, hh).transpose(0, 2, 1)
    eye = jnp.eye(g, dtype=bool)[:, None, :, None]
    return jnp.where(eye, blk[:, :, None, :], 0.0).reshape(g * hh, g * p)


def _bd_out(cc, g, p, hh):
    blk = cc.transpose(0, 2, 1)
    eye = jnp.eye(g, dtype=bool)[:, None, :, None]
    return jnp.where(eye, blk[:, :, None, :], 0.0).reshape(g * p, g * hh)


def _diag_in(dmat, g, p, hh):
    eye = jnp.eye(g, dtype=bool)[:, None, :, None]
    diag = jnp.sum(jnp.where(eye, dmat.reshape(g, hh, g, p), 0.0), axis=2)
    return diag.transpose(0, 2, 1).reshape(g * p, hh)


def _diag_out(dmat, g, p, hh):
    eye = jnp.eye(g, dtype=bool)[:, None, :, None]
    diag = jnp.sum(jnp.where(eye, dmat.reshape(g, p, g, hh), 0.0), axis=2)
    return diag.transpose(0, 2, 1)


def _softmax(qh, kh, scale):
    s = lax.dot_general(qh, kh, _DIMS["nt"], preferred_element_type=F32) * scale
    e = jnp.exp(s - jnp.max(s, axis=-1, keepdims=True))
    return e / jnp.sum(e, axis=-1, keepdims=True)


def _attn_fwd(q, kv):
    t, d = q.shape
    mm_ = kv.shape[0]
    hd = d // N_XHEADS
    scale = 1.0 / math.sqrt(hd)
    tm = _pick(t, 512, SUBLANES)

    def body(q_ref, kv_ref, o_ref):
        for h in range(N_XHEADS):
            sl = pl.ds(h * hd, hd)
            p = _softmax(q_ref[:, sl], kv_ref[:, sl], scale)
            o_ref[:, sl] = jnp.dot(p.astype(BF16), kv_ref[:, pl.ds(d + h * hd, hd)],
                                   preferred_element_type=F32).astype(BF16)

    return pl.pallas_call(
        body, name="attn_fwd", grid=(t // tm,),
        in_specs=[pl.BlockSpec((tm, d), lambda i: (i, 0)), pl.BlockSpec((mm_, 2 * d), lambda i: (0, 0))],
        out_specs=pl.BlockSpec((tm, d), lambda i: (i, 0)), out_shape=_out(t, d, BF16),
        compiler_params=_params(("parallel",)),
    )(q, kv)


def _attn_bwd(q, kv, do):
    t, d = q.shape
    mm_ = kv.shape[0]
    hd = d // N_XHEADS
    scale = 1.0 / math.sqrt(hd)
    tm = _pick(t, 512, SUBLANES)

    def body(q_ref, kv_ref, do_ref, dq_ref, dkv_ref):
        @pl.when(pl.program_id(0) == 0)
        def _():
            dkv_ref[...] = jnp.zeros_like(dkv_ref)

        for h in range(N_XHEADS):
            sl = pl.ds(h * hd, hd)
            vsl = pl.ds(d + h * hd, hd)
            qh, kh, doh = q_ref[:, sl], kv_ref[:, sl], do_ref[:, sl]
            p = _softmax(qh, kh, scale)
            dp = lax.dot_general(doh, kv_ref[:, vsl], _DIMS["nt"], preferred_element_type=F32)
            dkv_ref[:, vsl] += lax.dot_general(p.astype(BF16), doh, _DIMS["tn"], preferred_element_type=F32)
            ds = (p * (dp - jnp.sum(dp * p, axis=-1, keepdims=True)) * scale).astype(BF16)
            dq_ref[:, sl] = jnp.dot(ds, kh, preferred_element_type=F32).astype(BF16)
            dkv_ref[:, sl] += lax.dot_general(ds, qh, _DIMS["tn"], preferred_element_type=F32)

    row = pl.BlockSpec((tm, d), lambda i: (i, 0))
    full = pl.BlockSpec((mm_, 2 * d), lambda i: (0, 0))
    return pl.pallas_call(
        body, name="attn_bwd", grid=(t // tm,), in_specs=[row, full, row], out_specs=[row, full],
        out_shape=[_out(t, d, BF16), _out(mm_, 2 * d, F32)], compiler_params=_params(("arbitrary",)),
    )(q, kv, do)


def _ew(name, fn, ins, outs, rows_pref=256, rowvecs=()):
    r, c = ins[0].shape
    tr = _pick(r, rows_pref, SUBLANES)
    ni = len(ins) + len(rowvecs)

    def body(*refs):
        res = fn(*[x[...] for x in refs[:ni]])
        for o_ref, v in zip(refs[ni:], res):
            o_ref[...] = v.astype(o_ref.dtype)

    blk = pl.BlockSpec((tr, c), lambda i: (i, 0))
    vec = pl.BlockSpec((1, c), lambda i: (0, 0))
    return pl.pallas_call(
        body, name=name, grid=(r // tr,), in_specs=[blk] * len(ins) + [vec] * len(rowvecs), out_specs=[blk] * len(outs),
        out_shape=[_out(r, c, dt) for dt in outs], compiler_params=_params(("parallel",)),
    )(*ins, *rowvecs)


def _sum_slots(name, a, dtype):
    s, r, c = a.shape
    tr = _pick(r, 256, SUBLANES)

    def body(a_ref, o_ref):
        acc = a_ref[0].astype(F32)
        for k in range(1, s):
            acc = acc + a_ref[k].astype(F32)
        o_ref[...] = acc.astype(o_ref.dtype)

    return pl.pallas_call(
        body, name=name, grid=(r // tr,), in_specs=[pl.BlockSpec((s, tr, c), lambda i: (0, i, 0))],
        out_specs=pl.BlockSpec((tr, c), lambda i: (i, 0)), out_shape=_out(r, c, dtype),
        compiler_params=_params(("parallel",)),
    )(a)


def _adamw(name, w, g, m, v):
    bc1 = 1.0 - ADAM_B1 ** ADAM_STEP
    bc2 = 1.0 - ADAM_B2 ** ADAM_STEP

    def fn(wv, gv, mv, vv):
        m2 = ADAM_B1 * mv + (1.0 - ADAM_B1) * gv
        v2 = ADAM_B2 * vv + (1.0 - ADAM_B2) * (gv * gv)
        delta = -ADAM_LR * ((m2 / bc1) / (jnp.sqrt(v2 / bc2) + ADAM_EPS) + ADAM_WD * wv)
        return delta, m2, v2

    return _ew(name, fn, [w, g, m, v], [F32, F32, F32])


def _allgather(name, arrs):
    n = len(arrs)

    def body(*refs):
        ins, outs = refs[:n], refs[n:2 * n]
        send_sems, recv_sems, local_sems = refs[2 * n:]
        x, y, c = lax.axis_index("x"), lax.axis_index("y"), lax.axis_index("c")
        me, sibling = (x, y, c), (x, y, 1 - c)
        chips = [(1 - x, y), (x, 1 - y), (1 - x, 1 - y)]

        def rows(a, px, py, pc):
            r = ins[a].shape[0]
            return outs[a].at[pl.ds((4 * px + 2 * py + pc) * r, r), :]

        def copy(a, k, block, to, src=None):
            return pltpu.make_async_remote_copy(
                src_ref=rows(a, *block) if src is None else src, dst_ref=rows(a, *block),
                send_sem=send_sems.at[a, k], recv_sem=recv_sems.at[a, k], device_id=to, device_id_type=MESH)

        mine = [pltpu.make_async_copy(ins[a], rows(a, *me), local_sems.at[a]) for a in range(n)]
        for cp in mine:
            cp.start()
        first = []
        for a in range(n):
            first.append(copy(a, 0, me, sibling, src=ins[a]))
            first += [copy(a, 1 + j, me, (*chip, c), src=ins[a]) for j, chip in enumerate(chips)]
        for cp in first:
            cp.start()
        passed = []
        for j, chip in enumerate(chips):
            for a in range(n):
                copy(a, 1 + j, (*chip, c), me).wait_recv()
                cp = copy(a, 4 + j, (*chip, c), sibling)
                cp.start()
                passed.append(cp)
        for a in range(n):
            copy(a, 0, sibling, me).wait_recv()
            for j, chip in enumerate(chips):
                copy(a, 4 + j, (*chip, 1 - c), me).wait_recv()
        for cp in first + passed:
            cp.wait_send()
        for cp in mine:
            cp.wait()

    return pl.pallas_call(
        body, name=name, in_specs=[ANY] * n, out_specs=[ANY] * n,
        out_shape=[_out(N_DEV * a.shape[0], a.shape[1], a.dtype) for a in arrs],
        scratch_shapes=[pltpu.SemaphoreType.DMA((n, 7)), pltpu.SemaphoreType.DMA((n, 7)), pltpu.SemaphoreType.DMA((n,))],
    )(*arrs)


def _exchange_cores(name, g):
    _, r, c = g.shape
    nck = r // GRAD_ROW_TILE

    def body(g_ref, recv_ref, send_sems, recv_sems):
        x, y, cc = lax.axis_index("x"), lax.axis_index("y"), lax.axis_index("c")
        copies = []
        for q in range(4):
            for k in range(nck):
                rows = pl.ds(k * GRAD_ROW_TILE, GRAD_ROW_TILE)
                copies.append(pltpu.make_async_remote_copy(
                    src_ref=g_ref.at[2 * q + (1 - cc), rows], dst_ref=recv_ref.at[q, rows],
                    send_sem=send_sems.at[q, k], recv_sem=recv_sems.at[q, k], device_id=(x, y, 1 - cc),
                    device_id_type=MESH))
        for cp in copies:
            cp.start()
        for cp in copies:
            cp.wait()

    return pl.pallas_call(
        body, name=name, in_specs=[ANY], out_specs=ANY,
        out_shape=jax.ShapeDtypeStruct((4, r, c), g.dtype),
        scratch_shapes=[pltpu.SemaphoreType.DMA((4, nck)), pltpu.SemaphoreType.DMA((4, nck))],
    )(g)


def _pair_sum(name, g, recv, core):
    _, r, c = g.shape
    tr = GRAD_ROW_TILE

    def body(core_ref, g_ref, r_ref, o_ref):
        o_ref[...] = (g_ref[...].astype(F32) + r_ref[...].astype(F32)).astype(o_ref.dtype)

    blk = pl.BlockSpec((None, tr, c), lambda q, i, core_ref: (q, i, 0))
    return pl.pallas_call(
        body, name=name,
        grid_spec=pltpu.PrefetchScalarGridSpec(
            num_scalar_prefetch=1, grid=(4, r // tr),
            in_specs=[pl.BlockSpec((None, tr, c), lambda q, i, core_ref: (2 * q + core_ref[0], i, 0)), blk],
            out_specs=blk),
        out_shape=jax.ShapeDtypeStruct((4, r, c), g.dtype), compiler_params=_params(("parallel", "parallel")),
    )(core, g, recv)


def _peer(k, x, y, c):
    return (1 - x if k & 4 else x, 1 - y if k & 2 else y, 1 - c if k & 1 else c)


def _split_start(name, srcs, land_shapes, n_remote, n_local, build, after=None):
    ns, nl = len(srcs), len(land_shapes)
    n_sem = 3 if n_local else 2
    pins = [] if after is None else [after]

    def body(*refs):
        src_refs, land_refs = refs[:ns], refs[ns:ns + nl]
        sems = refs[ns + nl + len(pins):ns + nl + len(pins) + n_sem]
        token = refs[-1]
        remote, local = build(src_refs, land_refs, *sems)
        for cp in local + remote:
            cp.start()
        token[...] = jnp.zeros_like(token)

    sem_shapes = [pltpu.SemaphoreType.DMA((n_remote,)), pltpu.SemaphoreType.DMA((n_remote,))]
    if n_local:
        sem_shapes.append(pltpu.SemaphoreType.DMA((n_local,)))
    bufs = [pltpu.with_memory_space_constraint(a, pltpu.HBM) for a in srcs]
    bufs += [pltpu.with_memory_space_constraint(lax.empty(s.shape, s.dtype), pltpu.HBM) for s in land_shapes]
    outs = pl.pallas_call(
        body, name=name,
        out_shape=sem_shapes + [pltpu.HBM(b.shape, b.dtype) for b in bufs] + [jax.ShapeDtypeStruct((SUBLANES, LANES), F32)],
        in_specs=[HBM] * (ns + nl) + [ANY] * len(pins),
        out_specs=[SEM] * n_sem + [HBM] * (ns + nl) + [pl.BlockSpec(memory_space=pltpu.VMEM)],
        input_output_aliases={i: n_sem + i for i in range(ns + nl)},
        compiler_params=pltpu.CompilerParams(has_side_effects=SIDE_EFFECT),
    )(*bufs, *pins)
    return dict(sems=list(outs[:n_sem]), bufs=list(outs[n_sem:n_sem + ns + nl]), token=outs[-1], build=build, ns=ns)


def _split_wait(name, started, after):
    ns, n_buf, n_sem = started["ns"], len(started["bufs"]), len(started["sems"])

    def body(*refs):
        src_refs, land_refs = refs[:ns], refs[ns:n_buf]
        sems = refs[n_buf:n_buf + n_sem]
        remote, local = started["build"](src_refs, land_refs, *sems)
        for cp in local:
            cp.wait()
        for cp in remote:
            cp.wait_send()
            cp.wait_recv()

    outs = pl.pallas_call(
        body, name=name, out_shape=[pltpu.HBM(b.shape, b.dtype) for b in started["bufs"]],
        in_specs=[HBM] * n_buf + [SEM] * n_sem + [ANY], out_specs=[HBM] * n_buf,
        input_output_aliases={i: i for i in range(n_buf)},
        compiler_params=pltpu.CompilerParams(has_side_effects=SIDE_EFFECT),
    )(*started["bufs"], *started["sems"], after)
    return list(outs[ns:])


def _gather_start(name, shards, after):
    m = len(shards)

    def build(src_refs, land_refs, send_sems, recv_sems, local_sems):
        x, y, c = lax.axis_index("x"), lax.axis_index("y"), lax.axis_index("c")
        remote, local = [], []
        for j in range(m):
            r = src_refs[j].shape[0]
            dst = land_refs[j].at[pl.ds((4 * x + 2 * y + c) * r, r), :]
            local.append(pltpu.make_async_copy(src_refs[j], dst, local_sems.at[j]))
            for k in range(1, N_DEV):
                remote.append(pltpu.make_async_remote_copy(
                    src_ref=src_refs[j], dst_ref=dst, send_sem=send_sems.at[7 * j + k - 1],
                    recv_sem=recv_sems.at[7 * j + k - 1], device_id=_peer(k, x, y, c), device_id_type=MESH))
        return remote, local

    lands = [jax.ShapeDtypeStruct((N_DEV * a.shape[0], a.shape[1]), a.dtype) for a in shards]
    return _split_start(name, shards, lands, 7 * m, m, build, after)


def _slots_start(name, a):
    def build(src_refs, land_refs, send_sems, recv_sems, local_sems):
        x, y, c = lax.axis_index("x"), lax.axis_index("y"), lax.axis_index("c")
        dst = land_refs[0].at[4 * x + 2 * y + c]
        local = [pltpu.make_async_copy(src_refs[0], dst, local_sems.at[0])]
        remote = [pltpu.make_async_remote_copy(
            src_ref=src_refs[0], dst_ref=dst, send_sem=send_sems.at[k - 1], recv_sem=recv_sems.at[k - 1],
            device_id=_peer(k, x, y, c), device_id_type=MESH) for k in range(1, N_DEV)]
        return remote, local

    return _split_start(name, [a], [jax.ShapeDtypeStruct((N_DEV,) + a.shape, a.dtype)], 7, 1, build)


def _chips_start(name, p):
    _, r, c = p.shape
    nck = r // GRAD_ROW_TILE

    def build(src_refs, land_refs, send_sems, recv_sems):
        x, y, cc = lax.axis_index("x"), lax.axis_index("y"), lax.axis_index("c")
        remote = []
        for k in range(1, 4):
            px = 1 - x if k >> 1 else x
            py = 1 - y if k & 1 else y
            for j in range(nck):
                rows = pl.ds(j * GRAD_ROW_TILE, GRAD_ROW_TILE)
                remote.append(pltpu.make_async_remote_copy(
                    src_ref=src_refs[0].at[2 * px + py, rows], dst_ref=land_refs[0].at[k - 1, rows],
                    send_sem=send_sems.at[(k - 1) * nck + j], recv_sem=recv_sems.at[(k - 1) * nck + j],
                    device_id=(px, py, cc), device_id_type=MESH))
        return remote, []

    return _split_start(name, [p], [jax.ShapeDtypeStruct((3, r, c), p.dtype)], 3 * nck, 0, build)


def _chip_sum(name, p, recv, chip):
    _, r, c = p.shape
    tr = GRAD_ROW_TILE

    def body(chip_ref, p_ref, r_ref, o_ref):
        acc = p_ref[...].astype(F32)
        for k in range(3):
            acc = acc + r_ref[k].astype(F32)
        o_ref[...] = acc

    return pl.pallas_call(
        body, name=name,
        grid_spec=pltpu.PrefetchScalarGridSpec(
            num_scalar_prefetch=1, grid=(r // tr,),
            in_specs=[pl.BlockSpec((None, tr, c), lambda i, chip_ref: (chip_ref[0], i, 0)),
                      pl.BlockSpec((3, tr, c), lambda i, chip_ref: (0, i, 0))],
            out_specs=pl.BlockSpec((tr, c), lambda i, chip_ref: (i, 0))),
        out_shape=_out(r, c, F32), compiler_params=_params(("parallel",)),
    )(chip, p, recv)


def _local_step(x, mem, tgt, wt, sm, ev=None):
    t, d = x.shape
    n_mem = mem.shape[0]
    d_pool = sm["pool_scale"].shape[1]
    ng, pc = sm["pool_w"].shape[0], sm["pool_w"].shape[1]
    d_ssm = sm["ssm_d"].shape[1]
    _, sg, sp, sh = sm["ssm_b_re"].shape
    n_state = sg * sp
    gb, gs = {}, {}

    def emit(name, **kw):
        return ev(name, **kw) if ev is not None else None

    n1 = _rms_fwd("ffn1_norm", x, sm["ffn1_norm"])
    emit("ffn1_norm_done", marker=n1)
    h1, ffn1_saved = _ffn_fwd("ffn1", x, n1, wt["ffn1_w_gate"], wt["ffn1_w_up"], wt["ffn1_w_down"])
    emit("ffn1_fwd_done", marker=h1)
    u = _rms_fwd("mix_norm", h1, sm["mix_norm"])
    d_in = wt["w_in"].shape[0]
    tm, tn = _pick(t, 1024), _pick(d_in, 1408)
    proj = _mm1("in_proj", "nt", u, wt["w_in"], t, d_in, tm, tn, F32)
    off_s = d_pool // d_ssm
    off_gp = (d_pool + d_ssm)
    off_gs = off_gp + d

    pool_w_bf = sm["pool_w"].astype(BF16)
    pooled, pm = _pool_fwd(proj, pool_w_bf, sm["pool_scale"])

    cols = [sm["ssm_a_re"].reshape(-1, 1), sm["ssm_a_im"].reshape(-1, 1),
            jnp.broadcast_to(sm["ssm_log_dt"][:, :, None], (2, sg, sp)).reshape(-1, 1),
            sm["ssm_b_re"].reshape(-1, sh), sm["ssm_b_im"].reshape(-1, sh)]
    abr, abi, bbr, bbi = _ssm_disc(cols)
    abr2, abi2 = abr.reshape(2, n_state), abi.reshape(2, n_state)
    bbr4, bbi4 = bbr.reshape(2, sg * sp, sh), bbi.reshape(2, sg * sp, sh)
    b_re = [_bd_in(bbr4[dr], sg, sp, sh).astype(BF16) for dr in range(2)]
    b_im = [_bd_in(bbi4[dr], sg, sp, sh).astype(BF16) for dr in range(2)]
    c_re = [_bd_out(sm["ssm_c_re"][dr], sg, sp, sh).astype(BF16) for dr in range(2)]
    c_im = [_bd_out(-sm["ssm_c_im"][dr], sg, sp, sh).astype(BF16) for dr in range(2)]
    sp32 = _to_segments(proj[:, d_pool:d_pool + d_ssm])
    xs, y_parts = [], []
    for dr in range(2):
        xr, xi, y_part = _ssm_fwd(f"ssm_fwd{dr}", sp32, b_re[dr], b_im[dr], c_re[dr], c_im[dr], abr2[dr:dr + 1],
                                  abi2[dr:dr + 1], reverse=(dr == 1))
        xs.append((xr, xi))
        y_parts.append(y_part)
    y = _from_segments(_ew("ssm_sum", lambda p0, p1, sv, dv: (p0 + p1 + sv * dv,), y_parts + [sp32], [F32],
                           rowvecs=[sm["ssm_d"]])[0])
    tmy = _pick(t, 256)
    ys = _ew("ssm_gelu", lambda v: (jax.nn.gelu(v),), [y], [BF16])[0]
    emit("mix_in_done", marker=ys)

    tmm, tnm, tnx = _pick(t, 1024), _pick(d, 256), _pick(d, 512)
    gp_spec = _tile(tmm, tnm, off_gp // tnm)
    gs_spec = _tile(tmm, tnm, off_gs // tnm)

    def merge_epi(accs, gpv, gsv):
        z_pool, val, gate = accs
        return (jax.nn.sigmoid(gpv) * z_pool + jax.nn.sigmoid(gsv) * (val * jax.nn.sigmoid(gate)),)

    merged = _mm("mix_merge", "nt", [pm, ys], [wt["w_pool_proj"], wt["w_glu_val"], wt["w_glu_gate"]],
                 [[(0, 0)], [(1, 1)], [(1, 2)]], t, d, tmm, tnm, [(proj, gp_spec), (proj, gs_spec)], merge_epi,
                 [(_out(t, d, BF16), None)])[0]
    res_epi = lambda accs, hin: (hin + accs[0],)
    h2 = _mm("mix_out", "nn", [merged], [wt["w_mix_out"]], [[(0, 0)]], t, d, tmm, tnx, [(h1, _tile(tmm, tnx))],
             res_epi, [(_out(t, d, F32), None)])[0]

    un = _rms_fwd("xattn_norm", h2, sm["xattn_norm"])
    mn = _rms_fwd("mem_norm", mem, sm["mem_norm"])
    q = _mm1("xattn_q", "nn", un, wt["w_q"], t, d, tmm, tnx, BF16)
    kv = _mm1("xattn_kv", "nt", mn, wt["w_kv"], n_mem, 2 * d, n_mem, _pick(2 * d, 512), BF16)
    o = _attn_fwd(q, kv)
    h3 = _mm("xattn_out", "nn", [o], [wt["w_xo"]], [[(0, 0)]], t, d, tmm, tnx, [(h2, _tile(tmm, tnx))],
             res_epi, [(_out(t, d, F32), None)])[0]

    n2 = _rms_fwd("ffn2_norm", h3, sm["ffn2_norm"])
    h4, ffn2_saved = _ffn_fwd("ffn2", h3, n2, wt["ffn2_w_gate"], wt["ffn2_w_up"], wt["ffn2_w_down"])

    dh4, dh4_bf, gs["final_norm"], loss = _loss_head(h4, sm["final_norm"], tgt)
    dh3, dh3_bf, gs["ffn2_norm"], gb["ffn2_w_gate"], gb["ffn2_w_up"], gb["ffn2_w_down"] = _ffn_bwd(
        "ffn2", h3, sm["ffn2_norm"], wt["ffn2_w_gate"], wt["ffn2_w_up"], wt["ffn2_w_down"], ffn2_saved, dh4, dh4_bf)

    tw = _pick(d, 1024)
    do = _mm1("xattn_do", "nt", dh3_bf, wt["w_xo"], t, d, tmm, tnx, BF16)
    gb["w_xo"] = _mm1("xattn_dwxo", "tn", o, dh3_bf, d, d, tw, tnx, BF16)
    dq, dkv = _attn_bwd(q, kv, do)
    gb["w_q"] = _mm1("xattn_dwq", "tn", un, dq, d, d, tw, tnx, BF16)
    dun = _mm1("xattn_dun", "nt", dq, wt["w_q"], t, d, tmm, tnx, F32)
    dh2, dh2_bf, gs["xattn_norm"] = _rms_bwd("xattn_norm_bwd", h2, sm["xattn_norm"], dun, dh3)
    gb["w_kv"] = _mm1("xattn_dwkv", "tn", dkv, mn, 2 * d, d, _pick(2 * d, 512), d, BF16)
    dmn = _mm1("xattn_dmn", "nn", dkv, wt["w_kv"], n_mem, d, n_mem, tnx, F32)
    gs["mem_norm"] = _rms_bwd("mem_norm_bwd", mem, sm["mem_norm"], dmn)

    gb["w_mix_out"] = _mm1("mix_dwout", "tn", merged, dh2_bf, d, d, tw, tnx, BF16)

    def merge_bwd_epi(accs, gpv, gsv):
        dmerged, z_pool, val, gate = accs
        sp_, ss_, sg_ = jax.nn.sigmoid(gpv), jax.nn.sigmoid(gsv), jax.nn.sigmoid(gate)
        glu = val * sg_
        dz_pool = dmerged * sp_
        dg_pool = dmerged * z_pool * (sp_ * (1.0 - sp_))
        dz_ssm = dmerged * ss_
        dg_ssm = dmerged * glu * (ss_ * (1.0 - ss_))
        dval = dz_ssm * sg_
        dgate = dz_ssm * glu * (1.0 - sg_)
        return dz_pool, dg_pool, dg_ssm, dval, dgate

    dz_pool, dg_pool, dg_ssm, dval, dgate = _mm(
        "mix_merge_bwd", "nt", [dh2_bf, pm, ys], [wt["w_mix_out"], wt["w_pool_proj"], wt["w_glu_val"], wt["w_glu_gate"]],
        [[(0, 0)], [(1, 1)], [(2, 2)], [(2, 3)]], t, d, tmm, tnm, [(proj, gp_spec), (proj, gs_spec)], merge_bwd_epi,
        [(_out(t, d, BF16), None)] * 5)
    gb["w_pool_proj"] = _mm1("pool_dwproj", "tn", dz_pool, pm, d, d_pool, tw, d_pool, BF16)
    gb["w_glu_val"] = _mm1("glu_dwval", "tn", dval, ys, d, d_ssm, tw, d_ssm, BF16)
    gb["w_glu_gate"] = _mm1("glu_dwgate", "tn", dgate, ys, d, d_ssm, tw, d_ssm, BF16)

    def gelu_bwd_epi(accs, yv):
        _, vjp = jax.vjp(jax.nn.gelu, yv)
        return (vjp(accs[0])[0],)

    dy = _mm("glu_dy", "nn", [dval, dgate], [wt["w_glu_val"], wt["w_glu_gate"]], [[(0, 0), (1, 1)]], t, d_ssm, tmy, d_ssm,
             [(y, _tile(tmy, d_ssm))], gelu_bwd_epi, [(_out(t, d_ssm, F32), None)])[0]
    gs["ssm_d"] = _colsum_prod("ssm_dd", dy, proj, b_coff=off_s)
    dyp = _to_segments(dy)
    d_abr, d_abi, d_bbr, d_bbi, d_cre, d_cim, lams = [], [], [], [], [], [], []
    ts = _pick(n_state, 512)
    tc_ = _pick(n_state, 256)
    both = lambda accs: tuple(accs)
    for dr in range(2):
        lr, li, dar, dai = _ssm_bwd(f"ssm_bwd{dr}", dyp, c_re[dr], c_im[dr], xs[dr][0], xs[dr][1], abr2[dr:dr + 1],
                                    abi2[dr:dr + 1], reverse=(dr == 1))
        d_abr.append(dar)
        d_abi.append(dai)
        lams += [lr, li]
        d_br, d_bi = _mm(f"ssm_db{dr}", "tn", [sp32], [lr, li], [[(0, 0)], [(0, 1)]], d_ssm, n_state, d_ssm, ts, [], both,
                         [(_out(d_ssm, n_state, F32), None)] * 2)
        d_bbr.append(_diag_in(d_br, sg, sp, sh))
        d_bbi.append(_diag_in(d_bi, sg, sp, sh))
        d_cr, d_ci = _mm(f"ssm_dc{dr}", "tn", [xs[dr][0], xs[dr][1]], [dyp], [[(0, 0)], [(1, 0)]], n_state, d_ssm, tc_,
                         d_ssm, [], both, [(_out(n_state, d_ssm, F32), None)] * 2)
        d_cre.append(_diag_out(d_cr, sg, sp, sh))
        d_cim.append(-_diag_out(d_ci, sg, sp, sh))
    ds = _from_segments(_mm(
        "ssm_ds", "nt", lams, [b_re[0], b_im[0], b_re[1], b_im[1]], [[(k, k) for k in range(4)]], t, d_ssm, tmy,
        d_ssm, [(dyp, _tile(tmy, d_ssm)), (sm["ssm_d"], _rowvec(d_ssm))],
        lambda accs, dyv, dv: (dyv * dv + accs[0],), [(_out(t, d_ssm, BF16), None)])[0])
    cots = [jnp.concatenate(d_abr, axis=0).reshape(-1, 1), jnp.concatenate(d_abi, axis=0).reshape(-1, 1),
            jnp.concatenate(d_bbr, axis=0), jnp.concatenate(d_bbi, axis=0)]
    d_are, d_aim, d_ldt, d_bre, d_bim = _ssm_disc_bwd(cols, cots)
    gs["ssm_a_re"] = d_are.reshape(2, sg, sp)
    gs["ssm_a_im"] = d_aim.reshape(2, sg, sp)
    gs["ssm_log_dt"] = _rowsum("ssm_dlogdt", d_ldt.reshape(2 * sg, sp)).reshape(2, sg)
    gs["ssm_b_re"] = d_bre.reshape(2, sg, sp, sh)
    gs["ssm_b_im"] = d_bim.reshape(2, sg, sp, sh)
    gs["ssm_c_re"] = jnp.stack(d_cre, axis=0)
    gs["ssm_c_im"] = jnp.stack(d_cim, axis=0)

    dpm = _mm1("pool_dpm", "nn", dz_pool, wt["w_pool_proj"], t, d_pool, tmm, _pick(d_pool, 256), F32)
    dp, gs["pool_w"], gs["pool_scale"] = _pool_bwd(pooled, dpm, pool_w_bf, sm["pool_scale"])

    w_in = wt["w_in"]
    parts = [(dp, 0, d_pool), (ds, d_pool, d_ssm), (dg_pool, off_gp, d), (dg_ssm, off_gs, d)]
    w_in_parts = [w_in[o0:o0 + width] for _, o0, width in parts]
    gb["w_in"] = jnp.concatenate(
        [_mm1(f"in_proj_dw{k}", "tn", p_[0], u, p_[2], d, _pick(p_[2], 1024), tnx, BF16) for k, p_ in enumerate(parts)], axis=0)
    pin = emit("grads_main", gb=gb)
    du = _mm("in_proj_du", "nn", [p_[0] for p_ in parts], w_in_parts, [[(k, k) for k in range(4)]], t, d, tmm, tnx, [],
             lambda accs: (accs[0],), [(_out(t, d, F32), None)], after=pin)[0]
    dh1, dh1_bf, gs["mix_norm"] = _rms_bwd("mix_norm_bwd", h1, sm["mix_norm"], du, dh2)
    pin = emit("small_early", gs=gs, loss=loss)

    def ffn1_weights_done(d_wg, d_wu, d_wd):
        gb["ffn1_w_gate"], gb["ffn1_w_up"], gb["ffn1_w_down"] = d_wg, d_wu, d_wd
        return emit("grads_ffn1", gb=gb)

    dx, _, gs["ffn1_norm"], _, _, _ = _ffn_bwd(
        "ffn1", x, sm["ffn1_norm"], wt["ffn1_w_gate"], wt["ffn1_w_up"], wt["ffn1_w_down"], ffn1_saved, dh1, dh1_bf,
        weights_done=ffn1_weights_done, after=pin)
    return loss, dx, gb, gs


WEIGHTS = ["ffn1_norm", "ffn1_w_gate", "ffn1_w_up", "ffn1_w_down", "mix_norm", "w_in", "pool_w", "pool_scale",
           "w_pool_proj", "ssm_a_re", "ssm_a_im", "ssm_log_dt", "ssm_b_re", "ssm_b_im", "ssm_c_re", "ssm_c_im", "ssm_d",
           "w_glu_val", "w_glu_gate", "w_mix_out", "xattn_norm", "mem_norm", "w_q", "w_kv", "w_xo", "ffn2_norm",
           "ffn2_w_gate", "ffn2_w_up", "ffn2_w_down", "final_norm"]
COL_SHARDED = ["ffn1_w_gate", "ffn1_w_up", "w_in", "w_pool_proj", "w_glu_val", "w_glu_gate", "w_kv", "ffn2_w_gate",
               "ffn2_w_up"]
ROW_SHARDED = ["ffn1_w_down", "w_mix_out", "w_q", "w_xo", "ffn2_w_down"]
BIG = [n for n in WEIGHTS if n in COL_SHARDED or n in ROW_SHARDED]
SMALL = [n for n in WEIGHTS if n not in BIG]
FFN1_BIG = ["ffn1_w_gate", "ffn1_w_up", "ffn1_w_down"]
MAIN_BIG = [n for n in BIG if n not in FFN1_BIG]
LATE_SMALL = "ffn1_norm"
EARLY_SMALL = [n for n in SMALL if n != LATE_SMALL]
PACK_ROWS = SUBLANES * LANES
GRAD_ROW_TILE = 256


def _to_rows(name, w, width):
    if name in COL_SHARDED:
        w = w.T
    return w.reshape(-1, width)


def _from_rows(name, rows, shard_shape):
    if name in COL_SHARDED:
        return rows.reshape(shard_shape[1], shard_shape[0]).T
    return rows.reshape(shard_shape)


def _pack_small(vals):
    flat = []
    for v in vals:
        f = v.reshape(-1)
        flat.append(jnp.pad(f, (0, (-f.shape[0]) % PACK_ROWS)))
    total = sum(f.shape[0] for f in flat)
    flat.append(jnp.zeros(((-total) % (GRAD_ROW_TILE * LANES),), F32))
    return jnp.concatenate(flat).reshape(-1, LANES)


def _unpack_small(packed, shapes):
    out, row = [], 0
    for shp in shapes:
        size = math.prod(shp)
        rows = -(-size // PACK_ROWS) * SUBLANES
        out.append(packed[row:row + rows].reshape(-1)[:size].reshape(shp))
        row += rows
    return out


def kernel(x, mem, ffn1_norm, ffn1_w_gate, ffn1_w_up, ffn1_w_down, mix_norm, w_in, pool_w, pool_scale, w_pool_proj, ssm_a_re, ssm_a_im, ssm_log_dt, ssm_b_re, ssm_b_im, ssm_c_re, ssm_c_im, ssm_d, w_glu_val, w_glu_gate, w_mix_out, xattn_norm, mem_norm, w_q, w_kv, w_xo, ffn2_norm, ffn2_w_gate, ffn2_w_up, ffn2_w_down, final_norm, loss_target, m_ffn1_norm, m_ffn1_w_gate, m_ffn1_w_up, m_ffn1_w_down, m_mix_norm, m_w_in, m_pool_w, m_pool_scale, m_w_pool_proj, m_ssm_a_re, m_ssm_a_im, m_ssm_log_dt, m_ssm_b_re, m_ssm_b_im, m_ssm_c_re, m_ssm_c_im, m_ssm_d, m_w_glu_val, m_w_glu_gate, m_w_mix_out, m_xattn_norm, m_mem_norm, m_w_q, m_w_kv, m_w_xo, m_ffn2_norm, m_ffn2_w_gate, m_ffn2_w_up, m_ffn2_w_down, m_final_norm, v_ffn1_norm, v_ffn1_w_gate, v_ffn1_w_up, v_ffn1_w_down, v_mix_norm, v_w_in, v_pool_w, v_pool_scale, v_w_pool_proj, v_ssm_a_re, v_ssm_a_im, v_ssm_log_dt, v_ssm_b_re, v_ssm_b_im, v_ssm_c_re, v_ssm_c_im, v_ssm_d, v_w_glu_val, v_w_glu_gate, v_w_mix_out, v_xattn_norm, v_mem_norm, v_w_q, v_w_kv, v_w_xo, v_ffn2_norm, v_ffn2_w_gate, v_ffn2_w_up, v_ffn2_w_down, v_final_norm):
    given = dict(locals())
    wts = {n: given[n] for n in WEIGHTS}
    moms = {n: (given["m_" + n], given["v_" + n]) for n in WEIGHTS}
    x2, mem2, tgt2 = x[0], mem[0], loss_target[0]
    d = x2.shape[1]
    core = lax.axis_index("c").astype(jnp.int32).reshape(1)
    chip = (2 * lax.axis_index("x") + lax.axis_index("y")).astype(jnp.int32).reshape(1)

    def full_form(n, f):
        shard = wts[n][0].shape
        return f.reshape(N_DEV * shard[1], shard[0]) if n in COL_SHARDED else f.reshape(N_DEV * shard[0], shard[1])

    shards = {n: _to_rows(n, wts[n][0], d).astype(BF16) for n in BIG}
    wt = {n: full_form(n, f) for n, f in zip(FFN1_BIG, _allgather("weight_allgather_ffn1", [shards[n] for n in FFN1_BIG]))}
    rest = [n for n in MAIN_BIG if n != "w_in"]
    gather_in = _gather_start("weight_gather_in_start", [shards["w_in"]], wt[FFN1_BIG[0]])
    gather_rest = _gather_start("weight_gather_rest_start", [shards[n] for n in rest], gather_in["token"])
    sm = {n: (wts[n].reshape(1, -1) if wts[n].ndim <= 2 else wts[n][0]) for n in SMALL}
    sm["ffn1_norm"] = sm["ffn1_norm"] + gather_rest["token"][0, 0]

    pending = {}

    def reduce_start(tag, names, gb):
        blocks = [gb[n].reshape(N_DEV, -1, d) for n in names]
        pad_rows = (-sum(b.shape[1] for b in blocks)) % GRAD_ROW_TILE
        packed = jnp.concatenate(blocks + ([jnp.zeros((N_DEV, pad_rows, d), BF16)] if pad_rows else []), axis=1)
        pair = _pair_sum("grad_pair_sum_" + tag, packed, _exchange_cores("grad_exchange_cores_" + tag, packed), core)
        pending[tag] = (pair, _chips_start("grad_exchange_chips_start_" + tag, pair), [b.shape[1] for b in blocks])
        return pending[tag][1]["token"]

    def reduce_finish(tag, after):
        pair, started, rows = pending[tag]
        recv = _split_wait("grad_exchange_chips_wait_" + tag, started, after)[0]
        return _chip_sum("grad_chip_sum_" + tag, pair, recv, chip), rows

    def ev(name, gb=None, gs=None, loss=None, marker=None):
        if name == "ffn1_fwd_done":
            wt["w_in"] = full_form("w_in", _split_wait("weight_gather_in_wait", gather_in, marker)[0])
        elif name == "mix_in_done":
            for n, f in zip(rest, _split_wait("weight_gather_rest_wait", gather_rest, marker)):
                wt[n] = full_form(n, f)
        elif name == "grads_main":
            return reduce_start("main", MAIN_BIG, gb)
        elif name == "small_early":
            pending["small"] = _slots_start("small_gather_start", _pack_small([gs[n] for n in EARLY_SMALL] + [loss[:, :1]]))
            return pending["small"]["token"]
        elif name == "grads_ffn1":
            return reduce_start("ffn1", FFN1_BIG, gb)
        return None

    _, dx, _, gs = _local_step(x2, mem2, tgt2, wt, sm, ev)

    out_g, out_d, out_m, out_v = {}, {}, {}, {}

    def update(n, g_full):
        shape = wts[n].shape
        two_d = (-1, shape[-1])
        dl, m2, v2 = _adamw("adamw_" + n, wts[n].reshape(two_d), g_full.reshape(two_d), moms[n][0].reshape(two_d),
                            moms[n][1].reshape(two_d))
        out_g[n], out_d[n], out_m[n], out_v[n] = g_full, dl.reshape(shape), m2.reshape(shape), v2.reshape(shape)
        return dl

    def update_big(names, g_rows, rows):
        off = 0
        for n, r in zip(names, rows):
            shard = wts[n].shape
            dl = update(n, _from_rows(n, g_rows[off:off + r], shard[1:]).reshape(shard))
            off += r
        return dl

    last = update_big(MAIN_BIG, *reduce_finish("main", dx))

    small_sum = _sum_slots("small_sum", _split_wait("small_gather_wait", pending["small"], dx)[0], F32)
    late = _allgather("small_allgather_late", [gs[LATE_SMALL].reshape(-1, LANES)])[0]
    late_sum = _sum_slots("small_sum_late", late.reshape(N_DEV, -1, LANES), F32)
    vals = _unpack_small(small_sum, [wts[n].shape for n in EARLY_SMALL] + [(1, 1)])
    total_loss = vals[-1].reshape(())
    for n, g_full in zip(EARLY_SMALL + [LATE_SMALL], vals[:-1] + [late_sum.reshape(wts[LATE_SMALL].shape)]):
        update(n, g_full)

    update_big(FFN1_BIG, *reduce_finish("ffn1", last))

    return (total_loss, dx[None], *[out_g[n] for n in WEIGHTS], *[out_d[n] for n in WEIGHTS],
            *[out_m[n] for n in WEIGHTS], *[out_v[n] for n in WEIGHTS])
```

```python
import functools
import math

import jax
import jax.numpy as jnp
from jax import lax
from jax.experimental import pallas as pl
from jax.experimental.pallas import tpu as pltpu

F32 = jnp.float32
BF16 = jnp.bfloat16
EPS = 1e-6
N_XHEADS = 4
POOL_WINDOWS = (2, 4, 8, 16)
ADAM_LR = 0.001
ADAM_B1 = 0.9
ADAM_B2 = 0.999
ADAM_EPS = 1e-08
ADAM_WD = 0.01
ADAM_STEP = 10
N_DEV = 8
VMEM_LIMIT_V7X = 48 * 1024 * 1024
LANES = 128
SUBLANES = 8
SUB_ROWS = 256
POOL_PAD = 16
MESH = pl.DeviceIdType.MESH
ANY = pl.BlockSpec(memory_space=pl.ANY)
HBM = pl.BlockSpec(memory_space=pltpu.HBM)
SEM = pl.BlockSpec(memory_space=pltpu.SEMAPHORE)
SIDE_EFFECT = pltpu.SideEffectType.DATAFLOW_SIDE_EFFECTING

_DIMS = {
    "nt": (((1,), (1,)), ((), ())),
    "nn": (((1,), (0,)), ((), ())),
    "tn": (((0,), (0,)), ((), ())),
}


def _pick(dim, pref, mult=LANES):
    if dim <= pref:
        return dim
    for t in range(pref - pref % mult, 0, -mult):
        if dim % t == 0:
            return t
    return dim


def _params(sem):
    return pltpu.CompilerParams(dimension_semantics=sem, vmem_limit_bytes=VMEM_LIMIT_V7X)


def _tile(tm, tn, coff=0):
    return pl.BlockSpec((tm, tn), lambda i, j: (i, j + coff))


def _rowvec(tn, coff=0):
    return pl.BlockSpec((1, tn), lambda i, j: (0, j + coff))


def _out(m, n, dtype):
    return jax.ShapeDtypeStruct((m, n), dtype)


def _mm(name, form, a_list, b_list, groups, m, n, tm, tn, extras, epilogue, outs, after=None, sub=SUB_ROWS):
    na, nb, ne = len(a_list), len(b_list), len(extras)
    pins = [] if after is None else [after]
    step = tm if (sub is None or form == "tn" or tm % sub) else sub

    def a_spec(a):
        if form == "tn":
            return pl.BlockSpec((a.shape[0], tm), lambda i, j: (0, i))
        return pl.BlockSpec((tm, a.shape[1]), lambda i, j: (i, 0))

    def b_spec(b):
        if form == "nt":
            return pl.BlockSpec((tn, b.shape[1]), lambda i, j: (j, 0))
        return pl.BlockSpec((b.shape[0], tn), lambda i, j: (0, j))

    def body(*refs):
        a_refs, b_refs = refs[:na], refs[na:na + nb]
        e_refs, o_refs = refs[na + nb:na + nb + ne], refs[na + nb + ne + len(pins):]
        b_vals = {}
        for s0 in range(0, tm, step):
            rows = slice(None) if step == tm else pl.ds(s0, step)
            a_vals, accs = {}, []
            for group in groups:
                acc = None
                for ai, bi in group:
                    if ai not in a_vals:
                        a_vals[ai] = (a_refs[ai][...] if form == "tn" else a_refs[ai][rows, :]).astype(BF16)
                    if bi not in b_vals:
                        b_vals[bi] = b_refs[bi][...].astype(BF16)
                    d = lax.dot_general(a_vals[ai], b_vals[bi], _DIMS[form], preferred_element_type=F32)
                    acc = d if acc is None else acc + d
                accs.append(acc)
            res = epilogue(accs, *[e[rows, :] if e.shape[0] == tm else e[...] for e in e_refs])
            for o_ref, r in zip(o_refs, res):
                o_ref[rows, :] = r.astype(o_ref.dtype)

    out_specs = [_tile(tm, tn) if s is None else s for _, s in outs]
    res = pl.pallas_call(
        body, name=name, grid=(m // tm, n // tn),
        in_specs=[a_spec(a) for a in a_list] + [b_spec(b) for b in b_list] + [s for _, s in extras] + [ANY] * len(pins),
        out_specs=out_specs, out_shape=[o for o, _ in outs],
        compiler_params=_params(("parallel", "parallel")),
    )(*a_list, *b_list, *[e for e, _ in extras], *pins)
    return res


def _mm1(name, form, a, b, m, n, tm, tn, dtype, scale=None):
    epi = (lambda accs: (accs[0],)) if scale is None else (lambda accs: (accs[0] * scale,))
    return _mm(name, form, [a], [b], [[(0, 0)]], m, n, tm, tn, [], epi, [(_out(m, n, dtype), None)])[0]


def _rms_fwd(name, h, g):
    t, d = h.shape
    tm = _pick(t, 512, SUBLANES)

    def body(h_ref, g_ref, n_ref):
        hv = h_ref[...]
        r = lax.rsqrt(jnp.mean(hv * hv, axis=-1, keepdims=True) + EPS)
        n_ref[...] = ((hv * r) * g_ref[...]).astype(BF16)

    return pl.pallas_call(
        body, name=name, grid=(t // tm,),
        in_specs=[pl.BlockSpec((tm, d), lambda i: (i, 0)), pl.BlockSpec((1, d), lambda i: (0, 0))],
        out_specs=pl.BlockSpec((tm, d), lambda i: (i, 0)), out_shape=_out(t, d, BF16),
        compiler_params=_params(("parallel",)),
    )(h, g)


def _rms_bwd(name, h, g, dn, dres=None):
    t, d = h.shape
    tm = _pick(t, 512, SUBLANES)
    need_dh = dres is not None

    def body(*refs):
        if need_dh:
            h_ref, g_ref, dn_ref, dres_ref, dh_ref, dhb_ref, dg_ref = refs
        else:
            h_ref, g_ref, dn_ref, dg_ref = refs
        hv = h_ref[...]
        r = lax.rsqrt(jnp.mean(hv * hv, axis=-1, keepdims=True) + EPS)
        nh = hv * r
        dnv = dn_ref[...].astype(F32)

        @pl.when(pl.program_id(0) == 0)
        def _():
            dg_ref[...] = jnp.zeros_like(dg_ref)

        dg_ref[...] += jnp.sum(dnv * nh, axis=0, keepdims=True)
        if need_dh:
            dng = dnv * g_ref[...]
            dh = dres_ref[...] + r * (dng - nh * jnp.mean(dng * nh, axis=-1, keepdims=True))
            dh_ref[...] = dh
            dhb_ref[...] = dh.astype(BF16)

    row = pl.BlockSpec((tm, d), lambda i: (i, 0))
    vec = pl.BlockSpec((1, d), lambda i: (0, 0))
    if need_dh:
        return pl.pallas_call(
            body, name=name, grid=(t // tm,), in_specs=[row, vec, row, row], out_specs=[row, row, vec],
            out_shape=[_out(t, d, F32), _out(t, d, BF16), _out(1, d, F32)], compiler_params=_params(("arbitrary",)),
        )(h, g, dn, dres)
    return pl.pallas_call(
        body, name=name, grid=(t // tm,), in_specs=[row, vec, row], out_specs=vec,
        out_shape=_out(1, d, F32), compiler_params=_params(("arbitrary",)),
    )(h, g, dn)


def _loss_head(h, g, tgt):
    t, d = h.shape
    tm = _pick(t, 512, SUBLANES)

    def body(h_ref, g_ref, t_ref, dh_ref, dhb_ref, dg_ref, loss_ref):
        hv = h_ref[...]
        r = lax.rsqrt(jnp.mean(hv * hv, axis=-1, keepdims=True) + EPS)
        nh = hv * r
        err = nh * g_ref[...] - t_ref[...]

        @pl.when(pl.program_id(0) == 0)
        def _():
            dg_ref[...] = jnp.zeros_like(dg_ref)
            loss_ref[...] = jnp.zeros_like(loss_ref)

        per_row = jnp.mean(err * err, axis=-1, keepdims=True)
        loss_ref[...] += 0.5 * jnp.sum(per_row, axis=0, keepdims=True)
        dy = err * (1.0 / d)
        dg_ref[...] += jnp.sum(dy * nh, axis=0, keepdims=True)
        dng = dy * g_ref[...]
        dh = r * (dng - nh * jnp.mean(dng * nh, axis=-1, keepdims=True))
        dh_ref[...] = dh
        dhb_ref[...] = dh.astype(BF16)

    row = pl.BlockSpec((tm, d), lambda i: (i, 0))
    vec = pl.BlockSpec((1, d), lambda i: (0, 0))
    return pl.pallas_call(
        body, name="loss_head", grid=(t // tm,), in_specs=[row, vec, row],
        out_specs=[row, row, vec, pl.BlockSpec((1, LANES), lambda i: (0, 0))],
        out_shape=[_out(t, d, F32), _out(t, d, BF16), _out(1, d, F32), _out(1, LANES, F32)],
        compiler_params=_params(("arbitrary",)),
    )(h, g, tgt)


def _ffn_fwd(tag, h, n, wg_t, wu_t, wd):
    t, d = h.shape
    f = wg_t.shape[0]
    tm, tn = _pick(t, 1024), _pick(f, 1408)

    def up_epi(accs):
        a, b = accs
        return a, b, (a * jax.nn.sigmoid(a)) * b

    a, b, hid = _mm(tag + "_up", "nt", [n], [wg_t, wu_t], [[(0, 0)], [(0, 1)]], t, f, tm, tn, [], up_epi,
                    [(_out(t, f, BF16), None)] * 3)
    if callable(wd):
        wd = wd(hid)
    tm2, tn2 = _pick(t, 1024), _pick(d, 512)
    h_out = _mm(tag + "_down", "nn", [hid], [wd], [[(0, 0)]], t, d, tm2, tn2, [(h, _tile(tm2, tn2))],
                lambda accs, hin: (hin + 0.5 * accs[0],), [(_out(t, d, F32), None)])[0]
    return h_out, (n, a, b, hid)


def _ffn_bwd(tag, h, g, wg_t, wu_t, wd, saved, dh, dh_bf, weights_done=None, after=None):
    n, a, b, hid = saved
    t, d = h.shape
    f = wd.shape[0]
    tm, tn = _pick(t, 1024), _pick(f, 1408)

    def hid_epi(accs, av, bv):
        dhid = 0.5 * accs[0]
        av, bv = av.astype(F32), bv.astype(F32)
        sig = jax.nn.sigmoid(av)
        da = dhid * bv * (sig * (1.0 + av * (1.0 - sig)))
        db = dhid * (av * sig)
        return da, db

    da, db = _mm(tag + "_bwd_hid", "nt", [dh_bf], [wd], [[(0, 0)]], t, f, tm, tn,
                 [(a, _tile(tm, tn)), (b, _tile(tm, tn))], hid_epi, [(_out(t, f, BF16), None)] * 2, after=after)
    tw, tnw = _pick(f, 1408), _pick(d, 512)
    d_wd = _mm1(tag + "_dwd", "tn", hid, dh_bf, f, d, tw, tnw, BF16, scale=0.5)
    d_wg = _mm1(tag + "_dwg", "tn", da, n, f, d, tw, tnw, BF16)
    d_wu = _mm1(tag + "_dwu", "tn", db, n, f, d, tw, tnw, BF16)
    pin = weights_done(d_wg, d_wu, d_wd) if weights_done is not None else None
    tm2, tn2 = _pick(t, 1024), _pick(d, 512)
    dn = _mm(tag + "_dn", "nn", [da, db], [wg_t, wu_t], [[(0, 0), (1, 1)]], t, d, tm2, tn2, [],
             lambda accs: (accs[0],), [(_out(t, d, F32), None)], after=pin)[0]
    dh_in, dh_in_bf, dg = _rms_bwd(tag + "_norm_bwd", h, g, dn, dh)
    return dh_in, dh_in_bf, dg, d_wg, d_wu, d_wd


def _window_sum(win, offsets):
    n = win.shape[0]
    acc = None
    for j in offsets:
        term = win if j == 0 else pltpu.roll(win, (-j) % n, 0)
        acc = term if acc is None else acc + term
    return acc


def _pool_counts(r0, ch, c, left, right, t):
    pos = r0 + lax.broadcasted_iota(jnp.int32, (ch, c), 0)
    return (jnp.minimum(pos + right + 1, t) - jnp.maximum(pos - left, 0)).astype(F32)


def _pool_fwd(proj, pool_w_bf, pool_scale):
    t = proj.shape[0]
    ng, c, _ = pool_w_bf.shape
    ch = _pick(t, 256, SUBLANES)
    pad = POOL_PAD

    def body(p_ref, w_ref, s_ref, pooled_ref, pm_ref, buf):
        grp = pl.program_id(0)
        buf[pl.ds(0, pad), :] = jnp.zeros((pad, c), F32)
        buf[pl.ds(pad + t, pad), :] = jnp.zeros((pad, c), F32)

        def fill(ci, carry):
            r0 = pl.multiple_of(ci * ch, SUBLANES)
            buf[pl.ds(pl.multiple_of(r0 + pad, SUBLANES), ch), :] = p_ref[pl.ds(r0, ch), :]
            return carry

        lax.fori_loop(0, t // ch, fill, 0)
        for gi, w in enumerate(POOL_WINDOWS):
            left = w // 2
            right = w - 1 - left

            @pl.when(grp == gi)
            def _(left=left, right=right):
                def chunk(ci, carry):
                    r0 = pl.multiple_of(ci * ch, SUBLANES)
                    win = buf[pl.ds(r0, ch + 2 * pad), :]
                    s = _window_sum(win, range(-left, right + 1))[pad:pad + ch]
                    pooled = s / _pool_counts(r0, ch, c, left, right, t) - win[pad:pad + ch]
                    pooled_bf = pooled.astype(BF16)
                    mixed = jnp.dot(pooled_bf, w_ref[0], preferred_element_type=F32)
                    pooled_ref[pl.ds(r0, ch), :] = pooled_bf
                    pm_ref[pl.ds(r0, ch), :] = (mixed * s_ref[...]).astype(BF16)
                    return carry

                lax.fori_loop(0, t // ch, chunk, 0)

    col = pl.BlockSpec((t, c), lambda g: (0, g))
    return pl.pallas_call(
        body, name="pool_fwd", grid=(ng,),
        in_specs=[col, pl.BlockSpec((1, c, c), lambda g: (g, 0, 0)), pl.BlockSpec((1, c), lambda g: (0, g))],
        out_specs=[col, col], out_shape=[_out(t, ng * c, BF16), _out(t, ng * c, BF16)],
        scratch_shapes=[pltpu.VMEM((t + 2 * pad, c), F32)],
        compiler_params=_params(("parallel",)),
    )(proj, pool_w_bf, pool_scale)


def _pool_bwd(pooled, dpm, pool_w_bf, pool_scale):
    t = pooled.shape[0]
    ng, c, _ = pool_w_bf.shape
    ch = _pick(t, 256, SUBLANES)
    pad = POOL_PAD

    def body(pooled_ref, dpm_ref, w_ref, s_ref, dp_ref, dw_ref, ds_ref, buf, raw):
        grp = pl.program_id(0)
        buf[pl.ds(0, pad), :] = jnp.zeros((pad, c), F32)
        buf[pl.ds(pad + t, pad), :] = jnp.zeros((pad, c), F32)
        dw_ref[...] = jnp.zeros_like(dw_ref)
        ds_ref[...] = jnp.zeros_like(ds_ref)
        for gi, w in enumerate(POOL_WINDOWS):
            left = w // 2
            right = w - 1 - left

            @pl.when(grp == gi)
            def _(left=left, right=right):
                def first(ci, carry):
                    r0 = pl.multiple_of(ci * ch, SUBLANES)
                    pv = pooled_ref[pl.ds(r0, ch), :]
                    dpm_v = dpm_ref[pl.ds(r0, ch), :]
                    mixed = jnp.dot(pv, w_ref[0], preferred_element_type=F32)
                    ds_ref[...] += jnp.sum(dpm_v * mixed, axis=0, keepdims=True)
                    dmixed = (dpm_v * s_ref[...]).astype(BF16)
                    dw_ref[0] += lax.dot_general(pv, dmixed, _DIMS["tn"], preferred_element_type=F32)
                    dpooled = lax.dot_general(dmixed, w_ref[0], _DIMS["nt"], preferred_element_type=F32)
                    raw[pl.ds(r0, ch), :] = dpooled
                    buf[pl.ds(pl.multiple_of(r0 + pad, SUBLANES), ch), :] = (
                        dpooled / _pool_counts(r0, ch, c, left, right, t))
                    return carry

                lax.fori_loop(0, t // ch, first, 0)

                def second(ci, carry):
                    r0 = pl.multiple_of(ci * ch, SUBLANES)
                    win = buf[pl.ds(r0, ch + 2 * pad), :]
                    s = _window_sum(win, range(-right, left + 1))[pad:pad + ch]
                    dp_ref[pl.ds(r0, ch), :] = (s - raw[pl.ds(r0, ch), :]).astype(BF16)
                    return carry

                lax.fori_loop(0, t // ch, second, 0)

    col = pl.BlockSpec((t, c), lambda g: (0, g))
    return pl.pallas_call(
        body, name="pool_bwd", grid=(ng,),
        in_specs=[col, col, pl.BlockSpec((1, c, c), lambda g: (g, 0, 0)), pl.BlockSpec((1, c), lambda g: (0, g))],
        out_specs=[col, pl.BlockSpec((1, c, c), lambda g: (g, 0, 0)), pl.BlockSpec((1, c), lambda g: (0, g))],
        out_shape=[_out(t, ng * c, BF16), jax.ShapeDtypeStruct((ng, c, c), F32), _out(1, ng * c, F32)],
        scratch_shapes=[pltpu.VMEM((t + 2 * pad, c), F32), pltpu.VMEM((t, c), F32)],
        compiler_params=_params(("parallel",)),
    )(pooled, dpm, pool_w_bf, pool_scale)


def _discretise(a_re, a_im, log_dt, b_re, b_im):
    dt = jnp.exp(log_dt)
    mag = jnp.exp(dt * a_re)
    ang = dt * a_im
    abr = mag * jnp.cos(ang)
    abi = mag * jnp.sin(ang)
    den = a_re * a_re + a_im * a_im
    nr = abr - 1.0
    qr = (nr * a_re + abi * a_im) / den
    qi = (abi * a_re - nr * a_im) / den
    return abr, abi, qr * b_re - qi * b_im, qr * b_im + qi * b_re


def _ssm_disc(cols):
    n, hh = cols[3].shape

    def body(ar, ai, ld, br, bi, o1, o2, o3, o4):
        res = _discretise(ar[...], ai[...], ld[...], br[...], bi[...])
        for o, r in zip((o1, o2, o3, o4), res):
            o[...] = r

    return pl.pallas_call(
        body, name="ssm_disc",
        out_shape=[_out(n, 1, F32), _out(n, 1, F32), _out(n, hh, F32), _out(n, hh, F32)],
    )(*cols)


def _ssm_disc_bwd(cols, cots):
    n, hh = cols[3].shape

    def body(ar, ai, ld, br, bi, c1, c2, c3, c4, o1, o2, o3, o4, o5):
        _, vjp = jax.vjp(_discretise, ar[...], ai[...], ld[...], br[...], bi[...])
        res = vjp((c1[...], c2[...], c3[...], c4[...]))
        for o, r in zip((o1, o2, o3, o4, o5), res):
            o[...] = r

    return pl.pallas_call(
        body, name="ssm_disc_bwd",
        out_shape=[_out(n, 1, F32)] * 3 + [_out(n, hh, F32)] * 2,
    )(*cols, *cots)


def _rowsum(name, a):
    r, _ = a.shape

    def body(a_ref, o_ref):
        o_ref[...] = jnp.sum(a_ref[...], axis=-1, keepdims=True)

    return pl.pallas_call(body, name=name, out_shape=_out(r, 1, F32))(a)


def _cmul(pr, pi, qr, qi):
    return pr * qr - pi * qi, pr * qi + pi * qr


def _cpow(pr, pi, n):
    rr, ri = None, None
    while n:
        if n & 1:
            rr, ri = (pr, pi) if rr is None else _cmul(rr, ri, pr, pi)
        n >>= 1
        if n:
            pr, pi = _cmul(pr, pi, pr, pi)
    return rr, ri


def _segment_carry(er, ei, pr, pi, reverse):
    row = lax.broadcasted_iota(jnp.int32, er.shape, 0)
    cr, ci = jnp.zeros_like(er), jnp.zeros_like(ei)
    for _ in range(SUBLANES - 1):
        tr = er + pr * cr - pi * ci
        ti = ei + pr * ci + pi * cr
        if reverse:
            keep, shift = row < SUBLANES - 1, SUBLANES - 1
        else:
            keep, shift = row >= 1, 1
        cr = jnp.where(keep, pltpu.roll(tr, shift, 0), 0.0)
        ci = jnp.where(keep, pltpu.roll(ti, shift, 0), 0.0)
    return cr, ci


def _ssm_fwd(name, sp, b_re, b_im, c_re, c_im, ar, ai, reverse):
    t, c = sp.shape
    s = ar.shape[1]
    w = _pick(s, 512)
    ch = _pick(t, 512, SUBLANES)
    n_ch, gpc, steps = t // ch, ch // SUBLANES, t // SUBLANES

    def body(sp_ref, bre_ref, bim_ref, cre_ref, cim_ref, ar_ref, ai_ref, xr_ref, xi_ref, y_ref, ur, ui, xbr, xbi):
        a_r = jnp.broadcast_to(ar_ref[...], (SUBLANES, w))
        a_i = jnp.broadcast_to(ai_ref[...], (SUBLANES, w))

        @pl.when(pl.program_id(0) == 0)
        def _():
            y_ref[...] = jnp.zeros_like(y_ref)

        def sweep(h0, store):
            def chunk(k, h):
                ci = n_ch - 1 - k if reverse else k
                rows = pl.ds(pl.multiple_of(ci * ch, ch), ch)
                spv = sp_ref[rows, :].astype(BF16)
                ur[...] = jnp.dot(spv, bre_ref[...], preferred_element_type=F32)
                ui[...] = jnp.dot(spv, bim_ref[...], preferred_element_type=F32)

                def group(g, hh):
                    gi = gpc - 1 - g if reverse else g
                    r0 = pl.multiple_of(gi * SUBLANES, SUBLANES)
                    hr, hi = hh
                    nr = a_r * hr - a_i * hi + ur[pl.ds(r0, SUBLANES), :]
                    ni = a_r * hi + a_i * hr + ui[pl.ds(r0, SUBLANES), :]
                    if store:
                        xbr[pl.ds(r0, SUBLANES), :] = nr
                        xbi[pl.ds(r0, SUBLANES), :] = ni
                    return nr, ni

                h = lax.fori_loop(0, gpc, group, h)
                if store:
                    xr16, xi16 = xbr[...].astype(BF16), xbi[...].astype(BF16)
                    xr_ref[rows, :] = xr16
                    xi_ref[rows, :] = xi16
                    y_ref[rows, :] += (jnp.dot(xr16, cre_ref[...], preferred_element_type=F32)
                                       + jnp.dot(xi16, cim_ref[...], preferred_element_type=F32))
                return h

            return lax.fori_loop(0, n_ch, chunk, h0)

        zero = jnp.zeros((SUBLANES, w), F32)
        er, ei = sweep((zero, zero), False)
        pr, pi = _cpow(ar_ref[...], ai_ref[...], steps)
        sweep(_segment_carry(er, ei, pr, pi, reverse), True)

    col = lambda i: (0, i)
    return pl.pallas_call(
        body, name=name, grid=(s // w,),
        in_specs=[pl.BlockSpec((t, c), lambda i: (0, 0)), pl.BlockSpec((c, w), col), pl.BlockSpec((c, w), col),
                  pl.BlockSpec((w, c), lambda i: (i, 0)), pl.BlockSpec((w, c), lambda i: (i, 0)),
                  pl.BlockSpec((1, w), col), pl.BlockSpec((1, w), col)],
        out_specs=[pl.BlockSpec((t, w), col), pl.BlockSpec((t, w), col), pl.BlockSpec((t, c), lambda i: (0, 0))],
        out_shape=[_out(t, s, BF16), _out(t, s, BF16), _out(t, c, F32)],
        scratch_shapes=[pltpu.VMEM((ch, w), F32)] * 4,
        compiler_params=_params(("arbitrary",)),
    )(sp, b_re, b_im, c_re, c_im, ar, ai)


def _ssm_bwd(name, dyp, c_re, c_im, xr, xi, ar, ai, reverse):
    t, c = dyp.shape
    s = ar.shape[1]
    w = _pick(s, 512)
    ch = _pick(t, 512, SUBLANES)
    n_ch, gpc, steps = t // ch, ch // SUBLANES, t // SUBLANES
    back = not reverse
    edge = 2 * SUBLANES

    def body(dy_ref, cre_ref, cim_ref, xr_ref, xi_ref, ar_ref, ai_ref, lr_ref, li_ref, dar_ref, dai_ref,
             gr, gi_, lbr, lbi, xbr, xbi):
        a_r = jnp.broadcast_to(ar_ref[...], (SUBLANES, w))
        a_i = -jnp.broadcast_to(ai_ref[...], (SUBLANES, w))
        row = lax.broadcasted_iota(jnp.int32, (SUBLANES, w), 0)

        def neighbours(ci, x_ref, buf):
            rows = pl.ds(pl.multiple_of(ci * ch, ch), ch)
            if reverse:
                buf[pl.ds(0, ch), :] = x_ref[rows, :].astype(F32)
                nxt = x_ref[pl.ds(pl.multiple_of(jnp.minimum(ci + 1, n_ch - 1) * ch, ch), edge), :].astype(F32)[:SUBLANES]
                first = x_ref[pl.ds(0, edge), :].astype(F32)[:SUBLANES]
                wrap = jnp.where(row < SUBLANES - 1, pltpu.roll(first, SUBLANES - 1, 0), 0.0)
                buf[pl.ds(ch, SUBLANES), :] = jnp.where(ci == n_ch - 1, wrap, nxt)
            else:
                buf[pl.ds(SUBLANES, ch), :] = x_ref[rows, :].astype(F32)
                prv = x_ref[pl.ds(pl.multiple_of(jnp.maximum(ci * ch - edge, 0), edge), edge), :].astype(F32)[SUBLANES:]
                last = x_ref[pl.ds(t - edge, edge), :].astype(F32)[SUBLANES:]
                wrap = jnp.where(row >= 1, pltpu.roll(last, 1, 0), 0.0)
                buf[pl.ds(0, SUBLANES), :] = jnp.where(ci == 0, wrap, prv)

        def sweep(h0, store):
            def chunk(k, carry):
                ci = n_ch - 1 - k if back else k
                rows = pl.ds(pl.multiple_of(ci * ch, ch), ch)
                dyv = dy_ref[rows, :].astype(BF16)
                gr[...] = lax.dot_general(dyv, cre_ref[...], _DIMS["nt"], preferred_element_type=F32)
                gi_[...] = lax.dot_general(dyv, cim_ref[...], _DIMS["nt"], preferred_element_type=F32)
                if store:
                    neighbours(ci, xr_ref, xbr)
                    neighbours(ci, xi_ref, xbi)

                def group(g, cc):
                    gidx = gpc - 1 - g if back else g
                    r0 = pl.multiple_of(gidx * SUBLANES, SUBLANES)
                    hr, hi = cc[0], cc[1]
                    nr = a_r * hr - a_i * hi + gr[pl.ds(r0, SUBLANES), :]
                    ni = a_r * hi + a_i * hr + gi_[pl.ds(r0, SUBLANES), :]
                    if not store:
                        return nr, ni
                    lbr[pl.ds(r0, SUBLANES), :] = nr
                    lbi[pl.ds(r0, SUBLANES), :] = ni
                    x0 = pl.multiple_of(r0 + SUBLANES, SUBLANES) if reverse else r0
                    xpr, xpi = xbr[pl.ds(x0, SUBLANES), :], xbi[pl.ds(x0, SUBLANES), :]
                    return nr, ni, cc[2] + nr * xpr + ni * xpi, cc[3] + ni * xpr - nr * xpi

                carry = lax.fori_loop(0, gpc, group, carry)
                if store:
                    lr_ref[rows, :] = lbr[...].astype(BF16)
                    li_ref[rows, :] = lbi[...].astype(BF16)
                return carry

            return lax.fori_loop(0, n_ch, chunk, h0)

        zero = jnp.zeros((SUBLANES, w), F32)
        er, ei = sweep((zero, zero), False)
        pr, pi = _cpow(ar_ref[...], -ai_ref[...], steps)
        cr, ci0 = _segment_carry(er, ei, pr, pi, back)
        _, _, dar, dai = sweep((cr, ci0, zero, zero), True)
        dar_ref[...] = jnp.sum(dar, axis=0, keepdims=True)
        dai_ref[...] = jnp.sum(dai, axis=0, keepdims=True)

    col = lambda i: (0, i)
    return pl.pallas_call(
        body, name=name, grid=(s // w,),
        in_specs=[pl.BlockSpec((t, c), lambda i: (0, 0)), pl.BlockSpec((w, c), lambda i: (i, 0)),
                  pl.BlockSpec((w, c), lambda i: (i, 0)), pl.BlockSpec((t, w), col), pl.BlockSpec((t, w), col),
                  pl.BlockSpec((1, w), col), pl.BlockSpec((1, w), col)],
        out_specs=[pl.BlockSpec((t, w), col), pl.BlockSpec((t, w), col), pl.BlockSpec((1, w), col), pl.BlockSpec((1, w), col)],
        out_shape=[_out(t, s, BF16), _out(t, s, BF16), _out(1, s, F32), _out(1, s, F32)],
        scratch_shapes=[pltpu.VMEM((ch, w), F32)] * 4 + [pltpu.VMEM((ch + SUBLANES, w), F32)] * 2,
        compiler_params=_params(("parallel",)),
    )(dyp, c_re, c_im, xr, xi, ar, ai)


def _to_segments(a):
    t, c = a.shape
    return a.reshape(SUBLANES, t // SUBLANES, c).transpose(1, 0, 2).reshape(t, c)


def _from_segments(a):
    t, c = a.shape
    return a.reshape(t // SUBLANES, SUBLANES, c).transpose(1, 0, 2).reshape(t, c)


def _colsum_prod(name, a, b, b_coff=0):
    t, n = a.shape
    tm = _pick(t, 512, SUBLANES)

    def body(a_ref, b_ref, o_ref):
        @pl.when(pl.program_id(0) == 0)
        def _():
            o_ref[...] = jnp.zeros_like(o_ref)

        o_ref[...] += jnp.sum(a_ref[...].astype(F32) * b_ref[...].astype(F32), axis=0, keepdims=True)

    return pl.pallas_call(
        body, name=name, grid=(t // tm,),
        in_specs=[pl.BlockSpec((tm, n), lambda i: (i, 0)), pl.BlockSpec((tm, n), lambda i: (i, b_coff))],
        out_specs=pl.BlockSpec((1, n), lambda i: (0, 0)), out_shape=_out(1, n, F32),
        compiler_params=_params(("arbitrary",)),
    )(a, b)


def _bd_in(bb, g, p, hh):
    blk = bb.reshape(g, p, hh).transpose(0, 2, 1)
    eye = jnp.eye(g, dtype=bool)[:, None, :, None]
    return jnp.where(eye, blk[:, :, None, :], 0.0).reshape(g * hh, g * p)


def _bd_out(cc, g, p, hh):
    blk = cc.transpose(0, 2, 1)
    eye = jnp.eye(g, dtype=bool)[:, None, :, None]
    return jnp.where(eye, blk[:, :, None, :], 0.0).reshape(g * p, g * hh)


def _diag_in(dmat, g, p, hh):
    eye = jnp.eye(g, dtype=bool)[:, None, :, None]
    diag = jnp.sum(jnp.where(eye, dmat.reshape(g, hh, g, p), 0.0), axis=2)
    return diag.transpose(0, 2, 1).reshape(g * p, hh)


def _diag_out(dmat, g, p, hh):
    eye = jnp.eye(g, dtype=bool)[:, None, :, None]
    diag = jnp.sum(jnp.where(eye, dmat.reshape(g, p, g, hh), 0.0), axis=2)
    return diag.transpose(0, 2, 1)


def _softmax(qh, kh, scale):
    s = lax.dot_general(qh, kh, _DIMS["nt"], preferred_element_type=F32) * scale
    e = jnp.exp(s - jnp.max(s, axis=-1, keepdims=True))
    return e / jnp.sum(e, axis=-1, keepdims=True)


def _attn_fwd(q, kv):
    t, d = q.shape
    mm_ = kv.shape[0]
    hd = d // N_XHEADS
    scale = 1.0 / math.sqrt(hd)
    tm = _pick(t, 512, SUBLANES)

    def body(q_ref, kv_ref, o_ref):
        for h in range(N_XHEADS):
            sl = pl.ds(h * hd, hd)
            p = _softmax(q_ref[:, sl], kv_ref[:, sl], scale)
            o_ref[:, sl] = jnp.dot(p.astype(BF16), kv_ref[:, pl.ds(d + h * hd, hd)],
                                   preferred_element_type=F32).astype(BF16)

    return pl.pallas_call(
        body, name="attn_fwd", grid=(t // tm,),
        in_specs=[pl.BlockSpec((tm, d), lambda i: (i, 0)), pl.BlockSpec((mm_, 2 * d), lambda i: (0, 0))],
        out_specs=pl.BlockSpec((tm, d), lambda i: (i, 0)), out_shape=_out(t, d, BF16),
        compiler_params=_params(("parallel",)),
    )(q, kv)


def _attn_bwd(q, kv, do):
    t, d = q.shape
    mm_ = kv.shape[0]
    hd = d // N_XHEADS
    scale = 1.0 / math.sqrt(hd)
    tm = _pick(t, 512, SUBLANES)

    def body(q_ref, kv_ref, do_ref, dq_ref, dkv_ref):
        @pl.when(pl.program_id(0) == 0)
        def _():
            dkv_ref[...] = jnp.zeros_like(dkv_ref)

        for h in range(N_XHEADS):
            sl = pl.ds(h * hd, hd)
            vsl = pl.ds(d + h * hd, hd)
            qh, kh, doh = q_ref[:, sl], kv_ref[:, sl], do_ref[:, sl]
            p = _softmax(qh, kh, scale)
            dp = lax.dot_general(doh, kv_ref[:, vsl], _DIMS["nt"], preferred_element_type=F32)
            dkv_ref[:, vsl] += lax.dot_general(p.astype(BF16), doh, _DIMS["tn"], preferred_element_type=F32)
            ds = (p * (dp - jnp.sum(dp * p, axis=-1, keepdims=True)) * scale).astype(BF16)
            dq_ref[:, sl] = jnp.dot(ds, kh, preferred_element_type=F32).astype(BF16)
            dkv_ref[:, sl] += lax.dot_general(ds, qh, _DIMS["tn"], preferred_element_type=F32)

    row = pl.BlockSpec((tm, d), lambda i: (i, 0))
    full = pl.BlockSpec((mm_, 2 * d), lambda i: (0, 0))
    return pl.pallas_call(
        body, name="attn_bwd", grid=(t // tm,), in_specs=[row, full, row], out_specs=[row, full],
        out_shape=[_out(t, d, BF16), _out(mm_, 2 * d, F32)], compiler_params=_params(("arbitrary",)),
    )(q, kv, do)


def _ew(name, fn, ins, outs, rows_pref=256, rowvecs=()):
    r, c = ins[0].shape
    tr = _pick(r, rows_pref, SUBLANES)
    ni = len(ins) + len(rowvecs)

    def body(*refs):
        res = fn(*[x[...] for x in refs[:ni]])
        for o_ref, v in zip(refs[ni:], res):
            o_ref[...] = v.astype(o_ref.dtype)

    blk = pl.BlockSpec((tr, c), lambda i: (i, 0))
    vec = pl.BlockSpec((1, c), lambda i: (0, 0))
    return pl.pallas_call(
        body, name=name, grid=(r // tr,), in_specs=[blk] * len(ins) + [vec] * len(rowvecs), out_specs=[blk] * len(outs),
        out_shape=[_out(r, c, dt) for dt in outs], compiler_params=_params(("parallel",)),
    )(*ins, *rowvecs)


def _sum_slots(name, a, dtype):
    s, r, c = a.shape
    tr = _pick(r, 256, SUBLANES)

    def body(a_ref, o_ref):
        acc = a_ref[0].astype(F32)
        for k in range(1, s):
            acc = acc + a_ref[k].astype(F32)
        o_ref[...] = acc.astype(o_ref.dtype)

    return pl.pallas_call(
        body, name=name, grid=(r // tr,), in_specs=[pl.BlockSpec((s, tr, c), lambda i: (0, i, 0))],
        out_specs=pl.BlockSpec((tr, c), lambda i: (i, 0)), out_shape=_out(r, c, dtype),
        compiler_params=_params(("parallel",)),
    )(a)


def _adamw(name, w, g, m, v):
    bc1 = 1.0 - ADAM_B1 ** ADAM_STEP
    bc2 = 1.0 - ADAM_B2 ** ADAM_STEP

    def fn(wv, gv, mv, vv):
        m2 = ADAM_B1 * mv + (1.0 - ADAM_B1) * gv
        v2 = ADAM_B2 * vv + (1.0 - ADAM_B2) * (gv * gv)
        delta = -ADAM_LR * ((m2 / bc1) / (jnp.sqrt(v2 / bc2) + ADAM_EPS) + ADAM_WD * wv)
        return delta, m2, v2

    return _ew(name, fn, [w, g, m, v], [F32, F32, F32])


def _allgather(name, arrs):
    n = len(arrs)

    def body(*refs):
        ins, outs = refs[:n], refs[n:2 * n]
        send_sems, recv_sems, local_sems = refs[2 * n:]
        x, y, c = lax.axis_index("x"), lax.axis_index("y"), lax.axis_index("c")
        me, sibling = (x, y, c), (x, y, 1 - c)
        chips = [(1 - x, y), (x, 1 - y), (1 - x, 1 - y)]

        def rows(a, px, py, pc):
            r = ins[a].shape[0]
            return outs[a].at[pl.ds((4 * px + 2 * py + pc) * r, r), :]

        def copy(a, k, block, to, src=None):
            return pltpu.make_async_remote_copy(
                src_ref=rows(a, *block) if src is None else src, dst_ref=rows(a, *block),
                send_sem=send_sems.at[a, k], recv_sem=recv_sems.at[a, k], device_id=to, device_id_type=MESH)

        mine = [pltpu.make_async_copy(ins[a], rows(a, *me), local_sems.at[a]) for a in range(n)]
        for cp in mine:
            cp.start()
        first = []
        for a in range(n):
            first.append(copy(a, 0, me, sibling, src=ins[a]))
            first += [copy(a, 1 + j, me, (*chip, c), src=ins[a]) for j, chip in enumerate(chips)]
        for cp in first:
            cp.start()
        passed = []
        for j, chip in enumerate(chips):
            for a in range(n):
                copy(a, 1 + j, (*chip, c), me).wait_recv()
                cp = copy(a, 4 + j, (*chip, c), sibling)
                cp.start()
                passed.append(cp)
        for a in range(n):
            copy(a, 0, sibling, me).wait_recv()
            for j, chip in enumerate(chips):
                copy(a, 4 + j, (*chip, 1 - c), me).wait_recv()
        for cp in first + passed:
            cp.wait_send()
        for cp in mine:
            cp.wait()

    return pl.pallas_call(
        body, name=name, in_specs=[ANY] * n, out_specs=[ANY] * n,
        out_shape=[_out(N_DEV * a.shape[0], a.shape[1], a.dtype) for a in arrs],
        scratch_shapes=[pltpu.SemaphoreType.DMA((n, 7)), pltpu.SemaphoreType.DMA((n, 7)), pltpu.SemaphoreType.DMA((n,))],
    )(*arrs)


def _exchange_cores(name, g):
    _, r, c = g.shape
    nck = r // GRAD_ROW_TILE

    def body(g_ref, recv_ref, send_sems, recv_sems):
        x, y, cc = lax.axis_index("x"), lax.axis_index("y"), lax.axis_index("c")
        copies = []
        for q in range(4):
            for k in range(nck):
                rows = pl.ds(k * GRAD_ROW_TILE, GRAD_ROW_TILE)
                copies.append(pltpu.make_async_remote_copy(
                    src_ref=g_ref.at[2 * q + (1 - cc), rows], dst_ref=recv_ref.at[q, rows],
                    send_sem=send_sems.at[q, k], recv_sem=recv_sems.at[q, k], device_id=(x, y, 1 - cc),
                    device_id_type=MESH))
        for cp in copies:
            cp.start()
        for cp in copies:
            cp.wait()

    return pl.pallas_call(
        body, name=name, in_specs=[ANY], out_specs=ANY,
        out_shape=jax.ShapeDtypeStruct((4, r, c), g.dtype),
        scratch_shapes=[pltpu.SemaphoreType.DMA((4, nck)), pltpu.SemaphoreType.DMA((4, nck))],
    )(g)


def _pair_sum(name, g, recv, core):
    _, r, c = g.shape
    tr = _pick(r, 5 * GRAD_ROW_TILE, GRAD_ROW_TILE)

    def body(core_ref, g_ref, r_ref, o_ref):
        o_ref[...] = (g_ref[...].astype(F32) + r_ref[...].astype(F32)).astype(o_ref.dtype)

    blk = pl.BlockSpec((None, tr, c), lambda q, i, core_ref: (q, i, 0))
    return pl.pallas_call(
        body, name=name,
        grid_spec=pltpu.PrefetchScalarGridSpec(
            num_scalar_prefetch=1, grid=(4, r // tr),
            in_specs=[pl.BlockSpec((None, tr, c), lambda q, i, core_ref: (2 * q + core_ref[0], i, 0)), blk],
            out_specs=blk),
        out_shape=jax.ShapeDtypeStruct((4, r, c), g.dtype), compiler_params=_params(("parallel", "parallel")),
    )(core, g, recv)


def _peer(k, x, y, c):
    return (1 - x if k & 4 else x, 1 - y if k & 2 else y, 1 - c if k & 1 else c)


def _split_start(name, srcs, land_shapes, n_remote, n_local, build, after=None):
    ns, nl = len(srcs), len(land_shapes)
    n_sem = 3 if n_local else 2
    pins = [] if after is None else [after]

    def body(*refs):
        src_refs, land_refs = refs[:ns], refs[ns:ns + nl]
        sems = refs[ns + nl + len(pins):ns + nl + len(pins) + n_sem]
        token = refs[-1]
        remote, local = build(src_refs, land_refs, *sems)
        for cp in local + remote:
            cp.start()
        token[...] = jnp.zeros_like(token)

    sem_shapes = [pltpu.SemaphoreType.DMA((n_remote,)), pltpu.SemaphoreType.DMA((n_remote,))]
    if n_local:
        sem_shapes.append(pltpu.SemaphoreType.DMA((n_local,)))
    bufs = [pltpu.with_memory_space_constraint(a, pltpu.HBM) for a in srcs]
    bufs += [pltpu.with_memory_space_constraint(lax.empty(s.shape, s.dtype), pltpu.HBM) for s in land_shapes]
    outs = pl.pallas_call(
        body, name=name,
        out_shape=sem_shapes + [pltpu.HBM(b.shape, b.dtype) for b in bufs] + [jax.ShapeDtypeStruct((SUBLANES, LANES), F32)],
        in_specs=[HBM] * (ns + nl) + [ANY] * len(pins),
        out_specs=[SEM] * n_sem + [HBM] * (ns + nl) + [pl.BlockSpec(memory_space=pltpu.VMEM)],
        input_output_aliases={i: n_sem + i for i in range(ns + nl)},
        compiler_params=pltpu.CompilerParams(has_side_effects=SIDE_EFFECT),
    )(*bufs, *pins)
    return dict(sems=list(outs[:n_sem]), bufs=list(outs[n_sem:n_sem + ns + nl]), token=outs[-1], build=build, ns=ns)


def _split_wait(name, started, after):
    ns, n_buf, n_sem = started["ns"], len(started["bufs"]), len(started["sems"])

    def body(*refs):
        src_refs, land_refs = refs[:ns], refs[ns:n_buf]
        sems = refs[n_buf:n_buf + n_sem]
        remote, local = started["build"](src_refs, land_refs, *sems)
        for cp in local:
            cp.wait()
        for cp in remote:
            cp.wait_send()
            cp.wait_recv()

    outs = pl.pallas_call(
        body, name=name, out_shape=[pltpu.HBM(b.shape, b.dtype) for b in started["bufs"]],
        in_specs=[HBM] * n_buf + [SEM] * n_sem + [ANY], out_specs=[HBM] * n_buf,
        input_output_aliases={i: i for i in range(n_buf)},
        compiler_params=pltpu.CompilerParams(has_side_effects=SIDE_EFFECT),
    )(*started["bufs"], *started["sems"], after)
    return list(outs[ns:])


def _gather_start(name, shards, after):
    m = len(shards)

    def build(src_refs, land_refs, send_sems, recv_sems, local_sems):
        x, y, c = lax.axis_index("x"), lax.axis_index("y"), lax.axis_index("c")
        remote, local = [], []
        for j in range(m):
            r = src_refs[j].shape[0]
            dst = land_refs[j].at[pl.ds((4 * x + 2 * y + c) * r, r), :]
            local.append(pltpu.make_async_copy(src_refs[j], dst, local_sems.at[j]))
            for k in range(1, N_DEV):
                remote.append(pltpu.make_async_remote_copy(
                    src_ref=src_refs[j], dst_ref=dst, send_sem=send_sems.at[7 * j + k - 1],
                    recv_sem=recv_sems.at[7 * j + k - 1], device_id=_peer(k, x, y, c), device_id_type=MESH))
        return remote, local

    lands = [jax.ShapeDtypeStruct((N_DEV * a.shape[0], a.shape[1]), a.dtype) for a in shards]
    return _split_start(name, shards, lands, 7 * m, m, build, after)


def _slots_start(name, a):
    def build(src_refs, land_refs, send_sems, recv_sems, local_sems):
        x, y, c = lax.axis_index("x"), lax.axis_index("y"), lax.axis_index("c")
        dst = land_refs[0].at[4 * x + 2 * y + c]
        local = [pltpu.make_async_copy(src_refs[0], dst, local_sems.at[0])]
        remote = [pltpu.make_async_remote_copy(
            src_ref=src_refs[0], dst_ref=dst, send_sem=send_sems.at[k - 1], recv_sem=recv_sems.at[k - 1],
            device_id=_peer(k, x, y, c), device_id_type=MESH) for k in range(1, N_DEV)]
        return remote, local

    return _split_start(name, [a], [jax.ShapeDtypeStruct((N_DEV,) + a.shape, a.dtype)], 7, 1, build)


def _chips_start(name, p):
    _, r, c = p.shape
    nck = r // GRAD_ROW_TILE

    def build(src_refs, land_refs, send_sems, recv_sems):
        x, y, cc = lax.axis_index("x"), lax.axis_index("y"), lax.axis_index("c")
        remote = []
        for k in range(1, 4):
            px = 1 - x if k >> 1 else x
            py = 1 - y if k & 1 else y
            for j in range(nck):
                rows = pl.ds(j * GRAD_ROW_TILE, GRAD_ROW_TILE)
                remote.append(pltpu.make_async_remote_copy(
                    src_ref=src_refs[0].at[2 * px + py, rows], dst_ref=land_refs[0].at[k - 1, rows],
                    send_sem=send_sems.at[(k - 1) * nck + j], recv_sem=recv_sems.at[(k - 1) * nck + j],
                    device_id=(px, py, cc), device_id_type=MESH))
        return remote, []

    return _split_start(name, [p], [jax.ShapeDtypeStruct((3, r, c), p.dtype)], 3 * nck, 0, build)


def _chip_sum(name, p, recv, chip):
    _, r, c = p.shape
    tr = _pick(r, 5 * GRAD_ROW_TILE, GRAD_ROW_TILE)

    def body(chip_ref, p_ref, r_ref, o_ref):
        acc = p_ref[...].astype(F32)
        for k in range(3):
            acc = acc + r_ref[k].astype(F32)
        o_ref[...] = acc

    return pl.pallas_call(
        body, name=name,
        grid_spec=pltpu.PrefetchScalarGridSpec(
            num_scalar_prefetch=1, grid=(r // tr,),
            in_specs=[pl.BlockSpec((None, tr, c), lambda i, chip_ref: (chip_ref[0], i, 0)),
                      pl.BlockSpec((3, tr, c), lambda i, chip_ref: (0, i, 0))],
            out_specs=pl.BlockSpec((tr, c), lambda i, chip_ref: (i, 0))),
        out_shape=_out(r, c, F32), compiler_params=_params(("parallel",)),
    )(chip, p, recv)


def _local_step(x, mem, tgt, wt, sm, ev=None):
    t, d = x.shape
    n_mem = mem.shape[0]
    d_pool = sm["pool_scale"].shape[1]
    ng, pc = sm["pool_w"].shape[0], sm["pool_w"].shape[1]
    d_ssm = sm["ssm_d"].shape[1]
    _, sg, sp, sh = sm["ssm_b_re"].shape
    n_state = sg * sp
    gb, gs = {}, {}

    def emit(name, **kw):
        return ev(name, **kw) if ev is not None else None

    n1 = _rms_fwd("ffn1_norm", x, sm["ffn1_norm"])
    emit("ffn1_norm_done", marker=n1)
    def ffn1_down(hid):
        emit("ffn1_up_done", marker=hid)
        return wt["ffn1_w_down"]

    h1, ffn1_saved = _ffn_fwd("ffn1", x, n1, wt["ffn1_w_gate"], wt["ffn1_w_up"], ffn1_down)
    emit("ffn1_fwd_done", marker=h1)
    u = _rms_fwd("mix_norm", h1, sm["mix_norm"])
    d_in = wt["w_in"].shape[0]
    tm, tn = _pick(t, 1024), _pick(d_in, 1408)
    proj = _mm1("in_proj", "nt", u, wt["w_in"], t, d_in, tm, tn, F32)
    off_s = d_pool // d_ssm
    off_gp = (d_pool + d_ssm)
    off_gs = off_gp + d

    pool_w_bf = sm["pool_w"].astype(BF16)
    pooled, pm = _pool_fwd(proj, pool_w_bf, sm["pool_scale"])

    cols = [sm["ssm_a_re"].reshape(-1, 1), sm["ssm_a_im"].reshape(-1, 1),
            jnp.broadcast_to(sm["ssm_log_dt"][:, :, None], (2, sg, sp)).reshape(-1, 1),
            sm["ssm_b_re"].reshape(-1, sh), sm["ssm_b_im"].reshape(-1, sh)]
    abr, abi, bbr, bbi = _ssm_disc(cols)
    abr2, abi2 = abr.reshape(2, n_state), abi.reshape(2, n_state)
    bbr4, bbi4 = bbr.reshape(2, sg * sp, sh), bbi.reshape(2, sg * sp, sh)
    b_re = [_bd_in(bbr4[dr], sg, sp, sh).astype(BF16) for dr in range(2)]
    b_im = [_bd_in(bbi4[dr], sg, sp, sh).astype(BF16) for dr in range(2)]
    c_re = [_bd_out(sm["ssm_c_re"][dr], sg, sp, sh).astype(BF16) for dr in range(2)]
    c_im = [_bd_out(-sm["ssm_c_im"][dr], sg, sp, sh).astype(BF16) for dr in range(2)]
    sp32 = _to_segments(proj[:, d_pool:d_pool + d_ssm])
    xs, y_parts = [], []
    for dr in range(2):
        xr, xi, y_part = _ssm_fwd(f"ssm_fwd{dr}", sp32, b_re[dr], b_im[dr], c_re[dr], c_im[dr], abr2[dr:dr + 1],
                                  abi2[dr:dr + 1], reverse=(dr == 1))
        xs.append((xr, xi))
        y_parts.append(y_part)
    y = _from_segments(_ew("ssm_sum", lambda p0, p1, sv, dv: (p0 + p1 + sv * dv,), y_parts + [sp32], [F32],
                           rowvecs=[sm["ssm_d"]])[0])
    tmy = _pick(t, 256)
    ys = _ew("ssm_gelu", lambda v: (jax.nn.gelu(v),), [y], [BF16])[0]
    emit("mix_in_done", marker=ys)

    tmm, tnm, tnx = _pick(t, 1024), _pick(d, 256), _pick(d, 512)
    gp_spec = _tile(tmm, tnm, off_gp // tnm)
    gs_spec = _tile(tmm, tnm, off_gs // tnm)

    def merge_epi(accs, gpv, gsv):
        z_pool, val, gate = accs
        return (jax.nn.sigmoid(gpv) * z_pool + jax.nn.sigmoid(gsv) * (val * jax.nn.sigmoid(gate)),)

    merged = _mm("mix_merge", "nt", [pm, ys], [wt["w_pool_proj"], wt["w_glu_val"], wt["w_glu_gate"]],
                 [[(0, 0)], [(1, 1)], [(1, 2)]], t, d, tmm, tnm, [(proj, gp_spec), (proj, gs_spec)], merge_epi,
                 [(_out(t, d, BF16), None)])[0]
    res_epi = lambda accs, hin: (hin + accs[0],)
    h2 = _mm("mix_out", "nn", [merged], [wt["w_mix_out"]], [[(0, 0)]], t, d, tmm, tnx, [(h1, _tile(tmm, tnx))],
             res_epi, [(_out(t, d, F32), None)])[0]

    un = _rms_fwd("xattn_norm", h2, sm["xattn_norm"])
    mn = _rms_fwd("mem_norm", mem, sm["mem_norm"])
    q = _mm1("xattn_q", "nn", un, wt["w_q"], t, d, tmm, tnx, BF16)
    kv = _mm1("xattn_kv", "nt", mn, wt["w_kv"], n_mem, 2 * d, n_mem, _pick(2 * d, 512), BF16)
    o = _attn_fwd(q, kv)
    h3 = _mm("xattn_out", "nn", [o], [wt["w_xo"]], [[(0, 0)]], t, d, tmm, tnx, [(h2, _tile(tmm, tnx))],
             res_epi, [(_out(t, d, F32), None)])[0]

    n2 = _rms_fwd("ffn2_norm", h3, sm["ffn2_norm"])
    h4, ffn2_saved = _ffn_fwd("ffn2", h3, n2, wt["ffn2_w_gate"], wt["ffn2_w_up"], wt["ffn2_w_down"])

    dh4, dh4_bf, gs["final_norm"], loss = _loss_head(h4, sm["final_norm"], tgt)
    dh3, dh3_bf, gs["ffn2_norm"], gb["ffn2_w_gate"], gb["ffn2_w_up"], gb["ffn2_w_down"] = _ffn_bwd(
        "ffn2", h3, sm["ffn2_norm"], wt["ffn2_w_gate"], wt["ffn2_w_up"], wt["ffn2_w_down"], ffn2_saved, dh4, dh4_bf)

    tw = _pick(d, 1024)
    do = _mm1("xattn_do", "nt", dh3_bf, wt["w_xo"], t, d, tmm, tnx, BF16)
    gb["w_xo"] = _mm1("xattn_dwxo", "tn", o, dh3_bf, d, d, tw, tnx, BF16)
    dq, dkv = _attn_bwd(q, kv, do)
    gb["w_q"] = _mm1("xattn_dwq", "tn", un, dq, d, d, tw, tnx, BF16)
    dun = _mm1("xattn_dun", "nt", dq, wt["w_q"], t, d, tmm, tnx, F32)
    dh2, dh2_bf, gs["xattn_norm"] = _rms_bwd("xattn_norm_bwd", h2, sm["xattn_norm"], dun, dh3)
    gb["w_kv"] = _mm1("xattn_dwkv", "tn", dkv, mn, 2 * d, d, _pick(2 * d, 512), d, BF16)
    dmn = _mm1("xattn_dmn", "nn", dkv, wt["w_kv"], n_mem, d, n_mem, tnx, F32)
    gs["mem_norm"] = _rms_bwd("mem_norm_bwd", mem, sm["mem_norm"], dmn)

    gb["w_mix_out"] = _mm1("mix_dwout", "tn", merged, dh2_bf, d, d, tw, tnx, BF16)

    def merge_bwd_epi(accs, gpv, gsv):
        dmerged, z_pool, val, gate = accs
        sp_, ss_, sg_ = jax.nn.sigmoid(gpv), jax.nn.sigmoid(gsv), jax.nn.sigmoid(gate)
        glu = val * sg_
        dz_pool = dmerged * sp_
        dg_pool = dmerged * z_pool * (sp_ * (1.0 - sp_))
        dz_ssm = dmerged * ss_
        dg_ssm = dmerged * glu * (ss_ * (1.0 - ss_))
        dval = dz_ssm * sg_
        dgate = dz_ssm * glu * (1.0 - sg_)
        return dz_pool, dg_pool, dg_ssm, dval, dgate

    dz_pool, dg_pool, dg_ssm, dval, dgate = _mm(
        "mix_merge_bwd", "nt", [dh2_bf, pm, ys], [wt["w_mix_out"], wt["w_pool_proj"], wt["w_glu_val"], wt["w_glu_gate"]],
        [[(0, 0)], [(1, 1)], [(2, 2)], [(2, 3)]], t, d, tmm, tnm, [(proj, gp_spec), (proj, gs_spec)], merge_bwd_epi,
        [(_out(t, d, BF16), None)] * 5)
    gb["w_pool_proj"] = _mm1("pool_dwproj", "tn", dz_pool, pm, d, d_pool, tw, d_pool, BF16)
    gb["w_glu_val"] = _mm1("glu_dwval", "tn", dval, ys, d, d_ssm, tw, d_ssm, BF16)
    gb["w_glu_gate"] = _mm1("glu_dwgate", "tn", dgate, ys, d, d_ssm, tw, d_ssm, BF16)

    def gelu_bwd_epi(accs, yv):
        _, vjp = jax.vjp(jax.nn.gelu, yv)
        return (vjp(accs[0])[0],)

    dy = _mm("glu_dy", "nn", [dval, dgate], [wt["w_glu_val"], wt["w_glu_gate"]], [[(0, 0), (1, 1)]], t, d_ssm, tmy, d_ssm,
             [(y, _tile(tmy, d_ssm))], gelu_bwd_epi, [(_out(t, d_ssm, F32), None)])[0]
    gs["ssm_d"] = _colsum_prod("ssm_dd", dy, proj, b_coff=off_s)
    dyp = _to_segments(dy)
    d_abr, d_abi, d_bbr, d_bbi, d_cre, d_cim, lams = [], [], [], [], [], [], []
    ts = _pick(n_state, 512)
    tc_ = _pick(n_state, 256)
    both = lambda accs: tuple(accs)
    for dr in range(2):
        lr, li, dar, dai = _ssm_bwd(f"ssm_bwd{dr}", dyp, c_re[dr], c_im[dr], xs[dr][0], xs[dr][1], abr2[dr:dr + 1],
                                    abi2[dr:dr + 1], reverse=(dr == 1))
        d_abr.append(dar)
        d_abi.append(dai)
        lams += [lr, li]
        d_br, d_bi = _mm(f"ssm_db{dr}", "tn", [sp32], [lr, li], [[(0, 0)], [(0, 1)]], d_ssm, n_state, d_ssm, ts, [], both,
                         [(_out(d_ssm, n_state, F32), None)] * 2)
        d_bbr.append(_diag_in(d_br, sg, sp, sh))
        d_bbi.append(_diag_in(d_bi, sg, sp, sh))
        d_cr, d_ci = _mm(f"ssm_dc{dr}", "tn", [xs[dr][0], xs[dr][1]], [dyp], [[(0, 0)], [(1, 0)]], n_state, d_ssm, tc_,
                         d_ssm, [], both, [(_out(n_state, d_ssm, F32), None)] * 2)
        d_cre.append(_diag_out(d_cr, sg, sp, sh))
        d_cim.append(-_diag_out(d_ci, sg, sp, sh))
    ds = _from_segments(_mm(
        "ssm_ds", "nt", lams, [b_re[0], b_im[0], b_re[1], b_im[1]], [[(k, k) for k in range(4)]], t, d_ssm, tmy,
        d_ssm, [(dyp, _tile(tmy, d_ssm)), (sm["ssm_d"], _rowvec(d_ssm))],
        lambda accs, dyv, dv: (dyv * dv + accs[0],), [(_out(t, d_ssm, BF16), None)])[0])
    cots = [jnp.concatenate(d_abr, axis=0).reshape(-1, 1), jnp.concatenate(d_abi, axis=0).reshape(-1, 1),
            jnp.concatenate(d_bbr, axis=0), jnp.concatenate(d_bbi, axis=0)]
    d_are, d_aim, d_ldt, d_bre, d_bim = _ssm_disc_bwd(cols, cots)
    gs["ssm_a_re"] = d_are.reshape(2, sg, sp)
    gs["ssm_a_im"] = d_aim.reshape(2, sg, sp)
    gs["ssm_log_dt"] = _rowsum("ssm_dlogdt", d_ldt.reshape(2 * sg, sp)).reshape(2, sg)
    gs["ssm_b_re"] = d_bre.reshape(2, sg, sp, sh)
    gs["ssm_b_im"] = d_bim.reshape(2, sg, sp, sh)
    gs["ssm_c_re"] = jnp.stack(d_cre, axis=0)
    gs["ssm_c_im"] = jnp.stack(d_cim, axis=0)

    dpm = _mm1("pool_dpm", "nn", dz_pool, wt["w_pool_proj"], t, d_pool, tmm, _pick(d_pool, 256), F32)
    dp, gs["pool_w"], gs["pool_scale"] = _pool_bwd(pooled, dpm, pool_w_bf, sm["pool_scale"])

    w_in = wt["w_in"]
    parts = [(dp, 0, d_pool), (ds, d_pool, d_ssm), (dg_pool, off_gp, d), (dg_ssm, off_gs, d)]
    w_in_parts = [w_in[o0:o0 + width] for _, o0, width in parts]
    gb["w_in"] = jnp.concatenate(
        [_mm1(f"in_proj_dw{k}", "tn", p_[0], u, p_[2], d, _pick(p_[2], 1024), tnx, BF16) for k, p_ in enumerate(parts)], axis=0)
    pin = emit("grads_main", gb=gb)
    du = _mm("in_proj_du", "nn", [p_[0] for p_ in parts], w_in_parts, [[(k, k) for k in range(4)]], t, d, tmm, tnx, [],
             lambda accs: (accs[0],), [(_out(t, d, F32), None)], after=pin)[0]
    dh1, dh1_bf, gs["mix_norm"] = _rms_bwd("mix_norm_bwd", h1, sm["mix_norm"], du, dh2)
    pin = emit("small_early", gs=gs, loss=loss)

    def ffn1_weights_done(d_wg, d_wu, d_wd):
        gb["ffn1_w_gate"], gb["ffn1_w_up"], gb["ffn1_w_down"] = d_wg, d_wu, d_wd
        return emit("grads_ffn1", gb=gb)

    dx, _, gs["ffn1_norm"], _, _, _ = _ffn_bwd(
        "ffn1", x, sm["ffn1_norm"], wt["ffn1_w_gate"], wt["ffn1_w_up"], wt["ffn1_w_down"], ffn1_saved, dh1, dh1_bf,
        weights_done=ffn1_weights_done, after=pin)
    return loss, dx, gb, gs


WEIGHTS = ["ffn1_norm", "ffn1_w_gate", "ffn1_w_up", "ffn1_w_down", "mix_norm", "w_in", "pool_w", "pool_scale",
           "w_pool_proj", "ssm_a_re", "ssm_a_im", "ssm_log_dt", "ssm_b_re", "ssm_b_im", "ssm_c_re", "ssm_c_im", "ssm_d",
           "w_glu_val", "w_glu_gate", "w_mix_out", "xattn_norm", "mem_norm", "w_q", "w_kv", "w_xo", "ffn2_norm",
           "ffn2_w_gate", "ffn2_w_up", "ffn2_w_down", "final_norm"]
COL_SHARDED = ["ffn1_w_gate", "ffn1_w_up", "w_in", "w_pool_proj", "w_glu_val", "w_glu_gate", "w_kv", "ffn2_w_gate",
               "ffn2_w_up"]
ROW_SHARDED = ["ffn1_w_down", "w_mix_out", "w_q", "w_xo", "ffn2_w_down"]
BIG = [n for n in WEIGHTS if n in COL_SHARDED or n in ROW_SHARDED]
SMALL = [n for n in WEIGHTS if n not in BIG]
FFN1_BIG = ["ffn1_w_gate", "ffn1_w_up", "ffn1_w_down"]
MAIN_BIG = [n for n in BIG if n not in FFN1_BIG]
LATE_SMALL = "ffn1_norm"
EARLY_SMALL = [n for n in SMALL if n != LATE_SMALL]
PACK_ROWS = SUBLANES * LANES
GRAD_ROW_TILE = 256


def _to_rows(name, w, width):
    if name in COL_SHARDED:
        w = w.T
    return w.reshape(-1, width)


def _from_rows(name, rows, shard_shape):
    if name in COL_SHARDED:
        return rows.reshape(shard_shape[1], shard_shape[0]).T
    return rows.reshape(shard_shape)


def _pack_small(vals):
    flat = []
    for v in vals:
        f = v.reshape(-1)
        flat.append(jnp.pad(f, (0, (-f.shape[0]) % PACK_ROWS)))
    total = sum(f.shape[0] for f in flat)
    flat.append(jnp.zeros(((-total) % (GRAD_ROW_TILE * LANES),), F32))
    return jnp.concatenate(flat).reshape(-1, LANES)


def _unpack_small(packed, shapes):
    out, row = [], 0
    for shp in shapes:
        size = math.prod(shp)
        rows = -(-size // PACK_ROWS) * SUBLANES
        out.append(packed[row:row + rows].reshape(-1)[:size].reshape(shp))
        row += rows
    return out


def kernel(x, mem, ffn1_norm, ffn1_w_gate, ffn1_w_up, ffn1_w_down, mix_norm, w_in, pool_w, pool_scale, w_pool_proj, ssm_a_re, ssm_a_im, ssm_log_dt, ssm_b_re, ssm_b_im, ssm_c_re, ssm_c_im, ssm_d, w_glu_val, w_glu_gate, w_mix_out, xattn_norm, mem_norm, w_q, w_kv, w_xo, ffn2_norm, ffn2_w_gate, ffn2_w_up, ffn2_w_down, final_norm, loss_target, m_ffn1_norm, m_ffn1_w_gate, m_ffn1_w_up, m_ffn1_w_down, m_mix_norm, m_w_in, m_pool_w, m_pool_scale, m_w_pool_proj, m_ssm_a_re, m_ssm_a_im, m_ssm_log_dt, m_ssm_b_re, m_ssm_b_im, m_ssm_c_re, m_ssm_c_im, m_ssm_d, m_w_glu_val, m_w_glu_gate, m_w_mix_out, m_xattn_norm, m_mem_norm, m_w_q, m_w_kv, m_w_xo, m_ffn2_norm, m_ffn2_w_gate, m_ffn2_w_up, m_ffn2_w_down, m_final_norm, v_ffn1_norm, v_ffn1_w_gate, v_ffn1_w_up, v_ffn1_w_down, v_mix_norm, v_w_in, v_pool_w, v_pool_scale, v_w_pool_proj, v_ssm_a_re, v_ssm_a_im, v_ssm_log_dt, v_ssm_b_re, v_ssm_b_im, v_ssm_c_re, v_ssm_c_im, v_ssm_d, v_w_glu_val, v_w_glu_gate, v_w_mix_out, v_xattn_norm, v_mem_norm, v_w_q, v_w_kv, v_w_xo, v_ffn2_norm, v_ffn2_w_gate, v_ffn2_w_up, v_ffn2_w_down, v_final_norm):
    given = dict(locals())
    wts = {n: given[n] for n in WEIGHTS}
    moms = {n: (given["m_" + n], given["v_" + n]) for n in WEIGHTS}
    x2, mem2, tgt2 = x[0], mem[0], loss_target[0]
    d = x2.shape[1]
    core = lax.axis_index("c").astype(jnp.int32).reshape(1)
    chip = (2 * lax.axis_index("x") + lax.axis_index("y")).astype(jnp.int32).reshape(1)

    def full_form(n, f):
        shard = wts[n][0].shape
        return f.reshape(N_DEV * shard[1], shard[0]) if n in COL_SHARDED else f.reshape(N_DEV * shard[0], shard[1])

    shards = {n: _to_rows(n, wts[n][0], d).astype(BF16) for n in BIG}
    first = FFN1_BIG[:2]
    wt = {n: full_form(n, f) for n, f in zip(first, _allgather("weight_allgather_first", [shards[n] for n in first]))}
    rest = [n for n in MAIN_BIG if n != "w_in"]
    gather_down = _gather_start("weight_gather_down_start", [shards[FFN1_BIG[2]]], wt[first[0]])
    gather_in = _gather_start("weight_gather_in_start", [shards["w_in"]], gather_down["token"])
    gather_rest = _gather_start("weight_gather_rest_start", [shards[n] for n in rest], gather_in["token"])
    sm = {n: (wts[n].reshape(1, -1) if wts[n].ndim <= 2 else wts[n][0]) for n in SMALL}
    sm["ffn1_norm"] = sm["ffn1_norm"] + gather_rest["token"][0, 0]

    pending = {}

    def reduce_start(tag, names, gb):
        blocks = [gb[n].reshape(N_DEV, -1, d) for n in names]
        pad_rows = (-sum(b.shape[1] for b in blocks)) % GRAD_ROW_TILE
        packed = jnp.concatenate(blocks + ([jnp.zeros((N_DEV, pad_rows, d), BF16)] if pad_rows else []), axis=1)
        pair = _pair_sum("grad_pair_sum_" + tag, packed, _exchange_cores("grad_exchange_cores_" + tag, packed), core)
        pending[tag] = (pair, _chips_start("grad_exchange_chips_start_" + tag, pair), [b.shape[1] for b in blocks])
        return pending[tag][1]["token"]

    def reduce_finish(tag, after):
        pair, started, rows = pending[tag]
        recv = _split_wait("grad_exchange_chips_wait_" + tag, started, after)[0]
        return _chip_sum("grad_chip_sum_" + tag, pair, recv, chip), rows

    def ev(name, gb=None, gs=None, loss=None, marker=None):
        if name == "ffn1_up_done":
            down = FFN1_BIG[2]
            wt[down] = full_form(down, _split_wait("weight_gather_down_wait", gather_down, marker)[0])
        elif name == "ffn1_fwd_done":
            wt["w_in"] = full_form("w_in", _split_wait("weight_gather_in_wait", gather_in, marker)[0])
        elif name == "mix_in_done":
            for n, f in zip(rest, _split_wait("weight_gather_rest_wait", gather_rest, marker)):
                wt[n] = full_form(n, f)
        elif name == "grads_main":
            return reduce_start("main", MAIN_BIG, gb)
        elif name == "small_early":
            pending["small"] = _slots_start("small_gather_start", _pack_small([gs[n] for n in EARLY_SMALL] + [loss[:, :1]]))
            return pending["small"]["token"]
        elif name == "grads_ffn1":
            return reduce_start("ffn1", FFN1_BIG, gb)
        return None

    _, dx, _, gs = _local_step(x2, mem2, tgt2, wt, sm, ev)

    out_g, out_d, out_m, out_v = {}, {}, {}, {}

    def update(n, g_full):
        shape = wts[n].shape
        two_d = (-1, shape[-1])
        dl, m2, v2 = _adamw("adamw_" + n, wts[n].reshape(two_d), g_full.reshape(two_d), moms[n][0].reshape(two_d),
                            moms[n][1].reshape(two_d))
        out_g[n], out_d[n], out_m[n], out_v[n] = g_full, dl.reshape(shape), m2.reshape(shape), v2.reshape(shape)
        return dl

    def update_big(names, g_rows, rows):
        off = 0
        for n, r in zip(names, rows):
            shard = wts[n].shape
            dl = update(n, _from_rows(n, g_rows[off:off + r], shard[1:]).reshape(shard))
            off += r
        return dl

    last = update_big(MAIN_BIG, *reduce_finish("main", dx))

    small_sum = _sum_slots("small_sum", _split_wait("small_gather_wait", pending["small"], dx)[0], F32)
    late = _allgather("small_allgather_late", [gs[LATE_SMALL].reshape(-1, LANES)])[0]
    late_sum = _sum_slots("small_sum_late", late.reshape(N_DEV, -1, LANES), F32)
    vals = _unpack_small(small_sum, [wts[n].shape for n in EARLY_SMALL] + [(1, 1)])
    total_loss = vals[-1].reshape(())
    for n, g_full in zip(EARLY_SMALL + [LATE_SMALL], vals[:-1] + [late_sum.reshape(wts[LATE_SMALL].shape)]):
        update(n, g_full)

    update_big(FFN1_BIG, *reduce_finish("ffn1", last))

    return (total_loss, dx[None], *[out_g[n] for n in WEIGHTS], *[out_d[n] for n in WEIGHTS],
            *[out_m[n] for n in WEIGHTS], *[out_v[n] for n in WEIGHTS])
```

```python
import functools
import math

import jax
import jax.numpy as jnp
from jax import lax
from jax.experimental import pallas as pl
from jax.experimental.pallas import tpu as pltpu

F32 = jnp.float32
BF16 = jnp.bfloat16
EPS = 1e-6
N_XHEADS = 4
POOL_WINDOWS = (2, 4, 8, 16)
ADAM_LR = 0.001
ADAM_B1 = 0.9
ADAM_B2 = 0.999
ADAM_EPS = 1e-08
ADAM_WD = 0.01
ADAM_STEP = 10
N_DEV = 8
VMEM_LIMIT_V7X = 48 * 1024 * 1024
LANES = 128
SUBLANES = 8
SUB_ROWS = 256
POOL_PAD = 16
MESH = pl.DeviceIdType.MESH
ANY = pl.BlockSpec(memory_space=pl.ANY)
HBM = pl.BlockSpec(memory_space=pltpu.HBM)
SEM = pl.BlockSpec(memory_space=pltpu.SEMAPHORE)
SIDE_EFFECT = pltpu.SideEffectType.DATAFLOW_SIDE_EFFECTING

_DIMS = {
    "nt": (((1,), (1,)), ((), ())),
    "nn": (((1,), (0,)), ((), ())),
    "tn": (((0,), (0,)), ((), ())),
}


def _pick(dim, pref, mult=LANES):
    if dim <= pref:
        return dim
    for t in range(pref - pref % mult, 0, -mult):
        if dim % t == 0:
            return t
    return dim


def _params(sem):
    return pltpu.CompilerParams(dimension_semantics=sem, vmem_limit_bytes=VMEM_LIMIT_V7X)


def _tile(tm, tn, coff=0):
    return pl.BlockSpec((tm, tn), lambda i, j: (i, j + coff))


def _rowvec(tn, coff=0):
    return pl.BlockSpec((1, tn), lambda i, j: (0, j + coff))


def _out(m, n, dtype):
    return jax.ShapeDtypeStruct((m, n), dtype)


def _mm(name, form, a_list, b_list, groups, m, n, tm, tn, extras, epilogue, outs, after=None, sub=SUB_ROWS):
    na, nb, ne = len(a_list), len(b_list), len(extras)
    pins = [] if after is None else [after]
    step = tm if (sub is None or form == "tn" or tm % sub) else sub

    def a_spec(a):
        if form == "tn":
            return pl.BlockSpec((a.shape[0], tm), lambda i, j: (0, i))
        return pl.BlockSpec((tm, a.shape[1]), lambda i, j: (i, 0))

    def b_spec(b):
        if form == "nt":
            return pl.BlockSpec((tn, b.shape[1]), lambda i, j: (j, 0))
        return pl.BlockSpec((b.shape[0], tn), lambda i, j: (0, j))

    def body(*refs):
        a_refs, b_refs = refs[:na], refs[na:na + nb]
        e_refs, o_refs = refs[na + nb:na + nb + ne], refs[na + nb + ne + len(pins):]
        b_vals = {}
        for s0 in range(0, tm, step):
            rows = slice(None) if step == tm else pl.ds(s0, step)
            a_vals, accs = {}, []
            for group in groups:
                acc = None
                for ai, bi in group:
                    if ai not in a_vals:
                        a_vals[ai] = (a_refs[ai][...] if form == "tn" else a_refs[ai][rows, :]).astype(BF16)
                    if bi not in b_vals:
                        b_vals[bi] = b_refs[bi][...].astype(BF16)
                    d = lax.dot_general(a_vals[ai], b_vals[bi], _DIMS[form], preferred_element_type=F32)
                    acc = d if acc is None else acc + d
                accs.append(acc)
            res = epilogue(accs, *[e[rows, :] if e.shape[0] == tm else e[...] for e in e_refs])
            for o_ref, r in zip(o_refs, res):
                o_ref[rows, :] = r.astype(o_ref.dtype)

    out_specs = [_tile(tm, tn) if s is None else s for _, s in outs]
    res = pl.pallas_call(
        body, name=name, grid=(m // tm, n // tn),
        in_specs=[a_spec(a) for a in a_list] + [b_spec(b) for b in b_list] + [s for _, s in extras] + [ANY] * len(pins),
        out_specs=out_specs, out_shape=[o for o, _ in outs],
        compiler_params=_params(("parallel", "parallel")),
    )(*a_list, *b_list, *[e for e, _ in extras], *pins)
    return res


def _mm1(name, form, a, b, m, n, tm, tn, dtype, scale=None):
    epi = (lambda accs: (accs[0],)) if scale is None else (lambda accs: (accs[0] * scale,))
    return _mm(name, form, [a], [b], [[(0, 0)]], m, n, tm, tn, [], epi, [(_out(m, n, dtype), None)])[0]


def _rms_fwd(name, h, g):
    t, d = h.shape
    tm = _pick(t, 512, SUBLANES)

    def body(h_ref, g_ref, n_ref):
        hv = h_ref[...]
        r = lax.rsqrt(jnp.mean(hv * hv, axis=-1, keepdims=True) + EPS)
        n_ref[...] = ((hv * r) * g_ref[...]).astype(BF16)

    return pl.pallas_call(
        body, name=name, grid=(t // tm,),
        in_specs=[pl.BlockSpec((tm, d), lambda i: (i, 0)), pl.BlockSpec((1, d), lambda i: (0, 0))],
        out_specs=pl.BlockSpec((tm, d), lambda i: (i, 0)), out_shape=_out(t, d, BF16),
        compiler_params=_params(("parallel",)),
    )(h, g)


def _rms_bwd(name, h, g, dn, dres=None):
    t, d = h.shape
    tm = _pick(t, 512, SUBLANES)
    need_dh = dres is not None

    def body(*refs):
        if need_dh:
            h_ref, g_ref, dn_ref, dres_ref, dh_ref, dhb_ref, dg_ref = refs
        else:
            h_ref, g_ref, dn_ref, dg_ref = refs
        hv = h_ref[...]
        r = lax.rsqrt(jnp.mean(hv * hv, axis=-1, keepdims=True) + EPS)
        nh = hv * r
        dnv = dn_ref[...].astype(F32)

        @pl.when(pl.program_id(0) == 0)
        def _():
            dg_ref[...] = jnp.zeros_like(dg_ref)

        dg_ref[...] += jnp.sum(dnv * nh, axis=0, keepdims=True)
        if need_dh:
            dng = dnv * g_ref[...]
            dh = dres_ref[...] + r * (dng - nh * jnp.mean(dng * nh, axis=-1, keepdims=True))
            dh_ref[...] = dh
            dhb_ref[...] = dh.astype(BF16)

    row = pl.BlockSpec((tm, d), lambda i: (i, 0))
    vec = pl.BlockSpec((1, d), lambda i: (0, 0))
    if need_dh:
        return pl.pallas_call(
            body, name=name, grid=(t // tm,), in_specs=[row, vec, row, row], out_specs=[row, row, vec],
            out_shape=[_out(t, d, F32), _out(t, d, BF16), _out(1, d, F32)], compiler_params=_params(("arbitrary",)),
        )(h, g, dn, dres)
    return pl.pallas_call(
        body, name=name, grid=(t // tm,), in_specs=[row, vec, row], out_specs=vec,
        out_shape=_out(1, d, F32), compiler_params=_params(("arbitrary",)),
    )(h, g, dn)


def _loss_head(h, g, tgt):
    t, d = h.shape
    tm = _pick(t, 512, SUBLANES)

    def body(h_ref, g_ref, t_ref, dh_ref, dhb_ref, dg_ref, loss_ref):
        hv = h_ref[...]
        r = lax.rsqrt(jnp.mean(hv * hv, axis=-1, keepdims=True) + EPS)
        nh = hv * r
        err = nh * g_ref[...] - t_ref[...]

        @pl.when(pl.program_id(0) == 0)
        def _():
            dg_ref[...] = jnp.zeros_like(dg_ref)
            loss_ref[...] = jnp.zeros_like(loss_ref)

        per_row = jnp.mean(err * err, axis=-1, keepdims=True)
        loss_ref[...] += 0.5 * jnp.sum(per_row, axis=0, keepdims=True)
        dy = err * (1.0 / d)
        dg_ref[...] += jnp.sum(dy * nh, axis=0, keepdims=True)
        dng = dy * g_ref[...]
        dh = r * (dng - nh * jnp.mean(dng * nh, axis=-1, keepdims=True))
        dh_ref[...] = dh
        dhb_ref[...] = dh.astype(BF16)

    row = pl.BlockSpec((tm, d), lambda i: (i, 0))
    vec = pl.BlockSpec((1, d), lambda i: (0, 0))
    return pl.pallas_call(
        body, name="loss_head", grid=(t // tm,), in_specs=[row, vec, row],
        out_specs=[row, row, vec, pl.BlockSpec((1, LANES), lambda i: (0, 0))],
        out_shape=[_out(t, d, F32), _out(t, d, BF16), _out(1, d, F32), _out(1, LANES, F32)],
        compiler_params=_params(("arbitrary",)),
    )(h, g, tgt)


def _ffn_fwd(tag, h, n, wg_t, wu_t, wd):
    t, d = h.shape
    f = wg_t.shape[0]
    tm, tn = _pick(t, 1024), _pick(f, 1408)

    def up_epi(accs):
        a, b = accs
        return a, b, (a * jax.nn.sigmoid(a)) * b

    a, b, hid = _mm(tag + "_up", "nt", [n], [wg_t, wu_t], [[(0, 0)], [(0, 1)]], t, f, tm, tn, [], up_epi,
                    [(_out(t, f, BF16), None)] * 3)
    if callable(wd):
        wd = wd(hid)
    tm2, tn2 = _pick(t, 1024), _pick(d, 512)
    h_out = _mm(tag + "_down", "nn", [hid], [wd], [[(0, 0)]], t, d, tm2, tn2, [(h, _tile(tm2, tn2))],
                lambda accs, hin: (hin + 0.5 * accs[0],), [(_out(t, d, F32), None)])[0]
    return h_out, (n, a, b, hid)


def _ffn_bwd(tag, h, g, wg_t, wu_t, wd, saved, dh, dh_bf, weights_done=None, after=None):
    n, a, b, hid = saved
    t, d = h.shape
    f = wd.shape[0]
    tm, tn = _pick(t, 1024), _pick(f, 1408)

    def hid_epi(accs, av, bv):
        dhid = 0.5 * accs[0]
        av, bv = av.astype(F32), bv.astype(F32)
        sig = jax.nn.sigmoid(av)
        da = dhid * bv * (sig * (1.0 + av * (1.0 - sig)))
        db = dhid * (av * sig)
        return da, db

    da, db = _mm(tag + "_bwd_hid", "nt", [dh_bf], [wd], [[(0, 0)]], t, f, tm, tn,
                 [(a, _tile(tm, tn)), (b, _tile(tm, tn))], hid_epi, [(_out(t, f, BF16), None)] * 2, after=after)
    tw, tnw = _pick(f, 1408), _pick(d, 512)
    d_wd = _mm1(tag + "_dwd", "tn", hid, dh_bf, f, d, tw, tnw, BF16, scale=0.5)
    d_wg = _mm1(tag + "_dwg", "tn", da, n, f, d, tw, tnw, BF16)
    d_wu = _mm1(tag + "_dwu", "tn", db, n, f, d, tw, tnw, BF16)
    pin = weights_done(d_wg, d_wu, d_wd) if weights_done is not None else None
    tm2, tn2 = _pick(t, 1024), _pick(d, 512)
    dn = _mm(tag + "_dn", "nn", [da, db], [wg_t, wu_t], [[(0, 0), (1, 1)]], t, d, tm2, tn2, [],
             lambda accs: (accs[0],), [(_out(t, d, F32), None)], after=pin)[0]
    dh_in, dh_in_bf, dg = _rms_bwd(tag + "_norm_bwd", h, g, dn, dh)
    return dh_in, dh_in_bf, dg, d_wg, d_wu, d_wd


def _window_sum(win, offsets):
    n = win.shape[0]
    acc = None
    for j in offsets:
        term = win if j == 0 else pltpu.roll(win, (-j) % n, 0)
        acc = term if acc is None else acc + term
    return acc


def _pool_counts(r0, ch, c, left, right, t):
    pos = r0 + lax.broadcasted_iota(jnp.int32, (ch, c), 0)
    return (jnp.minimum(pos + right + 1, t) - jnp.maximum(pos - left, 0)).astype(F32)


def _pool_fwd(proj, pool_w_bf, pool_scale):
    t = proj.shape[0]
    ng, c, _ = pool_w_bf.shape
    ch = _pick(t, 256, SUBLANES)
    pad = POOL_PAD

    def body(p_ref, w_ref, s_ref, pooled_ref, pm_ref, buf):
        grp = pl.program_id(0)
        buf[pl.ds(0, pad), :] = jnp.zeros((pad, c), F32)
        buf[pl.ds(pad + t, pad), :] = jnp.zeros((pad, c), F32)

        def fill(ci, carry):
            r0 = pl.multiple_of(ci * ch, SUBLANES)
            buf[pl.ds(pl.multiple_of(r0 + pad, SUBLANES), ch), :] = p_ref[pl.ds(r0, ch), :]
            return carry

        lax.fori_loop(0, t // ch, fill, 0)
        for gi, w in enumerate(POOL_WINDOWS):
            left = w // 2
            right = w - 1 - left

            @pl.when(grp == gi)
            def _(left=left, right=right):
                def chunk(ci, carry):
                    r0 = pl.multiple_of(ci * ch, SUBLANES)
                    win = buf[pl.ds(r0, ch + 2 * pad), :]
                    s = _window_sum(win, range(-left, right + 1))[pad:pad + ch]
                    pooled = s / _pool_counts(r0, ch, c, left, right, t) - win[pad:pad + ch]
                    pooled_bf = pooled.astype(BF16)
                    mixed = jnp.dot(pooled_bf, w_ref[0], preferred_element_type=F32)
                    pooled_ref[pl.ds(r0, ch), :] = pooled_bf
                    pm_ref[pl.ds(r0, ch), :] = (mixed * s_ref[...]).astype(BF16)
                    return carry

                lax.fori_loop(0, t // ch, chunk, 0)

    col = pl.BlockSpec((t, c), lambda g: (0, g))
    return pl.pallas_call(
        body, name="pool_fwd", grid=(ng,),
        in_specs=[col, pl.BlockSpec((1, c, c), lambda g: (g, 0, 0)), pl.BlockSpec((1, c), lambda g: (0, g))],
        out_specs=[col, col], out_shape=[_out(t, ng * c, BF16), _out(t, ng * c, BF16)],
        scratch_shapes=[pltpu.VMEM((t + 2 * pad, c), F32)],
        compiler_params=_params(("parallel",)),
    )(proj, pool_w_bf, pool_scale)


def _pool_bwd(pooled, dpm, pool_w_bf, pool_scale):
    t = pooled.shape[0]
    ng, c, _ = pool_w_bf.shape
    ch = _pick(t, 256, SUBLANES)
    pad = POOL_PAD

    def body(pooled_ref, dpm_ref, w_ref, s_ref, dp_ref, dw_ref, ds_ref, buf, raw):
        grp = pl.program_id(0)
        buf[pl.ds(0, pad), :] = jnp.zeros((pad, c), F32)
        buf[pl.ds(pad + t, pad), :] = jnp.zeros((pad, c), F32)
        dw_ref[...] = jnp.zeros_like(dw_ref)
        ds_ref[...] = jnp.zeros_like(ds_ref)
        for gi, w in enumerate(POOL_WINDOWS):
            left = w // 2
            right = w - 1 - left

            @pl.when(grp == gi)
            def _(left=left, right=right):
                def first(ci, carry):
                    r0 = pl.multiple_of(ci * ch, SUBLANES)
                    pv = pooled_ref[pl.ds(r0, ch), :]
                    dpm_v = dpm_ref[pl.ds(r0, ch), :]
                    mixed = jnp.dot(pv, w_ref[0], preferred_element_type=F32)
                    ds_ref[...] += jnp.sum(dpm_v * mixed, axis=0, keepdims=True)
                    dmixed = (dpm_v * s_ref[...]).astype(BF16)
                    dw_ref[0] += lax.dot_general(pv, dmixed, _DIMS["tn"], preferred_element_type=F32)
                    dpooled = lax.dot_general(dmixed, w_ref[0], _DIMS["nt"], preferred_element_type=F32)
                    raw[pl.ds(r0, ch), :] = dpooled
                    buf[pl.ds(pl.multiple_of(r0 + pad, SUBLANES), ch), :] = (
                        dpooled / _pool_counts(r0, ch, c, left, right, t))
                    return carry

                lax.fori_loop(0, t // ch, first, 0)

                def second(ci, carry):
                    r0 = pl.multiple_of(ci * ch, SUBLANES)
                    win = buf[pl.ds(r0, ch + 2 * pad), :]
                    s = _window_sum(win, range(-right, left + 1))[pad:pad + ch]
                    dp_ref[pl.ds(r0, ch), :] = (s - raw[pl.ds(r0, ch), :]).astype(BF16)
                    return carry

                lax.fori_loop(0, t // ch, second, 0)

    col = pl.BlockSpec((t, c), lambda g: (0, g))
    return pl.pallas_call(
        body, name="pool_bwd", grid=(ng,),
        in_specs=[col, col, pl.BlockSpec((1, c, c), lambda g: (g, 0, 0)), pl.BlockSpec((1, c), lambda g: (0, g))],
        out_specs=[col, pl.BlockSpec((1, c, c), lambda g: (g, 0, 0)), pl.BlockSpec((1, c), lambda g: (0, g))],
        out_shape=[_out(t, ng * c, BF16), jax.ShapeDtypeStruct((ng, c, c), F32), _out(1, ng * c, F32)],
        scratch_shapes=[pltpu.VMEM((t + 2 * pad, c), F32), pltpu.VMEM((t, c), F32)],
        compiler_params=_params(("parallel",)),
    )(pooled, dpm, pool_w_bf, pool_scale)


def _discretise(a_re, a_im, log_dt, b_re, b_im):
    dt = jnp.exp(log_dt)
    mag = jnp.exp(dt * a_re)
    ang = dt * a_im
    abr = mag * jnp.cos(ang)
    abi = mag * jnp.sin(ang)
    den = a_re * a_re + a_im * a_im
    nr = abr - 1.0
    qr = (nr * a_re + abi * a_im) / den
    qi = (abi * a_re - nr * a_im) / den
    return abr, abi, qr * b_re - qi * b_im, qr * b_im + qi * b_re


def _ssm_disc(cols):
    n, hh = cols[3].shape

    def body(ar, ai, ld, br, bi, o1, o2, o3, o4):
        res = _discretise(ar[...], ai[...], ld[...], br[...], bi[...])
        for o, r in zip((o1, o2, o3, o4), res):
            o[...] = r

    return pl.pallas_call(
        body, name="ssm_disc",
        out_shape=[_out(n, 1, F32), _out(n, 1, F32), _out(n, hh, F32), _out(n, hh, F32)],
    )(*cols)


def _ssm_disc_bwd(cols, cots):
    n, hh = cols[3].shape

    def body(ar, ai, ld, br, bi, c1, c2, c3, c4, o1, o2, o3, o4, o5):
        _, vjp = jax.vjp(_discretise, ar[...], ai[...], ld[...], br[...], bi[...])
        res = vjp((c1[...], c2[...], c3[...], c4[...]))
        for o, r in zip((o1, o2, o3, o4, o5), res):
            o[...] = r

    return pl.pallas_call(
        body, name="ssm_disc_bwd",
        out_shape=[_out(n, 1, F32)] * 3 + [_out(n, hh, F32)] * 2,
    )(*cols, *cots)


def _rowsum(name, a):
    r, _ = a.shape

    def body(a_ref, o_ref):
        o_ref[...] = jnp.sum(a_ref[...], axis=-1, keepdims=True)

    return pl.pallas_call(body, name=name, out_shape=_out(r, 1, F32))(a)


def _cmul(pr, pi, qr, qi):
    return pr * qr - pi * qi, pr * qi + pi * qr


def _cpow(pr, pi, n):
    rr, ri = None, None
    while n:
        if n & 1:
            rr, ri = (pr, pi) if rr is None else _cmul(rr, ri, pr, pi)
        n >>= 1
        if n:
            pr, pi = _cmul(pr, pi, pr, pi)
    return rr, ri


def _segment_carry(er, ei, pr, pi, reverse):
    row = lax.broadcasted_iota(jnp.int32, er.shape, 0)
    cr, ci = jnp.zeros_like(er), jnp.zeros_like(ei)
    for _ in range(SUBLANES - 1):
        tr = er + pr * cr - pi * ci
        ti = ei + pr * ci + pi * cr
        if reverse:
            keep, shift = row < SUBLANES - 1, SUBLANES - 1
        else:
            keep, shift = row >= 1, 1
        cr = jnp.where(keep, pltpu.roll(tr, shift, 0), 0.0)
        ci = jnp.where(keep, pltpu.roll(ti, shift, 0), 0.0)
    return cr, ci


def _ssm_fwd(name, sp, b_re, b_im, c_re, c_im, ar, ai, reverse):
    t, c = sp.shape
    s = ar.shape[1]
    w = _pick(s, 512)
    ch = _pick(t, 512, SUBLANES)
    n_ch, gpc, steps = t // ch, ch // SUBLANES, t // SUBLANES

    def body(sp_ref, bre_ref, bim_ref, cre_ref, cim_ref, ar_ref, ai_ref, xr_ref, xi_ref, y_ref, ur, ui, xbr, xbi):
        a_r = jnp.broadcast_to(ar_ref[...], (SUBLANES, w))
        a_i = jnp.broadcast_to(ai_ref[...], (SUBLANES, w))

        @pl.when(pl.program_id(0) == 0)
        def _():
            y_ref[...] = jnp.zeros_like(y_ref)

        def sweep(h0, store):
            def chunk(k, h):
                ci = n_ch - 1 - k if reverse else k
                rows = pl.ds(pl.multiple_of(ci * ch, ch), ch)
                spv = sp_ref[rows, :].astype(BF16)
                ur[...] = jnp.dot(spv, bre_ref[...], preferred_element_type=F32)
                ui[...] = jnp.dot(spv, bim_ref[...], preferred_element_type=F32)

                def group(g, hh):
                    gi = gpc - 1 - g if reverse else g
                    r0 = pl.multiple_of(gi * SUBLANES, SUBLANES)
                    hr, hi = hh
                    nr = a_r * hr - a_i * hi + ur[pl.ds(r0, SUBLANES), :]
                    ni = a_r * hi + a_i * hr + ui[pl.ds(r0, SUBLANES), :]
                    if store:
                        xbr[pl.ds(r0, SUBLANES), :] = nr
                        xbi[pl.ds(r0, SUBLANES), :] = ni
                    return nr, ni

                h = lax.fori_loop(0, gpc, group, h)
                if store:
                    xr16, xi16 = xbr[...].astype(BF16), xbi[...].astype(BF16)
                    xr_ref[rows, :] = xr16
                    xi_ref[rows, :] = xi16
                    y_ref[rows, :] += (jnp.dot(xr16, cre_ref[...], preferred_element_type=F32)
                                       + jnp.dot(xi16, cim_ref[...], preferred_element_type=F32))
                return h

            return lax.fori_loop(0, n_ch, chunk, h0)

        zero = jnp.zeros((SUBLANES, w), F32)
        er, ei = sweep((zero, zero), False)
        pr, pi = _cpow(ar_ref[...], ai_ref[...], steps)
        sweep(_segment_carry(er, ei, pr, pi, reverse), True)

    col = lambda i: (0, i)
    return pl.pallas_call(
        body, name=name, grid=(s // w,),
        in_specs=[pl.BlockSpec((t, c), lambda i: (0, 0)), pl.BlockSpec((c, w), col), pl.BlockSpec((c, w), col),
                  pl.BlockSpec((w, c), lambda i: (i, 0)), pl.BlockSpec((w, c), lambda i: (i, 0)),
                  pl.BlockSpec((1, w), col), pl.BlockSpec((1, w), col)],
        out_specs=[pl.BlockSpec((t, w), col), pl.BlockSpec((t, w), col), pl.BlockSpec((t, c), lambda i: (0, 0))],
        out_shape=[_out(t, s, BF16), _out(t, s, BF16), _out(t, c, F32)],
        scratch_shapes=[pltpu.VMEM((ch, w), F32)] * 4,
        compiler_params=_params(("arbitrary",)),
    )(sp, b_re, b_im, c_re, c_im, ar, ai)


def _ssm_bwd(name, dyp, c_re, c_im, xr, xi, ar, ai, reverse):
    t, c = dyp.shape
    s = ar.shape[1]
    w = _pick(s, 512)
    ch = _pick(t, 512, SUBLANES)
    n_ch, gpc, steps = t // ch, ch // SUBLANES, t // SUBLANES
    back = not reverse
    edge = 2 * SUBLANES

    def body(dy_ref, cre_ref, cim_ref, xr_ref, xi_ref, ar_ref, ai_ref, lr_ref, li_ref, dar_ref, dai_ref,
             gr, gi_, lbr, lbi, xbr, xbi):
        a_r = jnp.broadcast_to(ar_ref[...], (SUBLANES, w))
        a_i = -jnp.broadcast_to(ai_ref[...], (SUBLANES, w))
        row = lax.broadcasted_iota(jnp.int32, (SUBLANES, w), 0)

        def neighbours(ci, x_ref, buf):
            rows = pl.ds(pl.multiple_of(ci * ch, ch), ch)
            if reverse:
                buf[pl.ds(0, ch), :] = x_ref[rows, :].astype(F32)
                nxt = x_ref[pl.ds(pl.multiple_of(jnp.minimum(ci + 1, n_ch - 1) * ch, ch), edge), :].astype(F32)[:SUBLANES]
                first = x_ref[pl.ds(0, edge), :].astype(F32)[:SUBLANES]
                wrap = jnp.where(row < SUBLANES - 1, pltpu.roll(first, SUBLANES - 1, 0), 0.0)
                buf[pl.ds(ch, SUBLANES), :] = jnp.where(ci == n_ch - 1, wrap, nxt)
            else:
                buf[pl.ds(SUBLANES, ch), :] = x_ref[rows, :].astype(F32)
                prv = x_ref[pl.ds(pl.multiple_of(jnp.maximum(ci * ch - edge, 0), edge), edge), :].astype(F32)[SUBLANES:]
                last = x_ref[pl.ds(t - edge, edge), :].astype(F32)[SUBLANES:]
                wrap = jnp.where(row >= 1, pltpu.roll(last, 1, 0), 0.0)
                buf[pl.ds(0, SUBLANES), :] = jnp.where(ci == 0, wrap, prv)

        def sweep(h0, store):
            def chunk(k, carry):
                ci = n_ch - 1 - k if back else k
                rows = pl.ds(pl.multiple_of(ci * ch, ch), ch)
                dyv = dy_ref[rows, :].astype(BF16)
                gr[...] = lax.dot_general(dyv, cre_ref[...], _DIMS["nt"], preferred_element_type=F32)
                gi_[...] = lax.dot_general(dyv, cim_ref[...], _DIMS["nt"], preferred_element_type=F32)
                if store:
                    neighbours(ci, xr_ref, xbr)
                    neighbours(ci, xi_ref, xbi)

                def group(g, cc):
                    gidx = gpc - 1 - g if back else g
                    r0 = pl.multiple_of(gidx * SUBLANES, SUBLANES)
                    hr, hi = cc[0], cc[1]
                    nr = a_r * hr - a_i * hi + gr[pl.ds(r0, SUBLANES), :]
                    ni = a_r * hi + a_i * hr + gi_[pl.ds(r0, SUBLANES), :]
                    if not store:
                        return nr, ni
                    lbr[pl.ds(r0, SUBLANES), :] = nr
                    lbi[pl.ds(r0, SUBLANES), :] = ni
                    x0 = pl.multiple_of(r0 + SUBLANES, SUBLANES) if reverse else r0
                    xpr, xpi = xbr[pl.ds(x0, SUBLANES), :], xbi[pl.ds(x0, SUBLANES), :]
                    return nr, ni, cc[2] + nr * xpr + ni * xpi, cc[3] + ni * xpr - nr * xpi

                carry = lax.fori_loop(0, gpc, group, carry)
                if store:
                    lr_ref[rows, :] = lbr[...].astype(BF16)
                    li_ref[rows, :] = lbi[...].astype(BF16)
                return carry

            return lax.fori_loop(0, n_ch, chunk, h0)

        zero = jnp.zeros((SUBLANES, w), F32)
        er, ei = sweep((zero, zero), False)
        pr, pi = _cpow(ar_ref[...], -ai_ref[...], steps)
        cr, ci0 = _segment_carry(er, ei, pr, pi, back)
        _, _, dar, dai = sweep((cr, ci0, zero, zero), True)
        dar_ref[...] = jnp.sum(dar, axis=0, keepdims=True)
        dai_ref[...] = jnp.sum(dai, axis=0, keepdims=True)

    col = lambda i: (0, i)
    return pl.pallas_call(
        body, name=name, grid=(s // w,),
        in_specs=[pl.BlockSpec((t, c), lambda i: (0, 0)), pl.BlockSpec((w, c), lambda i: (i, 0)),
                  pl.BlockSpec((w, c), lambda i: (i, 0)), pl.BlockSpec((t, w), col), pl.BlockSpec((t, w), col),
                  pl.BlockSpec((1, w), col), pl.BlockSpec((1, w), col)],
        out_specs=[pl.BlockSpec((t, w), col), pl.BlockSpec((t, w), col), pl.BlockSpec((1, w), col), pl.BlockSpec((1, w), col)],
        out_shape=[_out(t, s, BF16), _out(t, s, BF16), _out(1, s, F32), _out(1, s, F32)],
        scratch_shapes=[pltpu.VMEM((ch, w), F32)] * 4 + [pltpu.VMEM((ch + SUBLANES, w), F32)] * 2,
        compiler_params=_params(("parallel",)),
    )(dyp, c_re, c_im, xr, xi, ar, ai)


def _to_segments(a):
    t, c = a.shape
    return a.reshape(SUBLANES, t // SUBLANES, c).transpose(1, 0, 2).reshape(t, c)


def _from_segments(a):
    t, c = a.shape
    return a.reshape(t // SUBLANES, SUBLANES, c).transpose(1, 0, 2).reshape(t, c)


def _colsum_prod(name, a, b, b_coff=0):
    t, n = a.shape
    tm = _pick(t, 512, SUBLANES)

    def body(a_ref, b_ref, o_ref):
        @pl.when(pl.program_id(0) == 0)
        def _():
            o_ref[...] = jnp.zeros_like(o_ref)

        o_ref[...] += jnp.sum(a_ref[...].astype(F32) * b_ref[...].astype(F32), axis=0, keepdims=True)

    return pl.pallas_call(
        body, name=name, grid=(t // tm,),
        in_specs=[pl.BlockSpec((tm, n), lambda i: (i, 0)), pl.BlockSpec((tm, n), lambda i: (i, b_coff))],
        out_specs=pl.BlockSpec((1, n), lambda i: (0, 0)), out_shape=_out(1, n, F32),
        compiler_params=_params(("arbitrary",)),
    )(a, b)


def _bd_in(bb, g, p, hh):
    blk = bb.reshape(g, p, hh).transpose(0, 2, 1)
    eye = jnp.eye(g, dtype=bool)[:, None, :, None]
    return jnp.where(eye, blk[:, :, None, :], 0.0).reshape(g * hh, g * p)


def _bd_out(cc, g, p, hh):
    blk = cc.transpose(0, 2, 1)
    eye = jnp.eye(g, dtype=bool)[:, None, :, None]
    return jnp.where(eye, blk[:, :, None, :], 0.0).reshape(g * p, g * hh)


def _diag_in(dmat, g, p, hh):
    eye = jnp.eye(g, dtype=bool)[:, None, :, None]
    diag = jnp.sum(jnp.where(eye, dmat.reshape(g, hh, g, p), 0.0), axis=2)
    return diag.transpose(0, 2, 1).reshape(g * p, hh)


def _diag_out(dmat, g, p, hh):
    eye = jnp.eye(g, dtype=bool)[:, None, :, None]
    diag = jnp.sum(jnp.where(eye, dmat.reshape(g, p, g, hh), 0.0), axis=2)
    return diag.transpose(0, 2, 1)


def _softmax(qh, kh, scale):
    s = lax.dot_general(qh, kh, _DIMS["nt"], preferred_element_type=F32) * scale
    e = jnp.exp(s - jnp.max(s, axis=-1, keepdims=True))
    return e / jnp.sum(e, axis=-1, keepdims=True)


def _attn_fwd(q, kv):
    t, d = q.shape
    mm_ = kv.shape[0]
    hd = d // N_XHEADS
    scale = 1.0 / math.sqrt(hd)
    tm = _pick(t, 512, SUBLANES)

    def body(q_ref, kv_ref, o_ref):
        for h in range(N_XHEADS):
            sl = pl.ds(h * hd, hd)
            p = _softmax(q_ref[:, sl], kv_ref[:, sl], scale)
            o_ref[:, sl] = jnp.dot(p.astype(BF16), kv_ref[:, pl.ds(d + h * hd, hd)],
                                   preferred_element_type=F32).astype(BF16)

    return pl.pallas_call(
        body, name="attn_fwd", grid=(t // tm,),
        in_specs=[pl.BlockSpec((tm, d), lambda i: (i, 0)), pl.BlockSpec((mm_, 2 * d), lambda i: (0, 0))],
        out_specs=pl.BlockSpec((tm, d), lambda i: (i, 0)), out_shape=_out(t, d, BF16),
        compiler_params=_params(("parallel",)),
    )(q, kv)


def _attn_bwd(q, kv, do):
    t, d = q.shape
    mm_ = kv.shape[0]
    hd = d // N_XHEADS
    scale = 1.0 / math.sqrt(hd)
    tm = _pick(t, 512, SUBLANES)

    def body(q_ref, kv_ref, do_ref, dq_ref, dkv_ref):
        @pl.when(pl.program_id(0) == 0)
        def _():
            dkv_ref[...] = jnp.zeros_like(dkv_ref)

        for h in range(N_XHEADS):
            sl = pl.ds(h * hd, hd)
            vsl = pl.ds(d + h * hd, hd)
            qh, kh, doh = q_ref[:, sl], kv_ref[:, sl], do_ref[:, sl]
            p = _softmax(qh, kh, scale)
            dp = lax.dot_general(doh, kv_ref[:, vsl], _DIMS["nt"], preferred_element_type=F32)
            dkv_ref[:, vsl] += lax.dot_general(p.astype(BF16), doh, _DIMS["tn"], preferred_element_type=F32)
            ds = (p * (dp - jnp.sum(dp * p, axis=-1, keepdims=True)) * scale).astype(BF16)
            dq_ref[:, sl] = jnp.dot(ds, kh, preferred_element_type=F32).astype(BF16)
            dkv_ref[:, sl] += lax.dot_general(ds, qh, _DIMS["tn"], preferred_element_type=F32)

    row = pl.BlockSpec((tm, d), lambda i: (i, 0))
    full = pl.BlockSpec((mm_, 2 * d), lambda i: (0, 0))
    return pl.pallas_call(
        body, name="attn_bwd", grid=(t // tm,), in_specs=[row, full, row], out_specs=[row, full],
        out_shape=[_out(t, d, BF16), _out(mm_, 2 * d, F32)], compiler_params=_params(("arbitrary",)),
    )(q, kv, do)


def _ew(name, fn, ins, outs, rows_pref=256, rowvecs=()):
    r, c = ins[0].shape
    tr = _pick(r, rows_pref, SUBLANES)
    ni = len(ins) + len(rowvecs)

    def body(*refs):
        res = fn(*[x[...] for x in refs[:ni]])
        for o_ref, v in zip(refs[ni:], res):
            o_ref[...] = v.astype(o_ref.dtype)

    blk = pl.BlockSpec((tr, c), lambda i: (i, 0))
    vec = pl.BlockSpec((1, c), lambda i: (0, 0))
    return pl.pallas_call(
        body, name=name, grid=(r // tr,), in_specs=[blk] * len(ins) + [vec] * len(rowvecs), out_specs=[blk] * len(outs),
        out_shape=[_out(r, c, dt) for dt in outs], compiler_params=_params(("parallel",)),
    )(*ins, *rowvecs)


def _sum_slots(name, a, dtype):
    s, r, c = a.shape
    tr = _pick(r, 256, SUBLANES)

    def body(a_ref, o_ref):
        acc = a_ref[0].astype(F32)
        for k in range(1, s):
            acc = acc + a_ref[k].astype(F32)
        o_ref[...] = acc.astype(o_ref.dtype)

    return pl.pallas_call(
        body, name=name, grid=(r // tr,), in_specs=[pl.BlockSpec((s, tr, c), lambda i: (0, i, 0))],
        out_specs=pl.BlockSpec((tr, c), lambda i: (i, 0)), out_shape=_out(r, c, dtype),
        compiler_params=_params(("parallel",)),
    )(a)


def _adamw(name, w, g, m, v):
    bc1 = 1.0 - ADAM_B1 ** ADAM_STEP
    bc2 = 1.0 - ADAM_B2 ** ADAM_STEP

    def fn(wv, gv, mv, vv):
        m2 = ADAM_B1 * mv + (1.0 - ADAM_B1) * gv
        v2 = ADAM_B2 * vv + (1.0 - ADAM_B2) * (gv * gv)
        delta = -ADAM_LR * ((m2 / bc1) / (jnp.sqrt(v2 / bc2) + ADAM_EPS) + ADAM_WD * wv)
        return delta, m2, v2

    return _ew(name, fn, [w, g, m, v], [F32, F32, F32])


def _allgather(name, arrs):
    n = len(arrs)

    def body(*refs):
        ins, outs = refs[:n], refs[n:2 * n]
        send_sems, recv_sems, local_sems = refs[2 * n:]
        x, y, c = lax.axis_index("x"), lax.axis_index("y"), lax.axis_index("c")
        me, sibling = (x, y, c), (x, y, 1 - c)
        chips = [(1 - x, y), (x, 1 - y), (1 - x, 1 - y)]

        def rows(a, px, py, pc):
            r = ins[a].shape[0]
            return outs[a].at[pl.ds((4 * px + 2 * py + pc) * r, r), :]

        def copy(a, k, block, to, src=None):
            return pltpu.make_async_remote_copy(
                src_ref=rows(a, *block) if src is None else src, dst_ref=rows(a, *block),
                send_sem=send_sems.at[a, k], recv_sem=recv_sems.at[a, k], device_id=to, device_id_type=MESH)

        mine = [pltpu.make_async_copy(ins[a], rows(a, *me), local_sems.at[a]) for a in range(n)]
        for cp in mine:
            cp.start()
        first = []
        for a in range(n):
            first.append(copy(a, 0, me, sibling, src=ins[a]))
            first += [copy(a, 1 + j, me, (*chip, c), src=ins[a]) for j, chip in enumerate(chips)]
        for cp in first:
            cp.start()
        passed = []
        for j, chip in enumerate(chips):
            for a in range(n):
                copy(a, 1 + j, (*chip, c), me).wait_recv()
                cp = copy(a, 4 + j, (*chip, c), sibling)
                cp.start()
                passed.append(cp)
        for a in range(n):
            copy(a, 0, sibling, me).wait_recv()
            for j, chip in enumerate(chips):
                copy(a, 4 + j, (*chip, 1 - c), me).wait_recv()
        for cp in first + passed:
            cp.wait_send()
        for cp in mine:
            cp.wait()

    return pl.pallas_call(
        body, name=name, in_specs=[ANY] * n, out_specs=[ANY] * n,
        out_shape=[_out(N_DEV * a.shape[0], a.shape[1], a.dtype) for a in arrs],
        scratch_shapes=[pltpu.SemaphoreType.DMA((n, 7)), pltpu.SemaphoreType.DMA((n, 7)), pltpu.SemaphoreType.DMA((n,))],
    )(*arrs)


def _exchange_cores(name, g):
    _, r, c = g.shape
    nck = r // GRAD_ROW_TILE

    def body(g_ref, recv_ref, send_sems, recv_sems):
        x, y, cc = lax.axis_index("x"), lax.axis_index("y"), lax.axis_index("c")
        copies = []
        for q in range(4):
            for k in range(nck):
                rows = pl.ds(k * GRAD_ROW_TILE, GRAD_ROW_TILE)
                copies.append(pltpu.make_async_remote_copy(
                    src_ref=g_ref.at[2 * q + (1 - cc), rows], dst_ref=recv_ref.at[q, rows],
                    send_sem=send_sems.at[q, k], recv_sem=recv_sems.at[q, k], device_id=(x, y, 1 - cc),
                    device_id_type=MESH))
        for cp in copies:
            cp.start()
        for cp in copies:
            cp.wait()

    return pl.pallas_call(
        body, name=name, in_specs=[ANY], out_specs=ANY,
        out_shape=jax.ShapeDtypeStruct((4, r, c), g.dtype),
        scratch_shapes=[pltpu.SemaphoreType.DMA((4, nck)), pltpu.SemaphoreType.DMA((4, nck))],
    )(g)


def _pair_sum(name, g, recv, core):
    _, r, c = g.shape
    tr = _pick(r, 5 * GRAD_ROW_TILE, GRAD_ROW_TILE)

    def body(core_ref, g_ref, r_ref, o_ref):
        o_ref[...] = (g_ref[...].astype(F32) + r_ref[...].astype(F32)).astype(o_ref.dtype)

    blk = pl.BlockSpec((None, tr, c), lambda q, i, core_ref: (q, i, 0))
    return pl.pallas_call(
        body, name=name,
        grid_spec=pltpu.PrefetchScalarGridSpec(
            num_scalar_prefetch=1, grid=(4, r // tr),
            in_specs=[pl.BlockSpec((None, tr, c), lambda q, i, core_ref: (2 * q + core_ref[0], i, 0)), blk],
            out_specs=blk),
        out_shape=jax.ShapeDtypeStruct((4, r, c), g.dtype), compiler_params=_params(("parallel", "parallel")),
    )(core, g, recv)


def _peer(k, x, y, c):
    return (1 - x if k & 4 else x, 1 - y if k & 2 else y, 1 - c if k & 1 else c)


def _split_start(name, srcs, land_shapes, n_remote, n_local, build, after=None):
    ns, nl = len(srcs), len(land_shapes)
    n_sem = 3 if n_local else 2
    pins = [] if after is None else [after]

    def body(*refs):
        src_refs, land_refs = refs[:ns], refs[ns:ns + nl]
        sems = refs[ns + nl + len(pins):ns + nl + len(pins) + n_sem]
        token = refs[-1]
        remote, local = build(src_refs, land_refs, *sems)
        for cp in local + remote:
            cp.start()
        token[...] = jnp.zeros_like(token)

    sem_shapes = [pltpu.SemaphoreType.DMA((n_remote,)), pltpu.SemaphoreType.DMA((n_remote,))]
    if n_local:
        sem_shapes.append(pltpu.SemaphoreType.DMA((n_local,)))
    bufs = [pltpu.with_memory_space_constraint(a, pltpu.HBM) for a in srcs]
    bufs += [pltpu.with_memory_space_constraint(lax.empty(s.shape, s.dtype), pltpu.HBM) for s in land_shapes]
    outs = pl.pallas_call(
        body, name=name,
        out_shape=sem_shapes + [pltpu.HBM(b.shape, b.dtype) for b in bufs] + [jax.ShapeDtypeStruct((SUBLANES, LANES), F32)],
        in_specs=[HBM] * (ns + nl) + [ANY] * len(pins),
        out_specs=[SEM] * n_sem + [HBM] * (ns + nl) + [pl.BlockSpec(memory_space=pltpu.VMEM)],
        input_output_aliases={i: n_sem + i for i in range(ns + nl)},
        compiler_params=pltpu.CompilerParams(has_side_effects=SIDE_EFFECT),
    )(*bufs, *pins)
    return dict(sems=list(outs[:n_sem]), bufs=list(outs[n_sem:n_sem + ns + nl]), token=outs[-1], build=build, ns=ns)


def _split_wait(name, started, after):
    ns, n_buf, n_sem = started["ns"], len(started["bufs"]), len(started["sems"])

    def body(*refs):
        src_refs, land_refs = refs[:ns], refs[ns:n_buf]
        sems = refs[n_buf:n_buf + n_sem]
        remote, local = started["build"](src_refs, land_refs, *sems)
        for cp in local:
            cp.wait()
        for cp in remote:
            cp.wait_send()
            cp.wait_recv()

    outs = pl.pallas_call(
        body, name=name, out_shape=[pltpu.HBM(b.shape, b.dtype) for b in started["bufs"]],
        in_specs=[HBM] * n_buf + [SEM] * n_sem + [ANY], out_specs=[HBM] * n_buf,
        input_output_aliases={i: i for i in range(n_buf)},
        compiler_params=pltpu.CompilerParams(has_side_effects=SIDE_EFFECT),
    )(*started["bufs"], *started["sems"], after)
    return list(outs[ns:])


def _gather_start(name, shards, after):
    m = len(shards)

    def build(src_refs, land_refs, send_sems, recv_sems, local_sems):
        x, y, c = lax.axis_index("x"), lax.axis_index("y"), lax.axis_index("c")
        remote, local = [], []
        for j in range(m):
            r = src_refs[j].shape[0]
            dst = land_refs[j].at[pl.ds((4 * x + 2 * y + c) * r, r), :]
            local.append(pltpu.make_async_copy(src_refs[j], dst, local_sems.at[j]))
            for k in range(1, N_DEV):
                remote.append(pltpu.make_async_remote_copy(
                    src_ref=src_refs[j], dst_ref=dst, send_sem=send_sems.at[7 * j + k - 1],
                    recv_sem=recv_sems.at[7 * j + k - 1], device_id=_peer(k, x, y, c), device_id_type=MESH))
        return remote, local

    lands = [jax.ShapeDtypeStruct((N_DEV * a.shape[0], a.shape[1]), a.dtype) for a in shards]
    return _split_start(name, shards, lands, 7 * m, m, build, after)


def _slots_start(name, a):
    def build(src_refs, land_refs, send_sems, recv_sems, local_sems):
        x, y, c = lax.axis_index("x"), lax.axis_index("y"), lax.axis_index("c")
        dst = land_refs[0].at[4 * x + 2 * y + c]
        local = [pltpu.make_async_copy(src_refs[0], dst, local_sems.at[0])]
        remote = [pltpu.make_async_remote_copy(
            src_ref=src_refs[0], dst_ref=dst, send_sem=send_sems.at[k - 1], recv_sem=recv_sems.at[k - 1],
            device_id=_peer(k, x, y, c), device_id_type=MESH) for k in range(1, N_DEV)]
        return remote, local

    return _split_start(name, [a], [jax.ShapeDtypeStruct((N_DEV,) + a.shape, a.dtype)], 7, 1, build)


def _chips_start(name, p):
    _, r, c = p.shape
    nck = r // GRAD_ROW_TILE

    def build(src_refs, land_refs, send_sems, recv_sems):
        x, y, cc = lax.axis_index("x"), lax.axis_index("y"), lax.axis_index("c")
        remote = []
        for k in range(1, 4):
            px = 1 - x if k >> 1 else x
            py = 1 - y if k & 1 else y
            for j in range(nck):
                rows = pl.ds(j * GRAD_ROW_TILE, GRAD_ROW_TILE)
                remote.append(pltpu.make_async_remote_copy(
                    src_ref=src_refs[0].at[2 * px + py, rows], dst_ref=land_refs[0].at[k - 1, rows],
                    send_sem=send_sems.at[(k - 1) * nck + j], recv_sem=recv_sems.at[(k - 1) * nck + j],
                    device_id=(px, py, cc), device_id_type=MESH))
        return remote, []

    return _split_start(name, [p], [jax.ShapeDtypeStruct((3, r, c), p.dtype)], 3 * nck, 0, build)


def _chip_sum(name, p, recv, chip):
    _, r, c = p.shape
    tr = _pick(r, 5 * GRAD_ROW_TILE, GRAD_ROW_TILE)

    def body(chip_ref, p_ref, r_ref, o_ref):
        acc = p_ref[...].astype(F32)
        for k in range(3):
            acc = acc + r_ref[k].astype(F32)
        o_ref[...] = acc

    return pl.pallas_call(
        body, name=name,
        grid_spec=pltpu.PrefetchScalarGridSpec(
            num_scalar_prefetch=1, grid=(r // tr,),
            in_specs=[pl.BlockSpec((None, tr, c), lambda i, chip_ref: (chip_ref[0], i, 0)),
                      pl.BlockSpec((3, tr, c), lambda i, chip_ref: (0, i, 0))],
            out_specs=pl.BlockSpec((tr, c), lambda i, chip_ref: (i, 0))),
        out_shape=_out(r, c, F32), compiler_params=_params(("parallel",)),
    )(chip, p, recv)


def _local_step(x, mem, tgt, wt, sm, ev=None):
    t, d = x.shape
    n_mem = mem.shape[0]
    d_pool = sm["pool_scale"].shape[1]
    ng, pc = sm["pool_w"].shape[0], sm["pool_w"].shape[1]
    d_ssm = sm["ssm_d"].shape[1]
    _, sg, sp, sh = sm["ssm_b_re"].shape
    n_state = sg * sp
    gb, gs = {}, {}

    def emit(name, **kw):
        return ev(name, **kw) if ev is not None else None

    n1 = _rms_fwd("ffn1_norm", x, sm["ffn1_norm"])
    emit("ffn1_norm_done", marker=n1)
    def ffn1_down(hid):
        emit("ffn1_up_done", marker=hid)
        return wt["ffn1_w_down"]

    h1, ffn1_saved = _ffn_fwd("ffn1", x, n1, wt["ffn1_w_gate"], wt["ffn1_w_up"], ffn1_down)
    emit("ffn1_fwd_done", marker=h1)
    u = _rms_fwd("mix_norm", h1, sm["mix_norm"])
    d_in = wt["w_in"].shape[0]
    tm, tn = _pick(t, 1024), _pick(d_in, 1408)
    proj = _mm1("in_proj", "nt", u, wt["w_in"], t, d_in, tm, tn, F32)
    off_s = d_pool // d_ssm
    off_gp = (d_pool + d_ssm)
    off_gs = off_gp + d

    pool_w_bf = sm["pool_w"].astype(BF16)
    pooled, pm = _pool_fwd(proj, pool_w_bf, sm["pool_scale"])

    cols = [sm["ssm_a_re"].reshape(-1, 1), sm["ssm_a_im"].reshape(-1, 1),
            jnp.broadcast_to(sm["ssm_log_dt"][:, :, None], (2, sg, sp)).reshape(-1, 1),
            sm["ssm_b_re"].reshape(-1, sh), sm["ssm_b_im"].reshape(-1, sh)]
    abr, abi, bbr, bbi = _ssm_disc(cols)
    abr2, abi2 = abr.reshape(2, n_state), abi.reshape(2, n_state)
    bbr4, bbi4 = bbr.reshape(2, sg * sp, sh), bbi.reshape(2, sg * sp, sh)
    b_re = [_bd_in(bbr4[dr], sg, sp, sh).astype(BF16) for dr in range(2)]
    b_im = [_bd_in(bbi4[dr], sg, sp, sh).astype(BF16) for dr in range(2)]
    c_re = [_bd_out(sm["ssm_c_re"][dr], sg, sp, sh).astype(BF16) for dr in range(2)]
    c_im = [_bd_out(-sm["ssm_c_im"][dr], sg, sp, sh).astype(BF16) for dr in range(2)]
    sp32 = _to_segments(proj[:, d_pool:d_pool + d_ssm])
    xs, y_parts = [], []
    for dr in range(2):
        xr, xi, y_part = _ssm_fwd(f"ssm_fwd{dr}", sp32, b_re[dr], b_im[dr], c_re[dr], c_im[dr], abr2[dr:dr + 1],
                                  abi2[dr:dr + 1], reverse=(dr == 1))
        xs.append((xr, xi))
        y_parts.append(y_part)
    y = _from_segments(_ew("ssm_sum", lambda p0, p1, sv, dv: (p0 + p1 + sv * dv,), y_parts + [sp32], [F32],
                           rowvecs=[sm["ssm_d"]])[0])
    tmy = _pick(t, 256)
    ys = _ew("ssm_gelu", lambda v: (jax.nn.gelu(v),), [y], [BF16])[0]
    emit("mix_in_done", marker=ys)

    tmm, tnm, tnx = _pick(t, 1024), _pick(d, 256), _pick(d, 512)
    gp_spec = _tile(tmm, tnm, off_gp // tnm)
    gs_spec = _tile(tmm, tnm, off_gs // tnm)

    def merge_epi(accs, gpv, gsv):
        z_pool, val, gate = accs
        return (jax.nn.sigmoid(gpv) * z_pool + jax.nn.sigmoid(gsv) * (val * jax.nn.sigmoid(gate)),)

    merged = _mm("mix_merge", "nt", [pm, ys], [wt["w_pool_proj"], wt["w_glu_val"], wt["w_glu_gate"]],
                 [[(0, 0)], [(1, 1)], [(1, 2)]], t, d, tmm, tnm, [(proj, gp_spec), (proj, gs_spec)], merge_epi,
                 [(_out(t, d, BF16), None)])[0]
    res_epi = lambda accs, hin: (hin + accs[0],)
    h2 = _mm("mix_out", "nn", [merged], [wt["w_mix_out"]], [[(0, 0)]], t, d, tmm, tnx, [(h1, _tile(tmm, tnx))],
             res_epi, [(_out(t, d, F32), None)])[0]

    un = _rms_fwd("xattn_norm", h2, sm["xattn_norm"])
    mn = _rms_fwd("mem_norm", mem, sm["mem_norm"])
    emit("mix_done", marker=un)
    q = _mm1("xattn_q", "nn", un, wt["w_q"], t, d, tmm, tnx, BF16)
    kv = _mm1("xattn_kv", "nt", mn, wt["w_kv"], n_mem, 2 * d, n_mem, _pick(2 * d, 512), BF16)
    o = _attn_fwd(q, kv)
    h3 = _mm("xattn_out", "nn", [o], [wt["w_xo"]], [[(0, 0)]], t, d, tmm, tnx, [(h2, _tile(tmm, tnx))],
             res_epi, [(_out(t, d, F32), None)])[0]

    n2 = _rms_fwd("ffn2_norm", h3, sm["ffn2_norm"])
    emit("xattn_done", marker=n2)
    h4, ffn2_saved = _ffn_fwd("ffn2", h3, n2, wt["ffn2_w_gate"], wt["ffn2_w_up"], wt["ffn2_w_down"])

    dh4, dh4_bf, gs["final_norm"], loss = _loss_head(h4, sm["final_norm"], tgt)
    dh3, dh3_bf, gs["ffn2_norm"], gb["ffn2_w_gate"], gb["ffn2_w_up"], gb["ffn2_w_down"] = _ffn_bwd(
        "ffn2", h3, sm["ffn2_norm"], wt["ffn2_w_gate"], wt["ffn2_w_up"], wt["ffn2_w_down"], ffn2_saved, dh4, dh4_bf)

    tw = _pick(d, 1024)
    do = _mm1("xattn_do", "nt", dh3_bf, wt["w_xo"], t, d, tmm, tnx, BF16)
    gb["w_xo"] = _mm1("xattn_dwxo", "tn", o, dh3_bf, d, d, tw, tnx, BF16)
    dq, dkv = _attn_bwd(q, kv, do)
    gb["w_q"] = _mm1("xattn_dwq", "tn", un, dq, d, d, tw, tnx, BF16)
    dun = _mm1("xattn_dun", "nt", dq, wt["w_q"], t, d, tmm, tnx, F32)
    dh2, dh2_bf, gs["xattn_norm"] = _rms_bwd("xattn_norm_bwd", h2, sm["xattn_norm"], dun, dh3)
    gb["w_kv"] = _mm1("xattn_dwkv", "tn", dkv, mn, 2 * d, d, _pick(2 * d, 512), d, BF16)
    dmn = _mm1("xattn_dmn", "nn", dkv, wt["w_kv"], n_mem, d, n_mem, tnx, F32)
    gs["mem_norm"] = _rms_bwd("mem_norm_bwd", mem, sm["mem_norm"], dmn)

    gb["w_mix_out"] = _mm1("mix_dwout", "tn", merged, dh2_bf, d, d, tw, tnx, BF16)

    def merge_bwd_epi(accs, gpv, gsv):
        dmerged, z_pool, val, gate = accs
        sp_, ss_, sg_ = jax.nn.sigmoid(gpv), jax.nn.sigmoid(gsv), jax.nn.sigmoid(gate)
        glu = val * sg_
        dz_pool = dmerged * sp_
        dg_pool = dmerged * z_pool * (sp_ * (1.0 - sp_))
        dz_ssm = dmerged * ss_
        dg_ssm = dmerged * glu * (ss_ * (1.0 - ss_))
        dval = dz_ssm * sg_
        dgate = dz_ssm * glu * (1.0 - sg_)
        return dz_pool, dg_pool, dg_ssm, dval, dgate

    dz_pool, dg_pool, dg_ssm, dval, dgate = _mm(
        "mix_merge_bwd", "nt", [dh2_bf, pm, ys], [wt["w_mix_out"], wt["w_pool_proj"], wt["w_glu_val"], wt["w_glu_gate"]],
        [[(0, 0)], [(1, 1)], [(2, 2)], [(2, 3)]], t, d, tmm, tnm, [(proj, gp_spec), (proj, gs_spec)], merge_bwd_epi,
        [(_out(t, d, BF16), None)] * 5)
    gb["w_pool_proj"] = _mm1("pool_dwproj", "tn", dz_pool, pm, d, d_pool, tw, d_pool, BF16)
    gb["w_glu_val"] = _mm1("glu_dwval", "tn", dval, ys, d, d_ssm, tw, d_ssm, BF16)
    gb["w_glu_gate"] = _mm1("glu_dwgate", "tn", dgate, ys, d, d_ssm, tw, d_ssm, BF16)

    def gelu_bwd_epi(accs, yv):
        _, vjp = jax.vjp(jax.nn.gelu, yv)
        return (vjp(accs[0])[0],)

    dy = _mm("glu_dy", "nn", [dval, dgate], [wt["w_glu_val"], wt["w_glu_gate"]], [[(0, 0), (1, 1)]], t, d_ssm, tmy, d_ssm,
             [(y, _tile(tmy, d_ssm))], gelu_bwd_epi, [(_out(t, d_ssm, F32), None)])[0]
    gs["ssm_d"] = _colsum_prod("ssm_dd", dy, proj, b_coff=off_s)
    dyp = _to_segments(dy)
    d_abr, d_abi, d_bbr, d_bbi, d_cre, d_cim, lams = [], [], [], [], [], [], []
    ts = _pick(n_state, 512)
    tc_ = _pick(n_state, 256)
    both = lambda accs: tuple(accs)
    for dr in range(2):
        lr, li, dar, dai = _ssm_bwd(f"ssm_bwd{dr}", dyp, c_re[dr], c_im[dr], xs[dr][0], xs[dr][1], abr2[dr:dr + 1],
                                    abi2[dr:dr + 1], reverse=(dr == 1))
        d_abr.append(dar)
        d_abi.append(dai)
        lams += [lr, li]
        d_br, d_bi = _mm(f"ssm_db{dr}", "tn", [sp32], [lr, li], [[(0, 0)], [(0, 1)]], d_ssm, n_state, d_ssm, ts, [], both,
                         [(_out(d_ssm, n_state, F32), None)] * 2)
        d_bbr.append(_diag_in(d_br, sg, sp, sh))
        d_bbi.append(_diag_in(d_bi, sg, sp, sh))
        d_cr, d_ci = _mm(f"ssm_dc{dr}", "tn", [xs[dr][0], xs[dr][1]], [dyp], [[(0, 0)], [(1, 0)]], n_state, d_ssm, tc_,
                         d_ssm, [], both, [(_out(n_state, d_ssm, F32), None)] * 2)
        d_cre.append(_diag_out(d_cr, sg, sp, sh))
        d_cim.append(-_diag_out(d_ci, sg, sp, sh))
    ds = _from_segments(_mm(
        "ssm_ds", "nt", lams, [b_re[0], b_im[0], b_re[1], b_im[1]], [[(k, k) for k in range(4)]], t, d_ssm, tmy,
        d_ssm, [(dyp, _tile(tmy, d_ssm)), (sm["ssm_d"], _rowvec(d_ssm))],
        lambda accs, dyv, dv: (dyv * dv + accs[0],), [(_out(t, d_ssm, BF16), None)])[0])
    cots = [jnp.concatenate(d_abr, axis=0).reshape(-1, 1), jnp.concatenate(d_abi, axis=0).reshape(-1, 1),
            jnp.concatenate(d_bbr, axis=0), jnp.concatenate(d_bbi, axis=0)]
    d_are, d_aim, d_ldt, d_bre, d_bim = _ssm_disc_bwd(cols, cots)
    gs["ssm_a_re"] = d_are.reshape(2, sg, sp)
    gs["ssm_a_im"] = d_aim.reshape(2, sg, sp)
    gs["ssm_log_dt"] = _rowsum("ssm_dlogdt", d_ldt.reshape(2 * sg, sp)).reshape(2, sg)
    gs["ssm_b_re"] = d_bre.reshape(2, sg, sp, sh)
    gs["ssm_b_im"] = d_bim.reshape(2, sg, sp, sh)
    gs["ssm_c_re"] = jnp.stack(d_cre, axis=0)
    gs["ssm_c_im"] = jnp.stack(d_cim, axis=0)

    dpm = _mm1("pool_dpm", "nn", dz_pool, wt["w_pool_proj"], t, d_pool, tmm, _pick(d_pool, 256), F32)
    dp, gs["pool_w"], gs["pool_scale"] = _pool_bwd(pooled, dpm, pool_w_bf, sm["pool_scale"])

    w_in = wt["w_in"]
    parts = [(dp, 0, d_pool), (ds, d_pool, d_ssm), (dg_pool, off_gp, d), (dg_ssm, off_gs, d)]
    w_in_parts = [w_in[o0:o0 + width] for _, o0, width in parts]
    gb["w_in"] = jnp.concatenate(
        [_mm1(f"in_proj_dw{k}", "tn", p_[0], u, p_[2], d, _pick(p_[2], 1024), tnx, BF16) for k, p_ in enumerate(parts)], axis=0)
    pin = emit("grads_main", gb=gb)
    du = _mm("in_proj_du", "nn", [p_[0] for p_ in parts], w_in_parts, [[(k, k) for k in range(4)]], t, d, tmm, tnx, [],
             lambda accs: (accs[0],), [(_out(t, d, F32), None)], after=pin)[0]
    dh1, dh1_bf, gs["mix_norm"] = _rms_bwd("mix_norm_bwd", h1, sm["mix_norm"], du, dh2)
    pin = emit("small_early", gs=gs, loss=loss)

    def ffn1_weights_done(d_wg, d_wu, d_wd):
        gb["ffn1_w_gate"], gb["ffn1_w_up"], gb["ffn1_w_down"] = d_wg, d_wu, d_wd
        return emit("grads_ffn1", gb=gb)

    dx, _, gs["ffn1_norm"], _, _, _ = _ffn_bwd(
        "ffn1", x, sm["ffn1_norm"], wt["ffn1_w_gate"], wt["ffn1_w_up"], wt["ffn1_w_down"], ffn1_saved, dh1, dh1_bf,
        weights_done=ffn1_weights_done, after=pin)
    return loss, dx, gb, gs


WEIGHTS = ["ffn1_norm", "ffn1_w_gate", "ffn1_w_up", "ffn1_w_down", "mix_norm", "w_in", "pool_w", "pool_scale",
           "w_pool_proj", "ssm_a_re", "ssm_a_im", "ssm_log_dt", "ssm_b_re", "ssm_b_im", "ssm_c_re", "ssm_c_im", "ssm_d",
           "w_glu_val", "w_glu_gate", "w_mix_out", "xattn_norm", "mem_norm", "w_q", "w_kv", "w_xo", "ffn2_norm",
           "ffn2_w_gate", "ffn2_w_up", "ffn2_w_down", "final_norm"]
COL_SHARDED = ["ffn1_w_gate", "ffn1_w_up", "w_in", "w_pool_proj", "w_glu_val", "w_glu_gate", "w_kv", "ffn2_w_gate",
               "ffn2_w_up"]
ROW_SHARDED = ["ffn1_w_down", "w_mix_out", "w_q", "w_xo", "ffn2_w_down"]
BIG = [n for n in WEIGHTS if n in COL_SHARDED or n in ROW_SHARDED]
SMALL = [n for n in WEIGHTS if n not in BIG]
FFN1_BIG = ["ffn1_w_gate", "ffn1_w_up", "ffn1_w_down"]
MAIN_BIG = [n for n in BIG if n not in FFN1_BIG]
GATHER_PLAN = [("ffn1_up_done", ["ffn1_w_down"]), ("ffn1_fwd_done", ["w_in"]),
               ("mix_in_done", ["w_pool_proj", "w_glu_val", "w_glu_gate", "w_mix_out"]),
               ("mix_done", ["w_q", "w_kv", "w_xo"]), ("xattn_done", ["ffn2_w_gate", "ffn2_w_up", "ffn2_w_down"])]
LATE_SMALL = "ffn1_norm"
EARLY_SMALL = [n for n in SMALL if n != LATE_SMALL]
PACK_ROWS = SUBLANES * LANES
GRAD_ROW_TILE = 256


def _to_rows(name, w, width):
    if name in COL_SHARDED:
        w = w.T
    return w.reshape(-1, width)


def _from_rows(name, rows, shard_shape):
    if name in COL_SHARDED:
        return rows.reshape(shard_shape[1], shard_shape[0]).T
    return rows.reshape(shard_shape)


def _pack_small(vals):
    flat = []
    for v in vals:
        f = v.reshape(-1)
        flat.append(jnp.pad(f, (0, (-f.shape[0]) % PACK_ROWS)))
    total = sum(f.shape[0] for f in flat)
    flat.append(jnp.zeros(((-total) % (GRAD_ROW_TILE * LANES),), F32))
    return jnp.concatenate(flat).reshape(-1, LANES)


def _unpack_small(packed, shapes):
    out, row = [], 0
    for shp in shapes:
        size = math.prod(shp)
        rows = -(-size // PACK_ROWS) * SUBLANES
        out.append(packed[row:row + rows].reshape(-1)[:size].reshape(shp))
        row += rows
    return out


def kernel(x, mem, ffn1_norm, ffn1_w_gate, ffn1_w_up, ffn1_w_down, mix_norm, w_in, pool_w, pool_scale, w_pool_proj, ssm_a_re, ssm_a_im, ssm_log_dt, ssm_b_re, ssm_b_im, ssm_c_re, ssm_c_im, ssm_d, w_glu_val, w_glu_gate, w_mix_out, xattn_norm, mem_norm, w_q, w_kv, w_xo, ffn2_norm, ffn2_w_gate, ffn2_w_up, ffn2_w_down, final_norm, loss_target, m_ffn1_norm, m_ffn1_w_gate, m_ffn1_w_up, m_ffn1_w_down, m_mix_norm, m_w_in, m_pool_w, m_pool_scale, m_w_pool_proj, m_ssm_a_re, m_ssm_a_im, m_ssm_log_dt, m_ssm_b_re, m_ssm_b_im, m_ssm_c_re, m_ssm_c_im, m_ssm_d, m_w_glu_val, m_w_glu_gate, m_w_mix_out, m_xattn_norm, m_mem_norm, m_w_q, m_w_kv, m_w_xo, m_ffn2_norm, m_ffn2_w_gate, m_ffn2_w_up, m_ffn2_w_down, m_final_norm, v_ffn1_norm, v_ffn1_w_gate, v_ffn1_w_up, v_ffn1_w_down, v_mix_norm, v_w_in, v_pool_w, v_pool_scale, v_w_pool_proj, v_ssm_a_re, v_ssm_a_im, v_ssm_log_dt, v_ssm_b_re, v_ssm_b_im, v_ssm_c_re, v_ssm_c_im, v_ssm_d, v_w_glu_val, v_w_glu_gate, v_w_mix_out, v_xattn_norm, v_mem_norm, v_w_q, v_w_kv, v_w_xo, v_ffn2_norm, v_ffn2_w_gate, v_ffn2_w_up, v_ffn2_w_down, v_final_norm):
    given = dict(locals())
    wts = {n: given[n] for n in WEIGHTS}
    moms = {n: (given["m_" + n], given["v_" + n]) for n in WEIGHTS}
    x2, mem2, tgt2 = x[0], mem[0], loss_target[0]
    d = x2.shape[1]
    core = lax.axis_index("c").astype(jnp.int32).reshape(1)
    chip = (2 * lax.axis_index("x") + lax.axis_index("y")).astype(jnp.int32).reshape(1)

    def full_form(n, f):
        shard = wts[n][0].shape
        return f.reshape(N_DEV * shard[1], shard[0]) if n in COL_SHARDED else f.reshape(N_DEV * shard[0], shard[1])

    shards = {n: _to_rows(n, wts[n][0], d).astype(BF16) for n in BIG}
    first = FFN1_BIG[:2]
    wt = {n: full_form(n, f) for n, f in zip(first, _allgather("weight_allgather_first", [shards[n] for n in first]))}
    gathers, after = {}, wt[first[0]]
    for event, names in GATHER_PLAN:
        gathers[event] = (names, _gather_start("weight_gather_start_" + event, [shards[n] for n in names], after))
        after = gathers[event][1]["token"]
    sm = {n: (wts[n].reshape(1, -1) if wts[n].ndim <= 2 else wts[n][0]) for n in SMALL}
    sm["ffn1_norm"] = sm["ffn1_norm"] + after[0, 0]

    pending = {}

    def reduce_start(tag, names, gb):
        blocks = [gb[n].reshape(N_DEV, -1, d) for n in names]
        pad_rows = (-sum(b.shape[1] for b in blocks)) % GRAD_ROW_TILE
        packed = jnp.concatenate(blocks + ([jnp.zeros((N_DEV, pad_rows, d), BF16)] if pad_rows else []), axis=1)
        pair = _pair_sum("grad_pair_sum_" + tag, packed, _exchange_cores("grad_exchange_cores_" + tag, packed), core)
        pending[tag] = (pair, _chips_start("grad_exchange_chips_start_" + tag, pair), [b.shape[1] for b in blocks])
        return pending[tag][1]["token"]

    def reduce_finish(tag, after):
        pair, started, rows = pending[tag]
        recv = _split_wait("grad_exchange_chips_wait_" + tag, started, after)[0]
        return _chip_sum("grad_chip_sum_" + tag, pair, recv, chip), rows

    def ev(name, gb=None, gs=None, loss=None, marker=None):
        if name in gathers:
            names, started = gathers[name]
            for n, f in zip(names, _split_wait("weight_gather_wait_" + name, started, marker)):
                wt[n] = full_form(n, f)
        elif name == "grads_main":
            return reduce_start("main", MAIN_BIG, gb)
        elif name == "small_early":
            pending["small"] = _slots_start("small_gather_start", _pack_small([gs[n] for n in EARLY_SMALL] + [loss[:, :1]]))
            return pending["small"]["token"]
        elif name == "grads_ffn1":
            return reduce_start("ffn1", FFN1_BIG, gb)
        return None

    _, dx, _, gs = _local_step(x2, mem2, tgt2, wt, sm, ev)

    out_g, out_d, out_m, out_v = {}, {}, {}, {}

    def update(n, g_full):
        shape = wts[n].shape
        two_d = (-1, shape[-1])
        dl, m2, v2 = _adamw("adamw_" + n, wts[n].reshape(two_d), g_full.reshape(two_d), moms[n][0].reshape(two_d),
                            moms[n][1].reshape(two_d))
        out_g[n], out_d[n], out_m[n], out_v[n] = g_full, dl.reshape(shape), m2.reshape(shape), v2.reshape(shape)
        return dl

    def update_big(names, g_rows, rows):
        off = 0
        for n, r in zip(names, rows):
            shard = wts[n].shape
            dl = update(n, _from_rows(n, g_rows[off:off + r], shard[1:]).reshape(shard))
            off += r
        return dl

    last = update_big(MAIN_BIG, *reduce_finish("main", dx))

    small_sum = _sum_slots("small_sum", _split_wait("small_gather_wait", pending["small"], dx)[0], F32)
    late = _allgather("small_allgather_late", [gs[LATE_SMALL].reshape(-1, LANES)])[0]
    late_sum = _sum_slots("small_sum_late", late.reshape(N_DEV, -1, LANES), F32)
    vals = _unpack_small(small_sum, [wts[n].shape for n in EARLY_SMALL] + [(1, 1)])
    total_loss = vals[-1].reshape(())
    for n, g_full in zip(EARLY_SMALL + [LATE_SMALL], vals[:-1] + [late_sum.reshape(wts[LATE_SMALL].shape)]):
        update(n, g_full)

    update_big(FFN1_BIG, *reduce_finish("ffn1", last))

    return (total_loss, dx[None], *[out_g[n] for n in WEIGHTS], *[out_d[n] for n in WEIGHTS],
            *[out_m[n] for n in WEIGHTS], *[out_v[n] for n in WEIGHTS])
```

```python
import functools
import math

import jax
import jax.numpy as jnp
from jax import lax
from jax.experimental import pallas as pl
from jax.experimental.pallas import tpu as pltpu

F32 = jnp.float32
BF16 = jnp.bfloat16
EPS = 1e-6
N_XHEADS = 4
POOL_WINDOWS = (2, 4, 8, 16)
ADAM_LR = 0.001
ADAM_B1 = 0.9
ADAM_B2 = 0.999
ADAM_EPS = 1e-08
ADAM_WD = 0.01
ADAM_STEP = 10
N_DEV = 8
VMEM_LIMIT_V7X = 48 * 1024 * 1024
LANES = 128
SUBLANES = 8
SUB_ROWS = 256
POOL_PAD = 16
MESH = pl.DeviceIdType.MESH
ANY = pl.BlockSpec(memory_space=pl.ANY)
HBM = pl.BlockSpec(memory_space=pltpu.HBM)
SEM = pl.BlockSpec(memory_space=pltpu.SEMAPHORE)
SIDE_EFFECT = pltpu.SideEffectType.DATAFLOW_SIDE_EFFECTING

_DIMS = {
    "nt": (((1,), (1,)), ((), ())),
    "nn": (((1,), (0,)), ((), ())),
    "tn": (((0,), (0,)), ((), ())),
}


def _pick(dim, pref, mult=LANES):
    if dim <= pref:
        return dim
    for t in range(pref - pref % mult, 0, -mult):
        if dim % t == 0:
            return t
    return dim


def _params(sem):
    return pltpu.CompilerParams(dimension_semantics=sem, vmem_limit_bytes=VMEM_LIMIT_V7X)


def _hbm(*arrays):
    return [pltpu.with_memory_space_constraint(a, pltpu.HBM) for a in arrays]


def _tile(tm, tn, coff=0):
    return pl.BlockSpec((tm, tn), lambda i, j: (i, j + coff))


def _rowvec(tn, coff=0):
    return pl.BlockSpec((1, tn), lambda i, j: (0, j + coff))


def _out(m, n, dtype):
    return jax.ShapeDtypeStruct((m, n), dtype)


def _mm(name, form, a_list, b_list, groups, m, n, tm, tn, extras, epilogue, outs, after=None, sub=SUB_ROWS):
    na, nb, ne = len(a_list), len(b_list), len(extras)
    pins = [] if after is None else [after]
    step = tm if (sub is None or form == "tn" or tm % sub) else sub

    def a_spec(a):
        if form == "tn":
            return pl.BlockSpec((a.shape[0], tm), lambda i, j: (0, i))
        return pl.BlockSpec((tm, a.shape[1]), lambda i, j: (i, 0))

    def b_spec(b):
        if form == "nt":
            return pl.BlockSpec((tn, b.shape[1]), lambda i, j: (j, 0))
        return pl.BlockSpec((b.shape[0], tn), lambda i, j: (0, j))

    def body(*refs):
        a_refs, b_refs = refs[:na], refs[na:na + nb]
        e_refs, o_refs = refs[na + nb:na + nb + ne], refs[na + nb + ne + len(pins):]
        b_vals = {}
        for s0 in range(0, tm, step):
            rows = slice(None) if step == tm else pl.ds(s0, step)
            a_vals, accs = {}, []
            for group in groups:
                acc = None
                for ai, bi in group:
                    if ai not in a_vals:
                        a_vals[ai] = (a_refs[ai][...] if form == "tn" else a_refs[ai][rows, :]).astype(BF16)
                    if bi not in b_vals:
                        b_vals[bi] = b_refs[bi][...].astype(BF16)
                    d = lax.dot_general(a_vals[ai], b_vals[bi], _DIMS[form], preferred_element_type=F32)
                    acc = d if acc is None else acc + d
                accs.append(acc)
            res = epilogue(accs, *[e[rows, :] if e.shape[0] == tm else e[...] for e in e_refs])
            for o_ref, r in zip(o_refs, res):
                o_ref[rows, :] = r.astype(o_ref.dtype)

    out_specs = [_tile(tm, tn) if s is None else s for _, s in outs]
    res = pl.pallas_call(
        body, name=name, grid=(m // tm, n // tn),
        in_specs=[a_spec(a) for a in a_list] + [b_spec(b) for b in b_list] + [s for _, s in extras] + [ANY] * len(pins),
        out_specs=out_specs, out_shape=[o for o, _ in outs],
        compiler_params=_params(("parallel", "parallel")),
    )(*_hbm(*a_list, *b_list, *[e for e, _ in extras]), *pins)
    return res


def _mm1(name, form, a, b, m, n, tm, tn, dtype, scale=None):
    epi = (lambda accs: (accs[0],)) if scale is None else (lambda accs: (accs[0] * scale,))
    return _mm(name, form, [a], [b], [[(0, 0)]], m, n, tm, tn, [], epi, [(_out(m, n, dtype), None)])[0]


def _rms_fwd(name, h, g):
    t, d = h.shape
    tm = _pick(t, 512, SUBLANES)

    def body(h_ref, g_ref, n_ref):
        hv = h_ref[...]
        r = lax.rsqrt(jnp.mean(hv * hv, axis=-1, keepdims=True) + EPS)
        n_ref[...] = ((hv * r) * g_ref[...]).astype(BF16)

    return pl.pallas_call(
        body, name=name, grid=(t // tm,),
        in_specs=[pl.BlockSpec((tm, d), lambda i: (i, 0)), pl.BlockSpec((1, d), lambda i: (0, 0))],
        out_specs=pl.BlockSpec((tm, d), lambda i: (i, 0)), out_shape=_out(t, d, BF16),
        compiler_params=_params(("parallel",)),
    )(*_hbm(h, g))


def _rms_bwd(name, h, g, dn, dres=None):
    t, d = h.shape
    tm = _pick(t, 512, SUBLANES)
    need_dh = dres is not None

    def body(*refs):
        if need_dh:
            h_ref, g_ref, dn_ref, dres_ref, dh_ref, dhb_ref, dg_ref = refs
        else:
            h_ref, g_ref, dn_ref, dg_ref = refs
        hv = h_ref[...]
        r = lax.rsqrt(jnp.mean(hv * hv, axis=-1, keepdims=True) + EPS)
        nh = hv * r
        dnv = dn_ref[...].astype(F32)

        @pl.when(pl.program_id(0) == 0)
        def _():
            dg_ref[...] = jnp.zeros_like(dg_ref)

        dg_ref[...] += jnp.sum(dnv * nh, axis=0, keepdims=True)
        if need_dh:
            dng = dnv * g_ref[...]
            dh = dres_ref[...] + r * (dng - nh * jnp.mean(dng * nh, axis=-1, keepdims=True))
            dh_ref[...] = dh
            dhb_ref[...] = dh.astype(BF16)

    row = pl.BlockSpec((tm, d), lambda i: (i, 0))
    vec = pl.BlockSpec((1, d), lambda i: (0, 0))
    if need_dh:
        return pl.pallas_call(
            body, name=name, grid=(t // tm,), in_specs=[row, vec, row, row], out_specs=[row, row, vec],
            out_shape=[_out(t, d, F32), _out(t, d, BF16), _out(1, d, F32)], compiler_params=_params(("arbitrary",)),
        )(*_hbm(h, g, dn, dres))
    return pl.pallas_call(
        body, name=name, grid=(t // tm,), in_specs=[row, vec, row], out_specs=vec,
        out_shape=_out(1, d, F32), compiler_params=_params(("arbitrary",)),
    )(*_hbm(h, g, dn))


def _loss_head(h, g, tgt):
    t, d = h.shape
    tm = _pick(t, 512, SUBLANES)

    def body(h_ref, g_ref, t_ref, dh_ref, dhb_ref, dg_ref, loss_ref):
        hv = h_ref[...]
        r = lax.rsqrt(jnp.mean(hv * hv, axis=-1, keepdims=True) + EPS)
        nh = hv * r
        err = nh * g_ref[...] - t_ref[...]

        @pl.when(pl.program_id(0) == 0)
        def _():
            dg_ref[...] = jnp.zeros_like(dg_ref)
            loss_ref[...] = jnp.zeros_like(loss_ref)

        per_row = jnp.mean(err * err, axis=-1, keepdims=True)
        loss_ref[...] += 0.5 * jnp.sum(per_row, axis=0, keepdims=True)
        dy = err * (1.0 / d)
        dg_ref[...] += jnp.sum(dy * nh, axis=0, keepdims=True)
        dng = dy * g_ref[...]
        dh = r * (dng - nh * jnp.mean(dng * nh, axis=-1, keepdims=True))
        dh_ref[...] = dh
        dhb_ref[...] = dh.astype(BF16)

    row = pl.BlockSpec((tm, d), lambda i: (i, 0))
    vec = pl.BlockSpec((1, d), lambda i: (0, 0))
    return pl.pallas_call(
        body, name="loss_head", grid=(t // tm,), in_specs=[row, vec, row],
        out_specs=[row, row, vec, pl.BlockSpec((1, LANES), lambda i: (0, 0))],
        out_shape=[_out(t, d, F32), _out(t, d, BF16), _out(1, d, F32), _out(1, LANES, F32)],
        compiler_params=_params(("arbitrary",)),
    )(*_hbm(h, g, tgt))


def _ffn_fwd(tag, h, n, wg_t, wu_t, wd):
    t, d = h.shape
    f = wg_t.shape[0]
    tm, tn = _pick(t, 1024), _pick(f, 1408)

    def up_epi(accs):
        a, b = accs
        return a, b, (a * jax.nn.sigmoid(a)) * b

    a, b, hid = _mm(tag + "_up", "nt", [n], [wg_t, wu_t], [[(0, 0)], [(0, 1)]], t, f, tm, tn, [], up_epi,
                    [(_out(t, f, BF16), None)] * 3)
    if callable(wd):
        wd = wd(hid)
    tm2, tn2 = _pick(t, 1024), _pick(d, 512)
    h_out = _mm(tag + "_down", "nn", [hid], [wd], [[(0, 0)]], t, d, tm2, tn2, [(h, _tile(tm2, tn2))],
                lambda accs, hin: (hin + 0.5 * accs[0],), [(_out(t, d, F32), None)])[0]
    return h_out, (n, a, b, hid)


def _ffn_bwd(tag, h, g, wg_t, wu_t, wd, saved, dh, dh_bf, weights_done=None, after=None):
    n, a, b, hid = saved
    t, d = h.shape
    f = wd.shape[0]
    tm, tn = _pick(t, 1024), _pick(f, 1408)

    def hid_epi(accs, av, bv):
        dhid = 0.5 * accs[0]
        av, bv = av.astype(F32), bv.astype(F32)
        sig = jax.nn.sigmoid(av)
        da = dhid * bv * (sig * (1.0 + av * (1.0 - sig)))
        db = dhid * (av * sig)
        return da, db

    da, db = _mm(tag + "_bwd_hid", "nt", [dh_bf], [wd], [[(0, 0)]], t, f, tm, tn,
                 [(a, _tile(tm, tn)), (b, _tile(tm, tn))], hid_epi, [(_out(t, f, BF16), None)] * 2, after=after)
    tw, tnw = _pick(f, 1408), _pick(d, 512)
    d_wd = _mm1(tag + "_dwd", "tn", hid, dh_bf, f, d, tw, tnw, BF16, scale=0.5)
    d_wg = _mm1(tag + "_dwg", "tn", da, n, f, d, tw, tnw, BF16)
    d_wu = _mm1(tag + "_dwu", "tn", db, n, f, d, tw, tnw, BF16)
    pin = weights_done(d_wg, d_wu, d_wd) if weights_done is not None else None
    tm2, tn2 = _pick(t, 1024), _pick(d, 512)
    dn = _mm(tag + "_dn", "nn", [da, db], [wg_t, wu_t], [[(0, 0), (1, 1)]], t, d, tm2, tn2, [],
             lambda accs: (accs[0],), [(_out(t, d, F32), None)], after=pin)[0]
    dh_in, dh_in_bf, dg = _rms_bwd(tag + "_norm_bwd", h, g, dn, dh)
    return dh_in, dh_in_bf, dg, d_wg, d_wu, d_wd


def _window_sum(win, offsets):
    n = win.shape[0]
    acc = None
    for j in offsets:
        term = win if j == 0 else pltpu.roll(win, (-j) % n, 0)
        acc = term if acc is None else acc + term
    return acc


def _pool_counts(r0, ch, c, left, right, t):
    pos = r0 + lax.broadcasted_iota(jnp.int32, (ch, c), 0)
    return (jnp.minimum(pos + right + 1, t) - jnp.maximum(pos - left, 0)).astype(F32)


def _pool_fwd(proj, pool_w_bf, pool_scale):
    t = proj.shape[0]
    ng, c, _ = pool_w_bf.shape
    ch = _pick(t, 256, SUBLANES)
    pad = POOL_PAD

    def body(p_ref, w_ref, s_ref, pooled_ref, pm_ref, buf):
        grp = pl.program_id(0)
        buf[pl.ds(0, pad), :] = jnp.zeros((pad, c), F32)
        buf[pl.ds(pad + t, pad), :] = jnp.zeros((pad, c), F32)

        def fill(ci, carry):
            r0 = pl.multiple_of(ci * ch, SUBLANES)
            buf[pl.ds(pl.multiple_of(r0 + pad, SUBLANES), ch), :] = p_ref[pl.ds(r0, ch), :]
            return carry

        lax.fori_loop(0, t // ch, fill, 0)
        for gi, w in enumerate(POOL_WINDOWS):
            left = w // 2
            right = w - 1 - left

            @pl.when(grp == gi)
            def _(left=left, right=right):
                def chunk(ci, carry):
                    r0 = pl.multiple_of(ci * ch, SUBLANES)
                    win = buf[pl.ds(r0, ch + 2 * pad), :]
                    s = _window_sum(win, range(-left, right + 1))[pad:pad + ch]
                    pooled = s / _pool_counts(r0, ch, c, left, right, t) - win[pad:pad + ch]
                    pooled_bf = pooled.astype(BF16)
                    mixed = jnp.dot(pooled_bf, w_ref[0], preferred_element_type=F32)
                    pooled_ref[pl.ds(r0, ch), :] = pooled_bf
                    pm_ref[pl.ds(r0, ch), :] = (mixed * s_ref[...]).astype(BF16)
                    return carry

                lax.fori_loop(0, t // ch, chunk, 0)

    col = pl.BlockSpec((t, c), lambda g: (0, g))
    return pl.pallas_call(
        body, name="pool_fwd", grid=(ng,),
        in_specs=[col, pl.BlockSpec((1, c, c), lambda g: (g, 0, 0)), pl.BlockSpec((1, c), lambda g: (0, g))],
        out_specs=[col, col], out_shape=[_out(t, ng * c, BF16), _out(t, ng * c, BF16)],
        scratch_shapes=[pltpu.VMEM((t + 2 * pad, c), F32)],
        compiler_params=_params(("parallel",)),
    )(*_hbm(proj, pool_w_bf, pool_scale))


def _pool_bwd(pooled, dpm, pool_w_bf, pool_scale):
    t = pooled.shape[0]
    ng, c, _ = pool_w_bf.shape
    ch = _pick(t, 256, SUBLANES)
    pad = POOL_PAD

    def body(pooled_ref, dpm_ref, w_ref, s_ref, dp_ref, dw_ref, ds_ref, buf, raw):
        grp = pl.program_id(0)
        buf[pl.ds(0, pad), :] = jnp.zeros((pad, c), F32)
        buf[pl.ds(pad + t, pad), :] = jnp.zeros((pad, c), F32)
        dw_ref[...] = jnp.zeros_like(dw_ref)
        ds_ref[...] = jnp.zeros_like(ds_ref)
        for gi, w in enumerate(POOL_WINDOWS):
            left = w // 2
            right = w - 1 - left

            @pl.when(grp == gi)
            def _(left=left, right=right):
                def first(ci, carry):
                    r0 = pl.multiple_of(ci * ch, SUBLANES)
                    pv = pooled_ref[pl.ds(r0, ch), :]
                    dpm_v = dpm_ref[pl.ds(r0, ch), :]
                    mixed = jnp.dot(pv, w_ref[0], preferred_element_type=F32)
                    ds_ref[...] += jnp.sum(dpm_v * mixed, axis=0, keepdims=True)
                    dmixed = (dpm_v * s_ref[...]).astype(BF16)
                    dw_ref[0] += lax.dot_general(pv, dmixed, _DIMS["tn"], preferred_element_type=F32)
                    dpooled = lax.dot_general(dmixed, w_ref[0], _DIMS["nt"], preferred_element_type=F32)
                    raw[pl.ds(r0, ch), :] = dpooled
                    buf[pl.ds(pl.multiple_of(r0 + pad, SUBLANES), ch), :] = (
                        dpooled / _pool_counts(r0, ch, c, left, right, t))
                    return carry

                lax.fori_loop(0, t // ch, first, 0)

                def second(ci, carry):
                    r0 = pl.multiple_of(ci * ch, SUBLANES)
                    win = buf[pl.ds(r0, ch + 2 * pad), :]
                    s = _window_sum(win, range(-right, left + 1))[pad:pad + ch]
                    dp_ref[pl.ds(r0, ch), :] = (s - raw[pl.ds(r0, ch), :]).astype(BF16)
                    return carry

                lax.fori_loop(0, t // ch, second, 0)

    col = pl.BlockSpec((t, c), lambda g: (0, g))
    return pl.pallas_call(
        body, name="pool_bwd", grid=(ng,),
        in_specs=[col, col, pl.BlockSpec((1, c, c), lambda g: (g, 0, 0)), pl.BlockSpec((1, c), lambda g: (0, g))],
        out_specs=[col, pl.BlockSpec((1, c, c), lambda g: (g, 0, 0)), pl.BlockSpec((1, c), lambda g: (0, g))],
        out_shape=[_out(t, ng * c, BF16), jax.ShapeDtypeStruct((ng, c, c), F32), _out(1, ng * c, F32)],
        scratch_shapes=[pltpu.VMEM((t + 2 * pad, c), F32), pltpu.VMEM((t, c), F32)],
        compiler_params=_params(("parallel",)),
    )(*_hbm(pooled, dpm, pool_w_bf, pool_scale))


def _discretise(a_re, a_im, log_dt, b_re, b_im):
    dt = jnp.exp(log_dt)
    mag = jnp.exp(dt * a_re)
    ang = dt * a_im
    abr = mag * jnp.cos(ang)
    abi = mag * jnp.sin(ang)
    den = a_re * a_re + a_im * a_im
    nr = abr - 1.0
    qr = (nr * a_re + abi * a_im) / den
    qi = (abi * a_re - nr * a_im) / den
    return abr, abi, qr * b_re - qi * b_im, qr * b_im + qi * b_re


def _ssm_disc(cols):
    n, hh = cols[3].shape

    def body(ar, ai, ld, br, bi, o1, o2, o3, o4):
        res = _discretise(ar[...], ai[...], ld[...], br[...], bi[...])
        for o, r in zip((o1, o2, o3, o4), res):
            o[...] = r

    return pl.pallas_call(
        body, name="ssm_disc",
        out_shape=[_out(n, 1, F32), _out(n, 1, F32), _out(n, hh, F32), _out(n, hh, F32)],
    )(*cols)


def _ssm_disc_bwd(cols, cots):
    n, hh = cols[3].shape

    def body(ar, ai, ld, br, bi, c1, c2, c3, c4, o1, o2, o3, o4, o5):
        _, vjp = jax.vjp(_discretise, ar[...], ai[...], ld[...], br[...], bi[...])
        res = vjp((c1[...], c2[...], c3[...], c4[...]))
        for o, r in zip((o1, o2, o3, o4, o5), res):
            o[...] = r

    return pl.pallas_call(
        body, name="ssm_disc_bwd",
        out_shape=[_out(n, 1, F32)] * 3 + [_out(n, hh, F32)] * 2,
    )(*cols, *cots)


def _rowsum(name, a):
    r, _ = a.shape

    def body(a_ref, o_ref):
        o_ref[...] = jnp.sum(a_ref[...], axis=-1, keepdims=True)

    return pl.pallas_call(body, name=name, out_shape=_out(r, 1, F32))(a)


def _cmul(pr, pi, qr, qi):
    return pr * qr - pi * qi, pr * qi + pi * qr


def _cpow(pr, pi, n):
    rr, ri = None, None
    while n:
        if n & 1:
            rr, ri = (pr, pi) if rr is None else _cmul(rr, ri, pr, pi)
        n >>= 1
        if n:
            pr, pi = _cmul(pr, pi, pr, pi)
    return rr, ri


def _segment_carry(er, ei, pr, pi, reverse):
    row = lax.broadcasted_iota(jnp.int32, er.shape, 0)
    cr, ci = jnp.zeros_like(er), jnp.zeros_like(ei)
    for _ in range(SUBLANES - 1):
        tr = er + pr * cr - pi * ci
        ti = ei + pr * ci + pi * cr
        if reverse:
            keep, shift = row < SUBLANES - 1, SUBLANES - 1
        else:
            keep, shift = row >= 1, 1
        cr = jnp.where(keep, pltpu.roll(tr, shift, 0), 0.0)
        ci = jnp.where(keep, pltpu.roll(ti, shift, 0), 0.0)
    return cr, ci


def _ssm_fwd(name, sp, b_re, b_im, c_re, c_im, ar, ai, reverse):
    t, c = sp.shape
    s = ar.shape[1]
    w = _pick(s, 512)
    ch = _pick(t, 512, SUBLANES)
    n_ch, gpc, steps = t // ch, ch // SUBLANES, t // SUBLANES

    def body(sp_ref, bre_ref, bim_ref, cre_ref, cim_ref, ar_ref, ai_ref, xr_ref, xi_ref, y_ref, ur, ui, xbr, xbi):
        a_r = jnp.broadcast_to(ar_ref[...], (SUBLANES, w))
        a_i = jnp.broadcast_to(ai_ref[...], (SUBLANES, w))

        @pl.when(pl.program_id(0) == 0)
        def _():
            y_ref[...] = jnp.zeros_like(y_ref)

        def sweep(h0, store):
            def chunk(k, h):
                ci = n_ch - 1 - k if reverse else k
                rows = pl.ds(pl.multiple_of(ci * ch, ch), ch)
                spv = sp_ref[rows, :].astype(BF16)
                ur[...] = jnp.dot(spv, bre_ref[...], preferred_element_type=F32)
                ui[...] = jnp.dot(spv, bim_ref[...], preferred_element_type=F32)

                def group(g, hh):
                    gi = gpc - 1 - g if reverse else g
                    r0 = pl.multiple_of(gi * SUBLANES, SUBLANES)
                    hr, hi = hh
                    nr = a_r * hr - a_i * hi + ur[pl.ds(r0, SUBLANES), :]
                    ni = a_r * hi + a_i * hr + ui[pl.ds(r0, SUBLANES), :]
                    if store:
                        xbr[pl.ds(r0, SUBLANES), :] = nr
                        xbi[pl.ds(r0, SUBLANES), :] = ni
                    return nr, ni

                h = lax.fori_loop(0, gpc, group, h)
                if store:
                    xr16, xi16 = xbr[...].astype(BF16), xbi[...].astype(BF16)
                    xr_ref[rows, :] = xr16
                    xi_ref[rows, :] = xi16
                    y_ref[rows, :] += (jnp.dot(xr16, cre_ref[...], preferred_element_type=F32)
                                       + jnp.dot(xi16, cim_ref[...], preferred_element_type=F32))
                return h

            return lax.fori_loop(0, n_ch, chunk, h0)

        zero = jnp.zeros((SUBLANES, w), F32)
        er, ei = sweep((zero, zero), False)
        pr, pi = _cpow(ar_ref[...], ai_ref[...], steps)
        sweep(_segment_carry(er, ei, pr, pi, reverse), True)

    col = lambda i: (0, i)
    return pl.pallas_call(
        body, name=name, grid=(s // w,),
        in_specs=[pl.BlockSpec((t, c), lambda i: (0, 0)), pl.BlockSpec((c, w), col), pl.BlockSpec((c, w), col),
                  pl.BlockSpec((w, c), lambda i: (i, 0)), pl.BlockSpec((w, c), lambda i: (i, 0)),
                  pl.BlockSpec((1, w), col), pl.BlockSpec((1, w), col)],
        out_specs=[pl.BlockSpec((t, w), col), pl.BlockSpec((t, w), col), pl.BlockSpec((t, c), lambda i: (0, 0))],
        out_shape=[_out(t, s, BF16), _out(t, s, BF16), _out(t, c, F32)],
        scratch_shapes=[pltpu.VMEM((ch, w), F32)] * 4,
        compiler_params=_params(("arbitrary",)),
    )(*_hbm(sp, b_re, b_im, c_re, c_im, ar, ai))


def _ssm_bwd(name, dyp, c_re, c_im, xr, xi, ar, ai, reverse):
    t, c = dyp.shape
    s = ar.shape[1]
    w = _pick(s, 512)
    ch = _pick(t, 512, SUBLANES)
    n_ch, gpc, steps = t // ch, ch // SUBLANES, t // SUBLANES
    back = not reverse
    edge = 2 * SUBLANES

    def body(dy_ref, cre_ref, cim_ref, xr_ref, xi_ref, ar_ref, ai_ref, lr_ref, li_ref, dar_ref, dai_ref,
             gr, gi_, lbr, lbi, xbr, xbi):
        a_r = jnp.broadcast_to(ar_ref[...], (SUBLANES, w))
        a_i = -jnp.broadcast_to(ai_ref[...], (SUBLANES, w))
        row = lax.broadcasted_iota(jnp.int32, (SUBLANES, w), 0)

        def neighbours(ci, x_ref, buf):
            rows = pl.ds(pl.multiple_of(ci * ch, ch), ch)
            if reverse:
                buf[pl.ds(0, ch), :] = x_ref[rows, :].astype(F32)
                nxt = x_ref[pl.ds(pl.multiple_of(jnp.minimum(ci + 1, n_ch - 1) * ch, ch), edge), :].astype(F32)[:SUBLANES]
                first = x_ref[pl.ds(0, edge), :].astype(F32)[:SUBLANES]
                wrap = jnp.where(row < SUBLANES - 1, pltpu.roll(first, SUBLANES - 1, 0), 0.0)
                buf[pl.ds(ch, SUBLANES), :] = jnp.where(ci == n_ch - 1, wrap, nxt)
            else:
                buf[pl.ds(SUBLANES, ch), :] = x_ref[rows, :].astype(F32)
                prv = x_ref[pl.ds(pl.multiple_of(jnp.maximum(ci * ch - edge, 0), edge), edge), :].astype(F32)[SUBLANES:]
                last = x_ref[pl.ds(t - edge, edge), :].astype(F32)[SUBLANES:]
                wrap = jnp.where(row >= 1, pltpu.roll(last, 1, 0), 0.0)
                buf[pl.ds(0, SUBLANES), :] = jnp.where(ci == 0, wrap, prv)

        def sweep(h0, store):
            def chunk(k, carry):
                ci = n_ch - 1 - k if back else k
                rows = pl.ds(pl.multiple_of(ci * ch, ch), ch)
                dyv = dy_ref[rows, :].astype(BF16)
                gr[...] = lax.dot_general(dyv, cre_ref[...], _DIMS["nt"], preferred_element_type=F32)
                gi_[...] = lax.dot_general(dyv, cim_ref[...], _DIMS["nt"], preferred_element_type=F32)
                if store:
                    neighbours(ci, xr_ref, xbr)
                    neighbours(ci, xi_ref, xbi)

                def group(g, cc):
                    gidx = gpc - 1 - g if back else g
                    r0 = pl.multiple_of(gidx * SUBLANES, SUBLANES)
                    hr, hi = cc[0], cc[1]
                    nr = a_r * hr - a_i * hi + gr[pl.ds(r0, SUBLANES), :]
                    ni = a_r * hi + a_i * hr + gi_[pl.ds(r0, SUBLANES), :]
                    if not store:
                        return nr, ni
                    lbr[pl.ds(r0, SUBLANES), :] = nr
                    lbi[pl.ds(r0, SUBLANES), :] = ni
                    x0 = pl.multiple_of(r0 + SUBLANES, SUBLANES) if reverse else r0
                    xpr, xpi = xbr[pl.ds(x0, SUBLANES), :], xbi[pl.ds(x0, SUBLANES), :]
                    return nr, ni, cc[2] + nr * xpr + ni * xpi, cc[3] + ni * xpr - nr * xpi

                carry = lax.fori_loop(0, gpc, group, carry)
                if store:
                    lr_ref[rows, :] = lbr[...].astype(BF16)
                    li_ref[rows, :] = lbi[...].astype(BF16)
                return carry

            return lax.fori_loop(0, n_ch, chunk, h0)

        zero = jnp.zeros((SUBLANES, w), F32)
        er, ei = sweep((zero, zero), False)
        pr, pi = _cpow(ar_ref[...], -ai_ref[...], steps)
        cr, ci0 = _segment_carry(er, ei, pr, pi, back)
        _, _, dar, dai = sweep((cr, ci0, zero, zero), True)
        dar_ref[...] = jnp.sum(dar, axis=0, keepdims=True)
        dai_ref[...] = jnp.sum(dai, axis=0, keepdims=True)

    col = lambda i: (0, i)
    return pl.pallas_call(
        body, name=name, grid=(s // w,),
        in_specs=[pl.BlockSpec((t, c), lambda i: (0, 0)), pl.BlockSpec((w, c), lambda i: (i, 0)),
                  pl.BlockSpec((w, c), lambda i: (i, 0)), pl.BlockSpec((t, w), col), pl.BlockSpec((t, w), col),
                  pl.BlockSpec((1, w), col), pl.BlockSpec((1, w), col)],
        out_specs=[pl.BlockSpec((t, w), col), pl.BlockSpec((t, w), col), pl.BlockSpec((1, w), col), pl.BlockSpec((1, w), col)],
        out_shape=[_out(t, s, BF16), _out(t, s, BF16), _out(1, s, F32), _out(1, s, F32)],
        scratch_shapes=[pltpu.VMEM((ch, w), F32)] * 4 + [pltpu.VMEM((ch + SUBLANES, w), F32)] * 2,
        compiler_params=_params(("parallel",)),
    )(*_hbm(dyp, c_re, c_im, xr, xi, ar, ai))


def _to_segments(a):
    t, c = a.shape
    return a.reshape(SUBLANES, t // SUBLANES, c).transpose(1, 0, 2).reshape(t, c)


def _from_segments(a):
    t, c = a.shape
    return a.reshape(t // SUBLANES, SUBLANES, c).transpose(1, 0, 2).reshape(t, c)


def _colsum_prod(name, a, b, b_coff=0):
    t, n = a.shape
    tm = _pick(t, 512, SUBLANES)

    def body(a_ref, b_ref, o_ref):
        @pl.when(pl.program_id(0) == 0)
        def _():
            o_ref[...] = jnp.zeros_like(o_ref)

        o_ref[...] += jnp.sum(a_ref[...].astype(F32) * b_ref[...].astype(F32), axis=0, keepdims=True)

    return pl.pallas_call(
        body, name=name, grid=(t // tm,),
        in_specs=[pl.BlockSpec((tm, n), lambda i: (i, 0)), pl.BlockSpec((tm, n), lambda i: (i, b_coff))],
        out_specs=pl.BlockSpec((1, n), lambda i: (0, 0)), out_shape=_out(1, n, F32),
        compiler_params=_params(("arbitrary",)),
    )(*_hbm(a, b))


def _bd_in(bb, g, p, hh):
    blk = bb.reshape(g, p, hh).transpose(0, 2, 1)
    eye = jnp.eye(g, dtype=bool)[:, None, :, None]
    return jnp.where(eye, blk[:, :, None, :], 0.0).reshape(g * hh, g * p)


def _bd_out(cc, g, p, hh):
    blk = cc.transpose(0, 2, 1)
    eye = jnp.eye(g, dtype=bool)[:, None, :, None]
    return jnp.where(eye, blk[:, :, None, :], 0.0).reshape(g * p, g * hh)


def _diag_in(dmat, g, p, hh):
    eye = jnp.eye(g, dtype=bool)[:, None, :, None]
    diag = jnp.sum(jnp.where(eye, dmat.reshape(g, hh, g, p), 0.0), axis=2)
    return diag.transpose(0, 2, 1).reshape(g * p, hh)


def _diag_out(dmat, g, p, hh):
    eye = jnp.eye(g, dtype=bool)[:, None, :, None]
    diag = jnp.sum(jnp.where(eye, dmat.reshape(g, p, g, hh), 0.0), axis=2)
    return diag.transpose(0, 2, 1)


def _softmax(qh, kh, scale):
    s = lax.dot_general(qh, kh, _DIMS["nt"], preferred_element_type=F32) * scale
    e = jnp.exp(s - jnp.max(s, axis=-1, keepdims=True))
    return e / jnp.sum(e, axis=-1, keepdims=True)


def _attn_fwd(q, kv):
    t, d = q.shape
    mm_ = kv.shape[0]
    hd = d // N_XHEADS
    scale = 1.0 / math.sqrt(hd)
    tm = _pick(t, 512, SUBLANES)

    def body(q_ref, kv_ref, o_ref):
        for h in range(N_XHEADS):
            sl = pl.ds(h * hd, hd)
            p = _softmax(q_ref[:, sl], kv_ref[:, sl], scale)
            o_ref[:, sl] = jnp.dot(p.astype(BF16), kv_ref[:, pl.ds(d + h * hd, hd)],
                                   preferred_element_type=F32).astype(BF16)

    return pl.pallas_call(
        body, name="attn_fwd", grid=(t // tm,),
        in_specs=[pl.BlockSpec((tm, d), lambda i: (i, 0)), pl.BlockSpec((mm_, 2 * d), lambda i: (0, 0))],
        out_specs=pl.BlockSpec((tm, d), lambda i: (i, 0)), out_shape=_out(t, d, BF16),
        compiler_params=_params(("parallel",)),
    )(*_hbm(q, kv))


def _attn_bwd(q, kv, do):
    t, d = q.shape
    mm_ = kv.shape[0]
    hd = d // N_XHEADS
    scale = 1.0 / math.sqrt(hd)
    tm = _pick(t, 512, SUBLANES)

    def body(q_ref, kv_ref, do_ref, dq_ref, dkv_ref):
        @pl.when(pl.program_id(0) == 0)
        def _():
            dkv_ref[...] = jnp.zeros_like(dkv_ref)

        for h in range(N_XHEADS):
            sl = pl.ds(h * hd, hd)
            vsl = pl.ds(d + h * hd, hd)
            qh, kh, doh = q_ref[:, sl], kv_ref[:, sl], do_ref[:, sl]
            p = _softmax(qh, kh, scale)
            dp = lax.dot_general(doh, kv_ref[:, vsl], _DIMS["nt"], preferred_element_type=F32)
            dkv_ref[:, vsl] += lax.dot_general(p.astype(BF16), doh, _DIMS["tn"], preferred_element_type=F32)
            ds = (p * (dp - jnp.sum(dp * p, axis=-1, keepdims=True)) * scale).astype(BF16)
            dq_ref[:, sl] = jnp.dot(ds, kh, preferred_element_type=F32).astype(BF16)
            dkv_ref[:, sl] += lax.dot_general(ds, qh, _DIMS["tn"], preferred_element_type=F32)

    row = pl.BlockSpec((tm, d), lambda i: (i, 0))
    full = pl.BlockSpec((mm_, 2 * d), lambda i: (0, 0))
    return pl.pallas_call(
        body, name="attn_bwd", grid=(t // tm,), in_specs=[row, full, row], out_specs=[row, full],
        out_shape=[_out(t, d, BF16), _out(mm_, 2 * d, F32)], compiler_params=_params(("arbitrary",)),
    )(*_hbm(q, kv, do))


def _ew(name, fn, ins, outs, rows_pref=256, rowvecs=()):
    r, c = ins[0].shape
    tr = _pick(r, rows_pref, SUBLANES)
    ni = len(ins) + len(rowvecs)

    def body(*refs):
        res = fn(*[x[...] for x in refs[:ni]])
        for o_ref, v in zip(refs[ni:], res):
            o_ref[...] = v.astype(o_ref.dtype)

    blk = pl.BlockSpec((tr, c), lambda i: (i, 0))
    vec = pl.BlockSpec((1, c), lambda i: (0, 0))
    return pl.pallas_call(
        body, name=name, grid=(r // tr,), in_specs=[blk] * len(ins) + [vec] * len(rowvecs), out_specs=[blk] * len(outs),
        out_shape=[_out(r, c, dt) for dt in outs], compiler_params=_params(("parallel",)),
    )(*_hbm(*ins, *rowvecs))


def _sum_slots(name, a, dtype):
    s, r, c = a.shape
    tr = _pick(r, 256, SUBLANES)

    def body(a_ref, o_ref):
        acc = a_ref[0].astype(F32)
        for k in range(1, s):
            acc = acc + a_ref[k].astype(F32)
        o_ref[...] = acc.astype(o_ref.dtype)

    return pl.pallas_call(
        body, name=name, grid=(r // tr,), in_specs=[pl.BlockSpec((s, tr, c), lambda i: (0, i, 0))],
        out_specs=pl.BlockSpec((tr, c), lambda i: (i, 0)), out_shape=_out(r, c, dtype),
        compiler_params=_params(("parallel",)),
    )(*_hbm(a))


def _adamw(name, w, g, m, v):
    bc1 = 1.0 - ADAM_B1 ** ADAM_STEP
    bc2 = 1.0 - ADAM_B2 ** ADAM_STEP

    def fn(wv, gv, mv, vv):
        m2 = ADAM_B1 * mv + (1.0 - ADAM_B1) * gv
        v2 = ADAM_B2 * vv + (1.0 - ADAM_B2) * (gv * gv)
        delta = -ADAM_LR * ((m2 / bc1) / (jnp.sqrt(v2 / bc2) + ADAM_EPS) + ADAM_WD * wv)
        return delta, m2, v2

    return _ew(name, fn, [w, g, m, v], [F32, F32, F32])


def _allgather(name, arrs):
    n = len(arrs)

    def body(*refs):
        ins, outs = refs[:n], refs[n:2 * n]
        send_sems, recv_sems, local_sems = refs[2 * n:]
        x, y, c = lax.axis_index("x"), lax.axis_index("y"), lax.axis_index("c")
        me, sibling = (x, y, c), (x, y, 1 - c)
        chips = [(1 - x, y), (x, 1 - y), (1 - x, 1 - y)]

        def rows(a, px, py, pc):
            r = ins[a].shape[0]
            return outs[a].at[pl.ds((4 * px + 2 * py + pc) * r, r), :]

        def copy(a, k, block, to, src=None):
            return pltpu.make_async_remote_copy(
                src_ref=rows(a, *block) if src is None else src, dst_ref=rows(a, *block),
                send_sem=send_sems.at[a, k], recv_sem=recv_sems.at[a, k], device_id=to, device_id_type=MESH)

        mine = [pltpu.make_async_copy(ins[a], rows(a, *me), local_sems.at[a]) for a in range(n)]
        for cp in mine:
            cp.start()
        first = []
        for a in range(n):
            first.append(copy(a, 0, me, sibling, src=ins[a]))
            first += [copy(a, 1 + j, me, (*chip, c), src=ins[a]) for j, chip in enumerate(chips)]
        for cp in first:
            cp.start()
        passed = []
        for j, chip in enumerate(chips):
            for a in range(n):
                copy(a, 1 + j, (*chip, c), me).wait_recv()
                cp = copy(a, 4 + j, (*chip, c), sibling)
                cp.start()
                passed.append(cp)
        for a in range(n):
            copy(a, 0, sibling, me).wait_recv()
            for j, chip in enumerate(chips):
                copy(a, 4 + j, (*chip, 1 - c), me).wait_recv()
        for cp in first + passed:
            cp.wait_send()
        for cp in mine:
            cp.wait()

    return pl.pallas_call(
        body, name=name, in_specs=[ANY] * n, out_specs=[ANY] * n,
        out_shape=[_out(N_DEV * a.shape[0], a.shape[1], a.dtype) for a in arrs],
        scratch_shapes=[pltpu.SemaphoreType.DMA((n, 7)), pltpu.SemaphoreType.DMA((n, 7)), pltpu.SemaphoreType.DMA((n,))],
    )(*arrs)


def _exchange_cores(name, g):
    _, r, c = g.shape
    nck = r // GRAD_ROW_TILE

    def body(g_ref, recv_ref, send_sems, recv_sems):
        x, y, cc = lax.axis_index("x"), lax.axis_index("y"), lax.axis_index("c")
        copies = []
        for q in range(4):
            for k in range(nck):
                rows = pl.ds(k * GRAD_ROW_TILE, GRAD_ROW_TILE)
                copies.append(pltpu.make_async_remote_copy(
                    src_ref=g_ref.at[2 * q + (1 - cc), rows], dst_ref=recv_ref.at[q, rows],
                    send_sem=send_sems.at[q, k], recv_sem=recv_sems.at[q, k], device_id=(x, y, 1 - cc),
                    device_id_type=MESH))
        for cp in copies:
            cp.start()
        for cp in copies:
            cp.wait()

    return pl.pallas_call(
        body, name=name, in_specs=[ANY], out_specs=ANY,
        out_shape=jax.ShapeDtypeStruct((4, r, c), g.dtype),
        scratch_shapes=[pltpu.SemaphoreType.DMA((4, nck)), pltpu.SemaphoreType.DMA((4, nck))],
    )(g)


def _pair_sum(name, g, recv, core):
    _, r, c = g.shape
    tr = _pick(r, 5 * GRAD_ROW_TILE, GRAD_ROW_TILE)

    def body(core_ref, g_ref, r_ref, o_ref):
        o_ref[...] = (g_ref[...].astype(F32) + r_ref[...].astype(F32)).astype(o_ref.dtype)

    blk = pl.BlockSpec((None, tr, c), lambda q, i, core_ref: (q, i, 0))
    return pl.pallas_call(
        body, name=name,
        grid_spec=pltpu.PrefetchScalarGridSpec(
            num_scalar_prefetch=1, grid=(4, r // tr),
            in_specs=[pl.BlockSpec((None, tr, c), lambda q, i, core_ref: (2 * q + core_ref[0], i, 0)), blk],
            out_specs=blk),
        out_shape=jax.ShapeDtypeStruct((4, r, c), g.dtype), compiler_params=_params(("parallel", "parallel")),
    )(core, *_hbm(g, recv))


def _peer(k, x, y, c):
    return (1 - x if k & 4 else x, 1 - y if k & 2 else y, 1 - c if k & 1 else c)


def _split_start(name, srcs, land_shapes, n_remote, n_local, build, after=None):
    ns, nl = len(srcs), len(land_shapes)
    n_sem = 3 if n_local else 2
    pins = [] if after is None else [after]

    def body(*refs):
        src_refs, land_refs = refs[:ns], refs[ns:ns + nl]
        sems = refs[ns + nl + len(pins):ns + nl + len(pins) + n_sem]
        token = refs[-1]
        remote, local = build(src_refs, land_refs, *sems)
        for cp in local + remote:
            cp.start()
        token[...] = jnp.zeros_like(token)

    sem_shapes = [pltpu.SemaphoreType.DMA((n_remote,)), pltpu.SemaphoreType.DMA((n_remote,))]
    if n_local:
        sem_shapes.append(pltpu.SemaphoreType.DMA((n_local,)))
    bufs = [pltpu.with_memory_space_constraint(a, pltpu.HBM) for a in srcs]
    bufs += [pltpu.with_memory_space_constraint(lax.empty(s.shape, s.dtype), pltpu.HBM) for s in land_shapes]
    outs = pl.pallas_call(
        body, name=name,
        out_shape=sem_shapes + [pltpu.HBM(b.shape, b.dtype) for b in bufs] + [jax.ShapeDtypeStruct((SUBLANES, LANES), F32)],
        in_specs=[HBM] * (ns + nl) + [ANY] * len(pins),
        out_specs=[SEM] * n_sem + [HBM] * (ns + nl) + [pl.BlockSpec(memory_space=pltpu.VMEM)],
        input_output_aliases={i: n_sem + i for i in range(ns + nl)},
        compiler_params=pltpu.CompilerParams(has_side_effects=SIDE_EFFECT),
    )(*bufs, *pins)
    return dict(sems=list(outs[:n_sem]), bufs=list(outs[n_sem:n_sem + ns + nl]), token=outs[-1], build=build, ns=ns)


def _split_wait(name, started, after):
    ns, n_buf, n_sem = started["ns"], len(started["bufs"]), len(started["sems"])

    def body(*refs):
        src_refs, land_refs = refs[:ns], refs[ns:n_buf]
        sems = refs[n_buf:n_buf + n_sem]
        remote, local = started["build"](src_refs, land_refs, *sems)
        for cp in local:
            cp.wait()
        for cp in remote:
            cp.wait_send()
            cp.wait_recv()

    outs = pl.pallas_call(
        body, name=name, out_shape=[pltpu.HBM(b.shape, b.dtype) for b in started["bufs"]],
        in_specs=[HBM] * n_buf + [SEM] * n_sem + [ANY], out_specs=[HBM] * n_buf,
        input_output_aliases={i: i for i in range(n_buf)},
        compiler_params=pltpu.CompilerParams(has_side_effects=SIDE_EFFECT),
    )(*started["bufs"], *started["sems"], after)
    return list(outs[:ns]), list(outs[ns:])


def _gather_start(name, shards, after):
    m = len(shards)

    def build(src_refs, land_refs, send_sems, recv_sems, local_sems):
        x, y, c = lax.axis_index("x"), lax.axis_index("y"), lax.axis_index("c")
        remote, local = [], []
        for j in range(m):
            r = src_refs[j].shape[0]
            dst = land_refs[j].at[pl.ds((4 * x + 2 * y + c) * r, r), :]
            local.append(pltpu.make_async_copy(src_refs[j], dst, local_sems.at[j]))
            for k in range(1, N_DEV):
                remote.append(pltpu.make_async_remote_copy(
                    src_ref=src_refs[j], dst_ref=dst, send_sem=send_sems.at[7 * j + k - 1],
                    recv_sem=recv_sems.at[7 * j + k - 1], device_id=_peer(k, x, y, c), device_id_type=MESH))
        return remote, local

    lands = [jax.ShapeDtypeStruct((N_DEV * a.shape[0], a.shape[1]), a.dtype) for a in shards]
    return _split_start(name, shards, lands, 7 * m, m, build, after)


def _slots_start(name, a):
    def build(src_refs, land_refs, send_sems, recv_sems, local_sems):
        x, y, c = lax.axis_index("x"), lax.axis_index("y"), lax.axis_index("c")
        dst = land_refs[0].at[4 * x + 2 * y + c]
        local = [pltpu.make_async_copy(src_refs[0], dst, local_sems.at[0])]
        remote = [pltpu.make_async_remote_copy(
            src_ref=src_refs[0], dst_ref=dst, send_sem=send_sems.at[k - 1], recv_sem=recv_sems.at[k - 1],
            device_id=_peer(k, x, y, c), device_id_type=MESH) for k in range(1, N_DEV)]
        return remote, local

    return _split_start(name, [a], [jax.ShapeDtypeStruct((N_DEV,) + a.shape, a.dtype)], 7, 1, build)


def _chips_start(name, p):
    _, r, c = p.shape
    nck = r // GRAD_ROW_TILE

    def build(src_refs, land_refs, send_sems, recv_sems):
        x, y, cc = lax.axis_index("x"), lax.axis_index("y"), lax.axis_index("c")
        remote = []
        for k in range(1, 4):
            px = 1 - x if k >> 1 else x
            py = 1 - y if k & 1 else y
            for j in range(nck):
                rows = pl.ds(j * GRAD_ROW_TILE, GRAD_ROW_TILE)
                remote.append(pltpu.make_async_remote_copy(
                    src_ref=src_refs[0].at[2 * px + py, rows], dst_ref=land_refs[0].at[k - 1, rows],
                    send_sem=send_sems.at[(k - 1) * nck + j], recv_sem=recv_sems.at[(k - 1) * nck + j],
                    device_id=(px, py, cc), device_id_type=MESH))
        return remote, []

    return _split_start(name, [p], [jax.ShapeDtypeStruct((3, r, c), p.dtype)], 3 * nck, 0, build)


def _chip_sum(name, p, recv, chip):
    _, r, c = p.shape
    tr = _pick(r, 5 * GRAD_ROW_TILE, GRAD_ROW_TILE)

    def body(chip_ref, p_ref, r_ref, o_ref):
        acc = p_ref[...].astype(F32)
        for k in range(3):
            acc = acc + r_ref[k].astype(F32)
        o_ref[...] = acc

    return pl.pallas_call(
        body, name=name,
        grid_spec=pltpu.PrefetchScalarGridSpec(
            num_scalar_prefetch=1, grid=(r // tr,),
            in_specs=[pl.BlockSpec((None, tr, c), lambda i, chip_ref: (chip_ref[0], i, 0)),
                      pl.BlockSpec((3, tr, c), lambda i, chip_ref: (0, i, 0))],
            out_specs=pl.BlockSpec((tr, c), lambda i, chip_ref: (i, 0))),
        out_shape=_out(r, c, F32), compiler_params=_params(("parallel",)),
    )(chip, *_hbm(p, recv))


def _local_step(x, mem, tgt, wt, sm, ev=None):
    t, d = x.shape
    n_mem = mem.shape[0]
    d_pool = sm["pool_scale"].shape[1]
    ng, pc = sm["pool_w"].shape[0], sm["pool_w"].shape[1]
    d_ssm = sm["ssm_d"].shape[1]
    _, sg, sp, sh = sm["ssm_b_re"].shape
    n_state = sg * sp
    gb, gs = {}, {}

    def emit(name, **kw):
        return ev(name, **kw) if ev is not None else None

    n1 = _rms_fwd("ffn1_norm", x, sm["ffn1_norm"])
    emit("ffn1_norm_done", marker=n1)
    def ffn1_down(hid):
        emit("ffn1_up_done", marker=hid)
        return wt["ffn1_w_down"]

    h1, ffn1_saved = _ffn_fwd("ffn1", x, n1, wt["ffn1_w_gate"], wt["ffn1_w_up"], ffn1_down)
    emit("ffn1_fwd_done", marker=h1)
    u = _rms_fwd("mix_norm", h1, sm["mix_norm"])
    d_in = wt["w_in"].shape[0]
    tm, tn = _pick(t, 1024), _pick(d_in, 1408)
    proj = _mm1("in_proj", "nt", u, wt["w_in"], t, d_in, tm, tn, F32)
    off_s = d_pool // d_ssm
    off_gp = (d_pool + d_ssm)
    off_gs = off_gp + d

    pool_w_bf = sm["pool_w"].astype(BF16)
    pooled, pm = _pool_fwd(proj, pool_w_bf, sm["pool_scale"])

    cols = [sm["ssm_a_re"].reshape(-1, 1), sm["ssm_a_im"].reshape(-1, 1),
            jnp.broadcast_to(sm["ssm_log_dt"][:, :, None], (2, sg, sp)).reshape(-1, 1),
            sm["ssm_b_re"].reshape(-1, sh), sm["ssm_b_im"].reshape(-1, sh)]
    abr, abi, bbr, bbi = _ssm_disc(cols)
    abr2, abi2 = abr.reshape(2, n_state), abi.reshape(2, n_state)
    bbr4, bbi4 = bbr.reshape(2, sg * sp, sh), bbi.reshape(2, sg * sp, sh)
    b_re = [_bd_in(bbr4[dr], sg, sp, sh).astype(BF16) for dr in range(2)]
    b_im = [_bd_in(bbi4[dr], sg, sp, sh).astype(BF16) for dr in range(2)]
    c_re = [_bd_out(sm["ssm_c_re"][dr], sg, sp, sh).astype(BF16) for dr in range(2)]
    c_im = [_bd_out(-sm["ssm_c_im"][dr], sg, sp, sh).astype(BF16) for dr in range(2)]
    sp32 = _to_segments(proj[:, d_pool:d_pool + d_ssm])
    xs, y_parts = [], []
    for dr in range(2):
        xr, xi, y_part = _ssm_fwd(f"ssm_fwd{dr}", sp32, b_re[dr], b_im[dr], c_re[dr], c_im[dr], abr2[dr:dr + 1],
                                  abi2[dr:dr + 1], reverse=(dr == 1))
        xs.append((xr, xi))
        y_parts.append(y_part)
    y = _from_segments(_ew("ssm_sum", lambda p0, p1, sv, dv: (p0 + p1 + sv * dv,), y_parts + [sp32], [F32],
                           rowvecs=[sm["ssm_d"]])[0])
    tmy = _pick(t, 256)
    ys = _ew("ssm_gelu", lambda v: (jax.nn.gelu(v),), [y], [BF16])[0]
    emit("mix_in_done", marker=ys)

    tmm, tnm, tnx = _pick(t, 1024), _pick(d, 256), _pick(d, 512)
    gp_spec = _tile(tmm, tnm, off_gp // tnm)
    gs_spec = _tile(tmm, tnm, off_gs // tnm)

    def merge_epi(accs, gpv, gsv):
        z_pool, val, gate = accs
        return (jax.nn.sigmoid(gpv) * z_pool + jax.nn.sigmoid(gsv) * (val * jax.nn.sigmoid(gate)),)

    merged = _mm("mix_merge", "nt", [pm, ys], [wt["w_pool_proj"], wt["w_glu_val"], wt["w_glu_gate"]],
                 [[(0, 0)], [(1, 1)], [(1, 2)]], t, d, tmm, tnm, [(proj, gp_spec), (proj, gs_spec)], merge_epi,
                 [(_out(t, d, BF16), None)])[0]
    res_epi = lambda accs, hin: (hin + accs[0],)
    h2 = _mm("mix_out", "nn", [merged], [wt["w_mix_out"]], [[(0, 0)]], t, d, tmm, tnx, [(h1, _tile(tmm, tnx))],
             res_epi, [(_out(t, d, F32), None)])[0]

    un = _rms_fwd("xattn_norm", h2, sm["xattn_norm"])
    mn = _rms_fwd("mem_norm", mem, sm["mem_norm"])
    emit("mix_done", marker=un)
    q = _mm1("xattn_q", "nn", un, wt["w_q"], t, d, tmm, tnx, BF16)
    kv = _mm1("xattn_kv", "nt", mn, wt["w_kv"], n_mem, 2 * d, n_mem, _pick(2 * d, 512), BF16)
    o = _attn_fwd(q, kv)
    h3 = _mm("xattn_out", "nn", [o], [wt["w_xo"]], [[(0, 0)]], t, d, tmm, tnx, [(h2, _tile(tmm, tnx))],
             res_epi, [(_out(t, d, F32), None)])[0]

    n2 = _rms_fwd("ffn2_norm", h3, sm["ffn2_norm"])
    emit("xattn_done", marker=n2)
    h4, ffn2_saved = _ffn_fwd("ffn2", h3, n2, wt["ffn2_w_gate"], wt["ffn2_w_up"], wt["ffn2_w_down"])

    dh4, dh4_bf, gs["final_norm"], loss = _loss_head(h4, sm["final_norm"], tgt)
    dh3, dh3_bf, gs["ffn2_norm"], gb["ffn2_w_gate"], gb["ffn2_w_up"], gb["ffn2_w_down"] = _ffn_bwd(
        "ffn2", h3, sm["ffn2_norm"], wt["ffn2_w_gate"], wt["ffn2_w_up"], wt["ffn2_w_down"], ffn2_saved, dh4, dh4_bf)

    tw = _pick(d, 1024)
    do = _mm1("xattn_do", "nt", dh3_bf, wt["w_xo"], t, d, tmm, tnx, BF16)
    gb["w_xo"] = _mm1("xattn_dwxo", "tn", o, dh3_bf, d, d, tw, tnx, BF16)
    dq, dkv = _attn_bwd(q, kv, do)
    gb["w_q"] = _mm1("xattn_dwq", "tn", un, dq, d, d, tw, tnx, BF16)
    dun = _mm1("xattn_dun", "nt", dq, wt["w_q"], t, d, tmm, tnx, F32)
    dh2, dh2_bf, gs["xattn_norm"] = _rms_bwd("xattn_norm_bwd", h2, sm["xattn_norm"], dun, dh3)
    gb["w_kv"] = _mm1("xattn_dwkv", "tn", dkv, mn, 2 * d, d, _pick(2 * d, 512), d, BF16)
    dmn = _mm1("xattn_dmn", "nn", dkv, wt["w_kv"], n_mem, d, n_mem, tnx, F32)
    gs["mem_norm"] = _rms_bwd("mem_norm_bwd", mem, sm["mem_norm"], dmn)

    gb["w_mix_out"] = _mm1("mix_dwout", "tn", merged, dh2_bf, d, d, tw, tnx, BF16)

    def merge_bwd_epi(accs, gpv, gsv):
        dmerged, z_pool, val, gate = accs
        sp_, ss_, sg_ = jax.nn.sigmoid(gpv), jax.nn.sigmoid(gsv), jax.nn.sigmoid(gate)
        glu = val * sg_
        dz_pool = dmerged * sp_
        dg_pool = dmerged * z_pool * (sp_ * (1.0 - sp_))
        dz_ssm = dmerged * ss_
        dg_ssm = dmerged * glu * (ss_ * (1.0 - ss_))
        dval = dz_ssm * sg_
        dgate = dz_ssm * glu * (1.0 - sg_)
        return dz_pool, dg_pool, dg_ssm, dval, dgate

    dz_pool, dg_pool, dg_ssm, dval, dgate = _mm(
        "mix_merge_bwd", "nt", [dh2_bf, pm, ys], [wt["w_mix_out"], wt["w_pool_proj"], wt["w_glu_val"], wt["w_glu_gate"]],
        [[(0, 0)], [(1, 1)], [(2, 2)], [(2, 3)]], t, d, tmm, tnm, [(proj, gp_spec), (proj, gs_spec)], merge_bwd_epi,
        [(_out(t, d, BF16), None)] * 5)
    gb["w_pool_proj"] = _mm1("pool_dwproj", "tn", dz_pool, pm, d, d_pool, tw, d_pool, BF16)
    gb["w_glu_val"] = _mm1("glu_dwval", "tn", dval, ys, d, d_ssm, tw, d_ssm, BF16)
    gb["w_glu_gate"] = _mm1("glu_dwgate", "tn", dgate, ys, d, d_ssm, tw, d_ssm, BF16)

    def gelu_bwd_epi(accs, yv):
        _, vjp = jax.vjp(jax.nn.gelu, yv)
        return (vjp(accs[0])[0],)

    dy = _mm("glu_dy", "nn", [dval, dgate], [wt["w_glu_val"], wt["w_glu_gate"]], [[(0, 0), (1, 1)]], t, d_ssm, tmy, d_ssm,
             [(y, _tile(tmy, d_ssm))], gelu_bwd_epi, [(_out(t, d_ssm, F32), None)])[0]
    gs["ssm_d"] = _colsum_prod("ssm_dd", dy, proj, b_coff=off_s)
    dyp = _to_segments(dy)
    d_abr, d_abi, d_bbr, d_bbi, d_cre, d_cim, lams = [], [], [], [], [], [], []
    ts = _pick(n_state, 512)
    tc_ = _pick(n_state, 256)
    both = lambda accs: tuple(accs)
    for dr in range(2):
        lr, li, dar, dai = _ssm_bwd(f"ssm_bwd{dr}", dyp, c_re[dr], c_im[dr], xs[dr][0], xs[dr][1], abr2[dr:dr + 1],
                                    abi2[dr:dr + 1], reverse=(dr == 1))
        d_abr.append(dar)
        d_abi.append(dai)
        lams += [lr, li]
        d_br, d_bi = _mm(f"ssm_db{dr}", "tn", [sp32], [lr, li], [[(0, 0)], [(0, 1)]], d_ssm, n_state, d_ssm, ts, [], both,
                         [(_out(d_ssm, n_state, F32), None)] * 2)
        d_bbr.append(_diag_in(d_br, sg, sp, sh))
        d_bbi.append(_diag_in(d_bi, sg, sp, sh))
        d_cr, d_ci = _mm(f"ssm_dc{dr}", "tn", [xs[dr][0], xs[dr][1]], [dyp], [[(0, 0)], [(1, 0)]], n_state, d_ssm, tc_,
                         d_ssm, [], both, [(_out(n_state, d_ssm, F32), None)] * 2)
        d_cre.append(_diag_out(d_cr, sg, sp, sh))
        d_cim.append(-_diag_out(d_ci, sg, sp, sh))
    ds = _from_segments(_mm(
        "ssm_ds", "nt", lams, [b_re[0], b_im[0], b_re[1], b_im[1]], [[(k, k) for k in range(4)]], t, d_ssm, tmy,
        d_ssm, [(dyp, _tile(tmy, d_ssm)), (sm["ssm_d"], _rowvec(d_ssm))],
        lambda accs, dyv, dv: (dyv * dv + accs[0],), [(_out(t, d_ssm, BF16), None)])[0])
    cots = [jnp.concatenate(d_abr, axis=0).reshape(-1, 1), jnp.concatenate(d_abi, axis=0).reshape(-1, 1),
            jnp.concatenate(d_bbr, axis=0), jnp.concatenate(d_bbi, axis=0)]
    d_are, d_aim, d_ldt, d_bre, d_bim = _ssm_disc_bwd(cols, cots)
    gs["ssm_a_re"] = d_are.reshape(2, sg, sp)
    gs["ssm_a_im"] = d_aim.reshape(2, sg, sp)
    gs["ssm_log_dt"] = _rowsum("ssm_dlogdt", d_ldt.reshape(2 * sg, sp)).reshape(2, sg)
    gs["ssm_b_re"] = d_bre.reshape(2, sg, sp, sh)
    gs["ssm_b_im"] = d_bim.reshape(2, sg, sp, sh)
    gs["ssm_c_re"] = jnp.stack(d_cre, axis=0)
    gs["ssm_c_im"] = jnp.stack(d_cim, axis=0)

    dpm = _mm1("pool_dpm", "nn", dz_pool, wt["w_pool_proj"], t, d_pool, tmm, _pick(d_pool, 256), F32)
    dp, gs["pool_w"], gs["pool_scale"] = _pool_bwd(pooled, dpm, pool_w_bf, sm["pool_scale"])

    w_in = wt["w_in"]
    parts = [(dp, 0, d_pool), (ds, d_pool, d_ssm), (dg_pool, off_gp, d), (dg_ssm, off_gs, d)]
    w_in_parts = [w_in[o0:o0 + width] for _, o0, width in parts]
    gb["w_in"] = jnp.concatenate(
        [_mm1(f"in_proj_dw{k}", "tn", p_[0], u, p_[2], d, _pick(p_[2], 1024), tnx, BF16) for k, p_ in enumerate(parts)], axis=0)
    pin = emit("grads_main", gb=gb)
    du = _mm("in_proj_du", "nn", [p_[0] for p_ in parts], w_in_parts, [[(k, k) for k in range(4)]], t, d, tmm, tnx, [],
             lambda accs: (accs[0],), [(_out(t, d, F32), None)], after=pin)[0]
    dh1, dh1_bf, gs["mix_norm"] = _rms_bwd("mix_norm_bwd", h1, sm["mix_norm"], du, dh2)
    pin = emit("small_early", gs=gs, loss=loss)

    def ffn1_weights_done(d_wg, d_wu, d_wd):
        gb["ffn1_w_gate"], gb["ffn1_w_up"], gb["ffn1_w_down"] = d_wg, d_wu, d_wd
        return emit("grads_ffn1", gb=gb)

    dx, _, gs["ffn1_norm"], _, _, _ = _ffn_bwd(
        "ffn1", x, sm["ffn1_norm"], wt["ffn1_w_gate"], wt["ffn1_w_up"], wt["ffn1_w_down"], ffn1_saved, dh1, dh1_bf,
        weights_done=ffn1_weights_done, after=pin)
    return loss, dx, gb, gs


WEIGHTS = ["ffn1_norm", "ffn1_w_gate", "ffn1_w_up", "ffn1_w_down", "mix_norm", "w_in", "pool_w", "pool_scale",
           "w_pool_proj", "ssm_a_re", "ssm_a_im", "ssm_log_dt", "ssm_b_re", "ssm_b_im", "ssm_c_re", "ssm_c_im", "ssm_d",
           "w_glu_val", "w_glu_gate", "w_mix_out", "xattn_norm", "mem_norm", "w_q", "w_kv", "w_xo", "ffn2_norm",
           "ffn2_w_gate", "ffn2_w_up", "ffn2_w_down", "final_norm"]
COL_SHARDED = ["ffn1_w_gate", "ffn1_w_up", "w_in", "w_pool_proj", "w_glu_val", "w_glu_gate", "w_kv", "ffn2_w_gate",
               "ffn2_w_up"]
ROW_SHARDED = ["ffn1_w_down", "w_mix_out", "w_q", "w_xo", "ffn2_w_down"]
BIG = [n for n in WEIGHTS if n in COL_SHARDED or n in ROW_SHARDED]
SMALL = [n for n in WEIGHTS if n not in BIG]
FFN1_BIG = ["ffn1_w_gate", "ffn1_w_up", "ffn1_w_down"]
MAIN_BIG = [n for n in BIG if n not in FFN1_BIG]
GATHER_PLAN = [("ffn1_up_done", ["ffn1_w_down"]), ("ffn1_fwd_done", ["w_in"]),
               ("mix_in_done", ["w_pool_proj", "w_glu_val", "w_glu_gate", "w_mix_out"]),
               ("mix_done", ["w_q", "w_kv", "w_xo"]), ("xattn_done", ["ffn2_w_gate", "ffn2_w_up", "ffn2_w_down"])]
LATE_SMALL = "ffn1_norm"
EARLY_SMALL = [n for n in SMALL if n != LATE_SMALL]
PACK_ROWS = SUBLANES * LANES
GRAD_ROW_TILE = 256


def _to_rows(name, w, width):
    if name in COL_SHARDED:
        w = w.T
    return w.reshape(-1, width)


def _from_rows(name, rows, shard_shape):
    if name in COL_SHARDED:
        return rows.reshape(shard_shape[1], shard_shape[0]).T
    return rows.reshape(shard_shape)


def _pack_small(vals):
    flat = []
    for v in vals:
        f = v.reshape(-1)
        flat.append(jnp.pad(f, (0, (-f.shape[0]) % PACK_ROWS)))
    total = sum(f.shape[0] for f in flat)
    flat.append(jnp.zeros(((-total) % (GRAD_ROW_TILE * LANES),), F32))
    return jnp.concatenate(flat).reshape(-1, LANES)


def _unpack_small(packed, shapes):
    out, row = [], 0
    for shp in shapes:
        size = math.prod(shp)
        rows = -(-size // PACK_ROWS) * SUBLANES
        out.append(packed[row:row + rows].reshape(-1)[:size].reshape(shp))
        row += rows
    return out


def kernel(x, mem, ffn1_norm, ffn1_w_gate, ffn1_w_up, ffn1_w_down, mix_norm, w_in, pool_w, pool_scale, w_pool_proj, ssm_a_re, ssm_a_im, ssm_log_dt, ssm_b_re, ssm_b_im, ssm_c_re, ssm_c_im, ssm_d, w_glu_val, w_glu_gate, w_mix_out, xattn_norm, mem_norm, w_q, w_kv, w_xo, ffn2_norm, ffn2_w_gate, ffn2_w_up, ffn2_w_down, final_norm, loss_target, m_ffn1_norm, m_ffn1_w_gate, m_ffn1_w_up, m_ffn1_w_down, m_mix_norm, m_w_in, m_pool_w, m_pool_scale, m_w_pool_proj, m_ssm_a_re, m_ssm_a_im, m_ssm_log_dt, m_ssm_b_re, m_ssm_b_im, m_ssm_c_re, m_ssm_c_im, m_ssm_d, m_w_glu_val, m_w_glu_gate, m_w_mix_out, m_xattn_norm, m_mem_norm, m_w_q, m_w_kv, m_w_xo, m_ffn2_norm, m_ffn2_w_gate, m_ffn2_w_up, m_ffn2_w_down, m_final_norm, v_ffn1_norm, v_ffn1_w_gate, v_ffn1_w_up, v_ffn1_w_down, v_mix_norm, v_w_in, v_pool_w, v_pool_scale, v_w_pool_proj, v_ssm_a_re, v_ssm_a_im, v_ssm_log_dt, v_ssm_b_re, v_ssm_b_im, v_ssm_c_re, v_ssm_c_im, v_ssm_d, v_w_glu_val, v_w_glu_gate, v_w_mix_out, v_xattn_norm, v_mem_norm, v_w_q, v_w_kv, v_w_xo, v_ffn2_norm, v_ffn2_w_gate, v_ffn2_w_up, v_ffn2_w_down, v_final_norm):
    given = dict(locals())
    wts = {n: given[n] for n in WEIGHTS}
    moms = {n: (given["m_" + n], given["v_" + n]) for n in WEIGHTS}
    x2, mem2, tgt2 = x[0], mem[0], loss_target[0]
    d = x2.shape[1]
    core = lax.axis_index("c").astype(jnp.int32).reshape(1)
    chip = (2 * lax.axis_index("x") + lax.axis_index("y")).astype(jnp.int32).reshape(1)

    def full_form(n, f):
        shard = wts[n][0].shape
        return f.reshape(N_DEV * shard[1], shard[0]) if n in COL_SHARDED else f.reshape(N_DEV * shard[0], shard[1])

    shards = {n: _to_rows(n, wts[n][0], d).astype(BF16) for n in BIG}
    first = FFN1_BIG[:2]
    wt = {n: full_form(n, f) for n, f in zip(first, _allgather("weight_allgather_first", [shards[n] for n in first]))}
    gathers, after = {}, wt[first[0]]
    for event, names in GATHER_PLAN:
        gathers[event] = (names, _gather_start("weight_gather_start_" + event, [shards[n] for n in names], after))
        after = gathers[event][1]["token"]
    sm = {n: (wts[n].reshape(1, -1) if wts[n].ndim <= 2 else wts[n][0]) for n in SMALL}
    sm["ffn1_norm"] = sm["ffn1_norm"] + after[0, 0]

    pending = {}

    def reduce_start(tag, names, gb):
        blocks = [gb[n].reshape(N_DEV, -1, d) for n in names]
        pad_rows = (-sum(b.shape[1] for b in blocks)) % GRAD_ROW_TILE
        packed = jnp.concatenate(blocks + ([jnp.zeros((N_DEV, pad_rows, d), BF16)] if pad_rows else []), axis=1)
        pair = _pair_sum("grad_pair_sum_" + tag, packed, _exchange_cores("grad_exchange_cores_" + tag, packed), core)
        pending[tag] = (pair, _chips_start("grad_exchange_chips_start_" + tag, pair), [b.shape[1] for b in blocks])
        return pending[tag][1]["token"]

    def reduce_finish(tag, after):
        _, started, rows = pending[tag]
        (pair,), (recv,) = _split_wait("grad_exchange_chips_wait_" + tag, started, after)
        return _chip_sum("grad_chip_sum_" + tag, pair, recv, chip), rows

    def ev(name, gb=None, gs=None, loss=None, marker=None):
        if name in gathers:
            names, started = gathers[name]
            for n, f in zip(names, _split_wait("weight_gather_wait_" + name, started, marker)[1]):
                wt[n] = full_form(n, f)
        elif name == "grads_main":
            return reduce_start("main", MAIN_BIG, gb)
        elif name == "small_early":
            pending["small"] = _slots_start("small_gather_start", _pack_small([gs[n] for n in EARLY_SMALL] + [loss[:, :1]]))
            return pending["small"]["token"]
        elif name == "grads_ffn1":
            return reduce_start("ffn1", FFN1_BIG, gb)
        return None

    _, dx, _, gs = _local_step(x2, mem2, tgt2, wt, sm, ev)

    out_g, out_d, out_m, out_v = {}, {}, {}, {}

    def update(n, g_full):
        shape = wts[n].shape
        two_d = (-1, shape[-1])
        dl, m2, v2 = _adamw("adamw_" + n, wts[n].reshape(two_d), g_full.reshape(two_d), moms[n][0].reshape(two_d),
                            moms[n][1].reshape(two_d))
        out_g[n], out_d[n], out_m[n], out_v[n] = g_full, dl.reshape(shape), m2.reshape(shape), v2.reshape(shape)
        return dl

    def update_big(names, g_rows, rows):
        off = 0
        for n, r in zip(names, rows):
            shard = wts[n].shape
            dl = update(n, _from_rows(n, g_rows[off:off + r], shard[1:]).reshape(shard))
            off += r
        return dl

    last = update_big(MAIN_BIG, *reduce_finish("main", dx))

    small_sum = _sum_slots("small_sum", _split_wait("small_gather_wait", pending["small"], dx)[1][0], F32)
    late = _allgather("small_allgather_late", [gs[LATE_SMALL].reshape(-1, LANES)])[0]
    late_sum = _sum_slots("small_sum_late", late.reshape(N_DEV, -1, LANES), F32)
    vals = _unpack_small(small_sum, [wts[n].shape for n in EARLY_SMALL] + [(1, 1)])
    total_loss = vals[-1].reshape(())
    for n, g_full in zip(EARLY_SMALL + [LATE_SMALL], vals[:-1] + [late_sum.reshape(wts[LATE_SMALL].shape)]):
        update(n, g_full)

    update_big(FFN1_BIG, *reduce_finish("ffn1", last))

    return (total_loss, dx[None], *[out_g[n] for n in WEIGHTS], *[out_d[n] for n in WEIGHTS],
            *[out_m[n] for n in WEIGHTS], *[out_v[n] for n in WEIGHTS])
```

```python
import functools
import math

import jax
import jax.numpy as jnp
from jax import lax
from jax.experimental import pallas as pl
from jax.experimental.pallas import tpu as pltpu

F32 = jnp.float32
BF16 = jnp.bfloat16
EPS = 1e-6
N_XHEADS = 4
POOL_WINDOWS = (2, 4, 8, 16)
ADAM_LR = 0.001
ADAM_B1 = 0.9
ADAM_B2 = 0.999
ADAM_EPS = 1e-08
ADAM_WD = 0.01
ADAM_STEP = 10
N_DEV = 8
VMEM_LIMIT_V7X = 48 * 1024 * 1024
LANES = 128
SUBLANES = 8
SUB_ROWS = 256
POOL_PAD = 16
MESH = pl.DeviceIdType.MESH
ANY = pl.BlockSpec(memory_space=pl.ANY)
HBM = pl.BlockSpec(memory_space=pltpu.HBM)
SEM = pl.BlockSpec(memory_space=pltpu.SEMAPHORE)
SIDE_EFFECT = pltpu.SideEffectType.DATAFLOW_SIDE_EFFECTING

_DIMS = {
    "nt": (((1,), (1,)), ((), ())),
    "nn": (((1,), (0,)), ((), ())),
    "tn": (((0,), (0,)), ((), ())),
}


def _pick(dim, pref, mult=LANES):
    if dim <= pref:
        return dim
    for t in range(pref - pref % mult, 0, -mult):
        if dim % t == 0:
            return t
    return dim


def _params(sem):
    return pltpu.CompilerParams(dimension_semantics=sem, vmem_limit_bytes=VMEM_LIMIT_V7X)


def _tile(tm, tn, coff=0):
    return pl.BlockSpec((tm, tn), lambda i, j: (i, j + coff))


def _rowvec(tn, coff=0):
    return pl.BlockSpec((1, tn), lambda i, j: (0, j + coff))


def _out(m, n, dtype):
    return jax.ShapeDtypeStruct((m, n), dtype)


def _mm(name, form, a_list, b_list, groups, m, n, tm, tn, extras, epilogue, outs, after=None, sub=SUB_ROWS):
    na, nb, ne = len(a_list), len(b_list), len(extras)
    pins = [] if after is None else [after]
    step = tm if (sub is None or form == "tn" or tm % sub) else sub

    def a_spec(a):
        if form == "tn":
            return pl.BlockSpec((a.shape[0], tm), lambda i, j: (0, i))
        return pl.BlockSpec((tm, a.shape[1]), lambda i, j: (i, 0))

    def b_spec(b):
        if form == "nt":
            return pl.BlockSpec((tn, b.shape[1]), lambda i, j: (j, 0))
        return pl.BlockSpec((b.shape[0], tn), lambda i, j: (0, j))

    def body(*refs):
        a_refs, b_refs = refs[:na], refs[na:na + nb]
        e_refs, o_refs = refs[na + nb:na + nb + ne], refs[na + nb + ne + len(pins):]
        b_vals = {}
        for s0 in range(0, tm, step):
            rows = slice(None) if step == tm else pl.ds(s0, step)
            a_vals, accs = {}, []
            for group in groups:
                acc = None
                for ai, bi in group:
                    if ai not in a_vals:
                        a_vals[ai] = (a_refs[ai][...] if form == "tn" else a_refs[ai][rows, :]).astype(BF16)
                    if bi not in b_vals:
                        b_vals[bi] = b_refs[bi][...].astype(BF16)
                    d = lax.dot_general(a_vals[ai], b_vals[bi], _DIMS[form], preferred_element_type=F32)
                    acc = d if acc is None else acc + d
                accs.append(acc)
            res = epilogue(accs, *[e[rows, :] if e.shape[0] == tm else e[...] for e in e_refs])
            for o_ref, r in zip(o_refs, res):
                o_ref[rows, :] = r.astype(o_ref.dtype)

    out_specs = [_tile(tm, tn) if s is None else s for _, s in outs]
    res = pl.pallas_call(
        body, name=name, grid=(m // tm, n // tn),
        in_specs=[a_spec(a) for a in a_list] + [b_spec(b) for b in b_list] + [s for _, s in extras] + [ANY] * len(pins),
        out_specs=out_specs, out_shape=[o for o, _ in outs],
        compiler_params=_params(("parallel", "parallel")),
    )(*a_list, *b_list, *[e for e, _ in extras], *pins)
    return res


def _mm1(name, form, a, b, m, n, tm, tn, dtype, scale=None):
    epi = (lambda accs: (accs[0],)) if scale is None else (lambda accs: (accs[0] * scale,))
    return _mm(name, form, [a], [b], [[(0, 0)]], m, n, tm, tn, [], epi, [(_out(m, n, dtype), None)])[0]


def _rms_fwd(name, h, g):
    t, d = h.shape
    tm = _pick(t, 512, SUBLANES)

    def body(h_ref, g_ref, n_ref):
        hv = h_ref[...]
        r = lax.rsqrt(jnp.mean(hv * hv, axis=-1, keepdims=True) + EPS)
        n_ref[...] = ((hv * r) * g_ref[...]).astype(BF16)

    return pl.pallas_call(
        body, name=name, grid=(t // tm,),
        in_specs=[pl.BlockSpec((tm, d), lambda i: (i, 0)), pl.BlockSpec((1, d), lambda i: (0, 0))],
        out_specs=pl.BlockSpec((tm, d), lambda i: (i, 0)), out_shape=_out(t, d, BF16),
        compiler_params=_params(("parallel",)),
    )(h, g)


def _rms_bwd(name, h, g, dn, dres=None):
    t, d = h.shape
    tm = _pick(t, 512, SUBLANES)
    need_dh = dres is not None

    def body(*refs):
        if need_dh:
            h_ref, g_ref, dn_ref, dres_ref, dh_ref, dhb_ref, dg_ref = refs
        else:
            h_ref, g_ref, dn_ref, dg_ref = refs
        hv = h_ref[...]
        r = lax.rsqrt(jnp.mean(hv * hv, axis=-1, keepdims=True) + EPS)
        nh = hv * r
        dnv = dn_ref[...].astype(F32)

        @pl.when(pl.program_id(0) == 0)
        def _():
            dg_ref[...] = jnp.zeros_like(dg_ref)

        dg_ref[...] += jnp.sum(dnv * nh, axis=0, keepdims=True)
        if need_dh:
            dng = dnv * g_ref[...]
            dh = dres_ref[...] + r * (dng - nh * jnp.mean(dng * nh, axis=-1, keepdims=True))
            dh_ref[...] = dh
            dhb_ref[...] = dh.astype(BF16)

    row = pl.BlockSpec((tm, d), lambda i: (i, 0))
    vec = pl.BlockSpec((1, d), lambda i: (0, 0))
    if need_dh:
        return pl.pallas_call(
            body, name=name, grid=(t // tm,), in_specs=[row, vec, row, row], out_specs=[row, row, vec],
            out_shape=[_out(t, d, F32), _out(t, d, BF16), _out(1, d, F32)], compiler_params=_params(("arbitrary",)),
        )(h, g, dn, dres)
    return pl.pallas_call(
        body, name=name, grid=(t // tm,), in_specs=[row, vec, row], out_specs=vec,
        out_shape=_out(1, d, F32), compiler_params=_params(("arbitrary",)),
    )(h, g, dn)


def _loss_head(h, g, tgt):
    t, d = h.shape
    tm = _pick(t, 512, SUBLANES)

    def body(h_ref, g_ref, t_ref, dh_ref, dhb_ref, dg_ref, loss_ref):
        hv = h_ref[...]
        r = lax.rsqrt(jnp.mean(hv * hv, axis=-1, keepdims=True) + EPS)
        nh = hv * r
        err = nh * g_ref[...] - t_ref[...]

        @pl.when(pl.program_id(0) == 0)
        def _():
            dg_ref[...] = jnp.zeros_like(dg_ref)
            loss_ref[...] = jnp.zeros_like(loss_ref)

        per_row = jnp.mean(err * err, axis=-1, keepdims=True)
        loss_ref[...] += 0.5 * jnp.sum(per_row, axis=0, keepdims=True)
        dy = err * (1.0 / d)
        dg_ref[...] += jnp.sum(dy * nh, axis=0, keepdims=True)
        dng = dy * g_ref[...]
        dh = r * (dng - nh * jnp.mean(dng * nh, axis=-1, keepdims=True))
        dh_ref[...] = dh
        dhb_ref[...] = dh.astype(BF16)

    row = pl.BlockSpec((tm, d), lambda i: (i, 0))
    vec = pl.BlockSpec((1, d), lambda i: (0, 0))
    return pl.pallas_call(
        body, name="loss_head", grid=(t // tm,), in_specs=[row, vec, row],
        out_specs=[row, row, vec, pl.BlockSpec((1, LANES), lambda i: (0, 0))],
        out_shape=[_out(t, d, F32), _out(t, d, BF16), _out(1, d, F32), _out(1, LANES, F32)],
        compiler_params=_params(("arbitrary",)),
    )(h, g, tgt)


def _ffn_fwd(tag, h, n, wg_t, wu_t, wd):
    t, d = h.shape
    f = wg_t.shape[0]
    tm, tn = _pick(t, 1024), _pick(f, 1408)

    def up_epi(accs):
        a, b = accs
        return a, b, (a * jax.nn.sigmoid(a)) * b

    a, b, hid = _mm(tag + "_up", "nt", [n], [wg_t, wu_t], [[(0, 0)], [(0, 1)]], t, f, tm, tn, [], up_epi,
                    [(_out(t, f, BF16), None)] * 3)
    if callable(wd):
        wd = wd(hid)
    tm2, tn2 = _pick(t, 1024), _pick(d, 512)
    h_out = _mm(tag + "_down", "nn", [hid], [wd], [[(0, 0)]], t, d, tm2, tn2, [(h, _tile(tm2, tn2))],
                lambda accs, hin: (hin + 0.5 * accs[0],), [(_out(t, d, F32), None)])[0]
    return h_out, (n, a, b, hid)


def _ffn_bwd(tag, h, g, wg_t, wu_t, wd, saved, dh, dh_bf, weights_done=None, after=None):
    n, a, b, hid = saved
    t, d = h.shape
    f = wd.shape[0]
    tm, tn = _pick(t, 1024), _pick(f, 1408)

    def hid_epi(accs, av, bv):
        dhid = 0.5 * accs[0]
        av, bv = av.astype(F32), bv.astype(F32)
        sig = jax.nn.sigmoid(av)
        da = dhid * bv * (sig * (1.0 + av * (1.0 - sig)))
        db = dhid * (av * sig)
        return da, db

    da, db = _mm(tag + "_bwd_hid", "nt", [dh_bf], [wd], [[(0, 0)]], t, f, tm, tn,
                 [(a, _tile(tm, tn)), (b, _tile(tm, tn))], hid_epi, [(_out(t, f, BF16), None)] * 2, after=after)
    tw, tnw = _pick(f, 1408), _pick(d, 512)
    d_wd = _mm1(tag + "_dwd", "tn", hid, dh_bf, f, d, tw, tnw, BF16, scale=0.5)
    d_wg = _mm1(tag + "_dwg", "tn", da, n, f, d, tw, tnw, BF16)
    d_wu = _mm1(tag + "_dwu", "tn", db, n, f, d, tw, tnw, BF16)
    pin = weights_done(d_wg, d_wu, d_wd) if weights_done is not None else None
    tm2, tn2 = _pick(t, 1024), _pick(d, 512)
    dn = _mm(tag + "_dn", "nn", [da, db], [wg_t, wu_t], [[(0, 0), (1, 1)]], t, d, tm2, tn2, [],
             lambda accs: (accs[0],), [(_out(t, d, F32), None)], after=pin)[0]
    dh_in, dh_in_bf, dg = _rms_bwd(tag + "_norm_bwd", h, g, dn, dh)
    return dh_in, dh_in_bf, dg, d_wg, d_wu, d_wd


def _window_sum(win, offsets):
    n = win.shape[0]
    acc = None
    for j in offsets:
        term = win if j == 0 else pltpu.roll(win, (-j) % n, 0)
        acc = term if acc is None else acc + term
    return acc


def _pool_counts(r0, ch, c, left, right, t):
    pos = r0 + lax.broadcasted_iota(jnp.int32, (ch, c), 0)
    return (jnp.minimum(pos + right + 1, t) - jnp.maximum(pos - left, 0)).astype(F32)


def _pool_fwd(proj, pool_w_bf, pool_scale):
    t = proj.shape[0]
    ng, c, _ = pool_w_bf.shape
    ch = _pick(t, 256, SUBLANES)
    pad = POOL_PAD

    def body(p_ref, w_ref, s_ref, pooled_ref, pm_ref, buf):
        grp = pl.program_id(0)
        buf[pl.ds(0, pad), :] = jnp.zeros((pad, c), F32)
        buf[pl.ds(pad + t, pad), :] = jnp.zeros((pad, c), F32)

        def fill(ci, carry):
            r0 = pl.multiple_of(ci * ch, SUBLANES)
            buf[pl.ds(pl.multiple_of(r0 + pad, SUBLANES), ch), :] = p_ref[pl.ds(r0, ch), :]
            return carry

        lax.fori_loop(0, t // ch, fill, 0)
        for gi, w in enumerate(POOL_WINDOWS):
            left = w // 2
            right = w - 1 - left

            @pl.when(grp == gi)
            def _(left=left, right=right):
                def chunk(ci, carry):
                    r0 = pl.multiple_of(ci * ch, SUBLANES)
                    win = buf[pl.ds(r0, ch + 2 * pad), :]
                    s = _window_sum(win, range(-left, right + 1))[pad:pad + ch]
                    pooled = s / _pool_counts(r0, ch, c, left, right, t) - win[pad:pad + ch]
                    pooled_bf = pooled.astype(BF16)
                    mixed = jnp.dot(pooled_bf, w_ref[0], preferred_element_type=F32)
                    pooled_ref[pl.ds(r0, ch), :] = pooled_bf
                    pm_ref[pl.ds(r0, ch), :] = (mixed * s_ref[...]).astype(BF16)
                    return carry

                lax.fori_loop(0, t // ch, chunk, 0)

    col = pl.BlockSpec((t, c), lambda g: (0, g))
    return pl.pallas_call(
        body, name="pool_fwd", grid=(ng,),
        in_specs=[col, pl.BlockSpec((1, c, c), lambda g: (g, 0, 0)), pl.BlockSpec((1, c), lambda g: (0, g))],
        out_specs=[col, col], out_shape=[_out(t, ng * c, BF16), _out(t, ng * c, BF16)],
        scratch_shapes=[pltpu.VMEM((t + 2 * pad, c), F32)],
        compiler_params=_params(("parallel",)),
    )(proj, pool_w_bf, pool_scale)


def _pool_bwd(pooled, dpm, pool_w_bf, pool_scale):
    t = pooled.shape[0]
    ng, c, _ = pool_w_bf.shape
    ch = _pick(t, 256, SUBLANES)
    pad = POOL_PAD

    def body(pooled_ref, dpm_ref, w_ref, s_ref, dp_ref, dw_ref, ds_ref, buf, raw):
        grp = pl.program_id(0)
        buf[pl.ds(0, pad), :] = jnp.zeros((pad, c), F32)
        buf[pl.ds(pad + t, pad), :] = jnp.zeros((pad, c), F32)
        dw_ref[...] = jnp.zeros_like(dw_ref)
        ds_ref[...] = jnp.zeros_like(ds_ref)
        for gi, w in enumerate(POOL_WINDOWS):
            left = w // 2
            right = w - 1 - left

            @pl.when(grp == gi)
            def _(left=left, right=right):
                def first(ci, carry):
                    r0 = pl.multiple_of(ci * ch, SUBLANES)
                    pv = pooled_ref[pl.ds(r0, ch), :]
                    dpm_v = dpm_ref[pl.ds(r0, ch), :]
                    mixed = jnp.dot(pv, w_ref[0], preferred_element_type=F32)
                    ds_ref[...] += jnp.sum(dpm_v * mixed, axis=0, keepdims=True)
                    dmixed = (dpm_v * s_ref[...]).astype(BF16)
                    dw_ref[0] += lax.dot_general(pv, dmixed, _DIMS["tn"], preferred_element_type=F32)
                    dpooled = lax.dot_general(dmixed, w_ref[0], _DIMS["nt"], preferred_element_type=F32)
                    raw[pl.ds(r0, ch), :] = dpooled
                    buf[pl.ds(pl.multiple_of(r0 + pad, SUBLANES), ch), :] = (
                        dpooled / _pool_counts(r0, ch, c, left, right, t))
                    return carry

                lax.fori_loop(0, t // ch, first, 0)

                def second(ci, carry):
                    r0 = pl.multiple_of(ci * ch, SUBLANES)
                    win = buf[pl.ds(r0, ch + 2 * pad), :]
                    s = _window_sum(win, range(-right, left + 1))[pad:pad + ch]
                    dp_ref[pl.ds(r0, ch), :] = (s - raw[pl.ds(r0, ch), :]).astype(BF16)
                    return carry

                lax.fori_loop(0, t // ch, second, 0)

    col = pl.BlockSpec((t, c), lambda g: (0, g))
    return pl.pallas_call(
        body, name="pool_bwd", grid=(ng,),
        in_specs=[col, col, pl.BlockSpec((1, c, c), lambda g: (g, 0, 0)), pl.BlockSpec((1, c), lambda g: (0, g))],
        out_specs=[col, pl.BlockSpec((1, c, c), lambda g: (g, 0, 0)), pl.BlockSpec((1, c), lambda g: (0, g))],
        out_shape=[_out(t, ng * c, BF16), jax.ShapeDtypeStruct((ng, c, c), F32), _out(1, ng * c, F32)],
        scratch_shapes=[pltpu.VMEM((t + 2 * pad, c), F32), pltpu.VMEM((t, c), F32)],
        compiler_params=_params(("parallel",)),
    )(pooled, dpm, pool_w_bf, pool_scale)


def _discretise(a_re, a_im, log_dt, b_re, b_im):
    dt = jnp.exp(log_dt)
    mag = jnp.exp(dt * a_re)
    ang = dt * a_im
    abr = mag * jnp.cos(ang)
    abi = mag * jnp.sin(ang)
    den = a_re * a_re + a_im * a_im
    nr = abr - 1.0
    qr = (nr * a_re + abi * a_im) / den
    qi = (abi * a_re - nr * a_im) / den
    return abr, abi, qr * b_re - qi * b_im, qr * b_im + qi * b_re


def _ssm_disc(cols):
    n, hh = cols[3].shape

    def body(ar, ai, ld, br, bi, o1, o2, o3, o4):
        res = _discretise(ar[...], ai[...], ld[...], br[...], bi[...])
        for o, r in zip((o1, o2, o3, o4), res):
            o[...] = r

    return pl.pallas_call(
        body, name="ssm_disc",
        out_shape=[_out(n, 1, F32), _out(n, 1, F32), _out(n, hh, F32), _out(n, hh, F32)],
    )(*cols)


def _ssm_disc_bwd(cols, cots):
    n, hh = cols[3].shape

    def body(ar, ai, ld, br, bi, c1, c2, c3, c4, o1, o2, o3, o4, o5):
        _, vjp = jax.vjp(_discretise, ar[...], ai[...], ld[...], br[...], bi[...])
        res = vjp((c1[...], c2[...], c3[...], c4[...]))
        for o, r in zip((o1, o2, o3, o4, o5), res):
            o[...] = r

    return pl.pallas_call(
        body, name="ssm_disc_bwd",
        out_shape=[_out(n, 1, F32)] * 3 + [_out(n, hh, F32)] * 2,
    )(*cols, *cots)


def _rowsum(name, a):
    r, _ = a.shape

    def body(a_ref, o_ref):
        o_ref[...] = jnp.sum(a_ref[...], axis=-1, keepdims=True)

    return pl.pallas_call(body, name=name, out_shape=_out(r, 1, F32))(a)


def _cmul(pr, pi, qr, qi):
    return pr * qr - pi * qi, pr * qi + pi * qr


def _cpow(pr, pi, n):
    rr, ri = None, None
    while n:
        if n & 1:
            rr, ri = (pr, pi) if rr is None else _cmul(rr, ri, pr, pi)
        n >>= 1
        if n:
            pr, pi = _cmul(pr, pi, pr, pi)
    return rr, ri


def _segment_carry(er, ei, pr, pi, reverse):
    row = lax.broadcasted_iota(jnp.int32, er.shape, 0)
    cr, ci = jnp.zeros_like(er), jnp.zeros_like(ei)
    for _ in range(SUBLANES - 1):
        tr = er + pr * cr - pi * ci
        ti = ei + pr * ci + pi * cr
        if reverse:
            keep, shift = row < SUBLANES - 1, SUBLANES - 1
        else:
            keep, shift = row >= 1, 1
        cr = jnp.where(keep, pltpu.roll(tr, shift, 0), 0.0)
        ci = jnp.where(keep, pltpu.roll(ti, shift, 0), 0.0)
    return cr, ci


def _ssm_fwd(name, sp, b_re, b_im, c_re, c_im, ar, ai, reverse):
    t, c = sp.shape
    s = ar.shape[1]
    w = _pick(s, 512)
    ch = _pick(t, 512, SUBLANES)
    n_ch, gpc, steps = t // ch, ch // SUBLANES, t // SUBLANES

    def body(sp_ref, bre_ref, bim_ref, cre_ref, cim_ref, ar_ref, ai_ref, xr_ref, xi_ref, y_ref, ur, ui, xbr, xbi):
        a_r = jnp.broadcast_to(ar_ref[...], (SUBLANES, w))
        a_i = jnp.broadcast_to(ai_ref[...], (SUBLANES, w))

        @pl.when(pl.program_id(0) == 0)
        def _():
            y_ref[...] = jnp.zeros_like(y_ref)

        def sweep(h0, store):
            def chunk(k, h):
                ci = n_ch - 1 - k if reverse else k
                rows = pl.ds(pl.multiple_of(ci * ch, ch), ch)
                spv = sp_ref[rows, :].astype(BF16)
                ur[...] = jnp.dot(spv, bre_ref[...], preferred_element_type=F32)
                ui[...] = jnp.dot(spv, bim_ref[...], preferred_element_type=F32)

                def group(g, hh):
                    gi = gpc - 1 - g if reverse else g
                    r0 = pl.multiple_of(gi * SUBLANES, SUBLANES)
                    hr, hi = hh
                    nr = a_r * hr - a_i * hi + ur[pl.ds(r0, SUBLANES), :]
                    ni = a_r * hi + a_i * hr + ui[pl.ds(r0, SUBLANES), :]
                    if store:
                        xbr[pl.ds(r0, SUBLANES), :] = nr
                        xbi[pl.ds(r0, SUBLANES), :] = ni
                    return nr, ni

                h = lax.fori_loop(0, gpc, group, h)
                if store:
                    xr16, xi16 = xbr[...].astype(BF16), xbi[...].astype(BF16)
                    xr_ref[rows, :] = xr16
                    xi_ref[rows, :] = xi16
                    y_ref[rows, :] += (jnp.dot(xr16, cre_ref[...], preferred_element_type=F32)
                                       + jnp.dot(xi16, cim_ref[...], preferred_element_type=F32))
                return h

            return lax.fori_loop(0, n_ch, chunk, h0)

        zero = jnp.zeros((SUBLANES, w), F32)
        er, ei = sweep((zero, zero), False)
        pr, pi = _cpow(ar_ref[...], ai_ref[...], steps)
        sweep(_segment_carry(er, ei, pr, pi, reverse), True)

    col = lambda i: (0, i)
    return pl.pallas_call(
        body, name=name, grid=(s // w,),
        in_specs=[pl.BlockSpec((t, c), lambda i: (0, 0)), pl.BlockSpec((c, w), col), pl.BlockSpec((c, w), col),
                  pl.BlockSpec((w, c), lambda i: (i, 0)), pl.BlockSpec((w, c), lambda i: (i, 0)),
                  pl.BlockSpec((1, w), col), pl.BlockSpec((1, w), col)],
        out_specs=[pl.BlockSpec((t, w), col), pl.BlockSpec((t, w), col), pl.BlockSpec((t, c), lambda i: (0, 0))],
        out_shape=[_out(t, s, BF16), _out(t, s, BF16), _out(t, c, F32)],
        scratch_shapes=[pltpu.VMEM((ch, w), F32)] * 4,
        compiler_params=_params(("arbitrary",)),
    )(sp, b_re, b_im, c_re, c_im, ar, ai)


def _ssm_bwd(name, dyp, c_re, c_im, xr, xi, ar, ai, reverse):
    t, c = dyp.shape
    s = ar.shape[1]
    w = _pick(s, 512)
    ch = _pick(t, 512, SUBLANES)
    n_ch, gpc, steps = t // ch, ch // SUBLANES, t // SUBLANES
    back = not reverse
    edge = 2 * SUBLANES

    def body(dy_ref, cre_ref, cim_ref, xr_ref, xi_ref, ar_ref, ai_ref, lr_ref, li_ref, dar_ref, dai_ref,
             gr, gi_, lbr, lbi, xbr, xbi):
        a_r = jnp.broadcast_to(ar_ref[...], (SUBLANES, w))
        a_i = -jnp.broadcast_to(ai_ref[...], (SUBLANES, w))
        row = lax.broadcasted_iota(jnp.int32, (SUBLANES, w), 0)

        def neighbours(ci, x_ref, buf):
            rows = pl.ds(pl.multiple_of(ci * ch, ch), ch)
            if reverse:
                buf[pl.ds(0, ch), :] = x_ref[rows, :].astype(F32)
                nxt = x_ref[pl.ds(pl.multiple_of(jnp.minimum(ci + 1, n_ch - 1) * ch, ch), edge), :].astype(F32)[:SUBLANES]
                first = x_ref[pl.ds(0, edge), :].astype(F32)[:SUBLANES]
                wrap = jnp.where(row < SUBLANES - 1, pltpu.roll(first, SUBLANES - 1, 0), 0.0)
                buf[pl.ds(ch, SUBLANES), :] = jnp.where(ci == n_ch - 1, wrap, nxt)
            else:
                buf[pl.ds(SUBLANES, ch), :] = x_ref[rows, :].astype(F32)
                prv = x_ref[pl.ds(pl.multiple_of(jnp.maximum(ci * ch - edge, 0), edge), edge), :].astype(F32)[SUBLANES:]
                last = x_ref[pl.ds(t - edge, edge), :].astype(F32)[SUBLANES:]
                wrap = jnp.where(row >= 1, pltpu.roll(last, 1, 0), 0.0)
                buf[pl.ds(0, SUBLANES), :] = jnp.where(ci == 0, wrap, prv)

        def sweep(h0, store):
            def chunk(k, carry):
                ci = n_ch - 1 - k if back else k
                rows = pl.ds(pl.multiple_of(ci * ch, ch), ch)
                dyv = dy_ref[rows, :].astype(BF16)
                gr[...] = lax.dot_general(dyv, cre_ref[...], _DIMS["nt"], preferred_element_type=F32)
                gi_[...] = lax.dot_general(dyv, cim_ref[...], _DIMS["nt"], preferred_element_type=F32)
                if store:
                    neighbours(ci, xr_ref, xbr)
                    neighbours(ci, xi_ref, xbi)

                def group(g, cc):
                    gidx = gpc - 1 - g if back else g
                    r0 = pl.multiple_of(gidx * SUBLANES, SUBLANES)
                    hr, hi = cc[0], cc[1]
                    nr = a_r * hr - a_i * hi + gr[pl.ds(r0, SUBLANES), :]
                    ni = a_r * hi + a_i * hr + gi_[pl.ds(r0, SUBLANES), :]
                    if not store:
                        return nr, ni
                    lbr[pl.ds(r0, SUBLANES), :] = nr
                    lbi[pl.ds(r0, SUBLANES), :] = ni
                    x0 = pl.multiple_of(r0 + SUBLANES, SUBLANES) if reverse else r0
                    xpr, xpi = xbr[pl.ds(x0, SUBLANES), :], xbi[pl.ds(x0, SUBLANES), :]
                    return nr, ni, cc[2] + nr * xpr + ni * xpi, cc[3] + ni * xpr - nr * xpi

                carry = lax.fori_loop(0, gpc, group, carry)
                if store:
                    lr_ref[rows, :] = lbr[...].astype(BF16)
                    li_ref[rows, :] = lbi[...].astype(BF16)
                return carry

            return lax.fori_loop(0, n_ch, chunk, h0)

        zero = jnp.zeros((SUBLANES, w), F32)
        er, ei = sweep((zero, zero), False)
        pr, pi = _cpow(ar_ref[...], -ai_ref[...], steps)
        cr, ci0 = _segment_carry(er, ei, pr, pi, back)
        _, _, dar, dai = sweep((cr, ci0, zero, zero), True)
        dar_ref[...] = jnp.sum(dar, axis=0, keepdims=True)
        dai_ref[...] = jnp.sum(dai, axis=0, keepdims=True)

    col = lambda i: (0, i)
    return pl.pallas_call(
        body, name=name, grid=(s // w,),
        in_specs=[pl.BlockSpec((t, c), lambda i: (0, 0)), pl.BlockSpec((w, c), lambda i: (i, 0)),
                  pl.BlockSpec((w, c), lambda i: (i, 0)), pl.BlockSpec((t, w), col), pl.BlockSpec((t, w), col),
                  pl.BlockSpec((1, w), col), pl.BlockSpec((1, w), col)],
        out_specs=[pl.BlockSpec((t, w), col), pl.BlockSpec((t, w), col), pl.BlockSpec((1, w), col), pl.BlockSpec((1, w), col)],
        out_shape=[_out(t, s, BF16), _out(t, s, BF16), _out(1, s, F32), _out(1, s, F32)],
        scratch_shapes=[pltpu.VMEM((ch, w), F32)] * 4 + [pltpu.VMEM((ch + SUBLANES, w), F32)] * 2,
        compiler_params=_params(("parallel",)),
    )(dyp, c_re, c_im, xr, xi, ar, ai)


def _to_segments(a):
    t, c = a.shape
    return a.reshape(SUBLANES, t // SUBLANES, c).transpose(1, 0, 2).reshape(t, c)


def _from_segments(a):
    t, c = a.shape
    return a.reshape(t // SUBLANES, SUBLANES, c).transpose(1, 0, 2).reshape(t, c)


def _colsum_prod(name, a, b, b_coff=0):
    t, n = a.shape
    tm = _pick(t, 512, SUBLANES)

    def body(a_ref, b_ref, o_ref):
        @pl.when(pl.program_id(0) == 0)
        def _():
            o_ref[...] = jnp.zeros_like(o_ref)

        o_ref[...] += jnp.sum(a_ref[...].astype(F32) * b_ref[...].astype(F32), axis=0, keepdims=True)

    return pl.pallas_call(
        body, name=name, grid=(t // tm,),
        in_specs=[pl.BlockSpec((tm, n), lambda i: (i, 0)), pl.BlockSpec((tm, n), lambda i: (i, b_coff))],
        out_specs=pl.BlockSpec((1, n), lambda i: (0, 0)), out_shape=_out(1, n, F32),
        compiler_params=_params(("arbitrary",)),
    )(a, b)


def _bd_in(bb, g, p, hh):
    blk = bb.reshape(g, p, hh).transpose(0, 2, 1)
    eye = jnp.eye(g, dtype=bool)[:, None, :, None]
    return jnp.where(eye, blk[:, :, None, :], 0.0).reshape(g * hh, g * p)


def _bd_out(cc, g, p, hh):
    blk = cc.transpose(0, 2, 1)
    eye = jnp.eye(g, dtype=bool)[:, None, :, None]
    return jnp.where(eye, blk[:, :, None, :], 0.0).reshape(g * p, g * hh)


def _diag_in(dmat, g, p, hh):
    eye = jnp.eye(g, dtype=bool)[:, None, :, None]
    diag = jnp.sum(jnp.where(eye, dmat.reshape(g, hh, g, p), 0.0), axis=2)
    return diag.transpose(0, 2, 1).reshape(g * p, hh)


def _diag_out(dmat, g, p, hh):
    eye = jnp.eye(g, dtype=bool)[:, None, :, None]
    diag = jnp.sum(jnp.where(eye, dmat.reshape(g, p, g, hh), 0.0), axis=2)
    return diag.transpose(0, 2, 1)


def _softmax(qh, kh, scale):
    s = lax.dot_general(qh, kh, _DIMS["nt"], preferred_element_type=F32) * scale
    e = jnp.exp(s - jnp.max(s, axis=-1, keepdims=True))
    return e / jnp.sum(e, axis=-1, keepdims=True)


def _attn_fwd(q, kv):
    t, d = q.shape
    mm_ = kv.shape[0]
    hd = d // N_XHEADS
    scale = 1.0 / math.sqrt(hd)
    tm = _pick(t, 512, SUBLANES)

    def body(q_ref, kv_ref, o_ref):
        for h in range(N_XHEADS):
            sl = pl.ds(h * hd, hd)
            p = _softmax(q_ref[:, sl], kv_ref[:, sl], scale)
            o_ref[:, sl] = jnp.dot(p.astype(BF16), kv_ref[:, pl.ds(d + h * hd, hd)],
                                   preferred_element_type=F32).astype(BF16)

    return pl.pallas_call(
        body, name="attn_fwd", grid=(t // tm,),
        in_specs=[pl.BlockSpec((tm, d), lambda i: (i, 0)), pl.BlockSpec((mm_, 2 * d), lambda i: (0, 0))],
        out_specs=pl.BlockSpec((tm, d), lambda i: (i, 0)), out_shape=_out(t, d, BF16),
        compiler_params=_params(("parallel",)),
    )(q, kv)


def _attn_bwd(q, kv, do):
    t, d = q.shape
    mm_ = kv.shape[0]
    hd = d // N_XHEADS
    scale = 1.0 / math.sqrt(hd)
    tm = _pick(t, 512, SUBLANES)

    def body(q_ref, kv_ref, do_ref, dq_ref, dkv_ref):
        @pl.when(pl.program_id(0) == 0)
        def _():
            dkv_ref[...] = jnp.zeros_like(dkv_ref)

        for h in range(N_XHEADS):
            sl = pl.ds(h * hd, hd)
            vsl = pl.ds(d + h * hd, hd)
            qh, kh, doh = q_ref[:, sl], kv_ref[:, sl], do_ref[:, sl]
            p = _softmax(qh, kh, scale)
            dp = lax.dot_general(doh, kv_ref[:, vsl], _DIMS["nt"], preferred_element_type=F32)
            dkv_ref[:, vsl] += lax.dot_general(p.astype(BF16), doh, _DIMS["tn"], preferred_element_type=F32)
            ds = (p * (dp - jnp.sum(dp * p, axis=-1, keepdims=True)) * scale).astype(BF16)
            dq_ref[:, sl] = jnp.dot(ds, kh, preferred_element_type=F32).astype(BF16)
            dkv_ref[:, sl] += lax.dot_general(ds, qh, _DIMS["tn"], preferred_element_type=F32)

    row = pl.BlockSpec((tm, d), lambda i: (i, 0))
    full = pl.BlockSpec((mm_, 2 * d), lambda i: (0, 0))
    return pl.pallas_call(
        body, name="attn_bwd", grid=(t // tm,), in_specs=[row, full, row], out_specs=[row, full],
        out_shape=[_out(t, d, BF16), _out(mm_, 2 * d, F32)], compiler_params=_params(("arbitrary",)),
    )(q, kv, do)


def _ew(name, fn, ins, outs, rows_pref=256, rowvecs=()):
    r, c = ins[0].shape
    tr = _pick(r, rows_pref, SUBLANES)
    ni = len(ins) + len(rowvecs)

    def body(*refs):
        res = fn(*[x[...] for x in refs[:ni]])
        for o_ref, v in zip(refs[ni:], res):
            o_ref[...] = v.astype(o_ref.dtype)

    blk = pl.BlockSpec((tr, c), lambda i: (i, 0))
    vec = pl.BlockSpec((1, c), lambda i: (0, 0))
    return pl.pallas_call(
        body, name=name, grid=(r // tr,), in_specs=[blk] * len(ins) + [vec] * len(rowvecs), out_specs=[blk] * len(outs),
        out_shape=[_out(r, c, dt) for dt in outs], compiler_params=_params(("parallel",)),
    )(*ins, *rowvecs)


def _sum_slots(name, a, dtype):
    s, r, c = a.shape
    tr = _pick(r, 256, SUBLANES)

    def body(a_ref, o_ref):
        acc = a_ref[0].astype(F32)
        for k in range(1, s):
            acc = acc + a_ref[k].astype(F32)
        o_ref[...] = acc.astype(o_ref.dtype)

    return pl.pallas_call(
        body, name=name, grid=(r // tr,), in_specs=[pl.BlockSpec((s, tr, c), lambda i: (0, i, 0))],
        out_specs=pl.BlockSpec((tr, c), lambda i: (i, 0)), out_shape=_out(r, c, dtype),
        compiler_params=_params(("parallel",)),
    )(a)


def _adamw(name, w, g, m, v):
    bc1 = 1.0 - ADAM_B1 ** ADAM_STEP
    bc2 = 1.0 - ADAM_B2 ** ADAM_STEP

    def fn(wv, gv, mv, vv):
        m2 = ADAM_B1 * mv + (1.0 - ADAM_B1) * gv
        v2 = ADAM_B2 * vv + (1.0 - ADAM_B2) * (gv * gv)
        delta = -ADAM_LR * ((m2 / bc1) / (jnp.sqrt(v2 / bc2) + ADAM_EPS) + ADAM_WD * wv)
        return delta, m2, v2

    return _ew(name, fn, [w, g, m, v], [F32, F32, F32])


def _allgather(name, arrs):
    n = len(arrs)

    def body(*refs):
        ins, outs = refs[:n], refs[n:2 * n]
        send_sems, recv_sems, local_sems = refs[2 * n:]
        x, y, c = lax.axis_index("x"), lax.axis_index("y"), lax.axis_index("c")
        me, sibling = (x, y, c), (x, y, 1 - c)
        chips = [(1 - x, y), (x, 1 - y), (1 - x, 1 - y)]

        def rows(a, px, py, pc):
            r = ins[a].shape[0]
            return outs[a].at[pl.ds((4 * px + 2 * py + pc) * r, r), :]

        def copy(a, k, block, to, src=None):
            return pltpu.make_async_remote_copy(
                src_ref=rows(a, *block) if src is None else src, dst_ref=rows(a, *block),
                send_sem=send_sems.at[a, k], recv_sem=recv_sems.at[a, k], device_id=to, device_id_type=MESH)

        mine = [pltpu.make_async_copy(ins[a], rows(a, *me), local_sems.at[a]) for a in range(n)]
        for cp in mine:
            cp.start()
        first = []
        for a in range(n):
            first.append(copy(a, 0, me, sibling, src=ins[a]))
            first += [copy(a, 1 + j, me, (*chip, c), src=ins[a]) for j, chip in enumerate(chips)]
        for cp in first:
            cp.start()
        passed = []
        for j, chip in enumerate(chips):
            for a in range(n):
                copy(a, 1 + j, (*chip, c), me).wait_recv()
                cp = copy(a, 4 + j, (*chip, c), sibling)
                cp.start()
                passed.append(cp)
        for a in range(n):
            copy(a, 0, sibling, me).wait_recv()
            for j, chip in enumerate(chips):
                copy(a, 4 + j, (*chip, 1 - c), me).wait_recv()
        for cp in first + passed:
            cp.wait_send()
        for cp in mine:
            cp.wait()

    return pl.pallas_call(
        body, name=name, in_specs=[ANY] * n, out_specs=[ANY] * n,
        out_shape=[_out(N_DEV * a.shape[0], a.shape[1], a.dtype) for a in arrs],
        scratch_shapes=[pltpu.SemaphoreType.DMA((n, 7)), pltpu.SemaphoreType.DMA((n, 7)), pltpu.SemaphoreType.DMA((n,))],
    )(*arrs)


def _exchange_cores(name, blocks):
    n = len(blocks)
    c = blocks[0].shape[2]
    r = sum(b.shape[1] for b in blocks)

    def body(*refs):
        srcs, (own_ref, recv_ref, send_sems, recv_sems, local_sems) = refs[:n], refs[n:]
        x, y, cc = lax.axis_index("x"), lax.axis_index("y"), lax.axis_index("c")
        copies, off = [], 0
        for a, src in enumerate(srcs):
            rows = pl.ds(off, src.shape[1])
            off += src.shape[1]
            for q in range(4):
                copies.append(pltpu.make_async_copy(src.at[2 * q + cc], own_ref.at[q, rows], local_sems.at[a, q]))
                copies.append(pltpu.make_async_remote_copy(
                    src_ref=src.at[2 * q + (1 - cc)], dst_ref=recv_ref.at[q, rows], send_sem=send_sems.at[a, q],
                    recv_sem=recv_sems.at[a, q], device_id=(x, y, 1 - cc), device_id_type=MESH))
        for cp in copies:
            cp.start()
        for cp in copies:
            cp.wait()

    return pl.pallas_call(
        body, name=name, in_specs=[ANY] * n, out_specs=[ANY, ANY],
        out_shape=[jax.ShapeDtypeStruct((4, r, c), blocks[0].dtype)] * 2,
        scratch_shapes=[pltpu.SemaphoreType.DMA((n, 4))] * 3,
    )(*blocks)


def _peer(k, x, y, c):
    return (1 - x if k & 4 else x, 1 - y if k & 2 else y, 1 - c if k & 1 else c)


def _split_start(name, groups, after=None):
    pins = [] if after is None else [after]
    bufs, sem_shapes, spans = [], [], []
    for srcs, land_shapes, n_remote, n_local, _ in groups:
        sems = [pltpu.SemaphoreType.DMA((n_remote,)), pltpu.SemaphoreType.DMA((n_remote,))]
        sems += [pltpu.SemaphoreType.DMA((n_local,))] if n_local else []
        spans.append((len(bufs), len(srcs), len(land_shapes), len(sem_shapes), len(sems)))
        bufs += [pltpu.with_memory_space_constraint(a, pltpu.HBM) for a in srcs]
        bufs += [pltpu.with_memory_space_constraint(lax.empty(s.shape, s.dtype), pltpu.HBM) for s in land_shapes]
        sem_shapes += sems
    n_buf, n_sem = len(bufs), len(sem_shapes)

    def body(*refs):
        buf_refs, sem_refs, token = refs[:n_buf], refs[n_buf + len(pins):n_buf + len(pins) + n_sem], refs[-1]
        for (b0, ns, nl, s0, k), group in zip(spans, groups):
            remote, local = group[4](buf_refs[b0:b0 + ns], buf_refs[b0 + ns:b0 + ns + nl], *sem_refs[s0:s0 + k])
            for cp in local + remote:
                cp.start()
        token[...] = jnp.zeros_like(token)

    outs = pl.pallas_call(
        body, name=name,
        out_shape=sem_shapes + [pltpu.HBM(b.shape, b.dtype) for b in bufs] + [jax.ShapeDtypeStruct((SUBLANES, LANES), F32)],
        in_specs=[HBM] * n_buf + [ANY] * len(pins),
        out_specs=[SEM] * n_sem + [HBM] * n_buf + [pl.BlockSpec(memory_space=pltpu.VMEM)],
        input_output_aliases={i: n_sem + i for i in range(n_buf)},
        compiler_params=pltpu.CompilerParams(has_side_effects=SIDE_EFFECT),
    )(*bufs, *pins)
    return [dict(sems=list(outs[s0:s0 + k]), bufs=list(outs[n_sem + b0:n_sem + b0 + ns + nl]), token=outs[-1],
                 build=group[4], ns=ns) for (b0, ns, nl, s0, k), group in zip(spans, groups)]


def _split_wait(name, started, after):
    ns, n_buf, n_sem = started["ns"], len(started["bufs"]), len(started["sems"])

    def body(*refs):
        src_refs, land_refs = refs[:ns], refs[ns:n_buf]
        sems = refs[n_buf:n_buf + n_sem]
        remote, local = started["build"](src_refs, land_refs, *sems)
        for cp in local:
            cp.wait()
        for cp in remote:
            cp.wait_send()
            cp.wait_recv()

    outs = pl.pallas_call(
        body, name=name, out_shape=[pltpu.HBM(b.shape, b.dtype) for b in started["bufs"]],
        in_specs=[HBM] * n_buf + [SEM] * n_sem + [ANY], out_specs=[HBM] * n_buf,
        input_output_aliases={i: i for i in range(n_buf)},
        compiler_params=pltpu.CompilerParams(has_side_effects=SIDE_EFFECT),
    )(*started["bufs"], *started["sems"], after)
    return list(outs[:ns]), list(outs[ns:])


def _gather_group(shards):
    m = len(shards)

    def build(src_refs, land_refs, send_sems, recv_sems, local_sems):
        x, y, c = lax.axis_index("x"), lax.axis_index("y"), lax.axis_index("c")
        remote, local = [], []
        for j in range(m):
            r = src_refs[j].shape[0]
            dst = land_refs[j].at[pl.ds((4 * x + 2 * y + c) * r, r), :]
            local.append(pltpu.make_async_copy(src_refs[j], dst, local_sems.at[j]))
            for k in range(1, N_DEV):
                remote.append(pltpu.make_async_remote_copy(
                    src_ref=src_refs[j], dst_ref=dst, send_sem=send_sems.at[7 * j + k - 1],
                    recv_sem=recv_sems.at[7 * j + k - 1], device_id=_peer(k, x, y, c), device_id_type=MESH))
        return remote, local

    lands = [jax.ShapeDtypeStruct((N_DEV * a.shape[0], a.shape[1]), a.dtype) for a in shards]
    return shards, lands, 7 * m, m, build


def _slots_start(name, a):
    def build(src_refs, land_refs, send_sems, recv_sems, local_sems):
        x, y, c = lax.axis_index("x"), lax.axis_index("y"), lax.axis_index("c")
        dst = land_refs[0].at[4 * x + 2 * y + c]
        local = [pltpu.make_async_copy(src_refs[0], dst, local_sems.at[0])]
        remote = [pltpu.make_async_remote_copy(
            src_ref=src_refs[0], dst_ref=dst, send_sem=send_sems.at[k - 1], recv_sem=recv_sems.at[k - 1],
            device_id=_peer(k, x, y, c), device_id_type=MESH) for k in range(1, N_DEV)]
        return remote, local

    return _split_start(name, [([a], [jax.ShapeDtypeStruct((N_DEV,) + a.shape, a.dtype)], 7, 1, build)])[0]


def _chips_start(name, p):
    _, r, c = p.shape
    nck = r // GRAD_ROW_TILE

    def build(src_refs, land_refs, send_sems, recv_sems):
        x, y, cc = lax.axis_index("x"), lax.axis_index("y"), lax.axis_index("c")
        remote = []
        for k in range(1, 4):
            px = 1 - x if k >> 1 else x
            py = 1 - y if k & 1 else y
            for j in range(nck):
                rows = pl.ds(j * GRAD_ROW_TILE, GRAD_ROW_TILE)
                remote.append(pltpu.make_async_remote_copy(
                    src_ref=src_refs[0].at[2 * px + py, rows], dst_ref=land_refs[0].at[k - 1, rows],
                    send_sem=send_sems.at[(k - 1) * nck + j], recv_sem=recv_sems.at[(k - 1) * nck + j],
                    device_id=(px, py, cc), device_id_type=MESH))
        return remote, []

    return _split_start(name, [([p], [jax.ShapeDtypeStruct((3, r, c), p.dtype)], 3 * nck, 0, build)])[0]


def _chip_sum(name, p, recv, chip):
    _, r, c = p.shape
    tr = _pick(r, 5 * GRAD_ROW_TILE, GRAD_ROW_TILE)

    def body(chip_ref, p_ref, r_ref, o_ref):
        acc = p_ref[...].astype(F32)
        for k in range(3):
            acc = acc + r_ref[k].astype(F32)
        o_ref[...] = acc

    return pl.pallas_call(
        body, name=name,
        grid_spec=pltpu.PrefetchScalarGridSpec(
            num_scalar_prefetch=1, grid=(r // tr,),
            in_specs=[pl.BlockSpec((None, tr, c), lambda i, chip_ref: (chip_ref[0], i, 0)),
                      pl.BlockSpec((3, tr, c), lambda i, chip_ref: (0, i, 0))],
            out_specs=pl.BlockSpec((tr, c), lambda i, chip_ref: (i, 0))),
        out_shape=_out(r, c, F32), compiler_params=_params(("parallel",)),
    )(chip, p, recv)


def _local_step(x, mem, tgt, wt, sm, ev=None):
    t, d = x.shape
    n_mem = mem.shape[0]
    d_pool = sm["pool_scale"].shape[1]
    ng, pc = sm["pool_w"].shape[0], sm["pool_w"].shape[1]
    d_ssm = sm["ssm_d"].shape[1]
    _, sg, sp, sh = sm["ssm_b_re"].shape
    n_state = sg * sp
    gb, gs = {}, {}

    def emit(name, **kw):
        return ev(name, **kw) if ev is not None else None

    n1 = _rms_fwd("ffn1_norm", x, sm["ffn1_norm"])
    emit("ffn1_norm_done", marker=n1)
    def ffn1_down(hid):
        emit("ffn1_up_done", marker=hid)
        return wt["ffn1_w_down"]

    h1, ffn1_saved = _ffn_fwd("ffn1", x, n1, wt["ffn1_w_gate"], wt["ffn1_w_up"], ffn1_down)
    emit("ffn1_fwd_done", marker=h1)
    u = _rms_fwd("mix_norm", h1, sm["mix_norm"])
    d_in = wt["w_in"].shape[0]
    tm, tn = _pick(t, 1024), _pick(d_in, 1408)
    proj = _mm1("in_proj", "nt", u, wt["w_in"], t, d_in, tm, tn, F32)
    off_s = d_pool // d_ssm
    off_gp = (d_pool + d_ssm)
    off_gs = off_gp + d

    pool_w_bf = sm["pool_w"].astype(BF16)
    pooled, pm = _pool_fwd(proj, pool_w_bf, sm["pool_scale"])

    cols = [sm["ssm_a_re"].reshape(-1, 1), sm["ssm_a_im"].reshape(-1, 1),
            jnp.broadcast_to(sm["ssm_log_dt"][:, :, None], (2, sg, sp)).reshape(-1, 1),
            sm["ssm_b_re"].reshape(-1, sh), sm["ssm_b_im"].reshape(-1, sh)]
    abr, abi, bbr, bbi = _ssm_disc(cols)
    abr2, abi2 = abr.reshape(2, n_state), abi.reshape(2, n_state)
    bbr4, bbi4 = bbr.reshape(2, sg * sp, sh), bbi.reshape(2, sg * sp, sh)
    b_re = [_bd_in(bbr4[dr], sg, sp, sh).astype(BF16) for dr in range(2)]
    b_im = [_bd_in(bbi4[dr], sg, sp, sh).astype(BF16) for dr in range(2)]
    c_re = [_bd_out(sm["ssm_c_re"][dr], sg, sp, sh).astype(BF16) for dr in range(2)]
    c_im = [_bd_out(-sm["ssm_c_im"][dr], sg, sp, sh).astype(BF16) for dr in range(2)]
    sp32 = _to_segments(proj[:, d_pool:d_pool + d_ssm])
    xs, y_parts = [], []
    for dr in range(2):
        xr, xi, y_part = _ssm_fwd(f"ssm_fwd{dr}", sp32, b_re[dr], b_im[dr], c_re[dr], c_im[dr], abr2[dr:dr + 1],
                                  abi2[dr:dr + 1], reverse=(dr == 1))
        xs.append((xr, xi))
        y_parts.append(y_part)
    y = _from_segments(_ew("ssm_sum", lambda p0, p1, sv, dv: (p0 + p1 + sv * dv,), y_parts + [sp32], [F32],
                           rowvecs=[sm["ssm_d"]])[0])
    tmy = _pick(t, 256)
    ys = _ew("ssm_gelu", lambda v: (jax.nn.gelu(v),), [y], [BF16])[0]
    emit("mix_in_done", marker=ys)

    tmm, tnm, tnx = _pick(t, 1024), _pick(d, 256), _pick(d, 512)
    gp_spec = _tile(tmm, tnm, off_gp // tnm)
    gs_spec = _tile(tmm, tnm, off_gs // tnm)

    def merge_epi(accs, gpv, gsv):
        z_pool, val, gate = accs
        return (jax.nn.sigmoid(gpv) * z_pool + jax.nn.sigmoid(gsv) * (val * jax.nn.sigmoid(gate)),)

    merged = _mm("mix_merge", "nt", [pm, ys], [wt["w_pool_proj"], wt["w_glu_val"], wt["w_glu_gate"]],
                 [[(0, 0)], [(1, 1)], [(1, 2)]], t, d, tmm, tnm, [(proj, gp_spec), (proj, gs_spec)], merge_epi,
                 [(_out(t, d, BF16), None)])[0]
    res_epi = lambda accs, hin: (hin + accs[0],)
    h2 = _mm("mix_out", "nn", [merged], [wt["w_mix_out"]], [[(0, 0)]], t, d, tmm, tnx, [(h1, _tile(tmm, tnx))],
             res_epi, [(_out(t, d, F32), None)])[0]

    un = _rms_fwd("xattn_norm", h2, sm["xattn_norm"])
    mn = _rms_fwd("mem_norm", mem, sm["mem_norm"])
    emit("mix_done", marker=un)
    q = _mm1("xattn_q", "nn", un, wt["w_q"], t, d, tmm, tnx, BF16)
    kv = _mm1("xattn_kv", "nt", mn, wt["w_kv"], n_mem, 2 * d, n_mem, _pick(2 * d, 512), BF16)
    o = _attn_fwd(q, kv)
    h3 = _mm("xattn_out", "nn", [o], [wt["w_xo"]], [[(0, 0)]], t, d, tmm, tnx, [(h2, _tile(tmm, tnx))],
             res_epi, [(_out(t, d, F32), None)])[0]

    n2 = _rms_fwd("ffn2_norm", h3, sm["ffn2_norm"])
    emit("xattn_done", marker=n2)
    h4, ffn2_saved = _ffn_fwd("ffn2", h3, n2, wt["ffn2_w_gate"], wt["ffn2_w_up"], wt["ffn2_w_down"])

    dh4, dh4_bf, gs["final_norm"], loss = _loss_head(h4, sm["final_norm"], tgt)
    dh3, dh3_bf, gs["ffn2_norm"], gb["ffn2_w_gate"], gb["ffn2_w_up"], gb["ffn2_w_down"] = _ffn_bwd(
        "ffn2", h3, sm["ffn2_norm"], wt["ffn2_w_gate"], wt["ffn2_w_up"], wt["ffn2_w_down"], ffn2_saved, dh4, dh4_bf)

    tw = _pick(d, 1024)
    do = _mm1("xattn_do", "nt", dh3_bf, wt["w_xo"], t, d, tmm, tnx, BF16)
    gb["w_xo"] = _mm1("xattn_dwxo", "tn", o, dh3_bf, d, d, tw, tnx, BF16)
    dq, dkv = _attn_bwd(q, kv, do)
    gb["w_q"] = _mm1("xattn_dwq", "tn", un, dq, d, d, tw, tnx, BF16)
    dun = _mm1("xattn_dun", "nt", dq, wt["w_q"], t, d, tmm, tnx, F32)
    dh2, dh2_bf, gs["xattn_norm"] = _rms_bwd("xattn_norm_bwd", h2, sm["xattn_norm"], dun, dh3)
    gb["w_kv"] = _mm1("xattn_dwkv", "tn", dkv, mn, 2 * d, d, _pick(2 * d, 512), d, BF16)
    dmn = _mm1("xattn_dmn", "nn", dkv, wt["w_kv"], n_mem, d, n_mem, tnx, F32)
    gs["mem_norm"] = _rms_bwd("mem_norm_bwd", mem, sm["mem_norm"], dmn)

    gb["w_mix_out"] = _mm1("mix_dwout", "tn", merged, dh2_bf, d, d, tw, tnx, BF16)

    def merge_bwd_epi(accs, gpv, gsv):
        dmerged, z_pool, val, gate = accs
        sp_, ss_, sg_ = jax.nn.sigmoid(gpv), jax.nn.sigmoid(gsv), jax.nn.sigmoid(gate)
        glu = val * sg_
        dz_pool = dmerged * sp_
        dg_pool = dmerged * z_pool * (sp_ * (1.0 - sp_))
        dz_ssm = dmerged * ss_
        dg_ssm = dmerged * glu * (ss_ * (1.0 - ss_))
        dval = dz_ssm * sg_
        dgate = dz_ssm * glu * (1.0 - sg_)
        return dz_pool, dg_pool, dg_ssm, dval, dgate

    dz_pool, dg_pool, dg_ssm, dval, dgate = _mm(
        "mix_merge_bwd", "nt", [dh2_bf, pm, ys], [wt["w_mix_out"], wt["w_pool_proj"], wt["w_glu_val"], wt["w_glu_gate"]],
        [[(0, 0)], [(1, 1)], [(2, 2)], [(2, 3)]], t, d, tmm, tnm, [(proj, gp_spec), (proj, gs_spec)], merge_bwd_epi,
        [(_out(t, d, BF16), None)] * 5)
    gb["w_pool_proj"] = _mm1("pool_dwproj", "tn", dz_pool, pm, d, d_pool, tw, d_pool, BF16)
    gb["w_glu_val"] = _mm1("glu_dwval", "tn", dval, ys, d, d_ssm, tw, d_ssm, BF16)
    gb["w_glu_gate"] = _mm1("glu_dwgate", "tn", dgate, ys, d, d_ssm, tw, d_ssm, BF16)

    def gelu_bwd_epi(accs, yv):
        _, vjp = jax.vjp(jax.nn.gelu, yv)
        return (vjp(accs[0])[0],)

    dy = _mm("glu_dy", "nn", [dval, dgate], [wt["w_glu_val"], wt["w_glu_gate"]], [[(0, 0), (1, 1)]], t, d_ssm, tmy, d_ssm,
             [(y, _tile(tmy, d_ssm))], gelu_bwd_epi, [(_out(t, d_ssm, F32), None)])[0]
    gs["ssm_d"] = _colsum_prod("ssm_dd", dy, proj, b_coff=off_s)
    dyp = _to_segments(dy)
    d_abr, d_abi, d_bbr, d_bbi, d_cre, d_cim, lams = [], [], [], [], [], [], []
    ts = _pick(n_state, 512)
    tc_ = _pick(n_state, 256)
    both = lambda accs: tuple(accs)
    for dr in range(2):
        lr, li, dar, dai = _ssm_bwd(f"ssm_bwd{dr}", dyp, c_re[dr], c_im[dr], xs[dr][0], xs[dr][1], abr2[dr:dr + 1],
                                    abi2[dr:dr + 1], reverse=(dr == 1))
        d_abr.append(dar)
        d_abi.append(dai)
        lams += [lr, li]
        d_br, d_bi = _mm(f"ssm_db{dr}", "tn", [sp32], [lr, li], [[(0, 0)], [(0, 1)]], d_ssm, n_state, d_ssm, ts, [], both,
                         [(_out(d_ssm, n_state, F32), None)] * 2)
        d_bbr.append(_diag_in(d_br, sg, sp, sh))
        d_bbi.append(_diag_in(d_bi, sg, sp, sh))
        d_cr, d_ci = _mm(f"ssm_dc{dr}", "tn", [xs[dr][0], xs[dr][1]], [dyp], [[(0, 0)], [(1, 0)]], n_state, d_ssm, tc_,
                         d_ssm, [], both, [(_out(n_state, d_ssm, F32), None)] * 2)
        d_cre.append(_diag_out(d_cr, sg, sp, sh))
        d_cim.append(-_diag_out(d_ci, sg, sp, sh))
    ds = _from_segments(_mm(
        "ssm_ds", "nt", lams, [b_re[0], b_im[0], b_re[1], b_im[1]], [[(k, k) for k in range(4)]], t, d_ssm, tmy,
        d_ssm, [(dyp, _tile(tmy, d_ssm)), (sm["ssm_d"], _rowvec(d_ssm))],
        lambda accs, dyv, dv: (dyv * dv + accs[0],), [(_out(t, d_ssm, BF16), None)])[0])
    cots = [jnp.concatenate(d_abr, axis=0).reshape(-1, 1), jnp.concatenate(d_abi, axis=0).reshape(-1, 1),
            jnp.concatenate(d_bbr, axis=0), jnp.concatenate(d_bbi, axis=0)]
    d_are, d_aim, d_ldt, d_bre, d_bim = _ssm_disc_bwd(cols, cots)
    gs["ssm_a_re"] = d_are.reshape(2, sg, sp)
    gs["ssm_a_im"] = d_aim.reshape(2, sg, sp)
    gs["ssm_log_dt"] = _rowsum("ssm_dlogdt", d_ldt.reshape(2 * sg, sp)).reshape(2, sg)
    gs["ssm_b_re"] = d_bre.reshape(2, sg, sp, sh)
    gs["ssm_b_im"] = d_bim.reshape(2, sg, sp, sh)
    gs["ssm_c_re"] = jnp.stack(d_cre, axis=0)
    gs["ssm_c_im"] = jnp.stack(d_cim, axis=0)

    dpm = _mm1("pool_dpm", "nn", dz_pool, wt["w_pool_proj"], t, d_pool, tmm, _pick(d_pool, 256), F32)
    dp, gs["pool_w"], gs["pool_scale"] = _pool_bwd(pooled, dpm, pool_w_bf, sm["pool_scale"])

    w_in = wt["w_in"]
    parts = [(dp, 0, d_pool), (ds, d_pool, d_ssm), (dg_pool, off_gp, d), (dg_ssm, off_gs, d)]
    w_in_parts = [w_in[o0:o0 + width] for _, o0, width in parts]
    gb["w_in"] = jnp.concatenate(
        [_mm1(f"in_proj_dw{k}", "tn", p_[0], u, p_[2], d, _pick(p_[2], 1024), tnx, BF16) for k, p_ in enumerate(parts)], axis=0)
    pin = emit("grads_main", gb=gb)
    du = _mm("in_proj_du", "nn", [p_[0] for p_ in parts], w_in_parts, [[(k, k) for k in range(4)]], t, d, tmm, tnx, [],
             lambda accs: (accs[0],), [(_out(t, d, F32), None)], after=pin)[0]
    dh1, dh1_bf, gs["mix_norm"] = _rms_bwd("mix_norm_bwd", h1, sm["mix_norm"], du, dh2)
    pin = emit("small_early", gs=gs, loss=loss)

    def ffn1_weights_done(d_wg, d_wu, d_wd):
        gb["ffn1_w_gate"], gb["ffn1_w_up"], gb["ffn1_w_down"] = d_wg, d_wu, d_wd
        return emit("grads_ffn1", gb=gb)

    dx, _, gs["ffn1_norm"], _, _, _ = _ffn_bwd(
        "ffn1", x, sm["ffn1_norm"], wt["ffn1_w_gate"], wt["ffn1_w_up"], wt["ffn1_w_down"], ffn1_saved, dh1, dh1_bf,
        weights_done=ffn1_weights_done, after=pin)
    return loss, dx, gb, gs


WEIGHTS = ["ffn1_norm", "ffn1_w_gate", "ffn1_w_up", "ffn1_w_down", "mix_norm", "w_in", "pool_w", "pool_scale",
           "w_pool_proj", "ssm_a_re", "ssm_a_im", "ssm_log_dt", "ssm_b_re", "ssm_b_im", "ssm_c_re", "ssm_c_im", "ssm_d",
           "w_glu_val", "w_glu_gate", "w_mix_out", "xattn_norm", "mem_norm", "w_q", "w_kv", "w_xo", "ffn2_norm",
           "ffn2_w_gate", "ffn2_w_up", "ffn2_w_down", "final_norm"]
COL_SHARDED = ["ffn1_w_gate", "ffn1_w_up", "w_in", "w_pool_proj", "w_glu_val", "w_glu_gate", "w_kv", "ffn2_w_gate",
               "ffn2_w_up"]
ROW_SHARDED = ["ffn1_w_down", "w_mix_out", "w_q", "w_xo", "ffn2_w_down"]
BIG = [n for n in WEIGHTS if n in COL_SHARDED or n in ROW_SHARDED]
SMALL = [n for n in WEIGHTS if n not in BIG]
FFN1_BIG = ["ffn1_w_gate", "ffn1_w_up", "ffn1_w_down"]
MAIN_BIG = [n for n in BIG if n not in FFN1_BIG]
GATHER_PLAN = [("ffn1_up_done", ["ffn1_w_down"]), ("ffn1_fwd_done", ["w_in"]),
               ("mix_in_done", ["w_pool_proj", "w_glu_val", "w_glu_gate", "w_mix_out"]),
               ("mix_done", ["w_q", "w_kv", "w_xo"]), ("xattn_done", ["ffn2_w_gate", "ffn2_w_up", "ffn2_w_down"])]
LATE_SMALL = "ffn1_norm"
EARLY_SMALL = [n for n in SMALL if n != LATE_SMALL]
PACK_ROWS = SUBLANES * LANES
GRAD_ROW_TILE = 256


def _to_rows(name, w, width):
    if name in COL_SHARDED:
        w = w.T
    return w.reshape(-1, width)


def _from_rows(name, rows, shard_shape):
    if name in COL_SHARDED:
        return rows.reshape(shard_shape[1], shard_shape[0]).T
    return rows.reshape(shard_shape)


def _pack_small(vals):
    flat = []
    for v in vals:
        f = v.reshape(-1)
        flat.append(jnp.pad(f, (0, (-f.shape[0]) % PACK_ROWS)))
    total = sum(f.shape[0] for f in flat)
    flat.append(jnp.zeros(((-total) % (GRAD_ROW_TILE * LANES),), F32))
    return jnp.concatenate(flat).reshape(-1, LANES)


def _unpack_small(packed, shapes):
    out, row = [], 0
    for shp in shapes:
        size = math.prod(shp)
        rows = -(-size // PACK_ROWS) * SUBLANES
        out.append(packed[row:row + rows].reshape(-1)[:size].reshape(shp))
        row += rows
    return out


def kernel(x, mem, ffn1_norm, ffn1_w_gate, ffn1_w_up, ffn1_w_down, mix_norm, w_in, pool_w, pool_scale, w_pool_proj, ssm_a_re, ssm_a_im, ssm_log_dt, ssm_b_re, ssm_b_im, ssm_c_re, ssm_c_im, ssm_d, w_glu_val, w_glu_gate, w_mix_out, xattn_norm, mem_norm, w_q, w_kv, w_xo, ffn2_norm, ffn2_w_gate, ffn2_w_up, ffn2_w_down, final_norm, loss_target, m_ffn1_norm, m_ffn1_w_gate, m_ffn1_w_up, m_ffn1_w_down, m_mix_norm, m_w_in, m_pool_w, m_pool_scale, m_w_pool_proj, m_ssm_a_re, m_ssm_a_im, m_ssm_log_dt, m_ssm_b_re, m_ssm_b_im, m_ssm_c_re, m_ssm_c_im, m_ssm_d, m_w_glu_val, m_w_glu_gate, m_w_mix_out, m_xattn_norm, m_mem_norm, m_w_q, m_w_kv, m_w_xo, m_ffn2_norm, m_ffn2_w_gate, m_ffn2_w_up, m_ffn2_w_down, m_final_norm, v_ffn1_norm, v_ffn1_w_gate, v_ffn1_w_up, v_ffn1_w_down, v_mix_norm, v_w_in, v_pool_w, v_pool_scale, v_w_pool_proj, v_ssm_a_re, v_ssm_a_im, v_ssm_log_dt, v_ssm_b_re, v_ssm_b_im, v_ssm_c_re, v_ssm_c_im, v_ssm_d, v_w_glu_val, v_w_glu_gate, v_w_mix_out, v_xattn_norm, v_mem_norm, v_w_q, v_w_kv, v_w_xo, v_ffn2_norm, v_ffn2_w_gate, v_ffn2_w_up, v_ffn2_w_down, v_final_norm):
    given = dict(locals())
    wts = {n: given[n] for n in WEIGHTS}
    moms = {n: (given["m_" + n], given["v_" + n]) for n in WEIGHTS}
    x2, mem2, tgt2 = x[0], mem[0], loss_target[0]
    d = x2.shape[1]
    chip = (2 * lax.axis_index("x") + lax.axis_index("y")).astype(jnp.int32).reshape(1)

    def full_form(n, f):
        shard = wts[n][0].shape
        return f.reshape(N_DEV * shard[1], shard[0]) if n in COL_SHARDED else f.reshape(N_DEV * shard[0], shard[1])

    shards = {n: _to_rows(n, wts[n][0], d).astype(BF16) for n in BIG}
    first = FFN1_BIG[:2]
    wt = {n: full_form(n, f) for n, f in zip(first, _allgather("weight_allgather_first", [shards[n] for n in first]))}
    started = _split_start("weight_gather_start", [_gather_group([shards[n] for n in names]) for _, names in GATHER_PLAN],
                           after=wt[first[0]])
    gathers = {event: (names, st) for (event, names), st in zip(GATHER_PLAN, started)}
    sm = {n: (wts[n].reshape(1, -1) if wts[n].ndim <= 2 else wts[n][0]) for n in SMALL}
    sm["ffn1_norm"] = sm["ffn1_norm"] + started[0]["token"][0, 0]

    pending = {}

    def reduce_start(tag, names, gb):
        blocks = [gb[n].reshape(N_DEV, -1, d) for n in names]
        pad_rows = (-sum(b.shape[1] for b in blocks)) % GRAD_ROW_TILE
        pad = [jnp.zeros((N_DEV, pad_rows, d), BF16)] if pad_rows else []
        own, recv = _exchange_cores("grad_exchange_cores_" + tag, blocks + pad)
        rows_all = own.shape[1]
        pair = _ew("grad_pair_sum_" + tag, lambda a, b: (a.astype(F32) + b.astype(F32),),
                   [own.reshape(-1, d), recv.reshape(-1, d)], [BF16], rows_pref=5 * GRAD_ROW_TILE)[0]
        pair = pair.reshape(4, rows_all, d)
        pending[tag] = (pair, _chips_start("grad_exchange_chips_start_" + tag, pair), [b.shape[1] for b in blocks])
        return pending[tag][1]["token"]

    def reduce_finish(tag, after):
        _, started, rows = pending[tag]
        (pair,), (recv,) = _split_wait("grad_exchange_chips_wait_" + tag, started, after)
        return _chip_sum("grad_chip_sum_" + tag, pair, recv, chip), rows

    def ev(name, gb=None, gs=None, loss=None, marker=None):
        if name in gathers:
            names, started = gathers[name]
            for n, f in zip(names, _split_wait("weight_gather_wait_" + name, started, marker)[1]):
                wt[n] = full_form(n, f)
        elif name == "grads_main":
            return reduce_start("main", MAIN_BIG, gb)
        elif name == "small_early":
            pending["small"] = _slots_start("small_gather_start", _pack_small([gs[n] for n in EARLY_SMALL] + [loss[:, :1]]))
            return pending["small"]["token"]
        elif name == "grads_ffn1":
            return reduce_start("ffn1", FFN1_BIG, gb)
        return None

    _, dx, _, gs = _local_step(x2, mem2, tgt2, wt, sm, ev)

    out_g, out_d, out_m, out_v = {}, {}, {}, {}

    def update(n, g_full):
        shape = wts[n].shape
        two_d = (-1, shape[-1])
        dl, m2, v2 = _adamw("adamw_" + n, wts[n].reshape(two_d), g_full.reshape(two_d), moms[n][0].reshape(two_d),
                            moms[n][1].reshape(two_d))
        out_g[n], out_d[n], out_m[n], out_v[n] = g_full, dl.reshape(shape), m2.reshape(shape), v2.reshape(shape)
        return dl

    def update_big(names, g_rows, rows):
        off = 0
        for n, r in zip(names, rows):
            shard = wts[n].shape
            dl = update(n, _from_rows(n, g_rows[off:off + r], shard[1:]).reshape(shard))
            off += r
        return dl

    last = update_big(MAIN_BIG, *reduce_finish("main", dx))

    small_sum = _sum_slots("small_sum", _split_wait("small_gather_wait", pending["small"], dx)[1][0], F32)
    late = _allgather("small_allgather_late", [gs[LATE_SMALL].reshape(-1, LANES)])[0]
    late_sum = _sum_slots("small_sum_late", late.reshape(N_DEV, -1, LANES), F32)
    vals = _unpack_small(small_sum, [wts[n].shape for n in EARLY_SMALL] + [(1, 1)])
    total_loss = vals[-1].reshape(())
    for n, g_full in zip(EARLY_SMALL + [LATE_SMALL], vals[:-1] + [late_sum.reshape(wts[LATE_SMALL].shape)]):
        update(n, g_full)

    update_big(FFN1_BIG, *reduce_finish("ffn1", last))

    return (total_loss, dx[None], *[out_g[n] for n in WEIGHTS], *[out_d[n] for n in WEIGHTS],
            *[out_m[n] for n in WEIGHTS], *[out_v[n] for n in WEIGHTS])
```

```python
import functools
import math

import jax
import jax.numpy as jnp
from jax import lax
from jax.experimental import pallas as pl
from jax.experimental.pallas import tpu as pltpu

F32 = jnp.float32
BF16 = jnp.bfloat16
EPS = 1e-6
N_XHEADS = 4
POOL_WINDOWS = (2, 4, 8, 16)
ADAM_LR = 0.001
ADAM_B1 = 0.9
ADAM_B2 = 0.999
ADAM_EPS = 1e-08
ADAM_WD = 0.01
ADAM_STEP = 10
N_DEV = 8
VMEM_LIMIT_V7X = 48 * 1024 * 1024
LANES = 128
SUBLANES = 8
SUB_ROWS = 256
POOL_PAD = 16
MESH = pl.DeviceIdType.MESH
ANY = pl.BlockSpec(memory_space=pl.ANY)
HBM = pl.BlockSpec(memory_space=pltpu.HBM)
SEM = pl.BlockSpec(memory_space=pltpu.SEMAPHORE)
SIDE_EFFECT = pltpu.SideEffectType.DATAFLOW_SIDE_EFFECTING

_DIMS = {
    "nt": (((1,), (1,)), ((), ())),
    "nn": (((1,), (0,)), ((), ())),
    "tn": (((0,), (0,)), ((), ())),
}


def _pick(dim, pref, mult=LANES):
    if dim <= pref:
        return dim
    for t in range(pref - pref % mult, 0, -mult):
        if dim % t == 0:
            return t
    return dim


def _params(sem):
    return pltpu.CompilerParams(dimension_semantics=sem, vmem_limit_bytes=VMEM_LIMIT_V7X)


def _tile(tm, tn, coff=0):
    return pl.BlockSpec((tm, tn), lambda i, j: (i, j + coff))


def _rowvec(tn, coff=0):
    return pl.BlockSpec((1, tn), lambda i, j: (0, j + coff))


def _out(m, n, dtype):
    return jax.ShapeDtypeStruct((m, n), dtype)


def _mm(name, form, a_list, b_list, groups, m, n, tm, tn, extras, epilogue, outs, after=None, sub=SUB_ROWS):
    na, nb, ne = len(a_list), len(b_list), len(extras)
    pins = [] if after is None else [after]
    step = tm if (sub is None or form == "tn" or tm % sub) else sub

    def a_spec(a):
        if form == "tn":
            return pl.BlockSpec((a.shape[0], tm), lambda i, j: (0, i))
        return pl.BlockSpec((tm, a.shape[1]), lambda i, j: (i, 0))

    def b_spec(b):
        if form == "nt":
            return pl.BlockSpec((tn, b.shape[1]), lambda i, j: (j, 0))
        return pl.BlockSpec((b.shape[0], tn), lambda i, j: (0, j))

    def body(*refs):
        a_refs, b_refs = refs[:na], refs[na:na + nb]
        e_refs, o_refs = refs[na + nb:na + nb + ne], refs[na + nb + ne + len(pins):]
        b_vals = {}
        for s0 in range(0, tm, step):
            rows = slice(None) if step == tm else pl.ds(s0, step)
            a_vals, accs = {}, []
            for group in groups:
                acc = None
                for ai, bi in group:
                    if ai not in a_vals:
                        a_vals[ai] = (a_refs[ai][...] if form == "tn" else a_refs[ai][rows, :]).astype(BF16)
                    if bi not in b_vals:
                        b_vals[bi] = b_refs[bi][...].astype(BF16)
                    d = lax.dot_general(a_vals[ai], b_vals[bi], _DIMS[form], preferred_element_type=F32)
                    acc = d if acc is None else acc + d
                accs.append(acc)
            res = epilogue(accs, *[e[rows, :] if e.shape[0] == tm else e[...] for e in e_refs])
            for o_ref, r in zip(o_refs, res):
                o_ref[rows, :] = r.astype(o_ref.dtype)

    out_specs = [_tile(tm, tn) if s is None else s for _, s in outs]
    res = pl.pallas_call(
        body, name=name, grid=(m // tm, n // tn),
        in_specs=[a_spec(a) for a in a_list] + [b_spec(b) for b in b_list] + [s for _, s in extras] + [ANY] * len(pins),
        out_specs=out_specs, out_shape=[o for o, _ in outs],
        compiler_params=_params(("parallel", "parallel")),
    )(*a_list, *b_list, *[e for e, _ in extras], *pins)
    return res


def _mm1(name, form, a, b, m, n, tm, tn, dtype, scale=None):
    epi = (lambda accs: (accs[0],)) if scale is None else (lambda accs: (accs[0] * scale,))
    return _mm(name, form, [a], [b], [[(0, 0)]], m, n, tm, tn, [], epi, [(_out(m, n, dtype), None)])[0]


def _rms_fwd(name, h, g):
    t, d = h.shape
    tm = _pick(t, 512, SUBLANES)

    def body(h_ref, g_ref, n_ref):
        hv = h_ref[...]
        r = lax.rsqrt(jnp.mean(hv * hv, axis=-1, keepdims=True) + EPS)
        n_ref[...] = ((hv * r) * g_ref[...]).astype(BF16)

    return pl.pallas_call(
        body, name=name, grid=(t // tm,),
        in_specs=[pl.BlockSpec((tm, d), lambda i: (i, 0)), pl.BlockSpec((1, d), lambda i: (0, 0))],
        out_specs=pl.BlockSpec((tm, d), lambda i: (i, 0)), out_shape=_out(t, d, BF16),
        compiler_params=_params(("parallel",)),
    )(h, g)


def _rms_bwd(name, h, g, dn, dres=None):
    t, d = h.shape
    tm = _pick(t, 512, SUBLANES)
    need_dh = dres is not None

    def body(*refs):
        if need_dh:
            h_ref, g_ref, dn_ref, dres_ref, dh_ref, dhb_ref, dg_ref = refs
        else:
            h_ref, g_ref, dn_ref, dg_ref = refs
        hv = h_ref[...]
        r = lax.rsqrt(jnp.mean(hv * hv, axis=-1, keepdims=True) + EPS)
        nh = hv * r
        dnv = dn_ref[...].astype(F32)

        @pl.when(pl.program_id(0) == 0)
        def _():
            dg_ref[...] = jnp.zeros_like(dg_ref)

        dg_ref[...] += jnp.sum(dnv * nh, axis=0, keepdims=True)
        if need_dh:
            dng = dnv * g_ref[...]
            dh = dres_ref[...] + r * (dng - nh * jnp.mean(dng * nh, axis=-1, keepdims=True))
            dh_ref[...] = dh
            dhb_ref[...] = dh.astype(BF16)

    row = pl.BlockSpec((tm, d), lambda i: (i, 0))
    vec = pl.BlockSpec((1, d), lambda i: (0, 0))
    if need_dh:
        return pl.pallas_call(
            body, name=name, grid=(t // tm,), in_specs=[row, vec, row, row], out_specs=[row, row, vec],
            out_shape=[_out(t, d, F32), _out(t, d, BF16), _out(1, d, F32)], compiler_params=_params(("arbitrary",)),
        )(h, g, dn, dres)
    return pl.pallas_call(
        body, name=name, grid=(t // tm,), in_specs=[row, vec, row], out_specs=vec,
        out_shape=_out(1, d, F32), compiler_params=_params(("arbitrary",)),
    )(h, g, dn)


def _loss_head(h, g, tgt):
    t, d = h.shape
    tm = _pick(t, 512, SUBLANES)

    def body(h_ref, g_ref, t_ref, dh_ref, dhb_ref, dg_ref, loss_ref):
        hv = h_ref[...]
        r = lax.rsqrt(jnp.mean(hv * hv, axis=-1, keepdims=True) + EPS)
        nh = hv * r
        err = nh * g_ref[...] - t_ref[...]

        @pl.when(pl.program_id(0) == 0)
        def _():
            dg_ref[...] = jnp.zeros_like(dg_ref)
            loss_ref[...] = jnp.zeros_like(loss_ref)

        per_row = jnp.mean(err * err, axis=-1, keepdims=True)
        loss_ref[...] += 0.5 * jnp.sum(per_row, axis=0, keepdims=True)
        dy = err * (1.0 / d)
        dg_ref[...] += jnp.sum(dy * nh, axis=0, keepdims=True)
        dng = dy * g_ref[...]
        dh = r * (dng - nh * jnp.mean(dng * nh, axis=-1, keepdims=True))
        dh_ref[...] = dh
        dhb_ref[...] = dh.astype(BF16)

    row = pl.BlockSpec((tm, d), lambda i: (i, 0))
    vec = pl.BlockSpec((1, d), lambda i: (0, 0))
    return pl.pallas_call(
        body, name="loss_head", grid=(t // tm,), in_specs=[row, vec, row],
        out_specs=[row, row, vec, pl.BlockSpec((1, LANES), lambda i: (0, 0))],
        out_shape=[_out(t, d, F32), _out(t, d, BF16), _out(1, d, F32), _out(1, LANES, F32)],
        compiler_params=_params(("arbitrary",)),
    )(h, g, tgt)


def _ffn_fwd(tag, h, n, wg_t, wu_t, wd):
    t, d = h.shape
    f = wg_t.shape[0]
    tm, tn = _pick(t, 1024), _pick(f, 1408)

    def up_epi(accs):
        a, b = accs
        return a, b, (a * jax.nn.sigmoid(a)) * b

    a, b, hid = _mm(tag + "_up", "nt", [n], [wg_t, wu_t], [[(0, 0)], [(0, 1)]], t, f, tm, tn, [], up_epi,
                    [(_out(t, f, BF16), None)] * 3)
    if callable(wd):
        wd = wd(hid)
    tm2, tn2 = _pick(t, 1024), _pick(d, 512)
    h_out = _mm(tag + "_down", "nn", [hid], [wd], [[(0, 0)]], t, d, tm2, tn2, [(h, _tile(tm2, tn2))],
                lambda accs, hin: (hin + 0.5 * accs[0],), [(_out(t, d, F32), None)])[0]
    return h_out, (n, a, b, hid)


def _ffn_bwd(tag, h, g, wg_t, wu_t, wd, saved, dh, dh_bf, weights_done=None, after=None):
    n, a, b, hid = saved
    t, d = h.shape
    f = wd.shape[0]
    tm, tn = _pick(t, 1024), _pick(f, 1408)

    def hid_epi(accs, av, bv):
        dhid = 0.5 * accs[0]
        av, bv = av.astype(F32), bv.astype(F32)
        sig = jax.nn.sigmoid(av)
        da = dhid * bv * (sig * (1.0 + av * (1.0 - sig)))
        db = dhid * (av * sig)
        return da, db

    da, db = _mm(tag + "_bwd_hid", "nt", [dh_bf], [wd], [[(0, 0)]], t, f, tm, tn,
                 [(a, _tile(tm, tn)), (b, _tile(tm, tn))], hid_epi, [(_out(t, f, BF16), None)] * 2, after=after)
    tw, tnw = _pick(f, 1408), _pick(d, 512)
    d_wd = _mm1(tag + "_dwd", "tn", hid, dh_bf, f, d, tw, tnw, BF16, scale=0.5)
    d_wg = _mm1(tag + "_dwg", "tn", da, n, f, d, tw, tnw, BF16)
    d_wu = _mm1(tag + "_dwu", "tn", db, n, f, d, tw, tnw, BF16)
    pin = weights_done(d_wg, d_wu, d_wd) if weights_done is not None else None
    tm2, tn2 = _pick(t, 1024), _pick(d, 512)
    dn = _mm(tag + "_dn", "nn", [da, db], [wg_t, wu_t], [[(0, 0), (1, 1)]], t, d, tm2, tn2, [],
             lambda accs: (accs[0],), [(_out(t, d, F32), None)], after=pin)[0]
    dh_in, dh_in_bf, dg = _rms_bwd(tag + "_norm_bwd", h, g, dn, dh)
    return dh_in, dh_in_bf, dg, d_wg, d_wu, d_wd


def _window_sum(win, offsets):
    n = win.shape[0]
    acc = None
    for j in offsets:
        term = win if j == 0 else pltpu.roll(win, (-j) % n, 0)
        acc = term if acc is None else acc + term
    return acc


def _pool_counts(r0, ch, c, left, right, t):
    pos = r0 + lax.broadcasted_iota(jnp.int32, (ch, c), 0)
    return (jnp.minimum(pos + right + 1, t) - jnp.maximum(pos - left, 0)).astype(F32)


def _pool_fwd(proj, pool_w_bf, pool_scale):
    t = proj.shape[0]
    ng, c, _ = pool_w_bf.shape
    ch = _pick(t, 256, SUBLANES)
    pad = POOL_PAD

    def body(p_ref, w_ref, s_ref, pooled_ref, pm_ref, buf):
        grp = pl.program_id(0)
        buf[pl.ds(0, pad), :] = jnp.zeros((pad, c), F32)
        buf[pl.ds(pad + t, pad), :] = jnp.zeros((pad, c), F32)

        def fill(ci, carry):
            r0 = pl.multiple_of(ci * ch, SUBLANES)
            buf[pl.ds(pl.multiple_of(r0 + pad, SUBLANES), ch), :] = p_ref[pl.ds(r0, ch), :]
            return carry

        lax.fori_loop(0, t // ch, fill, 0)
        for gi, w in enumerate(POOL_WINDOWS):
            left = w // 2
            right = w - 1 - left

            @pl.when(grp == gi)
            def _(left=left, right=right):
                def chunk(ci, carry):
                    r0 = pl.multiple_of(ci * ch, SUBLANES)
                    win = buf[pl.ds(r0, ch + 2 * pad), :]
                    s = _window_sum(win, range(-left, right + 1))[pad:pad + ch]
                    pooled = s / _pool_counts(r0, ch, c, left, right, t) - win[pad:pad + ch]
                    pooled_bf = pooled.astype(BF16)
                    mixed = jnp.dot(pooled_bf, w_ref[0], preferred_element_type=F32)
                    pooled_ref[pl.ds(r0, ch), :] = pooled_bf
                    pm_ref[pl.ds(r0, ch), :] = (mixed * s_ref[...]).astype(BF16)
                    return carry

                lax.fori_loop(0, t // ch, chunk, 0)

    col = pl.BlockSpec((t, c), lambda g: (0, g))
    return pl.pallas_call(
        body, name="pool_fwd", grid=(ng,),
        in_specs=[col, pl.BlockSpec((1, c, c), lambda g: (g, 0, 0)), pl.BlockSpec((1, c), lambda g: (0, g))],
        out_specs=[col, col], out_shape=[_out(t, ng * c, BF16), _out(t, ng * c, BF16)],
        scratch_shapes=[pltpu.VMEM((t + 2 * pad, c), F32)],
        compiler_params=_params(("parallel",)),
    )(proj, pool_w_bf, pool_scale)


def _pool_bwd(pooled, dpm, pool_w_bf, pool_scale):
    t = pooled.shape[0]
    ng, c, _ = pool_w_bf.shape
    ch = _pick(t, 256, SUBLANES)
    pad = POOL_PAD

    def body(pooled_ref, dpm_ref, w_ref, s_ref, dp_ref, dw_ref, ds_ref, buf, raw):
        grp = pl.program_id(0)
        buf[pl.ds(0, pad), :] = jnp.zeros((pad, c), F32)
        buf[pl.ds(pad + t, pad), :] = jnp.zeros((pad, c), F32)
        dw_ref[...] = jnp.zeros_like(dw_ref)
        ds_ref[...] = jnp.zeros_like(ds_ref)
        for gi, w in enumerate(POOL_WINDOWS):
            left = w // 2
            right = w - 1 - left

            @pl.when(grp == gi)
            def _(left=left, right=right):
                def first(ci, carry):
                    r0 = pl.multiple_of(ci * ch, SUBLANES)
                    pv = pooled_ref[pl.ds(r0, ch), :]
                    dpm_v = dpm_ref[pl.ds(r0, ch), :]
                    mixed = jnp.dot(pv, w_ref[0], preferred_element_type=F32)
                    ds_ref[...] += jnp.sum(dpm_v * mixed, axis=0, keepdims=True)
                    dmixed = (dpm_v * s_ref[...]).astype(BF16)
                    dw_ref[0] += lax.dot_general(pv, dmixed, _DIMS["tn"], preferred_element_type=F32)
                    dpooled = lax.dot_general(dmixed, w_ref[0], _DIMS["nt"], preferred_element_type=F32)
                    raw[pl.ds(r0, ch), :] = dpooled
                    buf[pl.ds(pl.multiple_of(r0 + pad, SUBLANES), ch), :] = (
                        dpooled / _pool_counts(r0, ch, c, left, right, t))
                    return carry

                lax.fori_loop(0, t // ch, first, 0)

                def second(ci, carry):
                    r0 = pl.multiple_of(ci * ch, SUBLANES)
                    win = buf[pl.ds(r0, ch + 2 * pad), :]
                    s = _window_sum(win, range(-right, left + 1))[pad:pad + ch]
                    dp_ref[pl.ds(r0, ch), :] = (s - raw[pl.ds(r0, ch), :]).astype(BF16)
                    return carry

                lax.fori_loop(0, t // ch, second, 0)

    col = pl.BlockSpec((t, c), lambda g: (0, g))
    return pl.pallas_call(
        body, name="pool_bwd", grid=(ng,),
        in_specs=[col, col, pl.BlockSpec((1, c, c), lambda g: (g, 0, 0)), pl.BlockSpec((1, c), lambda g: (0, g))],
        out_specs=[col, pl.BlockSpec((1, c, c), lambda g: (g, 0, 0)), pl.BlockSpec((1, c), lambda g: (0, g))],
        out_shape=[_out(t, ng * c, BF16), jax.ShapeDtypeStruct((ng, c, c), F32), _out(1, ng * c, F32)],
        scratch_shapes=[pltpu.VMEM((t + 2 * pad, c), F32), pltpu.VMEM((t, c), F32)],
        compiler_params=_params(("parallel",)),
    )(pooled, dpm, pool_w_bf, pool_scale)


def _discretise(a_re, a_im, log_dt, b_re, b_im):
    dt = jnp.exp(log_dt)
    mag = jnp.exp(dt * a_re)
    ang = dt * a_im
    abr = mag * jnp.cos(ang)
    abi = mag * jnp.sin(ang)
    den = a_re * a_re + a_im * a_im
    nr = abr - 1.0
    qr = (nr * a_re + abi * a_im) / den
    qi = (abi * a_re - nr * a_im) / den
    return abr, abi, qr * b_re - qi * b_im, qr * b_im + qi * b_re


def _ssm_disc(cols):
    n, hh = cols[3].shape

    def body(ar, ai, ld, br, bi, o1, o2, o3, o4):
        res = _discretise(ar[...], ai[...], ld[...], br[...], bi[...])
        for o, r in zip((o1, o2, o3, o4), res):
            o[...] = r

    return pl.pallas_call(
        body, name="ssm_disc",
        out_shape=[_out(n, 1, F32), _out(n, 1, F32), _out(n, hh, F32), _out(n, hh, F32)],
    )(*cols)


def _ssm_disc_bwd(cols, cots):
    n, hh = cols[3].shape

    def body(ar, ai, ld, br, bi, c1, c2, c3, c4, o1, o2, o3, o4, o5):
        _, vjp = jax.vjp(_discretise, ar[...], ai[...], ld[...], br[...], bi[...])
        res = vjp((c1[...], c2[...], c3[...], c4[...]))
        for o, r in zip((o1, o2, o3, o4, o5), res):
            o[...] = r

    return pl.pallas_call(
        body, name="ssm_disc_bwd",
        out_shape=[_out(n, 1, F32)] * 3 + [_out(n, hh, F32)] * 2,
    )(*cols, *cots)


def _rowsum(name, a):
    r, _ = a.shape

    def body(a_ref, o_ref):
        o_ref[...] = jnp.sum(a_ref[...], axis=-1, keepdims=True)

    return pl.pallas_call(body, name=name, out_shape=_out(r, 1, F32))(a)


def _cmul(pr, pi, qr, qi):
    return pr * qr - pi * qi, pr * qi + pi * qr


def _cpow(pr, pi, n):
    rr, ri = None, None
    while n:
        if n & 1:
            rr, ri = (pr, pi) if rr is None else _cmul(rr, ri, pr, pi)
        n >>= 1
        if n:
            pr, pi = _cmul(pr, pi, pr, pi)
    return rr, ri


def _segment_carry(er, ei, pr, pi, reverse):
    row = lax.broadcasted_iota(jnp.int32, er.shape, 0)
    cr, ci = jnp.zeros_like(er), jnp.zeros_like(ei)
    for _ in range(SUBLANES - 1):
        tr = er + pr * cr - pi * ci
        ti = ei + pr * ci + pi * cr
        if reverse:
            keep, shift = row < SUBLANES - 1, SUBLANES - 1
        else:
            keep, shift = row >= 1, 1
        cr = jnp.where(keep, pltpu.roll(tr, shift, 0), 0.0)
        ci = jnp.where(keep, pltpu.roll(ti, shift, 0), 0.0)
    return cr, ci


def _ssm_fwd(name, sp, b_re, b_im, c_re, c_im, ar, ai, reverse):
    t, c = sp.shape
    s = ar.shape[1]
    w = _pick(s, 512)
    ch = _pick(t, 512, SUBLANES)
    n_ch, gpc, steps = t // ch, ch // SUBLANES, t // SUBLANES

    def body(sp_ref, bre_ref, bim_ref, cre_ref, cim_ref, ar_ref, ai_ref, xr_ref, xi_ref, y_ref, ur, ui, xbr, xbi):
        a_r = jnp.broadcast_to(ar_ref[...], (SUBLANES, w))
        a_i = jnp.broadcast_to(ai_ref[...], (SUBLANES, w))

        @pl.when(pl.program_id(0) == 0)
        def _():
            y_ref[...] = jnp.zeros_like(y_ref)

        def sweep(h0, store):
            def chunk(k, h):
                ci = n_ch - 1 - k if reverse else k
                rows = pl.ds(pl.multiple_of(ci * ch, ch), ch)
                spv = sp_ref[rows, :].astype(BF16)
                ur[...] = jnp.dot(spv, bre_ref[...], preferred_element_type=F32)
                ui[...] = jnp.dot(spv, bim_ref[...], preferred_element_type=F32)

                def group(g, hh):
                    gi = gpc - 1 - g if reverse else g
                    r0 = pl.multiple_of(gi * SUBLANES, SUBLANES)
                    hr, hi = hh
                    nr = a_r * hr - a_i * hi + ur[pl.ds(r0, SUBLANES), :]
                    ni = a_r * hi + a_i * hr + ui[pl.ds(r0, SUBLANES), :]
                    if store:
                        xbr[pl.ds(r0, SUBLANES), :] = nr
                        xbi[pl.ds(r0, SUBLANES), :] = ni
                    return nr, ni

                h = lax.fori_loop(0, gpc, group, h)
                if store:
                    xr16, xi16 = xbr[...].astype(BF16), xbi[...].astype(BF16)
                    xr_ref[rows, :] = xr16
                    xi_ref[rows, :] = xi16
                    y_ref[rows, :] += (jnp.dot(xr16, cre_ref[...], preferred_element_type=F32)
                                       + jnp.dot(xi16, cim_ref[...], preferred_element_type=F32))
                return h

            return lax.fori_loop(0, n_ch, chunk, h0)

        zero = jnp.zeros((SUBLANES, w), F32)
        er, ei = sweep((zero, zero), False)
        pr, pi = _cpow(ar_ref[...], ai_ref[...], steps)
        sweep(_segment_carry(er, ei, pr, pi, reverse), True)

    col = lambda i: (0, i)
    return pl.pallas_call(
        body, name=name, grid=(s // w,),
        in_specs=[pl.BlockSpec((t, c), lambda i: (0, 0)), pl.BlockSpec((c, w), col), pl.BlockSpec((c, w), col),
                  pl.BlockSpec((w, c), lambda i: (i, 0)), pl.BlockSpec((w, c), lambda i: (i, 0)),
                  pl.BlockSpec((1, w), col), pl.BlockSpec((1, w), col)],
        out_specs=[pl.BlockSpec((t, w), col), pl.BlockSpec((t, w), col), pl.BlockSpec((t, c), lambda i: (0, 0))],
        out_shape=[_out(t, s, BF16), _out(t, s, BF16), _out(t, c, F32)],
        scratch_shapes=[pltpu.VMEM((ch, w), F32)] * 4,
        compiler_params=_params(("arbitrary",)),
    )(sp, b_re, b_im, c_re, c_im, ar, ai)


def _ssm_bwd(name, dyp, c_re, c_im, xr, xi, ar, ai, reverse):
    t, c = dyp.shape
    s = ar.shape[1]
    w = _pick(s, 512)
    ch = _pick(t, 512, SUBLANES)
    n_ch, gpc, steps = t // ch, ch // SUBLANES, t // SUBLANES
    back = not reverse
    edge = 2 * SUBLANES

    def body(dy_ref, cre_ref, cim_ref, xr_ref, xi_ref, ar_ref, ai_ref, lr_ref, li_ref, dar_ref, dai_ref,
             gr, gi_, lbr, lbi, xbr, xbi):
        a_r = jnp.broadcast_to(ar_ref[...], (SUBLANES, w))
        a_i = -jnp.broadcast_to(ai_ref[...], (SUBLANES, w))
        row = lax.broadcasted_iota(jnp.int32, (SUBLANES, w), 0)

        def neighbours(ci, x_ref, buf):
            rows = pl.ds(pl.multiple_of(ci * ch, ch), ch)
            if reverse:
                buf[pl.ds(0, ch), :] = x_ref[rows, :].astype(F32)
                nxt = x_ref[pl.ds(pl.multiple_of(jnp.minimum(ci + 1, n_ch - 1) * ch, ch), edge), :].astype(F32)[:SUBLANES]
                first = x_ref[pl.ds(0, edge), :].astype(F32)[:SUBLANES]
                wrap = jnp.where(row < SUBLANES - 1, pltpu.roll(first, SUBLANES - 1, 0), 0.0)
                buf[pl.ds(ch, SUBLANES), :] = jnp.where(ci == n_ch - 1, wrap, nxt)
            else:
                buf[pl.ds(SUBLANES, ch), :] = x_ref[rows, :].astype(F32)
                prv = x_ref[pl.ds(pl.multiple_of(jnp.maximum(ci * ch - edge, 0), edge), edge), :].astype(F32)[SUBLANES:]
                last = x_ref[pl.ds(t - edge, edge), :].astype(F32)[SUBLANES:]
                wrap = jnp.where(row >= 1, pltpu.roll(last, 1, 0), 0.0)
                buf[pl.ds(0, SUBLANES), :] = jnp.where(ci == 0, wrap, prv)

        def sweep(h0, store):
            def chunk(k, carry):
                ci = n_ch - 1 - k if back else k
                rows = pl.ds(pl.multiple_of(ci * ch, ch), ch)
                dyv = dy_ref[rows, :].astype(BF16)
                gr[...] = lax.dot_general(dyv, cre_ref[...], _DIMS["nt"], preferred_element_type=F32)
                gi_[...] = lax.dot_general(dyv, cim_ref[...], _DIMS["nt"], preferred_element_type=F32)
                if store:
                    neighbours(ci, xr_ref, xbr)
                    neighbours(ci, xi_ref, xbi)

                def group(g, cc):
                    gidx = gpc - 1 - g if back else g
                    r0 = pl.multiple_of(gidx * SUBLANES, SUBLANES)
                    hr, hi = cc[0], cc[1]
                    nr = a_r * hr - a_i * hi + gr[pl.ds(r0, SUBLANES), :]
                    ni = a_r * hi + a_i * hr + gi_[pl.ds(r0, SUBLANES), :]
                    if not store:
                        return nr, ni
                    lbr[pl.ds(r0, SUBLANES), :] = nr
                    lbi[pl.ds(r0, SUBLANES), :] = ni
                    x0 = pl.multiple_of(r0 + SUBLANES, SUBLANES) if reverse else r0
                    xpr, xpi = xbr[pl.ds(x0, SUBLANES), :], xbi[pl.ds(x0, SUBLANES), :]
                    return nr, ni, cc[2] + nr * xpr + ni * xpi, cc[3] + ni * xpr - nr * xpi

                carry = lax.fori_loop(0, gpc, group, carry)
                if store:
                    lr_ref[rows, :] = lbr[...].astype(BF16)
                    li_ref[rows, :] = lbi[...].astype(BF16)
                return carry

            return lax.fori_loop(0, n_ch, chunk, h0)

        zero = jnp.zeros((SUBLANES, w), F32)
        er, ei = sweep((zero, zero), False)
        pr, pi = _cpow(ar_ref[...], -ai_ref[...], steps)
        cr, ci0 = _segment_carry(er, ei, pr, pi, back)
        _, _, dar, dai = sweep((cr, ci0, zero, zero), True)
        dar_ref[...] = jnp.sum(dar, axis=0, keepdims=True)
        dai_ref[...] = jnp.sum(dai, axis=0, keepdims=True)

    col = lambda i: (0, i)
    return pl.pallas_call(
        body, name=name, grid=(s // w,),
        in_specs=[pl.BlockSpec((t, c), lambda i: (0, 0)), pl.BlockSpec((w, c), lambda i: (i, 0)),
                  pl.BlockSpec((w, c), lambda i: (i, 0)), pl.BlockSpec((t, w), col), pl.BlockSpec((t, w), col),
                  pl.BlockSpec((1, w), col), pl.BlockSpec((1, w), col)],
        out_specs=[pl.BlockSpec((t, w), col), pl.BlockSpec((t, w), col), pl.BlockSpec((1, w), col), pl.BlockSpec((1, w), col)],
        out_shape=[_out(t, s, BF16), _out(t, s, BF16), _out(1, s, F32), _out(1, s, F32)],
        scratch_shapes=[pltpu.VMEM((ch, w), F32)] * 4 + [pltpu.VMEM((ch + SUBLANES, w), F32)] * 2,
        compiler_params=_params(("parallel",)),
    )(dyp, c_re, c_im, xr, xi, ar, ai)


def _to_segments(a):
    t, c = a.shape
    return a.reshape(SUBLANES, t // SUBLANES, c).transpose(1, 0, 2).reshape(t, c)


def _from_segments(a):
    t, c = a.shape
    return a.reshape(t // SUBLANES, SUBLANES, c).transpose(1, 0, 2).reshape(t, c)


def _colsum_prod(name, a, b, b_coff=0):
    t, n = a.shape
    tm = _pick(t, 512, SUBLANES)

    def body(a_ref, b_ref, o_ref):
        @pl.when(pl.program_id(0) == 0)
        def _():
            o_ref[...] = jnp.zeros_like(o_ref)

        o_ref[...] += jnp.sum(a_ref[...].astype(F32) * b_ref[...].astype(F32), axis=0, keepdims=True)

    return pl.pallas_call(
        body, name=name, grid=(t // tm,),
        in_specs=[pl.BlockSpec((tm, n), lambda i: (i, 0)), pl.BlockSpec((tm, n), lambda i: (i, b_coff))],
        out_specs=pl.BlockSpec((1, n), lambda i: (0, 0)), out_shape=_out(1, n, F32),
        compiler_params=_params(("arbitrary",)),
    )(a, b)


def _bd_in(bb, g, p, hh):
    blk = bb.reshape(g, p, hh).transpose(0, 2, 1)
    eye = jnp.eye(g, dtype=bool)[:, None, :, None]
    return jnp.where(eye, blk[:, :, None, :], 0.0).reshape(g * hh, g * p)


def _bd_out(cc, g, p, hh):
    blk = cc.transpose(0, 2, 1)
    eye = jnp.eye(g, dtype=bool)[:, None, :, None]
    return jnp.where(eye, blk[:, :, None, :], 0.0).reshape(g * p, g * hh)


def _diag_in(dmat, g, p, hh):
    eye = jnp.eye(g, dtype=bool)[:, None, :, None]
    diag = jnp.sum(jnp.where(eye, dmat.reshape(g, hh, g, p), 0.0), axis=2)
    return diag.transpose(0, 2, 1).reshape(g * p, hh)


def _diag_out(dmat, g, p, hh):
    eye = jnp.eye(g, dtype=bool)[:, None, :, None]
    diag = jnp.sum(jnp.where(eye, dmat.reshape(g, p, g, hh), 0.0), axis=2)
    return diag.transpose(0, 2, 1)


def _softmax(qh, kh, scale):
    s = lax.dot_general(qh, kh, _DIMS["nt"], preferred_element_type=F32) * scale
    e = jnp.exp(s - jnp.max(s, axis=-1, keepdims=True))
    return e / jnp.sum(e, axis=-1, keepdims=True)


def _attn_fwd(q, kv):
    t, d = q.shape
    mm_ = kv.shape[0]
    hd = d // N_XHEADS
    scale = 1.0 / math.sqrt(hd)
    tm = _pick(t, 512, SUBLANES)

    def body(q_ref, kv_ref, o_ref):
        for h in range(N_XHEADS):
            sl = pl.ds(h * hd, hd)
            p = _softmax(q_ref[:, sl], kv_ref[:, sl], scale)
            o_ref[:, sl] = jnp.dot(p.astype(BF16), kv_ref[:, pl.ds(d + h * hd, hd)],
                                   preferred_element_type=F32).astype(BF16)

    return pl.pallas_call(
        body, name="attn_fwd", grid=(t // tm,),
        in_specs=[pl.BlockSpec((tm, d), lambda i: (i, 0)), pl.BlockSpec((mm_, 2 * d), lambda i: (0, 0))],
        out_specs=pl.BlockSpec((tm, d), lambda i: (i, 0)), out_shape=_out(t, d, BF16),
        compiler_params=_params(("parallel",)),
    )(q, kv)


def _attn_bwd(q, kv, do):
    t, d = q.shape
    mm_ = kv.shape[0]
    hd = d // N_XHEADS
    scale = 1.0 / math.sqrt(hd)
    tm = _pick(t, 512, SUBLANES)

    def body(q_ref, kv_ref, do_ref, dq_ref, dkv_ref):
        @pl.when(pl.program_id(0) == 0)
        def _():
            dkv_ref[...] = jnp.zeros_like(dkv_ref)

        for h in range(N_XHEADS):
            sl = pl.ds(h * hd, hd)
            vsl = pl.ds(d + h * hd, hd)
            qh, kh, doh = q_ref[:, sl], kv_ref[:, sl], do_ref[:, sl]
            p = _softmax(qh, kh, scale)
            dp = lax.dot_general(doh, kv_ref[:, vsl], _DIMS["nt"], preferred_element_type=F32)
            dkv_ref[:, vsl] += lax.dot_general(p.astype(BF16), doh, _DIMS["tn"], preferred_element_type=F32)
            ds = (p * (dp - jnp.sum(dp * p, axis=-1, keepdims=True)) * scale).astype(BF16)
            dq_ref[:, sl] = jnp.dot(ds, kh, preferred_element_type=F32).astype(BF16)
            dkv_ref[:, sl] += lax.dot_general(ds, qh, _DIMS["tn"], preferred_element_type=F32)

    row = pl.BlockSpec((tm, d), lambda i: (i, 0))
    full = pl.BlockSpec((mm_, 2 * d), lambda i: (0, 0))
    return pl.pallas_call(
        body, name="attn_bwd", grid=(t // tm,), in_specs=[row, full, row], out_specs=[row, full],
        out_shape=[_out(t, d, BF16), _out(mm_, 2 * d, F32)], compiler_params=_params(("arbitrary",)),
    )(q, kv, do)


def _ew(name, fn, ins, outs, rows_pref=256, rowvecs=()):
    r, c = ins[0].shape
    tr = _pick(r, rows_pref, SUBLANES)
    ni = len(ins) + len(rowvecs)

    def body(*refs):
        res = fn(*[x[...] for x in refs[:ni]])
        for o_ref, v in zip(refs[ni:], res):
            o_ref[...] = v.astype(o_ref.dtype)

    blk = pl.BlockSpec((tr, c), lambda i: (i, 0))
    vec = pl.BlockSpec((1, c), lambda i: (0, 0))
    return pl.pallas_call(
        body, name=name, grid=(r // tr,), in_specs=[blk] * len(ins) + [vec] * len(rowvecs), out_specs=[blk] * len(outs),
        out_shape=[_out(r, c, dt) for dt in outs], compiler_params=_params(("parallel",)),
    )(*ins, *rowvecs)


def _sum_slots(name, a, dtype):
    s, r, c = a.shape
    tr = _pick(r, 256, SUBLANES)

    def body(a_ref, o_ref):
        acc = a_ref[0].astype(F32)
        for k in range(1, s):
            acc = acc + a_ref[k].astype(F32)
        o_ref[...] = acc.astype(o_ref.dtype)

    return pl.pallas_call(
        body, name=name, grid=(r // tr,), in_specs=[pl.BlockSpec((s, tr, c), lambda i: (0, i, 0))],
        out_specs=pl.BlockSpec((tr, c), lambda i: (i, 0)), out_shape=_out(r, c, dtype),
        compiler_params=_params(("parallel",)),
    )(a)


def _adamw(name, w, g, m, v):
    bc1 = 1.0 - ADAM_B1 ** ADAM_STEP
    bc2 = 1.0 - ADAM_B2 ** ADAM_STEP

    def fn(wv, gv, mv, vv):
        m2 = ADAM_B1 * mv + (1.0 - ADAM_B1) * gv
        v2 = ADAM_B2 * vv + (1.0 - ADAM_B2) * (gv * gv)
        delta = -ADAM_LR * ((m2 / bc1) / (jnp.sqrt(v2 / bc2) + ADAM_EPS) + ADAM_WD * wv)
        return delta, m2, v2

    return _ew(name, fn, [w, g, m, v], [F32, F32, F32])


def _allgather(name, arrs):
    n = len(arrs)

    def body(*refs):
        ins, outs = refs[:n], refs[n:2 * n]
        send_sems, recv_sems, local_sems = refs[2 * n:]
        x, y, c = lax.axis_index("x"), lax.axis_index("y"), lax.axis_index("c")
        me, sibling = (x, y, c), (x, y, 1 - c)
        chips = [(1 - x, y), (x, 1 - y), (1 - x, 1 - y)]

        def rows(a, px, py, pc):
            r = ins[a].shape[0]
            return outs[a].at[pl.ds((4 * px + 2 * py + pc) * r, r), :]

        def copy(a, k, block, to, src=None):
            return pltpu.make_async_remote_copy(
                src_ref=rows(a, *block) if src is None else src, dst_ref=rows(a, *block),
                send_sem=send_sems.at[a, k], recv_sem=recv_sems.at[a, k], device_id=to, device_id_type=MESH)

        mine = [pltpu.make_async_copy(ins[a], rows(a, *me), local_sems.at[a]) for a in range(n)]
        for cp in mine:
            cp.start()
        first = []
        for a in range(n):
            first.append(copy(a, 0, me, sibling, src=ins[a]))
            first += [copy(a, 1 + j, me, (*chip, c), src=ins[a]) for j, chip in enumerate(chips)]
        for cp in first:
            cp.start()
        passed = []
        for j, chip in enumerate(chips):
            for a in range(n):
                copy(a, 1 + j, (*chip, c), me).wait_recv()
                cp = copy(a, 4 + j, (*chip, c), sibling)
                cp.start()
                passed.append(cp)
        for a in range(n):
            copy(a, 0, sibling, me).wait_recv()
            for j, chip in enumerate(chips):
                copy(a, 4 + j, (*chip, 1 - c), me).wait_recv()
        for cp in first + passed:
            cp.wait_send()
        for cp in mine:
            cp.wait()

    return pl.pallas_call(
        body, name=name, in_specs=[ANY] * n, out_specs=[ANY] * n,
        out_shape=[_out(N_DEV * a.shape[0], a.shape[1], a.dtype) for a in arrs],
        scratch_shapes=[pltpu.SemaphoreType.DMA((n, 7)), pltpu.SemaphoreType.DMA((n, 7)), pltpu.SemaphoreType.DMA((n,))],
    )(*arrs)


def _exchange_cores(name, blocks):
    n = len(blocks)
    c = blocks[0].shape[2]
    r = sum(b.shape[1] for b in blocks)

    def body(*refs):
        srcs, (recv_ref, send_sems, recv_sems) = refs[:n], refs[n:]
        x, y, cc = lax.axis_index("x"), lax.axis_index("y"), lax.axis_index("c")
        copies, off = [], 0
        for a, src in enumerate(srcs):
            rows = pl.ds(off, src.shape[1])
            off += src.shape[1]
            for q in range(4):
                copies.append(pltpu.make_async_remote_copy(
                    src_ref=src.at[2 * q + (1 - cc)], dst_ref=recv_ref.at[q, rows], send_sem=send_sems.at[a, q],
                    recv_sem=recv_sems.at[a, q], device_id=(x, y, 1 - cc), device_id_type=MESH))
        for cp in copies:
            cp.start()
        for cp in copies:
            cp.wait()

    return pl.pallas_call(
        body, name=name, in_specs=[ANY] * n, out_specs=ANY,
        out_shape=jax.ShapeDtypeStruct((4, r, c), blocks[0].dtype),
        scratch_shapes=[pltpu.SemaphoreType.DMA((n, 4))] * 2,
    )(*blocks)


def _peer(k, x, y, c):
    return (1 - x if k & 4 else x, 1 - y if k & 2 else y, 1 - c if k & 1 else c)


def _split_start(name, groups, after=None):
    pins = [] if after is None else [after]
    bufs, sem_shapes, spans = [], [], []
    for srcs, land_shapes, n_remote, n_local, _ in groups:
        sems = [pltpu.SemaphoreType.DMA((n_remote,)), pltpu.SemaphoreType.DMA((n_remote,))]
        sems += [pltpu.SemaphoreType.DMA((n_local,))] if n_local else []
        spans.append((len(bufs), len(srcs), len(land_shapes), len(sem_shapes), len(sems)))
        bufs += [pltpu.with_memory_space_constraint(a, pltpu.HBM) for a in srcs]
        bufs += [pltpu.with_memory_space_constraint(lax.empty(s.shape, s.dtype), pltpu.HBM) for s in land_shapes]
        sem_shapes += sems
    n_buf, n_sem = len(bufs), len(sem_shapes)

    def body(*refs):
        buf_refs, sem_refs, token = refs[:n_buf], refs[n_buf + len(pins):n_buf + len(pins) + n_sem], refs[-1]
        for (b0, ns, nl, s0, k), group in zip(spans, groups):
            remote, local = group[4](buf_refs[b0:b0 + ns], buf_refs[b0 + ns:b0 + ns + nl], *sem_refs[s0:s0 + k])
            for cp in local + remote:
                cp.start()
        token[...] = jnp.zeros_like(token)

    outs = pl.pallas_call(
        body, name=name,
        out_shape=sem_shapes + [pltpu.HBM(b.shape, b.dtype) for b in bufs] + [jax.ShapeDtypeStruct((SUBLANES, LANES), F32)],
        in_specs=[HBM] * n_buf + [ANY] * len(pins),
        out_specs=[SEM] * n_sem + [HBM] * n_buf + [pl.BlockSpec(memory_space=pltpu.VMEM)],
        input_output_aliases={i: n_sem + i for i in range(n_buf)},
        compiler_params=pltpu.CompilerParams(has_side_effects=SIDE_EFFECT),
    )(*bufs, *pins)
    return [dict(sems=list(outs[s0:s0 + k]), bufs=list(outs[n_sem + b0:n_sem + b0 + ns + nl]), token=outs[-1],
                 build=group[4], ns=ns) for (b0, ns, nl, s0, k), group in zip(spans, groups)]


def _split_wait(name, started, after):
    ns, n_buf, n_sem = started["ns"], len(started["bufs"]), len(started["sems"])

    def body(*refs):
        src_refs, land_refs = refs[:ns], refs[ns:n_buf]
        sems = refs[n_buf:n_buf + n_sem]
        remote, local = started["build"](src_refs, land_refs, *sems)
        for cp in local:
            cp.wait()
        for cp in remote:
            cp.wait_send()
            cp.wait_recv()

    outs = pl.pallas_call(
        body, name=name, out_shape=[pltpu.HBM(b.shape, b.dtype) for b in started["bufs"]],
        in_specs=[HBM] * n_buf + [SEM] * n_sem + [ANY], out_specs=[HBM] * n_buf,
        input_output_aliases={i: i for i in range(n_buf)},
        compiler_params=pltpu.CompilerParams(has_side_effects=SIDE_EFFECT),
    )(*started["bufs"], *started["sems"], after)
    return list(outs[:ns]), list(outs[ns:])


def _gather_group(shards):
    m = len(shards)

    def build(src_refs, land_refs, send_sems, recv_sems, local_sems):
        x, y, c = lax.axis_index("x"), lax.axis_index("y"), lax.axis_index("c")
        remote, local = [], []
        for j in range(m):
            r = src_refs[j].shape[0]
            dst = land_refs[j].at[pl.ds((4 * x + 2 * y + c) * r, r), :]
            local.append(pltpu.make_async_copy(src_refs[j], dst, local_sems.at[j]))
            for k in range(1, N_DEV):
                remote.append(pltpu.make_async_remote_copy(
                    src_ref=src_refs[j], dst_ref=dst, send_sem=send_sems.at[7 * j + k - 1],
                    recv_sem=recv_sems.at[7 * j + k - 1], device_id=_peer(k, x, y, c), device_id_type=MESH))
        return remote, local

    lands = [jax.ShapeDtypeStruct((N_DEV * a.shape[0], a.shape[1]), a.dtype) for a in shards]
    return shards, lands, 7 * m, m, build


def _slots_start(name, a):
    def build(src_refs, land_refs, send_sems, recv_sems, local_sems):
        x, y, c = lax.axis_index("x"), lax.axis_index("y"), lax.axis_index("c")
        dst = land_refs[0].at[4 * x + 2 * y + c]
        local = [pltpu.make_async_copy(src_refs[0], dst, local_sems.at[0])]
        remote = [pltpu.make_async_remote_copy(
            src_ref=src_refs[0], dst_ref=dst, send_sem=send_sems.at[k - 1], recv_sem=recv_sems.at[k - 1],
            device_id=_peer(k, x, y, c), device_id_type=MESH) for k in range(1, N_DEV)]
        return remote, local

    return _split_start(name, [([a], [jax.ShapeDtypeStruct((N_DEV,) + a.shape, a.dtype)], 7, 1, build)])[0]


def _chips_start(name, p):
    _, r, c = p.shape
    nck = r // GRAD_ROW_TILE

    def build(src_refs, land_refs, send_sems, recv_sems):
        x, y, cc = lax.axis_index("x"), lax.axis_index("y"), lax.axis_index("c")
        remote = []
        for k in range(1, 4):
            px = 1 - x if k >> 1 else x
            py = 1 - y if k & 1 else y
            for j in range(nck):
                rows = pl.ds(j * GRAD_ROW_TILE, GRAD_ROW_TILE)
                remote.append(pltpu.make_async_remote_copy(
                    src_ref=src_refs[0].at[2 * px + py, rows], dst_ref=land_refs[0].at[k - 1, rows],
                    send_sem=send_sems.at[(k - 1) * nck + j], recv_sem=recv_sems.at[(k - 1) * nck + j],
                    device_id=(px, py, cc), device_id_type=MESH))
        return remote, []

    return _split_start(name, [([p], [jax.ShapeDtypeStruct((3, r, c), p.dtype)], 3 * nck, 0, build)])[0]


def _chip_sum(name, p, recv, chip):
    _, r, c = p.shape
    tr = _pick(r, 5 * GRAD_ROW_TILE, GRAD_ROW_TILE)

    def body(chip_ref, p_ref, r_ref, o_ref):
        acc = p_ref[...].astype(F32)
        for k in range(3):
            acc = acc + r_ref[k].astype(F32)
        o_ref[...] = acc

    return pl.pallas_call(
        body, name=name,
        grid_spec=pltpu.PrefetchScalarGridSpec(
            num_scalar_prefetch=1, grid=(r // tr,),
            in_specs=[pl.BlockSpec((None, tr, c), lambda i, chip_ref: (chip_ref[0], i, 0)),
                      pl.BlockSpec((3, tr, c), lambda i, chip_ref: (0, i, 0))],
            out_specs=pl.BlockSpec((tr, c), lambda i, chip_ref: (i, 0))),
        out_shape=_out(r, c, F32), compiler_params=_params(("parallel",)),
    )(chip, p, recv)


def _local_step(x, mem, tgt, wt, sm, ev=None):
    t, d = x.shape
    n_mem = mem.shape[0]
    d_pool = sm["pool_scale"].shape[1]
    ng, pc = sm["pool_w"].shape[0], sm["pool_w"].shape[1]
    d_ssm = sm["ssm_d"].shape[1]
    _, sg, sp, sh = sm["ssm_b_re"].shape
    n_state = sg * sp
    gb, gs = {}, {}

    def emit(name, **kw):
        return ev(name, **kw) if ev is not None else None

    n1 = _rms_fwd("ffn1_norm", x, sm["ffn1_norm"])
    emit("ffn1_norm_done", marker=n1)
    def ffn1_down(hid):
        emit("ffn1_up_done", marker=hid)
        return wt["ffn1_w_down"]

    h1, ffn1_saved = _ffn_fwd("ffn1", x, n1, wt["ffn1_w_gate"], wt["ffn1_w_up"], ffn1_down)
    emit("ffn1_fwd_done", marker=h1)
    u = _rms_fwd("mix_norm", h1, sm["mix_norm"])
    d_in = wt["w_in"].shape[0]
    tm, tn = _pick(t, 1024), _pick(d_in, 1408)
    proj = _mm1("in_proj", "nt", u, wt["w_in"], t, d_in, tm, tn, F32)
    off_s = d_pool // d_ssm
    off_gp = (d_pool + d_ssm)
    off_gs = off_gp + d

    pool_w_bf = sm["pool_w"].astype(BF16)
    pooled, pm = _pool_fwd(proj, pool_w_bf, sm["pool_scale"])

    cols = [sm["ssm_a_re"].reshape(-1, 1), sm["ssm_a_im"].reshape(-1, 1),
            jnp.broadcast_to(sm["ssm_log_dt"][:, :, None], (2, sg, sp)).reshape(-1, 1),
            sm["ssm_b_re"].reshape(-1, sh), sm["ssm_b_im"].reshape(-1, sh)]
    abr, abi, bbr, bbi = _ssm_disc(cols)
    abr2, abi2 = abr.reshape(2, n_state), abi.reshape(2, n_state)
    bbr4, bbi4 = bbr.reshape(2, sg * sp, sh), bbi.reshape(2, sg * sp, sh)
    b_re = [_bd_in(bbr4[dr], sg, sp, sh).astype(BF16) for dr in range(2)]
    b_im = [_bd_in(bbi4[dr], sg, sp, sh).astype(BF16) for dr in range(2)]
    c_re = [_bd_out(sm["ssm_c_re"][dr], sg, sp, sh).astype(BF16) for dr in range(2)]
    c_im = [_bd_out(-sm["ssm_c_im"][dr], sg, sp, sh).astype(BF16) for dr in range(2)]
    sp32 = _to_segments(proj[:, d_pool:d_pool + d_ssm])
    xs, y_parts = [], []
    for dr in range(2):
        xr, xi, y_part = _ssm_fwd(f"ssm_fwd{dr}", sp32, b_re[dr], b_im[dr], c_re[dr], c_im[dr], abr2[dr:dr + 1],
                                  abi2[dr:dr + 1], reverse=(dr == 1))
        xs.append((xr, xi))
        y_parts.append(y_part)
    y = _from_segments(_ew("ssm_sum", lambda p0, p1, sv, dv: (p0 + p1 + sv * dv,), y_parts + [sp32], [F32],
                           rowvecs=[sm["ssm_d"]])[0])
    tmy = _pick(t, 256)
    ys = _ew("ssm_gelu", lambda v: (jax.nn.gelu(v),), [y], [BF16])[0]
    emit("mix_in_done", marker=ys)

    tmm, tnm, tnx = _pick(t, 1024), _pick(d, 256), _pick(d, 512)
    gp_spec = _tile(tmm, tnm, off_gp // tnm)
    gs_spec = _tile(tmm, tnm, off_gs // tnm)

    def merge_epi(accs, gpv, gsv):
        z_pool, val, gate = accs
        return (jax.nn.sigmoid(gpv) * z_pool + jax.nn.sigmoid(gsv) * (val * jax.nn.sigmoid(gate)),)

    merged = _mm("mix_merge", "nt", [pm, ys], [wt["w_pool_proj"], wt["w_glu_val"], wt["w_glu_gate"]],
                 [[(0, 0)], [(1, 1)], [(1, 2)]], t, d, tmm, tnm, [(proj, gp_spec), (proj, gs_spec)], merge_epi,
                 [(_out(t, d, BF16), None)])[0]
    res_epi = lambda accs, hin: (hin + accs[0],)
    h2 = _mm("mix_out", "nn", [merged], [wt["w_mix_out"]], [[(0, 0)]], t, d, tmm, tnx, [(h1, _tile(tmm, tnx))],
             res_epi, [(_out(t, d, F32), None)])[0]

    un = _rms_fwd("xattn_norm", h2, sm["xattn_norm"])
    mn = _rms_fwd("mem_norm", mem, sm["mem_norm"])
    emit("mix_done", marker=un)
    q = _mm1("xattn_q", "nn", un, wt["w_q"], t, d, tmm, tnx, BF16)
    kv = _mm1("xattn_kv", "nt", mn, wt["w_kv"], n_mem, 2 * d, n_mem, _pick(2 * d, 512), BF16)
    o = _attn_fwd(q, kv)
    h3 = _mm("xattn_out", "nn", [o], [wt["w_xo"]], [[(0, 0)]], t, d, tmm, tnx, [(h2, _tile(tmm, tnx))],
             res_epi, [(_out(t, d, F32), None)])[0]

    n2 = _rms_fwd("ffn2_norm", h3, sm["ffn2_norm"])
    emit("xattn_done", marker=n2)
    h4, ffn2_saved = _ffn_fwd("ffn2", h3, n2, wt["ffn2_w_gate"], wt["ffn2_w_up"], wt["ffn2_w_down"])

    dh4, dh4_bf, gs["final_norm"], loss = _loss_head(h4, sm["final_norm"], tgt)
    dh3, dh3_bf, gs["ffn2_norm"], gb["ffn2_w_gate"], gb["ffn2_w_up"], gb["ffn2_w_down"] = _ffn_bwd(
        "ffn2", h3, sm["ffn2_norm"], wt["ffn2_w_gate"], wt["ffn2_w_up"], wt["ffn2_w_down"], ffn2_saved, dh4, dh4_bf)

    tw = _pick(d, 1024)
    do = _mm1("xattn_do", "nt", dh3_bf, wt["w_xo"], t, d, tmm, tnx, BF16)
    gb["w_xo"] = _mm1("xattn_dwxo", "tn", o, dh3_bf, d, d, tw, tnx, BF16)
    dq, dkv = _attn_bwd(q, kv, do)
    gb["w_q"] = _mm1("xattn_dwq", "tn", un, dq, d, d, tw, tnx, BF16)
    dun = _mm1("xattn_dun", "nt", dq, wt["w_q"], t, d, tmm, tnx, F32)
    dh2, dh2_bf, gs["xattn_norm"] = _rms_bwd("xattn_norm_bwd", h2, sm["xattn_norm"], dun, dh3)
    gb["w_kv"] = _mm1("xattn_dwkv", "tn", dkv, mn, 2 * d, d, _pick(2 * d, 512), d, BF16)
    dmn = _mm1("xattn_dmn", "nn", dkv, wt["w_kv"], n_mem, d, n_mem, tnx, F32)
    gs["mem_norm"] = _rms_bwd("mem_norm_bwd", mem, sm["mem_norm"], dmn)

    gb["w_mix_out"] = _mm1("mix_dwout", "tn", merged, dh2_bf, d, d, tw, tnx, BF16)

    def merge_bwd_epi(accs, gpv, gsv):
        dmerged, z_pool, val, gate = accs
        sp_, ss_, sg_ = jax.nn.sigmoid(gpv), jax.nn.sigmoid(gsv), jax.nn.sigmoid(gate)
        glu = val * sg_
        dz_pool = dmerged * sp_
        dg_pool = dmerged * z_pool * (sp_ * (1.0 - sp_))
        dz_ssm = dmerged * ss_
        dg_ssm = dmerged * glu * (ss_ * (1.0 - ss_))
        dval = dz_ssm * sg_
        dgate = dz_ssm * glu * (1.0 - sg_)
        return dz_pool, dg_pool, dg_ssm, dval, dgate

    dz_pool, dg_pool, dg_ssm, dval, dgate = _mm(
        "mix_merge_bwd", "nt", [dh2_bf, pm, ys], [wt["w_mix_out"], wt["w_pool_proj"], wt["w_glu_val"], wt["w_glu_gate"]],
        [[(0, 0)], [(1, 1)], [(2, 2)], [(2, 3)]], t, d, tmm, tnm, [(proj, gp_spec), (proj, gs_spec)], merge_bwd_epi,
        [(_out(t, d, BF16), None)] * 5)
    gb["w_pool_proj"] = _mm1("pool_dwproj", "tn", dz_pool, pm, d, d_pool, tw, d_pool, BF16)
    gb["w_glu_val"] = _mm1("glu_dwval", "tn", dval, ys, d, d_ssm, tw, d_ssm, BF16)
    gb["w_glu_gate"] = _mm1("glu_dwgate", "tn", dgate, ys, d, d_ssm, tw, d_ssm, BF16)

    def gelu_bwd_epi(accs, yv):
        _, vjp = jax.vjp(jax.nn.gelu, yv)
        return (vjp(accs[0])[0],)

    dy = _mm("glu_dy", "nn", [dval, dgate], [wt["w_glu_val"], wt["w_glu_gate"]], [[(0, 0), (1, 1)]], t, d_ssm, tmy, d_ssm,
             [(y, _tile(tmy, d_ssm))], gelu_bwd_epi, [(_out(t, d_ssm, F32), None)])[0]
    gs["ssm_d"] = _colsum_prod("ssm_dd", dy, proj, b_coff=off_s)
    dyp = _to_segments(dy)
    d_abr, d_abi, d_bbr, d_bbi, d_cre, d_cim, lams = [], [], [], [], [], [], []
    ts = _pick(n_state, 512)
    tc_ = _pick(n_state, 256)
    both = lambda accs: tuple(accs)
    for dr in range(2):
        lr, li, dar, dai = _ssm_bwd(f"ssm_bwd{dr}", dyp, c_re[dr], c_im[dr], xs[dr][0], xs[dr][1], abr2[dr:dr + 1],
                                    abi2[dr:dr + 1], reverse=(dr == 1))
        d_abr.append(dar)
        d_abi.append(dai)
        lams += [lr, li]
        d_br, d_bi = _mm(f"ssm_db{dr}", "tn", [sp32], [lr, li], [[(0, 0)], [(0, 1)]], d_ssm, n_state, d_ssm, ts, [], both,
                         [(_out(d_ssm, n_state, F32), None)] * 2)
        d_bbr.append(_diag_in(d_br, sg, sp, sh))
        d_bbi.append(_diag_in(d_bi, sg, sp, sh))
        d_cr, d_ci = _mm(f"ssm_dc{dr}", "tn", [xs[dr][0], xs[dr][1]], [dyp], [[(0, 0)], [(1, 0)]], n_state, d_ssm, tc_,
                         d_ssm, [], both, [(_out(n_state, d_ssm, F32), None)] * 2)
        d_cre.append(_diag_out(d_cr, sg, sp, sh))
        d_cim.append(-_diag_out(d_ci, sg, sp, sh))
    ds = _from_segments(_mm(
        "ssm_ds", "nt", lams, [b_re[0], b_im[0], b_re[1], b_im[1]], [[(k, k) for k in range(4)]], t, d_ssm, tmy,
        d_ssm, [(dyp, _tile(tmy, d_ssm)), (sm["ssm_d"], _rowvec(d_ssm))],
        lambda accs, dyv, dv: (dyv * dv + accs[0],), [(_out(t, d_ssm, BF16), None)])[0])
    cots = [jnp.concatenate(d_abr, axis=0).reshape(-1, 1), jnp.concatenate(d_abi, axis=0).reshape(-1, 1),
            jnp.concatenate(d_bbr, axis=0), jnp.concatenate(d_bbi, axis=0)]
    d_are, d_aim, d_ldt, d_bre, d_bim = _ssm_disc_bwd(cols, cots)
    gs["ssm_a_re"] = d_are.reshape(2, sg, sp)
    gs["ssm_a_im"] = d_aim.reshape(2, sg, sp)
    gs["ssm_log_dt"] = _rowsum("ssm_dlogdt", d_ldt.reshape(2 * sg, sp)).reshape(2, sg)
    gs["ssm_b_re"] = d_bre.reshape(2, sg, sp, sh)
    gs["ssm_b_im"] = d_bim.reshape(2, sg, sp, sh)
    gs["ssm_c_re"] = jnp.stack(d_cre, axis=0)
    gs["ssm_c_im"] = jnp.stack(d_cim, axis=0)

    dpm = _mm1("pool_dpm", "nn", dz_pool, wt["w_pool_proj"], t, d_pool, tmm, _pick(d_pool, 256), F32)
    dp, gs["pool_w"], gs["pool_scale"] = _pool_bwd(pooled, dpm, pool_w_bf, sm["pool_scale"])

    w_in = wt["w_in"]
    parts = [(dp, 0, d_pool), (ds, d_pool, d_ssm), (dg_pool, off_gp, d), (dg_ssm, off_gs, d)]
    w_in_parts = [w_in[o0:o0 + width] for _, o0, width in parts]
    gb["w_in"] = jnp.concatenate(
        [_mm1(f"in_proj_dw{k}", "tn", p_[0], u, p_[2], d, _pick(p_[2], 1024), tnx, BF16) for k, p_ in enumerate(parts)], axis=0)
    pin = emit("grads_main", gb=gb)
    du = _mm("in_proj_du", "nn", [p_[0] for p_ in parts], w_in_parts, [[(k, k) for k in range(4)]], t, d, tmm, tnx, [],
             lambda accs: (accs[0],), [(_out(t, d, F32), None)], after=pin)[0]
    dh1, dh1_bf, gs["mix_norm"] = _rms_bwd("mix_norm_bwd", h1, sm["mix_norm"], du, dh2)
    pin = emit("small_early", gs=gs, loss=loss)

    def ffn1_weights_done(d_wg, d_wu, d_wd):
        gb["ffn1_w_gate"], gb["ffn1_w_up"], gb["ffn1_w_down"] = d_wg, d_wu, d_wd
        return emit("grads_ffn1", gb=gb)

    dx, _, gs["ffn1_norm"], _, _, _ = _ffn_bwd(
        "ffn1", x, sm["ffn1_norm"], wt["ffn1_w_gate"], wt["ffn1_w_up"], wt["ffn1_w_down"], ffn1_saved, dh1, dh1_bf,
        weights_done=ffn1_weights_done, after=pin)
    return loss, dx, gb, gs


WEIGHTS = ["ffn1_norm", "ffn1_w_gate", "ffn1_w_up", "ffn1_w_down", "mix_norm", "w_in", "pool_w", "pool_scale",
           "w_pool_proj", "ssm_a_re", "ssm_a_im", "ssm_log_dt", "ssm_b_re", "ssm_b_im", "ssm_c_re", "ssm_c_im", "ssm_d",
           "w_glu_val", "w_glu_gate", "w_mix_out", "xattn_norm", "mem_norm", "w_q", "w_kv", "w_xo", "ffn2_norm",
           "ffn2_w_gate", "ffn2_w_up", "ffn2_w_down", "final_norm"]
COL_SHARDED = ["ffn1_w_gate", "ffn1_w_up", "w_in", "w_pool_proj", "w_glu_val", "w_glu_gate", "w_kv", "ffn2_w_gate",
               "ffn2_w_up"]
ROW_SHARDED = ["ffn1_w_down", "w_mix_out", "w_q", "w_xo", "ffn2_w_down"]
BIG = [n for n in WEIGHTS if n in COL_SHARDED or n in ROW_SHARDED]
SMALL = [n for n in WEIGHTS if n not in BIG]
FFN1_BIG = ["ffn1_w_gate", "ffn1_w_up", "ffn1_w_down"]
MAIN_BIG = [n for n in BIG if n not in FFN1_BIG]
GATHER_PLAN = [("ffn1_up_done", ["ffn1_w_down"]), ("ffn1_fwd_done", ["w_in"]),
               ("mix_in_done", ["w_pool_proj", "w_glu_val", "w_glu_gate", "w_mix_out"]),
               ("mix_done", ["w_q", "w_kv", "w_xo"]), ("xattn_done", ["ffn2_w_gate", "ffn2_w_up", "ffn2_w_down"])]
LATE_SMALL = "ffn1_norm"
EARLY_SMALL = [n for n in SMALL if n != LATE_SMALL]
PACK_ROWS = SUBLANES * LANES
GRAD_ROW_TILE = 256


def _to_rows(name, w, width):
    if name in COL_SHARDED:
        w = w.T
    return w.reshape(-1, width)


def _from_rows(name, rows, shard_shape):
    if name in COL_SHARDED:
        return rows.reshape(shard_shape[1], shard_shape[0]).T
    return rows.reshape(shard_shape)


def _pack_small(vals):
    flat = []
    for v in vals:
        f = v.reshape(-1)
        flat.append(jnp.pad(f, (0, (-f.shape[0]) % PACK_ROWS)))
    total = sum(f.shape[0] for f in flat)
    flat.append(jnp.zeros(((-total) % (GRAD_ROW_TILE * LANES),), F32))
    return jnp.concatenate(flat).reshape(-1, LANES)


def _unpack_small(packed, shapes):
    out, row = [], 0
    for shp in shapes:
        size = math.prod(shp)
        rows = -(-size // PACK_ROWS) * SUBLANES
        out.append(packed[row:row + rows].reshape(-1)[:size].reshape(shp))
        row += rows
    return out


def kernel(x, mem, ffn1_norm, ffn1_w_gate, ffn1_w_up, ffn1_w_down, mix_norm, w_in, pool_w, pool_scale, w_pool_proj, ssm_a_re, ssm_a_im, ssm_log_dt, ssm_b_re, ssm_b_im, ssm_c_re, ssm_c_im, ssm_d, w_glu_val, w_glu_gate, w_mix_out, xattn_norm, mem_norm, w_q, w_kv, w_xo, ffn2_norm, ffn2_w_gate, ffn2_w_up, ffn2_w_down, final_norm, loss_target, m_ffn1_norm, m_ffn1_w_gate, m_ffn1_w_up, m_ffn1_w_down, m_mix_norm, m_w_in, m_pool_w, m_pool_scale, m_w_pool_proj, m_ssm_a_re, m_ssm_a_im, m_ssm_log_dt, m_ssm_b_re, m_ssm_b_im, m_ssm_c_re, m_ssm_c_im, m_ssm_d, m_w_glu_val, m_w_glu_gate, m_w_mix_out, m_xattn_norm, m_mem_norm, m_w_q, m_w_kv, m_w_xo, m_ffn2_norm, m_ffn2_w_gate, m_ffn2_w_up, m_ffn2_w_down, m_final_norm, v_ffn1_norm, v_ffn1_w_gate, v_ffn1_w_up, v_ffn1_w_down, v_mix_norm, v_w_in, v_pool_w, v_pool_scale, v_w_pool_proj, v_ssm_a_re, v_ssm_a_im, v_ssm_log_dt, v_ssm_b_re, v_ssm_b_im, v_ssm_c_re, v_ssm_c_im, v_ssm_d, v_w_glu_val, v_w_glu_gate, v_w_mix_out, v_xattn_norm, v_mem_norm, v_w_q, v_w_kv, v_w_xo, v_ffn2_norm, v_ffn2_w_gate, v_ffn2_w_up, v_ffn2_w_down, v_final_norm):
    given = dict(locals())
    wts = {n: given[n] for n in WEIGHTS}
    moms = {n: (given["m_" + n], given["v_" + n]) for n in WEIGHTS}
    x2, mem2, tgt2 = x[0], mem[0], loss_target[0]
    d = x2.shape[1]
    chip = (2 * lax.axis_index("x") + lax.axis_index("y")).astype(jnp.int32).reshape(1)

    def full_form(n, f):
        shard = wts[n][0].shape
        return f.reshape(N_DEV * shard[1], shard[0]) if n in COL_SHARDED else f.reshape(N_DEV * shard[0], shard[1])

    shards = {n: _to_rows(n, wts[n][0], d).astype(BF16) for n in BIG}
    first = FFN1_BIG[:2]
    wt = {n: full_form(n, f) for n, f in zip(first, _allgather("weight_allgather_first", [shards[n] for n in first]))}
    started = _split_start("weight_gather_start", [_gather_group([shards[n] for n in names]) for _, names in GATHER_PLAN],
                           after=wt[first[0]])
    gathers = {event: (names, st) for (event, names), st in zip(GATHER_PLAN, started)}
    sm = {n: (wts[n].reshape(1, -1) if wts[n].ndim <= 2 else wts[n][0]) for n in SMALL}
    sm["ffn1_norm"] = sm["ffn1_norm"] + started[0]["token"][0, 0]

    pending = {}

    def reduce_start(tag, names, gb):
        blocks = [gb[n].reshape(N_DEV, -1, d) for n in names]
        pad_rows = (-sum(b.shape[1] for b in blocks)) % GRAD_ROW_TILE
        pad = [jnp.zeros((N_DEV, pad_rows, d), BF16)] if pad_rows else []
        recv = _exchange_cores("grad_exchange_cores_" + tag, blocks + pad)
        own = jnp.concatenate([lax.dynamic_index_in_dim(b.reshape(4, 2, b.shape[1], d), lax.axis_index("c"), 1, False)
                               for b in blocks + pad], axis=1)
        rows_all = own.shape[1]
        pair = _ew("grad_pair_sum_" + tag, lambda a, b: (a.astype(F32) + b.astype(F32),),
                   [own.reshape(-1, d), recv.reshape(-1, d)], [BF16], rows_pref=5 * GRAD_ROW_TILE)[0]
        pair = pair.reshape(4, rows_all, d)
        pending[tag] = (pair, _chips_start("grad_exchange_chips_start_" + tag, pair), [b.shape[1] for b in blocks])
        return pending[tag][1]["token"]

    def reduce_finish(tag, after):
        _, started, rows = pending[tag]
        (pair,), (recv,) = _split_wait("grad_exchange_chips_wait_" + tag, started, after)
        return _chip_sum("grad_chip_sum_" + tag, pair, recv, chip), rows

    def ev(name, gb=None, gs=None, loss=None, marker=None):
        if name in gathers:
            names, started = gathers[name]
            for n, f in zip(names, _split_wait("weight_gather_wait_" + name, started, marker)[1]):
                wt[n] = full_form(n, f)
        elif name == "grads_main":
            return reduce_start("main", MAIN_BIG, gb)
        elif name == "small_early":
            pending["small"] = _slots_start("small_gather_start", _pack_small([gs[n] for n in EARLY_SMALL] + [loss[:, :1]]))
            return pending["small"]["token"]
        elif name == "grads_ffn1":
            return reduce_start("ffn1", FFN1_BIG, gb)
        return None

    _, dx, _, gs = _local_step(x2, mem2, tgt2, wt, sm, ev)

    out_g, out_d, out_m, out_v = {}, {}, {}, {}

    def update(n, g_full):
        shape = wts[n].shape
        two_d = (-1, shape[-1])
        dl, m2, v2 = _adamw("adamw_" + n, wts[n].reshape(two_d), g_full.reshape(two_d), moms[n][0].reshape(two_d),
                            moms[n][1].reshape(two_d))
        out_g[n], out_d[n], out_m[n], out_v[n] = g_full, dl.reshape(shape), m2.reshape(shape), v2.reshape(shape)
        return dl

    def update_big(names, g_rows, rows):
        off = 0
        for n, r in zip(names, rows):
            shard = wts[n].shape
            dl = update(n, _from_rows(n, g_rows[off:off + r], shard[1:]).reshape(shard))
            off += r
        return dl

    last = update_big(MAIN_BIG, *reduce_finish("main", dx))

    small_sum = _sum_slots("small_sum", _split_wait("small_gather_wait", pending["small"], dx)[1][0], F32)
    late = _allgather("small_allgather_late", [gs[LATE_SMALL].reshape(-1, LANES)])[0]
    late_sum = _sum_slots("small_sum_late", late.reshape(N_DEV, -1, LANES), F32)
    vals = _unpack_small(small_sum, [wts[n].shape for n in EARLY_SMALL] + [(1, 1)])
    total_loss = vals[-1].reshape(())
    for n, g_full in zip(EARLY_SMALL + [LATE_SMALL], vals[:-1] + [late_sum.reshape(wts[LATE_SMALL].shape)]):
        update(n, g_full)

    update_big(FFN1_BIG, *reduce_finish("ffn1", last))

    return (total_loss, dx[None], *[out_g[n] for n in WEIGHTS], *[out_d[n] for n in WEIGHTS],
            *[out_m[n] for n in WEIGHTS], *[out_v[n] for n in WEIGHTS])
```

```python
import functools
import math

import jax
import jax.numpy as jnp
from jax import lax
from jax.experimental import pallas as pl
from jax.experimental.pallas import tpu as pltpu

F32 = jnp.float32
BF16 = jnp.bfloat16
EPS = 1e-6
N_XHEADS = 4
POOL_WINDOWS = (2, 4, 8, 16)
ADAM_LR = 0.001
ADAM_B1 = 0.9
ADAM_B2 = 0.999
ADAM_EPS = 1e-08
ADAM_WD = 0.01
ADAM_STEP = 10
N_DEV = 8
VMEM_LIMIT_V7X = 48 * 1024 * 1024
LANES = 128
SUBLANES = 8
SUB_ROWS = 256
POOL_PAD = 16
MESH = pl.DeviceIdType.MESH
ANY = pl.BlockSpec(memory_space=pl.ANY)
HBM = pl.BlockSpec(memory_space=pltpu.HBM)
SEM = pl.BlockSpec(memory_space=pltpu.SEMAPHORE)
SIDE_EFFECT = pltpu.SideEffectType.DATAFLOW_SIDE_EFFECTING

_DIMS = {
    "nt": (((1,), (1,)), ((), ())),
    "nn": (((1,), (0,)), ((), ())),
    "tn": (((0,), (0,)), ((), ())),
}


def _pick(dim, pref, mult=LANES):
    if dim <= pref:
        return dim
    for t in range(pref - pref % mult, 0, -mult):
        if dim % t == 0:
            return t
    return dim


def _params(sem):
    return pltpu.CompilerParams(dimension_semantics=sem, vmem_limit_bytes=VMEM_LIMIT_V7X)


def _tile(tm, tn, coff=0):
    return pl.BlockSpec((tm, tn), lambda i, j: (i, j + coff))


def _rowvec(tn, coff=0):
    return pl.BlockSpec((1, tn), lambda i, j: (0, j + coff))


def _out(m, n, dtype):
    return jax.ShapeDtypeStruct((m, n), dtype)


def _mm(name, form, a_list, b_list, groups, m, n, tm, tn, extras, epilogue, outs, after=None, sub=SUB_ROWS):
    na, nb, ne = len(a_list), len(b_list), len(extras)
    pins = [] if after is None else [after]
    step = tm if (sub is None or form == "tn" or tm % sub) else sub

    def a_spec(a):
        if form == "tn":
            return pl.BlockSpec((a.shape[0], tm), lambda i, j: (0, i))
        return pl.BlockSpec((tm, a.shape[1]), lambda i, j: (i, 0))

    def b_spec(b):
        if form == "nt":
            return pl.BlockSpec((tn, b.shape[1]), lambda i, j: (j, 0))
        return pl.BlockSpec((b.shape[0], tn), lambda i, j: (0, j))

    def body(*refs):
        a_refs, b_refs = refs[:na], refs[na:na + nb]
        e_refs, o_refs = refs[na + nb:na + nb + ne], refs[na + nb + ne + len(pins):]
        b_vals = {}
        for s0 in range(0, tm, step):
            rows = slice(None) if step == tm else pl.ds(s0, step)
            a_vals, accs = {}, []
            for group in groups:
                acc = None
                for ai, bi in group:
                    if ai not in a_vals:
                        a_vals[ai] = (a_refs[ai][...] if form == "tn" else a_refs[ai][rows, :]).astype(BF16)
                    if bi not in b_vals:
                        b_vals[bi] = b_refs[bi][...].astype(BF16)
                    d = lax.dot_general(a_vals[ai], b_vals[bi], _DIMS[form], preferred_element_type=F32)
                    acc = d if acc is None else acc + d
                accs.append(acc)
            res = epilogue(accs, *[e[rows, :] if e.shape[0] == tm else e[...] for e in e_refs])
            for o_ref, r in zip(o_refs, res):
                o_ref[rows, :] = r.astype(o_ref.dtype)

    out_specs = [_tile(tm, tn) if s is None else s for _, s in outs]
    res = pl.pallas_call(
        body, name=name, grid=(m // tm, n // tn),
        in_specs=[a_spec(a) for a in a_list] + [b_spec(b) for b in b_list] + [s for _, s in extras] + [ANY] * len(pins),
        out_specs=out_specs, out_shape=[o for o, _ in outs],
        compiler_params=_params(("parallel", "parallel")),
    )(*a_list, *b_list, *[e for e, _ in extras], *pins)
    return res


def _mm1(name, form, a, b, m, n, tm, tn, dtype, scale=None):
    epi = (lambda accs: (accs[0],)) if scale is None else (lambda accs: (accs[0] * scale,))
    return _mm(name, form, [a], [b], [[(0, 0)]], m, n, tm, tn, [], epi, [(_out(m, n, dtype), None)])[0]


def _rms_fwd(name, h, g):
    t, d = h.shape
    tm = _pick(t, 512, SUBLANES)

    def body(h_ref, g_ref, n_ref):
        hv = h_ref[...]
        r = lax.rsqrt(jnp.mean(hv * hv, axis=-1, keepdims=True) + EPS)
        n_ref[...] = ((hv * r) * g_ref[...]).astype(BF16)

    return pl.pallas_call(
        body, name=name, grid=(t // tm,),
        in_specs=[pl.BlockSpec((tm, d), lambda i: (i, 0)), pl.BlockSpec((1, d), lambda i: (0, 0))],
        out_specs=pl.BlockSpec((tm, d), lambda i: (i, 0)), out_shape=_out(t, d, BF16),
        compiler_params=_params(("parallel",)),
    )(h, g)


def _rms_bwd(name, h, g, dn, dres=None):
    t, d = h.shape
    tm = _pick(t, 512, SUBLANES)
    need_dh = dres is not None

    def body(*refs):
        if need_dh:
            h_ref, g_ref, dn_ref, dres_ref, dh_ref, dhb_ref, dg_ref = refs
        else:
            h_ref, g_ref, dn_ref, dg_ref = refs
        hv = h_ref[...]
        r = lax.rsqrt(jnp.mean(hv * hv, axis=-1, keepdims=True) + EPS)
        nh = hv * r
        dnv = dn_ref[...].astype(F32)

        @pl.when(pl.program_id(0) == 0)
        def _():
            dg_ref[...] = jnp.zeros_like(dg_ref)

        dg_ref[...] += jnp.sum(dnv * nh, axis=0, keepdims=True)
        if need_dh:
            dng = dnv * g_ref[...]
            dh = dres_ref[...] + r * (dng - nh * jnp.mean(dng * nh, axis=-1, keepdims=True))
            dh_ref[...] = dh
            dhb_ref[...] = dh.astype(BF16)

    row = pl.BlockSpec((tm, d), lambda i: (i, 0))
    vec = pl.BlockSpec((1, d), lambda i: (0, 0))
    if need_dh:
        return pl.pallas_call(
            body, name=name, grid=(t // tm,), in_specs=[row, vec, row, row], out_specs=[row, row, vec],
            out_shape=[_out(t, d, F32), _out(t, d, BF16), _out(1, d, F32)], compiler_params=_params(("arbitrary",)),
        )(h, g, dn, dres)
    return pl.pallas_call(
        body, name=name, grid=(t // tm,), in_specs=[row, vec, row], out_specs=vec,
        out_shape=_out(1, d, F32), compiler_params=_params(("arbitrary",)),
    )(h, g, dn)


def _loss_head(h, g, tgt):
    t, d = h.shape
    tm = _pick(t, 512, SUBLANES)

    def body(h_ref, g_ref, t_ref, dh_ref, dhb_ref, dg_ref, loss_ref):
        hv = h_ref[...]
        r = lax.rsqrt(jnp.mean(hv * hv, axis=-1, keepdims=True) + EPS)
        nh = hv * r
        err = nh * g_ref[...] - t_ref[...]

        @pl.when(pl.program_id(0) == 0)
        def _():
            dg_ref[...] = jnp.zeros_like(dg_ref)
            loss_ref[...] = jnp.zeros_like(loss_ref)

        per_row = jnp.mean(err * err, axis=-1, keepdims=True)
        loss_ref[...] += 0.5 * jnp.sum(per_row, axis=0, keepdims=True)
        dy = err * (1.0 / d)
        dg_ref[...] += jnp.sum(dy * nh, axis=0, keepdims=True)
        dng = dy * g_ref[...]
        dh = r * (dng - nh * jnp.mean(dng * nh, axis=-1, keepdims=True))
        dh_ref[...] = dh
        dhb_ref[...] = dh.astype(BF16)

    row = pl.BlockSpec((tm, d), lambda i: (i, 0))
    vec = pl.BlockSpec((1, d), lambda i: (0, 0))
    return pl.pallas_call(
        body, name="loss_head", grid=(t // tm,), in_specs=[row, vec, row],
        out_specs=[row, row, vec, pl.BlockSpec((1, LANES), lambda i: (0, 0))],
        out_shape=[_out(t, d, F32), _out(t, d, BF16), _out(1, d, F32), _out(1, LANES, F32)],
        compiler_params=_params(("arbitrary",)),
    )(h, g, tgt)


def _ffn_fwd(tag, h, n, wg_t, wu_t, wd):
    t, d = h.shape
    f = wg_t.shape[0]
    tm, tn = _pick(t, 1024), _pick(f, 1408)

    def up_epi(accs):
        a, b = accs
        return a, b, (a * jax.nn.sigmoid(a)) * b

    a, b, hid = _mm(tag + "_up", "nt", [n], [wg_t, wu_t], [[(0, 0)], [(0, 1)]], t, f, tm, tn, [], up_epi,
                    [(_out(t, f, BF16), None)] * 3)
    if callable(wd):
        wd = wd(hid)
    tm2, tn2 = _pick(t, 1024), _pick(d, 512)
    h_out = _mm(tag + "_down", "nn", [hid], [wd], [[(0, 0)]], t, d, tm2, tn2, [(h, _tile(tm2, tn2))],
                lambda accs, hin: (hin + 0.5 * accs[0],), [(_out(t, d, F32), None)])[0]
    return h_out, (n, a, b, hid)


def _ffn_bwd(tag, h, g, wg_t, wu_t, wd, saved, dh, dh_bf, weights_done=None, after=None):
    n, a, b, hid = saved
    t, d = h.shape
    f = wd.shape[0]
    tm, tn = _pick(t, 1024), _pick(f, 1408)

    def hid_epi(accs, av, bv):
        dhid = 0.5 * accs[0]
        av, bv = av.astype(F32), bv.astype(F32)
        sig = jax.nn.sigmoid(av)
        da = dhid * bv * (sig * (1.0 + av * (1.0 - sig)))
        db = dhid * (av * sig)
        return da, db

    da, db = _mm(tag + "_bwd_hid", "nt", [dh_bf], [wd], [[(0, 0)]], t, f, tm, tn,
                 [(a, _tile(tm, tn)), (b, _tile(tm, tn))], hid_epi, [(_out(t, f, BF16), None)] * 2, after=after)
    tw, tnw = _pick(f, 1408), _pick(d, 512)
    d_wd = _mm1(tag + "_dwd", "tn", hid, dh_bf, f, d, tw, tnw, BF16, scale=0.5)
    d_wg = _mm1(tag + "_dwg", "tn", da, n, f, d, tw, tnw, BF16)
    d_wu = _mm1(tag + "_dwu", "tn", db, n, f, d, tw, tnw, BF16)
    pin = weights_done(d_wg, d_wu, d_wd) if weights_done is not None else None
    tm2, tn2 = _pick(t, 1024), _pick(d, 512)
    dn = _mm(tag + "_dn", "nn", [da, db], [wg_t, wu_t], [[(0, 0), (1, 1)]], t, d, tm2, tn2, [],
             lambda accs: (accs[0],), [(_out(t, d, F32), None)], after=pin)[0]
    dh_in, dh_in_bf, dg = _rms_bwd(tag + "_norm_bwd", h, g, dn, dh)
    return dh_in, dh_in_bf, dg, d_wg, d_wu, d_wd


def _window_sum(win, offsets):
    n = win.shape[0]
    acc = None
    for j in offsets:
        term = win if j == 0 else pltpu.roll(win, (-j) % n, 0)
        acc = term if acc is None else acc + term
    return acc


def _pool_counts(r0, ch, c, left, right, t):
    pos = r0 + lax.broadcasted_iota(jnp.int32, (ch, c), 0)
    return (jnp.minimum(pos + right + 1, t) - jnp.maximum(pos - left, 0)).astype(F32)


def _pool_fwd(proj, pool_w_bf, pool_scale):
    t = proj.shape[0]
    ng, c, _ = pool_w_bf.shape
    ch = _pick(t, 256, SUBLANES)
    pad = POOL_PAD

    def body(p_ref, w_ref, s_ref, pooled_ref, pm_ref, buf):
        grp = pl.program_id(0)
        buf[pl.ds(0, pad), :] = jnp.zeros((pad, c), F32)
        buf[pl.ds(pad + t, pad), :] = jnp.zeros((pad, c), F32)

        def fill(ci, carry):
            r0 = pl.multiple_of(ci * ch, SUBLANES)
            buf[pl.ds(pl.multiple_of(r0 + pad, SUBLANES), ch), :] = p_ref[pl.ds(r0, ch), :]
            return carry

        lax.fori_loop(0, t // ch, fill, 0)
        for gi, w in enumerate(POOL_WINDOWS):
            left = w // 2
            right = w - 1 - left

            @pl.when(grp == gi)
            def _(left=left, right=right):
                def chunk(ci, carry):
                    r0 = pl.multiple_of(ci * ch, SUBLANES)
                    win = buf[pl.ds(r0, ch + 2 * pad), :]
                    s = _window_sum(win, range(-left, right + 1))[pad:pad + ch]
                    pooled = s / _pool_counts(r0, ch, c, left, right, t) - win[pad:pad + ch]
                    pooled_bf = pooled.astype(BF16)
                    mixed = jnp.dot(pooled_bf, w_ref[0], preferred_element_type=F32)
                    pooled_ref[pl.ds(r0, ch), :] = pooled_bf
                    pm_ref[pl.ds(r0, ch), :] = (mixed * s_ref[...]).astype(BF16)
                    return carry

                lax.fori_loop(0, t // ch, chunk, 0)

    col = pl.BlockSpec((t, c), lambda g: (0, g))
    return pl.pallas_call(
        body, name="pool_fwd", grid=(ng,),
        in_specs=[col, pl.BlockSpec((1, c, c), lambda g: (g, 0, 0)), pl.BlockSpec((1, c), lambda g: (0, g))],
        out_specs=[col, col], out_shape=[_out(t, ng * c, BF16), _out(t, ng * c, BF16)],
        scratch_shapes=[pltpu.VMEM((t + 2 * pad, c), F32)],
        compiler_params=_params(("parallel",)),
    )(proj, pool_w_bf, pool_scale)


def _pool_bwd(pooled, dpm, pool_w_bf, pool_scale):
    t = pooled.shape[0]
    ng, c, _ = pool_w_bf.shape
    ch = _pick(t, 256, SUBLANES)
    pad = POOL_PAD

    def body(pooled_ref, dpm_ref, w_ref, s_ref, dp_ref, dw_ref, ds_ref, buf, raw):
        grp = pl.program_id(0)
        buf[pl.ds(0, pad), :] = jnp.zeros((pad, c), F32)
        buf[pl.ds(pad + t, pad), :] = jnp.zeros((pad, c), F32)
        dw_ref[...] = jnp.zeros_like(dw_ref)
        ds_ref[...] = jnp.zeros_like(ds_ref)
        for gi, w in enumerate(POOL_WINDOWS):
            left = w // 2
            right = w - 1 - left

            @pl.when(grp == gi)
            def _(left=left, right=right):
                def first(ci, carry):
                    r0 = pl.multiple_of(ci * ch, SUBLANES)
                    pv = pooled_ref[pl.ds(r0, ch), :]
                    dpm_v = dpm_ref[pl.ds(r0, ch), :]
                    mixed = jnp.dot(pv, w_ref[0], preferred_element_type=F32)
                    ds_ref[...] += jnp.sum(dpm_v * mixed, axis=0, keepdims=True)
                    dmixed = (dpm_v * s_ref[...]).astype(BF16)
                    dw_ref[0] += lax.dot_general(pv, dmixed, _DIMS["tn"], preferred_element_type=F32)
                    dpooled = lax.dot_general(dmixed, w_ref[0], _DIMS["nt"], preferred_element_type=F32)
                    raw[pl.ds(r0, ch), :] = dpooled
                    buf[pl.ds(pl.multiple_of(r0 + pad, SUBLANES), ch), :] = (
                        dpooled / _pool_counts(r0, ch, c, left, right, t))
                    return carry

                lax.fori_loop(0, t // ch, first, 0)

                def second(ci, carry):
                    r0 = pl.multiple_of(ci * ch, SUBLANES)
                    win = buf[pl.ds(r0, ch + 2 * pad), :]
                    s = _window_sum(win, range(-right, left + 1))[pad:pad + ch]
                    dp_ref[pl.ds(r0, ch), :] = (s - raw[pl.ds(r0, ch), :]).astype(BF16)
                    return carry

                lax.fori_loop(0, t // ch, second, 0)

    col = pl.BlockSpec((t, c), lambda g: (0, g))
    return pl.pallas_call(
        body, name="pool_bwd", grid=(ng,),
        in_specs=[col, col, pl.BlockSpec((1, c, c), lambda g: (g, 0, 0)), pl.BlockSpec((1, c), lambda g: (0, g))],
        out_specs=[col, pl.BlockSpec((1, c, c), lambda g: (g, 0, 0)), pl.BlockSpec((1, c), lambda g: (0, g))],
        out_shape=[_out(t, ng * c, BF16), jax.ShapeDtypeStruct((ng, c, c), F32), _out(1, ng * c, F32)],
        scratch_shapes=[pltpu.VMEM((t + 2 * pad, c), F32), pltpu.VMEM((t, c), F32)],
        compiler_params=_params(("parallel",)),
    )(pooled, dpm, pool_w_bf, pool_scale)


def _discretise(a_re, a_im, log_dt, b_re, b_im):
    dt = jnp.exp(log_dt)
    mag = jnp.exp(dt * a_re)
    ang = dt * a_im
    abr = mag * jnp.cos(ang)
    abi = mag * jnp.sin(ang)
    den = a_re * a_re + a_im * a_im
    nr = abr - 1.0
    qr = (nr * a_re + abi * a_im) / den
    qi = (abi * a_re - nr * a_im) / den
    return abr, abi, qr * b_re - qi * b_im, qr * b_im + qi * b_re


def _ssm_disc(cols):
    n, hh = cols[3].shape

    def body(ar, ai, ld, br, bi, o1, o2, o3, o4):
        res = _discretise(ar[...], ai[...], ld[...], br[...], bi[...])
        for o, r in zip((o1, o2, o3, o4), res):
            o[...] = r

    return pl.pallas_call(
        body, name="ssm_disc",
        out_shape=[_out(n, 1, F32), _out(n, 1, F32), _out(n, hh, F32), _out(n, hh, F32)],
    )(*cols)


def _ssm_disc_bwd(cols, cots):
    n, hh = cols[3].shape

    def body(ar, ai, ld, br, bi, c1, c2, c3, c4, o1, o2, o3, o4, o5):
        _, vjp = jax.vjp(_discretise, ar[...], ai[...], ld[...], br[...], bi[...])
        res = vjp((c1[...], c2[...], c3[...], c4[...]))
        for o, r in zip((o1, o2, o3, o4, o5), res):
            o[...] = r

    return pl.pallas_call(
        body, name="ssm_disc_bwd",
        out_shape=[_out(n, 1, F32)] * 3 + [_out(n, hh, F32)] * 2,
    )(*cols, *cots)


def _rowsum(name, a):
    r, _ = a.shape

    def body(a_ref, o_ref):
        o_ref[...] = jnp.sum(a_ref[...], axis=-1, keepdims=True)

    return pl.pallas_call(body, name=name, out_shape=_out(r, 1, F32))(a)


def _cmul(pr, pi, qr, qi):
    return pr * qr - pi * qi, pr * qi + pi * qr


def _cpow(pr, pi, n):
    rr, ri = None, None
    while n:
        if n & 1:
            rr, ri = (pr, pi) if rr is None else _cmul(rr, ri, pr, pi)
        n >>= 1
        if n:
            pr, pi = _cmul(pr, pi, pr, pi)
    return rr, ri


def _segment_carry(er, ei, pr, pi, reverse):
    row = lax.broadcasted_iota(jnp.int32, er.shape, 0)
    cr, ci = jnp.zeros_like(er), jnp.zeros_like(ei)
    for _ in range(SUBLANES - 1):
        tr = er + pr * cr - pi * ci
        ti = ei + pr * ci + pi * cr
        if reverse:
            keep, shift = row < SUBLANES - 1, SUBLANES - 1
        else:
            keep, shift = row >= 1, 1
        cr = jnp.where(keep, pltpu.roll(tr, shift, 0), 0.0)
        ci = jnp.where(keep, pltpu.roll(ti, shift, 0), 0.0)
    return cr, ci


def _ssm_fwd(name, sp, b_re, b_im, c_re, c_im, ar, ai, reverse):
    t, c = sp.shape
    s = ar.shape[1]
    w = _pick(s, 512)
    ch = _pick(t, 512, SUBLANES)
    n_ch, gpc, steps = t // ch, ch // SUBLANES, t // SUBLANES

    def body(sp_ref, bre_ref, bim_ref, cre_ref, cim_ref, ar_ref, ai_ref, xr_ref, xi_ref, y_ref, ur, ui, xbr, xbi):
        a_r = jnp.broadcast_to(ar_ref[...], (SUBLANES, w))
        a_i = jnp.broadcast_to(ai_ref[...], (SUBLANES, w))

        @pl.when(pl.program_id(0) == 0)
        def _():
            y_ref[...] = jnp.zeros_like(y_ref)

        def sweep(h0, store):
            def chunk(k, h):
                ci = n_ch - 1 - k if reverse else k
                rows = pl.ds(pl.multiple_of(ci * ch, ch), ch)
                spv = sp_ref[rows, :].astype(BF16)
                ur[...] = jnp.dot(spv, bre_ref[...], preferred_element_type=F32)
                ui[...] = jnp.dot(spv, bim_ref[...], preferred_element_type=F32)

                def group(g, hh):
                    gi = gpc - 1 - g if reverse else g
                    r0 = pl.multiple_of(gi * SUBLANES, SUBLANES)
                    hr, hi = hh
                    nr = a_r * hr - a_i * hi + ur[pl.ds(r0, SUBLANES), :]
                    ni = a_r * hi + a_i * hr + ui[pl.ds(r0, SUBLANES), :]
                    if store:
                        xbr[pl.ds(r0, SUBLANES), :] = nr
                        xbi[pl.ds(r0, SUBLANES), :] = ni
                    return nr, ni

                h = lax.fori_loop(0, gpc, group, h)
                if store:
                    xr16, xi16 = xbr[...].astype(BF16), xbi[...].astype(BF16)
                    xr_ref[rows, :] = xr16
                    xi_ref[rows, :] = xi16
                    y_ref[rows, :] += (jnp.dot(xr16, cre_ref[...], preferred_element_type=F32)
                                       + jnp.dot(xi16, cim_ref[...], preferred_element_type=F32))
                return h

            return lax.fori_loop(0, n_ch, chunk, h0)

        zero = jnp.zeros((SUBLANES, w), F32)
        er, ei = sweep((zero, zero), False)
        pr, pi = _cpow(ar_ref[...], ai_ref[...], steps)
        sweep(_segment_carry(er, ei, pr, pi, reverse), True)

    col = lambda i: (0, i)
    return pl.pallas_call(
        body, name=name, grid=(s // w,),
        in_specs=[pl.BlockSpec((t, c), lambda i: (0, 0)), pl.BlockSpec((c, w), col), pl.BlockSpec((c, w), col),
                  pl.BlockSpec((w, c), lambda i: (i, 0)), pl.BlockSpec((w, c), lambda i: (i, 0)),
                  pl.BlockSpec((1, w), col), pl.BlockSpec((1, w), col)],
        out_specs=[pl.BlockSpec((t, w), col), pl.BlockSpec((t, w), col), pl.BlockSpec((t, c), lambda i: (0, 0))],
        out_shape=[_out(t, s, BF16), _out(t, s, BF16), _out(t, c, F32)],
        scratch_shapes=[pltpu.VMEM((ch, w), F32)] * 4,
        compiler_params=_params(("arbitrary",)),
    )(sp, b_re, b_im, c_re, c_im, ar, ai)


def _ssm_bwd(name, dyp, c_re, c_im, xr, xi, ar, ai, reverse):
    t, c = dyp.shape
    s = ar.shape[1]
    w = _pick(s, 512)
    ch = _pick(t, 512, SUBLANES)
    n_ch, gpc, steps = t // ch, ch // SUBLANES, t // SUBLANES
    back = not reverse
    edge = 2 * SUBLANES

    def body(dy_ref, cre_ref, cim_ref, xr_ref, xi_ref, ar_ref, ai_ref, lr_ref, li_ref, dar_ref, dai_ref,
             gr, gi_, lbr, lbi, xbr, xbi):
        a_r = jnp.broadcast_to(ar_ref[...], (SUBLANES, w))
        a_i = -jnp.broadcast_to(ai_ref[...], (SUBLANES, w))
        row = lax.broadcasted_iota(jnp.int32, (SUBLANES, w), 0)

        def neighbours(ci, x_ref, buf):
            rows = pl.ds(pl.multiple_of(ci * ch, ch), ch)
            if reverse:
                buf[pl.ds(0, ch), :] = x_ref[rows, :].astype(F32)
                nxt = x_ref[pl.ds(pl.multiple_of(jnp.minimum(ci + 1, n_ch - 1) * ch, ch), edge), :].astype(F32)[:SUBLANES]
                first = x_ref[pl.ds(0, edge), :].astype(F32)[:SUBLANES]
                wrap = jnp.where(row < SUBLANES - 1, pltpu.roll(first, SUBLANES - 1, 0), 0.0)
                buf[pl.ds(ch, SUBLANES), :] = jnp.where(ci == n_ch - 1, wrap, nxt)
            else:
                buf[pl.ds(SUBLANES, ch), :] = x_ref[rows, :].astype(F32)
                prv = x_ref[pl.ds(pl.multiple_of(jnp.maximum(ci * ch - edge, 0), edge), edge), :].astype(F32)[SUBLANES:]
                last = x_ref[pl.ds(t - edge, edge), :].astype(F32)[SUBLANES:]
                wrap = jnp.where(row >= 1, pltpu.roll(last, 1, 0), 0.0)
                buf[pl.ds(0, SUBLANES), :] = jnp.where(ci == 0, wrap, prv)

        def sweep(h0, store):
            def chunk(k, carry):
                ci = n_ch - 1 - k if back else k
                rows = pl.ds(pl.multiple_of(ci * ch, ch), ch)
                dyv = dy_ref[rows, :].astype(BF16)
                gr[...] = lax.dot_general(dyv, cre_ref[...], _DIMS["nt"], preferred_element_type=F32)
                gi_[...] = lax.dot_general(dyv, cim_ref[...], _DIMS["nt"], preferred_element_type=F32)
                if store:
                    neighbours(ci, xr_ref, xbr)
                    neighbours(ci, xi_ref, xbi)

                def group(g, cc):
                    gidx = gpc - 1 - g if back else g
                    r0 = pl.multiple_of(gidx * SUBLANES, SUBLANES)
                    hr, hi = cc[0], cc[1]
                    nr = a_r * hr - a_i * hi + gr[pl.ds(r0, SUBLANES), :]
                    ni = a_r * hi + a_i * hr + gi_[pl.ds(r0, SUBLANES), :]
                    if not store:
                        return nr, ni
                    lbr[pl.ds(r0, SUBLANES), :] = nr
                    lbi[pl.ds(r0, SUBLANES), :] = ni
                    x0 = pl.multiple_of(r0 + SUBLANES, SUBLANES) if reverse else r0
                    xpr, xpi = xbr[pl.ds(x0, SUBLANES), :], xbi[pl.ds(x0, SUBLANES), :]
                    return nr, ni, cc[2] + nr * xpr + ni * xpi, cc[3] + ni * xpr - nr * xpi

                carry = lax.fori_loop(0, gpc, group, carry)
                if store:
                    lr_ref[rows, :] = lbr[...].astype(BF16)
                    li_ref[rows, :] = lbi[...].astype(BF16)
                return carry

            return lax.fori_loop(0, n_ch, chunk, h0)

        zero = jnp.zeros((SUBLANES, w), F32)
        er, ei = sweep((zero, zero), False)
        pr, pi = _cpow(ar_ref[...], -ai_ref[...], steps)
        cr, ci0 = _segment_carry(er, ei, pr, pi, back)
        _, _, dar, dai = sweep((cr, ci0, zero, zero), True)
        dar_ref[...] = jnp.sum(dar, axis=0, keepdims=True)
        dai_ref[...] = jnp.sum(dai, axis=0, keepdims=True)

    col = lambda i: (0, i)
    return pl.pallas_call(
        body, name=name, grid=(s // w,),
        in_specs=[pl.BlockSpec((t, c), lambda i: (0, 0)), pl.BlockSpec((w, c), lambda i: (i, 0)),
                  pl.BlockSpec((w, c), lambda i: (i, 0)), pl.BlockSpec((t, w), col), pl.BlockSpec((t, w), col),
                  pl.BlockSpec((1, w), col), pl.BlockSpec((1, w), col)],
        out_specs=[pl.BlockSpec((t, w), col), pl.BlockSpec((t, w), col), pl.BlockSpec((1, w), col), pl.BlockSpec((1, w), col)],
        out_shape=[_out(t, s, BF16), _out(t, s, BF16), _out(1, s, F32), _out(1, s, F32)],
        scratch_shapes=[pltpu.VMEM((ch, w), F32)] * 4 + [pltpu.VMEM((ch + SUBLANES, w), F32)] * 2,
        compiler_params=_params(("parallel",)),
    )(dyp, c_re, c_im, xr, xi, ar, ai)


def _to_segments(a):
    t, c = a.shape
    return a.reshape(SUBLANES, t // SUBLANES, c).transpose(1, 0, 2).reshape(t, c)


def _from_segments(a):
    t, c = a.shape
    return a.reshape(t // SUBLANES, SUBLANES, c).transpose(1, 0, 2).reshape(t, c)


def _colsum_prod(name, a, b, b_coff=0):
    t, n = a.shape
    tm = _pick(t, 512, SUBLANES)

    def body(a_ref, b_ref, o_ref):
        @pl.when(pl.program_id(0) == 0)
        def _():
            o_ref[...] = jnp.zeros_like(o_ref)

        o_ref[...] += jnp.sum(a_ref[...].astype(F32) * b_ref[...].astype(F32), axis=0, keepdims=True)

    return pl.pallas_call(
        body, name=name, grid=(t // tm,),
        in_specs=[pl.BlockSpec((tm, n), lambda i: (i, 0)), pl.BlockSpec((tm, n), lambda i: (i, b_coff))],
        out_specs=pl.BlockSpec((1, n), lambda i: (0, 0)), out_shape=_out(1, n, F32),
        compiler_params=_params(("arbitrary",)),
    )(a, b)


def _bd_in(bb, g, p, hh):
    blk = bb.reshape(g, p, hh).transpose(0, 2, 1)
    eye = jnp.eye(g, dtype=bool)[:, None, :, None]
    return jnp.where(eye, blk[:, :, None, :], 0.0).reshape(g * hh, g * p)


def _bd_out(cc, g, p, hh):
    blk = cc.transpose(0, 2, 1)
    eye = jnp.eye(g, dtype=bool)[:, None, :, None]
    return jnp.where(eye, blk[:, :, None, :], 0.0).reshape(g * p, g * hh)


def _diag_in(dmat, g, p, hh):
    eye = jnp.eye(g, dtype=bool)[:, None, :, None]
    diag = jnp.sum(jnp.where(eye, dmat.reshape(g, hh, g, p), 0.0), axis=2)
    return diag.transpose(0, 2, 1).reshape(g * p, hh)


def _diag_out(dmat, g, p, hh):
    eye = jnp.eye(g, dtype=bool)[:, None, :, None]
    diag = jnp.sum(jnp.where(eye, dmat.reshape(g, p, g, hh), 0.0), axis=2)
    return diag.transpose(0, 2, 1)


def _softmax(qh, kh, scale):
    s = lax.dot_general(qh, kh, _DIMS["nt"], preferred_element_type=F32) * scale
    e = jnp.exp(s - jnp.max(s, axis=-1, keepdims=True))
    return e / jnp.sum(e, axis=-1, keepdims=True)


def _attn_fwd(q, kv):
    t, d = q.shape
    mm_ = kv.shape[0]
    hd = d // N_XHEADS
    scale = 1.0 / math.sqrt(hd)
    tm = _pick(t, 512, SUBLANES)

    def body(q_ref, kv_ref, o_ref):
        for h in range(N_XHEADS):
            sl = pl.ds(h * hd, hd)
            p = _softmax(q_ref[:, sl], kv_ref[:, sl], scale)
            o_ref[:, sl] = jnp.dot(p.astype(BF16), kv_ref[:, pl.ds(d + h * hd, hd)],
                                   preferred_element_type=F32).astype(BF16)

    return pl.pallas_call(
        body, name="attn_fwd", grid=(t // tm,),
        in_specs=[pl.BlockSpec((tm, d), lambda i: (i, 0)), pl.BlockSpec((mm_, 2 * d), lambda i: (0, 0))],
        out_specs=pl.BlockSpec((tm, d), lambda i: (i, 0)), out_shape=_out(t, d, BF16),
        compiler_params=_params(("parallel",)),
    )(q, kv)


def _attn_bwd(q, kv, do):
    t, d = q.shape
    mm_ = kv.shape[0]
    hd = d // N_XHEADS
    scale = 1.0 / math.sqrt(hd)
    tm = _pick(t, 512, SUBLANES)

    def body(q_ref, kv_ref, do_ref, dq_ref, dkv_ref):
        @pl.when(pl.program_id(0) == 0)
        def _():
            dkv_ref[...] = jnp.zeros_like(dkv_ref)

        for h in range(N_XHEADS):
            sl = pl.ds(h * hd, hd)
            vsl = pl.ds(d + h * hd, hd)
            qh, kh, doh = q_ref[:, sl], kv_ref[:, sl], do_ref[:, sl]
            p = _softmax(qh, kh, scale)
            dp = lax.dot_general(doh, kv_ref[:, vsl], _DIMS["nt"], preferred_element_type=F32)
            dkv_ref[:, vsl] += lax.dot_general(p.astype(BF16), doh, _DIMS["tn"], preferred_element_type=F32)
            ds = (p * (dp - jnp.sum(dp * p, axis=-1, keepdims=True)) * scale).astype(BF16)
            dq_ref[:, sl] = jnp.dot(ds, kh, preferred_element_type=F32).astype(BF16)
            dkv_ref[:, sl] += lax.dot_general(ds, qh, _DIMS["tn"], preferred_element_type=F32)

    row = pl.BlockSpec((tm, d), lambda i: (i, 0))
    full = pl.BlockSpec((mm_, 2 * d), lambda i: (0, 0))
    return pl.pallas_call(
        body, name="attn_bwd", grid=(t // tm,), in_specs=[row, full, row], out_specs=[row, full],
        out_shape=[_out(t, d, BF16), _out(mm_, 2 * d, F32)], compiler_params=_params(("arbitrary",)),
    )(q, kv, do)


def _ew(name, fn, ins, outs, rows_pref=256, rowvecs=()):
    r, c = ins[0].shape
    tr = _pick(r, rows_pref, SUBLANES)
    ni = len(ins) + len(rowvecs)

    def body(*refs):
        res = fn(*[x[...] for x in refs[:ni]])
        for o_ref, v in zip(refs[ni:], res):
            o_ref[...] = v.astype(o_ref.dtype)

    blk = pl.BlockSpec((tr, c), lambda i: (i, 0))
    vec = pl.BlockSpec((1, c), lambda i: (0, 0))
    return pl.pallas_call(
        body, name=name, grid=(r // tr,), in_specs=[blk] * len(ins) + [vec] * len(rowvecs), out_specs=[blk] * len(outs),
        out_shape=[_out(r, c, dt) for dt in outs], compiler_params=_params(("parallel",)),
    )(*ins, *rowvecs)


def _sum_slots(name, a, dtype):
    s, r, c = a.shape
    tr = _pick(r, 256, SUBLANES)

    def body(a_ref, o_ref):
        acc = a_ref[0].astype(F32)
        for k in range(1, s):
            acc = acc + a_ref[k].astype(F32)
        o_ref[...] = acc.astype(o_ref.dtype)

    return pl.pallas_call(
        body, name=name, grid=(r // tr,), in_specs=[pl.BlockSpec((s, tr, c), lambda i: (0, i, 0))],
        out_specs=pl.BlockSpec((tr, c), lambda i: (i, 0)), out_shape=_out(r, c, dtype),
        compiler_params=_params(("parallel",)),
    )(a)


def _adamw_step(wv, gv, mv, vv):
    bc1 = 1.0 - ADAM_B1 ** ADAM_STEP
    bc2 = 1.0 - ADAM_B2 ** ADAM_STEP
    m2 = ADAM_B1 * mv + (1.0 - ADAM_B1) * gv
    v2 = ADAM_B2 * vv + (1.0 - ADAM_B2) * (gv * gv)
    delta = -ADAM_LR * ((m2 / bc1) / (jnp.sqrt(v2 / bc2) + ADAM_EPS) + ADAM_WD * wv)
    return delta, m2, v2


def _adamw(name, w, g, m, v):
    return _ew(name, _adamw_step, [w, g, m, v], [F32, F32, F32])


def _adamw_rows(name, w, g_rows, m, v):
    k, r = w.shape
    tk = _pick(k, 256)

    def body(w_ref, g_ref, m_ref, v_ref, d_ref, m2_ref, v2_ref, gt_ref):
        gt = g_ref[...].T
        d_ref[...], m2_ref[...], v2_ref[...] = _adamw_step(w_ref[...], gt, m_ref[...], v_ref[...])
        gt_ref[...] = gt

    blk = pl.BlockSpec((tk, r), lambda j: (j, 0))
    return pl.pallas_call(
        body, name=name, grid=(k // tk,), in_specs=[blk, pl.BlockSpec((r, tk), lambda j: (0, j)), blk, blk],
        out_specs=[blk] * 4, out_shape=[_out(k, r, F32)] * 4, compiler_params=_params(("parallel",)),
    )(w, g_rows, m, v)


def _allgather(name, arrs):
    n = len(arrs)

    def body(*refs):
        ins, outs = refs[:n], refs[n:2 * n]
        send_sems, recv_sems, local_sems = refs[2 * n:]
        x, y, c = lax.axis_index("x"), lax.axis_index("y"), lax.axis_index("c")
        me, sibling = (x, y, c), (x, y, 1 - c)
        chips = [(1 - x, y), (x, 1 - y), (1 - x, 1 - y)]

        def rows(a, px, py, pc):
            r = ins[a].shape[0]
            return outs[a].at[pl.ds((4 * px + 2 * py + pc) * r, r), :]

        def copy(a, k, block, to, src=None):
            return pltpu.make_async_remote_copy(
                src_ref=rows(a, *block) if src is None else src, dst_ref=rows(a, *block),
                send_sem=send_sems.at[a, k], recv_sem=recv_sems.at[a, k], device_id=to, device_id_type=MESH)

        mine = [pltpu.make_async_copy(ins[a], rows(a, *me), local_sems.at[a]) for a in range(n)]
        for cp in mine:
            cp.start()
        first = []
        for a in range(n):
            first.append(copy(a, 0, me, sibling, src=ins[a]))
            first += [copy(a, 1 + j, me, (*chip, c), src=ins[a]) for j, chip in enumerate(chips)]
        for cp in first:
            cp.start()
        passed = []
        for j, chip in enumerate(chips):
            for a in range(n):
                copy(a, 1 + j, (*chip, c), me).wait_recv()
                cp = copy(a, 4 + j, (*chip, c), sibling)
                cp.start()
                passed.append(cp)
        for a in range(n):
            copy(a, 0, sibling, me).wait_recv()
            for j, chip in enumerate(chips):
                copy(a, 4 + j, (*chip, 1 - c), me).wait_recv()
        for cp in first + passed:
            cp.wait_send()
        for cp in mine:
            cp.wait()

    return pl.pallas_call(
        body, name=name, in_specs=[ANY] * n, out_specs=[ANY] * n,
        out_shape=[_out(N_DEV * a.shape[0], a.shape[1], a.dtype) for a in arrs],
        scratch_shapes=[pltpu.SemaphoreType.DMA((n, 7)), pltpu.SemaphoreType.DMA((n, 7)), pltpu.SemaphoreType.DMA((n,))],
    )(*arrs)


def _exchange_cores(name, blocks):
    n = len(blocks)
    c = blocks[0].shape[2]
    r = sum(b.shape[1] for b in blocks)

    def body(*refs):
        srcs, (recv_ref, send_sems, recv_sems) = refs[:n], refs[n:]
        x, y, cc = lax.axis_index("x"), lax.axis_index("y"), lax.axis_index("c")
        copies, off = [], 0
        for a, src in enumerate(srcs):
            rows = pl.ds(off, src.shape[1])
            off += src.shape[1]
            for q in range(4):
                copies.append(pltpu.make_async_remote_copy(
                    src_ref=src.at[2 * q + (1 - cc)], dst_ref=recv_ref.at[q, rows], send_sem=send_sems.at[a, q],
                    recv_sem=recv_sems.at[a, q], device_id=(x, y, 1 - cc), device_id_type=MESH))
        for cp in copies:
            cp.start()
        for cp in copies:
            cp.wait()

    return pl.pallas_call(
        body, name=name, in_specs=[ANY] * n, out_specs=ANY,
        out_shape=jax.ShapeDtypeStruct((4, r, c), blocks[0].dtype),
        scratch_shapes=[pltpu.SemaphoreType.DMA((n, 4))] * 2,
    )(*blocks)


def _peer(k, x, y, c):
    return (1 - x if k & 4 else x, 1 - y if k & 2 else y, 1 - c if k & 1 else c)


def _split_start(name, groups, after=None):
    pins = [] if after is None else [after]
    bufs, sem_shapes, spans = [], [], []
    for srcs, land_shapes, n_remote, n_local, _ in groups:
        sems = [pltpu.SemaphoreType.DMA((n_remote,)), pltpu.SemaphoreType.DMA((n_remote,))]
        sems += [pltpu.SemaphoreType.DMA((n_local,))] if n_local else []
        spans.append((len(bufs), len(srcs), len(land_shapes), len(sem_shapes), len(sems)))
        bufs += [pltpu.with_memory_space_constraint(a, pltpu.HBM) for a in srcs]
        bufs += [pltpu.with_memory_space_constraint(lax.empty(s.shape, s.dtype), pltpu.HBM) for s in land_shapes]
        sem_shapes += sems
    n_buf, n_sem = len(bufs), len(sem_shapes)

    def body(*refs):
        buf_refs, sem_refs, token = refs[:n_buf], refs[n_buf + len(pins):n_buf + len(pins) + n_sem], refs[-1]
        for (b0, ns, nl, s0, k), group in zip(spans, groups):
            remote, local = group[4](buf_refs[b0:b0 + ns], buf_refs[b0 + ns:b0 + ns + nl], *sem_refs[s0:s0 + k])
            for cp in local + remote:
                cp.start()
        token[...] = jnp.zeros_like(token)

    outs = pl.pallas_call(
        body, name=name,
        out_shape=sem_shapes + [pltpu.HBM(b.shape, b.dtype) for b in bufs] + [jax.ShapeDtypeStruct((SUBLANES, LANES), F32)],
        in_specs=[HBM] * n_buf + [ANY] * len(pins),
        out_specs=[SEM] * n_sem + [HBM] * n_buf + [pl.BlockSpec(memory_space=pltpu.VMEM)],
        input_output_aliases={i: n_sem + i for i in range(n_buf)},
        compiler_params=pltpu.CompilerParams(has_side_effects=SIDE_EFFECT),
    )(*bufs, *pins)
    return [dict(sems=list(outs[s0:s0 + k]), bufs=list(outs[n_sem + b0:n_sem + b0 + ns + nl]), token=outs[-1],
                 build=group[4], ns=ns) for (b0, ns, nl, s0, k), group in zip(spans, groups)]


def _split_wait(name, started, after):
    ns, n_buf, n_sem = started["ns"], len(started["bufs"]), len(started["sems"])

    def body(*refs):
        src_refs, land_refs = refs[:ns], refs[ns:n_buf]
        sems = refs[n_buf:n_buf + n_sem]
        remote, local = started["build"](src_refs, land_refs, *sems)
        for cp in local:
            cp.wait()
        for cp in remote:
            cp.wait_send()
            cp.wait_recv()

    outs = pl.pallas_call(
        body, name=name, out_shape=[pltpu.HBM(b.shape, b.dtype) for b in started["bufs"]],
        in_specs=[HBM] * n_buf + [SEM] * n_sem + [ANY], out_specs=[HBM] * n_buf,
        input_output_aliases={i: i for i in range(n_buf)},
        compiler_params=pltpu.CompilerParams(has_side_effects=SIDE_EFFECT),
    )(*started["bufs"], *started["sems"], after)
    return list(outs[:ns]), list(outs[ns:])


def _gather_group(shards):
    m = len(shards)

    def build(src_refs, land_refs, send_sems, recv_sems, local_sems):
        x, y, c = lax.axis_index("x"), lax.axis_index("y"), lax.axis_index("c")
        remote, local = [], []
        for j in range(m):
            r = src_refs[j].shape[0]
            dst = land_refs[j].at[pl.ds((4 * x + 2 * y + c) * r, r), :]
            local.append(pltpu.make_async_copy(src_refs[j], dst, local_sems.at[j]))
            for k in range(1, N_DEV):
                remote.append(pltpu.make_async_remote_copy(
                    src_ref=src_refs[j], dst_ref=dst, send_sem=send_sems.at[7 * j + k - 1],
                    recv_sem=recv_sems.at[7 * j + k - 1], device_id=_peer(k, x, y, c), device_id_type=MESH))
        return remote, local

    lands = [jax.ShapeDtypeStruct((N_DEV * a.shape[0], a.shape[1]), a.dtype) for a in shards]
    return shards, lands, 7 * m, m, build


def _slots_start(name, a):
    def build(src_refs, land_refs, send_sems, recv_sems, local_sems):
        x, y, c = lax.axis_index("x"), lax.axis_index("y"), lax.axis_index("c")
        dst = land_refs[0].at[4 * x + 2 * y + c]
        local = [pltpu.make_async_copy(src_refs[0], dst, local_sems.at[0])]
        remote = [pltpu.make_async_remote_copy(
            src_ref=src_refs[0], dst_ref=dst, send_sem=send_sems.at[k - 1], recv_sem=recv_sems.at[k - 1],
            device_id=_peer(k, x, y, c), device_id_type=MESH) for k in range(1, N_DEV)]
        return remote, local

    return _split_start(name, [([a], [jax.ShapeDtypeStruct((N_DEV,) + a.shape, a.dtype)], 7, 1, build)])[0]


def _chips_start(name, p):
    _, r, c = p.shape
    nck = r // GRAD_ROW_TILE

    def build(src_refs, land_refs, send_sems, recv_sems):
        x, y, cc = lax.axis_index("x"), lax.axis_index("y"), lax.axis_index("c")
        remote = []
        for k in range(1, 4):
            px = 1 - x if k >> 1 else x
            py = 1 - y if k & 1 else y
            for j in range(nck):
                rows = pl.ds(j * GRAD_ROW_TILE, GRAD_ROW_TILE)
                remote.append(pltpu.make_async_remote_copy(
                    src_ref=src_refs[0].at[2 * px + py, rows], dst_ref=land_refs[0].at[k - 1, rows],
                    send_sem=send_sems.at[(k - 1) * nck + j], recv_sem=recv_sems.at[(k - 1) * nck + j],
                    device_id=(px, py, cc), device_id_type=MESH))
        return remote, []

    return _split_start(name, [([p], [jax.ShapeDtypeStruct((3, r, c), p.dtype)], 3 * nck, 0, build)])[0]


def _chip_sum(name, p, recv, chip):
    _, r, c = p.shape
    tr = _pick(r, 5 * GRAD_ROW_TILE, GRAD_ROW_TILE)

    def body(chip_ref, p_ref, r_ref, o_ref):
        acc = p_ref[...].astype(F32)
        for k in range(3):
            acc = acc + r_ref[k].astype(F32)
        o_ref[...] = acc

    return pl.pallas_call(
        body, name=name,
        grid_spec=pltpu.PrefetchScalarGridSpec(
            num_scalar_prefetch=1, grid=(r // tr,),
            in_specs=[pl.BlockSpec((None, tr, c), lambda i, chip_ref: (chip_ref[0], i, 0)),
                      pl.BlockSpec((3, tr, c), lambda i, chip_ref: (0, i, 0))],
            out_specs=pl.BlockSpec((tr, c), lambda i, chip_ref: (i, 0))),
        out_shape=_out(r, c, F32), compiler_params=_params(("parallel",)),
    )(chip, p, recv)


def _local_step(x, mem, tgt, wt, sm, ev=None):
    t, d = x.shape
    n_mem = mem.shape[0]
    d_pool = sm["pool_scale"].shape[1]
    ng, pc = sm["pool_w"].shape[0], sm["pool_w"].shape[1]
    d_ssm = sm["ssm_d"].shape[1]
    _, sg, sp, sh = sm["ssm_b_re"].shape
    n_state = sg * sp
    gb, gs = {}, {}

    def emit(name, **kw):
        return ev(name, **kw) if ev is not None else None

    n1 = _rms_fwd("ffn1_norm", x, sm["ffn1_norm"])
    emit("ffn1_norm_done", marker=n1)
    def ffn1_down(hid):
        emit("ffn1_up_done", marker=hid)
        return wt["ffn1_w_down"]

    h1, ffn1_saved = _ffn_fwd("ffn1", x, n1, wt["ffn1_w_gate"], wt["ffn1_w_up"], ffn1_down)
    emit("ffn1_fwd_done", marker=h1)
    u = _rms_fwd("mix_norm", h1, sm["mix_norm"])
    d_in = wt["w_in"].shape[0]
    tm, tn = _pick(t, 1024), _pick(d_in, 1408)
    proj = _mm1("in_proj", "nt", u, wt["w_in"], t, d_in, tm, tn, F32)
    off_s = d_pool // d_ssm
    off_gp = (d_pool + d_ssm)
    off_gs = off_gp + d

    pool_w_bf = sm["pool_w"].astype(BF16)
    pooled, pm = _pool_fwd(proj, pool_w_bf, sm["pool_scale"])

    cols = [sm["ssm_a_re"].reshape(-1, 1), sm["ssm_a_im"].reshape(-1, 1),
            jnp.broadcast_to(sm["ssm_log_dt"][:, :, None], (2, sg, sp)).reshape(-1, 1),
            sm["ssm_b_re"].reshape(-1, sh), sm["ssm_b_im"].reshape(-1, sh)]
    abr, abi, bbr, bbi = _ssm_disc(cols)
    abr2, abi2 = abr.reshape(2, n_state), abi.reshape(2, n_state)
    bbr4, bbi4 = bbr.reshape(2, sg * sp, sh), bbi.reshape(2, sg * sp, sh)
    b_re = [_bd_in(bbr4[dr], sg, sp, sh).astype(BF16) for dr in range(2)]
    b_im = [_bd_in(bbi4[dr], sg, sp, sh).astype(BF16) for dr in range(2)]
    c_re = [_bd_out(sm["ssm_c_re"][dr], sg, sp, sh).astype(BF16) for dr in range(2)]
    c_im = [_bd_out(-sm["ssm_c_im"][dr], sg, sp, sh).astype(BF16) for dr in range(2)]
    sp32 = _to_segments(proj[:, d_pool:d_pool + d_ssm])
    xs, y_parts = [], []
    for dr in range(2):
        xr, xi, y_part = _ssm_fwd(f"ssm_fwd{dr}", sp32, b_re[dr], b_im[dr], c_re[dr], c_im[dr], abr2[dr:dr + 1],
                                  abi2[dr:dr + 1], reverse=(dr == 1))
        xs.append((xr, xi))
        y_parts.append(y_part)
    y = _from_segments(_ew("ssm_sum", lambda p0, p1, sv, dv: (p0 + p1 + sv * dv,), y_parts + [sp32], [F32],
                           rowvecs=[sm["ssm_d"]])[0])
    tmy = _pick(t, 256)
    ys = _ew("ssm_gelu", lambda v: (jax.nn.gelu(v),), [y], [BF16])[0]
    emit("mix_in_done", marker=ys)

    tmm, tnm, tnx = _pick(t, 1024), _pick(d, 256), _pick(d, 512)
    gp_spec = _tile(tmm, tnm, off_gp // tnm)
    gs_spec = _tile(tmm, tnm, off_gs // tnm)

    def merge_epi(accs, gpv, gsv):
        z_pool, val, gate = accs
        return (jax.nn.sigmoid(gpv) * z_pool + jax.nn.sigmoid(gsv) * (val * jax.nn.sigmoid(gate)),)

    merged = _mm("mix_merge", "nt", [pm, ys], [wt["w_pool_proj"], wt["w_glu_val"], wt["w_glu_gate"]],
                 [[(0, 0)], [(1, 1)], [(1, 2)]], t, d, tmm, tnm, [(proj, gp_spec), (proj, gs_spec)], merge_epi,
                 [(_out(t, d, BF16), None)])[0]
    res_epi = lambda accs, hin: (hin + accs[0],)
    h2 = _mm("mix_out", "nn", [merged], [wt["w_mix_out"]], [[(0, 0)]], t, d, tmm, tnx, [(h1, _tile(tmm, tnx))],
             res_epi, [(_out(t, d, F32), None)])[0]

    un = _rms_fwd("xattn_norm", h2, sm["xattn_norm"])
    mn = _rms_fwd("mem_norm", mem, sm["mem_norm"])
    emit("mix_done", marker=un)
    q = _mm1("xattn_q", "nn", un, wt["w_q"], t, d, tmm, tnx, BF16)
    kv = _mm1("xattn_kv", "nt", mn, wt["w_kv"], n_mem, 2 * d, n_mem, _pick(2 * d, 512), BF16)
    o = _attn_fwd(q, kv)
    h3 = _mm("xattn_out", "nn", [o], [wt["w_xo"]], [[(0, 0)]], t, d, tmm, tnx, [(h2, _tile(tmm, tnx))],
             res_epi, [(_out(t, d, F32), None)])[0]

    n2 = _rms_fwd("ffn2_norm", h3, sm["ffn2_norm"])
    emit("xattn_done", marker=n2)
    h4, ffn2_saved = _ffn_fwd("ffn2", h3, n2, wt["ffn2_w_gate"], wt["ffn2_w_up"], wt["ffn2_w_down"])

    dh4, dh4_bf, gs["final_norm"], loss = _loss_head(h4, sm["final_norm"], tgt)
    dh3, dh3_bf, gs["ffn2_norm"], gb["ffn2_w_gate"], gb["ffn2_w_up"], gb["ffn2_w_down"] = _ffn_bwd(
        "ffn2", h3, sm["ffn2_norm"], wt["ffn2_w_gate"], wt["ffn2_w_up"], wt["ffn2_w_down"], ffn2_saved, dh4, dh4_bf)

    tw = _pick(d, 1024)
    do = _mm1("xattn_do", "nt", dh3_bf, wt["w_xo"], t, d, tmm, tnx, BF16)
    gb["w_xo"] = _mm1("xattn_dwxo", "tn", o, dh3_bf, d, d, tw, tnx, BF16)
    dq, dkv = _attn_bwd(q, kv, do)
    gb["w_q"] = _mm1("xattn_dwq", "tn", un, dq, d, d, tw, tnx, BF16)
    dun = _mm1("xattn_dun", "nt", dq, wt["w_q"], t, d, tmm, tnx, F32)
    dh2, dh2_bf, gs["xattn_norm"] = _rms_bwd("xattn_norm_bwd", h2, sm["xattn_norm"], dun, dh3)
    gb["w_kv"] = _mm1("xattn_dwkv", "tn", dkv, mn, 2 * d, d, _pick(2 * d, 512), d, BF16)
    dmn = _mm1("xattn_dmn", "nn", dkv, wt["w_kv"], n_mem, d, n_mem, tnx, F32)
    gs["mem_norm"] = _rms_bwd("mem_norm_bwd", mem, sm["mem_norm"], dmn)

    gb["w_mix_out"] = _mm1("mix_dwout", "tn", merged, dh2_bf, d, d, tw, tnx, BF16)

    def merge_bwd_epi(accs, gpv, gsv):
        dmerged, z_pool, val, gate = accs
        sp_, ss_, sg_ = jax.nn.sigmoid(gpv), jax.nn.sigmoid(gsv), jax.nn.sigmoid(gate)
        glu = val * sg_
        dz_pool = dmerged * sp_
        dg_pool = dmerged * z_pool * (sp_ * (1.0 - sp_))
        dz_ssm = dmerged * ss_
        dg_ssm = dmerged * glu * (ss_ * (1.0 - ss_))
        dval = dz_ssm * sg_
        dgate = dz_ssm * glu * (1.0 - sg_)
        return dz_pool, dg_pool, dg_ssm, dval, dgate

    dz_pool, dg_pool, dg_ssm, dval, dgate = _mm(
        "mix_merge_bwd", "nt", [dh2_bf, pm, ys], [wt["w_mix_out"], wt["w_pool_proj"], wt["w_glu_val"], wt["w_glu_gate"]],
        [[(0, 0)], [(1, 1)], [(2, 2)], [(2, 3)]], t, d, tmm, tnm, [(proj, gp_spec), (proj, gs_spec)], merge_bwd_epi,
        [(_out(t, d, BF16), None)] * 5)
    gb["w_pool_proj"] = _mm1("pool_dwproj", "tn", dz_pool, pm, d, d_pool, tw, d_pool, BF16)
    gb["w_glu_val"] = _mm1("glu_dwval", "tn", dval, ys, d, d_ssm, tw, d_ssm, BF16)
    gb["w_glu_gate"] = _mm1("glu_dwgate", "tn", dgate, ys, d, d_ssm, tw, d_ssm, BF16)

    def gelu_bwd_epi(accs, yv):
        _, vjp = jax.vjp(jax.nn.gelu, yv)
        return (vjp(accs[0])[0],)

    dy = _mm("glu_dy", "nn", [dval, dgate], [wt["w_glu_val"], wt["w_glu_gate"]], [[(0, 0), (1, 1)]], t, d_ssm, tmy, d_ssm,
             [(y, _tile(tmy, d_ssm))], gelu_bwd_epi, [(_out(t, d_ssm, F32), None)])[0]
    gs["ssm_d"] = _colsum_prod("ssm_dd", dy, proj, b_coff=off_s)
    dyp = _to_segments(dy)
    d_abr, d_abi, d_bbr, d_bbi, d_cre, d_cim, lams = [], [], [], [], [], [], []
    ts = _pick(n_state, 512)
    tc_ = _pick(n_state, 256)
    both = lambda accs: tuple(accs)
    for dr in range(2):
        lr, li, dar, dai = _ssm_bwd(f"ssm_bwd{dr}", dyp, c_re[dr], c_im[dr], xs[dr][0], xs[dr][1], abr2[dr:dr + 1],
                                    abi2[dr:dr + 1], reverse=(dr == 1))
        d_abr.append(dar)
        d_abi.append(dai)
        lams += [lr, li]
        d_br, d_bi = _mm(f"ssm_db{dr}", "tn", [sp32], [lr, li], [[(0, 0)], [(0, 1)]], d_ssm, n_state, d_ssm, ts, [], both,
                         [(_out(d_ssm, n_state, F32), None)] * 2)
        d_bbr.append(_diag_in(d_br, sg, sp, sh))
        d_bbi.append(_diag_in(d_bi, sg, sp, sh))
        d_cr, d_ci = _mm(f"ssm_dc{dr}", "tn", [xs[dr][0], xs[dr][1]], [dyp], [[(0, 0)], [(1, 0)]], n_state, d_ssm, tc_,
                         d_ssm, [], both, [(_out(n_state, d_ssm, F32), None)] * 2)
        d_cre.append(_diag_out(d_cr, sg, sp, sh))
        d_cim.append(-_diag_out(d_ci, sg, sp, sh))
    ds = _from_segments(_mm(
        "ssm_ds", "nt", lams, [b_re[0], b_im[0], b_re[1], b_im[1]], [[(k, k) for k in range(4)]], t, d_ssm, tmy,
        d_ssm, [(dyp, _tile(tmy, d_ssm)), (sm["ssm_d"], _rowvec(d_ssm))],
        lambda accs, dyv, dv: (dyv * dv + accs[0],), [(_out(t, d_ssm, BF16), None)])[0])
    cots = [jnp.concatenate(d_abr, axis=0).reshape(-1, 1), jnp.concatenate(d_abi, axis=0).reshape(-1, 1),
            jnp.concatenate(d_bbr, axis=0), jnp.concatenate(d_bbi, axis=0)]
    d_are, d_aim, d_ldt, d_bre, d_bim = _ssm_disc_bwd(cols, cots)
    gs["ssm_a_re"] = d_are.reshape(2, sg, sp)
    gs["ssm_a_im"] = d_aim.reshape(2, sg, sp)
    gs["ssm_log_dt"] = _rowsum("ssm_dlogdt", d_ldt.reshape(2 * sg, sp)).reshape(2, sg)
    gs["ssm_b_re"] = d_bre.reshape(2, sg, sp, sh)
    gs["ssm_b_im"] = d_bim.reshape(2, sg, sp, sh)
    gs["ssm_c_re"] = jnp.stack(d_cre, axis=0)
    gs["ssm_c_im"] = jnp.stack(d_cim, axis=0)

    dpm = _mm1("pool_dpm", "nn", dz_pool, wt["w_pool_proj"], t, d_pool, tmm, _pick(d_pool, 256), F32)
    dp, gs["pool_w"], gs["pool_scale"] = _pool_bwd(pooled, dpm, pool_w_bf, sm["pool_scale"])

    w_in = wt["w_in"]
    parts = [(dp, 0, d_pool), (ds, d_pool, d_ssm), (dg_pool, off_gp, d), (dg_ssm, off_gs, d)]
    w_in_parts = [w_in[o0:o0 + width] for _, o0, width in parts]
    gb["w_in"] = jnp.concatenate(
        [_mm1(f"in_proj_dw{k}", "tn", p_[0], u, p_[2], d, _pick(p_[2], 1024), tnx, BF16) for k, p_ in enumerate(parts)], axis=0)
    pin = emit("grads_main", gb=gb)
    du = _mm("in_proj_du", "nn", [p_[0] for p_ in parts], w_in_parts, [[(k, k) for k in range(4)]], t, d, tmm, tnx, [],
             lambda accs: (accs[0],), [(_out(t, d, F32), None)], after=pin)[0]
    dh1, dh1_bf, gs["mix_norm"] = _rms_bwd("mix_norm_bwd", h1, sm["mix_norm"], du, dh2)
    pin = emit("small_early", gs=gs, loss=loss)

    def ffn1_weights_done(d_wg, d_wu, d_wd):
        gb["ffn1_w_gate"], gb["ffn1_w_up"], gb["ffn1_w_down"] = d_wg, d_wu, d_wd
        return emit("grads_ffn1", gb=gb)

    dx, _, gs["ffn1_norm"], _, _, _ = _ffn_bwd(
        "ffn1", x, sm["ffn1_norm"], wt["ffn1_w_gate"], wt["ffn1_w_up"], wt["ffn1_w_down"], ffn1_saved, dh1, dh1_bf,
        weights_done=ffn1_weights_done, after=pin)
    return loss, dx, gb, gs


WEIGHTS = ["ffn1_norm", "ffn1_w_gate", "ffn1_w_up", "ffn1_w_down", "mix_norm", "w_in", "pool_w", "pool_scale",
           "w_pool_proj", "ssm_a_re", "ssm_a_im", "ssm_log_dt", "ssm_b_re", "ssm_b_im", "ssm_c_re", "ssm_c_im", "ssm_d",
           "w_glu_val", "w_glu_gate", "w_mix_out", "xattn_norm", "mem_norm", "w_q", "w_kv", "w_xo", "ffn2_norm",
           "ffn2_w_gate", "ffn2_w_up", "ffn2_w_down", "final_norm"]
COL_SHARDED = ["ffn1_w_gate", "ffn1_w_up", "w_in", "w_pool_proj", "w_glu_val", "w_glu_gate", "w_kv", "ffn2_w_gate",
               "ffn2_w_up"]
ROW_SHARDED = ["ffn1_w_down", "w_mix_out", "w_q", "w_xo", "ffn2_w_down"]
BIG = [n for n in WEIGHTS if n in COL_SHARDED or n in ROW_SHARDED]
SMALL = [n for n in WEIGHTS if n not in BIG]
FFN1_BIG = ["ffn1_w_gate", "ffn1_w_up", "ffn1_w_down"]
MAIN_BIG = [n for n in BIG if n not in FFN1_BIG]
GATHER_PLAN = [("ffn1_up_done", ["ffn1_w_down"]), ("ffn1_fwd_done", ["w_in"]),
               ("mix_in_done", ["w_pool_proj", "w_glu_val", "w_glu_gate", "w_mix_out"]),
               ("mix_done", ["w_q", "w_kv", "w_xo"]), ("xattn_done", ["ffn2_w_gate", "ffn2_w_up", "ffn2_w_down"])]
LATE_SMALL = "ffn1_norm"
EARLY_SMALL = [n for n in SMALL if n != LATE_SMALL]
PACK_ROWS = SUBLANES * LANES
GRAD_ROW_TILE = 256


def _to_rows(name, w, width):
    if name in COL_SHARDED:
        w = w.T
    return w.reshape(-1, width)


def _pack_small(vals):
    flat = []
    for v in vals:
        f = v.reshape(-1)
        flat.append(jnp.pad(f, (0, (-f.shape[0]) % PACK_ROWS)))
    total = sum(f.shape[0] for f in flat)
    flat.append(jnp.zeros(((-total) % (GRAD_ROW_TILE * LANES),), F32))
    return jnp.concatenate(flat).reshape(-1, LANES)


def _unpack_small(packed, shapes):
    out, row = [], 0
    for shp in shapes:
        size = math.prod(shp)
        rows = -(-size // PACK_ROWS) * SUBLANES
        out.append(packed[row:row + rows].reshape(-1)[:size].reshape(shp))
        row += rows
    return out


def kernel(x, mem, ffn1_norm, ffn1_w_gate, ffn1_w_up, ffn1_w_down, mix_norm, w_in, pool_w, pool_scale, w_pool_proj, ssm_a_re, ssm_a_im, ssm_log_dt, ssm_b_re, ssm_b_im, ssm_c_re, ssm_c_im, ssm_d, w_glu_val, w_glu_gate, w_mix_out, xattn_norm, mem_norm, w_q, w_kv, w_xo, ffn2_norm, ffn2_w_gate, ffn2_w_up, ffn2_w_down, final_norm, loss_target, m_ffn1_norm, m_ffn1_w_gate, m_ffn1_w_up, m_ffn1_w_down, m_mix_norm, m_w_in, m_pool_w, m_pool_scale, m_w_pool_proj, m_ssm_a_re, m_ssm_a_im, m_ssm_log_dt, m_ssm_b_re, m_ssm_b_im, m_ssm_c_re, m_ssm_c_im, m_ssm_d, m_w_glu_val, m_w_glu_gate, m_w_mix_out, m_xattn_norm, m_mem_norm, m_w_q, m_w_kv, m_w_xo, m_ffn2_norm, m_ffn2_w_gate, m_ffn2_w_up, m_ffn2_w_down, m_final_norm, v_ffn1_norm, v_ffn1_w_gate, v_ffn1_w_up, v_ffn1_w_down, v_mix_norm, v_w_in, v_pool_w, v_pool_scale, v_w_pool_proj, v_ssm_a_re, v_ssm_a_im, v_ssm_log_dt, v_ssm_b_re, v_ssm_b_im, v_ssm_c_re, v_ssm_c_im, v_ssm_d, v_w_glu_val, v_w_glu_gate, v_w_mix_out, v_xattn_norm, v_mem_norm, v_w_q, v_w_kv, v_w_xo, v_ffn2_norm, v_ffn2_w_gate, v_ffn2_w_up, v_ffn2_w_down, v_final_norm):
    given = dict(locals())
    wts = {n: given[n] for n in WEIGHTS}
    moms = {n: (given["m_" + n], given["v_" + n]) for n in WEIGHTS}
    x2, mem2, tgt2 = x[0], mem[0], loss_target[0]
    d = x2.shape[1]
    chip = (2 * lax.axis_index("x") + lax.axis_index("y")).astype(jnp.int32).reshape(1)

    def full_form(n, f):
        shard = wts[n][0].shape
        return f.reshape(N_DEV * shard[1], shard[0]) if n in COL_SHARDED else f.reshape(N_DEV * shard[0], shard[1])

    shards = {n: _to_rows(n, wts[n][0], d).astype(BF16) for n in BIG}
    first = FFN1_BIG[:2]
    wt = {n: full_form(n, f) for n, f in zip(first, _allgather("weight_allgather_first", [shards[n] for n in first]))}
    started = _split_start("weight_gather_start", [_gather_group([shards[n] for n in names]) for _, names in GATHER_PLAN],
                           after=wt[first[0]])
    gathers = {event: (names, st) for (event, names), st in zip(GATHER_PLAN, started)}
    sm = {n: (wts[n].reshape(1, -1) if wts[n].ndim <= 2 else wts[n][0]) for n in SMALL}
    sm["ffn1_norm"] = sm["ffn1_norm"] + started[0]["token"][0, 0]

    pending = {}

    def reduce_start(tag, names, gb):
        blocks = [gb[n].reshape(N_DEV, -1, d) for n in names]
        pad_rows = (-sum(b.shape[1] for b in blocks)) % GRAD_ROW_TILE
        pad = [jnp.zeros((N_DEV, pad_rows, d), BF16)] if pad_rows else []
        recv = _exchange_cores("grad_exchange_cores_" + tag, blocks + pad)
        own = jnp.concatenate([lax.dynamic_index_in_dim(b.reshape(4, 2, b.shape[1], d), lax.axis_index("c"), 1, False)
                               for b in blocks + pad], axis=1)
        rows_all = own.shape[1]
        pair = _ew("grad_pair_sum_" + tag, lambda a, b: (a.astype(F32) + b.astype(F32),),
                   [own.reshape(-1, d), recv.reshape(-1, d)], [BF16], rows_pref=5 * GRAD_ROW_TILE)[0]
        pair = pair.reshape(4, rows_all, d)
        pending[tag] = (pair, _chips_start("grad_exchange_chips_start_" + tag, pair), [b.shape[1] for b in blocks])
        return pending[tag][1]["token"]

    def reduce_finish(tag, after):
        _, started, rows = pending[tag]
        (pair,), (recv,) = _split_wait("grad_exchange_chips_wait_" + tag, started, after)
        return _chip_sum("grad_chip_sum_" + tag, pair, recv, chip), rows

    def ev(name, gb=None, gs=None, loss=None, marker=None):
        if name in gathers:
            names, started = gathers[name]
            for n, f in zip(names, _split_wait("weight_gather_wait_" + name, started, marker)[1]):
                wt[n] = full_form(n, f)
        elif name == "grads_main":
            return reduce_start("main", MAIN_BIG, gb)
        elif name == "small_early":
            pending["small"] = _slots_start("small_gather_start", _pack_small([gs[n] for n in EARLY_SMALL] + [loss[:, :1]]))
            return pending["small"]["token"]
        elif name == "grads_ffn1":
            return reduce_start("ffn1", FFN1_BIG, gb)
        return None

    _, dx, _, gs = _local_step(x2, mem2, tgt2, wt, sm, ev)

    out_g, out_d, out_m, out_v = {}, {}, {}, {}

    def update(n, g_full):
        shape = wts[n].shape
        two_d = (-1, shape[-1])
        dl, m2, v2 = _adamw("adamw_" + n, wts[n].reshape(two_d), g_full.reshape(two_d), moms[n][0].reshape(two_d),
                            moms[n][1].reshape(two_d))
        out_g[n], out_d[n], out_m[n], out_v[n] = g_full, dl.reshape(shape), m2.reshape(shape), v2.reshape(shape)
        return dl

    def update_big(names, g_rows, rows):
        off = 0
        for n, r in zip(names, rows):
            shard = wts[n].shape
            if n in COL_SHARDED:
                w2, m2_, v2_ = (a.reshape(shard[1:]) for a in (wts[n], *moms[n]))
                res = _adamw_rows("adamw_" + n, w2, g_rows[off:off + r].reshape(shard[2], shard[1]), m2_, v2_)
                out_d[n], out_m[n], out_v[n], out_g[n] = (a.reshape(shard) for a in res)
                dl = res[0]
            else:
                dl = update(n, g_rows[off:off + r].reshape(shard))
            off += r
        return dl

    last = update_big(MAIN_BIG, *reduce_finish("main", dx))

    small_sum = _sum_slots("small_sum", _split_wait("small_gather_wait", pending["small"], dx)[1][0], F32)
    late = _allgather("small_allgather_late", [gs[LATE_SMALL].reshape(-1, LANES)])[0]
    late_sum = _sum_slots("small_sum_late", late.reshape(N_DEV, -1, LANES), F32)
    vals = _unpack_small(small_sum, [wts[n].shape for n in EARLY_SMALL] + [(1, 1)])
    total_loss = vals[-1].reshape(())
    for n, g_full in zip(EARLY_SMALL + [LATE_SMALL], vals[:-1] + [late_sum.reshape(wts[LATE_SMALL].shape)]):
        update(n, g_full)

    update_big(FFN1_BIG, *reduce_finish("ffn1", last))

    return (total_loss, dx[None], *[out_g[n] for n in WEIGHTS], *[out_d[n] for n in WEIGHTS],
            *[out_m[n] for n in WEIGHTS], *[out_v[n] for n in WEIGHTS])
```

```python
import functools
import math

import jax
import jax.numpy as jnp
from jax import lax
from jax.experimental import pallas as pl
from jax.experimental.pallas import tpu as pltpu

F32 = jnp.float32
BF16 = jnp.bfloat16
EPS = 1e-6
N_XHEADS = 4
POOL_WINDOWS = (2, 4, 8, 16)
ADAM_LR = 0.001
ADAM_B1 = 0.9
ADAM_B2 = 0.999
ADAM_EPS = 1e-08
ADAM_WD = 0.01
ADAM_STEP = 10
N_DEV = 8
VMEM_LIMIT_V7X = 48 * 1024 * 1024
LANES = 128
SUBLANES = 8
SUB_ROWS = 256
POOL_PAD = 16
MESH = pl.DeviceIdType.MESH
ANY = pl.BlockSpec(memory_space=pl.ANY)
HBM = pl.BlockSpec(memory_space=pltpu.HBM)
SEM = pl.BlockSpec(memory_space=pltpu.SEMAPHORE)
SIDE_EFFECT = pltpu.SideEffectType.DATAFLOW_SIDE_EFFECTING

_DIMS = {
    "nt": (((1,), (1,)), ((), ())),
    "nn": (((1,), (0,)), ((), ())),
    "tn": (((0,), (0,)), ((), ())),
}


def _pick(dim, pref, mult=LANES):
    if dim <= pref:
        return dim
    for t in range(pref - pref % mult, 0, -mult):
        if dim % t == 0:
            return t
    return dim


def _params(sem):
    return pltpu.CompilerParams(dimension_semantics=sem, vmem_limit_bytes=VMEM_LIMIT_V7X)


def _tile(tm, tn, coff=0):
    return pl.BlockSpec((tm, tn), lambda i, j: (i, j + coff))


def _rowvec(tn, coff=0):
    return pl.BlockSpec((1, tn), lambda i, j: (0, j + coff))


def _out(m, n, dtype):
    return jax.ShapeDtypeStruct((m, n), dtype)


def _mm(name, form, a_list, b_list, groups, m, n, tm, tn, extras, epilogue, outs, after=None, sub=SUB_ROWS):
    na, nb, ne = len(a_list), len(b_list), len(extras)
    pins = [] if after is None else [after]
    step = tm if (sub is None or form == "tn" or tm % sub) else sub

    def a_spec(a):
        if form == "tn":
            return pl.BlockSpec((a.shape[0], tm), lambda i, j: (0, i))
        return pl.BlockSpec((tm, a.shape[1]), lambda i, j: (i, 0))

    def b_spec(b):
        if form == "nt":
            return pl.BlockSpec((tn, b.shape[1]), lambda i, j: (j, 0))
        return pl.BlockSpec((b.shape[0], tn), lambda i, j: (0, j))

    def body(*refs):
        a_refs, b_refs = refs[:na], refs[na:na + nb]
        e_refs, o_refs = refs[na + nb:na + nb + ne], refs[na + nb + ne + len(pins):]
        b_vals = {}
        for s0 in range(0, tm, step):
            rows = slice(None) if step == tm else pl.ds(s0, step)
            a_vals, accs = {}, []
            for group in groups:
                acc = None
                for ai, bi in group:
                    if ai not in a_vals:
                        a_vals[ai] = (a_refs[ai][...] if form == "tn" else a_refs[ai][rows, :]).astype(BF16)
                    if bi not in b_vals:
                        b_vals[bi] = b_refs[bi][...].astype(BF16)
                    d = lax.dot_general(a_vals[ai], b_vals[bi], _DIMS[form], preferred_element_type=F32)
                    acc = d if acc is None else acc + d
                accs.append(acc)
            res = epilogue(accs, *[e[rows, :] if e.shape[0] == tm else e[...] for e in e_refs])
            for o_ref, r in zip(o_refs, res):
                o_ref[rows, :] = r.astype(o_ref.dtype)

    out_specs = [_tile(tm, tn) if s is None else s for _, s in outs]
    res = pl.pallas_call(
        body, name=name, grid=(m // tm, n // tn),
        in_specs=[a_spec(a) for a in a_list] + [b_spec(b) for b in b_list] + [s for _, s in extras] + [ANY] * len(pins),
        out_specs=out_specs, out_shape=[o for o, _ in outs],
        compiler_params=_params(("parallel", "parallel")),
    )(*a_list, *b_list, *[e for e, _ in extras], *pins)
    return res


def _mm1(name, form, a, b, m, n, tm, tn, dtype, scale=None):
    epi = (lambda accs: (accs[0],)) if scale is None else (lambda accs: (accs[0] * scale,))
    return _mm(name, form, [a], [b], [[(0, 0)]], m, n, tm, tn, [], epi, [(_out(m, n, dtype), None)])[0]


def _rms_fwd(name, h, g):
    t, d = h.shape
    tm = _pick(t, 512, SUBLANES)

    def body(h_ref, g_ref, n_ref):
        hv = h_ref[...]
        r = lax.rsqrt(jnp.mean(hv * hv, axis=-1, keepdims=True) + EPS)
        n_ref[...] = ((hv * r) * g_ref[...]).astype(BF16)

    return pl.pallas_call(
        body, name=name, grid=(t // tm,),
        in_specs=[pl.BlockSpec((tm, d), lambda i: (i, 0)), pl.BlockSpec((1, d), lambda i: (0, 0))],
        out_specs=pl.BlockSpec((tm, d), lambda i: (i, 0)), out_shape=_out(t, d, BF16),
        compiler_params=_params(("parallel",)),
    )(h, g)


def _rms_bwd(name, h, g, dn, dres=None):
    t, d = h.shape
    tm = _pick(t, 512, SUBLANES)
    need_dh = dres is not None

    def body(*refs):
        if need_dh:
            h_ref, g_ref, dn_ref, dres_ref, dh_ref, dhb_ref, dg_ref = refs
        else:
            h_ref, g_ref, dn_ref, dg_ref = refs
        hv = h_ref[...]
        r = lax.rsqrt(jnp.mean(hv * hv, axis=-1, keepdims=True) + EPS)
        nh = hv * r
        dnv = dn_ref[...].astype(F32)

        @pl.when(pl.program_id(0) == 0)
        def _():
            dg_ref[...] = jnp.zeros_like(dg_ref)

        dg_ref[...] += jnp.sum(dnv * nh, axis=0, keepdims=True)
        if need_dh:
            dng = dnv * g_ref[...]
            dh = dres_ref[...] + r * (dng - nh * jnp.mean(dng * nh, axis=-1, keepdims=True))
            dh_ref[...] = dh
            dhb_ref[...] = dh.astype(BF16)

    row = pl.BlockSpec((tm, d), lambda i: (i, 0))
    vec = pl.BlockSpec((1, d), lambda i: (0, 0))
    if need_dh:
        return pl.pallas_call(
            body, name=name, grid=(t // tm,), in_specs=[row, vec, row, row], out_specs=[row, row, vec],
            out_shape=[_out(t, d, F32), _out(t, d, BF16), _out(1, d, F32)], compiler_params=_params(("arbitrary",)),
        )(h, g, dn, dres)
    return pl.pallas_call(
        body, name=name, grid=(t // tm,), in_specs=[row, vec, row], out_specs=vec,
        out_shape=_out(1, d, F32), compiler_params=_params(("arbitrary",)),
    )(h, g, dn)


def _loss_head(h, g, tgt):
    t, d = h.shape
    tm = _pick(t, 512, SUBLANES)

    def body(h_ref, g_ref, t_ref, dh_ref, dhb_ref, dg_ref, loss_ref):
        hv = h_ref[...]
        r = lax.rsqrt(jnp.mean(hv * hv, axis=-1, keepdims=True) + EPS)
        nh = hv * r
        err = nh * g_ref[...] - t_ref[...]

        @pl.when(pl.program_id(0) == 0)
        def _():
            dg_ref[...] = jnp.zeros_like(dg_ref)
            loss_ref[...] = jnp.zeros_like(loss_ref)

        per_row = jnp.mean(err * err, axis=-1, keepdims=True)
        loss_ref[...] += 0.5 * jnp.sum(per_row, axis=0, keepdims=True)
        dy = err * (1.0 / d)
        dg_ref[...] += jnp.sum(dy * nh, axis=0, keepdims=True)
        dng = dy * g_ref[...]
        dh = r * (dng - nh * jnp.mean(dng * nh, axis=-1, keepdims=True))
        dh_ref[...] = dh
        dhb_ref[...] = dh.astype(BF16)

    row = pl.BlockSpec((tm, d), lambda i: (i, 0))
    vec = pl.BlockSpec((1, d), lambda i: (0, 0))
    return pl.pallas_call(
        body, name="loss_head", grid=(t // tm,), in_specs=[row, vec, row],
        out_specs=[row, row, vec, pl.BlockSpec((1, LANES), lambda i: (0, 0))],
        out_shape=[_out(t, d, F32), _out(t, d, BF16), _out(1, d, F32), _out(1, LANES, F32)],
        compiler_params=_params(("arbitrary",)),
    )(h, g, tgt)


def _ffn_fwd(tag, h, n, wg_t, wu_t, wd):
    t, d = h.shape
    f = wg_t.shape[0]
    tm, tn = _pick(t, 1024), _pick(f, 1408)

    def up_epi(accs):
        a, b = accs
        return a, b, (a * jax.nn.sigmoid(a)) * b

    a, b, hid = _mm(tag + "_up", "nt", [n], [wg_t, wu_t], [[(0, 0)], [(0, 1)]], t, f, tm, tn, [], up_epi,
                    [(_out(t, f, BF16), None)] * 3)
    if callable(wd):
        wd = wd(hid)
    tm2, tn2 = _pick(t, 1024), _pick(d, 512)
    h_out = _mm(tag + "_down", "nn", [hid], [wd], [[(0, 0)]], t, d, tm2, tn2, [(h, _tile(tm2, tn2))],
                lambda accs, hin: (hin + 0.5 * accs[0],), [(_out(t, d, F32), None)])[0]
    return h_out, (n, a, b, hid)


def _ffn_bwd(tag, h, g, wg_t, wu_t, wd, saved, dh, dh_bf, weights_done=None, after=None):
    n, a, b, hid = saved
    t, d = h.shape
    f = wd.shape[0]
    tm, tn = _pick(t, 1024), _pick(f, 1408)

    def hid_epi(accs, av, bv):
        dhid = 0.5 * accs[0]
        av, bv = av.astype(F32), bv.astype(F32)
        sig = jax.nn.sigmoid(av)
        da = dhid * bv * (sig * (1.0 + av * (1.0 - sig)))
        db = dhid * (av * sig)
        return da, db

    da, db = _mm(tag + "_bwd_hid", "nt", [dh_bf], [wd], [[(0, 0)]], t, f, tm, tn,
                 [(a, _tile(tm, tn)), (b, _tile(tm, tn))], hid_epi, [(_out(t, f, BF16), None)] * 2, after=after)
    tw, tnw = _pick(f, 1408), _pick(d, 512)
    d_wd = _mm1(tag + "_dwd", "tn", hid, dh_bf, f, d, tw, tnw, BF16, scale=0.5)
    d_wg = _mm1(tag + "_dwg", "tn", da, n, f, d, tw, tnw, BF16)
    d_wu = _mm1(tag + "_dwu", "tn", db, n, f, d, tw, tnw, BF16)
    pin = weights_done(d_wg, d_wu, d_wd) if weights_done is not None else None
    tm2, tn2 = _pick(t, 1024), _pick(d, 512)
    dn = _mm(tag + "_dn", "nn", [da, db], [wg_t, wu_t], [[(0, 0), (1, 1)]], t, d, tm2, tn2, [],
             lambda accs: (accs[0],), [(_out(t, d, F32), None)], after=pin)[0]
    dh_in, dh_in_bf, dg = _rms_bwd(tag + "_norm_bwd", h, g, dn, dh)
    return dh_in, dh_in_bf, dg, d_wg, d_wu, d_wd


def _window_sum(win, offsets):
    n = win.shape[0]
    acc = None
    for j in offsets:
        term = win if j == 0 else pltpu.roll(win, (-j) % n, 0)
        acc = term if acc is None else acc + term
    return acc


def _pool_counts(r0, ch, c, left, right, t):
    pos = r0 + lax.broadcasted_iota(jnp.int32, (ch, c), 0)
    return (jnp.minimum(pos + right + 1, t) - jnp.maximum(pos - left, 0)).astype(F32)


def _pool_fwd(proj, pool_w_bf, pool_scale):
    t = proj.shape[0]
    ng, c, _ = pool_w_bf.shape
    ch = _pick(t, 256, SUBLANES)
    pad = POOL_PAD

    def body(p_ref, w_ref, s_ref, pooled_ref, pm_ref, buf):
        grp = pl.program_id(0)
        buf[pl.ds(0, pad), :] = jnp.zeros((pad, c), F32)
        buf[pl.ds(pad + t, pad), :] = jnp.zeros((pad, c), F32)

        def fill(ci, carry):
            r0 = pl.multiple_of(ci * ch, SUBLANES)
            buf[pl.ds(pl.multiple_of(r0 + pad, SUBLANES), ch), :] = p_ref[pl.ds(r0, ch), :]
            return carry

        lax.fori_loop(0, t // ch, fill, 0)
        for gi, w in enumerate(POOL_WINDOWS):
            left = w // 2
            right = w - 1 - left

            @pl.when(grp == gi)
            def _(left=left, right=right):
                def chunk(ci, carry):
                    r0 = pl.multiple_of(ci * ch, SUBLANES)
                    win = buf[pl.ds(r0, ch + 2 * pad), :]
                    s = _window_sum(win, range(-left, right + 1))[pad:pad + ch]
                    pooled = s / _pool_counts(r0, ch, c, left, right, t) - win[pad:pad + ch]
                    pooled_bf = pooled.astype(BF16)
                    mixed = jnp.dot(pooled_bf, w_ref[0], preferred_element_type=F32)
                    pooled_ref[pl.ds(r0, ch), :] = pooled_bf
                    pm_ref[pl.ds(r0, ch), :] = (mixed * s_ref[...]).astype(BF16)
                    return carry

                lax.fori_loop(0, t // ch, chunk, 0)

    col = pl.BlockSpec((t, c), lambda g: (0, g))
    return pl.pallas_call(
        body, name="pool_fwd", grid=(ng,),
        in_specs=[col, pl.BlockSpec((1, c, c), lambda g: (g, 0, 0)), pl.BlockSpec((1, c), lambda g: (0, g))],
        out_specs=[col, col], out_shape=[_out(t, ng * c, BF16), _out(t, ng * c, BF16)],
        scratch_shapes=[pltpu.VMEM((t + 2 * pad, c), F32)],
        compiler_params=_params(("parallel",)),
    )(proj, pool_w_bf, pool_scale)


def _pool_bwd(pooled, dpm, pool_w_bf, pool_scale):
    t = pooled.shape[0]
    ng, c, _ = pool_w_bf.shape
    ch = _pick(t, 256, SUBLANES)
    pad = POOL_PAD

    def body(pooled_ref, dpm_ref, w_ref, s_ref, dp_ref, dw_ref, ds_ref, buf, raw):
        grp = pl.program_id(0)
        buf[pl.ds(0, pad), :] = jnp.zeros((pad, c), F32)
        buf[pl.ds(pad + t, pad), :] = jnp.zeros((pad, c), F32)
        dw_ref[...] = jnp.zeros_like(dw_ref)
        ds_ref[...] = jnp.zeros_like(ds_ref)
        for gi, w in enumerate(POOL_WINDOWS):
            left = w // 2
            right = w - 1 - left

            @pl.when(grp == gi)
            def _(left=left, right=right):
                def first(ci, carry):
                    r0 = pl.multiple_of(ci * ch, SUBLANES)
                    pv = pooled_ref[pl.ds(r0, ch), :]
                    dpm_v = dpm_ref[pl.ds(r0, ch), :]
                    mixed = jnp.dot(pv, w_ref[0], preferred_element_type=F32)
                    ds_ref[...] += jnp.sum(dpm_v * mixed, axis=0, keepdims=True)
                    dmixed = (dpm_v * s_ref[...]).astype(BF16)
                    dw_ref[0] += lax.dot_general(pv, dmixed, _DIMS["tn"], preferred_element_type=F32)
                    dpooled = lax.dot_general(dmixed, w_ref[0], _DIMS["nt"], preferred_element_type=F32)
                    raw[pl.ds(r0, ch), :] = dpooled
                    buf[pl.ds(pl.multiple_of(r0 + pad, SUBLANES), ch), :] = (
                        dpooled / _pool_counts(r0, ch, c, left, right, t))
                    return carry

                lax.fori_loop(0, t // ch, first, 0)

                def second(ci, carry):
                    r0 = pl.multiple_of(ci * ch, SUBLANES)
                    win = buf[pl.ds(r0, ch + 2 * pad), :]
                    s = _window_sum(win, range(-right, left + 1))[pad:pad + ch]
                    dp_ref[pl.ds(r0, ch), :] = (s - raw[pl.ds(r0, ch), :]).astype(BF16)
                    return carry

                lax.fori_loop(0, t // ch, second, 0)

    col = pl.BlockSpec((t, c), lambda g: (0, g))
    return pl.pallas_call(
        body, name="pool_bwd", grid=(ng,),
        in_specs=[col, col, pl.BlockSpec((1, c, c), lambda g: (g, 0, 0)), pl.BlockSpec((1, c), lambda g: (0, g))],
        out_specs=[col, pl.BlockSpec((1, c, c), lambda g: (g, 0, 0)), pl.BlockSpec((1, c), lambda g: (0, g))],
        out_shape=[_out(t, ng * c, BF16), jax.ShapeDtypeStruct((ng, c, c), F32), _out(1, ng * c, F32)],
        scratch_shapes=[pltpu.VMEM((t + 2 * pad, c), F32), pltpu.VMEM((t, c), F32)],
        compiler_params=_params(("parallel",)),
    )(pooled, dpm, pool_w_bf, pool_scale)


def _discretise(a_re, a_im, log_dt, b_re, b_im):
    dt = jnp.exp(log_dt)
    mag = jnp.exp(dt * a_re)
    ang = dt * a_im
    abr = mag * jnp.cos(ang)
    abi = mag * jnp.sin(ang)
    den = a_re * a_re + a_im * a_im
    nr = abr - 1.0
    qr = (nr * a_re + abi * a_im) / den
    qi = (abi * a_re - nr * a_im) / den
    return abr, abi, qr * b_re - qi * b_im, qr * b_im + qi * b_re


def _ssm_disc(cols):
    n, hh = cols[3].shape

    def body(ar, ai, ld, br, bi, o1, o2, o3, o4):
        res = _discretise(ar[...], ai[...], ld[...], br[...], bi[...])
        for o, r in zip((o1, o2, o3, o4), res):
            o[...] = r

    return pl.pallas_call(
        body, name="ssm_disc",
        out_shape=[_out(n, 1, F32), _out(n, 1, F32), _out(n, hh, F32), _out(n, hh, F32)],
    )(*cols)


def _ssm_disc_bwd(cols, cots):
    n, hh = cols[3].shape

    def body(ar, ai, ld, br, bi, c1, c2, c3, c4, o1, o2, o3, o4, o5):
        _, vjp = jax.vjp(_discretise, ar[...], ai[...], ld[...], br[...], bi[...])
        res = vjp((c1[...], c2[...], c3[...], c4[...]))
        for o, r in zip((o1, o2, o3, o4, o5), res):
            o[...] = r

    return pl.pallas_call(
        body, name="ssm_disc_bwd",
        out_shape=[_out(n, 1, F32)] * 3 + [_out(n, hh, F32)] * 2,
    )(*cols, *cots)


def _rowsum(name, a):
    r, _ = a.shape

    def body(a_ref, o_ref):
        o_ref[...] = jnp.sum(a_ref[...], axis=-1, keepdims=True)

    return pl.pallas_call(body, name=name, out_shape=_out(r, 1, F32))(a)


def _cmul(pr, pi, qr, qi):
    return pr * qr - pi * qi, pr * qi + pi * qr


def _cpow(pr, pi, n):
    rr, ri = None, None
    while n:
        if n & 1:
            rr, ri = (pr, pi) if rr is None else _cmul(rr, ri, pr, pi)
        n >>= 1
        if n:
            pr, pi = _cmul(pr, pi, pr, pi)
    return rr, ri


def _segment_carry(er, ei, pr, pi, reverse):
    row = lax.broadcasted_iota(jnp.int32, er.shape, 0)
    cr, ci = jnp.zeros_like(er), jnp.zeros_like(ei)
    for _ in range(SUBLANES - 1):
        tr = er + pr * cr - pi * ci
        ti = ei + pr * ci + pi * cr
        if reverse:
            keep, shift = row < SUBLANES - 1, SUBLANES - 1
        else:
            keep, shift = row >= 1, 1
        cr = jnp.where(keep, pltpu.roll(tr, shift, 0), 0.0)
        ci = jnp.where(keep, pltpu.roll(ti, shift, 0), 0.0)
    return cr, ci


def _ssm_fwd(name, sp, b_re, b_im, c_re, c_im, ar, ai, reverse):
    t, c = sp.shape
    s = ar.shape[1]
    w = _pick(s, 512)
    ch = _pick(t, 512, SUBLANES)
    n_ch, gpc, steps = t // ch, ch // SUBLANES, t // SUBLANES

    def body(sp_ref, bre_ref, bim_ref, cre_ref, cim_ref, ar_ref, ai_ref, xr_ref, xi_ref, y_ref, ur, ui, xbr, xbi):
        a_r = jnp.broadcast_to(ar_ref[...], (SUBLANES, w))
        a_i = jnp.broadcast_to(ai_ref[...], (SUBLANES, w))

        @pl.when(pl.program_id(0) == 0)
        def _():
            y_ref[...] = jnp.zeros_like(y_ref)

        def sweep(h0, store):
            def chunk(k, h):
                ci = n_ch - 1 - k if reverse else k
                rows = pl.ds(pl.multiple_of(ci * ch, ch), ch)
                spv = sp_ref[rows, :].astype(BF16)
                ur[...] = jnp.dot(spv, bre_ref[...], preferred_element_type=F32)
                ui[...] = jnp.dot(spv, bim_ref[...], preferred_element_type=F32)

                def group(g, hh):
                    gi = gpc - 1 - g if reverse else g
                    r0 = pl.multiple_of(gi * SUBLANES, SUBLANES)
                    hr, hi = hh
                    nr = a_r * hr - a_i * hi + ur[pl.ds(r0, SUBLANES), :]
                    ni = a_r * hi + a_i * hr + ui[pl.ds(r0, SUBLANES), :]
                    if store:
                        xbr[pl.ds(r0, SUBLANES), :] = nr
                        xbi[pl.ds(r0, SUBLANES), :] = ni
                    return nr, ni

                h = lax.fori_loop(0, gpc, group, h)
                if store:
                    xr16, xi16 = xbr[...].astype(BF16), xbi[...].astype(BF16)
                    xr_ref[rows, :] = xr16
                    xi_ref[rows, :] = xi16
                    y_ref[rows, :] += (jnp.dot(xr16, cre_ref[...], preferred_element_type=F32)
                                       + jnp.dot(xi16, cim_ref[...], preferred_element_type=F32))
                return h

            return lax.fori_loop(0, n_ch, chunk, h0)

        zero = jnp.zeros((SUBLANES, w), F32)
        er, ei = sweep((zero, zero), False)
        pr, pi = _cpow(ar_ref[...], ai_ref[...], steps)
        sweep(_segment_carry(er, ei, pr, pi, reverse), True)

    col = lambda i: (0, i)
    return pl.pallas_call(
        body, name=name, grid=(s // w,),
        in_specs=[pl.BlockSpec((t, c), lambda i: (0, 0)), pl.BlockSpec((c, w), col), pl.BlockSpec((c, w), col),
                  pl.BlockSpec((w, c), lambda i: (i, 0)), pl.BlockSpec((w, c), lambda i: (i, 0)),
                  pl.BlockSpec((1, w), col), pl.BlockSpec((1, w), col)],
        out_specs=[pl.BlockSpec((t, w), col), pl.BlockSpec((t, w), col), pl.BlockSpec((t, c), lambda i: (0, 0))],
        out_shape=[_out(t, s, BF16), _out(t, s, BF16), _out(t, c, F32)],
        scratch_shapes=[pltpu.VMEM((ch, w), F32)] * 4,
        compiler_params=_params(("arbitrary",)),
    )(sp, b_re, b_im, c_re, c_im, ar, ai)


def _ssm_bwd(name, dyp, c_re, c_im, xr, xi, ar, ai, reverse):
    t, c = dyp.shape
    s = ar.shape[1]
    w = _pick(s, 512)
    ch = _pick(t, 512, SUBLANES)
    n_ch, gpc, steps = t // ch, ch // SUBLANES, t // SUBLANES
    back = not reverse
    edge = 2 * SUBLANES

    def body(dy_ref, cre_ref, cim_ref, xr_ref, xi_ref, ar_ref, ai_ref, lr_ref, li_ref, dar_ref, dai_ref,
             gr, gi_, lbr, lbi, xbr, xbi):
        a_r = jnp.broadcast_to(ar_ref[...], (SUBLANES, w))
        a_i = -jnp.broadcast_to(ai_ref[...], (SUBLANES, w))
        row = lax.broadcasted_iota(jnp.int32, (SUBLANES, w), 0)

        def neighbours(ci, x_ref, buf):
            rows = pl.ds(pl.multiple_of(ci * ch, ch), ch)
            if reverse:
                buf[pl.ds(0, ch), :] = x_ref[rows, :].astype(F32)
                nxt = x_ref[pl.ds(pl.multiple_of(jnp.minimum(ci + 1, n_ch - 1) * ch, ch), edge), :].astype(F32)[:SUBLANES]
                first = x_ref[pl.ds(0, edge), :].astype(F32)[:SUBLANES]
                wrap = jnp.where(row < SUBLANES - 1, pltpu.roll(first, SUBLANES - 1, 0), 0.0)
                buf[pl.ds(ch, SUBLANES), :] = jnp.where(ci == n_ch - 1, wrap, nxt)
            else:
                buf[pl.ds(SUBLANES, ch), :] = x_ref[rows, :].astype(F32)
                prv = x_ref[pl.ds(pl.multiple_of(jnp.maximum(ci * ch - edge, 0), edge), edge), :].astype(F32)[SUBLANES:]
                last = x_ref[pl.ds(t - edge, edge), :].astype(F32)[SUBLANES:]
                wrap = jnp.where(row >= 1, pltpu.roll(last, 1, 0), 0.0)
                buf[pl.ds(0, SUBLANES), :] = jnp.where(ci == 0, wrap, prv)

        def sweep(h0, store):
            def chunk(k, carry):
                ci = n_ch - 1 - k if back else k
                rows = pl.ds(pl.multiple_of(ci * ch, ch), ch)
                dyv = dy_ref[rows, :].astype(BF16)
                gr[...] = lax.dot_general(dyv, cre_ref[...], _DIMS["nt"], preferred_element_type=F32)
                gi_[...] = lax.dot_general(dyv, cim_ref[...], _DIMS["nt"], preferred_element_type=F32)
                if store:
                    neighbours(ci, xr_ref, xbr)
                    neighbours(ci, xi_ref, xbi)

                def group(g, cc):
                    gidx = gpc - 1 - g if back else g
                    r0 = pl.multiple_of(gidx * SUBLANES, SUBLANES)
                    hr, hi = cc[0], cc[1]
                    nr = a_r * hr - a_i * hi + gr[pl.ds(r0, SUBLANES), :]
                    ni = a_r * hi + a_i * hr + gi_[pl.ds(r0, SUBLANES), :]
                    if not store:
                        return nr, ni
                    lbr[pl.ds(r0, SUBLANES), :] = nr
                    lbi[pl.ds(r0, SUBLANES), :] = ni
                    x0 = pl.multiple_of(r0 + SUBLANES, SUBLANES) if reverse else r0
                    xpr, xpi = xbr[pl.ds(x0, SUBLANES), :], xbi[pl.ds(x0, SUBLANES), :]
                    return nr, ni, cc[2] + nr * xpr + ni * xpi, cc[3] + ni * xpr - nr * xpi

                carry = lax.fori_loop(0, gpc, group, carry)
                if store:
                    lr_ref[rows, :] = lbr[...].astype(BF16)
                    li_ref[rows, :] = lbi[...].astype(BF16)
                return carry

            return lax.fori_loop(0, n_ch, chunk, h0)

        zero = jnp.zeros((SUBLANES, w), F32)
        er, ei = sweep((zero, zero), False)
        pr, pi = _cpow(ar_ref[...], -ai_ref[...], steps)
        cr, ci0 = _segment_carry(er, ei, pr, pi, back)
        _, _, dar, dai = sweep((cr, ci0, zero, zero), True)
        dar_ref[...] = jnp.sum(dar, axis=0, keepdims=True)
        dai_ref[...] = jnp.sum(dai, axis=0, keepdims=True)

    col = lambda i: (0, i)
    return pl.pallas_call(
        body, name=name, grid=(s // w,),
        in_specs=[pl.BlockSpec((t, c), lambda i: (0, 0)), pl.BlockSpec((w, c), lambda i: (i, 0)),
                  pl.BlockSpec((w, c), lambda i: (i, 0)), pl.BlockSpec((t, w), col), pl.BlockSpec((t, w), col),
                  pl.BlockSpec((1, w), col), pl.BlockSpec((1, w), col)],
        out_specs=[pl.BlockSpec((t, w), col), pl.BlockSpec((t, w), col), pl.BlockSpec((1, w), col), pl.BlockSpec((1, w), col)],
        out_shape=[_out(t, s, BF16), _out(t, s, BF16), _out(1, s, F32), _out(1, s, F32)],
        scratch_shapes=[pltpu.VMEM((ch, w), F32)] * 4 + [pltpu.VMEM((ch + SUBLANES, w), F32)] * 2,
        compiler_params=_params(("parallel",)),
    )(dyp, c_re, c_im, xr, xi, ar, ai)


def _to_segments(a):
    t, c = a.shape
    return a.reshape(SUBLANES, t // SUBLANES, c).transpose(1, 0, 2).reshape(t, c)


def _from_segments(a):
    t, c = a.shape
    return a.reshape(t // SUBLANES, SUBLANES, c).transpose(1, 0, 2).reshape(t, c)


def _colsum_prod(name, a, b, b_coff=0):
    t, n = a.shape
    tm = _pick(t, 512, SUBLANES)

    def body(a_ref, b_ref, o_ref):
        @pl.when(pl.program_id(0) == 0)
        def _():
            o_ref[...] = jnp.zeros_like(o_ref)

        o_ref[...] += jnp.sum(a_ref[...].astype(F32) * b_ref[...].astype(F32), axis=0, keepdims=True)

    return pl.pallas_call(
        body, name=name, grid=(t // tm,),
        in_specs=[pl.BlockSpec((tm, n), lambda i: (i, 0)), pl.BlockSpec((tm, n), lambda i: (i, b_coff))],
        out_specs=pl.BlockSpec((1, n), lambda i: (0, 0)), out_shape=_out(1, n, F32),
        compiler_params=_params(("arbitrary",)),
    )(a, b)


def _bd_in(bb, g, p, hh):
    blk = bb.reshape(g, p, hh).transpose(0, 2, 1)
    eye = jnp.eye(g, dtype=bool)[:, None, :, None]
    return jnp.where(eye, blk[:, :, None, :], 0.0).reshape(g * hh, g * p)


def _bd_out(cc, g, p, hh):
    blk = cc.transpose(0, 2, 1)
    eye = jnp.eye(g, dtype=bool)[:, None, :, None]
    return jnp.where(eye, blk[:, :, None, :], 0.0).reshape(g * p, g * hh)


def _diag_in(dmat, g, p, hh):
    eye = jnp.eye(g, dtype=bool)[:, None, :, None]
    diag = jnp.sum(jnp.where(eye, dmat.reshape(g, hh, g, p), 0.0), axis=2)
    return diag.transpose(0, 2, 1).reshape(g * p, hh)


def _diag_out(dmat, g, p, hh):
    eye = jnp.eye(g, dtype=bool)[:, None, :, None]
    diag = jnp.sum(jnp.where(eye, dmat.reshape(g, p, g, hh), 0.0), axis=2)
    return diag.transpose(0, 2, 1)


def _softmax(qh, kh, scale):
    s = lax.dot_general(qh, kh, _DIMS["nt"], preferred_element_type=F32) * scale
    e = jnp.exp(s - jnp.max(s, axis=-1, keepdims=True))
    return e / jnp.sum(e, axis=-1, keepdims=True)


def _attn_fwd(q, kv):
    t, d = q.shape
    mm_ = kv.shape[0]
    hd = d // N_XHEADS
    scale = 1.0 / math.sqrt(hd)
    tm = _pick(t, 512, SUBLANES)

    def body(q_ref, kv_ref, o_ref):
        for h in range(N_XHEADS):
            sl = pl.ds(h * hd, hd)
            p = _softmax(q_ref[:, sl], kv_ref[:, sl], scale)
            o_ref[:, sl] = jnp.dot(p.astype(BF16), kv_ref[:, pl.ds(d + h * hd, hd)],
                                   preferred_element_type=F32).astype(BF16)

    return pl.pallas_call(
        body, name="attn_fwd", grid=(t // tm,),
        in_specs=[pl.BlockSpec((tm, d), lambda i: (i, 0)), pl.BlockSpec((mm_, 2 * d), lambda i: (0, 0))],
        out_specs=pl.BlockSpec((tm, d), lambda i: (i, 0)), out_shape=_out(t, d, BF16),
        compiler_params=_params(("parallel",)),
    )(q, kv)


def _attn_bwd(q, kv, do):
    t, d = q.shape
    mm_ = kv.shape[0]
    hd = d // N_XHEADS
    scale = 1.0 / math.sqrt(hd)
    tm = _pick(t, 512, SUBLANES)

    def body(q_ref, kv_ref, do_ref, dq_ref, dkv_ref):
        @pl.when(pl.program_id(0) == 0)
        def _():
            dkv_ref[...] = jnp.zeros_like(dkv_ref)

        for h in range(N_XHEADS):
            sl = pl.ds(h * hd, hd)
            vsl = pl.ds(d + h * hd, hd)
            qh, kh, doh = q_ref[:, sl], kv_ref[:, sl], do_ref[:, sl]
            p = _softmax(qh, kh, scale)
            dp = lax.dot_general(doh, kv_ref[:, vsl], _DIMS["nt"], preferred_element_type=F32)
            dkv_ref[:, vsl] += lax.dot_general(p.astype(BF16), doh, _DIMS["tn"], preferred_element_type=F32)
            ds = (p * (dp - jnp.sum(dp * p, axis=-1, keepdims=True)) * scale).astype(BF16)
            dq_ref[:, sl] = jnp.dot(ds, kh, preferred_element_type=F32).astype(BF16)
            dkv_ref[:, sl] += lax.dot_general(ds, qh, _DIMS["tn"], preferred_element_type=F32)

    row = pl.BlockSpec((tm, d), lambda i: (i, 0))
    full = pl.BlockSpec((mm_, 2 * d), lambda i: (0, 0))
    return pl.pallas_call(
        body, name="attn_bwd", grid=(t // tm,), in_specs=[row, full, row], out_specs=[row, full],
        out_shape=[_out(t, d, BF16), _out(mm_, 2 * d, F32)], compiler_params=_params(("arbitrary",)),
    )(q, kv, do)


def _ew(name, fn, ins, outs, rows_pref=256, rowvecs=()):
    r, c = ins[0].shape
    tr = _pick(r, rows_pref, SUBLANES)
    ni = len(ins) + len(rowvecs)

    def body(*refs):
        res = fn(*[x[...] for x in refs[:ni]])
        for o_ref, v in zip(refs[ni:], res):
            o_ref[...] = v.astype(o_ref.dtype)

    blk = pl.BlockSpec((tr, c), lambda i: (i, 0))
    vec = pl.BlockSpec((1, c), lambda i: (0, 0))
    return pl.pallas_call(
        body, name=name, grid=(r // tr,), in_specs=[blk] * len(ins) + [vec] * len(rowvecs), out_specs=[blk] * len(outs),
        out_shape=[_out(r, c, dt) for dt in outs], compiler_params=_params(("parallel",)),
    )(*ins, *rowvecs)


def _sum_slots(name, a, dtype):
    s, r, c = a.shape
    tr = _pick(r, 256, SUBLANES)

    def body(a_ref, o_ref):
        acc = a_ref[0].astype(F32)
        for k in range(1, s):
            acc = acc + a_ref[k].astype(F32)
        o_ref[...] = acc.astype(o_ref.dtype)

    return pl.pallas_call(
        body, name=name, grid=(r // tr,), in_specs=[pl.BlockSpec((s, tr, c), lambda i: (0, i, 0))],
        out_specs=pl.BlockSpec((tr, c), lambda i: (i, 0)), out_shape=_out(r, c, dtype),
        compiler_params=_params(("parallel",)),
    )(a)


def _adamw_step(wv, gv, mv, vv):
    bc1 = 1.0 - ADAM_B1 ** ADAM_STEP
    bc2 = 1.0 - ADAM_B2 ** ADAM_STEP
    m2 = ADAM_B1 * mv + (1.0 - ADAM_B1) * gv
    v2 = ADAM_B2 * vv + (1.0 - ADAM_B2) * (gv * gv)
    delta = -ADAM_LR * ((m2 / bc1) / (jnp.sqrt(v2 / bc2) + ADAM_EPS) + ADAM_WD * wv)
    return delta, m2, v2


def _adamw_group(name, items, transposed):
    k, r = items[0][0].shape
    tk = _pick(k, 256, SUBLANES)
    n_out = 4 if transposed else 3

    def body(*refs):
        ins, outs = refs[:4 * len(items)], refs[4 * len(items):]
        for i in range(len(items)):
            w_ref, g_ref, m_ref, v_ref = ins[4 * i:4 * i + 4]
            gv = g_ref[...].T if transposed else g_ref[...]
            res = _adamw_step(w_ref[...], gv, m_ref[...], v_ref[...]) + ((gv,) if transposed else ())
            for o_ref, val in zip(outs[n_out * i:n_out * (i + 1)], res):
                o_ref[...] = val

    blk = pl.BlockSpec((tk, r), lambda j: (j, 0))
    g_blk = pl.BlockSpec((r, tk), lambda j: (0, j)) if transposed else blk
    res = pl.pallas_call(
        body, name=name, grid=(k // tk,), in_specs=[blk, g_blk, blk, blk] * len(items),
        out_specs=[blk] * (n_out * len(items)), out_shape=[_out(k, r, F32)] * (n_out * len(items)),
        compiler_params=_params(("parallel",)),
    )(*[a for item in items for a in item])
    return [res[n_out * i:n_out * (i + 1)] for i in range(len(items))]


def _allgather(name, arrs):
    n = len(arrs)

    def body(*refs):
        ins, outs = refs[:n], refs[n:2 * n]
        send_sems, recv_sems, local_sems = refs[2 * n:]
        x, y, c = lax.axis_index("x"), lax.axis_index("y"), lax.axis_index("c")
        me, sibling = (x, y, c), (x, y, 1 - c)
        chips = [(1 - x, y), (x, 1 - y), (1 - x, 1 - y)]

        def rows(a, px, py, pc):
            r = ins[a].shape[0]
            return outs[a].at[pl.ds((4 * px + 2 * py + pc) * r, r), :]

        def copy(a, k, block, to, src=None):
            return pltpu.make_async_remote_copy(
                src_ref=rows(a, *block) if src is None else src, dst_ref=rows(a, *block),
                send_sem=send_sems.at[a, k], recv_sem=recv_sems.at[a, k], device_id=to, device_id_type=MESH)

        mine = [pltpu.make_async_copy(ins[a], rows(a, *me), local_sems.at[a]) for a in range(n)]
        for cp in mine:
            cp.start()
        first = []
        for a in range(n):
            first.append(copy(a, 0, me, sibling, src=ins[a]))
            first += [copy(a, 1 + j, me, (*chip, c), src=ins[a]) for j, chip in enumerate(chips)]
        for cp in first:
            cp.start()
        passed = []
        for j, chip in enumerate(chips):
            for a in range(n):
                copy(a, 1 + j, (*chip, c), me).wait_recv()
                cp = copy(a, 4 + j, (*chip, c), sibling)
                cp.start()
                passed.append(cp)
        for a in range(n):
            copy(a, 0, sibling, me).wait_recv()
            for j, chip in enumerate(chips):
                copy(a, 4 + j, (*chip, 1 - c), me).wait_recv()
        for cp in first + passed:
            cp.wait_send()
        for cp in mine:
            cp.wait()

    return pl.pallas_call(
        body, name=name, in_specs=[ANY] * n, out_specs=[ANY] * n,
        out_shape=[_out(N_DEV * a.shape[0], a.shape[1], a.dtype) for a in arrs],
        scratch_shapes=[pltpu.SemaphoreType.DMA((n, 7)), pltpu.SemaphoreType.DMA((n, 7)), pltpu.SemaphoreType.DMA((n,))],
    )(*arrs)


def _exchange_cores(name, blocks):
    n = len(blocks)
    c = blocks[0].shape[2]
    r = sum(b.shape[1] for b in blocks)

    def body(*refs):
        srcs, (recv_ref, send_sems, recv_sems) = refs[:n], refs[n:]
        x, y, cc = lax.axis_index("x"), lax.axis_index("y"), lax.axis_index("c")
        copies, off = [], 0
        for a, src in enumerate(srcs):
            rows = pl.ds(off, src.shape[1])
            off += src.shape[1]
            for q in range(4):
                copies.append(pltpu.make_async_remote_copy(
                    src_ref=src.at[2 * q + (1 - cc)], dst_ref=recv_ref.at[q, rows], send_sem=send_sems.at[a, q],
                    recv_sem=recv_sems.at[a, q], device_id=(x, y, 1 - cc), device_id_type=MESH))
        for cp in copies:
            cp.start()
        for cp in copies:
            cp.wait()

    return pl.pallas_call(
        body, name=name, in_specs=[ANY] * n, out_specs=ANY,
        out_shape=jax.ShapeDtypeStruct((4, r, c), blocks[0].dtype),
        scratch_shapes=[pltpu.SemaphoreType.DMA((n, 4))] * 2,
    )(*blocks)


def _peer(k, x, y, c):
    return (1 - x if k & 4 else x, 1 - y if k & 2 else y, 1 - c if k & 1 else c)


def _split_start(name, groups, after=None):
    pins = [] if after is None else [after]
    bufs, sem_shapes, spans = [], [], []
    for srcs, land_shapes, n_remote, n_local, _ in groups:
        sems = [pltpu.SemaphoreType.DMA((n_remote,)), pltpu.SemaphoreType.DMA((n_remote,))]
        sems += [pltpu.SemaphoreType.DMA((n_local,))] if n_local else []
        spans.append((len(bufs), len(srcs), len(land_shapes), len(sem_shapes), len(sems)))
        bufs += [pltpu.with_memory_space_constraint(a, pltpu.HBM) for a in srcs]
        bufs += [pltpu.with_memory_space_constraint(lax.empty(s.shape, s.dtype), pltpu.HBM) for s in land_shapes]
        sem_shapes += sems
    n_buf, n_sem = len(bufs), len(sem_shapes)

    def body(*refs):
        buf_refs, sem_refs, token = refs[:n_buf], refs[n_buf + len(pins):n_buf + len(pins) + n_sem], refs[-1]
        for (b0, ns, nl, s0, k), group in zip(spans, groups):
            remote, local = group[4](buf_refs[b0:b0 + ns], buf_refs[b0 + ns:b0 + ns + nl], *sem_refs[s0:s0 + k])
            for cp in local + remote:
                cp.start()
        token[...] = jnp.zeros_like(token)

    outs = pl.pallas_call(
        body, name=name,
        out_shape=sem_shapes + [pltpu.HBM(b.shape, b.dtype) for b in bufs] + [jax.ShapeDtypeStruct((SUBLANES, LANES), F32)],
        in_specs=[HBM] * n_buf + [ANY] * len(pins),
        out_specs=[SEM] * n_sem + [HBM] * n_buf + [pl.BlockSpec(memory_space=pltpu.VMEM)],
        input_output_aliases={i: n_sem + i for i in range(n_buf)},
        compiler_params=pltpu.CompilerParams(has_side_effects=SIDE_EFFECT),
    )(*bufs, *pins)
    return [dict(sems=list(outs[s0:s0 + k]), bufs=list(outs[n_sem + b0:n_sem + b0 + ns + nl]), token=outs[-1],
                 build=group[4], ns=ns) for (b0, ns, nl, s0, k), group in zip(spans, groups)]


def _split_wait(name, started, after):
    ns, n_buf, n_sem = started["ns"], len(started["bufs"]), len(started["sems"])

    def body(*refs):
        src_refs, land_refs = refs[:ns], refs[ns:n_buf]
        sems = refs[n_buf:n_buf + n_sem]
        remote, local = started["build"](src_refs, land_refs, *sems)
        for cp in local:
            cp.wait()
        for cp in remote:
            cp.wait_send()
            cp.wait_recv()

    outs = pl.pallas_call(
        body, name=name, out_shape=[pltpu.HBM(b.shape, b.dtype) for b in started["bufs"]],
        in_specs=[HBM] * n_buf + [SEM] * n_sem + [ANY], out_specs=[HBM] * n_buf,
        input_output_aliases={i: i for i in range(n_buf)},
        compiler_params=pltpu.CompilerParams(has_side_effects=SIDE_EFFECT),
    )(*started["bufs"], *started["sems"], after)
    return list(outs[:ns]), list(outs[ns:])


def _gather_group(shards):
    m = len(shards)

    def build(src_refs, land_refs, send_sems, recv_sems, local_sems):
        x, y, c = lax.axis_index("x"), lax.axis_index("y"), lax.axis_index("c")
        remote, local = [], []
        for j in range(m):
            r = src_refs[j].shape[0]
            dst = land_refs[j].at[pl.ds((4 * x + 2 * y + c) * r, r), :]
            local.append(pltpu.make_async_copy(src_refs[j], dst, local_sems.at[j]))
            for k in range(1, N_DEV):
                remote.append(pltpu.make_async_remote_copy(
                    src_ref=src_refs[j], dst_ref=dst, send_sem=send_sems.at[7 * j + k - 1],
                    recv_sem=recv_sems.at[7 * j + k - 1], device_id=_peer(k, x, y, c), device_id_type=MESH))
        return remote, local

    lands = [jax.ShapeDtypeStruct((N_DEV * a.shape[0], a.shape[1]), a.dtype) for a in shards]
    return shards, lands, 7 * m, m, build


def _slots_start(name, a):
    def build(src_refs, land_refs, send_sems, recv_sems, local_sems):
        x, y, c = lax.axis_index("x"), lax.axis_index("y"), lax.axis_index("c")
        dst = land_refs[0].at[4 * x + 2 * y + c]
        local = [pltpu.make_async_copy(src_refs[0], dst, local_sems.at[0])]
        remote = [pltpu.make_async_remote_copy(
            src_ref=src_refs[0], dst_ref=dst, send_sem=send_sems.at[k - 1], recv_sem=recv_sems.at[k - 1],
            device_id=_peer(k, x, y, c), device_id_type=MESH) for k in range(1, N_DEV)]
        return remote, local

    return _split_start(name, [([a], [jax.ShapeDtypeStruct((N_DEV,) + a.shape, a.dtype)], 7, 1, build)])[0]


def _chips_start(name, p):
    _, r, c = p.shape
    nck = r // GRAD_ROW_TILE

    def build(src_refs, land_refs, send_sems, recv_sems):
        x, y, cc = lax.axis_index("x"), lax.axis_index("y"), lax.axis_index("c")
        remote = []
        for k in range(1, 4):
            px = 1 - x if k >> 1 else x
            py = 1 - y if k & 1 else y
            for j in range(nck):
                rows = pl.ds(j * GRAD_ROW_TILE, GRAD_ROW_TILE)
                remote.append(pltpu.make_async_remote_copy(
                    src_ref=src_refs[0].at[2 * px + py, rows], dst_ref=land_refs[0].at[k - 1, rows],
                    send_sem=send_sems.at[(k - 1) * nck + j], recv_sem=recv_sems.at[(k - 1) * nck + j],
                    device_id=(px, py, cc), device_id_type=MESH))
        return remote, []

    return _split_start(name, [([p], [jax.ShapeDtypeStruct((3, r, c), p.dtype)], 3 * nck, 0, build)])[0]


def _chip_sum(name, p, recv, chip):
    _, r, c = p.shape
    tr = _pick(r, 5 * GRAD_ROW_TILE, GRAD_ROW_TILE)

    def body(chip_ref, p_ref, r_ref, o_ref):
        acc = p_ref[...].astype(F32)
        for k in range(3):
            acc = acc + r_ref[k].astype(F32)
        o_ref[...] = acc

    return pl.pallas_call(
        body, name=name,
        grid_spec=pltpu.PrefetchScalarGridSpec(
            num_scalar_prefetch=1, grid=(r // tr,),
            in_specs=[pl.BlockSpec((None, tr, c), lambda i, chip_ref: (chip_ref[0], i, 0)),
                      pl.BlockSpec((3, tr, c), lambda i, chip_ref: (0, i, 0))],
            out_specs=pl.BlockSpec((tr, c), lambda i, chip_ref: (i, 0))),
        out_shape=_out(r, c, F32), compiler_params=_params(("parallel",)),
    )(chip, p, recv)


def _local_step(x, mem, tgt, wt, sm, ev=None):
    t, d = x.shape
    n_mem = mem.shape[0]
    d_pool = sm["pool_scale"].shape[1]
    ng, pc = sm["pool_w"].shape[0], sm["pool_w"].shape[1]
    d_ssm = sm["ssm_d"].shape[1]
    _, sg, sp, sh = sm["ssm_b_re"].shape
    n_state = sg * sp
    gb, gs = {}, {}

    def emit(name, **kw):
        return ev(name, **kw) if ev is not None else None

    n1 = _rms_fwd("ffn1_norm", x, sm["ffn1_norm"])
    emit("ffn1_norm_done", marker=n1)
    def ffn1_down(hid):
        emit("ffn1_up_done", marker=hid)
        return wt["ffn1_w_down"]

    h1, ffn1_saved = _ffn_fwd("ffn1", x, n1, wt["ffn1_w_gate"], wt["ffn1_w_up"], ffn1_down)
    emit("ffn1_fwd_done", marker=h1)
    u = _rms_fwd("mix_norm", h1, sm["mix_norm"])
    d_in = wt["w_in"].shape[0]
    tm, tn = _pick(t, 1024), _pick(d_in, 1408)
    proj = _mm1("in_proj", "nt", u, wt["w_in"], t, d_in, tm, tn, F32)
    off_s = d_pool // d_ssm
    off_gp = (d_pool + d_ssm)
    off_gs = off_gp + d

    pool_w_bf = sm["pool_w"].astype(BF16)
    pooled, pm = _pool_fwd(proj, pool_w_bf, sm["pool_scale"])

    cols = [sm["ssm_a_re"].reshape(-1, 1), sm["ssm_a_im"].reshape(-1, 1),
            jnp.broadcast_to(sm["ssm_log_dt"][:, :, None], (2, sg, sp)).reshape(-1, 1),
            sm["ssm_b_re"].reshape(-1, sh), sm["ssm_b_im"].reshape(-1, sh)]
    abr, abi, bbr, bbi = _ssm_disc(cols)
    abr2, abi2 = abr.reshape(2, n_state), abi.reshape(2, n_state)
    bbr4, bbi4 = bbr.reshape(2, sg * sp, sh), bbi.reshape(2, sg * sp, sh)
    b_re = [_bd_in(bbr4[dr], sg, sp, sh).astype(BF16) for dr in range(2)]
    b_im = [_bd_in(bbi4[dr], sg, sp, sh).astype(BF16) for dr in range(2)]
    c_re = [_bd_out(sm["ssm_c_re"][dr], sg, sp, sh).astype(BF16) for dr in range(2)]
    c_im = [_bd_out(-sm["ssm_c_im"][dr], sg, sp, sh).astype(BF16) for dr in range(2)]
    sp32 = _to_segments(proj[:, d_pool:d_pool + d_ssm])
    xs, y_parts = [], []
    for dr in range(2):
        xr, xi, y_part = _ssm_fwd(f"ssm_fwd{dr}", sp32, b_re[dr], b_im[dr], c_re[dr], c_im[dr], abr2[dr:dr + 1],
                                  abi2[dr:dr + 1], reverse=(dr == 1))
        xs.append((xr, xi))
        y_parts.append(y_part)
    y = _from_segments(_ew("ssm_sum", lambda p0, p1, sv, dv: (p0 + p1 + sv * dv,), y_parts + [sp32], [F32],
                           rowvecs=[sm["ssm_d"]])[0])
    tmy = _pick(t, 256)
    ys = _ew("ssm_gelu", lambda v: (jax.nn.gelu(v),), [y], [BF16])[0]
    emit("mix_in_done", marker=ys)

    tmm, tnm, tnx = _pick(t, 1024), _pick(d, 256), _pick(d, 512)
    gp_spec = _tile(tmm, tnm, off_gp // tnm)
    gs_spec = _tile(tmm, tnm, off_gs // tnm)

    def merge_epi(accs, gpv, gsv):
        z_pool, val, gate = accs
        return (jax.nn.sigmoid(gpv) * z_pool + jax.nn.sigmoid(gsv) * (val * jax.nn.sigmoid(gate)),)

    merged = _mm("mix_merge", "nt", [pm, ys], [wt["w_pool_proj"], wt["w_glu_val"], wt["w_glu_gate"]],
                 [[(0, 0)], [(1, 1)], [(1, 2)]], t, d, tmm, tnm, [(proj, gp_spec), (proj, gs_spec)], merge_epi,
                 [(_out(t, d, BF16), None)])[0]
    res_epi = lambda accs, hin: (hin + accs[0],)
    h2 = _mm("mix_out", "nn", [merged], [wt["w_mix_out"]], [[(0, 0)]], t, d, tmm, tnx, [(h1, _tile(tmm, tnx))],
             res_epi, [(_out(t, d, F32), None)])[0]

    un = _rms_fwd("xattn_norm", h2, sm["xattn_norm"])
    mn = _rms_fwd("mem_norm", mem, sm["mem_norm"])
    emit("mix_done", marker=un)
    q = _mm1("xattn_q", "nn", un, wt["w_q"], t, d, tmm, tnx, BF16)
    kv = _mm1("xattn_kv", "nt", mn, wt["w_kv"], n_mem, 2 * d, n_mem, _pick(2 * d, 512), BF16)
    o = _attn_fwd(q, kv)
    h3 = _mm("xattn_out", "nn", [o], [wt["w_xo"]], [[(0, 0)]], t, d, tmm, tnx, [(h2, _tile(tmm, tnx))],
             res_epi, [(_out(t, d, F32), None)])[0]

    n2 = _rms_fwd("ffn2_norm", h3, sm["ffn2_norm"])
    emit("xattn_done", marker=n2)
    h4, ffn2_saved = _ffn_fwd("ffn2", h3, n2, wt["ffn2_w_gate"], wt["ffn2_w_up"], wt["ffn2_w_down"])

    dh4, dh4_bf, gs["final_norm"], loss = _loss_head(h4, sm["final_norm"], tgt)
    dh3, dh3_bf, gs["ffn2_norm"], gb["ffn2_w_gate"], gb["ffn2_w_up"], gb["ffn2_w_down"] = _ffn_bwd(
        "ffn2", h3, sm["ffn2_norm"], wt["ffn2_w_gate"], wt["ffn2_w_up"], wt["ffn2_w_down"], ffn2_saved, dh4, dh4_bf)

    tw = _pick(d, 1024)
    do = _mm1("xattn_do", "nt", dh3_bf, wt["w_xo"], t, d, tmm, tnx, BF16)
    gb["w_xo"] = _mm1("xattn_dwxo", "tn", o, dh3_bf, d, d, tw, tnx, BF16)
    dq, dkv = _attn_bwd(q, kv, do)
    gb["w_q"] = _mm1("xattn_dwq", "tn", un, dq, d, d, tw, tnx, BF16)
    dun = _mm1("xattn_dun", "nt", dq, wt["w_q"], t, d, tmm, tnx, F32)
    dh2, dh2_bf, gs["xattn_norm"] = _rms_bwd("xattn_norm_bwd", h2, sm["xattn_norm"], dun, dh3)
    gb["w_kv"] = _mm1("xattn_dwkv", "tn", dkv, mn, 2 * d, d, _pick(2 * d, 512), d, BF16)
    dmn = _mm1("xattn_dmn", "nn", dkv, wt["w_kv"], n_mem, d, n_mem, tnx, F32)
    gs["mem_norm"] = _rms_bwd("mem_norm_bwd", mem, sm["mem_norm"], dmn)

    gb["w_mix_out"] = _mm1("mix_dwout", "tn", merged, dh2_bf, d, d, tw, tnx, BF16)

    def merge_bwd_epi(accs, gpv, gsv):
        dmerged, z_pool, val, gate = accs
        sp_, ss_, sg_ = jax.nn.sigmoid(gpv), jax.nn.sigmoid(gsv), jax.nn.sigmoid(gate)
        glu = val * sg_
        dz_pool = dmerged * sp_
        dg_pool = dmerged * z_pool * (sp_ * (1.0 - sp_))
        dz_ssm = dmerged * ss_
        dg_ssm = dmerged * glu * (ss_ * (1.0 - ss_))
        dval = dz_ssm * sg_
        dgate = dz_ssm * glu * (1.0 - sg_)
        return dz_pool, dg_pool, dg_ssm, dval, dgate

    dz_pool, dg_pool, dg_ssm, dval, dgate = _mm(
        "mix_merge_bwd", "nt", [dh2_bf, pm, ys], [wt["w_mix_out"], wt["w_pool_proj"], wt["w_glu_val"], wt["w_glu_gate"]],
        [[(0, 0)], [(1, 1)], [(2, 2)], [(2, 3)]], t, d, tmm, tnm, [(proj, gp_spec), (proj, gs_spec)], merge_bwd_epi,
        [(_out(t, d, BF16), None)] * 5)
    gb["w_pool_proj"] = _mm1("pool_dwproj", "tn", dz_pool, pm, d, d_pool, tw, d_pool, BF16)
    gb["w_glu_val"] = _mm1("glu_dwval", "tn", dval, ys, d, d_ssm, tw, d_ssm, BF16)
    gb["w_glu_gate"] = _mm1("glu_dwgate", "tn", dgate, ys, d, d_ssm, tw, d_ssm, BF16)

    def gelu_bwd_epi(accs, yv):
        _, vjp = jax.vjp(jax.nn.gelu, yv)
        return (vjp(accs[0])[0],)

    dy = _mm("glu_dy", "nn", [dval, dgate], [wt["w_glu_val"], wt["w_glu_gate"]], [[(0, 0), (1, 1)]], t, d_ssm, tmy, d_ssm,
             [(y, _tile(tmy, d_ssm))], gelu_bwd_epi, [(_out(t, d_ssm, F32), None)])[0]
    gs["ssm_d"] = _colsum_prod("ssm_dd", dy, proj, b_coff=off_s)
    dyp = _to_segments(dy)
    d_abr, d_abi, d_bbr, d_bbi, d_cre, d_cim, lams = [], [], [], [], [], [], []
    ts = _pick(n_state, 512)
    tc_ = _pick(n_state, 256)
    both = lambda accs: tuple(accs)
    for dr in range(2):
        lr, li, dar, dai = _ssm_bwd(f"ssm_bwd{dr}", dyp, c_re[dr], c_im[dr], xs[dr][0], xs[dr][1], abr2[dr:dr + 1],
                                    abi2[dr:dr + 1], reverse=(dr == 1))
        d_abr.append(dar)
        d_abi.append(dai)
        lams += [lr, li]
        d_br, d_bi = _mm(f"ssm_db{dr}", "tn", [sp32], [lr, li], [[(0, 0)], [(0, 1)]], d_ssm, n_state, d_ssm, ts, [], both,
                         [(_out(d_ssm, n_state, F32), None)] * 2)
        d_bbr.append(_diag_in(d_br, sg, sp, sh))
        d_bbi.append(_diag_in(d_bi, sg, sp, sh))
        d_cr, d_ci = _mm(f"ssm_dc{dr}", "tn", [xs[dr][0], xs[dr][1]], [dyp], [[(0, 0)], [(1, 0)]], n_state, d_ssm, tc_,
                         d_ssm, [], both, [(_out(n_state, d_ssm, F32), None)] * 2)
        d_cre.append(_diag_out(d_cr, sg, sp, sh))
        d_cim.append(-_diag_out(d_ci, sg, sp, sh))
    ds = _from_segments(_mm(
        "ssm_ds", "nt", lams, [b_re[0], b_im[0], b_re[1], b_im[1]], [[(k, k) for k in range(4)]], t, d_ssm, tmy,
        d_ssm, [(dyp, _tile(tmy, d_ssm)), (sm["ssm_d"], _rowvec(d_ssm))],
        lambda accs, dyv, dv: (dyv * dv + accs[0],), [(_out(t, d_ssm, BF16), None)])[0])
    cots = [jnp.concatenate(d_abr, axis=0).reshape(-1, 1), jnp.concatenate(d_abi, axis=0).reshape(-1, 1),
            jnp.concatenate(d_bbr, axis=0), jnp.concatenate(d_bbi, axis=0)]
    d_are, d_aim, d_ldt, d_bre, d_bim = _ssm_disc_bwd(cols, cots)
    gs["ssm_a_re"] = d_are.reshape(2, sg, sp)
    gs["ssm_a_im"] = d_aim.reshape(2, sg, sp)
    gs["ssm_log_dt"] = _rowsum("ssm_dlogdt", d_ldt.reshape(2 * sg, sp)).reshape(2, sg)
    gs["ssm_b_re"] = d_bre.reshape(2, sg, sp, sh)
    gs["ssm_b_im"] = d_bim.reshape(2, sg, sp, sh)
    gs["ssm_c_re"] = jnp.stack(d_cre, axis=0)
    gs["ssm_c_im"] = jnp.stack(d_cim, axis=0)

    dpm = _mm1("pool_dpm", "nn", dz_pool, wt["w_pool_proj"], t, d_pool, tmm, _pick(d_pool, 256), F32)
    dp, gs["pool_w"], gs["pool_scale"] = _pool_bwd(pooled, dpm, pool_w_bf, sm["pool_scale"])

    w_in = wt["w_in"]
    parts = [(dp, 0, d_pool), (ds, d_pool, d_ssm), (dg_pool, off_gp, d), (dg_ssm, off_gs, d)]
    w_in_parts = [w_in[o0:o0 + width] for _, o0, width in parts]
    gb["w_in"] = jnp.concatenate(
        [_mm1(f"in_proj_dw{k}", "tn", p_[0], u, p_[2], d, _pick(p_[2], 1024), tnx, BF16) for k, p_ in enumerate(parts)], axis=0)
    pin = emit("grads_main", gb=gb)
    du = _mm("in_proj_du", "nn", [p_[0] for p_ in parts], w_in_parts, [[(k, k) for k in range(4)]], t, d, tmm, tnx, [],
             lambda accs: (accs[0],), [(_out(t, d, F32), None)], after=pin)[0]
    dh1, dh1_bf, gs["mix_norm"] = _rms_bwd("mix_norm_bwd", h1, sm["mix_norm"], du, dh2)
    pin = emit("small_early", gs=gs, loss=loss)

    def ffn1_weights_done(d_wg, d_wu, d_wd):
        gb["ffn1_w_gate"], gb["ffn1_w_up"], gb["ffn1_w_down"] = d_wg, d_wu, d_wd
        return emit("grads_ffn1", gb=gb)

    dx, _, gs["ffn1_norm"], _, _, _ = _ffn_bwd(
        "ffn1", x, sm["ffn1_norm"], wt["ffn1_w_gate"], wt["ffn1_w_up"], wt["ffn1_w_down"], ffn1_saved, dh1, dh1_bf,
        weights_done=ffn1_weights_done, after=pin)
    return loss, dx, gb, gs


WEIGHTS = ["ffn1_norm", "ffn1_w_gate", "ffn1_w_up", "ffn1_w_down", "mix_norm", "w_in", "pool_w", "pool_scale",
           "w_pool_proj", "ssm_a_re", "ssm_a_im", "ssm_log_dt", "ssm_b_re", "ssm_b_im", "ssm_c_re", "ssm_c_im", "ssm_d",
           "w_glu_val", "w_glu_gate", "w_mix_out", "xattn_norm", "mem_norm", "w_q", "w_kv", "w_xo", "ffn2_norm",
           "ffn2_w_gate", "ffn2_w_up", "ffn2_w_down", "final_norm"]
COL_SHARDED = ["ffn1_w_gate", "ffn1_w_up", "w_in", "w_pool_proj", "w_glu_val", "w_glu_gate", "w_kv", "ffn2_w_gate",
               "ffn2_w_up"]
ROW_SHARDED = ["ffn1_w_down", "w_mix_out", "w_q", "w_xo", "ffn2_w_down"]
BIG = [n for n in WEIGHTS if n in COL_SHARDED or n in ROW_SHARDED]
SMALL = [n for n in WEIGHTS if n not in BIG]
FFN1_BIG = ["ffn1_w_gate", "ffn1_w_up", "ffn1_w_down"]
MAIN_BIG = [n for n in BIG if n not in FFN1_BIG]
GATHER_PLAN = [("ffn1_up_done", ["ffn1_w_down"]), ("ffn1_fwd_done", ["w_in"]),
               ("mix_in_done", ["w_pool_proj", "w_glu_val", "w_glu_gate", "w_mix_out"]),
               ("mix_done", ["w_q", "w_kv", "w_xo"]), ("xattn_done", ["ffn2_w_gate", "ffn2_w_up", "ffn2_w_down"])]
LATE_SMALL = "ffn1_norm"
EARLY_SMALL = [n for n in SMALL if n != LATE_SMALL]
PACK_ROWS = SUBLANES * LANES
GRAD_ROW_TILE = 256


def _to_rows(name, w, width):
    if name in COL_SHARDED:
        w = w.T
    return w.reshape(-1, width)


def _pack_small(vals):
    flat = []
    for v in vals:
        f = v.reshape(-1)
        flat.append(jnp.pad(f, (0, (-f.shape[0]) % PACK_ROWS)))
    total = sum(f.shape[0] for f in flat)
    flat.append(jnp.zeros(((-total) % (GRAD_ROW_TILE * LANES),), F32))
    return jnp.concatenate(flat).reshape(-1, LANES)


def _unpack_small(packed, shapes):
    out, row = [], 0
    for shp in shapes:
        size = math.prod(shp)
        rows = -(-size // PACK_ROWS) * SUBLANES
        out.append(packed[row:row + rows].reshape(-1)[:size].reshape(shp))
        row += rows
    return out


def kernel(x, mem, ffn1_norm, ffn1_w_gate, ffn1_w_up, ffn1_w_down, mix_norm, w_in, pool_w, pool_scale, w_pool_proj, ssm_a_re, ssm_a_im, ssm_log_dt, ssm_b_re, ssm_b_im, ssm_c_re, ssm_c_im, ssm_d, w_glu_val, w_glu_gate, w_mix_out, xattn_norm, mem_norm, w_q, w_kv, w_xo, ffn2_norm, ffn2_w_gate, ffn2_w_up, ffn2_w_down, final_norm, loss_target, m_ffn1_norm, m_ffn1_w_gate, m_ffn1_w_up, m_ffn1_w_down, m_mix_norm, m_w_in, m_pool_w, m_pool_scale, m_w_pool_proj, m_ssm_a_re, m_ssm_a_im, m_ssm_log_dt, m_ssm_b_re, m_ssm_b_im, m_ssm_c_re, m_ssm_c_im, m_ssm_d, m_w_glu_val, m_w_glu_gate, m_w_mix_out, m_xattn_norm, m_mem_norm, m_w_q, m_w_kv, m_w_xo, m_ffn2_norm, m_ffn2_w_gate, m_ffn2_w_up, m_ffn2_w_down, m_final_norm, v_ffn1_norm, v_ffn1_w_gate, v_ffn1_w_up, v_ffn1_w_down, v_mix_norm, v_w_in, v_pool_w, v_pool_scale, v_w_pool_proj, v_ssm_a_re, v_ssm_a_im, v_ssm_log_dt, v_ssm_b_re, v_ssm_b_im, v_ssm_c_re, v_ssm_c_im, v_ssm_d, v_w_glu_val, v_w_glu_gate, v_w_mix_out, v_xattn_norm, v_mem_norm, v_w_q, v_w_kv, v_w_xo, v_ffn2_norm, v_ffn2_w_gate, v_ffn2_w_up, v_ffn2_w_down, v_final_norm):
    given = dict(locals())
    wts = {n: given[n] for n in WEIGHTS}
    moms = {n: (given["m_" + n], given["v_" + n]) for n in WEIGHTS}
    x2, mem2, tgt2 = x[0], mem[0], loss_target[0]
    d = x2.shape[1]
    chip = (2 * lax.axis_index("x") + lax.axis_index("y")).astype(jnp.int32).reshape(1)

    def full_form(n, f):
        shard = wts[n][0].shape
        return f.reshape(N_DEV * shard[1], shard[0]) if n in COL_SHARDED else f.reshape(N_DEV * shard[0], shard[1])

    shards = {n: _to_rows(n, wts[n][0], d).astype(BF16) for n in BIG}
    first = FFN1_BIG[:2]
    wt = {n: full_form(n, f) for n, f in zip(first, _allgather("weight_allgather_first", [shards[n] for n in first]))}
    started = _split_start("weight_gather_start", [_gather_group([shards[n] for n in names]) for _, names in GATHER_PLAN],
                           after=wt[first[0]])
    gathers = {event: (names, st) for (event, names), st in zip(GATHER_PLAN, started)}
    sm = {n: (wts[n].reshape(1, -1) if wts[n].ndim <= 2 else wts[n][0]) for n in SMALL}
    sm["ffn1_norm"] = sm["ffn1_norm"] + started[0]["token"][0, 0]

    pending = {}

    def reduce_start(tag, names, gb):
        blocks = [gb[n].reshape(N_DEV, -1, d) for n in names]
        pad_rows = (-sum(b.shape[1] for b in blocks)) % GRAD_ROW_TILE
        pad = [jnp.zeros((N_DEV, pad_rows, d), BF16)] if pad_rows else []
        recv = _exchange_cores("grad_exchange_cores_" + tag, blocks + pad)
        own = jnp.concatenate([lax.dynamic_index_in_dim(b.reshape(4, 2, b.shape[1], d), lax.axis_index("c"), 1, False)
                               for b in blocks + pad], axis=1)
        rows_all = own.shape[1]
        pair = _ew("grad_pair_sum_" + tag, lambda a, b: (a.astype(F32) + b.astype(F32),),
                   [own.reshape(-1, d), recv.reshape(-1, d)], [BF16], rows_pref=5 * GRAD_ROW_TILE)[0]
        pair = pair.reshape(4, rows_all, d)
        pending[tag] = (pair, _chips_start("grad_exchange_chips_start_" + tag, pair), [b.shape[1] for b in blocks])
        return pending[tag][1]["token"]

    def reduce_finish(tag, after):
        _, started, rows = pending[tag]
        (pair,), (recv,) = _split_wait("grad_exchange_chips_wait_" + tag, started, after)
        return _chip_sum("grad_chip_sum_" + tag, pair, recv, chip), rows

    def ev(name, gb=None, gs=None, loss=None, marker=None):
        if name in gathers:
            names, started = gathers[name]
            for n, f in zip(names, _split_wait("weight_gather_wait_" + name, started, marker)[1]):
                wt[n] = full_form(n, f)
        elif name == "grads_main":
            return reduce_start("main", MAIN_BIG, gb)
        elif name == "small_early":
            pending["small"] = _slots_start("small_gather_start", _pack_small([gs[n] for n in EARLY_SMALL] + [loss[:, :1]]))
            return pending["small"]["token"]
        elif name == "grads_ffn1":
            return reduce_start("ffn1", FFN1_BIG, gb)
        return None

    _, dx, _, gs = _local_step(x2, mem2, tgt2, wt, sm, ev)

    grads = {}
    for tag, names in (("main", MAIN_BIG), ("ffn1", FFN1_BIG)):
        g_rows, rows = reduce_finish(tag, dx)
        off = 0
        for n, r in zip(names, rows):
            shard = wts[n].shape
            grads[n] = g_rows[off:off + r].reshape((shard[2], shard[1]) if n in COL_SHARDED else shard[1:])
            off += r
    small_sum = _sum_slots("small_sum", _split_wait("small_gather_wait", pending["small"], dx)[1][0], F32)
    late = _allgather("small_allgather_late", [gs[LATE_SMALL].reshape(-1, LANES)])[0]
    late_sum = _sum_slots("small_sum_late", late.reshape(N_DEV, -1, LANES), F32)
    vals = _unpack_small(small_sum, [wts[n].shape for n in EARLY_SMALL] + [(1, 1)])
    total_loss = vals[-1].reshape(())
    for n, g_full in zip(EARLY_SMALL + [LATE_SMALL], vals[:-1] + [late_sum]):
        grads[n] = g_full.reshape(-1, wts[n].shape[-1])

    out_g, out_d, out_m, out_v = {}, {}, {}, {}
    by_shape = {}
    for n in WEIGHTS:
        by_shape.setdefault((wts[n].size // wts[n].shape[-1], wts[n].shape[-1], n in COL_SHARDED), []).append(n)
    for (_, _, transposed), names in by_shape.items():
        two_d = (-1, wts[names[0]].shape[-1])
        items = [(wts[n].reshape(two_d), grads[n], moms[n][0].reshape(two_d), moms[n][1].reshape(two_d)) for n in names]
        for n, res in zip(names, _adamw_group("adamw_" + names[0], items, transposed)):
            shape = wts[n].shape
            out_d[n], out_m[n], out_v[n] = (a.reshape(shape) for a in res[:3])
            out_g[n] = (res[3] if transposed else grads[n]).reshape(shape)

    return (total_loss, dx[None], *[out_g[n] for n in WEIGHTS], *[out_d[n] for n in WEIGHTS],
            *[out_m[n] for n in WEIGHTS], *[out_v[n] for n in WEIGHTS])
```

```python
import functools
import math

import jax
import jax.numpy as jnp
from jax import lax
from jax.experimental import pallas as pl
from jax.experimental.pallas import tpu as pltpu

F32 = jnp.float32
BF16 = jnp.bfloat16
EPS = 1e-6
N_XHEADS = 4
POOL_WINDOWS = (2, 4, 8, 16)
ADAM_LR = 0.001
ADAM_B1 = 0.9
ADAM_B2 = 0.999
ADAM_EPS = 1e-08
ADAM_WD = 0.01
ADAM_STEP = 10
N_DEV = 8
VMEM_LIMIT_V7X = 48 * 1024 * 1024
LANES = 128
SUBLANES = 8
SUB_ROWS = 256
POOL_PAD = 16
MESH = pl.DeviceIdType.MESH
ANY = pl.BlockSpec(memory_space=pl.ANY)
HBM = pl.BlockSpec(memory_space=pltpu.HBM)
SEM = pl.BlockSpec(memory_space=pltpu.SEMAPHORE)
SIDE_EFFECT = pltpu.SideEffectType.DATAFLOW_SIDE_EFFECTING

_DIMS = {
    "nt": (((1,), (1,)), ((), ())),
    "nn": (((1,), (0,)), ((), ())),
    "tn": (((0,), (0,)), ((), ())),
}


def _pick(dim, pref, mult=LANES):
    if dim <= pref:
        return dim
    for t in range(pref - pref % mult, 0, -mult):
        if dim % t == 0:
            return t
    return dim


def _params(sem):
    return pltpu.CompilerParams(dimension_semantics=sem, vmem_limit_bytes=VMEM_LIMIT_V7X)


def _tile(tm, tn, coff=0):
    return pl.BlockSpec((tm, tn), lambda i, j: (i, j + coff))


def _rowvec(tn, coff=0):
    return pl.BlockSpec((1, tn), lambda i, j: (0, j + coff))


def _out(m, n, dtype):
    return jax.ShapeDtypeStruct((m, n), dtype)


def _mm(name, form, a_list, b_list, groups, m, n, tm, tn, extras, epilogue, outs, after=None, sub=SUB_ROWS):
    na, nb, ne = len(a_list), len(b_list), len(extras)
    pins = [] if after is None else [after]
    step = tm if (sub is None or form == "tn" or tm % sub) else sub

    def a_spec(a):
        if form == "tn":
            return pl.BlockSpec((a.shape[0], tm), lambda i, j: (0, i))
        return pl.BlockSpec((tm, a.shape[1]), lambda i, j: (i, 0))

    def b_spec(b):
        if form == "nt":
            return pl.BlockSpec((tn, b.shape[1]), lambda i, j: (j, 0))
        return pl.BlockSpec((b.shape[0], tn), lambda i, j: (0, j))

    def body(*refs):
        a_refs, b_refs = refs[:na], refs[na:na + nb]
        e_refs, o_refs = refs[na + nb:na + nb + ne], refs[na + nb + ne + len(pins):]
        b_vals = {}
        for s0 in range(0, tm, step):
            rows = slice(None) if step == tm else pl.ds(s0, step)
            a_vals, accs = {}, []
            for group in groups:
                acc = None
                for ai, bi in group:
                    if ai not in a_vals:
                        a_vals[ai] = (a_refs[ai][...] if form == "tn" else a_refs[ai][rows, :]).astype(BF16)
                    if bi not in b_vals:
                        b_vals[bi] = b_refs[bi][...].astype(BF16)
                    d = lax.dot_general(a_vals[ai], b_vals[bi], _DIMS[form], preferred_element_type=F32)
                    acc = d if acc is None else acc + d
                accs.append(acc)
            res = epilogue(accs, *[e[rows, :] if e.shape[0] == tm else e[...] for e in e_refs])
            for o_ref, r in zip(o_refs, res):
                o_ref[rows, :] = r.astype(o_ref.dtype)

    out_specs = [_tile(tm, tn) if s is None else s for _, s in outs]
    res = pl.pallas_call(
        body, name=name, grid=(m // tm, n // tn),
        in_specs=[a_spec(a) for a in a_list] + [b_spec(b) for b in b_list] + [s for _, s in extras] + [ANY] * len(pins),
        out_specs=out_specs, out_shape=[o for o, _ in outs],
        compiler_params=_params(("parallel", "parallel")),
    )(*a_list, *b_list, *[e for e, _ in extras], *pins)
    return res


def _mm1(name, form, a, b, m, n, tm, tn, dtype, scale=None):
    epi = (lambda accs: (accs[0],)) if scale is None else (lambda accs: (accs[0] * scale,))
    return _mm(name, form, [a], [b], [[(0, 0)]], m, n, tm, tn, [], epi, [(_out(m, n, dtype), None)])[0]


def _rms_fwd(name, h, g):
    t, d = h.shape
    tm = _pick(t, 512, SUBLANES)

    def body(h_ref, g_ref, n_ref):
        hv = h_ref[...]
        r = lax.rsqrt(jnp.mean(hv * hv, axis=-1, keepdims=True) + EPS)
        n_ref[...] = ((hv * r) * g_ref[...]).astype(BF16)

    return pl.pallas_call(
        body, name=name, grid=(t // tm,),
        in_specs=[pl.BlockSpec((tm, d), lambda i: (i, 0)), pl.BlockSpec((1, d), lambda i: (0, 0))],
        out_specs=pl.BlockSpec((tm, d), lambda i: (i, 0)), out_shape=_out(t, d, BF16),
        compiler_params=_params(("parallel",)),
    )(h, g)


def _rms_bwd(name, h, g, dn, dres=None):
    t, d = h.shape
    tm = _pick(t, 512, SUBLANES)
    need_dh = dres is not None

    def body(*refs):
        if need_dh:
            h_ref, g_ref, dn_ref, dres_ref, dh_ref, dhb_ref, dg_ref = refs
        else:
            h_ref, g_ref, dn_ref, dg_ref = refs
        hv = h_ref[...]
        r = lax.rsqrt(jnp.mean(hv * hv, axis=-1, keepdims=True) + EPS)
        nh = hv * r
        dnv = dn_ref[...].astype(F32)

        @pl.when(pl.program_id(0) == 0)
        def _():
            dg_ref[...] = jnp.zeros_like(dg_ref)

        dg_ref[...] += jnp.sum(dnv * nh, axis=0, keepdims=True)
        if need_dh:
            dng = dnv * g_ref[...]
            dh = dres_ref[...] + r * (dng - nh * jnp.mean(dng * nh, axis=-1, keepdims=True))
            dh_ref[...] = dh
            dhb_ref[...] = dh.astype(BF16)

    row = pl.BlockSpec((tm, d), lambda i: (i, 0))
    vec = pl.BlockSpec((1, d), lambda i: (0, 0))
    if need_dh:
        return pl.pallas_call(
            body, name=name, grid=(t // tm,), in_specs=[row, vec, row, row], out_specs=[row, row, vec],
            out_shape=[_out(t, d, F32), _out(t, d, BF16), _out(1, d, F32)], compiler_params=_params(("arbitrary",)),
        )(h, g, dn, dres)
    return pl.pallas_call(
        body, name=name, grid=(t // tm,), in_specs=[row, vec, row], out_specs=vec,
        out_shape=_out(1, d, F32), compiler_params=_params(("arbitrary",)),
    )(h, g, dn)


def _loss_head(h, g, tgt):
    t, d = h.shape
    tm = _pick(t, 512, SUBLANES)

    def body(h_ref, g_ref, t_ref, dh_ref, dhb_ref, dg_ref, loss_ref):
        hv = h_ref[...]
        r = lax.rsqrt(jnp.mean(hv * hv, axis=-1, keepdims=True) + EPS)
        nh = hv * r
        err = nh * g_ref[...] - t_ref[...]

        @pl.when(pl.program_id(0) == 0)
        def _():
            dg_ref[...] = jnp.zeros_like(dg_ref)
            loss_ref[...] = jnp.zeros_like(loss_ref)

        per_row = jnp.mean(err * err, axis=-1, keepdims=True)
        loss_ref[...] += 0.5 * jnp.sum(per_row, axis=0, keepdims=True)
        dy = err * (1.0 / d)
        dg_ref[...] += jnp.sum(dy * nh, axis=0, keepdims=True)
        dng = dy * g_ref[...]
        dh = r * (dng - nh * jnp.mean(dng * nh, axis=-1, keepdims=True))
        dh_ref[...] = dh
        dhb_ref[...] = dh.astype(BF16)

    row = pl.BlockSpec((tm, d), lambda i: (i, 0))
    vec = pl.BlockSpec((1, d), lambda i: (0, 0))
    return pl.pallas_call(
        body, name="loss_head", grid=(t // tm,), in_specs=[row, vec, row],
        out_specs=[row, row, vec, pl.BlockSpec((1, LANES), lambda i: (0, 0))],
        out_shape=[_out(t, d, F32), _out(t, d, BF16), _out(1, d, F32), _out(1, LANES, F32)],
        compiler_params=_params(("arbitrary",)),
    )(h, g, tgt)


def _ffn_fwd(tag, h, n, wg_t, wu_t, wd):
    t, d = h.shape
    f = wg_t.shape[0]
    tm, tn = _pick(t, 1024), _pick(f, 1408)

    def up_epi(accs):
        a, b = accs
        return a, b, (a * jax.nn.sigmoid(a)) * b

    a, b, hid = _mm(tag + "_up", "nt", [n], [wg_t, wu_t], [[(0, 0)], [(0, 1)]], t, f, tm, tn, [], up_epi,
                    [(_out(t, f, BF16), None)] * 3)
    if callable(wd):
        wd = wd(hid)
    tm2, tn2 = _pick(t, 1024), _pick(d, 512)
    h_out = _mm(tag + "_down", "nn", [hid], [wd], [[(0, 0)]], t, d, tm2, tn2, [(h, _tile(tm2, tn2))],
                lambda accs, hin: (hin + 0.5 * accs[0],), [(_out(t, d, F32), None)])[0]
    return h_out, (n, a, b, hid)


def _ffn_bwd(tag, h, g, wg_t, wu_t, wd, saved, dh, dh_bf, weights_done=None, after=None):
    n, a, b, hid = saved
    t, d = h.shape
    f = wd.shape[0]
    tm, tn = _pick(t, 1024), _pick(f, 1408)

    def hid_epi(accs, av, bv):
        dhid = 0.5 * accs[0]
        av, bv = av.astype(F32), bv.astype(F32)
        sig = jax.nn.sigmoid(av)
        da = dhid * bv * (sig * (1.0 + av * (1.0 - sig)))
        db = dhid * (av * sig)
        return da, db

    da, db = _mm(tag + "_bwd_hid", "nt", [dh_bf], [wd], [[(0, 0)]], t, f, tm, tn,
                 [(a, _tile(tm, tn)), (b, _tile(tm, tn))], hid_epi, [(_out(t, f, BF16), None)] * 2, after=after)
    tw, tnw = _pick(f, 1408), _pick(d, 512)
    d_wd = _mm1(tag + "_dwd", "tn", hid, dh_bf, f, d, tw, tnw, BF16, scale=0.5)
    d_wg = _mm1(tag + "_dwg", "tn", da, n, f, d, tw, tnw, BF16)
    d_wu = _mm1(tag + "_dwu", "tn", db, n, f, d, tw, tnw, BF16)
    pin = weights_done(d_wg, d_wu, d_wd) if weights_done is not None else None
    tm2, tn2 = _pick(t, 1024), _pick(d, 512)
    dn = _mm(tag + "_dn", "nn", [da, db], [wg_t, wu_t], [[(0, 0), (1, 1)]], t, d, tm2, tn2, [],
             lambda accs: (accs[0],), [(_out(t, d, F32), None)], after=pin)[0]
    dh_in, dh_in_bf, dg = _rms_bwd(tag + "_norm_bwd", h, g, dn, dh)
    return dh_in, dh_in_bf, dg, d_wg, d_wu, d_wd


def _window_sum(win, offsets):
    n = win.shape[0]
    acc = None
    for j in offsets:
        term = win if j == 0 else pltpu.roll(win, (-j) % n, 0)
        acc = term if acc is None else acc + term
    return acc


def _pool_counts(r0, ch, c, left, right, t):
    pos = r0 + lax.broadcasted_iota(jnp.int32, (ch, c), 0)
    return (jnp.minimum(pos + right + 1, t) - jnp.maximum(pos - left, 0)).astype(F32)


def _pool_fwd(proj, pool_w_bf, pool_scale):
    t = proj.shape[0]
    ng, c, _ = pool_w_bf.shape
    ch = _pick(t, 256, SUBLANES)
    pad = POOL_PAD

    def body(p_ref, w_ref, s_ref, pooled_ref, pm_ref, buf):
        grp = pl.program_id(0)
        buf[pl.ds(0, pad), :] = jnp.zeros((pad, c), F32)
        buf[pl.ds(pad + t, pad), :] = jnp.zeros((pad, c), F32)

        def fill(ci, carry):
            r0 = pl.multiple_of(ci * ch, SUBLANES)
            buf[pl.ds(pl.multiple_of(r0 + pad, SUBLANES), ch), :] = p_ref[pl.ds(r0, ch), :]
            return carry

        lax.fori_loop(0, t // ch, fill, 0)
        for gi, w in enumerate(POOL_WINDOWS):
            left = w // 2
            right = w - 1 - left

            @pl.when(grp == gi)
            def _(left=left, right=right):
                def chunk(ci, carry):
                    r0 = pl.multiple_of(ci * ch, SUBLANES)
                    win = buf[pl.ds(r0, ch + 2 * pad), :]
                    s = _window_sum(win, range(-left, right + 1))[pad:pad + ch]
                    pooled = s / _pool_counts(r0, ch, c, left, right, t) - win[pad:pad + ch]
                    pooled_bf = pooled.astype(BF16)
                    mixed = jnp.dot(pooled_bf, w_ref[0], preferred_element_type=F32)
                    pooled_ref[pl.ds(r0, ch), :] = pooled_bf
                    pm_ref[pl.ds(r0, ch), :] = (mixed * s_ref[...]).astype(BF16)
                    return carry

                lax.fori_loop(0, t // ch, chunk, 0)

    col = pl.BlockSpec((t, c), lambda g: (0, g))
    return pl.pallas_call(
        body, name="pool_fwd", grid=(ng,),
        in_specs=[col, pl.BlockSpec((1, c, c), lambda g: (g, 0, 0)), pl.BlockSpec((1, c), lambda g: (0, g))],
        out_specs=[col, col], out_shape=[_out(t, ng * c, BF16), _out(t, ng * c, BF16)],
        scratch_shapes=[pltpu.VMEM((t + 2 * pad, c), F32)],
        compiler_params=_params(("parallel",)),
    )(proj, pool_w_bf, pool_scale)


def _pool_bwd(pooled, dpm, pool_w_bf, pool_scale):
    t = pooled.shape[0]
    ng, c, _ = pool_w_bf.shape
    ch = _pick(t, 256, SUBLANES)
    pad = POOL_PAD

    def body(pooled_ref, dpm_ref, w_ref, s_ref, dp_ref, dw_ref, ds_ref, buf, raw):
        grp = pl.program_id(0)
        buf[pl.ds(0, pad), :] = jnp.zeros((pad, c), F32)
        buf[pl.ds(pad + t, pad), :] = jnp.zeros((pad, c), F32)
        dw_ref[...] = jnp.zeros_like(dw_ref)
        ds_ref[...] = jnp.zeros_like(ds_ref)
        for gi, w in enumerate(POOL_WINDOWS):
            left = w // 2
            right = w - 1 - left

            @pl.when(grp == gi)
            def _(left=left, right=right):
                def first(ci, carry):
                    r0 = pl.multiple_of(ci * ch, SUBLANES)
                    pv = pooled_ref[pl.ds(r0, ch), :]
                    dpm_v = dpm_ref[pl.ds(r0, ch), :]
                    mixed = jnp.dot(pv, w_ref[0], preferred_element_type=F32)
                    ds_ref[...] += jnp.sum(dpm_v * mixed, axis=0, keepdims=True)
                    dmixed = (dpm_v * s_ref[...]).astype(BF16)
                    dw_ref[0] += lax.dot_general(pv, dmixed, _DIMS["tn"], preferred_element_type=F32)
                    dpooled = lax.dot_general(dmixed, w_ref[0], _DIMS["nt"], preferred_element_type=F32)
                    raw[pl.ds(r0, ch), :] = dpooled
                    buf[pl.ds(pl.multiple_of(r0 + pad, SUBLANES), ch), :] = (
                        dpooled / _pool_counts(r0, ch, c, left, right, t))
                    return carry

                lax.fori_loop(0, t // ch, first, 0)

                def second(ci, carry):
                    r0 = pl.multiple_of(ci * ch, SUBLANES)
                    win = buf[pl.ds(r0, ch + 2 * pad), :]
                    s = _window_sum(win, range(-right, left + 1))[pad:pad + ch]
                    dp_ref[pl.ds(r0, ch), :] = (s - raw[pl.ds(r0, ch), :]).astype(BF16)
                    return carry

                lax.fori_loop(0, t // ch, second, 0)

    col = pl.BlockSpec((t, c), lambda g: (0, g))
    return pl.pallas_call(
        body, name="pool_bwd", grid=(ng,),
        in_specs=[col, col, pl.BlockSpec((1, c, c), lambda g: (g, 0, 0)), pl.BlockSpec((1, c), lambda g: (0, g))],
        out_specs=[col, pl.BlockSpec((1, c, c), lambda g: (g, 0, 0)), pl.BlockSpec((1, c), lambda g: (0, g))],
        out_shape=[_out(t, ng * c, BF16), jax.ShapeDtypeStruct((ng, c, c), F32), _out(1, ng * c, F32)],
        scratch_shapes=[pltpu.VMEM((t + 2 * pad, c), F32), pltpu.VMEM((t, c), F32)],
        compiler_params=_params(("parallel",)),
    )(pooled, dpm, pool_w_bf, pool_scale)


def _discretise(a_re, a_im, log_dt, b_re, b_im):
    dt = jnp.exp(log_dt)
    mag = jnp.exp(dt * a_re)
    ang = dt * a_im
    abr = mag * jnp.cos(ang)
    abi = mag * jnp.sin(ang)
    den = a_re * a_re + a_im * a_im
    nr = abr - 1.0
    qr = (nr * a_re + abi * a_im) / den
    qi = (abi * a_re - nr * a_im) / den
    return abr, abi, qr * b_re - qi * b_im, qr * b_im + qi * b_re


def _ssm_disc(cols):
    n, hh = cols[3].shape

    def body(ar, ai, ld, br, bi, o1, o2, o3, o4):
        res = _discretise(ar[...], ai[...], ld[...], br[...], bi[...])
        for o, r in zip((o1, o2, o3, o4), res):
            o[...] = r

    return pl.pallas_call(
        body, name="ssm_disc",
        out_shape=[_out(n, 1, F32), _out(n, 1, F32), _out(n, hh, F32), _out(n, hh, F32)],
    )(*cols)


def _ssm_disc_bwd(cols, cots):
    n, hh = cols[3].shape

    def body(ar, ai, ld, br, bi, c1, c2, c3, c4, o1, o2, o3, o4, o5):
        _, vjp = jax.vjp(_discretise, ar[...], ai[...], ld[...], br[...], bi[...])
        res = vjp((c1[...], c2[...], c3[...], c4[...]))
        for o, r in zip((o1, o2, o3, o4, o5), res):
            o[...] = r

    return pl.pallas_call(
        body, name="ssm_disc_bwd",
        out_shape=[_out(n, 1, F32)] * 3 + [_out(n, hh, F32)] * 2,
    )(*cols, *cots)


def _rowsum(name, a):
    r, _ = a.shape

    def body(a_ref, o_ref):
        o_ref[...] = jnp.sum(a_ref[...], axis=-1, keepdims=True)

    return pl.pallas_call(body, name=name, out_shape=_out(r, 1, F32))(a)


def _cmul(pr, pi, qr, qi):
    return pr * qr - pi * qi, pr * qi + pi * qr


def _cpow(pr, pi, n):
    rr, ri = None, None
    while n:
        if n & 1:
            rr, ri = (pr, pi) if rr is None else _cmul(rr, ri, pr, pi)
        n >>= 1
        if n:
            pr, pi = _cmul(pr, pi, pr, pi)
    return rr, ri


def _segment_carry(er, ei, pr, pi, reverse):
    row = lax.broadcasted_iota(jnp.int32, er.shape, 0)
    cr, ci = jnp.zeros_like(er), jnp.zeros_like(ei)
    for _ in range(SUBLANES - 1):
        tr = er + pr * cr - pi * ci
        ti = ei + pr * ci + pi * cr
        if reverse:
            keep, shift = row < SUBLANES - 1, SUBLANES - 1
        else:
            keep, shift = row >= 1, 1
        cr = jnp.where(keep, pltpu.roll(tr, shift, 0), 0.0)
        ci = jnp.where(keep, pltpu.roll(ti, shift, 0), 0.0)
    return cr, ci


def _ssm_fwd(name, sp, b_re, b_im, c_re, c_im, ar, ai, reverse):
    t, c = sp.shape
    s = ar.shape[1]
    w = _pick(s, 512)
    ch = _pick(t, 512, SUBLANES)
    n_ch, gpc, steps = t // ch, ch // SUBLANES, t // SUBLANES

    def body(sp_ref, bre_ref, bim_ref, cre_ref, cim_ref, ar_ref, ai_ref, xr_ref, xi_ref, y_ref, ur, ui, xbr, xbi):
        a_r = jnp.broadcast_to(ar_ref[...], (SUBLANES, w))
        a_i = jnp.broadcast_to(ai_ref[...], (SUBLANES, w))

        @pl.when(pl.program_id(0) == 0)
        def _():
            y_ref[...] = jnp.zeros_like(y_ref)

        def sweep(h0, store):
            def chunk(k, h):
                ci = n_ch - 1 - k if reverse else k
                rows = pl.ds(pl.multiple_of(ci * ch, ch), ch)
                spv = sp_ref[rows, :].astype(BF16)
                ur[...] = jnp.dot(spv, bre_ref[...], preferred_element_type=F32)
                ui[...] = jnp.dot(spv, bim_ref[...], preferred_element_type=F32)

                def group(g, hh):
                    gi = gpc - 1 - g if reverse else g
                    r0 = pl.multiple_of(gi * SUBLANES, SUBLANES)
                    hr, hi = hh
                    nr = a_r * hr - a_i * hi + ur[pl.ds(r0, SUBLANES), :]
                    ni = a_r * hi + a_i * hr + ui[pl.ds(r0, SUBLANES), :]
                    if store:
                        xbr[pl.ds(r0, SUBLANES), :] = nr
                        xbi[pl.ds(r0, SUBLANES), :] = ni
                    return nr, ni

                h = lax.fori_loop(0, gpc, group, h)
                if store:
                    xr16, xi16 = xbr[...].astype(BF16), xbi[...].astype(BF16)
                    xr_ref[rows, :] = xr16
                    xi_ref[rows, :] = xi16
                    y_ref[rows, :] += (jnp.dot(xr16, cre_ref[...], preferred_element_type=F32)
                                       + jnp.dot(xi16, cim_ref[...], preferred_element_type=F32))
                return h

            return lax.fori_loop(0, n_ch, chunk, h0)

        zero = jnp.zeros((SUBLANES, w), F32)
        er, ei = sweep((zero, zero), False)
        pr, pi = _cpow(ar_ref[...], ai_ref[...], steps)
        sweep(_segment_carry(er, ei, pr, pi, reverse), True)

    col = lambda i: (0, i)
    return pl.pallas_call(
        body, name=name, grid=(s // w,),
        in_specs=[pl.BlockSpec((t, c), lambda i: (0, 0)), pl.BlockSpec((c, w), col), pl.BlockSpec((c, w), col),
                  pl.BlockSpec((w, c), lambda i: (i, 0)), pl.BlockSpec((w, c), lambda i: (i, 0)),
                  pl.BlockSpec((1, w), col), pl.BlockSpec((1, w), col)],
        out_specs=[pl.BlockSpec((t, w), col), pl.BlockSpec((t, w), col), pl.BlockSpec((t, c), lambda i: (0, 0))],
        out_shape=[_out(t, s, BF16), _out(t, s, BF16), _out(t, c, F32)],
        scratch_shapes=[pltpu.VMEM((ch, w), F32)] * 4,
        compiler_params=_params(("arbitrary",)),
    )(sp, b_re, b_im, c_re, c_im, ar, ai)


def _ssm_bwd(name, dyp, c_re, c_im, xr, xi, ar, ai, reverse):
    t, c = dyp.shape
    s = ar.shape[1]
    w = _pick(s, 512)
    ch = _pick(t, 512, SUBLANES)
    n_ch, gpc, steps = t // ch, ch // SUBLANES, t // SUBLANES
    back = not reverse
    edge = 2 * SUBLANES

    def body(dy_ref, cre_ref, cim_ref, xr_ref, xi_ref, ar_ref, ai_ref, lr_ref, li_ref, dar_ref, dai_ref,
             gr, gi_, lbr, lbi, xbr, xbi):
        a_r = jnp.broadcast_to(ar_ref[...], (SUBLANES, w))
        a_i = -jnp.broadcast_to(ai_ref[...], (SUBLANES, w))
        row = lax.broadcasted_iota(jnp.int32, (SUBLANES, w), 0)

        def neighbours(ci, x_ref, buf):
            rows = pl.ds(pl.multiple_of(ci * ch, ch), ch)
            if reverse:
                buf[pl.ds(0, ch), :] = x_ref[rows, :].astype(F32)
                nxt = x_ref[pl.ds(pl.multiple_of(jnp.minimum(ci + 1, n_ch - 1) * ch, ch), edge), :].astype(F32)[:SUBLANES]
                first = x_ref[pl.ds(0, edge), :].astype(F32)[:SUBLANES]
                wrap = jnp.where(row < SUBLANES - 1, pltpu.roll(first, SUBLANES - 1, 0), 0.0)
                buf[pl.ds(ch, SUBLANES), :] = jnp.where(ci == n_ch - 1, wrap, nxt)
            else:
                buf[pl.ds(SUBLANES, ch), :] = x_ref[rows, :].astype(F32)
                prv = x_ref[pl.ds(pl.multiple_of(jnp.maximum(ci * ch - edge, 0), edge), edge), :].astype(F32)[SUBLANES:]
                last = x_ref[pl.ds(t - edge, edge), :].astype(F32)[SUBLANES:]
                wrap = jnp.where(row >= 1, pltpu.roll(last, 1, 0), 0.0)
                buf[pl.ds(0, SUBLANES), :] = jnp.where(ci == 0, wrap, prv)

        def sweep(h0, store):
            def chunk(k, carry):
                ci = n_ch - 1 - k if back else k
                rows = pl.ds(pl.multiple_of(ci * ch, ch), ch)
                dyv = dy_ref[rows, :].astype(BF16)
                gr[...] = lax.dot_general(dyv, cre_ref[...], _DIMS["nt"], preferred_element_type=F32)
                gi_[...] = lax.dot_general(dyv, cim_ref[...], _DIMS["nt"], preferred_element_type=F32)
                if store:
                    neighbours(ci, xr_ref, xbr)
                    neighbours(ci, xi_ref, xbi)

                def group(g, cc):
                    gidx = gpc - 1 - g if back else g
                    r0 = pl.multiple_of(gidx * SUBLANES, SUBLANES)
                    hr, hi = cc[0], cc[1]
                    nr = a_r * hr - a_i * hi + gr[pl.ds(r0, SUBLANES), :]
                    ni = a_r * hi + a_i * hr + gi_[pl.ds(r0, SUBLANES), :]
                    if not store:
                        return nr, ni
                    lbr[pl.ds(r0, SUBLANES), :] = nr
                    lbi[pl.ds(r0, SUBLANES), :] = ni
                    x0 = pl.multiple_of(r0 + SUBLANES, SUBLANES) if reverse else r0
                    xpr, xpi = xbr[pl.ds(x0, SUBLANES), :], xbi[pl.ds(x0, SUBLANES), :]
                    return nr, ni, cc[2] + nr * xpr + ni * xpi, cc[3] + ni * xpr - nr * xpi

                carry = lax.fori_loop(0, gpc, group, carry)
                if store:
                    lr_ref[rows, :] = lbr[...].astype(BF16)
                    li_ref[rows, :] = lbi[...].astype(BF16)
                return carry

            return lax.fori_loop(0, n_ch, chunk, h0)

        zero = jnp.zeros((SUBLANES, w), F32)
        er, ei = sweep((zero, zero), False)
        pr, pi = _cpow(ar_ref[...], -ai_ref[...], steps)
        cr, ci0 = _segment_carry(er, ei, pr, pi, back)
        _, _, dar, dai = sweep((cr, ci0, zero, zero), True)
        dar_ref[...] = jnp.sum(dar, axis=0, keepdims=True)
        dai_ref[...] = jnp.sum(dai, axis=0, keepdims=True)

    col = lambda i: (0, i)
    return pl.pallas_call(
        body, name=name, grid=(s // w,),
        in_specs=[pl.BlockSpec((t, c), lambda i: (0, 0)), pl.BlockSpec((w, c), lambda i: (i, 0)),
                  pl.BlockSpec((w, c), lambda i: (i, 0)), pl.BlockSpec((t, w), col), pl.BlockSpec((t, w), col),
                  pl.BlockSpec((1, w), col), pl.BlockSpec((1, w), col)],
        out_specs=[pl.BlockSpec((t, w), col), pl.BlockSpec((t, w), col), pl.BlockSpec((1, w), col), pl.BlockSpec((1, w), col)],
        out_shape=[_out(t, s, BF16), _out(t, s, BF16), _out(1, s, F32), _out(1, s, F32)],
        scratch_shapes=[pltpu.VMEM((ch, w), F32)] * 4 + [pltpu.VMEM((ch + SUBLANES, w), F32)] * 2,
        compiler_params=_params(("parallel",)),
    )(dyp, c_re, c_im, xr, xi, ar, ai)


def _to_segments(a):
    t, c = a.shape
    return a.reshape(SUBLANES, t // SUBLANES, c).transpose(1, 0, 2).reshape(t, c)


def _from_segments(a):
    t, c = a.shape
    return a.reshape(t // SUBLANES, SUBLANES, c).transpose(1, 0, 2).reshape(t, c)


def _colsum_prod(name, a, b, b_coff=0):
    t, n = a.shape
    tm = _pick(t, 512, SUBLANES)

    def body(a_ref, b_ref, o_ref):
        @pl.when(pl.program_id(0) == 0)
        def _():
            o_ref[...] = jnp.zeros_like(o_ref)

        o_ref[...] += jnp.sum(a_ref[...].astype(F32) * b_ref[...].astype(F32), axis=0, keepdims=True)

    return pl.pallas_call(
        body, name=name, grid=(t // tm,),
        in_specs=[pl.BlockSpec((tm, n), lambda i: (i, 0)), pl.BlockSpec((tm, n), lambda i: (i, b_coff))],
        out_specs=pl.BlockSpec((1, n), lambda i: (0, 0)), out_shape=_out(1, n, F32),
        compiler_params=_params(("arbitrary",)),
    )(a, b)


def _bd_in(bb, g, p, hh):
    blk = bb.reshape(g, p, hh).transpose(0, 2, 1)
    eye = jnp.eye(g, dtype=bool)[:, None, :, None]
    return jnp.where(eye, blk[:, :, None, :], 0.0).reshape(g * hh, g * p)


def _bd_out(cc, g, p, hh):
    blk = cc.transpose(0, 2, 1)
    eye = jnp.eye(g, dtype=bool)[:, None, :, None]
    return jnp.where(eye, blk[:, :, None, :], 0.0).reshape(g * p, g * hh)


def _diag_in(dmat, g, p, hh):
    eye = jnp.eye(g, dtype=bool)[:, None, :, None]
    diag = jnp.sum(jnp.where(eye, dmat.reshape(g, hh, g, p), 0.0), axis=2)
    return diag.transpose(0, 2, 1).reshape(g * p, hh)


def _diag_out(dmat, g, p, hh):
    eye = jnp.eye(g, dtype=bool)[:, None, :, None]
    diag = jnp.sum(jnp.where(eye, dmat.reshape(g, p, g, hh), 0.0), axis=2)
    return diag.transpose(0, 2, 1)


def _softmax(qh, kh, scale):
    s = lax.dot_general(qh, kh, _DIMS["nt"], preferred_element_type=F32) * scale
    e = jnp.exp(s - jnp.max(s, axis=-1, keepdims=True))
    return e / jnp.sum(e, axis=-1, keepdims=True)


def _attn_fwd(q, kv):
    t, d = q.shape
    mm_ = kv.shape[0]
    hd = d // N_XHEADS
    scale = 1.0 / math.sqrt(hd)
    tm = _pick(t, 512, SUBLANES)

    def body(q_ref, kv_ref, o_ref):
        for h in range(N_XHEADS):
            sl = pl.ds(h * hd, hd)
            p = _softmax(q_ref[:, sl], kv_ref[:, sl], scale)
            o_ref[:, sl] = jnp.dot(p.astype(BF16), kv_ref[:, pl.ds(d + h * hd, hd)],
                                   preferred_element_type=F32).astype(BF16)

    return pl.pallas_call(
        body, name="attn_fwd", grid=(t // tm,),
        in_specs=[pl.BlockSpec((tm, d), lambda i: (i, 0)), pl.BlockSpec((mm_, 2 * d), lambda i: (0, 0))],
        out_specs=pl.BlockSpec((tm, d), lambda i: (i, 0)), out_shape=_out(t, d, BF16),
        compiler_params=_params(("parallel",)),
    )(q, kv)


def _attn_bwd(q, kv, do):
    t, d = q.shape
    mm_ = kv.shape[0]
    hd = d // N_XHEADS
    scale = 1.0 / math.sqrt(hd)
    tm = _pick(t, 512, SUBLANES)

    def body(q_ref, kv_ref, do_ref, dq_ref, dkv_ref):
        @pl.when(pl.program_id(0) == 0)
        def _():
            dkv_ref[...] = jnp.zeros_like(dkv_ref)

        for h in range(N_XHEADS):
            sl = pl.ds(h * hd, hd)
            vsl = pl.ds(d + h * hd, hd)
            qh, kh, doh = q_ref[:, sl], kv_ref[:, sl], do_ref[:, sl]
            p = _softmax(qh, kh, scale)
            dp = lax.dot_general(doh, kv_ref[:, vsl], _DIMS["nt"], preferred_element_type=F32)
            dkv_ref[:, vsl] += lax.dot_general(p.astype(BF16), doh, _DIMS["tn"], preferred_element_type=F32)
            ds = (p * (dp - jnp.sum(dp * p, axis=-1, keepdims=True)) * scale).astype(BF16)
            dq_ref[:, sl] = jnp.dot(ds, kh, preferred_element_type=F32).astype(BF16)
            dkv_ref[:, sl] += lax.dot_general(ds, qh, _DIMS["tn"], preferred_element_type=F32)

    row = pl.BlockSpec((tm, d), lambda i: (i, 0))
    full = pl.BlockSpec((mm_, 2 * d), lambda i: (0, 0))
    return pl.pallas_call(
        body, name="attn_bwd", grid=(t // tm,), in_specs=[row, full, row], out_specs=[row, full],
        out_shape=[_out(t, d, BF16), _out(mm_, 2 * d, F32)], compiler_params=_params(("arbitrary",)),
    )(q, kv, do)


def _ew(name, fn, ins, outs, rows_pref=256, rowvecs=()):
    r, c = ins[0].shape
    tr = _pick(r, rows_pref, SUBLANES)
    ni = len(ins) + len(rowvecs)

    def body(*refs):
        res = fn(*[x[...] for x in refs[:ni]])
        for o_ref, v in zip(refs[ni:], res):
            o_ref[...] = v.astype(o_ref.dtype)

    blk = pl.BlockSpec((tr, c), lambda i: (i, 0))
    vec = pl.BlockSpec((1, c), lambda i: (0, 0))
    return pl.pallas_call(
        body, name=name, grid=(r // tr,), in_specs=[blk] * len(ins) + [vec] * len(rowvecs), out_specs=[blk] * len(outs),
        out_shape=[_out(r, c, dt) for dt in outs], compiler_params=_params(("parallel",)),
    )(*ins, *rowvecs)


def _sum_slots(name, a, dtype):
    s, r, c = a.shape
    tr = _pick(r, 256, SUBLANES)

    def body(a_ref, o_ref):
        acc = a_ref[0].astype(F32)
        for k in range(1, s):
            acc = acc + a_ref[k].astype(F32)
        o_ref[...] = acc.astype(o_ref.dtype)

    return pl.pallas_call(
        body, name=name, grid=(r // tr,), in_specs=[pl.BlockSpec((s, tr, c), lambda i: (0, i, 0))],
        out_specs=pl.BlockSpec((tr, c), lambda i: (i, 0)), out_shape=_out(r, c, dtype),
        compiler_params=_params(("parallel",)),
    )(a)


def _adamw_step(wv, gv, mv, vv):
    bc1 = 1.0 - ADAM_B1 ** ADAM_STEP
    bc2 = 1.0 - ADAM_B2 ** ADAM_STEP
    m2 = ADAM_B1 * mv + (1.0 - ADAM_B1) * gv
    v2 = ADAM_B2 * vv + (1.0 - ADAM_B2) * (gv * gv)
    delta = -ADAM_LR * ((m2 / bc1) / (jnp.sqrt(v2 / bc2) + ADAM_EPS) + ADAM_WD * wv)
    return delta, m2, v2


def _adamw_group(name, items, transposed):
    k, r = items[0][0].shape
    tk = _pick(k, 256, SUBLANES)
    n_out = 4 if transposed else 3

    def body(*refs):
        ins, outs = refs[:4 * len(items)], refs[4 * len(items):]
        for i in range(len(items)):
            w_ref, g_ref, m_ref, v_ref = ins[4 * i:4 * i + 4]
            gv = g_ref[...].T if transposed else g_ref[...]
            res = _adamw_step(w_ref[...], gv, m_ref[...], v_ref[...]) + ((gv,) if transposed else ())
            for o_ref, val in zip(outs[n_out * i:n_out * (i + 1)], res):
                o_ref[...] = val

    blk = pl.BlockSpec((tk, r), lambda j: (j, 0))
    g_blk = pl.BlockSpec((r, tk), lambda j: (0, j)) if transposed else blk
    res = pl.pallas_call(
        body, name=name, grid=(k // tk,), in_specs=[blk, g_blk, blk, blk] * len(items),
        out_specs=[blk] * (n_out * len(items)), out_shape=[pltpu.HBM((k, r), F32)] * (n_out * len(items)),
        compiler_params=_params(("parallel",)),
    )(*[pltpu.with_memory_space_constraint(a, pltpu.HBM) for item in items for a in item])
    return [res[n_out * i:n_out * (i + 1)] for i in range(len(items))]


def _allgather(name, arrs):
    n = len(arrs)

    def body(*refs):
        ins, outs = refs[:n], refs[n:2 * n]
        send_sems, recv_sems, local_sems = refs[2 * n:]
        x, y, c = lax.axis_index("x"), lax.axis_index("y"), lax.axis_index("c")
        me, sibling = (x, y, c), (x, y, 1 - c)
        chips = [(1 - x, y), (x, 1 - y), (1 - x, 1 - y)]

        def rows(a, px, py, pc):
            r = ins[a].shape[0]
            return outs[a].at[pl.ds((4 * px + 2 * py + pc) * r, r), :]

        def copy(a, k, block, to, src=None):
            return pltpu.make_async_remote_copy(
                src_ref=rows(a, *block) if src is None else src, dst_ref=rows(a, *block),
                send_sem=send_sems.at[a, k], recv_sem=recv_sems.at[a, k], device_id=to, device_id_type=MESH)

        mine = [pltpu.make_async_copy(ins[a], rows(a, *me), local_sems.at[a]) for a in range(n)]
        for cp in mine:
            cp.start()
        first = []
        for a in range(n):
            first.append(copy(a, 0, me, sibling, src=ins[a]))
            first += [copy(a, 1 + j, me, (*chip, c), src=ins[a]) for j, chip in enumerate(chips)]
        for cp in first:
            cp.start()
        passed = []
        for j, chip in enumerate(chips):
            for a in range(n):
                copy(a, 1 + j, (*chip, c), me).wait_recv()
                cp = copy(a, 4 + j, (*chip, c), sibling)
                cp.start()
                passed.append(cp)
        for a in range(n):
            copy(a, 0, sibling, me).wait_recv()
            for j, chip in enumerate(chips):
                copy(a, 4 + j, (*chip, 1 - c), me).wait_recv()
        for cp in first + passed:
            cp.wait_send()
        for cp in mine:
            cp.wait()

    return pl.pallas_call(
        body, name=name, in_specs=[ANY] * n, out_specs=[ANY] * n,
        out_shape=[_out(N_DEV * a.shape[0], a.shape[1], a.dtype) for a in arrs],
        scratch_shapes=[pltpu.SemaphoreType.DMA((n, 7)), pltpu.SemaphoreType.DMA((n, 7)), pltpu.SemaphoreType.DMA((n,))],
    )(*arrs)


def _exchange_cores(name, blocks):
    n = len(blocks)
    c = blocks[0].shape[2]
    r = sum(b.shape[1] for b in blocks)

    def body(*refs):
        srcs, (recv_ref, send_sems, recv_sems) = refs[:n], refs[n:]
        x, y, cc = lax.axis_index("x"), lax.axis_index("y"), lax.axis_index("c")
        copies, off = [], 0
        for a, src in enumerate(srcs):
            rows = pl.ds(off, src.shape[1])
            off += src.shape[1]
            for q in range(4):
                copies.append(pltpu.make_async_remote_copy(
                    src_ref=src.at[2 * q + (1 - cc)], dst_ref=recv_ref.at[q, rows], send_sem=send_sems.at[a, q],
                    recv_sem=recv_sems.at[a, q], device_id=(x, y, 1 - cc), device_id_type=MESH))
        for cp in copies:
            cp.start()
        for cp in copies:
            cp.wait()

    return pl.pallas_call(
        body, name=name, in_specs=[ANY] * n, out_specs=ANY,
        out_shape=jax.ShapeDtypeStruct((4, r, c), blocks[0].dtype),
        scratch_shapes=[pltpu.SemaphoreType.DMA((n, 4))] * 2,
    )(*blocks)


def _peer(k, x, y, c):
    return (1 - x if k & 4 else x, 1 - y if k & 2 else y, 1 - c if k & 1 else c)


def _split_start(name, groups, after=None):
    pins = [] if after is None else [after]
    bufs, sem_shapes, spans = [], [], []
    for srcs, land_shapes, n_remote, n_local, _ in groups:
        sems = [pltpu.SemaphoreType.DMA((n_remote,)), pltpu.SemaphoreType.DMA((n_remote,))]
        sems += [pltpu.SemaphoreType.DMA((n_local,))] if n_local else []
        spans.append((len(bufs), len(srcs), len(land_shapes), len(sem_shapes), len(sems)))
        bufs += [pltpu.with_memory_space_constraint(a, pltpu.HBM) for a in srcs]
        bufs += [pltpu.with_memory_space_constraint(lax.empty(s.shape, s.dtype), pltpu.HBM) for s in land_shapes]
        sem_shapes += sems
    n_buf, n_sem = len(bufs), len(sem_shapes)

    def body(*refs):
        buf_refs, sem_refs, token = refs[:n_buf], refs[n_buf + len(pins):n_buf + len(pins) + n_sem], refs[-1]
        for (b0, ns, nl, s0, k), group in zip(spans, groups):
            remote, local = group[4](buf_refs[b0:b0 + ns], buf_refs[b0 + ns:b0 + ns + nl], *sem_refs[s0:s0 + k])
            for cp in local + remote:
                cp.start()
        token[...] = jnp.zeros_like(token)

    outs = pl.pallas_call(
        body, name=name,
        out_shape=sem_shapes + [pltpu.HBM(b.shape, b.dtype) for b in bufs] + [jax.ShapeDtypeStruct((SUBLANES, LANES), F32)],
        in_specs=[HBM] * n_buf + [ANY] * len(pins),
        out_specs=[SEM] * n_sem + [HBM] * n_buf + [pl.BlockSpec(memory_space=pltpu.VMEM)],
        input_output_aliases={i: n_sem + i for i in range(n_buf)},
        compiler_params=pltpu.CompilerParams(has_side_effects=SIDE_EFFECT),
    )(*bufs, *pins)
    return [dict(sems=list(outs[s0:s0 + k]), bufs=list(outs[n_sem + b0:n_sem + b0 + ns + nl]), token=outs[-1],
                 build=group[4], ns=ns) for (b0, ns, nl, s0, k), group in zip(spans, groups)]


def _split_wait(name, started, after):
    ns, n_buf, n_sem = started["ns"], len(started["bufs"]), len(started["sems"])

    def body(*refs):
        src_refs, land_refs = refs[:ns], refs[ns:n_buf]
        sems = refs[n_buf:n_buf + n_sem]
        remote, local = started["build"](src_refs, land_refs, *sems)
        for cp in local:
            cp.wait()
        for cp in remote:
            cp.wait_send()
            cp.wait_recv()

    outs = pl.pallas_call(
        body, name=name, out_shape=[pltpu.HBM(b.shape, b.dtype) for b in started["bufs"]],
        in_specs=[HBM] * n_buf + [SEM] * n_sem + [ANY], out_specs=[HBM] * n_buf,
        input_output_aliases={i: i for i in range(n_buf)},
        compiler_params=pltpu.CompilerParams(has_side_effects=SIDE_EFFECT),
    )(*started["bufs"], *started["sems"], after)
    return list(outs[:ns]), list(outs[ns:])


def _gather_group(shards):
    m = len(shards)

    def build(src_refs, land_refs, send_sems, recv_sems, local_sems):
        x, y, c = lax.axis_index("x"), lax.axis_index("y"), lax.axis_index("c")
        remote, local = [], []
        for j in range(m):
            r = src_refs[j].shape[0]
            dst = land_refs[j].at[pl.ds((4 * x + 2 * y + c) * r, r), :]
            local.append(pltpu.make_async_copy(src_refs[j], dst, local_sems.at[j]))
            for k in range(1, N_DEV):
                remote.append(pltpu.make_async_remote_copy(
                    src_ref=src_refs[j], dst_ref=dst, send_sem=send_sems.at[7 * j + k - 1],
                    recv_sem=recv_sems.at[7 * j + k - 1], device_id=_peer(k, x, y, c), device_id_type=MESH))
        return remote, local

    lands = [jax.ShapeDtypeStruct((N_DEV * a.shape[0], a.shape[1]), a.dtype) for a in shards]
    return shards, lands, 7 * m, m, build


def _slots_start(name, a):
    def build(src_refs, land_refs, send_sems, recv_sems, local_sems):
        x, y, c = lax.axis_index("x"), lax.axis_index("y"), lax.axis_index("c")
        dst = land_refs[0].at[4 * x + 2 * y + c]
        local = [pltpu.make_async_copy(src_refs[0], dst, local_sems.at[0])]
        remote = [pltpu.make_async_remote_copy(
            src_ref=src_refs[0], dst_ref=dst, send_sem=send_sems.at[k - 1], recv_sem=recv_sems.at[k - 1],
            device_id=_peer(k, x, y, c), device_id_type=MESH) for k in range(1, N_DEV)]
        return remote, local

    return _split_start(name, [([a], [jax.ShapeDtypeStruct((N_DEV,) + a.shape, a.dtype)], 7, 1, build)])[0]


def _chips_start(name, p):
    _, r, c = p.shape
    nck = r // GRAD_ROW_TILE

    def build(src_refs, land_refs, send_sems, recv_sems):
        x, y, cc = lax.axis_index("x"), lax.axis_index("y"), lax.axis_index("c")
        remote = []
        for k in range(1, 4):
            px = 1 - x if k >> 1 else x
            py = 1 - y if k & 1 else y
            for j in range(nck):
                rows = pl.ds(j * GRAD_ROW_TILE, GRAD_ROW_TILE)
                remote.append(pltpu.make_async_remote_copy(
                    src_ref=src_refs[0].at[2 * px + py, rows], dst_ref=land_refs[0].at[k - 1, rows],
                    send_sem=send_sems.at[(k - 1) * nck + j], recv_sem=recv_sems.at[(k - 1) * nck + j],
                    device_id=(px, py, cc), device_id_type=MESH))
        return remote, []

    return _split_start(name, [([p], [jax.ShapeDtypeStruct((3, r, c), p.dtype)], 3 * nck, 0, build)])[0]


def _chip_sum(name, p, recv, chip):
    _, r, c = p.shape
    tr = _pick(r, 5 * GRAD_ROW_TILE, GRAD_ROW_TILE)

    def body(chip_ref, p_ref, r_ref, o_ref):
        acc = p_ref[...].astype(F32)
        for k in range(3):
            acc = acc + r_ref[k].astype(F32)
        o_ref[...] = acc

    return pl.pallas_call(
        body, name=name,
        grid_spec=pltpu.PrefetchScalarGridSpec(
            num_scalar_prefetch=1, grid=(r // tr,),
            in_specs=[pl.BlockSpec((None, tr, c), lambda i, chip_ref: (chip_ref[0], i, 0)),
                      pl.BlockSpec((3, tr, c), lambda i, chip_ref: (0, i, 0))],
            out_specs=pl.BlockSpec((tr, c), lambda i, chip_ref: (i, 0))),
        out_shape=_out(r, c, F32), compiler_params=_params(("parallel",)),
    )(chip, p, recv)


def _local_step(x, mem, tgt, wt, sm, ev=None):
    t, d = x.shape
    n_mem = mem.shape[0]
    d_pool = sm["pool_scale"].shape[1]
    ng, pc = sm["pool_w"].shape[0], sm["pool_w"].shape[1]
    d_ssm = sm["ssm_d"].shape[1]
    _, sg, sp, sh = sm["ssm_b_re"].shape
    n_state = sg * sp
    gb, gs = {}, {}

    def emit(name, **kw):
        return ev(name, **kw) if ev is not None else None

    n1 = _rms_fwd("ffn1_norm", x, sm["ffn1_norm"])
    emit("ffn1_norm_done", marker=n1)
    def ffn1_down(hid):
        emit("ffn1_up_done", marker=hid)
        return wt["ffn1_w_down"]

    h1, ffn1_saved = _ffn_fwd("ffn1", x, n1, wt["ffn1_w_gate"], wt["ffn1_w_up"], ffn1_down)
    emit("ffn1_fwd_done", marker=h1)
    u = _rms_fwd("mix_norm", h1, sm["mix_norm"])
    d_in = wt["w_in"].shape[0]
    tm, tn = _pick(t, 1024), _pick(d_in, 1408)
    proj = _mm1("in_proj", "nt", u, wt["w_in"], t, d_in, tm, tn, F32)
    off_s = d_pool // d_ssm
    off_gp = (d_pool + d_ssm)
    off_gs = off_gp + d

    pool_w_bf = sm["pool_w"].astype(BF16)
    pooled, pm = _pool_fwd(proj, pool_w_bf, sm["pool_scale"])

    cols = [sm["ssm_a_re"].reshape(-1, 1), sm["ssm_a_im"].reshape(-1, 1),
            jnp.broadcast_to(sm["ssm_log_dt"][:, :, None], (2, sg, sp)).reshape(-1, 1),
            sm["ssm_b_re"].reshape(-1, sh), sm["ssm_b_im"].reshape(-1, sh)]
    abr, abi, bbr, bbi = _ssm_disc(cols)
    abr2, abi2 = abr.reshape(2, n_state), abi.reshape(2, n_state)
    bbr4, bbi4 = bbr.reshape(2, sg * sp, sh), bbi.reshape(2, sg * sp, sh)
    b_re = [_bd_in(bbr4[dr], sg, sp, sh).astype(BF16) for dr in range(2)]
    b_im = [_bd_in(bbi4[dr], sg, sp, sh).astype(BF16) for dr in range(2)]
    c_re = [_bd_out(sm["ssm_c_re"][dr], sg, sp, sh).astype(BF16) for dr in range(2)]
    c_im = [_bd_out(-sm["ssm_c_im"][dr], sg, sp, sh).astype(BF16) for dr in range(2)]
    sp32 = _to_segments(proj[:, d_pool:d_pool + d_ssm])
    xs, y_parts = [], []
    for dr in range(2):
        xr, xi, y_part = _ssm_fwd(f"ssm_fwd{dr}", sp32, b_re[dr], b_im[dr], c_re[dr], c_im[dr], abr2[dr:dr + 1],
                                  abi2[dr:dr + 1], reverse=(dr == 1))
        xs.append((xr, xi))
        y_parts.append(y_part)
    y = _from_segments(_ew("ssm_sum", lambda p0, p1, sv, dv: (p0 + p1 + sv * dv,), y_parts + [sp32], [F32],
                           rowvecs=[sm["ssm_d"]])[0])
    tmy = _pick(t, 256)
    ys = _ew("ssm_gelu", lambda v: (jax.nn.gelu(v),), [y], [BF16])[0]
    emit("mix_in_done", marker=ys)

    tmm, tnm, tnx = _pick(t, 1024), _pick(d, 256), _pick(d, 512)
    gp_spec = _tile(tmm, tnm, off_gp // tnm)
    gs_spec = _tile(tmm, tnm, off_gs // tnm)

    def merge_epi(accs, gpv, gsv):
        z_pool, val, gate = accs
        return (jax.nn.sigmoid(gpv) * z_pool + jax.nn.sigmoid(gsv) * (val * jax.nn.sigmoid(gate)),)

    merged = _mm("mix_merge", "nt", [pm, ys], [wt["w_pool_proj"], wt["w_glu_val"], wt["w_glu_gate"]],
                 [[(0, 0)], [(1, 1)], [(1, 2)]], t, d, tmm, tnm, [(proj, gp_spec), (proj, gs_spec)], merge_epi,
                 [(_out(t, d, BF16), None)])[0]
    res_epi = lambda accs, hin: (hin + accs[0],)
    h2 = _mm("mix_out", "nn", [merged], [wt["w_mix_out"]], [[(0, 0)]], t, d, tmm, tnx, [(h1, _tile(tmm, tnx))],
             res_epi, [(_out(t, d, F32), None)])[0]

    un = _rms_fwd("xattn_norm", h2, sm["xattn_norm"])
    mn = _rms_fwd("mem_norm", mem, sm["mem_norm"])
    emit("mix_done", marker=un)
    q = _mm1("xattn_q", "nn", un, wt["w_q"], t, d, tmm, tnx, BF16)
    kv = _mm1("xattn_kv", "nt", mn, wt["w_kv"], n_mem, 2 * d, n_mem, _pick(2 * d, 512), BF16)
    o = _attn_fwd(q, kv)
    h3 = _mm("xattn_out", "nn", [o], [wt["w_xo"]], [[(0, 0)]], t, d, tmm, tnx, [(h2, _tile(tmm, tnx))],
             res_epi, [(_out(t, d, F32), None)])[0]

    n2 = _rms_fwd("ffn2_norm", h3, sm["ffn2_norm"])
    emit("xattn_done", marker=n2)
    h4, ffn2_saved = _ffn_fwd("ffn2", h3, n2, wt["ffn2_w_gate"], wt["ffn2_w_up"], wt["ffn2_w_down"])

    dh4, dh4_bf, gs["final_norm"], loss = _loss_head(h4, sm["final_norm"], tgt)
    dh3, dh3_bf, gs["ffn2_norm"], gb["ffn2_w_gate"], gb["ffn2_w_up"], gb["ffn2_w_down"] = _ffn_bwd(
        "ffn2", h3, sm["ffn2_norm"], wt["ffn2_w_gate"], wt["ffn2_w_up"], wt["ffn2_w_down"], ffn2_saved, dh4, dh4_bf)

    tw = _pick(d, 1024)
    do = _mm1("xattn_do", "nt", dh3_bf, wt["w_xo"], t, d, tmm, tnx, BF16)
    gb["w_xo"] = _mm1("xattn_dwxo", "tn", o, dh3_bf, d, d, tw, tnx, BF16)
    dq, dkv = _attn_bwd(q, kv, do)
    gb["w_q"] = _mm1("xattn_dwq", "tn", un, dq, d, d, tw, tnx, BF16)
    dun = _mm1("xattn_dun", "nt", dq, wt["w_q"], t, d, tmm, tnx, F32)
    dh2, dh2_bf, gs["xattn_norm"] = _rms_bwd("xattn_norm_bwd", h2, sm["xattn_norm"], dun, dh3)
    gb["w_kv"] = _mm1("xattn_dwkv", "tn", dkv, mn, 2 * d, d, _pick(2 * d, 512), d, BF16)
    dmn = _mm1("xattn_dmn", "nn", dkv, wt["w_kv"], n_mem, d, n_mem, tnx, F32)
    gs["mem_norm"] = _rms_bwd("mem_norm_bwd", mem, sm["mem_norm"], dmn)

    gb["w_mix_out"] = _mm1("mix_dwout", "tn", merged, dh2_bf, d, d, tw, tnx, BF16)

    def merge_bwd_epi(accs, gpv, gsv):
        dmerged, z_pool, val, gate = accs
        sp_, ss_, sg_ = jax.nn.sigmoid(gpv), jax.nn.sigmoid(gsv), jax.nn.sigmoid(gate)
        glu = val * sg_
        dz_pool = dmerged * sp_
        dg_pool = dmerged * z_pool * (sp_ * (1.0 - sp_))
        dz_ssm = dmerged * ss_
        dg_ssm = dmerged * glu * (ss_ * (1.0 - ss_))
        dval = dz_ssm * sg_
        dgate = dz_ssm * glu * (1.0 - sg_)
        return dz_pool, dg_pool, dg_ssm, dval, dgate

    dz_pool, dg_pool, dg_ssm, dval, dgate = _mm(
        "mix_merge_bwd", "nt", [dh2_bf, pm, ys], [wt["w_mix_out"], wt["w_pool_proj"], wt["w_glu_val"], wt["w_glu_gate"]],
        [[(0, 0)], [(1, 1)], [(2, 2)], [(2, 3)]], t, d, tmm, tnm, [(proj, gp_spec), (proj, gs_spec)], merge_bwd_epi,
        [(_out(t, d, BF16), None)] * 5)
    gb["w_pool_proj"] = _mm1("pool_dwproj", "tn", dz_pool, pm, d, d_pool, tw, d_pool, BF16)
    gb["w_glu_val"] = _mm1("glu_dwval", "tn", dval, ys, d, d_ssm, tw, d_ssm, BF16)
    gb["w_glu_gate"] = _mm1("glu_dwgate", "tn", dgate, ys, d, d_ssm, tw, d_ssm, BF16)

    def gelu_bwd_epi(accs, yv):
        _, vjp = jax.vjp(jax.nn.gelu, yv)
        return (vjp(accs[0])[0],)

    dy = _mm("glu_dy", "nn", [dval, dgate], [wt["w_glu_val"], wt["w_glu_gate"]], [[(0, 0), (1, 1)]], t, d_ssm, tmy, d_ssm,
             [(y, _tile(tmy, d_ssm))], gelu_bwd_epi, [(_out(t, d_ssm, F32), None)])[0]
    gs["ssm_d"] = _colsum_prod("ssm_dd", dy, proj, b_coff=off_s)
    dyp = _to_segments(dy)
    d_abr, d_abi, d_bbr, d_bbi, d_cre, d_cim, lams = [], [], [], [], [], [], []
    ts = _pick(n_state, 512)
    tc_ = _pick(n_state, 256)
    both = lambda accs: tuple(accs)
    for dr in range(2):
        lr, li, dar, dai = _ssm_bwd(f"ssm_bwd{dr}", dyp, c_re[dr], c_im[dr], xs[dr][0], xs[dr][1], abr2[dr:dr + 1],
                                    abi2[dr:dr + 1], reverse=(dr == 1))
        d_abr.append(dar)
        d_abi.append(dai)
        lams += [lr, li]
        d_br, d_bi = _mm(f"ssm_db{dr}", "tn", [sp32], [lr, li], [[(0, 0)], [(0, 1)]], d_ssm, n_state, d_ssm, ts, [], both,
                         [(_out(d_ssm, n_state, F32), None)] * 2)
        d_bbr.append(_diag_in(d_br, sg, sp, sh))
        d_bbi.append(_diag_in(d_bi, sg, sp, sh))
        d_cr, d_ci = _mm(f"ssm_dc{dr}", "tn", [xs[dr][0], xs[dr][1]], [dyp], [[(0, 0)], [(1, 0)]], n_state, d_ssm, tc_,
                         d_ssm, [], both, [(_out(n_state, d_ssm, F32), None)] * 2)
        d_cre.append(_diag_out(d_cr, sg, sp, sh))
        d_cim.append(-_diag_out(d_ci, sg, sp, sh))
    ds = _from_segments(_mm(
        "ssm_ds", "nt", lams, [b_re[0], b_im[0], b_re[1], b_im[1]], [[(k, k) for k in range(4)]], t, d_ssm, tmy,
        d_ssm, [(dyp, _tile(tmy, d_ssm)), (sm["ssm_d"], _rowvec(d_ssm))],
        lambda accs, dyv, dv: (dyv * dv + accs[0],), [(_out(t, d_ssm, BF16), None)])[0])
    cots = [jnp.concatenate(d_abr, axis=0).reshape(-1, 1), jnp.concatenate(d_abi, axis=0).reshape(-1, 1),
            jnp.concatenate(d_bbr, axis=0), jnp.concatenate(d_bbi, axis=0)]
    d_are, d_aim, d_ldt, d_bre, d_bim = _ssm_disc_bwd(cols, cots)
    gs["ssm_a_re"] = d_are.reshape(2, sg, sp)
    gs["ssm_a_im"] = d_aim.reshape(2, sg, sp)
    gs["ssm_log_dt"] = _rowsum("ssm_dlogdt", d_ldt.reshape(2 * sg, sp)).reshape(2, sg)
    gs["ssm_b_re"] = d_bre.reshape(2, sg, sp, sh)
    gs["ssm_b_im"] = d_bim.reshape(2, sg, sp, sh)
    gs["ssm_c_re"] = jnp.stack(d_cre, axis=0)
    gs["ssm_c_im"] = jnp.stack(d_cim, axis=0)

    dpm = _mm1("pool_dpm", "nn", dz_pool, wt["w_pool_proj"], t, d_pool, tmm, _pick(d_pool, 256), F32)
    dp, gs["pool_w"], gs["pool_scale"] = _pool_bwd(pooled, dpm, pool_w_bf, sm["pool_scale"])

    w_in = wt["w_in"]
    parts = [(dp, 0, d_pool), (ds, d_pool, d_ssm), (dg_pool, off_gp, d), (dg_ssm, off_gs, d)]
    w_in_parts = [w_in[o0:o0 + width] for _, o0, width in parts]
    gb["w_in"] = jnp.concatenate(
        [_mm1(f"in_proj_dw{k}", "tn", p_[0], u, p_[2], d, _pick(p_[2], 1024), tnx, BF16) for k, p_ in enumerate(parts)], axis=0)
    pin = emit("grads_main", gb=gb)
    du = _mm("in_proj_du", "nn", [p_[0] for p_ in parts], w_in_parts, [[(k, k) for k in range(4)]], t, d, tmm, tnx, [],
             lambda accs: (accs[0],), [(_out(t, d, F32), None)], after=pin)[0]
    dh1, dh1_bf, gs["mix_norm"] = _rms_bwd("mix_norm_bwd", h1, sm["mix_norm"], du, dh2)
    pin = emit("small_early", gs=gs, loss=loss)

    def ffn1_weights_done(d_wg, d_wu, d_wd):
        gb["ffn1_w_gate"], gb["ffn1_w_up"], gb["ffn1_w_down"] = d_wg, d_wu, d_wd
        return emit("grads_ffn1", gb=gb)

    dx, _, gs["ffn1_norm"], _, _, _ = _ffn_bwd(
        "ffn1", x, sm["ffn1_norm"], wt["ffn1_w_gate"], wt["ffn1_w_up"], wt["ffn1_w_down"], ffn1_saved, dh1, dh1_bf,
        weights_done=ffn1_weights_done, after=pin)
    return loss, dx, gb, gs


WEIGHTS = ["ffn1_norm", "ffn1_w_gate", "ffn1_w_up", "ffn1_w_down", "mix_norm", "w_in", "pool_w", "pool_scale",
           "w_pool_proj", "ssm_a_re", "ssm_a_im", "ssm_log_dt", "ssm_b_re", "ssm_b_im", "ssm_c_re", "ssm_c_im", "ssm_d",
           "w_glu_val", "w_glu_gate", "w_mix_out", "xattn_norm", "mem_norm", "w_q", "w_kv", "w_xo", "ffn2_norm",
           "ffn2_w_gate", "ffn2_w_up", "ffn2_w_down", "final_norm"]
COL_SHARDED = ["ffn1_w_gate", "ffn1_w_up", "w_in", "w_pool_proj", "w_glu_val", "w_glu_gate", "w_kv", "ffn2_w_gate",
               "ffn2_w_up"]
ROW_SHARDED = ["ffn1_w_down", "w_mix_out", "w_q", "w_xo", "ffn2_w_down"]
BIG = [n for n in WEIGHTS if n in COL_SHARDED or n in ROW_SHARDED]
SMALL = [n for n in WEIGHTS if n not in BIG]
FFN1_BIG = ["ffn1_w_gate", "ffn1_w_up", "ffn1_w_down"]
MAIN_BIG = [n for n in BIG if n not in FFN1_BIG]
GATHER_PLAN = [("ffn1_up_done", ["ffn1_w_down"]), ("ffn1_fwd_done", ["w_in"]),
               ("mix_in_done", ["w_pool_proj", "w_glu_val", "w_glu_gate", "w_mix_out"]),
               ("mix_done", ["w_q", "w_kv", "w_xo"]), ("xattn_done", ["ffn2_w_gate", "ffn2_w_up", "ffn2_w_down"])]
LATE_SMALL = "ffn1_norm"
EARLY_SMALL = [n for n in SMALL if n != LATE_SMALL]
PACK_ROWS = SUBLANES * LANES
GRAD_ROW_TILE = 256


def _to_rows(name, w, width):
    if name in COL_SHARDED:
        w = w.T
    return w.reshape(-1, width)


def _pack_small(vals):
    flat = []
    for v in vals:
        f = v.reshape(-1)
        flat.append(jnp.pad(f, (0, (-f.shape[0]) % PACK_ROWS)))
    total = sum(f.shape[0] for f in flat)
    flat.append(jnp.zeros(((-total) % (GRAD_ROW_TILE * LANES),), F32))
    return jnp.concatenate(flat).reshape(-1, LANES)


def _unpack_small(packed, shapes):
    out, row = [], 0
    for shp in shapes:
        size = math.prod(shp)
        rows = -(-size // PACK_ROWS) * SUBLANES
        out.append(packed[row:row + rows].reshape(-1)[:size].reshape(shp))
        row += rows
    return out


def kernel(x, mem, ffn1_norm, ffn1_w_gate, ffn1_w_up, ffn1_w_down, mix_norm, w_in, pool_w, pool_scale, w_pool_proj, ssm_a_re, ssm_a_im, ssm_log_dt, ssm_b_re, ssm_b_im, ssm_c_re, ssm_c_im, ssm_d, w_glu_val, w_glu_gate, w_mix_out, xattn_norm, mem_norm, w_q, w_kv, w_xo, ffn2_norm, ffn2_w_gate, ffn2_w_up, ffn2_w_down, final_norm, loss_target, m_ffn1_norm, m_ffn1_w_gate, m_ffn1_w_up, m_ffn1_w_down, m_mix_norm, m_w_in, m_pool_w, m_pool_scale, m_w_pool_proj, m_ssm_a_re, m_ssm_a_im, m_ssm_log_dt, m_ssm_b_re, m_ssm_b_im, m_ssm_c_re, m_ssm_c_im, m_ssm_d, m_w_glu_val, m_w_glu_gate, m_w_mix_out, m_xattn_norm, m_mem_norm, m_w_q, m_w_kv, m_w_xo, m_ffn2_norm, m_ffn2_w_gate, m_ffn2_w_up, m_ffn2_w_down, m_final_norm, v_ffn1_norm, v_ffn1_w_gate, v_ffn1_w_up, v_ffn1_w_down, v_mix_norm, v_w_in, v_pool_w, v_pool_scale, v_w_pool_proj, v_ssm_a_re, v_ssm_a_im, v_ssm_log_dt, v_ssm_b_re, v_ssm_b_im, v_ssm_c_re, v_ssm_c_im, v_ssm_d, v_w_glu_val, v_w_glu_gate, v_w_mix_out, v_xattn_norm, v_mem_norm, v_w_q, v_w_kv, v_w_xo, v_ffn2_norm, v_ffn2_w_gate, v_ffn2_w_up, v_ffn2_w_down, v_final_norm):
    given = dict(locals())
    wts = {n: given[n] for n in WEIGHTS}
    moms = {n: (given["m_" + n], given["v_" + n]) for n in WEIGHTS}
    x2, mem2, tgt2 = x[0], mem[0], loss_target[0]
    d = x2.shape[1]
    chip = (2 * lax.axis_index("x") + lax.axis_index("y")).astype(jnp.int32).reshape(1)

    def full_form(n, f):
        shard = wts[n][0].shape
        return f.reshape(N_DEV * shard[1], shard[0]) if n in COL_SHARDED else f.reshape(N_DEV * shard[0], shard[1])

    shards = {n: _to_rows(n, wts[n][0], d).astype(BF16) for n in BIG}
    first = FFN1_BIG[:2]
    wt = {n: full_form(n, f) for n, f in zip(first, _allgather("weight_allgather_first", [shards[n] for n in first]))}
    started = _split_start("weight_gather_start", [_gather_group([shards[n] for n in names]) for _, names in GATHER_PLAN],
                           after=wt[first[0]])
    gathers = {event: (names, st) for (event, names), st in zip(GATHER_PLAN, started)}
    sm = {n: (wts[n].reshape(1, -1) if wts[n].ndim <= 2 else wts[n][0]) for n in SMALL}
    sm["ffn1_norm"] = sm["ffn1_norm"] + started[0]["token"][0, 0]

    pending = {}

    def reduce_start(tag, names, gb):
        blocks = [gb[n].reshape(N_DEV, -1, d) for n in names]
        pad_rows = (-sum(b.shape[1] for b in blocks)) % GRAD_ROW_TILE
        pad = [jnp.zeros((N_DEV, pad_rows, d), BF16)] if pad_rows else []
        recv = _exchange_cores("grad_exchange_cores_" + tag, blocks + pad)
        own = jnp.concatenate([lax.dynamic_index_in_dim(b.reshape(4, 2, b.shape[1], d), lax.axis_index("c"), 1, False)
                               for b in blocks + pad], axis=1)
        rows_all = own.shape[1]
        pair = _ew("grad_pair_sum_" + tag, lambda a, b: (a.astype(F32) + b.astype(F32),),
                   [own.reshape(-1, d), recv.reshape(-1, d)], [BF16], rows_pref=5 * GRAD_ROW_TILE)[0]
        pair = pair.reshape(4, rows_all, d)
        pending[tag] = (pair, _chips_start("grad_exchange_chips_start_" + tag, pair), [b.shape[1] for b in blocks])
        return pending[tag][1]["token"]

    def reduce_finish(tag, after):
        _, started, rows = pending[tag]
        (pair,), (recv,) = _split_wait("grad_exchange_chips_wait_" + tag, started, after)
        return _chip_sum("grad_chip_sum_" + tag, pair, recv, chip), rows

    def ev(name, gb=None, gs=None, loss=None, marker=None):
        if name in gathers:
            names, started = gathers[name]
            for n, f in zip(names, _split_wait("weight_gather_wait_" + name, started, marker)[1]):
                wt[n] = full_form(n, f)
        elif name == "grads_main":
            return reduce_start("main", MAIN_BIG, gb)
        elif name == "small_early":
            pending["small"] = _slots_start("small_gather_start", _pack_small([gs[n] for n in EARLY_SMALL] + [loss[:, :1]]))
            return pending["small"]["token"]
        elif name == "grads_ffn1":
            return reduce_start("ffn1", FFN1_BIG, gb)
        return None

    _, dx, _, gs = _local_step(x2, mem2, tgt2, wt, sm, ev)

    grads = {}
    for tag, names in (("main", MAIN_BIG), ("ffn1", FFN1_BIG)):
        g_rows, rows = reduce_finish(tag, dx)
        off = 0
        for n, r in zip(names, rows):
            shard = wts[n].shape
            grads[n] = g_rows[off:off + r].reshape((shard[2], shard[1]) if n in COL_SHARDED else shard[1:])
            off += r
    small_sum = _sum_slots("small_sum", _split_wait("small_gather_wait", pending["small"], dx)[1][0], F32)
    late = _allgather("small_allgather_late", [gs[LATE_SMALL].reshape(-1, LANES)])[0]
    late_sum = _sum_slots("small_sum_late", late.reshape(N_DEV, -1, LANES), F32)
    vals = _unpack_small(small_sum, [wts[n].shape for n in EARLY_SMALL] + [(1, 1)])
    total_loss = vals[-1].reshape(())
    for n, g_full in zip(EARLY_SMALL + [LATE_SMALL], vals[:-1] + [late_sum]):
        grads[n] = g_full.reshape(-1, wts[n].shape[-1])

    out_g, out_d, out_m, out_v = {}, {}, {}, {}
    by_shape = {}
    for n in WEIGHTS:
        by_shape.setdefault((wts[n].size // wts[n].shape[-1], wts[n].shape[-1], n in COL_SHARDED), []).append(n)
    for (_, _, transposed), names in by_shape.items():
        two_d = (-1, wts[names[0]].shape[-1])
        items = [(wts[n].reshape(two_d), grads[n], moms[n][0].reshape(two_d), moms[n][1].reshape(two_d)) for n in names]
        for n, res in zip(names, _adamw_group("adamw_" + names[0], items, transposed)):
            shape = wts[n].shape
            out_d[n], out_m[n], out_v[n] = (a.reshape(shape) for a in res[:3])
            out_g[n] = (res[3] if transposed else grads[n]).reshape(shape)

    return (total_loss, dx[None], *[out_g[n] for n in WEIGHTS], *[out_d[n] for n in WEIGHTS],
            *[out_m[n] for n in WEIGHTS], *[out_v[n] for n in WEIGHTS])
```

```python
import functools
import math

import jax
import jax.numpy as jnp
from jax import lax
from jax.experimental import pallas as pl
from jax.experimental.pallas import tpu as pltpu

F32 = jnp.float32
BF16 = jnp.bfloat16
EPS = 1e-6
N_XHEADS = 4
POOL_WINDOWS = (2, 4, 8, 16)
ADAM_LR = 0.001
ADAM_B1 = 0.9
ADAM_B2 = 0.999
ADAM_EPS = 1e-08
ADAM_WD = 0.01
ADAM_STEP = 10
N_DEV = 8
VMEM_LIMIT_V7X = 48 * 1024 * 1024
LANES = 128
SUBLANES = 8
SUB_ROWS = 256
POOL_PAD = 16
MESH = pl.DeviceIdType.MESH
ANY = pl.BlockSpec(memory_space=pl.ANY)
HBM = pl.BlockSpec(memory_space=pltpu.HBM)
SEM = pl.BlockSpec(memory_space=pltpu.SEMAPHORE)
SIDE_EFFECT = pltpu.SideEffectType.DATAFLOW_SIDE_EFFECTING

_DIMS = {
    "nt": (((1,), (1,)), ((), ())),
    "nn": (((1,), (0,)), ((), ())),
    "tn": (((0,), (0,)), ((), ())),
}


def _pick(dim, pref, mult=LANES):
    if dim <= pref:
        return dim
    for t in range(pref - pref % mult, 0, -mult):
        if dim % t == 0:
            return t
    return dim


def _params(sem):
    return pltpu.CompilerParams(dimension_semantics=sem, vmem_limit_bytes=VMEM_LIMIT_V7X)


def _tile(tm, tn, coff=0):
    return pl.BlockSpec((tm, tn), lambda i, j: (i, j + coff))


def _rowvec(tn, coff=0):
    return pl.BlockSpec((1, tn), lambda i, j: (0, j + coff))


def _out(m, n, dtype):
    return jax.ShapeDtypeStruct((m, n), dtype)


def _mm(name, form, a_list, b_list, groups, m, n, tm, tn, extras, epilogue, outs, after=None, sub=SUB_ROWS):
    na, nb, ne = len(a_list), len(b_list), len(extras)
    pins = [] if after is None else [after]
    step = tm if (sub is None or form == "tn" or tm % sub) else sub

    def a_spec(a):
        if form == "tn":
            return pl.BlockSpec((a.shape[0], tm), lambda i, j: (0, i))
        return pl.BlockSpec((tm, a.shape[1]), lambda i, j: (i, 0))

    def b_spec(b):
        if form == "nt":
            return pl.BlockSpec((tn, b.shape[1]), lambda i, j: (j, 0))
        return pl.BlockSpec((b.shape[0], tn), lambda i, j: (0, j))

    def body(*refs):
        a_refs, b_refs = refs[:na], refs[na:na + nb]
        e_refs, o_refs = refs[na + nb:na + nb + ne], refs[na + nb + ne + len(pins):]
        b_vals = {}
        for s0 in range(0, tm, step):
            rows = slice(None) if step == tm else pl.ds(s0, step)
            a_vals, accs = {}, []
            for group in groups:
                acc = None
                for ai, bi in group:
                    if ai not in a_vals:
                        a_vals[ai] = (a_refs[ai][...] if form == "tn" else a_refs[ai][rows, :]).astype(BF16)
                    if bi not in b_vals:
                        b_vals[bi] = b_refs[bi][...].astype(BF16)
                    d = lax.dot_general(a_vals[ai], b_vals[bi], _DIMS[form], preferred_element_type=F32)
                    acc = d if acc is None else acc + d
                accs.append(acc)
            res = epilogue(accs, *[e[rows, :] if e.shape[0] == tm else e[...] for e in e_refs])
            for o_ref, r in zip(o_refs, res):
                o_ref[rows, :] = r.astype(o_ref.dtype)

    out_specs = [_tile(tm, tn) if s is None else s for _, s in outs]
    res = pl.pallas_call(
        body, name=name, grid=(m // tm, n // tn),
        in_specs=[a_spec(a) for a in a_list] + [b_spec(b) for b in b_list] + [s for _, s in extras] + [ANY] * len(pins),
        out_specs=out_specs, out_shape=[o for o, _ in outs],
        compiler_params=_params(("parallel", "parallel")),
    )(*a_list, *b_list, *[e for e, _ in extras], *pins)
    return res


def _mm1(name, form, a, b, m, n, tm, tn, dtype, scale=None):
    epi = (lambda accs: (accs[0],)) if scale is None else (lambda accs: (accs[0] * scale,))
    return _mm(name, form, [a], [b], [[(0, 0)]], m, n, tm, tn, [], epi, [(_out(m, n, dtype), None)])[0]


def _rms_fwd(name, h, g):
    t, d = h.shape
    tm = _pick(t, 512, SUBLANES)

    def body(h_ref, g_ref, n_ref):
        hv = h_ref[...]
        r = lax.rsqrt(jnp.mean(hv * hv, axis=-1, keepdims=True) + EPS)
        n_ref[...] = ((hv * r) * g_ref[...]).astype(BF16)

    return pl.pallas_call(
        body, name=name, grid=(t // tm,),
        in_specs=[pl.BlockSpec((tm, d), lambda i: (i, 0)), pl.BlockSpec((1, d), lambda i: (0, 0))],
        out_specs=pl.BlockSpec((tm, d), lambda i: (i, 0)), out_shape=_out(t, d, BF16),
        compiler_params=_params(("parallel",)),
    )(h, g)


def _rms_bwd(name, h, g, dn, dres=None):
    t, d = h.shape
    tm = _pick(t, 512, SUBLANES)
    need_dh = dres is not None

    def body(*refs):
        if need_dh:
            h_ref, g_ref, dn_ref, dres_ref, dh_ref, dhb_ref, dg_ref = refs
        else:
            h_ref, g_ref, dn_ref, dg_ref = refs
        hv = h_ref[...]
        r = lax.rsqrt(jnp.mean(hv * hv, axis=-1, keepdims=True) + EPS)
        nh = hv * r
        dnv = dn_ref[...].astype(F32)

        @pl.when(pl.program_id(0) == 0)
        def _():
            dg_ref[...] = jnp.zeros_like(dg_ref)

        dg_ref[...] += jnp.sum(dnv * nh, axis=0, keepdims=True)
        if need_dh:
            dng = dnv * g_ref[...]
            dh = dres_ref[...] + r * (dng - nh * jnp.mean(dng * nh, axis=-1, keepdims=True))
            dh_ref[...] = dh
            dhb_ref[...] = dh.astype(BF16)

    row = pl.BlockSpec((tm, d), lambda i: (i, 0))
    vec = pl.BlockSpec((1, d), lambda i: (0, 0))
    if need_dh:
        return pl.pallas_call(
            body, name=name, grid=(t // tm,), in_specs=[row, vec, row, row], out_specs=[row, row, vec],
            out_shape=[_out(t, d, F32), _out(t, d, BF16), _out(1, d, F32)], compiler_params=_params(("arbitrary",)),
        )(h, g, dn, dres)
    return pl.pallas_call(
        body, name=name, grid=(t // tm,), in_specs=[row, vec, row], out_specs=vec,
        out_shape=_out(1, d, F32), compiler_params=_params(("arbitrary",)),
    )(h, g, dn)


def _loss_head(h, g, tgt):
    t, d = h.shape
    tm = _pick(t, 512, SUBLANES)

    def body(h_ref, g_ref, t_ref, dh_ref, dhb_ref, dg_ref, loss_ref):
        hv = h_ref[...]
        r = lax.rsqrt(jnp.mean(hv * hv, axis=-1, keepdims=True) + EPS)
        nh = hv * r
        err = nh * g_ref[...] - t_ref[...]

        @pl.when(pl.program_id(0) == 0)
        def _():
            dg_ref[...] = jnp.zeros_like(dg_ref)
            loss_ref[...] = jnp.zeros_like(loss_ref)

        per_row = jnp.mean(err * err, axis=-1, keepdims=True)
        loss_ref[...] += 0.5 * jnp.sum(per_row, axis=0, keepdims=True)
        dy = err * (1.0 / d)
        dg_ref[...] += jnp.sum(dy * nh, axis=0, keepdims=True)
        dng = dy * g_ref[...]
        dh = r * (dng - nh * jnp.mean(dng * nh, axis=-1, keepdims=True))
        dh_ref[...] = dh
        dhb_ref[...] = dh.astype(BF16)

    row = pl.BlockSpec((tm, d), lambda i: (i, 0))
    vec = pl.BlockSpec((1, d), lambda i: (0, 0))
    return pl.pallas_call(
        body, name="loss_head", grid=(t // tm,), in_specs=[row, vec, row],
        out_specs=[row, row, vec, pl.BlockSpec((1, LANES), lambda i: (0, 0))],
        out_shape=[_out(t, d, F32), _out(t, d, BF16), _out(1, d, F32), _out(1, LANES, F32)],
        compiler_params=_params(("arbitrary",)),
    )(h, g, tgt)


def _ffn_fwd(tag, h, n, wg_t, wu_t, wd):
    t, d = h.shape
    f = wg_t.shape[0]
    tm, tn = _pick(t, 1024), _pick(f, 1408)

    def up_epi(accs):
        a, b = accs
        return a, b, (a * jax.nn.sigmoid(a)) * b

    a, b, hid = _mm(tag + "_up", "nt", [n], [wg_t, wu_t], [[(0, 0)], [(0, 1)]], t, f, tm, tn, [], up_epi,
                    [(_out(t, f, BF16), None)] * 3)
    if callable(wd):
        wd = wd(hid)
    tm2, tn2 = _pick(t, 1024), _pick(d, 512)
    h_out = _mm(tag + "_down", "nn", [hid], [wd], [[(0, 0)]], t, d, tm2, tn2, [(h, _tile(tm2, tn2))],
                lambda accs, hin: (hin + 0.5 * accs[0],), [(_out(t, d, F32), None)])[0]
    return h_out, (n, a, b, hid)


def _ffn_bwd(tag, h, g, wg_t, wu_t, wd, saved, dh, dh_bf, weights_done=None, after=None):
    n, a, b, hid = saved
    t, d = h.shape
    f = wd.shape[0]
    tm, tn = _pick(t, 1024), _pick(f, 1408)

    def hid_epi(accs, av, bv):
        dhid = 0.5 * accs[0]
        av, bv = av.astype(F32), bv.astype(F32)
        sig = jax.nn.sigmoid(av)
        da = dhid * bv * (sig * (1.0 + av * (1.0 - sig)))
        db = dhid * (av * sig)
        return da, db

    da, db = _mm(tag + "_bwd_hid", "nt", [dh_bf], [wd], [[(0, 0)]], t, f, tm, tn,
                 [(a, _tile(tm, tn)), (b, _tile(tm, tn))], hid_epi, [(_out(t, f, BF16), None)] * 2, after=after)
    tw, tnw = _pick(f, 1408), _pick(d, 512)
    d_wd = _mm1(tag + "_dwd", "tn", hid, dh_bf, f, d, tw, tnw, BF16, scale=0.5)
    d_wg = _mm1(tag + "_dwg", "tn", da, n, f, d, tw, tnw, BF16)
    d_wu = _mm1(tag + "_dwu", "tn", db, n, f, d, tw, tnw, BF16)
    pin = weights_done(d_wg, d_wu, d_wd) if weights_done is not None else None
    tm2, tn2 = _pick(t, 1024), _pick(d, 512)
    dn = _mm(tag + "_dn", "nn", [da, db], [wg_t, wu_t], [[(0, 0), (1, 1)]], t, d, tm2, tn2, [],
             lambda accs: (accs[0],), [(_out(t, d, F32), None)], after=pin)[0]
    dh_in, dh_in_bf, dg = _rms_bwd(tag + "_norm_bwd", h, g, dn, dh)
    return dh_in, dh_in_bf, dg, d_wg, d_wu, d_wd


def _window_sum(win, offsets):
    n = win.shape[0]
    acc = None
    for j in offsets:
        term = win if j == 0 else pltpu.roll(win, (-j) % n, 0)
        acc = term if acc is None else acc + term
    return acc


def _pool_counts(r0, ch, c, left, right, t):
    pos = r0 + lax.broadcasted_iota(jnp.int32, (ch, c), 0)
    return (jnp.minimum(pos + right + 1, t) - jnp.maximum(pos - left, 0)).astype(F32)


def _pool_fwd(proj, pool_w_bf, pool_scale):
    t = proj.shape[0]
    ng, c, _ = pool_w_bf.shape
    ch = _pick(t, 256, SUBLANES)
    pad = POOL_PAD

    def body(p_ref, w_ref, s_ref, pooled_ref, pm_ref, buf):
        grp = pl.program_id(0)
        buf[pl.ds(0, pad), :] = jnp.zeros((pad, c), F32)
        buf[pl.ds(pad + t, pad), :] = jnp.zeros((pad, c), F32)

        def fill(ci, carry):
            r0 = pl.multiple_of(ci * ch, SUBLANES)
            buf[pl.ds(pl.multiple_of(r0 + pad, SUBLANES), ch), :] = p_ref[pl.ds(r0, ch), :]
            return carry

        lax.fori_loop(0, t // ch, fill, 0)
        for gi, w in enumerate(POOL_WINDOWS):
            left = w // 2
            right = w - 1 - left

            @pl.when(grp == gi)
            def _(left=left, right=right):
                def chunk(ci, carry):
                    r0 = pl.multiple_of(ci * ch, SUBLANES)
                    win = buf[pl.ds(r0, ch + 2 * pad), :]
                    s = _window_sum(win, range(-left, right + 1))[pad:pad + ch]
                    pooled = s / _pool_counts(r0, ch, c, left, right, t) - win[pad:pad + ch]
                    pooled_bf = pooled.astype(BF16)
                    mixed = jnp.dot(pooled_bf, w_ref[0], preferred_element_type=F32)
                    pooled_ref[pl.ds(r0, ch), :] = pooled_bf
                    pm_ref[pl.ds(r0, ch), :] = (mixed * s_ref[...]).astype(BF16)
                    return carry

                lax.fori_loop(0, t // ch, chunk, 0)

    col = pl.BlockSpec((t, c), lambda g: (0, g))
    return pl.pallas_call(
        body, name="pool_fwd", grid=(ng,),
        in_specs=[col, pl.BlockSpec((1, c, c), lambda g: (g, 0, 0)), pl.BlockSpec((1, c), lambda g: (0, g))],
        out_specs=[col, col], out_shape=[_out(t, ng * c, BF16), _out(t, ng * c, BF16)],
        scratch_shapes=[pltpu.VMEM((t + 2 * pad, c), F32)],
        compiler_params=_params(("parallel",)),
    )(proj, pool_w_bf, pool_scale)


def _pool_bwd(pooled, dpm, pool_w_bf, pool_scale):
    t = pooled.shape[0]
    ng, c, _ = pool_w_bf.shape
    ch = _pick(t, 256, SUBLANES)
    pad = POOL_PAD

    def body(pooled_ref, dpm_ref, w_ref, s_ref, dp_ref, dw_ref, ds_ref, buf, raw):
        grp = pl.program_id(0)
        buf[pl.ds(0, pad), :] = jnp.zeros((pad, c), F32)
        buf[pl.ds(pad + t, pad), :] = jnp.zeros((pad, c), F32)
        dw_ref[...] = jnp.zeros_like(dw_ref)
        ds_ref[...] = jnp.zeros_like(ds_ref)
        for gi, w in enumerate(POOL_WINDOWS):
            left = w // 2
            right = w - 1 - left

            @pl.when(grp == gi)
            def _(left=left, right=right):
                def first(ci, carry):
                    r0 = pl.multiple_of(ci * ch, SUBLANES)
                    pv = pooled_ref[pl.ds(r0, ch), :]
                    dpm_v = dpm_ref[pl.ds(r0, ch), :]
                    mixed = jnp.dot(pv, w_ref[0], preferred_element_type=F32)
                    ds_ref[...] += jnp.sum(dpm_v * mixed, axis=0, keepdims=True)
                    dmixed = (dpm_v * s_ref[...]).astype(BF16)
                    dw_ref[0] += lax.dot_general(pv, dmixed, _DIMS["tn"], preferred_element_type=F32)
                    dpooled = lax.dot_general(dmixed, w_ref[0], _DIMS["nt"], preferred_element_type=F32)
                    raw[pl.ds(r0, ch), :] = dpooled
                    buf[pl.ds(pl.multiple_of(r0 + pad, SUBLANES), ch), :] = (
                        dpooled / _pool_counts(r0, ch, c, left, right, t))
                    return carry

                lax.fori_loop(0, t // ch, first, 0)

                def second(ci, carry):
                    r0 = pl.multiple_of(ci * ch, SUBLANES)
                    win = buf[pl.ds(r0, ch + 2 * pad), :]
                    s = _window_sum(win, range(-right, left + 1))[pad:pad + ch]
                    dp_ref[pl.ds(r0, ch), :] = (s - raw[pl.ds(r0, ch), :]).astype(BF16)
                    return carry

                lax.fori_loop(0, t // ch, second, 0)

    col = pl.BlockSpec((t, c), lambda g: (0, g))
    return pl.pallas_call(
        body, name="pool_bwd", grid=(ng,),
        in_specs=[col, col, pl.BlockSpec((1, c, c), lambda g: (g, 0, 0)), pl.BlockSpec((1, c), lambda g: (0, g))],
        out_specs=[col, pl.BlockSpec((1, c, c), lambda g: (g, 0, 0)), pl.BlockSpec((1, c), lambda g: (0, g))],
        out_shape=[_out(t, ng * c, BF16), jax.ShapeDtypeStruct((ng, c, c), F32), _out(1, ng * c, F32)],
        scratch_shapes=[pltpu.VMEM((t + 2 * pad, c), F32), pltpu.VMEM((t, c), F32)],
        compiler_params=_params(("parallel",)),
    )(pooled, dpm, pool_w_bf, pool_scale)


def _discretise(a_re, a_im, log_dt, b_re, b_im):
    dt = jnp.exp(log_dt)
    mag = jnp.exp(dt * a_re)
    ang = dt * a_im
    abr = mag * jnp.cos(ang)
    abi = mag * jnp.sin(ang)
    den = a_re * a_re + a_im * a_im
    nr = abr - 1.0
    qr = (nr * a_re + abi * a_im) / den
    qi = (abi * a_re - nr * a_im) / den
    return abr, abi, qr * b_re - qi * b_im, qr * b_im + qi * b_re


def _ssm_disc(cols):
    n, hh = cols[3].shape

    def body(ar, ai, ld, br, bi, o1, o2, o3, o4):
        res = _discretise(ar[...], ai[...], ld[...], br[...], bi[...])
        for o, r in zip((o1, o2, o3, o4), res):
            o[...] = r

    return pl.pallas_call(
        body, name="ssm_disc",
        out_shape=[_out(n, 1, F32), _out(n, 1, F32), _out(n, hh, F32), _out(n, hh, F32)],
    )(*cols)


def _ssm_disc_bwd(cols, cots):
    n, hh = cols[3].shape

    def body(ar, ai, ld, br, bi, c1, c2, c3, c4, o1, o2, o3, o4, o5):
        _, vjp = jax.vjp(_discretise, ar[...], ai[...], ld[...], br[...], bi[...])
        res = vjp((c1[...], c2[...], c3[...], c4[...]))
        for o, r in zip((o1, o2, o3, o4, o5), res):
            o[...] = r

    return pl.pallas_call(
        body, name="ssm_disc_bwd",
        out_shape=[_out(n, 1, F32)] * 3 + [_out(n, hh, F32)] * 2,
    )(*cols, *cots)


def _rowsum(name, a):
    r, _ = a.shape

    def body(a_ref, o_ref):
        o_ref[...] = jnp.sum(a_ref[...], axis=-1, keepdims=True)

    return pl.pallas_call(body, name=name, out_shape=_out(r, 1, F32))(a)


def _cmul(pr, pi, qr, qi):
    return pr * qr - pi * qi, pr * qi + pi * qr


def _cpow(pr, pi, n):
    rr, ri = None, None
    while n:
        if n & 1:
            rr, ri = (pr, pi) if rr is None else _cmul(rr, ri, pr, pi)
        n >>= 1
        if n:
            pr, pi = _cmul(pr, pi, pr, pi)
    return rr, ri


def _segment_carry(er, ei, pr, pi, reverse):
    row = lax.broadcasted_iota(jnp.int32, er.shape, 0)
    cr, ci = jnp.zeros_like(er), jnp.zeros_like(ei)
    for _ in range(SUBLANES - 1):
        tr = er + pr * cr - pi * ci
        ti = ei + pr * ci + pi * cr
        if reverse:
            keep, shift = row < SUBLANES - 1, SUBLANES - 1
        else:
            keep, shift = row >= 1, 1
        cr = jnp.where(keep, pltpu.roll(tr, shift, 0), 0.0)
        ci = jnp.where(keep, pltpu.roll(ti, shift, 0), 0.0)
    return cr, ci


def _ssm_fwd(name, sp, b_re, b_im, c_re, c_im, ar, ai, reverse):
    t, c = sp.shape
    s = ar.shape[1]
    w = _pick(s, 512)
    ch = _pick(t, 512, SUBLANES)
    n_ch, gpc, steps = t // ch, ch // SUBLANES, t // SUBLANES

    def body(sp_ref, bre_ref, bim_ref, cre_ref, cim_ref, ar_ref, ai_ref, xr_ref, xi_ref, y_ref, ur, ui, xbr, xbi):
        a_r = jnp.broadcast_to(ar_ref[...], (SUBLANES, w))
        a_i = jnp.broadcast_to(ai_ref[...], (SUBLANES, w))

        @pl.when(pl.program_id(0) == 0)
        def _():
            y_ref[...] = jnp.zeros_like(y_ref)

        def sweep(h0, store):
            def chunk(k, h):
                ci = n_ch - 1 - k if reverse else k
                rows = pl.ds(pl.multiple_of(ci * ch, ch), ch)
                spv = sp_ref[rows, :].astype(BF16)
                ur[...] = jnp.dot(spv, bre_ref[...], preferred_element_type=F32)
                ui[...] = jnp.dot(spv, bim_ref[...], preferred_element_type=F32)

                def group(g, hh):
                    gi = gpc - 1 - g if reverse else g
                    r0 = pl.multiple_of(gi * SUBLANES, SUBLANES)
                    hr, hi = hh
                    nr = a_r * hr - a_i * hi + ur[pl.ds(r0, SUBLANES), :]
                    ni = a_r * hi + a_i * hr + ui[pl.ds(r0, SUBLANES), :]
                    if store:
                        xbr[pl.ds(r0, SUBLANES), :] = nr
                        xbi[pl.ds(r0, SUBLANES), :] = ni
                    return nr, ni

                h = lax.fori_loop(0, gpc, group, h)
                if store:
                    xr16, xi16 = xbr[...].astype(BF16), xbi[...].astype(BF16)
                    xr_ref[rows, :] = xr16
                    xi_ref[rows, :] = xi16
                    y_ref[rows, :] += (jnp.dot(xr16, cre_ref[...], preferred_element_type=F32)
                                       + jnp.dot(xi16, cim_ref[...], preferred_element_type=F32))
                return h

            return lax.fori_loop(0, n_ch, chunk, h0)

        zero = jnp.zeros((SUBLANES, w), F32)
        er, ei = sweep((zero, zero), False)
        pr, pi = _cpow(ar_ref[...], ai_ref[...], steps)
        sweep(_segment_carry(er, ei, pr, pi, reverse), True)

    col = lambda i: (0, i)
    return pl.pallas_call(
        body, name=name, grid=(s // w,),
        in_specs=[pl.BlockSpec((t, c), lambda i: (0, 0)), pl.BlockSpec((c, w), col), pl.BlockSpec((c, w), col),
                  pl.BlockSpec((w, c), lambda i: (i, 0)), pl.BlockSpec((w, c), lambda i: (i, 0)),
                  pl.BlockSpec((1, w), col), pl.BlockSpec((1, w), col)],
        out_specs=[pl.BlockSpec((t, w), col), pl.BlockSpec((t, w), col), pl.BlockSpec((t, c), lambda i: (0, 0))],
        out_shape=[_out(t, s, BF16), _out(t, s, BF16), _out(t, c, F32)],
        scratch_shapes=[pltpu.VMEM((ch, w), F32)] * 4,
        compiler_params=_params(("arbitrary",)),
    )(sp, b_re, b_im, c_re, c_im, ar, ai)


def _ssm_bwd(name, dyp, c_re, c_im, xr, xi, ar, ai, reverse):
    t, c = dyp.shape
    s = ar.shape[1]
    w = _pick(s, 512)
    ch = _pick(t, 512, SUBLANES)
    n_ch, gpc, steps = t // ch, ch // SUBLANES, t // SUBLANES
    back = not reverse
    edge = 2 * SUBLANES

    def body(dy_ref, cre_ref, cim_ref, xr_ref, xi_ref, ar_ref, ai_ref, lr_ref, li_ref, dar_ref, dai_ref,
             gr, gi_, lbr, lbi, xbr, xbi):
        a_r = jnp.broadcast_to(ar_ref[...], (SUBLANES, w))
        a_i = -jnp.broadcast_to(ai_ref[...], (SUBLANES, w))
        row = lax.broadcasted_iota(jnp.int32, (SUBLANES, w), 0)

        def neighbours(ci, x_ref, buf):
            rows = pl.ds(pl.multiple_of(ci * ch, ch), ch)
            if reverse:
                buf[pl.ds(0, ch), :] = x_ref[rows, :].astype(F32)
                nxt = x_ref[pl.ds(pl.multiple_of(jnp.minimum(ci + 1, n_ch - 1) * ch, ch), edge), :].astype(F32)[:SUBLANES]
                first = x_ref[pl.ds(0, edge), :].astype(F32)[:SUBLANES]
                wrap = jnp.where(row < SUBLANES - 1, pltpu.roll(first, SUBLANES - 1, 0), 0.0)
                buf[pl.ds(ch, SUBLANES), :] = jnp.where(ci == n_ch - 1, wrap, nxt)
            else:
                buf[pl.ds(SUBLANES, ch), :] = x_ref[rows, :].astype(F32)
                prv = x_ref[pl.ds(pl.multiple_of(jnp.maximum(ci * ch - edge, 0), edge), edge), :].astype(F32)[SUBLANES:]
                last = x_ref[pl.ds(t - edge, edge), :].astype(F32)[SUBLANES:]
                wrap = jnp.where(row >= 1, pltpu.roll(last, 1, 0), 0.0)
                buf[pl.ds(0, SUBLANES), :] = jnp.where(ci == 0, wrap, prv)

        def sweep(h0, store):
            def chunk(k, carry):
                ci = n_ch - 1 - k if back else k
                rows = pl.ds(pl.multiple_of(ci * ch, ch), ch)
                dyv = dy_ref[rows, :].astype(BF16)
                gr[...] = lax.dot_general(dyv, cre_ref[...], _DIMS["nt"], preferred_element_type=F32)
                gi_[...] = lax.dot_general(dyv, cim_ref[...], _DIMS["nt"], preferred_element_type=F32)
                if store:
                    neighbours(ci, xr_ref, xbr)
                    neighbours(ci, xi_ref, xbi)

                def group(g, cc):
                    gidx = gpc - 1 - g if back else g
                    r0 = pl.multiple_of(gidx * SUBLANES, SUBLANES)
                    hr, hi = cc[0], cc[1]
                    nr = a_r * hr - a_i * hi + gr[pl.ds(r0, SUBLANES), :]
                    ni = a_r * hi + a_i * hr + gi_[pl.ds(r0, SUBLANES), :]
                    if not store:
                        return nr, ni
                    lbr[pl.ds(r0, SUBLANES), :] = nr
                    lbi[pl.ds(r0, SUBLANES), :] = ni
                    x0 = pl.multiple_of(r0 + SUBLANES, SUBLANES) if reverse else r0
                    xpr, xpi = xbr[pl.ds(x0, SUBLANES), :], xbi[pl.ds(x0, SUBLANES), :]
                    return nr, ni, cc[2] + nr * xpr + ni * xpi, cc[3] + ni * xpr - nr * xpi

                carry = lax.fori_loop(0, gpc, group, carry)
                if store:
                    lr_ref[rows, :] = lbr[...].astype(BF16)
                    li_ref[rows, :] = lbi[...].astype(BF16)
                return carry

            return lax.fori_loop(0, n_ch, chunk, h0)

        zero = jnp.zeros((SUBLANES, w), F32)
        er, ei = sweep((zero, zero), False)
        pr, pi = _cpow(ar_ref[...], -ai_ref[...], steps)
        cr, ci0 = _segment_carry(er, ei, pr, pi, back)
        _, _, dar, dai = sweep((cr, ci0, zero, zero), True)
        dar_ref[...] = jnp.sum(dar, axis=0, keepdims=True)
        dai_ref[...] = jnp.sum(dai, axis=0, keepdims=True)

    col = lambda i: (0, i)
    return pl.pallas_call(
        body, name=name, grid=(s // w,),
        in_specs=[pl.BlockSpec((t, c), lambda i: (0, 0)), pl.BlockSpec((w, c), lambda i: (i, 0)),
                  pl.BlockSpec((w, c), lambda i: (i, 0)), pl.BlockSpec((t, w), col), pl.BlockSpec((t, w), col),
                  pl.BlockSpec((1, w), col), pl.BlockSpec((1, w), col)],
        out_specs=[pl.BlockSpec((t, w), col), pl.BlockSpec((t, w), col), pl.BlockSpec((1, w), col), pl.BlockSpec((1, w), col)],
        out_shape=[_out(t, s, BF16), _out(t, s, BF16), _out(1, s, F32), _out(1, s, F32)],
        scratch_shapes=[pltpu.VMEM((ch, w), F32)] * 4 + [pltpu.VMEM((ch + SUBLANES, w), F32)] * 2,
        compiler_params=_params(("parallel",)),
    )(dyp, c_re, c_im, xr, xi, ar, ai)


def _to_segments(a):
    t, c = a.shape
    return a.reshape(SUBLANES, t // SUBLANES, c).transpose(1, 0, 2).reshape(t, c)


def _from_segments(a):
    t, c = a.shape
    return a.reshape(t // SUBLANES, SUBLANES, c).transpose(1, 0, 2).reshape(t, c)


def _colsum_prod(name, a, b, b_coff=0):
    t, n = a.shape
    tm = _pick(t, 512, SUBLANES)

    def body(a_ref, b_ref, o_ref):
        @pl.when(pl.program_id(0) == 0)
        def _():
            o_ref[...] = jnp.zeros_like(o_ref)

        o_ref[...] += jnp.sum(a_ref[...].astype(F32) * b_ref[...].astype(F32), axis=0, keepdims=True)

    return pl.pallas_call(
        body, name=name, grid=(t // tm,),
        in_specs=[pl.BlockSpec((tm, n), lambda i: (i, 0)), pl.BlockSpec((tm, n), lambda i: (i, b_coff))],
        out_specs=pl.BlockSpec((1, n), lambda i: (0, 0)), out_shape=_out(1, n, F32),
        compiler_params=_params(("arbitrary",)),
    )(a, b)


def _bd_in(bb, g, p, hh):
    blk = bb.reshape(g, p, hh).transpose(0, 2, 1)
    eye = jnp.eye(g, dtype=bool)[:, None, :, None]
    return jnp.where(eye, blk[:, :, None, :], 0.0).reshape(g * hh, g * p)


def _bd_out(cc, g, p, hh):
    blk = cc.transpose(0, 2, 1)
    eye = jnp.eye(g, dtype=bool)[:, None, :, None]
    return jnp.where(eye, blk[:, :, None, :], 0.0).reshape(g * p, g * hh)


def _diag_in(dmat, g, p, hh):
    eye = jnp.eye(g, dtype=bool)[:, None, :, None]
    diag = jnp.sum(jnp.where(eye, dmat.reshape(g, hh, g, p), 0.0), axis=2)
    return diag.transpose(0, 2, 1).reshape(g * p, hh)


def _diag_out(dmat, g, p, hh):
    eye = jnp.eye(g, dtype=bool)[:, None, :, None]
    diag = jnp.sum(jnp.where(eye, dmat.reshape(g, p, g, hh), 0.0), axis=2)
    return diag.transpose(0, 2, 1)


def _softmax(qh, kh, scale):
    s = lax.dot_general(qh, kh, _DIMS["nt"], preferred_element_type=F32) * scale
    e = jnp.exp(s - jnp.max(s, axis=-1, keepdims=True))
    return e / jnp.sum(e, axis=-1, keepdims=True)


def _attn_fwd(q, kv):
    t, d = q.shape
    mm_ = kv.shape[0]
    hd = d // N_XHEADS
    scale = 1.0 / math.sqrt(hd)
    tm = _pick(t, 512, SUBLANES)

    def body(q_ref, kv_ref, o_ref):
        for h in range(N_XHEADS):
            sl = pl.ds(h * hd, hd)
            p = _softmax(q_ref[:, sl], kv_ref[:, sl], scale)
            o_ref[:, sl] = jnp.dot(p.astype(BF16), kv_ref[:, pl.ds(d + h * hd, hd)],
                                   preferred_element_type=F32).astype(BF16)

    return pl.pallas_call(
        body, name="attn_fwd", grid=(t // tm,),
        in_specs=[pl.BlockSpec((tm, d), lambda i: (i, 0)), pl.BlockSpec((mm_, 2 * d), lambda i: (0, 0))],
        out_specs=pl.BlockSpec((tm, d), lambda i: (i, 0)), out_shape=_out(t, d, BF16),
        compiler_params=_params(("parallel",)),
    )(q, kv)


def _attn_bwd(q, kv, do):
    t, d = q.shape
    mm_ = kv.shape[0]
    hd = d // N_XHEADS
    scale = 1.0 / math.sqrt(hd)
    tm = _pick(t, 512, SUBLANES)

    def body(q_ref, kv_ref, do_ref, dq_ref, dkv_ref):
        @pl.when(pl.program_id(0) == 0)
        def _():
            dkv_ref[...] = jnp.zeros_like(dkv_ref)

        for h in range(N_XHEADS):
            sl = pl.ds(h * hd, hd)
            vsl = pl.ds(d + h * hd, hd)
            qh, kh, doh = q_ref[:, sl], kv_ref[:, sl], do_ref[:, sl]
            p = _softmax(qh, kh, scale)
            dp = lax.dot_general(doh, kv_ref[:, vsl], _DIMS["nt"], preferred_element_type=F32)
            dkv_ref[:, vsl] += lax.dot_general(p.astype(BF16), doh, _DIMS["tn"], preferred_element_type=F32)
            ds = (p * (dp - jnp.sum(dp * p, axis=-1, keepdims=True)) * scale).astype(BF16)
            dq_ref[:, sl] = jnp.dot(ds, kh, preferred_element_type=F32).astype(BF16)
            dkv_ref[:, sl] += lax.dot_general(ds, qh, _DIMS["tn"], preferred_element_type=F32)

    row = pl.BlockSpec((tm, d), lambda i: (i, 0))
    full = pl.BlockSpec((mm_, 2 * d), lambda i: (0, 0))
    return pl.pallas_call(
        body, name="attn_bwd", grid=(t // tm,), in_specs=[row, full, row], out_specs=[row, full],
        out_shape=[_out(t, d, BF16), _out(mm_, 2 * d, F32)], compiler_params=_params(("arbitrary",)),
    )(q, kv, do)


def _ew(name, fn, ins, outs, rows_pref=256, rowvecs=()):
    r, c = ins[0].shape
    tr = _pick(r, rows_pref, SUBLANES)
    ni = len(ins) + len(rowvecs)

    def body(*refs):
        res = fn(*[x[...] for x in refs[:ni]])
        for o_ref, v in zip(refs[ni:], res):
            o_ref[...] = v.astype(o_ref.dtype)

    blk = pl.BlockSpec((tr, c), lambda i: (i, 0))
    vec = pl.BlockSpec((1, c), lambda i: (0, 0))
    return pl.pallas_call(
        body, name=name, grid=(r // tr,), in_specs=[blk] * len(ins) + [vec] * len(rowvecs), out_specs=[blk] * len(outs),
        out_shape=[_out(r, c, dt) for dt in outs], compiler_params=_params(("parallel",)),
    )(*ins, *rowvecs)


def _sum_slots(name, a, dtype):
    s, r, c = a.shape
    tr = _pick(r, 256, SUBLANES)

    def body(a_ref, o_ref):
        acc = a_ref[0].astype(F32)
        for k in range(1, s):
            acc = acc + a_ref[k].astype(F32)
        o_ref[...] = acc.astype(o_ref.dtype)

    return pl.pallas_call(
        body, name=name, grid=(r // tr,), in_specs=[pl.BlockSpec((s, tr, c), lambda i: (0, i, 0))],
        out_specs=pl.BlockSpec((tr, c), lambda i: (i, 0)), out_shape=_out(r, c, dtype),
        compiler_params=_params(("parallel",)),
    )(a)


def _adamw_step(wv, gv, mv, vv):
    bc1 = 1.0 - ADAM_B1 ** ADAM_STEP
    bc2 = 1.0 - ADAM_B2 ** ADAM_STEP
    m2 = ADAM_B1 * mv + (1.0 - ADAM_B1) * gv
    v2 = ADAM_B2 * vv + (1.0 - ADAM_B2) * (gv * gv)
    delta = -ADAM_LR * ((m2 / bc1) / (jnp.sqrt(v2 / bc2) + ADAM_EPS) + ADAM_WD * wv)
    return delta, m2, v2


def _adamw_group(name, items, transposed):
    k, r = items[0][0].shape
    tk = _pick(k, 256, SUBLANES)
    rows_out = transposed and r % LANES != 0
    n_out = 3 if (rows_out or not transposed) else 4

    def body(*refs):
        ins, outs = refs[:4 * len(items)], refs[4 * len(items):]
        for i in range(len(items)):
            wv, gv, mv, vv = (a[...] for a in ins[4 * i:4 * i + 4])
            if rows_out:
                wv, mv, vv = wv.T, mv.T, vv.T
            elif transposed:
                gv = gv.T
            res = _adamw_step(wv, gv, mv, vv) + ((gv,) if n_out == 4 else ())
            for o_ref, val in zip(outs[n_out * i:n_out * (i + 1)], res):
                o_ref[...] = val

    blk = pl.BlockSpec((tk, r), lambda j: (j, 0))
    row_blk = pl.BlockSpec((r, tk), lambda j: (0, j))
    res = pl.pallas_call(
        body, name=name, grid=(k // tk,), in_specs=[blk, row_blk if transposed else blk, blk, blk] * len(items),
        out_specs=[row_blk if rows_out else blk] * (n_out * len(items)),
        out_shape=[pltpu.HBM((r, k) if rows_out else (k, r), F32)] * (n_out * len(items)),
        compiler_params=_params(("parallel",)),
    )(*[pltpu.with_memory_space_constraint(a, pltpu.HBM) for item in items for a in item])
    out = []
    for i, item in enumerate(items):
        got = list(res[n_out * i:n_out * (i + 1)]) + ([item[1]] if n_out == 3 else [])
        out.append([a.T for a in got] if rows_out else got)
    return out


def _allgather(name, arrs):
    n = len(arrs)

    def body(*refs):
        ins, outs = refs[:n], refs[n:2 * n]
        send_sems, recv_sems, local_sems = refs[2 * n:]
        x, y, c = lax.axis_index("x"), lax.axis_index("y"), lax.axis_index("c")
        me, sibling = (x, y, c), (x, y, 1 - c)
        chips = [(1 - x, y), (x, 1 - y), (1 - x, 1 - y)]

        def rows(a, px, py, pc):
            r = ins[a].shape[0]
            return outs[a].at[pl.ds((4 * px + 2 * py + pc) * r, r), :]

        def copy(a, k, block, to, src=None):
            return pltpu.make_async_remote_copy(
                src_ref=rows(a, *block) if src is None else src, dst_ref=rows(a, *block),
                send_sem=send_sems.at[a, k], recv_sem=recv_sems.at[a, k], device_id=to, device_id_type=MESH)

        mine = [pltpu.make_async_copy(ins[a], rows(a, *me), local_sems.at[a]) for a in range(n)]
        for cp in mine:
            cp.start()
        first = []
        for a in range(n):
            first.append(copy(a, 0, me, sibling, src=ins[a]))
            first += [copy(a, 1 + j, me, (*chip, c), src=ins[a]) for j, chip in enumerate(chips)]
        for cp in first:
            cp.start()
        passed = []
        for j, chip in enumerate(chips):
            for a in range(n):
                copy(a, 1 + j, (*chip, c), me).wait_recv()
                cp = copy(a, 4 + j, (*chip, c), sibling)
                cp.start()
                passed.append(cp)
        for a in range(n):
            copy(a, 0, sibling, me).wait_recv()
            for j, chip in enumerate(chips):
                copy(a, 4 + j, (*chip, 1 - c), me).wait_recv()
        for cp in first + passed:
            cp.wait_send()
        for cp in mine:
            cp.wait()

    return pl.pallas_call(
        body, name=name, in_specs=[ANY] * n, out_specs=[ANY] * n,
        out_shape=[_out(N_DEV * a.shape[0], a.shape[1], a.dtype) for a in arrs],
        scratch_shapes=[pltpu.SemaphoreType.DMA((n, 7)), pltpu.SemaphoreType.DMA((n, 7)), pltpu.SemaphoreType.DMA((n,))],
    )(*arrs)


def _exchange_cores(name, blocks):
    n = len(blocks)
    c = blocks[0].shape[2]
    r = sum(b.shape[1] for b in blocks)

    def body(*refs):
        srcs, (recv_ref, send_sems, recv_sems) = refs[:n], refs[n:]
        x, y, cc = lax.axis_index("x"), lax.axis_index("y"), lax.axis_index("c")
        copies, off = [], 0
        for a, src in enumerate(srcs):
            rows = pl.ds(off, src.shape[1])
            off += src.shape[1]
            for q in range(4):
                copies.append(pltpu.make_async_remote_copy(
                    src_ref=src.at[2 * q + (1 - cc)], dst_ref=recv_ref.at[q, rows], send_sem=send_sems.at[a, q],
                    recv_sem=recv_sems.at[a, q], device_id=(x, y, 1 - cc), device_id_type=MESH))
        for cp in copies:
            cp.start()
        for cp in copies:
            cp.wait()

    return pl.pallas_call(
        body, name=name, in_specs=[ANY] * n, out_specs=ANY,
        out_shape=jax.ShapeDtypeStruct((4, r, c), blocks[0].dtype),
        scratch_shapes=[pltpu.SemaphoreType.DMA((n, 4))] * 2,
    )(*blocks)


def _peer(k, x, y, c):
    return (1 - x if k & 4 else x, 1 - y if k & 2 else y, 1 - c if k & 1 else c)


def _split_start(name, groups, after=None):
    pins = [] if after is None else [after]
    bufs, sem_shapes, spans = [], [], []
    for srcs, land_shapes, n_remote, n_local, _ in groups:
        sems = [pltpu.SemaphoreType.DMA((n_remote,)), pltpu.SemaphoreType.DMA((n_remote,))]
        sems += [pltpu.SemaphoreType.DMA((n_local,))] if n_local else []
        spans.append((len(bufs), len(srcs), len(land_shapes), len(sem_shapes), len(sems)))
        bufs += [pltpu.with_memory_space_constraint(a, pltpu.HBM) for a in srcs]
        bufs += [pltpu.with_memory_space_constraint(lax.empty(s.shape, s.dtype), pltpu.HBM) for s in land_shapes]
        sem_shapes += sems
    n_buf, n_sem = len(bufs), len(sem_shapes)

    def body(*refs):
        buf_refs, sem_refs, token = refs[:n_buf], refs[n_buf + len(pins):n_buf + len(pins) + n_sem], refs[-1]
        for (b0, ns, nl, s0, k), group in zip(spans, groups):
            remote, local = group[4](buf_refs[b0:b0 + ns], buf_refs[b0 + ns:b0 + ns + nl], *sem_refs[s0:s0 + k])
            for cp in local + remote:
                cp.start()
        token[...] = jnp.zeros_like(token)

    outs = pl.pallas_call(
        body, name=name,
        out_shape=sem_shapes + [pltpu.HBM(b.shape, b.dtype) for b in bufs] + [jax.ShapeDtypeStruct((SUBLANES, LANES), F32)],
        in_specs=[HBM] * n_buf + [ANY] * len(pins),
        out_specs=[SEM] * n_sem + [HBM] * n_buf + [pl.BlockSpec(memory_space=pltpu.VMEM)],
        input_output_aliases={i: n_sem + i for i in range(n_buf)},
        compiler_params=pltpu.CompilerParams(has_side_effects=SIDE_EFFECT),
    )(*bufs, *pins)
    return [dict(sems=list(outs[s0:s0 + k]), bufs=list(outs[n_sem + b0:n_sem + b0 + ns + nl]), token=outs[-1],
                 build=group[4], ns=ns) for (b0, ns, nl, s0, k), group in zip(spans, groups)]


def _split_wait(name, started, after):
    ns, n_buf, n_sem = started["ns"], len(started["bufs"]), len(started["sems"])

    def body(*refs):
        src_refs, land_refs = refs[:ns], refs[ns:n_buf]
        sems = refs[n_buf:n_buf + n_sem]
        remote, local = started["build"](src_refs, land_refs, *sems)
        for cp in local:
            cp.wait()
        for cp in remote:
            cp.wait_send()
            cp.wait_recv()

    outs = pl.pallas_call(
        body, name=name, out_shape=[pltpu.HBM(b.shape, b.dtype) for b in started["bufs"]],
        in_specs=[HBM] * n_buf + [SEM] * n_sem + [ANY], out_specs=[HBM] * n_buf,
        input_output_aliases={i: i for i in range(n_buf)},
        compiler_params=pltpu.CompilerParams(has_side_effects=SIDE_EFFECT),
    )(*started["bufs"], *started["sems"], after)
    return list(outs[:ns]), list(outs[ns:])


def _gather_group(shards):
    m = len(shards)

    def build(src_refs, land_refs, send_sems, recv_sems, local_sems):
        x, y, c = lax.axis_index("x"), lax.axis_index("y"), lax.axis_index("c")
        remote, local = [], []
        for j in range(m):
            r = src_refs[j].shape[0]
            dst = land_refs[j].at[pl.ds((4 * x + 2 * y + c) * r, r), :]
            local.append(pltpu.make_async_copy(src_refs[j], dst, local_sems.at[j]))
            for k in range(1, N_DEV):
                remote.append(pltpu.make_async_remote_copy(
                    src_ref=src_refs[j], dst_ref=dst, send_sem=send_sems.at[7 * j + k - 1],
                    recv_sem=recv_sems.at[7 * j + k - 1], device_id=_peer(k, x, y, c), device_id_type=MESH))
        return remote, local

    lands = [jax.ShapeDtypeStruct((N_DEV * a.shape[0], a.shape[1]), a.dtype) for a in shards]
    return shards, lands, 7 * m, m, build


def _slots_start(name, a):
    def build(src_refs, land_refs, send_sems, recv_sems, local_sems):
        x, y, c = lax.axis_index("x"), lax.axis_index("y"), lax.axis_index("c")
        dst = land_refs[0].at[4 * x + 2 * y + c]
        local = [pltpu.make_async_copy(src_refs[0], dst, local_sems.at[0])]
        remote = [pltpu.make_async_remote_copy(
            src_ref=src_refs[0], dst_ref=dst, send_sem=send_sems.at[k - 1], recv_sem=recv_sems.at[k - 1],
            device_id=_peer(k, x, y, c), device_id_type=MESH) for k in range(1, N_DEV)]
        return remote, local

    return _split_start(name, [([a], [jax.ShapeDtypeStruct((N_DEV,) + a.shape, a.dtype)], 7, 1, build)])[0]


def _chips_start(name, p):
    _, r, c = p.shape
    nck = r // GRAD_ROW_TILE

    def build(src_refs, land_refs, send_sems, recv_sems):
        x, y, cc = lax.axis_index("x"), lax.axis_index("y"), lax.axis_index("c")
        remote = []
        for k in range(1, 4):
            px = 1 - x if k >> 1 else x
            py = 1 - y if k & 1 else y
            for j in range(nck):
                rows = pl.ds(j * GRAD_ROW_TILE, GRAD_ROW_TILE)
                remote.append(pltpu.make_async_remote_copy(
                    src_ref=src_refs[0].at[2 * px + py, rows], dst_ref=land_refs[0].at[k - 1, rows],
                    send_sem=send_sems.at[(k - 1) * nck + j], recv_sem=recv_sems.at[(k - 1) * nck + j],
                    device_id=(px, py, cc), device_id_type=MESH))
        return remote, []

    return _split_start(name, [([p], [jax.ShapeDtypeStruct((3, r, c), p.dtype)], 3 * nck, 0, build)])[0]


def _chip_sum(name, p, recv, chip):
    _, r, c = p.shape
    tr = _pick(r, 5 * GRAD_ROW_TILE, GRAD_ROW_TILE)

    def body(chip_ref, p_ref, r_ref, o_ref):
        acc = p_ref[...].astype(F32)
        for k in range(3):
            acc = acc + r_ref[k].astype(F32)
        o_ref[...] = acc

    return pl.pallas_call(
        body, name=name,
        grid_spec=pltpu.PrefetchScalarGridSpec(
            num_scalar_prefetch=1, grid=(r // tr,),
            in_specs=[pl.BlockSpec((None, tr, c), lambda i, chip_ref: (chip_ref[0], i, 0)),
                      pl.BlockSpec((3, tr, c), lambda i, chip_ref: (0, i, 0))],
            out_specs=pl.BlockSpec((tr, c), lambda i, chip_ref: (i, 0))),
        out_shape=_out(r, c, F32), compiler_params=_params(("parallel",)),
    )(chip, p, recv)


def _local_step(x, mem, tgt, wt, sm, ev=None):
    t, d = x.shape
    n_mem = mem.shape[0]
    d_pool = sm["pool_scale"].shape[1]
    ng, pc = sm["pool_w"].shape[0], sm["pool_w"].shape[1]
    d_ssm = sm["ssm_d"].shape[1]
    _, sg, sp, sh = sm["ssm_b_re"].shape
    n_state = sg * sp
    gb, gs = {}, {}

    def emit(name, **kw):
        return ev(name, **kw) if ev is not None else None

    n1 = _rms_fwd("ffn1_norm", x, sm["ffn1_norm"])
    emit("ffn1_norm_done", marker=n1)
    def ffn1_down(hid):
        emit("ffn1_up_done", marker=hid)
        return wt["ffn1_w_down"]

    h1, ffn1_saved = _ffn_fwd("ffn1", x, n1, wt["ffn1_w_gate"], wt["ffn1_w_up"], ffn1_down)
    emit("ffn1_fwd_done", marker=h1)
    u = _rms_fwd("mix_norm", h1, sm["mix_norm"])
    d_in = wt["w_in"].shape[0]
    tm, tn = _pick(t, 1024), _pick(d_in, 1408)
    proj = _mm1("in_proj", "nt", u, wt["w_in"], t, d_in, tm, tn, F32)
    off_s = d_pool // d_ssm
    off_gp = (d_pool + d_ssm)
    off_gs = off_gp + d

    pool_w_bf = sm["pool_w"].astype(BF16)
    pooled, pm = _pool_fwd(proj, pool_w_bf, sm["pool_scale"])

    cols = [sm["ssm_a_re"].reshape(-1, 1), sm["ssm_a_im"].reshape(-1, 1),
            jnp.broadcast_to(sm["ssm_log_dt"][:, :, None], (2, sg, sp)).reshape(-1, 1),
            sm["ssm_b_re"].reshape(-1, sh), sm["ssm_b_im"].reshape(-1, sh)]
    abr, abi, bbr, bbi = _ssm_disc(cols)
    abr2, abi2 = abr.reshape(2, n_state), abi.reshape(2, n_state)
    bbr4, bbi4 = bbr.reshape(2, sg * sp, sh), bbi.reshape(2, sg * sp, sh)
    b_re = [_bd_in(bbr4[dr], sg, sp, sh).astype(BF16) for dr in range(2)]
    b_im = [_bd_in(bbi4[dr], sg, sp, sh).astype(BF16) for dr in range(2)]
    c_re = [_bd_out(sm["ssm_c_re"][dr], sg, sp, sh).astype(BF16) for dr in range(2)]
    c_im = [_bd_out(-sm["ssm_c_im"][dr], sg, sp, sh).astype(BF16) for dr in range(2)]
    sp32 = _to_segments(proj[:, d_pool:d_pool + d_ssm])
    xs, y_parts = [], []
    for dr in range(2):
        xr, xi, y_part = _ssm_fwd(f"ssm_fwd{dr}", sp32, b_re[dr], b_im[dr], c_re[dr], c_im[dr], abr2[dr:dr + 1],
                                  abi2[dr:dr + 1], reverse=(dr == 1))
        xs.append((xr, xi))
        y_parts.append(y_part)
    y = _from_segments(_ew("ssm_sum", lambda p0, p1, sv, dv: (p0 + p1 + sv * dv,), y_parts + [sp32], [F32],
                           rowvecs=[sm["ssm_d"]])[0])
    tmy = _pick(t, 256)
    ys = _ew("ssm_gelu", lambda v: (jax.nn.gelu(v),), [y], [BF16])[0]
    emit("mix_in_done", marker=ys)

    tmm, tnm, tnx = _pick(t, 1024), _pick(d, 256), _pick(d, 512)
    gp_spec = _tile(tmm, tnm, off_gp // tnm)
    gs_spec = _tile(tmm, tnm, off_gs // tnm)

    def merge_epi(accs, gpv, gsv):
        z_pool, val, gate = accs
        return (jax.nn.sigmoid(gpv) * z_pool + jax.nn.sigmoid(gsv) * (val * jax.nn.sigmoid(gate)),)

    merged = _mm("mix_merge", "nt", [pm, ys], [wt["w_pool_proj"], wt["w_glu_val"], wt["w_glu_gate"]],
                 [[(0, 0)], [(1, 1)], [(1, 2)]], t, d, tmm, tnm, [(proj, gp_spec), (proj, gs_spec)], merge_epi,
                 [(_out(t, d, BF16), None)])[0]
    res_epi = lambda accs, hin: (hin + accs[0],)
    h2 = _mm("mix_out", "nn", [merged], [wt["w_mix_out"]], [[(0, 0)]], t, d, tmm, tnx, [(h1, _tile(tmm, tnx))],
             res_epi, [(_out(t, d, F32), None)])[0]

    un = _rms_fwd("xattn_norm", h2, sm["xattn_norm"])
    mn = _rms_fwd("mem_norm", mem, sm["mem_norm"])
    emit("mix_done", marker=un)
    q = _mm1("xattn_q", "nn", un, wt["w_q"], t, d, tmm, tnx, BF16)
    kv = _mm1("xattn_kv", "nt", mn, wt["w_kv"], n_mem, 2 * d, n_mem, _pick(2 * d, 512), BF16)
    o = _attn_fwd(q, kv)
    h3 = _mm("xattn_out", "nn", [o], [wt["w_xo"]], [[(0, 0)]], t, d, tmm, tnx, [(h2, _tile(tmm, tnx))],
             res_epi, [(_out(t, d, F32), None)])[0]

    n2 = _rms_fwd("ffn2_norm", h3, sm["ffn2_norm"])
    emit("xattn_done", marker=n2)
    h4, ffn2_saved = _ffn_fwd("ffn2", h3, n2, wt["ffn2_w_gate"], wt["ffn2_w_up"], wt["ffn2_w_down"])

    dh4, dh4_bf, gs["final_norm"], loss = _loss_head(h4, sm["final_norm"], tgt)
    dh3, dh3_bf, gs["ffn2_norm"], gb["ffn2_w_gate"], gb["ffn2_w_up"], gb["ffn2_w_down"] = _ffn_bwd(
        "ffn2", h3, sm["ffn2_norm"], wt["ffn2_w_gate"], wt["ffn2_w_up"], wt["ffn2_w_down"], ffn2_saved, dh4, dh4_bf)

    tw = _pick(d, 1024)
    do = _mm1("xattn_do", "nt", dh3_bf, wt["w_xo"], t, d, tmm, tnx, BF16)
    gb["w_xo"] = _mm1("xattn_dwxo", "tn", o, dh3_bf, d, d, tw, tnx, BF16)
    dq, dkv = _attn_bwd(q, kv, do)
    gb["w_q"] = _mm1("xattn_dwq", "tn", un, dq, d, d, tw, tnx, BF16)
    dun = _mm1("xattn_dun", "nt", dq, wt["w_q"], t, d, tmm, tnx, F32)
    dh2, dh2_bf, gs["xattn_norm"] = _rms_bwd("xattn_norm_bwd", h2, sm["xattn_norm"], dun, dh3)
    gb["w_kv"] = _mm1("xattn_dwkv", "tn", dkv, mn, 2 * d, d, _pick(2 * d, 512), d, BF16)
    dmn = _mm1("xattn_dmn", "nn", dkv, wt["w_kv"], n_mem, d, n_mem, tnx, F32)
    gs["mem_norm"] = _rms_bwd("mem_norm_bwd", mem, sm["mem_norm"], dmn)

    gb["w_mix_out"] = _mm1("mix_dwout", "tn", merged, dh2_bf, d, d, tw, tnx, BF16)

    def merge_bwd_epi(accs, gpv, gsv):
        dmerged, z_pool, val, gate = accs
        sp_, ss_, sg_ = jax.nn.sigmoid(gpv), jax.nn.sigmoid(gsv), jax.nn.sigmoid(gate)
        glu = val * sg_
        dz_pool = dmerged * sp_
        dg_pool = dmerged * z_pool * (sp_ * (1.0 - sp_))
        dz_ssm = dmerged * ss_
        dg_ssm = dmerged * glu * (ss_ * (1.0 - ss_))
        dval = dz_ssm * sg_
        dgate = dz_ssm * glu * (1.0 - sg_)
        return dz_pool, dg_pool, dg_ssm, dval, dgate

    dz_pool, dg_pool, dg_ssm, dval, dgate = _mm(
        "mix_merge_bwd", "nt", [dh2_bf, pm, ys], [wt["w_mix_out"], wt["w_pool_proj"], wt["w_glu_val"], wt["w_glu_gate"]],
        [[(0, 0)], [(1, 1)], [(2, 2)], [(2, 3)]], t, d, tmm, tnm, [(proj, gp_spec), (proj, gs_spec)], merge_bwd_epi,
        [(_out(t, d, BF16), None)] * 5)
    gb["w_pool_proj"] = _mm1("pool_dwproj", "tn", dz_pool, pm, d, d_pool, tw, d_pool, BF16)
    gb["w_glu_val"] = _mm1("glu_dwval", "tn", dval, ys, d, d_ssm, tw, d_ssm, BF16)
    gb["w_glu_gate"] = _mm1("glu_dwgate", "tn", dgate, ys, d, d_ssm, tw, d_ssm, BF16)

    def gelu_bwd_epi(accs, yv):
        _, vjp = jax.vjp(jax.nn.gelu, yv)
        return (vjp(accs[0])[0],)

    dy = _mm("glu_dy", "nn", [dval, dgate], [wt["w_glu_val"], wt["w_glu_gate"]], [[(0, 0), (1, 1)]], t, d_ssm, tmy, d_ssm,
             [(y, _tile(tmy, d_ssm))], gelu_bwd_epi, [(_out(t, d_ssm, F32), None)])[0]
    gs["ssm_d"] = _colsum_prod("ssm_dd", dy, proj, b_coff=off_s)
    dyp = _to_segments(dy)
    d_abr, d_abi, d_bbr, d_bbi, d_cre, d_cim, lams = [], [], [], [], [], [], []
    ts = _pick(n_state, 512)
    tc_ = _pick(n_state, 256)
    both = lambda accs: tuple(accs)
    for dr in range(2):
        lr, li, dar, dai = _ssm_bwd(f"ssm_bwd{dr}", dyp, c_re[dr], c_im[dr], xs[dr][0], xs[dr][1], abr2[dr:dr + 1],
                                    abi2[dr:dr + 1], reverse=(dr == 1))
        d_abr.append(dar)
        d_abi.append(dai)
        lams += [lr, li]
        d_br, d_bi = _mm(f"ssm_db{dr}", "tn", [sp32], [lr, li], [[(0, 0)], [(0, 1)]], d_ssm, n_state, d_ssm, ts, [], both,
                         [(_out(d_ssm, n_state, F32), None)] * 2)
        d_bbr.append(_diag_in(d_br, sg, sp, sh))
        d_bbi.append(_diag_in(d_bi, sg, sp, sh))
        d_cr, d_ci = _mm(f"ssm_dc{dr}", "tn", [xs[dr][0], xs[dr][1]], [dyp], [[(0, 0)], [(1, 0)]], n_state, d_ssm, tc_,
                         d_ssm, [], both, [(_out(n_state, d_ssm, F32), None)] * 2)
        d_cre.append(_diag_out(d_cr, sg, sp, sh))
        d_cim.append(-_diag_out(d_ci, sg, sp, sh))
    ds = _from_segments(_mm(
        "ssm_ds", "nt", lams, [b_re[0], b_im[0], b_re[1], b_im[1]], [[(k, k) for k in range(4)]], t, d_ssm, tmy,
        d_ssm, [(dyp, _tile(tmy, d_ssm)), (sm["ssm_d"], _rowvec(d_ssm))],
        lambda accs, dyv, dv: (dyv * dv + accs[0],), [(_out(t, d_ssm, BF16), None)])[0])
    cots = [jnp.concatenate(d_abr, axis=0).reshape(-1, 1), jnp.concatenate(d_abi, axis=0).reshape(-1, 1),
            jnp.concatenate(d_bbr, axis=0), jnp.concatenate(d_bbi, axis=0)]
    d_are, d_aim, d_ldt, d_bre, d_bim = _ssm_disc_bwd(cols, cots)
    gs["ssm_a_re"] = d_are.reshape(2, sg, sp)
    gs["ssm_a_im"] = d_aim.reshape(2, sg, sp)
    gs["ssm_log_dt"] = _rowsum("ssm_dlogdt", d_ldt.reshape(2 * sg, sp)).reshape(2, sg)
    gs["ssm_b_re"] = d_bre.reshape(2, sg, sp, sh)
    gs["ssm_b_im"] = d_bim.reshape(2, sg, sp, sh)
    gs["ssm_c_re"] = jnp.stack(d_cre, axis=0)
    gs["ssm_c_im"] = jnp.stack(d_cim, axis=0)

    dpm = _mm1("pool_dpm", "nn", dz_pool, wt["w_pool_proj"], t, d_pool, tmm, _pick(d_pool, 256), F32)
    dp, gs["pool_w"], gs["pool_scale"] = _pool_bwd(pooled, dpm, pool_w_bf, sm["pool_scale"])

    w_in = wt["w_in"]
    parts = [(dp, 0, d_pool), (ds, d_pool, d_ssm), (dg_pool, off_gp, d), (dg_ssm, off_gs, d)]
    w_in_parts = [w_in[o0:o0 + width] for _, o0, width in parts]
    gb["w_in"] = jnp.concatenate(
        [_mm1(f"in_proj_dw{k}", "tn", p_[0], u, p_[2], d, _pick(p_[2], 1024), tnx, BF16) for k, p_ in enumerate(parts)], axis=0)
    pin = emit("grads_main", gb=gb)
    du = _mm("in_proj_du", "nn", [p_[0] for p_ in parts], w_in_parts, [[(k, k) for k in range(4)]], t, d, tmm, tnx, [],
             lambda accs: (accs[0],), [(_out(t, d, F32), None)], after=pin)[0]
    dh1, dh1_bf, gs["mix_norm"] = _rms_bwd("mix_norm_bwd", h1, sm["mix_norm"], du, dh2)
    pin = emit("small_early", gs=gs, loss=loss)

    def ffn1_weights_done(d_wg, d_wu, d_wd):
        gb["ffn1_w_gate"], gb["ffn1_w_up"], gb["ffn1_w_down"] = d_wg, d_wu, d_wd
        return emit("grads_ffn1", gb=gb)

    dx, _, gs["ffn1_norm"], _, _, _ = _ffn_bwd(
        "ffn1", x, sm["ffn1_norm"], wt["ffn1_w_gate"], wt["ffn1_w_up"], wt["ffn1_w_down"], ffn1_saved, dh1, dh1_bf,
        weights_done=ffn1_weights_done, after=pin)
    return loss, dx, gb, gs


WEIGHTS = ["ffn1_norm", "ffn1_w_gate", "ffn1_w_up", "ffn1_w_down", "mix_norm", "w_in", "pool_w", "pool_scale",
           "w_pool_proj", "ssm_a_re", "ssm_a_im", "ssm_log_dt", "ssm_b_re", "ssm_b_im", "ssm_c_re", "ssm_c_im", "ssm_d",
           "w_glu_val", "w_glu_gate", "w_mix_out", "xattn_norm", "mem_norm", "w_q", "w_kv", "w_xo", "ffn2_norm",
           "ffn2_w_gate", "ffn2_w_up", "ffn2_w_down", "final_norm"]
COL_SHARDED = ["ffn1_w_gate", "ffn1_w_up", "w_in", "w_pool_proj", "w_glu_val", "w_glu_gate", "w_kv", "ffn2_w_gate",
               "ffn2_w_up"]
ROW_SHARDED = ["ffn1_w_down", "w_mix_out", "w_q", "w_xo", "ffn2_w_down"]
BIG = [n for n in WEIGHTS if n in COL_SHARDED or n in ROW_SHARDED]
SMALL = [n for n in WEIGHTS if n not in BIG]
FFN1_BIG = ["ffn1_w_gate", "ffn1_w_up", "ffn1_w_down"]
MAIN_BIG = [n for n in BIG if n not in FFN1_BIG]
GATHER_PLAN = [("ffn1_up_done", ["ffn1_w_down"]), ("ffn1_fwd_done", ["w_in"]),
               ("mix_in_done", ["w_pool_proj", "w_glu_val", "w_glu_gate", "w_mix_out"]),
               ("mix_done", ["w_q", "w_kv", "w_xo"]), ("xattn_done", ["ffn2_w_gate", "ffn2_w_up", "ffn2_w_down"])]
LATE_SMALL = "ffn1_norm"
EARLY_SMALL = [n for n in SMALL if n != LATE_SMALL]
PACK_ROWS = SUBLANES * LANES
GRAD_ROW_TILE = 256


def _to_rows(name, w, width):
    if name in COL_SHARDED:
        w = w.T
    return w.reshape(-1, width)


def _pack_small(vals):
    flat = []
    for v in vals:
        f = v.reshape(-1)
        flat.append(jnp.pad(f, (0, (-f.shape[0]) % PACK_ROWS)))
    total = sum(f.shape[0] for f in flat)
    flat.append(jnp.zeros(((-total) % (GRAD_ROW_TILE * LANES),), F32))
    return jnp.concatenate(flat).reshape(-1, LANES)


def _unpack_small(packed, shapes):
    out, row = [], 0
    for shp in shapes:
        size = math.prod(shp)
        rows = -(-size // PACK_ROWS) * SUBLANES
        out.append(packed[row:row + rows].reshape(-1)[:size].reshape(shp))
        row += rows
    return out


def kernel(x, mem, ffn1_norm, ffn1_w_gate, ffn1_w_up, ffn1_w_down, mix_norm, w_in, pool_w, pool_scale, w_pool_proj, ssm_a_re, ssm_a_im, ssm_log_dt, ssm_b_re, ssm_b_im, ssm_c_re, ssm_c_im, ssm_d, w_glu_val, w_glu_gate, w_mix_out, xattn_norm, mem_norm, w_q, w_kv, w_xo, ffn2_norm, ffn2_w_gate, ffn2_w_up, ffn2_w_down, final_norm, loss_target, m_ffn1_norm, m_ffn1_w_gate, m_ffn1_w_up, m_ffn1_w_down, m_mix_norm, m_w_in, m_pool_w, m_pool_scale, m_w_pool_proj, m_ssm_a_re, m_ssm_a_im, m_ssm_log_dt, m_ssm_b_re, m_ssm_b_im, m_ssm_c_re, m_ssm_c_im, m_ssm_d, m_w_glu_val, m_w_glu_gate, m_w_mix_out, m_xattn_norm, m_mem_norm, m_w_q, m_w_kv, m_w_xo, m_ffn2_norm, m_ffn2_w_gate, m_ffn2_w_up, m_ffn2_w_down, m_final_norm, v_ffn1_norm, v_ffn1_w_gate, v_ffn1_w_up, v_ffn1_w_down, v_mix_norm, v_w_in, v_pool_w, v_pool_scale, v_w_pool_proj, v_ssm_a_re, v_ssm_a_im, v_ssm_log_dt, v_ssm_b_re, v_ssm_b_im, v_ssm_c_re, v_ssm_c_im, v_ssm_d, v_w_glu_val, v_w_glu_gate, v_w_mix_out, v_xattn_norm, v_mem_norm, v_w_q, v_w_kv, v_w_xo, v_ffn2_norm, v_ffn2_w_gate, v_ffn2_w_up, v_ffn2_w_down, v_final_norm):
    given = dict(locals())
    wts = {n: given[n] for n in WEIGHTS}
    moms = {n: (given["m_" + n], given["v_" + n]) for n in WEIGHTS}
    x2, mem2, tgt2 = x[0], mem[0], loss_target[0]
    d = x2.shape[1]
    chip = (2 * lax.axis_index("x") + lax.axis_index("y")).astype(jnp.int32).reshape(1)

    def full_form(n, f):
        shard = wts[n][0].shape
        return f.reshape(N_DEV * shard[1], shard[0]) if n in COL_SHARDED else f.reshape(N_DEV * shard[0], shard[1])

    shards = {n: _to_rows(n, wts[n][0], d).astype(BF16) for n in BIG}
    first = FFN1_BIG[:2]
    wt = {n: full_form(n, f) for n, f in zip(first, _allgather("weight_allgather_first", [shards[n] for n in first]))}
    started = _split_start("weight_gather_start", [_gather_group([shards[n] for n in names]) for _, names in GATHER_PLAN],
                           after=wt[first[0]])
    gathers = {event: (names, st) for (event, names), st in zip(GATHER_PLAN, started)}
    sm = {n: (wts[n].reshape(1, -1) if wts[n].ndim <= 2 else wts[n][0]) for n in SMALL}
    sm["ffn1_norm"] = sm["ffn1_norm"] + started[0]["token"][0, 0]

    pending = {}

    def reduce_start(tag, names, gb):
        blocks = [gb[n].reshape(N_DEV, -1, d) for n in names]
        pad_rows = (-sum(b.shape[1] for b in blocks)) % GRAD_ROW_TILE
        pad = [jnp.zeros((N_DEV, pad_rows, d), BF16)] if pad_rows else []
        recv = _exchange_cores("grad_exchange_cores_" + tag, blocks + pad)
        own = jnp.concatenate([lax.dynamic_index_in_dim(b.reshape(4, 2, b.shape[1], d), lax.axis_index("c"), 1, False)
                               for b in blocks + pad], axis=1)
        rows_all = own.shape[1]
        pair = _ew("grad_pair_sum_" + tag, lambda a, b: (a.astype(F32) + b.astype(F32),),
                   [own.reshape(-1, d), recv.reshape(-1, d)], [BF16], rows_pref=5 * GRAD_ROW_TILE)[0]
        pair = pair.reshape(4, rows_all, d)
        pending[tag] = (pair, _chips_start("grad_exchange_chips_start_" + tag, pair), [b.shape[1] for b in blocks])
        return pending[tag][1]["token"]

    def reduce_finish(tag, after):
        _, started, rows = pending[tag]
        (pair,), (recv,) = _split_wait("grad_exchange_chips_wait_" + tag, started, after)
        return _chip_sum("grad_chip_sum_" + tag, pair, recv, chip), rows

    def ev(name, gb=None, gs=None, loss=None, marker=None):
        if name in gathers:
            names, started = gathers[name]
            for n, f in zip(names, _split_wait("weight_gather_wait_" + name, started, marker)[1]):
                wt[n] = full_form(n, f)
        elif name == "grads_main":
            return reduce_start("main", MAIN_BIG, gb)
        elif name == "small_early":
            pending["small"] = _slots_start("small_gather_start", _pack_small([gs[n] for n in EARLY_SMALL] + [loss[:, :1]]))
            return pending["small"]["token"]
        elif name == "grads_ffn1":
            return reduce_start("ffn1", FFN1_BIG, gb)
        return None

    _, dx, _, gs = _local_step(x2, mem2, tgt2, wt, sm, ev)

    grads = {}
    for tag, names in (("main", MAIN_BIG), ("ffn1", FFN1_BIG)):
        g_rows, rows = reduce_finish(tag, dx)
        off = 0
        for n, r in zip(names, rows):
            shard = wts[n].shape
            grads[n] = g_rows[off:off + r].reshape((shard[2], shard[1]) if n in COL_SHARDED else shard[1:])
            off += r
    small_sum = _sum_slots("small_sum", _split_wait("small_gather_wait", pending["small"], dx)[1][0], F32)
    late = _allgather("small_allgather_late", [gs[LATE_SMALL].reshape(-1, LANES)])[0]
    late_sum = _sum_slots("small_sum_late", late.reshape(N_DEV, -1, LANES), F32)
    vals = _unpack_small(small_sum, [wts[n].shape for n in EARLY_SMALL] + [(1, 1)])
    total_loss = vals[-1].reshape(())
    for n, g_full in zip(EARLY_SMALL + [LATE_SMALL], vals[:-1] + [late_sum]):
        grads[n] = g_full.reshape(-1, wts[n].shape[-1])

    out_g, out_d, out_m, out_v = {}, {}, {}, {}
    by_shape = {}
    for n in WEIGHTS:
        by_shape.setdefault((wts[n].size // wts[n].shape[-1], wts[n].shape[-1], n in COL_SHARDED), []).append(n)
    for (_, _, transposed), names in by_shape.items():
        two_d = (-1, wts[names[0]].shape[-1])
        items = [(wts[n].reshape(two_d), grads[n], moms[n][0].reshape(two_d), moms[n][1].reshape(two_d)) for n in names]
        for n, res in zip(names, _adamw_group("adamw_" + names[0], items, transposed)):
            shape = wts[n].shape
            out_d[n], out_m[n], out_v[n], out_g[n] = (a.reshape(shape) for a in res)

    return (total_loss, dx[None], *[out_g[n] for n in WEIGHTS], *[out_d[n] for n in WEIGHTS],
            *[out_m[n] for n in WEIGHTS], *[out_v[n] for n in WEIGHTS])
```

```python
import functools
import math

import jax
import jax.numpy as jnp
from jax import lax
from jax.experimental import pallas as pl
from jax.experimental.pallas import tpu as pltpu

F32 = jnp.float32
BF16 = jnp.bfloat16
EPS = 1e-6
N_XHEADS = 4
POOL_WINDOWS = (2, 4, 8, 16)
ADAM_LR = 0.001
ADAM_B1 = 0.9
ADAM_B2 = 0.999
ADAM_EPS = 1e-08
ADAM_WD = 0.01
ADAM_STEP = 10
N_DEV = 8
VMEM_LIMIT_V7X = 48 * 1024 * 1024
LANES = 128
SUBLANES = 8
SUB_ROWS = 256
POOL_PAD = 16
MESH = pl.DeviceIdType.MESH
ANY = pl.BlockSpec(memory_space=pl.ANY)
HBM = pl.BlockSpec(memory_space=pltpu.HBM)
SEM = pl.BlockSpec(memory_space=pltpu.SEMAPHORE)
SIDE_EFFECT = pltpu.SideEffectType.DATAFLOW_SIDE_EFFECTING

_DIMS = {
    "nt": (((1,), (1,)), ((), ())),
    "nn": (((1,), (0,)), ((), ())),
    "tn": (((0,), (0,)), ((), ())),
}


def _pick(dim, pref, mult=LANES):
    if dim <= pref:
        return dim
    for t in range(pref - pref % mult, 0, -mult):
        if dim % t == 0:
            return t
    return dim


def _params(sem):
    return pltpu.CompilerParams(dimension_semantics=sem, vmem_limit_bytes=VMEM_LIMIT_V7X)


def _tile(tm, tn, coff=0):
    return pl.BlockSpec((tm, tn), lambda i, j: (i, j + coff))


def _rowvec(tn, coff=0):
    return pl.BlockSpec((1, tn), lambda i, j: (0, j + coff))


def _out(m, n, dtype):
    return jax.ShapeDtypeStruct((m, n), dtype)


def _mm(name, form, a_list, b_list, groups, m, n, tm, tn, extras, epilogue, outs, after=None, sub=SUB_ROWS):
    na, nb, ne = len(a_list), len(b_list), len(extras)
    pins = [] if after is None else [after]
    step = tm if (sub is None or form == "tn" or tm % sub) else sub

    def a_spec(a):
        if form == "tn":
            return pl.BlockSpec((a.shape[0], tm), lambda i, j: (0, i))
        return pl.BlockSpec((tm, a.shape[1]), lambda i, j: (i, 0))

    def b_spec(b):
        if form == "nt":
            return pl.BlockSpec((tn, b.shape[1]), lambda i, j: (j, 0))
        return pl.BlockSpec((b.shape[0], tn), lambda i, j: (0, j))

    def body(*refs):
        a_refs, b_refs = refs[:na], refs[na:na + nb]
        e_refs, o_refs = refs[na + nb:na + nb + ne], refs[na + nb + ne + len(pins):]
        b_vals = {}
        for s0 in range(0, tm, step):
            rows = slice(None) if step == tm else pl.ds(s0, step)
            a_vals, accs = {}, []
            for group in groups:
                acc = None
                for ai, bi in group:
                    if ai not in a_vals:
                        a_vals[ai] = (a_refs[ai][...] if form == "tn" else a_refs[ai][rows, :]).astype(BF16)
                    if bi not in b_vals:
                        b_vals[bi] = b_refs[bi][...].astype(BF16)
                    d = lax.dot_general(a_vals[ai], b_vals[bi], _DIMS[form], preferred_element_type=F32)
                    acc = d if acc is None else acc + d
                accs.append(acc)
            res = epilogue(accs, *[e[rows, :] if e.shape[0] == tm else e[...] for e in e_refs])
            for o_ref, r in zip(o_refs, res):
                o_ref[rows, :] = r.astype(o_ref.dtype)

    out_specs = [_tile(tm, tn) if s is None else s for _, s in outs]
    res = pl.pallas_call(
        body, name=name, grid=(m // tm, n // tn),
        in_specs=[a_spec(a) for a in a_list] + [b_spec(b) for b in b_list] + [s for _, s in extras] + [ANY] * len(pins),
        out_specs=out_specs, out_shape=[o for o, _ in outs],
        compiler_params=_params(("parallel", "parallel")),
    )(*a_list, *b_list, *[e for e, _ in extras], *pins)
    return res


def _mm1(name, form, a, b, m, n, tm, tn, dtype, scale=None):
    epi = (lambda accs: (accs[0],)) if scale is None else (lambda accs: (accs[0] * scale,))
    return _mm(name, form, [a], [b], [[(0, 0)]], m, n, tm, tn, [], epi, [(_out(m, n, dtype), None)])[0]


def _rms_fwd(name, h, g):
    t, d = h.shape
    tm = _pick(t, 512, SUBLANES)

    def body(h_ref, g_ref, n_ref):
        hv = h_ref[...]
        r = lax.rsqrt(jnp.mean(hv * hv, axis=-1, keepdims=True) + EPS)
        n_ref[...] = ((hv * r) * g_ref[...]).astype(BF16)

    return pl.pallas_call(
        body, name=name, grid=(t // tm,),
        in_specs=[pl.BlockSpec((tm, d), lambda i: (i, 0)), pl.BlockSpec((1, d), lambda i: (0, 0))],
        out_specs=pl.BlockSpec((tm, d), lambda i: (i, 0)), out_shape=_out(t, d, BF16),
        compiler_params=_params(("parallel",)),
    )(h, g)


def _rms_bwd(name, h, g, dn, dres=None):
    t, d = h.shape
    tm = _pick(t, 512, SUBLANES)
    need_dh = dres is not None

    def body(*refs):
        if need_dh:
            h_ref, g_ref, dn_ref, dres_ref, dh_ref, dhb_ref, dg_ref = refs
        else:
            h_ref, g_ref, dn_ref, dg_ref = refs
        hv = h_ref[...]
        r = lax.rsqrt(jnp.mean(hv * hv, axis=-1, keepdims=True) + EPS)
        nh = hv * r
        dnv = dn_ref[...].astype(F32)

        @pl.when(pl.program_id(0) == 0)
        def _():
            dg_ref[...] = jnp.zeros_like(dg_ref)

        dg_ref[...] += jnp.sum(dnv * nh, axis=0, keepdims=True)
        if need_dh:
            dng = dnv * g_ref[...]
            dh = dres_ref[...] + r * (dng - nh * jnp.mean(dng * nh, axis=-1, keepdims=True))
            dh_ref[...] = dh
            dhb_ref[...] = dh.astype(BF16)

    row = pl.BlockSpec((tm, d), lambda i: (i, 0))
    vec = pl.BlockSpec((1, d), lambda i: (0, 0))
    if need_dh:
        return pl.pallas_call(
            body, name=name, grid=(t // tm,), in_specs=[row, vec, row, row], out_specs=[row, row, vec],
            out_shape=[_out(t, d, F32), _out(t, d, BF16), _out(1, d, F32)], compiler_params=_params(("arbitrary",)),
        )(h, g, dn, dres)
    return pl.pallas_call(
        body, name=name, grid=(t // tm,), in_specs=[row, vec, row], out_specs=vec,
        out_shape=_out(1, d, F32), compiler_params=_params(("arbitrary",)),
    )(h, g, dn)


def _loss_head(h, g, tgt):
    t, d = h.shape
    tm = _pick(t, 512, SUBLANES)

    def body(h_ref, g_ref, t_ref, dh_ref, dhb_ref, dg_ref, loss_ref):
        hv = h_ref[...]
        r = lax.rsqrt(jnp.mean(hv * hv, axis=-1, keepdims=True) + EPS)
        nh = hv * r
        err = nh * g_ref[...] - t_ref[...]

        @pl.when(pl.program_id(0) == 0)
        def _():
            dg_ref[...] = jnp.zeros_like(dg_ref)
            loss_ref[...] = jnp.zeros_like(loss_ref)

        per_row = jnp.mean(err * err, axis=-1, keepdims=True)
        loss_ref[...] += 0.5 * jnp.sum(per_row, axis=0, keepdims=True)
        dy = err * (1.0 / d)
        dg_ref[...] += jnp.sum(dy * nh, axis=0, keepdims=True)
        dng = dy * g_ref[...]
        dh = r * (dng - nh * jnp.mean(dng * nh, axis=-1, keepdims=True))
        dh_ref[...] = dh
        dhb_ref[...] = dh.astype(BF16)

    row = pl.BlockSpec((tm, d), lambda i: (i, 0))
    vec = pl.BlockSpec((1, d), lambda i: (0, 0))
    return pl.pallas_call(
        body, name="loss_head", grid=(t // tm,), in_specs=[row, vec, row],
        out_specs=[row, row, vec, pl.BlockSpec((1, LANES), lambda i: (0, 0))],
        out_shape=[_out(t, d, F32), _out(t, d, BF16), _out(1, d, F32), _out(1, LANES, F32)],
        compiler_params=_params(("arbitrary",)),
    )(h, g, tgt)


def _ffn_fwd(tag, h, n, wg_t, wu_t, wd):
    t, d = h.shape
    f = wg_t.shape[0]
    tm, tn = _pick(t, 1024), _pick(f, 1408)

    def up_epi(accs):
        a, b = accs
        return a, b, (a * jax.nn.sigmoid(a)) * b

    a, b, hid = _mm(tag + "_up", "nt", [n], [wg_t, wu_t], [[(0, 0)], [(0, 1)]], t, f, tm, tn, [], up_epi,
                    [(_out(t, f, BF16), None)] * 3)
    if callable(wd):
        wd = wd(hid)
    tm2, tn2 = _pick(t, 1024), _pick(d, 512)
    h_out = _mm(tag + "_down", "nn", [hid], [wd], [[(0, 0)]], t, d, tm2, tn2, [(h, _tile(tm2, tn2))],
                lambda accs, hin: (hin + 0.5 * accs[0],), [(_out(t, d, F32), None)])[0]
    return h_out, (n, a, b, hid)


def _ffn_bwd(tag, h, g, wg_t, wu_t, wd, saved, dh, dh_bf, weights_done=None, after=None):
    n, a, b, hid = saved
    t, d = h.shape
    f = wd.shape[0]
    tm, tn = _pick(t, 1024), _pick(f, 1408)

    def hid_epi(accs, av, bv):
        dhid = 0.5 * accs[0]
        av, bv = av.astype(F32), bv.astype(F32)
        sig = jax.nn.sigmoid(av)
        da = dhid * bv * (sig * (1.0 + av * (1.0 - sig)))
        db = dhid * (av * sig)
        return da, db

    da, db = _mm(tag + "_bwd_hid", "nt", [dh_bf], [wd], [[(0, 0)]], t, f, tm, tn,
                 [(a, _tile(tm, tn)), (b, _tile(tm, tn))], hid_epi, [(_out(t, f, BF16), None)] * 2, after=after)
    tw, tnw = _pick(f, 1408), _pick(d, 512)
    d_wd = _mm1(tag + "_dwd", "tn", hid, dh_bf, f, d, tw, tnw, BF16, scale=0.5)
    d_wg = _mm1(tag + "_dwg", "tn", da, n, f, d, tw, tnw, BF16)
    d_wu = _mm1(tag + "_dwu", "tn", db, n, f, d, tw, tnw, BF16)
    pin = weights_done(d_wg, d_wu, d_wd) if weights_done is not None else None
    tm2, tn2 = _pick(t, 1024), _pick(d, 512)
    dn = _mm(tag + "_dn", "nn", [da, db], [wg_t, wu_t], [[(0, 0), (1, 1)]], t, d, tm2, tn2, [],
             lambda accs: (accs[0],), [(_out(t, d, F32), None)], after=pin)[0]
    dh_in, dh_in_bf, dg = _rms_bwd(tag + "_norm_bwd", h, g, dn, dh)
    return dh_in, dh_in_bf, dg, d_wg, d_wu, d_wd


def _window_sum(win, offsets):
    n = win.shape[0]
    acc = None
    for j in offsets:
        term = win if j == 0 else pltpu.roll(win, (-j) % n, 0)
        acc = term if acc is None else acc + term
    return acc


def _pool_counts(r0, ch, c, left, right, t):
    pos = r0 + lax.broadcasted_iota(jnp.int32, (ch, c), 0)
    return (jnp.minimum(pos + right + 1, t) - jnp.maximum(pos - left, 0)).astype(F32)


def _pool_fwd(proj, pool_w_bf, pool_scale):
    t = proj.shape[0]
    ng, c, _ = pool_w_bf.shape
    ch = _pick(t, 256, SUBLANES)
    pad = POOL_PAD

    def body(p_ref, w_ref, s_ref, pooled_ref, pm_ref, buf):
        grp = pl.program_id(0)
        buf[pl.ds(0, pad), :] = jnp.zeros((pad, c), F32)
        buf[pl.ds(pad + t, pad), :] = jnp.zeros((pad, c), F32)

        def fill(ci, carry):
            r0 = pl.multiple_of(ci * ch, SUBLANES)
            buf[pl.ds(pl.multiple_of(r0 + pad, SUBLANES), ch), :] = p_ref[pl.ds(r0, ch), :]
            return carry

        lax.fori_loop(0, t // ch, fill, 0)
        for gi, w in enumerate(POOL_WINDOWS):
            left = w // 2
            right = w - 1 - left

            @pl.when(grp == gi)
            def _(left=left, right=right):
                def chunk(ci, carry):
                    r0 = pl.multiple_of(ci * ch, SUBLANES)
                    win = buf[pl.ds(r0, ch + 2 * pad), :]
                    s = _window_sum(win, range(-left, right + 1))[pad:pad + ch]
                    pooled = s / _pool_counts(r0, ch, c, left, right, t) - win[pad:pad + ch]
                    pooled_bf = pooled.astype(BF16)
                    mixed = jnp.dot(pooled_bf, w_ref[0], preferred_element_type=F32)
                    pooled_ref[pl.ds(r0, ch), :] = pooled_bf
                    pm_ref[pl.ds(r0, ch), :] = (mixed * s_ref[...]).astype(BF16)
                    return carry

                lax.fori_loop(0, t // ch, chunk, 0)

    col = pl.BlockSpec((t, c), lambda g: (0, g))
    return pl.pallas_call(
        body, name="pool_fwd", grid=(ng,),
        in_specs=[col, pl.BlockSpec((1, c, c), lambda g: (g, 0, 0)), pl.BlockSpec((1, c), lambda g: (0, g))],
        out_specs=[col, col], out_shape=[_out(t, ng * c, BF16), _out(t, ng * c, BF16)],
        scratch_shapes=[pltpu.VMEM((t + 2 * pad, c), F32)],
        compiler_params=_params(("parallel",)),
    )(proj, pool_w_bf, pool_scale)


def _pool_bwd(pooled, dpm, pool_w_bf, pool_scale):
    t = pooled.shape[0]
    ng, c, _ = pool_w_bf.shape
    ch = _pick(t, 256, SUBLANES)
    pad = POOL_PAD

    def body(pooled_ref, dpm_ref, w_ref, s_ref, dp_ref, dw_ref, ds_ref, buf, raw):
        grp = pl.program_id(0)
        buf[pl.ds(0, pad), :] = jnp.zeros((pad, c), F32)
        buf[pl.ds(pad + t, pad), :] = jnp.zeros((pad, c), F32)
        dw_ref[...] = jnp.zeros_like(dw_ref)
        ds_ref[...] = jnp.zeros_like(ds_ref)
        for gi, w in enumerate(POOL_WINDOWS):
            left = w // 2
            right = w - 1 - left

            @pl.when(grp == gi)
            def _(left=left, right=right):
                def first(ci, carry):
                    r0 = pl.multiple_of(ci * ch, SUBLANES)
                    pv = pooled_ref[pl.ds(r0, ch), :]
                    dpm_v = dpm_ref[pl.ds(r0, ch), :]
                    mixed = jnp.dot(pv, w_ref[0], preferred_element_type=F32)
                    ds_ref[...] += jnp.sum(dpm_v * mixed, axis=0, keepdims=True)
                    dmixed = (dpm_v * s_ref[...]).astype(BF16)
                    dw_ref[0] += lax.dot_general(pv, dmixed, _DIMS["tn"], preferred_element_type=F32)
                    dpooled = lax.dot_general(dmixed, w_ref[0], _DIMS["nt"], preferred_element_type=F32)
                    raw[pl.ds(r0, ch), :] = dpooled
                    buf[pl.ds(pl.multiple_of(r0 + pad, SUBLANES), ch), :] = (
                        dpooled / _pool_counts(r0, ch, c, left, right, t))
                    return carry

                lax.fori_loop(0, t // ch, first, 0)

                def second(ci, carry):
                    r0 = pl.multiple_of(ci * ch, SUBLANES)
                    win = buf[pl.ds(r0, ch + 2 * pad), :]
                    s = _window_sum(win, range(-right, left + 1))[pad:pad + ch]
                    dp_ref[pl.ds(r0, ch), :] = (s - raw[pl.ds(r0, ch), :]).astype(BF16)
                    return carry

                lax.fori_loop(0, t // ch, second, 0)

    col = pl.BlockSpec((t, c), lambda g: (0, g))
    return pl.pallas_call(
        body, name="pool_bwd", grid=(ng,),
        in_specs=[col, col, pl.BlockSpec((1, c, c), lambda g: (g, 0, 0)), pl.BlockSpec((1, c), lambda g: (0, g))],
        out_specs=[col, pl.BlockSpec((1, c, c), lambda g: (g, 0, 0)), pl.BlockSpec((1, c), lambda g: (0, g))],
        out_shape=[_out(t, ng * c, BF16), jax.ShapeDtypeStruct((ng, c, c), F32), _out(1, ng * c, F32)],
        scratch_shapes=[pltpu.VMEM((t + 2 * pad, c), F32), pltpu.VMEM((t, c), F32)],
        compiler_params=_params(("parallel",)),
    )(pooled, dpm, pool_w_bf, pool_scale)


def _discretise(a_re, a_im, log_dt, b_re, b_im):
    dt = jnp.exp(log_dt)
    mag = jnp.exp(dt * a_re)
    ang = dt * a_im
    abr = mag * jnp.cos(ang)
    abi = mag * jnp.sin(ang)
    den = a_re * a_re + a_im * a_im
    nr = abr - 1.0
    qr = (nr * a_re + abi * a_im) / den
    qi = (abi * a_re - nr * a_im) / den
    return abr, abi, qr * b_re - qi * b_im, qr * b_im + qi * b_re


def _ssm_disc(cols):
    n, hh = cols[3].shape

    def body(ar, ai, ld, br, bi, o1, o2, o3, o4):
        res = _discretise(ar[...], ai[...], ld[...], br[...], bi[...])
        for o, r in zip((o1, o2, o3, o4), res):
            o[...] = r

    return pl.pallas_call(
        body, name="ssm_disc",
        out_shape=[_out(n, 1, F32), _out(n, 1, F32), _out(n, hh, F32), _out(n, hh, F32)],
    )(*cols)


def _ssm_disc_bwd(cols, cots):
    n, hh = cols[3].shape

    def body(ar, ai, ld, br, bi, c1, c2, c3, c4, o1, o2, o3, o4, o5):
        _, vjp = jax.vjp(_discretise, ar[...], ai[...], ld[...], br[...], bi[...])
        res = vjp((c1[...], c2[...], c3[...], c4[...]))
        for o, r in zip((o1, o2, o3, o4, o5), res):
            o[...] = r

    return pl.pallas_call(
        body, name="ssm_disc_bwd",
        out_shape=[_out(n, 1, F32)] * 3 + [_out(n, hh, F32)] * 2,
    )(*cols, *cots)


def _rowsum(name, a):
    r, _ = a.shape

    def body(a_ref, o_ref):
        o_ref[...] = jnp.sum(a_ref[...], axis=-1, keepdims=True)

    return pl.pallas_call(body, name=name, out_shape=_out(r, 1, F32))(a)


def _cmul(pr, pi, qr, qi):
    return pr * qr - pi * qi, pr * qi + pi * qr


def _cpow(pr, pi, n):
    rr, ri = None, None
    while n:
        if n & 1:
            rr, ri = (pr, pi) if rr is None else _cmul(rr, ri, pr, pi)
        n >>= 1
        if n:
            pr, pi = _cmul(pr, pi, pr, pi)
    return rr, ri


def _segment_carry(er, ei, pr, pi, reverse):
    row = lax.broadcasted_iota(jnp.int32, er.shape, 0)
    cr, ci = jnp.zeros_like(er), jnp.zeros_like(ei)
    for _ in range(SUBLANES - 1):
        tr = er + pr * cr - pi * ci
        ti = ei + pr * ci + pi * cr
        if reverse:
            keep, shift = row < SUBLANES - 1, SUBLANES - 1
        else:
            keep, shift = row >= 1, 1
        cr = jnp.where(keep, pltpu.roll(tr, shift, 0), 0.0)
        ci = jnp.where(keep, pltpu.roll(ti, shift, 0), 0.0)
    return cr, ci


def _ssm_fwd(name, sp, b_re, b_im, c_re, c_im, ar, ai, reverse):
    t, c = sp.shape
    s = ar.shape[1]
    w = _pick(s, 512)
    ch = _pick(t, 512, SUBLANES)
    n_ch, gpc, steps = t // ch, ch // SUBLANES, t // SUBLANES

    def body(sp_ref, bre_ref, bim_ref, cre_ref, cim_ref, ar_ref, ai_ref, xr_ref, xi_ref, y_ref, ur, ui, xbr, xbi):
        a_r = jnp.broadcast_to(ar_ref[...], (SUBLANES, w))
        a_i = jnp.broadcast_to(ai_ref[...], (SUBLANES, w))

        @pl.when(pl.program_id(0) == 0)
        def _():
            y_ref[...] = jnp.zeros_like(y_ref)

        def sweep(h0, store):
            def chunk(k, h):
                ci = n_ch - 1 - k if reverse else k
                rows = pl.ds(pl.multiple_of(ci * ch, ch), ch)
                spv = sp_ref[rows, :].astype(BF16)
                ur[...] = jnp.dot(spv, bre_ref[...], preferred_element_type=F32)
                ui[...] = jnp.dot(spv, bim_ref[...], preferred_element_type=F32)

                def group(g, hh):
                    gi = gpc - 1 - g if reverse else g
                    r0 = pl.multiple_of(gi * SUBLANES, SUBLANES)
                    hr, hi = hh
                    nr = a_r * hr - a_i * hi + ur[pl.ds(r0, SUBLANES), :]
                    ni = a_r * hi + a_i * hr + ui[pl.ds(r0, SUBLANES), :]
                    if store:
                        xbr[pl.ds(r0, SUBLANES), :] = nr
                        xbi[pl.ds(r0, SUBLANES), :] = ni
                    return nr, ni

                h = lax.fori_loop(0, gpc, group, h)
                if store:
                    xr16, xi16 = xbr[...].astype(BF16), xbi[...].astype(BF16)
                    xr_ref[rows, :] = xr16
                    xi_ref[rows, :] = xi16
                    y_ref[rows, :] += (jnp.dot(xr16, cre_ref[...], preferred_element_type=F32)
                                       + jnp.dot(xi16, cim_ref[...], preferred_element_type=F32))
                return h

            return lax.fori_loop(0, n_ch, chunk, h0)

        zero = jnp.zeros((SUBLANES, w), F32)
        er, ei = sweep((zero, zero), False)
        pr, pi = _cpow(ar_ref[...], ai_ref[...], steps)
        sweep(_segment_carry(er, ei, pr, pi, reverse), True)

    col = lambda i: (0, i)
    return pl.pallas_call(
        body, name=name, grid=(s // w,),
        in_specs=[pl.BlockSpec((t, c), lambda i: (0, 0)), pl.BlockSpec((c, w), col), pl.BlockSpec((c, w), col),
                  pl.BlockSpec((w, c), lambda i: (i, 0)), pl.BlockSpec((w, c), lambda i: (i, 0)),
                  pl.BlockSpec((1, w), col), pl.BlockSpec((1, w), col)],
        out_specs=[pl.BlockSpec((t, w), col), pl.BlockSpec((t, w), col), pl.BlockSpec((t, c), lambda i: (0, 0))],
        out_shape=[_out(t, s, BF16), _out(t, s, BF16), _out(t, c, F32)],
        scratch_shapes=[pltpu.VMEM((ch, w), F32)] * 4,
        compiler_params=_params(("arbitrary",)),
    )(sp, b_re, b_im, c_re, c_im, ar, ai)


def _ssm_bwd(name, dyp, c_re, c_im, xr, xi, ar, ai, reverse):
    t, c = dyp.shape
    s = ar.shape[1]
    w = _pick(s, 512)
    ch = _pick(t, 512, SUBLANES)
    n_ch, gpc, steps = t // ch, ch // SUBLANES, t // SUBLANES
    back = not reverse
    edge = 2 * SUBLANES

    def body(dy_ref, cre_ref, cim_ref, xr_ref, xi_ref, ar_ref, ai_ref, lr_ref, li_ref, dar_ref, dai_ref,
             gr, gi_, lbr, lbi, xbr, xbi):
        a_r = jnp.broadcast_to(ar_ref[...], (SUBLANES, w))
        a_i = -jnp.broadcast_to(ai_ref[...], (SUBLANES, w))
        row = lax.broadcasted_iota(jnp.int32, (SUBLANES, w), 0)

        def neighbours(ci, x_ref, buf):
            rows = pl.ds(pl.multiple_of(ci * ch, ch), ch)
            if reverse:
                buf[pl.ds(0, ch), :] = x_ref[rows, :].astype(F32)
                nxt = x_ref[pl.ds(pl.multiple_of(jnp.minimum(ci + 1, n_ch - 1) * ch, ch), edge), :].astype(F32)[:SUBLANES]
                first = x_ref[pl.ds(0, edge), :].astype(F32)[:SUBLANES]
                wrap = jnp.where(row < SUBLANES - 1, pltpu.roll(first, SUBLANES - 1, 0), 0.0)
                buf[pl.ds(ch, SUBLANES), :] = jnp.where(ci == n_ch - 1, wrap, nxt)
            else:
                buf[pl.ds(SUBLANES, ch), :] = x_ref[rows, :].astype(F32)
                prv = x_ref[pl.ds(pl.multiple_of(jnp.maximum(ci * ch - edge, 0), edge), edge), :].astype(F32)[SUBLANES:]
                last = x_ref[pl.ds(t - edge, edge), :].astype(F32)[SUBLANES:]
                wrap = jnp.where(row >= 1, pltpu.roll(last, 1, 0), 0.0)
                buf[pl.ds(0, SUBLANES), :] = jnp.where(ci == 0, wrap, prv)

        def sweep(h0, store):
            def chunk(k, carry):
                ci = n_ch - 1 - k if back else k
                rows = pl.ds(pl.multiple_of(ci * ch, ch), ch)
                dyv = dy_ref[rows, :].astype(BF16)
                gr[...] = lax.dot_general(dyv, cre_ref[...], _DIMS["nt"], preferred_element_type=F32)
                gi_[...] = lax.dot_general(dyv, cim_ref[...], _DIMS["nt"], preferred_element_type=F32)
                if store:
                    neighbours(ci, xr_ref, xbr)
                    neighbours(ci, xi_ref, xbi)

                def group(g, cc):
                    gidx = gpc - 1 - g if back else g
                    r0 = pl.multiple_of(gidx * SUBLANES, SUBLANES)
                    hr, hi = cc[0], cc[1]
                    nr = a_r * hr - a_i * hi + gr[pl.ds(r0, SUBLANES), :]
                    ni = a_r * hi + a_i * hr + gi_[pl.ds(r0, SUBLANES), :]
                    if not store:
                        return nr, ni
                    lbr[pl.ds(r0, SUBLANES), :] = nr
                    lbi[pl.ds(r0, SUBLANES), :] = ni
                    x0 = pl.multiple_of(r0 + SUBLANES, SUBLANES) if reverse else r0
                    xpr, xpi = xbr[pl.ds(x0, SUBLANES), :], xbi[pl.ds(x0, SUBLANES), :]
                    return nr, ni, cc[2] + nr * xpr + ni * xpi, cc[3] + ni * xpr - nr * xpi

                carry = lax.fori_loop(0, gpc, group, carry)
                if store:
                    lr_ref[rows, :] = lbr[...].astype(BF16)
                    li_ref[rows, :] = lbi[...].astype(BF16)
                return carry

            return lax.fori_loop(0, n_ch, chunk, h0)

        zero = jnp.zeros((SUBLANES, w), F32)
        er, ei = sweep((zero, zero), False)
        pr, pi = _cpow(ar_ref[...], -ai_ref[...], steps)
        cr, ci0 = _segment_carry(er, ei, pr, pi, back)
        _, _, dar, dai = sweep((cr, ci0, zero, zero), True)
        dar_ref[...] = jnp.sum(dar, axis=0, keepdims=True)
        dai_ref[...] = jnp.sum(dai, axis=0, keepdims=True)

    col = lambda i: (0, i)
    return pl.pallas_call(
        body, name=name, grid=(s // w,),
        in_specs=[pl.BlockSpec((t, c), lambda i: (0, 0)), pl.BlockSpec((w, c), lambda i: (i, 0)),
                  pl.BlockSpec((w, c), lambda i: (i, 0)), pl.BlockSpec((t, w), col), pl.BlockSpec((t, w), col),
                  pl.BlockSpec((1, w), col), pl.BlockSpec((1, w), col)],
        out_specs=[pl.BlockSpec((t, w), col), pl.BlockSpec((t, w), col), pl.BlockSpec((1, w), col), pl.BlockSpec((1, w), col)],
        out_shape=[_out(t, s, BF16), _out(t, s, BF16), _out(1, s, F32), _out(1, s, F32)],
        scratch_shapes=[pltpu.VMEM((ch, w), F32)] * 4 + [pltpu.VMEM((ch + SUBLANES, w), F32)] * 2,
        compiler_params=_params(("parallel",)),
    )(dyp, c_re, c_im, xr, xi, ar, ai)


def _to_segments(a):
    t, c = a.shape
    return a.reshape(SUBLANES, t // SUBLANES, c).transpose(1, 0, 2).reshape(t, c)


def _from_segments(a):
    t, c = a.shape
    return a.reshape(t // SUBLANES, SUBLANES, c).transpose(1, 0, 2).reshape(t, c)


def _colsum_prod(name, a, b, b_coff=0):
    t, n = a.shape
    tm = _pick(t, 512, SUBLANES)

    def body(a_ref, b_ref, o_ref):
        @pl.when(pl.program_id(0) == 0)
        def _():
            o_ref[...] = jnp.zeros_like(o_ref)

        o_ref[...] += jnp.sum(a_ref[...].astype(F32) * b_ref[...].astype(F32), axis=0, keepdims=True)

    return pl.pallas_call(
        body, name=name, grid=(t // tm,),
        in_specs=[pl.BlockSpec((tm, n), lambda i: (i, 0)), pl.BlockSpec((tm, n), lambda i: (i, b_coff))],
        out_specs=pl.BlockSpec((1, n), lambda i: (0, 0)), out_shape=_out(1, n, F32),
        compiler_params=_params(("arbitrary",)),
    )(a, b)


def _bd_in(bb, g, p, hh):
    blk = bb.reshape(g, p, hh).transpose(0, 2, 1)
    eye = jnp.eye(g, dtype=bool)[:, None, :, None]
    return jnp.where(eye, blk[:, :, None, :], 0.0).reshape(g * hh, g * p)


def _bd_out(cc, g, p, hh):
    blk = cc.transpose(0, 2, 1)
    eye = jnp.eye(g, dtype=bool)[:, None, :, None]
    return jnp.where(eye, blk[:, :, None, :], 0.0).reshape(g * p, g * hh)


def _diag_in(dmat, g, p, hh):
    eye = jnp.eye(g, dtype=bool)[:, None, :, None]
    diag = jnp.sum(jnp.where(eye, dmat.reshape(g, hh, g, p), 0.0), axis=2)
    return diag.transpose(0, 2, 1).reshape(g * p, hh)


def _diag_out(dmat, g, p, hh):
    eye = jnp.eye(g, dtype=bool)[:, None, :, None]
    diag = jnp.sum(jnp.where(eye, dmat.reshape(g, p, g, hh), 0.0), axis=2)
    return diag.transpose(0, 2, 1)


def _softmax(qh, kh, scale):
    s = lax.dot_general(qh, kh, _DIMS["nt"], preferred_element_type=F32) * scale
    e = jnp.exp(s - jnp.max(s, axis=-1, keepdims=True))
    return e / jnp.sum(e, axis=-1, keepdims=True)


def _attn_fwd(q, kv):
    t, d = q.shape
    mm_ = kv.shape[0]
    hd = d // N_XHEADS
    scale = 1.0 / math.sqrt(hd)
    tm = _pick(t, 512, SUBLANES)

    def body(q_ref, kv_ref, o_ref):
        for h in range(N_XHEADS):
            sl = pl.ds(h * hd, hd)
            p = _softmax(q_ref[:, sl], kv_ref[:, sl], scale)
            o_ref[:, sl] = jnp.dot(p.astype(BF16), kv_ref[:, pl.ds(d + h * hd, hd)],
                                   preferred_element_type=F32).astype(BF16)

    return pl.pallas_call(
        body, name="attn_fwd", grid=(t // tm,),
        in_specs=[pl.BlockSpec((tm, d), lambda i: (i, 0)), pl.BlockSpec((mm_, 2 * d), lambda i: (0, 0))],
        out_specs=pl.BlockSpec((tm, d), lambda i: (i, 0)), out_shape=_out(t, d, BF16),
        compiler_params=_params(("parallel",)),
    )(q, kv)


def _attn_bwd(q, kv, do):
    t, d = q.shape
    mm_ = kv.shape[0]
    hd = d // N_XHEADS
    scale = 1.0 / math.sqrt(hd)
    tm = _pick(t, 512, SUBLANES)

    def body(q_ref, kv_ref, do_ref, dq_ref, dkv_ref):
        @pl.when(pl.program_id(0) == 0)
        def _():
            dkv_ref[...] = jnp.zeros_like(dkv_ref)

        for h in range(N_XHEADS):
            sl = pl.ds(h * hd, hd)
            vsl = pl.ds(d + h * hd, hd)
            qh, kh, doh = q_ref[:, sl], kv_ref[:, sl], do_ref[:, sl]
            p = _softmax(qh, kh, scale)
            dp = lax.dot_general(doh, kv_ref[:, vsl], _DIMS["nt"], preferred_element_type=F32)
            dkv_ref[:, vsl] += lax.dot_general(p.astype(BF16), doh, _DIMS["tn"], preferred_element_type=F32)
            ds = (p * (dp - jnp.sum(dp * p, axis=-1, keepdims=True)) * scale).astype(BF16)
            dq_ref[:, sl] = jnp.dot(ds, kh, preferred_element_type=F32).astype(BF16)
            dkv_ref[:, sl] += lax.dot_general(ds, qh, _DIMS["tn"], preferred_element_type=F32)

    row = pl.BlockSpec((tm, d), lambda i: (i, 0))
    full = pl.BlockSpec((mm_, 2 * d), lambda i: (0, 0))
    return pl.pallas_call(
        body, name="attn_bwd", grid=(t // tm,), in_specs=[row, full, row], out_specs=[row, full],
        out_shape=[_out(t, d, BF16), _out(mm_, 2 * d, F32)], compiler_params=_params(("arbitrary",)),
    )(q, kv, do)


def _ew(name, fn, ins, outs, rows_pref=256, rowvecs=()):
    r, c = ins[0].shape
    tr = _pick(r, rows_pref, SUBLANES)
    ni = len(ins) + len(rowvecs)

    def body(*refs):
        res = fn(*[x[...] for x in refs[:ni]])
        for o_ref, v in zip(refs[ni:], res):
            o_ref[...] = v.astype(o_ref.dtype)

    blk = pl.BlockSpec((tr, c), lambda i: (i, 0))
    vec = pl.BlockSpec((1, c), lambda i: (0, 0))
    return pl.pallas_call(
        body, name=name, grid=(r // tr,), in_specs=[blk] * len(ins) + [vec] * len(rowvecs), out_specs=[blk] * len(outs),
        out_shape=[_out(r, c, dt) for dt in outs], compiler_params=_params(("parallel",)),
    )(*ins, *rowvecs)


def _sum_slots(name, a, dtype):
    s, r, c = a.shape
    tr = _pick(r, 256, SUBLANES)

    def body(a_ref, o_ref):
        acc = a_ref[0].astype(F32)
        for k in range(1, s):
            acc = acc + a_ref[k].astype(F32)
        o_ref[...] = acc.astype(o_ref.dtype)

    return pl.pallas_call(
        body, name=name, grid=(r // tr,), in_specs=[pl.BlockSpec((s, tr, c), lambda i: (0, i, 0))],
        out_specs=pl.BlockSpec((tr, c), lambda i: (i, 0)), out_shape=_out(r, c, dtype),
        compiler_params=_params(("parallel",)),
    )(a)


def _adamw_step(wv, gv, mv, vv):
    bc1 = 1.0 - ADAM_B1 ** ADAM_STEP
    bc2 = 1.0 - ADAM_B2 ** ADAM_STEP
    m2 = ADAM_B1 * mv + (1.0 - ADAM_B1) * gv
    v2 = ADAM_B2 * vv + (1.0 - ADAM_B2) * (gv * gv)
    delta = -ADAM_LR * ((m2 / bc1) / (jnp.sqrt(v2 / bc2) + ADAM_EPS) + ADAM_WD * wv)
    return delta, m2, v2


def _adamw_group(name, items, transposed):
    k, r = items[0][0].shape
    if transposed and r % LANES != 0:
        rows = _adamw_group(name, [(w.T, g, m.T, v.T) for w, g, m, v in items], False)
        return [[a.T for a in item] for item in rows]
    tk = _pick(k, max(SUBLANES, ADAMW_STEP_WORDS // (r * len(items))), SUBLANES)
    n_out = 4 if transposed else 3

    def body(*refs):
        ins, outs = refs[:4 * len(items)], refs[4 * len(items):]
        for i in range(len(items)):
            wv, gv, mv, vv = (a[...] for a in ins[4 * i:4 * i + 4])
            if transposed:
                gv = gv.T
            res = _adamw_step(wv, gv, mv, vv) + ((gv,) if transposed else ())
            for o_ref, val in zip(outs[n_out * i:n_out * (i + 1)], res):
                o_ref[...] = val

    blk = pl.BlockSpec((tk, r), lambda j: (j, 0))
    g_blk = pl.BlockSpec((r, tk), lambda j: (0, j)) if transposed else blk
    res = pl.pallas_call(
        body, name=name, grid=(k // tk,), in_specs=[blk, g_blk, blk, blk] * len(items),
        out_specs=[blk] * (n_out * len(items)), out_shape=[pltpu.HBM((k, r), F32)] * (n_out * len(items)),
        compiler_params=_params(("parallel",)),
    )(*[pltpu.with_memory_space_constraint(a, pltpu.HBM) for item in items for a in item])
    return [list(res[n_out * i:n_out * (i + 1)]) + ([] if transposed else [items[i][1]]) for i in range(len(items))]


def _allgather(name, arrs):
    n = len(arrs)

    def body(*refs):
        ins, outs = refs[:n], refs[n:2 * n]
        send_sems, recv_sems, local_sems = refs[2 * n:]
        x, y, c = lax.axis_index("x"), lax.axis_index("y"), lax.axis_index("c")
        me, sibling = (x, y, c), (x, y, 1 - c)
        chips = [(1 - x, y), (x, 1 - y), (1 - x, 1 - y)]

        def rows(a, px, py, pc):
            r = ins[a].shape[0]
            return outs[a].at[pl.ds((4 * px + 2 * py + pc) * r, r), :]

        def copy(a, k, block, to, src=None):
            return pltpu.make_async_remote_copy(
                src_ref=rows(a, *block) if src is None else src, dst_ref=rows(a, *block),
                send_sem=send_sems.at[a, k], recv_sem=recv_sems.at[a, k], device_id=to, device_id_type=MESH)

        mine = [pltpu.make_async_copy(ins[a], rows(a, *me), local_sems.at[a]) for a in range(n)]
        for cp in mine:
            cp.start()
        first = []
        for a in range(n):
            first.append(copy(a, 0, me, sibling, src=ins[a]))
            first += [copy(a, 1 + j, me, (*chip, c), src=ins[a]) for j, chip in enumerate(chips)]
        for cp in first:
            cp.start()
        passed = []
        for j, chip in enumerate(chips):
            for a in range(n):
                copy(a, 1 + j, (*chip, c), me).wait_recv()
                cp = copy(a, 4 + j, (*chip, c), sibling)
                cp.start()
                passed.append(cp)
        for a in range(n):
            copy(a, 0, sibling, me).wait_recv()
            for j, chip in enumerate(chips):
                copy(a, 4 + j, (*chip, 1 - c), me).wait_recv()
        for cp in first + passed:
            cp.wait_send()
        for cp in mine:
            cp.wait()

    return pl.pallas_call(
        body, name=name, in_specs=[ANY] * n, out_specs=[ANY] * n,
        out_shape=[_out(N_DEV * a.shape[0], a.shape[1], a.dtype) for a in arrs],
        scratch_shapes=[pltpu.SemaphoreType.DMA((n, 7)), pltpu.SemaphoreType.DMA((n, 7)), pltpu.SemaphoreType.DMA((n,))],
    )(*arrs)


def _exchange_cores(name, blocks):
    n = len(blocks)
    c = blocks[0].shape[2]
    r = sum(b.shape[1] for b in blocks)

    def body(*refs):
        srcs, (recv_ref, send_sems, recv_sems) = refs[:n], refs[n:]
        x, y, cc = lax.axis_index("x"), lax.axis_index("y"), lax.axis_index("c")
        copies, off = [], 0
        for a, src in enumerate(srcs):
            rows = pl.ds(off, src.shape[1])
            off += src.shape[1]
            for q in range(4):
                copies.append(pltpu.make_async_remote_copy(
                    src_ref=src.at[2 * q + (1 - cc)], dst_ref=recv_ref.at[q, rows], send_sem=send_sems.at[a, q],
                    recv_sem=recv_sems.at[a, q], device_id=(x, y, 1 - cc), device_id_type=MESH))
        for cp in copies:
            cp.start()
        for cp in copies:
            cp.wait()

    return pl.pallas_call(
        body, name=name, in_specs=[ANY] * n, out_specs=ANY,
        out_shape=jax.ShapeDtypeStruct((4, r, c), blocks[0].dtype),
        scratch_shapes=[pltpu.SemaphoreType.DMA((n, 4))] * 2,
    )(*blocks)


def _peer(k, x, y, c):
    return (1 - x if k & 4 else x, 1 - y if k & 2 else y, 1 - c if k & 1 else c)


def _split_start(name, groups, after=None):
    pins = [] if after is None else [after]
    bufs, sem_shapes, spans = [], [], []
    for srcs, land_shapes, n_remote, n_local, _ in groups:
        sems = [pltpu.SemaphoreType.DMA((n_remote,)), pltpu.SemaphoreType.DMA((n_remote,))]
        sems += [pltpu.SemaphoreType.DMA((n_local,))] if n_local else []
        spans.append((len(bufs), len(srcs), len(land_shapes), len(sem_shapes), len(sems)))
        bufs += [pltpu.with_memory_space_constraint(a, pltpu.HBM) for a in srcs]
        bufs += [pltpu.with_memory_space_constraint(lax.empty(s.shape, s.dtype), pltpu.HBM) for s in land_shapes]
        sem_shapes += sems
    n_buf, n_sem = len(bufs), len(sem_shapes)

    def body(*refs):
        buf_refs, sem_refs, token = refs[:n_buf], refs[n_buf + len(pins):n_buf + len(pins) + n_sem], refs[-1]
        for (b0, ns, nl, s0, k), group in zip(spans, groups):
            remote, local = group[4](buf_refs[b0:b0 + ns], buf_refs[b0 + ns:b0 + ns + nl], *sem_refs[s0:s0 + k])
            for cp in local + remote:
                cp.start()
        token[...] = jnp.zeros_like(token)

    outs = pl.pallas_call(
        body, name=name,
        out_shape=sem_shapes + [pltpu.HBM(b.shape, b.dtype) for b in bufs] + [jax.ShapeDtypeStruct((SUBLANES, LANES), F32)],
        in_specs=[HBM] * n_buf + [ANY] * len(pins),
        out_specs=[SEM] * n_sem + [HBM] * n_buf + [pl.BlockSpec(memory_space=pltpu.VMEM)],
        input_output_aliases={i: n_sem + i for i in range(n_buf)},
        compiler_params=pltpu.CompilerParams(has_side_effects=SIDE_EFFECT),
    )(*bufs, *pins)
    return [dict(sems=list(outs[s0:s0 + k]), bufs=list(outs[n_sem + b0:n_sem + b0 + ns + nl]), token=outs[-1],
                 build=group[4], ns=ns) for (b0, ns, nl, s0, k), group in zip(spans, groups)]


def _split_wait(name, started, after):
    ns, n_buf, n_sem = started["ns"], len(started["bufs"]), len(started["sems"])

    def body(*refs):
        src_refs, land_refs = refs[:ns], refs[ns:n_buf]
        sems = refs[n_buf:n_buf + n_sem]
        remote, local = started["build"](src_refs, land_refs, *sems)
        for cp in local:
            cp.wait()
        for cp in remote:
            cp.wait_send()
            cp.wait_recv()

    outs = pl.pallas_call(
        body, name=name, out_shape=[pltpu.HBM(b.shape, b.dtype) for b in started["bufs"]],
        in_specs=[HBM] * n_buf + [SEM] * n_sem + [ANY], out_specs=[HBM] * n_buf,
        input_output_aliases={i: i for i in range(n_buf)},
        compiler_params=pltpu.CompilerParams(has_side_effects=SIDE_EFFECT),
    )(*started["bufs"], *started["sems"], after)
    return list(outs[:ns]), list(outs[ns:])


def _gather_group(shards):
    m = len(shards)

    def build(src_refs, land_refs, send_sems, recv_sems, local_sems):
        x, y, c = lax.axis_index("x"), lax.axis_index("y"), lax.axis_index("c")
        remote, local = [], []
        for j in range(m):
            r = src_refs[j].shape[0]
            dst = land_refs[j].at[pl.ds((4 * x + 2 * y + c) * r, r), :]
            local.append(pltpu.make_async_copy(src_refs[j], dst, local_sems.at[j]))
            for k in range(1, N_DEV):
                remote.append(pltpu.make_async_remote_copy(
                    src_ref=src_refs[j], dst_ref=dst, send_sem=send_sems.at[7 * j + k - 1],
                    recv_sem=recv_sems.at[7 * j + k - 1], device_id=_peer(k, x, y, c), device_id_type=MESH))
        return remote, local

    lands = [jax.ShapeDtypeStruct((N_DEV * a.shape[0], a.shape[1]), a.dtype) for a in shards]
    return shards, lands, 7 * m, m, build


def _slots_start(name, a):
    def build(src_refs, land_refs, send_sems, recv_sems, local_sems):
        x, y, c = lax.axis_index("x"), lax.axis_index("y"), lax.axis_index("c")
        dst = land_refs[0].at[4 * x + 2 * y + c]
        local = [pltpu.make_async_copy(src_refs[0], dst, local_sems.at[0])]
        remote = [pltpu.make_async_remote_copy(
            src_ref=src_refs[0], dst_ref=dst, send_sem=send_sems.at[k - 1], recv_sem=recv_sems.at[k - 1],
            device_id=_peer(k, x, y, c), device_id_type=MESH) for k in range(1, N_DEV)]
        return remote, local

    return _split_start(name, [([a], [jax.ShapeDtypeStruct((N_DEV,) + a.shape, a.dtype)], 7, 1, build)])[0]


def _chips_start(name, p):
    _, r, c = p.shape
    nck = r // GRAD_ROW_TILE

    def build(src_refs, land_refs, send_sems, recv_sems):
        x, y, cc = lax.axis_index("x"), lax.axis_index("y"), lax.axis_index("c")
        remote = []
        for k in range(1, 4):
            px = 1 - x if k >> 1 else x
            py = 1 - y if k & 1 else y
            for j in range(nck):
                rows = pl.ds(j * GRAD_ROW_TILE, GRAD_ROW_TILE)
                remote.append(pltpu.make_async_remote_copy(
                    src_ref=src_refs[0].at[2 * px + py, rows], dst_ref=land_refs[0].at[k - 1, rows],
                    send_sem=send_sems.at[(k - 1) * nck + j], recv_sem=recv_sems.at[(k - 1) * nck + j],
                    device_id=(px, py, cc), device_id_type=MESH))
        return remote, []

    return _split_start(name, [([p], [jax.ShapeDtypeStruct((3, r, c), p.dtype)], 3 * nck, 0, build)])[0]


def _chip_sum(name, p, recv, chip):
    _, r, c = p.shape
    tr = _pick(r, 5 * GRAD_ROW_TILE, GRAD_ROW_TILE)

    def body(chip_ref, p_ref, r_ref, o_ref):
        acc = p_ref[...].astype(F32)
        for k in range(3):
            acc = acc + r_ref[k].astype(F32)
        o_ref[...] = acc

    return pl.pallas_call(
        body, name=name,
        grid_spec=pltpu.PrefetchScalarGridSpec(
            num_scalar_prefetch=1, grid=(r // tr,),
            in_specs=[pl.BlockSpec((None, tr, c), lambda i, chip_ref: (chip_ref[0], i, 0)),
                      pl.BlockSpec((3, tr, c), lambda i, chip_ref: (0, i, 0))],
            out_specs=pl.BlockSpec((tr, c), lambda i, chip_ref: (i, 0))),
        out_shape=_out(r, c, F32), compiler_params=_params(("parallel",)),
    )(chip, p, recv)


def _local_step(x, mem, tgt, wt, sm, ev=None):
    t, d = x.shape
    n_mem = mem.shape[0]
    d_pool = sm["pool_scale"].shape[1]
    ng, pc = sm["pool_w"].shape[0], sm["pool_w"].shape[1]
    d_ssm = sm["ssm_d"].shape[1]
    _, sg, sp, sh = sm["ssm_b_re"].shape
    n_state = sg * sp
    gb, gs = {}, {}

    def emit(name, **kw):
        return ev(name, **kw) if ev is not None else None

    n1 = _rms_fwd("ffn1_norm", x, sm["ffn1_norm"])
    emit("ffn1_norm_done", marker=n1)
    def ffn1_down(hid):
        emit("ffn1_up_done", marker=hid)
        return wt["ffn1_w_down"]

    h1, ffn1_saved = _ffn_fwd("ffn1", x, n1, wt["ffn1_w_gate"], wt["ffn1_w_up"], ffn1_down)
    emit("ffn1_fwd_done", marker=h1)
    u = _rms_fwd("mix_norm", h1, sm["mix_norm"])
    d_in = wt["w_in"].shape[0]
    tm, tn = _pick(t, 1024), _pick(d_in, 1408)
    proj = _mm1("in_proj", "nt", u, wt["w_in"], t, d_in, tm, tn, F32)
    off_s = d_pool // d_ssm
    off_gp = (d_pool + d_ssm)
    off_gs = off_gp + d

    pool_w_bf = sm["pool_w"].astype(BF16)
    pooled, pm = _pool_fwd(proj, pool_w_bf, sm["pool_scale"])

    cols = [sm["ssm_a_re"].reshape(-1, 1), sm["ssm_a_im"].reshape(-1, 1),
            jnp.broadcast_to(sm["ssm_log_dt"][:, :, None], (2, sg, sp)).reshape(-1, 1),
            sm["ssm_b_re"].reshape(-1, sh), sm["ssm_b_im"].reshape(-1, sh)]
    abr, abi, bbr, bbi = _ssm_disc(cols)
    abr2, abi2 = abr.reshape(2, n_state), abi.reshape(2, n_state)
    bbr4, bbi4 = bbr.reshape(2, sg * sp, sh), bbi.reshape(2, sg * sp, sh)
    b_re = [_bd_in(bbr4[dr], sg, sp, sh).astype(BF16) for dr in range(2)]
    b_im = [_bd_in(bbi4[dr], sg, sp, sh).astype(BF16) for dr in range(2)]
    c_re = [_bd_out(sm["ssm_c_re"][dr], sg, sp, sh).astype(BF16) for dr in range(2)]
    c_im = [_bd_out(-sm["ssm_c_im"][dr], sg, sp, sh).astype(BF16) for dr in range(2)]
    sp32 = _to_segments(proj[:, d_pool:d_pool + d_ssm])
    xs, y_parts = [], []
    for dr in range(2):
        xr, xi, y_part = _ssm_fwd(f"ssm_fwd{dr}", sp32, b_re[dr], b_im[dr], c_re[dr], c_im[dr], abr2[dr:dr + 1],
                                  abi2[dr:dr + 1], reverse=(dr == 1))
        xs.append((xr, xi))
        y_parts.append(y_part)
    y = _from_segments(_ew("ssm_sum", lambda p0, p1, sv, dv: (p0 + p1 + sv * dv,), y_parts + [sp32], [F32],
                           rowvecs=[sm["ssm_d"]])[0])
    tmy = _pick(t, 256)
    ys = _ew("ssm_gelu", lambda v: (jax.nn.gelu(v),), [y], [BF16])[0]
    emit("mix_in_done", marker=ys)

    tmm, tnm, tnx = _pick(t, 1024), _pick(d, 256), _pick(d, 512)
    gp_spec = _tile(tmm, tnm, off_gp // tnm)
    gs_spec = _tile(tmm, tnm, off_gs // tnm)

    def merge_epi(accs, gpv, gsv):
        z_pool, val, gate = accs
        return (jax.nn.sigmoid(gpv) * z_pool + jax.nn.sigmoid(gsv) * (val * jax.nn.sigmoid(gate)),)

    merged = _mm("mix_merge", "nt", [pm, ys], [wt["w_pool_proj"], wt["w_glu_val"], wt["w_glu_gate"]],
                 [[(0, 0)], [(1, 1)], [(1, 2)]], t, d, tmm, tnm, [(proj, gp_spec), (proj, gs_spec)], merge_epi,
                 [(_out(t, d, BF16), None)])[0]
    res_epi = lambda accs, hin: (hin + accs[0],)
    h2 = _mm("mix_out", "nn", [merged], [wt["w_mix_out"]], [[(0, 0)]], t, d, tmm, tnx, [(h1, _tile(tmm, tnx))],
             res_epi, [(_out(t, d, F32), None)])[0]

    un = _rms_fwd("xattn_norm", h2, sm["xattn_norm"])
    mn = _rms_fwd("mem_norm", mem, sm["mem_norm"])
    emit("mix_done", marker=un)
    q = _mm1("xattn_q", "nn", un, wt["w_q"], t, d, tmm, tnx, BF16)
    kv = _mm1("xattn_kv", "nt", mn, wt["w_kv"], n_mem, 2 * d, n_mem, _pick(2 * d, 512), BF16)
    o = _attn_fwd(q, kv)
    h3 = _mm("xattn_out", "nn", [o], [wt["w_xo"]], [[(0, 0)]], t, d, tmm, tnx, [(h2, _tile(tmm, tnx))],
             res_epi, [(_out(t, d, F32), None)])[0]

    n2 = _rms_fwd("ffn2_norm", h3, sm["ffn2_norm"])
    emit("xattn_done", marker=n2)
    h4, ffn2_saved = _ffn_fwd("ffn2", h3, n2, wt["ffn2_w_gate"], wt["ffn2_w_up"], wt["ffn2_w_down"])

    dh4, dh4_bf, gs["final_norm"], loss = _loss_head(h4, sm["final_norm"], tgt)
    dh3, dh3_bf, gs["ffn2_norm"], gb["ffn2_w_gate"], gb["ffn2_w_up"], gb["ffn2_w_down"] = _ffn_bwd(
        "ffn2", h3, sm["ffn2_norm"], wt["ffn2_w_gate"], wt["ffn2_w_up"], wt["ffn2_w_down"], ffn2_saved, dh4, dh4_bf)

    tw = _pick(d, 1024)
    do = _mm1("xattn_do", "nt", dh3_bf, wt["w_xo"], t, d, tmm, tnx, BF16)
    gb["w_xo"] = _mm1("xattn_dwxo", "tn", o, dh3_bf, d, d, tw, tnx, BF16)
    dq, dkv = _attn_bwd(q, kv, do)
    gb["w_q"] = _mm1("xattn_dwq", "tn", un, dq, d, d, tw, tnx, BF16)
    dun = _mm1("xattn_dun", "nt", dq, wt["w_q"], t, d, tmm, tnx, F32)
    dh2, dh2_bf, gs["xattn_norm"] = _rms_bwd("xattn_norm_bwd", h2, sm["xattn_norm"], dun, dh3)
    gb["w_kv"] = _mm1("xattn_dwkv", "tn", dkv, mn, 2 * d, d, _pick(2 * d, 512), d, BF16)
    dmn = _mm1("xattn_dmn", "nn", dkv, wt["w_kv"], n_mem, d, n_mem, tnx, F32)
    gs["mem_norm"] = _rms_bwd("mem_norm_bwd", mem, sm["mem_norm"], dmn)

    gb["w_mix_out"] = _mm1("mix_dwout", "tn", merged, dh2_bf, d, d, tw, tnx, BF16)

    def merge_bwd_epi(accs, gpv, gsv):
        dmerged, z_pool, val, gate = accs
        sp_, ss_, sg_ = jax.nn.sigmoid(gpv), jax.nn.sigmoid(gsv), jax.nn.sigmoid(gate)
        glu = val * sg_
        dz_pool = dmerged * sp_
        dg_pool = dmerged * z_pool * (sp_ * (1.0 - sp_))
        dz_ssm = dmerged * ss_
        dg_ssm = dmerged * glu * (ss_ * (1.0 - ss_))
        dval = dz_ssm * sg_
        dgate = dz_ssm * glu * (1.0 - sg_)
        return dz_pool, dg_pool, dg_ssm, dval, dgate

    dz_pool, dg_pool, dg_ssm, dval, dgate = _mm(
        "mix_merge_bwd", "nt", [dh2_bf, pm, ys], [wt["w_mix_out"], wt["w_pool_proj"], wt["w_glu_val"], wt["w_glu_gate"]],
        [[(0, 0)], [(1, 1)], [(2, 2)], [(2, 3)]], t, d, tmm, tnm, [(proj, gp_spec), (proj, gs_spec)], merge_bwd_epi,
        [(_out(t, d, BF16), None)] * 5)
    gb["w_pool_proj"] = _mm1("pool_dwproj", "tn", dz_pool, pm, d, d_pool, tw, d_pool, BF16)
    gb["w_glu_val"] = _mm1("glu_dwval", "tn", dval, ys, d, d_ssm, tw, d_ssm, BF16)
    gb["w_glu_gate"] = _mm1("glu_dwgate", "tn", dgate, ys, d, d_ssm, tw, d_ssm, BF16)

    def gelu_bwd_epi(accs, yv):
        _, vjp = jax.vjp(jax.nn.gelu, yv)
        return (vjp(accs[0])[0],)

    dy = _mm("glu_dy", "nn", [dval, dgate], [wt["w_glu_val"], wt["w_glu_gate"]], [[(0, 0), (1, 1)]], t, d_ssm, tmy, d_ssm,
             [(y, _tile(tmy, d_ssm))], gelu_bwd_epi, [(_out(t, d_ssm, F32), None)])[0]
    gs["ssm_d"] = _colsum_prod("ssm_dd", dy, proj, b_coff=off_s)
    dyp = _to_segments(dy)
    d_abr, d_abi, d_bbr, d_bbi, d_cre, d_cim, lams = [], [], [], [], [], [], []
    ts = _pick(n_state, 512)
    tc_ = _pick(n_state, 256)
    both = lambda accs: tuple(accs)
    for dr in range(2):
        lr, li, dar, dai = _ssm_bwd(f"ssm_bwd{dr}", dyp, c_re[dr], c_im[dr], xs[dr][0], xs[dr][1], abr2[dr:dr + 1],
                                    abi2[dr:dr + 1], reverse=(dr == 1))
        d_abr.append(dar)
        d_abi.append(dai)
        lams += [lr, li]
        d_br, d_bi = _mm(f"ssm_db{dr}", "tn", [sp32], [lr, li], [[(0, 0)], [(0, 1)]], d_ssm, n_state, d_ssm, ts, [], both,
                         [(_out(d_ssm, n_state, F32), None)] * 2)
        d_bbr.append(_diag_in(d_br, sg, sp, sh))
        d_bbi.append(_diag_in(d_bi, sg, sp, sh))
        d_cr, d_ci = _mm(f"ssm_dc{dr}", "tn", [xs[dr][0], xs[dr][1]], [dyp], [[(0, 0)], [(1, 0)]], n_state, d_ssm, tc_,
                         d_ssm, [], both, [(_out(n_state, d_ssm, F32), None)] * 2)
        d_cre.append(_diag_out(d_cr, sg, sp, sh))
        d_cim.append(-_diag_out(d_ci, sg, sp, sh))
    ds = _from_segments(_mm(
        "ssm_ds", "nt", lams, [b_re[0], b_im[0], b_re[1], b_im[1]], [[(k, k) for k in range(4)]], t, d_ssm, tmy,
        d_ssm, [(dyp, _tile(tmy, d_ssm)), (sm["ssm_d"], _rowvec(d_ssm))],
        lambda accs, dyv, dv: (dyv * dv + accs[0],), [(_out(t, d_ssm, BF16), None)])[0])
    cots = [jnp.concatenate(d_abr, axis=0).reshape(-1, 1), jnp.concatenate(d_abi, axis=0).reshape(-1, 1),
            jnp.concatenate(d_bbr, axis=0), jnp.concatenate(d_bbi, axis=0)]
    d_are, d_aim, d_ldt, d_bre, d_bim = _ssm_disc_bwd(cols, cots)
    gs["ssm_a_re"] = d_are.reshape(2, sg, sp)
    gs["ssm_a_im"] = d_aim.reshape(2, sg, sp)
    gs["ssm_log_dt"] = _rowsum("ssm_dlogdt", d_ldt.reshape(2 * sg, sp)).reshape(2, sg)
    gs["ssm_b_re"] = d_bre.reshape(2, sg, sp, sh)
    gs["ssm_b_im"] = d_bim.reshape(2, sg, sp, sh)
    gs["ssm_c_re"] = jnp.stack(d_cre, axis=0)
    gs["ssm_c_im"] = jnp.stack(d_cim, axis=0)

    dpm = _mm1("pool_dpm", "nn", dz_pool, wt["w_pool_proj"], t, d_pool, tmm, _pick(d_pool, 256), F32)
    dp, gs["pool_w"], gs["pool_scale"] = _pool_bwd(pooled, dpm, pool_w_bf, sm["pool_scale"])

    w_in = wt["w_in"]
    parts = [(dp, 0, d_pool), (ds, d_pool, d_ssm), (dg_pool, off_gp, d), (dg_ssm, off_gs, d)]
    w_in_parts = [w_in[o0:o0 + width] for _, o0, width in parts]
    gb["w_in"] = jnp.concatenate(
        [_mm1(f"in_proj_dw{k}", "tn", p_[0], u, p_[2], d, _pick(p_[2], 1024), tnx, BF16) for k, p_ in enumerate(parts)], axis=0)
    pin = emit("grads_main", gb=gb)
    du = _mm("in_proj_du", "nn", [p_[0] for p_ in parts], w_in_parts, [[(k, k) for k in range(4)]], t, d, tmm, tnx, [],
             lambda accs: (accs[0],), [(_out(t, d, F32), None)], after=pin)[0]
    dh1, dh1_bf, gs["mix_norm"] = _rms_bwd("mix_norm_bwd", h1, sm["mix_norm"], du, dh2)
    pin = emit("small_early", gs=gs, loss=loss)

    def ffn1_weights_done(d_wg, d_wu, d_wd):
        gb["ffn1_w_gate"], gb["ffn1_w_up"], gb["ffn1_w_down"] = d_wg, d_wu, d_wd
        return emit("grads_ffn1", gb=gb)

    dx, _, gs["ffn1_norm"], _, _, _ = _ffn_bwd(
        "ffn1", x, sm["ffn1_norm"], wt["ffn1_w_gate"], wt["ffn1_w_up"], wt["ffn1_w_down"], ffn1_saved, dh1, dh1_bf,
        weights_done=ffn1_weights_done, after=pin)
    return loss, dx, gb, gs


WEIGHTS = ["ffn1_norm", "ffn1_w_gate", "ffn1_w_up", "ffn1_w_down", "mix_norm", "w_in", "pool_w", "pool_scale",
           "w_pool_proj", "ssm_a_re", "ssm_a_im", "ssm_log_dt", "ssm_b_re", "ssm_b_im", "ssm_c_re", "ssm_c_im", "ssm_d",
           "w_glu_val", "w_glu_gate", "w_mix_out", "xattn_norm", "mem_norm", "w_q", "w_kv", "w_xo", "ffn2_norm",
           "ffn2_w_gate", "ffn2_w_up", "ffn2_w_down", "final_norm"]
COL_SHARDED = ["ffn1_w_gate", "ffn1_w_up", "w_in", "w_pool_proj", "w_glu_val", "w_glu_gate", "w_kv", "ffn2_w_gate",
               "ffn2_w_up"]
ROW_SHARDED = ["ffn1_w_down", "w_mix_out", "w_q", "w_xo", "ffn2_w_down"]
BIG = [n for n in WEIGHTS if n in COL_SHARDED or n in ROW_SHARDED]
SMALL = [n for n in WEIGHTS if n not in BIG]
FFN1_BIG = ["ffn1_w_gate", "ffn1_w_up", "ffn1_w_down"]
MAIN_BIG = [n for n in BIG if n not in FFN1_BIG]
GATHER_PLAN = [("ffn1_up_done", ["ffn1_w_down"]), ("ffn1_fwd_done", ["w_in"]),
               ("mix_in_done", ["w_pool_proj", "w_glu_val", "w_glu_gate", "w_mix_out"]),
               ("mix_done", ["w_q", "w_kv", "w_xo"]), ("xattn_done", ["ffn2_w_gate", "ffn2_w_up", "ffn2_w_down"])]
LATE_SMALL = "ffn1_norm"
EARLY_SMALL = [n for n in SMALL if n != LATE_SMALL]
PACK_ROWS = SUBLANES * LANES
GRAD_ROW_TILE = 256
ADAMW_STEP_WORDS = 1 << 19


def _to_rows(name, w, width):
    if name in COL_SHARDED:
        w = w.T
    return w.reshape(-1, width)


def _pack_small(vals):
    flat = []
    for v in vals:
        f = v.reshape(-1)
        flat.append(jnp.pad(f, (0, (-f.shape[0]) % PACK_ROWS)))
    total = sum(f.shape[0] for f in flat)
    flat.append(jnp.zeros(((-total) % (GRAD_ROW_TILE * LANES),), F32))
    return jnp.concatenate(flat).reshape(-1, LANES)


def _unpack_small(packed, shapes):
    out, row = [], 0
    for shp in shapes:
        size = math.prod(shp)
        rows = -(-size // PACK_ROWS) * SUBLANES
        out.append(packed[row:row + rows].reshape(-1)[:size].reshape(shp))
        row += rows
    return out


def kernel(x, mem, ffn1_norm, ffn1_w_gate, ffn1_w_up, ffn1_w_down, mix_norm, w_in, pool_w, pool_scale, w_pool_proj, ssm_a_re, ssm_a_im, ssm_log_dt, ssm_b_re, ssm_b_im, ssm_c_re, ssm_c_im, ssm_d, w_glu_val, w_glu_gate, w_mix_out, xattn_norm, mem_norm, w_q, w_kv, w_xo, ffn2_norm, ffn2_w_gate, ffn2_w_up, ffn2_w_down, final_norm, loss_target, m_ffn1_norm, m_ffn1_w_gate, m_ffn1_w_up, m_ffn1_w_down, m_mix_norm, m_w_in, m_pool_w, m_pool_scale, m_w_pool_proj, m_ssm_a_re, m_ssm_a_im, m_ssm_log_dt, m_ssm_b_re, m_ssm_b_im, m_ssm_c_re, m_ssm_c_im, m_ssm_d, m_w_glu_val, m_w_glu_gate, m_w_mix_out, m_xattn_norm, m_mem_norm, m_w_q, m_w_kv, m_w_xo, m_ffn2_norm, m_ffn2_w_gate, m_ffn2_w_up, m_ffn2_w_down, m_final_norm, v_ffn1_norm, v_ffn1_w_gate, v_ffn1_w_up, v_ffn1_w_down, v_mix_norm, v_w_in, v_pool_w, v_pool_scale, v_w_pool_proj, v_ssm_a_re, v_ssm_a_im, v_ssm_log_dt, v_ssm_b_re, v_ssm_b_im, v_ssm_c_re, v_ssm_c_im, v_ssm_d, v_w_glu_val, v_w_glu_gate, v_w_mix_out, v_xattn_norm, v_mem_norm, v_w_q, v_w_kv, v_w_xo, v_ffn2_norm, v_ffn2_w_gate, v_ffn2_w_up, v_ffn2_w_down, v_final_norm):
    given = dict(locals())
    wts = {n: given[n] for n in WEIGHTS}
    moms = {n: (given["m_" + n], given["v_" + n]) for n in WEIGHTS}
    x2, mem2, tgt2 = x[0], mem[0], loss_target[0]
    d = x2.shape[1]
    chip = (2 * lax.axis_index("x") + lax.axis_index("y")).astype(jnp.int32).reshape(1)

    def full_form(n, f):
        shard = wts[n][0].shape
        return f.reshape(N_DEV * shard[1], shard[0]) if n in COL_SHARDED else f.reshape(N_DEV * shard[0], shard[1])

    shards = {n: _to_rows(n, wts[n][0], d).astype(BF16) for n in BIG}
    first = FFN1_BIG[:2]
    wt = {n: full_form(n, f) for n, f in zip(first, _allgather("weight_allgather_first", [shards[n] for n in first]))}
    started = _split_start("weight_gather_start", [_gather_group([shards[n] for n in names]) for _, names in GATHER_PLAN],
                           after=wt[first[0]])
    gathers = {event: (names, st) for (event, names), st in zip(GATHER_PLAN, started)}
    sm = {n: (wts[n].reshape(1, -1) if wts[n].ndim <= 2 else wts[n][0]) for n in SMALL}
    sm["ffn1_norm"] = sm["ffn1_norm"] + started[0]["token"][0, 0]

    pending = {}

    def reduce_start(tag, names, gb):
        blocks = [gb[n].reshape(N_DEV, -1, d) for n in names]
        pad_rows = (-sum(b.shape[1] for b in blocks)) % GRAD_ROW_TILE
        pad = [jnp.zeros((N_DEV, pad_rows, d), BF16)] if pad_rows else []
        recv = _exchange_cores("grad_exchange_cores_" + tag, blocks + pad)
        own = jnp.concatenate([lax.dynamic_index_in_dim(b.reshape(4, 2, b.shape[1], d), lax.axis_index("c"), 1, False)
                               for b in blocks + pad], axis=1)
        rows_all = own.shape[1]
        pair = _ew("grad_pair_sum_" + tag, lambda a, b: (a.astype(F32) + b.astype(F32),),
                   [own.reshape(-1, d), recv.reshape(-1, d)], [BF16], rows_pref=5 * GRAD_ROW_TILE)[0]
        pair = pair.reshape(4, rows_all, d)
        pending[tag] = (pair, _chips_start("grad_exchange_chips_start_" + tag, pair), [b.shape[1] for b in blocks])
        return pending[tag][1]["token"]

    def reduce_finish(tag, after):
        _, started, rows = pending[tag]
        (pair,), (recv,) = _split_wait("grad_exchange_chips_wait_" + tag, started, after)
        return _chip_sum("grad_chip_sum_" + tag, pair, recv, chip), rows

    def ev(name, gb=None, gs=None, loss=None, marker=None):
        if name in gathers:
            names, started = gathers[name]
            for n, f in zip(names, _split_wait("weight_gather_wait_" + name, started, marker)[1]):
                wt[n] = full_form(n, f)
        elif name == "grads_main":
            return reduce_start("main", MAIN_BIG, gb)
        elif name == "small_early":
            pending["small"] = _slots_start("small_gather_start", _pack_small([gs[n] for n in EARLY_SMALL] + [loss[:, :1]]))
            return pending["small"]["token"]
        elif name == "grads_ffn1":
            return reduce_start("ffn1", FFN1_BIG, gb)
        return None

    _, dx, _, gs = _local_step(x2, mem2, tgt2, wt, sm, ev)

    grads = {}
    for tag, names in (("main", MAIN_BIG), ("ffn1", FFN1_BIG)):
        g_rows, rows = reduce_finish(tag, dx)
        off = 0
        for n, r in zip(names, rows):
            shard = wts[n].shape
            grads[n] = g_rows[off:off + r].reshape((shard[2], shard[1]) if n in COL_SHARDED else shard[1:])
            off += r
    small_sum = _sum_slots("small_sum", _split_wait("small_gather_wait", pending["small"], dx)[1][0], F32)
    late = _allgather("small_allgather_late", [gs[LATE_SMALL].reshape(-1, LANES)])[0]
    late_sum = _sum_slots("small_sum_late", late.reshape(N_DEV, -1, LANES), F32)
    vals = _unpack_small(small_sum, [wts[n].shape for n in EARLY_SMALL] + [(1, 1)])
    total_loss = vals[-1].reshape(())
    for n, g_full in zip(EARLY_SMALL + [LATE_SMALL], vals[:-1] + [late_sum]):
        grads[n] = g_full.reshape(-1, wts[n].shape[-1])

    out_g, out_d, out_m, out_v = {}, {}, {}, {}
    by_shape = {}
    for n in WEIGHTS:
        by_shape.setdefault((wts[n].size // wts[n].shape[-1], wts[n].shape[-1], n in COL_SHARDED), []).append(n)
    for (_, _, transposed), names in by_shape.items():
        two_d = (-1, wts[names[0]].shape[-1])
        items = [(wts[n].reshape(two_d), grads[n], moms[n][0].reshape(two_d), moms[n][1].reshape(two_d)) for n in names]
        for n, res in zip(names, _adamw_group("adamw_" + names[0], items, transposed)):
            shape = wts[n].shape
            out_d[n], out_m[n], out_v[n], out_g[n] = (a.reshape(shape) for a in res)

    return (total_loss, dx[None], *[out_g[n] for n in WEIGHTS], *[out_d[n] for n in WEIGHTS],
            *[out_m[n] for n in WEIGHTS], *[out_v[n] for n in WEIGHTS])
```

```python
import functools
import math

import jax
import jax.numpy as jnp
from jax import lax
from jax.experimental import pallas as pl
from jax.experimental.pallas import tpu as pltpu

F32 = jnp.float32
BF16 = jnp.bfloat16
EPS = 1e-6
N_XHEADS = 4
POOL_WINDOWS = (2, 4, 8, 16)
ADAM_LR = 0.001
ADAM_B1 = 0.9
ADAM_B2 = 0.999
ADAM_EPS = 1e-08
ADAM_WD = 0.01
ADAM_STEP = 10
N_DEV = 8
VMEM_LIMIT_V7X = 48 * 1024 * 1024
LANES = 128
SUBLANES = 8
SUB_ROWS = 256
POOL_PAD = 16
MESH = pl.DeviceIdType.MESH
ANY = pl.BlockSpec(memory_space=pl.ANY)
HBM = pl.BlockSpec(memory_space=pltpu.HBM)
SEM = pl.BlockSpec(memory_space=pltpu.SEMAPHORE)
SIDE_EFFECT = pltpu.SideEffectType.DATAFLOW_SIDE_EFFECTING

_DIMS = {
    "nt": (((1,), (1,)), ((), ())),
    "nn": (((1,), (0,)), ((), ())),
    "tn": (((0,), (0,)), ((), ())),
}


def _pick(dim, pref, mult=LANES):
    if dim <= pref:
        return dim
    for t in range(pref - pref % mult, 0, -mult):
        if dim % t == 0:
            return t
    return dim


def _params(sem):
    return pltpu.CompilerParams(dimension_semantics=sem, vmem_limit_bytes=VMEM_LIMIT_V7X)


def _tile(tm, tn, coff=0):
    return pl.BlockSpec((tm, tn), lambda i, j: (i, j + coff))


def _rowvec(tn, coff=0):
    return pl.BlockSpec((1, tn), lambda i, j: (0, j + coff))


def _out(m, n, dtype):
    return jax.ShapeDtypeStruct((m, n), dtype)


def _mm(name, form, a_list, b_list, groups, m, n, tm, tn, extras, epilogue, outs, after=None, sub=SUB_ROWS):
    na, nb, ne = len(a_list), len(b_list), len(extras)
    pins = [] if after is None else [after]
    step = tm if (sub is None or form == "tn" or tm % sub) else sub

    def a_spec(a):
        if form == "tn":
            return pl.BlockSpec((a.shape[0], tm), lambda i, j: (0, i))
        return pl.BlockSpec((tm, a.shape[1]), lambda i, j: (i, 0))

    def b_spec(b):
        if form == "nt":
            return pl.BlockSpec((tn, b.shape[1]), lambda i, j: (j, 0))
        return pl.BlockSpec((b.shape[0], tn), lambda i, j: (0, j))

    def body(*refs):
        a_refs, b_refs = refs[:na], refs[na:na + nb]
        e_refs, o_refs = refs[na + nb:na + nb + ne], refs[na + nb + ne + len(pins):]
        b_vals = {}
        for s0 in range(0, tm, step):
            rows = slice(None) if step == tm else pl.ds(s0, step)
            a_vals, accs = {}, []
            for group in groups:
                acc = None
                for ai, bi in group:
                    if ai not in a_vals:
                        a_vals[ai] = (a_refs[ai][...] if form == "tn" else a_refs[ai][rows, :]).astype(BF16)
                    if bi not in b_vals:
                        b_vals[bi] = b_refs[bi][...].astype(BF16)
                    d = lax.dot_general(a_vals[ai], b_vals[bi], _DIMS[form], preferred_element_type=F32)
                    acc = d if acc is None else acc + d
                accs.append(acc)
            res = epilogue(accs, *[e[rows, :] if e.shape[0] == tm else e[...] for e in e_refs])
            for o_ref, r in zip(o_refs, res):
                o_ref[rows, :] = r.astype(o_ref.dtype)

    out_specs = [_tile(tm, tn) if s is None else s for _, s in outs]
    res = pl.pallas_call(
        body, name=name, grid=(m // tm, n // tn),
        in_specs=[a_spec(a) for a in a_list] + [b_spec(b) for b in b_list] + [s for _, s in extras] + [ANY] * len(pins),
        out_specs=out_specs, out_shape=[o for o, _ in outs],
        compiler_params=_params(("parallel", "parallel")),
    )(*a_list, *b_list, *[e for e, _ in extras], *pins)
    return res


def _mm1(name, form, a, b, m, n, tm, tn, dtype, scale=None):
    epi = (lambda accs: (accs[0],)) if scale is None else (lambda accs: (accs[0] * scale,))
    return _mm(name, form, [a], [b], [[(0, 0)]], m, n, tm, tn, [], epi, [(_out(m, n, dtype), None)])[0]


def _rms_fwd(name, h, g):
    t, d = h.shape
    tm = _pick(t, 512, SUBLANES)

    def body(h_ref, g_ref, n_ref):
        hv = h_ref[...]
        r = lax.rsqrt(jnp.mean(hv * hv, axis=-1, keepdims=True) + EPS)
        n_ref[...] = ((hv * r) * g_ref[...]).astype(BF16)

    return pl.pallas_call(
        body, name=name, grid=(t // tm,),
        in_specs=[pl.BlockSpec((tm, d), lambda i: (i, 0)), pl.BlockSpec((1, d), lambda i: (0, 0))],
        out_specs=pl.BlockSpec((tm, d), lambda i: (i, 0)), out_shape=_out(t, d, BF16),
        compiler_params=_params(("parallel",)),
    )(h, g)


def _rms_bwd(name, h, g, dn, dres=None):
    t, d = h.shape
    tm = _pick(t, 512, SUBLANES)
    need_dh = dres is not None

    def body(*refs):
        if need_dh:
            h_ref, g_ref, dn_ref, dres_ref, dh_ref, dhb_ref, dg_ref = refs
        else:
            h_ref, g_ref, dn_ref, dg_ref = refs
        hv = h_ref[...]
        r = lax.rsqrt(jnp.mean(hv * hv, axis=-1, keepdims=True) + EPS)
        nh = hv * r
        dnv = dn_ref[...].astype(F32)

        @pl.when(pl.program_id(0) == 0)
        def _():
            dg_ref[...] = jnp.zeros_like(dg_ref)

        dg_ref[...] += jnp.sum(dnv * nh, axis=0, keepdims=True)
        if need_dh:
            dng = dnv * g_ref[...]
            dh = dres_ref[...] + r * (dng - nh * jnp.mean(dng * nh, axis=-1, keepdims=True))
            dh_ref[...] = dh
            dhb_ref[...] = dh.astype(BF16)

    row = pl.BlockSpec((tm, d), lambda i: (i, 0))
    vec = pl.BlockSpec((1, d), lambda i: (0, 0))
    if need_dh:
        return pl.pallas_call(
            body, name=name, grid=(t // tm,), in_specs=[row, vec, row, row], out_specs=[row, row, vec],
            out_shape=[_out(t, d, F32), _out(t, d, BF16), _out(1, d, F32)], compiler_params=_params(("arbitrary",)),
        )(h, g, dn, dres)
    return pl.pallas_call(
        body, name=name, grid=(t // tm,), in_specs=[row, vec, row], out_specs=vec,
        out_shape=_out(1, d, F32), compiler_params=_params(("arbitrary",)),
    )(h, g, dn)


def _loss_head(h, g, tgt):
    t, d = h.shape
    tm = _pick(t, 512, SUBLANES)

    def body(h_ref, g_ref, t_ref, dh_ref, dhb_ref, dg_ref, loss_ref):
        hv = h_ref[...]
        r = lax.rsqrt(jnp.mean(hv * hv, axis=-1, keepdims=True) + EPS)
        nh = hv * r
        err = nh * g_ref[...] - t_ref[...]

        @pl.when(pl.program_id(0) == 0)
        def _():
            dg_ref[...] = jnp.zeros_like(dg_ref)
            loss_ref[...] = jnp.zeros_like(loss_ref)

        per_row = jnp.mean(err * err, axis=-1, keepdims=True)
        loss_ref[...] += 0.5 * jnp.sum(per_row, axis=0, keepdims=True)
        dy = err * (1.0 / d)
        dg_ref[...] += jnp.sum(dy * nh, axis=0, keepdims=True)
        dng = dy * g_ref[...]
        dh = r * (dng - nh * jnp.mean(dng * nh, axis=-1, keepdims=True))
        dh_ref[...] = dh
        dhb_ref[...] = dh.astype(BF16)

    row = pl.BlockSpec((tm, d), lambda i: (i, 0))
    vec = pl.BlockSpec((1, d), lambda i: (0, 0))
    return pl.pallas_call(
        body, name="loss_head", grid=(t // tm,), in_specs=[row, vec, row],
        out_specs=[row, row, vec, pl.BlockSpec((1, LANES), lambda i: (0, 0))],
        out_shape=[_out(t, d, F32), _out(t, d, BF16), _out(1, d, F32), _out(1, LANES, F32)],
        compiler_params=_params(("arbitrary",)),
    )(h, g, tgt)


def _ffn_fwd(tag, h, n, wg_t, wu_t, wd):
    t, d = h.shape
    f = wg_t.shape[0]
    tm, tn = _pick(t, 1024), _pick(f, 1408)

    def up_epi(accs):
        a, b = accs
        return a, b, (a * jax.nn.sigmoid(a)) * b

    a, b, hid = _mm(tag + "_up", "nt", [n], [wg_t, wu_t], [[(0, 0)], [(0, 1)]], t, f, tm, tn, [], up_epi,
                    [(_out(t, f, BF16), None)] * 3)
    if callable(wd):
        wd = wd(hid)
    tm2, tn2 = _pick(t, 1024), _pick(d, 512)
    h_out = _mm(tag + "_down", "nn", [hid], [wd], [[(0, 0)]], t, d, tm2, tn2, [(h, _tile(tm2, tn2))],
                lambda accs, hin: (hin + 0.5 * accs[0],), [(_out(t, d, F32), None)])[0]
    return h_out, (n, a, b, hid)


def _ffn_bwd(tag, h, g, wg_t, wu_t, wd, saved, dh, dh_bf, weights_done=None, after=None):
    n, a, b, hid = saved
    t, d = h.shape
    f = wd.shape[0]
    tm, tn = _pick(t, 1024), _pick(f, 1408)

    def hid_epi(accs, av, bv):
        dhid = 0.5 * accs[0]
        av, bv = av.astype(F32), bv.astype(F32)
        sig = jax.nn.sigmoid(av)
        da = dhid * bv * (sig * (1.0 + av * (1.0 - sig)))
        db = dhid * (av * sig)
        return da, db

    da, db = _mm(tag + "_bwd_hid", "nt", [dh_bf], [wd], [[(0, 0)]], t, f, tm, tn,
                 [(a, _tile(tm, tn)), (b, _tile(tm, tn))], hid_epi, [(_out(t, f, BF16), None)] * 2, after=after)
    tw, tnw = _pick(f, 1408), _pick(d, 512)
    d_wd = _mm1(tag + "_dwd", "tn", hid, dh_bf, f, d, tw, tnw, BF16, scale=0.5)
    d_wg = _mm1(tag + "_dwg", "tn", da, n, f, d, tw, tnw, BF16)
    d_wu = _mm1(tag + "_dwu", "tn", db, n, f, d, tw, tnw, BF16)
    pin = weights_done(d_wg, d_wu, d_wd) if weights_done is not None else None
    tm2, tn2 = _pick(t, 1024), _pick(d, 512)
    dn = _mm(tag + "_dn", "nn", [da, db], [wg_t, wu_t], [[(0, 0), (1, 1)]], t, d, tm2, tn2, [],
             lambda accs: (accs[0],), [(_out(t, d, F32), None)], after=pin)[0]
    dh_in, dh_in_bf, dg = _rms_bwd(tag + "_norm_bwd", h, g, dn, dh)
    return dh_in, dh_in_bf, dg, d_wg, d_wu, d_wd


def _window_sum(win, offsets):
    n = win.shape[0]
    acc = None
    for j in offsets:
        term = win if j == 0 else pltpu.roll(win, (-j) % n, 0)
        acc = term if acc is None else acc + term
    return acc


def _pool_counts(r0, ch, c, left, right, t):
    pos = r0 + lax.broadcasted_iota(jnp.int32, (ch, c), 0)
    return (jnp.minimum(pos + right + 1, t) - jnp.maximum(pos - left, 0)).astype(F32)


def _pool_fwd(proj, pool_w_bf, pool_scale):
    t = proj.shape[0]
    ng, c, _ = pool_w_bf.shape
    ch = _pick(t, 256, SUBLANES)
    pad = POOL_PAD

    def body(p_ref, w_ref, s_ref, pooled_ref, pm_ref, buf):
        grp = pl.program_id(0)
        buf[pl.ds(0, pad), :] = jnp.zeros((pad, c), F32)
        buf[pl.ds(pad + t, pad), :] = jnp.zeros((pad, c), F32)

        def fill(ci, carry):
            r0 = pl.multiple_of(ci * ch, SUBLANES)
            buf[pl.ds(pl.multiple_of(r0 + pad, SUBLANES), ch), :] = p_ref[pl.ds(r0, ch), :]
            return carry

        lax.fori_loop(0, t // ch, fill, 0)
        for gi, w in enumerate(POOL_WINDOWS):
            left = w // 2
            right = w - 1 - left

            @pl.when(grp == gi)
            def _(left=left, right=right):
                def chunk(ci, carry):
                    r0 = pl.multiple_of(ci * ch, SUBLANES)
                    win = buf[pl.ds(r0, ch + 2 * pad), :]
                    s = _window_sum(win, range(-left, right + 1))[pad:pad + ch]
                    pooled = s / _pool_counts(r0, ch, c, left, right, t) - win[pad:pad + ch]
                    pooled_bf = pooled.astype(BF16)
                    mixed = jnp.dot(pooled_bf, w_ref[0], preferred_element_type=F32)
                    pooled_ref[pl.ds(r0, ch), :] = pooled_bf
                    pm_ref[pl.ds(r0, ch), :] = (mixed * s_ref[...]).astype(BF16)
                    return carry

                lax.fori_loop(0, t // ch, chunk, 0)

    col = pl.BlockSpec((t, c), lambda g: (0, g))
    return pl.pallas_call(
        body, name="pool_fwd", grid=(ng,),
        in_specs=[col, pl.BlockSpec((1, c, c), lambda g: (g, 0, 0)), pl.BlockSpec((1, c), lambda g: (0, g))],
        out_specs=[col, col], out_shape=[_out(t, ng * c, BF16), _out(t, ng * c, BF16)],
        scratch_shapes=[pltpu.VMEM((t + 2 * pad, c), F32)],
        compiler_params=_params(("parallel",)),
    )(proj, pool_w_bf, pool_scale)


def _pool_bwd(pooled, dpm, pool_w_bf, pool_scale):
    t = pooled.shape[0]
    ng, c, _ = pool_w_bf.shape
    ch = _pick(t, 256, SUBLANES)
    pad = POOL_PAD

    def body(pooled_ref, dpm_ref, w_ref, s_ref, dp_ref, dw_ref, ds_ref, buf, raw):
        grp = pl.program_id(0)
        buf[pl.ds(0, pad), :] = jnp.zeros((pad, c), F32)
        buf[pl.ds(pad + t, pad), :] = jnp.zeros((pad, c), F32)
        dw_ref[...] = jnp.zeros_like(dw_ref)
        ds_ref[...] = jnp.zeros_like(ds_ref)
        for gi, w in enumerate(POOL_WINDOWS):
            left = w // 2
            right = w - 1 - left

            @pl.when(grp == gi)
            def _(left=left, right=right):
                def first(ci, carry):
                    r0 = pl.multiple_of(ci * ch, SUBLANES)
                    pv = pooled_ref[pl.ds(r0, ch), :]
                    dpm_v = dpm_ref[pl.ds(r0, ch), :]
                    mixed = jnp.dot(pv, w_ref[0], preferred_element_type=F32)
                    ds_ref[...] += jnp.sum(dpm_v * mixed, axis=0, keepdims=True)
                    dmixed = (dpm_v * s_ref[...]).astype(BF16)
                    dw_ref[0] += lax.dot_general(pv, dmixed, _DIMS["tn"], preferred_element_type=F32)
                    dpooled = lax.dot_general(dmixed, w_ref[0], _DIMS["nt"], preferred_element_type=F32)
                    raw[pl.ds(r0, ch), :] = dpooled
                    buf[pl.ds(pl.multiple_of(r0 + pad, SUBLANES), ch), :] = (
                        dpooled / _pool_counts(r0, ch, c, left, right, t))
                    return carry

                lax.fori_loop(0, t // ch, first, 0)

                def second(ci, carry):
                    r0 = pl.multiple_of(ci * ch, SUBLANES)
                    win = buf[pl.ds(r0, ch + 2 * pad), :]
                    s = _window_sum(win, range(-right, left + 1))[pad:pad + ch]
                    dp_ref[pl.ds(r0, ch), :] = (s - raw[pl.ds(r0, ch), :]).astype(BF16)
                    return carry

                lax.fori_loop(0, t // ch, second, 0)

    col = pl.BlockSpec((t, c), lambda g: (0, g))
    return pl.pallas_call(
        body, name="pool_bwd", grid=(ng,),
        in_specs=[col, col, pl.BlockSpec((1, c, c), lambda g: (g, 0, 0)), pl.BlockSpec((1, c), lambda g: (0, g))],
        out_specs=[col, pl.BlockSpec((1, c, c), lambda g: (g, 0, 0)), pl.BlockSpec((1, c), lambda g: (0, g))],
        out_shape=[_out(t, ng * c, BF16), jax.ShapeDtypeStruct((ng, c, c), F32), _out(1, ng * c, F32)],
        scratch_shapes=[pltpu.VMEM((t + 2 * pad, c), F32), pltpu.VMEM((t, c), F32)],
        compiler_params=_params(("parallel",)),
    )(pooled, dpm, pool_w_bf, pool_scale)


def _discretise(a_re, a_im, log_dt, b_re, b_im):
    dt = jnp.exp(log_dt)
    mag = jnp.exp(dt * a_re)
    ang = dt * a_im
    abr = mag * jnp.cos(ang)
    abi = mag * jnp.sin(ang)
    den = a_re * a_re + a_im * a_im
    nr = abr - 1.0
    qr = (nr * a_re + abi * a_im) / den
    qi = (abi * a_re - nr * a_im) / den
    return abr, abi, qr * b_re - qi * b_im, qr * b_im + qi * b_re


def _ssm_disc(cols):
    n, hh = cols[3].shape

    def body(ar, ai, ld, br, bi, o1, o2, o3, o4):
        res = _discretise(ar[...], ai[...], ld[...], br[...], bi[...])
        for o, r in zip((o1, o2, o3, o4), res):
            o[...] = r

    return pl.pallas_call(
        body, name="ssm_disc",
        out_shape=[_out(n, 1, F32), _out(n, 1, F32), _out(n, hh, F32), _out(n, hh, F32)],
    )(*cols)


def _ssm_disc_bwd(cols, cots):
    n, hh = cols[3].shape

    def body(ar, ai, ld, br, bi, c1, c2, c3, c4, o1, o2, o3, o4, o5):
        _, vjp = jax.vjp(_discretise, ar[...], ai[...], ld[...], br[...], bi[...])
        res = vjp((c1[...], c2[...], c3[...], c4[...]))
        for o, r in zip((o1, o2, o3, o4, o5), res):
            o[...] = r

    return pl.pallas_call(
        body, name="ssm_disc_bwd",
        out_shape=[_out(n, 1, F32)] * 3 + [_out(n, hh, F32)] * 2,
    )(*cols, *cots)


def _rowsum(name, a):
    r, _ = a.shape

    def body(a_ref, o_ref):
        o_ref[...] = jnp.sum(a_ref[...], axis=-1, keepdims=True)

    return pl.pallas_call(body, name=name, out_shape=_out(r, 1, F32))(a)


def _cmul(pr, pi, qr, qi):
    return pr * qr - pi * qi, pr * qi + pi * qr


def _cpow(pr, pi, n):
    rr, ri = None, None
    while n:
        if n & 1:
            rr, ri = (pr, pi) if rr is None else _cmul(rr, ri, pr, pi)
        n >>= 1
        if n:
            pr, pi = _cmul(pr, pi, pr, pi)
    return rr, ri


def _segment_carry(er, ei, pr, pi, reverse):
    row = lax.broadcasted_iota(jnp.int32, er.shape, 0)
    cr, ci = jnp.zeros_like(er), jnp.zeros_like(ei)
    for _ in range(SUBLANES - 1):
        tr = er + pr * cr - pi * ci
        ti = ei + pr * ci + pi * cr
        if reverse:
            keep, shift = row < SUBLANES - 1, SUBLANES - 1
        else:
            keep, shift = row >= 1, 1
        cr = jnp.where(keep, pltpu.roll(tr, shift, 0), 0.0)
        ci = jnp.where(keep, pltpu.roll(ti, shift, 0), 0.0)
    return cr, ci


def _ssm_fwd(name, sp, b_re, b_im, c_re, c_im, ar, ai, reverse):
    t, c = sp.shape
    s = ar.shape[1]
    w = _pick(s, 512)
    ch = _pick(t, 512, SUBLANES)
    n_ch, gpc, steps = t // ch, ch // SUBLANES, t // SUBLANES

    def body(sp_ref, bre_ref, bim_ref, cre_ref, cim_ref, ar_ref, ai_ref, xr_ref, xi_ref, y_ref, ur, ui, xbr, xbi):
        a_r = jnp.broadcast_to(ar_ref[...], (SUBLANES, w))
        a_i = jnp.broadcast_to(ai_ref[...], (SUBLANES, w))

        @pl.when(pl.program_id(0) == 0)
        def _():
            y_ref[...] = jnp.zeros_like(y_ref)

        def sweep(h0, store):
            def chunk(k, h):
                ci = n_ch - 1 - k if reverse else k
                rows = pl.ds(pl.multiple_of(ci * ch, ch), ch)
                spv = sp_ref[rows, :].astype(BF16)
                ur[...] = jnp.dot(spv, bre_ref[...], preferred_element_type=F32)
                ui[...] = jnp.dot(spv, bim_ref[...], preferred_element_type=F32)

                def group(g, hh):
                    gi = gpc - 1 - g if reverse else g
                    r0 = pl.multiple_of(gi * SUBLANES, SUBLANES)
                    hr, hi = hh
                    nr = a_r * hr - a_i * hi + ur[pl.ds(r0, SUBLANES), :]
                    ni = a_r * hi + a_i * hr + ui[pl.ds(r0, SUBLANES), :]
                    if store:
                        xbr[pl.ds(r0, SUBLANES), :] = nr
                        xbi[pl.ds(r0, SUBLANES), :] = ni
                    return nr, ni

                h = lax.fori_loop(0, gpc, group, h)
                if store:
                    xr16, xi16 = xbr[...].astype(BF16), xbi[...].astype(BF16)
                    xr_ref[rows, :] = xr16
                    xi_ref[rows, :] = xi16
                    y_ref[rows, :] += (jnp.dot(xr16, cre_ref[...], preferred_element_type=F32)
                                       + jnp.dot(xi16, cim_ref[...], preferred_element_type=F32))
                return h

            return lax.fori_loop(0, n_ch, chunk, h0)

        zero = jnp.zeros((SUBLANES, w), F32)
        er, ei = sweep((zero, zero), False)
        pr, pi = _cpow(ar_ref[...], ai_ref[...], steps)
        sweep(_segment_carry(er, ei, pr, pi, reverse), True)

    col = lambda i: (0, i)
    return pl.pallas_call(
        body, name=name, grid=(s // w,),
        in_specs=[pl.BlockSpec((t, c), lambda i: (0, 0)), pl.BlockSpec((c, w), col), pl.BlockSpec((c, w), col),
                  pl.BlockSpec((w, c), lambda i: (i, 0)), pl.BlockSpec((w, c), lambda i: (i, 0)),
                  pl.BlockSpec((1, w), col), pl.BlockSpec((1, w), col)],
        out_specs=[pl.BlockSpec((t, w), col), pl.BlockSpec((t, w), col), pl.BlockSpec((t, c), lambda i: (0, 0))],
        out_shape=[_out(t, s, BF16), _out(t, s, BF16), _out(t, c, F32)],
        scratch_shapes=[pltpu.VMEM((ch, w), F32)] * 4,
        compiler_params=_params(("arbitrary",)),
    )(sp, b_re, b_im, c_re, c_im, ar, ai)


def _ssm_bwd(name, dyp, c_re, c_im, xr, xi, ar, ai, reverse):
    t, c = dyp.shape
    s = ar.shape[1]
    w = _pick(s, 512)
    ch = _pick(t, 512, SUBLANES)
    n_ch, gpc, steps = t // ch, ch // SUBLANES, t // SUBLANES
    back = not reverse
    edge = 2 * SUBLANES

    def body(dy_ref, cre_ref, cim_ref, xr_ref, xi_ref, ar_ref, ai_ref, lr_ref, li_ref, dar_ref, dai_ref,
             gr, gi_, lbr, lbi, xbr, xbi):
        a_r = jnp.broadcast_to(ar_ref[...], (SUBLANES, w))
        a_i = -jnp.broadcast_to(ai_ref[...], (SUBLANES, w))
        row = lax.broadcasted_iota(jnp.int32, (SUBLANES, w), 0)

        def neighbours(ci, x_ref, buf):
            rows = pl.ds(pl.multiple_of(ci * ch, ch), ch)
            if reverse:
                buf[pl.ds(0, ch), :] = x_ref[rows, :].astype(F32)
                nxt = x_ref[pl.ds(pl.multiple_of(jnp.minimum(ci + 1, n_ch - 1) * ch, ch), edge), :].astype(F32)[:SUBLANES]
                first = x_ref[pl.ds(0, edge), :].astype(F32)[:SUBLANES]
                wrap = jnp.where(row < SUBLANES - 1, pltpu.roll(first, SUBLANES - 1, 0), 0.0)
                buf[pl.ds(ch, SUBLANES), :] = jnp.where(ci == n_ch - 1, wrap, nxt)
            else:
                buf[pl.ds(SUBLANES, ch), :] = x_ref[rows, :].astype(F32)
                prv = x_ref[pl.ds(pl.multiple_of(jnp.maximum(ci * ch - edge, 0), edge), edge), :].astype(F32)[SUBLANES:]
                last = x_ref[pl.ds(t - edge, edge), :].astype(F32)[SUBLANES:]
                wrap = jnp.where(row >= 1, pltpu.roll(last, 1, 0), 0.0)
                buf[pl.ds(0, SUBLANES), :] = jnp.where(ci == 0, wrap, prv)

        def sweep(h0, store):
            def chunk(k, carry):
                ci = n_ch - 1 - k if back else k
                rows = pl.ds(pl.multiple_of(ci * ch, ch), ch)
                dyv = dy_ref[rows, :].astype(BF16)
                gr[...] = lax.dot_general(dyv, cre_ref[...], _DIMS["nt"], preferred_element_type=F32)
                gi_[...] = lax.dot_general(dyv, cim_ref[...], _DIMS["nt"], preferred_element_type=F32)
                if store:
                    neighbours(ci, xr_ref, xbr)
                    neighbours(ci, xi_ref, xbi)

                def group(g, cc):
                    gidx = gpc - 1 - g if back else g
                    r0 = pl.multiple_of(gidx * SUBLANES, SUBLANES)
                    hr, hi = cc[0], cc[1]
                    nr = a_r * hr - a_i * hi + gr[pl.ds(r0, SUBLANES), :]
                    ni = a_r * hi + a_i * hr + gi_[pl.ds(r0, SUBLANES), :]
                    if not store:
                        return nr, ni
                    lbr[pl.ds(r0, SUBLANES), :] = nr
                    lbi[pl.ds(r0, SUBLANES), :] = ni
                    x0 = pl.multiple_of(r0 + SUBLANES, SUBLANES) if reverse else r0
                    xpr, xpi = xbr[pl.ds(x0, SUBLANES), :], xbi[pl.ds(x0, SUBLANES), :]
                    return nr, ni, cc[2] + nr * xpr + ni * xpi, cc[3] + ni * xpr - nr * xpi

                carry = lax.fori_loop(0, gpc, group, carry)
                if store:
                    lr_ref[rows, :] = lbr[...].astype(BF16)
                    li_ref[rows, :] = lbi[...].astype(BF16)
                return carry

            return lax.fori_loop(0, n_ch, chunk, h0)

        zero = jnp.zeros((SUBLANES, w), F32)
        er, ei = sweep((zero, zero), False)
        pr, pi = _cpow(ar_ref[...], -ai_ref[...], steps)
        cr, ci0 = _segment_carry(er, ei, pr, pi, back)
        _, _, dar, dai = sweep((cr, ci0, zero, zero), True)
        dar_ref[...] = jnp.sum(dar, axis=0, keepdims=True)
        dai_ref[...] = jnp.sum(dai, axis=0, keepdims=True)

    col = lambda i: (0, i)
    return pl.pallas_call(
        body, name=name, grid=(s // w,),
        in_specs=[pl.BlockSpec((t, c), lambda i: (0, 0)), pl.BlockSpec((w, c), lambda i: (i, 0)),
                  pl.BlockSpec((w, c), lambda i: (i, 0)), pl.BlockSpec((t, w), col), pl.BlockSpec((t, w), col),
                  pl.BlockSpec((1, w), col), pl.BlockSpec((1, w), col)],
        out_specs=[pl.BlockSpec((t, w), col), pl.BlockSpec((t, w), col), pl.BlockSpec((1, w), col), pl.BlockSpec((1, w), col)],
        out_shape=[_out(t, s, BF16), _out(t, s, BF16), _out(1, s, F32), _out(1, s, F32)],
        scratch_shapes=[pltpu.VMEM((ch, w), F32)] * 4 + [pltpu.VMEM((ch + SUBLANES, w), F32)] * 2,
        compiler_params=_params(("parallel",)),
    )(dyp, c_re, c_im, xr, xi, ar, ai)


def _to_segments(a):
    t, c = a.shape
    return a.reshape(SUBLANES, t // SUBLANES, c).transpose(1, 0, 2).reshape(t, c)


def _from_segments(a):
    t, c = a.shape
    return a.reshape(t // SUBLANES, SUBLANES, c).transpose(1, 0, 2).reshape(t, c)


def _colsum_prod(name, a, b, b_coff=0):
    t, n = a.shape
    tm = _pick(t, 512, SUBLANES)

    def body(a_ref, b_ref, o_ref):
        @pl.when(pl.program_id(0) == 0)
        def _():
            o_ref[...] = jnp.zeros_like(o_ref)

        o_ref[...] += jnp.sum(a_ref[...].astype(F32) * b_ref[...].astype(F32), axis=0, keepdims=True)

    return pl.pallas_call(
        body, name=name, grid=(t // tm,),
        in_specs=[pl.BlockSpec((tm, n), lambda i: (i, 0)), pl.BlockSpec((tm, n), lambda i: (i, b_coff))],
        out_specs=pl.BlockSpec((1, n), lambda i: (0, 0)), out_shape=_out(1, n, F32),
        compiler_params=_params(("arbitrary",)),
    )(a, b)


def _bd_in(bb, g, p, hh):
    blk = bb.reshape(g, p, hh).transpose(0, 2, 1)
    eye = jnp.eye(g, dtype=bool)[:, None, :, None]
    return jnp.where(eye, blk[:, :, None, :], 0.0).reshape(g * hh, g * p)


def _bd_out(cc, g, p, hh):
    blk = cc.transpose(0, 2, 1)
    eye = jnp.eye(g, dtype=bool)[:, None, :, None]
    return jnp.where(eye, blk[:, :, None, :], 0.0).reshape(g * p, g * hh)


def _diag_in(dmat, g, p, hh):
    eye = jnp.eye(g, dtype=bool)[:, None, :, None]
    diag = jnp.sum(jnp.where(eye, dmat.reshape(g, hh, g, p), 0.0), axis=2)
    return diag.transpose(0, 2, 1).reshape(g * p, hh)


def _diag_out(dmat, g, p, hh):
    eye = jnp.eye(g, dtype=bool)[:, None, :, None]
    diag = jnp.sum(jnp.where(eye, dmat.reshape(g, p, g, hh), 0.0), axis=2)
    return diag.transpose(0, 2, 1)


def _softmax(qh, kh, scale):
    s = lax.dot_general(qh, kh, _DIMS["nt"], preferred_element_type=F32) * scale
    e = jnp.exp(s - jnp.max(s, axis=-1, keepdims=True))
    return e / jnp.sum(e, axis=-1, keepdims=True)


def _attn_fwd(q, kv):
    t, d = q.shape
    mm_ = kv.shape[0]
    hd = d // N_XHEADS
    scale = 1.0 / math.sqrt(hd)
    tm = _pick(t, 512, SUBLANES)

    def body(q_ref, kv_ref, o_ref):
        for h in range(N_XHEADS):
            sl = pl.ds(h * hd, hd)
            p = _softmax(q_ref[:, sl], kv_ref[:, sl], scale)
            o_ref[:, sl] = jnp.dot(p.astype(BF16), kv_ref[:, pl.ds(d + h * hd, hd)],
                                   preferred_element_type=F32).astype(BF16)

    return pl.pallas_call(
        body, name="attn_fwd", grid=(t // tm,),
        in_specs=[pl.BlockSpec((tm, d), lambda i: (i, 0)), pl.BlockSpec((mm_, 2 * d), lambda i: (0, 0))],
        out_specs=pl.BlockSpec((tm, d), lambda i: (i, 0)), out_shape=_out(t, d, BF16),
        compiler_params=_params(("parallel",)),
    )(q, kv)


def _attn_bwd(q, kv, do):
    t, d = q.shape
    mm_ = kv.shape[0]
    hd = d // N_XHEADS
    scale = 1.0 / math.sqrt(hd)
    tm = _pick(t, 512, SUBLANES)

    def body(q_ref, kv_ref, do_ref, dq_ref, dkv_ref):
        @pl.when(pl.program_id(0) == 0)
        def _():
            dkv_ref[...] = jnp.zeros_like(dkv_ref)

        for h in range(N_XHEADS):
            sl = pl.ds(h * hd, hd)
            vsl = pl.ds(d + h * hd, hd)
            qh, kh, doh = q_ref[:, sl], kv_ref[:, sl], do_ref[:, sl]
            p = _softmax(qh, kh, scale)
            dp = lax.dot_general(doh, kv_ref[:, vsl], _DIMS["nt"], preferred_element_type=F32)
            dkv_ref[:, vsl] += lax.dot_general(p.astype(BF16), doh, _DIMS["tn"], preferred_element_type=F32)
            ds = (p * (dp - jnp.sum(dp * p, axis=-1, keepdims=True)) * scale).astype(BF16)
            dq_ref[:, sl] = jnp.dot(ds, kh, preferred_element_type=F32).astype(BF16)
            dkv_ref[:, sl] += lax.dot_general(ds, qh, _DIMS["tn"], preferred_element_type=F32)

    row = pl.BlockSpec((tm, d), lambda i: (i, 0))
    full = pl.BlockSpec((mm_, 2 * d), lambda i: (0, 0))
    return pl.pallas_call(
        body, name="attn_bwd", grid=(t // tm,), in_specs=[row, full, row], out_specs=[row, full],
        out_shape=[_out(t, d, BF16), _out(mm_, 2 * d, F32)], compiler_params=_params(("arbitrary",)),
    )(q, kv, do)


def _ew(name, fn, ins, outs, rows_pref=256, rowvecs=()):
    r, c = ins[0].shape
    tr = _pick(r, rows_pref, SUBLANES)
    ni = len(ins) + len(rowvecs)

    def body(*refs):
        res = fn(*[x[...] for x in refs[:ni]])
        for o_ref, v in zip(refs[ni:], res):
            o_ref[...] = v.astype(o_ref.dtype)

    blk = pl.BlockSpec((tr, c), lambda i: (i, 0))
    vec = pl.BlockSpec((1, c), lambda i: (0, 0))
    return pl.pallas_call(
        body, name=name, grid=(r // tr,), in_specs=[blk] * len(ins) + [vec] * len(rowvecs), out_specs=[blk] * len(outs),
        out_shape=[_out(r, c, dt) for dt in outs], compiler_params=_params(("parallel",)),
    )(*ins, *rowvecs)


def _sum_slots(name, a, dtype):
    s, r, c = a.shape
    tr = _pick(r, 256, SUBLANES)

    def body(a_ref, o_ref):
        acc = a_ref[0].astype(F32)
        for k in range(1, s):
            acc = acc + a_ref[k].astype(F32)
        o_ref[...] = acc.astype(o_ref.dtype)

    return pl.pallas_call(
        body, name=name, grid=(r // tr,), in_specs=[pl.BlockSpec((s, tr, c), lambda i: (0, i, 0))],
        out_specs=pl.BlockSpec((tr, c), lambda i: (i, 0)), out_shape=_out(r, c, dtype),
        compiler_params=_params(("parallel",)),
    )(a)


def _adamw_step(wv, gv, mv, vv):
    bc1 = 1.0 - ADAM_B1 ** ADAM_STEP
    bc2 = 1.0 - ADAM_B2 ** ADAM_STEP
    m2 = ADAM_B1 * mv + (1.0 - ADAM_B1) * gv
    v2 = ADAM_B2 * vv + (1.0 - ADAM_B2) * (gv * gv)
    delta = -ADAM_LR * ((m2 / bc1) / (jnp.sqrt(v2 / bc2) + ADAM_EPS) + ADAM_WD * wv)
    return delta, m2, v2


def _adamw_group(name, items, transposed):
    k, r = items[0][0].shape
    if transposed and r % LANES != 0:
        rows = _adamw_group(name, [(w.T, g, m.T, v.T) for w, g, m, v in items], False)
        return [[a.T for a in item] for item in rows]
    tk = _pick(k, max(SUBLANES, ADAMW_STEP_WORDS // (r * len(items))), SUBLANES)
    n_out = 4 if transposed else 3

    def body(*refs):
        ins, outs = refs[:4 * len(items)], refs[4 * len(items):]
        for i in range(len(items)):
            wv, gv, mv, vv = (a[...] for a in ins[4 * i:4 * i + 4])
            if transposed:
                gv = gv.T
            res = _adamw_step(wv, gv, mv, vv) + ((gv,) if transposed else ())
            for o_ref, val in zip(outs[n_out * i:n_out * (i + 1)], res):
                o_ref[...] = val

    blk = pl.BlockSpec((tk, r), lambda j: (j, 0))
    g_blk = pl.BlockSpec((r, tk), lambda j: (0, j)) if transposed else blk
    res = pl.pallas_call(
        body, name=name, grid=(k // tk,), in_specs=[blk, g_blk, blk, blk] * len(items),
        out_specs=[blk] * (n_out * len(items)), out_shape=[pltpu.HBM((k, r), F32)] * (n_out * len(items)),
        compiler_params=_params(("parallel",)),
    )(*[pltpu.with_memory_space_constraint(a, pltpu.HBM) for item in items for a in item])
    return [list(res[n_out * i:n_out * (i + 1)]) + ([] if transposed else [items[i][1]]) for i in range(len(items))]


def _allgather(name, arrs):
    n = len(arrs)

    def body(*refs):
        ins, outs = refs[:n], refs[n:2 * n]
        send_sems, recv_sems, local_sems = refs[2 * n:]
        x, y, c = lax.axis_index("x"), lax.axis_index("y"), lax.axis_index("c")
        me, sibling = (x, y, c), (x, y, 1 - c)
        chips = [(1 - x, y), (x, 1 - y), (1 - x, 1 - y)]

        def rows(a, px, py, pc):
            r = ins[a].shape[0]
            return outs[a].at[pl.ds((4 * px + 2 * py + pc) * r, r), :]

        def copy(a, k, block, to, src=None):
            return pltpu.make_async_remote_copy(
                src_ref=rows(a, *block) if src is None else src, dst_ref=rows(a, *block),
                send_sem=send_sems.at[a, k], recv_sem=recv_sems.at[a, k], device_id=to, device_id_type=MESH)

        mine = [pltpu.make_async_copy(ins[a], rows(a, *me), local_sems.at[a]) for a in range(n)]
        for cp in mine:
            cp.start()
        first = []
        for a in range(n):
            first.append(copy(a, 0, me, sibling, src=ins[a]))
            first += [copy(a, 1 + j, me, (*chip, c), src=ins[a]) for j, chip in enumerate(chips)]
        for cp in first:
            cp.start()
        passed = []
        for j, chip in enumerate(chips):
            for a in range(n):
                copy(a, 1 + j, (*chip, c), me).wait_recv()
                cp = copy(a, 4 + j, (*chip, c), sibling)
                cp.start()
                passed.append(cp)
        for a in range(n):
            copy(a, 0, sibling, me).wait_recv()
            for j, chip in enumerate(chips):
                copy(a, 4 + j, (*chip, 1 - c), me).wait_recv()
        for cp in first + passed:
            cp.wait_send()
        for cp in mine:
            cp.wait()

    return pl.pallas_call(
        body, name=name, in_specs=[ANY] * n, out_specs=[ANY] * n,
        out_shape=[_out(N_DEV * a.shape[0], a.shape[1], a.dtype) for a in arrs],
        scratch_shapes=[pltpu.SemaphoreType.DMA((n, 7)), pltpu.SemaphoreType.DMA((n, 7)), pltpu.SemaphoreType.DMA((n,))],
    )(*arrs)


def _exchange_cores(name, blocks):
    n = len(blocks)
    c = blocks[0].shape[2]
    r = sum(b.shape[1] for b in blocks)

    def body(*refs):
        srcs, (recv_ref, send_sems, recv_sems) = refs[:n], refs[n:]
        x, y, cc = lax.axis_index("x"), lax.axis_index("y"), lax.axis_index("c")
        copies, off = [], 0
        for a, src in enumerate(srcs):
            rows = pl.ds(off, src.shape[1])
            off += src.shape[1]
            for q in range(4):
                copies.append(pltpu.make_async_remote_copy(
                    src_ref=src.at[2 * q + (1 - cc)], dst_ref=recv_ref.at[q, rows], send_sem=send_sems.at[a, q],
                    recv_sem=recv_sems.at[a, q], device_id=(x, y, 1 - cc), device_id_type=MESH))
        for cp in copies:
            cp.start()
        for cp in copies:
            cp.wait()

    return pl.pallas_call(
        body, name=name, in_specs=[ANY] * n, out_specs=ANY,
        out_shape=jax.ShapeDtypeStruct((4, r, c), blocks[0].dtype),
        scratch_shapes=[pltpu.SemaphoreType.DMA((n, 4))] * 2,
    )(*blocks)


def _peer(k, x, y, c):
    return (1 - x if k & 4 else x, 1 - y if k & 2 else y, 1 - c if k & 1 else c)


def _split_start(name, groups, after=None):
    pins = [] if after is None else [after]
    bufs, sem_shapes, spans = [], [], []
    for srcs, land_shapes, n_remote, n_local, _ in groups:
        sems = [pltpu.SemaphoreType.DMA((n_remote,)), pltpu.SemaphoreType.DMA((n_remote,))]
        sems += [pltpu.SemaphoreType.DMA((n_local,))] if n_local else []
        spans.append((len(bufs), len(srcs), len(land_shapes), len(sem_shapes), len(sems)))
        bufs += [pltpu.with_memory_space_constraint(a, pltpu.HBM) for a in srcs]
        bufs += [pltpu.with_memory_space_constraint(lax.empty(s.shape, s.dtype), pltpu.HBM) for s in land_shapes]
        sem_shapes += sems
    n_buf, n_sem = len(bufs), len(sem_shapes)

    def body(*refs):
        buf_refs, sem_refs, token = refs[:n_buf], refs[n_buf + len(pins):n_buf + len(pins) + n_sem], refs[-1]
        for (b0, ns, nl, s0, k), group in zip(spans, groups):
            remote, local = group[4](buf_refs[b0:b0 + ns], buf_refs[b0 + ns:b0 + ns + nl], *sem_refs[s0:s0 + k])
            for cp in local + remote:
                cp.start()
        token[...] = jnp.zeros_like(token)

    outs = pl.pallas_call(
        body, name=name,
        out_shape=sem_shapes + [pltpu.HBM(b.shape, b.dtype) for b in bufs] + [jax.ShapeDtypeStruct((SUBLANES, LANES), F32)],
        in_specs=[HBM] * n_buf + [ANY] * len(pins),
        out_specs=[SEM] * n_sem + [HBM] * n_buf + [pl.BlockSpec(memory_space=pltpu.VMEM)],
        input_output_aliases={i: n_sem + i for i in range(n_buf)},
        compiler_params=pltpu.CompilerParams(has_side_effects=SIDE_EFFECT),
    )(*bufs, *pins)
    return [dict(sems=list(outs[s0:s0 + k]), bufs=list(outs[n_sem + b0:n_sem + b0 + ns + nl]), token=outs[-1],
                 build=group[4], ns=ns) for (b0, ns, nl, s0, k), group in zip(spans, groups)]


def _split_wait(name, started, after):
    ns, n_buf, n_sem = started["ns"], len(started["bufs"]), len(started["sems"])

    def body(*refs):
        src_refs, land_refs = refs[:ns], refs[ns:n_buf]
        sems = refs[n_buf:n_buf + n_sem]
        remote, local = started["build"](src_refs, land_refs, *sems)
        for cp in local:
            cp.wait()
        for cp in remote:
            cp.wait_send()
            cp.wait_recv()

    outs = pl.pallas_call(
        body, name=name, out_shape=[pltpu.HBM(b.shape, b.dtype) for b in started["bufs"]],
        in_specs=[HBM] * n_buf + [SEM] * n_sem + [ANY], out_specs=[HBM] * n_buf,
        input_output_aliases={i: i for i in range(n_buf)},
        compiler_params=pltpu.CompilerParams(has_side_effects=SIDE_EFFECT),
    )(*started["bufs"], *started["sems"], after)
    return list(outs[:ns]), list(outs[ns:])


def _gather_group(shards):
    m = len(shards)

    def build(src_refs, land_refs, send_sems, recv_sems, local_sems):
        x, y, c = lax.axis_index("x"), lax.axis_index("y"), lax.axis_index("c")
        remote, local = [], []
        for j in range(m):
            r = src_refs[j].shape[0]
            dst = land_refs[j].at[pl.ds((4 * x + 2 * y + c) * r, r), :]
            local.append(pltpu.make_async_copy(src_refs[j], dst, local_sems.at[j]))
            for k in range(1, N_DEV):
                remote.append(pltpu.make_async_remote_copy(
                    src_ref=src_refs[j], dst_ref=dst, send_sem=send_sems.at[7 * j + k - 1],
                    recv_sem=recv_sems.at[7 * j + k - 1], device_id=_peer(k, x, y, c), device_id_type=MESH))
        return remote, local

    lands = [jax.ShapeDtypeStruct((N_DEV * a.shape[0], a.shape[1]), a.dtype) for a in shards]
    return shards, lands, 7 * m, m, build


def _slots_start(name, a):
    def build(src_refs, land_refs, send_sems, recv_sems, local_sems):
        x, y, c = lax.axis_index("x"), lax.axis_index("y"), lax.axis_index("c")
        dst = land_refs[0].at[4 * x + 2 * y + c]
        local = [pltpu.make_async_copy(src_refs[0], dst, local_sems.at[0])]
        remote = [pltpu.make_async_remote_copy(
            src_ref=src_refs[0], dst_ref=dst, send_sem=send_sems.at[k - 1], recv_sem=recv_sems.at[k - 1],
            device_id=_peer(k, x, y, c), device_id_type=MESH) for k in range(1, N_DEV)]
        return remote, local

    return _split_start(name, [([a], [jax.ShapeDtypeStruct((N_DEV,) + a.shape, a.dtype)], 7, 1, build)])[0]


def _chips_start(name, p):
    _, r, c = p.shape
    nck = r // GRAD_ROW_TILE

    def build(src_refs, land_refs, send_sems, recv_sems):
        x, y, cc = lax.axis_index("x"), lax.axis_index("y"), lax.axis_index("c")
        remote = []
        for k in range(1, 4):
            px = 1 - x if k >> 1 else x
            py = 1 - y if k & 1 else y
            for j in range(nck):
                rows = pl.ds(j * GRAD_ROW_TILE, GRAD_ROW_TILE)
                remote.append(pltpu.make_async_remote_copy(
                    src_ref=src_refs[0].at[2 * px + py, rows], dst_ref=land_refs[0].at[k - 1, rows],
                    send_sem=send_sems.at[(k - 1) * nck + j], recv_sem=recv_sems.at[(k - 1) * nck + j],
                    device_id=(px, py, cc), device_id_type=MESH))
        return remote, []

    return _split_start(name, [([p], [jax.ShapeDtypeStruct((3, r, c), p.dtype)], 3 * nck, 0, build)])[0]


def _chip_sum(name, p, recv, chip):
    _, r, c = p.shape
    tr = _pick(r, 5 * GRAD_ROW_TILE, GRAD_ROW_TILE)

    def body(chip_ref, p_ref, r_ref, o_ref):
        acc = p_ref[...].astype(F32)
        for k in range(3):
            acc = acc + r_ref[k].astype(F32)
        o_ref[...] = acc

    return pl.pallas_call(
        body, name=name,
        grid_spec=pltpu.PrefetchScalarGridSpec(
            num_scalar_prefetch=1, grid=(r // tr,),
            in_specs=[pl.BlockSpec((None, tr, c), lambda i, chip_ref: (chip_ref[0], i, 0)),
                      pl.BlockSpec((3, tr, c), lambda i, chip_ref: (0, i, 0))],
            out_specs=pl.BlockSpec((tr, c), lambda i, chip_ref: (i, 0))),
        out_shape=_out(r, c, F32), compiler_params=_params(("parallel",)),
    )(chip, p, recv)


def _local_step(x, mem, tgt, wt, sm, ev=None):
    t, d = x.shape
    n_mem = mem.shape[0]
    d_pool = sm["pool_scale"].shape[1]
    ng, pc = sm["pool_w"].shape[0], sm["pool_w"].shape[1]
    d_ssm = sm["ssm_d"].shape[1]
    _, sg, sp, sh = sm["ssm_b_re"].shape
    n_state = sg * sp
    gb, gs = {}, {}

    def emit(name, **kw):
        return ev(name, **kw) if ev is not None else None

    n1 = _rms_fwd("ffn1_norm", x, sm["ffn1_norm"])
    emit("ffn1_norm_done", marker=n1)
    def ffn1_down(hid):
        emit("ffn1_up_done", marker=hid)
        return wt["ffn1_w_down"]

    h1, ffn1_saved = _ffn_fwd("ffn1", x, n1, wt["ffn1_w_gate"], wt["ffn1_w_up"], ffn1_down)
    emit("ffn1_fwd_done", marker=h1)
    u = _rms_fwd("mix_norm", h1, sm["mix_norm"])
    d_in = wt["w_in"].shape[0]
    tm, tn = _pick(t, 1024), _pick(d_in, 1408)
    proj = _mm1("in_proj", "nt", u, wt["w_in"], t, d_in, tm, tn, F32)
    off_s = d_pool // d_ssm
    off_gp = (d_pool + d_ssm)
    off_gs = off_gp + d

    pool_w_bf = sm["pool_w"].astype(BF16)
    pooled, pm = _pool_fwd(proj, pool_w_bf, sm["pool_scale"])

    cols = [sm["ssm_a_re"].reshape(-1, 1), sm["ssm_a_im"].reshape(-1, 1),
            jnp.broadcast_to(sm["ssm_log_dt"][:, :, None], (2, sg, sp)).reshape(-1, 1),
            sm["ssm_b_re"].reshape(-1, sh), sm["ssm_b_im"].reshape(-1, sh)]
    abr, abi, bbr, bbi = _ssm_disc(cols)
    abr2, abi2 = abr.reshape(2, n_state), abi.reshape(2, n_state)
    bbr4, bbi4 = bbr.reshape(2, sg * sp, sh), bbi.reshape(2, sg * sp, sh)
    b_re = [_bd_in(bbr4[dr], sg, sp, sh).astype(BF16) for dr in range(2)]
    b_im = [_bd_in(bbi4[dr], sg, sp, sh).astype(BF16) for dr in range(2)]
    c_re = [_bd_out(sm["ssm_c_re"][dr], sg, sp, sh).astype(BF16) for dr in range(2)]
    c_im = [_bd_out(-sm["ssm_c_im"][dr], sg, sp, sh).astype(BF16) for dr in range(2)]
    sp32 = _to_segments(proj[:, d_pool:d_pool + d_ssm])
    xs, y_parts = [], []
    for dr in range(2):
        xr, xi, y_part = _ssm_fwd(f"ssm_fwd{dr}", sp32, b_re[dr], b_im[dr], c_re[dr], c_im[dr], abr2[dr:dr + 1],
                                  abi2[dr:dr + 1], reverse=(dr == 1))
        xs.append((xr, xi))
        y_parts.append(y_part)
    y = _from_segments(_ew("ssm_sum", lambda p0, p1, sv, dv: (p0 + p1 + sv * dv,), y_parts + [sp32], [F32],
                           rowvecs=[sm["ssm_d"]])[0])
    tmy = _pick(t, 256)
    ys = _ew("ssm_gelu", lambda v: (jax.nn.gelu(v),), [y], [BF16])[0]
    emit("mix_in_done", marker=ys)

    tmm, tnm, tnx = _pick(t, 1024), _pick(d, 256), _pick(d, 512)
    gp_spec = _tile(tmm, tnm, off_gp // tnm)
    gs_spec = _tile(tmm, tnm, off_gs // tnm)

    def merge_epi(accs, gpv, gsv):
        z_pool, val, gate = accs
        return (jax.nn.sigmoid(gpv) * z_pool + jax.nn.sigmoid(gsv) * (val * jax.nn.sigmoid(gate)),)

    merged = _mm("mix_merge", "nt", [pm, ys], [wt["w_pool_proj"], wt["w_glu_val"], wt["w_glu_gate"]],
                 [[(0, 0)], [(1, 1)], [(1, 2)]], t, d, tmm, tnm, [(proj, gp_spec), (proj, gs_spec)], merge_epi,
                 [(_out(t, d, BF16), None)])[0]
    res_epi = lambda accs, hin: (hin + accs[0],)
    h2 = _mm("mix_out", "nn", [merged], [wt["w_mix_out"]], [[(0, 0)]], t, d, tmm, tnx, [(h1, _tile(tmm, tnx))],
             res_epi, [(_out(t, d, F32), None)])[0]

    un = _rms_fwd("xattn_norm", h2, sm["xattn_norm"])
    mn = _rms_fwd("mem_norm", mem, sm["mem_norm"])
    emit("mix_done", marker=un)
    q = _mm1("xattn_q", "nn", un, wt["w_q"], t, d, tmm, tnx, BF16)
    kv = _mm1("xattn_kv", "nt", mn, wt["w_kv"], n_mem, 2 * d, n_mem, _pick(2 * d, 512), BF16)
    o = _attn_fwd(q, kv)
    h3 = _mm("xattn_out", "nn", [o], [wt["w_xo"]], [[(0, 0)]], t, d, tmm, tnx, [(h2, _tile(tmm, tnx))],
             res_epi, [(_out(t, d, F32), None)])[0]

    n2 = _rms_fwd("ffn2_norm", h3, sm["ffn2_norm"])
    emit("xattn_done", marker=n2)
    h4, ffn2_saved = _ffn_fwd("ffn2", h3, n2, wt["ffn2_w_gate"], wt["ffn2_w_up"], wt["ffn2_w_down"])

    dh4, dh4_bf, gs["final_norm"], loss = _loss_head(h4, sm["final_norm"], tgt)
    dh3, dh3_bf, gs["ffn2_norm"], gb["ffn2_w_gate"], gb["ffn2_w_up"], gb["ffn2_w_down"] = _ffn_bwd(
        "ffn2", h3, sm["ffn2_norm"], wt["ffn2_w_gate"], wt["ffn2_w_up"], wt["ffn2_w_down"], ffn2_saved, dh4, dh4_bf)

    tw = _pick(d, 1024)
    do = _mm1("xattn_do", "nt", dh3_bf, wt["w_xo"], t, d, tmm, tnx, BF16)
    gb["w_xo"] = _mm1("xattn_dwxo", "tn", o, dh3_bf, d, d, tw, tnx, BF16)
    dq, dkv = _attn_bwd(q, kv, do)
    gb["w_q"] = _mm1("xattn_dwq", "tn", un, dq, d, d, tw, tnx, BF16)
    dun = _mm1("xattn_dun", "nt", dq, wt["w_q"], t, d, tmm, tnx, F32)
    dh2, dh2_bf, gs["xattn_norm"] = _rms_bwd("xattn_norm_bwd", h2, sm["xattn_norm"], dun, dh3)
    gb["w_kv"] = _mm1("xattn_dwkv", "tn", dkv, mn, 2 * d, d, _pick(2 * d, 512), d, BF16)
    dmn = _mm1("xattn_dmn", "nn", dkv, wt["w_kv"], n_mem, d, n_mem, tnx, F32)
    gs["mem_norm"] = _rms_bwd("mem_norm_bwd", mem, sm["mem_norm"], dmn)

    gb["w_mix_out"] = _mm1("mix_dwout", "tn", merged, dh2_bf, d, d, tw, tnx, BF16)

    def merge_bwd_epi(accs, gpv, gsv):
        dmerged, z_pool, val, gate = accs
        sp_, ss_, sg_ = jax.nn.sigmoid(gpv), jax.nn.sigmoid(gsv), jax.nn.sigmoid(gate)
        glu = val * sg_
        dz_pool = dmerged * sp_
        dg_pool = dmerged * z_pool * (sp_ * (1.0 - sp_))
        dz_ssm = dmerged * ss_
        dg_ssm = dmerged * glu * (ss_ * (1.0 - ss_))
        dval = dz_ssm * sg_
        dgate = dz_ssm * glu * (1.0 - sg_)
        return dz_pool, dg_pool, dg_ssm, dval, dgate

    dz_pool, dg_pool, dg_ssm, dval, dgate = _mm(
        "mix_merge_bwd", "nt", [dh2_bf, pm, ys], [wt["w_mix_out"], wt["w_pool_proj"], wt["w_glu_val"], wt["w_glu_gate"]],
        [[(0, 0)], [(1, 1)], [(2, 2)], [(2, 3)]], t, d, tmm, tnm, [(proj, gp_spec), (proj, gs_spec)], merge_bwd_epi,
        [(_out(t, d, BF16), None)] * 5)
    gb["w_pool_proj"] = _mm1("pool_dwproj", "tn", dz_pool, pm, d, d_pool, tw, d_pool, BF16)
    gb["w_glu_val"] = _mm1("glu_dwval", "tn", dval, ys, d, d_ssm, tw, d_ssm, BF16)
    gb["w_glu_gate"] = _mm1("glu_dwgate", "tn", dgate, ys, d, d_ssm, tw, d_ssm, BF16)

    def gelu_bwd_epi(accs, yv):
        _, vjp = jax.vjp(jax.nn.gelu, yv)
        return (vjp(accs[0])[0],)

    dy = _mm("glu_dy", "nn", [dval, dgate], [wt["w_glu_val"], wt["w_glu_gate"]], [[(0, 0), (1, 1)]], t, d_ssm, tmy, d_ssm,
             [(y, _tile(tmy, d_ssm))], gelu_bwd_epi, [(_out(t, d_ssm, F32), None)])[0]
    gs["ssm_d"] = _colsum_prod("ssm_dd", dy, proj, b_coff=off_s)
    dyp = _to_segments(dy)
    d_abr, d_abi, d_bbr, d_bbi, d_cre, d_cim, lams = [], [], [], [], [], [], []
    ts = _pick(n_state, 512)
    tc_ = _pick(n_state, 256)
    both = lambda accs: tuple(accs)
    for dr in range(2):
        lr, li, dar, dai = _ssm_bwd(f"ssm_bwd{dr}", dyp, c_re[dr], c_im[dr], xs[dr][0], xs[dr][1], abr2[dr:dr + 1],
                                    abi2[dr:dr + 1], reverse=(dr == 1))
        d_abr.append(dar)
        d_abi.append(dai)
        lams += [lr, li]
        d_br, d_bi = _mm(f"ssm_db{dr}", "tn", [sp32], [lr, li], [[(0, 0)], [(0, 1)]], d_ssm, n_state, d_ssm, ts, [], both,
                         [(_out(d_ssm, n_state, F32), None)] * 2)
        d_bbr.append(_diag_in(d_br, sg, sp, sh))
        d_bbi.append(_diag_in(d_bi, sg, sp, sh))
        d_cr, d_ci = _mm(f"ssm_dc{dr}", "tn", [xs[dr][0], xs[dr][1]], [dyp], [[(0, 0)], [(1, 0)]], n_state, d_ssm, tc_,
                         d_ssm, [], both, [(_out(n_state, d_ssm, F32), None)] * 2)
        d_cre.append(_diag_out(d_cr, sg, sp, sh))
        d_cim.append(-_diag_out(d_ci, sg, sp, sh))
    ds = _from_segments(_mm(
        "ssm_ds", "nt", lams, [b_re[0], b_im[0], b_re[1], b_im[1]], [[(k, k) for k in range(4)]], t, d_ssm, tmy,
        d_ssm, [(dyp, _tile(tmy, d_ssm)), (sm["ssm_d"], _rowvec(d_ssm))],
        lambda accs, dyv, dv: (dyv * dv + accs[0],), [(_out(t, d_ssm, BF16), None)])[0])
    cots = [jnp.concatenate(d_abr, axis=0).reshape(-1, 1), jnp.concatenate(d_abi, axis=0).reshape(-1, 1),
            jnp.concatenate(d_bbr, axis=0), jnp.concatenate(d_bbi, axis=0)]
    d_are, d_aim, d_ldt, d_bre, d_bim = _ssm_disc_bwd(cols, cots)
    gs["ssm_a_re"] = d_are.reshape(2, sg, sp)
    gs["ssm_a_im"] = d_aim.reshape(2, sg, sp)
    gs["ssm_log_dt"] = _rowsum("ssm_dlogdt", d_ldt.reshape(2 * sg, sp)).reshape(2, sg)
    gs["ssm_b_re"] = d_bre.reshape(2, sg, sp, sh)
    gs["ssm_b_im"] = d_bim.reshape(2, sg, sp, sh)
    gs["ssm_c_re"] = jnp.stack(d_cre, axis=0)
    gs["ssm_c_im"] = jnp.stack(d_cim, axis=0)

    dpm = _mm1("pool_dpm", "nn", dz_pool, wt["w_pool_proj"], t, d_pool, tmm, _pick(d_pool, 256), F32)
    dp, gs["pool_w"], gs["pool_scale"] = _pool_bwd(pooled, dpm, pool_w_bf, sm["pool_scale"])

    w_in = wt["w_in"]
    parts = [(dp, 0, d_pool), (ds, d_pool, d_ssm), (dg_pool, off_gp, d), (dg_ssm, off_gs, d)]
    w_in_parts = [w_in[o0:o0 + width] for _, o0, width in parts]
    gb["w_in"] = jnp.concatenate(
        [_mm1(f"in_proj_dw{k}", "tn", p_[0], u, p_[2], d, _pick(p_[2], 1024), tnx, BF16) for k, p_ in enumerate(parts)], axis=0)
    pin = emit("grads_main", gb=gb)
    du = _mm("in_proj_du", "nn", [p_[0] for p_ in parts], w_in_parts, [[(k, k) for k in range(4)]], t, d, tmm, tnx, [],
             lambda accs: (accs[0],), [(_out(t, d, F32), None)], after=pin)[0]
    dh1, dh1_bf, gs["mix_norm"] = _rms_bwd("mix_norm_bwd", h1, sm["mix_norm"], du, dh2)
    pin = emit("small_early", gs=gs, loss=loss)

    def ffn1_weights_done(d_wg, d_wu, d_wd):
        gb["ffn1_w_gate"], gb["ffn1_w_up"], gb["ffn1_w_down"] = d_wg, d_wu, d_wd
        return emit("grads_ffn1", gb=gb)

    dx, _, gs["ffn1_norm"], _, _, _ = _ffn_bwd(
        "ffn1", x, sm["ffn1_norm"], wt["ffn1_w_gate"], wt["ffn1_w_up"], wt["ffn1_w_down"], ffn1_saved, dh1, dh1_bf,
        weights_done=ffn1_weights_done, after=pin)
    return loss, dx, gb, gs


WEIGHTS = ["ffn1_norm", "ffn1_w_gate", "ffn1_w_up", "ffn1_w_down", "mix_norm", "w_in", "pool_w", "pool_scale",
           "w_pool_proj", "ssm_a_re", "ssm_a_im", "ssm_log_dt", "ssm_b_re", "ssm_b_im", "ssm_c_re", "ssm_c_im", "ssm_d",
           "w_glu_val", "w_glu_gate", "w_mix_out", "xattn_norm", "mem_norm", "w_q", "w_kv", "w_xo", "ffn2_norm",
           "ffn2_w_gate", "ffn2_w_up", "ffn2_w_down", "final_norm"]
COL_SHARDED = ["ffn1_w_gate", "ffn1_w_up", "w_in", "w_pool_proj", "w_glu_val", "w_glu_gate", "w_kv", "ffn2_w_gate",
               "ffn2_w_up"]
ROW_SHARDED = ["ffn1_w_down", "w_mix_out", "w_q", "w_xo", "ffn2_w_down"]
BIG = [n for n in WEIGHTS if n in COL_SHARDED or n in ROW_SHARDED]
SMALL = [n for n in WEIGHTS if n not in BIG]
FFN1_BIG = ["ffn1_w_gate", "ffn1_w_up", "ffn1_w_down"]
MAIN_BIG = [n for n in BIG if n not in FFN1_BIG]
GATHER_PLAN = [("ffn1_up_done", ["ffn1_w_down"]), ("ffn1_fwd_done", ["w_in"]),
               ("mix_in_done", ["w_pool_proj", "w_glu_val", "w_glu_gate", "w_mix_out"]),
               ("mix_done", ["w_q", "w_kv", "w_xo"]), ("xattn_done", ["ffn2_w_gate", "ffn2_w_up", "ffn2_w_down"])]
MINOR_SWAPPED = ["ssm_b_re", "ssm_b_im"]
LATE_SMALL = "ffn1_norm"
EARLY_SMALL = [n for n in SMALL if n != LATE_SMALL]
PACK_ROWS = SUBLANES * LANES
GRAD_ROW_TILE = 256
ADAMW_STEP_WORDS = 1 << 19


def _to_rows(name, w, width):
    if name in COL_SHARDED:
        w = w.T
    return w.reshape(-1, width)


def _pack_small(vals):
    flat = []
    for v in vals:
        f = v.reshape(-1)
        flat.append(jnp.pad(f, (0, (-f.shape[0]) % PACK_ROWS)))
    total = sum(f.shape[0] for f in flat)
    flat.append(jnp.zeros(((-total) % (GRAD_ROW_TILE * LANES),), F32))
    return jnp.concatenate(flat).reshape(-1, LANES)


def _unpack_small(packed, shapes):
    out, row = [], 0
    for shp in shapes:
        size = math.prod(shp)
        rows = -(-size // PACK_ROWS) * SUBLANES
        out.append(packed[row:row + rows].reshape(-1)[:size].reshape(shp))
        row += rows
    return out


def kernel(x, mem, ffn1_norm, ffn1_w_gate, ffn1_w_up, ffn1_w_down, mix_norm, w_in, pool_w, pool_scale, w_pool_proj, ssm_a_re, ssm_a_im, ssm_log_dt, ssm_b_re, ssm_b_im, ssm_c_re, ssm_c_im, ssm_d, w_glu_val, w_glu_gate, w_mix_out, xattn_norm, mem_norm, w_q, w_kv, w_xo, ffn2_norm, ffn2_w_gate, ffn2_w_up, ffn2_w_down, final_norm, loss_target, m_ffn1_norm, m_ffn1_w_gate, m_ffn1_w_up, m_ffn1_w_down, m_mix_norm, m_w_in, m_pool_w, m_pool_scale, m_w_pool_proj, m_ssm_a_re, m_ssm_a_im, m_ssm_log_dt, m_ssm_b_re, m_ssm_b_im, m_ssm_c_re, m_ssm_c_im, m_ssm_d, m_w_glu_val, m_w_glu_gate, m_w_mix_out, m_xattn_norm, m_mem_norm, m_w_q, m_w_kv, m_w_xo, m_ffn2_norm, m_ffn2_w_gate, m_ffn2_w_up, m_ffn2_w_down, m_final_norm, v_ffn1_norm, v_ffn1_w_gate, v_ffn1_w_up, v_ffn1_w_down, v_mix_norm, v_w_in, v_pool_w, v_pool_scale, v_w_pool_proj, v_ssm_a_re, v_ssm_a_im, v_ssm_log_dt, v_ssm_b_re, v_ssm_b_im, v_ssm_c_re, v_ssm_c_im, v_ssm_d, v_w_glu_val, v_w_glu_gate, v_w_mix_out, v_xattn_norm, v_mem_norm, v_w_q, v_w_kv, v_w_xo, v_ffn2_norm, v_ffn2_w_gate, v_ffn2_w_up, v_ffn2_w_down, v_final_norm):
    given = dict(locals())
    wts = {n: given[n] for n in WEIGHTS}
    moms = {n: (given["m_" + n], given["v_" + n]) for n in WEIGHTS}
    x2, mem2, tgt2 = x[0], mem[0], loss_target[0]
    d = x2.shape[1]
    chip = (2 * lax.axis_index("x") + lax.axis_index("y")).astype(jnp.int32).reshape(1)

    def full_form(n, f):
        shard = wts[n][0].shape
        return f.reshape(N_DEV * shard[1], shard[0]) if n in COL_SHARDED else f.reshape(N_DEV * shard[0], shard[1])

    shards = {n: _to_rows(n, wts[n][0], d).astype(BF16) for n in BIG}
    first = FFN1_BIG[:2]
    wt = {n: full_form(n, f) for n, f in zip(first, _allgather("weight_allgather_first", [shards[n] for n in first]))}
    started = _split_start("weight_gather_start", [_gather_group([shards[n] for n in names]) for _, names in GATHER_PLAN],
                           after=wt[first[0]])
    gathers = {event: (names, st) for (event, names), st in zip(GATHER_PLAN, started)}
    sm = {n: (wts[n].reshape(1, -1) if wts[n].ndim <= 2 else wts[n][0]) for n in SMALL}
    sm["ffn1_norm"] = sm["ffn1_norm"] + started[0]["token"][0, 0]

    pending = {}

    def reduce_start(tag, names, gb):
        blocks = [gb[n].reshape(N_DEV, -1, d) for n in names]
        pad_rows = (-sum(b.shape[1] for b in blocks)) % GRAD_ROW_TILE
        pad = [jnp.zeros((N_DEV, pad_rows, d), BF16)] if pad_rows else []
        recv = _exchange_cores("grad_exchange_cores_" + tag, blocks + pad)
        own = jnp.concatenate([lax.dynamic_index_in_dim(b.reshape(4, 2, b.shape[1], d), lax.axis_index("c"), 1, False)
                               for b in blocks + pad], axis=1)
        rows_all = own.shape[1]
        pair = _ew("grad_pair_sum_" + tag, lambda a, b: (a.astype(F32) + b.astype(F32),),
                   [own.reshape(-1, d), recv.reshape(-1, d)], [BF16], rows_pref=5 * GRAD_ROW_TILE)[0]
        pair = pair.reshape(4, rows_all, d)
        pending[tag] = (pair, _chips_start("grad_exchange_chips_start_" + tag, pair), [b.shape[1] for b in blocks])
        return pending[tag][1]["token"]

    def reduce_finish(tag, after):
        _, started, rows = pending[tag]
        (pair,), (recv,) = _split_wait("grad_exchange_chips_wait_" + tag, started, after)
        return _chip_sum("grad_chip_sum_" + tag, pair, recv, chip), rows

    def ev(name, gb=None, gs=None, loss=None, marker=None):
        if name in gathers:
            names, started = gathers[name]
            for n, f in zip(names, _split_wait("weight_gather_wait_" + name, started, marker)[1]):
                wt[n] = full_form(n, f)
        elif name == "grads_main":
            return reduce_start("main", MAIN_BIG, gb)
        elif name == "small_early":
            pending["small"] = _slots_start("small_gather_start", _pack_small([gs[n] for n in EARLY_SMALL] + [loss[:, :1]]))
            return pending["small"]["token"]
        elif name == "grads_ffn1":
            return reduce_start("ffn1", FFN1_BIG, gb)
        return None

    _, dx, _, gs = _local_step(x2, mem2, tgt2, wt, sm, ev)

    grads = {}
    for tag, names in (("main", MAIN_BIG), ("ffn1", FFN1_BIG)):
        g_rows, rows = reduce_finish(tag, dx)
        off = 0
        for n, r in zip(names, rows):
            shard = wts[n].shape
            grads[n] = g_rows[off:off + r].reshape((shard[2], shard[1]) if n in COL_SHARDED else shard[1:])
            off += r
    small_sum = _sum_slots("small_sum", _split_wait("small_gather_wait", pending["small"], dx)[1][0], F32)
    late = _allgather("small_allgather_late", [gs[LATE_SMALL].reshape(-1, LANES)])[0]
    late_sum = _sum_slots("small_sum_late", late.reshape(N_DEV, -1, LANES), F32)
    vals = _unpack_small(small_sum, [wts[n].shape for n in EARLY_SMALL] + [(1, 1)])
    total_loss = vals[-1].reshape(())
    def flat(n, a):
        a = a.reshape(wts[n].shape)
        a = jnp.swapaxes(a, -1, -2) if n in MINOR_SWAPPED else a
        return a.reshape(-1, a.shape[-1])

    def unflat(n, a):
        shape = wts[n].shape
        if n in MINOR_SWAPPED:
            return jnp.swapaxes(a.reshape(shape[:-2] + (shape[-1], shape[-2])), -1, -2)
        return a.reshape(shape)

    for n, g_full in zip(EARLY_SMALL + [LATE_SMALL], vals[:-1] + [late_sum]):
        grads[n] = flat(n, g_full)

    out_g, out_d, out_m, out_v = {}, {}, {}, {}
    by_shape = {}
    for n in WEIGHTS:
        by_shape.setdefault((flat(n, wts[n]).shape, n in COL_SHARDED), []).append(n)
    for (_, transposed), names in by_shape.items():
        items = [(flat(n, wts[n]), grads[n], flat(n, moms[n][0]), flat(n, moms[n][1])) for n in names]
        for n, res in zip(names, _adamw_group("adamw_" + names[0], items, transposed)):
            out_d[n], out_m[n], out_v[n], out_g[n] = (unflat(n, a) for a in res)

    return (total_loss, dx[None], *[out_g[n] for n in WEIGHTS], *[out_d[n] for n in WEIGHTS],
            *[out_m[n] for n in WEIGHTS], *[out_v[n] for n in WEIGHTS])
```

```python
import functools
import math

import jax
import jax.numpy as jnp
from jax import lax
from jax.experimental import pallas as pl
from jax.experimental.pallas import tpu as pltpu

F32 = jnp.float32
BF16 = jnp.bfloat16
EPS = 1e-6
N_XHEADS = 4
POOL_WINDOWS = (2, 4, 8, 16)
ADAM_LR = 0.001
ADAM_B1 = 0.9
ADAM_B2 = 0.999
ADAM_EPS = 1e-08
ADAM_WD = 0.01
ADAM_STEP = 10
N_DEV = 8
VMEM_LIMIT_V7X = 48 * 1024 * 1024
LANES = 128
SUBLANES = 8
SUB_ROWS = 256
POOL_PAD = 16
MESH = pl.DeviceIdType.MESH
ANY = pl.BlockSpec(memory_space=pl.ANY)
HBM = pl.BlockSpec(memory_space=pltpu.HBM)
SEM = pl.BlockSpec(memory_space=pltpu.SEMAPHORE)
SIDE_EFFECT = pltpu.SideEffectType.DATAFLOW_SIDE_EFFECTING

_DIMS = {
    "nt": (((1,), (1,)), ((), ())),
    "nn": (((1,), (0,)), ((), ())),
    "tn": (((0,), (0,)), ((), ())),
}


def _pick(dim, pref, mult=LANES):
    if dim <= pref:
        return dim
    for t in range(pref - pref % mult, 0, -mult):
        if dim % t == 0:
            return t
    return dim


def _params(sem):
    return pltpu.CompilerParams(dimension_semantics=sem, vmem_limit_bytes=VMEM_LIMIT_V7X)


def _tile(tm, tn, coff=0):
    return pl.BlockSpec((tm, tn), lambda i, j: (i, j + coff))


def _rowvec(tn, coff=0):
    return pl.BlockSpec((1, tn), lambda i, j: (0, j + coff))


def _out(m, n, dtype):
    return jax.ShapeDtypeStruct((m, n), dtype)


def _mm(name, form, a_list, b_list, groups, m, n, tm, tn, extras, epilogue, outs, after=None, sub=SUB_ROWS):
    na, nb, ne = len(a_list), len(b_list), len(extras)
    pins = [] if after is None else [after]
    step = tm if (sub is None or form == "tn" or tm % sub) else sub

    def a_spec(a):
        if form == "tn":
            return pl.BlockSpec((a.shape[0], tm), lambda i, j: (0, i))
        return pl.BlockSpec((tm, a.shape[1]), lambda i, j: (i, 0))

    def b_spec(b):
        if form == "nt":
            return pl.BlockSpec((tn, b.shape[1]), lambda i, j: (j, 0))
        return pl.BlockSpec((b.shape[0], tn), lambda i, j: (0, j))

    def body(*refs):
        a_refs, b_refs = refs[:na], refs[na:na + nb]
        e_refs, o_refs = refs[na + nb:na + nb + ne], refs[na + nb + ne + len(pins):]
        b_vals = {}
        for s0 in range(0, tm, step):
            rows = slice(None) if step == tm else pl.ds(s0, step)
            a_vals, accs = {}, []
            for group in groups:
                acc = None
                for ai, bi in group:
                    if ai not in a_vals:
                        a_vals[ai] = (a_refs[ai][...] if form == "tn" else a_refs[ai][rows, :]).astype(BF16)
                    if bi not in b_vals:
                        b_vals[bi] = b_refs[bi][...].astype(BF16)
                    d = lax.dot_general(a_vals[ai], b_vals[bi], _DIMS[form], preferred_element_type=F32)
                    acc = d if acc is None else acc + d
                accs.append(acc)
            res = epilogue(accs, *[e[rows, :] if e.shape[0] == tm else e[...] for e in e_refs])
            for o_ref, r in zip(o_refs, res):
                o_ref[rows, :] = r.astype(o_ref.dtype)

    out_specs = [_tile(tm, tn) if s is None else s for _, s in outs]
    res = pl.pallas_call(
        body, name=name, grid=(m // tm, n // tn),
        in_specs=[a_spec(a) for a in a_list] + [b_spec(b) for b in b_list] + [s for _, s in extras] + [ANY] * len(pins),
        out_specs=out_specs, out_shape=[o for o, _ in outs],
        compiler_params=_params(("parallel", "parallel")),
    )(*a_list, *b_list, *[e for e, _ in extras], *pins)
    return res


def _mm1(name, form, a, b, m, n, tm, tn, dtype, scale=None):
    epi = (lambda accs: (accs[0],)) if scale is None else (lambda accs: (accs[0] * scale,))
    return _mm(name, form, [a], [b], [[(0, 0)]], m, n, tm, tn, [], epi, [(_out(m, n, dtype), None)])[0]


def _rms_fwd(name, h, g):
    t, d = h.shape
    tm = _pick(t, 512, SUBLANES)

    def body(h_ref, g_ref, n_ref):
        hv = h_ref[...]
        r = lax.rsqrt(jnp.mean(hv * hv, axis=-1, keepdims=True) + EPS)
        n_ref[...] = ((hv * r) * g_ref[...]).astype(BF16)

    return pl.pallas_call(
        body, name=name, grid=(t // tm,),
        in_specs=[pl.BlockSpec((tm, d), lambda i: (i, 0)), pl.BlockSpec((1, d), lambda i: (0, 0))],
        out_specs=pl.BlockSpec((tm, d), lambda i: (i, 0)), out_shape=_out(t, d, BF16),
        compiler_params=_params(("parallel",)),
    )(h, g)


def _rms_bwd(name, h, g, dn, dres=None):
    t, d = h.shape
    tm = _pick(t, 512, SUBLANES)
    need_dh = dres is not None

    def body(*refs):
        if need_dh:
            h_ref, g_ref, dn_ref, dres_ref, dh_ref, dhb_ref, dg_ref = refs
        else:
            h_ref, g_ref, dn_ref, dg_ref = refs
        hv = h_ref[...]
        r = lax.rsqrt(jnp.mean(hv * hv, axis=-1, keepdims=True) + EPS)
        nh = hv * r
        dnv = dn_ref[...].astype(F32)

        @pl.when(pl.program_id(0) == 0)
        def _():
            dg_ref[...] = jnp.zeros_like(dg_ref)

        dg_ref[...] += jnp.sum(dnv * nh, axis=0, keepdims=True)
        if need_dh:
            dng = dnv * g_ref[...]
            dh = dres_ref[...] + r * (dng - nh * jnp.mean(dng * nh, axis=-1, keepdims=True))
            dh_ref[...] = dh
            dhb_ref[...] = dh.astype(BF16)

    row = pl.BlockSpec((tm, d), lambda i: (i, 0))
    vec = pl.BlockSpec((1, d), lambda i: (0, 0))
    if need_dh:
        return pl.pallas_call(
            body, name=name, grid=(t // tm,), in_specs=[row, vec, row, row], out_specs=[row, row, vec],
            out_shape=[_out(t, d, F32), _out(t, d, BF16), _out(1, d, F32)], compiler_params=_params(("arbitrary",)),
        )(h, g, dn, dres)
    return pl.pallas_call(
        body, name=name, grid=(t // tm,), in_specs=[row, vec, row], out_specs=vec,
        out_shape=_out(1, d, F32), compiler_params=_params(("arbitrary",)),
    )(h, g, dn)


def _loss_head(h, g, tgt):
    t, d = h.shape
    tm = _pick(t, 512, SUBLANES)

    def body(h_ref, g_ref, t_ref, dh_ref, dhb_ref, dg_ref, loss_ref):
        hv = h_ref[...]
        r = lax.rsqrt(jnp.mean(hv * hv, axis=-1, keepdims=True) + EPS)
        nh = hv * r
        err = nh * g_ref[...] - t_ref[...]

        @pl.when(pl.program_id(0) == 0)
        def _():
            dg_ref[...] = jnp.zeros_like(dg_ref)
            loss_ref[...] = jnp.zeros_like(loss_ref)

        per_row = jnp.mean(err * err, axis=-1, keepdims=True)
        loss_ref[...] += 0.5 * jnp.sum(per_row, axis=0, keepdims=True)
        dy = err * (1.0 / d)
        dg_ref[...] += jnp.sum(dy * nh, axis=0, keepdims=True)
        dng = dy * g_ref[...]
        dh = r * (dng - nh * jnp.mean(dng * nh, axis=-1, keepdims=True))
        dh_ref[...] = dh
        dhb_ref[...] = dh.astype(BF16)

    row = pl.BlockSpec((tm, d), lambda i: (i, 0))
    vec = pl.BlockSpec((1, d), lambda i: (0, 0))
    return pl.pallas_call(
        body, name="loss_head", grid=(t // tm,), in_specs=[row, vec, row],
        out_specs=[row, row, vec, pl.BlockSpec((1, LANES), lambda i: (0, 0))],
        out_shape=[_out(t, d, F32), _out(t, d, BF16), _out(1, d, F32), _out(1, LANES, F32)],
        compiler_params=_params(("arbitrary",)),
    )(h, g, tgt)


def _ffn_fwd(tag, h, n, wg_t, wu_t, wd):
    t, d = h.shape
    f = wg_t.shape[0]
    tm, tn = _pick(t, 1024), _pick(f, 1408)

    def up_epi(accs):
        a, b = accs
        return a, b, (a * jax.nn.sigmoid(a)) * b

    a, b, hid = _mm(tag + "_up", "nt", [n], [wg_t, wu_t], [[(0, 0)], [(0, 1)]], t, f, tm, tn, [], up_epi,
                    [(_out(t, f, BF16), None)] * 3)
    if callable(wd):
        wd = wd(hid)
    tm2, tn2 = _pick(t, 1024), _pick(d, 512)
    h_out = _mm(tag + "_down", "nn", [hid], [wd], [[(0, 0)]], t, d, tm2, tn2, [(h, _tile(tm2, tn2))],
                lambda accs, hin: (hin + 0.5 * accs[0],), [(_out(t, d, F32), None)])[0]
    return h_out, (n, a, b, hid)


def _ffn_bwd(tag, h, g, wg_t, wu_t, wd, saved, dh, dh_bf, weights_done=None, after=None):
    n, a, b, hid = saved
    t, d = h.shape
    f = wd.shape[0]
    tm, tn = _pick(t, 1024), _pick(f, 1408)

    def hid_epi(accs, av, bv):
        dhid = 0.5 * accs[0]
        av, bv = av.astype(F32), bv.astype(F32)
        sig = jax.nn.sigmoid(av)
        da = dhid * bv * (sig * (1.0 + av * (1.0 - sig)))
        db = dhid * (av * sig)
        return da, db

    da, db = _mm(tag + "_bwd_hid", "nt", [dh_bf], [wd], [[(0, 0)]], t, f, tm, tn,
                 [(a, _tile(tm, tn)), (b, _tile(tm, tn))], hid_epi, [(_out(t, f, BF16), None)] * 2, after=after)
    tw, tnw = _pick(f, 1408), _pick(d, 512)
    d_wd = _mm1(tag + "_dwd", "tn", hid, dh_bf, f, d, tw, tnw, BF16, scale=0.5)
    d_wg = _mm1(tag + "_dwg", "tn", da, n, f, d, tw, tnw, BF16)
    d_wu = _mm1(tag + "_dwu", "tn", db, n, f, d, tw, tnw, BF16)
    pin = weights_done(d_wg, d_wu, d_wd) if weights_done is not None else None
    tm2, tn2 = _pick(t, 1024), _pick(d, 512)
    dn = _mm(tag + "_dn", "nn", [da, db], [wg_t, wu_t], [[(0, 0), (1, 1)]], t, d, tm2, tn2, [],
             lambda accs: (accs[0],), [(_out(t, d, F32), None)], after=pin)[0]
    dh_in, dh_in_bf, dg = _rms_bwd(tag + "_norm_bwd", h, g, dn, dh)
    return dh_in, dh_in_bf, dg, d_wg, d_wu, d_wd


def _window_sum(win, offsets):
    n = win.shape[0]
    acc = None
    for j in offsets:
        term = win if j == 0 else pltpu.roll(win, (-j) % n, 0)
        acc = term if acc is None else acc + term
    return acc


def _pool_counts(r0, ch, c, left, right, t):
    pos = r0 + lax.broadcasted_iota(jnp.int32, (ch, c), 0)
    return (jnp.minimum(pos + right + 1, t) - jnp.maximum(pos - left, 0)).astype(F32)


def _pool_fwd(proj, pool_w_bf, pool_scale):
    t = proj.shape[0]
    ng, c, _ = pool_w_bf.shape
    ch = _pick(t, 256, SUBLANES)
    pad = POOL_PAD

    def body(p_ref, w_ref, s_ref, pooled_ref, pm_ref, buf):
        grp = pl.program_id(0)
        buf[pl.ds(0, pad), :] = jnp.zeros((pad, c), F32)
        buf[pl.ds(pad + t, pad), :] = jnp.zeros((pad, c), F32)

        def fill(ci, carry):
            r0 = pl.multiple_of(ci * ch, SUBLANES)
            buf[pl.ds(pl.multiple_of(r0 + pad, SUBLANES), ch), :] = p_ref[pl.ds(r0, ch), :]
            return carry

        lax.fori_loop(0, t // ch, fill, 0)
        for gi, w in enumerate(POOL_WINDOWS):
            left = w // 2
            right = w - 1 - left

            @pl.when(grp == gi)
            def _(left=left, right=right):
                def chunk(ci, carry):
                    r0 = pl.multiple_of(ci * ch, SUBLANES)
                    win = buf[pl.ds(r0, ch + 2 * pad), :]
                    s = _window_sum(win, range(-left, right + 1))[pad:pad + ch]
                    pooled = s / _pool_counts(r0, ch, c, left, right, t) - win[pad:pad + ch]
                    pooled_bf = pooled.astype(BF16)
                    mixed = jnp.dot(pooled_bf, w_ref[0], preferred_element_type=F32)
                    pooled_ref[pl.ds(r0, ch), :] = pooled_bf
                    pm_ref[pl.ds(r0, ch), :] = (mixed * s_ref[...]).astype(BF16)
                    return carry

                lax.fori_loop(0, t // ch, chunk, 0)

    col = pl.BlockSpec((t, c), lambda g: (0, g))
    return pl.pallas_call(
        body, name="pool_fwd", grid=(ng,),
        in_specs=[col, pl.BlockSpec((1, c, c), lambda g: (g, 0, 0)), pl.BlockSpec((1, c), lambda g: (0, g))],
        out_specs=[col, col], out_shape=[_out(t, ng * c, BF16), _out(t, ng * c, BF16)],
        scratch_shapes=[pltpu.VMEM((t + 2 * pad, c), F32)],
        compiler_params=_params(("parallel",)),
    )(proj, pool_w_bf, pool_scale)


def _pool_bwd(pooled, dpm, pool_w_bf, pool_scale):
    t = pooled.shape[0]
    ng, c, _ = pool_w_bf.shape
    ch = _pick(t, 256, SUBLANES)
    pad = POOL_PAD

    def body(pooled_ref, dpm_ref, w_ref, s_ref, dp_ref, dw_ref, ds_ref, buf, raw):
        grp = pl.program_id(0)
        buf[pl.ds(0, pad), :] = jnp.zeros((pad, c), F32)
        buf[pl.ds(pad + t, pad), :] = jnp.zeros((pad, c), F32)
        dw_ref[...] = jnp.zeros_like(dw_ref)
        ds_ref[...] = jnp.zeros_like(ds_ref)
        for gi, w in enumerate(POOL_WINDOWS):
            left = w // 2
            right = w - 1 - left

            @pl.when(grp == gi)
            def _(left=left, right=right):
                def first(ci, carry):
                    r0 = pl.multiple_of(ci * ch, SUBLANES)
                    pv = pooled_ref[pl.ds(r0, ch), :]
                    dpm_v = dpm_ref[pl.ds(r0, ch), :]
                    mixed = jnp.dot(pv, w_ref[0], preferred_element_type=F32)
                    ds_ref[...] += jnp.sum(dpm_v * mixed, axis=0, keepdims=True)
                    dmixed = (dpm_v * s_ref[...]).astype(BF16)
                    dw_ref[0] += lax.dot_general(pv, dmixed, _DIMS["tn"], preferred_element_type=F32)
                    dpooled = lax.dot_general(dmixed, w_ref[0], _DIMS["nt"], preferred_element_type=F32)
                    raw[pl.ds(r0, ch), :] = dpooled
                    buf[pl.ds(pl.multiple_of(r0 + pad, SUBLANES), ch), :] = (
                        dpooled / _pool_counts(r0, ch, c, left, right, t))
                    return carry

                lax.fori_loop(0, t // ch, first, 0)

                def second(ci, carry):
                    r0 = pl.multiple_of(ci * ch, SUBLANES)
                    win = buf[pl.ds(r0, ch + 2 * pad), :]
                    s = _window_sum(win, range(-right, left + 1))[pad:pad + ch]
                    dp_ref[pl.ds(r0, ch), :] = (s - raw[pl.ds(r0, ch), :]).astype(BF16)
                    return carry

                lax.fori_loop(0, t // ch, second, 0)

    col = pl.BlockSpec((t, c), lambda g: (0, g))
    return pl.pallas_call(
        body, name="pool_bwd", grid=(ng,),
        in_specs=[col, col, pl.BlockSpec((1, c, c), lambda g: (g, 0, 0)), pl.BlockSpec((1, c), lambda g: (0, g))],
        out_specs=[col, pl.BlockSpec((1, c, c), lambda g: (g, 0, 0)), pl.BlockSpec((1, c), lambda g: (0, g))],
        out_shape=[_out(t, ng * c, BF16), jax.ShapeDtypeStruct((ng, c, c), F32), _out(1, ng * c, F32)],
        scratch_shapes=[pltpu.VMEM((t + 2 * pad, c), F32), pltpu.VMEM((t, c), F32)],
        compiler_params=_params(("parallel",)),
    )(pooled, dpm, pool_w_bf, pool_scale)


def _discretise(a_re, a_im, log_dt, b_re, b_im):
    dt = jnp.exp(log_dt)
    mag = jnp.exp(dt * a_re)
    ang = dt * a_im
    abr = mag * jnp.cos(ang)
    abi = mag * jnp.sin(ang)
    den = a_re * a_re + a_im * a_im
    nr = abr - 1.0
    qr = (nr * a_re + abi * a_im) / den
    qi = (abi * a_re - nr * a_im) / den
    return abr, abi, qr * b_re - qi * b_im, qr * b_im + qi * b_re


def _ssm_disc(args):
    def body(ar, ai, ld, br, bi, o1, o2, o3, o4):
        res = _discretise(ar[...], ai[...], ld[...], br[...], bi[...])
        for o, r in zip((o1, o2, o3, o4), res):
            o[...] = r

    like = lambda a: jax.ShapeDtypeStruct(a.shape, F32)
    return pl.pallas_call(
        body, name="ssm_disc", out_shape=[like(args[0]), like(args[0]), like(args[3]), like(args[3])],
    )(*args)


def _ssm_disc_bwd(args, cots):
    def body(ar, ai, ld, br, bi, c1, c2, c3, c4, o1, o2, o3, o4, o5):
        _, vjp = jax.vjp(_discretise, ar[...], ai[...], ld[...], br[...], bi[...])
        res = vjp((c1[...], c2[...], c3[...], c4[...]))
        for o, r in zip((o1, o2, o3, o4, o5), res):
            o[...] = r

    return pl.pallas_call(
        body, name="ssm_disc_bwd", out_shape=[jax.ShapeDtypeStruct(a.shape, F32) for a in args],
    )(*args, *cots)


def _cmul(pr, pi, qr, qi):
    return pr * qr - pi * qi, pr * qi + pi * qr


def _cpow(pr, pi, n):
    rr, ri = None, None
    while n:
        if n & 1:
            rr, ri = (pr, pi) if rr is None else _cmul(rr, ri, pr, pi)
        n >>= 1
        if n:
            pr, pi = _cmul(pr, pi, pr, pi)
    return rr, ri


def _segment_carry(er, ei, pr, pi, reverse):
    row = lax.broadcasted_iota(jnp.int32, er.shape, 0)
    cr, ci = jnp.zeros_like(er), jnp.zeros_like(ei)
    for _ in range(SUBLANES - 1):
        tr = er + pr * cr - pi * ci
        ti = ei + pr * ci + pi * cr
        if reverse:
            keep, shift = row < SUBLANES - 1, SUBLANES - 1
        else:
            keep, shift = row >= 1, 1
        cr = jnp.where(keep, pltpu.roll(tr, shift, 0), 0.0)
        ci = jnp.where(keep, pltpu.roll(ti, shift, 0), 0.0)
    return cr, ci


def _ssm_fwd(name, sp, b_re, b_im, c_re, c_im, ar, ai, reverse):
    t, c = sp.shape
    s = ar.shape[1]
    w = _pick(s, 512)
    ch = _pick(t, 512, SUBLANES)
    n_ch, gpc, steps = t // ch, ch // SUBLANES, t // SUBLANES

    def body(sp_ref, bre_ref, bim_ref, cre_ref, cim_ref, ar_ref, ai_ref, xr_ref, xi_ref, y_ref, ur, ui, xbr, xbi):
        a_r = jnp.broadcast_to(ar_ref[...], (SUBLANES, w))
        a_i = jnp.broadcast_to(ai_ref[...], (SUBLANES, w))

        @pl.when(pl.program_id(0) == 0)
        def _():
            y_ref[...] = jnp.zeros_like(y_ref)

        def sweep(h0, store):
            def chunk(k, h):
                ci = n_ch - 1 - k if reverse else k
                rows = pl.ds(pl.multiple_of(ci * ch, ch), ch)
                spv = sp_ref[rows, :].astype(BF16)
                ur[...] = jnp.dot(spv, bre_ref[...], preferred_element_type=F32)
                ui[...] = jnp.dot(spv, bim_ref[...], preferred_element_type=F32)

                def group(g, hh):
                    gi = gpc - 1 - g if reverse else g
                    r0 = pl.multiple_of(gi * SUBLANES, SUBLANES)
                    hr, hi = hh
                    nr = a_r * hr - a_i * hi + ur[pl.ds(r0, SUBLANES), :]
                    ni = a_r * hi + a_i * hr + ui[pl.ds(r0, SUBLANES), :]
                    if store:
                        xbr[pl.ds(r0, SUBLANES), :] = nr
                        xbi[pl.ds(r0, SUBLANES), :] = ni
                    return nr, ni

                h = lax.fori_loop(0, gpc, group, h)
                if store:
                    xr16, xi16 = xbr[...].astype(BF16), xbi[...].astype(BF16)
                    xr_ref[rows, :] = xr16
                    xi_ref[rows, :] = xi16
                    y_ref[rows, :] += (lax.dot_general(xr16, cre_ref[...], _DIMS["nt"], preferred_element_type=F32)
                                       + lax.dot_general(xi16, cim_ref[...], _DIMS["nt"], preferred_element_type=F32))
                return h

            return lax.fori_loop(0, n_ch, chunk, h0)

        zero = jnp.zeros((SUBLANES, w), F32)
        er, ei = sweep((zero, zero), False)
        pr, pi = _cpow(ar_ref[...], ai_ref[...], steps)
        sweep(_segment_carry(er, ei, pr, pi, reverse), True)

    col = lambda i: (0, i)
    return pl.pallas_call(
        body, name=name, grid=(s // w,),
        in_specs=[pl.BlockSpec((t, c), lambda i: (0, 0))] + [pl.BlockSpec((c, w), col)] * 4
        + [pl.BlockSpec((1, w), col)] * 2,
        out_specs=[pl.BlockSpec((t, w), col), pl.BlockSpec((t, w), col), pl.BlockSpec((t, c), lambda i: (0, 0))],
        out_shape=[_out(t, s, BF16), _out(t, s, BF16), _out(t, c, F32)],
        scratch_shapes=[pltpu.VMEM((ch, w), F32)] * 4,
        compiler_params=_params(("arbitrary",)),
    )(sp, b_re, b_im, c_re, c_im, ar, ai)


def _ssm_bwd(name, dyp, c_re, c_im, xr, xi, ar, ai, reverse):
    t, c = dyp.shape
    s = ar.shape[1]
    w = _pick(s, 512)
    ch = _pick(t, 512, SUBLANES)
    n_ch, gpc, steps = t // ch, ch // SUBLANES, t // SUBLANES
    back = not reverse
    edge = 2 * SUBLANES

    def body(dy_ref, cre_ref, cim_ref, xr_ref, xi_ref, ar_ref, ai_ref, lr_ref, li_ref, dar_ref, dai_ref,
             gr, gi_, lbr, lbi, xbr, xbi):
        a_r = jnp.broadcast_to(ar_ref[...], (SUBLANES, w))
        a_i = -jnp.broadcast_to(ai_ref[...], (SUBLANES, w))
        row = lax.broadcasted_iota(jnp.int32, (SUBLANES, w), 0)

        def neighbours(ci, x_ref, buf):
            rows = pl.ds(pl.multiple_of(ci * ch, ch), ch)
            if reverse:
                buf[pl.ds(0, ch), :] = x_ref[rows, :].astype(F32)
                nxt = x_ref[pl.ds(pl.multiple_of(jnp.minimum(ci + 1, n_ch - 1) * ch, ch), edge), :].astype(F32)[:SUBLANES]
                first = x_ref[pl.ds(0, edge), :].astype(F32)[:SUBLANES]
                wrap = jnp.where(row < SUBLANES - 1, pltpu.roll(first, SUBLANES - 1, 0), 0.0)
                buf[pl.ds(ch, SUBLANES), :] = jnp.where(ci == n_ch - 1, wrap, nxt)
            else:
                buf[pl.ds(SUBLANES, ch), :] = x_ref[rows, :].astype(F32)
                prv = x_ref[pl.ds(pl.multiple_of(jnp.maximum(ci * ch - edge, 0), edge), edge), :].astype(F32)[SUBLANES:]
                last = x_ref[pl.ds(t - edge, edge), :].astype(F32)[SUBLANES:]
                wrap = jnp.where(row >= 1, pltpu.roll(last, 1, 0), 0.0)
                buf[pl.ds(0, SUBLANES), :] = jnp.where(ci == 0, wrap, prv)

        def sweep(h0, store):
            def chunk(k, carry):
                ci = n_ch - 1 - k if back else k
                rows = pl.ds(pl.multiple_of(ci * ch, ch), ch)
                dyv = dy_ref[rows, :].astype(BF16)
                gr[...] = jnp.dot(dyv, cre_ref[...], preferred_element_type=F32)
                gi_[...] = jnp.dot(dyv, cim_ref[...], preferred_element_type=F32)
                if store:
                    neighbours(ci, xr_ref, xbr)
                    neighbours(ci, xi_ref, xbi)

                def group(g, cc):
                    gidx = gpc - 1 - g if back else g
                    r0 = pl.multiple_of(gidx * SUBLANES, SUBLANES)
                    hr, hi = cc[0], cc[1]
                    nr = a_r * hr - a_i * hi + gr[pl.ds(r0, SUBLANES), :]
                    ni = a_r * hi + a_i * hr + gi_[pl.ds(r0, SUBLANES), :]
                    if not store:
                        return nr, ni
                    lbr[pl.ds(r0, SUBLANES), :] = nr
                    lbi[pl.ds(r0, SUBLANES), :] = ni
                    x0 = pl.multiple_of(r0 + SUBLANES, SUBLANES) if reverse else r0
                    xpr, xpi = xbr[pl.ds(x0, SUBLANES), :], xbi[pl.ds(x0, SUBLANES), :]
                    return nr, ni, cc[2] + nr * xpr + ni * xpi, cc[3] + ni * xpr - nr * xpi

                carry = lax.fori_loop(0, gpc, group, carry)
                if store:
                    lr_ref[rows, :] = lbr[...].astype(BF16)
                    li_ref[rows, :] = lbi[...].astype(BF16)
                return carry

            return lax.fori_loop(0, n_ch, chunk, h0)

        zero = jnp.zeros((SUBLANES, w), F32)
        er, ei = sweep((zero, zero), False)
        pr, pi = _cpow(ar_ref[...], -ai_ref[...], steps)
        cr, ci0 = _segment_carry(er, ei, pr, pi, back)
        _, _, dar, dai = sweep((cr, ci0, zero, zero), True)
        dar_ref[...] = jnp.sum(dar, axis=0, keepdims=True)
        dai_ref[...] = jnp.sum(dai, axis=0, keepdims=True)

    col = lambda i: (0, i)
    return pl.pallas_call(
        body, name=name, grid=(s // w,),
        in_specs=[pl.BlockSpec((t, c), lambda i: (0, 0)), pl.BlockSpec((c, w), col), pl.BlockSpec((c, w), col),
                  pl.BlockSpec((t, w), col), pl.BlockSpec((t, w), col), pl.BlockSpec((1, w), col), pl.BlockSpec((1, w), col)],
        out_specs=[pl.BlockSpec((t, w), col), pl.BlockSpec((t, w), col), pl.BlockSpec((1, w), col), pl.BlockSpec((1, w), col)],
        out_shape=[_out(t, s, BF16), _out(t, s, BF16), _out(1, s, F32), _out(1, s, F32)],
        scratch_shapes=[pltpu.VMEM((ch, w), F32)] * 4 + [pltpu.VMEM((ch + SUBLANES, w), F32)] * 2,
        compiler_params=_params(("parallel",)),
    )(dyp, c_re, c_im, xr, xi, ar, ai)


def _to_segments(a):
    t, c = a.shape
    return a.reshape(SUBLANES, t // SUBLANES, c).transpose(1, 0, 2).reshape(t, c)


def _from_segments(a):
    t, c = a.shape
    return a.reshape(t // SUBLANES, SUBLANES, c).transpose(1, 0, 2).reshape(t, c)


def _colsum_prod(name, a, b, b_coff=0):
    t, n = a.shape
    tm = _pick(t, 512, SUBLANES)

    def body(a_ref, b_ref, o_ref):
        @pl.when(pl.program_id(0) == 0)
        def _():
            o_ref[...] = jnp.zeros_like(o_ref)

        o_ref[...] += jnp.sum(a_ref[...].astype(F32) * b_ref[...].astype(F32), axis=0, keepdims=True)

    return pl.pallas_call(
        body, name=name, grid=(t // tm,),
        in_specs=[pl.BlockSpec((tm, n), lambda i: (i, 0)), pl.BlockSpec((tm, n), lambda i: (i, b_coff))],
        out_specs=pl.BlockSpec((1, n), lambda i: (0, 0)), out_shape=_out(1, n, F32),
        compiler_params=_params(("arbitrary",)),
    )(a, b)


def _bd(blk):
    g, hh, p = blk.shape
    eye = jnp.eye(g, dtype=bool)[:, None, :, None]
    return jnp.where(eye, blk[:, :, None, :], 0.0).reshape(g * hh, g * p)


def _diag(dmat, g, hh, p):
    eye = jnp.eye(g, dtype=bool)[:, None, :, None]
    return jnp.sum(jnp.where(eye, dmat.reshape(g, hh, g, p), 0.0), axis=2)


def _softmax(qh, kh, scale):
    s = lax.dot_general(qh, kh, _DIMS["nt"], preferred_element_type=F32) * scale
    e = jnp.exp(s - jnp.max(s, axis=-1, keepdims=True))
    return e / jnp.sum(e, axis=-1, keepdims=True)


def _attn_fwd(q, kv):
    t, d = q.shape
    mm_ = kv.shape[0]
    hd = d // N_XHEADS
    scale = 1.0 / math.sqrt(hd)
    tm = _pick(t, 512, SUBLANES)

    def body(q_ref, kv_ref, o_ref):
        for h in range(N_XHEADS):
            sl = pl.ds(h * hd, hd)
            p = _softmax(q_ref[:, sl], kv_ref[:, sl], scale)
            o_ref[:, sl] = jnp.dot(p.astype(BF16), kv_ref[:, pl.ds(d + h * hd, hd)],
                                   preferred_element_type=F32).astype(BF16)

    return pl.pallas_call(
        body, name="attn_fwd", grid=(t // tm,),
        in_specs=[pl.BlockSpec((tm, d), lambda i: (i, 0)), pl.BlockSpec((mm_, 2 * d), lambda i: (0, 0))],
        out_specs=pl.BlockSpec((tm, d), lambda i: (i, 0)), out_shape=_out(t, d, BF16),
        compiler_params=_params(("parallel",)),
    )(q, kv)


def _attn_bwd(q, kv, do):
    t, d = q.shape
    mm_ = kv.shape[0]
    hd = d // N_XHEADS
    scale = 1.0 / math.sqrt(hd)
    tm = _pick(t, 512, SUBLANES)

    def body(q_ref, kv_ref, do_ref, dq_ref, dkv_ref):
        @pl.when(pl.program_id(0) == 0)
        def _():
            dkv_ref[...] = jnp.zeros_like(dkv_ref)

        for h in range(N_XHEADS):
            sl = pl.ds(h * hd, hd)
            vsl = pl.ds(d + h * hd, hd)
            qh, kh, doh = q_ref[:, sl], kv_ref[:, sl], do_ref[:, sl]
            p = _softmax(qh, kh, scale)
            dp = lax.dot_general(doh, kv_ref[:, vsl], _DIMS["nt"], preferred_element_type=F32)
            dkv_ref[:, vsl] += lax.dot_general(p.astype(BF16), doh, _DIMS["tn"], preferred_element_type=F32)
            ds = (p * (dp - jnp.sum(dp * p, axis=-1, keepdims=True)) * scale).astype(BF16)
            dq_ref[:, sl] = jnp.dot(ds, kh, preferred_element_type=F32).astype(BF16)
            dkv_ref[:, sl] += lax.dot_general(ds, qh, _DIMS["tn"], preferred_element_type=F32)

    row = pl.BlockSpec((tm, d), lambda i: (i, 0))
    full = pl.BlockSpec((mm_, 2 * d), lambda i: (0, 0))
    return pl.pallas_call(
        body, name="attn_bwd", grid=(t // tm,), in_specs=[row, full, row], out_specs=[row, full],
        out_shape=[_out(t, d, BF16), _out(mm_, 2 * d, F32)], compiler_params=_params(("arbitrary",)),
    )(q, kv, do)


def _ew(name, fn, ins, outs, rows_pref=256, rowvecs=()):
    r, c = ins[0].shape
    tr = _pick(r, rows_pref, SUBLANES)
    ni = len(ins) + len(rowvecs)

    def body(*refs):
        res = fn(*[x[...] for x in refs[:ni]])
        for o_ref, v in zip(refs[ni:], res):
            o_ref[...] = v.astype(o_ref.dtype)

    blk = pl.BlockSpec((tr, c), lambda i: (i, 0))
    vec = pl.BlockSpec((1, c), lambda i: (0, 0))
    return pl.pallas_call(
        body, name=name, grid=(r // tr,), in_specs=[blk] * len(ins) + [vec] * len(rowvecs), out_specs=[blk] * len(outs),
        out_shape=[_out(r, c, dt) for dt in outs], compiler_params=_params(("parallel",)),
    )(*ins, *rowvecs)


def _sum_slots(name, a, dtype):
    s, r, c = a.shape
    tr = _pick(r, 256, SUBLANES)

    def body(a_ref, o_ref):
        acc = a_ref[0].astype(F32)
        for k in range(1, s):
            acc = acc + a_ref[k].astype(F32)
        o_ref[...] = acc.astype(o_ref.dtype)

    return pl.pallas_call(
        body, name=name, grid=(r // tr,), in_specs=[pl.BlockSpec((s, tr, c), lambda i: (0, i, 0))],
        out_specs=pl.BlockSpec((tr, c), lambda i: (i, 0)), out_shape=_out(r, c, dtype),
        compiler_params=_params(("parallel",)),
    )(a)


def _adamw_step(wv, gv, mv, vv):
    bc1 = 1.0 - ADAM_B1 ** ADAM_STEP
    bc2 = 1.0 - ADAM_B2 ** ADAM_STEP
    m2 = ADAM_B1 * mv + (1.0 - ADAM_B1) * gv
    v2 = ADAM_B2 * vv + (1.0 - ADAM_B2) * (gv * gv)
    delta = -ADAM_LR * ((m2 / bc1) / (jnp.sqrt(v2 / bc2) + ADAM_EPS) + ADAM_WD * wv)
    return delta, m2, v2


def _adamw_group(name, items, transposed):
    k, r = items[0][0].shape
    if transposed and r % LANES != 0:
        rows = _adamw_group(name, [(w.T, g, m.T, v.T) for w, g, m, v in items], False)
        return [[a.T for a in item] for item in rows]
    tk = _pick(k, max(SUBLANES, ADAMW_STEP_WORDS // (r * len(items))), SUBLANES)
    n_out = 4 if transposed else 3

    def body(*refs):
        ins, outs = refs[:4 * len(items)], refs[4 * len(items):]
        for i in range(len(items)):
            wv, gv, mv, vv = (a[...] for a in ins[4 * i:4 * i + 4])
            if transposed:
                gv = gv.T
            res = _adamw_step(wv, gv, mv, vv) + ((gv,) if transposed else ())
            for o_ref, val in zip(outs[n_out * i:n_out * (i + 1)], res):
                o_ref[...] = val

    blk = pl.BlockSpec((tk, r), lambda j: (j, 0))
    g_blk = pl.BlockSpec((r, tk), lambda j: (0, j)) if transposed else blk
    res = pl.pallas_call(
        body, name=name, grid=(k // tk,), in_specs=[blk, g_blk, blk, blk] * len(items),
        out_specs=[blk] * (n_out * len(items)), out_shape=[pltpu.HBM((k, r), F32)] * (n_out * len(items)),
        compiler_params=_params(("parallel",)),
    )(*[pltpu.with_memory_space_constraint(a, pltpu.HBM) for item in items for a in item])
    return [list(res[n_out * i:n_out * (i + 1)]) + ([] if transposed else [items[i][1]]) for i in range(len(items))]


def _allgather(name, arrs):
    n = len(arrs)

    def body(*refs):
        ins, outs = refs[:n], refs[n:2 * n]
        send_sems, recv_sems, local_sems = refs[2 * n:]
        x, y, c = lax.axis_index("x"), lax.axis_index("y"), lax.axis_index("c")
        me, sibling = (x, y, c), (x, y, 1 - c)
        chips = [(1 - x, y), (x, 1 - y), (1 - x, 1 - y)]

        def rows(a, px, py, pc):
            r = ins[a].shape[0]
            return outs[a].at[pl.ds((4 * px + 2 * py + pc) * r, r), :]

        def copy(a, k, block, to, src=None):
            return pltpu.make_async_remote_copy(
                src_ref=rows(a, *block) if src is None else src, dst_ref=rows(a, *block),
                send_sem=send_sems.at[a, k], recv_sem=recv_sems.at[a, k], device_id=to, device_id_type=MESH)

        mine = [pltpu.make_async_copy(ins[a], rows(a, *me), local_sems.at[a]) for a in range(n)]
        for cp in mine:
            cp.start()
        first = []
        for a in range(n):
            first.append(copy(a, 0, me, sibling, src=ins[a]))
            first += [copy(a, 1 + j, me, (*chip, c), src=ins[a]) for j, chip in enumerate(chips)]
        for cp in first:
            cp.start()
        passed = []
        for j, chip in enumerate(chips):
            for a in range(n):
                copy(a, 1 + j, (*chip, c), me).wait_recv()
                cp = copy(a, 4 + j, (*chip, c), sibling)
                cp.start()
                passed.append(cp)
        for a in range(n):
            copy(a, 0, sibling, me).wait_recv()
            for j, chip in enumerate(chips):
                copy(a, 4 + j, (*chip, 1 - c), me).wait_recv()
        for cp in first + passed:
            cp.wait_send()
        for cp in mine:
            cp.wait()

    return pl.pallas_call(
        body, name=name, in_specs=[ANY] * n, out_specs=[ANY] * n,
        out_shape=[_out(N_DEV * a.shape[0], a.shape[1], a.dtype) for a in arrs],
        scratch_shapes=[pltpu.SemaphoreType.DMA((n, 7)), pltpu.SemaphoreType.DMA((n, 7)), pltpu.SemaphoreType.DMA((n,))],
    )(*arrs)


def _exchange_cores(name, blocks):
    n = len(blocks)
    c = blocks[0].shape[2]
    r = sum(b.shape[1] for b in blocks)

    def body(*refs):
        srcs, (recv_ref, send_sems, recv_sems) = refs[:n], refs[n:]
        x, y, cc = lax.axis_index("x"), lax.axis_index("y"), lax.axis_index("c")
        copies, off = [], 0
        for a, src in enumerate(srcs):
            rows = pl.ds(off, src.shape[1])
            off += src.shape[1]
            for q in range(4):
                copies.append(pltpu.make_async_remote_copy(
                    src_ref=src.at[2 * q + (1 - cc)], dst_ref=recv_ref.at[q, rows], send_sem=send_sems.at[a, q],
                    recv_sem=recv_sems.at[a, q], device_id=(x, y, 1 - cc), device_id_type=MESH))
        for cp in copies:
            cp.start()
        for cp in copies:
            cp.wait()

    return pl.pallas_call(
        body, name=name, in_specs=[ANY] * n, out_specs=ANY,
        out_shape=jax.ShapeDtypeStruct((4, r, c), blocks[0].dtype),
        scratch_shapes=[pltpu.SemaphoreType.DMA((n, 4))] * 2,
    )(*blocks)


def _peer(k, x, y, c):
    return (1 - x if k & 4 else x, 1 - y if k & 2 else y, 1 - c if k & 1 else c)


def _split_start(name, groups, after=None):
    pins = [] if after is None else [after]
    bufs, sem_shapes, spans = [], [], []
    for srcs, land_shapes, n_remote, n_local, _ in groups:
        sems = [pltpu.SemaphoreType.DMA((n_remote,)), pltpu.SemaphoreType.DMA((n_remote,))]
        sems += [pltpu.SemaphoreType.DMA((n_local,))] if n_local else []
        spans.append((len(bufs), len(srcs), len(land_shapes), len(sem_shapes), len(sems)))
        bufs += [pltpu.with_memory_space_constraint(a, pltpu.HBM) for a in srcs]
        bufs += [pltpu.with_memory_space_constraint(lax.empty(s.shape, s.dtype), pltpu.HBM) for s in land_shapes]
        sem_shapes += sems
    n_buf, n_sem = len(bufs), len(sem_shapes)

    def body(*refs):
        buf_refs, sem_refs, token = refs[:n_buf], refs[n_buf + len(pins):n_buf + len(pins) + n_sem], refs[-1]
        for (b0, ns, nl, s0, k), group in zip(spans, groups):
            remote, local = group[4](buf_refs[b0:b0 + ns], buf_refs[b0 + ns:b0 + ns + nl], *sem_refs[s0:s0 + k])
            for cp in local + remote:
                cp.start()
        token[...] = jnp.zeros_like(token)

    outs = pl.pallas_call(
        body, name=name,
        out_shape=sem_shapes + [pltpu.HBM(b.shape, b.dtype) for b in bufs] + [jax.ShapeDtypeStruct((SUBLANES, LANES), F32)],
        in_specs=[HBM] * n_buf + [ANY] * len(pins),
        out_specs=[SEM] * n_sem + [HBM] * n_buf + [pl.BlockSpec(memory_space=pltpu.VMEM)],
        input_output_aliases={i: n_sem + i for i in range(n_buf)},
        compiler_params=pltpu.CompilerParams(has_side_effects=SIDE_EFFECT),
    )(*bufs, *pins)
    return [dict(sems=list(outs[s0:s0 + k]), bufs=list(outs[n_sem + b0:n_sem + b0 + ns + nl]), token=outs[-1],
                 build=group[4], ns=ns) for (b0, ns, nl, s0, k), group in zip(spans, groups)]


def _split_wait(name, started, after):
    ns, n_buf, n_sem = started["ns"], len(started["bufs"]), len(started["sems"])

    def body(*refs):
        src_refs, land_refs = refs[:ns], refs[ns:n_buf]
        sems = refs[n_buf:n_buf + n_sem]
        remote, local = started["build"](src_refs, land_refs, *sems)
        for cp in local:
            cp.wait()
        for cp in remote:
            cp.wait_send()
            cp.wait_recv()

    outs = pl.pallas_call(
        body, name=name, out_shape=[pltpu.HBM(b.shape, b.dtype) for b in started["bufs"]],
        in_specs=[HBM] * n_buf + [SEM] * n_sem + [ANY], out_specs=[HBM] * n_buf,
        input_output_aliases={i: i for i in range(n_buf)},
        compiler_params=pltpu.CompilerParams(has_side_effects=SIDE_EFFECT),
    )(*started["bufs"], *started["sems"], after)
    return list(outs[:ns]), list(outs[ns:])


def _gather_group(shards):
    m = len(shards)

    def build(src_refs, land_refs, send_sems, recv_sems, local_sems):
        x, y, c = lax.axis_index("x"), lax.axis_index("y"), lax.axis_index("c")
        remote, local = [], []
        for j in range(m):
            r = src_refs[j].shape[0]
            dst = land_refs[j].at[pl.ds((4 * x + 2 * y + c) * r, r), :]
            local.append(pltpu.make_async_copy(src_refs[j], dst, local_sems.at[j]))
            for k in range(1, N_DEV):
                remote.append(pltpu.make_async_remote_copy(
                    src_ref=src_refs[j], dst_ref=dst, send_sem=send_sems.at[7 * j + k - 1],
                    recv_sem=recv_sems.at[7 * j + k - 1], device_id=_peer(k, x, y, c), device_id_type=MESH))
        return remote, local

    lands = [jax.ShapeDtypeStruct((N_DEV * a.shape[0], a.shape[1]), a.dtype) for a in shards]
    return shards, lands, 7 * m, m, build


def _slots_start(name, a):
    def build(src_refs, land_refs, send_sems, recv_sems, local_sems):
        x, y, c = lax.axis_index("x"), lax.axis_index("y"), lax.axis_index("c")
        dst = land_refs[0].at[4 * x + 2 * y + c]
        local = [pltpu.make_async_copy(src_refs[0], dst, local_sems.at[0])]
        remote = [pltpu.make_async_remote_copy(
            src_ref=src_refs[0], dst_ref=dst, send_sem=send_sems.at[k - 1], recv_sem=recv_sems.at[k - 1],
            device_id=_peer(k, x, y, c), device_id_type=MESH) for k in range(1, N_DEV)]
        return remote, local

    return _split_start(name, [([a], [jax.ShapeDtypeStruct((N_DEV,) + a.shape, a.dtype)], 7, 1, build)])[0]


def _chips_start(name, p):
    _, r, c = p.shape
    nck = r // GRAD_ROW_TILE

    def build(src_refs, land_refs, send_sems, recv_sems):
        x, y, cc = lax.axis_index("x"), lax.axis_index("y"), lax.axis_index("c")
        remote = []
        for k in range(1, 4):
            px = 1 - x if k >> 1 else x
            py = 1 - y if k & 1 else y
            for j in range(nck):
                rows = pl.ds(j * GRAD_ROW_TILE, GRAD_ROW_TILE)
                remote.append(pltpu.make_async_remote_copy(
                    src_ref=src_refs[0].at[2 * px + py, rows], dst_ref=land_refs[0].at[k - 1, rows],
                    send_sem=send_sems.at[(k - 1) * nck + j], recv_sem=recv_sems.at[(k - 1) * nck + j],
                    device_id=(px, py, cc), device_id_type=MESH))
        return remote, []

    return _split_start(name, [([p], [jax.ShapeDtypeStruct((3, r, c), p.dtype)], 3 * nck, 0, build)])[0]


def _chip_sum(name, p, recv, chip):
    _, r, c = p.shape
    tr = _pick(r, 5 * GRAD_ROW_TILE, GRAD_ROW_TILE)

    def body(chip_ref, p_ref, r_ref, o_ref):
        acc = p_ref[...].astype(F32)
        for k in range(3):
            acc = acc + r_ref[k].astype(F32)
        o_ref[...] = acc

    return pl.pallas_call(
        body, name=name,
        grid_spec=pltpu.PrefetchScalarGridSpec(
            num_scalar_prefetch=1, grid=(r // tr,),
            in_specs=[pl.BlockSpec((None, tr, c), lambda i, chip_ref: (chip_ref[0], i, 0)),
                      pl.BlockSpec((3, tr, c), lambda i, chip_ref: (0, i, 0))],
            out_specs=pl.BlockSpec((tr, c), lambda i, chip_ref: (i, 0))),
        out_shape=_out(r, c, F32), compiler_params=_params(("parallel",)),
    )(chip, p, recv)


def _local_step(x, mem, tgt, wt, sm, ev=None):
    t, d = x.shape
    n_mem = mem.shape[0]
    d_pool = sm["pool_scale"].shape[1]
    ng, pc = sm["pool_w"].shape[0], sm["pool_w"].shape[1]
    d_ssm = sm["ssm_d"].shape[1]
    _, sg, sp, sh = sm["ssm_b_re"].shape
    n_state = sg * sp
    gb, gs = {}, {}

    def emit(name, **kw):
        return ev(name, **kw) if ev is not None else None

    n1 = _rms_fwd("ffn1_norm", x, sm["ffn1_norm"])
    emit("ffn1_norm_done", marker=n1)
    def ffn1_down(hid):
        emit("ffn1_up_done", marker=hid)
        return wt["ffn1_w_down"]

    h1, ffn1_saved = _ffn_fwd("ffn1", x, n1, wt["ffn1_w_gate"], wt["ffn1_w_up"], ffn1_down)
    emit("ffn1_fwd_done", marker=h1)
    u = _rms_fwd("mix_norm", h1, sm["mix_norm"])
    d_in = wt["w_in"].shape[0]
    tm, tn = _pick(t, 1024), _pick(d_in, 1408)
    proj = _mm1("in_proj", "nt", u, wt["w_in"], t, d_in, tm, tn, F32)
    off_s = d_pool // d_ssm
    off_gp = (d_pool + d_ssm)
    off_gs = off_gp + d

    pool_w_bf = sm["pool_w"].astype(BF16)
    pooled, pm = _pool_fwd(proj, pool_w_bf, sm["pool_scale"])

    by_p = lambda a: jnp.swapaxes(a, -1, -2).reshape(2 * sg, sh, sp)
    disc_args = [sm["ssm_a_re"].reshape(2 * sg, 1, sp), sm["ssm_a_im"].reshape(2 * sg, 1, sp),
                 sm["ssm_log_dt"].reshape(2 * sg, 1, 1), by_p(sm["ssm_b_re"]), by_p(sm["ssm_b_im"])]
    abr, abi, bbr, bbi = _ssm_disc(disc_args)
    abr2, abi2 = abr.reshape(2, n_state), abi.reshape(2, n_state)
    per_dir = lambda a: [_bd(a.reshape(2, sg, sh, sp)[dr]).astype(BF16) for dr in range(2)]
    b_re, b_im, c_re, c_im = per_dir(bbr), per_dir(bbi), per_dir(sm["ssm_c_re"]), per_dir(-sm["ssm_c_im"])
    sp32 = _to_segments(proj[:, d_pool:d_pool + d_ssm])
    xs, y_parts = [], []
    for dr in range(2):
        xr, xi, y_part = _ssm_fwd(f"ssm_fwd{dr}", sp32, b_re[dr], b_im[dr], c_re[dr], c_im[dr], abr2[dr:dr + 1],
                                  abi2[dr:dr + 1], reverse=(dr == 1))
        xs.append((xr, xi))
        y_parts.append(y_part)
    y = _from_segments(_ew("ssm_sum", lambda p0, p1, sv, dv: (p0 + p1 + sv * dv,), y_parts + [sp32], [F32],
                           rowvecs=[sm["ssm_d"]])[0])
    tmy = _pick(t, 256)
    ys = _ew("ssm_gelu", lambda v: (jax.nn.gelu(v),), [y], [BF16])[0]
    emit("mix_in_done", marker=ys)

    tmm, tnm, tnx = _pick(t, 1024), _pick(d, 256), _pick(d, 512)
    gp_spec = _tile(tmm, tnm, off_gp // tnm)
    gs_spec = _tile(tmm, tnm, off_gs // tnm)

    def merge_epi(accs, gpv, gsv):
        z_pool, val, gate = accs
        return (jax.nn.sigmoid(gpv) * z_pool + jax.nn.sigmoid(gsv) * (val * jax.nn.sigmoid(gate)),)

    merged = _mm("mix_merge", "nt", [pm, ys], [wt["w_pool_proj"], wt["w_glu_val"], wt["w_glu_gate"]],
                 [[(0, 0)], [(1, 1)], [(1, 2)]], t, d, tmm, tnm, [(proj, gp_spec), (proj, gs_spec)], merge_epi,
                 [(_out(t, d, BF16), None)])[0]
    res_epi = lambda accs, hin: (hin + accs[0],)
    h2 = _mm("mix_out", "nn", [merged], [wt["w_mix_out"]], [[(0, 0)]], t, d, tmm, tnx, [(h1, _tile(tmm, tnx))],
             res_epi, [(_out(t, d, F32), None)])[0]

    un = _rms_fwd("xattn_norm", h2, sm["xattn_norm"])
    mn = _rms_fwd("mem_norm", mem, sm["mem_norm"])
    emit("mix_done", marker=un)
    q = _mm1("xattn_q", "nn", un, wt["w_q"], t, d, tmm, tnx, BF16)
    kv = _mm1("xattn_kv", "nt", mn, wt["w_kv"], n_mem, 2 * d, n_mem, _pick(2 * d, 512), BF16)
    o = _attn_fwd(q, kv)
    h3 = _mm("xattn_out", "nn", [o], [wt["w_xo"]], [[(0, 0)]], t, d, tmm, tnx, [(h2, _tile(tmm, tnx))],
             res_epi, [(_out(t, d, F32), None)])[0]

    n2 = _rms_fwd("ffn2_norm", h3, sm["ffn2_norm"])
    emit("xattn_done", marker=n2)
    h4, ffn2_saved = _ffn_fwd("ffn2", h3, n2, wt["ffn2_w_gate"], wt["ffn2_w_up"], wt["ffn2_w_down"])

    dh4, dh4_bf, gs["final_norm"], loss = _loss_head(h4, sm["final_norm"], tgt)
    dh3, dh3_bf, gs["ffn2_norm"], gb["ffn2_w_gate"], gb["ffn2_w_up"], gb["ffn2_w_down"] = _ffn_bwd(
        "ffn2", h3, sm["ffn2_norm"], wt["ffn2_w_gate"], wt["ffn2_w_up"], wt["ffn2_w_down"], ffn2_saved, dh4, dh4_bf)

    tw = _pick(d, 1024)
    do = _mm1("xattn_do", "nt", dh3_bf, wt["w_xo"], t, d, tmm, tnx, BF16)
    gb["w_xo"] = _mm1("xattn_dwxo", "tn", o, dh3_bf, d, d, tw, tnx, BF16)
    dq, dkv = _attn_bwd(q, kv, do)
    gb["w_q"] = _mm1("xattn_dwq", "tn", un, dq, d, d, tw, tnx, BF16)
    dun = _mm1("xattn_dun", "nt", dq, wt["w_q"], t, d, tmm, tnx, F32)
    dh2, dh2_bf, gs["xattn_norm"] = _rms_bwd("xattn_norm_bwd", h2, sm["xattn_norm"], dun, dh3)
    gb["w_kv"] = _mm1("xattn_dwkv", "tn", dkv, mn, 2 * d, d, _pick(2 * d, 512), d, BF16)
    dmn = _mm1("xattn_dmn", "nn", dkv, wt["w_kv"], n_mem, d, n_mem, tnx, F32)
    gs["mem_norm"] = _rms_bwd("mem_norm_bwd", mem, sm["mem_norm"], dmn)

    gb["w_mix_out"] = _mm1("mix_dwout", "tn", merged, dh2_bf, d, d, tw, tnx, BF16)

    def merge_bwd_epi(accs, gpv, gsv):
        dmerged, z_pool, val, gate = accs
        sp_, ss_, sg_ = jax.nn.sigmoid(gpv), jax.nn.sigmoid(gsv), jax.nn.sigmoid(gate)
        glu = val * sg_
        dz_pool = dmerged * sp_
        dg_pool = dmerged * z_pool * (sp_ * (1.0 - sp_))
        dz_ssm = dmerged * ss_
        dg_ssm = dmerged * glu * (ss_ * (1.0 - ss_))
        dval = dz_ssm * sg_
        dgate = dz_ssm * glu * (1.0 - sg_)
        return dz_pool, dg_pool, dg_ssm, dval, dgate

    dz_pool, dg_pool, dg_ssm, dval, dgate = _mm(
        "mix_merge_bwd", "nt", [dh2_bf, pm, ys], [wt["w_mix_out"], wt["w_pool_proj"], wt["w_glu_val"], wt["w_glu_gate"]],
        [[(0, 0)], [(1, 1)], [(2, 2)], [(2, 3)]], t, d, tmm, tnm, [(proj, gp_spec), (proj, gs_spec)], merge_bwd_epi,
        [(_out(t, d, BF16), None)] * 5)
    gb["w_pool_proj"] = _mm1("pool_dwproj", "tn", dz_pool, pm, d, d_pool, tw, d_pool, BF16)
    gb["w_glu_val"] = _mm1("glu_dwval", "tn", dval, ys, d, d_ssm, tw, d_ssm, BF16)
    gb["w_glu_gate"] = _mm1("glu_dwgate", "tn", dgate, ys, d, d_ssm, tw, d_ssm, BF16)

    def gelu_bwd_epi(accs, yv):
        _, vjp = jax.vjp(jax.nn.gelu, yv)
        return (vjp(accs[0])[0],)

    dy = _mm("glu_dy", "nn", [dval, dgate], [wt["w_glu_val"], wt["w_glu_gate"]], [[(0, 0), (1, 1)]], t, d_ssm, tmy, d_ssm,
             [(y, _tile(tmy, d_ssm))], gelu_bwd_epi, [(_out(t, d_ssm, F32), None)])[0]
    gs["ssm_d"] = _colsum_prod("ssm_dd", dy, proj, b_coff=off_s)
    dyp = _to_segments(dy)
    d_abr, d_abi, d_bbr, d_bbi, d_cre, d_cim, lams = [], [], [], [], [], [], []
    ts = _pick(n_state, 512)
    for dr in range(2):
        lr, li, dar, dai = _ssm_bwd(f"ssm_bwd{dr}", dyp, c_re[dr], c_im[dr], xs[dr][0], xs[dr][1], abr2[dr:dr + 1],
                                    abi2[dr:dr + 1], reverse=(dr == 1))
        d_abr.append(dar)
        d_abi.append(dai)
        lams += [lr, li]
        maps = _mm(f"ssm_dmaps{dr}", "tn", [sp32, dyp], [lr, li, xs[dr][0], xs[dr][1]],
                   [[(0, 0)], [(0, 1)], [(1, 2)], [(1, 3)]], d_ssm, n_state, d_ssm, ts, [], lambda accs: tuple(accs),
                   [(_out(d_ssm, n_state, F32), None)] * 4)
        for acc, m in zip((d_bbr, d_bbi, d_cre, d_cim), maps):
            acc.append(_diag(m, sg, sh, sp))
    ds = _from_segments(_mm(
        "ssm_ds", "nt", lams, [b_re[0], b_im[0], b_re[1], b_im[1]], [[(k, k) for k in range(4)]], t, d_ssm, tmy,
        d_ssm, [(dyp, _tile(tmy, d_ssm)), (sm["ssm_d"], _rowvec(d_ssm))],
        lambda accs, dyv, dv: (dyv * dv + accs[0],), [(_out(t, d_ssm, BF16), None)])[0])
    cots = [jnp.concatenate(d_abr, axis=0).reshape(2 * sg, 1, sp), jnp.concatenate(d_abi, axis=0).reshape(2 * sg, 1, sp),
            jnp.concatenate(d_bbr, axis=0), jnp.concatenate(d_bbi, axis=0)]
    d_are, d_aim, d_ldt, d_bre, d_bim = _ssm_disc_bwd(disc_args, cots)
    gs["ssm_a_re"] = d_are.reshape(2, sg, sp)
    gs["ssm_a_im"] = d_aim.reshape(2, sg, sp)
    gs["ssm_log_dt"] = d_ldt.reshape(2, sg)
    from_p = lambda a: jnp.swapaxes(a.reshape(2, sg, sh, sp), -1, -2)
    gs["ssm_b_re"], gs["ssm_b_im"] = from_p(d_bre), from_p(d_bim)
    gs["ssm_c_re"] = jnp.stack(d_cre, axis=0)
    gs["ssm_c_im"] = -jnp.stack(d_cim, axis=0)

    dpm = _mm1("pool_dpm", "nn", dz_pool, wt["w_pool_proj"], t, d_pool, tmm, _pick(d_pool, 256), F32)
    dp, gs["pool_w"], gs["pool_scale"] = _pool_bwd(pooled, dpm, pool_w_bf, sm["pool_scale"])

    w_in = wt["w_in"]
    parts = [(dp, 0, d_pool), (ds, d_pool, d_ssm), (dg_pool, off_gp, d), (dg_ssm, off_gs, d)]
    w_in_parts = [w_in[o0:o0 + width] for _, o0, width in parts]
    gb["w_in"] = jnp.concatenate(
        [_mm1(f"in_proj_dw{k}", "tn", p_[0], u, p_[2], d, _pick(p_[2], 1024), tnx, BF16) for k, p_ in enumerate(parts)], axis=0)
    pin = emit("grads_main", gb=gb)
    du = _mm("in_proj_du", "nn", [p_[0] for p_ in parts], w_in_parts, [[(k, k) for k in range(4)]], t, d, tmm, tnx, [],
             lambda accs: (accs[0],), [(_out(t, d, F32), None)], after=pin)[0]
    dh1, dh1_bf, gs["mix_norm"] = _rms_bwd("mix_norm_bwd", h1, sm["mix_norm"], du, dh2)
    pin = emit("small_early", gs=gs, loss=loss)

    def ffn1_weights_done(d_wg, d_wu, d_wd):
        gb["ffn1_w_gate"], gb["ffn1_w_up"], gb["ffn1_w_down"] = d_wg, d_wu, d_wd
        return emit("grads_ffn1", gb=gb)

    dx, _, gs["ffn1_norm"], _, _, _ = _ffn_bwd(
        "ffn1", x, sm["ffn1_norm"], wt["ffn1_w_gate"], wt["ffn1_w_up"], wt["ffn1_w_down"], ffn1_saved, dh1, dh1_bf,
        weights_done=ffn1_weights_done, after=pin)
    return loss, dx, gb, gs


WEIGHTS = ["ffn1_norm", "ffn1_w_gate", "ffn1_w_up", "ffn1_w_down", "mix_norm", "w_in", "pool_w", "pool_scale",
           "w_pool_proj", "ssm_a_re", "ssm_a_im", "ssm_log_dt", "ssm_b_re", "ssm_b_im", "ssm_c_re", "ssm_c_im", "ssm_d",
           "w_glu_val", "w_glu_gate", "w_mix_out", "xattn_norm", "mem_norm", "w_q", "w_kv", "w_xo", "ffn2_norm",
           "ffn2_w_gate", "ffn2_w_up", "ffn2_w_down", "final_norm"]
COL_SHARDED = ["ffn1_w_gate", "ffn1_w_up", "w_in", "w_pool_proj", "w_glu_val", "w_glu_gate", "w_kv", "ffn2_w_gate",
               "ffn2_w_up"]
ROW_SHARDED = ["ffn1_w_down", "w_mix_out", "w_q", "w_xo", "ffn2_w_down"]
BIG = [n for n in WEIGHTS if n in COL_SHARDED or n in ROW_SHARDED]
SMALL = [n for n in WEIGHTS if n not in BIG]
FFN1_BIG = ["ffn1_w_gate", "ffn1_w_up", "ffn1_w_down"]
MAIN_BIG = [n for n in BIG if n not in FFN1_BIG]
GATHER_PLAN = [("ffn1_up_done", ["ffn1_w_down"]), ("ffn1_fwd_done", ["w_in"]),
               ("mix_in_done", ["w_pool_proj", "w_glu_val", "w_glu_gate", "w_mix_out"]),
               ("mix_done", ["w_q", "w_kv", "w_xo"]), ("xattn_done", ["ffn2_w_gate", "ffn2_w_up", "ffn2_w_down"])]
MINOR_SWAPPED = ["ssm_b_re", "ssm_b_im"]
LATE_SMALL = "ffn1_norm"
EARLY_SMALL = [n for n in SMALL if n != LATE_SMALL]
PACK_ROWS = SUBLANES * LANES
GRAD_ROW_TILE = 256
ADAMW_STEP_WORDS = 1 << 19


def _to_rows(name, w, width):
    if name in COL_SHARDED:
        w = w.T
    return w.reshape(-1, width)


def _pack_small(vals):
    flat = []
    for v in vals:
        f = v.reshape(-1)
        flat.append(jnp.pad(f, (0, (-f.shape[0]) % PACK_ROWS)))
    total = sum(f.shape[0] for f in flat)
    flat.append(jnp.zeros(((-total) % (GRAD_ROW_TILE * LANES),), F32))
    return jnp.concatenate(flat).reshape(-1, LANES)


def _unpack_small(packed, shapes):
    out, row = [], 0
    for shp in shapes:
        size = math.prod(shp)
        rows = -(-size // PACK_ROWS) * SUBLANES
        out.append(packed[row:row + rows].reshape(-1)[:size].reshape(shp))
        row += rows
    return out


def kernel(x, mem, ffn1_norm, ffn1_w_gate, ffn1_w_up, ffn1_w_down, mix_norm, w_in, pool_w, pool_scale, w_pool_proj, ssm_a_re, ssm_a_im, ssm_log_dt, ssm_b_re, ssm_b_im, ssm_c_re, ssm_c_im, ssm_d, w_glu_val, w_glu_gate, w_mix_out, xattn_norm, mem_norm, w_q, w_kv, w_xo, ffn2_norm, ffn2_w_gate, ffn2_w_up, ffn2_w_down, final_norm, loss_target, m_ffn1_norm, m_ffn1_w_gate, m_ffn1_w_up, m_ffn1_w_down, m_mix_norm, m_w_in, m_pool_w, m_pool_scale, m_w_pool_proj, m_ssm_a_re, m_ssm_a_im, m_ssm_log_dt, m_ssm_b_re, m_ssm_b_im, m_ssm_c_re, m_ssm_c_im, m_ssm_d, m_w_glu_val, m_w_glu_gate, m_w_mix_out, m_xattn_norm, m_mem_norm, m_w_q, m_w_kv, m_w_xo, m_ffn2_norm, m_ffn2_w_gate, m_ffn2_w_up, m_ffn2_w_down, m_final_norm, v_ffn1_norm, v_ffn1_w_gate, v_ffn1_w_up, v_ffn1_w_down, v_mix_norm, v_w_in, v_pool_w, v_pool_scale, v_w_pool_proj, v_ssm_a_re, v_ssm_a_im, v_ssm_log_dt, v_ssm_b_re, v_ssm_b_im, v_ssm_c_re, v_ssm_c_im, v_ssm_d, v_w_glu_val, v_w_glu_gate, v_w_mix_out, v_xattn_norm, v_mem_norm, v_w_q, v_w_kv, v_w_xo, v_ffn2_norm, v_ffn2_w_gate, v_ffn2_w_up, v_ffn2_w_down, v_final_norm):
    given = dict(locals())
    wts = {n: given[n] for n in WEIGHTS}
    moms = {n: (given["m_" + n], given["v_" + n]) for n in WEIGHTS}
    x2, mem2, tgt2 = x[0], mem[0], loss_target[0]
    d = x2.shape[1]
    chip = (2 * lax.axis_index("x") + lax.axis_index("y")).astype(jnp.int32).reshape(1)

    def full_form(n, f):
        shard = wts[n][0].shape
        return f.reshape(N_DEV * shard[1], shard[0]) if n in COL_SHARDED else f.reshape(N_DEV * shard[0], shard[1])

    shards = {n: _to_rows(n, wts[n][0], d).astype(BF16) for n in BIG}
    first = FFN1_BIG[:2]
    wt = {n: full_form(n, f) for n, f in zip(first, _allgather("weight_allgather_first", [shards[n] for n in first]))}
    started = _split_start("weight_gather_start", [_gather_group([shards[n] for n in names]) for _, names in GATHER_PLAN],
                           after=wt[first[0]])
    gathers = {event: (names, st) for (event, names), st in zip(GATHER_PLAN, started)}
    sm = {n: (wts[n].reshape(1, -1) if wts[n].ndim <= 2 else wts[n][0]) for n in SMALL}
    sm["ffn1_norm"] = sm["ffn1_norm"] + started[0]["token"][0, 0]

    pending = {}

    def reduce_start(tag, names, gb):
        blocks = [gb[n].reshape(N_DEV, -1, d) for n in names]
        pad_rows = (-sum(b.shape[1] for b in blocks)) % GRAD_ROW_TILE
        pad = [jnp.zeros((N_DEV, pad_rows, d), BF16)] if pad_rows else []
        recv = _exchange_cores("grad_exchange_cores_" + tag, blocks + pad)
        own = jnp.concatenate([lax.dynamic_index_in_dim(b.reshape(4, 2, b.shape[1], d), lax.axis_index("c"), 1, False)
                               for b in blocks + pad], axis=1)
        rows_all = own.shape[1]
        pair = _ew("grad_pair_sum_" + tag, lambda a, b: (a.astype(F32) + b.astype(F32),),
                   [own.reshape(-1, d), recv.reshape(-1, d)], [BF16], rows_pref=5 * GRAD_ROW_TILE)[0]
        pair = pair.reshape(4, rows_all, d)
        pending[tag] = (pair, _chips_start("grad_exchange_chips_start_" + tag, pair), [b.shape[1] for b in blocks])
        return pending[tag][1]["token"]

    def reduce_finish(tag, after):
        _, started, rows = pending[tag]
        (pair,), (recv,) = _split_wait("grad_exchange_chips_wait_" + tag, started, after)
        return _chip_sum("grad_chip_sum_" + tag, pair, recv, chip), rows

    def ev(name, gb=None, gs=None, loss=None, marker=None):
        if name in gathers:
            names, started = gathers[name]
            for n, f in zip(names, _split_wait("weight_gather_wait_" + name, started, marker)[1]):
                wt[n] = full_form(n, f)
        elif name == "grads_main":
            return reduce_start("main", MAIN_BIG, gb)
        elif name == "small_early":
            pending["small"] = _slots_start("small_gather_start", _pack_small([gs[n] for n in EARLY_SMALL] + [loss[:, :1]]))
            return pending["small"]["token"]
        elif name == "grads_ffn1":
            return reduce_start("ffn1", FFN1_BIG, gb)
        return None

    _, dx, _, gs = _local_step(x2, mem2, tgt2, wt, sm, ev)

    grads = {}
    for tag, names in (("main", MAIN_BIG), ("ffn1", FFN1_BIG)):
        g_rows, rows = reduce_finish(tag, dx)
        off = 0
        for n, r in zip(names, rows):
            shard = wts[n].shape
            grads[n] = g_rows[off:off + r].reshape((shard[2], shard[1]) if n in COL_SHARDED else shard[1:])
            off += r
    small_sum = _sum_slots("small_sum", _split_wait("small_gather_wait", pending["small"], dx)[1][0], F32)
    late = _allgather("small_allgather_late", [gs[LATE_SMALL].reshape(-1, LANES)])[0]
    late_sum = _sum_slots("small_sum_late", late.reshape(N_DEV, -1, LANES), F32)
    vals = _unpack_small(small_sum, [wts[n].shape for n in EARLY_SMALL] + [(1, 1)])
    total_loss = vals[-1].reshape(())
    def flat(n, a):
        a = a.reshape(wts[n].shape)
        a = jnp.swapaxes(a, -1, -2) if n in MINOR_SWAPPED else a
        return a.reshape(-1, a.shape[-1])

    def unflat(n, a):
        shape = wts[n].shape
        if n in MINOR_SWAPPED:
            return jnp.swapaxes(a.reshape(shape[:-2] + (shape[-1], shape[-2])), -1, -2)
        return a.reshape(shape)

    for n, g_full in zip(EARLY_SMALL + [LATE_SMALL], vals[:-1] + [late_sum]):
        grads[n] = flat(n, g_full)

    out_g, out_d, out_m, out_v = {}, {}, {}, {}
    by_shape = {}
    for n in WEIGHTS:
        by_shape.setdefault((flat(n, wts[n]).shape, n in COL_SHARDED), []).append(n)
    for (_, transposed), names in by_shape.items():
        items = [(flat(n, wts[n]), grads[n], flat(n, moms[n][0]), flat(n, moms[n][1])) for n in names]
        for n, res in zip(names, _adamw_group("adamw_" + names[0], items, transposed)):
            out_d[n], out_m[n], out_v[n], out_g[n] = (unflat(n, a) for a in res)

    return (total_loss, dx[None], *[out_g[n] for n in WEIGHTS], *[out_d[n] for n in WEIGHTS],
            *[out_m[n] for n in WEIGHTS], *[out_v[n] for n in WEIGHTS])
```

```python
import functools
import math

import jax
import jax.numpy as jnp
from jax import lax
from jax.experimental import pallas as pl
from jax.experimental.pallas import tpu as pltpu

F32 = jnp.float32
BF16 = jnp.bfloat16
EPS = 1e-6
N_XHEADS = 4
POOL_WINDOWS = (2, 4, 8, 16)
ADAM_LR = 0.001
ADAM_B1 = 0.9
ADAM_B2 = 0.999
ADAM_EPS = 1e-08
ADAM_WD = 0.01
ADAM_STEP = 10
N_DEV = 8
VMEM_LIMIT_V7X = 48 * 1024 * 1024
LANES = 128
SUBLANES = 8
SUB_ROWS = 256
POOL_PAD = 16
MESH = pl.DeviceIdType.MESH
ANY = pl.BlockSpec(memory_space=pl.ANY)
HBM = pl.BlockSpec(memory_space=pltpu.HBM)
SEM = pl.BlockSpec(memory_space=pltpu.SEMAPHORE)
SIDE_EFFECT = pltpu.SideEffectType.DATAFLOW_SIDE_EFFECTING

_DIMS = {
    "nt": (((1,), (1,)), ((), ())),
    "nn": (((1,), (0,)), ((), ())),
    "tn": (((0,), (0,)), ((), ())),
}


def _pick(dim, pref, mult=LANES):
    if dim <= pref:
        return dim
    for t in range(pref - pref % mult, 0, -mult):
        if dim % t == 0:
            return t
    return dim


def _params(sem):
    return pltpu.CompilerParams(dimension_semantics=sem, vmem_limit_bytes=VMEM_LIMIT_V7X)


def _tile(tm, tn, coff=0):
    return pl.BlockSpec((tm, tn), lambda i, j: (i, j + coff))


def _rowvec(tn, coff=0):
    return pl.BlockSpec((1, tn), lambda i, j: (0, j + coff))


def _out(m, n, dtype):
    return jax.ShapeDtypeStruct((m, n), dtype)


def _mm(name, form, a_list, b_list, groups, m, n, tm, tn, extras, epilogue, outs, after=None, sub=SUB_ROWS):
    na, nb, ne = len(a_list), len(b_list), len(extras)
    pins = [] if after is None else [after]
    step = tm if (sub is None or form == "tn" or tm % sub) else sub

    def a_spec(a):
        if form == "tn":
            return pl.BlockSpec((a.shape[0], tm), lambda i, j: (0, i))
        return pl.BlockSpec((tm, a.shape[1]), lambda i, j: (i, 0))

    def b_spec(b):
        if form == "nt":
            return pl.BlockSpec((tn, b.shape[1]), lambda i, j: (j, 0))
        return pl.BlockSpec((b.shape[0], tn), lambda i, j: (0, j))

    def body(*refs):
        a_refs, b_refs = refs[:na], refs[na:na + nb]
        e_refs, o_refs = refs[na + nb:na + nb + ne], refs[na + nb + ne + len(pins):]
        b_vals = {}
        for s0 in range(0, tm, step):
            rows = slice(None) if step == tm else pl.ds(s0, step)
            a_vals, accs = {}, []
            for group in groups:
                acc = None
                for ai, bi in group:
                    if ai not in a_vals:
                        a_vals[ai] = (a_refs[ai][...] if form == "tn" else a_refs[ai][rows, :]).astype(BF16)
                    if bi not in b_vals:
                        b_vals[bi] = b_refs[bi][...].astype(BF16)
                    d = lax.dot_general(a_vals[ai], b_vals[bi], _DIMS[form], preferred_element_type=F32)
                    acc = d if acc is None else acc + d
                accs.append(acc)
            res = epilogue(accs, *[e[rows, :] if e.shape[0] == tm else e[...] for e in e_refs])
            for o_ref, r in zip(o_refs, res):
                o_ref[rows, :] = r.astype(o_ref.dtype)

    out_specs = [_tile(tm, tn) if s is None else s for _, s in outs]
    res = pl.pallas_call(
        body, name=name, grid=(m // tm, n // tn),
        in_specs=[a_spec(a) for a in a_list] + [b_spec(b) for b in b_list] + [s for _, s in extras] + [ANY] * len(pins),
        out_specs=out_specs, out_shape=[o for o, _ in outs],
        compiler_params=_params(("parallel", "parallel")),
    )(*a_list, *b_list, *[e for e, _ in extras], *pins)
    return res


def _mm1(name, form, a, b, m, n, tm, tn, dtype, scale=None):
    epi = (lambda accs: (accs[0],)) if scale is None else (lambda accs: (accs[0] * scale,))
    return _mm(name, form, [a], [b], [[(0, 0)]], m, n, tm, tn, [], epi, [(_out(m, n, dtype), None)])[0]


def _rms_fwd(name, h, g):
    t, d = h.shape
    tm = _pick(t, 512, SUBLANES)

    def body(h_ref, g_ref, n_ref):
        hv = h_ref[...]
        r = lax.rsqrt(jnp.mean(hv * hv, axis=-1, keepdims=True) + EPS)
        n_ref[...] = ((hv * r) * g_ref[...]).astype(BF16)

    return pl.pallas_call(
        body, name=name, grid=(t // tm,),
        in_specs=[pl.BlockSpec((tm, d), lambda i: (i, 0)), pl.BlockSpec((1, d), lambda i: (0, 0))],
        out_specs=pl.BlockSpec((tm, d), lambda i: (i, 0)), out_shape=_out(t, d, BF16),
        compiler_params=_params(("parallel",)),
    )(h, g)


def _rms_bwd(name, h, g, dn, dres=None):
    t, d = h.shape
    tm = _pick(t, 512, SUBLANES)
    need_dh = dres is not None

    def body(*refs):
        if need_dh:
            h_ref, g_ref, dn_ref, dres_ref, dh_ref, dhb_ref, dg_ref = refs
        else:
            h_ref, g_ref, dn_ref, dg_ref = refs
        hv = h_ref[...]
        r = lax.rsqrt(jnp.mean(hv * hv, axis=-1, keepdims=True) + EPS)
        nh = hv * r
        dnv = dn_ref[...].astype(F32)

        @pl.when(pl.program_id(0) == 0)
        def _():
            dg_ref[...] = jnp.zeros_like(dg_ref)

        dg_ref[...] += jnp.sum(dnv * nh, axis=0, keepdims=True)
        if need_dh:
            dng = dnv * g_ref[...]
            dh = dres_ref[...] + r * (dng - nh * jnp.mean(dng * nh, axis=-1, keepdims=True))
            dh_ref[...] = dh
            dhb_ref[...] = dh.astype(BF16)

    row = pl.BlockSpec((tm, d), lambda i: (i, 0))
    vec = pl.BlockSpec((1, d), lambda i: (0, 0))
    if need_dh:
        return pl.pallas_call(
            body, name=name, grid=(t // tm,), in_specs=[row, vec, row, row], out_specs=[row, row, vec],
            out_shape=[_out(t, d, F32), _out(t, d, BF16), _out(1, d, F32)], compiler_params=_params(("arbitrary",)),
        )(h, g, dn, dres)
    return pl.pallas_call(
        body, name=name, grid=(t // tm,), in_specs=[row, vec, row], out_specs=vec,
        out_shape=_out(1, d, F32), compiler_params=_params(("arbitrary",)),
    )(h, g, dn)


def _loss_head(h, g, tgt):
    t, d = h.shape
    tm = _pick(t, 512, SUBLANES)

    def body(h_ref, g_ref, t_ref, dh_ref, dhb_ref, dg_ref, loss_ref):
        hv = h_ref[...]
        r = lax.rsqrt(jnp.mean(hv * hv, axis=-1, keepdims=True) + EPS)
        nh = hv * r
        err = nh * g_ref[...] - t_ref[...]

        @pl.when(pl.program_id(0) == 0)
        def _():
            dg_ref[...] = jnp.zeros_like(dg_ref)
            loss_ref[...] = jnp.zeros_like(loss_ref)

        per_row = jnp.mean(err * err, axis=-1, keepdims=True)
        loss_ref[...] += 0.5 * jnp.sum(per_row, axis=0, keepdims=True)
        dy = err * (1.0 / d)
        dg_ref[...] += jnp.sum(dy * nh, axis=0, keepdims=True)
        dng = dy * g_ref[...]
        dh = r * (dng - nh * jnp.mean(dng * nh, axis=-1, keepdims=True))
        dh_ref[...] = dh
        dhb_ref[...] = dh.astype(BF16)

    row = pl.BlockSpec((tm, d), lambda i: (i, 0))
    vec = pl.BlockSpec((1, d), lambda i: (0, 0))
    return pl.pallas_call(
        body, name="loss_head", grid=(t // tm,), in_specs=[row, vec, row],
        out_specs=[row, row, vec, pl.BlockSpec((1, LANES), lambda i: (0, 0))],
        out_shape=[_out(t, d, F32), _out(t, d, BF16), _out(1, d, F32), _out(1, LANES, F32)],
        compiler_params=_params(("arbitrary",)),
    )(h, g, tgt)


def _ffn_fwd(tag, h, n, wg_t, wu_t, wd):
    t, d = h.shape
    f = wg_t.shape[0]
    tm, tn = _pick(t, 1024), _pick(f, 1408)

    def up_epi(accs):
        a, b = accs
        return a, b, (a * jax.nn.sigmoid(a)) * b

    a, b, hid = _mm(tag + "_up", "nt", [n], [wg_t, wu_t], [[(0, 0)], [(0, 1)]], t, f, tm, tn, [], up_epi,
                    [(_out(t, f, BF16), None)] * 3)
    if callable(wd):
        wd = wd(hid)
    tm2, tn2 = _pick(t, 1024), _pick(d, 512)
    h_out = _mm(tag + "_down", "nn", [hid], [wd], [[(0, 0)]], t, d, tm2, tn2, [(h, _tile(tm2, tn2))],
                lambda accs, hin: (hin + 0.5 * accs[0],), [(_out(t, d, F32), None)])[0]
    return h_out, (n, a, b, hid)


def _ffn_bwd(tag, h, g, wg_t, wu_t, wd, saved, dh, dh_bf, weights_done=None, after=None):
    n, a, b, hid = saved
    t, d = h.shape
    f = wd.shape[0]
    tm, tn = _pick(t, 1024), _pick(f, 1408)

    def hid_epi(accs, av, bv):
        dhid = 0.5 * accs[0]
        av, bv = av.astype(F32), bv.astype(F32)
        sig = jax.nn.sigmoid(av)
        da = dhid * bv * (sig * (1.0 + av * (1.0 - sig)))
        db = dhid * (av * sig)
        return da, db

    da, db = _mm(tag + "_bwd_hid", "nt", [dh_bf], [wd], [[(0, 0)]], t, f, tm, tn,
                 [(a, _tile(tm, tn)), (b, _tile(tm, tn))], hid_epi, [(_out(t, f, BF16), None)] * 2, after=after)
    tw, tnw = _pick(f, 1408), _pick(d, 512)
    d_wd = _mm1(tag + "_dwd", "tn", hid, dh_bf, f, d, tw, tnw, BF16, scale=0.5)
    d_wg = _mm1(tag + "_dwg", "tn", da, n, f, d, tw, tnw, BF16)
    d_wu = _mm1(tag + "_dwu", "tn", db, n, f, d, tw, tnw, BF16)
    pin = weights_done(d_wg, d_wu, d_wd) if weights_done is not None else None
    tm2, tn2 = _pick(t, 1024), _pick(d, 512)
    dn = _mm(tag + "_dn", "nn", [da, db], [wg_t, wu_t], [[(0, 0), (1, 1)]], t, d, tm2, tn2, [],
             lambda accs: (accs[0],), [(_out(t, d, F32), None)], after=pin)[0]
    dh_in, dh_in_bf, dg = _rms_bwd(tag + "_norm_bwd", h, g, dn, dh)
    return dh_in, dh_in_bf, dg, d_wg, d_wu, d_wd


def _window_sum(win, offsets):
    n = win.shape[0]
    acc = None
    for j in offsets:
        term = win if j == 0 else pltpu.roll(win, (-j) % n, 0)
        acc = term if acc is None else acc + term
    return acc


def _pool_counts(r0, ch, c, left, right, t):
    pos = r0 + lax.broadcasted_iota(jnp.int32, (ch, c), 0)
    return (jnp.minimum(pos + right + 1, t) - jnp.maximum(pos - left, 0)).astype(F32)


def _pool_fwd(proj, pool_w_bf, pool_scale):
    t = proj.shape[0]
    ng, c, _ = pool_w_bf.shape
    ch = _pick(t, 256, SUBLANES)
    pad = POOL_PAD

    def body(p_ref, w_ref, s_ref, pooled_ref, pm_ref, buf):
        grp = pl.program_id(0)
        buf[pl.ds(0, pad), :] = jnp.zeros((pad, c), F32)
        buf[pl.ds(pad + t, pad), :] = jnp.zeros((pad, c), F32)

        def fill(ci, carry):
            r0 = pl.multiple_of(ci * ch, SUBLANES)
            buf[pl.ds(pl.multiple_of(r0 + pad, SUBLANES), ch), :] = p_ref[pl.ds(r0, ch), :]
            return carry

        lax.fori_loop(0, t // ch, fill, 0)
        for gi, w in enumerate(POOL_WINDOWS):
            left = w // 2
            right = w - 1 - left

            @pl.when(grp == gi)
            def _(left=left, right=right):
                def chunk(ci, carry):
                    r0 = pl.multiple_of(ci * ch, SUBLANES)
                    win = buf[pl.ds(r0, ch + 2 * pad), :]
                    s = _window_sum(win, range(-left, right + 1))[pad:pad + ch]
                    pooled = s / _pool_counts(r0, ch, c, left, right, t) - win[pad:pad + ch]
                    pooled_bf = pooled.astype(BF16)
                    mixed = jnp.dot(pooled_bf, w_ref[0], preferred_element_type=F32)
                    pooled_ref[pl.ds(r0, ch), :] = pooled_bf
                    pm_ref[pl.ds(r0, ch), :] = (mixed * s_ref[...]).astype(BF16)
                    return carry

                lax.fori_loop(0, t // ch, chunk, 0)

    col = pl.BlockSpec((t, c), lambda g: (0, g))
    return pl.pallas_call(
        body, name="pool_fwd", grid=(ng,),
        in_specs=[col, pl.BlockSpec((1, c, c), lambda g: (g, 0, 0)), pl.BlockSpec((1, c), lambda g: (0, g))],
        out_specs=[col, col], out_shape=[_out(t, ng * c, BF16), _out(t, ng * c, BF16)],
        scratch_shapes=[pltpu.VMEM((t + 2 * pad, c), F32)],
        compiler_params=_params(("parallel",)),
    )(proj, pool_w_bf, pool_scale)


def _pool_bwd(pooled, dpm, pool_w_bf, pool_scale):
    t = pooled.shape[0]
    ng, c, _ = pool_w_bf.shape
    ch = _pick(t, 256, SUBLANES)
    pad = POOL_PAD

    def body(pooled_ref, dpm_ref, w_ref, s_ref, dp_ref, dw_ref, ds_ref, buf, raw):
        grp = pl.program_id(0)
        buf[pl.ds(0, pad), :] = jnp.zeros((pad, c), F32)
        buf[pl.ds(pad + t, pad), :] = jnp.zeros((pad, c), F32)
        dw_ref[...] = jnp.zeros_like(dw_ref)
        ds_ref[...] = jnp.zeros_like(ds_ref)
        for gi, w in enumerate(POOL_WINDOWS):
            left = w // 2
            right = w - 1 - left

            @pl.when(grp == gi)
            def _(left=left, right=right):
                def first(ci, carry):
                    r0 = pl.multiple_of(ci * ch, SUBLANES)
                    pv = pooled_ref[pl.ds(r0, ch), :]
                    dpm_v = dpm_ref[pl.ds(r0, ch), :]
                    mixed = jnp.dot(pv, w_ref[0], preferred_element_type=F32)
                    ds_ref[...] += jnp.sum(dpm_v * mixed, axis=0, keepdims=True)
                    dmixed = (dpm_v * s_ref[...]).astype(BF16)
                    dw_ref[0] += lax.dot_general(pv, dmixed, _DIMS["tn"], preferred_element_type=F32)
                    dpooled = lax.dot_general(dmixed, w_ref[0], _DIMS["nt"], preferred_element_type=F32)
                    raw[pl.ds(r0, ch), :] = dpooled
                    buf[pl.ds(pl.multiple_of(r0 + pad, SUBLANES), ch), :] = (
                        dpooled / _pool_counts(r0, ch, c, left, right, t))
                    return carry

                lax.fori_loop(0, t // ch, first, 0)

                def second(ci, carry):
                    r0 = pl.multiple_of(ci * ch, SUBLANES)
                    win = buf[pl.ds(r0, ch + 2 * pad), :]
                    s = _window_sum(win, range(-right, left + 1))[pad:pad + ch]
                    dp_ref[pl.ds(r0, ch), :] = (s - raw[pl.ds(r0, ch), :]).astype(BF16)
                    return carry

                lax.fori_loop(0, t // ch, second, 0)

    col = pl.BlockSpec((t, c), lambda g: (0, g))
    return pl.pallas_call(
        body, name="pool_bwd", grid=(ng,),
        in_specs=[col, col, pl.BlockSpec((1, c, c), lambda g: (g, 0, 0)), pl.BlockSpec((1, c), lambda g: (0, g))],
        out_specs=[col, pl.BlockSpec((1, c, c), lambda g: (g, 0, 0)), pl.BlockSpec((1, c), lambda g: (0, g))],
        out_shape=[_out(t, ng * c, BF16), jax.ShapeDtypeStruct((ng, c, c), F32), _out(1, ng * c, F32)],
        scratch_shapes=[pltpu.VMEM((t + 2 * pad, c), F32), pltpu.VMEM((t, c), F32)],
        compiler_params=_params(("parallel",)),
    )(pooled, dpm, pool_w_bf, pool_scale)


def _discretise(a_re, a_im, log_dt, b_re, b_im):
    dt = jnp.exp(log_dt)
    mag = jnp.exp(dt * a_re)
    ang = dt * a_im
    abr = mag * jnp.cos(ang)
    abi = mag * jnp.sin(ang)
    den = a_re * a_re + a_im * a_im
    nr = abr - 1.0
    qr = (nr * a_re + abi * a_im) / den
    qi = (abi * a_re - nr * a_im) / den
    return abr, abi, qr * b_re - qi * b_im, qr * b_im + qi * b_re


def _ssm_disc(args):
    def body(ar, ai, ld, br, bi, o1, o2, o3, o4):
        res = _discretise(ar[...], ai[...], ld[...], br[...], bi[...])
        for o, r in zip((o1, o2, o3, o4), res):
            o[...] = r

    like = lambda a: jax.ShapeDtypeStruct(a.shape, F32)
    return pl.pallas_call(
        body, name="ssm_disc", out_shape=[like(args[0]), like(args[0]), like(args[3]), like(args[3])],
    )(*args)


def _ssm_disc_bwd(args, cots):
    def body(ar, ai, ld, br, bi, c1, c2, c3, c4, o1, o2, o3, o4, o5):
        _, vjp = jax.vjp(_discretise, ar[...], ai[...], ld[...], br[...], bi[...])
        res = vjp((c1[...], c2[...], c3[...], c4[...]))
        for o, r in zip((o1, o2, o3, o4, o5), res):
            o[...] = r

    return pl.pallas_call(
        body, name="ssm_disc_bwd", out_shape=[jax.ShapeDtypeStruct(a.shape, F32) for a in args],
    )(*args, *cots)


def _cmul(pr, pi, qr, qi):
    return pr * qr - pi * qi, pr * qi + pi * qr


def _cpow(pr, pi, n):
    rr, ri = None, None
    while n:
        if n & 1:
            rr, ri = (pr, pi) if rr is None else _cmul(rr, ri, pr, pi)
        n >>= 1
        if n:
            pr, pi = _cmul(pr, pi, pr, pi)
    return rr, ri


def _segment_carry(er, ei, pr, pi, reverse):
    row = lax.broadcasted_iota(jnp.int32, er.shape, 0)
    cr, ci = jnp.zeros_like(er), jnp.zeros_like(ei)
    for _ in range(SUBLANES - 1):
        tr = er + pr * cr - pi * ci
        ti = ei + pr * ci + pi * cr
        if reverse:
            keep, shift = row < SUBLANES - 1, SUBLANES - 1
        else:
            keep, shift = row >= 1, 1
        cr = jnp.where(keep, pltpu.roll(tr, shift, 0), 0.0)
        ci = jnp.where(keep, pltpu.roll(ti, shift, 0), 0.0)
    return cr, ci


def _ssm_fwd(name, sp, b_re, b_im, c_re, c_im, ar, ai, reverse):
    t, c = sp.shape
    s = ar.shape[1]
    w = _pick(s, 512)
    ch = _pick(t, 512, SUBLANES)
    n_ch, gpc, steps = t // ch, ch // SUBLANES, t // SUBLANES

    def body(sp_ref, bre_ref, bim_ref, cre_ref, cim_ref, ar_ref, ai_ref, xr_ref, xi_ref, y_ref, ur, ui, xbr, xbi):
        a_r = jnp.broadcast_to(ar_ref[...], (SUBLANES, w))
        a_i = jnp.broadcast_to(ai_ref[...], (SUBLANES, w))

        @pl.when(pl.program_id(0) == 0)
        def _():
            y_ref[...] = jnp.zeros_like(y_ref)

        def sweep(h0, store):
            def chunk(k, h):
                ci = n_ch - 1 - k if reverse else k
                rows = pl.ds(pl.multiple_of(ci * ch, ch), ch)
                spv = sp_ref[rows, :].astype(BF16)
                ur[...] = jnp.dot(spv, bre_ref[...], preferred_element_type=F32)
                ui[...] = jnp.dot(spv, bim_ref[...], preferred_element_type=F32)

                def group(g, hh):
                    gi = gpc - 1 - g if reverse else g
                    r0 = pl.multiple_of(gi * SUBLANES, SUBLANES)
                    hr, hi = hh
                    nr = a_r * hr - a_i * hi + ur[pl.ds(r0, SUBLANES), :]
                    ni = a_r * hi + a_i * hr + ui[pl.ds(r0, SUBLANES), :]
                    if store:
                        xbr[pl.ds(r0, SUBLANES), :] = nr
                        xbi[pl.ds(r0, SUBLANES), :] = ni
                    return nr, ni

                h = lax.fori_loop(0, gpc, group, h)
                if store:
                    xr16, xi16 = xbr[...].astype(BF16), xbi[...].astype(BF16)
                    xr_ref[rows, :] = xr16
                    xi_ref[rows, :] = xi16
                    y_ref[rows, :] += (lax.dot_general(xr16, cre_ref[...], _DIMS["nt"], preferred_element_type=F32)
                                       + lax.dot_general(xi16, cim_ref[...], _DIMS["nt"], preferred_element_type=F32))
                return h

            return lax.fori_loop(0, n_ch, chunk, h0)

        zero = jnp.zeros((SUBLANES, w), F32)
        er, ei = sweep((zero, zero), False)
        pr, pi = _cpow(ar_ref[...], ai_ref[...], steps)
        sweep(_segment_carry(er, ei, pr, pi, reverse), True)

    col = lambda i: (0, i)
    return pl.pallas_call(
        body, name=name, grid=(s // w,),
        in_specs=[pl.BlockSpec((t, c), lambda i: (0, 0))] + [pl.BlockSpec((c, w), col)] * 4
        + [pl.BlockSpec((1, w), col)] * 2,
        out_specs=[pl.BlockSpec((t, w), col), pl.BlockSpec((t, w), col), pl.BlockSpec((t, c), lambda i: (0, 0))],
        out_shape=[_out(t, s, BF16), _out(t, s, BF16), _out(t, c, F32)],
        scratch_shapes=[pltpu.VMEM((ch, w), F32)] * 4,
        compiler_params=_params(("arbitrary",)),
    )(sp, b_re, b_im, c_re, c_im, ar, ai)


def _ssm_bwd(name, dyp, c_re, c_im, xr, xi, ar, ai, reverse):
    t, c = dyp.shape
    s = ar.shape[1]
    w = _pick(s, 512)
    ch = _pick(t, 512, SUBLANES)
    n_ch, gpc, steps = t // ch, ch // SUBLANES, t // SUBLANES
    back = not reverse
    edge = 2 * SUBLANES

    def body(dy_ref, cre_ref, cim_ref, xr_ref, xi_ref, ar_ref, ai_ref, lr_ref, li_ref, dar_ref, dai_ref,
             gr, gi_, lbr, lbi, xbr, xbi):
        a_r = jnp.broadcast_to(ar_ref[...], (SUBLANES, w))
        a_i = -jnp.broadcast_to(ai_ref[...], (SUBLANES, w))
        row = lax.broadcasted_iota(jnp.int32, (SUBLANES, w), 0)

        def neighbours(ci, x_ref, buf):
            rows = pl.ds(pl.multiple_of(ci * ch, ch), ch)
            if reverse:
                buf[pl.ds(0, ch), :] = x_ref[rows, :].astype(F32)
                nxt = x_ref[pl.ds(pl.multiple_of(jnp.minimum(ci + 1, n_ch - 1) * ch, ch), edge), :].astype(F32)[:SUBLANES]
                first = x_ref[pl.ds(0, edge), :].astype(F32)[:SUBLANES]
                wrap = jnp.where(row < SUBLANES - 1, pltpu.roll(first, SUBLANES - 1, 0), 0.0)
                buf[pl.ds(ch, SUBLANES), :] = jnp.where(ci == n_ch - 1, wrap, nxt)
            else:
                buf[pl.ds(SUBLANES, ch), :] = x_ref[rows, :].astype(F32)
                prv = x_ref[pl.ds(pl.multiple_of(jnp.maximum(ci * ch - edge, 0), edge), edge), :].astype(F32)[SUBLANES:]
                last = x_ref[pl.ds(t - edge, edge), :].astype(F32)[SUBLANES:]
                wrap = jnp.where(row >= 1, pltpu.roll(last, 1, 0), 0.0)
                buf[pl.ds(0, SUBLANES), :] = jnp.where(ci == 0, wrap, prv)

        def sweep(h0, store):
            def chunk(k, carry):
                ci = n_ch - 1 - k if back else k
                rows = pl.ds(pl.multiple_of(ci * ch, ch), ch)
                dyv = dy_ref[rows, :].astype(BF16)
                gr[...] = jnp.dot(dyv, cre_ref[...], preferred_element_type=F32)
                gi_[...] = jnp.dot(dyv, cim_ref[...], preferred_element_type=F32)
                if store:
                    neighbours(ci, xr_ref, xbr)
                    neighbours(ci, xi_ref, xbi)

                def group(g, cc):
                    gidx = gpc - 1 - g if back else g
                    r0 = pl.multiple_of(gidx * SUBLANES, SUBLANES)
                    hr, hi = cc[0], cc[1]
                    nr = a_r * hr - a_i * hi + gr[pl.ds(r0, SUBLANES), :]
                    ni = a_r * hi + a_i * hr + gi_[pl.ds(r0, SUBLANES), :]
                    if not store:
                        return nr, ni
                    lbr[pl.ds(r0, SUBLANES), :] = nr
                    lbi[pl.ds(r0, SUBLANES), :] = ni
                    x0 = pl.multiple_of(r0 + SUBLANES, SUBLANES) if reverse else r0
                    xpr, xpi = xbr[pl.ds(x0, SUBLANES), :], xbi[pl.ds(x0, SUBLANES), :]
                    return nr, ni, cc[2] + nr * xpr + ni * xpi, cc[3] + ni * xpr - nr * xpi

                carry = lax.fori_loop(0, gpc, group, carry)
                if store:
                    lr_ref[rows, :] = lbr[...].astype(BF16)
                    li_ref[rows, :] = lbi[...].astype(BF16)
                return carry

            return lax.fori_loop(0, n_ch, chunk, h0)

        zero = jnp.zeros((SUBLANES, w), F32)
        er, ei = sweep((zero, zero), False)
        pr, pi = _cpow(ar_ref[...], -ai_ref[...], steps)
        cr, ci0 = _segment_carry(er, ei, pr, pi, back)
        _, _, dar, dai = sweep((cr, ci0, zero, zero), True)
        dar_ref[...] = jnp.sum(dar, axis=0, keepdims=True)
        dai_ref[...] = jnp.sum(dai, axis=0, keepdims=True)

    col = lambda i: (0, i)
    return pl.pallas_call(
        body, name=name, grid=(s // w,),
        in_specs=[pl.BlockSpec((t, c), lambda i: (0, 0)), pl.BlockSpec((c, w), col), pl.BlockSpec((c, w), col),
                  pl.BlockSpec((t, w), col), pl.BlockSpec((t, w), col), pl.BlockSpec((1, w), col), pl.BlockSpec((1, w), col)],
        out_specs=[pl.BlockSpec((t, w), col), pl.BlockSpec((t, w), col), pl.BlockSpec((1, w), col), pl.BlockSpec((1, w), col)],
        out_shape=[_out(t, s, BF16), _out(t, s, BF16), _out(1, s, F32), _out(1, s, F32)],
        scratch_shapes=[pltpu.VMEM((ch, w), F32)] * 4 + [pltpu.VMEM((ch + SUBLANES, w), F32)] * 2,
        compiler_params=_params(("parallel",)),
    )(dyp, c_re, c_im, xr, xi, ar, ai)


def _to_segments(a):
    t, c = a.shape
    return a.reshape(SUBLANES, t // SUBLANES, c).transpose(1, 0, 2).reshape(t, c)


def _from_segments(a):
    t, c = a.shape
    return a.reshape(t // SUBLANES, SUBLANES, c).transpose(1, 0, 2).reshape(t, c)


def _colsum_prod(name, a, b, b_coff=0):
    t, n = a.shape
    tm = _pick(t, 512, SUBLANES)

    def body(a_ref, b_ref, o_ref):
        @pl.when(pl.program_id(0) == 0)
        def _():
            o_ref[...] = jnp.zeros_like(o_ref)

        o_ref[...] += jnp.sum(a_ref[...].astype(F32) * b_ref[...].astype(F32), axis=0, keepdims=True)

    return pl.pallas_call(
        body, name=name, grid=(t // tm,),
        in_specs=[pl.BlockSpec((tm, n), lambda i: (i, 0)), pl.BlockSpec((tm, n), lambda i: (i, b_coff))],
        out_specs=pl.BlockSpec((1, n), lambda i: (0, 0)), out_shape=_out(1, n, F32),
        compiler_params=_params(("arbitrary",)),
    )(a, b)


def _bd(blk):
    g, hh, p = blk.shape
    eye = jnp.eye(g, dtype=bool)[:, None, :, None]
    return jnp.where(eye, blk[:, :, None, :], 0.0).reshape(g * hh, g * p)


def _diag(dmat, g, hh, p):
    eye = jnp.eye(g, dtype=bool)[:, None, :, None]
    return jnp.sum(jnp.where(eye, dmat.reshape(g, hh, g, p), 0.0), axis=2)


def _softmax(qh, kh, scale):
    s = lax.dot_general(qh, kh, _DIMS["nt"], preferred_element_type=F32) * scale
    e = jnp.exp(s - jnp.max(s, axis=-1, keepdims=True))
    return e / jnp.sum(e, axis=-1, keepdims=True)


def _attn_fwd(q, kv):
    t, d = q.shape
    mm_ = kv.shape[0]
    hd = d // N_XHEADS
    scale = 1.0 / math.sqrt(hd)
    tm = _pick(t, 512, SUBLANES)

    def body(q_ref, kv_ref, o_ref):
        for h in range(N_XHEADS):
            sl = pl.ds(h * hd, hd)
            p = _softmax(q_ref[:, sl], kv_ref[:, sl], scale)
            o_ref[:, sl] = jnp.dot(p.astype(BF16), kv_ref[:, pl.ds(d + h * hd, hd)],
                                   preferred_element_type=F32).astype(BF16)

    return pl.pallas_call(
        body, name="attn_fwd", grid=(t // tm,),
        in_specs=[pl.BlockSpec((tm, d), lambda i: (i, 0)), pl.BlockSpec((mm_, 2 * d), lambda i: (0, 0))],
        out_specs=pl.BlockSpec((tm, d), lambda i: (i, 0)), out_shape=_out(t, d, BF16),
        compiler_params=_params(("parallel",)),
    )(q, kv)


def _attn_bwd(q, kv, do):
    t, d = q.shape
    mm_ = kv.shape[0]
    hd = d // N_XHEADS
    scale = 1.0 / math.sqrt(hd)
    tm = _pick(t, 512, SUBLANES)

    def body(q_ref, kv_ref, do_ref, dq_ref, dkv_ref):
        @pl.when(pl.program_id(0) == 0)
        def _():
            dkv_ref[...] = jnp.zeros_like(dkv_ref)

        for h in range(N_XHEADS):
            sl = pl.ds(h * hd, hd)
            vsl = pl.ds(d + h * hd, hd)
            qh, kh, doh = q_ref[:, sl], kv_ref[:, sl], do_ref[:, sl]
            p = _softmax(qh, kh, scale)
            dp = lax.dot_general(doh, kv_ref[:, vsl], _DIMS["nt"], preferred_element_type=F32)
            dkv_ref[:, vsl] += lax.dot_general(p.astype(BF16), doh, _DIMS["tn"], preferred_element_type=F32)
            ds = (p * (dp - jnp.sum(dp * p, axis=-1, keepdims=True)) * scale).astype(BF16)
            dq_ref[:, sl] = jnp.dot(ds, kh, preferred_element_type=F32).astype(BF16)
            dkv_ref[:, sl] += lax.dot_general(ds, qh, _DIMS["tn"], preferred_element_type=F32)

    row = pl.BlockSpec((tm, d), lambda i: (i, 0))
    full = pl.BlockSpec((mm_, 2 * d), lambda i: (0, 0))
    return pl.pallas_call(
        body, name="attn_bwd", grid=(t // tm,), in_specs=[row, full, row], out_specs=[row, full],
        out_shape=[_out(t, d, BF16), _out(mm_, 2 * d, F32)], compiler_params=_params(("arbitrary",)),
    )(q, kv, do)


def _ew(name, fn, ins, outs, rows_pref=256, rowvecs=()):
    r, c = ins[0].shape
    tr = _pick(r, rows_pref, SUBLANES)
    ni = len(ins) + len(rowvecs)

    def body(*refs):
        res = fn(*[x[...] for x in refs[:ni]])
        for o_ref, v in zip(refs[ni:], res):
            o_ref[...] = v.astype(o_ref.dtype)

    blk = pl.BlockSpec((tr, c), lambda i: (i, 0))
    vec = pl.BlockSpec((1, c), lambda i: (0, 0))
    return pl.pallas_call(
        body, name=name, grid=(r // tr,), in_specs=[blk] * len(ins) + [vec] * len(rowvecs), out_specs=[blk] * len(outs),
        out_shape=[_out(r, c, dt) for dt in outs], compiler_params=_params(("parallel",)),
    )(*ins, *rowvecs)


def _sum_slots(name, a, dtype):
    s, r, c = a.shape
    tr = _pick(r, 256, SUBLANES)

    def body(a_ref, o_ref):
        acc = a_ref[0].astype(F32)
        for k in range(1, s):
            acc = acc + a_ref[k].astype(F32)
        o_ref[...] = acc.astype(o_ref.dtype)

    return pl.pallas_call(
        body, name=name, grid=(r // tr,), in_specs=[pl.BlockSpec((s, tr, c), lambda i: (0, i, 0))],
        out_specs=pl.BlockSpec((tr, c), lambda i: (i, 0)), out_shape=_out(r, c, dtype),
        compiler_params=_params(("parallel",)),
    )(a)


def _adamw_step(wv, gv, mv, vv):
    bc1 = 1.0 - ADAM_B1 ** ADAM_STEP
    bc2 = 1.0 - ADAM_B2 ** ADAM_STEP
    m2 = ADAM_B1 * mv + (1.0 - ADAM_B1) * gv
    v2 = ADAM_B2 * vv + (1.0 - ADAM_B2) * (gv * gv)
    delta = -ADAM_LR * ((m2 / bc1) / (jnp.sqrt(v2 / bc2) + ADAM_EPS) + ADAM_WD * wv)
    return delta, m2, v2


def _adamw_group(name, items, transposed):
    k, r = items[0][0].shape
    if transposed and r % LANES != 0:
        rows = _adamw_group(name, [(w.T, g, m.T, v.T) for w, g, m, v in items], False)
        return [[a.T for a in item] for item in rows]
    tk = _pick(k, max(SUBLANES, ADAMW_STEP_WORDS // (r * len(items))), SUBLANES)
    n_out = 4 if transposed else 3

    def body(*refs):
        ins, outs = refs[:4 * len(items)], refs[4 * len(items):]
        for i in range(len(items)):
            wv, gv, mv, vv = (a[...] for a in ins[4 * i:4 * i + 4])
            if transposed:
                gv = gv.T
            res = _adamw_step(wv, gv, mv, vv) + ((gv,) if transposed else ())
            for o_ref, val in zip(outs[n_out * i:n_out * (i + 1)], res):
                o_ref[...] = val

    blk = pl.BlockSpec((tk, r), lambda j: (j, 0))
    g_blk = pl.BlockSpec((r, tk), lambda j: (0, j)) if transposed else blk
    res = pl.pallas_call(
        body, name=name, grid=(k // tk,), in_specs=[blk, g_blk, blk, blk] * len(items),
        out_specs=[blk] * (n_out * len(items)), out_shape=[pltpu.HBM((k, r), F32)] * (n_out * len(items)),
        compiler_params=_params(("parallel",)),
    )(*[pltpu.with_memory_space_constraint(a, pltpu.HBM) for item in items for a in item])
    return [list(res[n_out * i:n_out * (i + 1)]) + ([] if transposed else [items[i][1]]) for i in range(len(items))]


def _allgather(name, arrs):
    n = len(arrs)

    def body(*refs):
        ins, outs = refs[:n], refs[n:2 * n]
        send_sems, recv_sems, local_sems = refs[2 * n:]
        x, y, c = lax.axis_index("x"), lax.axis_index("y"), lax.axis_index("c")
        me, sibling = (x, y, c), (x, y, 1 - c)
        chips = [(1 - x, y), (x, 1 - y), (1 - x, 1 - y)]

        def rows(a, px, py, pc):
            r = ins[a].shape[0]
            return outs[a].at[pl.ds((4 * px + 2 * py + pc) * r, r), :]

        def copy(a, k, block, to, src=None):
            return pltpu.make_async_remote_copy(
                src_ref=rows(a, *block) if src is None else src, dst_ref=rows(a, *block),
                send_sem=send_sems.at[a, k], recv_sem=recv_sems.at[a, k], device_id=to, device_id_type=MESH)

        mine = [pltpu.make_async_copy(ins[a], rows(a, *me), local_sems.at[a]) for a in range(n)]
        for cp in mine:
            cp.start()
        first = []
        for a in range(n):
            first.append(copy(a, 0, me, sibling, src=ins[a]))
            first += [copy(a, 1 + j, me, (*chip, c), src=ins[a]) for j, chip in enumerate(chips)]
        for cp in first:
            cp.start()
        passed = []
        for j, chip in enumerate(chips):
            for a in range(n):
                copy(a, 1 + j, (*chip, c), me).wait_recv()
                cp = copy(a, 4 + j, (*chip, c), sibling)
                cp.start()
                passed.append(cp)
        for a in range(n):
            copy(a, 0, sibling, me).wait_recv()
            for j, chip in enumerate(chips):
                copy(a, 4 + j, (*chip, 1 - c), me).wait_recv()
        for cp in first + passed:
            cp.wait_send()
        for cp in mine:
            cp.wait()

    return pl.pallas_call(
        body, name=name, in_specs=[ANY] * n, out_specs=[ANY] * n,
        out_shape=[_out(N_DEV * a.shape[0], a.shape[1], a.dtype) for a in arrs],
        scratch_shapes=[pltpu.SemaphoreType.DMA((n, 7)), pltpu.SemaphoreType.DMA((n, 7)), pltpu.SemaphoreType.DMA((n,))],
    )(*arrs)


def _cores_start(name, blocks):
    n = len(blocks)
    c = blocks[0].shape[2]
    r = sum(b.shape[1] for b in blocks)

    def build(src_refs, land_refs, send_sems, recv_sems):
        x, y, cc = lax.axis_index("x"), lax.axis_index("y"), lax.axis_index("c")
        remote, off = [], 0
        for a, src in enumerate(src_refs):
            rows = pl.ds(off, src.shape[1])
            off += src.shape[1]
            for q in range(4):
                remote.append(pltpu.make_async_remote_copy(
                    src_ref=src.at[2 * q + (1 - cc)], dst_ref=land_refs[0].at[q, rows], send_sem=send_sems.at[4 * a + q],
                    recv_sem=recv_sems.at[4 * a + q], device_id=(x, y, 1 - cc), device_id_type=MESH))
        return remote, []

    return _split_start(name, [(blocks, [jax.ShapeDtypeStruct((4, r, c), blocks[0].dtype)], 4 * n, 0, build)])[0]


def _peer(k, x, y, c):
    return (1 - x if k & 4 else x, 1 - y if k & 2 else y, 1 - c if k & 1 else c)


def _split_start(name, groups, after=None):
    pins = [] if after is None else [after]
    bufs, sem_shapes, spans = [], [], []
    for srcs, land_shapes, n_remote, n_local, _ in groups:
        sems = [pltpu.SemaphoreType.DMA((n_remote,)), pltpu.SemaphoreType.DMA((n_remote,))]
        sems += [pltpu.SemaphoreType.DMA((n_local,))] if n_local else []
        spans.append((len(bufs), len(srcs), len(land_shapes), len(sem_shapes), len(sems)))
        bufs += [pltpu.with_memory_space_constraint(a, pltpu.HBM) for a in srcs]
        bufs += [pltpu.with_memory_space_constraint(lax.empty(s.shape, s.dtype), pltpu.HBM) for s in land_shapes]
        sem_shapes += sems
    n_buf, n_sem = len(bufs), len(sem_shapes)

    def body(*refs):
        buf_refs, sem_refs, token = refs[:n_buf], refs[n_buf + len(pins):n_buf + len(pins) + n_sem], refs[-1]
        for (b0, ns, nl, s0, k), group in zip(spans, groups):
            remote, local = group[4](buf_refs[b0:b0 + ns], buf_refs[b0 + ns:b0 + ns + nl], *sem_refs[s0:s0 + k])
            for cp in local + remote:
                cp.start()
        token[...] = jnp.zeros_like(token)

    outs = pl.pallas_call(
        body, name=name,
        out_shape=sem_shapes + [pltpu.HBM(b.shape, b.dtype) for b in bufs] + [jax.ShapeDtypeStruct((SUBLANES, LANES), F32)],
        in_specs=[HBM] * n_buf + [ANY] * len(pins),
        out_specs=[SEM] * n_sem + [HBM] * n_buf + [pl.BlockSpec(memory_space=pltpu.VMEM)],
        input_output_aliases={i: n_sem + i for i in range(n_buf)},
        compiler_params=pltpu.CompilerParams(has_side_effects=SIDE_EFFECT),
    )(*bufs, *pins)
    return [dict(sems=list(outs[s0:s0 + k]), bufs=list(outs[n_sem + b0:n_sem + b0 + ns + nl]), token=outs[-1],
                 build=group[4], ns=ns) for (b0, ns, nl, s0, k), group in zip(spans, groups)]


def _split_wait(name, started, after):
    ns, n_buf, n_sem = started["ns"], len(started["bufs"]), len(started["sems"])

    def body(*refs):
        src_refs, land_refs = refs[:ns], refs[ns:n_buf]
        sems = refs[n_buf:n_buf + n_sem]
        remote, local = started["build"](src_refs, land_refs, *sems)
        for cp in local:
            cp.wait()
        for cp in remote:
            cp.wait_send()
            cp.wait_recv()

    outs = pl.pallas_call(
        body, name=name, out_shape=[pltpu.HBM(b.shape, b.dtype) for b in started["bufs"]],
        in_specs=[HBM] * n_buf + [SEM] * n_sem + [ANY], out_specs=[HBM] * n_buf,
        input_output_aliases={i: i for i in range(n_buf)},
        compiler_params=pltpu.CompilerParams(has_side_effects=SIDE_EFFECT),
    )(*started["bufs"], *started["sems"], after)
    return list(outs[:ns]), list(outs[ns:])


def _gather_group(shards):
    m = len(shards)

    def build(src_refs, land_refs, send_sems, recv_sems, local_sems):
        x, y, c = lax.axis_index("x"), lax.axis_index("y"), lax.axis_index("c")
        remote, local = [], []
        for j in range(m):
            r = src_refs[j].shape[0]
            dst = land_refs[j].at[pl.ds((4 * x + 2 * y + c) * r, r), :]
            local.append(pltpu.make_async_copy(src_refs[j], dst, local_sems.at[j]))
            for k in range(1, N_DEV):
                remote.append(pltpu.make_async_remote_copy(
                    src_ref=src_refs[j], dst_ref=dst, send_sem=send_sems.at[7 * j + k - 1],
                    recv_sem=recv_sems.at[7 * j + k - 1], device_id=_peer(k, x, y, c), device_id_type=MESH))
        return remote, local

    lands = [jax.ShapeDtypeStruct((N_DEV * a.shape[0], a.shape[1]), a.dtype) for a in shards]
    return shards, lands, 7 * m, m, build


def _slots_start(name, a):
    def build(src_refs, land_refs, send_sems, recv_sems, local_sems):
        x, y, c = lax.axis_index("x"), lax.axis_index("y"), lax.axis_index("c")
        dst = land_refs[0].at[4 * x + 2 * y + c]
        local = [pltpu.make_async_copy(src_refs[0], dst, local_sems.at[0])]
        remote = [pltpu.make_async_remote_copy(
            src_ref=src_refs[0], dst_ref=dst, send_sem=send_sems.at[k - 1], recv_sem=recv_sems.at[k - 1],
            device_id=_peer(k, x, y, c), device_id_type=MESH) for k in range(1, N_DEV)]
        return remote, local

    return _split_start(name, [([a], [jax.ShapeDtypeStruct((N_DEV,) + a.shape, a.dtype)], 7, 1, build)])[0]


def _chips_start(name, p):
    _, r, c = p.shape
    nck = r // GRAD_ROW_TILE

    def build(src_refs, land_refs, send_sems, recv_sems):
        x, y, cc = lax.axis_index("x"), lax.axis_index("y"), lax.axis_index("c")
        remote = []
        for k in range(1, 4):
            px = 1 - x if k >> 1 else x
            py = 1 - y if k & 1 else y
            for j in range(nck):
                rows = pl.ds(j * GRAD_ROW_TILE, GRAD_ROW_TILE)
                remote.append(pltpu.make_async_remote_copy(
                    src_ref=src_refs[0].at[2 * px + py, rows], dst_ref=land_refs[0].at[k - 1, rows],
                    send_sem=send_sems.at[(k - 1) * nck + j], recv_sem=recv_sems.at[(k - 1) * nck + j],
                    device_id=(px, py, cc), device_id_type=MESH))
        return remote, []

    return _split_start(name, [([p], [jax.ShapeDtypeStruct((3, r, c), p.dtype)], 3 * nck, 0, build)])[0]


def _chip_sum(name, p, recv, chip):
    _, r, c = p.shape
    tr = _pick(r, 5 * GRAD_ROW_TILE, GRAD_ROW_TILE)

    def body(chip_ref, p_ref, r_ref, o_ref):
        acc = p_ref[...].astype(F32)
        for k in range(3):
            acc = acc + r_ref[k].astype(F32)
        o_ref[...] = acc

    return pl.pallas_call(
        body, name=name,
        grid_spec=pltpu.PrefetchScalarGridSpec(
            num_scalar_prefetch=1, grid=(r // tr,),
            in_specs=[pl.BlockSpec((None, tr, c), lambda i, chip_ref: (chip_ref[0], i, 0)),
                      pl.BlockSpec((3, tr, c), lambda i, chip_ref: (0, i, 0))],
            out_specs=pl.BlockSpec((tr, c), lambda i, chip_ref: (i, 0))),
        out_shape=_out(r, c, F32), compiler_params=_params(("parallel",)),
    )(chip, p, recv)


def _local_step(x, mem, tgt, wt, sm, ev=None):
    t, d = x.shape
    n_mem = mem.shape[0]
    d_pool = sm["pool_scale"].shape[1]
    ng, pc = sm["pool_w"].shape[0], sm["pool_w"].shape[1]
    d_ssm = sm["ssm_d"].shape[1]
    _, sg, sp, sh = sm["ssm_b_re"].shape
    n_state = sg * sp
    gb, gs = {}, {}

    def emit(name, **kw):
        return ev(name, **kw) if ev is not None else None

    n1 = _rms_fwd("ffn1_norm", x, sm["ffn1_norm"])
    emit("ffn1_norm_done", marker=n1)
    def ffn1_down(hid):
        emit("ffn1_up_done", marker=hid)
        return wt["ffn1_w_down"]

    h1, ffn1_saved = _ffn_fwd("ffn1", x, n1, wt["ffn1_w_gate"], wt["ffn1_w_up"], ffn1_down)
    emit("ffn1_fwd_done", marker=h1)
    u = _rms_fwd("mix_norm", h1, sm["mix_norm"])
    d_in = wt["w_in"].shape[0]
    tm, tn = _pick(t, 1024), _pick(d_in, 1408)
    proj = _mm1("in_proj", "nt", u, wt["w_in"], t, d_in, tm, tn, F32)
    off_s = d_pool // d_ssm
    off_gp = (d_pool + d_ssm)
    off_gs = off_gp + d

    pool_w_bf = sm["pool_w"].astype(BF16)
    pooled, pm = _pool_fwd(proj, pool_w_bf, sm["pool_scale"])

    by_p = lambda a: jnp.swapaxes(a, -1, -2).reshape(2 * sg, sh, sp)
    disc_args = [sm["ssm_a_re"].reshape(2 * sg, 1, sp), sm["ssm_a_im"].reshape(2 * sg, 1, sp),
                 sm["ssm_log_dt"].reshape(2 * sg, 1, 1), by_p(sm["ssm_b_re"]), by_p(sm["ssm_b_im"])]
    abr, abi, bbr, bbi = _ssm_disc(disc_args)
    abr2, abi2 = abr.reshape(2, n_state), abi.reshape(2, n_state)
    per_dir = lambda a: [_bd(a.reshape(2, sg, sh, sp)[dr]).astype(BF16) for dr in range(2)]
    b_re, b_im, c_re, c_im = per_dir(bbr), per_dir(bbi), per_dir(sm["ssm_c_re"]), per_dir(-sm["ssm_c_im"])
    sp32 = _to_segments(proj[:, d_pool:d_pool + d_ssm])
    xs, y_parts = [], []
    for dr in range(2):
        xr, xi, y_part = _ssm_fwd(f"ssm_fwd{dr}", sp32, b_re[dr], b_im[dr], c_re[dr], c_im[dr], abr2[dr:dr + 1],
                                  abi2[dr:dr + 1], reverse=(dr == 1))
        xs.append((xr, xi))
        y_parts.append(y_part)
    y = _from_segments(_ew("ssm_sum", lambda p0, p1, sv, dv: (p0 + p1 + sv * dv,), y_parts + [sp32], [F32],
                           rowvecs=[sm["ssm_d"]])[0])
    tmy = _pick(t, 256)
    ys = _ew("ssm_gelu", lambda v: (jax.nn.gelu(v),), [y], [BF16])[0]
    emit("mix_in_done", marker=ys)

    tmm, tnm, tnx = _pick(t, 1024), _pick(d, 256), _pick(d, 512)
    gp_spec = _tile(tmm, tnm, off_gp // tnm)
    gs_spec = _tile(tmm, tnm, off_gs // tnm)

    def merge_epi(accs, gpv, gsv):
        z_pool, val, gate = accs
        return (jax.nn.sigmoid(gpv) * z_pool + jax.nn.sigmoid(gsv) * (val * jax.nn.sigmoid(gate)),)

    merged = _mm("mix_merge", "nt", [pm, ys], [wt["w_pool_proj"], wt["w_glu_val"], wt["w_glu_gate"]],
                 [[(0, 0)], [(1, 1)], [(1, 2)]], t, d, tmm, tnm, [(proj, gp_spec), (proj, gs_spec)], merge_epi,
                 [(_out(t, d, BF16), None)])[0]
    res_epi = lambda accs, hin: (hin + accs[0],)
    h2 = _mm("mix_out", "nn", [merged], [wt["w_mix_out"]], [[(0, 0)]], t, d, tmm, tnx, [(h1, _tile(tmm, tnx))],
             res_epi, [(_out(t, d, F32), None)])[0]

    un = _rms_fwd("xattn_norm", h2, sm["xattn_norm"])
    mn = _rms_fwd("mem_norm", mem, sm["mem_norm"])
    emit("mix_done", marker=un)
    q = _mm1("xattn_q", "nn", un, wt["w_q"], t, d, tmm, tnx, BF16)
    kv = _mm1("xattn_kv", "nt", mn, wt["w_kv"], n_mem, 2 * d, n_mem, _pick(2 * d, 512), BF16)
    o = _attn_fwd(q, kv)
    h3 = _mm("xattn_out", "nn", [o], [wt["w_xo"]], [[(0, 0)]], t, d, tmm, tnx, [(h2, _tile(tmm, tnx))],
             res_epi, [(_out(t, d, F32), None)])[0]

    n2 = _rms_fwd("ffn2_norm", h3, sm["ffn2_norm"])
    emit("xattn_done", marker=n2)
    h4, ffn2_saved = _ffn_fwd("ffn2", h3, n2, wt["ffn2_w_gate"], wt["ffn2_w_up"], wt["ffn2_w_down"])

    dh4, dh4_bf, gs["final_norm"], loss = _loss_head(h4, sm["final_norm"], tgt)
    dh3, dh3_bf, gs["ffn2_norm"], gb["ffn2_w_gate"], gb["ffn2_w_up"], gb["ffn2_w_down"] = _ffn_bwd(
        "ffn2", h3, sm["ffn2_norm"], wt["ffn2_w_gate"], wt["ffn2_w_up"], wt["ffn2_w_down"], ffn2_saved, dh4, dh4_bf)

    tw = _pick(d, 1024)
    do = _mm1("xattn_do", "nt", dh3_bf, wt["w_xo"], t, d, tmm, tnx, BF16)
    gb["w_xo"] = _mm1("xattn_dwxo", "tn", o, dh3_bf, d, d, tw, tnx, BF16)
    dq, dkv = _attn_bwd(q, kv, do)
    gb["w_q"] = _mm1("xattn_dwq", "tn", un, dq, d, d, tw, tnx, BF16)
    dun = _mm1("xattn_dun", "nt", dq, wt["w_q"], t, d, tmm, tnx, F32)
    dh2, dh2_bf, gs["xattn_norm"] = _rms_bwd("xattn_norm_bwd", h2, sm["xattn_norm"], dun, dh3)
    gb["w_kv"] = _mm1("xattn_dwkv", "tn", dkv, mn, 2 * d, d, _pick(2 * d, 512), d, BF16)
    dmn = _mm1("xattn_dmn", "nn", dkv, wt["w_kv"], n_mem, d, n_mem, tnx, F32)
    gs["mem_norm"] = _rms_bwd("mem_norm_bwd", mem, sm["mem_norm"], dmn)

    gb["w_mix_out"] = _mm1("mix_dwout", "tn", merged, dh2_bf, d, d, tw, tnx, BF16)

    def merge_bwd_epi(accs, gpv, gsv):
        dmerged, z_pool, val, gate = accs
        sp_, ss_, sg_ = jax.nn.sigmoid(gpv), jax.nn.sigmoid(gsv), jax.nn.sigmoid(gate)
        glu = val * sg_
        dz_pool = dmerged * sp_
        dg_pool = dmerged * z_pool * (sp_ * (1.0 - sp_))
        dz_ssm = dmerged * ss_
        dg_ssm = dmerged * glu * (ss_ * (1.0 - ss_))
        dval = dz_ssm * sg_
        dgate = dz_ssm * glu * (1.0 - sg_)
        return dz_pool, dg_pool, dg_ssm, dval, dgate

    dz_pool, dg_pool, dg_ssm, dval, dgate = _mm(
        "mix_merge_bwd", "nt", [dh2_bf, pm, ys], [wt["w_mix_out"], wt["w_pool_proj"], wt["w_glu_val"], wt["w_glu_gate"]],
        [[(0, 0)], [(1, 1)], [(2, 2)], [(2, 3)]], t, d, tmm, tnm, [(proj, gp_spec), (proj, gs_spec)], merge_bwd_epi,
        [(_out(t, d, BF16), None)] * 5)
    gb["w_pool_proj"] = _mm1("pool_dwproj", "tn", dz_pool, pm, d, d_pool, tw, d_pool, BF16)
    gb["w_glu_val"] = _mm1("glu_dwval", "tn", dval, ys, d, d_ssm, tw, d_ssm, BF16)
    gb["w_glu_gate"] = _mm1("glu_dwgate", "tn", dgate, ys, d, d_ssm, tw, d_ssm, BF16)

    def gelu_bwd_epi(accs, yv):
        _, vjp = jax.vjp(jax.nn.gelu, yv)
        return (vjp(accs[0])[0],)

    dy = _mm("glu_dy", "nn", [dval, dgate], [wt["w_glu_val"], wt["w_glu_gate"]], [[(0, 0), (1, 1)]], t, d_ssm, tmy, d_ssm,
             [(y, _tile(tmy, d_ssm))], gelu_bwd_epi, [(_out(t, d_ssm, F32), None)])[0]
    gs["ssm_d"] = _colsum_prod("ssm_dd", dy, proj, b_coff=off_s)
    dyp = _to_segments(dy)
    d_abr, d_abi, d_bbr, d_bbi, d_cre, d_cim, lams = [], [], [], [], [], [], []
    ts = _pick(n_state, 512)
    for dr in range(2):
        lr, li, dar, dai = _ssm_bwd(f"ssm_bwd{dr}", dyp, c_re[dr], c_im[dr], xs[dr][0], xs[dr][1], abr2[dr:dr + 1],
                                    abi2[dr:dr + 1], reverse=(dr == 1))
        d_abr.append(dar)
        d_abi.append(dai)
        lams += [lr, li]
        maps = _mm(f"ssm_dmaps{dr}", "tn", [sp32, dyp], [lr, li, xs[dr][0], xs[dr][1]],
                   [[(0, 0)], [(0, 1)], [(1, 2)], [(1, 3)]], d_ssm, n_state, d_ssm, ts, [], lambda accs: tuple(accs),
                   [(_out(d_ssm, n_state, F32), None)] * 4)
        for acc, m in zip((d_bbr, d_bbi, d_cre, d_cim), maps):
            acc.append(_diag(m, sg, sh, sp))
    ds = _from_segments(_mm(
        "ssm_ds", "nt", lams, [b_re[0], b_im[0], b_re[1], b_im[1]], [[(k, k) for k in range(4)]], t, d_ssm, tmy,
        d_ssm, [(dyp, _tile(tmy, d_ssm)), (sm["ssm_d"], _rowvec(d_ssm))],
        lambda accs, dyv, dv: (dyv * dv + accs[0],), [(_out(t, d_ssm, BF16), None)])[0])
    cots = [jnp.concatenate(d_abr, axis=0).reshape(2 * sg, 1, sp), jnp.concatenate(d_abi, axis=0).reshape(2 * sg, 1, sp),
            jnp.concatenate(d_bbr, axis=0), jnp.concatenate(d_bbi, axis=0)]
    d_are, d_aim, d_ldt, d_bre, d_bim = _ssm_disc_bwd(disc_args, cots)
    gs["ssm_a_re"] = d_are.reshape(2, sg, sp)
    gs["ssm_a_im"] = d_aim.reshape(2, sg, sp)
    gs["ssm_log_dt"] = d_ldt.reshape(2, sg)
    from_p = lambda a: jnp.swapaxes(a.reshape(2, sg, sh, sp), -1, -2)
    gs["ssm_b_re"], gs["ssm_b_im"] = from_p(d_bre), from_p(d_bim)
    gs["ssm_c_re"] = jnp.stack(d_cre, axis=0)
    gs["ssm_c_im"] = -jnp.stack(d_cim, axis=0)

    dpm = _mm1("pool_dpm", "nn", dz_pool, wt["w_pool_proj"], t, d_pool, tmm, _pick(d_pool, 256), F32)
    dp, gs["pool_w"], gs["pool_scale"] = _pool_bwd(pooled, dpm, pool_w_bf, sm["pool_scale"])

    w_in = wt["w_in"]
    parts = [(dp, 0, d_pool), (ds, d_pool, d_ssm), (dg_pool, off_gp, d), (dg_ssm, off_gs, d)]
    w_in_parts = [w_in[o0:o0 + width] for _, o0, width in parts]
    gb["w_in"] = jnp.concatenate(
        [_mm1(f"in_proj_dw{k}", "tn", p_[0], u, p_[2], d, _pick(p_[2], 1024), tnx, BF16) for k, p_ in enumerate(parts)], axis=0)
    pin = emit("grads_main", gb=gb)
    du = _mm("in_proj_du", "nn", [p_[0] for p_ in parts], w_in_parts, [[(k, k) for k in range(4)]], t, d, tmm, tnx, [],
             lambda accs: (accs[0],), [(_out(t, d, F32), None)], after=pin)[0]
    dh1, dh1_bf, gs["mix_norm"] = _rms_bwd("mix_norm_bwd", h1, sm["mix_norm"], du, dh2)
    pin = emit("small_early", gs=gs, loss=loss)

    def ffn1_weights_done(d_wg, d_wu, d_wd):
        gb["ffn1_w_gate"], gb["ffn1_w_up"], gb["ffn1_w_down"] = d_wg, d_wu, d_wd
        return emit("grads_ffn1", gb=gb)

    dx, _, gs["ffn1_norm"], _, _, _ = _ffn_bwd(
        "ffn1", x, sm["ffn1_norm"], wt["ffn1_w_gate"], wt["ffn1_w_up"], wt["ffn1_w_down"], ffn1_saved, dh1, dh1_bf,
        weights_done=ffn1_weights_done, after=pin)
    return loss, dx, gb, gs


WEIGHTS = ["ffn1_norm", "ffn1_w_gate", "ffn1_w_up", "ffn1_w_down", "mix_norm", "w_in", "pool_w", "pool_scale",
           "w_pool_proj", "ssm_a_re", "ssm_a_im", "ssm_log_dt", "ssm_b_re", "ssm_b_im", "ssm_c_re", "ssm_c_im", "ssm_d",
           "w_glu_val", "w_glu_gate", "w_mix_out", "xattn_norm", "mem_norm", "w_q", "w_kv", "w_xo", "ffn2_norm",
           "ffn2_w_gate", "ffn2_w_up", "ffn2_w_down", "final_norm"]
COL_SHARDED = ["ffn1_w_gate", "ffn1_w_up", "w_in", "w_pool_proj", "w_glu_val", "w_glu_gate", "w_kv", "ffn2_w_gate",
               "ffn2_w_up"]
ROW_SHARDED = ["ffn1_w_down", "w_mix_out", "w_q", "w_xo", "ffn2_w_down"]
BIG = [n for n in WEIGHTS if n in COL_SHARDED or n in ROW_SHARDED]
SMALL = [n for n in WEIGHTS if n not in BIG]
FFN1_BIG = ["ffn1_w_gate", "ffn1_w_up", "ffn1_w_down"]
MAIN_BIG = [n for n in BIG if n not in FFN1_BIG]
GATHER_PLAN = [("ffn1_up_done", ["ffn1_w_down"]), ("ffn1_fwd_done", ["w_in"]),
               ("mix_in_done", ["w_pool_proj", "w_glu_val", "w_glu_gate", "w_mix_out"]),
               ("mix_done", ["w_q", "w_kv", "w_xo"]), ("xattn_done", ["ffn2_w_gate", "ffn2_w_up", "ffn2_w_down"])]
MINOR_SWAPPED = ["ssm_b_re", "ssm_b_im"]
LATE_SMALL = "ffn1_norm"
EARLY_SMALL = [n for n in SMALL if n != LATE_SMALL]
PACK_ROWS = SUBLANES * LANES
GRAD_ROW_TILE = 256
ADAMW_STEP_WORDS = 1 << 19


def _to_rows(name, w, width):
    if name in COL_SHARDED:
        w = w.T
    return w.reshape(-1, width)


def _pack_small(vals):
    flat = []
    for v in vals:
        f = v.reshape(-1)
        flat.append(jnp.pad(f, (0, (-f.shape[0]) % PACK_ROWS)))
    total = sum(f.shape[0] for f in flat)
    flat.append(jnp.zeros(((-total) % (GRAD_ROW_TILE * LANES),), F32))
    return jnp.concatenate(flat).reshape(-1, LANES)


def _unpack_small(packed, shapes):
    out, row = [], 0
    for shp in shapes:
        size = math.prod(shp)
        rows = -(-size // PACK_ROWS) * SUBLANES
        out.append(packed[row:row + rows].reshape(-1)[:size].reshape(shp))
        row += rows
    return out


def kernel(x, mem, ffn1_norm, ffn1_w_gate, ffn1_w_up, ffn1_w_down, mix_norm, w_in, pool_w, pool_scale, w_pool_proj, ssm_a_re, ssm_a_im, ssm_log_dt, ssm_b_re, ssm_b_im, ssm_c_re, ssm_c_im, ssm_d, w_glu_val, w_glu_gate, w_mix_out, xattn_norm, mem_norm, w_q, w_kv, w_xo, ffn2_norm, ffn2_w_gate, ffn2_w_up, ffn2_w_down, final_norm, loss_target, m_ffn1_norm, m_ffn1_w_gate, m_ffn1_w_up, m_ffn1_w_down, m_mix_norm, m_w_in, m_pool_w, m_pool_scale, m_w_pool_proj, m_ssm_a_re, m_ssm_a_im, m_ssm_log_dt, m_ssm_b_re, m_ssm_b_im, m_ssm_c_re, m_ssm_c_im, m_ssm_d, m_w_glu_val, m_w_glu_gate, m_w_mix_out, m_xattn_norm, m_mem_norm, m_w_q, m_w_kv, m_w_xo, m_ffn2_norm, m_ffn2_w_gate, m_ffn2_w_up, m_ffn2_w_down, m_final_norm, v_ffn1_norm, v_ffn1_w_gate, v_ffn1_w_up, v_ffn1_w_down, v_mix_norm, v_w_in, v_pool_w, v_pool_scale, v_w_pool_proj, v_ssm_a_re, v_ssm_a_im, v_ssm_log_dt, v_ssm_b_re, v_ssm_b_im, v_ssm_c_re, v_ssm_c_im, v_ssm_d, v_w_glu_val, v_w_glu_gate, v_w_mix_out, v_xattn_norm, v_mem_norm, v_w_q, v_w_kv, v_w_xo, v_ffn2_norm, v_ffn2_w_gate, v_ffn2_w_up, v_ffn2_w_down, v_final_norm):
    given = dict(locals())
    wts = {n: given[n] for n in WEIGHTS}
    moms = {n: (given["m_" + n], given["v_" + n]) for n in WEIGHTS}
    x2, mem2, tgt2 = x[0], mem[0], loss_target[0]
    d = x2.shape[1]
    chip = (2 * lax.axis_index("x") + lax.axis_index("y")).astype(jnp.int32).reshape(1)

    def full_form(n, f):
        shard = wts[n][0].shape
        return f.reshape(N_DEV * shard[1], shard[0]) if n in COL_SHARDED else f.reshape(N_DEV * shard[0], shard[1])

    shards = {n: _to_rows(n, wts[n][0], d).astype(BF16) for n in BIG}
    first = FFN1_BIG[:2]
    wt = {n: full_form(n, f) for n, f in zip(first, _allgather("weight_allgather_first", [shards[n] for n in first]))}
    started = _split_start("weight_gather_start", [_gather_group([shards[n] for n in names]) for _, names in GATHER_PLAN],
                           after=wt[first[0]])
    gathers = {event: (names, st) for (event, names), st in zip(GATHER_PLAN, started)}
    sm = {n: (wts[n].reshape(1, -1) if wts[n].ndim <= 2 else wts[n][0]) for n in SMALL}
    sm["ffn1_norm"] = sm["ffn1_norm"] + started[0]["token"][0, 0]

    pending = {}

    def reduce_start(tag, names, gb):
        blocks = [gb[n].reshape(N_DEV, -1, d) for n in names]
        pad_rows = (-sum(b.shape[1] for b in blocks)) % GRAD_ROW_TILE
        pad = [jnp.zeros((N_DEV, pad_rows, d), BF16)] if pad_rows else []
        started = _cores_start("grad_exchange_cores_start_" + tag, blocks + pad)
        own = jnp.concatenate([lax.dynamic_index_in_dim(b.reshape(4, 2, b.shape[1], d), lax.axis_index("c"), 1, False)
                               for b in started["bufs"][:len(blocks + pad)]], axis=1)
        _, (recv,) = _split_wait("grad_exchange_cores_wait_" + tag, started, own)
        rows_all = own.shape[1]
        pair = _ew("grad_pair_sum_" + tag, lambda a, b: (a.astype(F32) + b.astype(F32),),
                   [own.reshape(-1, d), recv.reshape(-1, d)], [BF16], rows_pref=5 * GRAD_ROW_TILE)[0]
        pair = pair.reshape(4, rows_all, d)
        pending[tag] = (pair, _chips_start("grad_exchange_chips_start_" + tag, pair), [b.shape[1] for b in blocks])
        return pending[tag][1]["token"]

    def reduce_finish(tag, after):
        _, started, rows = pending[tag]
        (pair,), (recv,) = _split_wait("grad_exchange_chips_wait_" + tag, started, after)
        return _chip_sum("grad_chip_sum_" + tag, pair, recv, chip), rows

    def ev(name, gb=None, gs=None, loss=None, marker=None):
        if name in gathers:
            names, started = gathers[name]
            for n, f in zip(names, _split_wait("weight_gather_wait_" + name, started, marker)[1]):
                wt[n] = full_form(n, f)
        elif name == "grads_main":
            return reduce_start("main", MAIN_BIG, gb)
        elif name == "small_early":
            pending["small"] = _slots_start("small_gather_start", _pack_small([gs[n] for n in EARLY_SMALL] + [loss[:, :1]]))
            return pending["small"]["token"]
        elif name == "grads_ffn1":
            return reduce_start("ffn1", FFN1_BIG, gb)
        return None

    _, dx, _, gs = _local_step(x2, mem2, tgt2, wt, sm, ev)

    grads = {}
    for tag, names in (("main", MAIN_BIG), ("ffn1", FFN1_BIG)):
        g_rows, rows = reduce_finish(tag, dx)
        off = 0
        for n, r in zip(names, rows):
            shard = wts[n].shape
            grads[n] = g_rows[off:off + r].reshape((shard[2], shard[1]) if n in COL_SHARDED else shard[1:])
            off += r
    small_sum = _sum_slots("small_sum", _split_wait("small_gather_wait", pending["small"], dx)[1][0], F32)
    late = _allgather("small_allgather_late", [gs[LATE_SMALL].reshape(-1, LANES)])[0]
    late_sum = _sum_slots("small_sum_late", late.reshape(N_DEV, -1, LANES), F32)
    vals = _unpack_small(small_sum, [wts[n].shape for n in EARLY_SMALL] + [(1, 1)])
    total_loss = vals[-1].reshape(())
    def flat(n, a):
        a = a.reshape(wts[n].shape)
        a = jnp.swapaxes(a, -1, -2) if n in MINOR_SWAPPED else a
        return a.reshape(-1, a.shape[-1])

    def unflat(n, a):
        shape = wts[n].shape
        if n in MINOR_SWAPPED:
            return jnp.swapaxes(a.reshape(shape[:-2] + (shape[-1], shape[-2])), -1, -2)
        return a.reshape(shape)

    for n, g_full in zip(EARLY_SMALL + [LATE_SMALL], vals[:-1] + [late_sum]):
        grads[n] = flat(n, g_full)

    out_g, out_d, out_m, out_v = {}, {}, {}, {}
    by_shape = {}
    for n in WEIGHTS:
        by_shape.setdefault((flat(n, wts[n]).shape, n in COL_SHARDED), []).append(n)
    for (_, transposed), names in by_shape.items():
        items = [(flat(n, wts[n]), grads[n], flat(n, moms[n][0]), flat(n, moms[n][1])) for n in names]
        for n, res in zip(names, _adamw_group("adamw_" + names[0], items, transposed)):
            out_d[n], out_m[n], out_v[n], out_g[n] = (unflat(n, a) for a in res)

    return (total_loss, dx[None], *[out_g[n] for n in WEIGHTS], *[out_d[n] for n in WEIGHTS],
            *[out_m[n] for n in WEIGHTS], *[out_v[n] for n in WEIGHTS])
```

```python
import functools
import math

import jax
import jax.numpy as jnp
from jax import lax
from jax.experimental import pallas as pl
from jax.experimental.pallas import tpu as pltpu

F32 = jnp.float32
BF16 = jnp.bfloat16
EPS = 1e-6
N_XHEADS = 4
POOL_WINDOWS = (2, 4, 8, 16)
ADAM_LR = 0.001
ADAM_B1 = 0.9
ADAM_B2 = 0.999
ADAM_EPS = 1e-08
ADAM_WD = 0.01
ADAM_STEP = 10
N_DEV = 8
VMEM_LIMIT_V7X = 48 * 1024 * 1024
LANES = 128
SUBLANES = 8
SUB_ROWS = 256
POOL_PAD = 16
MESH = pl.DeviceIdType.MESH
ANY = pl.BlockSpec(memory_space=pl.ANY)
HBM = pl.BlockSpec(memory_space=pltpu.HBM)
SEM = pl.BlockSpec(memory_space=pltpu.SEMAPHORE)
SIDE_EFFECT = pltpu.SideEffectType.DATAFLOW_SIDE_EFFECTING

_DIMS = {
    "nt": (((1,), (1,)), ((), ())),
    "nn": (((1,), (0,)), ((), ())),
    "tn": (((0,), (0,)), ((), ())),
}


def _pick(dim, pref, mult=LANES):
    if dim <= pref:
        return dim
    for t in range(pref - pref % mult, 0, -mult):
        if dim % t == 0:
            return t
    return dim


def _params(sem):
    return pltpu.CompilerParams(dimension_semantics=sem, vmem_limit_bytes=VMEM_LIMIT_V7X)


def _tile(tm, tn, coff=0):
    return pl.BlockSpec((tm, tn), lambda i, j: (i, j + coff))


def _rowvec(tn, coff=0):
    return pl.BlockSpec((1, tn), lambda i, j: (0, j + coff))


def _out(m, n, dtype):
    return jax.ShapeDtypeStruct((m, n), dtype)


def _mm(name, form, a_list, b_list, groups, m, n, tm, tn, extras, epilogue, outs, after=None, sub=SUB_ROWS):
    na, nb, ne = len(a_list), len(b_list), len(extras)
    pins = [] if after is None else [after]
    step = tm if (sub is None or form == "tn" or tm % sub) else sub

    def a_spec(a):
        if form == "tn":
            return pl.BlockSpec((a.shape[0], tm), lambda i, j: (0, i))
        return pl.BlockSpec((tm, a.shape[1]), lambda i, j: (i, 0))

    def b_spec(b):
        if form == "nt":
            return pl.BlockSpec((tn, b.shape[1]), lambda i, j: (j, 0))
        return pl.BlockSpec((b.shape[0], tn), lambda i, j: (0, j))

    def body(*refs):
        a_refs, b_refs = refs[:na], refs[na:na + nb]
        e_refs, o_refs = refs[na + nb:na + nb + ne], refs[na + nb + ne + len(pins):]
        b_vals = {}
        for s0 in range(0, tm, step):
            rows = slice(None) if step == tm else pl.ds(s0, step)
            a_vals, accs = {}, []
            for group in groups:
                acc = None
                for ai, bi in group:
                    if ai not in a_vals:
                        a_vals[ai] = (a_refs[ai][...] if form == "tn" else a_refs[ai][rows, :]).astype(BF16)
                    if bi not in b_vals:
                        b_vals[bi] = b_refs[bi][...].astype(BF16)
                    d = lax.dot_general(a_vals[ai], b_vals[bi], _DIMS[form], preferred_element_type=F32)
                    acc = d if acc is None else acc + d
                accs.append(acc)
            res = epilogue(accs, *[e[rows, :] if e.shape[0] == tm else e[...] for e in e_refs])
            for o_ref, r in zip(o_refs, res):
                o_ref[rows, :] = r.astype(o_ref.dtype)

    out_specs = [_tile(tm, tn) if s is None else s for _, s in outs]
    res = pl.pallas_call(
        body, name=name, grid=(m // tm, n // tn),
        in_specs=[a_spec(a) for a in a_list] + [b_spec(b) for b in b_list] + [s for _, s in extras] + [ANY] * len(pins),
        out_specs=out_specs, out_shape=[o for o, _ in outs],
        compiler_params=_params(("parallel", "parallel")),
    )(*a_list, *b_list, *[e for e, _ in extras], *pins)
    return res


def _mm1(name, form, a, b, m, n, tm, tn, dtype, scale=None):
    epi = (lambda accs: (accs[0],)) if scale is None else (lambda accs: (accs[0] * scale,))
    return _mm(name, form, [a], [b], [[(0, 0)]], m, n, tm, tn, [], epi, [(_out(m, n, dtype), None)])[0]


def _rms_fwd(name, h, g):
    t, d = h.shape
    tm = _pick(t, 512, SUBLANES)

    def body(h_ref, g_ref, n_ref):
        hv = h_ref[...]
        r = lax.rsqrt(jnp.mean(hv * hv, axis=-1, keepdims=True) + EPS)
        n_ref[...] = ((hv * r) * g_ref[...]).astype(BF16)

    return pl.pallas_call(
        body, name=name, grid=(t // tm,),
        in_specs=[pl.BlockSpec((tm, d), lambda i: (i, 0)), pl.BlockSpec((1, d), lambda i: (0, 0))],
        out_specs=pl.BlockSpec((tm, d), lambda i: (i, 0)), out_shape=_out(t, d, BF16),
        compiler_params=_params(("parallel",)),
    )(h, g)


def _rms_bwd(name, h, g, dn, dres=None):
    t, d = h.shape
    tm = _pick(t, 512, SUBLANES)
    need_dh = dres is not None

    def body(*refs):
        if need_dh:
            h_ref, g_ref, dn_ref, dres_ref, dh_ref, dhb_ref, dg_ref = refs
        else:
            h_ref, g_ref, dn_ref, dg_ref = refs
        hv = h_ref[...]
        r = lax.rsqrt(jnp.mean(hv * hv, axis=-1, keepdims=True) + EPS)
        nh = hv * r
        dnv = dn_ref[...].astype(F32)

        @pl.when(pl.program_id(0) == 0)
        def _():
            dg_ref[...] = jnp.zeros_like(dg_ref)

        dg_ref[...] += jnp.sum(dnv * nh, axis=0, keepdims=True)
        if need_dh:
            dng = dnv * g_ref[...]
            dh = dres_ref[...] + r * (dng - nh * jnp.mean(dng * nh, axis=-1, keepdims=True))
            dh_ref[...] = dh
            dhb_ref[...] = dh.astype(BF16)

    row = pl.BlockSpec((tm, d), lambda i: (i, 0))
    vec = pl.BlockSpec((1, d), lambda i: (0, 0))
    if need_dh:
        return pl.pallas_call(
            body, name=name, grid=(t // tm,), in_specs=[row, vec, row, row], out_specs=[row, row, vec],
            out_shape=[_out(t, d, F32), _out(t, d, BF16), _out(1, d, F32)], compiler_params=_params(("arbitrary",)),
        )(h, g, dn, dres)
    return pl.pallas_call(
        body, name=name, grid=(t // tm,), in_specs=[row, vec, row], out_specs=vec,
        out_shape=_out(1, d, F32), compiler_params=_params(("arbitrary",)),
    )(h, g, dn)


def _loss_head(h, g, tgt):
    t, d = h.shape
    tm = _pick(t, 512, SUBLANES)

    def body(h_ref, g_ref, t_ref, dh_ref, dhb_ref, dg_ref, loss_ref):
        hv = h_ref[...]
        r = lax.rsqrt(jnp.mean(hv * hv, axis=-1, keepdims=True) + EPS)
        nh = hv * r
        err = nh * g_ref[...] - t_ref[...]

        @pl.when(pl.program_id(0) == 0)
        def _():
            dg_ref[...] = jnp.zeros_like(dg_ref)
            loss_ref[...] = jnp.zeros_like(loss_ref)

        per_row = jnp.mean(err * err, axis=-1, keepdims=True)
        loss_ref[...] += 0.5 * jnp.sum(per_row, axis=0, keepdims=True)
        dy = err * (1.0 / d)
        dg_ref[...] += jnp.sum(dy * nh, axis=0, keepdims=True)
        dng = dy * g_ref[...]
        dh = r * (dng - nh * jnp.mean(dng * nh, axis=-1, keepdims=True))
        dh_ref[...] = dh
        dhb_ref[...] = dh.astype(BF16)

    row = pl.BlockSpec((tm, d), lambda i: (i, 0))
    vec = pl.BlockSpec((1, d), lambda i: (0, 0))
    return pl.pallas_call(
        body, name="loss_head", grid=(t // tm,), in_specs=[row, vec, row],
        out_specs=[row, row, vec, pl.BlockSpec((1, LANES), lambda i: (0, 0))],
        out_shape=[_out(t, d, F32), _out(t, d, BF16), _out(1, d, F32), _out(1, LANES, F32)],
        compiler_params=_params(("arbitrary",)),
    )(h, g, tgt)


def _ffn_fwd(tag, h, n, wg_t, wu_t, wd):
    t, d = h.shape
    f = wg_t.shape[0]
    tm, tn = _pick(t, 1024), _pick(f, 1408)

    def up_epi(accs):
        a, b = accs
        return a, b, (a * jax.nn.sigmoid(a)) * b

    a, b, hid = _mm(tag + "_up", "nt", [n], [wg_t, wu_t], [[(0, 0)], [(0, 1)]], t, f, tm, tn, [], up_epi,
                    [(_out(t, f, BF16), None)] * 3)
    if callable(wd):
        wd = wd(hid)
    tm2, tn2 = _pick(t, 1024), _pick(d, 512)
    h_out = _mm(tag + "_down", "nn", [hid], [wd], [[(0, 0)]], t, d, tm2, tn2, [(h, _tile(tm2, tn2))],
                lambda accs, hin: (hin + 0.5 * accs[0],), [(_out(t, d, F32), None)])[0]
    return h_out, (n, a, b, hid)


def _ffn_bwd(tag, h, g, wg_t, wu_t, wd, saved, dh, dh_bf, weights_done=None, after=None):
    n, a, b, hid = saved
    t, d = h.shape
    f = wd.shape[0]
    tm, tn = _pick(t, 1024), _pick(f, 1408)

    def hid_epi(accs, av, bv):
        dhid = 0.5 * accs[0]
        av, bv = av.astype(F32), bv.astype(F32)
        sig = jax.nn.sigmoid(av)
        da = dhid * bv * (sig * (1.0 + av * (1.0 - sig)))
        db = dhid * (av * sig)
        return da, db

    da, db = _mm(tag + "_bwd_hid", "nt", [dh_bf], [wd], [[(0, 0)]], t, f, tm, tn,
                 [(a, _tile(tm, tn)), (b, _tile(tm, tn))], hid_epi, [(_out(t, f, BF16), None)] * 2, after=after)
    tw, tnw = _pick(f, 1408), _pick(d, 512)
    d_wd = _mm1(tag + "_dwd", "tn", hid, dh_bf, f, d, tw, tnw, BF16, scale=0.5)
    d_wg = _mm1(tag + "_dwg", "tn", da, n, f, d, tw, tnw, BF16)
    d_wu = _mm1(tag + "_dwu", "tn", db, n, f, d, tw, tnw, BF16)
    pin = weights_done(d_wg, d_wu, d_wd) if weights_done is not None else None
    tm2, tn2 = _pick(t, 1024), _pick(d, 512)
    dn = _mm(tag + "_dn", "nn", [da, db], [wg_t, wu_t], [[(0, 0), (1, 1)]], t, d, tm2, tn2, [],
             lambda accs: (accs[0],), [(_out(t, d, F32), None)], after=pin)[0]
    dh_in, dh_in_bf, dg = _rms_bwd(tag + "_norm_bwd", h, g, dn, dh)
    return dh_in, dh_in_bf, dg, d_wg, d_wu, d_wd


def _window_sum(win, offsets):
    n = win.shape[0]
    acc = None
    for j in offsets:
        term = win if j == 0 else pltpu.roll(win, (-j) % n, 0)
        acc = term if acc is None else acc + term
    return acc


def _pool_counts(r0, ch, c, left, right, t):
    pos = r0 + lax.broadcasted_iota(jnp.int32, (ch, c), 0)
    return (jnp.minimum(pos + right + 1, t) - jnp.maximum(pos - left, 0)).astype(F32)


def _pool_fwd(proj, pool_w_bf, pool_scale):
    t = proj.shape[0]
    ng, c, _ = pool_w_bf.shape
    ch = _pick(t, 256, SUBLANES)
    pad = POOL_PAD

    def body(p_ref, w_ref, s_ref, pooled_ref, pm_ref, buf):
        grp = pl.program_id(0)
        buf[pl.ds(0, pad), :] = jnp.zeros((pad, c), F32)
        buf[pl.ds(pad + t, pad), :] = jnp.zeros((pad, c), F32)

        def fill(ci, carry):
            r0 = pl.multiple_of(ci * ch, SUBLANES)
            buf[pl.ds(pl.multiple_of(r0 + pad, SUBLANES), ch), :] = p_ref[pl.ds(r0, ch), :]
            return carry

        lax.fori_loop(0, t // ch, fill, 0)
        for gi, w in enumerate(POOL_WINDOWS):
            left = w // 2
            right = w - 1 - left

            @pl.when(grp == gi)
            def _(left=left, right=right):
                def chunk(ci, carry):
                    r0 = pl.multiple_of(ci * ch, SUBLANES)
                    win = buf[pl.ds(r0, ch + 2 * pad), :]
                    s = _window_sum(win, range(-left, right + 1))[pad:pad + ch]
                    pooled = s / _pool_counts(r0, ch, c, left, right, t) - win[pad:pad + ch]
                    pooled_bf = pooled.astype(BF16)
                    mixed = jnp.dot(pooled_bf, w_ref[0], preferred_element_type=F32)
                    pooled_ref[pl.ds(r0, ch), :] = pooled_bf
                    pm_ref[pl.ds(r0, ch), :] = (mixed * s_ref[...]).astype(BF16)
                    return carry

                lax.fori_loop(0, t // ch, chunk, 0)

    col = pl.BlockSpec((t, c), lambda g: (0, g))
    return pl.pallas_call(
        body, name="pool_fwd", grid=(ng,),
        in_specs=[col, pl.BlockSpec((1, c, c), lambda g: (g, 0, 0)), pl.BlockSpec((1, c), lambda g: (0, g))],
        out_specs=[col, col], out_shape=[_out(t, ng * c, BF16), _out(t, ng * c, BF16)],
        scratch_shapes=[pltpu.VMEM((t + 2 * pad, c), F32)],
        compiler_params=_params(("parallel",)),
    )(proj, pool_w_bf, pool_scale)


def _pool_bwd(pooled, dpm, pool_w_bf, pool_scale):
    t = pooled.shape[0]
    ng, c, _ = pool_w_bf.shape
    ch = _pick(t, 256, SUBLANES)
    pad = POOL_PAD

    def body(pooled_ref, dpm_ref, w_ref, s_ref, dp_ref, dw_ref, ds_ref, buf, raw):
        grp = pl.program_id(0)
        buf[pl.ds(0, pad), :] = jnp.zeros((pad, c), F32)
        buf[pl.ds(pad + t, pad), :] = jnp.zeros((pad, c), F32)
        dw_ref[...] = jnp.zeros_like(dw_ref)
        ds_ref[...] = jnp.zeros_like(ds_ref)
        for gi, w in enumerate(POOL_WINDOWS):
            left = w // 2
            right = w - 1 - left

            @pl.when(grp == gi)
            def _(left=left, right=right):
                def first(ci, carry):
                    r0 = pl.multiple_of(ci * ch, SUBLANES)
                    pv = pooled_ref[pl.ds(r0, ch), :]
                    dpm_v = dpm_ref[pl.ds(r0, ch), :]
                    mixed = jnp.dot(pv, w_ref[0], preferred_element_type=F32)
                    ds_ref[...] += jnp.sum(dpm_v * mixed, axis=0, keepdims=True)
                    dmixed = (dpm_v * s_ref[...]).astype(BF16)
                    dw_ref[0] += lax.dot_general(pv, dmixed, _DIMS["tn"], preferred_element_type=F32)
                    dpooled = lax.dot_general(dmixed, w_ref[0], _DIMS["nt"], preferred_element_type=F32)
                    raw[pl.ds(r0, ch), :] = dpooled
                    buf[pl.ds(pl.multiple_of(r0 + pad, SUBLANES), ch), :] = (
                        dpooled / _pool_counts(r0, ch, c, left, right, t))
                    return carry

                lax.fori_loop(0, t // ch, first, 0)

                def second(ci, carry):
                    r0 = pl.multiple_of(ci * ch, SUBLANES)
                    win = buf[pl.ds(r0, ch + 2 * pad), :]
                    s = _window_sum(win, range(-right, left + 1))[pad:pad + ch]
                    dp_ref[pl.ds(r0, ch), :] = (s - raw[pl.ds(r0, ch), :]).astype(BF16)
                    return carry

                lax.fori_loop(0, t // ch, second, 0)

    col = pl.BlockSpec((t, c), lambda g: (0, g))
    return pl.pallas_call(
        body, name="pool_bwd", grid=(ng,),
        in_specs=[col, col, pl.BlockSpec((1, c, c), lambda g: (g, 0, 0)), pl.BlockSpec((1, c), lambda g: (0, g))],
        out_specs=[col, pl.BlockSpec((1, c, c), lambda g: (g, 0, 0)), pl.BlockSpec((1, c), lambda g: (0, g))],
        out_shape=[_out(t, ng * c, BF16), jax.ShapeDtypeStruct((ng, c, c), F32), _out(1, ng * c, F32)],
        scratch_shapes=[pltpu.VMEM((t + 2 * pad, c), F32), pltpu.VMEM((t, c), F32)],
        compiler_params=_params(("parallel",)),
    )(pooled, dpm, pool_w_bf, pool_scale)


def _discretise(a_re, a_im, log_dt, b_re, b_im):
    dt = jnp.exp(log_dt)
    mag = jnp.exp(dt * a_re)
    ang = dt * a_im
    abr = mag * jnp.cos(ang)
    abi = mag * jnp.sin(ang)
    den = a_re * a_re + a_im * a_im
    nr = abr - 1.0
    qr = (nr * a_re + abi * a_im) / den
    qi = (abi * a_re - nr * a_im) / den
    return abr, abi, qr * b_re - qi * b_im, qr * b_im + qi * b_re


def _ssm_disc(args):
    def body(ar, ai, ld, br, bi, o1, o2, o3, o4):
        res = _discretise(ar[...], ai[...], ld[...], br[...], bi[...])
        for o, r in zip((o1, o2, o3, o4), res):
            o[...] = r

    like = lambda a: jax.ShapeDtypeStruct(a.shape, F32)
    return pl.pallas_call(
        body, name="ssm_disc", out_shape=[like(args[0]), like(args[0]), like(args[3]), like(args[3])],
    )(*args)


def _ssm_disc_bwd(args, cots):
    def body(ar, ai, ld, br, bi, c1, c2, c3, c4, o1, o2, o3, o4, o5):
        _, vjp = jax.vjp(_discretise, ar[...], ai[...], ld[...], br[...], bi[...])
        res = vjp((c1[...], c2[...], c3[...], c4[...]))
        for o, r in zip((o1, o2, o3, o4, o5), res):
            o[...] = r

    return pl.pallas_call(
        body, name="ssm_disc_bwd", out_shape=[jax.ShapeDtypeStruct(a.shape, F32) for a in args],
    )(*args, *cots)


def _cmul(pr, pi, qr, qi):
    return pr * qr - pi * qi, pr * qi + pi * qr


def _cpow(pr, pi, n):
    rr, ri = None, None
    while n:
        if n & 1:
            rr, ri = (pr, pi) if rr is None else _cmul(rr, ri, pr, pi)
        n >>= 1
        if n:
            pr, pi = _cmul(pr, pi, pr, pi)
    return rr, ri


def _segment_carry(er, ei, pr, pi, reverse):
    row = lax.broadcasted_iota(jnp.int32, er.shape, 0)
    cr, ci = jnp.zeros_like(er), jnp.zeros_like(ei)
    for _ in range(SUBLANES - 1):
        tr = er + pr * cr - pi * ci
        ti = ei + pr * ci + pi * cr
        if reverse:
            keep, shift = row < SUBLANES - 1, SUBLANES - 1
        else:
            keep, shift = row >= 1, 1
        cr = jnp.where(keep, pltpu.roll(tr, shift, 0), 0.0)
        ci = jnp.where(keep, pltpu.roll(ti, shift, 0), 0.0)
    return cr, ci


def _ssm_fwd(name, sp, b_re, b_im, c_re, c_im, ar, ai, reverse):
    t, c = sp.shape
    s = ar.shape[1]
    w = _pick(s, 512)
    ch = _pick(t, 512, SUBLANES)
    n_ch, gpc, steps = t // ch, ch // SUBLANES, t // SUBLANES

    def body(sp_ref, bre_ref, bim_ref, cre_ref, cim_ref, ar_ref, ai_ref, xr_ref, xi_ref, y_ref, ur, ui, xbr, xbi):
        a_r = jnp.broadcast_to(ar_ref[...], (SUBLANES, w))
        a_i = jnp.broadcast_to(ai_ref[...], (SUBLANES, w))

        @pl.when(pl.program_id(0) == 0)
        def _():
            y_ref[...] = jnp.zeros_like(y_ref)

        def sweep(h0, store):
            def chunk(k, h):
                ci = n_ch - 1 - k if reverse else k
                rows = pl.ds(pl.multiple_of(ci * ch, ch), ch)
                spv = sp_ref[rows, :].astype(BF16)
                ur[...] = jnp.dot(spv, bre_ref[...], preferred_element_type=F32)
                ui[...] = jnp.dot(spv, bim_ref[...], preferred_element_type=F32)

                def group(g, hh):
                    gi = gpc - 1 - g if reverse else g
                    r0 = pl.multiple_of(gi * SUBLANES, SUBLANES)
                    hr, hi = hh
                    nr = a_r * hr - a_i * hi + ur[pl.ds(r0, SUBLANES), :]
                    ni = a_r * hi + a_i * hr + ui[pl.ds(r0, SUBLANES), :]
                    if store:
                        xbr[pl.ds(r0, SUBLANES), :] = nr
                        xbi[pl.ds(r0, SUBLANES), :] = ni
                    return nr, ni

                h = lax.fori_loop(0, gpc, group, h)
                if store:
                    xr16, xi16 = xbr[...].astype(BF16), xbi[...].astype(BF16)
                    xr_ref[rows, :] = xr16
                    xi_ref[rows, :] = xi16
                    y_ref[rows, :] += (lax.dot_general(xr16, cre_ref[...], _DIMS["nt"], preferred_element_type=F32)
                                       + lax.dot_general(xi16, cim_ref[...], _DIMS["nt"], preferred_element_type=F32))
                return h

            return lax.fori_loop(0, n_ch, chunk, h0)

        zero = jnp.zeros((SUBLANES, w), F32)
        er, ei = sweep((zero, zero), False)
        pr, pi = _cpow(ar_ref[...], ai_ref[...], steps)
        sweep(_segment_carry(er, ei, pr, pi, reverse), True)

    col = lambda i: (0, i)
    return pl.pallas_call(
        body, name=name, grid=(s // w,),
        in_specs=[pl.BlockSpec((t, c), lambda i: (0, 0))] + [pl.BlockSpec((c, w), col)] * 4
        + [pl.BlockSpec((1, w), col)] * 2,
        out_specs=[pl.BlockSpec((t, w), col), pl.BlockSpec((t, w), col), pl.BlockSpec((t, c), lambda i: (0, 0))],
        out_shape=[_out(t, s, BF16), _out(t, s, BF16), _out(t, c, F32)],
        scratch_shapes=[pltpu.VMEM((ch, w), F32)] * 4,
        compiler_params=_params(("arbitrary",)),
    )(sp, b_re, b_im, c_re, c_im, ar, ai)


def _ssm_bwd(name, dyp, c_re, c_im, xr, xi, ar, ai, reverse):
    t, c = dyp.shape
    s = ar.shape[1]
    w = _pick(s, 512)
    ch = _pick(t, 512, SUBLANES)
    n_ch, gpc, steps = t // ch, ch // SUBLANES, t // SUBLANES
    back = not reverse
    edge = 2 * SUBLANES

    def body(dy_ref, cre_ref, cim_ref, xr_ref, xi_ref, ar_ref, ai_ref, lr_ref, li_ref, dar_ref, dai_ref,
             gr, gi_, lbr, lbi, xbr, xbi):
        a_r = jnp.broadcast_to(ar_ref[...], (SUBLANES, w))
        a_i = -jnp.broadcast_to(ai_ref[...], (SUBLANES, w))
        row = lax.broadcasted_iota(jnp.int32, (SUBLANES, w), 0)

        def neighbours(ci, x_ref, buf):
            rows = pl.ds(pl.multiple_of(ci * ch, ch), ch)
            if reverse:
                buf[pl.ds(0, ch), :] = x_ref[rows, :].astype(F32)
                nxt = x_ref[pl.ds(pl.multiple_of(jnp.minimum(ci + 1, n_ch - 1) * ch, ch), edge), :].astype(F32)[:SUBLANES]
                first = x_ref[pl.ds(0, edge), :].astype(F32)[:SUBLANES]
                wrap = jnp.where(row < SUBLANES - 1, pltpu.roll(first, SUBLANES - 1, 0), 0.0)
                buf[pl.ds(ch, SUBLANES), :] = jnp.where(ci == n_ch - 1, wrap, nxt)
            else:
                buf[pl.ds(SUBLANES, ch), :] = x_ref[rows, :].astype(F32)
                prv = x_ref[pl.ds(pl.multiple_of(jnp.maximum(ci * ch - edge, 0), edge), edge), :].astype(F32)[SUBLANES:]
                last = x_ref[pl.ds(t - edge, edge), :].astype(F32)[SUBLANES:]
                wrap = jnp.where(row >= 1, pltpu.roll(last, 1, 0), 0.0)
                buf[pl.ds(0, SUBLANES), :] = jnp.where(ci == 0, wrap, prv)

        def sweep(h0, store):
            def chunk(k, carry):
                ci = n_ch - 1 - k if back else k
                rows = pl.ds(pl.multiple_of(ci * ch, ch), ch)
                dyv = dy_ref[rows, :].astype(BF16)
                gr[...] = jnp.dot(dyv, cre_ref[...], preferred_element_type=F32)
                gi_[...] = jnp.dot(dyv, cim_ref[...], preferred_element_type=F32)
                if store:
                    neighbours(ci, xr_ref, xbr)
                    neighbours(ci, xi_ref, xbi)

                def group(g, cc):
                    gidx = gpc - 1 - g if back else g
                    r0 = pl.multiple_of(gidx * SUBLANES, SUBLANES)
                    hr, hi = cc[0], cc[1]
                    nr = a_r * hr - a_i * hi + gr[pl.ds(r0, SUBLANES), :]
                    ni = a_r * hi + a_i * hr + gi_[pl.ds(r0, SUBLANES), :]
                    if not store:
                        return nr, ni
                    lbr[pl.ds(r0, SUBLANES), :] = nr
                    lbi[pl.ds(r0, SUBLANES), :] = ni
                    x0 = pl.multiple_of(r0 + SUBLANES, SUBLANES) if reverse else r0
                    xpr, xpi = xbr[pl.ds(x0, SUBLANES), :], xbi[pl.ds(x0, SUBLANES), :]
                    return nr, ni, cc[2] + nr * xpr + ni * xpi, cc[3] + ni * xpr - nr * xpi

                carry = lax.fori_loop(0, gpc, group, carry)
                if store:
                    lr_ref[rows, :] = lbr[...].astype(BF16)
                    li_ref[rows, :] = lbi[...].astype(BF16)
                return carry

            return lax.fori_loop(0, n_ch, chunk, h0)

        zero = jnp.zeros((SUBLANES, w), F32)
        er, ei = sweep((zero, zero), False)
        pr, pi = _cpow(ar_ref[...], -ai_ref[...], steps)
        cr, ci0 = _segment_carry(er, ei, pr, pi, back)
        _, _, dar, dai = sweep((cr, ci0, zero, zero), True)
        dar_ref[...] = jnp.sum(dar, axis=0, keepdims=True)
        dai_ref[...] = jnp.sum(dai, axis=0, keepdims=True)

    col = lambda i: (0, i)
    return pl.pallas_call(
        body, name=name, grid=(s // w,),
        in_specs=[pl.BlockSpec((t, c), lambda i: (0, 0)), pl.BlockSpec((c, w), col), pl.BlockSpec((c, w), col),
                  pl.BlockSpec((t, w), col), pl.BlockSpec((t, w), col), pl.BlockSpec((1, w), col), pl.BlockSpec((1, w), col)],
        out_specs=[pl.BlockSpec((t, w), col), pl.BlockSpec((t, w), col), pl.BlockSpec((1, w), col), pl.BlockSpec((1, w), col)],
        out_shape=[_out(t, s, BF16), _out(t, s, BF16), _out(1, s, F32), _out(1, s, F32)],
        scratch_shapes=[pltpu.VMEM((ch, w), F32)] * 4 + [pltpu.VMEM((ch + SUBLANES, w), F32)] * 2,
        compiler_params=_params(("parallel",)),
    )(dyp, c_re, c_im, xr, xi, ar, ai)


def _ssm_finish(y0, y1, sp, skip):
    t, c = sp.shape
    steps = t // SUBLANES
    w = _pick(c, LANES)

    def body(y0_ref, y1_ref, sp_ref, d_ref, y_ref, ys_ref):
        rows = pl.ds(pl.program_id(1), steps, stride=SUBLANES)
        y = y0_ref[rows, :] + y1_ref[rows, :] + sp_ref[rows, :] * d_ref[...]
        y_ref[...] = y
        ys_ref[...] = jax.nn.gelu(y).astype(BF16)

    whole = pl.BlockSpec((t, w), lambda j, k: (0, j))
    seg = pl.BlockSpec((steps, w), lambda j, k: (k, j))
    return pl.pallas_call(
        body, name="ssm_finish", grid=(c // w, SUBLANES),
        in_specs=[whole, whole, whole, pl.BlockSpec((1, w), lambda j, k: (0, j))], out_specs=[seg, seg],
        out_shape=[_out(t, c, F32), _out(t, c, BF16)], compiler_params=_params(("parallel", "arbitrary")),
    )(y0, y1, sp, skip)


def _to_segments(a):
    t, c = a.shape
    return a.reshape(SUBLANES, t // SUBLANES, c).transpose(1, 0, 2).reshape(t, c)


def _from_segments(a):
    t, c = a.shape
    return a.reshape(t // SUBLANES, SUBLANES, c).transpose(1, 0, 2).reshape(t, c)


def _colsum_prod(name, a, b, b_coff=0):
    t, n = a.shape
    tm = _pick(t, 512, SUBLANES)

    def body(a_ref, b_ref, o_ref):
        @pl.when(pl.program_id(0) == 0)
        def _():
            o_ref[...] = jnp.zeros_like(o_ref)

        o_ref[...] += jnp.sum(a_ref[...].astype(F32) * b_ref[...].astype(F32), axis=0, keepdims=True)

    return pl.pallas_call(
        body, name=name, grid=(t // tm,),
        in_specs=[pl.BlockSpec((tm, n), lambda i: (i, 0)), pl.BlockSpec((tm, n), lambda i: (i, b_coff))],
        out_specs=pl.BlockSpec((1, n), lambda i: (0, 0)), out_shape=_out(1, n, F32),
        compiler_params=_params(("arbitrary",)),
    )(a, b)


def _bd(blk):
    g, hh, p = blk.shape
    eye = jnp.eye(g, dtype=bool)[:, None, :, None]
    return jnp.where(eye, blk[:, :, None, :], 0.0).reshape(g * hh, g * p)


def _diag(dmat, g, hh, p):
    eye = jnp.eye(g, dtype=bool)[:, None, :, None]
    return jnp.sum(jnp.where(eye, dmat.reshape(g, hh, g, p), 0.0), axis=2)


def _softmax(qh, kh, scale):
    s = lax.dot_general(qh, kh, _DIMS["nt"], preferred_element_type=F32) * scale
    e = jnp.exp(s - jnp.max(s, axis=-1, keepdims=True))
    return e / jnp.sum(e, axis=-1, keepdims=True)


def _attn_fwd(q, kv):
    t, d = q.shape
    mm_ = kv.shape[0]
    hd = d // N_XHEADS
    scale = 1.0 / math.sqrt(hd)
    tm = _pick(t, 512, SUBLANES)

    def body(q_ref, kv_ref, o_ref):
        for h in range(N_XHEADS):
            sl = pl.ds(h * hd, hd)
            p = _softmax(q_ref[:, sl], kv_ref[:, sl], scale)
            o_ref[:, sl] = jnp.dot(p.astype(BF16), kv_ref[:, pl.ds(d + h * hd, hd)],
                                   preferred_element_type=F32).astype(BF16)

    return pl.pallas_call(
        body, name="attn_fwd", grid=(t // tm,),
        in_specs=[pl.BlockSpec((tm, d), lambda i: (i, 0)), pl.BlockSpec((mm_, 2 * d), lambda i: (0, 0))],
        out_specs=pl.BlockSpec((tm, d), lambda i: (i, 0)), out_shape=_out(t, d, BF16),
        compiler_params=_params(("parallel",)),
    )(q, kv)


def _attn_bwd(q, kv, do):
    t, d = q.shape
    mm_ = kv.shape[0]
    hd = d // N_XHEADS
    scale = 1.0 / math.sqrt(hd)
    tm = _pick(t, 512, SUBLANES)

    def body(q_ref, kv_ref, do_ref, dq_ref, dkv_ref):
        @pl.when(pl.program_id(0) == 0)
        def _():
            dkv_ref[...] = jnp.zeros_like(dkv_ref)

        for h in range(N_XHEADS):
            sl = pl.ds(h * hd, hd)
            vsl = pl.ds(d + h * hd, hd)
            qh, kh, doh = q_ref[:, sl], kv_ref[:, sl], do_ref[:, sl]
            p = _softmax(qh, kh, scale)
            dp = lax.dot_general(doh, kv_ref[:, vsl], _DIMS["nt"], preferred_element_type=F32)
            dkv_ref[:, vsl] += lax.dot_general(p.astype(BF16), doh, _DIMS["tn"], preferred_element_type=F32)
            ds = (p * (dp - jnp.sum(dp * p, axis=-1, keepdims=True)) * scale).astype(BF16)
            dq_ref[:, sl] = jnp.dot(ds, kh, preferred_element_type=F32).astype(BF16)
            dkv_ref[:, sl] += lax.dot_general(ds, qh, _DIMS["tn"], preferred_element_type=F32)

    row = pl.BlockSpec((tm, d), lambda i: (i, 0))
    full = pl.BlockSpec((mm_, 2 * d), lambda i: (0, 0))
    return pl.pallas_call(
        body, name="attn_bwd", grid=(t // tm,), in_specs=[row, full, row], out_specs=[row, full],
        out_shape=[_out(t, d, BF16), _out(mm_, 2 * d, F32)], compiler_params=_params(("arbitrary",)),
    )(q, kv, do)


def _ew(name, fn, ins, outs, rows_pref=256, rowvecs=()):
    r, c = ins[0].shape
    tr = _pick(r, rows_pref, SUBLANES)
    ni = len(ins) + len(rowvecs)

    def body(*refs):
        res = fn(*[x[...] for x in refs[:ni]])
        for o_ref, v in zip(refs[ni:], res):
            o_ref[...] = v.astype(o_ref.dtype)

    blk = pl.BlockSpec((tr, c), lambda i: (i, 0))
    vec = pl.BlockSpec((1, c), lambda i: (0, 0))
    return pl.pallas_call(
        body, name=name, grid=(r // tr,), in_specs=[blk] * len(ins) + [vec] * len(rowvecs), out_specs=[blk] * len(outs),
        out_shape=[_out(r, c, dt) for dt in outs], compiler_params=_params(("parallel",)),
    )(*ins, *rowvecs)


def _sum_slots(name, a, dtype):
    s, r, c = a.shape
    tr = _pick(r, 256, SUBLANES)

    def body(a_ref, o_ref):
        acc = a_ref[0].astype(F32)
        for k in range(1, s):
            acc = acc + a_ref[k].astype(F32)
        o_ref[...] = acc.astype(o_ref.dtype)

    return pl.pallas_call(
        body, name=name, grid=(r // tr,), in_specs=[pl.BlockSpec((s, tr, c), lambda i: (0, i, 0))],
        out_specs=pl.BlockSpec((tr, c), lambda i: (i, 0)), out_shape=_out(r, c, dtype),
        compiler_params=_params(("parallel",)),
    )(a)


def _adamw_step(wv, gv, mv, vv):
    bc1 = 1.0 - ADAM_B1 ** ADAM_STEP
    bc2 = 1.0 - ADAM_B2 ** ADAM_STEP
    m2 = ADAM_B1 * mv + (1.0 - ADAM_B1) * gv
    v2 = ADAM_B2 * vv + (1.0 - ADAM_B2) * (gv * gv)
    delta = -ADAM_LR * ((m2 / bc1) / (jnp.sqrt(v2 / bc2) + ADAM_EPS) + ADAM_WD * wv)
    return delta, m2, v2


def _adamw_group(name, items, transposed):
    k, r = items[0][0].shape
    if transposed and r % LANES != 0:
        rows = _adamw_group(name, [(w.T, g, m.T, v.T) for w, g, m, v in items], False)
        return [[a.T for a in item] for item in rows]
    tk = _pick(k, max(SUBLANES, ADAMW_STEP_WORDS // (r * len(items))), SUBLANES)
    n_out = 4 if transposed else 3

    def body(*refs):
        ins, outs = refs[:4 * len(items)], refs[4 * len(items):]
        for i in range(len(items)):
            wv, gv, mv, vv = (a[...] for a in ins[4 * i:4 * i + 4])
            if transposed:
                gv = gv.T
            res = _adamw_step(wv, gv, mv, vv) + ((gv,) if transposed else ())
            for o_ref, val in zip(outs[n_out * i:n_out * (i + 1)], res):
                o_ref[...] = val

    blk = pl.BlockSpec((tk, r), lambda j: (j, 0))
    g_blk = pl.BlockSpec((r, tk), lambda j: (0, j)) if transposed else blk
    res = pl.pallas_call(
        body, name=name, grid=(k // tk,), in_specs=[blk, g_blk, blk, blk] * len(items),
        out_specs=[blk] * (n_out * len(items)), out_shape=[pltpu.HBM((k, r), F32)] * (n_out * len(items)),
        compiler_params=_params(("parallel",)),
    )(*[pltpu.with_memory_space_constraint(a, pltpu.HBM) for item in items for a in item])
    return [list(res[n_out * i:n_out * (i + 1)]) + ([] if transposed else [items[i][1]]) for i in range(len(items))]


def _allgather(name, arrs):
    n = len(arrs)

    def body(*refs):
        ins, outs = refs[:n], refs[n:2 * n]
        send_sems, recv_sems, local_sems = refs[2 * n:]
        x, y, c = lax.axis_index("x"), lax.axis_index("y"), lax.axis_index("c")
        me, sibling = (x, y, c), (x, y, 1 - c)
        chips = [(1 - x, y), (x, 1 - y), (1 - x, 1 - y)]

        def rows(a, px, py, pc):
            r = ins[a].shape[0]
            return outs[a].at[pl.ds((4 * px + 2 * py + pc) * r, r), :]

        def copy(a, k, block, to, src=None):
            return pltpu.make_async_remote_copy(
                src_ref=rows(a, *block) if src is None else src, dst_ref=rows(a, *block),
                send_sem=send_sems.at[a, k], recv_sem=recv_sems.at[a, k], device_id=to, device_id_type=MESH)

        mine = [pltpu.make_async_copy(ins[a], rows(a, *me), local_sems.at[a]) for a in range(n)]
        for cp in mine:
            cp.start()
        first = []
        for a in range(n):
            first.append(copy(a, 0, me, sibling, src=ins[a]))
            first += [copy(a, 1 + j, me, (*chip, c), src=ins[a]) for j, chip in enumerate(chips)]
        for cp in first:
            cp.start()
        passed = []
        for j, chip in enumerate(chips):
            for a in range(n):
                copy(a, 1 + j, (*chip, c), me).wait_recv()
                cp = copy(a, 4 + j, (*chip, c), sibling)
                cp.start()
                passed.append(cp)
        for a in range(n):
            copy(a, 0, sibling, me).wait_recv()
            for j, chip in enumerate(chips):
                copy(a, 4 + j, (*chip, 1 - c), me).wait_recv()
        for cp in first + passed:
            cp.wait_send()
        for cp in mine:
            cp.wait()

    return pl.pallas_call(
        body, name=name, in_specs=[ANY] * n, out_specs=[ANY] * n,
        out_shape=[_out(N_DEV * a.shape[0], a.shape[1], a.dtype) for a in arrs],
        scratch_shapes=[pltpu.SemaphoreType.DMA((n, 7)), pltpu.SemaphoreType.DMA((n, 7)), pltpu.SemaphoreType.DMA((n,))],
    )(*arrs)


def _cores_start(name, blocks):
    n = len(blocks)
    c = blocks[0].shape[2]
    r = sum(b.shape[1] for b in blocks)

    def build(src_refs, land_refs, send_sems, recv_sems):
        x, y, cc = lax.axis_index("x"), lax.axis_index("y"), lax.axis_index("c")
        remote, off = [], 0
        for a, src in enumerate(src_refs):
            rows = pl.ds(off, src.shape[1])
            off += src.shape[1]
            for q in range(4):
                remote.append(pltpu.make_async_remote_copy(
                    src_ref=src.at[2 * q + (1 - cc)], dst_ref=land_refs[0].at[q, rows], send_sem=send_sems.at[4 * a + q],
                    recv_sem=recv_sems.at[4 * a + q], device_id=(x, y, 1 - cc), device_id_type=MESH))
        return remote, []

    return _split_start(name, [(blocks, [jax.ShapeDtypeStruct((4, r, c), blocks[0].dtype)], 4 * n, 0, build)])[0]


def _peer(k, x, y, c):
    return (1 - x if k & 4 else x, 1 - y if k & 2 else y, 1 - c if k & 1 else c)


def _split_start(name, groups, after=None):
    pins = [] if after is None else [after]
    bufs, sem_shapes, spans = [], [], []
    for srcs, land_shapes, n_remote, n_local, _ in groups:
        sems = [pltpu.SemaphoreType.DMA((n_remote,)), pltpu.SemaphoreType.DMA((n_remote,))]
        sems += [pltpu.SemaphoreType.DMA((n_local,))] if n_local else []
        spans.append((len(bufs), len(srcs), len(land_shapes), len(sem_shapes), len(sems)))
        bufs += [pltpu.with_memory_space_constraint(a, pltpu.HBM) for a in srcs]
        bufs += [pltpu.with_memory_space_constraint(lax.empty(s.shape, s.dtype), pltpu.HBM) for s in land_shapes]
        sem_shapes += sems
    n_buf, n_sem = len(bufs), len(sem_shapes)

    def body(*refs):
        buf_refs, sem_refs, token = refs[:n_buf], refs[n_buf + len(pins):n_buf + len(pins) + n_sem], refs[-1]
        for (b0, ns, nl, s0, k), group in zip(spans, groups):
            remote, local = group[4](buf_refs[b0:b0 + ns], buf_refs[b0 + ns:b0 + ns + nl], *sem_refs[s0:s0 + k])
            for cp in local + remote:
                cp.start()
        token[...] = jnp.zeros_like(token)

    outs = pl.pallas_call(
        body, name=name,
        out_shape=sem_shapes + [pltpu.HBM(b.shape, b.dtype) for b in bufs] + [jax.ShapeDtypeStruct((SUBLANES, LANES), F32)],
        in_specs=[HBM] * n_buf + [ANY] * len(pins),
        out_specs=[SEM] * n_sem + [HBM] * n_buf + [pl.BlockSpec(memory_space=pltpu.VMEM)],
        input_output_aliases={i: n_sem + i for i in range(n_buf)},
        compiler_params=pltpu.CompilerParams(has_side_effects=SIDE_EFFECT),
    )(*bufs, *pins)
    return [dict(sems=list(outs[s0:s0 + k]), bufs=list(outs[n_sem + b0:n_sem + b0 + ns + nl]), token=outs[-1],
                 build=group[4], ns=ns) for (b0, ns, nl, s0, k), group in zip(spans, groups)]


def _split_wait(name, started, after):
    ns, n_buf, n_sem = started["ns"], len(started["bufs"]), len(started["sems"])

    def body(*refs):
        src_refs, land_refs = refs[:ns], refs[ns:n_buf]
        sems = refs[n_buf:n_buf + n_sem]
        remote, local = started["build"](src_refs, land_refs, *sems)
        for cp in local:
            cp.wait()
        for cp in remote:
            cp.wait_send()
            cp.wait_recv()

    outs = pl.pallas_call(
        body, name=name, out_shape=[pltpu.HBM(b.shape, b.dtype) for b in started["bufs"]],
        in_specs=[HBM] * n_buf + [SEM] * n_sem + [ANY], out_specs=[HBM] * n_buf,
        input_output_aliases={i: i for i in range(n_buf)},
        compiler_params=pltpu.CompilerParams(has_side_effects=SIDE_EFFECT),
    )(*started["bufs"], *started["sems"], after)
    return list(outs[:ns]), list(outs[ns:])


def _gather_group(shards):
    m = len(shards)

    def build(src_refs, land_refs, send_sems, recv_sems, local_sems):
        x, y, c = lax.axis_index("x"), lax.axis_index("y"), lax.axis_index("c")
        remote, local = [], []
        for j in range(m):
            r = src_refs[j].shape[0]
            dst = land_refs[j].at[pl.ds((4 * x + 2 * y + c) * r, r), :]
            local.append(pltpu.make_async_copy(src_refs[j], dst, local_sems.at[j]))
            for k in range(1, N_DEV):
                remote.append(pltpu.make_async_remote_copy(
                    src_ref=src_refs[j], dst_ref=dst, send_sem=send_sems.at[7 * j + k - 1],
                    recv_sem=recv_sems.at[7 * j + k - 1], device_id=_peer(k, x, y, c), device_id_type=MESH))
        return remote, local

    lands = [jax.ShapeDtypeStruct((N_DEV * a.shape[0], a.shape[1]), a.dtype) for a in shards]
    return shards, lands, 7 * m, m, build


def _slots_start(name, a):
    def build(src_refs, land_refs, send_sems, recv_sems, local_sems):
        x, y, c = lax.axis_index("x"), lax.axis_index("y"), lax.axis_index("c")
        dst = land_refs[0].at[4 * x + 2 * y + c]
        local = [pltpu.make_async_copy(src_refs[0], dst, local_sems.at[0])]
        remote = [pltpu.make_async_remote_copy(
            src_ref=src_refs[0], dst_ref=dst, send_sem=send_sems.at[k - 1], recv_sem=recv_sems.at[k - 1],
            device_id=_peer(k, x, y, c), device_id_type=MESH) for k in range(1, N_DEV)]
        return remote, local

    return _split_start(name, [([a], [jax.ShapeDtypeStruct((N_DEV,) + a.shape, a.dtype)], 7, 1, build)])[0]


def _chips_start(name, p):
    _, r, c = p.shape
    nck = r // GRAD_ROW_TILE

    def build(src_refs, land_refs, send_sems, recv_sems):
        x, y, cc = lax.axis_index("x"), lax.axis_index("y"), lax.axis_index("c")
        remote = []
        for k in range(1, 4):
            px = 1 - x if k >> 1 else x
            py = 1 - y if k & 1 else y
            for j in range(nck):
                rows = pl.ds(j * GRAD_ROW_TILE, GRAD_ROW_TILE)
                remote.append(pltpu.make_async_remote_copy(
                    src_ref=src_refs[0].at[2 * px + py, rows], dst_ref=land_refs[0].at[k - 1, rows],
                    send_sem=send_sems.at[(k - 1) * nck + j], recv_sem=recv_sems.at[(k - 1) * nck + j],
                    device_id=(px, py, cc), device_id_type=MESH))
        return remote, []

    return _split_start(name, [([p], [jax.ShapeDtypeStruct((3, r, c), p.dtype)], 3 * nck, 0, build)])[0]


def _chip_sum(name, p, recv, chip):
    _, r, c = p.shape
    tr = _pick(r, 5 * GRAD_ROW_TILE, GRAD_ROW_TILE)

    def body(chip_ref, p_ref, r_ref, o_ref):
        acc = p_ref[...].astype(F32)
        for k in range(3):
            acc = acc + r_ref[k].astype(F32)
        o_ref[...] = acc

    return pl.pallas_call(
        body, name=name,
        grid_spec=pltpu.PrefetchScalarGridSpec(
            num_scalar_prefetch=1, grid=(r // tr,),
            in_specs=[pl.BlockSpec((None, tr, c), lambda i, chip_ref: (chip_ref[0], i, 0)),
                      pl.BlockSpec((3, tr, c), lambda i, chip_ref: (0, i, 0))],
            out_specs=pl.BlockSpec((tr, c), lambda i, chip_ref: (i, 0))),
        out_shape=_out(r, c, F32), compiler_params=_params(("parallel",)),
    )(chip, p, recv)


def _local_step(x, mem, tgt, wt, sm, ev=None):
    t, d = x.shape
    n_mem = mem.shape[0]
    d_pool = sm["pool_scale"].shape[1]
    ng, pc = sm["pool_w"].shape[0], sm["pool_w"].shape[1]
    d_ssm = sm["ssm_d"].shape[1]
    _, sg, sp, sh = sm["ssm_b_re"].shape
    n_state = sg * sp
    gb, gs = {}, {}

    def emit(name, **kw):
        return ev(name, **kw) if ev is not None else None

    n1 = _rms_fwd("ffn1_norm", x, sm["ffn1_norm"])
    emit("ffn1_norm_done", marker=n1)
    def ffn1_down(hid):
        emit("ffn1_up_done", marker=hid)
        return wt["ffn1_w_down"]

    h1, ffn1_saved = _ffn_fwd("ffn1", x, n1, wt["ffn1_w_gate"], wt["ffn1_w_up"], ffn1_down)
    emit("ffn1_fwd_done", marker=h1)
    u = _rms_fwd("mix_norm", h1, sm["mix_norm"])
    d_in = wt["w_in"].shape[0]
    tm, tn = _pick(t, 1024), _pick(d_in, 1408)
    proj = _mm1("in_proj", "nt", u, wt["w_in"], t, d_in, tm, tn, F32)
    off_s = d_pool // d_ssm
    off_gp = (d_pool + d_ssm)
    off_gs = off_gp + d

    pool_w_bf = sm["pool_w"].astype(BF16)
    pooled, pm = _pool_fwd(proj, pool_w_bf, sm["pool_scale"])

    by_p = lambda a: jnp.swapaxes(a, -1, -2).reshape(2 * sg, sh, sp)
    disc_args = [sm["ssm_a_re"].reshape(2 * sg, 1, sp), sm["ssm_a_im"].reshape(2 * sg, 1, sp),
                 sm["ssm_log_dt"].reshape(2 * sg, 1, 1), by_p(sm["ssm_b_re"]), by_p(sm["ssm_b_im"])]
    abr, abi, bbr, bbi = _ssm_disc(disc_args)
    abr2, abi2 = abr.reshape(2, n_state), abi.reshape(2, n_state)
    per_dir = lambda a: [_bd(a.reshape(2, sg, sh, sp)[dr]).astype(BF16) for dr in range(2)]
    b_re, b_im, c_re, c_im = per_dir(bbr), per_dir(bbi), per_dir(sm["ssm_c_re"]), per_dir(-sm["ssm_c_im"])
    sp32 = _to_segments(proj[:, d_pool:d_pool + d_ssm])
    xs, y_parts = [], []
    for dr in range(2):
        xr, xi, y_part = _ssm_fwd(f"ssm_fwd{dr}", sp32, b_re[dr], b_im[dr], c_re[dr], c_im[dr], abr2[dr:dr + 1],
                                  abi2[dr:dr + 1], reverse=(dr == 1))
        xs.append((xr, xi))
        y_parts.append(y_part)
    y, ys = _ssm_finish(y_parts[0], y_parts[1], sp32, sm["ssm_d"])
    tmy = _pick(t, 256)
    emit("mix_in_done", marker=ys)

    tmm, tnm, tnx = _pick(t, 1024), _pick(d, 256), _pick(d, 512)
    gp_spec = _tile(tmm, tnm, off_gp // tnm)
    gs_spec = _tile(tmm, tnm, off_gs // tnm)

    def merge_epi(accs, gpv, gsv):
        z_pool, val, gate = accs
        return (jax.nn.sigmoid(gpv) * z_pool + jax.nn.sigmoid(gsv) * (val * jax.nn.sigmoid(gate)),)

    merged = _mm("mix_merge", "nt", [pm, ys], [wt["w_pool_proj"], wt["w_glu_val"], wt["w_glu_gate"]],
                 [[(0, 0)], [(1, 1)], [(1, 2)]], t, d, tmm, tnm, [(proj, gp_spec), (proj, gs_spec)], merge_epi,
                 [(_out(t, d, BF16), None)])[0]
    res_epi = lambda accs, hin: (hin + accs[0],)
    h2 = _mm("mix_out", "nn", [merged], [wt["w_mix_out"]], [[(0, 0)]], t, d, tmm, tnx, [(h1, _tile(tmm, tnx))],
             res_epi, [(_out(t, d, F32), None)])[0]

    un = _rms_fwd("xattn_norm", h2, sm["xattn_norm"])
    mn = _rms_fwd("mem_norm", mem, sm["mem_norm"])
    emit("mix_done", marker=un)
    q = _mm1("xattn_q", "nn", un, wt["w_q"], t, d, tmm, tnx, BF16)
    kv = _mm1("xattn_kv", "nt", mn, wt["w_kv"], n_mem, 2 * d, n_mem, _pick(2 * d, 512), BF16)
    o = _attn_fwd(q, kv)
    h3 = _mm("xattn_out", "nn", [o], [wt["w_xo"]], [[(0, 0)]], t, d, tmm, tnx, [(h2, _tile(tmm, tnx))],
             res_epi, [(_out(t, d, F32), None)])[0]

    n2 = _rms_fwd("ffn2_norm", h3, sm["ffn2_norm"])
    emit("xattn_done", marker=n2)
    h4, ffn2_saved = _ffn_fwd("ffn2", h3, n2, wt["ffn2_w_gate"], wt["ffn2_w_up"], wt["ffn2_w_down"])

    dh4, dh4_bf, gs["final_norm"], loss = _loss_head(h4, sm["final_norm"], tgt)
    dh3, dh3_bf, gs["ffn2_norm"], gb["ffn2_w_gate"], gb["ffn2_w_up"], gb["ffn2_w_down"] = _ffn_bwd(
        "ffn2", h3, sm["ffn2_norm"], wt["ffn2_w_gate"], wt["ffn2_w_up"], wt["ffn2_w_down"], ffn2_saved, dh4, dh4_bf)

    tw = _pick(d, 1024)
    do = _mm1("xattn_do", "nt", dh3_bf, wt["w_xo"], t, d, tmm, tnx, BF16)
    gb["w_xo"] = _mm1("xattn_dwxo", "tn", o, dh3_bf, d, d, tw, tnx, BF16)
    dq, dkv = _attn_bwd(q, kv, do)
    gb["w_q"] = _mm1("xattn_dwq", "tn", un, dq, d, d, tw, tnx, BF16)
    dun = _mm1("xattn_dun", "nt", dq, wt["w_q"], t, d, tmm, tnx, F32)
    dh2, dh2_bf, gs["xattn_norm"] = _rms_bwd("xattn_norm_bwd", h2, sm["xattn_norm"], dun, dh3)
    gb["w_kv"] = _mm1("xattn_dwkv", "tn", dkv, mn, 2 * d, d, _pick(2 * d, 512), d, BF16)
    dmn = _mm1("xattn_dmn", "nn", dkv, wt["w_kv"], n_mem, d, n_mem, tnx, F32)
    gs["mem_norm"] = _rms_bwd("mem_norm_bwd", mem, sm["mem_norm"], dmn)

    gb["w_mix_out"] = _mm1("mix_dwout", "tn", merged, dh2_bf, d, d, tw, tnx, BF16)

    def merge_bwd_epi(accs, gpv, gsv):
        dmerged, z_pool, val, gate = accs
        sp_, ss_, sg_ = jax.nn.sigmoid(gpv), jax.nn.sigmoid(gsv), jax.nn.sigmoid(gate)
        glu = val * sg_
        dz_pool = dmerged * sp_
        dg_pool = dmerged * z_pool * (sp_ * (1.0 - sp_))
        dz_ssm = dmerged * ss_
        dg_ssm = dmerged * glu * (ss_ * (1.0 - ss_))
        dval = dz_ssm * sg_
        dgate = dz_ssm * glu * (1.0 - sg_)
        return dz_pool, dg_pool, dg_ssm, dval, dgate

    dz_pool, dg_pool, dg_ssm, dval, dgate = _mm(
        "mix_merge_bwd", "nt", [dh2_bf, pm, ys], [wt["w_mix_out"], wt["w_pool_proj"], wt["w_glu_val"], wt["w_glu_gate"]],
        [[(0, 0)], [(1, 1)], [(2, 2)], [(2, 3)]], t, d, tmm, tnm, [(proj, gp_spec), (proj, gs_spec)], merge_bwd_epi,
        [(_out(t, d, BF16), None)] * 5)
    gb["w_pool_proj"] = _mm1("pool_dwproj", "tn", dz_pool, pm, d, d_pool, tw, d_pool, BF16)
    gb["w_glu_val"] = _mm1("glu_dwval", "tn", dval, ys, d, d_ssm, tw, d_ssm, BF16)
    gb["w_glu_gate"] = _mm1("glu_dwgate", "tn", dgate, ys, d, d_ssm, tw, d_ssm, BF16)

    def gelu_bwd_epi(accs, yv):
        _, vjp = jax.vjp(jax.nn.gelu, yv)
        return (vjp(accs[0])[0],)

    dy = _mm("glu_dy", "nn", [dval, dgate], [wt["w_glu_val"], wt["w_glu_gate"]], [[(0, 0), (1, 1)]], t, d_ssm, tmy, d_ssm,
             [(y, _tile(tmy, d_ssm))], gelu_bwd_epi, [(_out(t, d_ssm, F32), None)])[0]
    gs["ssm_d"] = _colsum_prod("ssm_dd", dy, proj, b_coff=off_s)
    dyp = _to_segments(dy)
    d_abr, d_abi, d_bbr, d_bbi, d_cre, d_cim, lams = [], [], [], [], [], [], []
    ts = _pick(n_state, 512)
    for dr in range(2):
        lr, li, dar, dai = _ssm_bwd(f"ssm_bwd{dr}", dyp, c_re[dr], c_im[dr], xs[dr][0], xs[dr][1], abr2[dr:dr + 1],
                                    abi2[dr:dr + 1], reverse=(dr == 1))
        d_abr.append(dar)
        d_abi.append(dai)
        lams += [lr, li]
        maps = _mm(f"ssm_dmaps{dr}", "tn", [sp32, dyp], [lr, li, xs[dr][0], xs[dr][1]],
                   [[(0, 0)], [(0, 1)], [(1, 2)], [(1, 3)]], d_ssm, n_state, d_ssm, ts, [], lambda accs: tuple(accs),
                   [(_out(d_ssm, n_state, F32), None)] * 4)
        for acc, m in zip((d_bbr, d_bbi, d_cre, d_cim), maps):
            acc.append(_diag(m, sg, sh, sp))
    ds = _from_segments(_mm(
        "ssm_ds", "nt", lams, [b_re[0], b_im[0], b_re[1], b_im[1]], [[(k, k) for k in range(4)]], t, d_ssm, tmy,
        d_ssm, [(dyp, _tile(tmy, d_ssm)), (sm["ssm_d"], _rowvec(d_ssm))],
        lambda accs, dyv, dv: (dyv * dv + accs[0],), [(_out(t, d_ssm, BF16), None)])[0])
    cots = [jnp.concatenate(d_abr, axis=0).reshape(2 * sg, 1, sp), jnp.concatenate(d_abi, axis=0).reshape(2 * sg, 1, sp),
            jnp.concatenate(d_bbr, axis=0), jnp.concatenate(d_bbi, axis=0)]
    d_are, d_aim, d_ldt, d_bre, d_bim = _ssm_disc_bwd(disc_args, cots)
    gs["ssm_a_re"] = d_are.reshape(2, sg, sp)
    gs["ssm_a_im"] = d_aim.reshape(2, sg, sp)
    gs["ssm_log_dt"] = d_ldt.reshape(2, sg)
    from_p = lambda a: jnp.swapaxes(a.reshape(2, sg, sh, sp), -1, -2)
    gs["ssm_b_re"], gs["ssm_b_im"] = from_p(d_bre), from_p(d_bim)
    gs["ssm_c_re"] = jnp.stack(d_cre, axis=0)
    gs["ssm_c_im"] = -jnp.stack(d_cim, axis=0)

    dpm = _mm1("pool_dpm", "nn", dz_pool, wt["w_pool_proj"], t, d_pool, tmm, _pick(d_pool, 256), F32)
    dp, gs["pool_w"], gs["pool_scale"] = _pool_bwd(pooled, dpm, pool_w_bf, sm["pool_scale"])

    w_in = wt["w_in"]
    parts = [(dp, 0, d_pool), (ds, d_pool, d_ssm), (dg_pool, off_gp, d), (dg_ssm, off_gs, d)]
    w_in_parts = [w_in[o0:o0 + width] for _, o0, width in parts]
    gb["w_in"] = jnp.concatenate(
        [_mm1(f"in_proj_dw{k}", "tn", p_[0], u, p_[2], d, _pick(p_[2], 1024), tnx, BF16) for k, p_ in enumerate(parts)], axis=0)
    pin = emit("grads_main", gb=gb)
    du = _mm("in_proj_du", "nn", [p_[0] for p_ in parts], w_in_parts, [[(k, k) for k in range(4)]], t, d, tmm, tnx, [],
             lambda accs: (accs[0],), [(_out(t, d, F32), None)], after=pin)[0]
    dh1, dh1_bf, gs["mix_norm"] = _rms_bwd("mix_norm_bwd", h1, sm["mix_norm"], du, dh2)
    pin = emit("small_early", gs=gs, loss=loss)

    def ffn1_weights_done(d_wg, d_wu, d_wd):
        gb["ffn1_w_gate"], gb["ffn1_w_up"], gb["ffn1_w_down"] = d_wg, d_wu, d_wd
        return emit("grads_ffn1", gb=gb)

    dx, _, gs["ffn1_norm"], _, _, _ = _ffn_bwd(
        "ffn1", x, sm["ffn1_norm"], wt["ffn1_w_gate"], wt["ffn1_w_up"], wt["ffn1_w_down"], ffn1_saved, dh1, dh1_bf,
        weights_done=ffn1_weights_done, after=pin)
    return loss, dx, gb, gs


WEIGHTS = ["ffn1_norm", "ffn1_w_gate", "ffn1_w_up", "ffn1_w_down", "mix_norm", "w_in", "pool_w", "pool_scale",
           "w_pool_proj", "ssm_a_re", "ssm_a_im", "ssm_log_dt", "ssm_b_re", "ssm_b_im", "ssm_c_re", "ssm_c_im", "ssm_d",
           "w_glu_val", "w_glu_gate", "w_mix_out", "xattn_norm", "mem_norm", "w_q", "w_kv", "w_xo", "ffn2_norm",
           "ffn2_w_gate", "ffn2_w_up", "ffn2_w_down", "final_norm"]
COL_SHARDED = ["ffn1_w_gate", "ffn1_w_up", "w_in", "w_pool_proj", "w_glu_val", "w_glu_gate", "w_kv", "ffn2_w_gate",
               "ffn2_w_up"]
ROW_SHARDED = ["ffn1_w_down", "w_mix_out", "w_q", "w_xo", "ffn2_w_down"]
BIG = [n for n in WEIGHTS if n in COL_SHARDED or n in ROW_SHARDED]
SMALL = [n for n in WEIGHTS if n not in BIG]
FFN1_BIG = ["ffn1_w_gate", "ffn1_w_up", "ffn1_w_down"]
MAIN_BIG = [n for n in BIG if n not in FFN1_BIG]
GATHER_PLAN = [("ffn1_up_done", ["ffn1_w_down"]), ("ffn1_fwd_done", ["w_in"]),
               ("mix_in_done", ["w_pool_proj", "w_glu_val", "w_glu_gate", "w_mix_out"]),
               ("mix_done", ["w_q", "w_kv", "w_xo"]), ("xattn_done", ["ffn2_w_gate", "ffn2_w_up", "ffn2_w_down"])]
MINOR_SWAPPED = ["ssm_b_re", "ssm_b_im"]
LATE_SMALL = "ffn1_norm"
EARLY_SMALL = [n for n in SMALL if n != LATE_SMALL]
PACK_ROWS = SUBLANES * LANES
GRAD_ROW_TILE = 256
ADAMW_STEP_WORDS = 1 << 19


def _to_rows(name, w, width):
    if name in COL_SHARDED:
        w = w.T
    return w.reshape(-1, width)


def _pack_small(vals):
    flat = []
    for v in vals:
        f = v.reshape(-1)
        flat.append(jnp.pad(f, (0, (-f.shape[0]) % PACK_ROWS)))
    total = sum(f.shape[0] for f in flat)
    flat.append(jnp.zeros(((-total) % (GRAD_ROW_TILE * LANES),), F32))
    return jnp.concatenate(flat).reshape(-1, LANES)


def _unpack_small(packed, shapes):
    out, row = [], 0
    for shp in shapes:
        size = math.prod(shp)
        rows = -(-size // PACK_ROWS) * SUBLANES
        out.append(packed[row:row + rows].reshape(-1)[:size].reshape(shp))
        row += rows
    return out


def kernel(x, mem, ffn1_norm, ffn1_w_gate, ffn1_w_up, ffn1_w_down, mix_norm, w_in, pool_w, pool_scale, w_pool_proj, ssm_a_re, ssm_a_im, ssm_log_dt, ssm_b_re, ssm_b_im, ssm_c_re, ssm_c_im, ssm_d, w_glu_val, w_glu_gate, w_mix_out, xattn_norm, mem_norm, w_q, w_kv, w_xo, ffn2_norm, ffn2_w_gate, ffn2_w_up, ffn2_w_down, final_norm, loss_target, m_ffn1_norm, m_ffn1_w_gate, m_ffn1_w_up, m_ffn1_w_down, m_mix_norm, m_w_in, m_pool_w, m_pool_scale, m_w_pool_proj, m_ssm_a_re, m_ssm_a_im, m_ssm_log_dt, m_ssm_b_re, m_ssm_b_im, m_ssm_c_re, m_ssm_c_im, m_ssm_d, m_w_glu_val, m_w_glu_gate, m_w_mix_out, m_xattn_norm, m_mem_norm, m_w_q, m_w_kv, m_w_xo, m_ffn2_norm, m_ffn2_w_gate, m_ffn2_w_up, m_ffn2_w_down, m_final_norm, v_ffn1_norm, v_ffn1_w_gate, v_ffn1_w_up, v_ffn1_w_down, v_mix_norm, v_w_in, v_pool_w, v_pool_scale, v_w_pool_proj, v_ssm_a_re, v_ssm_a_im, v_ssm_log_dt, v_ssm_b_re, v_ssm_b_im, v_ssm_c_re, v_ssm_c_im, v_ssm_d, v_w_glu_val, v_w_glu_gate, v_w_mix_out, v_xattn_norm, v_mem_norm, v_w_q, v_w_kv, v_w_xo, v_ffn2_norm, v_ffn2_w_gate, v_ffn2_w_up, v_ffn2_w_down, v_final_norm):
    given = dict(locals())
    wts = {n: given[n] for n in WEIGHTS}
    moms = {n: (given["m_" + n], given["v_" + n]) for n in WEIGHTS}
    x2, mem2, tgt2 = x[0], mem[0], loss_target[0]
    d = x2.shape[1]
    chip = (2 * lax.axis_index("x") + lax.axis_index("y")).astype(jnp.int32).reshape(1)

    def full_form(n, f):
        shard = wts[n][0].shape
        return f.reshape(N_DEV * shard[1], shard[0]) if n in COL_SHARDED else f.reshape(N_DEV * shard[0], shard[1])

    shards = {n: _to_rows(n, wts[n][0], d).astype(BF16) for n in BIG}
    first = FFN1_BIG[:2]
    wt = {n: full_form(n, f) for n, f in zip(first, _allgather("weight_allgather_first", [shards[n] for n in first]))}
    started = _split_start("weight_gather_start", [_gather_group([shards[n] for n in names]) for _, names in GATHER_PLAN],
                           after=wt[first[0]])
    gathers = {event: (names, st) for (event, names), st in zip(GATHER_PLAN, started)}
    sm = {n: (wts[n].reshape(1, -1) if wts[n].ndim <= 2 else wts[n][0]) for n in SMALL}
    sm["ffn1_norm"] = sm["ffn1_norm"] + started[0]["token"][0, 0]

    pending = {}

    def reduce_start(tag, names, gb):
        blocks = [gb[n].reshape(N_DEV, -1, d) for n in names]
        pad_rows = (-sum(b.shape[1] for b in blocks)) % GRAD_ROW_TILE
        pad = [jnp.zeros((N_DEV, pad_rows, d), BF16)] if pad_rows else []
        started = _cores_start("grad_exchange_cores_start_" + tag, blocks + pad)
        own = jnp.concatenate([lax.dynamic_index_in_dim(b.reshape(4, 2, b.shape[1], d), lax.axis_index("c"), 1, False)
                               for b in started["bufs"][:len(blocks + pad)]], axis=1)
        _, (recv,) = _split_wait("grad_exchange_cores_wait_" + tag, started, own)
        rows_all = own.shape[1]
        pair = _ew("grad_pair_sum_" + tag, lambda a, b: (a.astype(F32) + b.astype(F32),),
                   [own.reshape(-1, d), recv.reshape(-1, d)], [BF16], rows_pref=5 * GRAD_ROW_TILE)[0]
        pair = pair.reshape(4, rows_all, d)
        pending[tag] = (pair, _chips_start("grad_exchange_chips_start_" + tag, pair), [b.shape[1] for b in blocks])
        return pending[tag][1]["token"]

    def reduce_finish(tag, after):
        _, started, rows = pending[tag]
        (pair,), (recv,) = _split_wait("grad_exchange_chips_wait_" + tag, started, after)
        return _chip_sum("grad_chip_sum_" + tag, pair, recv, chip), rows

    def ev(name, gb=None, gs=None, loss=None, marker=None):
        if name in gathers:
            names, started = gathers[name]
            for n, f in zip(names, _split_wait("weight_gather_wait_" + name, started, marker)[1]):
                wt[n] = full_form(n, f)
        elif name == "grads_main":
            return reduce_start("main", MAIN_BIG, gb)
        elif name == "small_early":
            pending["small"] = _slots_start("small_gather_start", _pack_small([gs[n] for n in EARLY_SMALL] + [loss[:, :1]]))
            return pending["small"]["token"]
        elif name == "grads_ffn1":
            return reduce_start("ffn1", FFN1_BIG, gb)
        return None

    _, dx, _, gs = _local_step(x2, mem2, tgt2, wt, sm, ev)

    grads = {}
    for tag, names in (("main", MAIN_BIG), ("ffn1", FFN1_BIG)):
        g_rows, rows = reduce_finish(tag, dx)
        off = 0
        for n, r in zip(names, rows):
            shard = wts[n].shape
            grads[n] = g_rows[off:off + r].reshape((shard[2], shard[1]) if n in COL_SHARDED else shard[1:])
            off += r
    small_sum = _sum_slots("small_sum", _split_wait("small_gather_wait", pending["small"], dx)[1][0], F32)
    late = _allgather("small_allgather_late", [gs[LATE_SMALL].reshape(-1, LANES)])[0]
    late_sum = _sum_slots("small_sum_late", late.reshape(N_DEV, -1, LANES), F32)
    vals = _unpack_small(small_sum, [wts[n].shape for n in EARLY_SMALL] + [(1, 1)])
    total_loss = vals[-1].reshape(())
    def flat(n, a):
        a = a.reshape(wts[n].shape)
        a = jnp.swapaxes(a, -1, -2) if n in MINOR_SWAPPED else a
        return a.reshape(-1, a.shape[-1])

    def unflat(n, a):
        shape = wts[n].shape
        if n in MINOR_SWAPPED:
            return jnp.swapaxes(a.reshape(shape[:-2] + (shape[-1], shape[-2])), -1, -2)
        return a.reshape(shape)

    for n, g_full in zip(EARLY_SMALL + [LATE_SMALL], vals[:-1] + [late_sum]):
        grads[n] = flat(n, g_full)

    out_g, out_d, out_m, out_v = {}, {}, {}, {}
    by_shape = {}
    for n in WEIGHTS:
        by_shape.setdefault((flat(n, wts[n]).shape, n in COL_SHARDED), []).append(n)
    for (_, transposed), names in by_shape.items():
        items = [(flat(n, wts[n]), grads[n], flat(n, moms[n][0]), flat(n, moms[n][1])) for n in names]
        for n, res in zip(names, _adamw_group("adamw_" + names[0], items, transposed)):
            out_d[n], out_m[n], out_v[n], out_g[n] = (unflat(n, a) for a in res)

    return (total_loss, dx[None], *[out_g[n] for n in WEIGHTS], *[out_d[n] for n in WEIGHTS],
            *[out_m[n] for n in WEIGHTS], *[out_v[n] for n in WEIGHTS])
```

```python
import functools
import math

import jax
import jax.numpy as jnp
from jax import lax
from jax.experimental import pallas as pl
from jax.experimental.pallas import tpu as pltpu

F32 = jnp.float32
BF16 = jnp.bfloat16
EPS = 1e-6
N_XHEADS = 4
POOL_WINDOWS = (2, 4, 8, 16)
ADAM_LR = 0.001
ADAM_B1 = 0.9
ADAM_B2 = 0.999
ADAM_EPS = 1e-08
ADAM_WD = 0.01
ADAM_STEP = 10
N_DEV = 8
VMEM_LIMIT_V7X = 48 * 1024 * 1024
LANES = 128
SUBLANES = 8
SUB_ROWS = 256
POOL_PAD = 16
MESH = pl.DeviceIdType.MESH
ANY = pl.BlockSpec(memory_space=pl.ANY)
HBM = pl.BlockSpec(memory_space=pltpu.HBM)
SEM = pl.BlockSpec(memory_space=pltpu.SEMAPHORE)
SIDE_EFFECT = pltpu.SideEffectType.DATAFLOW_SIDE_EFFECTING

_DIMS = {
    "nt": (((1,), (1,)), ((), ())),
    "nn": (((1,), (0,)), ((), ())),
    "tn": (((0,), (0,)), ((), ())),
}


def _pick(dim, pref, mult=LANES):
    if dim <= pref:
        return dim
    for t in range(pref - pref % mult, 0, -mult):
        if dim % t == 0:
            return t
    return dim


def _params(sem):
    return pltpu.CompilerParams(dimension_semantics=sem, vmem_limit_bytes=VMEM_LIMIT_V7X)


def _tile(tm, tn, coff=0):
    return pl.BlockSpec((tm, tn), lambda i, j: (i, j + coff))


def _rowvec(tn, coff=0):
    return pl.BlockSpec((1, tn), lambda i, j: (0, j + coff))


def _out(m, n, dtype):
    return jax.ShapeDtypeStruct((m, n), dtype)


def _mm(name, form, a_list, b_list, groups, m, n, tm, tn, extras, epilogue, outs, after=None, sub=SUB_ROWS):
    na, nb, ne = len(a_list), len(b_list), len(extras)
    pins = [] if after is None else [after]
    step = tm if (sub is None or form == "tn" or tm % sub) else sub

    def a_spec(a):
        if form == "tn":
            return pl.BlockSpec((a.shape[0], tm), lambda i, j: (0, i))
        return pl.BlockSpec((tm, a.shape[1]), lambda i, j: (i, 0))

    def b_spec(b):
        if form == "nt":
            return pl.BlockSpec((tn, b.shape[1]), lambda i, j: (j, 0))
        return pl.BlockSpec((b.shape[0], tn), lambda i, j: (0, j))

    def body(*refs):
        a_refs, b_refs = refs[:na], refs[na:na + nb]
        e_refs, o_refs = refs[na + nb:na + nb + ne], refs[na + nb + ne + len(pins):]
        b_vals = {}
        for s0 in range(0, tm, step):
            rows = slice(None) if step == tm else pl.ds(s0, step)
            a_vals, accs = {}, []
            for group in groups:
                acc = None
                for ai, bi in group:
                    if ai not in a_vals:
                        a_vals[ai] = (a_refs[ai][...] if form == "tn" else a_refs[ai][rows, :]).astype(BF16)
                    if bi not in b_vals:
                        b_vals[bi] = b_refs[bi][...].astype(BF16)
                    d = lax.dot_general(a_vals[ai], b_vals[bi], _DIMS[form], preferred_element_type=F32)
                    acc = d if acc is None else acc + d
                accs.append(acc)
            res = epilogue(accs, *[e[rows, :] if e.shape[0] == tm else e[...] for e in e_refs])
            for o_ref, r in zip(o_refs, res):
                o_ref[rows, :] = r.astype(o_ref.dtype)

    out_specs = [_tile(tm, tn) if s is None else s for _, s in outs]
    res = pl.pallas_call(
        body, name=name, grid=(m // tm, n // tn),
        in_specs=[a_spec(a) for a in a_list] + [b_spec(b) for b in b_list] + [s for _, s in extras] + [ANY] * len(pins),
        out_specs=out_specs, out_shape=[o for o, _ in outs],
        compiler_params=_params(("parallel", "parallel")),
    )(*a_list, *b_list, *[e for e, _ in extras], *pins)
    return res


def _mm1(name, form, a, b, m, n, tm, tn, dtype, scale=None):
    epi = (lambda accs: (accs[0],)) if scale is None else (lambda accs: (accs[0] * scale,))
    return _mm(name, form, [a], [b], [[(0, 0)]], m, n, tm, tn, [], epi, [(_out(m, n, dtype), None)])[0]


def _rms_fwd(name, h, g):
    t, d = h.shape
    tm = _pick(t, 512, SUBLANES)

    def body(h_ref, g_ref, n_ref):
        hv = h_ref[...]
        r = lax.rsqrt(jnp.mean(hv * hv, axis=-1, keepdims=True) + EPS)
        n_ref[...] = ((hv * r) * g_ref[...]).astype(BF16)

    return pl.pallas_call(
        body, name=name, grid=(t // tm,),
        in_specs=[pl.BlockSpec((tm, d), lambda i: (i, 0)), pl.BlockSpec((1, d), lambda i: (0, 0))],
        out_specs=pl.BlockSpec((tm, d), lambda i: (i, 0)), out_shape=_out(t, d, BF16),
        compiler_params=_params(("parallel",)),
    )(h, g)


def _rms_bwd(name, h, g, dn, dres=None):
    t, d = h.shape
    tm = _pick(t, 512, SUBLANES)
    need_dh = dres is not None

    def body(*refs):
        if need_dh:
            h_ref, g_ref, dn_ref, dres_ref, dh_ref, dhb_ref, dg_ref = refs
        else:
            h_ref, g_ref, dn_ref, dg_ref = refs
        hv = h_ref[...]
        r = lax.rsqrt(jnp.mean(hv * hv, axis=-1, keepdims=True) + EPS)
        nh = hv * r
        dnv = dn_ref[...].astype(F32)

        @pl.when(pl.program_id(0) == 0)
        def _():
            dg_ref[...] = jnp.zeros_like(dg_ref)

        dg_ref[...] += jnp.sum(dnv * nh, axis=0, keepdims=True)
        if need_dh:
            dng = dnv * g_ref[...]
            dh = dres_ref[...] + r * (dng - nh * jnp.mean(dng * nh, axis=-1, keepdims=True))
            dh_ref[...] = dh
            dhb_ref[...] = dh.astype(BF16)

    row = pl.BlockSpec((tm, d), lambda i: (i, 0))
    vec = pl.BlockSpec((1, d), lambda i: (0, 0))
    if need_dh:
        return pl.pallas_call(
            body, name=name, grid=(t // tm,), in_specs=[row, vec, row, row], out_specs=[row, row, vec],
            out_shape=[_out(t, d, F32), _out(t, d, BF16), _out(1, d, F32)], compiler_params=_params(("arbitrary",)),
        )(h, g, dn, dres)
    return pl.pallas_call(
        body, name=name, grid=(t // tm,), in_specs=[row, vec, row], out_specs=vec,
        out_shape=_out(1, d, F32), compiler_params=_params(("arbitrary",)),
    )(h, g, dn)


def _loss_head(h, g, tgt):
    t, d = h.shape
    tm = _pick(t, 512, SUBLANES)

    def body(h_ref, g_ref, t_ref, dh_ref, dhb_ref, dg_ref, loss_ref):
        hv = h_ref[...]
        r = lax.rsqrt(jnp.mean(hv * hv, axis=-1, keepdims=True) + EPS)
        nh = hv * r
        err = nh * g_ref[...] - t_ref[...]

        @pl.when(pl.program_id(0) == 0)
        def _():
            dg_ref[...] = jnp.zeros_like(dg_ref)
            loss_ref[...] = jnp.zeros_like(loss_ref)

        per_row = jnp.mean(err * err, axis=-1, keepdims=True)
        loss_ref[...] += 0.5 * jnp.sum(per_row, axis=0, keepdims=True)
        dy = err * (1.0 / d)
        dg_ref[...] += jnp.sum(dy * nh, axis=0, keepdims=True)
        dng = dy * g_ref[...]
        dh = r * (dng - nh * jnp.mean(dng * nh, axis=-1, keepdims=True))
        dh_ref[...] = dh
        dhb_ref[...] = dh.astype(BF16)

    row = pl.BlockSpec((tm, d), lambda i: (i, 0))
    vec = pl.BlockSpec((1, d), lambda i: (0, 0))
    return pl.pallas_call(
        body, name="loss_head", grid=(t // tm,), in_specs=[row, vec, row],
        out_specs=[row, row, vec, pl.BlockSpec((1, LANES), lambda i: (0, 0))],
        out_shape=[_out(t, d, F32), _out(t, d, BF16), _out(1, d, F32), _out(1, LANES, F32)],
        compiler_params=_params(("arbitrary",)),
    )(h, g, tgt)


def _ffn_fwd(tag, h, n, wg_t, wu_t, wd):
    t, d = h.shape
    f = wg_t.shape[0]
    tm, tn = _pick(t, 1024), _pick(f, 1408)

    def up_epi(accs):
        a, b = accs
        return a, b, (a * jax.nn.sigmoid(a)) * b

    a, b, hid = _mm(tag + "_up", "nt", [n], [wg_t, wu_t], [[(0, 0)], [(0, 1)]], t, f, tm, tn, [], up_epi,
                    [(_out(t, f, BF16), None)] * 3)
    if callable(wd):
        wd = wd(hid)
    tm2, tn2 = _pick(t, 1024), _pick(d, 512)
    h_out = _mm(tag + "_down", "nn", [hid], [wd], [[(0, 0)]], t, d, tm2, tn2, [(h, _tile(tm2, tn2))],
                lambda accs, hin: (hin + 0.5 * accs[0],), [(_out(t, d, F32), None)])[0]
    return h_out, (n, a, b, hid)


def _ffn_bwd(tag, h, g, wg_t, wu_t, wd, saved, dh, dh_bf, weights_done=None, after=None):
    n, a, b, hid = saved
    t, d = h.shape
    f = wd.shape[0]
    tm, tn = _pick(t, 1024), _pick(f, 1408)

    def hid_epi(accs, av, bv):
        dhid = 0.5 * accs[0]
        av, bv = av.astype(F32), bv.astype(F32)
        sig = jax.nn.sigmoid(av)
        da = dhid * bv * (sig * (1.0 + av * (1.0 - sig)))
        db = dhid * (av * sig)
        return da, db

    da, db = _mm(tag + "_bwd_hid", "nt", [dh_bf], [wd], [[(0, 0)]], t, f, tm, tn,
                 [(a, _tile(tm, tn)), (b, _tile(tm, tn))], hid_epi, [(_out(t, f, BF16), None)] * 2, after=after)
    tw, tnw = _pick(f, 1408), _pick(d, 512)
    d_wd = _mm1(tag + "_dwd", "tn", hid, dh_bf, f, d, tw, tnw, BF16, scale=0.5)
    d_wg = _mm1(tag + "_dwg", "tn", da, n, f, d, tw, tnw, BF16)
    d_wu = _mm1(tag + "_dwu", "tn", db, n, f, d, tw, tnw, BF16)
    pin = weights_done(d_wg, d_wu, d_wd) if weights_done is not None else None
    tm2, tn2 = _pick(t, 1024), _pick(d, 512)
    dn = _mm(tag + "_dn", "nn", [da, db], [wg_t, wu_t], [[(0, 0), (1, 1)]], t, d, tm2, tn2, [],
             lambda accs: (accs[0],), [(_out(t, d, F32), None)], after=pin)[0]
    dh_in, dh_in_bf, dg = _rms_bwd(tag + "_norm_bwd", h, g, dn, dh)
    return dh_in, dh_in_bf, dg, d_wg, d_wu, d_wd


def _window_sum(win, offsets):
    n = win.shape[0]
    acc = None
    for j in offsets:
        term = win if j == 0 else pltpu.roll(win, (-j) % n, 0)
        acc = term if acc is None else acc + term
    return acc


def _pool_counts(r0, ch, c, left, right, t):
    pos = r0 + lax.broadcasted_iota(jnp.int32, (ch, c), 0)
    return (jnp.minimum(pos + right + 1, t) - jnp.maximum(pos - left, 0)).astype(F32)


def _pool_fwd(proj, pool_w_bf, pool_scale):
    t = proj.shape[0]
    ng, c, _ = pool_w_bf.shape
    ch = _pick(t, 256, SUBLANES)
    pad = POOL_PAD

    def body(p_ref, w_ref, s_ref, pooled_ref, pm_ref, buf):
        grp = pl.program_id(0)
        buf[pl.ds(0, pad), :] = jnp.zeros((pad, c), F32)
        buf[pl.ds(pad + t, pad), :] = jnp.zeros((pad, c), F32)

        def fill(ci, carry):
            r0 = pl.multiple_of(ci * ch, SUBLANES)
            buf[pl.ds(pl.multiple_of(r0 + pad, SUBLANES), ch), :] = p_ref[pl.ds(r0, ch), :]
            return carry

        lax.fori_loop(0, t // ch, fill, 0)
        for gi, w in enumerate(POOL_WINDOWS):
            left = w // 2
            right = w - 1 - left

            @pl.when(grp == gi)
            def _(left=left, right=right):
                def chunk(ci, carry):
                    r0 = pl.multiple_of(ci * ch, SUBLANES)
                    win = buf[pl.ds(r0, ch + 2 * pad), :]
                    s = _window_sum(win, range(-left, right + 1))[pad:pad + ch]
                    pooled = s / _pool_counts(r0, ch, c, left, right, t) - win[pad:pad + ch]
                    pooled_bf = pooled.astype(BF16)
                    mixed = jnp.dot(pooled_bf, w_ref[0], preferred_element_type=F32)
                    pooled_ref[pl.ds(r0, ch), :] = pooled_bf
                    pm_ref[pl.ds(r0, ch), :] = (mixed * s_ref[...]).astype(BF16)
                    return carry

                lax.fori_loop(0, t // ch, chunk, 0)

    col = pl.BlockSpec((t, c), lambda g: (0, g))
    return pl.pallas_call(
        body, name="pool_fwd", grid=(ng,),
        in_specs=[col, pl.BlockSpec((1, c, c), lambda g: (g, 0, 0)), pl.BlockSpec((1, c), lambda g: (0, g))],
        out_specs=[col, col], out_shape=[_out(t, ng * c, BF16), _out(t, ng * c, BF16)],
        scratch_shapes=[pltpu.VMEM((t + 2 * pad, c), F32)],
        compiler_params=_params(("parallel",)),
    )(proj, pool_w_bf, pool_scale)


def _pool_bwd(pooled, dpm, pool_w_bf, pool_scale):
    t = pooled.shape[0]
    ng, c, _ = pool_w_bf.shape
    ch = _pick(t, 256, SUBLANES)
    pad = POOL_PAD

    def body(pooled_ref, dpm_ref, w_ref, s_ref, dp_ref, dw_ref, ds_ref, buf, raw):
        grp = pl.program_id(0)
        buf[pl.ds(0, pad), :] = jnp.zeros((pad, c), F32)
        buf[pl.ds(pad + t, pad), :] = jnp.zeros((pad, c), F32)
        dw_ref[...] = jnp.zeros_like(dw_ref)
        ds_ref[...] = jnp.zeros_like(ds_ref)
        for gi, w in enumerate(POOL_WINDOWS):
            left = w // 2
            right = w - 1 - left

            @pl.when(grp == gi)
            def _(left=left, right=right):
                def first(ci, carry):
                    r0 = pl.multiple_of(ci * ch, SUBLANES)
                    pv = pooled_ref[pl.ds(r0, ch), :]
                    dpm_v = dpm_ref[pl.ds(r0, ch), :]
                    mixed = jnp.dot(pv, w_ref[0], preferred_element_type=F32)
                    ds_ref[...] += jnp.sum(dpm_v * mixed, axis=0, keepdims=True)
                    dmixed = (dpm_v * s_ref[...]).astype(BF16)
                    dw_ref[0] += lax.dot_general(pv, dmixed, _DIMS["tn"], preferred_element_type=F32)
                    dpooled = lax.dot_general(dmixed, w_ref[0], _DIMS["nt"], preferred_element_type=F32)
                    raw[pl.ds(r0, ch), :] = dpooled
                    buf[pl.ds(pl.multiple_of(r0 + pad, SUBLANES), ch), :] = (
                        dpooled / _pool_counts(r0, ch, c, left, right, t))
                    return carry

                lax.fori_loop(0, t // ch, first, 0)

                def second(ci, carry):
                    r0 = pl.multiple_of(ci * ch, SUBLANES)
                    win = buf[pl.ds(r0, ch + 2 * pad), :]
                    s = _window_sum(win, range(-right, left + 1))[pad:pad + ch]
                    dp_ref[pl.ds(r0, ch), :] = (s - raw[pl.ds(r0, ch), :]).astype(BF16)
                    return carry

                lax.fori_loop(0, t // ch, second, 0)

    col = pl.BlockSpec((t, c), lambda g: (0, g))
    return pl.pallas_call(
        body, name="pool_bwd", grid=(ng,),
        in_specs=[col, col, pl.BlockSpec((1, c, c), lambda g: (g, 0, 0)), pl.BlockSpec((1, c), lambda g: (0, g))],
        out_specs=[col, pl.BlockSpec((1, c, c), lambda g: (g, 0, 0)), pl.BlockSpec((1, c), lambda g: (0, g))],
        out_shape=[_out(t, ng * c, BF16), jax.ShapeDtypeStruct((ng, c, c), F32), _out(1, ng * c, F32)],
        scratch_shapes=[pltpu.VMEM((t + 2 * pad, c), F32), pltpu.VMEM((t, c), F32)],
        compiler_params=_params(("parallel",)),
    )(pooled, dpm, pool_w_bf, pool_scale)


def _discretise(a_re, a_im, log_dt, b_re, b_im):
    dt = jnp.exp(log_dt)
    mag = jnp.exp(dt * a_re)
    ang = dt * a_im
    abr = mag * jnp.cos(ang)
    abi = mag * jnp.sin(ang)
    den = a_re * a_re + a_im * a_im
    nr = abr - 1.0
    qr = (nr * a_re + abi * a_im) / den
    qi = (abi * a_re - nr * a_im) / den
    return abr, abi, qr * b_re - qi * b_im, qr * b_im + qi * b_re


def _ssm_disc(args):
    def body(ar, ai, ld, br, bi, o1, o2, o3, o4):
        res = _discretise(ar[...], ai[...], ld[...], br[...], bi[...])
        for o, r in zip((o1, o2, o3, o4), res):
            o[...] = r

    like = lambda a: jax.ShapeDtypeStruct(a.shape, F32)
    return pl.pallas_call(
        body, name="ssm_disc", out_shape=[like(args[0]), like(args[0]), like(args[3]), like(args[3])],
    )(*args)


def _ssm_disc_bwd(args, cots):
    def body(ar, ai, ld, br, bi, c1, c2, c3, c4, o1, o2, o3, o4, o5):
        _, vjp = jax.vjp(_discretise, ar[...], ai[...], ld[...], br[...], bi[...])
        res = vjp((c1[...], c2[...], c3[...], c4[...]))
        for o, r in zip((o1, o2, o3, o4, o5), res):
            o[...] = r

    return pl.pallas_call(
        body, name="ssm_disc_bwd", out_shape=[jax.ShapeDtypeStruct(a.shape, F32) for a in args],
    )(*args, *cots)


def _cmul(pr, pi, qr, qi):
    return pr * qr - pi * qi, pr * qi + pi * qr


def _cpow(pr, pi, n):
    rr, ri = None, None
    while n:
        if n & 1:
            rr, ri = (pr, pi) if rr is None else _cmul(rr, ri, pr, pi)
        n >>= 1
        if n:
            pr, pi = _cmul(pr, pi, pr, pi)
    return rr, ri


def _segment_carry(er, ei, pr, pi, reverse):
    row = lax.broadcasted_iota(jnp.int32, er.shape, 0)
    cr, ci = jnp.zeros_like(er), jnp.zeros_like(ei)
    for _ in range(SUBLANES - 1):
        tr = er + pr * cr - pi * ci
        ti = ei + pr * ci + pi * cr
        if reverse:
            keep, shift = row < SUBLANES - 1, SUBLANES - 1
        else:
            keep, shift = row >= 1, 1
        cr = jnp.where(keep, pltpu.roll(tr, shift, 0), 0.0)
        ci = jnp.where(keep, pltpu.roll(ti, shift, 0), 0.0)
    return cr, ci


def _ssm_fwd(name, sp, b_re, b_im, c_re, c_im, ar, ai, reverse):
    t, c = sp.shape
    s = ar.shape[1]
    w = _pick(s, 512)
    ch = _pick(t, 512, SUBLANES)
    n_ch, gpc, steps = t // ch, ch // SUBLANES, t // SUBLANES

    def body(sp_ref, bre_ref, bim_ref, cre_ref, cim_ref, ar_ref, ai_ref, xr_ref, xi_ref, y_ref, ur, ui, xbr, xbi):
        a_r = jnp.broadcast_to(ar_ref[...], (SUBLANES, w))
        a_i = jnp.broadcast_to(ai_ref[...], (SUBLANES, w))

        @pl.when(pl.program_id(0) == 0)
        def _():
            y_ref[...] = jnp.zeros_like(y_ref)

        def sweep(h0, store):
            def chunk(k, h):
                ci = n_ch - 1 - k if reverse else k
                rows = pl.ds(pl.multiple_of(ci * ch, ch), ch)
                spv = sp_ref[rows, :].astype(BF16)
                ur[...] = jnp.dot(spv, bre_ref[...], preferred_element_type=F32)
                ui[...] = jnp.dot(spv, bim_ref[...], preferred_element_type=F32)

                def group(g, hh):
                    gi = gpc - 1 - g if reverse else g
                    r0 = pl.multiple_of(gi * SUBLANES, SUBLANES)
                    hr, hi = hh
                    nr = a_r * hr - a_i * hi + ur[pl.ds(r0, SUBLANES), :]
                    ni = a_r * hi + a_i * hr + ui[pl.ds(r0, SUBLANES), :]
                    if store:
                        xbr[pl.ds(r0, SUBLANES), :] = nr
                        xbi[pl.ds(r0, SUBLANES), :] = ni
                    return nr, ni

                h = lax.fori_loop(0, gpc, group, h)
                if store:
                    xr16, xi16 = xbr[...].astype(BF16), xbi[...].astype(BF16)
                    xr_ref[rows, :] = xr16
                    xi_ref[rows, :] = xi16
                    y_ref[rows, :] += (lax.dot_general(xr16, cre_ref[...], _DIMS["nt"], preferred_element_type=F32)
                                       + lax.dot_general(xi16, cim_ref[...], _DIMS["nt"], preferred_element_type=F32))
                return h

            return lax.fori_loop(0, n_ch, chunk, h0)

        zero = jnp.zeros((SUBLANES, w), F32)
        er, ei = sweep((zero, zero), False)
        pr, pi = _cpow(ar_ref[...], ai_ref[...], steps)
        sweep(_segment_carry(er, ei, pr, pi, reverse), True)

    col = lambda i: (0, i)
    return pl.pallas_call(
        body, name=name, grid=(s // w,),
        in_specs=[pl.BlockSpec((t, c), lambda i: (0, 0))] + [pl.BlockSpec((c, w), col)] * 4
        + [pl.BlockSpec((1, w), col)] * 2,
        out_specs=[pl.BlockSpec((t, w), col), pl.BlockSpec((t, w), col), pl.BlockSpec((t, c), lambda i: (0, 0))],
        out_shape=[_out(t, s, BF16), _out(t, s, BF16), _out(t, c, F32)],
        scratch_shapes=[pltpu.VMEM((ch, w), F32)] * 4,
        compiler_params=_params(("arbitrary",)),
    )(sp, b_re, b_im, c_re, c_im, ar, ai)


def _ssm_bwd(name, dyp, c_re, c_im, xr, xi, ar, ai, reverse):
    t, c = dyp.shape
    s = ar.shape[1]
    w = _pick(s, 512)
    ch = _pick(t, 512, SUBLANES)
    n_ch, gpc, steps = t // ch, ch // SUBLANES, t // SUBLANES
    back = not reverse
    edge = 2 * SUBLANES

    def body(dy_ref, cre_ref, cim_ref, xr_ref, xi_ref, ar_ref, ai_ref, lr_ref, li_ref, dar_ref, dai_ref,
             gr, gi_, lbr, lbi, xbr, xbi):
        a_r = jnp.broadcast_to(ar_ref[...], (SUBLANES, w))
        a_i = -jnp.broadcast_to(ai_ref[...], (SUBLANES, w))
        row = lax.broadcasted_iota(jnp.int32, (SUBLANES, w), 0)

        def neighbours(ci, x_ref, buf):
            rows = pl.ds(pl.multiple_of(ci * ch, ch), ch)
            if reverse:
                buf[pl.ds(0, ch), :] = x_ref[rows, :].astype(F32)
                nxt = x_ref[pl.ds(pl.multiple_of(jnp.minimum(ci + 1, n_ch - 1) * ch, ch), edge), :].astype(F32)[:SUBLANES]
                first = x_ref[pl.ds(0, edge), :].astype(F32)[:SUBLANES]
                wrap = jnp.where(row < SUBLANES - 1, pltpu.roll(first, SUBLANES - 1, 0), 0.0)
                buf[pl.ds(ch, SUBLANES), :] = jnp.where(ci == n_ch - 1, wrap, nxt)
            else:
                buf[pl.ds(SUBLANES, ch), :] = x_ref[rows, :].astype(F32)
                prv = x_ref[pl.ds(pl.multiple_of(jnp.maximum(ci * ch - edge, 0), edge), edge), :].astype(F32)[SUBLANES:]
                last = x_ref[pl.ds(t - edge, edge), :].astype(F32)[SUBLANES:]
                wrap = jnp.where(row >= 1, pltpu.roll(last, 1, 0), 0.0)
                buf[pl.ds(0, SUBLANES), :] = jnp.where(ci == 0, wrap, prv)

        def sweep(h0, store):
            def chunk(k, carry):
                ci = n_ch - 1 - k if back else k
                rows = pl.ds(pl.multiple_of(ci * ch, ch), ch)
                dyv = dy_ref[rows, :].astype(BF16)
                gr[...] = jnp.dot(dyv, cre_ref[...], preferred_element_type=F32)
                gi_[...] = jnp.dot(dyv, cim_ref[...], preferred_element_type=F32)
                if store:
                    neighbours(ci, xr_ref, xbr)
                    neighbours(ci, xi_ref, xbi)

                def group(g, cc):
                    gidx = gpc - 1 - g if back else g
                    r0 = pl.multiple_of(gidx * SUBLANES, SUBLANES)
                    hr, hi = cc[0], cc[1]
                    nr = a_r * hr - a_i * hi + gr[pl.ds(r0, SUBLANES), :]
                    ni = a_r * hi + a_i * hr + gi_[pl.ds(r0, SUBLANES), :]
                    if not store:
                        return nr, ni
                    lbr[pl.ds(r0, SUBLANES), :] = nr
                    lbi[pl.ds(r0, SUBLANES), :] = ni
                    x0 = pl.multiple_of(r0 + SUBLANES, SUBLANES) if reverse else r0
                    xpr, xpi = xbr[pl.ds(x0, SUBLANES), :], xbi[pl.ds(x0, SUBLANES), :]
                    return nr, ni, cc[2] + nr * xpr + ni * xpi, cc[3] + ni * xpr - nr * xpi

                carry = lax.fori_loop(0, gpc, group, carry)
                if store:
                    lr_ref[rows, :] = lbr[...].astype(BF16)
                    li_ref[rows, :] = lbi[...].astype(BF16)
                return carry

            return lax.fori_loop(0, n_ch, chunk, h0)

        zero = jnp.zeros((SUBLANES, w), F32)
        er, ei = sweep((zero, zero), False)
        pr, pi = _cpow(ar_ref[...], -ai_ref[...], steps)
        cr, ci0 = _segment_carry(er, ei, pr, pi, back)
        _, _, dar, dai = sweep((cr, ci0, zero, zero), True)
        dar_ref[...] = jnp.sum(dar, axis=0, keepdims=True)
        dai_ref[...] = jnp.sum(dai, axis=0, keepdims=True)

    col = lambda i: (0, i)
    return pl.pallas_call(
        body, name=name, grid=(s // w,),
        in_specs=[pl.BlockSpec((t, c), lambda i: (0, 0)), pl.BlockSpec((c, w), col), pl.BlockSpec((c, w), col),
                  pl.BlockSpec((t, w), col), pl.BlockSpec((t, w), col), pl.BlockSpec((1, w), col), pl.BlockSpec((1, w), col)],
        out_specs=[pl.BlockSpec((t, w), col), pl.BlockSpec((t, w), col), pl.BlockSpec((1, w), col), pl.BlockSpec((1, w), col)],
        out_shape=[_out(t, s, BF16), _out(t, s, BF16), _out(1, s, F32), _out(1, s, F32)],
        scratch_shapes=[pltpu.VMEM((ch, w), F32)] * 4 + [pltpu.VMEM((ch + SUBLANES, w), F32)] * 2,
        compiler_params=_params(("parallel",)),
    )(dyp, c_re, c_im, xr, xi, ar, ai)


def _ssm_finish(y0, y1, sp, skip):
    t, c = sp.shape
    steps = t // SUBLANES
    w = _pick(c, LANES)

    def body(y0_ref, y1_ref, sp_ref, d_ref, y_ref, ys_ref):
        rows = pl.ds(pl.program_id(1), steps, stride=SUBLANES)
        y = y0_ref[rows, :] + y1_ref[rows, :] + sp_ref[rows, :] * d_ref[...]
        y_ref[...] = y
        ys_ref[...] = jax.nn.gelu(y).astype(BF16)

    whole = pl.BlockSpec((t, w), lambda j, k: (0, j))
    seg = pl.BlockSpec((steps, w), lambda j, k: (k, j))
    return pl.pallas_call(
        body, name="ssm_finish", grid=(c // w, SUBLANES),
        in_specs=[whole, whole, whole, pl.BlockSpec((1, w), lambda j, k: (0, j))], out_specs=[seg, seg],
        out_shape=[_out(t, c, F32), _out(t, c, BF16)], compiler_params=_params(("parallel", "arbitrary")),
    )(y0, y1, sp, skip)


def _to_segments(name, a, start=0, width=None):
    t = a.shape[0]
    width = a.shape[1] if width is None else width
    steps = t // SUBLANES
    w = _pick(width, LANES)

    def body(a_ref, o_ref):
        o_ref[pl.ds(pl.program_id(1), steps, stride=SUBLANES), :] = a_ref[...]

    return pl.pallas_call(
        body, name=name, grid=(width // w, SUBLANES),
        in_specs=[pl.BlockSpec((steps, w), lambda j, k: (k, start // w + j))],
        out_specs=pl.BlockSpec((t, w), lambda j, k: (0, j)), out_shape=_out(t, width, F32),
        compiler_params=_params(("parallel", "arbitrary")),
    )(a)


def _from_segments(a):
    t, c = a.shape
    return a.reshape(t // SUBLANES, SUBLANES, c).transpose(1, 0, 2).reshape(t, c)


def _colsum_prod(name, a, b, b_coff=0):
    t, n = a.shape
    tm = _pick(t, 512, SUBLANES)

    def body(a_ref, b_ref, o_ref):
        @pl.when(pl.program_id(0) == 0)
        def _():
            o_ref[...] = jnp.zeros_like(o_ref)

        o_ref[...] += jnp.sum(a_ref[...].astype(F32) * b_ref[...].astype(F32), axis=0, keepdims=True)

    return pl.pallas_call(
        body, name=name, grid=(t // tm,),
        in_specs=[pl.BlockSpec((tm, n), lambda i: (i, 0)), pl.BlockSpec((tm, n), lambda i: (i, b_coff))],
        out_specs=pl.BlockSpec((1, n), lambda i: (0, 0)), out_shape=_out(1, n, F32),
        compiler_params=_params(("arbitrary",)),
    )(a, b)


def _bd(blk):
    g, hh, p = blk.shape
    eye = jnp.eye(g, dtype=bool)[:, None, :, None]
    return jnp.where(eye, blk[:, :, None, :], 0.0).reshape(g * hh, g * p)


def _diag(dmat, g, hh, p):
    eye = jnp.eye(g, dtype=bool)[:, None, :, None]
    return jnp.sum(jnp.where(eye, dmat.reshape(g, hh, g, p), 0.0), axis=2)


def _softmax(qh, kh, scale):
    s = lax.dot_general(qh, kh, _DIMS["nt"], preferred_element_type=F32) * scale
    e = jnp.exp(s - jnp.max(s, axis=-1, keepdims=True))
    return e / jnp.sum(e, axis=-1, keepdims=True)


def _attn_fwd(q, kv):
    t, d = q.shape
    mm_ = kv.shape[0]
    hd = d // N_XHEADS
    scale = 1.0 / math.sqrt(hd)
    tm = _pick(t, 512, SUBLANES)

    def body(q_ref, kv_ref, o_ref):
        for h in range(N_XHEADS):
            sl = pl.ds(h * hd, hd)
            p = _softmax(q_ref[:, sl], kv_ref[:, sl], scale)
            o_ref[:, sl] = jnp.dot(p.astype(BF16), kv_ref[:, pl.ds(d + h * hd, hd)],
                                   preferred_element_type=F32).astype(BF16)

    return pl.pallas_call(
        body, name="attn_fwd", grid=(t // tm,),
        in_specs=[pl.BlockSpec((tm, d), lambda i: (i, 0)), pl.BlockSpec((mm_, 2 * d), lambda i: (0, 0))],
        out_specs=pl.BlockSpec((tm, d), lambda i: (i, 0)), out_shape=_out(t, d, BF16),
        compiler_params=_params(("parallel",)),
    )(q, kv)


def _attn_bwd(q, kv, do):
    t, d = q.shape
    mm_ = kv.shape[0]
    hd = d // N_XHEADS
    scale = 1.0 / math.sqrt(hd)
    tm = _pick(t, 512, SUBLANES)

    def body(q_ref, kv_ref, do_ref, dq_ref, dkv_ref):
        @pl.when(pl.program_id(0) == 0)
        def _():
            dkv_ref[...] = jnp.zeros_like(dkv_ref)

        for h in range(N_XHEADS):
            sl = pl.ds(h * hd, hd)
            vsl = pl.ds(d + h * hd, hd)
            qh, kh, doh = q_ref[:, sl], kv_ref[:, sl], do_ref[:, sl]
            p = _softmax(qh, kh, scale)
            dp = lax.dot_general(doh, kv_ref[:, vsl], _DIMS["nt"], preferred_element_type=F32)
            dkv_ref[:, vsl] += lax.dot_general(p.astype(BF16), doh, _DIMS["tn"], preferred_element_type=F32)
            ds = (p * (dp - jnp.sum(dp * p, axis=-1, keepdims=True)) * scale).astype(BF16)
            dq_ref[:, sl] = jnp.dot(ds, kh, preferred_element_type=F32).astype(BF16)
            dkv_ref[:, sl] += lax.dot_general(ds, qh, _DIMS["tn"], preferred_element_type=F32)

    row = pl.BlockSpec((tm, d), lambda i: (i, 0))
    full = pl.BlockSpec((mm_, 2 * d), lambda i: (0, 0))
    return pl.pallas_call(
        body, name="attn_bwd", grid=(t // tm,), in_specs=[row, full, row], out_specs=[row, full],
        out_shape=[_out(t, d, BF16), _out(mm_, 2 * d, F32)], compiler_params=_params(("arbitrary",)),
    )(q, kv, do)


def _ew(name, fn, ins, outs, rows_pref=256):
    r, c = ins[0].shape
    tr = _pick(r, rows_pref, SUBLANES)
    ni = len(ins)

    def body(*refs):
        res = fn(*[x[...] for x in refs[:ni]])
        for o_ref, v in zip(refs[ni:], res):
            o_ref[...] = v.astype(o_ref.dtype)

    blk = pl.BlockSpec((tr, c), lambda i: (i, 0))
    return pl.pallas_call(
        body, name=name, grid=(r // tr,), in_specs=[blk] * ni, out_specs=[blk] * len(outs),
        out_shape=[_out(r, c, dt) for dt in outs], compiler_params=_params(("parallel",)),
    )(*ins)


def _sum_slots(name, a, dtype):
    s, r, c = a.shape
    tr = _pick(r, 256, SUBLANES)

    def body(a_ref, o_ref):
        acc = a_ref[0].astype(F32)
        for k in range(1, s):
            acc = acc + a_ref[k].astype(F32)
        o_ref[...] = acc.astype(o_ref.dtype)

    return pl.pallas_call(
        body, name=name, grid=(r // tr,), in_specs=[pl.BlockSpec((s, tr, c), lambda i: (0, i, 0))],
        out_specs=pl.BlockSpec((tr, c), lambda i: (i, 0)), out_shape=_out(r, c, dtype),
        compiler_params=_params(("parallel",)),
    )(a)


def _adamw_step(wv, gv, mv, vv):
    bc1 = 1.0 - ADAM_B1 ** ADAM_STEP
    bc2 = 1.0 - ADAM_B2 ** ADAM_STEP
    m2 = ADAM_B1 * mv + (1.0 - ADAM_B1) * gv
    v2 = ADAM_B2 * vv + (1.0 - ADAM_B2) * (gv * gv)
    delta = -ADAM_LR * ((m2 / bc1) / (jnp.sqrt(v2 / bc2) + ADAM_EPS) + ADAM_WD * wv)
    return delta, m2, v2


def _adamw_group(name, items, transposed):
    k, r = items[0][0].shape
    if transposed and r % LANES != 0:
        rows = _adamw_group(name, [(w.T, g, m.T, v.T) for w, g, m, v in items], False)
        return [[a.T for a in item] for item in rows]
    tk = _pick(k, max(SUBLANES, ADAMW_STEP_WORDS // (r * len(items))), SUBLANES)
    n_out = 4 if transposed else 3

    def body(*refs):
        ins, outs = refs[:4 * len(items)], refs[4 * len(items):]
        for i in range(len(items)):
            wv, gv, mv, vv = (a[...] for a in ins[4 * i:4 * i + 4])
            if transposed:
                gv = gv.T
            res = _adamw_step(wv, gv, mv, vv) + ((gv,) if transposed else ())
            for o_ref, val in zip(outs[n_out * i:n_out * (i + 1)], res):
                o_ref[...] = val

    blk = pl.BlockSpec((tk, r), lambda j: (j, 0))
    g_blk = pl.BlockSpec((r, tk), lambda j: (0, j)) if transposed else blk
    res = pl.pallas_call(
        body, name=name, grid=(k // tk,), in_specs=[blk, g_blk, blk, blk] * len(items),
        out_specs=[blk] * (n_out * len(items)), out_shape=[pltpu.HBM((k, r), F32)] * (n_out * len(items)),
        compiler_params=_params(("parallel",)),
    )(*[pltpu.with_memory_space_constraint(a, pltpu.HBM) for item in items for a in item])
    return [list(res[n_out * i:n_out * (i + 1)]) + ([] if transposed else [items[i][1]]) for i in range(len(items))]


def _allgather(name, arrs):
    n = len(arrs)

    def body(*refs):
        ins, outs = refs[:n], refs[n:2 * n]
        send_sems, recv_sems, local_sems = refs[2 * n:]
        x, y, c = lax.axis_index("x"), lax.axis_index("y"), lax.axis_index("c")
        me, sibling = (x, y, c), (x, y, 1 - c)
        chips = [(1 - x, y), (x, 1 - y), (1 - x, 1 - y)]

        def rows(a, px, py, pc):
            r = ins[a].shape[0]
            return outs[a].at[pl.ds((4 * px + 2 * py + pc) * r, r), :]

        def copy(a, k, block, to, src=None):
            return pltpu.make_async_remote_copy(
                src_ref=rows(a, *block) if src is None else src, dst_ref=rows(a, *block),
                send_sem=send_sems.at[a, k], recv_sem=recv_sems.at[a, k], device_id=to, device_id_type=MESH)

        mine = [pltpu.make_async_copy(ins[a], rows(a, *me), local_sems.at[a]) for a in range(n)]
        for cp in mine:
            cp.start()
        first = []
        for a in range(n):
            first.append(copy(a, 0, me, sibling, src=ins[a]))
            first += [copy(a, 1 + j, me, (*chip, c), src=ins[a]) for j, chip in enumerate(chips)]
        for cp in first:
            cp.start()
        passed = []
        for j, chip in enumerate(chips):
            for a in range(n):
                copy(a, 1 + j, (*chip, c), me).wait_recv()
                cp = copy(a, 4 + j, (*chip, c), sibling)
                cp.start()
                passed.append(cp)
        for a in range(n):
            copy(a, 0, sibling, me).wait_recv()
            for j, chip in enumerate(chips):
                copy(a, 4 + j, (*chip, 1 - c), me).wait_recv()
        for cp in first + passed:
            cp.wait_send()
        for cp in mine:
            cp.wait()

    return pl.pallas_call(
        body, name=name, in_specs=[ANY] * n, out_specs=[ANY] * n,
        out_shape=[_out(N_DEV * a.shape[0], a.shape[1], a.dtype) for a in arrs],
        scratch_shapes=[pltpu.SemaphoreType.DMA((n, 7)), pltpu.SemaphoreType.DMA((n, 7)), pltpu.SemaphoreType.DMA((n,))],
    )(*arrs)


def _cores_start(name, blocks):
    n = len(blocks)
    c = blocks[0].shape[2]
    r = sum(b.shape[1] for b in blocks)

    def build(src_refs, land_refs, send_sems, recv_sems):
        x, y, cc = lax.axis_index("x"), lax.axis_index("y"), lax.axis_index("c")
        remote, off = [], 0
        for a, src in enumerate(src_refs):
            rows = pl.ds(off, src.shape[1])
            off += src.shape[1]
            for q in range(4):
                remote.append(pltpu.make_async_remote_copy(
                    src_ref=src.at[2 * q + (1 - cc)], dst_ref=land_refs[0].at[q, rows], send_sem=send_sems.at[4 * a + q],
                    recv_sem=recv_sems.at[4 * a + q], device_id=(x, y, 1 - cc), device_id_type=MESH))
        return remote, []

    return _split_start(name, [(blocks, [jax.ShapeDtypeStruct((4, r, c), blocks[0].dtype)], 4 * n, 0, build)])[0]


def _peer(k, x, y, c):
    return (1 - x if k & 4 else x, 1 - y if k & 2 else y, 1 - c if k & 1 else c)


def _split_start(name, groups, after=None):
    pins = [] if after is None else [after]
    bufs, sem_shapes, spans = [], [], []
    for srcs, land_shapes, n_remote, n_local, _ in groups:
        sems = [pltpu.SemaphoreType.DMA((n_remote,)), pltpu.SemaphoreType.DMA((n_remote,))]
        sems += [pltpu.SemaphoreType.DMA((n_local,))] if n_local else []
        spans.append((len(bufs), len(srcs), len(land_shapes), len(sem_shapes), len(sems)))
        bufs += [pltpu.with_memory_space_constraint(a, pltpu.HBM) for a in srcs]
        bufs += [pltpu.with_memory_space_constraint(lax.empty(s.shape, s.dtype), pltpu.HBM) for s in land_shapes]
        sem_shapes += sems
    n_buf, n_sem = len(bufs), len(sem_shapes)

    def body(*refs):
        buf_refs, sem_refs, token = refs[:n_buf], refs[n_buf + len(pins):n_buf + len(pins) + n_sem], refs[-1]
        for (b0, ns, nl, s0, k), group in zip(spans, groups):
            remote, local = group[4](buf_refs[b0:b0 + ns], buf_refs[b0 + ns:b0 + ns + nl], *sem_refs[s0:s0 + k])
            for cp in local + remote:
                cp.start()
        token[...] = jnp.zeros_like(token)

    outs = pl.pallas_call(
        body, name=name,
        out_shape=sem_shapes + [pltpu.HBM(b.shape, b.dtype) for b in bufs] + [jax.ShapeDtypeStruct((SUBLANES, LANES), F32)],
        in_specs=[HBM] * n_buf + [ANY] * len(pins),
        out_specs=[SEM] * n_sem + [HBM] * n_buf + [pl.BlockSpec(memory_space=pltpu.VMEM)],
        input_output_aliases={i: n_sem + i for i in range(n_buf)},
        compiler_params=pltpu.CompilerParams(has_side_effects=SIDE_EFFECT),
    )(*bufs, *pins)
    return [dict(sems=list(outs[s0:s0 + k]), bufs=list(outs[n_sem + b0:n_sem + b0 + ns + nl]), token=outs[-1],
                 build=group[4], ns=ns) for (b0, ns, nl, s0, k), group in zip(spans, groups)]


def _split_wait(name, started, after):
    ns, n_buf, n_sem = started["ns"], len(started["bufs"]), len(started["sems"])

    def body(*refs):
        src_refs, land_refs = refs[:ns], refs[ns:n_buf]
        sems = refs[n_buf:n_buf + n_sem]
        remote, local = started["build"](src_refs, land_refs, *sems)
        for cp in local:
            cp.wait()
        for cp in remote:
            cp.wait_send()
            cp.wait_recv()

    outs = pl.pallas_call(
        body, name=name, out_shape=[pltpu.HBM(b.shape, b.dtype) for b in started["bufs"]],
        in_specs=[HBM] * n_buf + [SEM] * n_sem + [ANY], out_specs=[HBM] * n_buf,
        input_output_aliases={i: i for i in range(n_buf)},
        compiler_params=pltpu.CompilerParams(has_side_effects=SIDE_EFFECT),
    )(*started["bufs"], *started["sems"], after)
    return list(outs[:ns]), list(outs[ns:])


def _gather_group(shards):
    m = len(shards)

    def build(src_refs, land_refs, send_sems, recv_sems, local_sems):
        x, y, c = lax.axis_index("x"), lax.axis_index("y"), lax.axis_index("c")
        remote, local = [], []
        for j in range(m):
            r = src_refs[j].shape[0]
            dst = land_refs[j].at[pl.ds((4 * x + 2 * y + c) * r, r), :]
            local.append(pltpu.make_async_copy(src_refs[j], dst, local_sems.at[j]))
            for k in range(1, N_DEV):
                remote.append(pltpu.make_async_remote_copy(
                    src_ref=src_refs[j], dst_ref=dst, send_sem=send_sems.at[7 * j + k - 1],
                    recv_sem=recv_sems.at[7 * j + k - 1], device_id=_peer(k, x, y, c), device_id_type=MESH))
        return remote, local

    lands = [jax.ShapeDtypeStruct((N_DEV * a.shape[0], a.shape[1]), a.dtype) for a in shards]
    return shards, lands, 7 * m, m, build


def _slots_start(name, a):
    def build(src_refs, land_refs, send_sems, recv_sems, local_sems):
        x, y, c = lax.axis_index("x"), lax.axis_index("y"), lax.axis_index("c")
        dst = land_refs[0].at[4 * x + 2 * y + c]
        local = [pltpu.make_async_copy(src_refs[0], dst, local_sems.at[0])]
        remote = [pltpu.make_async_remote_copy(
            src_ref=src_refs[0], dst_ref=dst, send_sem=send_sems.at[k - 1], recv_sem=recv_sems.at[k - 1],
            device_id=_peer(k, x, y, c), device_id_type=MESH) for k in range(1, N_DEV)]
        return remote, local

    return _split_start(name, [([a], [jax.ShapeDtypeStruct((N_DEV,) + a.shape, a.dtype)], 7, 1, build)])[0]


def _chips_start(name, p):
    _, r, c = p.shape
    nck = r // GRAD_ROW_TILE

    def build(src_refs, land_refs, send_sems, recv_sems):
        x, y, cc = lax.axis_index("x"), lax.axis_index("y"), lax.axis_index("c")
        remote = []
        for k in range(1, 4):
            px = 1 - x if k >> 1 else x
            py = 1 - y if k & 1 else y
            for j in range(nck):
                rows = pl.ds(j * GRAD_ROW_TILE, GRAD_ROW_TILE)
                remote.append(pltpu.make_async_remote_copy(
                    src_ref=src_refs[0].at[2 * px + py, rows], dst_ref=land_refs[0].at[k - 1, rows],
                    send_sem=send_sems.at[(k - 1) * nck + j], recv_sem=recv_sems.at[(k - 1) * nck + j],
                    device_id=(px, py, cc), device_id_type=MESH))
        return remote, []

    return _split_start(name, [([p], [jax.ShapeDtypeStruct((3, r, c), p.dtype)], 3 * nck, 0, build)])[0]


def _chip_sum(name, p, recv, chip):
    _, r, c = p.shape
    tr = _pick(r, 5 * GRAD_ROW_TILE, GRAD_ROW_TILE)

    def body(chip_ref, p_ref, r_ref, o_ref):
        acc = p_ref[...].astype(F32)
        for k in range(3):
            acc = acc + r_ref[k].astype(F32)
        o_ref[...] = acc

    return pl.pallas_call(
        body, name=name,
        grid_spec=pltpu.PrefetchScalarGridSpec(
            num_scalar_prefetch=1, grid=(r // tr,),
            in_specs=[pl.BlockSpec((None, tr, c), lambda i, chip_ref: (chip_ref[0], i, 0)),
                      pl.BlockSpec((3, tr, c), lambda i, chip_ref: (0, i, 0))],
            out_specs=pl.BlockSpec((tr, c), lambda i, chip_ref: (i, 0))),
        out_shape=_out(r, c, F32), compiler_params=_params(("parallel",)),
    )(chip, p, recv)


def _local_step(x, mem, tgt, wt, sm, ev=None):
    t, d = x.shape
    n_mem = mem.shape[0]
    d_pool = sm["pool_scale"].shape[1]
    ng, pc = sm["pool_w"].shape[0], sm["pool_w"].shape[1]
    d_ssm = sm["ssm_d"].shape[1]
    _, sg, sp, sh = sm["ssm_b_re"].shape
    n_state = sg * sp
    gb, gs = {}, {}

    def emit(name, **kw):
        return ev(name, **kw) if ev is not None else None

    n1 = _rms_fwd("ffn1_norm", x, sm["ffn1_norm"])
    emit("ffn1_norm_done", marker=n1)
    def ffn1_down(hid):
        emit("ffn1_up_done", marker=hid)
        return wt["ffn1_w_down"]

    h1, ffn1_saved = _ffn_fwd("ffn1", x, n1, wt["ffn1_w_gate"], wt["ffn1_w_up"], ffn1_down)
    emit("ffn1_fwd_done", marker=h1)
    u = _rms_fwd("mix_norm", h1, sm["mix_norm"])
    d_in = wt["w_in"].shape[0]
    tm, tn = _pick(t, 1024), _pick(d_in, 1408)
    proj = _mm1("in_proj", "nt", u, wt["w_in"], t, d_in, tm, tn, F32)
    off_s = d_pool // d_ssm
    off_gp = (d_pool + d_ssm)
    off_gs = off_gp + d

    pool_w_bf = sm["pool_w"].astype(BF16)
    pooled, pm = _pool_fwd(proj, pool_w_bf, sm["pool_scale"])

    by_p = lambda a: jnp.swapaxes(a, -1, -2).reshape(2 * sg, sh, sp)
    disc_args = [sm["ssm_a_re"].reshape(2 * sg, 1, sp), sm["ssm_a_im"].reshape(2 * sg, 1, sp),
                 sm["ssm_log_dt"].reshape(2 * sg, 1, 1), by_p(sm["ssm_b_re"]), by_p(sm["ssm_b_im"])]
    abr, abi, bbr, bbi = _ssm_disc(disc_args)
    abr2, abi2 = abr.reshape(2, n_state), abi.reshape(2, n_state)
    per_dir = lambda a: [_bd(a.reshape(2, sg, sh, sp)[dr]).astype(BF16) for dr in range(2)]
    b_re, b_im, c_re, c_im = per_dir(bbr), per_dir(bbi), per_dir(sm["ssm_c_re"]), per_dir(-sm["ssm_c_im"])
    sp32 = _to_segments("ssm_in_segments", proj, d_pool, d_ssm)
    xs, y_parts = [], []
    for dr in range(2):
        xr, xi, y_part = _ssm_fwd(f"ssm_fwd{dr}", sp32, b_re[dr], b_im[dr], c_re[dr], c_im[dr], abr2[dr:dr + 1],
                                  abi2[dr:dr + 1], reverse=(dr == 1))
        xs.append((xr, xi))
        y_parts.append(y_part)
    y, ys = _ssm_finish(y_parts[0], y_parts[1], sp32, sm["ssm_d"])
    tmy = _pick(t, 256)
    emit("mix_in_done", marker=ys)

    tmm, tnm, tnx = _pick(t, 1024), _pick(d, 256), _pick(d, 512)
    gp_spec = _tile(tmm, tnm, off_gp // tnm)
    gs_spec = _tile(tmm, tnm, off_gs // tnm)

    def merge_epi(accs, gpv, gsv):
        z_pool, val, gate = accs
        return (jax.nn.sigmoid(gpv) * z_pool + jax.nn.sigmoid(gsv) * (val * jax.nn.sigmoid(gate)),)

    merged = _mm("mix_merge", "nt", [pm, ys], [wt["w_pool_proj"], wt["w_glu_val"], wt["w_glu_gate"]],
                 [[(0, 0)], [(1, 1)], [(1, 2)]], t, d, tmm, tnm, [(proj, gp_spec), (proj, gs_spec)], merge_epi,
                 [(_out(t, d, BF16), None)])[0]
    res_epi = lambda accs, hin: (hin + accs[0],)
    h2 = _mm("mix_out", "nn", [merged], [wt["w_mix_out"]], [[(0, 0)]], t, d, tmm, tnx, [(h1, _tile(tmm, tnx))],
             res_epi, [(_out(t, d, F32), None)])[0]

    un = _rms_fwd("xattn_norm", h2, sm["xattn_norm"])
    mn = _rms_fwd("mem_norm", mem, sm["mem_norm"])
    emit("mix_done", marker=un)
    q = _mm1("xattn_q", "nn", un, wt["w_q"], t, d, tmm, tnx, BF16)
    kv = _mm1("xattn_kv", "nt", mn, wt["w_kv"], n_mem, 2 * d, n_mem, _pick(2 * d, 512), BF16)
    o = _attn_fwd(q, kv)
    h3 = _mm("xattn_out", "nn", [o], [wt["w_xo"]], [[(0, 0)]], t, d, tmm, tnx, [(h2, _tile(tmm, tnx))],
             res_epi, [(_out(t, d, F32), None)])[0]

    n2 = _rms_fwd("ffn2_norm", h3, sm["ffn2_norm"])
    emit("xattn_done", marker=n2)
    h4, ffn2_saved = _ffn_fwd("ffn2", h3, n2, wt["ffn2_w_gate"], wt["ffn2_w_up"], wt["ffn2_w_down"])

    dh4, dh4_bf, gs["final_norm"], loss = _loss_head(h4, sm["final_norm"], tgt)
    dh3, dh3_bf, gs["ffn2_norm"], gb["ffn2_w_gate"], gb["ffn2_w_up"], gb["ffn2_w_down"] = _ffn_bwd(
        "ffn2", h3, sm["ffn2_norm"], wt["ffn2_w_gate"], wt["ffn2_w_up"], wt["ffn2_w_down"], ffn2_saved, dh4, dh4_bf)

    tw = _pick(d, 1024)
    do = _mm1("xattn_do", "nt", dh3_bf, wt["w_xo"], t, d, tmm, tnx, BF16)
    gb["w_xo"] = _mm1("xattn_dwxo", "tn", o, dh3_bf, d, d, tw, tnx, BF16)
    dq, dkv = _attn_bwd(q, kv, do)
    gb["w_q"] = _mm1("xattn_dwq", "tn", un, dq, d, d, tw, tnx, BF16)
    dun = _mm1("xattn_dun", "nt", dq, wt["w_q"], t, d, tmm, tnx, F32)
    dh2, dh2_bf, gs["xattn_norm"] = _rms_bwd("xattn_norm_bwd", h2, sm["xattn_norm"], dun, dh3)
    gb["w_kv"] = _mm1("xattn_dwkv", "tn", dkv, mn, 2 * d, d, _pick(2 * d, 512), d, BF16)
    dmn = _mm1("xattn_dmn", "nn", dkv, wt["w_kv"], n_mem, d, n_mem, tnx, F32)
    gs["mem_norm"] = _rms_bwd("mem_norm_bwd", mem, sm["mem_norm"], dmn)

    gb["w_mix_out"] = _mm1("mix_dwout", "tn", merged, dh2_bf, d, d, tw, tnx, BF16)

    def merge_bwd_epi(accs, gpv, gsv):
        dmerged, z_pool, val, gate = accs
        sp_, ss_, sg_ = jax.nn.sigmoid(gpv), jax.nn.sigmoid(gsv), jax.nn.sigmoid(gate)
        glu = val * sg_
        dz_pool = dmerged * sp_
        dg_pool = dmerged * z_pool * (sp_ * (1.0 - sp_))
        dz_ssm = dmerged * ss_
        dg_ssm = dmerged * glu * (ss_ * (1.0 - ss_))
        dval = dz_ssm * sg_
        dgate = dz_ssm * glu * (1.0 - sg_)
        return dz_pool, dg_pool, dg_ssm, dval, dgate

    dz_pool, dg_pool, dg_ssm, dval, dgate = _mm(
        "mix_merge_bwd", "nt", [dh2_bf, pm, ys], [wt["w_mix_out"], wt["w_pool_proj"], wt["w_glu_val"], wt["w_glu_gate"]],
        [[(0, 0)], [(1, 1)], [(2, 2)], [(2, 3)]], t, d, tmm, tnm, [(proj, gp_spec), (proj, gs_spec)], merge_bwd_epi,
        [(_out(t, d, BF16), None)] * 5)
    gb["w_pool_proj"] = _mm1("pool_dwproj", "tn", dz_pool, pm, d, d_pool, tw, d_pool, BF16)
    gb["w_glu_val"] = _mm1("glu_dwval", "tn", dval, ys, d, d_ssm, tw, d_ssm, BF16)
    gb["w_glu_gate"] = _mm1("glu_dwgate", "tn", dgate, ys, d, d_ssm, tw, d_ssm, BF16)

    def gelu_bwd_epi(accs, yv):
        _, vjp = jax.vjp(jax.nn.gelu, yv)
        return (vjp(accs[0])[0],)

    dy = _mm("glu_dy", "nn", [dval, dgate], [wt["w_glu_val"], wt["w_glu_gate"]], [[(0, 0), (1, 1)]], t, d_ssm, tmy, d_ssm,
             [(y, _tile(tmy, d_ssm))], gelu_bwd_epi, [(_out(t, d_ssm, F32), None)])[0]
    gs["ssm_d"] = _colsum_prod("ssm_dd", dy, proj, b_coff=off_s)
    dyp = _to_segments("ssm_dy_segments", dy)
    d_abr, d_abi, d_bbr, d_bbi, d_cre, d_cim, lams = [], [], [], [], [], [], []
    ts = _pick(n_state, 512)
    for dr in range(2):
        lr, li, dar, dai = _ssm_bwd(f"ssm_bwd{dr}", dyp, c_re[dr], c_im[dr], xs[dr][0], xs[dr][1], abr2[dr:dr + 1],
                                    abi2[dr:dr + 1], reverse=(dr == 1))
        d_abr.append(dar)
        d_abi.append(dai)
        lams += [lr, li]
        maps = _mm(f"ssm_dmaps{dr}", "tn", [sp32, dyp], [lr, li, xs[dr][0], xs[dr][1]],
                   [[(0, 0)], [(0, 1)], [(1, 2)], [(1, 3)]], d_ssm, n_state, d_ssm, ts, [], lambda accs: tuple(accs),
                   [(_out(d_ssm, n_state, F32), None)] * 4)
        for acc, m in zip((d_bbr, d_bbi, d_cre, d_cim), maps):
            acc.append(_diag(m, sg, sh, sp))
    ds = _from_segments(_mm(
        "ssm_ds", "nt", lams, [b_re[0], b_im[0], b_re[1], b_im[1]], [[(k, k) for k in range(4)]], t, d_ssm, tmy,
        d_ssm, [(dyp, _tile(tmy, d_ssm)), (sm["ssm_d"], _rowvec(d_ssm))],
        lambda accs, dyv, dv: (dyv * dv + accs[0],), [(_out(t, d_ssm, BF16), None)])[0])
    cots = [jnp.concatenate(d_abr, axis=0).reshape(2 * sg, 1, sp), jnp.concatenate(d_abi, axis=0).reshape(2 * sg, 1, sp),
            jnp.concatenate(d_bbr, axis=0), jnp.concatenate(d_bbi, axis=0)]
    d_are, d_aim, d_ldt, d_bre, d_bim = _ssm_disc_bwd(disc_args, cots)
    gs["ssm_a_re"] = d_are.reshape(2, sg, sp)
    gs["ssm_a_im"] = d_aim.reshape(2, sg, sp)
    gs["ssm_log_dt"] = d_ldt.reshape(2, sg)
    from_p = lambda a: jnp.swapaxes(a.reshape(2, sg, sh, sp), -1, -2)
    gs["ssm_b_re"], gs["ssm_b_im"] = from_p(d_bre), from_p(d_bim)
    gs["ssm_c_re"] = jnp.stack(d_cre, axis=0)
    gs["ssm_c_im"] = -jnp.stack(d_cim, axis=0)

    dpm = _mm1("pool_dpm", "nn", dz_pool, wt["w_pool_proj"], t, d_pool, tmm, _pick(d_pool, 256), F32)
    dp, gs["pool_w"], gs["pool_scale"] = _pool_bwd(pooled, dpm, pool_w_bf, sm["pool_scale"])

    w_in = wt["w_in"]
    parts = [(dp, 0, d_pool), (ds, d_pool, d_ssm), (dg_pool, off_gp, d), (dg_ssm, off_gs, d)]
    w_in_parts = [w_in[o0:o0 + width] for _, o0, width in parts]
    gb["w_in"] = jnp.concatenate(
        [_mm1(f"in_proj_dw{k}", "tn", p_[0], u, p_[2], d, _pick(p_[2], 1024), tnx, BF16) for k, p_ in enumerate(parts)], axis=0)
    pin = emit("grads_main", gb=gb)
    du = _mm("in_proj_du", "nn", [p_[0] for p_ in parts], w_in_parts, [[(k, k) for k in range(4)]], t, d, tmm, tnx, [],
             lambda accs: (accs[0],), [(_out(t, d, F32), None)], after=pin)[0]
    dh1, dh1_bf, gs["mix_norm"] = _rms_bwd("mix_norm_bwd", h1, sm["mix_norm"], du, dh2)
    pin = emit("small_early", gs=gs, loss=loss)

    def ffn1_weights_done(d_wg, d_wu, d_wd):
        gb["ffn1_w_gate"], gb["ffn1_w_up"], gb["ffn1_w_down"] = d_wg, d_wu, d_wd
        return emit("grads_ffn1", gb=gb)

    dx, _, gs["ffn1_norm"], _, _, _ = _ffn_bwd(
        "ffn1", x, sm["ffn1_norm"], wt["ffn1_w_gate"], wt["ffn1_w_up"], wt["ffn1_w_down"], ffn1_saved, dh1, dh1_bf,
        weights_done=ffn1_weights_done, after=pin)
    return loss, dx, gb, gs


WEIGHTS = ["ffn1_norm", "ffn1_w_gate", "ffn1_w_up", "ffn1_w_down", "mix_norm", "w_in", "pool_w", "pool_scale",
           "w_pool_proj", "ssm_a_re", "ssm_a_im", "ssm_log_dt", "ssm_b_re", "ssm_b_im", "ssm_c_re", "ssm_c_im", "ssm_d",
           "w_glu_val", "w_glu_gate", "w_mix_out", "xattn_norm", "mem_norm", "w_q", "w_kv", "w_xo", "ffn2_norm",
           "ffn2_w_gate", "ffn2_w_up", "ffn2_w_down", "final_norm"]
COL_SHARDED = ["ffn1_w_gate", "ffn1_w_up", "w_in", "w_pool_proj", "w_glu_val", "w_glu_gate", "w_kv", "ffn2_w_gate",
               "ffn2_w_up"]
ROW_SHARDED = ["ffn1_w_down", "w_mix_out", "w_q", "w_xo", "ffn2_w_down"]
BIG = [n for n in WEIGHTS if n in COL_SHARDED or n in ROW_SHARDED]
SMALL = [n for n in WEIGHTS if n not in BIG]
FFN1_BIG = ["ffn1_w_gate", "ffn1_w_up", "ffn1_w_down"]
MAIN_BIG = [n for n in BIG if n not in FFN1_BIG]
GATHER_PLAN = [("ffn1_up_done", ["ffn1_w_down"]), ("ffn1_fwd_done", ["w_in"]),
               ("mix_in_done", ["w_pool_proj", "w_glu_val", "w_glu_gate", "w_mix_out"]),
               ("mix_done", ["w_q", "w_kv", "w_xo"]), ("xattn_done", ["ffn2_w_gate", "ffn2_w_up", "ffn2_w_down"])]
MINOR_SWAPPED = ["ssm_b_re", "ssm_b_im"]
LATE_SMALL = "ffn1_norm"
EARLY_SMALL = [n for n in SMALL if n != LATE_SMALL]
PACK_ROWS = SUBLANES * LANES
GRAD_ROW_TILE = 256
ADAMW_STEP_WORDS = 1 << 19


def _to_rows(name, w):
    return w.T if name in COL_SHARDED else w


def _pack_small(vals):
    flat = []
    for v in vals:
        f = v.reshape(-1)
        flat.append(jnp.pad(f, (0, (-f.shape[0]) % PACK_ROWS)))
    total = sum(f.shape[0] for f in flat)
    flat.append(jnp.zeros(((-total) % (GRAD_ROW_TILE * LANES),), F32))
    return jnp.concatenate(flat).reshape(-1, LANES)


def _unpack_small(packed, shapes):
    out, row = [], 0
    for shp in shapes:
        size = math.prod(shp)
        rows = -(-size // PACK_ROWS) * SUBLANES
        out.append(packed[row:row + rows].reshape(-1)[:size].reshape(shp))
        row += rows
    return out


def kernel(x, mem, ffn1_norm, ffn1_w_gate, ffn1_w_up, ffn1_w_down, mix_norm, w_in, pool_w, pool_scale, w_pool_proj, ssm_a_re, ssm_a_im, ssm_log_dt, ssm_b_re, ssm_b_im, ssm_c_re, ssm_c_im, ssm_d, w_glu_val, w_glu_gate, w_mix_out, xattn_norm, mem_norm, w_q, w_kv, w_xo, ffn2_norm, ffn2_w_gate, ffn2_w_up, ffn2_w_down, final_norm, loss_target, m_ffn1_norm, m_ffn1_w_gate, m_ffn1_w_up, m_ffn1_w_down, m_mix_norm, m_w_in, m_pool_w, m_pool_scale, m_w_pool_proj, m_ssm_a_re, m_ssm_a_im, m_ssm_log_dt, m_ssm_b_re, m_ssm_b_im, m_ssm_c_re, m_ssm_c_im, m_ssm_d, m_w_glu_val, m_w_glu_gate, m_w_mix_out, m_xattn_norm, m_mem_norm, m_w_q, m_w_kv, m_w_xo, m_ffn2_norm, m_ffn2_w_gate, m_ffn2_w_up, m_ffn2_w_down, m_final_norm, v_ffn1_norm, v_ffn1_w_gate, v_ffn1_w_up, v_ffn1_w_down, v_mix_norm, v_w_in, v_pool_w, v_pool_scale, v_w_pool_proj, v_ssm_a_re, v_ssm_a_im, v_ssm_log_dt, v_ssm_b_re, v_ssm_b_im, v_ssm_c_re, v_ssm_c_im, v_ssm_d, v_w_glu_val, v_w_glu_gate, v_w_mix_out, v_xattn_norm, v_mem_norm, v_w_q, v_w_kv, v_w_xo, v_ffn2_norm, v_ffn2_w_gate, v_ffn2_w_up, v_ffn2_w_down, v_final_norm):
    given = dict(locals())
    wts = {n: given[n] for n in WEIGHTS}
    moms = {n: (given["m_" + n], given["v_" + n]) for n in WEIGHTS}
    x2, mem2, tgt2 = x[0], mem[0], loss_target[0]
    d = x2.shape[1]
    chip = (2 * lax.axis_index("x") + lax.axis_index("y")).astype(jnp.int32).reshape(1)

    def full_form(n, f):
        shard = wts[n][0].shape
        return f.reshape(N_DEV * shard[1], shard[0]) if n in COL_SHARDED else f.reshape(N_DEV * shard[0], shard[1])

    shards = {n: _to_rows(n, wts[n][0]).astype(BF16) for n in BIG}
    first = FFN1_BIG[:2]
    wt = {n: full_form(n, f) for n, f in zip(first, _allgather("weight_allgather_first", [shards[n] for n in first]))}
    started = _split_start("weight_gather_start", [_gather_group([shards[n] for n in names]) for _, names in GATHER_PLAN],
                           after=wt[first[0]])
    gathers = {event: (names, st) for (event, names), st in zip(GATHER_PLAN, started)}
    sm = {n: (wts[n].reshape(1, -1) if wts[n].ndim <= 2 else wts[n][0]) for n in SMALL}
    sm["ffn1_norm"] = sm["ffn1_norm"] + started[0]["token"][0, 0]

    pending = {}

    def reduce_start(tag, names, gb):
        blocks = [gb[n].reshape(N_DEV, -1, d) for n in names]
        pad_rows = (-sum(b.shape[1] for b in blocks)) % GRAD_ROW_TILE
        pad = [jnp.zeros((N_DEV, pad_rows, d), BF16)] if pad_rows else []
        started = _cores_start("grad_exchange_cores_start_" + tag, blocks + pad)
        own = jnp.concatenate([lax.dynamic_index_in_dim(b.reshape(4, 2, b.shape[1], d), lax.axis_index("c"), 1, False)
                               for b in started["bufs"][:len(blocks + pad)]], axis=1)
        _, (recv,) = _split_wait("grad_exchange_cores_wait_" + tag, started, own)
        rows_all = own.shape[1]
        pair = _ew("grad_pair_sum_" + tag, lambda a, b: (a.astype(F32) + b.astype(F32),),
                   [own.reshape(-1, d), recv.reshape(-1, d)], [BF16], rows_pref=5 * GRAD_ROW_TILE)[0]
        pair = pair.reshape(4, rows_all, d)
        pending[tag] = (pair, _chips_start("grad_exchange_chips_start_" + tag, pair), [b.shape[1] for b in blocks])
        return pending[tag][1]["token"]

    def reduce_finish(tag, after):
        _, started, rows = pending[tag]
        (pair,), (recv,) = _split_wait("grad_exchange_chips_wait_" + tag, started, after)
        return _chip_sum("grad_chip_sum_" + tag, pair, recv, chip), rows

    def ev(name, gb=None, gs=None, loss=None, marker=None):
        if name in gathers:
            names, started = gathers[name]
            for n, f in zip(names, _split_wait("weight_gather_wait_" + name, started, marker)[1]):
                wt[n] = full_form(n, f)
        elif name == "grads_main":
            return reduce_start("main", MAIN_BIG, gb)
        elif name == "small_early":
            pending["small"] = _slots_start("small_gather_start", _pack_small([gs[n] for n in EARLY_SMALL] + [loss[:, :1]]))
            return pending["small"]["token"]
        elif name == "grads_ffn1":
            return reduce_start("ffn1", FFN1_BIG, gb)
        return None

    _, dx, _, gs = _local_step(x2, mem2, tgt2, wt, sm, ev)

    grads = {}
    for tag, names in (("main", MAIN_BIG), ("ffn1", FFN1_BIG)):
        g_rows, rows = reduce_finish(tag, dx)
        off = 0
        for n, r in zip(names, rows):
            shard = wts[n].shape
            grads[n] = g_rows[off:off + r].reshape((shard[2], shard[1]) if n in COL_SHARDED else shard[1:])
            off += r
    small_sum = _sum_slots("small_sum", _split_wait("small_gather_wait", pending["small"], dx)[1][0], F32)
    late = _allgather("small_allgather_late", [gs[LATE_SMALL].reshape(-1, LANES)])[0]
    late_sum = _sum_slots("small_sum_late", late.reshape(N_DEV, -1, LANES), F32)
    vals = _unpack_small(small_sum, [wts[n].shape for n in EARLY_SMALL] + [(1, 1)])
    total_loss = vals[-1].reshape(())
    def flat(n, a):
        a = a.reshape(wts[n].shape)
        a = jnp.swapaxes(a, -1, -2) if n in MINOR_SWAPPED else a
        return a.reshape(-1, a.shape[-1])

    def unflat(n, a):
        shape = wts[n].shape
        if n in MINOR_SWAPPED:
            return jnp.swapaxes(a.reshape(shape[:-2] + (shape[-1], shape[-2])), -1, -2)
        return a.reshape(shape)

    for n, g_full in zip(EARLY_SMALL + [LATE_SMALL], vals[:-1] + [late_sum]):
        grads[n] = flat(n, g_full)

    out_g, out_d, out_m, out_v = {}, {}, {}, {}
    by_shape = {}
    for n in WEIGHTS:
        by_shape.setdefault((flat(n, wts[n]).shape, n in COL_SHARDED), []).append(n)
    for (_, transposed), names in by_shape.items():
        items = [(flat(n, wts[n]), grads[n], flat(n, moms[n][0]), flat(n, moms[n][1])) for n in names]
        for n, res in zip(names, _adamw_group("adamw_" + names[0], items, transposed)):
            out_d[n], out_m[n], out_v[n], out_g[n] = (unflat(n, a) for a in res)

    return (total_loss, dx[None], *[out_g[n] for n in WEIGHTS], *[out_d[n] for n in WEIGHTS],
            *[out_m[n] for n in WEIGHTS], *[out_v[n] for n in WEIGHTS])
```

```python
import functools
import math

import jax
import jax.numpy as jnp
from jax import lax
from jax.experimental import pallas as pl
from jax.experimental.pallas import tpu as pltpu

F32 = jnp.float32
BF16 = jnp.bfloat16
EPS = 1e-6
N_XHEADS = 4
POOL_WINDOWS = (2, 4, 8, 16)
ADAM_LR = 0.001
ADAM_B1 = 0.9
ADAM_B2 = 0.999
ADAM_EPS = 1e-08
ADAM_WD = 0.01
ADAM_STEP = 10
N_DEV = 8
VMEM_LIMIT_V7X = 48 * 1024 * 1024
LANES = 128
SUBLANES = 8
SUB_ROWS = 256
POOL_PAD = 16
MESH = pl.DeviceIdType.MESH
ANY = pl.BlockSpec(memory_space=pl.ANY)
HBM = pl.BlockSpec(memory_space=pltpu.HBM)
SEM = pl.BlockSpec(memory_space=pltpu.SEMAPHORE)
SIDE_EFFECT = pltpu.SideEffectType.DATAFLOW_SIDE_EFFECTING

_DIMS = {
    "nt": (((1,), (1,)), ((), ())),
    "nn": (((1,), (0,)), ((), ())),
    "tn": (((0,), (0,)), ((), ())),
}


def _pick(dim, pref, mult=LANES):
    if dim <= pref:
        return dim
    for t in range(pref - pref % mult, 0, -mult):
        if dim % t == 0:
            return t
    return dim


def _params(sem):
    return pltpu.CompilerParams(dimension_semantics=sem, vmem_limit_bytes=VMEM_LIMIT_V7X)


def _tile(tm, tn, coff=0):
    return pl.BlockSpec((tm, tn), lambda i, j: (i, j + coff))


def _rowvec(tn, coff=0):
    return pl.BlockSpec((1, tn), lambda i, j: (0, j + coff))


def _out(m, n, dtype):
    return jax.ShapeDtypeStruct((m, n), dtype)


def _mm(name, form, a_list, b_list, groups, m, n, tm, tn, extras, epilogue, outs, after=None, sub=SUB_ROWS):
    na, nb, ne = len(a_list), len(b_list), len(extras)
    pins = [] if after is None else [after]
    step = tm if (sub is None or form == "tn" or tm % sub) else sub

    def a_spec(a):
        if form == "tn":
            return pl.BlockSpec((a.shape[0], tm), lambda i, j: (0, i))
        return pl.BlockSpec((tm, a.shape[1]), lambda i, j: (i, 0))

    def b_spec(b):
        if form == "nt":
            return pl.BlockSpec((tn, b.shape[1]), lambda i, j: (j, 0))
        return pl.BlockSpec((b.shape[0], tn), lambda i, j: (0, j))

    def body(*refs):
        a_refs, b_refs = refs[:na], refs[na:na + nb]
        e_refs, o_refs = refs[na + nb:na + nb + ne], refs[na + nb + ne + len(pins):]
        b_vals = {}
        for s0 in range(0, tm, step):
            rows = slice(None) if step == tm else pl.ds(s0, step)
            a_vals, accs = {}, []
            for group in groups:
                acc = None
                for ai, bi in group:
                    if ai not in a_vals:
                        a_vals[ai] = (a_refs[ai][...] if form == "tn" else a_refs[ai][rows, :]).astype(BF16)
                    if bi not in b_vals:
                        b_vals[bi] = b_refs[bi][...].astype(BF16)
                    d = lax.dot_general(a_vals[ai], b_vals[bi], _DIMS[form], preferred_element_type=F32)
                    acc = d if acc is None else acc + d
                accs.append(acc)
            res = epilogue(accs, *[e[rows, :] if e.shape[0] == tm else e[...] for e in e_refs])
            for o_ref, r in zip(o_refs, res):
                o_ref[rows, :] = r.astype(o_ref.dtype)

    out_specs = [_tile(tm, tn) if s is None else s for _, s in outs]
    res = pl.pallas_call(
        body, name=name, grid=(m // tm, n // tn),
        in_specs=[a_spec(a) for a in a_list] + [b_spec(b) for b in b_list] + [s for _, s in extras] + [ANY] * len(pins),
        out_specs=out_specs, out_shape=[o for o, _ in outs],
        compiler_params=_params(("parallel", "parallel")),
    )(*a_list, *b_list, *[e for e, _ in extras], *pins)
    return res


def _mm1(name, form, a, b, m, n, tm, tn, dtype, scale=None):
    epi = (lambda accs: (accs[0],)) if scale is None else (lambda accs: (accs[0] * scale,))
    return _mm(name, form, [a], [b], [[(0, 0)]], m, n, tm, tn, [], epi, [(_out(m, n, dtype), None)])[0]


def _rms_fwd(name, h, g):
    t, d = h.shape
    tm = _pick(t, 512, SUBLANES)

    def body(h_ref, g_ref, n_ref):
        hv = h_ref[...]
        r = lax.rsqrt(jnp.mean(hv * hv, axis=-1, keepdims=True) + EPS)
        n_ref[...] = ((hv * r) * g_ref[...]).astype(BF16)

    return pl.pallas_call(
        body, name=name, grid=(t // tm,),
        in_specs=[pl.BlockSpec((tm, d), lambda i: (i, 0)), pl.BlockSpec((1, d), lambda i: (0, 0))],
        out_specs=pl.BlockSpec((tm, d), lambda i: (i, 0)), out_shape=_out(t, d, BF16),
        compiler_params=_params(("parallel",)),
    )(h, g)


def _rms_bwd(name, h, g, dn, dres=None):
    t, d = h.shape
    tm = _pick(t, 512, SUBLANES)
    need_dh = dres is not None

    def body(*refs):
        if need_dh:
            h_ref, g_ref, dn_ref, dres_ref, dh_ref, dhb_ref, dg_ref = refs
        else:
            h_ref, g_ref, dn_ref, dg_ref = refs
        hv = h_ref[...]
        r = lax.rsqrt(jnp.mean(hv * hv, axis=-1, keepdims=True) + EPS)
        nh = hv * r
        dnv = dn_ref[...].astype(F32)

        @pl.when(pl.program_id(0) == 0)
        def _():
            dg_ref[...] = jnp.zeros_like(dg_ref)

        dg_ref[...] += jnp.sum(dnv * nh, axis=0, keepdims=True)
        if need_dh:
            dng = dnv * g_ref[...]
            dh = dres_ref[...] + r * (dng - nh * jnp.mean(dng * nh, axis=-1, keepdims=True))
            dh_ref[...] = dh
            dhb_ref[...] = dh.astype(BF16)

    row = pl.BlockSpec((tm, d), lambda i: (i, 0))
    vec = pl.BlockSpec((1, d), lambda i: (0, 0))
    if need_dh:
        return pl.pallas_call(
            body, name=name, grid=(t // tm,), in_specs=[row, vec, row, row], out_specs=[row, row, vec],
            out_shape=[_out(t, d, F32), _out(t, d, BF16), _out(1, d, F32)], compiler_params=_params(("arbitrary",)),
        )(h, g, dn, dres)
    return pl.pallas_call(
        body, name=name, grid=(t // tm,), in_specs=[row, vec, row], out_specs=vec,
        out_shape=_out(1, d, F32), compiler_params=_params(("arbitrary",)),
    )(h, g, dn)


def _loss_head(h, g, tgt):
    t, d = h.shape
    tm = _pick(t, 512, SUBLANES)

    def body(h_ref, g_ref, t_ref, dh_ref, dhb_ref, dg_ref, loss_ref):
        hv = h_ref[...]
        r = lax.rsqrt(jnp.mean(hv * hv, axis=-1, keepdims=True) + EPS)
        nh = hv * r
        err = nh * g_ref[...] - t_ref[...]

        @pl.when(pl.program_id(0) == 0)
        def _():
            dg_ref[...] = jnp.zeros_like(dg_ref)
            loss_ref[...] = jnp.zeros_like(loss_ref)

        per_row = jnp.mean(err * err, axis=-1, keepdims=True)
        loss_ref[...] += 0.5 * jnp.sum(per_row, axis=0, keepdims=True)
        dy = err * (1.0 / d)
        dg_ref[...] += jnp.sum(dy * nh, axis=0, keepdims=True)
        dng = dy * g_ref[...]
        dh = r * (dng - nh * jnp.mean(dng * nh, axis=-1, keepdims=True))
        dh_ref[...] = dh
        dhb_ref[...] = dh.astype(BF16)

    row = pl.BlockSpec((tm, d), lambda i: (i, 0))
    vec = pl.BlockSpec((1, d), lambda i: (0, 0))
    return pl.pallas_call(
        body, name="loss_head", grid=(t // tm,), in_specs=[row, vec, row],
        out_specs=[row, row, vec, pl.BlockSpec((1, LANES), lambda i: (0, 0))],
        out_shape=[_out(t, d, F32), _out(t, d, BF16), _out(1, d, F32), _out(1, LANES, F32)],
        compiler_params=_params(("arbitrary",)),
    )(h, g, tgt)


def _ffn_fwd(tag, h, n, wg_t, wu_t, wd):
    t, d = h.shape
    f = wg_t.shape[0]
    tm, tn = _pick(t, 1024), _pick(f, 1408)

    def up_epi(accs):
        a, b = accs
        return a, b, (a * jax.nn.sigmoid(a)) * b

    a, b, hid = _mm(tag + "_up", "nt", [n], [wg_t, wu_t], [[(0, 0)], [(0, 1)]], t, f, tm, tn, [], up_epi,
                    [(_out(t, f, BF16), None)] * 3)
    if callable(wd):
        wd = wd(hid)
    tm2, tn2 = _pick(t, 1024), _pick(d, 512)
    h_out = _mm(tag + "_down", "nn", [hid], [wd], [[(0, 0)]], t, d, tm2, tn2, [(h, _tile(tm2, tn2))],
                lambda accs, hin: (hin + 0.5 * accs[0],), [(_out(t, d, F32), None)])[0]
    return h_out, (n, a, b, hid)


def _ffn_bwd(tag, h, g, wg_t, wu_t, wd, saved, dh, dh_bf, weights_done=None, after=None):
    n, a, b, hid = saved
    t, d = h.shape
    f = wd.shape[0]
    tm, tn = _pick(t, 1024), _pick(f, 1408)

    def hid_epi(accs, av, bv):
        dhid = 0.5 * accs[0]
        av, bv = av.astype(F32), bv.astype(F32)
        sig = jax.nn.sigmoid(av)
        da = dhid * bv * (sig * (1.0 + av * (1.0 - sig)))
        db = dhid * (av * sig)
        return da, db

    da, db = _mm(tag + "_bwd_hid", "nt", [dh_bf], [wd], [[(0, 0)]], t, f, tm, tn,
                 [(a, _tile(tm, tn)), (b, _tile(tm, tn))], hid_epi, [(_out(t, f, BF16), None)] * 2, after=after)
    tw, tnw = _pick(f, 1408), _pick(d, 512)
    d_wd = _mm1(tag + "_dwd", "tn", hid, dh_bf, f, d, tw, tnw, BF16, scale=0.5)
    d_wg = _mm1(tag + "_dwg", "tn", da, n, f, d, tw, tnw, BF16)
    d_wu = _mm1(tag + "_dwu", "tn", db, n, f, d, tw, tnw, BF16)
    pin = weights_done(d_wg, d_wu, d_wd) if weights_done is not None else None
    tm2, tn2 = _pick(t, 1024), _pick(d, 512)
    dn = _mm(tag + "_dn", "nn", [da, db], [wg_t, wu_t], [[(0, 0), (1, 1)]], t, d, tm2, tn2, [],
             lambda accs: (accs[0],), [(_out(t, d, F32), None)], after=pin)[0]
    dh_in, dh_in_bf, dg = _rms_bwd(tag + "_norm_bwd", h, g, dn, dh)
    return dh_in, dh_in_bf, dg, d_wg, d_wu, d_wd


def _window_sum(win, offsets):
    n = win.shape[0]
    acc = None
    for j in offsets:
        term = win if j == 0 else pltpu.roll(win, (-j) % n, 0)
        acc = term if acc is None else acc + term
    return acc


def _pool_counts(r0, ch, c, left, right, t):
    pos = r0 + lax.broadcasted_iota(jnp.int32, (ch, c), 0)
    return (jnp.minimum(pos + right + 1, t) - jnp.maximum(pos - left, 0)).astype(F32)


def _pool_fwd(proj, pool_w_bf, pool_scale):
    t = proj.shape[0]
    ng, c, _ = pool_w_bf.shape
    ch = _pick(t, 256, SUBLANES)
    pad = POOL_PAD

    def body(p_ref, w_ref, s_ref, pooled_ref, pm_ref, buf):
        grp = pl.program_id(0)
        buf[pl.ds(0, pad), :] = jnp.zeros((pad, c), F32)
        buf[pl.ds(pad + t, pad), :] = jnp.zeros((pad, c), F32)

        def fill(ci, carry):
            r0 = pl.multiple_of(ci * ch, SUBLANES)
            buf[pl.ds(pl.multiple_of(r0 + pad, SUBLANES), ch), :] = p_ref[pl.ds(r0, ch), :]
            return carry

        lax.fori_loop(0, t // ch, fill, 0)
        for gi, w in enumerate(POOL_WINDOWS):
            left = w // 2
            right = w - 1 - left

            @pl.when(grp == gi)
            def _(left=left, right=right):
                def chunk(ci, carry):
                    r0 = pl.multiple_of(ci * ch, SUBLANES)
                    win = buf[pl.ds(r0, ch + 2 * pad), :]
                    s = _window_sum(win, range(-left, right + 1))[pad:pad + ch]
                    pooled = s / _pool_counts(r0, ch, c, left, right, t) - win[pad:pad + ch]
                    pooled_bf = pooled.astype(BF16)
                    mixed = jnp.dot(pooled_bf, w_ref[0], preferred_element_type=F32)
                    pooled_ref[pl.ds(r0, ch), :] = pooled_bf
                    pm_ref[pl.ds(r0, ch), :] = (mixed * s_ref[...]).astype(BF16)
                    return carry

                lax.fori_loop(0, t // ch, chunk, 0)

    col = pl.BlockSpec((t, c), lambda g: (0, g))
    return pl.pallas_call(
        body, name="pool_fwd", grid=(ng,),
        in_specs=[col, pl.BlockSpec((1, c, c), lambda g: (g, 0, 0)), pl.BlockSpec((1, c), lambda g: (0, g))],
        out_specs=[col, col], out_shape=[_out(t, ng * c, BF16), _out(t, ng * c, BF16)],
        scratch_shapes=[pltpu.VMEM((t + 2 * pad, c), F32)],
        compiler_params=_params(("parallel",)),
    )(proj, pool_w_bf, pool_scale)


def _pool_bwd(pooled, dpm, pool_w_bf, pool_scale):
    t = pooled.shape[0]
    ng, c, _ = pool_w_bf.shape
    ch = _pick(t, 256, SUBLANES)
    pad = POOL_PAD

    def body(pooled_ref, dpm_ref, w_ref, s_ref, dp_ref, dw_ref, ds_ref, buf, raw):
        grp = pl.program_id(0)
        buf[pl.ds(0, pad), :] = jnp.zeros((pad, c), F32)
        buf[pl.ds(pad + t, pad), :] = jnp.zeros((pad, c), F32)
        dw_ref[...] = jnp.zeros_like(dw_ref)
        ds_ref[...] = jnp.zeros_like(ds_ref)
        for gi, w in enumerate(POOL_WINDOWS):
            left = w // 2
            right = w - 1 - left

            @pl.when(grp == gi)
            def _(left=left, right=right):
                def first(ci, carry):
                    r0 = pl.multiple_of(ci * ch, SUBLANES)
                    pv = pooled_ref[pl.ds(r0, ch), :]
                    dpm_v = dpm_ref[pl.ds(r0, ch), :]
                    mixed = jnp.dot(pv, w_ref[0], preferred_element_type=F32)
                    ds_ref[...] += jnp.sum(dpm_v * mixed, axis=0, keepdims=True)
                    dmixed = (dpm_v * s_ref[...]).astype(BF16)
                    dw_ref[0] += lax.dot_general(pv, dmixed, _DIMS["tn"], preferred_element_type=F32)
                    dpooled = lax.dot_general(dmixed, w_ref[0], _DIMS["nt"], preferred_element_type=F32)
                    raw[pl.ds(r0, ch), :] = dpooled
                    buf[pl.ds(pl.multiple_of(r0 + pad, SUBLANES), ch), :] = (
                        dpooled / _pool_counts(r0, ch, c, left, right, t))
                    return carry

                lax.fori_loop(0, t // ch, first, 0)

                def second(ci, carry):
                    r0 = pl.multiple_of(ci * ch, SUBLANES)
                    win = buf[pl.ds(r0, ch + 2 * pad), :]
                    s = _window_sum(win, range(-right, left + 1))[pad:pad + ch]
                    dp_ref[pl.ds(r0, ch), :] = (s - raw[pl.ds(r0, ch), :]).astype(BF16)
                    return carry

                lax.fori_loop(0, t // ch, second, 0)

    col = pl.BlockSpec((t, c), lambda g: (0, g))
    return pl.pallas_call(
        body, name="pool_bwd", grid=(ng,),
        in_specs=[col, col, pl.BlockSpec((1, c, c), lambda g: (g, 0, 0)), pl.BlockSpec((1, c), lambda g: (0, g))],
        out_specs=[col, pl.BlockSpec((1, c, c), lambda g: (g, 0, 0)), pl.BlockSpec((1, c), lambda g: (0, g))],
        out_shape=[_out(t, ng * c, BF16), jax.ShapeDtypeStruct((ng, c, c), F32), _out(1, ng * c, F32)],
        scratch_shapes=[pltpu.VMEM((t + 2 * pad, c), F32), pltpu.VMEM((t, c), F32)],
        compiler_params=_params(("parallel",)),
    )(pooled, dpm, pool_w_bf, pool_scale)


def _discretise(a_re, a_im, log_dt, b_re, b_im):
    dt = jnp.exp(log_dt)
    mag = jnp.exp(dt * a_re)
    ang = dt * a_im
    abr = mag * jnp.cos(ang)
    abi = mag * jnp.sin(ang)
    den = a_re * a_re + a_im * a_im
    nr = abr - 1.0
    qr = (nr * a_re + abi * a_im) / den
    qi = (abi * a_re - nr * a_im) / den
    return abr, abi, qr * b_re - qi * b_im, qr * b_im + qi * b_re


def _ssm_disc(args):
    def body(ar, ai, ld, br, bi, o1, o2, o3, o4):
        res = _discretise(ar[...], ai[...], ld[...], br[...], bi[...])
        for o, r in zip((o1, o2, o3, o4), res):
            o[...] = r

    like = lambda a: jax.ShapeDtypeStruct(a.shape, F32)
    return pl.pallas_call(
        body, name="ssm_disc", out_shape=[like(args[0]), like(args[0]), like(args[3]), like(args[3])],
    )(*args)


def _ssm_disc_bwd(args, cots):
    def body(ar, ai, ld, br, bi, c1, c2, c3, c4, o1, o2, o3, o4, o5):
        _, vjp = jax.vjp(_discretise, ar[...], ai[...], ld[...], br[...], bi[...])
        res = vjp((c1[...], c2[...], c3[...], c4[...]))
        for o, r in zip((o1, o2, o3, o4, o5), res):
            o[...] = r

    return pl.pallas_call(
        body, name="ssm_disc_bwd", out_shape=[jax.ShapeDtypeStruct(a.shape, F32) for a in args],
    )(*args, *cots)


def _cmul(pr, pi, qr, qi):
    return pr * qr - pi * qi, pr * qi + pi * qr


def _cpow(pr, pi, n):
    rr, ri = None, None
    while n:
        if n & 1:
            rr, ri = (pr, pi) if rr is None else _cmul(rr, ri, pr, pi)
        n >>= 1
        if n:
            pr, pi = _cmul(pr, pi, pr, pi)
    return rr, ri


def _segment_carry(er, ei, pr, pi, reverse):
    row = lax.broadcasted_iota(jnp.int32, er.shape, 0)
    cr, ci = jnp.zeros_like(er), jnp.zeros_like(ei)
    for _ in range(SUBLANES - 1):
        tr = er + pr * cr - pi * ci
        ti = ei + pr * ci + pi * cr
        if reverse:
            keep, shift = row < SUBLANES - 1, SUBLANES - 1
        else:
            keep, shift = row >= 1, 1
        cr = jnp.where(keep, pltpu.roll(tr, shift, 0), 0.0)
        ci = jnp.where(keep, pltpu.roll(ti, shift, 0), 0.0)
    return cr, ci


def _ssm_fwd(name, sp, b_re, b_im, c_re, c_im, ar, ai, reverse):
    t, c = sp.shape
    s = ar.shape[1]
    w = _pick(s, 512)
    ch = _pick(t, 512, SUBLANES)
    n_ch, gpc, steps = t // ch, ch // SUBLANES, t // SUBLANES

    def body(sp_ref, bre_ref, bim_ref, cre_ref, cim_ref, ar_ref, ai_ref, xr_ref, xi_ref, y_ref, ur, ui, xbr, xbi):
        a_r = jnp.broadcast_to(ar_ref[...], (SUBLANES, w))
        a_i = jnp.broadcast_to(ai_ref[...], (SUBLANES, w))

        @pl.when(pl.program_id(0) == 0)
        def _():
            y_ref[...] = jnp.zeros_like(y_ref)

        def sweep(h0, store):
            def chunk(k, h):
                ci = n_ch - 1 - k if reverse else k
                rows = pl.ds(pl.multiple_of(ci * ch, ch), ch)
                spv = sp_ref[rows, :].astype(BF16)
                ur[...] = jnp.dot(spv, bre_ref[...], preferred_element_type=F32)
                ui[...] = jnp.dot(spv, bim_ref[...], preferred_element_type=F32)

                def group(g, hh):
                    gi = gpc - 1 - g if reverse else g
                    r0 = pl.multiple_of(gi * SUBLANES, SUBLANES)
                    hr, hi = hh
                    nr = a_r * hr - a_i * hi + ur[pl.ds(r0, SUBLANES), :]
                    ni = a_r * hi + a_i * hr + ui[pl.ds(r0, SUBLANES), :]
                    if store:
                        xbr[pl.ds(r0, SUBLANES), :] = nr
                        xbi[pl.ds(r0, SUBLANES), :] = ni
                    return nr, ni

                h = lax.fori_loop(0, gpc, group, h)
                if store:
                    xr16, xi16 = xbr[...].astype(BF16), xbi[...].astype(BF16)
                    xr_ref[rows, :] = xr16
                    xi_ref[rows, :] = xi16
                    y_ref[rows, :] += (lax.dot_general(xr16, cre_ref[...], _DIMS["nt"], preferred_element_type=F32)
                                       + lax.dot_general(xi16, cim_ref[...], _DIMS["nt"], preferred_element_type=F32))
                return h

            return lax.fori_loop(0, n_ch, chunk, h0)

        zero = jnp.zeros((SUBLANES, w), F32)
        er, ei = sweep((zero, zero), False)
        pr, pi = _cpow(ar_ref[...], ai_ref[...], steps)
        sweep(_segment_carry(er, ei, pr, pi, reverse), True)

    col = lambda i: (0, i)
    return pl.pallas_call(
        body, name=name, grid=(s // w,),
        in_specs=[pl.BlockSpec((t, c), lambda i: (0, 0))] + [pl.BlockSpec((c, w), col)] * 4
        + [pl.BlockSpec((1, w), col)] * 2,
        out_specs=[pl.BlockSpec((t, w), col), pl.BlockSpec((t, w), col), pl.BlockSpec((t, c), lambda i: (0, 0))],
        out_shape=[_out(t, s, BF16), _out(t, s, BF16), _out(t, c, F32)],
        scratch_shapes=[pltpu.VMEM((ch, w), F32)] * 4,
        compiler_params=_params(("arbitrary",)),
    )(sp, b_re, b_im, c_re, c_im, ar, ai)


def _ssm_bwd(name, dyp, c_re, c_im, xr, xi, ar, ai, reverse):
    t, c = dyp.shape
    s = ar.shape[1]
    w = _pick(s, 512)
    ch = _pick(t, 512, SUBLANES)
    n_ch, gpc, steps = t // ch, ch // SUBLANES, t // SUBLANES
    back = not reverse
    edge = 2 * SUBLANES

    def body(dy_ref, cre_ref, cim_ref, xr_ref, xi_ref, ar_ref, ai_ref, lr_ref, li_ref, dar_ref, dai_ref,
             gr, gi_, lbr, lbi, xbr, xbi):
        a_r = jnp.broadcast_to(ar_ref[...], (SUBLANES, w))
        a_i = -jnp.broadcast_to(ai_ref[...], (SUBLANES, w))
        row = lax.broadcasted_iota(jnp.int32, (SUBLANES, w), 0)

        def neighbours(ci, x_ref, buf):
            rows = pl.ds(pl.multiple_of(ci * ch, ch), ch)
            if reverse:
                buf[pl.ds(0, ch), :] = x_ref[rows, :].astype(F32)
                nxt = x_ref[pl.ds(pl.multiple_of(jnp.minimum(ci + 1, n_ch - 1) * ch, ch), edge), :].astype(F32)[:SUBLANES]
                first = x_ref[pl.ds(0, edge), :].astype(F32)[:SUBLANES]
                wrap = jnp.where(row < SUBLANES - 1, pltpu.roll(first, SUBLANES - 1, 0), 0.0)
                buf[pl.ds(ch, SUBLANES), :] = jnp.where(ci == n_ch - 1, wrap, nxt)
            else:
                buf[pl.ds(SUBLANES, ch), :] = x_ref[rows, :].astype(F32)
                prv = x_ref[pl.ds(pl.multiple_of(jnp.maximum(ci * ch - edge, 0), edge), edge), :].astype(F32)[SUBLANES:]
                last = x_ref[pl.ds(t - edge, edge), :].astype(F32)[SUBLANES:]
                wrap = jnp.where(row >= 1, pltpu.roll(last, 1, 0), 0.0)
                buf[pl.ds(0, SUBLANES), :] = jnp.where(ci == 0, wrap, prv)

        def sweep(h0, store):
            def chunk(k, carry):
                ci = n_ch - 1 - k if back else k
                rows = pl.ds(pl.multiple_of(ci * ch, ch), ch)
                dyv = dy_ref[rows, :].astype(BF16)
                gr[...] = jnp.dot(dyv, cre_ref[...], preferred_element_type=F32)
                gi_[...] = jnp.dot(dyv, cim_ref[...], preferred_element_type=F32)
                if store:
                    neighbours(ci, xr_ref, xbr)
                    neighbours(ci, xi_ref, xbi)

                def group(g, cc):
                    gidx = gpc - 1 - g if back else g
                    r0 = pl.multiple_of(gidx * SUBLANES, SUBLANES)
                    hr, hi = cc[0], cc[1]
                    nr = a_r * hr - a_i * hi + gr[pl.ds(r0, SUBLANES), :]
                    ni = a_r * hi + a_i * hr + gi_[pl.ds(r0, SUBLANES), :]
                    if not store:
                        return nr, ni
                    lbr[pl.ds(r0, SUBLANES), :] = nr
                    lbi[pl.ds(r0, SUBLANES), :] = ni
                    x0 = pl.multiple_of(r0 + SUBLANES, SUBLANES) if reverse else r0
                    xpr, xpi = xbr[pl.ds(x0, SUBLANES), :], xbi[pl.ds(x0, SUBLANES), :]
                    return nr, ni, cc[2] + nr * xpr + ni * xpi, cc[3] + ni * xpr - nr * xpi

                carry = lax.fori_loop(0, gpc, group, carry)
                if store:
                    lr_ref[rows, :] = lbr[...].astype(BF16)
                    li_ref[rows, :] = lbi[...].astype(BF16)
                return carry

            return lax.fori_loop(0, n_ch, chunk, h0)

        zero = jnp.zeros((SUBLANES, w), F32)
        er, ei = sweep((zero, zero), False)
        pr, pi = _cpow(ar_ref[...], -ai_ref[...], steps)
        cr, ci0 = _segment_carry(er, ei, pr, pi, back)
        _, _, dar, dai = sweep((cr, ci0, zero, zero), True)
        dar_ref[...] = jnp.sum(dar, axis=0, keepdims=True)
        dai_ref[...] = jnp.sum(dai, axis=0, keepdims=True)

    col = lambda i: (0, i)
    return pl.pallas_call(
        body, name=name, grid=(s // w,),
        in_specs=[pl.BlockSpec((t, c), lambda i: (0, 0)), pl.BlockSpec((c, w), col), pl.BlockSpec((c, w), col),
                  pl.BlockSpec((t, w), col), pl.BlockSpec((t, w), col), pl.BlockSpec((1, w), col), pl.BlockSpec((1, w), col)],
        out_specs=[pl.BlockSpec((t, w), col), pl.BlockSpec((t, w), col), pl.BlockSpec((1, w), col), pl.BlockSpec((1, w), col)],
        out_shape=[_out(t, s, BF16), _out(t, s, BF16), _out(1, s, F32), _out(1, s, F32)],
        scratch_shapes=[pltpu.VMEM((ch, w), F32)] * 4 + [pltpu.VMEM((ch + SUBLANES, w), F32)] * 2,
        compiler_params=_params(("parallel",)),
    )(dyp, c_re, c_im, xr, xi, ar, ai)


def _ssm_finish(y0, y1, sp, skip):
    t, c = sp.shape
    steps = t // SUBLANES
    w = _pick(c, LANES)

    def body(y0_ref, y1_ref, sp_ref, d_ref, y_ref, ys_ref):
        rows = pl.ds(pl.program_id(1), steps, stride=SUBLANES)
        y = y0_ref[rows, :] + y1_ref[rows, :] + sp_ref[rows, :] * d_ref[...]
        y_ref[...] = y
        ys_ref[...] = jax.nn.gelu(y).astype(BF16)

    whole = pl.BlockSpec((t, w), lambda j, k: (0, j))
    seg = pl.BlockSpec((steps, w), lambda j, k: (k, j))
    return pl.pallas_call(
        body, name="ssm_finish", grid=(c // w, SUBLANES),
        in_specs=[whole, whole, whole, pl.BlockSpec((1, w), lambda j, k: (0, j))], out_specs=[seg, seg],
        out_shape=[_out(t, c, F32), _out(t, c, BF16)], compiler_params=_params(("parallel", "arbitrary")),
    )(y0, y1, sp, skip)


def _to_segments(a):
    t, c = a.shape
    return a.reshape(SUBLANES, t // SUBLANES, c).transpose(1, 0, 2).reshape(t, c)


def _from_segments(a):
    t, c = a.shape
    return a.reshape(t // SUBLANES, SUBLANES, c).transpose(1, 0, 2).reshape(t, c)


def _colsum_prod(name, a, b, b_coff=0):
    t, n = a.shape
    tm = _pick(t, 512, SUBLANES)

    def body(a_ref, b_ref, o_ref):
        @pl.when(pl.program_id(0) == 0)
        def _():
            o_ref[...] = jnp.zeros_like(o_ref)

        o_ref[...] += jnp.sum(a_ref[...].astype(F32) * b_ref[...].astype(F32), axis=0, keepdims=True)

    return pl.pallas_call(
        body, name=name, grid=(t // tm,),
        in_specs=[pl.BlockSpec((tm, n), lambda i: (i, 0)), pl.BlockSpec((tm, n), lambda i: (i, b_coff))],
        out_specs=pl.BlockSpec((1, n), lambda i: (0, 0)), out_shape=_out(1, n, F32),
        compiler_params=_params(("arbitrary",)),
    )(a, b)


def _bd(blk):
    g, hh, p = blk.shape
    eye = jnp.eye(g, dtype=bool)[:, None, :, None]
    return jnp.where(eye, blk[:, :, None, :], 0.0).reshape(g * hh, g * p)


def _softmax(qh, kh, scale):
    s = lax.dot_general(qh, kh, _DIMS["nt"], preferred_element_type=F32) * scale
    e = jnp.exp(s - jnp.max(s, axis=-1, keepdims=True))
    return e / jnp.sum(e, axis=-1, keepdims=True)


def _attn_fwd(q, kv):
    t, d = q.shape
    mm_ = kv.shape[0]
    hd = d // N_XHEADS
    scale = 1.0 / math.sqrt(hd)
    tm = _pick(t, 512, SUBLANES)

    def body(q_ref, kv_ref, o_ref):
        for h in range(N_XHEADS):
            sl = pl.ds(h * hd, hd)
            p = _softmax(q_ref[:, sl], kv_ref[:, sl], scale)
            o_ref[:, sl] = jnp.dot(p.astype(BF16), kv_ref[:, pl.ds(d + h * hd, hd)],
                                   preferred_element_type=F32).astype(BF16)

    return pl.pallas_call(
        body, name="attn_fwd", grid=(t // tm,),
        in_specs=[pl.BlockSpec((tm, d), lambda i: (i, 0)), pl.BlockSpec((mm_, 2 * d), lambda i: (0, 0))],
        out_specs=pl.BlockSpec((tm, d), lambda i: (i, 0)), out_shape=_out(t, d, BF16),
        compiler_params=_params(("parallel",)),
    )(q, kv)


def _attn_bwd(q, kv, do):
    t, d = q.shape
    mm_ = kv.shape[0]
    hd = d // N_XHEADS
    scale = 1.0 / math.sqrt(hd)
    tm = _pick(t, 512, SUBLANES)

    def body(q_ref, kv_ref, do_ref, dq_ref, dkv_ref):
        @pl.when(pl.program_id(0) == 0)
        def _():
            dkv_ref[...] = jnp.zeros_like(dkv_ref)

        for h in range(N_XHEADS):
            sl = pl.ds(h * hd, hd)
            vsl = pl.ds(d + h * hd, hd)
            qh, kh, doh = q_ref[:, sl], kv_ref[:, sl], do_ref[:, sl]
            p = _softmax(qh, kh, scale)
            dp = lax.dot_general(doh, kv_ref[:, vsl], _DIMS["nt"], preferred_element_type=F32)
            dkv_ref[:, vsl] += lax.dot_general(p.astype(BF16), doh, _DIMS["tn"], preferred_element_type=F32)
            ds = (p * (dp - jnp.sum(dp * p, axis=-1, keepdims=True)) * scale).astype(BF16)
            dq_ref[:, sl] = jnp.dot(ds, kh, preferred_element_type=F32).astype(BF16)
            dkv_ref[:, sl] += lax.dot_general(ds, qh, _DIMS["tn"], preferred_element_type=F32)

    row = pl.BlockSpec((tm, d), lambda i: (i, 0))
    full = pl.BlockSpec((mm_, 2 * d), lambda i: (0, 0))
    return pl.pallas_call(
        body, name="attn_bwd", grid=(t // tm,), in_specs=[row, full, row], out_specs=[row, full],
        out_shape=[_out(t, d, BF16), _out(mm_, 2 * d, F32)], compiler_params=_params(("arbitrary",)),
    )(q, kv, do)


def _ew(name, fn, ins, outs, rows_pref=256):
    r, c = ins[0].shape
    tr = _pick(r, rows_pref, SUBLANES)
    ni = len(ins)

    def body(*refs):
        res = fn(*[x[...] for x in refs[:ni]])
        for o_ref, v in zip(refs[ni:], res):
            o_ref[...] = v.astype(o_ref.dtype)

    blk = pl.BlockSpec((tr, c), lambda i: (i, 0))
    return pl.pallas_call(
        body, name=name, grid=(r // tr,), in_specs=[blk] * ni, out_specs=[blk] * len(outs),
        out_shape=[_out(r, c, dt) for dt in outs], compiler_params=_params(("parallel",)),
    )(*ins)


def _sum_slots(name, a, dtype):
    s, r, c = a.shape
    tr = _pick(r, 256, SUBLANES)

    def body(a_ref, o_ref):
        acc = a_ref[0].astype(F32)
        for k in range(1, s):
            acc = acc + a_ref[k].astype(F32)
        o_ref[...] = acc.astype(o_ref.dtype)

    return pl.pallas_call(
        body, name=name, grid=(r // tr,), in_specs=[pl.BlockSpec((s, tr, c), lambda i: (0, i, 0))],
        out_specs=pl.BlockSpec((tr, c), lambda i: (i, 0)), out_shape=_out(r, c, dtype),
        compiler_params=_params(("parallel",)),
    )(a)


def _adamw_step(wv, gv, mv, vv):
    bc1 = 1.0 - ADAM_B1 ** ADAM_STEP
    bc2 = 1.0 - ADAM_B2 ** ADAM_STEP
    m2 = ADAM_B1 * mv + (1.0 - ADAM_B1) * gv
    v2 = ADAM_B2 * vv + (1.0 - ADAM_B2) * (gv * gv)
    delta = -ADAM_LR * ((m2 / bc1) / (jnp.sqrt(v2 / bc2) + ADAM_EPS) + ADAM_WD * wv)
    return delta, m2, v2


def _adamw_group(name, items, transposed):
    k, r = items[0][0].shape
    if transposed and r % LANES != 0:
        rows = _adamw_group(name, [(w.T, g, m.T, v.T) for w, g, m, v in items], False)
        return [[a.T for a in item] for item in rows]
    tk = _pick(k, max(SUBLANES, ADAMW_STEP_WORDS // (r * len(items))), SUBLANES)
    n_out = 4 if transposed else 3

    def body(*refs):
        ins, outs = refs[:4 * len(items)], refs[4 * len(items):]
        for i in range(len(items)):
            wv, gv, mv, vv = (a[...] for a in ins[4 * i:4 * i + 4])
            if transposed:
                gv = gv.T
            res = _adamw_step(wv, gv, mv, vv) + ((gv,) if transposed else ())
            for o_ref, val in zip(outs[n_out * i:n_out * (i + 1)], res):
                o_ref[...] = val

    blk = pl.BlockSpec((tk, r), lambda j: (j, 0))
    g_blk = pl.BlockSpec((r, tk), lambda j: (0, j)) if transposed else blk
    res = pl.pallas_call(
        body, name=name, grid=(k // tk,), in_specs=[blk, g_blk, blk, blk] * len(items),
        out_specs=[blk] * (n_out * len(items)), out_shape=[pltpu.HBM((k, r), F32)] * (n_out * len(items)),
        compiler_params=_params(("parallel",)),
    )(*[pltpu.with_memory_space_constraint(a, pltpu.HBM) for item in items for a in item])
    return [list(res[n_out * i:n_out * (i + 1)]) + ([] if transposed else [items[i][1]]) for i in range(len(items))]


def _allgather(name, arrs):
    n = len(arrs)

    def body(*refs):
        ins, outs = refs[:n], refs[n:2 * n]
        send_sems, recv_sems, local_sems = refs[2 * n:]
        x, y, c = lax.axis_index("x"), lax.axis_index("y"), lax.axis_index("c")
        me, sibling = (x, y, c), (x, y, 1 - c)
        chips = [(1 - x, y), (x, 1 - y), (1 - x, 1 - y)]

        def rows(a, px, py, pc):
            r = ins[a].shape[0]
            return outs[a].at[pl.ds((4 * px + 2 * py + pc) * r, r), :]

        def copy(a, k, block, to, src=None):
            return pltpu.make_async_remote_copy(
                src_ref=rows(a, *block) if src is None else src, dst_ref=rows(a, *block),
                send_sem=send_sems.at[a, k], recv_sem=recv_sems.at[a, k], device_id=to, device_id_type=MESH)

        mine = [pltpu.make_async_copy(ins[a], rows(a, *me), local_sems.at[a]) for a in range(n)]
        for cp in mine:
            cp.start()
        first = []
        for a in range(n):
            first.append(copy(a, 0, me, sibling, src=ins[a]))
            first += [copy(a, 1 + j, me, (*chip, c), src=ins[a]) for j, chip in enumerate(chips)]
        for cp in first:
            cp.start()
        passed = []
        for j, chip in enumerate(chips):
            for a in range(n):
                copy(a, 1 + j, (*chip, c), me).wait_recv()
                cp = copy(a, 4 + j, (*chip, c), sibling)
                cp.start()
                passed.append(cp)
        for a in range(n):
            copy(a, 0, sibling, me).wait_recv()
            for j, chip in enumerate(chips):
                copy(a, 4 + j, (*chip, 1 - c), me).wait_recv()
        for cp in first + passed:
            cp.wait_send()
        for cp in mine:
            cp.wait()

    return pl.pallas_call(
        body, name=name, in_specs=[ANY] * n, out_specs=[ANY] * n,
        out_shape=[_out(N_DEV * a.shape[0], a.shape[1], a.dtype) for a in arrs],
        scratch_shapes=[pltpu.SemaphoreType.DMA((n, 7)), pltpu.SemaphoreType.DMA((n, 7)), pltpu.SemaphoreType.DMA((n,))],
    )(*arrs)


def _cores_start(name, blocks):
    n = len(blocks)
    c = blocks[0].shape[2]
    r = sum(b.shape[1] for b in blocks)

    def build(src_refs, land_refs, send_sems, recv_sems):
        x, y, cc = lax.axis_index("x"), lax.axis_index("y"), lax.axis_index("c")
        remote, off = [], 0
        for a, src in enumerate(src_refs):
            rows = pl.ds(off, src.shape[1])
            off += src.shape[1]
            for q in range(4):
                remote.append(pltpu.make_async_remote_copy(
                    src_ref=src.at[2 * q + (1 - cc)], dst_ref=land_refs[0].at[q, rows], send_sem=send_sems.at[4 * a + q],
                    recv_sem=recv_sems.at[4 * a + q], device_id=(x, y, 1 - cc), device_id_type=MESH))
        return remote, []

    return _split_start(name, [(blocks, [jax.ShapeDtypeStruct((4, r, c), blocks[0].dtype)], 4 * n, 0, build)])[0]


def _peer(k, x, y, c):
    return (1 - x if k & 4 else x, 1 - y if k & 2 else y, 1 - c if k & 1 else c)


def _split_start(name, groups, after=None):
    pins = [] if after is None else [after]
    bufs, sem_shapes, spans = [], [], []
    for srcs, land_shapes, n_remote, n_local, _ in groups:
        sems = [pltpu.SemaphoreType.DMA((n_remote,)), pltpu.SemaphoreType.DMA((n_remote,))]
        sems += [pltpu.SemaphoreType.DMA((n_local,))] if n_local else []
        spans.append((len(bufs), len(srcs), len(land_shapes), len(sem_shapes), len(sems)))
        bufs += [pltpu.with_memory_space_constraint(a, pltpu.HBM) for a in srcs]
        bufs += [pltpu.with_memory_space_constraint(lax.empty(s.shape, s.dtype), pltpu.HBM) for s in land_shapes]
        sem_shapes += sems
    n_buf, n_sem = len(bufs), len(sem_shapes)

    def body(*refs):
        buf_refs, sem_refs, token = refs[:n_buf], refs[n_buf + len(pins):n_buf + len(pins) + n_sem], refs[-1]
        for (b0, ns, nl, s0, k), group in zip(spans, groups):
            remote, local = group[4](buf_refs[b0:b0 + ns], buf_refs[b0 + ns:b0 + ns + nl], *sem_refs[s0:s0 + k])
            for cp in local + remote:
                cp.start()
        token[...] = jnp.zeros_like(token)

    outs = pl.pallas_call(
        body, name=name,
        out_shape=sem_shapes + [pltpu.HBM(b.shape, b.dtype) for b in bufs] + [jax.ShapeDtypeStruct((SUBLANES, LANES), F32)],
        in_specs=[HBM] * n_buf + [ANY] * len(pins),
        out_specs=[SEM] * n_sem + [HBM] * n_buf + [pl.BlockSpec(memory_space=pltpu.VMEM)],
        input_output_aliases={i: n_sem + i for i in range(n_buf)},
        compiler_params=pltpu.CompilerParams(has_side_effects=SIDE_EFFECT),
    )(*bufs, *pins)
    return [dict(sems=list(outs[s0:s0 + k]), bufs=list(outs[n_sem + b0:n_sem + b0 + ns + nl]), token=outs[-1],
                 build=group[4], ns=ns) for (b0, ns, nl, s0, k), group in zip(spans, groups)]


def _split_wait(name, started, after):
    ns, n_buf, n_sem = started["ns"], len(started["bufs"]), len(started["sems"])

    def body(*refs):
        src_refs, land_refs = refs[:ns], refs[ns:n_buf]
        sems = refs[n_buf:n_buf + n_sem]
        remote, local = started["build"](src_refs, land_refs, *sems)
        for cp in local:
            cp.wait()
        for cp in remote:
            cp.wait_send()
            cp.wait_recv()

    outs = pl.pallas_call(
        body, name=name, out_shape=[pltpu.HBM(b.shape, b.dtype) for b in started["bufs"]],
        in_specs=[HBM] * n_buf + [SEM] * n_sem + [ANY], out_specs=[HBM] * n_buf,
        input_output_aliases={i: i for i in range(n_buf)},
        compiler_params=pltpu.CompilerParams(has_side_effects=SIDE_EFFECT),
    )(*started["bufs"], *started["sems"], after)
    return list(outs[:ns]), list(outs[ns:])


def _gather_group(shards):
    m = len(shards)

    def build(src_refs, land_refs, send_sems, recv_sems, local_sems):
        x, y, c = lax.axis_index("x"), lax.axis_index("y"), lax.axis_index("c")
        remote, local = [], []
        for j in range(m):
            r = src_refs[j].shape[0]
            dst = land_refs[j].at[pl.ds((4 * x + 2 * y + c) * r, r), :]
            local.append(pltpu.make_async_copy(src_refs[j], dst, local_sems.at[j]))
            for k in range(1, N_DEV):
                remote.append(pltpu.make_async_remote_copy(
                    src_ref=src_refs[j], dst_ref=dst, send_sem=send_sems.at[7 * j + k - 1],
                    recv_sem=recv_sems.at[7 * j + k - 1], device_id=_peer(k, x, y, c), device_id_type=MESH))
        return remote, local

    lands = [jax.ShapeDtypeStruct((N_DEV * a.shape[0], a.shape[1]), a.dtype) for a in shards]
    return shards, lands, 7 * m, m, build


def _slots_start(name, a):
    def build(src_refs, land_refs, send_sems, recv_sems, local_sems):
        x, y, c = lax.axis_index("x"), lax.axis_index("y"), lax.axis_index("c")
        dst = land_refs[0].at[4 * x + 2 * y + c]
        local = [pltpu.make_async_copy(src_refs[0], dst, local_sems.at[0])]
        remote = [pltpu.make_async_remote_copy(
            src_ref=src_refs[0], dst_ref=dst, send_sem=send_sems.at[k - 1], recv_sem=recv_sems.at[k - 1],
            device_id=_peer(k, x, y, c), device_id_type=MESH) for k in range(1, N_DEV)]
        return remote, local

    return _split_start(name, [([a], [jax.ShapeDtypeStruct((N_DEV,) + a.shape, a.dtype)], 7, 1, build)])[0]


def _chips_start(name, p):
    _, r, c = p.shape
    nck = r // GRAD_ROW_TILE

    def build(src_refs, land_refs, send_sems, recv_sems):
        x, y, cc = lax.axis_index("x"), lax.axis_index("y"), lax.axis_index("c")
        remote = []
        for k in range(1, 4):
            px = 1 - x if k >> 1 else x
            py = 1 - y if k & 1 else y
            for j in range(nck):
                rows = pl.ds(j * GRAD_ROW_TILE, GRAD_ROW_TILE)
                remote.append(pltpu.make_async_remote_copy(
                    src_ref=src_refs[0].at[2 * px + py, rows], dst_ref=land_refs[0].at[k - 1, rows],
                    send_sem=send_sems.at[(k - 1) * nck + j], recv_sem=recv_sems.at[(k - 1) * nck + j],
                    device_id=(px, py, cc), device_id_type=MESH))
        return remote, []

    return _split_start(name, [([p], [jax.ShapeDtypeStruct((3, r, c), p.dtype)], 3 * nck, 0, build)])[0]


def _chip_sum(name, p, recv, chip):
    _, r, c = p.shape
    tr = _pick(r, 5 * GRAD_ROW_TILE, GRAD_ROW_TILE)

    def body(chip_ref, p_ref, r_ref, o_ref):
        acc = p_ref[...].astype(F32)
        for k in range(3):
            acc = acc + r_ref[k].astype(F32)
        o_ref[...] = acc

    return pl.pallas_call(
        body, name=name,
        grid_spec=pltpu.PrefetchScalarGridSpec(
            num_scalar_prefetch=1, grid=(r // tr,),
            in_specs=[pl.BlockSpec((None, tr, c), lambda i, chip_ref: (chip_ref[0], i, 0)),
                      pl.BlockSpec((3, tr, c), lambda i, chip_ref: (0, i, 0))],
            out_specs=pl.BlockSpec((tr, c), lambda i, chip_ref: (i, 0))),
        out_shape=_out(r, c, F32), compiler_params=_params(("parallel",)),
    )(chip, p, recv)


def _local_step(x, mem, tgt, wt, sm, ev=None):
    t, d = x.shape
    n_mem = mem.shape[0]
    d_pool = sm["pool_scale"].shape[1]
    ng, pc = sm["pool_w"].shape[0], sm["pool_w"].shape[1]
    d_ssm = sm["ssm_d"].shape[1]
    _, sg, sp, sh = sm["ssm_b_re"].shape
    n_state = sg * sp
    gb, gs = {}, {}

    def emit(name, **kw):
        return ev(name, **kw) if ev is not None else None

    n1 = _rms_fwd("ffn1_norm", x, sm["ffn1_norm"])
    emit("ffn1_norm_done", marker=n1)
    def ffn1_down(hid):
        emit("ffn1_up_done", marker=hid)
        return wt["ffn1_w_down"]

    h1, ffn1_saved = _ffn_fwd("ffn1", x, n1, wt["ffn1_w_gate"], wt["ffn1_w_up"], ffn1_down)
    emit("ffn1_fwd_done", marker=h1)
    u = _rms_fwd("mix_norm", h1, sm["mix_norm"])
    d_in = wt["w_in"].shape[0]
    tm, tn = _pick(t, 1024), _pick(d_in, 1408)
    proj = _mm1("in_proj", "nt", u, wt["w_in"], t, d_in, tm, tn, F32)
    off_s = d_pool // d_ssm
    off_gp = (d_pool + d_ssm)
    off_gs = off_gp + d

    pool_w_bf = sm["pool_w"].astype(BF16)
    pooled, pm = _pool_fwd(proj, pool_w_bf, sm["pool_scale"])

    by_p = lambda a: jnp.swapaxes(a, -1, -2).reshape(2 * sg, sh, sp)
    disc_args = [sm["ssm_a_re"].reshape(2 * sg, 1, sp), sm["ssm_a_im"].reshape(2 * sg, 1, sp),
                 sm["ssm_log_dt"].reshape(2 * sg, 1, 1), by_p(sm["ssm_b_re"]), by_p(sm["ssm_b_im"])]
    abr, abi, bbr, bbi = _ssm_disc(disc_args)
    abr2, abi2 = abr.reshape(2, n_state), abi.reshape(2, n_state)
    per_dir = lambda a: [_bd(a.reshape(2, sg, sh, sp)[dr]).astype(BF16) for dr in range(2)]
    b_re, b_im, c_re, c_im = per_dir(bbr), per_dir(bbi), per_dir(sm["ssm_c_re"]), per_dir(-sm["ssm_c_im"])
    sp32 = _to_segments(proj[:, d_pool:d_pool + d_ssm])
    xs, y_parts = [], []
    for dr in range(2):
        xr, xi, y_part = _ssm_fwd(f"ssm_fwd{dr}", sp32, b_re[dr], b_im[dr], c_re[dr], c_im[dr], abr2[dr:dr + 1],
                                  abi2[dr:dr + 1], reverse=(dr == 1))
        xs.append((xr, xi))
        y_parts.append(y_part)
    y, ys = _ssm_finish(y_parts[0], y_parts[1], sp32, sm["ssm_d"])
    tmy = _pick(t, 256)
    emit("mix_in_done", marker=ys)

    tmm, tnm, tnx = _pick(t, 1024), _pick(d, 256), _pick(d, 512)
    gp_spec = _tile(tmm, tnm, off_gp // tnm)
    gs_spec = _tile(tmm, tnm, off_gs // tnm)

    def merge_epi(accs, gpv, gsv):
        z_pool, val, gate = accs
        return (jax.nn.sigmoid(gpv) * z_pool + jax.nn.sigmoid(gsv) * (val * jax.nn.sigmoid(gate)),)

    merged = _mm("mix_merge", "nt", [pm, ys], [wt["w_pool_proj"], wt["w_glu_val"], wt["w_glu_gate"]],
                 [[(0, 0)], [(1, 1)], [(1, 2)]], t, d, tmm, tnm, [(proj, gp_spec), (proj, gs_spec)], merge_epi,
                 [(_out(t, d, BF16), None)])[0]
    res_epi = lambda accs, hin: (hin + accs[0],)
    h2 = _mm("mix_out", "nn", [merged], [wt["w_mix_out"]], [[(0, 0)]], t, d, tmm, tnx, [(h1, _tile(tmm, tnx))],
             res_epi, [(_out(t, d, F32), None)])[0]

    un = _rms_fwd("xattn_norm", h2, sm["xattn_norm"])
    mn = _rms_fwd("mem_norm", mem, sm["mem_norm"])
    emit("mix_done", marker=un)
    q = _mm1("xattn_q", "nn", un, wt["w_q"], t, d, tmm, tnx, BF16)
    kv = _mm1("xattn_kv", "nt", mn, wt["w_kv"], n_mem, 2 * d, n_mem, _pick(2 * d, 512), BF16)
    o = _attn_fwd(q, kv)
    h3 = _mm("xattn_out", "nn", [o], [wt["w_xo"]], [[(0, 0)]], t, d, tmm, tnx, [(h2, _tile(tmm, tnx))],
             res_epi, [(_out(t, d, F32), None)])[0]

    n2 = _rms_fwd("ffn2_norm", h3, sm["ffn2_norm"])
    emit("xattn_done", marker=n2)
    h4, ffn2_saved = _ffn_fwd("ffn2", h3, n2, wt["ffn2_w_gate"], wt["ffn2_w_up"], wt["ffn2_w_down"])

    dh4, dh4_bf, gs["final_norm"], loss = _loss_head(h4, sm["final_norm"], tgt)
    dh3, dh3_bf, gs["ffn2_norm"], gb["ffn2_w_gate"], gb["ffn2_w_up"], gb["ffn2_w_down"] = _ffn_bwd(
        "ffn2", h3, sm["ffn2_norm"], wt["ffn2_w_gate"], wt["ffn2_w_up"], wt["ffn2_w_down"], ffn2_saved, dh4, dh4_bf)

    tw = _pick(d, 1024)
    do = _mm1("xattn_do", "nt", dh3_bf, wt["w_xo"], t, d, tmm, tnx, BF16)
    gb["w_xo"] = _mm1("xattn_dwxo", "tn", o, dh3_bf, d, d, tw, tnx, BF16)
    dq, dkv = _attn_bwd(q, kv, do)
    gb["w_q"] = _mm1("xattn_dwq", "tn", un, dq, d, d, tw, tnx, BF16)
    dun = _mm1("xattn_dun", "nt", dq, wt["w_q"], t, d, tmm, tnx, F32)
    dh2, dh2_bf, gs["xattn_norm"] = _rms_bwd("xattn_norm_bwd", h2, sm["xattn_norm"], dun, dh3)
    gb["w_kv"] = _mm1("xattn_dwkv", "tn", dkv, mn, 2 * d, d, _pick(2 * d, 512), d, BF16)
    dmn = _mm1("xattn_dmn", "nn", dkv, wt["w_kv"], n_mem, d, n_mem, tnx, F32)
    gs["mem_norm"] = _rms_bwd("mem_norm_bwd", mem, sm["mem_norm"], dmn)

    gb["w_mix_out"] = _mm1("mix_dwout", "tn", merged, dh2_bf, d, d, tw, tnx, BF16)

    def merge_bwd_epi(accs, gpv, gsv):
        dmerged, z_pool, val, gate = accs
        sp_, ss_, sg_ = jax.nn.sigmoid(gpv), jax.nn.sigmoid(gsv), jax.nn.sigmoid(gate)
        glu = val * sg_
        dz_pool = dmerged * sp_
        dg_pool = dmerged * z_pool * (sp_ * (1.0 - sp_))
        dz_ssm = dmerged * ss_
        dg_ssm = dmerged * glu * (ss_ * (1.0 - ss_))
        dval = dz_ssm * sg_
        dgate = dz_ssm * glu * (1.0 - sg_)
        return dz_pool, dg_pool, dg_ssm, dval, dgate

    dz_pool, dg_pool, dg_ssm, dval, dgate = _mm(
        "mix_merge_bwd", "nt", [dh2_bf, pm, ys], [wt["w_mix_out"], wt["w_pool_proj"], wt["w_glu_val"], wt["w_glu_gate"]],
        [[(0, 0)], [(1, 1)], [(2, 2)], [(2, 3)]], t, d, tmm, tnm, [(proj, gp_spec), (proj, gs_spec)], merge_bwd_epi,
        [(_out(t, d, BF16), None)] * 5)
    gb["w_pool_proj"] = _mm1("pool_dwproj", "tn", dz_pool, pm, d, d_pool, tw, d_pool, BF16)
    gb["w_glu_val"] = _mm1("glu_dwval", "tn", dval, ys, d, d_ssm, tw, d_ssm, BF16)
    gb["w_glu_gate"] = _mm1("glu_dwgate", "tn", dgate, ys, d, d_ssm, tw, d_ssm, BF16)

    def gelu_bwd_epi(accs, yv):
        _, vjp = jax.vjp(jax.nn.gelu, yv)
        return (vjp(accs[0])[0],)

    dy = _mm("glu_dy", "nn", [dval, dgate], [wt["w_glu_val"], wt["w_glu_gate"]], [[(0, 0), (1, 1)]], t, d_ssm, tmy, d_ssm,
             [(y, _tile(tmy, d_ssm))], gelu_bwd_epi, [(_out(t, d_ssm, F32), None)])[0]
    gs["ssm_d"] = _colsum_prod("ssm_dd", dy, proj, b_coff=off_s)
    dyp = _to_segments(dy)
    d_abr, d_abi, d_bbr, d_bbi, d_cre, d_cim, lams = [], [], [], [], [], [], []
    ts = _pick(n_state, 512)

    def fold_diag(accs):
        first = pl.program_id(1) * (ts // sp)
        row_group = lax.broadcasted_iota(jnp.int32, (d_ssm, sp), 0) // sh
        folded = []
        for acc in accs:
            out = jnp.zeros((d_ssm, sp), F32)
            for k in range(ts // sp):
                out = out + jnp.where(row_group == first + k, acc[:, sp * k:sp * (k + 1)], 0.0)
            folded.append(out)
        return tuple(folded)

    for dr in range(2):
        lr, li, dar, dai = _ssm_bwd(f"ssm_bwd{dr}", dyp, c_re[dr], c_im[dr], xs[dr][0], xs[dr][1], abr2[dr:dr + 1],
                                    abi2[dr:dr + 1], reverse=(dr == 1))
        d_abr.append(dar)
        d_abi.append(dai)
        lams += [lr, li]
        maps = _mm(f"ssm_dmaps{dr}", "tn", [sp32, dyp], [lr, li, xs[dr][0], xs[dr][1]],
                   [[(0, 0)], [(0, 1)], [(1, 2)], [(1, 3)]], d_ssm, n_state, d_ssm, ts, [], fold_diag,
                   [(_out(n_state // ts * d_ssm, sp, F32), pl.BlockSpec((d_ssm, sp), lambda i, j: (j, 0)))] * 4)
        for acc, m in zip((d_bbr, d_bbi, d_cre, d_cim), maps):
            acc.append(jnp.sum(m.reshape(n_state // ts, sg, sh, sp), axis=0))
    ds = _from_segments(_mm(
        "ssm_ds", "nt", lams, [b_re[0], b_im[0], b_re[1], b_im[1]], [[(k, k) for k in range(4)]], t, d_ssm, tmy,
        d_ssm, [(dyp, _tile(tmy, d_ssm)), (sm["ssm_d"], _rowvec(d_ssm))],
        lambda accs, dyv, dv: (dyv * dv + accs[0],), [(_out(t, d_ssm, BF16), None)])[0])
    cots = [jnp.concatenate(d_abr, axis=0).reshape(2 * sg, 1, sp), jnp.concatenate(d_abi, axis=0).reshape(2 * sg, 1, sp),
            jnp.concatenate(d_bbr, axis=0), jnp.concatenate(d_bbi, axis=0)]
    d_are, d_aim, d_ldt, d_bre, d_bim = _ssm_disc_bwd(disc_args, cots)
    gs["ssm_a_re"] = d_are.reshape(2, sg, sp)
    gs["ssm_a_im"] = d_aim.reshape(2, sg, sp)
    gs["ssm_log_dt"] = d_ldt.reshape(2, sg)
    from_p = lambda a: jnp.swapaxes(a.reshape(2, sg, sh, sp), -1, -2)
    gs["ssm_b_re"], gs["ssm_b_im"] = from_p(d_bre), from_p(d_bim)
    gs["ssm_c_re"] = jnp.stack(d_cre, axis=0)
    gs["ssm_c_im"] = -jnp.stack(d_cim, axis=0)

    dpm = _mm1("pool_dpm", "nn", dz_pool, wt["w_pool_proj"], t, d_pool, tmm, _pick(d_pool, 256), F32)
    dp, gs["pool_w"], gs["pool_scale"] = _pool_bwd(pooled, dpm, pool_w_bf, sm["pool_scale"])

    w_in = wt["w_in"]
    parts = [(dp, 0, d_pool), (ds, d_pool, d_ssm), (dg_pool, off_gp, d), (dg_ssm, off_gs, d)]
    w_in_parts = [w_in[o0:o0 + width] for _, o0, width in parts]
    gb["w_in"] = jnp.concatenate(
        [_mm1(f"in_proj_dw{k}", "tn", p_[0], u, p_[2], d, _pick(p_[2], 1024), tnx, BF16) for k, p_ in enumerate(parts)], axis=0)
    pin = emit("grads_main", gb=gb)
    du = _mm("in_proj_du", "nn", [p_[0] for p_ in parts], w_in_parts, [[(k, k) for k in range(4)]], t, d, tmm, tnx, [],
             lambda accs: (accs[0],), [(_out(t, d, F32), None)], after=pin)[0]
    dh1, dh1_bf, gs["mix_norm"] = _rms_bwd("mix_norm_bwd", h1, sm["mix_norm"], du, dh2)
    pin = emit("small_early", gs=gs, loss=loss)

    def ffn1_weights_done(d_wg, d_wu, d_wd):
        gb["ffn1_w_gate"], gb["ffn1_w_up"], gb["ffn1_w_down"] = d_wg, d_wu, d_wd
        return emit("grads_ffn1", gb=gb)

    dx, _, gs["ffn1_norm"], _, _, _ = _ffn_bwd(
        "ffn1", x, sm["ffn1_norm"], wt["ffn1_w_gate"], wt["ffn1_w_up"], wt["ffn1_w_down"], ffn1_saved, dh1, dh1_bf,
        weights_done=ffn1_weights_done, after=pin)
    return loss, dx, gb, gs


WEIGHTS = ["ffn1_norm", "ffn1_w_gate", "ffn1_w_up", "ffn1_w_down", "mix_norm", "w_in", "pool_w", "pool_scale",
           "w_pool_proj", "ssm_a_re", "ssm_a_im", "ssm_log_dt", "ssm_b_re", "ssm_b_im", "ssm_c_re", "ssm_c_im", "ssm_d",
           "w_glu_val", "w_glu_gate", "w_mix_out", "xattn_norm", "mem_norm", "w_q", "w_kv", "w_xo", "ffn2_norm",
           "ffn2_w_gate", "ffn2_w_up", "ffn2_w_down", "final_norm"]
COL_SHARDED = ["ffn1_w_gate", "ffn1_w_up", "w_in", "w_pool_proj", "w_glu_val", "w_glu_gate", "w_kv", "ffn2_w_gate",
               "ffn2_w_up"]
ROW_SHARDED = ["ffn1_w_down", "w_mix_out", "w_q", "w_xo", "ffn2_w_down"]
BIG = [n for n in WEIGHTS if n in COL_SHARDED or n in ROW_SHARDED]
SMALL = [n for n in WEIGHTS if n not in BIG]
FFN1_BIG = ["ffn1_w_gate", "ffn1_w_up", "ffn1_w_down"]
MAIN_BIG = [n for n in BIG if n not in FFN1_BIG]
GATHER_PLAN = [("ffn1_up_done", ["ffn1_w_down"]), ("ffn1_fwd_done", ["w_in"]),
               ("mix_in_done", ["w_pool_proj", "w_glu_val", "w_glu_gate", "w_mix_out"]),
               ("mix_done", ["w_q", "w_kv", "w_xo"]), ("xattn_done", ["ffn2_w_gate", "ffn2_w_up", "ffn2_w_down"])]
MINOR_SWAPPED = ["ssm_b_re", "ssm_b_im"]
LATE_SMALL = "ffn1_norm"
EARLY_SMALL = [n for n in SMALL if n != LATE_SMALL]
PACK_ROWS = SUBLANES * LANES
GRAD_ROW_TILE = 256
ADAMW_STEP_WORDS = 1 << 19


def _to_rows(name, w):
    return w.T if name in COL_SHARDED else w


def _pack_small(vals):
    flat = []
    for v in vals:
        f = v.reshape(-1)
        flat.append(jnp.pad(f, (0, (-f.shape[0]) % PACK_ROWS)))
    total = sum(f.shape[0] for f in flat)
    flat.append(jnp.zeros(((-total) % (GRAD_ROW_TILE * LANES),), F32))
    return jnp.concatenate(flat).reshape(-1, LANES)


def _unpack_small(packed, shapes):
    out, row = [], 0
    for shp in shapes:
        size = math.prod(shp)
        rows = -(-size // PACK_ROWS) * SUBLANES
        out.append(packed[row:row + rows].reshape(-1)[:size].reshape(shp))
        row += rows
    return out


def kernel(x, mem, ffn1_norm, ffn1_w_gate, ffn1_w_up, ffn1_w_down, mix_norm, w_in, pool_w, pool_scale, w_pool_proj, ssm_a_re, ssm_a_im, ssm_log_dt, ssm_b_re, ssm_b_im, ssm_c_re, ssm_c_im, ssm_d, w_glu_val, w_glu_gate, w_mix_out, xattn_norm, mem_norm, w_q, w_kv, w_xo, ffn2_norm, ffn2_w_gate, ffn2_w_up, ffn2_w_down, final_norm, loss_target, m_ffn1_norm, m_ffn1_w_gate, m_ffn1_w_up, m_ffn1_w_down, m_mix_norm, m_w_in, m_pool_w, m_pool_scale, m_w_pool_proj, m_ssm_a_re, m_ssm_a_im, m_ssm_log_dt, m_ssm_b_re, m_ssm_b_im, m_ssm_c_re, m_ssm_c_im, m_ssm_d, m_w_glu_val, m_w_glu_gate, m_w_mix_out, m_xattn_norm, m_mem_norm, m_w_q, m_w_kv, m_w_xo, m_ffn2_norm, m_ffn2_w_gate, m_ffn2_w_up, m_ffn2_w_down, m_final_norm, v_ffn1_norm, v_ffn1_w_gate, v_ffn1_w_up, v_ffn1_w_down, v_mix_norm, v_w_in, v_pool_w, v_pool_scale, v_w_pool_proj, v_ssm_a_re, v_ssm_a_im, v_ssm_log_dt, v_ssm_b_re, v_ssm_b_im, v_ssm_c_re, v_ssm_c_im, v_ssm_d, v_w_glu_val, v_w_glu_gate, v_w_mix_out, v_xattn_norm, v_mem_norm, v_w_q, v_w_kv, v_w_xo, v_ffn2_norm, v_ffn2_w_gate, v_ffn2_w_up, v_ffn2_w_down, v_final_norm):
    given = dict(locals())
    wts = {n: given[n] for n in WEIGHTS}
    moms = {n: (given["m_" + n], given["v_" + n]) for n in WEIGHTS}
    x2, mem2, tgt2 = x[0], mem[0], loss_target[0]
    d = x2.shape[1]
    chip = (2 * lax.axis_index("x") + lax.axis_index("y")).astype(jnp.int32).reshape(1)

    def full_form(n, f):
        shard = wts[n][0].shape
        return f.reshape(N_DEV * shard[1], shard[0]) if n in COL_SHARDED else f.reshape(N_DEV * shard[0], shard[1])

    shards = {n: _to_rows(n, wts[n][0]).astype(BF16) for n in BIG}
    first = FFN1_BIG[:2]
    wt = {n: full_form(n, f) for n, f in zip(first, _allgather("weight_allgather_first", [shards[n] for n in first]))}
    started = _split_start("weight_gather_start", [_gather_group([shards[n] for n in names]) for _, names in GATHER_PLAN],
                           after=wt[first[0]])
    gathers = {event: (names, st) for (event, names), st in zip(GATHER_PLAN, started)}
    sm = {n: (wts[n].reshape(1, -1) if wts[n].ndim <= 2 else wts[n][0]) for n in SMALL}
    sm["ffn1_norm"] = sm["ffn1_norm"] + started[0]["token"][0, 0]

    pending = {}

    def reduce_start(tag, names, gb):
        blocks = [gb[n].reshape(N_DEV, -1, d) for n in names]
        pad_rows = (-sum(b.shape[1] for b in blocks)) % GRAD_ROW_TILE
        pad = [jnp.zeros((N_DEV, pad_rows, d), BF16)] if pad_rows else []
        started = _cores_start("grad_exchange_cores_start_" + tag, blocks + pad)
        own = jnp.concatenate([lax.dynamic_index_in_dim(b.reshape(4, 2, b.shape[1], d), lax.axis_index("c"), 1, False)
                               for b in started["bufs"][:len(blocks + pad)]], axis=1)
        _, (recv,) = _split_wait("grad_exchange_cores_wait_" + tag, started, own)
        rows_all = own.shape[1]
        pair = _ew("grad_pair_sum_" + tag, lambda a, b: (a.astype(F32) + b.astype(F32),),
                   [own.reshape(-1, d), recv.reshape(-1, d)], [BF16], rows_pref=5 * GRAD_ROW_TILE)[0]
        pair = pair.reshape(4, rows_all, d)
        pending[tag] = (pair, _chips_start("grad_exchange_chips_start_" + tag, pair), [b.shape[1] for b in blocks])
        return pending[tag][1]["token"]

    def reduce_finish(tag, after):
        _, started, rows = pending[tag]
        (pair,), (recv,) = _split_wait("grad_exchange_chips_wait_" + tag, started, after)
        return _chip_sum("grad_chip_sum_" + tag, pair, recv, chip), rows

    def ev(name, gb=None, gs=None, loss=None, marker=None):
        if name in gathers:
            names, started = gathers[name]
            for n, f in zip(names, _split_wait("weight_gather_wait_" + name, started, marker)[1]):
                wt[n] = full_form(n, f)
        elif name == "grads_main":
            return reduce_start("main", MAIN_BIG, gb)
        elif name == "small_early":
            pending["small"] = _slots_start("small_gather_start", _pack_small([gs[n] for n in EARLY_SMALL] + [loss[:, :1]]))
            return pending["small"]["token"]
        elif name == "grads_ffn1":
            return reduce_start("ffn1", FFN1_BIG, gb)
        return None

    _, dx, _, gs = _local_step(x2, mem2, tgt2, wt, sm, ev)

    grads = {}
    for tag, names in (("main", MAIN_BIG), ("ffn1", FFN1_BIG)):
        g_rows, rows = reduce_finish(tag, dx)
        off = 0
        for n, r in zip(names, rows):
            shard = wts[n].shape
            grads[n] = g_rows[off:off + r].reshape((shard[2], shard[1]) if n in COL_SHARDED else shard[1:])
            off += r
    small_sum = _sum_slots("small_sum", _split_wait("small_gather_wait", pending["small"], dx)[1][0], F32)
    late = _allgather("small_allgather_late", [gs[LATE_SMALL].reshape(-1, LANES)])[0]
    late_sum = _sum_slots("small_sum_late", late.reshape(N_DEV, -1, LANES), F32)
    vals = _unpack_small(small_sum, [wts[n].shape for n in EARLY_SMALL] + [(1, 1)])
    total_loss = vals[-1].reshape(())
    def flat(n, a):
        a = a.reshape(wts[n].shape)
        a = jnp.swapaxes(a, -1, -2) if n in MINOR_SWAPPED else a
        return a.reshape(-1, a.shape[-1])

    def unflat(n, a):
        shape = wts[n].shape
        if n in MINOR_SWAPPED:
            return jnp.swapaxes(a.reshape(shape[:-2] + (shape[-1], shape[-2])), -1, -2)
        return a.reshape(shape)

    for n, g_full in zip(EARLY_SMALL + [LATE_SMALL], vals[:-1] + [late_sum]):
        grads[n] = flat(n, g_full)

    out_g, out_d, out_m, out_v = {}, {}, {}, {}
    by_shape = {}
    for n in WEIGHTS:
        by_shape.setdefault((flat(n, wts[n]).shape, n in COL_SHARDED), []).append(n)
    for (_, transposed), names in by_shape.items():
        items = [(flat(n, wts[n]), grads[n], flat(n, moms[n][0]), flat(n, moms[n][1])) for n in names]
        for n, res in zip(names, _adamw_group("adamw_" + names[0], items, transposed)):
            out_d[n], out_m[n], out_v[n], out_g[n] = (unflat(n, a) for a in res)

    return (total_loss, dx[None], *[out_g[n] for n in WEIGHTS], *[out_d[n] for n in WEIGHTS],
            *[out_m[n] for n in WEIGHTS], *[out_v[n] for n in WEIGHTS])
```

```python
import functools
import math

import jax
import jax.numpy as jnp
from jax import lax
from jax.experimental import pallas as pl
from jax.experimental.pallas import tpu as pltpu

F32 = jnp.float32
BF16 = jnp.bfloat16
EPS = 1e-6
N_XHEADS = 4
POOL_WINDOWS = (2, 4, 8, 16)
ADAM_LR = 0.001
ADAM_B1 = 0.9
ADAM_B2 = 0.999
ADAM_EPS = 1e-08
ADAM_WD = 0.01
ADAM_STEP = 10
N_DEV = 8
VMEM_LIMIT_V7X = 48 * 1024 * 1024
LANES = 128
SUBLANES = 8
SUB_ROWS = 256
POOL_PAD = 16
MESH = pl.DeviceIdType.MESH
ANY = pl.BlockSpec(memory_space=pl.ANY)
HBM = pl.BlockSpec(memory_space=pltpu.HBM)
SEM = pl.BlockSpec(memory_space=pltpu.SEMAPHORE)
SIDE_EFFECT = pltpu.SideEffectType.DATAFLOW_SIDE_EFFECTING

_DIMS = {
    "nt": (((1,), (1,)), ((), ())),
    "nn": (((1,), (0,)), ((), ())),
    "tn": (((0,), (0,)), ((), ())),
}


def _pick(dim, pref, mult=LANES):
    if dim <= pref:
        return dim
    for t in range(pref - pref % mult, 0, -mult):
        if dim % t == 0:
            return t
    return dim


def _params(sem):
    return pltpu.CompilerParams(dimension_semantics=sem, vmem_limit_bytes=VMEM_LIMIT_V7X)


def _tile(tm, tn, coff=0):
    return pl.BlockSpec((tm, tn), lambda i, j: (i, j + coff))


def _rowvec(tn, coff=0):
    return pl.BlockSpec((1, tn), lambda i, j: (0, j + coff))


def _out(m, n, dtype):
    return jax.ShapeDtypeStruct((m, n), dtype)


def _mm(name, form, a_list, b_list, groups, m, n, tm, tn, extras, epilogue, outs, after=None, sub=SUB_ROWS):
    na, nb, ne = len(a_list), len(b_list), len(extras)
    pins = [] if after is None else [after]
    step = tm if (sub is None or form == "tn" or tm % sub) else sub

    def a_spec(a):
        if form == "tn":
            return pl.BlockSpec((a.shape[0], tm), lambda i, j: (0, i))
        return pl.BlockSpec((tm, a.shape[1]), lambda i, j: (i, 0))

    def b_spec(b):
        if form == "nt":
            return pl.BlockSpec((tn, b.shape[1]), lambda i, j: (j, 0))
        return pl.BlockSpec((b.shape[0], tn), lambda i, j: (0, j))

    def body(*refs):
        a_refs, b_refs = refs[:na], refs[na:na + nb]
        e_refs, o_refs = refs[na + nb:na + nb + ne], refs[na + nb + ne + len(pins):]
        b_vals = {}
        for s0 in range(0, tm, step):
            rows = slice(None) if step == tm else pl.ds(s0, step)
            a_vals, accs = {}, []
            for group in groups:
                acc = None
                for ai, bi in group:
                    if ai not in a_vals:
                        a_vals[ai] = (a_refs[ai][...] if form == "tn" else a_refs[ai][rows, :]).astype(BF16)
                    if bi not in b_vals:
                        b_vals[bi] = b_refs[bi][...].astype(BF16)
                    d = lax.dot_general(a_vals[ai], b_vals[bi], _DIMS[form], preferred_element_type=F32)
                    acc = d if acc is None else acc + d
                accs.append(acc)
            res = epilogue(accs, *[e[rows, :] if e.shape[0] == tm else e[...] for e in e_refs])
            for o_ref, r in zip(o_refs, res):
                o_ref[rows, :] = r.astype(o_ref.dtype)

    out_specs = [_tile(tm, tn) if s is None else s for _, s in outs]
    res = pl.pallas_call(
        body, name=name, grid=(m // tm, n // tn),
        in_specs=[a_spec(a) for a in a_list] + [b_spec(b) for b in b_list] + [s for _, s in extras] + [ANY] * len(pins),
        out_specs=out_specs, out_shape=[o for o, _ in outs],
        compiler_params=_params(("parallel", "parallel")),
    )(*a_list, *b_list, *[e for e, _ in extras], *pins)
    return res


def _mm1(name, form, a, b, m, n, tm, tn, dtype, scale=None):
    epi = (lambda accs: (accs[0],)) if scale is None else (lambda accs: (accs[0] * scale,))
    return _mm(name, form, [a], [b], [[(0, 0)]], m, n, tm, tn, [], epi, [(_out(m, n, dtype), None)])[0]


def _rms_fwd(name, h, g):
    t, d = h.shape
    tm = _pick(t, 512, SUBLANES)

    def body(h_ref, g_ref, n_ref):
        hv = h_ref[...]
        r = lax.rsqrt(jnp.mean(hv * hv, axis=-1, keepdims=True) + EPS)
        n_ref[...] = ((hv * r) * g_ref[...]).astype(BF16)

    return pl.pallas_call(
        body, name=name, grid=(t // tm,),
        in_specs=[pl.BlockSpec((tm, d), lambda i: (i, 0)), pl.BlockSpec((1, d), lambda i: (0, 0))],
        out_specs=pl.BlockSpec((tm, d), lambda i: (i, 0)), out_shape=_out(t, d, BF16),
        compiler_params=_params(("parallel",)),
    )(h, g)


def _rms_bwd(name, h, g, dn, dres=None):
    t, d = h.shape
    tm = _pick(t, 512, SUBLANES)
    need_dh = dres is not None

    def body(*refs):
        if need_dh:
            h_ref, g_ref, dn_ref, dres_ref, dh_ref, dhb_ref, dg_ref = refs
        else:
            h_ref, g_ref, dn_ref, dg_ref = refs
        hv = h_ref[...]
        r = lax.rsqrt(jnp.mean(hv * hv, axis=-1, keepdims=True) + EPS)
        nh = hv * r
        dnv = dn_ref[...].astype(F32)

        @pl.when(pl.program_id(0) == 0)
        def _():
            dg_ref[...] = jnp.zeros_like(dg_ref)

        dg_ref[...] += jnp.sum(dnv * nh, axis=0, keepdims=True)
        if need_dh:
            dng = dnv * g_ref[...]
            dh = dres_ref[...] + r * (dng - nh * jnp.mean(dng * nh, axis=-1, keepdims=True))
            dh_ref[...] = dh
            dhb_ref[...] = dh.astype(BF16)

    row = pl.BlockSpec((tm, d), lambda i: (i, 0))
    vec = pl.BlockSpec((1, d), lambda i: (0, 0))
    if need_dh:
        return pl.pallas_call(
            body, name=name, grid=(t // tm,), in_specs=[row, vec, row, row], out_specs=[row, row, vec],
            out_shape=[_out(t, d, F32), _out(t, d, BF16), _out(1, d, F32)], compiler_params=_params(("arbitrary",)),
        )(h, g, dn, dres)
    return pl.pallas_call(
        body, name=name, grid=(t // tm,), in_specs=[row, vec, row], out_specs=vec,
        out_shape=_out(1, d, F32), compiler_params=_params(("arbitrary",)),
    )(h, g, dn)


def _loss_head(h, g, tgt):
    t, d = h.shape
    tm = _pick(t, 512, SUBLANES)

    def body(h_ref, g_ref, t_ref, dh_ref, dhb_ref, dg_ref, loss_ref):
        hv = h_ref[...]
        r = lax.rsqrt(jnp.mean(hv * hv, axis=-1, keepdims=True) + EPS)
        nh = hv * r
        err = nh * g_ref[...] - t_ref[...]

        @pl.when(pl.program_id(0) == 0)
        def _():
            dg_ref[...] = jnp.zeros_like(dg_ref)
            loss_ref[...] = jnp.zeros_like(loss_ref)

        per_row = jnp.mean(err * err, axis=-1, keepdims=True)
        loss_ref[...] += 0.5 * jnp.sum(per_row, axis=0, keepdims=True)
        dy = err * (1.0 / d)
        dg_ref[...] += jnp.sum(dy * nh, axis=0, keepdims=True)
        dng = dy * g_ref[...]
        dh = r * (dng - nh * jnp.mean(dng * nh, axis=-1, keepdims=True))
        dh_ref[...] = dh
        dhb_ref[...] = dh.astype(BF16)

    row = pl.BlockSpec((tm, d), lambda i: (i, 0))
    vec = pl.BlockSpec((1, d), lambda i: (0, 0))
    return pl.pallas_call(
        body, name="loss_head", grid=(t // tm,), in_specs=[row, vec, row],
        out_specs=[row, row, vec, pl.BlockSpec((1, LANES), lambda i: (0, 0))],
        out_shape=[_out(t, d, F32), _out(t, d, BF16), _out(1, d, F32), _out(1, LANES, F32)],
        compiler_params=_params(("arbitrary",)),
    )(h, g, tgt)


def _ffn_fwd(tag, h, n, wg_t, wu_t, wd):
    t, d = h.shape
    f = wg_t.shape[0]
    tm, tn = _pick(t, 1024), _pick(f, 1408)

    def up_epi(accs):
        a, b = accs
        return a, b, (a * jax.nn.sigmoid(a)) * b

    a, b, hid = _mm(tag + "_up", "nt", [n], [wg_t, wu_t], [[(0, 0)], [(0, 1)]], t, f, tm, tn, [], up_epi,
                    [(_out(t, f, BF16), None)] * 3)
    if callable(wd):
        wd = wd(hid)
    tm2, tn2 = _pick(t, 1024), _pick(d, 512)
    h_out = _mm(tag + "_down", "nn", [hid], [wd], [[(0, 0)]], t, d, tm2, tn2, [(h, _tile(tm2, tn2))],
                lambda accs, hin: (hin + 0.5 * accs[0],), [(_out(t, d, F32), None)])[0]
    return h_out, (n, a, b, hid)


def _ffn_bwd(tag, h, g, wg_t, wu_t, wd, saved, dh, dh_bf, weights_done=None, after=None):
    n, a, b, hid = saved
    t, d = h.shape
    f = wd.shape[0]
    tm, tn = _pick(t, 1024), _pick(f, 1408)

    def hid_epi(accs, av, bv):
        dhid = 0.5 * accs[0]
        av, bv = av.astype(F32), bv.astype(F32)
        sig = jax.nn.sigmoid(av)
        da = dhid * bv * (sig * (1.0 + av * (1.0 - sig)))
        db = dhid * (av * sig)
        return da, db

    da, db = _mm(tag + "_bwd_hid", "nt", [dh_bf], [wd], [[(0, 0)]], t, f, tm, tn,
                 [(a, _tile(tm, tn)), (b, _tile(tm, tn))], hid_epi, [(_out(t, f, BF16), None)] * 2, after=after)
    tw, tnw = _pick(f, 1408), _pick(d, 512)
    d_wd = _mm1(tag + "_dwd", "tn", hid, dh_bf, f, d, tw, tnw, BF16, scale=0.5)
    d_wg = _mm1(tag + "_dwg", "tn", da, n, f, d, tw, tnw, BF16)
    d_wu = _mm1(tag + "_dwu", "tn", db, n, f, d, tw, tnw, BF16)
    pin = weights_done(d_wg, d_wu, d_wd) if weights_done is not None else None
    tm2, tn2 = _pick(t, 1024), _pick(d, 512)
    dn = _mm(tag + "_dn", "nn", [da, db], [wg_t, wu_t], [[(0, 0), (1, 1)]], t, d, tm2, tn2, [],
             lambda accs: (accs[0],), [(_out(t, d, F32), None)], after=pin)[0]
    dh_in, dh_in_bf, dg = _rms_bwd(tag + "_norm_bwd", h, g, dn, dh)
    return dh_in, dh_in_bf, dg, d_wg, d_wu, d_wd


def _window_sum(win, offsets):
    n = win.shape[0]
    acc = None
    for j in offsets:
        term = win if j == 0 else pltpu.roll(win, (-j) % n, 0)
        acc = term if acc is None else acc + term
    return acc


def _pool_counts(r0, ch, c, left, right, t):
    pos = r0 + lax.broadcasted_iota(jnp.int32, (ch, c), 0)
    return (jnp.minimum(pos + right + 1, t) - jnp.maximum(pos - left, 0)).astype(F32)


def _pool_fwd(proj, pool_w_bf, pool_scale):
    t = proj.shape[0]
    ng, c, _ = pool_w_bf.shape
    ch = _pick(t, 256, SUBLANES)
    pad = POOL_PAD

    def body(p_ref, w_ref, s_ref, pooled_ref, pm_ref, buf):
        grp = pl.program_id(0)
        buf[pl.ds(0, pad), :] = jnp.zeros((pad, c), F32)
        buf[pl.ds(pad + t, pad), :] = jnp.zeros((pad, c), F32)

        def fill(ci, carry):
            r0 = pl.multiple_of(ci * ch, SUBLANES)
            buf[pl.ds(pl.multiple_of(r0 + pad, SUBLANES), ch), :] = p_ref[pl.ds(r0, ch), :]
            return carry

        lax.fori_loop(0, t // ch, fill, 0)
        for gi, w in enumerate(POOL_WINDOWS):
            left = w // 2
            right = w - 1 - left

            @pl.when(grp == gi)
            def _(left=left, right=right):
                def chunk(ci, carry):
                    r0 = pl.multiple_of(ci * ch, SUBLANES)
                    win = buf[pl.ds(r0, ch + 2 * pad), :]
                    s = _window_sum(win, range(-left, right + 1))[pad:pad + ch]
                    pooled = s / _pool_counts(r0, ch, c, left, right, t) - win[pad:pad + ch]
                    pooled_bf = pooled.astype(BF16)
                    mixed = jnp.dot(pooled_bf, w_ref[0], preferred_element_type=F32)
                    pooled_ref[pl.ds(r0, ch), :] = pooled_bf
                    pm_ref[pl.ds(r0, ch), :] = (mixed * s_ref[...]).astype(BF16)
                    return carry

                lax.fori_loop(0, t // ch, chunk, 0)

    col = pl.BlockSpec((t, c), lambda g: (0, g))
    return pl.pallas_call(
        body, name="pool_fwd", grid=(ng,),
        in_specs=[col, pl.BlockSpec((1, c, c), lambda g: (g, 0, 0)), pl.BlockSpec((1, c), lambda g: (0, g))],
        out_specs=[col, col], out_shape=[_out(t, ng * c, BF16), _out(t, ng * c, BF16)],
        scratch_shapes=[pltpu.VMEM((t + 2 * pad, c), F32)],
        compiler_params=_params(("parallel",)),
    )(proj, pool_w_bf, pool_scale)


def _pool_bwd(pooled, dpm, pool_w_bf, pool_scale):
    t = pooled.shape[0]
    ng, c, _ = pool_w_bf.shape
    ch = _pick(t, 256, SUBLANES)
    pad = POOL_PAD

    def body(pooled_ref, dpm_ref, w_ref, s_ref, dp_ref, dw_ref, ds_ref, buf, raw):
        grp = pl.program_id(0)
        buf[pl.ds(0, pad), :] = jnp.zeros((pad, c), F32)
        buf[pl.ds(pad + t, pad), :] = jnp.zeros((pad, c), F32)
        dw_ref[...] = jnp.zeros_like(dw_ref)
        ds_ref[...] = jnp.zeros_like(ds_ref)
        for gi, w in enumerate(POOL_WINDOWS):
            left = w // 2
            right = w - 1 - left

            @pl.when(grp == gi)
            def _(left=left, right=right):
                def first(ci, carry):
                    r0 = pl.multiple_of(ci * ch, SUBLANES)
                    pv = pooled_ref[pl.ds(r0, ch), :]
                    dpm_v = dpm_ref[pl.ds(r0, ch), :]
                    mixed = jnp.dot(pv, w_ref[0], preferred_element_type=F32)
                    ds_ref[...] += jnp.sum(dpm_v * mixed, axis=0, keepdims=True)
                    dmixed = (dpm_v * s_ref[...]).astype(BF16)
                    dw_ref[0] += lax.dot_general(pv, dmixed, _DIMS["tn"], preferred_element_type=F32)
                    dpooled = lax.dot_general(dmixed, w_ref[0], _DIMS["nt"], preferred_element_type=F32)
                    raw[pl.ds(r0, ch), :] = dpooled
                    buf[pl.ds(pl.multiple_of(r0 + pad, SUBLANES), ch), :] = (
                        dpooled / _pool_counts(r0, ch, c, left, right, t))
                    return carry

                lax.fori_loop(0, t // ch, first, 0)

                def second(ci, carry):
                    r0 = pl.multiple_of(ci * ch, SUBLANES)
                    win = buf[pl.ds(r0, ch + 2 * pad), :]
                    s = _window_sum(win, range(-right, left + 1))[pad:pad + ch]
                    dp_ref[pl.ds(r0, ch), :] = (s - raw[pl.ds(r0, ch), :]).astype(BF16)
                    return carry

                lax.fori_loop(0, t // ch, second, 0)

    col = pl.BlockSpec((t, c), lambda g: (0, g))
    return pl.pallas_call(
        body, name="pool_bwd", grid=(ng,),
        in_specs=[col, col, pl.BlockSpec((1, c, c), lambda g: (g, 0, 0)), pl.BlockSpec((1, c), lambda g: (0, g))],
        out_specs=[col, pl.BlockSpec((1, c, c), lambda g: (g, 0, 0)), pl.BlockSpec((1, c), lambda g: (0, g))],
        out_shape=[_out(t, ng * c, BF16), jax.ShapeDtypeStruct((ng, c, c), F32), _out(1, ng * c, F32)],
        scratch_shapes=[pltpu.VMEM((t + 2 * pad, c), F32), pltpu.VMEM((t, c), F32)],
        compiler_params=_params(("parallel",)),
    )(pooled, dpm, pool_w_bf, pool_scale)


def _discretise(a_re, a_im, log_dt, b_re, b_im):
    dt = jnp.exp(log_dt)
    mag = jnp.exp(dt * a_re)
    ang = dt * a_im
    abr = mag * jnp.cos(ang)
    abi = mag * jnp.sin(ang)
    den = a_re * a_re + a_im * a_im
    nr = abr - 1.0
    qr = (nr * a_re + abi * a_im) / den
    qi = (abi * a_re - nr * a_im) / den
    return abr, abi, qr * b_re - qi * b_im, qr * b_im + qi * b_re


def _ssm_disc(args):
    def body(ar, ai, ld, br, bi, o1, o2, o3, o4):
        res = _discretise(ar[...], ai[...], ld[...], br[...], bi[...])
        for o, r in zip((o1, o2, o3, o4), res):
            o[...] = r

    like = lambda a: jax.ShapeDtypeStruct(a.shape, F32)
    return pl.pallas_call(
        body, name="ssm_disc", out_shape=[like(args[0]), like(args[0]), like(args[3]), like(args[3])],
    )(*args)


def _ssm_disc_bwd(args, cots):
    def body(ar, ai, ld, br, bi, c1, c2, c3, c4, o1, o2, o3, o4, o5):
        _, vjp = jax.vjp(_discretise, ar[...], ai[...], ld[...], br[...], bi[...])
        res = vjp((c1[...], c2[...], c3[...], c4[...]))
        for o, r in zip((o1, o2, o3, o4, o5), res):
            o[...] = r

    return pl.pallas_call(
        body, name="ssm_disc_bwd", out_shape=[jax.ShapeDtypeStruct(a.shape, F32) for a in args],
    )(*args, *cots)


def _cmul(pr, pi, qr, qi):
    return pr * qr - pi * qi, pr * qi + pi * qr


def _cpow(pr, pi, n):
    rr, ri = None, None
    while n:
        if n & 1:
            rr, ri = (pr, pi) if rr is None else _cmul(rr, ri, pr, pi)
        n >>= 1
        if n:
            pr, pi = _cmul(pr, pi, pr, pi)
    return rr, ri


def _segment_carry(er, ei, pr, pi, reverse):
    row = lax.broadcasted_iota(jnp.int32, er.shape, 0)
    cr, ci = jnp.zeros_like(er), jnp.zeros_like(ei)
    for _ in range(SUBLANES - 1):
        tr = er + pr * cr - pi * ci
        ti = ei + pr * ci + pi * cr
        if reverse:
            keep, shift = row < SUBLANES - 1, SUBLANES - 1
        else:
            keep, shift = row >= 1, 1
        cr = jnp.where(keep, pltpu.roll(tr, shift, 0), 0.0)
        ci = jnp.where(keep, pltpu.roll(ti, shift, 0), 0.0)
    return cr, ci


def _ssm_fwd(name, sp, b_re, b_im, c_re, c_im, ar, ai, reverse):
    t, c = sp.shape
    s = ar.shape[1]
    w = _pick(s, 512)
    ch = _pick(t, 512, SUBLANES)
    n_ch, gpc, steps = t // ch, ch // SUBLANES, t // SUBLANES

    def body(sp_ref, bre_ref, bim_ref, cre_ref, cim_ref, ar_ref, ai_ref, xr_ref, xi_ref, y_ref, ur, ui, xbr, xbi):
        a_r = jnp.broadcast_to(ar_ref[...], (SUBLANES, w))
        a_i = jnp.broadcast_to(ai_ref[...], (SUBLANES, w))

        @pl.when(pl.program_id(0) == 0)
        def _():
            y_ref[...] = jnp.zeros_like(y_ref)

        def sweep(h0, store):
            def chunk(k, h):
                ci = n_ch - 1 - k if reverse else k
                rows = pl.ds(pl.multiple_of(ci * ch, ch), ch)
                spv = sp_ref[rows, :].astype(BF16)
                ur[...] = jnp.dot(spv, bre_ref[...], preferred_element_type=F32)
                ui[...] = jnp.dot(spv, bim_ref[...], preferred_element_type=F32)

                def group(g, hh):
                    gi = gpc - 1 - g if reverse else g
                    r0 = pl.multiple_of(gi * SUBLANES, SUBLANES)
                    hr, hi = hh
                    nr = a_r * hr - a_i * hi + ur[pl.ds(r0, SUBLANES), :]
                    ni = a_r * hi + a_i * hr + ui[pl.ds(r0, SUBLANES), :]
                    if store:
                        xbr[pl.ds(r0, SUBLANES), :] = nr
                        xbi[pl.ds(r0, SUBLANES), :] = ni
                    return nr, ni

                h = lax.fori_loop(0, gpc, group, h)
                if store:
                    xr16, xi16 = xbr[...].astype(BF16), xbi[...].astype(BF16)
                    xr_ref[rows, :] = xr16
                    xi_ref[rows, :] = xi16
                    y_ref[rows, :] += (lax.dot_general(xr16, cre_ref[...], _DIMS["nt"], preferred_element_type=F32)
                                       + lax.dot_general(xi16, cim_ref[...], _DIMS["nt"], preferred_element_type=F32))
                return h

            return lax.fori_loop(0, n_ch, chunk, h0)

        zero = jnp.zeros((SUBLANES, w), F32)
        er, ei = sweep((zero, zero), False)
        pr, pi = _cpow(ar_ref[...], ai_ref[...], steps)
        sweep(_segment_carry(er, ei, pr, pi, reverse), True)

    col = lambda i: (0, i)
    return pl.pallas_call(
        body, name=name, grid=(s // w,),
        in_specs=[pl.BlockSpec((t, c), lambda i: (0, 0))] + [pl.BlockSpec((c, w), col)] * 4
        + [pl.BlockSpec((1, w), col)] * 2,
        out_specs=[pl.BlockSpec((t, w), col), pl.BlockSpec((t, w), col), pl.BlockSpec((t, c), lambda i: (0, 0))],
        out_shape=[_out(t, s, BF16), _out(t, s, BF16), _out(t, c, F32)],
        scratch_shapes=[pltpu.VMEM((ch, w), F32)] * 4,
        compiler_params=_params(("arbitrary",)),
    )(sp, b_re, b_im, c_re, c_im, ar, ai)


def _ssm_bwd(name, dyp, c_re, c_im, xr, xi, ar, ai, reverse):
    t, c = dyp.shape
    s = ar.shape[1]
    w = _pick(s, 512)
    ch = _pick(t, 512, SUBLANES)
    n_ch, gpc, steps = t // ch, ch // SUBLANES, t // SUBLANES
    back = not reverse
    edge = 2 * SUBLANES

    def body(dy_ref, cre_ref, cim_ref, xr_ref, xi_ref, ar_ref, ai_ref, lr_ref, li_ref, dar_ref, dai_ref,
             gr, gi_, lbr, lbi, xbr, xbi):
        a_r = jnp.broadcast_to(ar_ref[...], (SUBLANES, w))
        a_i = -jnp.broadcast_to(ai_ref[...], (SUBLANES, w))
        row = lax.broadcasted_iota(jnp.int32, (SUBLANES, w), 0)

        def neighbours(ci, x_ref, buf):
            rows = pl.ds(pl.multiple_of(ci * ch, ch), ch)
            if reverse:
                buf[pl.ds(0, ch), :] = x_ref[rows, :].astype(F32)
                nxt = x_ref[pl.ds(pl.multiple_of(jnp.minimum(ci + 1, n_ch - 1) * ch, ch), edge), :].astype(F32)[:SUBLANES]
                first = x_ref[pl.ds(0, edge), :].astype(F32)[:SUBLANES]
                wrap = jnp.where(row < SUBLANES - 1, pltpu.roll(first, SUBLANES - 1, 0), 0.0)
                buf[pl.ds(ch, SUBLANES), :] = jnp.where(ci == n_ch - 1, wrap, nxt)
            else:
                buf[pl.ds(SUBLANES, ch), :] = x_ref[rows, :].astype(F32)
                prv = x_ref[pl.ds(pl.multiple_of(jnp.maximum(ci * ch - edge, 0), edge), edge), :].astype(F32)[SUBLANES:]
                last = x_ref[pl.ds(t - edge, edge), :].astype(F32)[SUBLANES:]
                wrap = jnp.where(row >= 1, pltpu.roll(last, 1, 0), 0.0)
                buf[pl.ds(0, SUBLANES), :] = jnp.where(ci == 0, wrap, prv)

        def sweep(h0, store):
            def chunk(k, carry):
                ci = n_ch - 1 - k if back else k
                rows = pl.ds(pl.multiple_of(ci * ch, ch), ch)
                dyv = dy_ref[rows, :].astype(BF16)
                gr[...] = jnp.dot(dyv, cre_ref[...], preferred_element_type=F32)
                gi_[...] = jnp.dot(dyv, cim_ref[...], preferred_element_type=F32)
                if store:
                    neighbours(ci, xr_ref, xbr)
                    neighbours(ci, xi_ref, xbi)

                def group(g, cc):
                    gidx = gpc - 1 - g if back else g
                    r0 = pl.multiple_of(gidx * SUBLANES, SUBLANES)
                    hr, hi = cc[0], cc[1]
                    nr = a_r * hr - a_i * hi + gr[pl.ds(r0, SUBLANES), :]
                    ni = a_r * hi + a_i * hr + gi_[pl.ds(r0, SUBLANES), :]
                    if not store:
                        return nr, ni
                    lbr[pl.ds(r0, SUBLANES), :] = nr
                    lbi[pl.ds(r0, SUBLANES), :] = ni
                    x0 = pl.multiple_of(r0 + SUBLANES, SUBLANES) if reverse else r0
                    xpr, xpi = xbr[pl.ds(x0, SUBLANES), :], xbi[pl.ds(x0, SUBLANES), :]
                    return nr, ni, cc[2] + nr * xpr + ni * xpi, cc[3] + ni * xpr - nr * xpi

                carry = lax.fori_loop(0, gpc, group, carry)
                if store:
                    lr_ref[rows, :] = lbr[...].astype(BF16)
                    li_ref[rows, :] = lbi[...].astype(BF16)
                return carry

            return lax.fori_loop(0, n_ch, chunk, h0)

        zero = jnp.zeros((SUBLANES, w), F32)
        er, ei = sweep((zero, zero), False)
        pr, pi = _cpow(ar_ref[...], -ai_ref[...], steps)
        cr, ci0 = _segment_carry(er, ei, pr, pi, back)
        _, _, dar, dai = sweep((cr, ci0, zero, zero), True)
        dar_ref[...] = jnp.sum(dar, axis=0, keepdims=True)
        dai_ref[...] = jnp.sum(dai, axis=0, keepdims=True)

    col = lambda i: (0, i)
    return pl.pallas_call(
        body, name=name, grid=(s // w,),
        in_specs=[pl.BlockSpec((t, c), lambda i: (0, 0)), pl.BlockSpec((c, w), col), pl.BlockSpec((c, w), col),
                  pl.BlockSpec((t, w), col), pl.BlockSpec((t, w), col), pl.BlockSpec((1, w), col), pl.BlockSpec((1, w), col)],
        out_specs=[pl.BlockSpec((t, w), col), pl.BlockSpec((t, w), col), pl.BlockSpec((1, w), col), pl.BlockSpec((1, w), col)],
        out_shape=[_out(t, s, BF16), _out(t, s, BF16), _out(1, s, F32), _out(1, s, F32)],
        scratch_shapes=[pltpu.VMEM((ch, w), F32)] * 4 + [pltpu.VMEM((ch + SUBLANES, w), F32)] * 2,
        compiler_params=_params(("parallel",)),
    )(dyp, c_re, c_im, xr, xi, ar, ai)


def _ssm_finish(y0, y1, sp, skip):
    t, c = sp.shape
    steps = t // SUBLANES
    w = _pick(c, LANES)

    def body(y0_ref, y1_ref, sp_ref, d_ref, y_ref, ys_ref):
        rows = pl.ds(pl.program_id(1), steps, stride=SUBLANES)
        y = y0_ref[rows, :] + y1_ref[rows, :] + sp_ref[rows, :] * d_ref[...]
        y_ref[...] = y
        ys_ref[...] = jax.nn.gelu(y).astype(BF16)

    whole = pl.BlockSpec((t, w), lambda j, k: (0, j))
    seg = pl.BlockSpec((steps, w), lambda j, k: (k, j))
    return pl.pallas_call(
        body, name="ssm_finish", grid=(c // w, SUBLANES),
        in_specs=[whole, whole, whole, pl.BlockSpec((1, w), lambda j, k: (0, j))], out_specs=[seg, seg],
        out_shape=[_out(t, c, F32), _out(t, c, BF16)], compiler_params=_params(("parallel", "arbitrary")),
    )(y0, y1, sp, skip)


def _to_segments(a):
    t, c = a.shape
    return a.reshape(SUBLANES, t // SUBLANES, c).transpose(1, 0, 2).reshape(t, c)


def _from_segments(a):
    t, c = a.shape
    return a.reshape(t // SUBLANES, SUBLANES, c).transpose(1, 0, 2).reshape(t, c)


def _colsum_prod(name, a, b, b_coff=0):
    t, n = a.shape
    tm = _pick(t, 512, SUBLANES)

    def body(a_ref, b_ref, o_ref):
        @pl.when(pl.program_id(0) == 0)
        def _():
            o_ref[...] = jnp.zeros_like(o_ref)

        o_ref[...] += jnp.sum(a_ref[...].astype(F32) * b_ref[...].astype(F32), axis=0, keepdims=True)

    return pl.pallas_call(
        body, name=name, grid=(t // tm,),
        in_specs=[pl.BlockSpec((tm, n), lambda i: (i, 0)), pl.BlockSpec((tm, n), lambda i: (i, b_coff))],
        out_specs=pl.BlockSpec((1, n), lambda i: (0, 0)), out_shape=_out(1, n, F32),
        compiler_params=_params(("arbitrary",)),
    )(a, b)


def _ssm_maps(arrs, signs):
    n2, hh, p = arrs[0].shape
    g = n2 // 2

    def body(*refs):
        ins, outs = refs[:len(arrs)], refs[len(arrs):]
        for a, (a_ref, sign) in enumerate(zip(ins, signs)):
            for d in range(2):
                o_ref = outs[2 * a + d]
                o_ref[...] = jnp.zeros_like(o_ref)
                for k in range(g):
                    o_ref[pl.ds(k * hh, hh), pl.ds(k * p, p)] = (sign * a_ref[d * g + k]).astype(BF16)

    outs = pl.pallas_call(body, name="ssm_maps", out_shape=[_out(g * hh, g * p, BF16)] * (2 * len(arrs)))(*arrs)
    return [outs[2 * a:2 * a + 2] for a in range(len(arrs))]


def _softmax(qh, kh, scale):
    s = lax.dot_general(qh, kh, _DIMS["nt"], preferred_element_type=F32) * scale
    e = jnp.exp(s - jnp.max(s, axis=-1, keepdims=True))
    return e / jnp.sum(e, axis=-1, keepdims=True)


def _attn_fwd(q, kv):
    t, d = q.shape
    mm_ = kv.shape[0]
    hd = d // N_XHEADS
    scale = 1.0 / math.sqrt(hd)
    tm = _pick(t, 512, SUBLANES)

    def body(q_ref, kv_ref, o_ref):
        for h in range(N_XHEADS):
            sl = pl.ds(h * hd, hd)
            p = _softmax(q_ref[:, sl], kv_ref[:, sl], scale)
            o_ref[:, sl] = jnp.dot(p.astype(BF16), kv_ref[:, pl.ds(d + h * hd, hd)],
                                   preferred_element_type=F32).astype(BF16)

    return pl.pallas_call(
        body, name="attn_fwd", grid=(t // tm,),
        in_specs=[pl.BlockSpec((tm, d), lambda i: (i, 0)), pl.BlockSpec((mm_, 2 * d), lambda i: (0, 0))],
        out_specs=pl.BlockSpec((tm, d), lambda i: (i, 0)), out_shape=_out(t, d, BF16),
        compiler_params=_params(("parallel",)),
    )(q, kv)


def _attn_bwd(q, kv, do):
    t, d = q.shape
    mm_ = kv.shape[0]
    hd = d // N_XHEADS
    scale = 1.0 / math.sqrt(hd)
    tm = _pick(t, 512, SUBLANES)

    def body(q_ref, kv_ref, do_ref, dq_ref, dkv_ref):
        @pl.when(pl.program_id(0) == 0)
        def _():
            dkv_ref[...] = jnp.zeros_like(dkv_ref)

        for h in range(N_XHEADS):
            sl = pl.ds(h * hd, hd)
            vsl = pl.ds(d + h * hd, hd)
            qh, kh, doh = q_ref[:, sl], kv_ref[:, sl], do_ref[:, sl]
            p = _softmax(qh, kh, scale)
            dp = lax.dot_general(doh, kv_ref[:, vsl], _DIMS["nt"], preferred_element_type=F32)
            dkv_ref[:, vsl] += lax.dot_general(p.astype(BF16), doh, _DIMS["tn"], preferred_element_type=F32)
            ds = (p * (dp - jnp.sum(dp * p, axis=-1, keepdims=True)) * scale).astype(BF16)
            dq_ref[:, sl] = jnp.dot(ds, kh, preferred_element_type=F32).astype(BF16)
            dkv_ref[:, sl] += lax.dot_general(ds, qh, _DIMS["tn"], preferred_element_type=F32)

    row = pl.BlockSpec((tm, d), lambda i: (i, 0))
    full = pl.BlockSpec((mm_, 2 * d), lambda i: (0, 0))
    return pl.pallas_call(
        body, name="attn_bwd", grid=(t // tm,), in_specs=[row, full, row], out_specs=[row, full],
        out_shape=[_out(t, d, BF16), _out(mm_, 2 * d, F32)], compiler_params=_params(("arbitrary",)),
    )(q, kv, do)


def _ew(name, fn, ins, outs, rows_pref=256):
    r, c = ins[0].shape
    tr = _pick(r, rows_pref, SUBLANES)
    ni = len(ins)

    def body(*refs):
        res = fn(*[x[...] for x in refs[:ni]])
        for o_ref, v in zip(refs[ni:], res):
            o_ref[...] = v.astype(o_ref.dtype)

    blk = pl.BlockSpec((tr, c), lambda i: (i, 0))
    return pl.pallas_call(
        body, name=name, grid=(r // tr,), in_specs=[blk] * ni, out_specs=[blk] * len(outs),
        out_shape=[_out(r, c, dt) for dt in outs], compiler_params=_params(("parallel",)),
    )(*ins)


def _sum_slots(name, a, dtype):
    s, r, c = a.shape
    tr = _pick(r, 256, SUBLANES)

    def body(a_ref, o_ref):
        acc = a_ref[0].astype(F32)
        for k in range(1, s):
            acc = acc + a_ref[k].astype(F32)
        o_ref[...] = acc.astype(o_ref.dtype)

    return pl.pallas_call(
        body, name=name, grid=(r // tr,), in_specs=[pl.BlockSpec((s, tr, c), lambda i: (0, i, 0))],
        out_specs=pl.BlockSpec((tr, c), lambda i: (i, 0)), out_shape=_out(r, c, dtype),
        compiler_params=_params(("parallel",)),
    )(a)


def _adamw_step(wv, gv, mv, vv):
    bc1 = 1.0 - ADAM_B1 ** ADAM_STEP
    bc2 = 1.0 - ADAM_B2 ** ADAM_STEP
    m2 = ADAM_B1 * mv + (1.0 - ADAM_B1) * gv
    v2 = ADAM_B2 * vv + (1.0 - ADAM_B2) * (gv * gv)
    delta = -ADAM_LR * ((m2 / bc1) / (jnp.sqrt(v2 / bc2) + ADAM_EPS) + ADAM_WD * wv)
    return delta, m2, v2


def _adamw_group(name, items, transposed):
    k, r = items[0][0].shape
    if transposed and r % LANES != 0:
        rows = _adamw_group(name, [(w.T, g, m.T, v.T) for w, g, m, v in items], False)
        return [[a.T for a in item] for item in rows]
    tk = _pick(k, max(SUBLANES, ADAMW_STEP_WORDS // (r * len(items))), SUBLANES)
    n_out = 4 if transposed else 3

    def body(*refs):
        ins, outs = refs[:4 * len(items)], refs[4 * len(items):]
        for i in range(len(items)):
            wv, gv, mv, vv = (a[...] for a in ins[4 * i:4 * i + 4])
            if transposed:
                gv = gv.T
            res = _adamw_step(wv, gv, mv, vv) + ((gv,) if transposed else ())
            for o_ref, val in zip(outs[n_out * i:n_out * (i + 1)], res):
                o_ref[...] = val

    blk = pl.BlockSpec((tk, r), lambda j: (j, 0))
    g_blk = pl.BlockSpec((r, tk), lambda j: (0, j)) if transposed else blk
    res = pl.pallas_call(
        body, name=name, grid=(k // tk,), in_specs=[blk, g_blk, blk, blk] * len(items),
        out_specs=[blk] * (n_out * len(items)), out_shape=[pltpu.HBM((k, r), F32)] * (n_out * len(items)),
        compiler_params=_params(("parallel",)),
    )(*[pltpu.with_memory_space_constraint(a, pltpu.HBM) for item in items for a in item])
    return [list(res[n_out * i:n_out * (i + 1)]) + ([] if transposed else [items[i][1]]) for i in range(len(items))]


def _allgather(name, arrs):
    n = len(arrs)

    def body(*refs):
        ins, outs = refs[:n], refs[n:2 * n]
        send_sems, recv_sems, local_sems = refs[2 * n:]
        x, y, c = lax.axis_index("x"), lax.axis_index("y"), lax.axis_index("c")
        me, sibling = (x, y, c), (x, y, 1 - c)
        chips = [(1 - x, y), (x, 1 - y), (1 - x, 1 - y)]

        def rows(a, px, py, pc):
            r = ins[a].shape[0]
            return outs[a].at[pl.ds((4 * px + 2 * py + pc) * r, r), :]

        def copy(a, k, block, to, src=None):
            return pltpu.make_async_remote_copy(
                src_ref=rows(a, *block) if src is None else src, dst_ref=rows(a, *block),
                send_sem=send_sems.at[a, k], recv_sem=recv_sems.at[a, k], device_id=to, device_id_type=MESH)

        mine = [pltpu.make_async_copy(ins[a], rows(a, *me), local_sems.at[a]) for a in range(n)]
        for cp in mine:
            cp.start()
        first = []
        for a in range(n):
            first.append(copy(a, 0, me, sibling, src=ins[a]))
            first += [copy(a, 1 + j, me, (*chip, c), src=ins[a]) for j, chip in enumerate(chips)]
        for cp in first:
            cp.start()
        passed = []
        for j, chip in enumerate(chips):
            for a in range(n):
                copy(a, 1 + j, (*chip, c), me).wait_recv()
                cp = copy(a, 4 + j, (*chip, c), sibling)
                cp.start()
                passed.append(cp)
        for a in range(n):
            copy(a, 0, sibling, me).wait_recv()
            for j, chip in enumerate(chips):
                copy(a, 4 + j, (*chip, 1 - c), me).wait_recv()
        for cp in first + passed:
            cp.wait_send()
        for cp in mine:
            cp.wait()

    return pl.pallas_call(
        body, name=name, in_specs=[ANY] * n, out_specs=[ANY] * n,
        out_shape=[_out(N_DEV * a.shape[0], a.shape[1], a.dtype) for a in arrs],
        scratch_shapes=[pltpu.SemaphoreType.DMA((n, 7)), pltpu.SemaphoreType.DMA((n, 7)), pltpu.SemaphoreType.DMA((n,))],
    )(*arrs)


def _cores_start(name, blocks):
    n = len(blocks)
    c = blocks[0].shape[2]
    r = sum(b.shape[1] for b in blocks)

    def build(src_refs, land_refs, send_sems, recv_sems):
        x, y, cc = lax.axis_index("x"), lax.axis_index("y"), lax.axis_index("c")
        remote, off = [], 0
        for a, src in enumerate(src_refs):
            rows = pl.ds(off, src.shape[1])
            off += src.shape[1]
            for q in range(4):
                remote.append(pltpu.make_async_remote_copy(
                    src_ref=src.at[2 * q + (1 - cc)], dst_ref=land_refs[0].at[q, rows], send_sem=send_sems.at[4 * a + q],
                    recv_sem=recv_sems.at[4 * a + q], device_id=(x, y, 1 - cc), device_id_type=MESH))
        return remote, []

    return _split_start(name, [(blocks, [jax.ShapeDtypeStruct((4, r, c), blocks[0].dtype)], 4 * n, 0, build)])[0]


def _peer(k, x, y, c):
    return (1 - x if k & 4 else x, 1 - y if k & 2 else y, 1 - c if k & 1 else c)


def _split_start(name, groups, after=None):
    pins = [] if after is None else [after]
    bufs, sem_shapes, spans = [], [], []
    for srcs, land_shapes, n_remote, n_local, _ in groups:
        sems = [pltpu.SemaphoreType.DMA((n_remote,)), pltpu.SemaphoreType.DMA((n_remote,))]
        sems += [pltpu.SemaphoreType.DMA((n_local,))] if n_local else []
        spans.append((len(bufs), len(srcs), len(land_shapes), len(sem_shapes), len(sems)))
        bufs += [pltpu.with_memory_space_constraint(a, pltpu.HBM) for a in srcs]
        bufs += [pltpu.with_memory_space_constraint(lax.empty(s.shape, s.dtype), pltpu.HBM) for s in land_shapes]
        sem_shapes += sems
    n_buf, n_sem = len(bufs), len(sem_shapes)

    def body(*refs):
        buf_refs, sem_refs, token = refs[:n_buf], refs[n_buf + len(pins):n_buf + len(pins) + n_sem], refs[-1]
        for (b0, ns, nl, s0, k), group in zip(spans, groups):
            remote, local = group[4](buf_refs[b0:b0 + ns], buf_refs[b0 + ns:b0 + ns + nl], *sem_refs[s0:s0 + k])
            for cp in local + remote:
                cp.start()
        token[...] = jnp.zeros_like(token)

    outs = pl.pallas_call(
        body, name=name,
        out_shape=sem_shapes + [pltpu.HBM(b.shape, b.dtype) for b in bufs] + [jax.ShapeDtypeStruct((SUBLANES, LANES), F32)],
        in_specs=[HBM] * n_buf + [ANY] * len(pins),
        out_specs=[SEM] * n_sem + [HBM] * n_buf + [pl.BlockSpec(memory_space=pltpu.VMEM)],
        input_output_aliases={i: n_sem + i for i in range(n_buf)},
        compiler_params=pltpu.CompilerParams(has_side_effects=SIDE_EFFECT),
    )(*bufs, *pins)
    return [dict(sems=list(outs[s0:s0 + k]), bufs=list(outs[n_sem + b0:n_sem + b0 + ns + nl]), token=outs[-1],
                 build=group[4], ns=ns) for (b0, ns, nl, s0, k), group in zip(spans, groups)]


def _split_wait(name, started, after):
    ns, n_buf, n_sem = started["ns"], len(started["bufs"]), len(started["sems"])

    def body(*refs):
        src_refs, land_refs = refs[:ns], refs[ns:n_buf]
        sems = refs[n_buf:n_buf + n_sem]
        remote, local = started["build"](src_refs, land_refs, *sems)
        for cp in local:
            cp.wait()
        for cp in remote:
            cp.wait_send()
            cp.wait_recv()

    outs = pl.pallas_call(
        body, name=name, out_shape=[pltpu.HBM(b.shape, b.dtype) for b in started["bufs"]],
        in_specs=[HBM] * n_buf + [SEM] * n_sem + [ANY], out_specs=[HBM] * n_buf,
        input_output_aliases={i: i for i in range(n_buf)},
        compiler_params=pltpu.CompilerParams(has_side_effects=SIDE_EFFECT),
    )(*started["bufs"], *started["sems"], after)
    return list(outs[:ns]), list(outs[ns:])


def _gather_group(shards):
    m = len(shards)

    def build(src_refs, land_refs, send_sems, recv_sems, local_sems):
        x, y, c = lax.axis_index("x"), lax.axis_index("y"), lax.axis_index("c")
        remote, local = [], []
        for j in range(m):
            r = src_refs[j].shape[0]
            dst = land_refs[j].at[pl.ds((4 * x + 2 * y + c) * r, r), :]
            local.append(pltpu.make_async_copy(src_refs[j], dst, local_sems.at[j]))
            for k in range(1, N_DEV):
                remote.append(pltpu.make_async_remote_copy(
                    src_ref=src_refs[j], dst_ref=dst, send_sem=send_sems.at[7 * j + k - 1],
                    recv_sem=recv_sems.at[7 * j + k - 1], device_id=_peer(k, x, y, c), device_id_type=MESH))
        return remote, local

    lands = [jax.ShapeDtypeStruct((N_DEV * a.shape[0], a.shape[1]), a.dtype) for a in shards]
    return shards, lands, 7 * m, m, build


def _slots_start(name, a):
    def build(src_refs, land_refs, send_sems, recv_sems, local_sems):
        x, y, c = lax.axis_index("x"), lax.axis_index("y"), lax.axis_index("c")
        dst = land_refs[0].at[4 * x + 2 * y + c]
        local = [pltpu.make_async_copy(src_refs[0], dst, local_sems.at[0])]
        remote = [pltpu.make_async_remote_copy(
            src_ref=src_refs[0], dst_ref=dst, send_sem=send_sems.at[k - 1], recv_sem=recv_sems.at[k - 1],
            device_id=_peer(k, x, y, c), device_id_type=MESH) for k in range(1, N_DEV)]
        return remote, local

    return _split_start(name, [([a], [jax.ShapeDtypeStruct((N_DEV,) + a.shape, a.dtype)], 7, 1, build)])[0]


def _chips_start(name, p):
    _, r, c = p.shape
    nck = r // GRAD_ROW_TILE

    def build(src_refs, land_refs, send_sems, recv_sems):
        x, y, cc = lax.axis_index("x"), lax.axis_index("y"), lax.axis_index("c")
        remote = []
        for k in range(1, 4):
            px = 1 - x if k >> 1 else x
            py = 1 - y if k & 1 else y
            for j in range(nck):
                rows = pl.ds(j * GRAD_ROW_TILE, GRAD_ROW_TILE)
                remote.append(pltpu.make_async_remote_copy(
                    src_ref=src_refs[0].at[2 * px + py, rows], dst_ref=land_refs[0].at[k - 1, rows],
                    send_sem=send_sems.at[(k - 1) * nck + j], recv_sem=recv_sems.at[(k - 1) * nck + j],
                    device_id=(px, py, cc), device_id_type=MESH))
        return remote, []

    return _split_start(name, [([p], [jax.ShapeDtypeStruct((3, r, c), p.dtype)], 3 * nck, 0, build)])[0]


def _chip_sum(name, p, recv, chip):
    _, r, c = p.shape
    tr = _pick(r, 5 * GRAD_ROW_TILE, GRAD_ROW_TILE)

    def body(chip_ref, p_ref, r_ref, o_ref):
        acc = p_ref[...].astype(F32)
        for k in range(3):
            acc = acc + r_ref[k].astype(F32)
        o_ref[...] = acc

    return pl.pallas_call(
        body, name=name,
        grid_spec=pltpu.PrefetchScalarGridSpec(
            num_scalar_prefetch=1, grid=(r // tr,),
            in_specs=[pl.BlockSpec((None, tr, c), lambda i, chip_ref: (chip_ref[0], i, 0)),
                      pl.BlockSpec((3, tr, c), lambda i, chip_ref: (0, i, 0))],
            out_specs=pl.BlockSpec((tr, c), lambda i, chip_ref: (i, 0))),
        out_shape=_out(r, c, F32), compiler_params=_params(("parallel",)),
    )(chip, p, recv)


def _local_step(x, mem, tgt, wt, sm, ev=None):
    t, d = x.shape
    n_mem = mem.shape[0]
    d_pool = sm["pool_scale"].shape[1]
    ng, pc = sm["pool_w"].shape[0], sm["pool_w"].shape[1]
    d_ssm = sm["ssm_d"].shape[1]
    _, sg, sp, sh = sm["ssm_b_re"].shape
    n_state = sg * sp
    gb, gs = {}, {}

    def emit(name, **kw):
        return ev(name, **kw) if ev is not None else None

    n1 = _rms_fwd("ffn1_norm", x, sm["ffn1_norm"])
    emit("ffn1_norm_done", marker=n1)
    def ffn1_down(hid):
        emit("ffn1_up_done", marker=hid)
        return wt["ffn1_w_down"]

    h1, ffn1_saved = _ffn_fwd("ffn1", x, n1, wt["ffn1_w_gate"], wt["ffn1_w_up"], ffn1_down)
    emit("ffn1_fwd_done", marker=h1)
    u = _rms_fwd("mix_norm", h1, sm["mix_norm"])
    d_in = wt["w_in"].shape[0]
    tm, tn = _pick(t, 1024), _pick(d_in, 1408)
    proj = _mm1("in_proj", "nt", u, wt["w_in"], t, d_in, tm, tn, F32)
    off_s = d_pool // d_ssm
    off_gp = (d_pool + d_ssm)
    off_gs = off_gp + d

    pool_w_bf = sm["pool_w"].astype(BF16)
    pooled, pm = _pool_fwd(proj, pool_w_bf, sm["pool_scale"])

    by_p = lambda a: jnp.swapaxes(a, -1, -2).reshape(2 * sg, sh, sp)
    disc_args = [sm["ssm_a_re"].reshape(2 * sg, 1, sp), sm["ssm_a_im"].reshape(2 * sg, 1, sp),
                 sm["ssm_log_dt"].reshape(2 * sg, 1, 1), by_p(sm["ssm_b_re"]), by_p(sm["ssm_b_im"])]
    abr, abi, bbr, bbi = _ssm_disc(disc_args)
    abr2, abi2 = abr.reshape(2, n_state), abi.reshape(2, n_state)
    b_re, b_im, c_re, c_im = _ssm_maps(
        [bbr, bbi, sm["ssm_c_re"].reshape(2 * sg, sh, sp), sm["ssm_c_im"].reshape(2 * sg, sh, sp)], [1.0, 1.0, 1.0, -1.0])
    sp32 = _to_segments(proj[:, d_pool:d_pool + d_ssm])
    xs, y_parts = [], []
    for dr in range(2):
        xr, xi, y_part = _ssm_fwd(f"ssm_fwd{dr}", sp32, b_re[dr], b_im[dr], c_re[dr], c_im[dr], abr2[dr:dr + 1],
                                  abi2[dr:dr + 1], reverse=(dr == 1))
        xs.append((xr, xi))
        y_parts.append(y_part)
    y, ys = _ssm_finish(y_parts[0], y_parts[1], sp32, sm["ssm_d"])
    tmy = _pick(t, 256)
    emit("mix_in_done", marker=ys)

    tmm, tnm, tnx = _pick(t, 1024), _pick(d, 256), _pick(d, 512)
    gp_spec = _tile(tmm, tnm, off_gp // tnm)
    gs_spec = _tile(tmm, tnm, off_gs // tnm)

    def merge_epi(accs, gpv, gsv):
        z_pool, val, gate = accs
        return (jax.nn.sigmoid(gpv) * z_pool + jax.nn.sigmoid(gsv) * (val * jax.nn.sigmoid(gate)),)

    merged = _mm("mix_merge", "nt", [pm, ys], [wt["w_pool_proj"], wt["w_glu_val"], wt["w_glu_gate"]],
                 [[(0, 0)], [(1, 1)], [(1, 2)]], t, d, tmm, tnm, [(proj, gp_spec), (proj, gs_spec)], merge_epi,
                 [(_out(t, d, BF16), None)])[0]
    res_epi = lambda accs, hin: (hin + accs[0],)
    h2 = _mm("mix_out", "nn", [merged], [wt["w_mix_out"]], [[(0, 0)]], t, d, tmm, tnx, [(h1, _tile(tmm, tnx))],
             res_epi, [(_out(t, d, F32), None)])[0]

    un = _rms_fwd("xattn_norm", h2, sm["xattn_norm"])
    mn = _rms_fwd("mem_norm", mem, sm["mem_norm"])
    emit("mix_done", marker=un)
    q = _mm1("xattn_q", "nn", un, wt["w_q"], t, d, tmm, tnx, BF16)
    kv = _mm1("xattn_kv", "nt", mn, wt["w_kv"], n_mem, 2 * d, n_mem, _pick(2 * d, 512), BF16)
    o = _attn_fwd(q, kv)
    h3 = _mm("xattn_out", "nn", [o], [wt["w_xo"]], [[(0, 0)]], t, d, tmm, tnx, [(h2, _tile(tmm, tnx))],
             res_epi, [(_out(t, d, F32), None)])[0]

    n2 = _rms_fwd("ffn2_norm", h3, sm["ffn2_norm"])
    emit("xattn_done", marker=n2)
    h4, ffn2_saved = _ffn_fwd("ffn2", h3, n2, wt["ffn2_w_gate"], wt["ffn2_w_up"], wt["ffn2_w_down"])

    dh4, dh4_bf, gs["final_norm"], loss = _loss_head(h4, sm["final_norm"], tgt)
    dh3, dh3_bf, gs["ffn2_norm"], gb["ffn2_w_gate"], gb["ffn2_w_up"], gb["ffn2_w_down"] = _ffn_bwd(
        "ffn2", h3, sm["ffn2_norm"], wt["ffn2_w_gate"], wt["ffn2_w_up"], wt["ffn2_w_down"], ffn2_saved, dh4, dh4_bf)

    tw = _pick(d, 1024)
    do = _mm1("xattn_do", "nt", dh3_bf, wt["w_xo"], t, d, tmm, tnx, BF16)
    gb["w_xo"] = _mm1("xattn_dwxo", "tn", o, dh3_bf, d, d, tw, tnx, BF16)
    dq, dkv = _attn_bwd(q, kv, do)
    gb["w_q"] = _mm1("xattn_dwq", "tn", un, dq, d, d, tw, tnx, BF16)
    dun = _mm1("xattn_dun", "nt", dq, wt["w_q"], t, d, tmm, tnx, F32)
    dh2, dh2_bf, gs["xattn_norm"] = _rms_bwd("xattn_norm_bwd", h2, sm["xattn_norm"], dun, dh3)
    gb["w_kv"] = _mm1("xattn_dwkv", "tn", dkv, mn, 2 * d, d, _pick(2 * d, 512), d, BF16)
    dmn = _mm1("xattn_dmn", "nn", dkv, wt["w_kv"], n_mem, d, n_mem, tnx, F32)
    gs["mem_norm"] = _rms_bwd("mem_norm_bwd", mem, sm["mem_norm"], dmn)

    gb["w_mix_out"] = _mm1("mix_dwout", "tn", merged, dh2_bf, d, d, tw, tnx, BF16)

    def merge_bwd_epi(accs, gpv, gsv):
        dmerged, z_pool, val, gate = accs
        sp_, ss_, sg_ = jax.nn.sigmoid(gpv), jax.nn.sigmoid(gsv), jax.nn.sigmoid(gate)
        glu = val * sg_
        dz_pool = dmerged * sp_
        dg_pool = dmerged * z_pool * (sp_ * (1.0 - sp_))
        dz_ssm = dmerged * ss_
        dg_ssm = dmerged * glu * (ss_ * (1.0 - ss_))
        dval = dz_ssm * sg_
        dgate = dz_ssm * glu * (1.0 - sg_)
        return dz_pool, dg_pool, dg_ssm, dval, dgate

    dz_pool, dg_pool, dg_ssm, dval, dgate = _mm(
        "mix_merge_bwd", "nt", [dh2_bf, pm, ys], [wt["w_mix_out"], wt["w_pool_proj"], wt["w_glu_val"], wt["w_glu_gate"]],
        [[(0, 0)], [(1, 1)], [(2, 2)], [(2, 3)]], t, d, tmm, tnm, [(proj, gp_spec), (proj, gs_spec)], merge_bwd_epi,
        [(_out(t, d, BF16), None)] * 5)
    gb["w_pool_proj"] = _mm1("pool_dwproj", "tn", dz_pool, pm, d, d_pool, tw, d_pool, BF16)
    gb["w_glu_val"] = _mm1("glu_dwval", "tn", dval, ys, d, d_ssm, tw, d_ssm, BF16)
    gb["w_glu_gate"] = _mm1("glu_dwgate", "tn", dgate, ys, d, d_ssm, tw, d_ssm, BF16)

    def gelu_bwd_epi(accs, yv):
        _, vjp = jax.vjp(jax.nn.gelu, yv)
        return (vjp(accs[0])[0],)

    dy = _mm("glu_dy", "nn", [dval, dgate], [wt["w_glu_val"], wt["w_glu_gate"]], [[(0, 0), (1, 1)]], t, d_ssm, tmy, d_ssm,
             [(y, _tile(tmy, d_ssm))], gelu_bwd_epi, [(_out(t, d_ssm, F32), None)])[0]
    gs["ssm_d"] = _colsum_prod("ssm_dd", dy, proj, b_coff=off_s)
    dyp = _to_segments(dy)
    d_abr, d_abi, d_bbr, d_bbi, d_cre, d_cim, lams = [], [], [], [], [], [], []
    ts = _pick(n_state, 512)

    def fold_diag(accs):
        first = pl.program_id(1) * (ts // sp)
        row_group = lax.broadcasted_iota(jnp.int32, (d_ssm, sp), 0) // sh
        folded = []
        for acc in accs:
            out = jnp.zeros((d_ssm, sp), F32)
            for k in range(ts // sp):
                out = out + jnp.where(row_group == first + k, acc[:, sp * k:sp * (k + 1)], 0.0)
            folded.append(out)
        return tuple(folded)

    for dr in range(2):
        lr, li, dar, dai = _ssm_bwd(f"ssm_bwd{dr}", dyp, c_re[dr], c_im[dr], xs[dr][0], xs[dr][1], abr2[dr:dr + 1],
                                    abi2[dr:dr + 1], reverse=(dr == 1))
        d_abr.append(dar)
        d_abi.append(dai)
        lams += [lr, li]
        maps = _mm(f"ssm_dmaps{dr}", "tn", [sp32, dyp], [lr, li, xs[dr][0], xs[dr][1]],
                   [[(0, 0)], [(0, 1)], [(1, 2)], [(1, 3)]], d_ssm, n_state, d_ssm, ts, [], fold_diag,
                   [(_out(n_state // ts * d_ssm, sp, F32), pl.BlockSpec((d_ssm, sp), lambda i, j: (j, 0)))] * 4)
        for acc, m in zip((d_bbr, d_bbi, d_cre, d_cim), maps):
            acc.append(jnp.sum(m.reshape(n_state // ts, sg, sh, sp), axis=0))
    ds = _from_segments(_mm(
        "ssm_ds", "nt", lams, [b_re[0], b_im[0], b_re[1], b_im[1]], [[(k, k) for k in range(4)]], t, d_ssm, tmy,
        d_ssm, [(dyp, _tile(tmy, d_ssm)), (sm["ssm_d"], _rowvec(d_ssm))],
        lambda accs, dyv, dv: (dyv * dv + accs[0],), [(_out(t, d_ssm, BF16), None)])[0])
    cots = [jnp.concatenate(d_abr, axis=0).reshape(2 * sg, 1, sp), jnp.concatenate(d_abi, axis=0).reshape(2 * sg, 1, sp),
            jnp.concatenate(d_bbr, axis=0), jnp.concatenate(d_bbi, axis=0)]
    d_are, d_aim, d_ldt, d_bre, d_bim = _ssm_disc_bwd(disc_args, cots)
    gs["ssm_a_re"] = d_are.reshape(2, sg, sp)
    gs["ssm_a_im"] = d_aim.reshape(2, sg, sp)
    gs["ssm_log_dt"] = d_ldt.reshape(2, sg)
    from_p = lambda a: jnp.swapaxes(a.reshape(2, sg, sh, sp), -1, -2)
    gs["ssm_b_re"], gs["ssm_b_im"] = from_p(d_bre), from_p(d_bim)
    gs["ssm_c_re"] = jnp.stack(d_cre, axis=0)
    gs["ssm_c_im"] = -jnp.stack(d_cim, axis=0)

    dpm = _mm1("pool_dpm", "nn", dz_pool, wt["w_pool_proj"], t, d_pool, tmm, _pick(d_pool, 256), F32)
    dp, gs["pool_w"], gs["pool_scale"] = _pool_bwd(pooled, dpm, pool_w_bf, sm["pool_scale"])

    w_in = wt["w_in"]
    parts = [(dp, 0, d_pool), (ds, d_pool, d_ssm), (dg_pool, off_gp, d), (dg_ssm, off_gs, d)]
    w_in_parts = [w_in[o0:o0 + width] for _, o0, width in parts]
    gb["w_in"] = jnp.concatenate(
        [_mm1(f"in_proj_dw{k}", "tn", p_[0], u, p_[2], d, _pick(p_[2], 1024), tnx, BF16) for k, p_ in enumerate(parts)], axis=0)
    pin = emit("grads_main", gb=gb)
    du = _mm("in_proj_du", "nn", [p_[0] for p_ in parts], w_in_parts, [[(k, k) for k in range(4)]], t, d, tmm, tnx, [],
             lambda accs: (accs[0],), [(_out(t, d, F32), None)], after=pin)[0]
    dh1, dh1_bf, gs["mix_norm"] = _rms_bwd("mix_norm_bwd", h1, sm["mix_norm"], du, dh2)
    pin = emit("small_early", gs=gs, loss=loss)

    def ffn1_weights_done(d_wg, d_wu, d_wd):
        gb["ffn1_w_gate"], gb["ffn1_w_up"], gb["ffn1_w_down"] = d_wg, d_wu, d_wd
        return emit("grads_ffn1", gb=gb)

    dx, _, gs["ffn1_norm"], _, _, _ = _ffn_bwd(
        "ffn1", x, sm["ffn1_norm"], wt["ffn1_w_gate"], wt["ffn1_w_up"], wt["ffn1_w_down"], ffn1_saved, dh1, dh1_bf,
        weights_done=ffn1_weights_done, after=pin)
    return loss, dx, gb, gs


WEIGHTS = ["ffn1_norm", "ffn1_w_gate", "ffn1_w_up", "ffn1_w_down", "mix_norm", "w_in", "pool_w", "pool_scale",
           "w_pool_proj", "ssm_a_re", "ssm_a_im", "ssm_log_dt", "ssm_b_re", "ssm_b_im", "ssm_c_re", "ssm_c_im", "ssm_d",
           "w_glu_val", "w_glu_gate", "w_mix_out", "xattn_norm", "mem_norm", "w_q", "w_kv", "w_xo", "ffn2_norm",
           "ffn2_w_gate", "ffn2_w_up", "ffn2_w_down", "final_norm"]
COL_SHARDED = ["ffn1_w_gate", "ffn1_w_up", "w_in", "w_pool_proj", "w_glu_val", "w_glu_gate", "w_kv", "ffn2_w_gate",
               "ffn2_w_up"]
ROW_SHARDED = ["ffn1_w_down", "w_mix_out", "w_q", "w_xo", "ffn2_w_down"]
BIG = [n for n in WEIGHTS if n in COL_SHARDED or n in ROW_SHARDED]
SMALL = [n for n in WEIGHTS if n not in BIG]
FFN1_BIG = ["ffn1_w_gate", "ffn1_w_up", "ffn1_w_down"]
MAIN_BIG = [n for n in BIG if n not in FFN1_BIG]
GATHER_PLAN = [("ffn1_up_done", ["ffn1_w_down"]), ("ffn1_fwd_done", ["w_in"]),
               ("mix_in_done", ["w_pool_proj", "w_glu_val", "w_glu_gate", "w_mix_out"]),
               ("mix_done", ["w_q", "w_kv", "w_xo"]), ("xattn_done", ["ffn2_w_gate", "ffn2_w_up", "ffn2_w_down"])]
MINOR_SWAPPED = ["ssm_b_re", "ssm_b_im"]
LATE_SMALL = "ffn1_norm"
EARLY_SMALL = [n for n in SMALL if n != LATE_SMALL]
PACK_ROWS = SUBLANES * LANES
GRAD_ROW_TILE = 256
ADAMW_STEP_WORDS = 1 << 19


def _to_rows(name, w):
    return w.T if name in COL_SHARDED else w


def _pack_small(vals):
    flat = []
    for v in vals:
        f = v.reshape(-1)
        flat.append(jnp.pad(f, (0, (-f.shape[0]) % PACK_ROWS)))
    total = sum(f.shape[0] for f in flat)
    flat.append(jnp.zeros(((-total) % (GRAD_ROW_TILE * LANES),), F32))
    return jnp.concatenate(flat).reshape(-1, LANES)


def _unpack_small(packed, shapes):
    out, row = [], 0
    for shp in shapes:
        size = math.prod(shp)
        rows = -(-size // PACK_ROWS) * SUBLANES
        out.append(packed[row:row + rows].reshape(-1)[:size].reshape(shp))
        row += rows
    return out


def kernel(x, mem, ffn1_norm, ffn1_w_gate, ffn1_w_up, ffn1_w_down, mix_norm, w_in, pool_w, pool_scale, w_pool_proj, ssm_a_re, ssm_a_im, ssm_log_dt, ssm_b_re, ssm_b_im, ssm_c_re, ssm_c_im, ssm_d, w_glu_val, w_glu_gate, w_mix_out, xattn_norm, mem_norm, w_q, w_kv, w_xo, ffn2_norm, ffn2_w_gate, ffn2_w_up, ffn2_w_down, final_norm, loss_target, m_ffn1_norm, m_ffn1_w_gate, m_ffn1_w_up, m_ffn1_w_down, m_mix_norm, m_w_in, m_pool_w, m_pool_scale, m_w_pool_proj, m_ssm_a_re, m_ssm_a_im, m_ssm_log_dt, m_ssm_b_re, m_ssm_b_im, m_ssm_c_re, m_ssm_c_im, m_ssm_d, m_w_glu_val, m_w_glu_gate, m_w_mix_out, m_xattn_norm, m_mem_norm, m_w_q, m_w_kv, m_w_xo, m_ffn2_norm, m_ffn2_w_gate, m_ffn2_w_up, m_ffn2_w_down, m_final_norm, v_ffn1_norm, v_ffn1_w_gate, v_ffn1_w_up, v_ffn1_w_down, v_mix_norm, v_w_in, v_pool_w, v_pool_scale, v_w_pool_proj, v_ssm_a_re, v_ssm_a_im, v_ssm_log_dt, v_ssm_b_re, v_ssm_b_im, v_ssm_c_re, v_ssm_c_im, v_ssm_d, v_w_glu_val, v_w_glu_gate, v_w_mix_out, v_xattn_norm, v_mem_norm, v_w_q, v_w_kv, v_w_xo, v_ffn2_norm, v_ffn2_w_gate, v_ffn2_w_up, v_ffn2_w_down, v_final_norm):
    given = dict(locals())
    wts = {n: given[n] for n in WEIGHTS}
    moms = {n: (given["m_" + n], given["v_" + n]) for n in WEIGHTS}
    x2, mem2, tgt2 = x[0], mem[0], loss_target[0]
    d = x2.shape[1]
    chip = (2 * lax.axis_index("x") + lax.axis_index("y")).astype(jnp.int32).reshape(1)

    def full_form(n, f):
        shard = wts[n][0].shape
        return f.reshape(N_DEV * shard[1], shard[0]) if n in COL_SHARDED else f.reshape(N_DEV * shard[0], shard[1])

    shards = {n: _to_rows(n, wts[n][0]).astype(BF16) for n in BIG}
    first = FFN1_BIG[:2]
    wt = {n: full_form(n, f) for n, f in zip(first, _allgather("weight_allgather_first", [shards[n] for n in first]))}
    started = _split_start("weight_gather_start", [_gather_group([shards[n] for n in names]) for _, names in GATHER_PLAN],
                           after=wt[first[0]])
    gathers = {event: (names, st) for (event, names), st in zip(GATHER_PLAN, started)}
    sm = {n: (wts[n].reshape(1, -1) if wts[n].ndim <= 2 else wts[n][0]) for n in SMALL}
    sm["ffn1_norm"] = sm["ffn1_norm"] + started[0]["token"][0, 0]

    pending = {}

    def reduce_start(tag, names, gb):
        blocks = [gb[n].reshape(N_DEV, -1, d) for n in names]
        pad_rows = (-sum(b.shape[1] for b in blocks)) % GRAD_ROW_TILE
        pad = [jnp.zeros((N_DEV, pad_rows, d), BF16)] if pad_rows else []
        started = _cores_start("grad_exchange_cores_start_" + tag, blocks + pad)
        own = jnp.concatenate([lax.dynamic_index_in_dim(b.reshape(4, 2, b.shape[1], d), lax.axis_index("c"), 1, False)
                               for b in started["bufs"][:len(blocks + pad)]], axis=1)
        _, (recv,) = _split_wait("grad_exchange_cores_wait_" + tag, started, own)
        rows_all = own.shape[1]
        pair = _ew("grad_pair_sum_" + tag, lambda a, b: (a.astype(F32) + b.astype(F32),),
                   [own.reshape(-1, d), recv.reshape(-1, d)], [BF16], rows_pref=5 * GRAD_ROW_TILE)[0]
        pair = pair.reshape(4, rows_all, d)
        pending[tag] = (pair, _chips_start("grad_exchange_chips_start_" + tag, pair), [b.shape[1] for b in blocks])
        return pending[tag][1]["token"]

    def reduce_finish(tag, after):
        _, started, rows = pending[tag]
        (pair,), (recv,) = _split_wait("grad_exchange_chips_wait_" + tag, started, after)
        return _chip_sum("grad_chip_sum_" + tag, pair, recv, chip), rows

    def ev(name, gb=None, gs=None, loss=None, marker=None):
        if name in gathers:
            names, started = gathers[name]
            for n, f in zip(names, _split_wait("weight_gather_wait_" + name, started, marker)[1]):
                wt[n] = full_form(n, f)
        elif name == "grads_main":
            return reduce_start("main", MAIN_BIG, gb)
        elif name == "small_early":
            pending["small"] = _slots_start("small_gather_start", _pack_small([gs[n] for n in EARLY_SMALL] + [loss[:, :1]]))
            return pending["small"]["token"]
        elif name == "grads_ffn1":
            return reduce_start("ffn1", FFN1_BIG, gb)
        return None

    _, dx, _, gs = _local_step(x2, mem2, tgt2, wt, sm, ev)

    grads = {}
    for tag, names in (("main", MAIN_BIG), ("ffn1", FFN1_BIG)):
        g_rows, rows = reduce_finish(tag, dx)
        off = 0
        for n, r in zip(names, rows):
            shard = wts[n].shape
            grads[n] = g_rows[off:off + r].reshape((shard[2], shard[1]) if n in COL_SHARDED else shard[1:])
            off += r
    small_sum = _sum_slots("small_sum", _split_wait("small_gather_wait", pending["small"], dx)[1][0], F32)
    late = _allgather("small_allgather_late", [gs[LATE_SMALL].reshape(-1, LANES)])[0]
    late_sum = _sum_slots("small_sum_late", late.reshape(N_DEV, -1, LANES), F32)
    vals = _unpack_small(small_sum, [wts[n].shape for n in EARLY_SMALL] + [(1, 1)])
    total_loss = vals[-1].reshape(())
    def flat(n, a):
        a = a.reshape(wts[n].shape)
        a = jnp.swapaxes(a, -1, -2) if n in MINOR_SWAPPED else a
        return a.reshape(-1, a.shape[-1])

    def unflat(n, a):
        shape = wts[n].shape
        if n in MINOR_SWAPPED:
            return jnp.swapaxes(a.reshape(shape[:-2] + (shape[-1], shape[-2])), -1, -2)
        return a.reshape(shape)

    for n, g_full in zip(EARLY_SMALL + [LATE_SMALL], vals[:-1] + [late_sum]):
        grads[n] = flat(n, g_full)

    out_g, out_d, out_m, out_v = {}, {}, {}, {}
    by_shape = {}
    for n in WEIGHTS:
        by_shape.setdefault((flat(n, wts[n]).shape, n in COL_SHARDED), []).append(n)
    for (_, transposed), names in by_shape.items():
        items = [(flat(n, wts[n]), grads[n], flat(n, moms[n][0]), flat(n, moms[n][1])) for n in names]
        for n, res in zip(names, _adamw_group("adamw_" + names[0], items, transposed)):
            out_d[n], out_m[n], out_v[n], out_g[n] = (unflat(n, a) for a in res)

    return (total_loss, dx[None], *[out_g[n] for n in WEIGHTS], *[out_d[n] for n in WEIGHTS],
            *[out_m[n] for n in WEIGHTS], *[out_v[n] for n in WEIGHTS])
```

```python
import functools
import math

import jax
import jax.numpy as jnp
from jax import lax
from jax.experimental import pallas as pl
from jax.experimental.pallas import tpu as pltpu

F32 = jnp.float32
BF16 = jnp.bfloat16
EPS = 1e-6
N_XHEADS = 4
POOL_WINDOWS = (2, 4, 8, 16)
ADAM_LR = 0.001
ADAM_B1 = 0.9
ADAM_B2 = 0.999
ADAM_EPS = 1e-08
ADAM_WD = 0.01
ADAM_STEP = 10
N_DEV = 8
VMEM_LIMIT_V7X = 48 * 1024 * 1024
LANES = 128
SUBLANES = 8
SUB_ROWS = 256
POOL_PAD = 16
MESH = pl.DeviceIdType.MESH
ANY = pl.BlockSpec(memory_space=pl.ANY)
HBM = pl.BlockSpec(memory_space=pltpu.HBM)
SEM = pl.BlockSpec(memory_space=pltpu.SEMAPHORE)
SIDE_EFFECT = pltpu.SideEffectType.DATAFLOW_SIDE_EFFECTING

_DIMS = {
    "nt": (((1,), (1,)), ((), ())),
    "nn": (((1,), (0,)), ((), ())),
    "tn": (((0,), (0,)), ((), ())),
}


def _pick(dim, pref, mult=LANES):
    if dim <= pref:
        return dim
    for t in range(pref - pref % mult, 0, -mult):
        if dim % t == 0:
            return t
    return dim


def _params(sem):
    return pltpu.CompilerParams(dimension_semantics=sem, vmem_limit_bytes=VMEM_LIMIT_V7X)


def _tile(tm, tn, coff=0):
    return pl.BlockSpec((tm, tn), lambda i, j: (i, j + coff))


def _rowvec(tn, coff=0):
    return pl.BlockSpec((1, tn), lambda i, j: (0, j + coff))


def _out(m, n, dtype):
    return jax.ShapeDtypeStruct((m, n), dtype)


def _mm(name, form, a_list, b_list, groups, m, n, tm, tn, extras, epilogue, outs, after=None, sub=SUB_ROWS):
    na, nb, ne = len(a_list), len(b_list), len(extras)
    pins = [] if after is None else [after]
    step = tm if (sub is None or form == "tn" or tm % sub) else sub

    def a_spec(a):
        if form == "tn":
            return pl.BlockSpec((a.shape[0], tm), lambda i, j: (0, i))
        return pl.BlockSpec((tm, a.shape[1]), lambda i, j: (i, 0))

    def b_spec(b):
        if form == "nt":
            return pl.BlockSpec((tn, b.shape[1]), lambda i, j: (j, 0))
        return pl.BlockSpec((b.shape[0], tn), lambda i, j: (0, j))

    def body(*refs):
        a_refs, b_refs = refs[:na], refs[na:na + nb]
        e_refs, o_refs = refs[na + nb:na + nb + ne], refs[na + nb + ne + len(pins):]
        b_vals = {}
        for s0 in range(0, tm, step):
            rows = slice(None) if step == tm else pl.ds(s0, step)
            a_vals, accs = {}, []
            for group in groups:
                acc = None
                for ai, bi in group:
                    if ai not in a_vals:
                        a_vals[ai] = (a_refs[ai][...] if form == "tn" else a_refs[ai][rows, :]).astype(BF16)
                    if bi not in b_vals:
                        b_vals[bi] = b_refs[bi][...].astype(BF16)
                    d = lax.dot_general(a_vals[ai], b_vals[bi], _DIMS[form], preferred_element_type=F32)
                    acc = d if acc is None else acc + d
                accs.append(acc)
            res = epilogue(accs, *[e[rows, :] if e.shape[0] == tm else e[...] for e in e_refs])
            for o_ref, r in zip(o_refs, res):
                o_ref[rows, :] = r.astype(o_ref.dtype)

    out_specs = [_tile(tm, tn) if s is None else s for _, s in outs]
    res = pl.pallas_call(
        body, name=name, grid=(m // tm, n // tn),
        in_specs=[a_spec(a) for a in a_list] + [b_spec(b) for b in b_list] + [s for _, s in extras] + [ANY] * len(pins),
        out_specs=out_specs, out_shape=[o for o, _ in outs],
        compiler_params=_params(("parallel", "parallel")),
    )(*a_list, *b_list, *[e for e, _ in extras], *pins)
    return res


def _mm1(name, form, a, b, m, n, tm, tn, dtype, scale=None):
    epi = (lambda accs: (accs[0],)) if scale is None else (lambda accs: (accs[0] * scale,))
    return _mm(name, form, [a], [b], [[(0, 0)]], m, n, tm, tn, [], epi, [(_out(m, n, dtype), None)])[0]


def _rms_fwd(name, h, g):
    t, d = h.shape
    tm = _pick(t, 512, SUBLANES)

    def body(h_ref, g_ref, n_ref):
        hv = h_ref[...]
        r = lax.rsqrt(jnp.mean(hv * hv, axis=-1, keepdims=True) + EPS)
        n_ref[...] = ((hv * r) * g_ref[...]).astype(BF16)

    return pl.pallas_call(
        body, name=name, grid=(t // tm,),
        in_specs=[pl.BlockSpec((tm, d), lambda i: (i, 0)), pl.BlockSpec((1, d), lambda i: (0, 0))],
        out_specs=pl.BlockSpec((tm, d), lambda i: (i, 0)), out_shape=_out(t, d, BF16),
        compiler_params=_params(("parallel",)),
    )(h, g)


def _rms_bwd(name, h, g, dn, dres=None):
    t, d = h.shape
    tm = _pick(t, 512, SUBLANES)
    need_dh = dres is not None

    def body(*refs):
        if need_dh:
            h_ref, g_ref, dn_ref, dres_ref, dh_ref, dhb_ref, dg_ref = refs
        else:
            h_ref, g_ref, dn_ref, dg_ref = refs
        hv = h_ref[...]
        r = lax.rsqrt(jnp.mean(hv * hv, axis=-1, keepdims=True) + EPS)
        nh = hv * r
        dnv = dn_ref[...].astype(F32)

        @pl.when(pl.program_id(0) == 0)
        def _():
            dg_ref[...] = jnp.zeros_like(dg_ref)

        dg_ref[...] += jnp.sum(dnv * nh, axis=0, keepdims=True)
        if need_dh:
            dng = dnv * g_ref[...]
            dh = dres_ref[...] + r * (dng - nh * jnp.mean(dng * nh, axis=-1, keepdims=True))
            dh_ref[...] = dh
            dhb_ref[...] = dh.astype(BF16)

    row = pl.BlockSpec((tm, d), lambda i: (i, 0))
    vec = pl.BlockSpec((1, d), lambda i: (0, 0))
    if need_dh:
        return pl.pallas_call(
            body, name=name, grid=(t // tm,), in_specs=[row, vec, row, row], out_specs=[row, row, vec],
            out_shape=[_out(t, d, F32), _out(t, d, BF16), _out(1, d, F32)], compiler_params=_params(("arbitrary",)),
        )(h, g, dn, dres)
    return pl.pallas_call(
        body, name=name, grid=(t // tm,), in_specs=[row, vec, row], out_specs=vec,
        out_shape=_out(1, d, F32), compiler_params=_params(("arbitrary",)),
    )(h, g, dn)


def _loss_head(h, g, tgt):
    t, d = h.shape
    tm = _pick(t, 512, SUBLANES)

    def body(h_ref, g_ref, t_ref, dh_ref, dhb_ref, dg_ref, loss_ref):
        hv = h_ref[...]
        r = lax.rsqrt(jnp.mean(hv * hv, axis=-1, keepdims=True) + EPS)
        nh = hv * r
        err = nh * g_ref[...] - t_ref[...]

        @pl.when(pl.program_id(0) == 0)
        def _():
            dg_ref[...] = jnp.zeros_like(dg_ref)
            loss_ref[...] = jnp.zeros_like(loss_ref)

        per_row = jnp.mean(err * err, axis=-1, keepdims=True)
        loss_ref[...] += 0.5 * jnp.sum(per_row, axis=0, keepdims=True)
        dy = err * (1.0 / d)
        dg_ref[...] += jnp.sum(dy * nh, axis=0, keepdims=True)
        dng = dy * g_ref[...]
        dh = r * (dng - nh * jnp.mean(dng * nh, axis=-1, keepdims=True))
        dh_ref[...] = dh
        dhb_ref[...] = dh.astype(BF16)

    row = pl.BlockSpec((tm, d), lambda i: (i, 0))
    vec = pl.BlockSpec((1, d), lambda i: (0, 0))
    return pl.pallas_call(
        body, name="loss_head", grid=(t // tm,), in_specs=[row, vec, row],
        out_specs=[row, row, vec, pl.BlockSpec((1, LANES), lambda i: (0, 0))],
        out_shape=[_out(t, d, F32), _out(t, d, BF16), _out(1, d, F32), _out(1, LANES, F32)],
        compiler_params=_params(("arbitrary",)),
    )(h, g, tgt)


def _ffn_fwd(tag, h, n, wg_t, wu_t, wd):
    t, d = h.shape
    f = wg_t.shape[0]
    tm, tn = _pick(t, 1024), _pick(f, 1408)

    def up_epi(accs):
        a, b = accs
        return a, b, (a * jax.nn.sigmoid(a)) * b

    a, b, hid = _mm(tag + "_up", "nt", [n], [wg_t, wu_t], [[(0, 0)], [(0, 1)]], t, f, tm, tn, [], up_epi,
                    [(_out(t, f, BF16), None)] * 3)
    if callable(wd):
        wd = wd(hid)
    tm2, tn2 = _pick(t, 1024), _pick(d, 512)
    h_out = _mm(tag + "_down", "nn", [hid], [wd], [[(0, 0)]], t, d, tm2, tn2, [(h, _tile(tm2, tn2))],
                lambda accs, hin: (hin + 0.5 * accs[0],), [(_out(t, d, F32), None)])[0]
    return h_out, (n, a, b, hid)


def _ffn_bwd(tag, h, g, wg_t, wu_t, wd, saved, dh, dh_bf, weights_done=None, after=None):
    n, a, b, hid = saved
    t, d = h.shape
    f = wd.shape[0]
    tm, tn = _pick(t, 1024), _pick(f, 1408)

    def hid_epi(accs, av, bv):
        dhid = 0.5 * accs[0]
        av, bv = av.astype(F32), bv.astype(F32)
        sig = jax.nn.sigmoid(av)
        da = dhid * bv * (sig * (1.0 + av * (1.0 - sig)))
        db = dhid * (av * sig)
        return da, db

    da, db = _mm(tag + "_bwd_hid", "nt", [dh_bf], [wd], [[(0, 0)]], t, f, tm, tn,
                 [(a, _tile(tm, tn)), (b, _tile(tm, tn))], hid_epi, [(_out(t, f, BF16), None)] * 2, after=after)
    tw, tnw = _pick(f, 1408), _pick(d, 512)
    d_wd = _mm1(tag + "_dwd", "tn", hid, dh_bf, f, d, tw, tnw, BF16, scale=0.5)
    d_wg = _mm1(tag + "_dwg", "tn", da, n, f, d, tw, tnw, BF16)
    d_wu = _mm1(tag + "_dwu", "tn", db, n, f, d, tw, tnw, BF16)
    pin = weights_done(d_wg, d_wu, d_wd) if weights_done is not None else None
    tm2, tn2 = _pick(t, 1024), _pick(d, 512)
    dn = _mm(tag + "_dn", "nn", [da, db], [wg_t, wu_t], [[(0, 0), (1, 1)]], t, d, tm2, tn2, [],
             lambda accs: (accs[0],), [(_out(t, d, F32), None)], after=pin)[0]
    dh_in, dh_in_bf, dg = _rms_bwd(tag + "_norm_bwd", h, g, dn, dh)
    return dh_in, dh_in_bf, dg, d_wg, d_wu, d_wd


def _window_sum(win, offsets):
    n = win.shape[0]
    acc = None
    for j in offsets:
        term = win if j == 0 else pltpu.roll(win, (-j) % n, 0)
        acc = term if acc is None else acc + term
    return acc


def _pool_counts(r0, ch, c, left, right, t):
    pos = r0 + lax.broadcasted_iota(jnp.int32, (ch, c), 0)
    return (jnp.minimum(pos + right + 1, t) - jnp.maximum(pos - left, 0)).astype(F32)


def _pool_fwd(proj, pool_w_bf, pool_scale):
    t = proj.shape[0]
    ng, c, _ = pool_w_bf.shape
    ch = _pick(t, 256, SUBLANES)
    pad = POOL_PAD

    def body(p_ref, w_ref, s_ref, pooled_ref, pm_ref, buf):
        grp = pl.program_id(0)
        buf[pl.ds(0, pad), :] = jnp.zeros((pad, c), F32)
        buf[pl.ds(pad + t, pad), :] = jnp.zeros((pad, c), F32)

        def fill(ci, carry):
            r0 = pl.multiple_of(ci * ch, SUBLANES)
            buf[pl.ds(pl.multiple_of(r0 + pad, SUBLANES), ch), :] = p_ref[pl.ds(r0, ch), :]
            return carry

        lax.fori_loop(0, t // ch, fill, 0)
        for gi, w in enumerate(POOL_WINDOWS):
            left = w // 2
            right = w - 1 - left

            @pl.when(grp == gi)
            def _(left=left, right=right):
                def chunk(ci, carry):
                    r0 = pl.multiple_of(ci * ch, SUBLANES)
                    win = buf[pl.ds(r0, ch + 2 * pad), :]
                    s = _window_sum(win, range(-left, right + 1))[pad:pad + ch]
                    pooled = s / _pool_counts(r0, ch, c, left, right, t) - win[pad:pad + ch]
                    pooled_bf = pooled.astype(BF16)
                    mixed = jnp.dot(pooled_bf, w_ref[0], preferred_element_type=F32)
                    pooled_ref[pl.ds(r0, ch), :] = pooled_bf
                    pm_ref[pl.ds(r0, ch), :] = (mixed * s_ref[...]).astype(BF16)
                    return carry

                lax.fori_loop(0, t // ch, chunk, 0)

    col = pl.BlockSpec((t, c), lambda g: (0, g))
    return pl.pallas_call(
        body, name="pool_fwd", grid=(ng,),
        in_specs=[col, pl.BlockSpec((1, c, c), lambda g: (g, 0, 0)), pl.BlockSpec((1, c), lambda g: (0, g))],
        out_specs=[col, col], out_shape=[_out(t, ng * c, BF16), _out(t, ng * c, BF16)],
        scratch_shapes=[pltpu.VMEM((t + 2 * pad, c), F32)],
        compiler_params=_params(("parallel",)),
    )(proj, pool_w_bf, pool_scale)


def _pool_bwd(pooled, dpm, pool_w_bf, pool_scale):
    t = pooled.shape[0]
    ng, c, _ = pool_w_bf.shape
    ch = _pick(t, 256, SUBLANES)
    pad = POOL_PAD

    def body(pooled_ref, dpm_ref, w_ref, s_ref, dp_ref, dw_ref, ds_ref, buf, raw):
        grp = pl.program_id(0)
        buf[pl.ds(0, pad), :] = jnp.zeros((pad, c), F32)
        buf[pl.ds(pad + t, pad), :] = jnp.zeros((pad, c), F32)
        dw_ref[...] = jnp.zeros_like(dw_ref)
        ds_ref[...] = jnp.zeros_like(ds_ref)
        for gi, w in enumerate(POOL_WINDOWS):
            left = w // 2
            right = w - 1 - left

            @pl.when(grp == gi)
            def _(left=left, right=right):
                def first(ci, carry):
                    r0 = pl.multiple_of(ci * ch, SUBLANES)
                    pv = pooled_ref[pl.ds(r0, ch), :]
                    dpm_v = dpm_ref[pl.ds(r0, ch), :]
                    mixed = jnp.dot(pv, w_ref[0], preferred_element_type=F32)
                    ds_ref[...] += jnp.sum(dpm_v * mixed, axis=0, keepdims=True)
                    dmixed = (dpm_v * s_ref[...]).astype(BF16)
                    dw_ref[0] += lax.dot_general(pv, dmixed, _DIMS["tn"], preferred_element_type=F32)
                    dpooled = lax.dot_general(dmixed, w_ref[0], _DIMS["nt"], preferred_element_type=F32)
                    raw[pl.ds(r0, ch), :] = dpooled
                    buf[pl.ds(pl.multiple_of(r0 + pad, SUBLANES), ch), :] = (
                        dpooled / _pool_counts(r0, ch, c, left, right, t))
                    return carry

                lax.fori_loop(0, t // ch, first, 0)

                def second(ci, carry):
                    r0 = pl.multiple_of(ci * ch, SUBLANES)
                    win = buf[pl.ds(r0, ch + 2 * pad), :]
                    s = _window_sum(win, range(-right, left + 1))[pad:pad + ch]
                    dp_ref[pl.ds(r0, ch), :] = (s - raw[pl.ds(r0, ch), :]).astype(BF16)
                    return carry

                lax.fori_loop(0, t // ch, second, 0)

    col = pl.BlockSpec((t, c), lambda g: (0, g))
    return pl.pallas_call(
        body, name="pool_bwd", grid=(ng,),
        in_specs=[col, col, pl.BlockSpec((1, c, c), lambda g: (g, 0, 0)), pl.BlockSpec((1, c), lambda g: (0, g))],
        out_specs=[col, pl.BlockSpec((1, c, c), lambda g: (g, 0, 0)), pl.BlockSpec((1, c), lambda g: (0, g))],
        out_shape=[_out(t, ng * c, BF16), jax.ShapeDtypeStruct((ng, c, c), F32), _out(1, ng * c, F32)],
        scratch_shapes=[pltpu.VMEM((t + 2 * pad, c), F32), pltpu.VMEM((t, c), F32)],
        compiler_params=_params(("parallel",)),
    )(pooled, dpm, pool_w_bf, pool_scale)


def _discretise(a_re, a_im, log_dt, b_re, b_im):
    dt = jnp.exp(log_dt)
    mag = jnp.exp(dt * a_re)
    ang = dt * a_im
    abr = mag * jnp.cos(ang)
    abi = mag * jnp.sin(ang)
    den = a_re * a_re + a_im * a_im
    nr = abr - 1.0
    qr = (nr * a_re + abi * a_im) / den
    qi = (abi * a_re - nr * a_im) / den
    return abr, abi, qr * b_re - qi * b_im, qr * b_im + qi * b_re


def _ssm_disc(args):
    def body(ar, ai, ld, br, bi, o1, o2, o3, o4):
        res = _discretise(ar[...], ai[...], ld[...], br[...], bi[...])
        for o, r in zip((o1, o2, o3, o4), res):
            o[...] = r

    like = lambda a: jax.ShapeDtypeStruct(a.shape, F32)
    return pl.pallas_call(
        body, name="ssm_disc", out_shape=[like(args[0]), like(args[0]), like(args[3]), like(args[3])],
    )(*args)


def _ssm_disc_bwd(args, cots):
    def body(ar, ai, ld, br, bi, c1, c2, c3, c4, o1, o2, o3, o4, o5):
        _, vjp = jax.vjp(_discretise, ar[...], ai[...], ld[...], br[...], bi[...])
        res = vjp((c1[...], c2[...], c3[...], c4[...]))
        for o, r in zip((o1, o2, o3, o4, o5), res):
            o[...] = r

    return pl.pallas_call(
        body, name="ssm_disc_bwd", out_shape=[jax.ShapeDtypeStruct(a.shape, F32) for a in args],
    )(*args, *cots)


def _cmul(pr, pi, qr, qi):
    return pr * qr - pi * qi, pr * qi + pi * qr


def _cpow(pr, pi, n):
    rr, ri = None, None
    while n:
        if n & 1:
            rr, ri = (pr, pi) if rr is None else _cmul(rr, ri, pr, pi)
        n >>= 1
        if n:
            pr, pi = _cmul(pr, pi, pr, pi)
    return rr, ri


def _segment_carry(er, ei, pr, pi, reverse):
    row = lax.broadcasted_iota(jnp.int32, er.shape, 0)
    cr, ci = jnp.zeros_like(er), jnp.zeros_like(ei)
    for _ in range(SUBLANES - 1):
        tr = er + pr * cr - pi * ci
        ti = ei + pr * ci + pi * cr
        if reverse:
            keep, shift = row < SUBLANES - 1, SUBLANES - 1
        else:
            keep, shift = row >= 1, 1
        cr = jnp.where(keep, pltpu.roll(tr, shift, 0), 0.0)
        ci = jnp.where(keep, pltpu.roll(ti, shift, 0), 0.0)
    return cr, ci


def _ssm_fwd(name, sp, b_re, b_im, c_re, c_im, ar, ai, reverse):
    t, c = sp.shape
    s = ar.shape[1]
    w = _pick(s, 512)
    ch = _pick(t, 512, SUBLANES)
    n_ch, gpc, steps = t // ch, ch // SUBLANES, t // SUBLANES

    def body(sp_ref, bre_ref, bim_ref, cre_ref, cim_ref, ar_ref, ai_ref, xr_ref, xi_ref, y_ref, ur, ui, xbr, xbi):
        a_r = jnp.broadcast_to(ar_ref[...], (SUBLANES, w))
        a_i = jnp.broadcast_to(ai_ref[...], (SUBLANES, w))

        @pl.when(pl.program_id(0) == 0)
        def _():
            y_ref[...] = jnp.zeros_like(y_ref)

        def sweep(h0, store):
            def chunk(k, h):
                ci = n_ch - 1 - k if reverse else k
                rows = pl.ds(pl.multiple_of(ci * ch, ch), ch)
                spv = sp_ref[rows, :].astype(BF16)
                ur[...] = jnp.dot(spv, bre_ref[...], preferred_element_type=F32)
                ui[...] = jnp.dot(spv, bim_ref[...], preferred_element_type=F32)

                def group(g, hh):
                    gi = gpc - 1 - g if reverse else g
                    r0 = pl.multiple_of(gi * SUBLANES, SUBLANES)
                    hr, hi = hh
                    nr = a_r * hr - a_i * hi + ur[pl.ds(r0, SUBLANES), :]
                    ni = a_r * hi + a_i * hr + ui[pl.ds(r0, SUBLANES), :]
                    if store:
                        xbr[pl.ds(r0, SUBLANES), :] = nr
                        xbi[pl.ds(r0, SUBLANES), :] = ni
                    return nr, ni

                h = lax.fori_loop(0, gpc, group, h)
                if store:
                    xr16, xi16 = xbr[...].astype(BF16), xbi[...].astype(BF16)
                    xr_ref[rows, :] = xr16
                    xi_ref[rows, :] = xi16
                    y_ref[rows, :] += (lax.dot_general(xr16, cre_ref[...], _DIMS["nt"], preferred_element_type=F32)
                                       + lax.dot_general(xi16, cim_ref[...], _DIMS["nt"], preferred_element_type=F32))
                return h

            return lax.fori_loop(0, n_ch, chunk, h0)

        zero = jnp.zeros((SUBLANES, w), F32)
        er, ei = sweep((zero, zero), False)
        pr, pi = _cpow(ar_ref[...], ai_ref[...], steps)
        sweep(_segment_carry(er, ei, pr, pi, reverse), True)

    col = lambda i: (0, i)
    return pl.pallas_call(
        body, name=name, grid=(s // w,),
        in_specs=[pl.BlockSpec((t, c), lambda i: (0, 0))] + [pl.BlockSpec((c, w), col)] * 4
        + [pl.BlockSpec((1, w), col)] * 2,
        out_specs=[pl.BlockSpec((t, w), col), pl.BlockSpec((t, w), col), pl.BlockSpec((t, c), lambda i: (0, 0))],
        out_shape=[_out(t, s, BF16), _out(t, s, BF16), _out(t, c, F32)],
        scratch_shapes=[pltpu.VMEM((ch, w), F32)] * 4,
        compiler_params=_params(("arbitrary",)),
    )(sp, b_re, b_im, c_re, c_im, ar, ai)


def _ssm_bwd(name, dyp, c_re, c_im, xr, xi, ar, ai, reverse):
    t, c = dyp.shape
    s = ar.shape[1]
    w = _pick(s, 512)
    ch = _pick(t, 512, SUBLANES)
    n_ch, gpc, steps = t // ch, ch // SUBLANES, t // SUBLANES
    back = not reverse
    edge = 2 * SUBLANES

    def body(dy_ref, cre_ref, cim_ref, xr_ref, xi_ref, ar_ref, ai_ref, lr_ref, li_ref, dar_ref, dai_ref,
             gr, gi_, lbr, lbi, xbr, xbi):
        a_r = jnp.broadcast_to(ar_ref[...], (SUBLANES, w))
        a_i = -jnp.broadcast_to(ai_ref[...], (SUBLANES, w))
        row = lax.broadcasted_iota(jnp.int32, (SUBLANES, w), 0)

        def neighbours(ci, x_ref, buf):
            rows = pl.ds(pl.multiple_of(ci * ch, ch), ch)
            if reverse:
                buf[pl.ds(0, ch), :] = x_ref[rows, :].astype(F32)
                nxt = x_ref[pl.ds(pl.multiple_of(jnp.minimum(ci + 1, n_ch - 1) * ch, ch), edge), :].astype(F32)[:SUBLANES]
                first = x_ref[pl.ds(0, edge), :].astype(F32)[:SUBLANES]
                wrap = jnp.where(row < SUBLANES - 1, pltpu.roll(first, SUBLANES - 1, 0), 0.0)
                buf[pl.ds(ch, SUBLANES), :] = jnp.where(ci == n_ch - 1, wrap, nxt)
            else:
                buf[pl.ds(SUBLANES, ch), :] = x_ref[rows, :].astype(F32)
                prv = x_ref[pl.ds(pl.multiple_of(jnp.maximum(ci * ch - edge, 0), edge), edge), :].astype(F32)[SUBLANES:]
                last = x_ref[pl.ds(t - edge, edge), :].astype(F32)[SUBLANES:]
                wrap = jnp.where(row >= 1, pltpu.roll(last, 1, 0), 0.0)
                buf[pl.ds(0, SUBLANES), :] = jnp.where(ci == 0, wrap, prv)

        def sweep(h0, store):
            def chunk(k, carry):
                ci = n_ch - 1 - k if back else k
                rows = pl.ds(pl.multiple_of(ci * ch, ch), ch)
                dyv = dy_ref[rows, :].astype(BF16)
                gr[...] = jnp.dot(dyv, cre_ref[...], preferred_element_type=F32)
                gi_[...] = jnp.dot(dyv, cim_ref[...], preferred_element_type=F32)
                if store:
                    neighbours(ci, xr_ref, xbr)
                    neighbours(ci, xi_ref, xbi)

                def group(g, cc):
                    gidx = gpc - 1 - g if back else g
                    r0 = pl.multiple_of(gidx * SUBLANES, SUBLANES)
                    hr, hi = cc[0], cc[1]
                    nr = a_r * hr - a_i * hi + gr[pl.ds(r0, SUBLANES), :]
                    ni = a_r * hi + a_i * hr + gi_[pl.ds(r0, SUBLANES), :]
                    if not store:
                        return nr, ni
                    lbr[pl.ds(r0, SUBLANES), :] = nr
                    lbi[pl.ds(r0, SUBLANES), :] = ni
                    x0 = pl.multiple_of(r0 + SUBLANES, SUBLANES) if reverse else r0
                    xpr, xpi = xbr[pl.ds(x0, SUBLANES), :], xbi[pl.ds(x0, SUBLANES), :]
                    return nr, ni, cc[2] + nr * xpr + ni * xpi, cc[3] + ni * xpr - nr * xpi

                carry = lax.fori_loop(0, gpc, group, carry)
                if store:
                    lr_ref[rows, :] = lbr[...].astype(BF16)
                    li_ref[rows, :] = lbi[...].astype(BF16)
                return carry

            return lax.fori_loop(0, n_ch, chunk, h0)

        zero = jnp.zeros((SUBLANES, w), F32)
        er, ei = sweep((zero, zero), False)
        pr, pi = _cpow(ar_ref[...], -ai_ref[...], steps)
        cr, ci0 = _segment_carry(er, ei, pr, pi, back)
        _, _, dar, dai = sweep((cr, ci0, zero, zero), True)
        dar_ref[...] = jnp.sum(dar, axis=0, keepdims=True)
        dai_ref[...] = jnp.sum(dai, axis=0, keepdims=True)

    col = lambda i: (0, i)
    return pl.pallas_call(
        body, name=name, grid=(s // w,),
        in_specs=[pl.BlockSpec((t, c), lambda i: (0, 0)), pl.BlockSpec((c, w), col), pl.BlockSpec((c, w), col),
                  pl.BlockSpec((t, w), col), pl.BlockSpec((t, w), col), pl.BlockSpec((1, w), col), pl.BlockSpec((1, w), col)],
        out_specs=[pl.BlockSpec((t, w), col), pl.BlockSpec((t, w), col), pl.BlockSpec((1, w), col), pl.BlockSpec((1, w), col)],
        out_shape=[_out(t, s, BF16), _out(t, s, BF16), _out(1, s, F32), _out(1, s, F32)],
        scratch_shapes=[pltpu.VMEM((ch, w), F32)] * 4 + [pltpu.VMEM((ch + SUBLANES, w), F32)] * 2,
        compiler_params=_params(("parallel",)),
    )(dyp, c_re, c_im, xr, xi, ar, ai)


def _ssm_finish(y0, y1, sp, skip):
    t, c = sp.shape
    steps = t // SUBLANES
    w = _pick(c, LANES)

    def body(y0_ref, y1_ref, sp_ref, d_ref, y_ref, ys_ref):
        rows = pl.ds(pl.program_id(1), steps, stride=SUBLANES)
        y = y0_ref[rows, :] + y1_ref[rows, :] + sp_ref[rows, :] * d_ref[...]
        y_ref[...] = y
        ys_ref[...] = jax.nn.gelu(y).astype(BF16)

    whole = pl.BlockSpec((t, w), lambda j, k: (0, j))
    seg = pl.BlockSpec((steps, w), lambda j, k: (k, j))
    return pl.pallas_call(
        body, name="ssm_finish", grid=(c // w, SUBLANES),
        in_specs=[whole, whole, whole, pl.BlockSpec((1, w), lambda j, k: (0, j))], out_specs=[seg, seg],
        out_shape=[_out(t, c, F32), _out(t, c, BF16)], compiler_params=_params(("parallel", "arbitrary")),
    )(y0, y1, sp, skip)


def _to_segments(a):
    t, c = a.shape
    return a.reshape(SUBLANES, t // SUBLANES, c).transpose(1, 0, 2).reshape(t, c)


def _from_segments(a):
    t, c = a.shape
    return a.reshape(t // SUBLANES, SUBLANES, c).transpose(1, 0, 2).reshape(t, c)


def _colsum_prod(name, a, b, b_coff=0):
    t, n = a.shape
    tm = _pick(t, 512, SUBLANES)

    def body(a_ref, b_ref, o_ref):
        @pl.when(pl.program_id(0) == 0)
        def _():
            o_ref[...] = jnp.zeros_like(o_ref)

        o_ref[...] += jnp.sum(a_ref[...].astype(F32) * b_ref[...].astype(F32), axis=0, keepdims=True)

    return pl.pallas_call(
        body, name=name, grid=(t // tm,),
        in_specs=[pl.BlockSpec((tm, n), lambda i: (i, 0)), pl.BlockSpec((tm, n), lambda i: (i, b_coff))],
        out_specs=pl.BlockSpec((1, n), lambda i: (0, 0)), out_shape=_out(1, n, F32),
        compiler_params=_params(("arbitrary",)),
    )(a, b)


def _ssm_maps(arrs, signs):
    n2, hh, p = arrs[0].shape
    g = n2 // 2

    def body(*refs):
        ins, outs = refs[:len(arrs)], refs[len(arrs):]
        for a, (a_ref, sign) in enumerate(zip(ins, signs)):
            for d in range(2):
                o_ref = outs[2 * a + d]
                o_ref[...] = jnp.zeros_like(o_ref)
                for k in range(g):
                    o_ref[pl.ds(k * hh, hh), pl.ds(k * p, p)] = (sign * a_ref[d * g + k]).astype(BF16)

    outs = pl.pallas_call(body, name="ssm_maps", out_shape=[_out(g * hh, g * p, BF16)] * (2 * len(arrs)))(*arrs)
    return [outs[2 * a:2 * a + 2] for a in range(len(arrs))]


def _softmax(qh, kh, scale):
    s = lax.dot_general(qh, kh, _DIMS["nt"], preferred_element_type=F32) * scale
    e = jnp.exp(s - jnp.max(s, axis=-1, keepdims=True))
    return e / jnp.sum(e, axis=-1, keepdims=True)


def _attn_fwd(q, kv):
    t, d = q.shape
    mm_ = kv.shape[0]
    hd = d // N_XHEADS
    scale = 1.0 / math.sqrt(hd)
    tm = _pick(t, 512, SUBLANES)

    def body(q_ref, kv_ref, o_ref):
        for h in range(N_XHEADS):
            sl = pl.ds(h * hd, hd)
            p = _softmax(q_ref[:, sl], kv_ref[:, sl], scale)
            o_ref[:, sl] = jnp.dot(p.astype(BF16), kv_ref[:, pl.ds(d + h * hd, hd)],
                                   preferred_element_type=F32).astype(BF16)

    return pl.pallas_call(
        body, name="attn_fwd", grid=(t // tm,),
        in_specs=[pl.BlockSpec((tm, d), lambda i: (i, 0)), pl.BlockSpec((mm_, 2 * d), lambda i: (0, 0))],
        out_specs=pl.BlockSpec((tm, d), lambda i: (i, 0)), out_shape=_out(t, d, BF16),
        compiler_params=_params(("parallel",)),
    )(q, kv)


def _attn_bwd(q, kv, do):
    t, d = q.shape
    mm_ = kv.shape[0]
    hd = d // N_XHEADS
    scale = 1.0 / math.sqrt(hd)
    tm = _pick(t, 512, SUBLANES)

    def body(q_ref, kv_ref, do_ref, dq_ref, dkv_ref):
        @pl.when(pl.program_id(0) == 0)
        def _():
            dkv_ref[...] = jnp.zeros_like(dkv_ref)

        for h in range(N_XHEADS):
            sl = pl.ds(h * hd, hd)
            vsl = pl.ds(d + h * hd, hd)
            qh, kh, doh = q_ref[:, sl], kv_ref[:, sl], do_ref[:, sl]
            p = _softmax(qh, kh, scale)
            dp = lax.dot_general(doh, kv_ref[:, vsl], _DIMS["nt"], preferred_element_type=F32)
            dkv_ref[:, vsl] += lax.dot_general(p.astype(BF16), doh, _DIMS["tn"], preferred_element_type=F32)
            ds = (p * (dp - jnp.sum(dp * p, axis=-1, keepdims=True)) * scale).astype(BF16)
            dq_ref[:, sl] = jnp.dot(ds, kh, preferred_element_type=F32).astype(BF16)
            dkv_ref[:, sl] += lax.dot_general(ds, qh, _DIMS["tn"], preferred_element_type=F32)

    row = pl.BlockSpec((tm, d), lambda i: (i, 0))
    full = pl.BlockSpec((mm_, 2 * d), lambda i: (0, 0))
    return pl.pallas_call(
        body, name="attn_bwd", grid=(t // tm,), in_specs=[row, full, row], out_specs=[row, full],
        out_shape=[_out(t, d, BF16), _out(mm_, 2 * d, F32)], compiler_params=_params(("arbitrary",)),
    )(q, kv, do)


def _ew(name, fn, ins, outs, rows_pref=256):
    r, c = ins[0].shape
    tr = _pick(r, rows_pref, SUBLANES)
    ni = len(ins)

    def body(*refs):
        res = fn(*[x[...] for x in refs[:ni]])
        for o_ref, v in zip(refs[ni:], res):
            o_ref[...] = v.astype(o_ref.dtype)

    blk = pl.BlockSpec((tr, c), lambda i: (i, 0))
    return pl.pallas_call(
        body, name=name, grid=(r // tr,), in_specs=[blk] * ni, out_specs=[blk] * len(outs),
        out_shape=[_out(r, c, dt) for dt in outs], compiler_params=_params(("parallel",)),
    )(*ins)


def _sum_slots(name, a, dtype):
    s, r, c = a.shape
    tr = _pick(r, 256, SUBLANES)

    def body(a_ref, o_ref):
        acc = a_ref[0].astype(F32)
        for k in range(1, s):
            acc = acc + a_ref[k].astype(F32)
        o_ref[...] = acc.astype(o_ref.dtype)

    return pl.pallas_call(
        body, name=name, grid=(r // tr,), in_specs=[pl.BlockSpec((s, tr, c), lambda i: (0, i, 0))],
        out_specs=pl.BlockSpec((tr, c), lambda i: (i, 0)), out_shape=_out(r, c, dtype),
        compiler_params=_params(("parallel",)),
    )(a)


def _adamw_step(wv, gv, mv, vv):
    bc1 = 1.0 - ADAM_B1 ** ADAM_STEP
    bc2 = 1.0 - ADAM_B2 ** ADAM_STEP
    m2 = ADAM_B1 * mv + (1.0 - ADAM_B1) * gv
    v2 = ADAM_B2 * vv + (1.0 - ADAM_B2) * (gv * gv)
    delta = -ADAM_LR * ((m2 / bc1) / (jnp.sqrt(v2 / bc2) + ADAM_EPS) + ADAM_WD * wv)
    return delta, m2, v2


def _adamw_group(name, items, transposed):
    k, r = items[0][0].shape
    if transposed and r % LANES != 0:
        rows = _adamw_group(name, [(w.T, g, m.T, v.T) for w, g, m, v in items], False)
        return [[a.T for a in item] for item in rows]
    tk = _pick(k, max(SUBLANES, ADAMW_STEP_WORDS // (r * len(items))), SUBLANES)
    n_out = 4 if transposed else 3

    def body(*refs):
        ins, outs = refs[:4 * len(items)], refs[4 * len(items):]
        for i in range(len(items)):
            wv, gv, mv, vv = (a[...] for a in ins[4 * i:4 * i + 4])
            if transposed:
                gv = gv.T
            res = _adamw_step(wv, gv, mv, vv) + ((gv,) if transposed else ())
            for o_ref, val in zip(outs[n_out * i:n_out * (i + 1)], res):
                o_ref[...] = val

    blk = pl.BlockSpec((tk, r), lambda j: (j, 0))
    g_blk = pl.BlockSpec((r, tk), lambda j: (0, j)) if transposed else blk
    res = pl.pallas_call(
        body, name=name, grid=(k // tk,), in_specs=[blk, g_blk, blk, blk] * len(items),
        out_specs=[blk] * (n_out * len(items)), out_shape=[pltpu.HBM((k, r), F32)] * (n_out * len(items)),
        compiler_params=_params(("parallel",)),
    )(*[pltpu.with_memory_space_constraint(a, pltpu.HBM) for item in items for a in item])
    return [list(res[n_out * i:n_out * (i + 1)]) + ([] if transposed else [items[i][1]]) for i in range(len(items))]


def _allgather(name, arrs):
    n = len(arrs)

    def body(*refs):
        ins, outs = refs[:n], refs[n:2 * n]
        send_sems, recv_sems, local_sems = refs[2 * n:]
        x, y, c = lax.axis_index("x"), lax.axis_index("y"), lax.axis_index("c")
        me, sibling = (x, y, c), (x, y, 1 - c)
        chips = [(1 - x, y), (x, 1 - y), (1 - x, 1 - y)]

        def rows(a, px, py, pc):
            r = ins[a].shape[0]
            return outs[a].at[pl.ds((4 * px + 2 * py + pc) * r, r), :]

        def copy(a, k, block, to, src=None):
            return pltpu.make_async_remote_copy(
                src_ref=rows(a, *block) if src is None else src, dst_ref=rows(a, *block),
                send_sem=send_sems.at[a, k], recv_sem=recv_sems.at[a, k], device_id=to, device_id_type=MESH)

        mine = [pltpu.make_async_copy(ins[a], rows(a, *me), local_sems.at[a]) for a in range(n)]
        for cp in mine:
            cp.start()
        first = []
        for a in range(n):
            first.append(copy(a, 0, me, sibling, src=ins[a]))
            first += [copy(a, 1 + j, me, (*chip, c), src=ins[a]) for j, chip in enumerate(chips)]
        for cp in first:
            cp.start()
        passed = []
        for j, chip in enumerate(chips):
            for a in range(n):
                copy(a, 1 + j, (*chip, c), me).wait_recv()
                cp = copy(a, 4 + j, (*chip, c), sibling)
                cp.start()
                passed.append(cp)
        for a in range(n):
            copy(a, 0, sibling, me).wait_recv()
            for j, chip in enumerate(chips):
                copy(a, 4 + j, (*chip, 1 - c), me).wait_recv()
        for cp in first + passed:
            cp.wait_send()
        for cp in mine:
            cp.wait()

    return pl.pallas_call(
        body, name=name, in_specs=[ANY] * n, out_specs=[ANY] * n,
        out_shape=[_out(N_DEV * a.shape[0], a.shape[1], a.dtype) for a in arrs],
        scratch_shapes=[pltpu.SemaphoreType.DMA((n, 7)), pltpu.SemaphoreType.DMA((n, 7)), pltpu.SemaphoreType.DMA((n,))],
    )(*arrs)


def _cores_start(name, blocks):
    n = len(blocks)
    c = blocks[0].shape[2]
    r = sum(b.shape[1] for b in blocks)

    def build(src_refs, land_refs, send_sems, recv_sems):
        x, y, cc = lax.axis_index("x"), lax.axis_index("y"), lax.axis_index("c")
        remote, off = [], 0
        for a, src in enumerate(src_refs):
            rows = pl.ds(off, src.shape[1])
            off += src.shape[1]
            for q in range(4):
                remote.append(pltpu.make_async_remote_copy(
                    src_ref=src.at[2 * q + (1 - cc)], dst_ref=land_refs[0].at[q, rows], send_sem=send_sems.at[4 * a + q],
                    recv_sem=recv_sems.at[4 * a + q], device_id=(x, y, 1 - cc), device_id_type=MESH))
        return remote, []

    return _split_start(name, [(blocks, [jax.ShapeDtypeStruct((4, r, c), blocks[0].dtype)], 4 * n, 0, build)])[0]


def _peer(k, x, y, c):
    return (1 - x if k & 4 else x, 1 - y if k & 2 else y, 1 - c if k & 1 else c)


def _split_start(name, groups, after=None):
    pins = [] if after is None else [after]
    bufs, sem_shapes, spans = [], [], []
    for srcs, land_shapes, n_remote, n_local, _ in groups:
        sems = [pltpu.SemaphoreType.DMA((n_remote,)), pltpu.SemaphoreType.DMA((n_remote,))]
        sems += [pltpu.SemaphoreType.DMA((n_local,))] if n_local else []
        spans.append((len(bufs), len(srcs), len(land_shapes), len(sem_shapes), len(sems)))
        bufs += [pltpu.with_memory_space_constraint(a, pltpu.HBM) for a in srcs]
        bufs += [pltpu.with_memory_space_constraint(lax.empty(s.shape, s.dtype), pltpu.HBM) for s in land_shapes]
        sem_shapes += sems
    n_buf, n_sem = len(bufs), len(sem_shapes)

    def body(*refs):
        buf_refs, sem_refs, token = refs[:n_buf], refs[n_buf + len(pins):n_buf + len(pins) + n_sem], refs[-1]
        for (b0, ns, nl, s0, k), group in zip(spans, groups):
            remote, local = group[4](buf_refs[b0:b0 + ns], buf_refs[b0 + ns:b0 + ns + nl], *sem_refs[s0:s0 + k])
            for cp in local + remote:
                cp.start()
        token[...] = jnp.zeros_like(token)

    outs = pl.pallas_call(
        body, name=name,
        out_shape=sem_shapes + [pltpu.HBM(b.shape, b.dtype) for b in bufs] + [jax.ShapeDtypeStruct((SUBLANES, LANES), F32)],
        in_specs=[HBM] * n_buf + [ANY] * len(pins),
        out_specs=[SEM] * n_sem + [HBM] * n_buf + [pl.BlockSpec(memory_space=pltpu.VMEM)],
        input_output_aliases={i: n_sem + i for i in range(n_buf)},
        compiler_params=pltpu.CompilerParams(has_side_effects=SIDE_EFFECT),
    )(*bufs, *pins)
    return [dict(sems=list(outs[s0:s0 + k]), bufs=list(outs[n_sem + b0:n_sem + b0 + ns + nl]), token=outs[-1],
                 build=group[4], ns=ns) for (b0, ns, nl, s0, k), group in zip(spans, groups)]


def _split_wait(name, started, after):
    ns, n_buf, n_sem = started["ns"], len(started["bufs"]), len(started["sems"])

    def body(*refs):
        src_refs, land_refs = refs[:ns], refs[ns:n_buf]
        sems = refs[n_buf:n_buf + n_sem]
        remote, local = started["build"](src_refs, land_refs, *sems)
        for cp in local:
            cp.wait()
        for cp in remote:
            cp.wait_send()
            cp.wait_recv()

    outs = pl.pallas_call(
        body, name=name, out_shape=[pltpu.HBM(b.shape, b.dtype) for b in started["bufs"]],
        in_specs=[HBM] * n_buf + [SEM] * n_sem + [ANY], out_specs=[HBM] * n_buf,
        input_output_aliases={i: i for i in range(n_buf)},
        compiler_params=pltpu.CompilerParams(has_side_effects=SIDE_EFFECT),
    )(*started["bufs"], *started["sems"], after)
    return list(outs[:ns]), list(outs[ns:])


def _gather_group(shards):
    m = len(shards)

    def build(src_refs, land_refs, send_sems, recv_sems, local_sems):
        x, y, c = lax.axis_index("x"), lax.axis_index("y"), lax.axis_index("c")
        remote, local = [], []
        for j in range(m):
            r = src_refs[j].shape[0]
            dst = land_refs[j].at[pl.ds((4 * x + 2 * y + c) * r, r), :]
            local.append(pltpu.make_async_copy(src_refs[j], dst, local_sems.at[j]))
            for k in range(1, N_DEV):
                remote.append(pltpu.make_async_remote_copy(
                    src_ref=src_refs[j], dst_ref=dst, send_sem=send_sems.at[7 * j + k - 1],
                    recv_sem=recv_sems.at[7 * j + k - 1], device_id=_peer(k, x, y, c), device_id_type=MESH))
        return remote, local

    lands = [jax.ShapeDtypeStruct((N_DEV * a.shape[0], a.shape[1]), a.dtype) for a in shards]
    return shards, lands, 7 * m, m, build


def _slots_start(name, a):
    def build(src_refs, land_refs, send_sems, recv_sems, local_sems):
        x, y, c = lax.axis_index("x"), lax.axis_index("y"), lax.axis_index("c")
        dst = land_refs[0].at[4 * x + 2 * y + c]
        local = [pltpu.make_async_copy(src_refs[0], dst, local_sems.at[0])]
        remote = [pltpu.make_async_remote_copy(
            src_ref=src_refs[0], dst_ref=dst, send_sem=send_sems.at[k - 1], recv_sem=recv_sems.at[k - 1],
            device_id=_peer(k, x, y, c), device_id_type=MESH) for k in range(1, N_DEV)]
        return remote, local

    return _split_start(name, [([a], [jax.ShapeDtypeStruct((N_DEV,) + a.shape, a.dtype)], 7, 1, build)])[0]


def _chips_start(name, p):
    _, r, c = p.shape
    nck = r // GRAD_ROW_TILE

    def build(src_refs, land_refs, send_sems, recv_sems):
        x, y, cc = lax.axis_index("x"), lax.axis_index("y"), lax.axis_index("c")
        remote = []
        for k in range(1, 4):
            px = 1 - x if k >> 1 else x
            py = 1 - y if k & 1 else y
            for j in range(nck):
                rows = pl.ds(j * GRAD_ROW_TILE, GRAD_ROW_TILE)
                remote.append(pltpu.make_async_remote_copy(
                    src_ref=src_refs[0].at[2 * px + py, rows], dst_ref=land_refs[0].at[k - 1, rows],
                    send_sem=send_sems.at[(k - 1) * nck + j], recv_sem=recv_sems.at[(k - 1) * nck + j],
                    device_id=(px, py, cc), device_id_type=MESH))
        return remote, []

    return _split_start(name, [([p], [jax.ShapeDtypeStruct((3, r, c), p.dtype)], 3 * nck, 0, build)])[0]


def _chip_sum(name, p, recv, chip):
    _, r, c = p.shape
    tr = _pick(r, 5 * GRAD_ROW_TILE, GRAD_ROW_TILE)

    def body(chip_ref, p_ref, r_ref, o_ref):
        acc = p_ref[...].astype(F32)
        for k in range(3):
            acc = acc + r_ref[k].astype(F32)
        o_ref[...] = acc

    return pl.pallas_call(
        body, name=name,
        grid_spec=pltpu.PrefetchScalarGridSpec(
            num_scalar_prefetch=1, grid=(r // tr,),
            in_specs=[pl.BlockSpec((None, tr, c), lambda i, chip_ref: (chip_ref[0], i, 0)),
                      pl.BlockSpec((3, tr, c), lambda i, chip_ref: (0, i, 0))],
            out_specs=pl.BlockSpec((tr, c), lambda i, chip_ref: (i, 0))),
        out_shape=_out(r, c, F32), compiler_params=_params(("parallel",)),
    )(chip, p, recv)


def _local_step(x, mem, tgt, wt, sm, ev=None):
    t, d = x.shape
    n_mem = mem.shape[0]
    d_pool = sm["pool_scale"].shape[1]
    ng, pc = sm["pool_w"].shape[0], sm["pool_w"].shape[1]
    d_ssm = sm["ssm_d"].shape[1]
    _, sg, sp, sh = sm["ssm_b_re"].shape
    n_state = sg * sp
    gb, gs = {}, {}

    def emit(name, **kw):
        return ev(name, **kw) if ev is not None else None

    n1 = _rms_fwd("ffn1_norm", x, sm["ffn1_norm"])
    emit("ffn1_norm_done", marker=n1)
    def ffn1_down(hid):
        emit("ffn1_up_done", marker=hid)
        return wt["ffn1_w_down"]

    h1, ffn1_saved = _ffn_fwd("ffn1", x, n1, wt["ffn1_w_gate"], wt["ffn1_w_up"], ffn1_down)
    emit("ffn1_fwd_done", marker=h1)
    u = _rms_fwd("mix_norm", h1, sm["mix_norm"])
    d_in = wt["w_in"].shape[0]
    tm, tn = _pick(t, 1024), _pick(d_in, 1408)
    proj = _mm1("in_proj", "nt", u, wt["w_in"], t, d_in, tm, tn, F32)
    off_s = d_pool // d_ssm
    off_gp = (d_pool + d_ssm)
    off_gs = off_gp + d

    pool_w_bf = sm["pool_w"].astype(BF16)
    pooled, pm = _pool_fwd(proj, pool_w_bf, sm["pool_scale"])

    by_p = lambda a: jnp.swapaxes(a, -1, -2).reshape(2 * sg, sh, sp)
    disc_args = [sm["ssm_a_re"].reshape(2 * sg, 1, sp), sm["ssm_a_im"].reshape(2 * sg, 1, sp),
                 sm["ssm_log_dt"].reshape(2 * sg, 1, 1), by_p(sm["ssm_b_re"]), by_p(sm["ssm_b_im"])]
    abr, abi, bbr, bbi = _ssm_disc(disc_args)
    abr2, abi2 = abr.reshape(2, n_state), abi.reshape(2, n_state)
    b_re, b_im, c_re, c_im = _ssm_maps(
        [bbr, bbi, sm["ssm_c_re"].reshape(2 * sg, sh, sp), sm["ssm_c_im"].reshape(2 * sg, sh, sp)], [1.0, 1.0, 1.0, -1.0])
    sp32 = _to_segments(proj[:, d_pool:d_pool + d_ssm])
    xs, y_parts = [], []
    for dr in range(2):
        xr, xi, y_part = _ssm_fwd(f"ssm_fwd{dr}", sp32, b_re[dr], b_im[dr], c_re[dr], c_im[dr], abr2[dr:dr + 1],
                                  abi2[dr:dr + 1], reverse=(dr == 1))
        xs.append((xr, xi))
        y_parts.append(y_part)
    y, ys = _ssm_finish(y_parts[0], y_parts[1], sp32, sm["ssm_d"])
    tmy = _pick(t, 256)
    emit("mix_in_done", marker=ys)

    tmm, tnm, tnx = _pick(t, 2048), _pick(d, 256), _pick(d, 512)
    gp_spec = _tile(tmm, tnm, off_gp // tnm)
    gs_spec = _tile(tmm, tnm, off_gs // tnm)

    def merge_epi(accs, gpv, gsv):
        z_pool, val, gate = accs
        return (jax.nn.sigmoid(gpv) * z_pool + jax.nn.sigmoid(gsv) * (val * jax.nn.sigmoid(gate)),)

    merged = _mm("mix_merge", "nt", [pm, ys], [wt["w_pool_proj"], wt["w_glu_val"], wt["w_glu_gate"]],
                 [[(0, 0)], [(1, 1)], [(1, 2)]], t, d, tmm, tnm, [(proj, gp_spec), (proj, gs_spec)], merge_epi,
                 [(_out(t, d, BF16), None)])[0]
    res_epi = lambda accs, hin: (hin + accs[0],)
    h2 = _mm("mix_out", "nn", [merged], [wt["w_mix_out"]], [[(0, 0)]], t, d, tmm, tnx, [(h1, _tile(tmm, tnx))],
             res_epi, [(_out(t, d, F32), None)])[0]

    un = _rms_fwd("xattn_norm", h2, sm["xattn_norm"])
    mn = _rms_fwd("mem_norm", mem, sm["mem_norm"])
    emit("mix_done", marker=un)
    q = _mm1("xattn_q", "nn", un, wt["w_q"], t, d, tmm, tnx, BF16)
    kv = _mm1("xattn_kv", "nt", mn, wt["w_kv"], n_mem, 2 * d, n_mem, _pick(2 * d, 512), BF16)
    o = _attn_fwd(q, kv)
    h3 = _mm("xattn_out", "nn", [o], [wt["w_xo"]], [[(0, 0)]], t, d, tmm, tnx, [(h2, _tile(tmm, tnx))],
             res_epi, [(_out(t, d, F32), None)])[0]

    n2 = _rms_fwd("ffn2_norm", h3, sm["ffn2_norm"])
    emit("xattn_done", marker=n2)
    h4, ffn2_saved = _ffn_fwd("ffn2", h3, n2, wt["ffn2_w_gate"], wt["ffn2_w_up"], wt["ffn2_w_down"])

    dh4, dh4_bf, gs["final_norm"], loss = _loss_head(h4, sm["final_norm"], tgt)
    dh3, dh3_bf, gs["ffn2_norm"], gb["ffn2_w_gate"], gb["ffn2_w_up"], gb["ffn2_w_down"] = _ffn_bwd(
        "ffn2", h3, sm["ffn2_norm"], wt["ffn2_w_gate"], wt["ffn2_w_up"], wt["ffn2_w_down"], ffn2_saved, dh4, dh4_bf)

    tw = _pick(d, 1024)
    do = _mm1("xattn_do", "nt", dh3_bf, wt["w_xo"], t, d, tmm, tnx, BF16)
    gb["w_xo"] = _mm1("xattn_dwxo", "tn", o, dh3_bf, d, d, tw, tnx, BF16)
    dq, dkv = _attn_bwd(q, kv, do)
    gb["w_q"] = _mm1("xattn_dwq", "tn", un, dq, d, d, tw, tnx, BF16)
    dun = _mm1("xattn_dun", "nt", dq, wt["w_q"], t, d, tmm, tnx, F32)
    dh2, dh2_bf, gs["xattn_norm"] = _rms_bwd("xattn_norm_bwd", h2, sm["xattn_norm"], dun, dh3)
    gb["w_kv"] = _mm1("xattn_dwkv", "tn", dkv, mn, 2 * d, d, _pick(2 * d, 512), d, BF16)
    dmn = _mm1("xattn_dmn", "nn", dkv, wt["w_kv"], n_mem, d, n_mem, tnx, F32)
    gs["mem_norm"] = _rms_bwd("mem_norm_bwd", mem, sm["mem_norm"], dmn)

    gb["w_mix_out"] = _mm1("mix_dwout", "tn", merged, dh2_bf, d, d, tw, tnx, BF16)

    def merge_bwd_epi(accs, gpv, gsv):
        dmerged, z_pool, val, gate = accs
        sp_, ss_, sg_ = jax.nn.sigmoid(gpv), jax.nn.sigmoid(gsv), jax.nn.sigmoid(gate)
        glu = val * sg_
        dz_pool = dmerged * sp_
        dg_pool = dmerged * z_pool * (sp_ * (1.0 - sp_))
        dz_ssm = dmerged * ss_
        dg_ssm = dmerged * glu * (ss_ * (1.0 - ss_))
        dval = dz_ssm * sg_
        dgate = dz_ssm * glu * (1.0 - sg_)
        return dz_pool, dg_pool, dg_ssm, dval, dgate

    dz_pool, dg_pool, dg_ssm, dval, dgate = _mm(
        "mix_merge_bwd", "nt", [dh2_bf, pm, ys], [wt["w_mix_out"], wt["w_pool_proj"], wt["w_glu_val"], wt["w_glu_gate"]],
        [[(0, 0)], [(1, 1)], [(2, 2)], [(2, 3)]], t, d, tmm, tnm, [(proj, gp_spec), (proj, gs_spec)], merge_bwd_epi,
        [(_out(t, d, BF16), None)] * 5)
    gb["w_pool_proj"] = _mm1("pool_dwproj", "tn", dz_pool, pm, d, d_pool, tw, d_pool, BF16)
    gb["w_glu_val"] = _mm1("glu_dwval", "tn", dval, ys, d, d_ssm, tw, d_ssm, BF16)
    gb["w_glu_gate"] = _mm1("glu_dwgate", "tn", dgate, ys, d, d_ssm, tw, d_ssm, BF16)

    def gelu_bwd_epi(accs, yv):
        _, vjp = jax.vjp(jax.nn.gelu, yv)
        return (vjp(accs[0])[0],)

    dy = _mm("glu_dy", "nn", [dval, dgate], [wt["w_glu_val"], wt["w_glu_gate"]], [[(0, 0), (1, 1)]], t, d_ssm, tmy, d_ssm,
             [(y, _tile(tmy, d_ssm))], gelu_bwd_epi, [(_out(t, d_ssm, F32), None)])[0]
    gs["ssm_d"] = _colsum_prod("ssm_dd", dy, proj, b_coff=off_s)
    dyp = _to_segments(dy)
    d_abr, d_abi, d_bbr, d_bbi, d_cre, d_cim, lams = [], [], [], [], [], [], []
    ts = _pick(n_state, 512)

    def fold_diag(accs):
        first = pl.program_id(1) * (ts // sp)
        row_group = lax.broadcasted_iota(jnp.int32, (d_ssm, sp), 0) // sh
        folded = []
        for acc in accs:
            out = jnp.zeros((d_ssm, sp), F32)
            for k in range(ts // sp):
                out = out + jnp.where(row_group == first + k, acc[:, sp * k:sp * (k + 1)], 0.0)
            folded.append(out)
        return tuple(folded)

    for dr in range(2):
        lr, li, dar, dai = _ssm_bwd(f"ssm_bwd{dr}", dyp, c_re[dr], c_im[dr], xs[dr][0], xs[dr][1], abr2[dr:dr + 1],
                                    abi2[dr:dr + 1], reverse=(dr == 1))
        d_abr.append(dar)
        d_abi.append(dai)
        lams += [lr, li]
        maps = _mm(f"ssm_dmaps{dr}", "tn", [sp32, dyp], [lr, li, xs[dr][0], xs[dr][1]],
                   [[(0, 0)], [(0, 1)], [(1, 2)], [(1, 3)]], d_ssm, n_state, d_ssm, ts, [], fold_diag,
                   [(_out(n_state // ts * d_ssm, sp, F32), pl.BlockSpec((d_ssm, sp), lambda i, j: (j, 0)))] * 4)
        for acc, m in zip((d_bbr, d_bbi, d_cre, d_cim), maps):
            acc.append(jnp.sum(m.reshape(n_state // ts, sg, sh, sp), axis=0))
    ds = _from_segments(_mm(
        "ssm_ds", "nt", lams, [b_re[0], b_im[0], b_re[1], b_im[1]], [[(k, k) for k in range(4)]], t, d_ssm, tmy,
        d_ssm, [(dyp, _tile(tmy, d_ssm)), (sm["ssm_d"], _rowvec(d_ssm))],
        lambda accs, dyv, dv: (dyv * dv + accs[0],), [(_out(t, d_ssm, BF16), None)])[0])
    cots = [jnp.concatenate(d_abr, axis=0).reshape(2 * sg, 1, sp), jnp.concatenate(d_abi, axis=0).reshape(2 * sg, 1, sp),
            jnp.concatenate(d_bbr, axis=0), jnp.concatenate(d_bbi, axis=0)]
    d_are, d_aim, d_ldt, d_bre, d_bim = _ssm_disc_bwd(disc_args, cots)
    gs["ssm_a_re"] = d_are.reshape(2, sg, sp)
    gs["ssm_a_im"] = d_aim.reshape(2, sg, sp)
    gs["ssm_log_dt"] = d_ldt.reshape(2, sg)
    from_p = lambda a: jnp.swapaxes(a.reshape(2, sg, sh, sp), -1, -2)
    gs["ssm_b_re"], gs["ssm_b_im"] = from_p(d_bre), from_p(d_bim)
    gs["ssm_c_re"] = jnp.stack(d_cre, axis=0)
    gs["ssm_c_im"] = -jnp.stack(d_cim, axis=0)

    dpm = _mm1("pool_dpm", "nn", dz_pool, wt["w_pool_proj"], t, d_pool, tmm, _pick(d_pool, 256), F32)
    dp, gs["pool_w"], gs["pool_scale"] = _pool_bwd(pooled, dpm, pool_w_bf, sm["pool_scale"])

    w_in = wt["w_in"]
    parts = [(dp, 0, d_pool), (ds, d_pool, d_ssm), (dg_pool, off_gp, d), (dg_ssm, off_gs, d)]
    w_in_parts = [w_in[o0:o0 + width] for _, o0, width in parts]
    gb["w_in"] = jnp.concatenate(
        [_mm1(f"in_proj_dw{k}", "tn", p_[0], u, p_[2], d, _pick(p_[2], 1024), tnx, BF16) for k, p_ in enumerate(parts)], axis=0)
    pin = emit("grads_main", gb=gb)
    du = _mm("in_proj_du", "nn", [p_[0] for p_ in parts], w_in_parts, [[(k, k) for k in range(4)]], t, d, tmm, tnx, [],
             lambda accs: (accs[0],), [(_out(t, d, F32), None)], after=pin)[0]
    dh1, dh1_bf, gs["mix_norm"] = _rms_bwd("mix_norm_bwd", h1, sm["mix_norm"], du, dh2)
    pin = emit("small_early", gs=gs, loss=loss)

    def ffn1_weights_done(d_wg, d_wu, d_wd):
        gb["ffn1_w_gate"], gb["ffn1_w_up"], gb["ffn1_w_down"] = d_wg, d_wu, d_wd
        return emit("grads_ffn1", gb=gb)

    dx, _, gs["ffn1_norm"], _, _, _ = _ffn_bwd(
        "ffn1", x, sm["ffn1_norm"], wt["ffn1_w_gate"], wt["ffn1_w_up"], wt["ffn1_w_down"], ffn1_saved, dh1, dh1_bf,
        weights_done=ffn1_weights_done, after=pin)
    return loss, dx, gb, gs


WEIGHTS = ["ffn1_norm", "ffn1_w_gate", "ffn1_w_up", "ffn1_w_down", "mix_norm", "w_in", "pool_w", "pool_scale",
           "w_pool_proj", "ssm_a_re", "ssm_a_im", "ssm_log_dt", "ssm_b_re", "ssm_b_im", "ssm_c_re", "ssm_c_im", "ssm_d",
           "w_glu_val", "w_glu_gate", "w_mix_out", "xattn_norm", "mem_norm", "w_q", "w_kv", "w_xo", "ffn2_norm",
           "ffn2_w_gate", "ffn2_w_up", "ffn2_w_down", "final_norm"]
COL_SHARDED = ["ffn1_w_gate", "ffn1_w_up", "w_in", "w_pool_proj", "w_glu_val", "w_glu_gate", "w_kv", "ffn2_w_gate",
               "ffn2_w_up"]
ROW_SHARDED = ["ffn1_w_down", "w_mix_out", "w_q", "w_xo", "ffn2_w_down"]
BIG = [n for n in WEIGHTS if n in COL_SHARDED or n in ROW_SHARDED]
SMALL = [n for n in WEIGHTS if n not in BIG]
FFN1_BIG = ["ffn1_w_gate", "ffn1_w_up", "ffn1_w_down"]
MAIN_BIG = [n for n in BIG if n not in FFN1_BIG]
GATHER_PLAN = [("ffn1_up_done", ["ffn1_w_down"]), ("ffn1_fwd_done", ["w_in"]),
               ("mix_in_done", ["w_pool_proj", "w_glu_val", "w_glu_gate", "w_mix_out"]),
               ("mix_done", ["w_q", "w_kv", "w_xo"]), ("xattn_done", ["ffn2_w_gate", "ffn2_w_up", "ffn2_w_down"])]
MINOR_SWAPPED = ["ssm_b_re", "ssm_b_im"]
LATE_SMALL = "ffn1_norm"
EARLY_SMALL = [n for n in SMALL if n != LATE_SMALL]
PACK_ROWS = SUBLANES * LANES
GRAD_ROW_TILE = 256
ADAMW_STEP_WORDS = 1 << 19


def _to_rows(name, w):
    return w.T if name in COL_SHARDED else w


def _pack_small(vals):
    flat = []
    for v in vals:
        f = v.reshape(-1)
        flat.append(jnp.pad(f, (0, (-f.shape[0]) % PACK_ROWS)))
    total = sum(f.shape[0] for f in flat)
    flat.append(jnp.zeros(((-total) % (GRAD_ROW_TILE * LANES),), F32))
    return jnp.concatenate(flat).reshape(-1, LANES)


def _unpack_small(packed, shapes):
    out, row = [], 0
    for shp in shapes:
        size = math.prod(shp)
        rows = -(-size // PACK_ROWS) * SUBLANES
        out.append(packed[row:row + rows].reshape(-1)[:size].reshape(shp))
        row += rows
    return out


def kernel(x, mem, ffn1_norm, ffn1_w_gate, ffn1_w_up, ffn1_w_down, mix_norm, w_in, pool_w, pool_scale, w_pool_proj, ssm_a_re, ssm_a_im, ssm_log_dt, ssm_b_re, ssm_b_im, ssm_c_re, ssm_c_im, ssm_d, w_glu_val, w_glu_gate, w_mix_out, xattn_norm, mem_norm, w_q, w_kv, w_xo, ffn2_norm, ffn2_w_gate, ffn2_w_up, ffn2_w_down, final_norm, loss_target, m_ffn1_norm, m_ffn1_w_gate, m_ffn1_w_up, m_ffn1_w_down, m_mix_norm, m_w_in, m_pool_w, m_pool_scale, m_w_pool_proj, m_ssm_a_re, m_ssm_a_im, m_ssm_log_dt, m_ssm_b_re, m_ssm_b_im, m_ssm_c_re, m_ssm_c_im, m_ssm_d, m_w_glu_val, m_w_glu_gate, m_w_mix_out, m_xattn_norm, m_mem_norm, m_w_q, m_w_kv, m_w_xo, m_ffn2_norm, m_ffn2_w_gate, m_ffn2_w_up, m_ffn2_w_down, m_final_norm, v_ffn1_norm, v_ffn1_w_gate, v_ffn1_w_up, v_ffn1_w_down, v_mix_norm, v_w_in, v_pool_w, v_pool_scale, v_w_pool_proj, v_ssm_a_re, v_ssm_a_im, v_ssm_log_dt, v_ssm_b_re, v_ssm_b_im, v_ssm_c_re, v_ssm_c_im, v_ssm_d, v_w_glu_val, v_w_glu_gate, v_w_mix_out, v_xattn_norm, v_mem_norm, v_w_q, v_w_kv, v_w_xo, v_ffn2_norm, v_ffn2_w_gate, v_ffn2_w_up, v_ffn2_w_down, v_final_norm):
    given = dict(locals())
    wts = {n: given[n] for n in WEIGHTS}
    moms = {n: (given["m_" + n], given["v_" + n]) for n in WEIGHTS}
    x2, mem2, tgt2 = x[0], mem[0], loss_target[0]
    d = x2.shape[1]
    chip = (2 * lax.axis_index("x") + lax.axis_index("y")).astype(jnp.int32).reshape(1)

    def full_form(n, f):
        shard = wts[n][0].shape
        return f.reshape(N_DEV * shard[1], shard[0]) if n in COL_SHARDED else f.reshape(N_DEV * shard[0], shard[1])

    shards = {n: _to_rows(n, wts[n][0]).astype(BF16) for n in BIG}
    first = FFN1_BIG[:2]
    wt = {n: full_form(n, f) for n, f in zip(first, _allgather("weight_allgather_first", [shards[n] for n in first]))}
    started = _split_start("weight_gather_start", [_gather_group([shards[n] for n in names]) for _, names in GATHER_PLAN],
                           after=wt[first[0]])
    gathers = {event: (names, st) for (event, names), st in zip(GATHER_PLAN, started)}
    sm = {n: (wts[n].reshape(1, -1) if wts[n].ndim <= 2 else wts[n][0]) for n in SMALL}
    sm["ffn1_norm"] = sm["ffn1_norm"] + started[0]["token"][0, 0]

    pending = {}

    def reduce_start(tag, names, gb):
        blocks = [gb[n].reshape(N_DEV, -1, d) for n in names]
        pad_rows = (-sum(b.shape[1] for b in blocks)) % GRAD_ROW_TILE
        pad = [jnp.zeros((N_DEV, pad_rows, d), BF16)] if pad_rows else []
        started = _cores_start("grad_exchange_cores_start_" + tag, blocks + pad)
        own = jnp.concatenate([lax.dynamic_index_in_dim(b.reshape(4, 2, b.shape[1], d), lax.axis_index("c"), 1, False)
                               for b in started["bufs"][:len(blocks + pad)]], axis=1)
        _, (recv,) = _split_wait("grad_exchange_cores_wait_" + tag, started, own)
        rows_all = own.shape[1]
        pair = _ew("grad_pair_sum_" + tag, lambda a, b: (a.astype(F32) + b.astype(F32),),
                   [own.reshape(-1, d), recv.reshape(-1, d)], [BF16], rows_pref=5 * GRAD_ROW_TILE)[0]
        pair = pair.reshape(4, rows_all, d)
        pending[tag] = (pair, _chips_start("grad_exchange_chips_start_" + tag, pair), [b.shape[1] for b in blocks])
        return pending[tag][1]["token"]

    def reduce_finish(tag, after):
        _, started, rows = pending[tag]
        (pair,), (recv,) = _split_wait("grad_exchange_chips_wait_" + tag, started, after)
        return _chip_sum("grad_chip_sum_" + tag, pair, recv, chip), rows

    def ev(name, gb=None, gs=None, loss=None, marker=None):
        if name in gathers:
            names, started = gathers[name]
            for n, f in zip(names, _split_wait("weight_gather_wait_" + name, started, marker)[1]):
                wt[n] = full_form(n, f)
        elif name == "grads_main":
            return reduce_start("main", MAIN_BIG, gb)
        elif name == "small_early":
            pending["small"] = _slots_start("small_gather_start", _pack_small([gs[n] for n in EARLY_SMALL] + [loss[:, :1]]))
            return pending["small"]["token"]
        elif name == "grads_ffn1":
            return reduce_start("ffn1", FFN1_BIG, gb)
        return None

    _, dx, _, gs = _local_step(x2, mem2, tgt2, wt, sm, ev)

    grads = {}
    for tag, names in (("main", MAIN_BIG), ("ffn1", FFN1_BIG)):
        g_rows, rows = reduce_finish(tag, dx)
        off = 0
        for n, r in zip(names, rows):
            shard = wts[n].shape
            grads[n] = g_rows[off:off + r].reshape((shard[2], shard[1]) if n in COL_SHARDED else shard[1:])
            off += r
    small_sum = _sum_slots("small_sum", _split_wait("small_gather_wait", pending["small"], dx)[1][0], F32)
    late = _allgather("small_allgather_late", [gs[LATE_SMALL].reshape(-1, LANES)])[0]
    late_sum = _sum_slots("small_sum_late", late.reshape(N_DEV, -1, LANES), F32)
    vals = _unpack_small(small_sum, [wts[n].shape for n in EARLY_SMALL] + [(1, 1)])
    total_loss = vals[-1].reshape(())
    def flat(n, a):
        a = a.reshape(wts[n].shape)
        a = jnp.swapaxes(a, -1, -2) if n in MINOR_SWAPPED else a
        return a.reshape(-1, a.shape[-1])

    def unflat(n, a):
        shape = wts[n].shape
        if n in MINOR_SWAPPED:
            return jnp.swapaxes(a.reshape(shape[:-2] + (shape[-1], shape[-2])), -1, -2)
        return a.reshape(shape)

    for n, g_full in zip(EARLY_SMALL + [LATE_SMALL], vals[:-1] + [late_sum]):
        grads[n] = flat(n, g_full)

    out_g, out_d, out_m, out_v = {}, {}, {}, {}
    by_shape = {}
    for n in WEIGHTS:
        by_shape.setdefault((flat(n, wts[n]).shape, n in COL_SHARDED), []).append(n)
    for (_, transposed), names in by_shape.items():
        items = [(flat(n, wts[n]), grads[n], flat(n, moms[n][0]), flat(n, moms[n][1])) for n in names]
        for n, res in zip(names, _adamw_group("adamw_" + names[0], items, transposed)):
            out_d[n], out_m[n], out_v[n], out_g[n] = (unflat(n, a) for a in res)

    return (total_loss, dx[None], *[out_g[n] for n in WEIGHTS], *[out_d[n] for n in WEIGHTS],
            *[out_m[n] for n in WEIGHTS], *[out_v[n] for n in WEIGHTS])
```

```python
import functools
import math

import jax
import jax.numpy as jnp
from jax import lax
from jax.experimental import pallas as pl
from jax.experimental.pallas import tpu as pltpu

F32 = jnp.float32
BF16 = jnp.bfloat16
EPS = 1e-6
N_XHEADS = 4
POOL_WINDOWS = (2, 4, 8, 16)
ADAM_LR = 0.001
ADAM_B1 = 0.9
ADAM_B2 = 0.999
ADAM_EPS = 1e-08
ADAM_WD = 0.01
ADAM_STEP = 10
N_DEV = 8
VMEM_LIMIT_V7X = 48 * 1024 * 1024
LANES = 128
SUBLANES = 8
SUB_ROWS = 256
POOL_PAD = 16
MESH = pl.DeviceIdType.MESH
ANY = pl.BlockSpec(memory_space=pl.ANY)
HBM = pl.BlockSpec(memory_space=pltpu.HBM)
SEM = pl.BlockSpec(memory_space=pltpu.SEMAPHORE)
SIDE_EFFECT = pltpu.SideEffectType.DATAFLOW_SIDE_EFFECTING

_DIMS = {
    "nt": (((1,), (1,)), ((), ())),
    "nn": (((1,), (0,)), ((), ())),
    "tn": (((0,), (0,)), ((), ())),
}


def _pick(dim, pref, mult=LANES):
    if dim <= pref:
        return dim
    for t in range(pref - pref % mult, 0, -mult):
        if dim % t == 0:
            return t
    return dim


def _params(sem):
    return pltpu.CompilerParams(dimension_semantics=sem, vmem_limit_bytes=VMEM_LIMIT_V7X)


def _tile(tm, tn, coff=0):
    return pl.BlockSpec((tm, tn), lambda i, j: (i, j + coff))


def _rowvec(tn, coff=0):
    return pl.BlockSpec((1, tn), lambda i, j: (0, j + coff))


def _out(m, n, dtype):
    return jax.ShapeDtypeStruct((m, n), dtype)


def _mm(name, form, a_list, b_list, groups, m, n, tm, tn, extras, epilogue, outs, after=None, sub=SUB_ROWS):
    na, nb, ne = len(a_list), len(b_list), len(extras)
    pins = [] if after is None else [after]
    step = tm if (sub is None or form == "tn" or tm % sub) else sub

    def a_spec(a):
        if form == "tn":
            return pl.BlockSpec((a.shape[0], tm), lambda i, j: (0, i))
        return pl.BlockSpec((tm, a.shape[1]), lambda i, j: (i, 0))

    def b_spec(b):
        if form == "nt":
            return pl.BlockSpec((tn, b.shape[1]), lambda i, j: (j, 0))
        return pl.BlockSpec((b.shape[0], tn), lambda i, j: (0, j))

    def body(*refs):
        a_refs, b_refs = refs[:na], refs[na:na + nb]
        e_refs, o_refs = refs[na + nb:na + nb + ne], refs[na + nb + ne + len(pins):]
        b_vals = {}
        for s0 in range(0, tm, step):
            rows = slice(None) if step == tm else pl.ds(s0, step)
            a_vals, accs = {}, []
            for group in groups:
                acc = None
                for ai, bi in group:
                    if ai not in a_vals:
                        a_vals[ai] = (a_refs[ai][...] if form == "tn" else a_refs[ai][rows, :]).astype(BF16)
                    if bi not in b_vals:
                        b_vals[bi] = b_refs[bi][...].astype(BF16)
                    d = lax.dot_general(a_vals[ai], b_vals[bi], _DIMS[form], preferred_element_type=F32)
                    acc = d if acc is None else acc + d
                accs.append(acc)
            res = epilogue(accs, *[e[rows, :] if e.shape[0] == tm else e[...] for e in e_refs])
            for o_ref, r in zip(o_refs, res):
                o_ref[rows, :] = r.astype(o_ref.dtype)

    out_specs = [_tile(tm, tn) if s is None else s for _, s in outs]
    res = pl.pallas_call(
        body, name=name, grid=(m // tm, n // tn),
        in_specs=[a_spec(a) for a in a_list] + [b_spec(b) for b in b_list] + [s for _, s in extras] + [ANY] * len(pins),
        out_specs=out_specs, out_shape=[o for o, _ in outs],
        compiler_params=_params(("parallel", "parallel")),
    )(*a_list, *b_list, *[e for e, _ in extras], *pins)
    return res


def _mm1(name, form, a, b, m, n, tm, tn, dtype, scale=None):
    epi = (lambda accs: (accs[0],)) if scale is None else (lambda accs: (accs[0] * scale,))
    return _mm(name, form, [a], [b], [[(0, 0)]], m, n, tm, tn, [], epi, [(_out(m, n, dtype), None)])[0]


def _rms_fwd(name, h, g):
    t, d = h.shape
    tm = _pick(t, 512, SUBLANES)

    def body(h_ref, g_ref, n_ref):
        hv = h_ref[...]
        r = lax.rsqrt(jnp.mean(hv * hv, axis=-1, keepdims=True) + EPS)
        n_ref[...] = ((hv * r) * g_ref[...]).astype(BF16)

    return pl.pallas_call(
        body, name=name, grid=(t // tm,),
        in_specs=[pl.BlockSpec((tm, d), lambda i: (i, 0)), pl.BlockSpec((1, d), lambda i: (0, 0))],
        out_specs=pl.BlockSpec((tm, d), lambda i: (i, 0)), out_shape=_out(t, d, BF16),
        compiler_params=_params(("parallel",)),
    )(h, g)


def _rms_bwd(name, h, g, dn, dres=None):
    t, d = h.shape
    tm = _pick(t, 512, SUBLANES)
    need_dh = dres is not None

    def body(*refs):
        if need_dh:
            h_ref, g_ref, dn_ref, dres_ref, dh_ref, dhb_ref, dg_ref = refs
        else:
            h_ref, g_ref, dn_ref, dg_ref = refs
        hv = h_ref[...]
        r = lax.rsqrt(jnp.mean(hv * hv, axis=-1, keepdims=True) + EPS)
        nh = hv * r
        dnv = dn_ref[...].astype(F32)

        @pl.when(pl.program_id(0) == 0)
        def _():
            dg_ref[...] = jnp.zeros_like(dg_ref)

        dg_ref[...] += jnp.sum(dnv * nh, axis=0, keepdims=True)
        if need_dh:
            dng = dnv * g_ref[...]
            dh = dres_ref[...] + r * (dng - nh * jnp.mean(dng * nh, axis=-1, keepdims=True))
            dh_ref[...] = dh
            dhb_ref[...] = dh.astype(BF16)

    row = pl.BlockSpec((tm, d), lambda i: (i, 0))
    vec = pl.BlockSpec((1, d), lambda i: (0, 0))
    if need_dh:
        return pl.pallas_call(
            body, name=name, grid=(t // tm,), in_specs=[row, vec, row, row], out_specs=[row, row, vec],
            out_shape=[_out(t, d, F32), _out(t, d, BF16), _out(1, d, F32)], compiler_params=_params(("arbitrary",)),
        )(h, g, dn, dres)
    return pl.pallas_call(
        body, name=name, grid=(t // tm,), in_specs=[row, vec, row], out_specs=vec,
        out_shape=_out(1, d, F32), compiler_params=_params(("arbitrary",)),
    )(h, g, dn)


def _loss_head(h, g, tgt):
    t, d = h.shape
    tm = _pick(t, 512, SUBLANES)

    def body(h_ref, g_ref, t_ref, dh_ref, dhb_ref, dg_ref, loss_ref):
        hv = h_ref[...]
        r = lax.rsqrt(jnp.mean(hv * hv, axis=-1, keepdims=True) + EPS)
        nh = hv * r
        err = nh * g_ref[...] - t_ref[...]

        @pl.when(pl.program_id(0) == 0)
        def _():
            dg_ref[...] = jnp.zeros_like(dg_ref)
            loss_ref[...] = jnp.zeros_like(loss_ref)

        per_row = jnp.mean(err * err, axis=-1, keepdims=True)
        loss_ref[...] += 0.5 * jnp.sum(per_row, axis=0, keepdims=True)
        dy = err * (1.0 / d)
        dg_ref[...] += jnp.sum(dy * nh, axis=0, keepdims=True)
        dng = dy * g_ref[...]
        dh = r * (dng - nh * jnp.mean(dng * nh, axis=-1, keepdims=True))
        dh_ref[...] = dh
        dhb_ref[...] = dh.astype(BF16)

    row = pl.BlockSpec((tm, d), lambda i: (i, 0))
    vec = pl.BlockSpec((1, d), lambda i: (0, 0))
    return pl.pallas_call(
        body, name="loss_head", grid=(t // tm,), in_specs=[row, vec, row],
        out_specs=[row, row, vec, pl.BlockSpec((1, LANES), lambda i: (0, 0))],
        out_shape=[_out(t, d, F32), _out(t, d, BF16), _out(1, d, F32), _out(1, LANES, F32)],
        compiler_params=_params(("arbitrary",)),
    )(h, g, tgt)


def _ffn_fwd(tag, h, n, wg_t, wu_t, wd):
    t, d = h.shape
    f = wg_t.shape[0]
    tm, tn = _pick(t, 1024), _pick(f, 1408)

    def up_epi(accs):
        a, b = accs
        return a, b, (a * jax.nn.sigmoid(a)) * b

    a, b, hid = _mm(tag + "_up", "nt", [n], [wg_t, wu_t], [[(0, 0)], [(0, 1)]], t, f, tm, tn, [], up_epi,
                    [(_out(t, f, BF16), None)] * 3)
    if callable(wd):
        wd = wd(hid)
    tm2, tn2 = _pick(t, 1024), _pick(d, 512)
    h_out = _mm(tag + "_down", "nn", [hid], [wd], [[(0, 0)]], t, d, tm2, tn2, [(h, _tile(tm2, tn2))],
                lambda accs, hin: (hin + 0.5 * accs[0],), [(_out(t, d, F32), None)])[0]
    return h_out, (n, a, b, hid)


def _ffn_bwd(tag, h, g, wg_t, wu_t, wd, saved, dh, dh_bf, weights_done=None, after=None):
    n, a, b, hid = saved
    t, d = h.shape
    f = wd.shape[0]
    tm, tn = _pick(t, 1024), _pick(f, 1408)

    def hid_epi(accs, av, bv):
        dhid = 0.5 * accs[0]
        av, bv = av.astype(F32), bv.astype(F32)
        sig = jax.nn.sigmoid(av)
        da = dhid * bv * (sig * (1.0 + av * (1.0 - sig)))
        db = dhid * (av * sig)
        return da, db

    da, db = _mm(tag + "_bwd_hid", "nt", [dh_bf], [wd], [[(0, 0)]], t, f, tm, tn,
                 [(a, _tile(tm, tn)), (b, _tile(tm, tn))], hid_epi, [(_out(t, f, BF16), None)] * 2, after=after)
    tw, tnw = _pick(f, 1408), _pick(d, 512)
    d_wd = _mm1(tag + "_dwd", "tn", hid, dh_bf, f, d, tw, tnw, BF16, scale=0.5)
    d_wg = _mm1(tag + "_dwg", "tn", da, n, f, d, tw, tnw, BF16)
    d_wu = _mm1(tag + "_dwu", "tn", db, n, f, d, tw, tnw, BF16)
    pin = weights_done(d_wg, d_wu, d_wd) if weights_done is not None else None
    tm2, tn2 = _pick(t, 1024), _pick(d, 512)
    dn = _mm(tag + "_dn", "nn", [da, db], [wg_t, wu_t], [[(0, 0), (1, 1)]], t, d, tm2, tn2, [],
             lambda accs: (accs[0],), [(_out(t, d, F32), None)], after=pin)[0]
    dh_in, dh_in_bf, dg = _rms_bwd(tag + "_norm_bwd", h, g, dn, dh)
    return dh_in, dh_in_bf, dg, d_wg, d_wu, d_wd


def _window_sum(win, offsets):
    n = win.shape[0]
    acc = None
    for j in offsets:
        term = win if j == 0 else pltpu.roll(win, (-j) % n, 0)
        acc = term if acc is None else acc + term
    return acc


def _pool_counts(r0, ch, c, left, right, t):
    pos = r0 + lax.broadcasted_iota(jnp.int32, (ch, c), 0)
    return (jnp.minimum(pos + right + 1, t) - jnp.maximum(pos - left, 0)).astype(F32)


def _pool_fwd(proj, pool_w_bf, pool_scale):
    t = proj.shape[0]
    ng, c, _ = pool_w_bf.shape
    ch = _pick(t, 256, SUBLANES)
    pad = POOL_PAD

    def body(p_ref, w_ref, s_ref, pooled_ref, pm_ref, buf):
        grp = pl.program_id(0)
        buf[pl.ds(0, pad), :] = jnp.zeros((pad, c), F32)
        buf[pl.ds(pad + t, pad), :] = jnp.zeros((pad, c), F32)

        def fill(ci, carry):
            r0 = pl.multiple_of(ci * ch, SUBLANES)
            buf[pl.ds(pl.multiple_of(r0 + pad, SUBLANES), ch), :] = p_ref[pl.ds(r0, ch), :]
            return carry

        lax.fori_loop(0, t // ch, fill, 0)
        for gi, w in enumerate(POOL_WINDOWS):
            left = w // 2
            right = w - 1 - left

            @pl.when(grp == gi)
            def _(left=left, right=right):
                def chunk(ci, carry):
                    r0 = pl.multiple_of(ci * ch, SUBLANES)
                    win = buf[pl.ds(r0, ch + 2 * pad), :]
                    s = _window_sum(win, range(-left, right + 1))[pad:pad + ch]
                    pooled = s / _pool_counts(r0, ch, c, left, right, t) - win[pad:pad + ch]
                    pooled_bf = pooled.astype(BF16)
                    mixed = jnp.dot(pooled_bf, w_ref[0], preferred_element_type=F32)
                    pooled_ref[pl.ds(r0, ch), :] = pooled_bf
                    pm_ref[pl.ds(r0, ch), :] = (mixed * s_ref[...]).astype(BF16)
                    return carry

                lax.fori_loop(0, t // ch, chunk, 0)

    col = pl.BlockSpec((t, c), lambda g: (0, g))
    return pl.pallas_call(
        body, name="pool_fwd", grid=(ng,),
        in_specs=[col, pl.BlockSpec((1, c, c), lambda g: (g, 0, 0)), pl.BlockSpec((1, c), lambda g: (0, g))],
        out_specs=[col, col], out_shape=[_out(t, ng * c, BF16), _out(t, ng * c, BF16)],
        scratch_shapes=[pltpu.VMEM((t + 2 * pad, c), F32)],
        compiler_params=_params(("parallel",)),
    )(proj, pool_w_bf, pool_scale)


def _pool_bwd(pooled, dpm, pool_w_bf, pool_scale):
    t = pooled.shape[0]
    ng, c, _ = pool_w_bf.shape
    ch = _pick(t, 256, SUBLANES)
    pad = POOL_PAD

    def body(pooled_ref, dpm_ref, w_ref, s_ref, dp_ref, dw_ref, ds_ref, buf, raw):
        grp = pl.program_id(0)
        buf[pl.ds(0, pad), :] = jnp.zeros((pad, c), F32)
        buf[pl.ds(pad + t, pad), :] = jnp.zeros((pad, c), F32)
        dw_ref[...] = jnp.zeros_like(dw_ref)
        ds_ref[...] = jnp.zeros_like(ds_ref)
        for gi, w in enumerate(POOL_WINDOWS):
            left = w // 2
            right = w - 1 - left

            @pl.when(grp == gi)
            def _(left=left, right=right):
                def first(ci, carry):
                    r0 = pl.multiple_of(ci * ch, SUBLANES)
                    pv = pooled_ref[pl.ds(r0, ch), :]
                    dpm_v = dpm_ref[pl.ds(r0, ch), :]
                    mixed = jnp.dot(pv, w_ref[0], preferred_element_type=F32)
                    ds_ref[...] += jnp.sum(dpm_v * mixed, axis=0, keepdims=True)
                    dmixed = (dpm_v * s_ref[...]).astype(BF16)
                    dw_ref[0] += lax.dot_general(pv, dmixed, _DIMS["tn"], preferred_element_type=F32)
                    dpooled = lax.dot_general(dmixed, w_ref[0], _DIMS["nt"], preferred_element_type=F32)
                    raw[pl.ds(r0, ch), :] = dpooled
                    buf[pl.ds(pl.multiple_of(r0 + pad, SUBLANES), ch), :] = (
                        dpooled / _pool_counts(r0, ch, c, left, right, t))
                    return carry

                lax.fori_loop(0, t // ch, first, 0)

                def second(ci, carry):
                    r0 = pl.multiple_of(ci * ch, SUBLANES)
                    win = buf[pl.ds(r0, ch + 2 * pad), :]
                    s = _window_sum(win, range(-right, left + 1))[pad:pad + ch]
                    dp_ref[pl.ds(r0, ch), :] = (s - raw[pl.ds(r0, ch), :]).astype(BF16)
                    return carry

                lax.fori_loop(0, t // ch, second, 0)

    col = pl.BlockSpec((t, c), lambda g: (0, g))
    return pl.pallas_call(
        body, name="pool_bwd", grid=(ng,),
        in_specs=[col, col, pl.BlockSpec((1, c, c), lambda g: (g, 0, 0)), pl.BlockSpec((1, c), lambda g: (0, g))],
        out_specs=[col, pl.BlockSpec((1, c, c), lambda g: (g, 0, 0)), pl.BlockSpec((1, c), lambda g: (0, g))],
        out_shape=[_out(t, ng * c, BF16), jax.ShapeDtypeStruct((ng, c, c), F32), _out(1, ng * c, F32)],
        scratch_shapes=[pltpu.VMEM((t + 2 * pad, c), F32), pltpu.VMEM((t, c), F32)],
        compiler_params=_params(("parallel",)),
    )(pooled, dpm, pool_w_bf, pool_scale)


def _discretise(a_re, a_im, log_dt, b_re, b_im):
    dt = jnp.exp(log_dt)
    mag = jnp.exp(dt * a_re)
    ang = dt * a_im
    abr = mag * jnp.cos(ang)
    abi = mag * jnp.sin(ang)
    den = a_re * a_re + a_im * a_im
    nr = abr - 1.0
    qr = (nr * a_re + abi * a_im) / den
    qi = (abi * a_re - nr * a_im) / den
    return abr, abi, qr * b_re - qi * b_im, qr * b_im + qi * b_re


def _ssm_disc(args):
    def body(ar, ai, ld, br, bi, o1, o2, o3, o4):
        res = _discretise(ar[...], ai[...], ld[...], br[...], bi[...])
        for o, r in zip((o1, o2, o3, o4), res):
            o[...] = r

    like = lambda a: jax.ShapeDtypeStruct(a.shape, F32)
    return pl.pallas_call(
        body, name="ssm_disc", out_shape=[like(args[0]), like(args[0]), like(args[3]), like(args[3])],
    )(*args)


def _ssm_disc_bwd(args, cots):
    def body(ar, ai, ld, br, bi, c1, c2, c3, c4, o1, o2, o3, o4, o5):
        _, vjp = jax.vjp(_discretise, ar[...], ai[...], ld[...], br[...], bi[...])
        res = vjp((c1[...], c2[...], c3[...], c4[...]))
        for o, r in zip((o1, o2, o3, o4, o5), res):
            o[...] = r

    return pl.pallas_call(
        body, name="ssm_disc_bwd", out_shape=[jax.ShapeDtypeStruct(a.shape, F32) for a in args],
    )(*args, *cots)


def _cmul(pr, pi, qr, qi):
    return pr * qr - pi * qi, pr * qi + pi * qr


def _cpow(pr, pi, n):
    rr, ri = None, None
    while n:
        if n & 1:
            rr, ri = (pr, pi) if rr is None else _cmul(rr, ri, pr, pi)
        n >>= 1
        if n:
            pr, pi = _cmul(pr, pi, pr, pi)
    return rr, ri


def _segment_carry(er, ei, pr, pi, reverse):
    row = lax.broadcasted_iota(jnp.int32, er.shape, 0)
    cr, ci = jnp.zeros_like(er), jnp.zeros_like(ei)
    for _ in range(SUBLANES - 1):
        tr = er + pr * cr - pi * ci
        ti = ei + pr * ci + pi * cr
        if reverse:
            keep, shift = row < SUBLANES - 1, SUBLANES - 1
        else:
            keep, shift = row >= 1, 1
        cr = jnp.where(keep, pltpu.roll(tr, shift, 0), 0.0)
        ci = jnp.where(keep, pltpu.roll(ti, shift, 0), 0.0)
    return cr, ci


def _ssm_fwd(name, sp, b_re, b_im, c_re, c_im, ar, ai, reverse):
    t, c = sp.shape
    s = ar.shape[1]
    w = _pick(s, 512)
    ch = _pick(t, 512, SUBLANES)
    n_ch, gpc, steps = t // ch, ch // SUBLANES, t // SUBLANES

    def body(sp_ref, bre_ref, bim_ref, cre_ref, cim_ref, ar_ref, ai_ref, xr_ref, xi_ref, y_ref, ur, ui, xbr, xbi):
        a_r = jnp.broadcast_to(ar_ref[...], (SUBLANES, w))
        a_i = jnp.broadcast_to(ai_ref[...], (SUBLANES, w))

        @pl.when(pl.program_id(0) == 0)
        def _():
            y_ref[...] = jnp.zeros_like(y_ref)

        def sweep(h0, store):
            def chunk(k, h):
                ci = n_ch - 1 - k if reverse else k
                rows = pl.ds(pl.multiple_of(ci * ch, ch), ch)
                spv = sp_ref[rows, :].astype(BF16)
                ur[...] = jnp.dot(spv, bre_ref[...], preferred_element_type=F32)
                ui[...] = jnp.dot(spv, bim_ref[...], preferred_element_type=F32)

                def group(g, hh):
                    gi = gpc - 1 - g if reverse else g
                    r0 = pl.multiple_of(gi * SUBLANES, SUBLANES)
                    hr, hi = hh
                    nr = a_r * hr - a_i * hi + ur[pl.ds(r0, SUBLANES), :]
                    ni = a_r * hi + a_i * hr + ui[pl.ds(r0, SUBLANES), :]
                    if store:
                        xbr[pl.ds(r0, SUBLANES), :] = nr
                        xbi[pl.ds(r0, SUBLANES), :] = ni
                    return nr, ni

                h = lax.fori_loop(0, gpc, group, h)
                if store:
                    xr16, xi16 = xbr[...].astype(BF16), xbi[...].astype(BF16)
                    xr_ref[rows, :] = xr16
                    xi_ref[rows, :] = xi16
                    y_ref[rows, :] += (lax.dot_general(xr16, cre_ref[...], _DIMS["nt"], preferred_element_type=F32)
                                       + lax.dot_general(xi16, cim_ref[...], _DIMS["nt"], preferred_element_type=F32))
                return h

            return lax.fori_loop(0, n_ch, chunk, h0)

        zero = jnp.zeros((SUBLANES, w), F32)
        er, ei = sweep((zero, zero), False)
        pr, pi = _cpow(ar_ref[...], ai_ref[...], steps)
        sweep(_segment_carry(er, ei, pr, pi, reverse), True)

    col = lambda i: (0, i)
    return pl.pallas_call(
        body, name=name, grid=(s // w,),
        in_specs=[pl.BlockSpec((t, c), lambda i: (0, 0))] + [pl.BlockSpec((c, w), col)] * 4
        + [pl.BlockSpec((1, w), col)] * 2,
        out_specs=[pl.BlockSpec((t, w), col), pl.BlockSpec((t, w), col), pl.BlockSpec((t, c), lambda i: (0, 0))],
        out_shape=[_out(t, s, BF16), _out(t, s, BF16), _out(t, c, F32)],
        scratch_shapes=[pltpu.VMEM((ch, w), F32)] * 4,
        compiler_params=_params(("arbitrary",)),
    )(sp, b_re, b_im, c_re, c_im, ar, ai)


def _ssm_bwd(name, dyp, c_re, c_im, xr, xi, ar, ai, reverse):
    t, c = dyp.shape
    s = ar.shape[1]
    w = _pick(s, 512)
    ch = _pick(t, 512, SUBLANES)
    n_ch, gpc, steps = t // ch, ch // SUBLANES, t // SUBLANES
    back = not reverse
    edge = 2 * SUBLANES

    def body(dy_ref, cre_ref, cim_ref, xr_ref, xi_ref, ar_ref, ai_ref, lr_ref, li_ref, dar_ref, dai_ref,
             gr, gi_, lbr, lbi, xbr, xbi):
        a_r = jnp.broadcast_to(ar_ref[...], (SUBLANES, w))
        a_i = -jnp.broadcast_to(ai_ref[...], (SUBLANES, w))
        row = lax.broadcasted_iota(jnp.int32, (SUBLANES, w), 0)

        def neighbours(ci, x_ref, buf):
            rows = pl.ds(pl.multiple_of(ci * ch, ch), ch)
            if reverse:
                buf[pl.ds(0, ch), :] = x_ref[rows, :].astype(F32)
                nxt = x_ref[pl.ds(pl.multiple_of(jnp.minimum(ci + 1, n_ch - 1) * ch, ch), edge), :].astype(F32)[:SUBLANES]
                first = x_ref[pl.ds(0, edge), :].astype(F32)[:SUBLANES]
                wrap = jnp.where(row < SUBLANES - 1, pltpu.roll(first, SUBLANES - 1, 0), 0.0)
                buf[pl.ds(ch, SUBLANES), :] = jnp.where(ci == n_ch - 1, wrap, nxt)
            else:
                buf[pl.ds(SUBLANES, ch), :] = x_ref[rows, :].astype(F32)
                prv = x_ref[pl.ds(pl.multiple_of(jnp.maximum(ci * ch - edge, 0), edge), edge), :].astype(F32)[SUBLANES:]
                last = x_ref[pl.ds(t - edge, edge), :].astype(F32)[SUBLANES:]
                wrap = jnp.where(row >= 1, pltpu.roll(last, 1, 0), 0.0)
                buf[pl.ds(0, SUBLANES), :] = jnp.where(ci == 0, wrap, prv)

        def sweep(h0, store):
            def chunk(k, carry):
                ci = n_ch - 1 - k if back else k
                rows = pl.ds(pl.multiple_of(ci * ch, ch), ch)
                dyv = dy_ref[rows, :].astype(BF16)
                gr[...] = jnp.dot(dyv, cre_ref[...], preferred_element_type=F32)
                gi_[...] = jnp.dot(dyv, cim_ref[...], preferred_element_type=F32)
                if store:
                    neighbours(ci, xr_ref, xbr)
                    neighbours(ci, xi_ref, xbi)

                def group(g, cc):
                    gidx = gpc - 1 - g if back else g
                    r0 = pl.multiple_of(gidx * SUBLANES, SUBLANES)
                    hr, hi = cc[0], cc[1]
                    nr = a_r * hr - a_i * hi + gr[pl.ds(r0, SUBLANES), :]
                    ni = a_r * hi + a_i * hr + gi_[pl.ds(r0, SUBLANES), :]
                    if not store:
                        return nr, ni
                    lbr[pl.ds(r0, SUBLANES), :] = nr
                    lbi[pl.ds(r0, SUBLANES), :] = ni
                    x0 = pl.multiple_of(r0 + SUBLANES, SUBLANES) if reverse else r0
                    xpr, xpi = xbr[pl.ds(x0, SUBLANES), :], xbi[pl.ds(x0, SUBLANES), :]
                    return nr, ni, cc[2] + nr * xpr + ni * xpi, cc[3] + ni * xpr - nr * xpi

                carry = lax.fori_loop(0, gpc, group, carry)
                if store:
                    lr_ref[rows, :] = lbr[...].astype(BF16)
                    li_ref[rows, :] = lbi[...].astype(BF16)
                return carry

            return lax.fori_loop(0, n_ch, chunk, h0)

        zero = jnp.zeros((SUBLANES, w), F32)
        er, ei = sweep((zero, zero), False)
        pr, pi = _cpow(ar_ref[...], -ai_ref[...], steps)
        cr, ci0 = _segment_carry(er, ei, pr, pi, back)
        _, _, dar, dai = sweep((cr, ci0, zero, zero), True)
        dar_ref[...] = jnp.sum(dar, axis=0, keepdims=True)
        dai_ref[...] = jnp.sum(dai, axis=0, keepdims=True)

    col = lambda i: (0, i)
    return pl.pallas_call(
        body, name=name, grid=(s // w,),
        in_specs=[pl.BlockSpec((t, c), lambda i: (0, 0)), pl.BlockSpec((c, w), col), pl.BlockSpec((c, w), col),
                  pl.BlockSpec((t, w), col), pl.BlockSpec((t, w), col), pl.BlockSpec((1, w), col), pl.BlockSpec((1, w), col)],
        out_specs=[pl.BlockSpec((t, w), col), pl.BlockSpec((t, w), col), pl.BlockSpec((1, w), col), pl.BlockSpec((1, w), col)],
        out_shape=[_out(t, s, BF16), _out(t, s, BF16), _out(1, s, F32), _out(1, s, F32)],
        scratch_shapes=[pltpu.VMEM((ch, w), F32)] * 4 + [pltpu.VMEM((ch + SUBLANES, w), F32)] * 2,
        compiler_params=_params(("parallel",)),
    )(dyp, c_re, c_im, xr, xi, ar, ai)


def _ssm_finish(y0, y1, sp, skip):
    t, c = sp.shape
    steps = t // SUBLANES
    w = _pick(c, LANES)

    def body(y0_ref, y1_ref, sp_ref, d_ref, y_ref, ys_ref):
        rows = pl.ds(pl.program_id(1), steps, stride=SUBLANES)
        y = y0_ref[rows, :] + y1_ref[rows, :] + sp_ref[rows, :] * d_ref[...]
        y_ref[...] = y
        ys_ref[...] = jax.nn.gelu(y).astype(BF16)

    whole = pl.BlockSpec((t, w), lambda j, k: (0, j))
    seg = pl.BlockSpec((steps, w), lambda j, k: (k, j))
    return pl.pallas_call(
        body, name="ssm_finish", grid=(c // w, SUBLANES),
        in_specs=[whole, whole, whole, pl.BlockSpec((1, w), lambda j, k: (0, j))], out_specs=[seg, seg],
        out_shape=[_out(t, c, F32), _out(t, c, BF16)], compiler_params=_params(("parallel", "arbitrary")),
    )(y0, y1, sp, skip)


def _to_segments(a):
    t, c = a.shape
    return a.reshape(SUBLANES, t // SUBLANES, c).transpose(1, 0, 2).reshape(t, c)


def _from_segments(a):
    t, c = a.shape
    return a.reshape(t // SUBLANES, SUBLANES, c).transpose(1, 0, 2).reshape(t, c)


def _colsum_prod(name, a, b, b_coff=0):
    t, n = a.shape
    tm = _pick(t, 512, SUBLANES)

    def body(a_ref, b_ref, o_ref):
        @pl.when(pl.program_id(0) == 0)
        def _():
            o_ref[...] = jnp.zeros_like(o_ref)

        o_ref[...] += jnp.sum(a_ref[...].astype(F32) * b_ref[...].astype(F32), axis=0, keepdims=True)

    return pl.pallas_call(
        body, name=name, grid=(t // tm,),
        in_specs=[pl.BlockSpec((tm, n), lambda i: (i, 0)), pl.BlockSpec((tm, n), lambda i: (i, b_coff))],
        out_specs=pl.BlockSpec((1, n), lambda i: (0, 0)), out_shape=_out(1, n, F32),
        compiler_params=_params(("arbitrary",)),
    )(a, b)


def _ssm_maps(arrs, signs):
    n2, hh, p = arrs[0].shape
    g = n2 // 2

    def body(*refs):
        ins, outs = refs[:len(arrs)], refs[len(arrs):]
        for a, (a_ref, sign) in enumerate(zip(ins, signs)):
            for d in range(2):
                o_ref = outs[2 * a + d]
                o_ref[...] = jnp.zeros_like(o_ref)
                for k in range(g):
                    o_ref[pl.ds(k * hh, hh), pl.ds(k * p, p)] = (sign * a_ref[d * g + k]).astype(BF16)

    outs = pl.pallas_call(body, name="ssm_maps", out_shape=[_out(g * hh, g * p, BF16)] * (2 * len(arrs)))(*arrs)
    return [outs[2 * a:2 * a + 2] for a in range(len(arrs))]


def _softmax(qh, kh, scale):
    s = lax.dot_general(qh, kh, _DIMS["nt"], preferred_element_type=F32) * scale
    e = jnp.exp(s - jnp.max(s, axis=-1, keepdims=True))
    return e / jnp.sum(e, axis=-1, keepdims=True)


def _attn_fwd(q, kv):
    t, d = q.shape
    mm_ = kv.shape[0]
    hd = d // N_XHEADS
    scale = 1.0 / math.sqrt(hd)
    tm = _pick(t, 1024, SUBLANES)

    def body(q_ref, kv_ref, o_ref):
        for h in range(N_XHEADS):
            sl = pl.ds(h * hd, hd)
            p = _softmax(q_ref[:, sl], kv_ref[:, sl], scale)
            o_ref[:, sl] = jnp.dot(p.astype(BF16), kv_ref[:, pl.ds(d + h * hd, hd)],
                                   preferred_element_type=F32).astype(BF16)

    return pl.pallas_call(
        body, name="attn_fwd", grid=(t // tm,),
        in_specs=[pl.BlockSpec((tm, d), lambda i: (i, 0)), pl.BlockSpec((mm_, 2 * d), lambda i: (0, 0))],
        out_specs=pl.BlockSpec((tm, d), lambda i: (i, 0)), out_shape=_out(t, d, BF16),
        compiler_params=_params(("parallel",)),
    )(q, kv)


def _attn_bwd(q, kv, do):
    t, d = q.shape
    mm_ = kv.shape[0]
    hd = d // N_XHEADS
    scale = 1.0 / math.sqrt(hd)
    tm = _pick(t, 1024, SUBLANES)

    def body(q_ref, kv_ref, do_ref, dq_ref, dkv_ref):
        @pl.when(pl.program_id(0) == 0)
        def _():
            dkv_ref[...] = jnp.zeros_like(dkv_ref)

        for h in range(N_XHEADS):
            sl = pl.ds(h * hd, hd)
            vsl = pl.ds(d + h * hd, hd)
            qh, kh, doh = q_ref[:, sl], kv_ref[:, sl], do_ref[:, sl]
            p = _softmax(qh, kh, scale)
            dp = lax.dot_general(doh, kv_ref[:, vsl], _DIMS["nt"], preferred_element_type=F32)
            dkv_ref[:, vsl] += lax.dot_general(p.astype(BF16), doh, _DIMS["tn"], preferred_element_type=F32)
            ds = (p * (dp - jnp.sum(dp * p, axis=-1, keepdims=True)) * scale).astype(BF16)
            dq_ref[:, sl] = jnp.dot(ds, kh, preferred_element_type=F32).astype(BF16)
            dkv_ref[:, sl] += lax.dot_general(ds, qh, _DIMS["tn"], preferred_element_type=F32)

    row = pl.BlockSpec((tm, d), lambda i: (i, 0))
    full = pl.BlockSpec((mm_, 2 * d), lambda i: (0, 0))
    return pl.pallas_call(
        body, name="attn_bwd", grid=(t // tm,), in_specs=[row, full, row], out_specs=[row, full],
        out_shape=[_out(t, d, BF16), _out(mm_, 2 * d, F32)], compiler_params=_params(("arbitrary",)),
    )(q, kv, do)


def _ew(name, fn, ins, outs, rows_pref=256):
    r, c = ins[0].shape
    tr = _pick(r, rows_pref, SUBLANES)
    ni = len(ins)

    def body(*refs):
        res = fn(*[x[...] for x in refs[:ni]])
        for o_ref, v in zip(refs[ni:], res):
            o_ref[...] = v.astype(o_ref.dtype)

    blk = pl.BlockSpec((tr, c), lambda i: (i, 0))
    return pl.pallas_call(
        body, name=name, grid=(r // tr,), in_specs=[blk] * ni, out_specs=[blk] * len(outs),
        out_shape=[_out(r, c, dt) for dt in outs], compiler_params=_params(("parallel",)),
    )(*ins)


def _sum_slots(name, a, dtype):
    s, r, c = a.shape
    tr = _pick(r, 256, SUBLANES)

    def body(a_ref, o_ref):
        acc = a_ref[0].astype(F32)
        for k in range(1, s):
            acc = acc + a_ref[k].astype(F32)
        o_ref[...] = acc.astype(o_ref.dtype)

    return pl.pallas_call(
        body, name=name, grid=(r // tr,), in_specs=[pl.BlockSpec((s, tr, c), lambda i: (0, i, 0))],
        out_specs=pl.BlockSpec((tr, c), lambda i: (i, 0)), out_shape=_out(r, c, dtype),
        compiler_params=_params(("parallel",)),
    )(a)


def _adamw_step(wv, gv, mv, vv):
    bc1 = 1.0 - ADAM_B1 ** ADAM_STEP
    bc2 = 1.0 - ADAM_B2 ** ADAM_STEP
    m2 = ADAM_B1 * mv + (1.0 - ADAM_B1) * gv
    v2 = ADAM_B2 * vv + (1.0 - ADAM_B2) * (gv * gv)
    delta = -ADAM_LR * ((m2 / bc1) / (jnp.sqrt(v2 / bc2) + ADAM_EPS) + ADAM_WD * wv)
    return delta, m2, v2


def _adamw_group(name, items, transposed):
    k, r = items[0][0].shape
    if transposed and r % LANES != 0:
        rows = _adamw_group(name, [(w.T, g, m.T, v.T) for w, g, m, v in items], False)
        return [[a.T for a in item] for item in rows]
    tk = _pick(k, max(SUBLANES, ADAMW_STEP_WORDS // (r * len(items))), SUBLANES)
    n_out = 4 if transposed else 3

    def body(*refs):
        ins, outs = refs[:4 * len(items)], refs[4 * len(items):]
        for i in range(len(items)):
            wv, gv, mv, vv = (a[...] for a in ins[4 * i:4 * i + 4])
            if transposed:
                gv = gv.T
            res = _adamw_step(wv, gv, mv, vv) + ((gv,) if transposed else ())
            for o_ref, val in zip(outs[n_out * i:n_out * (i + 1)], res):
                o_ref[...] = val

    blk = pl.BlockSpec((tk, r), lambda j: (j, 0))
    g_blk = pl.BlockSpec((r, tk), lambda j: (0, j)) if transposed else blk
    res = pl.pallas_call(
        body, name=name, grid=(k // tk,), in_specs=[blk, g_blk, blk, blk] * len(items),
        out_specs=[blk] * (n_out * len(items)), out_shape=[pltpu.HBM((k, r), F32)] * (n_out * len(items)),
        compiler_params=_params(("parallel",)),
    )(*[pltpu.with_memory_space_constraint(a, pltpu.HBM) for item in items for a in item])
    return [list(res[n_out * i:n_out * (i + 1)]) + ([] if transposed else [items[i][1]]) for i in range(len(items))]


def _allgather(name, arrs):
    n = len(arrs)

    def body(*refs):
        ins, outs = refs[:n], refs[n:2 * n]
        send_sems, recv_sems, local_sems = refs[2 * n:]
        x, y, c = lax.axis_index("x"), lax.axis_index("y"), lax.axis_index("c")
        me, sibling = (x, y, c), (x, y, 1 - c)
        chips = [(1 - x, y), (x, 1 - y), (1 - x, 1 - y)]

        def rows(a, px, py, pc):
            r = ins[a].shape[0]
            return outs[a].at[pl.ds((4 * px + 2 * py + pc) * r, r), :]

        def copy(a, k, block, to, src=None):
            return pltpu.make_async_remote_copy(
                src_ref=rows(a, *block) if src is None else src, dst_ref=rows(a, *block),
                send_sem=send_sems.at[a, k], recv_sem=recv_sems.at[a, k], device_id=to, device_id_type=MESH)

        mine = [pltpu.make_async_copy(ins[a], rows(a, *me), local_sems.at[a]) for a in range(n)]
        for cp in mine:
            cp.start()
        first = []
        for a in range(n):
            first.append(copy(a, 0, me, sibling, src=ins[a]))
            first += [copy(a, 1 + j, me, (*chip, c), src=ins[a]) for j, chip in enumerate(chips)]
        for cp in first:
            cp.start()
        passed = []
        for j, chip in enumerate(chips):
            for a in range(n):
                copy(a, 1 + j, (*chip, c), me).wait_recv()
                cp = copy(a, 4 + j, (*chip, c), sibling)
                cp.start()
                passed.append(cp)
        for a in range(n):
            copy(a, 0, sibling, me).wait_recv()
            for j, chip in enumerate(chips):
                copy(a, 4 + j, (*chip, 1 - c), me).wait_recv()
        for cp in first + passed:
            cp.wait_send()
        for cp in mine:
            cp.wait()

    return pl.pallas_call(
        body, name=name, in_specs=[ANY] * n, out_specs=[ANY] * n,
        out_shape=[_out(N_DEV * a.shape[0], a.shape[1], a.dtype) for a in arrs],
        scratch_shapes=[pltpu.SemaphoreType.DMA((n, 7)), pltpu.SemaphoreType.DMA((n, 7)), pltpu.SemaphoreType.DMA((n,))],
    )(*arrs)


def _cores_start(name, blocks):
    n = len(blocks)
    c = blocks[0].shape[2]
    r = sum(b.shape[1] for b in blocks)

    def build(src_refs, land_refs, send_sems, recv_sems):
        x, y, cc = lax.axis_index("x"), lax.axis_index("y"), lax.axis_index("c")
        remote, off = [], 0
        for a, src in enumerate(src_refs):
            rows = pl.ds(off, src.shape[1])
            off += src.shape[1]
            for q in range(4):
                remote.append(pltpu.make_async_remote_copy(
                    src_ref=src.at[2 * q + (1 - cc)], dst_ref=land_refs[0].at[q, rows], send_sem=send_sems.at[4 * a + q],
                    recv_sem=recv_sems.at[4 * a + q], device_id=(x, y, 1 - cc), device_id_type=MESH))
        return remote, []

    return _split_start(name, [(blocks, [jax.ShapeDtypeStruct((4, r, c), blocks[0].dtype)], 4 * n, 0, build)])[0]


def _peer(k, x, y, c):
    return (1 - x if k & 4 else x, 1 - y if k & 2 else y, 1 - c if k & 1 else c)


def _split_start(name, groups, after=None):
    pins = [] if after is None else [after]
    bufs, sem_shapes, spans = [], [], []
    for srcs, land_shapes, n_remote, n_local, _ in groups:
        sems = [pltpu.SemaphoreType.DMA((n_remote,)), pltpu.SemaphoreType.DMA((n_remote,))]
        sems += [pltpu.SemaphoreType.DMA((n_local,))] if n_local else []
        spans.append((len(bufs), len(srcs), len(land_shapes), len(sem_shapes), len(sems)))
        bufs += [pltpu.with_memory_space_constraint(a, pltpu.HBM) for a in srcs]
        bufs += [pltpu.with_memory_space_constraint(lax.empty(s.shape, s.dtype), pltpu.HBM) for s in land_shapes]
        sem_shapes += sems
    n_buf, n_sem = len(bufs), len(sem_shapes)

    def body(*refs):
        buf_refs, sem_refs, token = refs[:n_buf], refs[n_buf + len(pins):n_buf + len(pins) + n_sem], refs[-1]
        for (b0, ns, nl, s0, k), group in zip(spans, groups):
            remote, local = group[4](buf_refs[b0:b0 + ns], buf_refs[b0 + ns:b0 + ns + nl], *sem_refs[s0:s0 + k])
            for cp in local + remote:
                cp.start()
        token[...] = jnp.zeros_like(token)

    outs = pl.pallas_call(
        body, name=name,
        out_shape=sem_shapes + [pltpu.HBM(b.shape, b.dtype) for b in bufs] + [jax.ShapeDtypeStruct((SUBLANES, LANES), F32)],
        in_specs=[HBM] * n_buf + [ANY] * len(pins),
        out_specs=[SEM] * n_sem + [HBM] * n_buf + [pl.BlockSpec(memory_space=pltpu.VMEM)],
        input_output_aliases={i: n_sem + i for i in range(n_buf)},
        compiler_params=pltpu.CompilerParams(has_side_effects=SIDE_EFFECT),
    )(*bufs, *pins)
    return [dict(sems=list(outs[s0:s0 + k]), bufs=list(outs[n_sem + b0:n_sem + b0 + ns + nl]), token=outs[-1],
                 build=group[4], ns=ns) for (b0, ns, nl, s0, k), group in zip(spans, groups)]


def _split_wait(name, started, after):
    ns, n_buf, n_sem = started["ns"], len(started["bufs"]), len(started["sems"])

    def body(*refs):
        src_refs, land_refs = refs[:ns], refs[ns:n_buf]
        sems = refs[n_buf:n_buf + n_sem]
        remote, local = started["build"](src_refs, land_refs, *sems)
        for cp in local:
            cp.wait()
        for cp in remote:
            cp.wait_send()
            cp.wait_recv()

    outs = pl.pallas_call(
        body, name=name, out_shape=[pltpu.HBM(b.shape, b.dtype) for b in started["bufs"]],
        in_specs=[HBM] * n_buf + [SEM] * n_sem + [ANY], out_specs=[HBM] * n_buf,
        input_output_aliases={i: i for i in range(n_buf)},
        compiler_params=pltpu.CompilerParams(has_side_effects=SIDE_EFFECT),
    )(*started["bufs"], *started["sems"], after)
    return list(outs[:ns]), list(outs[ns:])


def _gather_group(shards):
    m = len(shards)

    def build(src_refs, land_refs, send_sems, recv_sems, local_sems):
        x, y, c = lax.axis_index("x"), lax.axis_index("y"), lax.axis_index("c")
        remote, local = [], []
        for j in range(m):
            r = src_refs[j].shape[0]
            dst = land_refs[j].at[pl.ds((4 * x + 2 * y + c) * r, r), :]
            local.append(pltpu.make_async_copy(src_refs[j], dst, local_sems.at[j]))
            for k in range(1, N_DEV):
                remote.append(pltpu.make_async_remote_copy(
                    src_ref=src_refs[j], dst_ref=dst, send_sem=send_sems.at[7 * j + k - 1],
                    recv_sem=recv_sems.at[7 * j + k - 1], device_id=_peer(k, x, y, c), device_id_type=MESH))
        return remote, local

    lands = [jax.ShapeDtypeStruct((N_DEV * a.shape[0], a.shape[1]), a.dtype) for a in shards]
    return shards, lands, 7 * m, m, build


def _slots_start(name, a):
    def build(src_refs, land_refs, send_sems, recv_sems, local_sems):
        x, y, c = lax.axis_index("x"), lax.axis_index("y"), lax.axis_index("c")
        dst = land_refs[0].at[4 * x + 2 * y + c]
        local = [pltpu.make_async_copy(src_refs[0], dst, local_sems.at[0])]
        remote = [pltpu.make_async_remote_copy(
            src_ref=src_refs[0], dst_ref=dst, send_sem=send_sems.at[k - 1], recv_sem=recv_sems.at[k - 1],
            device_id=_peer(k, x, y, c), device_id_type=MESH) for k in range(1, N_DEV)]
        return remote, local

    return _split_start(name, [([a], [jax.ShapeDtypeStruct((N_DEV,) + a.shape, a.dtype)], 7, 1, build)])[0]


def _chips_start(name, p):
    _, r, c = p.shape
    nck = r // GRAD_ROW_TILE

    def build(src_refs, land_refs, send_sems, recv_sems):
        x, y, cc = lax.axis_index("x"), lax.axis_index("y"), lax.axis_index("c")
        remote = []
        for k in range(1, 4):
            px = 1 - x if k >> 1 else x
            py = 1 - y if k & 1 else y
            for j in range(nck):
                rows = pl.ds(j * GRAD_ROW_TILE, GRAD_ROW_TILE)
                remote.append(pltpu.make_async_remote_copy(
                    src_ref=src_refs[0].at[2 * px + py, rows], dst_ref=land_refs[0].at[k - 1, rows],
                    send_sem=send_sems.at[(k - 1) * nck + j], recv_sem=recv_sems.at[(k - 1) * nck + j],
                    device_id=(px, py, cc), device_id_type=MESH))
        return remote, []

    return _split_start(name, [([p], [jax.ShapeDtypeStruct((3, r, c), p.dtype)], 3 * nck, 0, build)])[0]


def _chip_sum(name, p, recv, chip):
    _, r, c = p.shape
    tr = _pick(r, 5 * GRAD_ROW_TILE, GRAD_ROW_TILE)

    def body(chip_ref, p_ref, r_ref, o_ref):
        acc = p_ref[...].astype(F32)
        for k in range(3):
            acc = acc + r_ref[k].astype(F32)
        o_ref[...] = acc

    return pl.pallas_call(
        body, name=name,
        grid_spec=pltpu.PrefetchScalarGridSpec(
            num_scalar_prefetch=1, grid=(r // tr,),
            in_specs=[pl.BlockSpec((None, tr, c), lambda i, chip_ref: (chip_ref[0], i, 0)),
                      pl.BlockSpec((3, tr, c), lambda i, chip_ref: (0, i, 0))],
            out_specs=pl.BlockSpec((tr, c), lambda i, chip_ref: (i, 0))),
        out_shape=_out(r, c, F32), compiler_params=_params(("parallel",)),
    )(chip, p, recv)


def _local_step(x, mem, tgt, wt, sm, ev=None):
    t, d = x.shape
    n_mem = mem.shape[0]
    d_pool = sm["pool_scale"].shape[1]
    ng, pc = sm["pool_w"].shape[0], sm["pool_w"].shape[1]
    d_ssm = sm["ssm_d"].shape[1]
    _, sg, sp, sh = sm["ssm_b_re"].shape
    n_state = sg * sp
    gb, gs = {}, {}

    def emit(name, **kw):
        return ev(name, **kw) if ev is not None else None

    n1 = _rms_fwd("ffn1_norm", x, sm["ffn1_norm"])
    emit("ffn1_norm_done", marker=n1)
    def ffn1_down(hid):
        emit("ffn1_up_done", marker=hid)
        return wt["ffn1_w_down"]

    h1, ffn1_saved = _ffn_fwd("ffn1", x, n1, wt["ffn1_w_gate"], wt["ffn1_w_up"], ffn1_down)
    emit("ffn1_fwd_done", marker=h1)
    u = _rms_fwd("mix_norm", h1, sm["mix_norm"])
    d_in = wt["w_in"].shape[0]
    tm, tn = _pick(t, 2048), _pick(d_in, 1408)
    proj = _mm1("in_proj", "nt", u, wt["w_in"], t, d_in, tm, tn, F32)
    off_s = d_pool // d_ssm
    off_gp = (d_pool + d_ssm)
    off_gs = off_gp + d

    pool_w_bf = sm["pool_w"].astype(BF16)
    pooled, pm = _pool_fwd(proj, pool_w_bf, sm["pool_scale"])

    by_p = lambda a: jnp.swapaxes(a, -1, -2).reshape(2 * sg, sh, sp)
    disc_args = [sm["ssm_a_re"].reshape(2 * sg, 1, sp), sm["ssm_a_im"].reshape(2 * sg, 1, sp),
                 sm["ssm_log_dt"].reshape(2 * sg, 1, 1), by_p(sm["ssm_b_re"]), by_p(sm["ssm_b_im"])]
    abr, abi, bbr, bbi = _ssm_disc(disc_args)
    abr2, abi2 = abr.reshape(2, n_state), abi.reshape(2, n_state)
    b_re, b_im, c_re, c_im = _ssm_maps(
        [bbr, bbi, sm["ssm_c_re"].reshape(2 * sg, sh, sp), sm["ssm_c_im"].reshape(2 * sg, sh, sp)], [1.0, 1.0, 1.0, -1.0])
    sp32 = _to_segments(proj[:, d_pool:d_pool + d_ssm])
    xs, y_parts = [], []
    for dr in range(2):
        xr, xi, y_part = _ssm_fwd(f"ssm_fwd{dr}", sp32, b_re[dr], b_im[dr], c_re[dr], c_im[dr], abr2[dr:dr + 1],
                                  abi2[dr:dr + 1], reverse=(dr == 1))
        xs.append((xr, xi))
        y_parts.append(y_part)
    y, ys = _ssm_finish(y_parts[0], y_parts[1], sp32, sm["ssm_d"])
    tmy = _pick(t, 1024)
    emit("mix_in_done", marker=ys)

    tmm, tnm, tnx = _pick(t, 2048), _pick(d, 256), _pick(d, 512)
    gp_spec = _tile(tmm, tnm, off_gp // tnm)
    gs_spec = _tile(tmm, tnm, off_gs // tnm)

    def merge_epi(accs, gpv, gsv):
        z_pool, val, gate = accs
        return (jax.nn.sigmoid(gpv) * z_pool + jax.nn.sigmoid(gsv) * (val * jax.nn.sigmoid(gate)),)

    merged = _mm("mix_merge", "nt", [pm, ys], [wt["w_pool_proj"], wt["w_glu_val"], wt["w_glu_gate"]],
                 [[(0, 0)], [(1, 1)], [(1, 2)]], t, d, tmm, tnm, [(proj, gp_spec), (proj, gs_spec)], merge_epi,
                 [(_out(t, d, BF16), None)])[0]
    res_epi = lambda accs, hin: (hin + accs[0],)
    h2 = _mm("mix_out", "nn", [merged], [wt["w_mix_out"]], [[(0, 0)]], t, d, tmm, tnx, [(h1, _tile(tmm, tnx))],
             res_epi, [(_out(t, d, F32), None)])[0]

    un = _rms_fwd("xattn_norm", h2, sm["xattn_norm"])
    mn = _rms_fwd("mem_norm", mem, sm["mem_norm"])
    emit("mix_done", marker=un)
    q = _mm1("xattn_q", "nn", un, wt["w_q"], t, d, tmm, tnx, BF16)
    kv = _mm1("xattn_kv", "nt", mn, wt["w_kv"], n_mem, 2 * d, n_mem, _pick(2 * d, 512), BF16)
    o = _attn_fwd(q, kv)
    h3 = _mm("xattn_out", "nn", [o], [wt["w_xo"]], [[(0, 0)]], t, d, tmm, tnx, [(h2, _tile(tmm, tnx))],
             res_epi, [(_out(t, d, F32), None)])[0]

    n2 = _rms_fwd("ffn2_norm", h3, sm["ffn2_norm"])
    emit("xattn_done", marker=n2)
    h4, ffn2_saved = _ffn_fwd("ffn2", h3, n2, wt["ffn2_w_gate"], wt["ffn2_w_up"], wt["ffn2_w_down"])

    dh4, dh4_bf, gs["final_norm"], loss = _loss_head(h4, sm["final_norm"], tgt)
    dh3, dh3_bf, gs["ffn2_norm"], gb["ffn2_w_gate"], gb["ffn2_w_up"], gb["ffn2_w_down"] = _ffn_bwd(
        "ffn2", h3, sm["ffn2_norm"], wt["ffn2_w_gate"], wt["ffn2_w_up"], wt["ffn2_w_down"], ffn2_saved, dh4, dh4_bf)

    tw = _pick(d, 1024)
    do = _mm1("xattn_do", "nt", dh3_bf, wt["w_xo"], t, d, tmm, tnx, BF16)
    gb["w_xo"] = _mm1("xattn_dwxo", "tn", o, dh3_bf, d, d, tw, tnx, BF16)
    dq, dkv = _attn_bwd(q, kv, do)
    gb["w_q"] = _mm1("xattn_dwq", "tn", un, dq, d, d, tw, tnx, BF16)
    dun = _mm1("xattn_dun", "nt", dq, wt["w_q"], t, d, tmm, tnx, F32)
    dh2, dh2_bf, gs["xattn_norm"] = _rms_bwd("xattn_norm_bwd", h2, sm["xattn_norm"], dun, dh3)
    gb["w_kv"] = _mm1("xattn_dwkv", "tn", dkv, mn, 2 * d, d, _pick(2 * d, 512), d, BF16)
    dmn = _mm1("xattn_dmn", "nn", dkv, wt["w_kv"], n_mem, d, n_mem, tnx, F32)
    gs["mem_norm"] = _rms_bwd("mem_norm_bwd", mem, sm["mem_norm"], dmn)

    gb["w_mix_out"] = _mm1("mix_dwout", "tn", merged, dh2_bf, d, d, tw, tnx, BF16)

    def merge_bwd_epi(accs, gpv, gsv):
        dmerged, z_pool, val, gate = accs
        sp_, ss_, sg_ = jax.nn.sigmoid(gpv), jax.nn.sigmoid(gsv), jax.nn.sigmoid(gate)
        glu = val * sg_
        dz_pool = dmerged * sp_
        dg_pool = dmerged * z_pool * (sp_ * (1.0 - sp_))
        dz_ssm = dmerged * ss_
        dg_ssm = dmerged * glu * (ss_ * (1.0 - ss_))
        dval = dz_ssm * sg_
        dgate = dz_ssm * glu * (1.0 - sg_)
        return dz_pool, dg_pool, dg_ssm, dval, dgate

    dz_pool, dg_pool, dg_ssm, dval, dgate = _mm(
        "mix_merge_bwd", "nt", [dh2_bf, pm, ys], [wt["w_mix_out"], wt["w_pool_proj"], wt["w_glu_val"], wt["w_glu_gate"]],
        [[(0, 0)], [(1, 1)], [(2, 2)], [(2, 3)]], t, d, tmm, tnm, [(proj, gp_spec), (proj, gs_spec)], merge_bwd_epi,
        [(_out(t, d, BF16), None)] * 5)
    gb["w_pool_proj"] = _mm1("pool_dwproj", "tn", dz_pool, pm, d, d_pool, tw, d_pool, BF16)
    gb["w_glu_val"] = _mm1("glu_dwval", "tn", dval, ys, d, d_ssm, tw, d_ssm, BF16)
    gb["w_glu_gate"] = _mm1("glu_dwgate", "tn", dgate, ys, d, d_ssm, tw, d_ssm, BF16)

    def gelu_bwd_epi(accs, yv):
        _, vjp = jax.vjp(jax.nn.gelu, yv)
        return (vjp(accs[0])[0],)

    dy = _mm("glu_dy", "nn", [dval, dgate], [wt["w_glu_val"], wt["w_glu_gate"]], [[(0, 0), (1, 1)]], t, d_ssm, tmy, d_ssm,
             [(y, _tile(tmy, d_ssm))], gelu_bwd_epi, [(_out(t, d_ssm, F32), None)])[0]
    gs["ssm_d"] = _colsum_prod("ssm_dd", dy, proj, b_coff=off_s)
    dyp = _to_segments(dy)
    d_abr, d_abi, d_bbr, d_bbi, d_cre, d_cim, lams = [], [], [], [], [], [], []
    ts = _pick(n_state, 512)

    def fold_diag(accs):
        first = pl.program_id(1) * (ts // sp)
        row_group = lax.broadcasted_iota(jnp.int32, (d_ssm, sp), 0) // sh
        folded = []
        for acc in accs:
            out = jnp.zeros((d_ssm, sp), F32)
            for k in range(ts // sp):
                out = out + jnp.where(row_group == first + k, acc[:, sp * k:sp * (k + 1)], 0.0)
            folded.append(out)
        return tuple(folded)

    for dr in range(2):
        lr, li, dar, dai = _ssm_bwd(f"ssm_bwd{dr}", dyp, c_re[dr], c_im[dr], xs[dr][0], xs[dr][1], abr2[dr:dr + 1],
                                    abi2[dr:dr + 1], reverse=(dr == 1))
        d_abr.append(dar)
        d_abi.append(dai)
        lams += [lr, li]
        maps = _mm(f"ssm_dmaps{dr}", "tn", [sp32, dyp], [lr, li, xs[dr][0], xs[dr][1]],
                   [[(0, 0)], [(0, 1)], [(1, 2)], [(1, 3)]], d_ssm, n_state, d_ssm, ts, [], fold_diag,
                   [(_out(n_state // ts * d_ssm, sp, F32), pl.BlockSpec((d_ssm, sp), lambda i, j: (j, 0)))] * 4)
        for acc, m in zip((d_bbr, d_bbi, d_cre, d_cim), maps):
            acc.append(jnp.sum(m.reshape(n_state // ts, sg, sh, sp), axis=0))
    ds = _from_segments(_mm(
        "ssm_ds", "nt", lams, [b_re[0], b_im[0], b_re[1], b_im[1]], [[(k, k) for k in range(4)]], t, d_ssm, tmy,
        d_ssm, [(dyp, _tile(tmy, d_ssm)), (sm["ssm_d"], _rowvec(d_ssm))],
        lambda accs, dyv, dv: (dyv * dv + accs[0],), [(_out(t, d_ssm, BF16), None)])[0])
    cots = [jnp.concatenate(d_abr, axis=0).reshape(2 * sg, 1, sp), jnp.concatenate(d_abi, axis=0).reshape(2 * sg, 1, sp),
            jnp.concatenate(d_bbr, axis=0), jnp.concatenate(d_bbi, axis=0)]
    d_are, d_aim, d_ldt, d_bre, d_bim = _ssm_disc_bwd(disc_args, cots)
    gs["ssm_a_re"] = d_are.reshape(2, sg, sp)
    gs["ssm_a_im"] = d_aim.reshape(2, sg, sp)
    gs["ssm_log_dt"] = d_ldt.reshape(2, sg)
    from_p = lambda a: jnp.swapaxes(a.reshape(2, sg, sh, sp), -1, -2)
    gs["ssm_b_re"], gs["ssm_b_im"] = from_p(d_bre), from_p(d_bim)
    gs["ssm_c_re"] = jnp.stack(d_cre, axis=0)
    gs["ssm_c_im"] = -jnp.stack(d_cim, axis=0)

    dpm = _mm1("pool_dpm", "nn", dz_pool, wt["w_pool_proj"], t, d_pool, tmm, _pick(d_pool, 256), F32)
    dp, gs["pool_w"], gs["pool_scale"] = _pool_bwd(pooled, dpm, pool_w_bf, sm["pool_scale"])

    w_in = wt["w_in"]
    parts = [(dp, 0, d_pool), (ds, d_pool, d_ssm), (dg_pool, off_gp, d), (dg_ssm, off_gs, d)]
    w_in_parts = [w_in[o0:o0 + width] for _, o0, width in parts]
    gb["w_in"] = jnp.concatenate(
        [_mm1(f"in_proj_dw{k}", "tn", p_[0], u, p_[2], d, _pick(p_[2], 1024), tnx, BF16) for k, p_ in enumerate(parts)], axis=0)
    pin = emit("grads_main", gb=gb)
    du = _mm("in_proj_du", "nn", [p_[0] for p_ in parts], w_in_parts, [[(k, k) for k in range(4)]], t, d, tmm, tnx, [],
             lambda accs: (accs[0],), [(_out(t, d, F32), None)], after=pin)[0]
    dh1, dh1_bf, gs["mix_norm"] = _rms_bwd("mix_norm_bwd", h1, sm["mix_norm"], du, dh2)
    pin = emit("small_early", gs=gs, loss=loss)

    def ffn1_weights_done(d_wg, d_wu, d_wd):
        gb["ffn1_w_gate"], gb["ffn1_w_up"], gb["ffn1_w_down"] = d_wg, d_wu, d_wd
        return emit("grads_ffn1", gb=gb)

    dx, _, gs["ffn1_norm"], _, _, _ = _ffn_bwd(
        "ffn1", x, sm["ffn1_norm"], wt["ffn1_w_gate"], wt["ffn1_w_up"], wt["ffn1_w_down"], ffn1_saved, dh1, dh1_bf,
        weights_done=ffn1_weights_done, after=pin)
    return loss, dx, gb, gs


WEIGHTS = ["ffn1_norm", "ffn1_w_gate", "ffn1_w_up", "ffn1_w_down", "mix_norm", "w_in", "pool_w", "pool_scale",
           "w_pool_proj", "ssm_a_re", "ssm_a_im", "ssm_log_dt", "ssm_b_re", "ssm_b_im", "ssm_c_re", "ssm_c_im", "ssm_d",
           "w_glu_val", "w_glu_gate", "w_mix_out", "xattn_norm", "mem_norm", "w_q", "w_kv", "w_xo", "ffn2_norm",
           "ffn2_w_gate", "ffn2_w_up", "ffn2_w_down", "final_norm"]
COL_SHARDED = ["ffn1_w_gate", "ffn1_w_up", "w_in", "w_pool_proj", "w_glu_val", "w_glu_gate", "w_kv", "ffn2_w_gate",
               "ffn2_w_up"]
ROW_SHARDED = ["ffn1_w_down", "w_mix_out", "w_q", "w_xo", "ffn2_w_down"]
BIG = [n for n in WEIGHTS if n in COL_SHARDED or n in ROW_SHARDED]
SMALL = [n for n in WEIGHTS if n not in BIG]
FFN1_BIG = ["ffn1_w_gate", "ffn1_w_up", "ffn1_w_down"]
MAIN_BIG = [n for n in BIG if n not in FFN1_BIG]
GATHER_PLAN = [("ffn1_up_done", ["ffn1_w_down"]), ("ffn1_fwd_done", ["w_in"]),
               ("mix_in_done", ["w_pool_proj", "w_glu_val", "w_glu_gate", "w_mix_out"]),
               ("mix_done", ["w_q", "w_kv", "w_xo"]), ("xattn_done", ["ffn2_w_gate", "ffn2_w_up", "ffn2_w_down"])]
MINOR_SWAPPED = ["ssm_b_re", "ssm_b_im"]
LATE_SMALL = "ffn1_norm"
EARLY_SMALL = [n for n in SMALL if n != LATE_SMALL]
PACK_ROWS = SUBLANES * LANES
GRAD_ROW_TILE = 256
ADAMW_STEP_WORDS = 1 << 19


def _to_rows(name, w):
    return w.T if name in COL_SHARDED else w


def _pack_small(vals):
    flat = []
    for v in vals:
        f = v.reshape(-1)
        flat.append(jnp.pad(f, (0, (-f.shape[0]) % PACK_ROWS)))
    total = sum(f.shape[0] for f in flat)
    flat.append(jnp.zeros(((-total) % (GRAD_ROW_TILE * LANES),), F32))
    return jnp.concatenate(flat).reshape(-1, LANES)


def _unpack_small(packed, shapes):
    out, row = [], 0
    for shp in shapes:
        size = math.prod(shp)
        rows = -(-size // PACK_ROWS) * SUBLANES
        out.append(packed[row:row + rows].reshape(-1)[:size].reshape(shp))
        row += rows
    return out


def kernel(x, mem, ffn1_norm, ffn1_w_gate, ffn1_w_up, ffn1_w_down, mix_norm, w_in, pool_w, pool_scale, w_pool_proj, ssm_a_re, ssm_a_im, ssm_log_dt, ssm_b_re, ssm_b_im, ssm_c_re, ssm_c_im, ssm_d, w_glu_val, w_glu_gate, w_mix_out, xattn_norm, mem_norm, w_q, w_kv, w_xo, ffn2_norm, ffn2_w_gate, ffn2_w_up, ffn2_w_down, final_norm, loss_target, m_ffn1_norm, m_ffn1_w_gate, m_ffn1_w_up, m_ffn1_w_down, m_mix_norm, m_w_in, m_pool_w, m_pool_scale, m_w_pool_proj, m_ssm_a_re, m_ssm_a_im, m_ssm_log_dt, m_ssm_b_re, m_ssm_b_im, m_ssm_c_re, m_ssm_c_im, m_ssm_d, m_w_glu_val, m_w_glu_gate, m_w_mix_out, m_xattn_norm, m_mem_norm, m_w_q, m_w_kv, m_w_xo, m_ffn2_norm, m_ffn2_w_gate, m_ffn2_w_up, m_ffn2_w_down, m_final_norm, v_ffn1_norm, v_ffn1_w_gate, v_ffn1_w_up, v_ffn1_w_down, v_mix_norm, v_w_in, v_pool_w, v_pool_scale, v_w_pool_proj, v_ssm_a_re, v_ssm_a_im, v_ssm_log_dt, v_ssm_b_re, v_ssm_b_im, v_ssm_c_re, v_ssm_c_im, v_ssm_d, v_w_glu_val, v_w_glu_gate, v_w_mix_out, v_xattn_norm, v_mem_norm, v_w_q, v_w_kv, v_w_xo, v_ffn2_norm, v_ffn2_w_gate, v_ffn2_w_up, v_ffn2_w_down, v_final_norm):
    given = dict(locals())
    wts = {n: given[n] for n in WEIGHTS}
    moms = {n: (given["m_" + n], given["v_" + n]) for n in WEIGHTS}
    x2, mem2, tgt2 = x[0], mem[0], loss_target[0]
    d = x2.shape[1]
    chip = (2 * lax.axis_index("x") + lax.axis_index("y")).astype(jnp.int32).reshape(1)

    def full_form(n, f):
        shard = wts[n][0].shape
        return f.reshape(N_DEV * shard[1], shard[0]) if n in COL_SHARDED else f.reshape(N_DEV * shard[0], shard[1])

    shards = {n: _to_rows(n, wts[n][0]).astype(BF16) for n in BIG}
    first = FFN1_BIG[:2]
    wt = {n: full_form(n, f) for n, f in zip(first, _allgather("weight_allgather_first", [shards[n] for n in first]))}
    started = _split_start("weight_gather_start", [_gather_group([shards[n] for n in names]) for _, names in GATHER_PLAN],
                           after=wt[first[0]])
    gathers = {event: (names, st) for (event, names), st in zip(GATHER_PLAN, started)}
    sm = {n: (wts[n].reshape(1, -1) if wts[n].ndim <= 2 else wts[n][0]) for n in SMALL}
    sm["ffn1_norm"] = sm["ffn1_norm"] + started[0]["token"][0, 0]

    pending = {}

    def reduce_start(tag, names, gb):
        blocks = [gb[n].reshape(N_DEV, -1, d) for n in names]
        pad_rows = (-sum(b.shape[1] for b in blocks)) % GRAD_ROW_TILE
        pad = [jnp.zeros((N_DEV, pad_rows, d), BF16)] if pad_rows else []
        started = _cores_start("grad_exchange_cores_start_" + tag, blocks + pad)
        own = jnp.concatenate([lax.dynamic_index_in_dim(b.reshape(4, 2, b.shape[1], d), lax.axis_index("c"), 1, False)
                               for b in started["bufs"][:len(blocks + pad)]], axis=1)
        _, (recv,) = _split_wait("grad_exchange_cores_wait_" + tag, started, own)
        rows_all = own.shape[1]
        pair = _ew("grad_pair_sum_" + tag, lambda a, b: (a.astype(F32) + b.astype(F32),),
                   [own.reshape(-1, d), recv.reshape(-1, d)], [BF16], rows_pref=5 * GRAD_ROW_TILE)[0]
        pair = pair.reshape(4, rows_all, d)
        pending[tag] = (pair, _chips_start("grad_exchange_chips_start_" + tag, pair), [b.shape[1] for b in blocks])
        return pending[tag][1]["token"]

    def reduce_finish(tag, after):
        _, started, rows = pending[tag]
        (pair,), (recv,) = _split_wait("grad_exchange_chips_wait_" + tag, started, after)
        return _chip_sum("grad_chip_sum_" + tag, pair, recv, chip), rows

    def ev(name, gb=None, gs=None, loss=None, marker=None):
        if name in gathers:
            names, started = gathers[name]
            for n, f in zip(names, _split_wait("weight_gather_wait_" + name, started, marker)[1]):
                wt[n] = full_form(n, f)
        elif name == "grads_main":
            return reduce_start("main", MAIN_BIG, gb)
        elif name == "small_early":
            pending["small"] = _slots_start("small_gather_start", _pack_small([gs[n] for n in EARLY_SMALL] + [loss[:, :1]]))
            return pending["small"]["token"]
        elif name == "grads_ffn1":
            return reduce_start("ffn1", FFN1_BIG, gb)
        return None

    _, dx, _, gs = _local_step(x2, mem2, tgt2, wt, sm, ev)

    grads = {}
    for tag, names in (("main", MAIN_BIG), ("ffn1", FFN1_BIG)):
        g_rows, rows = reduce_finish(tag, dx)
        off = 0
        for n, r in zip(names, rows):
            shard = wts[n].shape
            grads[n] = g_rows[off:off + r].reshape((shard[2], shard[1]) if n in COL_SHARDED else shard[1:])
            off += r
    small_sum = _sum_slots("small_sum", _split_wait("small_gather_wait", pending["small"], dx)[1][0], F32)
    late = _allgather("small_allgather_late", [gs[LATE_SMALL].reshape(-1, LANES)])[0]
    late_sum = _sum_slots("small_sum_late", late.reshape(N_DEV, -1, LANES), F32)
    vals = _unpack_small(small_sum, [wts[n].shape for n in EARLY_SMALL] + [(1, 1)])
    total_loss = vals[-1].reshape(())
    def flat(n, a):
        a = a.reshape(wts[n].shape)
        a = jnp.swapaxes(a, -1, -2) if n in MINOR_SWAPPED else a
        return a.reshape(-1, a.shape[-1])

    def unflat(n, a):
        shape = wts[n].shape
        if n in MINOR_SWAPPED:
            return jnp.swapaxes(a.reshape(shape[:-2] + (shape[-1], shape[-2])), -1, -2)
        return a.reshape(shape)

    for n, g_full in zip(EARLY_SMALL + [LATE_SMALL], vals[:-1] + [late_sum]):
        grads[n] = flat(n, g_full)

    out_g, out_d, out_m, out_v = {}, {}, {}, {}
    by_shape = {}
    for n in WEIGHTS:
        by_shape.setdefault((flat(n, wts[n]).shape, n in COL_SHARDED), []).append(n)
    for (_, transposed), names in by_shape.items():
        items = [(flat(n, wts[n]), grads[n], flat(n, moms[n][0]), flat(n, moms[n][1])) for n in names]
        for n, res in zip(names, _adamw_group("adamw_" + names[0], items, transposed)):
            out_d[n], out_m[n], out_v[n], out_g[n] = (unflat(n, a) for a in res)

    return (total_loss, dx[None], *[out_g[n] for n in WEIGHTS], *[out_d[n] for n in WEIGHTS],
            *[out_m[n] for n in WEIGHTS], *[out_v[n] for n in WEIGHTS])
```

```python
import math

import jax
import jax.numpy as jnp
from jax import lax
from jax.experimental import pallas as pl
from jax.experimental.pallas import tpu as pltpu

F32 = jnp.float32
BF16 = jnp.bfloat16
EPS = 1e-6
N_XHEADS = 4
POOL_WINDOWS = (2, 4, 8, 16)
ADAM_LR = 0.001
ADAM_B1 = 0.9
ADAM_B2 = 0.999
ADAM_EPS = 1e-08
ADAM_WD = 0.01
ADAM_STEP = 10
N_DEV = 8
VMEM_LIMIT_V7X = 48 * 1024 * 1024
LANES = 128
SUBLANES = 8
SUB_ROWS = 256
POOL_PAD = 16
MESH = pl.DeviceIdType.MESH
ANY = pl.BlockSpec(memory_space=pl.ANY)
HBM = pl.BlockSpec(memory_space=pltpu.HBM)
SEM = pl.BlockSpec(memory_space=pltpu.SEMAPHORE)
SIDE_EFFECT = pltpu.SideEffectType.DATAFLOW_SIDE_EFFECTING

_DIMS = {
    "nt": (((1,), (1,)), ((), ())),
    "nn": (((1,), (0,)), ((), ())),
    "tn": (((0,), (0,)), ((), ())),
}


def _pick(dim, pref, mult=LANES):
    if dim <= pref:
        return dim
    for t in range(pref - pref % mult, 0, -mult):
        if dim % t == 0:
            return t
    return dim


def _params(sem):
    return pltpu.CompilerParams(dimension_semantics=sem, vmem_limit_bytes=VMEM_LIMIT_V7X)


def _tile(tm, tn, coff=0):
    return pl.BlockSpec((tm, tn), lambda i, j: (i, j + coff))


def _rowvec(tn, coff=0):
    return pl.BlockSpec((1, tn), lambda i, j: (0, j + coff))


def _out(m, n, dtype):
    return jax.ShapeDtypeStruct((m, n), dtype)


def _mm(name, form, a_list, b_list, groups, m, n, tm, tn, extras, epilogue, outs, after=None, sub=SUB_ROWS):
    na, nb, ne = len(a_list), len(b_list), len(extras)
    pins = [] if after is None else [after]
    step = tm if (sub is None or form == "tn" or tm % sub) else sub

    def a_spec(a):
        if form == "tn":
            return pl.BlockSpec((a.shape[0], tm), lambda i, j: (0, i))
        return pl.BlockSpec((tm, a.shape[1]), lambda i, j: (i, 0))

    def b_spec(b):
        if form == "nt":
            return pl.BlockSpec((tn, b.shape[1]), lambda i, j: (j, 0))
        return pl.BlockSpec((b.shape[0], tn), lambda i, j: (0, j))

    def body(*refs):
        a_refs, b_refs = refs[:na], refs[na:na + nb]
        e_refs, o_refs = refs[na + nb:na + nb + ne], refs[na + nb + ne + len(pins):]
        b_vals = {}
        for s0 in range(0, tm, step):
            rows = slice(None) if step == tm else pl.ds(s0, step)
            a_vals, accs = {}, []
            for group in groups:
                acc = None
                for ai, bi in group:
                    if ai not in a_vals:
                        a_vals[ai] = (a_refs[ai][...] if form == "tn" else a_refs[ai][rows, :]).astype(BF16)
                    if bi not in b_vals:
                        b_vals[bi] = b_refs[bi][...].astype(BF16)
                    d = lax.dot_general(a_vals[ai], b_vals[bi], _DIMS[form], preferred_element_type=F32)
                    acc = d if acc is None else acc + d
                accs.append(acc)
            res = epilogue(accs, *[e[rows, :] if e.shape[0] == tm else e[...] for e in e_refs])
            for o_ref, r in zip(o_refs, res):
                o_ref[rows, :] = r.astype(o_ref.dtype)

    out_specs = [_tile(tm, tn) if s is None else s for _, s in outs]
    res = pl.pallas_call(
        body, name=name, grid=(m // tm, n // tn),
        in_specs=[a_spec(a) for a in a_list] + [b_spec(b) for b in b_list] + [s for _, s in extras] + [ANY] * len(pins),
        out_specs=out_specs, out_shape=[o for o, _ in outs],
        compiler_params=_params(("parallel", "parallel")),
    )(*a_list, *b_list, *[e for e, _ in extras], *pins)
    return res


def _mm1(name, form, a, b, m, n, tm, tn, dtype, scale=None):
    epi = (lambda accs: (accs[0],)) if scale is None else (lambda accs: (accs[0] * scale,))
    return _mm(name, form, [a], [b], [[(0, 0)]], m, n, tm, tn, [], epi, [(_out(m, n, dtype), None)])[0]


def _rms_fwd(name, h, g):
    t, d = h.shape
    tm = _pick(t, 512, SUBLANES)

    def body(h_ref, g_ref, n_ref):
        hv = h_ref[...]
        r = lax.rsqrt(jnp.mean(hv * hv, axis=-1, keepdims=True) + EPS)
        n_ref[...] = ((hv * r) * g_ref[...]).astype(BF16)

    return pl.pallas_call(
        body, name=name, grid=(t // tm,),
        in_specs=[pl.BlockSpec((tm, d), lambda i: (i, 0)), pl.BlockSpec((1, d), lambda i: (0, 0))],
        out_specs=pl.BlockSpec((tm, d), lambda i: (i, 0)), out_shape=_out(t, d, BF16),
        compiler_params=_params(("parallel",)),
    )(h, g)


def _rms_bwd(name, h, g, dn, dres=None):
    t, d = h.shape
    tm = _pick(t, 512, SUBLANES)
    need_dh = dres is not None

    def body(*refs):
        if need_dh:
            h_ref, g_ref, dn_ref, dres_ref, dh_ref, dhb_ref, dg_ref = refs
        else:
            h_ref, g_ref, dn_ref, dg_ref = refs
        hv = h_ref[...]
        r = lax.rsqrt(jnp.mean(hv * hv, axis=-1, keepdims=True) + EPS)
        nh = hv * r
        dnv = dn_ref[...].astype(F32)

        @pl.when(pl.program_id(0) == 0)
        def _():
            dg_ref[...] = jnp.zeros_like(dg_ref)

        dg_ref[...] += jnp.sum(dnv * nh, axis=0, keepdims=True)
        if need_dh:
            dng = dnv * g_ref[...]
            dh = dres_ref[...] + r * (dng - nh * jnp.mean(dng * nh, axis=-1, keepdims=True))
            dh_ref[...] = dh
            dhb_ref[...] = dh.astype(BF16)

    row = pl.BlockSpec((tm, d), lambda i: (i, 0))
    vec = pl.BlockSpec((1, d), lambda i: (0, 0))
    if need_dh:
        return pl.pallas_call(
            body, name=name, grid=(t // tm,), in_specs=[row, vec, row, row], out_specs=[row, row, vec],
            out_shape=[_out(t, d, F32), _out(t, d, BF16), _out(1, d, F32)], compiler_params=_params(("arbitrary",)),
        )(h, g, dn, dres)
    return pl.pallas_call(
        body, name=name, grid=(t // tm,), in_specs=[row, vec, row], out_specs=vec,
        out_shape=_out(1, d, F32), compiler_params=_params(("arbitrary",)),
    )(h, g, dn)


def _loss_head(h, g, tgt):
    t, d = h.shape
    tm = _pick(t, 512, SUBLANES)

    def body(h_ref, g_ref, t_ref, dh_ref, dhb_ref, dg_ref, loss_ref):
        hv = h_ref[...]
        r = lax.rsqrt(jnp.mean(hv * hv, axis=-1, keepdims=True) + EPS)
        nh = hv * r
        err = nh * g_ref[...] - t_ref[...]

        @pl.when(pl.program_id(0) == 0)
        def _():
            dg_ref[...] = jnp.zeros_like(dg_ref)
            loss_ref[...] = jnp.zeros_like(loss_ref)

        per_row = jnp.mean(err * err, axis=-1, keepdims=True)
        loss_ref[...] += 0.5 * jnp.sum(per_row, axis=0, keepdims=True)
        dy = err * (1.0 / d)
        dg_ref[...] += jnp.sum(dy * nh, axis=0, keepdims=True)
        dng = dy * g_ref[...]
        dh = r * (dng - nh * jnp.mean(dng * nh, axis=-1, keepdims=True))
        dh_ref[...] = dh
        dhb_ref[...] = dh.astype(BF16)

    row = pl.BlockSpec((tm, d), lambda i: (i, 0))
    vec = pl.BlockSpec((1, d), lambda i: (0, 0))
    return pl.pallas_call(
        body, name="loss_head", grid=(t // tm,), in_specs=[row, vec, row],
        out_specs=[row, row, vec, pl.BlockSpec((1, LANES), lambda i: (0, 0))],
        out_shape=[_out(t, d, F32), _out(t, d, BF16), _out(1, d, F32), _out(1, LANES, F32)],
        compiler_params=_params(("arbitrary",)),
    )(h, g, tgt)


def _ffn_fwd(tag, h, n, wg_t, wu_t, wd):
    t, d = h.shape
    f = wg_t.shape[0]
    tm, tn = _pick(t, 1024), _pick(f, 1408)

    def up_epi(accs):
        a, b = accs
        return a, b, (a * jax.nn.sigmoid(a)) * b

    a, b, hid = _mm(tag + "_up", "nt", [n], [wg_t, wu_t], [[(0, 0)], [(0, 1)]], t, f, tm, tn, [], up_epi,
                    [(_out(t, f, BF16), None)] * 3)
    if callable(wd):
        wd = wd(hid)
    tm2, tn2 = _pick(t, 1024), _pick(d, 512)
    h_out = _mm(tag + "_down", "nn", [hid], [wd], [[(0, 0)]], t, d, tm2, tn2, [(h, _tile(tm2, tn2))],
                lambda accs, hin: (hin + 0.5 * accs[0],), [(_out(t, d, F32), None)])[0]
    return h_out, (n, a, b, hid)


def _ffn_bwd(tag, h, g, wg_t, wu_t, wd, saved, dh, dh_bf, weights_done=None, after=None):
    n, a, b, hid = saved
    t, d = h.shape
    f = wd.shape[0]
    tm, tn = _pick(t, 1024), _pick(f, 1408)

    def hid_epi(accs, av, bv):
        dhid = 0.5 * accs[0]
        av, bv = av.astype(F32), bv.astype(F32)
        sig = jax.nn.sigmoid(av)
        da = dhid * bv * (sig * (1.0 + av * (1.0 - sig)))
        db = dhid * (av * sig)
        return da, db

    da, db = _mm(tag + "_bwd_hid", "nt", [dh_bf], [wd], [[(0, 0)]], t, f, tm, tn,
                 [(a, _tile(tm, tn)), (b, _tile(tm, tn))], hid_epi, [(_out(t, f, BF16), None)] * 2, after=after)
    tw, tnw = _pick(f, 1408), _pick(d, 512)
    d_wd = _mm1(tag + "_dwd", "tn", hid, dh_bf, f, d, tw, tnw, BF16, scale=0.5)
    d_wg = _mm1(tag + "_dwg", "tn", da, n, f, d, tw, tnw, BF16)
    d_wu = _mm1(tag + "_dwu", "tn", db, n, f, d, tw, tnw, BF16)
    pin = weights_done(d_wg, d_wu, d_wd) if weights_done is not None else None
    tm2, tn2 = _pick(t, 1024), _pick(d, 512)
    dn = _mm(tag + "_dn", "nn", [da, db], [wg_t, wu_t], [[(0, 0), (1, 1)]], t, d, tm2, tn2, [],
             lambda accs: (accs[0],), [(_out(t, d, F32), None)], after=pin)[0]
    dh_in, dh_in_bf, dg = _rms_bwd(tag + "_norm_bwd", h, g, dn, dh)
    return dh_in, dh_in_bf, dg, d_wg, d_wu, d_wd


def _window_sum(win, offsets):
    n = win.shape[0]
    acc = None
    for j in offsets:
        term = win if j == 0 else pltpu.roll(win, (-j) % n, 0)
        acc = term if acc is None else acc + term
    return acc


def _pool_counts(r0, ch, c, left, right, t):
    pos = r0 + lax.broadcasted_iota(jnp.int32, (ch, c), 0)
    return (jnp.minimum(pos + right + 1, t) - jnp.maximum(pos - left, 0)).astype(F32)


def _pool_fwd(proj, pool_w_bf, pool_scale):
    t = proj.shape[0]
    ng, c, _ = pool_w_bf.shape
    ch = _pick(t, 256, SUBLANES)
    pad = POOL_PAD

    def body(p_ref, w_ref, s_ref, pooled_ref, pm_ref, buf):
        grp = pl.program_id(0)
        buf[pl.ds(0, pad), :] = jnp.zeros((pad, c), F32)
        buf[pl.ds(pad + t, pad), :] = jnp.zeros((pad, c), F32)

        def fill(ci, carry):
            r0 = pl.multiple_of(ci * ch, SUBLANES)
            buf[pl.ds(pl.multiple_of(r0 + pad, SUBLANES), ch), :] = p_ref[pl.ds(r0, ch), :]
            return carry

        lax.fori_loop(0, t // ch, fill, 0)
        for gi, w in enumerate(POOL_WINDOWS):
            left = w // 2
            right = w - 1 - left

            @pl.when(grp == gi)
            def _(left=left, right=right):
                def chunk(ci, carry):
                    r0 = pl.multiple_of(ci * ch, SUBLANES)
                    win = buf[pl.ds(r0, ch + 2 * pad), :]
                    s = _window_sum(win, range(-left, right + 1))[pad:pad + ch]
                    pooled = s / _pool_counts(r0, ch, c, left, right, t) - win[pad:pad + ch]
                    pooled_bf = pooled.astype(BF16)
                    mixed = jnp.dot(pooled_bf, w_ref[0], preferred_element_type=F32)
                    pooled_ref[pl.ds(r0, ch), :] = pooled_bf
                    pm_ref[pl.ds(r0, ch), :] = (mixed * s_ref[...]).astype(BF16)
                    return carry

                lax.fori_loop(0, t // ch, chunk, 0)

    col = pl.BlockSpec((t, c), lambda g: (0, g))
    return pl.pallas_call(
        body, name="pool_fwd", grid=(ng,),
        in_specs=[col, pl.BlockSpec((1, c, c), lambda g: (g, 0, 0)), pl.BlockSpec((1, c), lambda g: (0, g))],
        out_specs=[col, col], out_shape=[_out(t, ng * c, BF16), _out(t, ng * c, BF16)],
        scratch_shapes=[pltpu.VMEM((t + 2 * pad, c), F32)],
        compiler_params=_params(("parallel",)),
    )(proj, pool_w_bf, pool_scale)


def _pool_bwd(pooled, dpm, pool_w_bf, pool_scale):
    t = pooled.shape[0]
    ng, c, _ = pool_w_bf.shape
    ch = _pick(t, 256, SUBLANES)
    pad = POOL_PAD

    def body(pooled_ref, dpm_ref, w_ref, s_ref, dp_ref, dw_ref, ds_ref, buf, raw):
        grp = pl.program_id(0)
        buf[pl.ds(0, pad), :] = jnp.zeros((pad, c), F32)
        buf[pl.ds(pad + t, pad), :] = jnp.zeros((pad, c), F32)
        dw_ref[...] = jnp.zeros_like(dw_ref)
        ds_ref[...] = jnp.zeros_like(ds_ref)
        for gi, w in enumerate(POOL_WINDOWS):
            left = w // 2
            right = w - 1 - left

            @pl.when(grp == gi)
            def _(left=left, right=right):
                def first(ci, carry):
                    r0 = pl.multiple_of(ci * ch, SUBLANES)
                    pv = pooled_ref[pl.ds(r0, ch), :]
                    dpm_v = dpm_ref[pl.ds(r0, ch), :]
                    mixed = jnp.dot(pv, w_ref[0], preferred_element_type=F32)
                    ds_ref[...] += jnp.sum(dpm_v * mixed, axis=0, keepdims=True)
                    dmixed = (dpm_v * s_ref[...]).astype(BF16)
                    dw_ref[0] += lax.dot_general(pv, dmixed, _DIMS["tn"], preferred_element_type=F32)
                    dpooled = lax.dot_general(dmixed, w_ref[0], _DIMS["nt"], preferred_element_type=F32)
                    raw[pl.ds(r0, ch), :] = dpooled
                    buf[pl.ds(pl.multiple_of(r0 + pad, SUBLANES), ch), :] = (
                        dpooled / _pool_counts(r0, ch, c, left, right, t))
                    return carry

                lax.fori_loop(0, t // ch, first, 0)

                def second(ci, carry):
                    r0 = pl.multiple_of(ci * ch, SUBLANES)
                    win = buf[pl.ds(r0, ch + 2 * pad), :]
                    s = _window_sum(win, range(-right, left + 1))[pad:pad + ch]
                    dp_ref[pl.ds(r0, ch), :] = (s - raw[pl.ds(r0, ch), :]).astype(BF16)
                    return carry

                lax.fori_loop(0, t // ch, second, 0)

    col = pl.BlockSpec((t, c), lambda g: (0, g))
    return pl.pallas_call(
        body, name="pool_bwd", grid=(ng,),
        in_specs=[col, col, pl.BlockSpec((1, c, c), lambda g: (g, 0, 0)), pl.BlockSpec((1, c), lambda g: (0, g))],
        out_specs=[col, pl.BlockSpec((1, c, c), lambda g: (g, 0, 0)), pl.BlockSpec((1, c), lambda g: (0, g))],
        out_shape=[_out(t, ng * c, BF16), jax.ShapeDtypeStruct((ng, c, c), F32), _out(1, ng * c, F32)],
        scratch_shapes=[pltpu.VMEM((t + 2 * pad, c), F32), pltpu.VMEM((t, c), F32)],
        compiler_params=_params(("parallel",)),
    )(pooled, dpm, pool_w_bf, pool_scale)


def _discretise(a_re, a_im, log_dt, b_re, b_im):
    dt = jnp.exp(log_dt)
    mag = jnp.exp(dt * a_re)
    ang = dt * a_im
    abr = mag * jnp.cos(ang)
    abi = mag * jnp.sin(ang)
    den = a_re * a_re + a_im * a_im
    nr = abr - 1.0
    qr = (nr * a_re + abi * a_im) / den
    qi = (abi * a_re - nr * a_im) / den
    return abr, abi, qr * b_re - qi * b_im, qr * b_im + qi * b_re


def _ssm_disc(args):
    def body(ar, ai, ld, br, bi, o1, o2, o3, o4):
        res = _discretise(ar[...], ai[...], ld[...], br[...], bi[...])
        for o, r in zip((o1, o2, o3, o4), res):
            o[...] = r

    like = lambda a: jax.ShapeDtypeStruct(a.shape, F32)
    return pl.pallas_call(
        body, name="ssm_disc", out_shape=[like(args[0]), like(args[0]), like(args[3]), like(args[3])],
    )(*args)


def _ssm_disc_bwd(args, cots):
    def body(ar, ai, ld, br, bi, c1, c2, c3, c4, o1, o2, o3, o4, o5):
        _, vjp = jax.vjp(_discretise, ar[...], ai[...], ld[...], br[...], bi[...])
        res = vjp((c1[...], c2[...], c3[...], c4[...]))
        for o, r in zip((o1, o2, o3, o4, o5), res):
            o[...] = r

    return pl.pallas_call(
        body, name="ssm_disc_bwd", out_shape=[jax.ShapeDtypeStruct(a.shape, F32) for a in args],
    )(*args, *cots)


def _cmul(pr, pi, qr, qi):
    return pr * qr - pi * qi, pr * qi + pi * qr


def _cpow(pr, pi, n):
    rr, ri = None, None
    while n:
        if n & 1:
            rr, ri = (pr, pi) if rr is None else _cmul(rr, ri, pr, pi)
        n >>= 1
        if n:
            pr, pi = _cmul(pr, pi, pr, pi)
    return rr, ri


def _segment_carry(er, ei, pr, pi, reverse):
    row = lax.broadcasted_iota(jnp.int32, er.shape, 0)
    cr, ci = jnp.zeros_like(er), jnp.zeros_like(ei)
    for _ in range(SUBLANES - 1):
        tr = er + pr * cr - pi * ci
        ti = ei + pr * ci + pi * cr
        if reverse:
            keep, shift = row < SUBLANES - 1, SUBLANES - 1
        else:
            keep, shift = row >= 1, 1
        cr = jnp.where(keep, pltpu.roll(tr, shift, 0), 0.0)
        ci = jnp.where(keep, pltpu.roll(ti, shift, 0), 0.0)
    return cr, ci


def _ssm_fwd(name, sp, b_re, b_im, c_re, c_im, ar, ai, reverse):
    t, c = sp.shape
    s = ar.shape[1]
    w = _pick(s, 512)
    ch = _pick(t, 1024, SUBLANES)
    n_ch, gpc, steps = t // ch, ch // SUBLANES, t // SUBLANES

    def body(sp_ref, bre_ref, bim_ref, cre_ref, cim_ref, ar_ref, ai_ref, xr_ref, xi_ref, y_ref, ur, ui, xbr, xbi):
        a_r = jnp.broadcast_to(ar_ref[...], (SUBLANES, w))
        a_i = jnp.broadcast_to(ai_ref[...], (SUBLANES, w))

        @pl.when(pl.program_id(0) == 0)
        def _():
            y_ref[...] = jnp.zeros_like(y_ref)

        def sweep(h0, store):
            def chunk(k, h):
                ci = n_ch - 1 - k if reverse else k
                rows = pl.ds(pl.multiple_of(ci * ch, ch), ch)
                spv = sp_ref[rows, :].astype(BF16)
                ur[...] = jnp.dot(spv, bre_ref[...], preferred_element_type=F32)
                ui[...] = jnp.dot(spv, bim_ref[...], preferred_element_type=F32)

                def group(g, hh):
                    gi = gpc - 1 - g if reverse else g
                    r0 = pl.multiple_of(gi * SUBLANES, SUBLANES)
                    hr, hi = hh
                    nr = a_r * hr - a_i * hi + ur[pl.ds(r0, SUBLANES), :]
                    ni = a_r * hi + a_i * hr + ui[pl.ds(r0, SUBLANES), :]
                    if store:
                        xbr[pl.ds(r0, SUBLANES), :] = nr
                        xbi[pl.ds(r0, SUBLANES), :] = ni
                    return nr, ni

                h = lax.fori_loop(0, gpc, group, h)
                if store:
                    xr16, xi16 = xbr[...].astype(BF16), xbi[...].astype(BF16)
                    xr_ref[rows, :] = xr16
                    xi_ref[rows, :] = xi16
                    y_ref[rows, :] += (lax.dot_general(xr16, cre_ref[...], _DIMS["nt"], preferred_element_type=F32)
                                       + lax.dot_general(xi16, cim_ref[...], _DIMS["nt"], preferred_element_type=F32))
                return h

            return lax.fori_loop(0, n_ch, chunk, h0)

        zero = jnp.zeros((SUBLANES, w), F32)
        er, ei = sweep((zero, zero), False)
        pr, pi = _cpow(ar_ref[...], ai_ref[...], steps)
        sweep(_segment_carry(er, ei, pr, pi, reverse), True)

    col = lambda i: (0, i)
    return pl.pallas_call(
        body, name=name, grid=(s // w,),
        in_specs=[pl.BlockSpec((t, c), lambda i: (0, 0))] + [pl.BlockSpec((c, w), col)] * 4
        + [pl.BlockSpec((1, w), col)] * 2,
        out_specs=[pl.BlockSpec((t, w), col), pl.BlockSpec((t, w), col), pl.BlockSpec((t, c), lambda i: (0, 0))],
        out_shape=[_out(t, s, BF16), _out(t, s, BF16), _out(t, c, F32)],
        scratch_shapes=[pltpu.VMEM((ch, w), F32)] * 4,
        compiler_params=_params(("arbitrary",)),
    )(sp, b_re, b_im, c_re, c_im, ar, ai)


def _ssm_bwd(name, dyp, c_re, c_im, xr, xi, ar, ai, reverse):
    t, c = dyp.shape
    s = ar.shape[1]
    w = _pick(s, 512)
    ch = _pick(t, 512, SUBLANES)
    n_ch, gpc, steps = t // ch, ch // SUBLANES, t // SUBLANES
    back = not reverse
    edge = 2 * SUBLANES

    def body(dy_ref, cre_ref, cim_ref, xr_ref, xi_ref, ar_ref, ai_ref, lr_ref, li_ref, dar_ref, dai_ref,
             gr, gi_, lbr, lbi, xbr, xbi):
        a_r = jnp.broadcast_to(ar_ref[...], (SUBLANES, w))
        a_i = -jnp.broadcast_to(ai_ref[...], (SUBLANES, w))
        row = lax.broadcasted_iota(jnp.int32, (SUBLANES, w), 0)

        def neighbours(ci, x_ref, buf):
            rows = pl.ds(pl.multiple_of(ci * ch, ch), ch)
            if reverse:
                buf[pl.ds(0, ch), :] = x_ref[rows, :].astype(F32)
                nxt = x_ref[pl.ds(pl.multiple_of(jnp.minimum(ci + 1, n_ch - 1) * ch, ch), edge), :].astype(F32)[:SUBLANES]
                first = x_ref[pl.ds(0, edge), :].astype(F32)[:SUBLANES]
                wrap = jnp.where(row < SUBLANES - 1, pltpu.roll(first, SUBLANES - 1, 0), 0.0)
                buf[pl.ds(ch, SUBLANES), :] = jnp.where(ci == n_ch - 1, wrap, nxt)
            else:
                buf[pl.ds(SUBLANES, ch), :] = x_ref[rows, :].astype(F32)
                prv = x_ref[pl.ds(pl.multiple_of(jnp.maximum(ci * ch - edge, 0), edge), edge), :].astype(F32)[SUBLANES:]
                last = x_ref[pl.ds(t - edge, edge), :].astype(F32)[SUBLANES:]
                wrap = jnp.where(row >= 1, pltpu.roll(last, 1, 0), 0.0)
                buf[pl.ds(0, SUBLANES), :] = jnp.where(ci == 0, wrap, prv)

        def sweep(h0, store):
            def chunk(k, carry):
                ci = n_ch - 1 - k if back else k
                rows = pl.ds(pl.multiple_of(ci * ch, ch), ch)
                dyv = dy_ref[rows, :].astype(BF16)
                gr[...] = jnp.dot(dyv, cre_ref[...], preferred_element_type=F32)
                gi_[...] = jnp.dot(dyv, cim_ref[...], preferred_element_type=F32)
                if store:
                    neighbours(ci, xr_ref, xbr)
                    neighbours(ci, xi_ref, xbi)

                def group(g, cc):
                    gidx = gpc - 1 - g if back else g
                    r0 = pl.multiple_of(gidx * SUBLANES, SUBLANES)
                    hr, hi = cc[0], cc[1]
                    nr = a_r * hr - a_i * hi + gr[pl.ds(r0, SUBLANES), :]
                    ni = a_r * hi + a_i * hr + gi_[pl.ds(r0, SUBLANES), :]
                    if not store:
                        return nr, ni
                    lbr[pl.ds(r0, SUBLANES), :] = nr
                    lbi[pl.ds(r0, SUBLANES), :] = ni
                    x0 = pl.multiple_of(r0 + SUBLANES, SUBLANES) if reverse else r0
                    xpr, xpi = xbr[pl.ds(x0, SUBLANES), :], xbi[pl.ds(x0, SUBLANES), :]
                    return nr, ni, cc[2] + nr * xpr + ni * xpi, cc[3] + ni * xpr - nr * xpi

                carry = lax.fori_loop(0, gpc, group, carry)
                if store:
                    lr_ref[rows, :] = lbr[...].astype(BF16)
                    li_ref[rows, :] = lbi[...].astype(BF16)
                return carry

            return lax.fori_loop(0, n_ch, chunk, h0)

        zero = jnp.zeros((SUBLANES, w), F32)
        er, ei = sweep((zero, zero), False)
        pr, pi = _cpow(ar_ref[...], -ai_ref[...], steps)
        cr, ci0 = _segment_carry(er, ei, pr, pi, back)
        _, _, dar, dai = sweep((cr, ci0, zero, zero), True)
        dar_ref[...] = jnp.sum(dar, axis=0, keepdims=True)
        dai_ref[...] = jnp.sum(dai, axis=0, keepdims=True)

    col = lambda i: (0, i)
    return pl.pallas_call(
        body, name=name, grid=(s // w,),
        in_specs=[pl.BlockSpec((t, c), lambda i: (0, 0)), pl.BlockSpec((c, w), col), pl.BlockSpec((c, w), col),
                  pl.BlockSpec((t, w), col), pl.BlockSpec((t, w), col), pl.BlockSpec((1, w), col), pl.BlockSpec((1, w), col)],
        out_specs=[pl.BlockSpec((t, w), col), pl.BlockSpec((t, w), col), pl.BlockSpec((1, w), col), pl.BlockSpec((1, w), col)],
        out_shape=[_out(t, s, BF16), _out(t, s, BF16), _out(1, s, F32), _out(1, s, F32)],
        scratch_shapes=[pltpu.VMEM((ch, w), F32)] * 4 + [pltpu.VMEM((ch + SUBLANES, w), F32)] * 2,
        compiler_params=_params(("parallel",)),
    )(dyp, c_re, c_im, xr, xi, ar, ai)


def _ssm_finish(y0, y1, sp, skip):
    t, c = sp.shape
    steps = t // SUBLANES
    w = _pick(c, LANES)

    def body(y0_ref, y1_ref, sp_ref, d_ref, y_ref, ys_ref):
        rows = pl.ds(pl.program_id(1), steps, stride=SUBLANES)
        y = y0_ref[rows, :] + y1_ref[rows, :] + sp_ref[rows, :] * d_ref[...]
        y_ref[...] = y
        ys_ref[...] = jax.nn.gelu(y).astype(BF16)

    whole = pl.BlockSpec((t, w), lambda j, k: (0, j))
    seg = pl.BlockSpec((steps, w), lambda j, k: (k, j))
    return pl.pallas_call(
        body, name="ssm_finish", grid=(c // w, SUBLANES),
        in_specs=[whole, whole, whole, pl.BlockSpec((1, w), lambda j, k: (0, j))], out_specs=[seg, seg],
        out_shape=[_out(t, c, F32), _out(t, c, BF16)], compiler_params=_params(("parallel", "arbitrary")),
    )(y0, y1, sp, skip)


def _to_segments(a):
    t, c = a.shape
    return a.reshape(SUBLANES, t // SUBLANES, c).transpose(1, 0, 2).reshape(t, c)


def _from_segments(a):
    t, c = a.shape
    return a.reshape(t // SUBLANES, SUBLANES, c).transpose(1, 0, 2).reshape(t, c)


def _colsum_prod(name, a, b, b_coff=0):
    t, n = a.shape
    tm = _pick(t, 512, SUBLANES)

    def body(a_ref, b_ref, o_ref):
        @pl.when(pl.program_id(0) == 0)
        def _():
            o_ref[...] = jnp.zeros_like(o_ref)

        o_ref[...] += jnp.sum(a_ref[...].astype(F32) * b_ref[...].astype(F32), axis=0, keepdims=True)

    return pl.pallas_call(
        body, name=name, grid=(t // tm,),
        in_specs=[pl.BlockSpec((tm, n), lambda i: (i, 0)), pl.BlockSpec((tm, n), lambda i: (i, b_coff))],
        out_specs=pl.BlockSpec((1, n), lambda i: (0, 0)), out_shape=_out(1, n, F32),
        compiler_params=_params(("arbitrary",)),
    )(a, b)


def _ssm_maps(arrs, signs):
    n2, hh, p = arrs[0].shape
    g = n2 // 2

    def body(*refs):
        ins, outs = refs[:len(arrs)], refs[len(arrs):]
        for a, (a_ref, sign) in enumerate(zip(ins, signs)):
            for d in range(2):
                o_ref = outs[2 * a + d]
                o_ref[...] = jnp.zeros_like(o_ref)
                for k in range(g):
                    o_ref[pl.ds(k * hh, hh), pl.ds(k * p, p)] = (sign * a_ref[d * g + k]).astype(BF16)

    outs = pl.pallas_call(body, name="ssm_maps", out_shape=[_out(g * hh, g * p, BF16)] * (2 * len(arrs)))(*arrs)
    return [outs[2 * a:2 * a + 2] for a in range(len(arrs))]


def _softmax(qh, kh, scale):
    s = lax.dot_general(qh, kh, _DIMS["nt"], preferred_element_type=F32) * scale
    e = jnp.exp(s - jnp.max(s, axis=-1, keepdims=True))
    return e / jnp.sum(e, axis=-1, keepdims=True)


def _attn_fwd(q, kv):
    t, d = q.shape
    mm_ = kv.shape[0]
    hd = d // N_XHEADS
    scale = 1.0 / math.sqrt(hd)
    tm = _pick(t, 1024, SUBLANES)

    def body(q_ref, kv_ref, o_ref):
        for h in range(N_XHEADS):
            sl = pl.ds(h * hd, hd)
            p = _softmax(q_ref[:, sl], kv_ref[:, sl], scale)
            o_ref[:, sl] = jnp.dot(p.astype(BF16), kv_ref[:, pl.ds(d + h * hd, hd)],
                                   preferred_element_type=F32).astype(BF16)

    return pl.pallas_call(
        body, name="attn_fwd", grid=(t // tm,),
        in_specs=[pl.BlockSpec((tm, d), lambda i: (i, 0)), pl.BlockSpec((mm_, 2 * d), lambda i: (0, 0))],
        out_specs=pl.BlockSpec((tm, d), lambda i: (i, 0)), out_shape=_out(t, d, BF16),
        compiler_params=_params(("parallel",)),
    )(q, kv)


def _attn_bwd(q, kv, do):
    t, d = q.shape
    mm_ = kv.shape[0]
    hd = d // N_XHEADS
    scale = 1.0 / math.sqrt(hd)
    tm = _pick(t, 1024, SUBLANES)

    def body(q_ref, kv_ref, do_ref, dq_ref, dkv_ref):
        @pl.when(pl.program_id(0) == 0)
        def _():
            dkv_ref[...] = jnp.zeros_like(dkv_ref)

        for h in range(N_XHEADS):
            sl = pl.ds(h * hd, hd)
            vsl = pl.ds(d + h * hd, hd)
            qh, kh, doh = q_ref[:, sl], kv_ref[:, sl], do_ref[:, sl]
            p = _softmax(qh, kh, scale)
            dp = lax.dot_general(doh, kv_ref[:, vsl], _DIMS["nt"], preferred_element_type=F32)
            dkv_ref[:, vsl] += lax.dot_general(p.astype(BF16), doh, _DIMS["tn"], preferred_element_type=F32)
            ds = (p * (dp - jnp.sum(dp * p, axis=-1, keepdims=True)) * scale).astype(BF16)
            dq_ref[:, sl] = jnp.dot(ds, kh, preferred_element_type=F32).astype(BF16)
            dkv_ref[:, sl] += lax.dot_general(ds, qh, _DIMS["tn"], preferred_element_type=F32)

    row = pl.BlockSpec((tm, d), lambda i: (i, 0))
    full = pl.BlockSpec((mm_, 2 * d), lambda i: (0, 0))
    return pl.pallas_call(
        body, name="attn_bwd", grid=(t // tm,), in_specs=[row, full, row], out_specs=[row, full],
        out_shape=[_out(t, d, BF16), _out(mm_, 2 * d, F32)], compiler_params=_params(("arbitrary",)),
    )(q, kv, do)


def _ew(name, fn, ins, outs, rows_pref=256):
    r, c = ins[0].shape
    tr = _pick(r, rows_pref, SUBLANES)
    ni = len(ins)

    def body(*refs):
        res = fn(*[x[...] for x in refs[:ni]])
        for o_ref, v in zip(refs[ni:], res):
            o_ref[...] = v.astype(o_ref.dtype)

    blk = pl.BlockSpec((tr, c), lambda i: (i, 0))
    return pl.pallas_call(
        body, name=name, grid=(r // tr,), in_specs=[blk] * ni, out_specs=[blk] * len(outs),
        out_shape=[_out(r, c, dt) for dt in outs], compiler_params=_params(("parallel",)),
    )(*ins)


def _sum_slots(name, a, dtype):
    s, r, c = a.shape
    tr = _pick(r, 256, SUBLANES)

    def body(a_ref, o_ref):
        acc = a_ref[0].astype(F32)
        for k in range(1, s):
            acc = acc + a_ref[k].astype(F32)
        o_ref[...] = acc.astype(o_ref.dtype)

    return pl.pallas_call(
        body, name=name, grid=(r // tr,), in_specs=[pl.BlockSpec((s, tr, c), lambda i: (0, i, 0))],
        out_specs=pl.BlockSpec((tr, c), lambda i: (i, 0)), out_shape=_out(r, c, dtype),
        compiler_params=_params(("parallel",)),
    )(a)


def _adamw_step(wv, gv, mv, vv):
    bc1 = 1.0 - ADAM_B1 ** ADAM_STEP
    bc2 = 1.0 - ADAM_B2 ** ADAM_STEP
    m2 = ADAM_B1 * mv + (1.0 - ADAM_B1) * gv
    v2 = ADAM_B2 * vv + (1.0 - ADAM_B2) * (gv * gv)
    delta = -ADAM_LR * ((m2 / bc1) / (jnp.sqrt(v2 / bc2) + ADAM_EPS) + ADAM_WD * wv)
    return delta, m2, v2


def _adamw_group(name, items, transposed):
    k, r = items[0][0].shape
    if transposed and r % LANES != 0:
        rows = _adamw_group(name, [(w.T, g, m.T, v.T) for w, g, m, v in items], False)
        return [[a.T for a in item] for item in rows]
    tk = _pick(k, max(SUBLANES, ADAMW_STEP_WORDS // (r * len(items))), SUBLANES)
    n_out = 4 if transposed else 3

    def body(*refs):
        ins, outs = refs[:4 * len(items)], refs[4 * len(items):]
        for i in range(len(items)):
            wv, gv, mv, vv = (a[...] for a in ins[4 * i:4 * i + 4])
            if transposed:
                gv = gv.T
            res = _adamw_step(wv, gv, mv, vv) + ((gv,) if transposed else ())
            for o_ref, val in zip(outs[n_out * i:n_out * (i + 1)], res):
                o_ref[...] = val

    blk = pl.BlockSpec((tk, r), lambda j: (j, 0))
    g_blk = pl.BlockSpec((r, tk), lambda j: (0, j)) if transposed else blk
    res = pl.pallas_call(
        body, name=name, grid=(k // tk,), in_specs=[blk, g_blk, blk, blk] * len(items),
        out_specs=[blk] * (n_out * len(items)), out_shape=[pltpu.HBM((k, r), F32)] * (n_out * len(items)),
        compiler_params=_params(("parallel",)),
    )(*[pltpu.with_memory_space_constraint(a, pltpu.HBM) for item in items for a in item])
    return [list(res[n_out * i:n_out * (i + 1)]) + ([] if transposed else [items[i][1]]) for i in range(len(items))]


def _allgather(name, arrs):
    n = len(arrs)

    def body(*refs):
        ins, outs = refs[:n], refs[n:2 * n]
        send_sems, recv_sems, local_sems = refs[2 * n:]
        x, y, c = lax.axis_index("x"), lax.axis_index("y"), lax.axis_index("c")
        me, sibling = (x, y, c), (x, y, 1 - c)
        chips = [(1 - x, y), (x, 1 - y), (1 - x, 1 - y)]

        def rows(a, px, py, pc):
            r = ins[a].shape[0]
            return outs[a].at[pl.ds((4 * px + 2 * py + pc) * r, r), :]

        def copy(a, k, block, to, src=None):
            return pltpu.make_async_remote_copy(
                src_ref=rows(a, *block) if src is None else src, dst_ref=rows(a, *block),
                send_sem=send_sems.at[a, k], recv_sem=recv_sems.at[a, k], device_id=to, device_id_type=MESH)

        mine = [pltpu.make_async_copy(ins[a], rows(a, *me), local_sems.at[a]) for a in range(n)]
        for cp in mine:
            cp.start()
        first = []
        for a in range(n):
            first.append(copy(a, 0, me, sibling, src=ins[a]))
            first += [copy(a, 1 + j, me, (*chip, c), src=ins[a]) for j, chip in enumerate(chips)]
        for cp in first:
            cp.start()
        passed = []
        for j, chip in enumerate(chips):
            for a in range(n):
                copy(a, 1 + j, (*chip, c), me).wait_recv()
                cp = copy(a, 4 + j, (*chip, c), sibling)
                cp.start()
                passed.append(cp)
        for a in range(n):
            copy(a, 0, sibling, me).wait_recv()
            for j, chip in enumerate(chips):
                copy(a, 4 + j, (*chip, 1 - c), me).wait_recv()
        for cp in first + passed:
            cp.wait_send()
        for cp in mine:
            cp.wait()

    return pl.pallas_call(
        body, name=name, in_specs=[ANY] * n, out_specs=[ANY] * n,
        out_shape=[_out(N_DEV * a.shape[0], a.shape[1], a.dtype) for a in arrs],
        scratch_shapes=[pltpu.SemaphoreType.DMA((n, 7)), pltpu.SemaphoreType.DMA((n, 7)), pltpu.SemaphoreType.DMA((n,))],
    )(*arrs)


def _cores_start(name, blocks):
    n = len(blocks)
    c = blocks[0].shape[2]
    r = sum(b.shape[1] for b in blocks)

    def build(src_refs, land_refs, send_sems, recv_sems):
        x, y, cc = lax.axis_index("x"), lax.axis_index("y"), lax.axis_index("c")
        remote, off = [], 0
        for a, src in enumerate(src_refs):
            rows = pl.ds(off, src.shape[1])
            off += src.shape[1]
            for q in range(4):
                remote.append(pltpu.make_async_remote_copy(
                    src_ref=src.at[2 * q + (1 - cc)], dst_ref=land_refs[0].at[q, rows], send_sem=send_sems.at[4 * a + q],
                    recv_sem=recv_sems.at[4 * a + q], device_id=(x, y, 1 - cc), device_id_type=MESH))
        return remote, []

    return _split_start(name, [(blocks, [jax.ShapeDtypeStruct((4, r, c), blocks[0].dtype)], 4 * n, 0, build)])[0]


def _peer(k, x, y, c):
    return (1 - x if k & 4 else x, 1 - y if k & 2 else y, 1 - c if k & 1 else c)


def _split_start(name, groups, after=None):
    pins = [] if after is None else [after]
    bufs, sem_shapes, spans = [], [], []
    for srcs, land_shapes, n_remote, n_local, _ in groups:
        sems = [pltpu.SemaphoreType.DMA((n_remote,)), pltpu.SemaphoreType.DMA((n_remote,))]
        sems += [pltpu.SemaphoreType.DMA((n_local,))] if n_local else []
        spans.append((len(bufs), len(srcs), len(land_shapes), len(sem_shapes), len(sems)))
        bufs += [pltpu.with_memory_space_constraint(a, pltpu.HBM) for a in srcs]
        bufs += [pltpu.with_memory_space_constraint(lax.empty(s.shape, s.dtype), pltpu.HBM) for s in land_shapes]
        sem_shapes += sems
    n_buf, n_sem = len(bufs), len(sem_shapes)

    def body(*refs):
        buf_refs, sem_refs, token = refs[:n_buf], refs[n_buf + len(pins):n_buf + len(pins) + n_sem], refs[-1]
        for (b0, ns, nl, s0, k), group in zip(spans, groups):
            remote, local = group[4](buf_refs[b0:b0 + ns], buf_refs[b0 + ns:b0 + ns + nl], *sem_refs[s0:s0 + k])
            for cp in local + remote:
                cp.start()
        token[...] = jnp.zeros_like(token)

    outs = pl.pallas_call(
        body, name=name,
        out_shape=sem_shapes + [pltpu.HBM(b.shape, b.dtype) for b in bufs] + [jax.ShapeDtypeStruct((SUBLANES, LANES), F32)],
        in_specs=[HBM] * n_buf + [ANY] * len(pins),
        out_specs=[SEM] * n_sem + [HBM] * n_buf + [pl.BlockSpec(memory_space=pltpu.VMEM)],
        input_output_aliases={i: n_sem + i for i in range(n_buf)},
        compiler_params=pltpu.CompilerParams(has_side_effects=SIDE_EFFECT),
    )(*bufs, *pins)
    return [dict(sems=list(outs[s0:s0 + k]), bufs=list(outs[n_sem + b0:n_sem + b0 + ns + nl]), token=outs[-1],
                 build=group[4], ns=ns) for (b0, ns, nl, s0, k), group in zip(spans, groups)]


def _split_wait(name, started, after):
    ns, n_buf, n_sem = started["ns"], len(started["bufs"]), len(started["sems"])

    def body(*refs):
        src_refs, land_refs = refs[:ns], refs[ns:n_buf]
        sems = refs[n_buf:n_buf + n_sem]
        remote, local = started["build"](src_refs, land_refs, *sems)
        for cp in local:
            cp.wait()
        for cp in remote:
            cp.wait_send()
            cp.wait_recv()

    outs = pl.pallas_call(
        body, name=name, out_shape=[pltpu.HBM(b.shape, b.dtype) for b in started["bufs"]],
        in_specs=[HBM] * n_buf + [SEM] * n_sem + [ANY], out_specs=[HBM] * n_buf,
        input_output_aliases={i: i for i in range(n_buf)},
        compiler_params=pltpu.CompilerParams(has_side_effects=SIDE_EFFECT),
    )(*started["bufs"], *started["sems"], after)
    return list(outs[:ns]), list(outs[ns:])


def _gather_group(shards):
    m = len(shards)

    def build(src_refs, land_refs, send_sems, recv_sems, local_sems):
        x, y, c = lax.axis_index("x"), lax.axis_index("y"), lax.axis_index("c")
        remote, local = [], []
        for j in range(m):
            r = src_refs[j].shape[0]
            dst = land_refs[j].at[pl.ds((4 * x + 2 * y + c) * r, r), :]
            local.append(pltpu.make_async_copy(src_refs[j], dst, local_sems.at[j]))
            for k in range(1, N_DEV):
                remote.append(pltpu.make_async_remote_copy(
                    src_ref=src_refs[j], dst_ref=dst, send_sem=send_sems.at[7 * j + k - 1],
                    recv_sem=recv_sems.at[7 * j + k - 1], device_id=_peer(k, x, y, c), device_id_type=MESH))
        return remote, local

    lands = [jax.ShapeDtypeStruct((N_DEV * a.shape[0], a.shape[1]), a.dtype) for a in shards]
    return shards, lands, 7 * m, m, build


def _slots_start(name, a):
    def build(src_refs, land_refs, send_sems, recv_sems, local_sems):
        x, y, c = lax.axis_index("x"), lax.axis_index("y"), lax.axis_index("c")
        dst = land_refs[0].at[4 * x + 2 * y + c]
        local = [pltpu.make_async_copy(src_refs[0], dst, local_sems.at[0])]
        remote = [pltpu.make_async_remote_copy(
            src_ref=src_refs[0], dst_ref=dst, send_sem=send_sems.at[k - 1], recv_sem=recv_sems.at[k - 1],
            device_id=_peer(k, x, y, c), device_id_type=MESH) for k in range(1, N_DEV)]
        return remote, local

    return _split_start(name, [([a], [jax.ShapeDtypeStruct((N_DEV,) + a.shape, a.dtype)], 7, 1, build)])[0]


def _chips_start(name, p):
    _, r, c = p.shape
    nck = r // GRAD_ROW_TILE

    def build(src_refs, land_refs, send_sems, recv_sems):
        x, y, cc = lax.axis_index("x"), lax.axis_index("y"), lax.axis_index("c")
        remote = []
        for k in range(1, 4):
            px = 1 - x if k >> 1 else x
            py = 1 - y if k & 1 else y
            for j in range(nck):
                rows = pl.ds(j * GRAD_ROW_TILE, GRAD_ROW_TILE)
                remote.append(pltpu.make_async_remote_copy(
                    src_ref=src_refs[0].at[2 * px + py, rows], dst_ref=land_refs[0].at[k - 1, rows],
                    send_sem=send_sems.at[(k - 1) * nck + j], recv_sem=recv_sems.at[(k - 1) * nck + j],
                    device_id=(px, py, cc), device_id_type=MESH))
        return remote, []

    return _split_start(name, [([p], [jax.ShapeDtypeStruct((3, r, c), p.dtype)], 3 * nck, 0, build)])[0]


def _chip_sum(name, p, recv, chip):
    _, r, c = p.shape
    tr = _pick(r, 5 * GRAD_ROW_TILE, GRAD_ROW_TILE)

    def body(chip_ref, p_ref, r_ref, o_ref):
        acc = p_ref[...].astype(F32)
        for k in range(3):
            acc = acc + r_ref[k].astype(F32)
        o_ref[...] = acc

    return pl.pallas_call(
        body, name=name,
        grid_spec=pltpu.PrefetchScalarGridSpec(
            num_scalar_prefetch=1, grid=(r // tr,),
            in_specs=[pl.BlockSpec((None, tr, c), lambda i, chip_ref: (chip_ref[0], i, 0)),
                      pl.BlockSpec((3, tr, c), lambda i, chip_ref: (0, i, 0))],
            out_specs=pl.BlockSpec((tr, c), lambda i, chip_ref: (i, 0))),
        out_shape=_out(r, c, F32), compiler_params=_params(("parallel",)),
    )(chip, p, recv)


def _local_step(x, mem, tgt, wt, sm, ev=None):
    t, d = x.shape
    n_mem = mem.shape[0]
    d_pool = sm["pool_scale"].shape[1]
    d_ssm = sm["ssm_d"].shape[1]
    _, sg, sp, sh = sm["ssm_b_re"].shape
    n_state = sg * sp
    gb, gs = {}, {}

    def emit(name, **kw):
        return ev(name, **kw) if ev is not None else None

    n1 = _rms_fwd("ffn1_norm", x, sm["ffn1_norm"])
    emit("ffn1_norm_done", marker=n1)
    def ffn1_down(hid):
        emit("ffn1_up_done", marker=hid)
        return wt["ffn1_w_down"]

    h1, ffn1_saved = _ffn_fwd("ffn1", x, n1, wt["ffn1_w_gate"], wt["ffn1_w_up"], ffn1_down)
    emit("ffn1_fwd_done", marker=h1)
    u = _rms_fwd("mix_norm", h1, sm["mix_norm"])
    d_in = wt["w_in"].shape[0]
    tm, tn = _pick(t, 2048), _pick(d_in, 1408)
    proj = _mm1("in_proj", "nt", u, wt["w_in"], t, d_in, tm, tn, F32)
    off_s = d_pool // d_ssm
    off_gp = (d_pool + d_ssm)
    off_gs = off_gp + d

    pool_w_bf = sm["pool_w"].astype(BF16)
    pooled, pm = _pool_fwd(proj, pool_w_bf, sm["pool_scale"])

    by_p = lambda a: jnp.swapaxes(a, -1, -2).reshape(2 * sg, sh, sp)
    disc_args = [sm["ssm_a_re"].reshape(2 * sg, 1, sp), sm["ssm_a_im"].reshape(2 * sg, 1, sp),
                 sm["ssm_log_dt"].reshape(2 * sg, 1, 1), by_p(sm["ssm_b_re"]), by_p(sm["ssm_b_im"])]
    abr, abi, bbr, bbi = _ssm_disc(disc_args)
    abr2, abi2 = abr.reshape(2, n_state), abi.reshape(2, n_state)
    b_re, b_im, c_re, c_im = _ssm_maps(
        [bbr, bbi, sm["ssm_c_re"].reshape(2 * sg, sh, sp), sm["ssm_c_im"].reshape(2 * sg, sh, sp)], [1.0, 1.0, 1.0, -1.0])
    sp32 = _to_segments(proj[:, d_pool:d_pool + d_ssm])
    xs, y_parts = [], []
    for dr in range(2):
        xr, xi, y_part = _ssm_fwd(f"ssm_fwd{dr}", sp32, b_re[dr], b_im[dr], c_re[dr], c_im[dr], abr2[dr:dr + 1],
                                  abi2[dr:dr + 1], reverse=(dr == 1))
        xs.append((xr, xi))
        y_parts.append(y_part)
    y, ys = _ssm_finish(y_parts[0], y_parts[1], sp32, sm["ssm_d"])
    tmy = _pick(t, 1024)
    emit("mix_in_done", marker=ys)

    tmm, tnm, tnx = _pick(t, 2048), _pick(d, 256), _pick(d, 512)
    gp_spec = _tile(tmm, tnm, off_gp // tnm)
    gs_spec = _tile(tmm, tnm, off_gs // tnm)

    def merge_epi(accs, gpv, gsv):
        z_pool, val, gate = accs
        return (jax.nn.sigmoid(gpv) * z_pool + jax.nn.sigmoid(gsv) * (val * jax.nn.sigmoid(gate)),)

    merged = _mm("mix_merge", "nt", [pm, ys], [wt["w_pool_proj"], wt["w_glu_val"], wt["w_glu_gate"]],
                 [[(0, 0)], [(1, 1)], [(1, 2)]], t, d, tmm, tnm, [(proj, gp_spec), (proj, gs_spec)], merge_epi,
                 [(_out(t, d, BF16), None)])[0]
    res_epi = lambda accs, hin: (hin + accs[0],)
    h2 = _mm("mix_out", "nn", [merged], [wt["w_mix_out"]], [[(0, 0)]], t, d, tmm, tnx, [(h1, _tile(tmm, tnx))],
             res_epi, [(_out(t, d, F32), None)])[0]

    un = _rms_fwd("xattn_norm", h2, sm["xattn_norm"])
    mn = _rms_fwd("mem_norm", mem, sm["mem_norm"])
    emit("mix_done", marker=un)
    q = _mm1("xattn_q", "nn", un, wt["w_q"], t, d, tmm, tnx, BF16)
    kv = _mm1("xattn_kv", "nt", mn, wt["w_kv"], n_mem, 2 * d, n_mem, _pick(2 * d, 512), BF16)
    o = _attn_fwd(q, kv)
    h3 = _mm("xattn_out", "nn", [o], [wt["w_xo"]], [[(0, 0)]], t, d, tmm, tnx, [(h2, _tile(tmm, tnx))],
             res_epi, [(_out(t, d, F32), None)])[0]

    n2 = _rms_fwd("ffn2_norm", h3, sm["ffn2_norm"])
    emit("xattn_done", marker=n2)
    h4, ffn2_saved = _ffn_fwd("ffn2", h3, n2, wt["ffn2_w_gate"], wt["ffn2_w_up"], wt["ffn2_w_down"])

    dh4, dh4_bf, gs["final_norm"], loss = _loss_head(h4, sm["final_norm"], tgt)
    dh3, dh3_bf, gs["ffn2_norm"], gb["ffn2_w_gate"], gb["ffn2_w_up"], gb["ffn2_w_down"] = _ffn_bwd(
        "ffn2", h3, sm["ffn2_norm"], wt["ffn2_w_gate"], wt["ffn2_w_up"], wt["ffn2_w_down"], ffn2_saved, dh4, dh4_bf)

    tw = _pick(d, 1024)
    do = _mm1("xattn_do", "nt", dh3_bf, wt["w_xo"], t, d, tmm, tnx, BF16)
    gb["w_xo"] = _mm1("xattn_dwxo", "tn", o, dh3_bf, d, d, tw, tnx, BF16)
    dq, dkv = _attn_bwd(q, kv, do)
    gb["w_q"] = _mm1("xattn_dwq", "tn", un, dq, d, d, tw, tnx, BF16)
    dun = _mm1("xattn_dun", "nt", dq, wt["w_q"], t, d, tmm, tnx, F32)
    dh2, dh2_bf, gs["xattn_norm"] = _rms_bwd("xattn_norm_bwd", h2, sm["xattn_norm"], dun, dh3)
    gb["w_kv"] = _mm1("xattn_dwkv", "tn", dkv, mn, 2 * d, d, _pick(2 * d, 512), d, BF16)
    dmn = _mm1("xattn_dmn", "nn", dkv, wt["w_kv"], n_mem, d, n_mem, tnx, F32)
    gs["mem_norm"] = _rms_bwd("mem_norm_bwd", mem, sm["mem_norm"], dmn)

    gb["w_mix_out"] = _mm1("mix_dwout", "tn", merged, dh2_bf, d, d, tw, tnx, BF16)

    def merge_bwd_epi(accs, gpv, gsv):
        dmerged, z_pool, val, gate = accs
        sp_, ss_, sg_ = jax.nn.sigmoid(gpv), jax.nn.sigmoid(gsv), jax.nn.sigmoid(gate)
        glu = val * sg_
        dz_pool = dmerged * sp_
        dg_pool = dmerged * z_pool * (sp_ * (1.0 - sp_))
        dz_ssm = dmerged * ss_
        dg_ssm = dmerged * glu * (ss_ * (1.0 - ss_))
        dval = dz_ssm * sg_
        dgate = dz_ssm * glu * (1.0 - sg_)
        return dz_pool, dg_pool, dg_ssm, dval, dgate

    dz_pool, dg_pool, dg_ssm, dval, dgate = _mm(
        "mix_merge_bwd", "nt", [dh2_bf, pm, ys], [wt["w_mix_out"], wt["w_pool_proj"], wt["w_glu_val"], wt["w_glu_gate"]],
        [[(0, 0)], [(1, 1)], [(2, 2)], [(2, 3)]], t, d, tmm, tnm, [(proj, gp_spec), (proj, gs_spec)], merge_bwd_epi,
        [(_out(t, d, BF16), None)] * 5)
    gb["w_pool_proj"] = _mm1("pool_dwproj", "tn", dz_pool, pm, d, d_pool, tw, d_pool, BF16)
    gb["w_glu_val"] = _mm1("glu_dwval", "tn", dval, ys, d, d_ssm, tw, d_ssm, BF16)
    gb["w_glu_gate"] = _mm1("glu_dwgate", "tn", dgate, ys, d, d_ssm, tw, d_ssm, BF16)

    def gelu_bwd_epi(accs, yv):
        _, vjp = jax.vjp(jax.nn.gelu, yv)
        return (vjp(accs[0])[0],)

    dy = _mm("glu_dy", "nn", [dval, dgate], [wt["w_glu_val"], wt["w_glu_gate"]], [[(0, 0), (1, 1)]], t, d_ssm, tmy, d_ssm,
             [(y, _tile(tmy, d_ssm))], gelu_bwd_epi, [(_out(t, d_ssm, F32), None)])[0]
    gs["ssm_d"] = _colsum_prod("ssm_dd", dy, proj, b_coff=off_s)
    dyp = _to_segments(dy)
    d_abr, d_abi, d_bbr, d_bbi, d_cre, d_cim, lams = [], [], [], [], [], [], []
    ts = _pick(n_state, 512)

    def fold_diag(accs):
        first = pl.program_id(1) * (ts // sp)
        row_group = lax.broadcasted_iota(jnp.int32, (d_ssm, sp), 0) // sh
        folded = []
        for acc in accs:
            out = jnp.zeros((d_ssm, sp), F32)
            for k in range(ts // sp):
                out = out + jnp.where(row_group == first + k, acc[:, sp * k:sp * (k + 1)], 0.0)
            folded.append(out)
        return tuple(folded)

    for dr in range(2):
        lr, li, dar, dai = _ssm_bwd(f"ssm_bwd{dr}", dyp, c_re[dr], c_im[dr], xs[dr][0], xs[dr][1], abr2[dr:dr + 1],
                                    abi2[dr:dr + 1], reverse=(dr == 1))
        d_abr.append(dar)
        d_abi.append(dai)
        lams += [lr, li]
        maps = _mm(f"ssm_dmaps{dr}", "tn", [sp32, dyp], [lr, li, xs[dr][0], xs[dr][1]],
                   [[(0, 0)], [(0, 1)], [(1, 2)], [(1, 3)]], d_ssm, n_state, d_ssm, ts, [], fold_diag,
                   [(_out(n_state // ts * d_ssm, sp, F32), pl.BlockSpec((d_ssm, sp), lambda i, j: (j, 0)))] * 4)
        for acc, m in zip((d_bbr, d_bbi, d_cre, d_cim), maps):
            acc.append(jnp.sum(m.reshape(n_state // ts, sg, sh, sp), axis=0))
    ds = _from_segments(_mm(
        "ssm_ds", "nt", lams, [b_re[0], b_im[0], b_re[1], b_im[1]], [[(k, k) for k in range(4)]], t, d_ssm, tmy,
        d_ssm, [(dyp, _tile(tmy, d_ssm)), (sm["ssm_d"], _rowvec(d_ssm))],
        lambda accs, dyv, dv: (dyv * dv + accs[0],), [(_out(t, d_ssm, BF16), None)])[0])
    cots = [jnp.concatenate(d_abr, axis=0).reshape(2 * sg, 1, sp), jnp.concatenate(d_abi, axis=0).reshape(2 * sg, 1, sp),
            jnp.concatenate(d_bbr, axis=0), jnp.concatenate(d_bbi, axis=0)]
    d_are, d_aim, d_ldt, d_bre, d_bim = _ssm_disc_bwd(disc_args, cots)
    gs["ssm_a_re"] = d_are.reshape(2, sg, sp)
    gs["ssm_a_im"] = d_aim.reshape(2, sg, sp)
    gs["ssm_log_dt"] = d_ldt.reshape(2, sg)
    from_p = lambda a: jnp.swapaxes(a.reshape(2, sg, sh, sp), -1, -2)
    gs["ssm_b_re"], gs["ssm_b_im"] = from_p(d_bre), from_p(d_bim)
    gs["ssm_c_re"] = jnp.stack(d_cre, axis=0)
    gs["ssm_c_im"] = -jnp.stack(d_cim, axis=0)

    dpm = _mm1("pool_dpm", "nn", dz_pool, wt["w_pool_proj"], t, d_pool, tmm, _pick(d_pool, 256), F32)
    dp, gs["pool_w"], gs["pool_scale"] = _pool_bwd(pooled, dpm, pool_w_bf, sm["pool_scale"])

    w_in = wt["w_in"]
    parts = [(dp, 0, d_pool), (ds, d_pool, d_ssm), (dg_pool, off_gp, d), (dg_ssm, off_gs, d)]
    w_in_parts = [w_in[o0:o0 + width] for _, o0, width in parts]
    gb["w_in"] = jnp.concatenate(
        [_mm1(f"in_proj_dw{k}", "tn", p_[0], u, p_[2], d, _pick(p_[2], 1024), tnx, BF16) for k, p_ in enumerate(parts)], axis=0)
    pin = emit("grads_main", gb=gb)
    du = _mm("in_proj_du", "nn", [p_[0] for p_ in parts], w_in_parts, [[(k, k) for k in range(4)]], t, d, tmm, tnx, [],
             lambda accs: (accs[0],), [(_out(t, d, F32), None)], after=pin)[0]
    dh1, dh1_bf, gs["mix_norm"] = _rms_bwd("mix_norm_bwd", h1, sm["mix_norm"], du, dh2)
    pin = emit("small_early", gs=gs, loss=loss)

    def ffn1_weights_done(d_wg, d_wu, d_wd):
        gb["ffn1_w_gate"], gb["ffn1_w_up"], gb["ffn1_w_down"] = d_wg, d_wu, d_wd
        return emit("grads_ffn1", gb=gb)

    dx, _, gs["ffn1_norm"], _, _, _ = _ffn_bwd(
        "ffn1", x, sm["ffn1_norm"], wt["ffn1_w_gate"], wt["ffn1_w_up"], wt["ffn1_w_down"], ffn1_saved, dh1, dh1_bf,
        weights_done=ffn1_weights_done, after=pin)
    return loss, dx, gb, gs


WEIGHTS = ["ffn1_norm", "ffn1_w_gate", "ffn1_w_up", "ffn1_w_down", "mix_norm", "w_in", "pool_w", "pool_scale",
           "w_pool_proj", "ssm_a_re", "ssm_a_im", "ssm_log_dt", "ssm_b_re", "ssm_b_im", "ssm_c_re", "ssm_c_im", "ssm_d",
           "w_glu_val", "w_glu_gate", "w_mix_out", "xattn_norm", "mem_norm", "w_q", "w_kv", "w_xo", "ffn2_norm",
           "ffn2_w_gate", "ffn2_w_up", "ffn2_w_down", "final_norm"]
COL_SHARDED = ["ffn1_w_gate", "ffn1_w_up", "w_in", "w_pool_proj", "w_glu_val", "w_glu_gate", "w_kv", "ffn2_w_gate",
               "ffn2_w_up"]
ROW_SHARDED = ["ffn1_w_down", "w_mix_out", "w_q", "w_xo", "ffn2_w_down"]
BIG = [n for n in WEIGHTS if n in COL_SHARDED or n in ROW_SHARDED]
SMALL = [n for n in WEIGHTS if n not in BIG]
FFN1_BIG = ["ffn1_w_gate", "ffn1_w_up", "ffn1_w_down"]
MAIN_BIG = [n for n in BIG if n not in FFN1_BIG]
GATHER_PLAN = [("ffn1_up_done", ["ffn1_w_down"]), ("ffn1_fwd_done", ["w_in"]),
               ("mix_in_done", ["w_pool_proj", "w_glu_val", "w_glu_gate", "w_mix_out"]),
               ("mix_done", ["w_q", "w_kv", "w_xo"]), ("xattn_done", ["ffn2_w_gate", "ffn2_w_up", "ffn2_w_down"])]
MINOR_SWAPPED = ["ssm_b_re", "ssm_b_im"]
LATE_SMALL = "ffn1_norm"
EARLY_SMALL = [n for n in SMALL if n != LATE_SMALL]
PACK_ROWS = SUBLANES * LANES
GRAD_ROW_TILE = 256
ADAMW_STEP_WORDS = 1 << 19


def _to_rows(name, w):
    return w.T if name in COL_SHARDED else w


def _pack_small(vals):
    flat = []
    for v in vals:
        f = v.reshape(-1)
        flat.append(jnp.pad(f, (0, (-f.shape[0]) % PACK_ROWS)))
    total = sum(f.shape[0] for f in flat)
    flat.append(jnp.zeros(((-total) % (GRAD_ROW_TILE * LANES),), F32))
    return jnp.concatenate(flat).reshape(-1, LANES)


def _unpack_small(packed, shapes):
    out, row = [], 0
    for shp in shapes:
        size = math.prod(shp)
        rows = -(-size // PACK_ROWS) * SUBLANES
        out.append(packed[row:row + rows].reshape(-1)[:size].reshape(shp))
        row += rows
    return out


def kernel(x, mem, ffn1_norm, ffn1_w_gate, ffn1_w_up, ffn1_w_down, mix_norm, w_in, pool_w, pool_scale, w_pool_proj, ssm_a_re, ssm_a_im, ssm_log_dt, ssm_b_re, ssm_b_im, ssm_c_re, ssm_c_im, ssm_d, w_glu_val, w_glu_gate, w_mix_out, xattn_norm, mem_norm, w_q, w_kv, w_xo, ffn2_norm, ffn2_w_gate, ffn2_w_up, ffn2_w_down, final_norm, loss_target, m_ffn1_norm, m_ffn1_w_gate, m_ffn1_w_up, m_ffn1_w_down, m_mix_norm, m_w_in, m_pool_w, m_pool_scale, m_w_pool_proj, m_ssm_a_re, m_ssm_a_im, m_ssm_log_dt, m_ssm_b_re, m_ssm_b_im, m_ssm_c_re, m_ssm_c_im, m_ssm_d, m_w_glu_val, m_w_glu_gate, m_w_mix_out, m_xattn_norm, m_mem_norm, m_w_q, m_w_kv, m_w_xo, m_ffn2_norm, m_ffn2_w_gate, m_ffn2_w_up, m_ffn2_w_down, m_final_norm, v_ffn1_norm, v_ffn1_w_gate, v_ffn1_w_up, v_ffn1_w_down, v_mix_norm, v_w_in, v_pool_w, v_pool_scale, v_w_pool_proj, v_ssm_a_re, v_ssm_a_im, v_ssm_log_dt, v_ssm_b_re, v_ssm_b_im, v_ssm_c_re, v_ssm_c_im, v_ssm_d, v_w_glu_val, v_w_glu_gate, v_w_mix_out, v_xattn_norm, v_mem_norm, v_w_q, v_w_kv, v_w_xo, v_ffn2_norm, v_ffn2_w_gate, v_ffn2_w_up, v_ffn2_w_down, v_final_norm):
    given = dict(locals())
    wts = {n: given[n] for n in WEIGHTS}
    moms = {n: (given["m_" + n], given["v_" + n]) for n in WEIGHTS}
    x2, mem2, tgt2 = x[0], mem[0], loss_target[0]
    d = x2.shape[1]
    chip = (2 * lax.axis_index("x") + lax.axis_index("y")).astype(jnp.int32).reshape(1)

    def full_form(n, f):
        shard = wts[n][0].shape
        return f.reshape(N_DEV * shard[1], shard[0]) if n in COL_SHARDED else f.reshape(N_DEV * shard[0], shard[1])

    shards = {n: _to_rows(n, wts[n][0]).astype(BF16) for n in BIG}
    first = FFN1_BIG[:2]
    wt = {n: full_form(n, f) for n, f in zip(first, _allgather("weight_allgather_first", [shards[n] for n in first]))}
    started = _split_start("weight_gather_start", [_gather_group([shards[n] for n in names]) for _, names in GATHER_PLAN],
                           after=wt[first[0]])
    gathers = {event: (names, st) for (event, names), st in zip(GATHER_PLAN, started)}
    sm = {n: (wts[n].reshape(1, -1) if wts[n].ndim <= 2 else wts[n][0]) for n in SMALL}
    sm["ffn1_norm"] = sm["ffn1_norm"] + started[0]["token"][0, 0]

    pending = {}

    def reduce_start(tag, names, gb):
        blocks = [gb[n].reshape(N_DEV, -1, d) for n in names]
        pad_rows = (-sum(b.shape[1] for b in blocks)) % GRAD_ROW_TILE
        pad = [jnp.zeros((N_DEV, pad_rows, d), BF16)] if pad_rows else []
        started = _cores_start("grad_exchange_cores_start_" + tag, blocks + pad)
        own = jnp.concatenate([lax.dynamic_index_in_dim(b.reshape(4, 2, b.shape[1], d), lax.axis_index("c"), 1, False)
                               for b in started["bufs"][:len(blocks + pad)]], axis=1)
        _, (recv,) = _split_wait("grad_exchange_cores_wait_" + tag, started, own)
        rows_all = own.shape[1]
        pair = _ew("grad_pair_sum_" + tag, lambda a, b: (a.astype(F32) + b.astype(F32),),
                   [own.reshape(-1, d), recv.reshape(-1, d)], [BF16], rows_pref=5 * GRAD_ROW_TILE)[0]
        pair = pair.reshape(4, rows_all, d)
        pending[tag] = (pair, _chips_start("grad_exchange_chips_start_" + tag, pair), [b.shape[1] for b in blocks])
        return pending[tag][1]["token"]

    def reduce_finish(tag, after):
        _, started, rows = pending[tag]
        (pair,), (recv,) = _split_wait("grad_exchange_chips_wait_" + tag, started, after)
        return _chip_sum("grad_chip_sum_" + tag, pair, recv, chip), rows

    def ev(name, gb=None, gs=None, loss=None, marker=None):
        if name in gathers:
            names, started = gathers[name]
            for n, f in zip(names, _split_wait("weight_gather_wait_" + name, started, marker)[1]):
                wt[n] = full_form(n, f)
        elif name == "grads_main":
            return reduce_start("main", MAIN_BIG, gb)
        elif name == "small_early":
            pending["small"] = _slots_start("small_gather_start", _pack_small([gs[n] for n in EARLY_SMALL] + [loss[:, :1]]))
            return pending["small"]["token"]
        elif name == "grads_ffn1":
            return reduce_start("ffn1", FFN1_BIG, gb)
        return None

    _, dx, _, gs = _local_step(x2, mem2, tgt2, wt, sm, ev)

    grads = {}
    for tag, names in (("main", MAIN_BIG), ("ffn1", FFN1_BIG)):
        g_rows, rows = reduce_finish(tag, dx)
        off = 0
        for n, r in zip(names, rows):
            shard = wts[n].shape
            grads[n] = g_rows[off:off + r].reshape((shard[2], shard[1]) if n in COL_SHARDED else shard[1:])
            off += r
    small_sum = _sum_slots("small_sum", _split_wait("small_gather_wait", pending["small"], dx)[1][0], F32)
    late = _allgather("small_allgather_late", [gs[LATE_SMALL].reshape(-1, LANES)])[0]
    late_sum = _sum_slots("small_sum_late", late.reshape(N_DEV, -1, LANES), F32)
    vals = _unpack_small(small_sum, [wts[n].shape for n in EARLY_SMALL] + [(1, 1)])
    total_loss = vals[-1].reshape(())
    def flat(n, a):
        a = a.reshape(wts[n].shape)
        a = jnp.swapaxes(a, -1, -2) if n in MINOR_SWAPPED else a
        return a.reshape(-1, a.shape[-1])

    def unflat(n, a):
        shape = wts[n].shape
        if n in MINOR_SWAPPED:
            return jnp.swapaxes(a.reshape(shape[:-2] + (shape[-1], shape[-2])), -1, -2)
        return a.reshape(shape)

    for n, g_full in zip(EARLY_SMALL + [LATE_SMALL], vals[:-1] + [late_sum]):
        grads[n] = flat(n, g_full)

    out_g, out_d, out_m, out_v = {}, {}, {}, {}
    by_shape = {}
    for n in WEIGHTS:
        by_shape.setdefault((flat(n, wts[n]).shape, n in COL_SHARDED), []).append(n)
    for (_, transposed), names in by_shape.items():
        items = [(flat(n, wts[n]), grads[n], flat(n, moms[n][0]), flat(n, moms[n][1])) for n in names]
        for n, res in zip(names, _adamw_group("adamw_" + names[0], items, transposed)):
            out_d[n], out_m[n], out_v[n], out_g[n] = (unflat(n, a) for a in res)

    return (total_loss, dx[None], *[out_g[n] for n in WEIGHTS], *[out_d[n] for n in WEIGHTS],
            *[out_m[n] for n in WEIGHTS], *[out_v[n] for n in WEIGHTS])
```

```python
import math

import jax
import jax.numpy as jnp
from jax import lax
from jax.experimental import pallas as pl
from jax.experimental.pallas import tpu as pltpu

F32 = jnp.float32
BF16 = jnp.bfloat16
EPS = 1e-6
N_XHEADS = 4
POOL_WINDOWS = (2, 4, 8, 16)
ADAM_LR = 0.001
ADAM_B1 = 0.9
ADAM_B2 = 0.999
ADAM_EPS = 1e-08
ADAM_WD = 0.01
ADAM_STEP = 10
N_DEV = 8
VMEM_LIMIT_V7X = 48 * 1024 * 1024
LANES = 128
SUBLANES = 8
SUB_ROWS = 256
POOL_PAD = 16
MESH = pl.DeviceIdType.MESH
ANY = pl.BlockSpec(memory_space=pl.ANY)
HBM = pl.BlockSpec(memory_space=pltpu.HBM)
SEM = pl.BlockSpec(memory_space=pltpu.SEMAPHORE)
SIDE_EFFECT = pltpu.SideEffectType.DATAFLOW_SIDE_EFFECTING

_DIMS = {
    "nt": (((1,), (1,)), ((), ())),
    "nn": (((1,), (0,)), ((), ())),
    "tn": (((0,), (0,)), ((), ())),
}


def _pick(dim, pref, mult=LANES):
    if dim <= pref:
        return dim
    for t in range(pref - pref % mult, 0, -mult):
        if dim % t == 0:
            return t
    return dim


def _params(sem):
    return pltpu.CompilerParams(dimension_semantics=sem, vmem_limit_bytes=VMEM_LIMIT_V7X)


def _tile(tm, tn, coff=0):
    return pl.BlockSpec((tm, tn), lambda i, j: (i, j + coff))


def _rowvec(tn, coff=0):
    return pl.BlockSpec((1, tn), lambda i, j: (0, j + coff))


def _out(m, n, dtype):
    return jax.ShapeDtypeStruct((m, n), dtype)


def _mm(name, form, a_list, b_list, groups, m, n, tm, tn, extras, epilogue, outs, after=None, sub=SUB_ROWS):
    na, nb, ne = len(a_list), len(b_list), len(extras)
    pins = [] if after is None else [after]
    step = tm if (sub is None or form == "tn" or tm % sub) else sub

    def a_spec(a):
        if form == "tn":
            return pl.BlockSpec((a.shape[0], tm), lambda i, j: (0, i))
        return pl.BlockSpec((tm, a.shape[1]), lambda i, j: (i, 0))

    def b_spec(b):
        if form == "nt":
            return pl.BlockSpec((tn, b.shape[1]), lambda i, j: (j, 0))
        return pl.BlockSpec((b.shape[0], tn), lambda i, j: (0, j))

    def body(*refs):
        a_refs, b_refs = refs[:na], refs[na:na + nb]
        e_refs, o_refs = refs[na + nb:na + nb + ne], refs[na + nb + ne + len(pins):]
        b_vals = {}
        for s0 in range(0, tm, step):
            rows = slice(None) if step == tm else pl.ds(s0, step)
            a_vals, accs = {}, []
            for group in groups:
                acc = None
                for ai, bi in group:
                    if ai not in a_vals:
                        a_vals[ai] = (a_refs[ai][...] if form == "tn" else a_refs[ai][rows, :]).astype(BF16)
                    if bi not in b_vals:
                        b_vals[bi] = b_refs[bi][...].astype(BF16)
                    d = lax.dot_general(a_vals[ai], b_vals[bi], _DIMS[form], preferred_element_type=F32)
                    acc = d if acc is None else acc + d
                accs.append(acc)
            res = epilogue(accs, *[e[rows, :] if e.shape[0] == tm else e[...] for e in e_refs])
            for o_ref, r in zip(o_refs, res):
                o_ref[rows, :] = r.astype(o_ref.dtype)

    out_specs = [_tile(tm, tn) if s is None else s for _, s in outs]
    res = pl.pallas_call(
        body, name=name, grid=(m // tm, n // tn),
        in_specs=[a_spec(a) for a in a_list] + [b_spec(b) for b in b_list] + [s for _, s in extras] + [ANY] * len(pins),
        out_specs=out_specs, out_shape=[o for o, _ in outs],
        compiler_params=_params(("parallel", "parallel")),
    )(*a_list, *b_list, *[e for e, _ in extras], *pins)
    return res


def _mm1(name, form, a, b, m, n, tm, tn, dtype, scale=None):
    epi = (lambda accs: (accs[0],)) if scale is None else (lambda accs: (accs[0] * scale,))
    return _mm(name, form, [a], [b], [[(0, 0)]], m, n, tm, tn, [], epi, [(_out(m, n, dtype), None)])[0]


def _rms_fwd(name, h, g):
    t, d = h.shape
    tm = _pick(t, 1024, SUBLANES)

    def body(h_ref, g_ref, n_ref):
        hv = h_ref[...]
        r = lax.rsqrt(jnp.mean(hv * hv, axis=-1, keepdims=True) + EPS)
        n_ref[...] = ((hv * r) * g_ref[...]).astype(BF16)

    return pl.pallas_call(
        body, name=name, grid=(t // tm,),
        in_specs=[pl.BlockSpec((tm, d), lambda i: (i, 0)), pl.BlockSpec((1, d), lambda i: (0, 0))],
        out_specs=pl.BlockSpec((tm, d), lambda i: (i, 0)), out_shape=_out(t, d, BF16),
        compiler_params=_params(("parallel",)),
    )(h, g)


def _rms_bwd(name, h, g, dn, dres=None):
    t, d = h.shape
    tm = _pick(t, 1024, SUBLANES)
    need_dh = dres is not None

    def body(*refs):
        if need_dh:
            h_ref, g_ref, dn_ref, dres_ref, dh_ref, dhb_ref, dg_ref = refs
        else:
            h_ref, g_ref, dn_ref, dg_ref = refs
        hv = h_ref[...]
        r = lax.rsqrt(jnp.mean(hv * hv, axis=-1, keepdims=True) + EPS)
        nh = hv * r
        dnv = dn_ref[...].astype(F32)

        @pl.when(pl.program_id(0) == 0)
        def _():
            dg_ref[...] = jnp.zeros_like(dg_ref)

        dg_ref[...] += jnp.sum(dnv * nh, axis=0, keepdims=True)
        if need_dh:
            dng = dnv * g_ref[...]
            dh = dres_ref[...] + r * (dng - nh * jnp.mean(dng * nh, axis=-1, keepdims=True))
            dh_ref[...] = dh
            dhb_ref[...] = dh.astype(BF16)

    row = pl.BlockSpec((tm, d), lambda i: (i, 0))
    vec = pl.BlockSpec((1, d), lambda i: (0, 0))
    if need_dh:
        return pl.pallas_call(
            body, name=name, grid=(t // tm,), in_specs=[row, vec, row, row], out_specs=[row, row, vec],
            out_shape=[_out(t, d, F32), _out(t, d, BF16), _out(1, d, F32)], compiler_params=_params(("arbitrary",)),
        )(h, g, dn, dres)
    return pl.pallas_call(
        body, name=name, grid=(t // tm,), in_specs=[row, vec, row], out_specs=vec,
        out_shape=_out(1, d, F32), compiler_params=_params(("arbitrary",)),
    )(h, g, dn)


def _loss_head(h, g, tgt):
    t, d = h.shape
    tm = _pick(t, 1024, SUBLANES)

    def body(h_ref, g_ref, t_ref, dh_ref, dhb_ref, dg_ref, loss_ref):
        hv = h_ref[...]
        r = lax.rsqrt(jnp.mean(hv * hv, axis=-1, keepdims=True) + EPS)
        nh = hv * r
        err = nh * g_ref[...] - t_ref[...]

        @pl.when(pl.program_id(0) == 0)
        def _():
            dg_ref[...] = jnp.zeros_like(dg_ref)
            loss_ref[...] = jnp.zeros_like(loss_ref)

        per_row = jnp.mean(err * err, axis=-1, keepdims=True)
        loss_ref[...] += 0.5 * jnp.sum(per_row, axis=0, keepdims=True)
        dy = err * (1.0 / d)
        dg_ref[...] += jnp.sum(dy * nh, axis=0, keepdims=True)
        dng = dy * g_ref[...]
        dh = r * (dng - nh * jnp.mean(dng * nh, axis=-1, keepdims=True))
        dh_ref[...] = dh
        dhb_ref[...] = dh.astype(BF16)

    row = pl.BlockSpec((tm, d), lambda i: (i, 0))
    vec = pl.BlockSpec((1, d), lambda i: (0, 0))
    return pl.pallas_call(
        body, name="loss_head", grid=(t // tm,), in_specs=[row, vec, row],
        out_specs=[row, row, vec, pl.BlockSpec((1, LANES), lambda i: (0, 0))],
        out_shape=[_out(t, d, F32), _out(t, d, BF16), _out(1, d, F32), _out(1, LANES, F32)],
        compiler_params=_params(("arbitrary",)),
    )(h, g, tgt)


def _ffn_fwd(tag, h, n, wg_t, wu_t, wd):
    t, d = h.shape
    f = wg_t.shape[0]
    tm, tn = _pick(t, 1024), _pick(f, 1408)

    def up_epi(accs):
        a, b = accs
        return a, b, (a * jax.nn.sigmoid(a)) * b

    a, b, hid = _mm(tag + "_up", "nt", [n], [wg_t, wu_t], [[(0, 0)], [(0, 1)]], t, f, tm, tn, [], up_epi,
                    [(_out(t, f, BF16), None)] * 3)
    if callable(wd):
        wd = wd(hid)
    tm2, tn2 = _pick(t, 1024), _pick(d, 512)
    h_out = _mm(tag + "_down", "nn", [hid], [wd], [[(0, 0)]], t, d, tm2, tn2, [(h, _tile(tm2, tn2))],
                lambda accs, hin: (hin + 0.5 * accs[0],), [(_out(t, d, F32), None)])[0]
    return h_out, (n, a, b, hid)


def _ffn_bwd(tag, h, g, wg_t, wu_t, wd, saved, dh, dh_bf, weights_done=None, after=None):
    n, a, b, hid = saved
    t, d = h.shape
    f = wd.shape[0]
    tm, tn = _pick(t, 1024), _pick(f, 1408)

    def hid_epi(accs, av, bv):
        dhid = 0.5 * accs[0]
        av, bv = av.astype(F32), bv.astype(F32)
        sig = jax.nn.sigmoid(av)
        da = dhid * bv * (sig * (1.0 + av * (1.0 - sig)))
        db = dhid * (av * sig)
        return da, db

    da, db = _mm(tag + "_bwd_hid", "nt", [dh_bf], [wd], [[(0, 0)]], t, f, tm, tn,
                 [(a, _tile(tm, tn)), (b, _tile(tm, tn))], hid_epi, [(_out(t, f, BF16), None)] * 2, after=after)
    tw, tnw = _pick(f, 1408), _pick(d, 512)
    d_wd = _mm1(tag + "_dwd", "tn", hid, dh_bf, f, d, tw, tnw, BF16, scale=0.5)
    d_wg = _mm1(tag + "_dwg", "tn", da, n, f, d, tw, tnw, BF16)
    d_wu = _mm1(tag + "_dwu", "tn", db, n, f, d, tw, tnw, BF16)
    pin = weights_done(d_wg, d_wu, d_wd) if weights_done is not None else None
    tm2, tn2 = _pick(t, 1024), _pick(d, 512)
    dn = _mm(tag + "_dn", "nn", [da, db], [wg_t, wu_t], [[(0, 0), (1, 1)]], t, d, tm2, tn2, [],
             lambda accs: (accs[0],), [(_out(t, d, F32), None)], after=pin)[0]
    dh_in, dh_in_bf, dg = _rms_bwd(tag + "_norm_bwd", h, g, dn, dh)
    return dh_in, dh_in_bf, dg, d_wg, d_wu, d_wd


def _window_sum(win, offsets):
    n = win.shape[0]
    acc = None
    for j in offsets:
        term = win if j == 0 else pltpu.roll(win, (-j) % n, 0)
        acc = term if acc is None else acc + term
    return acc


def _pool_counts(r0, ch, c, left, right, t):
    pos = r0 + lax.broadcasted_iota(jnp.int32, (ch, c), 0)
    return (jnp.minimum(pos + right + 1, t) - jnp.maximum(pos - left, 0)).astype(F32)


def _pool_fwd(proj, pool_w_bf, pool_scale):
    t = proj.shape[0]
    ng, c, _ = pool_w_bf.shape
    ch = _pick(t, 256, SUBLANES)
    pad = POOL_PAD

    def body(p_ref, w_ref, s_ref, pooled_ref, pm_ref, buf):
        grp = pl.program_id(0)
        buf[pl.ds(0, pad), :] = jnp.zeros((pad, c), F32)
        buf[pl.ds(pad + t, pad), :] = jnp.zeros((pad, c), F32)

        def fill(ci, carry):
            r0 = pl.multiple_of(ci * ch, SUBLANES)
            buf[pl.ds(pl.multiple_of(r0 + pad, SUBLANES), ch), :] = p_ref[pl.ds(r0, ch), :]
            return carry

        lax.fori_loop(0, t // ch, fill, 0)
        for gi, w in enumerate(POOL_WINDOWS):
            left = w // 2
            right = w - 1 - left

            @pl.when(grp == gi)
            def _(left=left, right=right):
                def chunk(ci, carry):
                    r0 = pl.multiple_of(ci * ch, SUBLANES)
                    win = buf[pl.ds(r0, ch + 2 * pad), :]
                    s = _window_sum(win, range(-left, right + 1))[pad:pad + ch]
                    pooled = s / _pool_counts(r0, ch, c, left, right, t) - win[pad:pad + ch]
                    pooled_bf = pooled.astype(BF16)
                    mixed = jnp.dot(pooled_bf, w_ref[0], preferred_element_type=F32)
                    pooled_ref[pl.ds(r0, ch), :] = pooled_bf
                    pm_ref[pl.ds(r0, ch), :] = (mixed * s_ref[...]).astype(BF16)
                    return carry

                lax.fori_loop(0, t // ch, chunk, 0)

    col = pl.BlockSpec((t, c), lambda g: (0, g))
    return pl.pallas_call(
        body, name="pool_fwd", grid=(ng,),
        in_specs=[col, pl.BlockSpec((1, c, c), lambda g: (g, 0, 0)), pl.BlockSpec((1, c), lambda g: (0, g))],
        out_specs=[col, col], out_shape=[_out(t, ng * c, BF16), _out(t, ng * c, BF16)],
        scratch_shapes=[pltpu.VMEM((t + 2 * pad, c), F32)],
        compiler_params=_params(("parallel",)),
    )(proj, pool_w_bf, pool_scale)


def _pool_bwd(pooled, dpm, pool_w_bf, pool_scale):
    t = pooled.shape[0]
    ng, c, _ = pool_w_bf.shape
    ch = _pick(t, 256, SUBLANES)
    pad = POOL_PAD

    def body(pooled_ref, dpm_ref, w_ref, s_ref, dp_ref, dw_ref, ds_ref, buf, raw):
        grp = pl.program_id(0)
        buf[pl.ds(0, pad), :] = jnp.zeros((pad, c), F32)
        buf[pl.ds(pad + t, pad), :] = jnp.zeros((pad, c), F32)
        dw_ref[...] = jnp.zeros_like(dw_ref)
        ds_ref[...] = jnp.zeros_like(ds_ref)
        for gi, w in enumerate(POOL_WINDOWS):
            left = w // 2
            right = w - 1 - left

            @pl.when(grp == gi)
            def _(left=left, right=right):
                def first(ci, carry):
                    r0 = pl.multiple_of(ci * ch, SUBLANES)
                    pv = pooled_ref[pl.ds(r0, ch), :]
                    dpm_v = dpm_ref[pl.ds(r0, ch), :]
                    mixed = jnp.dot(pv, w_ref[0], preferred_element_type=F32)
                    ds_ref[...] += jnp.sum(dpm_v * mixed, axis=0, keepdims=True)
                    dmixed = (dpm_v * s_ref[...]).astype(BF16)
                    dw_ref[0] += lax.dot_general(pv, dmixed, _DIMS["tn"], preferred_element_type=F32)
                    dpooled = lax.dot_general(dmixed, w_ref[0], _DIMS["nt"], preferred_element_type=F32)
                    raw[pl.ds(r0, ch), :] = dpooled
                    buf[pl.ds(pl.multiple_of(r0 + pad, SUBLANES), ch), :] = (
                        dpooled / _pool_counts(r0, ch, c, left, right, t))
                    return carry

                lax.fori_loop(0, t // ch, first, 0)

                def second(ci, carry):
                    r0 = pl.multiple_of(ci * ch, SUBLANES)
                    win = buf[pl.ds(r0, ch + 2 * pad), :]
                    s = _window_sum(win, range(-right, left + 1))[pad:pad + ch]
                    dp_ref[pl.ds(r0, ch), :] = (s - raw[pl.ds(r0, ch), :]).astype(BF16)
                    return carry

                lax.fori_loop(0, t // ch, second, 0)

    col = pl.BlockSpec((t, c), lambda g: (0, g))
    return pl.pallas_call(
        body, name="pool_bwd", grid=(ng,),
        in_specs=[col, col, pl.BlockSpec((1, c, c), lambda g: (g, 0, 0)), pl.BlockSpec((1, c), lambda g: (0, g))],
        out_specs=[col, pl.BlockSpec((1, c, c), lambda g: (g, 0, 0)), pl.BlockSpec((1, c), lambda g: (0, g))],
        out_shape=[_out(t, ng * c, BF16), jax.ShapeDtypeStruct((ng, c, c), F32), _out(1, ng * c, F32)],
        scratch_shapes=[pltpu.VMEM((t + 2 * pad, c), F32), pltpu.VMEM((t, c), F32)],
        compiler_params=_params(("parallel",)),
    )(pooled, dpm, pool_w_bf, pool_scale)


def _discretise(a_re, a_im, log_dt, b_re, b_im):
    dt = jnp.exp(log_dt)
    mag = jnp.exp(dt * a_re)
    ang = dt * a_im
    abr = mag * jnp.cos(ang)
    abi = mag * jnp.sin(ang)
    den = a_re * a_re + a_im * a_im
    nr = abr - 1.0
    qr = (nr * a_re + abi * a_im) / den
    qi = (abi * a_re - nr * a_im) / den
    return abr, abi, qr * b_re - qi * b_im, qr * b_im + qi * b_re


def _ssm_disc(args):
    def body(ar, ai, ld, br, bi, o1, o2, o3, o4):
        res = _discretise(ar[...], ai[...], ld[...], br[...], bi[...])
        for o, r in zip((o1, o2, o3, o4), res):
            o[...] = r

    like = lambda a: jax.ShapeDtypeStruct(a.shape, F32)
    return pl.pallas_call(
        body, name="ssm_disc", out_shape=[like(args[0]), like(args[0]), like(args[3]), like(args[3])],
    )(*args)


def _ssm_disc_bwd(args, cots):
    def body(ar, ai, ld, br, bi, c1, c2, c3, c4, o1, o2, o3, o4, o5):
        _, vjp = jax.vjp(_discretise, ar[...], ai[...], ld[...], br[...], bi[...])
        res = vjp((c1[...], c2[...], c3[...], c4[...]))
        for o, r in zip((o1, o2, o3, o4, o5), res):
            o[...] = r

    return pl.pallas_call(
        body, name="ssm_disc_bwd", out_shape=[jax.ShapeDtypeStruct(a.shape, F32) for a in args],
    )(*args, *cots)


def _cmul(pr, pi, qr, qi):
    return pr * qr - pi * qi, pr * qi + pi * qr


def _cpow(pr, pi, n):
    rr, ri = None, None
    while n:
        if n & 1:
            rr, ri = (pr, pi) if rr is None else _cmul(rr, ri, pr, pi)
        n >>= 1
        if n:
            pr, pi = _cmul(pr, pi, pr, pi)
    return rr, ri


def _segment_carry(er, ei, pr, pi, reverse):
    row = lax.broadcasted_iota(jnp.int32, er.shape, 0)
    cr, ci = jnp.zeros_like(er), jnp.zeros_like(ei)
    for _ in range(SUBLANES - 1):
        tr = er + pr * cr - pi * ci
        ti = ei + pr * ci + pi * cr
        if reverse:
            keep, shift = row < SUBLANES - 1, SUBLANES - 1
        else:
            keep, shift = row >= 1, 1
        cr = jnp.where(keep, pltpu.roll(tr, shift, 0), 0.0)
        ci = jnp.where(keep, pltpu.roll(ti, shift, 0), 0.0)
    return cr, ci


def _ssm_fwd(name, sp, b_re, b_im, c_re, c_im, ar, ai, reverse):
    t, c = sp.shape
    s = ar.shape[1]
    w = _pick(s, 512)
    ch = _pick(t, 1024, SUBLANES)
    n_ch, gpc, steps = t // ch, ch // SUBLANES, t // SUBLANES

    def body(sp_ref, bre_ref, bim_ref, cre_ref, cim_ref, ar_ref, ai_ref, xr_ref, xi_ref, y_ref, ur, ui, xbr, xbi):
        a_r = jnp.broadcast_to(ar_ref[...], (SUBLANES, w))
        a_i = jnp.broadcast_to(ai_ref[...], (SUBLANES, w))

        @pl.when(pl.program_id(0) == 0)
        def _():
            y_ref[...] = jnp.zeros_like(y_ref)

        def sweep(h0, store):
            def chunk(k, h):
                ci = n_ch - 1 - k if reverse else k
                rows = pl.ds(pl.multiple_of(ci * ch, ch), ch)
                spv = sp_ref[rows, :].astype(BF16)
                ur[...] = jnp.dot(spv, bre_ref[...], preferred_element_type=F32)
                ui[...] = jnp.dot(spv, bim_ref[...], preferred_element_type=F32)

                def group(g, hh):
                    gi = gpc - 1 - g if reverse else g
                    r0 = pl.multiple_of(gi * SUBLANES, SUBLANES)
                    hr, hi = hh
                    nr = a_r * hr - a_i * hi + ur[pl.ds(r0, SUBLANES), :]
                    ni = a_r * hi + a_i * hr + ui[pl.ds(r0, SUBLANES), :]
                    if store:
                        xbr[pl.ds(r0, SUBLANES), :] = nr
                        xbi[pl.ds(r0, SUBLANES), :] = ni
                    return nr, ni

                h = lax.fori_loop(0, gpc, group, h)
                if store:
                    xr16, xi16 = xbr[...].astype(BF16), xbi[...].astype(BF16)
                    xr_ref[rows, :] = xr16
                    xi_ref[rows, :] = xi16
                    y_ref[rows, :] += (lax.dot_general(xr16, cre_ref[...], _DIMS["nt"], preferred_element_type=F32)
                                       + lax.dot_general(xi16, cim_ref[...], _DIMS["nt"], preferred_element_type=F32))
                return h

            return lax.fori_loop(0, n_ch, chunk, h0)

        zero = jnp.zeros((SUBLANES, w), F32)
        er, ei = sweep((zero, zero), False)
        pr, pi = _cpow(ar_ref[...], ai_ref[...], steps)
        sweep(_segment_carry(er, ei, pr, pi, reverse), True)

    col = lambda i: (0, i)
    return pl.pallas_call(
        body, name=name, grid=(s // w,),
        in_specs=[pl.BlockSpec((t, c), lambda i: (0, 0))] + [pl.BlockSpec((c, w), col)] * 4
        + [pl.BlockSpec((1, w), col)] * 2,
        out_specs=[pl.BlockSpec((t, w), col), pl.BlockSpec((t, w), col), pl.BlockSpec((t, c), lambda i: (0, 0))],
        out_shape=[_out(t, s, BF16), _out(t, s, BF16), _out(t, c, F32)],
        scratch_shapes=[pltpu.VMEM((ch, w), F32)] * 4,
        compiler_params=_params(("arbitrary",)),
    )(sp, b_re, b_im, c_re, c_im, ar, ai)


def _ssm_bwd(name, dyp, c_re, c_im, xr, xi, ar, ai, reverse):
    t, c = dyp.shape
    s = ar.shape[1]
    w = _pick(s, 512)
    ch = _pick(t, 512, SUBLANES)
    n_ch, gpc, steps = t // ch, ch // SUBLANES, t // SUBLANES
    back = not reverse
    edge = 2 * SUBLANES

    def body(dy_ref, cre_ref, cim_ref, xr_ref, xi_ref, ar_ref, ai_ref, lr_ref, li_ref, dar_ref, dai_ref,
             gr, gi_, lbr, lbi, xbr, xbi):
        a_r = jnp.broadcast_to(ar_ref[...], (SUBLANES, w))
        a_i = -jnp.broadcast_to(ai_ref[...], (SUBLANES, w))
        row = lax.broadcasted_iota(jnp.int32, (SUBLANES, w), 0)

        def neighbours(ci, x_ref, buf):
            rows = pl.ds(pl.multiple_of(ci * ch, ch), ch)
            if reverse:
                buf[pl.ds(0, ch), :] = x_ref[rows, :].astype(F32)
                nxt = x_ref[pl.ds(pl.multiple_of(jnp.minimum(ci + 1, n_ch - 1) * ch, ch), edge), :].astype(F32)[:SUBLANES]
                first = x_ref[pl.ds(0, edge), :].astype(F32)[:SUBLANES]
                wrap = jnp.where(row < SUBLANES - 1, pltpu.roll(first, SUBLANES - 1, 0), 0.0)
                buf[pl.ds(ch, SUBLANES), :] = jnp.where(ci == n_ch - 1, wrap, nxt)
            else:
                buf[pl.ds(SUBLANES, ch), :] = x_ref[rows, :].astype(F32)
                prv = x_ref[pl.ds(pl.multiple_of(jnp.maximum(ci * ch - edge, 0), edge), edge), :].astype(F32)[SUBLANES:]
                last = x_ref[pl.ds(t - edge, edge), :].astype(F32)[SUBLANES:]
                wrap = jnp.where(row >= 1, pltpu.roll(last, 1, 0), 0.0)
                buf[pl.ds(0, SUBLANES), :] = jnp.where(ci == 0, wrap, prv)

        def sweep(h0, store):
            def chunk(k, carry):
                ci = n_ch - 1 - k if back else k
                rows = pl.ds(pl.multiple_of(ci * ch, ch), ch)
                dyv = dy_ref[rows, :].astype(BF16)
                gr[...] = jnp.dot(dyv, cre_ref[...], preferred_element_type=F32)
                gi_[...] = jnp.dot(dyv, cim_ref[...], preferred_element_type=F32)
                if store:
                    neighbours(ci, xr_ref, xbr)
                    neighbours(ci, xi_ref, xbi)

                def group(g, cc):
                    gidx = gpc - 1 - g if back else g
                    r0 = pl.multiple_of(gidx * SUBLANES, SUBLANES)
                    hr, hi = cc[0], cc[1]
                    nr = a_r * hr - a_i * hi + gr[pl.ds(r0, SUBLANES), :]
                    ni = a_r * hi + a_i * hr + gi_[pl.ds(r0, SUBLANES), :]
                    if not store:
                        return nr, ni
                    lbr[pl.ds(r0, SUBLANES), :] = nr
                    lbi[pl.ds(r0, SUBLANES), :] = ni
                    x0 = pl.multiple_of(r0 + SUBLANES, SUBLANES) if reverse else r0
                    xpr, xpi = xbr[pl.ds(x0, SUBLANES), :], xbi[pl.ds(x0, SUBLANES), :]
                    return nr, ni, cc[2] + nr * xpr + ni * xpi, cc[3] + ni * xpr - nr * xpi

                carry = lax.fori_loop(0, gpc, group, carry)
                if store:
                    lr_ref[rows, :] = lbr[...].astype(BF16)
                    li_ref[rows, :] = lbi[...].astype(BF16)
                return carry

            return lax.fori_loop(0, n_ch, chunk, h0)

        zero = jnp.zeros((SUBLANES, w), F32)
        er, ei = sweep((zero, zero), False)
        pr, pi = _cpow(ar_ref[...], -ai_ref[...], steps)
        cr, ci0 = _segment_carry(er, ei, pr, pi, back)
        _, _, dar, dai = sweep((cr, ci0, zero, zero), True)
        dar_ref[...] = jnp.sum(dar, axis=0, keepdims=True)
        dai_ref[...] = jnp.sum(dai, axis=0, keepdims=True)

    col = lambda i: (0, i)
    return pl.pallas_call(
        body, name=name, grid=(s // w,),
        in_specs=[pl.BlockSpec((t, c), lambda i: (0, 0)), pl.BlockSpec((c, w), col), pl.BlockSpec((c, w), col),
                  pl.BlockSpec((t, w), col), pl.BlockSpec((t, w), col), pl.BlockSpec((1, w), col), pl.BlockSpec((1, w), col)],
        out_specs=[pl.BlockSpec((t, w), col), pl.BlockSpec((t, w), col), pl.BlockSpec((1, w), col), pl.BlockSpec((1, w), col)],
        out_shape=[_out(t, s, BF16), _out(t, s, BF16), _out(1, s, F32), _out(1, s, F32)],
        scratch_shapes=[pltpu.VMEM((ch, w), F32)] * 4 + [pltpu.VMEM((ch + SUBLANES, w), F32)] * 2,
        compiler_params=_params(("parallel",)),
    )(dyp, c_re, c_im, xr, xi, ar, ai)


def _ssm_finish(y0, y1, sp, skip):
    t, c = sp.shape
    steps = t // SUBLANES
    w = _pick(c, LANES)

    def body(y0_ref, y1_ref, sp_ref, d_ref, y_ref, ys_ref):
        rows = pl.ds(pl.program_id(1), steps, stride=SUBLANES)
        y = y0_ref[rows, :] + y1_ref[rows, :] + sp_ref[rows, :] * d_ref[...]
        y_ref[...] = y
        ys_ref[...] = jax.nn.gelu(y).astype(BF16)

    whole = pl.BlockSpec((t, w), lambda j, k: (0, j))
    seg = pl.BlockSpec((steps, w), lambda j, k: (k, j))
    return pl.pallas_call(
        body, name="ssm_finish", grid=(c // w, SUBLANES),
        in_specs=[whole, whole, whole, pl.BlockSpec((1, w), lambda j, k: (0, j))], out_specs=[seg, seg],
        out_shape=[_out(t, c, F32), _out(t, c, BF16)], compiler_params=_params(("parallel", "arbitrary")),
    )(y0, y1, sp, skip)


def _to_segments(a):
    t, c = a.shape
    return a.reshape(SUBLANES, t // SUBLANES, c).transpose(1, 0, 2).reshape(t, c)


def _from_segments(a):
    t, c = a.shape
    return a.reshape(t // SUBLANES, SUBLANES, c).transpose(1, 0, 2).reshape(t, c)


def _colsum_prod(name, a, b, b_coff=0):
    t, n = a.shape
    tm = _pick(t, 1024, SUBLANES)

    def body(a_ref, b_ref, o_ref):
        @pl.when(pl.program_id(0) == 0)
        def _():
            o_ref[...] = jnp.zeros_like(o_ref)

        o_ref[...] += jnp.sum(a_ref[...].astype(F32) * b_ref[...].astype(F32), axis=0, keepdims=True)

    return pl.pallas_call(
        body, name=name, grid=(t // tm,),
        in_specs=[pl.BlockSpec((tm, n), lambda i: (i, 0)), pl.BlockSpec((tm, n), lambda i: (i, b_coff))],
        out_specs=pl.BlockSpec((1, n), lambda i: (0, 0)), out_shape=_out(1, n, F32),
        compiler_params=_params(("arbitrary",)),
    )(a, b)


def _ssm_maps(arrs, signs):
    n2, hh, p = arrs[0].shape
    g = n2 // 2

    def body(*refs):
        ins, outs = refs[:len(arrs)], refs[len(arrs):]
        for a, (a_ref, sign) in enumerate(zip(ins, signs)):
            for d in range(2):
                o_ref = outs[2 * a + d]
                o_ref[...] = jnp.zeros_like(o_ref)
                for k in range(g):
                    o_ref[pl.ds(k * hh, hh), pl.ds(k * p, p)] = (sign * a_ref[d * g + k]).astype(BF16)

    outs = pl.pallas_call(body, name="ssm_maps", out_shape=[_out(g * hh, g * p, BF16)] * (2 * len(arrs)))(*arrs)
    return [outs[2 * a:2 * a + 2] for a in range(len(arrs))]


def _softmax(qh, kh, scale):
    s = lax.dot_general(qh, kh, _DIMS["nt"], preferred_element_type=F32) * scale
    e = jnp.exp(s - jnp.max(s, axis=-1, keepdims=True))
    return e / jnp.sum(e, axis=-1, keepdims=True)


def _attn_fwd(q, kv):
    t, d = q.shape
    mm_ = kv.shape[0]
    hd = d // N_XHEADS
    scale = 1.0 / math.sqrt(hd)
    tm = _pick(t, 1024, SUBLANES)

    def body(q_ref, kv_ref, o_ref):
        for h in range(N_XHEADS):
            sl = pl.ds(h * hd, hd)
            p = _softmax(q_ref[:, sl], kv_ref[:, sl], scale)
            o_ref[:, sl] = jnp.dot(p.astype(BF16), kv_ref[:, pl.ds(d + h * hd, hd)],
                                   preferred_element_type=F32).astype(BF16)

    return pl.pallas_call(
        body, name="attn_fwd", grid=(t // tm,),
        in_specs=[pl.BlockSpec((tm, d), lambda i: (i, 0)), pl.BlockSpec((mm_, 2 * d), lambda i: (0, 0))],
        out_specs=pl.BlockSpec((tm, d), lambda i: (i, 0)), out_shape=_out(t, d, BF16),
        compiler_params=_params(("parallel",)),
    )(q, kv)


def _attn_bwd(q, kv, do):
    t, d = q.shape
    mm_ = kv.shape[0]
    hd = d // N_XHEADS
    scale = 1.0 / math.sqrt(hd)
    tm = _pick(t, 1024, SUBLANES)

    def body(q_ref, kv_ref, do_ref, dq_ref, dkv_ref):
        @pl.when(pl.program_id(0) == 0)
        def _():
            dkv_ref[...] = jnp.zeros_like(dkv_ref)

        for h in range(N_XHEADS):
            sl = pl.ds(h * hd, hd)
            vsl = pl.ds(d + h * hd, hd)
            qh, kh, doh = q_ref[:, sl], kv_ref[:, sl], do_ref[:, sl]
            p = _softmax(qh, kh, scale)
            dp = lax.dot_general(doh, kv_ref[:, vsl], _DIMS["nt"], preferred_element_type=F32)
            dkv_ref[:, vsl] += lax.dot_general(p.astype(BF16), doh, _DIMS["tn"], preferred_element_type=F32)
            ds = (p * (dp - jnp.sum(dp * p, axis=-1, keepdims=True)) * scale).astype(BF16)
            dq_ref[:, sl] = jnp.dot(ds, kh, preferred_element_type=F32).astype(BF16)
            dkv_ref[:, sl] += lax.dot_general(ds, qh, _DIMS["tn"], preferred_element_type=F32)

    row = pl.BlockSpec((tm, d), lambda i: (i, 0))
    full = pl.BlockSpec((mm_, 2 * d), lambda i: (0, 0))
    return pl.pallas_call(
        body, name="attn_bwd", grid=(t // tm,), in_specs=[row, full, row], out_specs=[row, full],
        out_shape=[_out(t, d, BF16), _out(mm_, 2 * d, F32)], compiler_params=_params(("arbitrary",)),
    )(q, kv, do)


def _ew(name, fn, ins, outs, rows_pref=256):
    r, c = ins[0].shape
    tr = _pick(r, rows_pref, SUBLANES)
    ni = len(ins)

    def body(*refs):
        res = fn(*[x[...] for x in refs[:ni]])
        for o_ref, v in zip(refs[ni:], res):
            o_ref[...] = v.astype(o_ref.dtype)

    blk = pl.BlockSpec((tr, c), lambda i: (i, 0))
    return pl.pallas_call(
        body, name=name, grid=(r // tr,), in_specs=[blk] * ni, out_specs=[blk] * len(outs),
        out_shape=[_out(r, c, dt) for dt in outs], compiler_params=_params(("parallel",)),
    )(*ins)


def _sum_slots(name, a, dtype):
    s, r, c = a.shape
    tr = _pick(r, 256, SUBLANES)

    def body(a_ref, o_ref):
        acc = a_ref[0].astype(F32)
        for k in range(1, s):
            acc = acc + a_ref[k].astype(F32)
        o_ref[...] = acc.astype(o_ref.dtype)

    return pl.pallas_call(
        body, name=name, grid=(r // tr,), in_specs=[pl.BlockSpec((s, tr, c), lambda i: (0, i, 0))],
        out_specs=pl.BlockSpec((tr, c), lambda i: (i, 0)), out_shape=_out(r, c, dtype),
        compiler_params=_params(("parallel",)),
    )(a)


def _adamw_step(wv, gv, mv, vv):
    bc1 = 1.0 - ADAM_B1 ** ADAM_STEP
    bc2 = 1.0 - ADAM_B2 ** ADAM_STEP
    m2 = ADAM_B1 * mv + (1.0 - ADAM_B1) * gv
    v2 = ADAM_B2 * vv + (1.0 - ADAM_B2) * (gv * gv)
    delta = -ADAM_LR * ((m2 / bc1) / (jnp.sqrt(v2 / bc2) + ADAM_EPS) + ADAM_WD * wv)
    return delta, m2, v2


def _adamw_group(name, items, transposed):
    k, r = items[0][0].shape
    if transposed and r % LANES != 0:
        rows = _adamw_group(name, [(w.T, g, m.T, v.T) for w, g, m, v in items], False)
        return [[a.T for a in item] for item in rows]
    tk = _pick(k, max(SUBLANES, ADAMW_STEP_WORDS // (r * len(items))), SUBLANES)
    n_out = 4 if transposed else 3

    def body(*refs):
        ins, outs = refs[:4 * len(items)], refs[4 * len(items):]
        for i in range(len(items)):
            wv, gv, mv, vv = (a[...] for a in ins[4 * i:4 * i + 4])
            if transposed:
                gv = gv.T
            res = _adamw_step(wv, gv, mv, vv) + ((gv,) if transposed else ())
            for o_ref, val in zip(outs[n_out * i:n_out * (i + 1)], res):
                o_ref[...] = val

    blk = pl.BlockSpec((tk, r), lambda j: (j, 0))
    g_blk = pl.BlockSpec((r, tk), lambda j: (0, j)) if transposed else blk
    res = pl.pallas_call(
        body, name=name, grid=(k // tk,), in_specs=[blk, g_blk, blk, blk] * len(items),
        out_specs=[blk] * (n_out * len(items)), out_shape=[pltpu.HBM((k, r), F32)] * (n_out * len(items)),
        compiler_params=_params(("parallel",)),
    )(*[pltpu.with_memory_space_constraint(a, pltpu.HBM) for item in items for a in item])
    return [list(res[n_out * i:n_out * (i + 1)]) + ([] if transposed else [items[i][1]]) for i in range(len(items))]


def _allgather(name, arrs):
    n = len(arrs)

    def body(*refs):
        ins, outs = refs[:n], refs[n:2 * n]
        send_sems, recv_sems, local_sems = refs[2 * n:]
        x, y, c = lax.axis_index("x"), lax.axis_index("y"), lax.axis_index("c")
        me, sibling = (x, y, c), (x, y, 1 - c)
        chips = [(1 - x, y), (x, 1 - y), (1 - x, 1 - y)]

        def rows(a, px, py, pc):
            r = ins[a].shape[0]
            return outs[a].at[pl.ds((4 * px + 2 * py + pc) * r, r), :]

        def copy(a, k, block, to, src=None):
            return pltpu.make_async_remote_copy(
                src_ref=rows(a, *block) if src is None else src, dst_ref=rows(a, *block),
                send_sem=send_sems.at[a, k], recv_sem=recv_sems.at[a, k], device_id=to, device_id_type=MESH)

        mine = [pltpu.make_async_copy(ins[a], rows(a, *me), local_sems.at[a]) for a in range(n)]
        for cp in mine:
            cp.start()
        first = []
        for a in range(n):
            first.append(copy(a, 0, me, sibling, src=ins[a]))
            first += [copy(a, 1 + j, me, (*chip, c), src=ins[a]) for j, chip in enumerate(chips)]
        for cp in first:
            cp.start()
        passed = []
        for j, chip in enumerate(chips):
            for a in range(n):
                copy(a, 1 + j, (*chip, c), me).wait_recv()
                cp = copy(a, 4 + j, (*chip, c), sibling)
                cp.start()
                passed.append(cp)
        for a in range(n):
            copy(a, 0, sibling, me).wait_recv()
            for j, chip in enumerate(chips):
                copy(a, 4 + j, (*chip, 1 - c), me).wait_recv()
        for cp in first + passed:
            cp.wait_send()
        for cp in mine:
            cp.wait()

    return pl.pallas_call(
        body, name=name, in_specs=[ANY] * n, out_specs=[ANY] * n,
        out_shape=[_out(N_DEV * a.shape[0], a.shape[1], a.dtype) for a in arrs],
        scratch_shapes=[pltpu.SemaphoreType.DMA((n, 7)), pltpu.SemaphoreType.DMA((n, 7)), pltpu.SemaphoreType.DMA((n,))],
    )(*arrs)


def _cores_start(name, blocks):
    n = len(blocks)
    c = blocks[0].shape[2]
    r = sum(b.shape[1] for b in blocks)

    def build(src_refs, land_refs, send_sems, recv_sems):
        x, y, cc = lax.axis_index("x"), lax.axis_index("y"), lax.axis_index("c")
        remote, off = [], 0
        for a, src in enumerate(src_refs):
            rows = pl.ds(off, src.shape[1])
            off += src.shape[1]
            for q in range(4):
                remote.append(pltpu.make_async_remote_copy(
                    src_ref=src.at[2 * q + (1 - cc)], dst_ref=land_refs[0].at[q, rows], send_sem=send_sems.at[4 * a + q],
                    recv_sem=recv_sems.at[4 * a + q], device_id=(x, y, 1 - cc), device_id_type=MESH))
        return remote, []

    return _split_start(name, [(blocks, [jax.ShapeDtypeStruct((4, r, c), blocks[0].dtype)], 4 * n, 0, build)])[0]


def _peer(k, x, y, c):
    return (1 - x if k & 4 else x, 1 - y if k & 2 else y, 1 - c if k & 1 else c)


def _split_start(name, groups, after=None):
    pins = [] if after is None else [after]
    bufs, sem_shapes, spans = [], [], []
    for srcs, land_shapes, n_remote, n_local, _ in groups:
        sems = [pltpu.SemaphoreType.DMA((n_remote,)), pltpu.SemaphoreType.DMA((n_remote,))]
        sems += [pltpu.SemaphoreType.DMA((n_local,))] if n_local else []
        spans.append((len(bufs), len(srcs), len(land_shapes), len(sem_shapes), len(sems)))
        bufs += [pltpu.with_memory_space_constraint(a, pltpu.HBM) for a in srcs]
        bufs += [pltpu.with_memory_space_constraint(lax.empty(s.shape, s.dtype), pltpu.HBM) for s in land_shapes]
        sem_shapes += sems
    n_buf, n_sem = len(bufs), len(sem_shapes)

    def body(*refs):
        buf_refs, sem_refs, token = refs[:n_buf], refs[n_buf + len(pins):n_buf + len(pins) + n_sem], refs[-1]
        for (b0, ns, nl, s0, k), group in zip(spans, groups):
            remote, local = group[4](buf_refs[b0:b0 + ns], buf_refs[b0 + ns:b0 + ns + nl], *sem_refs[s0:s0 + k])
            for cp in local + remote:
                cp.start()
        token[...] = jnp.zeros_like(token)

    outs = pl.pallas_call(
        body, name=name,
        out_shape=sem_shapes + [pltpu.HBM(b.shape, b.dtype) for b in bufs] + [jax.ShapeDtypeStruct((SUBLANES, LANES), F32)],
        in_specs=[HBM] * n_buf + [ANY] * len(pins),
        out_specs=[SEM] * n_sem + [HBM] * n_buf + [pl.BlockSpec(memory_space=pltpu.VMEM)],
        input_output_aliases={i: n_sem + i for i in range(n_buf)},
        compiler_params=pltpu.CompilerParams(has_side_effects=SIDE_EFFECT),
    )(*bufs, *pins)
    return [dict(sems=list(outs[s0:s0 + k]), bufs=list(outs[n_sem + b0:n_sem + b0 + ns + nl]), token=outs[-1],
                 build=group[4], ns=ns) for (b0, ns, nl, s0, k), group in zip(spans, groups)]


def _split_wait(name, started, after):
    ns, n_buf, n_sem = started["ns"], len(started["bufs"]), len(started["sems"])

    def body(*refs):
        src_refs, land_refs = refs[:ns], refs[ns:n_buf]
        sems = refs[n_buf:n_buf + n_sem]
        remote, local = started["build"](src_refs, land_refs, *sems)
        for cp in local:
            cp.wait()
        for cp in remote:
            cp.wait_send()
            cp.wait_recv()

    outs = pl.pallas_call(
        body, name=name, out_shape=[pltpu.HBM(b.shape, b.dtype) for b in started["bufs"]],
        in_specs=[HBM] * n_buf + [SEM] * n_sem + [ANY], out_specs=[HBM] * n_buf,
        input_output_aliases={i: i for i in range(n_buf)},
        compiler_params=pltpu.CompilerParams(has_side_effects=SIDE_EFFECT),
    )(*started["bufs"], *started["sems"], after)
    return list(outs[:ns]), list(outs[ns:])


def _gather_group(shards):
    m = len(shards)

    def build(src_refs, land_refs, send_sems, recv_sems, local_sems):
        x, y, c = lax.axis_index("x"), lax.axis_index("y"), lax.axis_index("c")
        remote, local = [], []
        for j in range(m):
            r = src_refs[j].shape[0]
            dst = land_refs[j].at[pl.ds((4 * x + 2 * y + c) * r, r), :]
            local.append(pltpu.make_async_copy(src_refs[j], dst, local_sems.at[j]))
            for k in range(1, N_DEV):
                remote.append(pltpu.make_async_remote_copy(
                    src_ref=src_refs[j], dst_ref=dst, send_sem=send_sems.at[7 * j + k - 1],
                    recv_sem=recv_sems.at[7 * j + k - 1], device_id=_peer(k, x, y, c), device_id_type=MESH))
        return remote, local

    lands = [jax.ShapeDtypeStruct((N_DEV * a.shape[0], a.shape[1]), a.dtype) for a in shards]
    return shards, lands, 7 * m, m, build


def _slots_start(name, a):
    def build(src_refs, land_refs, send_sems, recv_sems, local_sems):
        x, y, c = lax.axis_index("x"), lax.axis_index("y"), lax.axis_index("c")
        dst = land_refs[0].at[4 * x + 2 * y + c]
        local = [pltpu.make_async_copy(src_refs[0], dst, local_sems.at[0])]
        remote = [pltpu.make_async_remote_copy(
            src_ref=src_refs[0], dst_ref=dst, send_sem=send_sems.at[k - 1], recv_sem=recv_sems.at[k - 1],
            device_id=_peer(k, x, y, c), device_id_type=MESH) for k in range(1, N_DEV)]
        return remote, local

    return _split_start(name, [([a], [jax.ShapeDtypeStruct((N_DEV,) + a.shape, a.dtype)], 7, 1, build)])[0]


def _chips_start(name, p):
    _, r, c = p.shape
    nck = r // GRAD_ROW_TILE

    def build(src_refs, land_refs, send_sems, recv_sems):
        x, y, cc = lax.axis_index("x"), lax.axis_index("y"), lax.axis_index("c")
        remote = []
        for k in range(1, 4):
            px = 1 - x if k >> 1 else x
            py = 1 - y if k & 1 else y
            for j in range(nck):
                rows = pl.ds(j * GRAD_ROW_TILE, GRAD_ROW_TILE)
                remote.append(pltpu.make_async_remote_copy(
                    src_ref=src_refs[0].at[2 * px + py, rows], dst_ref=land_refs[0].at[k - 1, rows],
                    send_sem=send_sems.at[(k - 1) * nck + j], recv_sem=recv_sems.at[(k - 1) * nck + j],
                    device_id=(px, py, cc), device_id_type=MESH))
        return remote, []

    return _split_start(name, [([p], [jax.ShapeDtypeStruct((3, r, c), p.dtype)], 3 * nck, 0, build)])[0]


def _chip_sum(name, p, recv, chip):
    _, r, c = p.shape
    tr = _pick(r, 5 * GRAD_ROW_TILE, GRAD_ROW_TILE)

    def body(chip_ref, p_ref, r_ref, o_ref):
        acc = p_ref[...].astype(F32)
        for k in range(3):
            acc = acc + r_ref[k].astype(F32)
        o_ref[...] = acc

    return pl.pallas_call(
        body, name=name,
        grid_spec=pltpu.PrefetchScalarGridSpec(
            num_scalar_prefetch=1, grid=(r // tr,),
            in_specs=[pl.BlockSpec((None, tr, c), lambda i, chip_ref: (chip_ref[0], i, 0)),
                      pl.BlockSpec((3, tr, c), lambda i, chip_ref: (0, i, 0))],
            out_specs=pl.BlockSpec((tr, c), lambda i, chip_ref: (i, 0))),
        out_shape=_out(r, c, F32), compiler_params=_params(("parallel",)),
    )(chip, p, recv)


def _local_step(x, mem, tgt, wt, sm, ev=None):
    t, d = x.shape
    n_mem = mem.shape[0]
    d_pool = sm["pool_scale"].shape[1]
    d_ssm = sm["ssm_d"].shape[1]
    _, sg, sp, sh = sm["ssm_b_re"].shape
    n_state = sg * sp
    gb, gs = {}, {}

    def emit(name, **kw):
        return ev(name, **kw) if ev is not None else None

    n1 = _rms_fwd("ffn1_norm", x, sm["ffn1_norm"])
    emit("ffn1_norm_done", marker=n1)
    def ffn1_down(hid):
        emit("ffn1_up_done", marker=hid)
        return wt["ffn1_w_down"]

    h1, ffn1_saved = _ffn_fwd("ffn1", x, n1, wt["ffn1_w_gate"], wt["ffn1_w_up"], ffn1_down)
    emit("ffn1_fwd_done", marker=h1)
    u = _rms_fwd("mix_norm", h1, sm["mix_norm"])
    d_in = wt["w_in"].shape[0]
    tm, tn = _pick(t, 2048), _pick(d_in, 1408)
    proj = _mm1("in_proj", "nt", u, wt["w_in"], t, d_in, tm, tn, F32)
    off_s = d_pool // d_ssm
    off_gp = (d_pool + d_ssm)
    off_gs = off_gp + d

    pool_w_bf = sm["pool_w"].astype(BF16)
    pooled, pm = _pool_fwd(proj, pool_w_bf, sm["pool_scale"])

    by_p = lambda a: jnp.swapaxes(a, -1, -2).reshape(2 * sg, sh, sp)
    disc_args = [sm["ssm_a_re"].reshape(2 * sg, 1, sp), sm["ssm_a_im"].reshape(2 * sg, 1, sp),
                 sm["ssm_log_dt"].reshape(2 * sg, 1, 1), by_p(sm["ssm_b_re"]), by_p(sm["ssm_b_im"])]
    abr, abi, bbr, bbi = _ssm_disc(disc_args)
    abr2, abi2 = abr.reshape(2, n_state), abi.reshape(2, n_state)
    b_re, b_im, c_re, c_im = _ssm_maps(
        [bbr, bbi, sm["ssm_c_re"].reshape(2 * sg, sh, sp), sm["ssm_c_im"].reshape(2 * sg, sh, sp)], [1.0, 1.0, 1.0, -1.0])
    sp32 = _to_segments(proj[:, d_pool:d_pool + d_ssm])
    xs, y_parts = [], []
    for dr in range(2):
        xr, xi, y_part = _ssm_fwd(f"ssm_fwd{dr}", sp32, b_re[dr], b_im[dr], c_re[dr], c_im[dr], abr2[dr:dr + 1],
                                  abi2[dr:dr + 1], reverse=(dr == 1))
        xs.append((xr, xi))
        y_parts.append(y_part)
    y, ys = _ssm_finish(y_parts[0], y_parts[1], sp32, sm["ssm_d"])
    tmy = _pick(t, 1024)
    emit("mix_in_done", marker=ys)

    tmm, tnm, tnx = _pick(t, 2048), _pick(d, 256), _pick(d, 512)
    gp_spec = _tile(tmm, tnm, off_gp // tnm)
    gs_spec = _tile(tmm, tnm, off_gs // tnm)

    def merge_epi(accs, gpv, gsv):
        z_pool, val, gate = accs
        return (jax.nn.sigmoid(gpv) * z_pool + jax.nn.sigmoid(gsv) * (val * jax.nn.sigmoid(gate)),)

    merged = _mm("mix_merge", "nt", [pm, ys], [wt["w_pool_proj"], wt["w_glu_val"], wt["w_glu_gate"]],
                 [[(0, 0)], [(1, 1)], [(1, 2)]], t, d, tmm, tnm, [(proj, gp_spec), (proj, gs_spec)], merge_epi,
                 [(_out(t, d, BF16), None)])[0]
    res_epi = lambda accs, hin: (hin + accs[0],)
    h2 = _mm("mix_out", "nn", [merged], [wt["w_mix_out"]], [[(0, 0)]], t, d, tmm, tnx, [(h1, _tile(tmm, tnx))],
             res_epi, [(_out(t, d, F32), None)])[0]

    un = _rms_fwd("xattn_norm", h2, sm["xattn_norm"])
    mn = _rms_fwd("mem_norm", mem, sm["mem_norm"])
    emit("mix_done", marker=un)
    q = _mm1("xattn_q", "nn", un, wt["w_q"], t, d, tmm, tnx, BF16)
    kv = _mm1("xattn_kv", "nt", mn, wt["w_kv"], n_mem, 2 * d, n_mem, _pick(2 * d, 512), BF16)
    o = _attn_fwd(q, kv)
    h3 = _mm("xattn_out", "nn", [o], [wt["w_xo"]], [[(0, 0)]], t, d, tmm, tnx, [(h2, _tile(tmm, tnx))],
             res_epi, [(_out(t, d, F32), None)])[0]

    n2 = _rms_fwd("ffn2_norm", h3, sm["ffn2_norm"])
    emit("xattn_done", marker=n2)
    h4, ffn2_saved = _ffn_fwd("ffn2", h3, n2, wt["ffn2_w_gate"], wt["ffn2_w_up"], wt["ffn2_w_down"])

    dh4, dh4_bf, gs["final_norm"], loss = _loss_head(h4, sm["final_norm"], tgt)
    dh3, dh3_bf, gs["ffn2_norm"], gb["ffn2_w_gate"], gb["ffn2_w_up"], gb["ffn2_w_down"] = _ffn_bwd(
        "ffn2", h3, sm["ffn2_norm"], wt["ffn2_w_gate"], wt["ffn2_w_up"], wt["ffn2_w_down"], ffn2_saved, dh4, dh4_bf)

    tw = _pick(d, 1024)
    do = _mm1("xattn_do", "nt", dh3_bf, wt["w_xo"], t, d, tmm, tnx, BF16)
    gb["w_xo"] = _mm1("xattn_dwxo", "tn", o, dh3_bf, d, d, tw, tnx, BF16)
    dq, dkv = _attn_bwd(q, kv, do)
    gb["w_q"] = _mm1("xattn_dwq", "tn", un, dq, d, d, tw, tnx, BF16)
    dun = _mm1("xattn_dun", "nt", dq, wt["w_q"], t, d, tmm, tnx, F32)
    dh2, dh2_bf, gs["xattn_norm"] = _rms_bwd("xattn_norm_bwd", h2, sm["xattn_norm"], dun, dh3)
    gb["w_kv"] = _mm1("xattn_dwkv", "tn", dkv, mn, 2 * d, d, _pick(2 * d, 512), d, BF16)
    dmn = _mm1("xattn_dmn", "nn", dkv, wt["w_kv"], n_mem, d, n_mem, tnx, F32)
    gs["mem_norm"] = _rms_bwd("mem_norm_bwd", mem, sm["mem_norm"], dmn)

    gb["w_mix_out"] = _mm1("mix_dwout", "tn", merged, dh2_bf, d, d, tw, tnx, BF16)

    def merge_bwd_epi(accs, gpv, gsv):
        dmerged, z_pool, val, gate = accs
        sp_, ss_, sg_ = jax.nn.sigmoid(gpv), jax.nn.sigmoid(gsv), jax.nn.sigmoid(gate)
        glu = val * sg_
        dz_pool = dmerged * sp_
        dg_pool = dmerged * z_pool * (sp_ * (1.0 - sp_))
        dz_ssm = dmerged * ss_
        dg_ssm = dmerged * glu * (ss_ * (1.0 - ss_))
        dval = dz_ssm * sg_
        dgate = dz_ssm * glu * (1.0 - sg_)
        return dz_pool, dg_pool, dg_ssm, dval, dgate

    dz_pool, dg_pool, dg_ssm, dval, dgate = _mm(
        "mix_merge_bwd", "nt", [dh2_bf, pm, ys], [wt["w_mix_out"], wt["w_pool_proj"], wt["w_glu_val"], wt["w_glu_gate"]],
        [[(0, 0)], [(1, 1)], [(2, 2)], [(2, 3)]], t, d, tmm, tnm, [(proj, gp_spec), (proj, gs_spec)], merge_bwd_epi,
        [(_out(t, d, BF16), None)] * 5)
    gb["w_pool_proj"] = _mm1("pool_dwproj", "tn", dz_pool, pm, d, d_pool, tw, d_pool, BF16)
    gb["w_glu_val"] = _mm1("glu_dwval", "tn", dval, ys, d, d_ssm, tw, d_ssm, BF16)
    gb["w_glu_gate"] = _mm1("glu_dwgate", "tn", dgate, ys, d, d_ssm, tw, d_ssm, BF16)

    def gelu_bwd_epi(accs, yv):
        _, vjp = jax.vjp(jax.nn.gelu, yv)
        return (vjp(accs[0])[0],)

    dy = _mm("glu_dy", "nn", [dval, dgate], [wt["w_glu_val"], wt["w_glu_gate"]], [[(0, 0), (1, 1)]], t, d_ssm, tmy, d_ssm,
             [(y, _tile(tmy, d_ssm))], gelu_bwd_epi, [(_out(t, d_ssm, F32), None)])[0]
    gs["ssm_d"] = _colsum_prod("ssm_dd", dy, proj, b_coff=off_s)
    dyp = _to_segments(dy)
    d_abr, d_abi, d_bbr, d_bbi, d_cre, d_cim, lams = [], [], [], [], [], [], []
    ts = _pick(n_state, 512)

    def fold_diag(accs):
        first = pl.program_id(1) * (ts // sp)
        row_group = lax.broadcasted_iota(jnp.int32, (d_ssm, sp), 0) // sh
        folded = []
        for acc in accs:
            out = jnp.zeros((d_ssm, sp), F32)
            for k in range(ts // sp):
                out = out + jnp.where(row_group == first + k, acc[:, sp * k:sp * (k + 1)], 0.0)
            folded.append(out)
        return tuple(folded)

    for dr in range(2):
        lr, li, dar, dai = _ssm_bwd(f"ssm_bwd{dr}", dyp, c_re[dr], c_im[dr], xs[dr][0], xs[dr][1], abr2[dr:dr + 1],
                                    abi2[dr:dr + 1], reverse=(dr == 1))
        d_abr.append(dar)
        d_abi.append(dai)
        lams += [lr, li]
        maps = _mm(f"ssm_dmaps{dr}", "tn", [sp32, dyp], [lr, li, xs[dr][0], xs[dr][1]],
                   [[(0, 0)], [(0, 1)], [(1, 2)], [(1, 3)]], d_ssm, n_state, d_ssm, ts, [], fold_diag,
                   [(_out(n_state // ts * d_ssm, sp, F32), pl.BlockSpec((d_ssm, sp), lambda i, j: (j, 0)))] * 4)
        for acc, m in zip((d_bbr, d_bbi, d_cre, d_cim), maps):
            acc.append(jnp.sum(m.reshape(n_state // ts, sg, sh, sp), axis=0))
    ds = _from_segments(_mm(
        "ssm_ds", "nt", lams, [b_re[0], b_im[0], b_re[1], b_im[1]], [[(k, k) for k in range(4)]], t, d_ssm, tmy,
        d_ssm, [(dyp, _tile(tmy, d_ssm)), (sm["ssm_d"], _rowvec(d_ssm))],
        lambda accs, dyv, dv: (dyv * dv + accs[0],), [(_out(t, d_ssm, BF16), None)])[0])
    cots = [jnp.concatenate(d_abr, axis=0).reshape(2 * sg, 1, sp), jnp.concatenate(d_abi, axis=0).reshape(2 * sg, 1, sp),
            jnp.concatenate(d_bbr, axis=0), jnp.concatenate(d_bbi, axis=0)]
    d_are, d_aim, d_ldt, d_bre, d_bim = _ssm_disc_bwd(disc_args, cots)
    gs["ssm_a_re"] = d_are.reshape(2, sg, sp)
    gs["ssm_a_im"] = d_aim.reshape(2, sg, sp)
    gs["ssm_log_dt"] = d_ldt.reshape(2, sg)
    from_p = lambda a: jnp.swapaxes(a.reshape(2, sg, sh, sp), -1, -2)
    gs["ssm_b_re"], gs["ssm_b_im"] = from_p(d_bre), from_p(d_bim)
    gs["ssm_c_re"] = jnp.stack(d_cre, axis=0)
    gs["ssm_c_im"] = -jnp.stack(d_cim, axis=0)

    dpm = _mm1("pool_dpm", "nn", dz_pool, wt["w_pool_proj"], t, d_pool, tmm, _pick(d_pool, 256), F32)
    dp, gs["pool_w"], gs["pool_scale"] = _pool_bwd(pooled, dpm, pool_w_bf, sm["pool_scale"])

    w_in = wt["w_in"]
    parts = [(dp, 0, d_pool), (ds, d_pool, d_ssm), (dg_pool, off_gp, d), (dg_ssm, off_gs, d)]
    w_in_parts = [w_in[o0:o0 + width] for _, o0, width in parts]
    gb["w_in"] = jnp.concatenate(
        [_mm1(f"in_proj_dw{k}", "tn", p_[0], u, p_[2], d, _pick(p_[2], 1024), tnx, BF16) for k, p_ in enumerate(parts)], axis=0)
    pin = emit("grads_main", gb=gb)
    du = _mm("in_proj_du", "nn", [p_[0] for p_ in parts], w_in_parts, [[(k, k) for k in range(4)]], t, d, tmm, tnx, [],
             lambda accs: (accs[0],), [(_out(t, d, F32), None)], after=pin)[0]
    dh1, dh1_bf, gs["mix_norm"] = _rms_bwd("mix_norm_bwd", h1, sm["mix_norm"], du, dh2)
    pin = emit("small_early", gs=gs, loss=loss)

    def ffn1_weights_done(d_wg, d_wu, d_wd):
        gb["ffn1_w_gate"], gb["ffn1_w_up"], gb["ffn1_w_down"] = d_wg, d_wu, d_wd
        return emit("grads_ffn1", gb=gb)

    dx, _, gs["ffn1_norm"], _, _, _ = _ffn_bwd(
        "ffn1", x, sm["ffn1_norm"], wt["ffn1_w_gate"], wt["ffn1_w_up"], wt["ffn1_w_down"], ffn1_saved, dh1, dh1_bf,
        weights_done=ffn1_weights_done, after=pin)
    return loss, dx, gb, gs


WEIGHTS = ["ffn1_norm", "ffn1_w_gate", "ffn1_w_up", "ffn1_w_down", "mix_norm", "w_in", "pool_w", "pool_scale",
           "w_pool_proj", "ssm_a_re", "ssm_a_im", "ssm_log_dt", "ssm_b_re", "ssm_b_im", "ssm_c_re", "ssm_c_im", "ssm_d",
           "w_glu_val", "w_glu_gate", "w_mix_out", "xattn_norm", "mem_norm", "w_q", "w_kv", "w_xo", "ffn2_norm",
           "ffn2_w_gate", "ffn2_w_up", "ffn2_w_down", "final_norm"]
COL_SHARDED = ["ffn1_w_gate", "ffn1_w_up", "w_in", "w_pool_proj", "w_glu_val", "w_glu_gate", "w_kv", "ffn2_w_gate",
               "ffn2_w_up"]
ROW_SHARDED = ["ffn1_w_down", "w_mix_out", "w_q", "w_xo", "ffn2_w_down"]
BIG = [n for n in WEIGHTS if n in COL_SHARDED or n in ROW_SHARDED]
SMALL = [n for n in WEIGHTS if n not in BIG]
FFN1_BIG = ["ffn1_w_gate", "ffn1_w_up", "ffn1_w_down"]
MAIN_BIG = [n for n in BIG if n not in FFN1_BIG]
GATHER_PLAN = [("ffn1_up_done", ["ffn1_w_down"]), ("ffn1_fwd_done", ["w_in"]),
               ("mix_in_done", ["w_pool_proj", "w_glu_val", "w_glu_gate", "w_mix_out"]),
               ("mix_done", ["w_q", "w_kv", "w_xo"]), ("xattn_done", ["ffn2_w_gate", "ffn2_w_up", "ffn2_w_down"])]
MINOR_SWAPPED = ["ssm_b_re", "ssm_b_im"]
LATE_SMALL = "ffn1_norm"
EARLY_SMALL = [n for n in SMALL if n != LATE_SMALL]
PACK_ROWS = SUBLANES * LANES
GRAD_ROW_TILE = 256
ADAMW_STEP_WORDS = 1 << 19


def _to_rows(name, w):
    return w.T if name in COL_SHARDED else w


def _pack_small(vals):
    flat = []
    for v in vals:
        f = v.reshape(-1)
        flat.append(jnp.pad(f, (0, (-f.shape[0]) % PACK_ROWS)))
    total = sum(f.shape[0] for f in flat)
    flat.append(jnp.zeros(((-total) % (GRAD_ROW_TILE * LANES),), F32))
    return jnp.concatenate(flat).reshape(-1, LANES)


def _unpack_small(packed, shapes):
    out, row = [], 0
    for shp in shapes:
        size = math.prod(shp)
        rows = -(-size // PACK_ROWS) * SUBLANES
        out.append(packed[row:row + rows].reshape(-1)[:size].reshape(shp))
        row += rows
    return out


def kernel(x, mem, ffn1_norm, ffn1_w_gate, ffn1_w_up, ffn1_w_down, mix_norm, w_in, pool_w, pool_scale, w_pool_proj, ssm_a_re, ssm_a_im, ssm_log_dt, ssm_b_re, ssm_b_im, ssm_c_re, ssm_c_im, ssm_d, w_glu_val, w_glu_gate, w_mix_out, xattn_norm, mem_norm, w_q, w_kv, w_xo, ffn2_norm, ffn2_w_gate, ffn2_w_up, ffn2_w_down, final_norm, loss_target, m_ffn1_norm, m_ffn1_w_gate, m_ffn1_w_up, m_ffn1_w_down, m_mix_norm, m_w_in, m_pool_w, m_pool_scale, m_w_pool_proj, m_ssm_a_re, m_ssm_a_im, m_ssm_log_dt, m_ssm_b_re, m_ssm_b_im, m_ssm_c_re, m_ssm_c_im, m_ssm_d, m_w_glu_val, m_w_glu_gate, m_w_mix_out, m_xattn_norm, m_mem_norm, m_w_q, m_w_kv, m_w_xo, m_ffn2_norm, m_ffn2_w_gate, m_ffn2_w_up, m_ffn2_w_down, m_final_norm, v_ffn1_norm, v_ffn1_w_gate, v_ffn1_w_up, v_ffn1_w_down, v_mix_norm, v_w_in, v_pool_w, v_pool_scale, v_w_pool_proj, v_ssm_a_re, v_ssm_a_im, v_ssm_log_dt, v_ssm_b_re, v_ssm_b_im, v_ssm_c_re, v_ssm_c_im, v_ssm_d, v_w_glu_val, v_w_glu_gate, v_w_mix_out, v_xattn_norm, v_mem_norm, v_w_q, v_w_kv, v_w_xo, v_ffn2_norm, v_ffn2_w_gate, v_ffn2_w_up, v_ffn2_w_down, v_final_norm):
    given = dict(locals())
    wts = {n: given[n] for n in WEIGHTS}
    moms = {n: (given["m_" + n], given["v_" + n]) for n in WEIGHTS}
    x2, mem2, tgt2 = x[0], mem[0], loss_target[0]
    d = x2.shape[1]
    chip = (2 * lax.axis_index("x") + lax.axis_index("y")).astype(jnp.int32).reshape(1)

    def full_form(n, f):
        shard = wts[n][0].shape
        return f.reshape(N_DEV * shard[1], shard[0]) if n in COL_SHARDED else f.reshape(N_DEV * shard[0], shard[1])

    shards = {n: _to_rows(n, wts[n][0]).astype(BF16) for n in BIG}
    first = FFN1_BIG[:2]
    wt = {n: full_form(n, f) for n, f in zip(first, _allgather("weight_allgather_first", [shards[n] for n in first]))}
    started = _split_start("weight_gather_start", [_gather_group([shards[n] for n in names]) for _, names in GATHER_PLAN],
                           after=wt[first[0]])
    gathers = {event: (names, st) for (event, names), st in zip(GATHER_PLAN, started)}
    sm = {n: (wts[n].reshape(1, -1) if wts[n].ndim <= 2 else wts[n][0]) for n in SMALL}
    sm["ffn1_norm"] = sm["ffn1_norm"] + started[0]["token"][0, 0]

    pending = {}

    def reduce_start(tag, names, gb):
        blocks = [gb[n].reshape(N_DEV, -1, d) for n in names]
        pad_rows = (-sum(b.shape[1] for b in blocks)) % GRAD_ROW_TILE
        pad = [jnp.zeros((N_DEV, pad_rows, d), BF16)] if pad_rows else []
        started = _cores_start("grad_exchange_cores_start_" + tag, blocks + pad)
        own = jnp.concatenate([lax.dynamic_index_in_dim(b.reshape(4, 2, b.shape[1], d), lax.axis_index("c"), 1, False)
                               for b in started["bufs"][:len(blocks + pad)]], axis=1)
        _, (recv,) = _split_wait("grad_exchange_cores_wait_" + tag, started, own)
        rows_all = own.shape[1]
        pair = _ew("grad_pair_sum_" + tag, lambda a, b: (a.astype(F32) + b.astype(F32),),
                   [own.reshape(-1, d), recv.reshape(-1, d)], [BF16], rows_pref=5 * GRAD_ROW_TILE)[0]
        pair = pair.reshape(4, rows_all, d)
        pending[tag] = (pair, _chips_start("grad_exchange_chips_start_" + tag, pair), [b.shape[1] for b in blocks])
        return pending[tag][1]["token"]

    def reduce_finish(tag, after):
        _, started, rows = pending[tag]
        (pair,), (recv,) = _split_wait("grad_exchange_chips_wait_" + tag, started, after)
        return _chip_sum("grad_chip_sum_" + tag, pair, recv, chip), rows

    def ev(name, gb=None, gs=None, loss=None, marker=None):
        if name in gathers:
            names, started = gathers[name]
            for n, f in zip(names, _split_wait("weight_gather_wait_" + name, started, marker)[1]):
                wt[n] = full_form(n, f)
        elif name == "grads_main":
            return reduce_start("main", MAIN_BIG, gb)
        elif name == "small_early":
            pending["small"] = _slots_start("small_gather_start", _pack_small([gs[n] for n in EARLY_SMALL] + [loss[:, :1]]))
            return pending["small"]["token"]
        elif name == "grads_ffn1":
            return reduce_start("ffn1", FFN1_BIG, gb)
        return None

    _, dx, _, gs = _local_step(x2, mem2, tgt2, wt, sm, ev)

    grads = {}
    for tag, names in (("main", MAIN_BIG), ("ffn1", FFN1_BIG)):
        g_rows, rows = reduce_finish(tag, dx)
        off = 0
        for n, r in zip(names, rows):
            shard = wts[n].shape
            grads[n] = g_rows[off:off + r].reshape((shard[2], shard[1]) if n in COL_SHARDED else shard[1:])
            off += r
    small_sum = _sum_slots("small_sum", _split_wait("small_gather_wait", pending["small"], dx)[1][0], F32)
    late = _allgather("small_allgather_late", [gs[LATE_SMALL].reshape(-1, LANES)])[0]
    late_sum = _sum_slots("small_sum_late", late.reshape(N_DEV, -1, LANES), F32)
    vals = _unpack_small(small_sum, [wts[n].shape for n in EARLY_SMALL] + [(1, 1)])
    total_loss = vals[-1].reshape(())
    def flat(n, a):
        a = a.reshape(wts[n].shape)
        a = jnp.swapaxes(a, -1, -2) if n in MINOR_SWAPPED else a
        return a.reshape(-1, a.shape[-1])

    def unflat(n, a):
        shape = wts[n].shape
        if n in MINOR_SWAPPED:
            return jnp.swapaxes(a.reshape(shape[:-2] + (shape[-1], shape[-2])), -1, -2)
        return a.reshape(shape)

    for n, g_full in zip(EARLY_SMALL + [LATE_SMALL], vals[:-1] + [late_sum]):
        grads[n] = flat(n, g_full)

    out_g, out_d, out_m, out_v = {}, {}, {}, {}
    by_shape = {}
    for n in WEIGHTS:
        by_shape.setdefault((flat(n, wts[n]).shape, n in COL_SHARDED), []).append(n)
    for (_, transposed), names in by_shape.items():
        items = [(flat(n, wts[n]), grads[n], flat(n, moms[n][0]), flat(n, moms[n][1])) for n in names]
        for n, res in zip(names, _adamw_group("adamw_" + names[0], items, transposed)):
            out_d[n], out_m[n], out_v[n], out_g[n] = (unflat(n, a) for a in res)

    return (total_loss, dx[None], *[out_g[n] for n in WEIGHTS], *[out_d[n] for n in WEIGHTS],
            *[out_m[n] for n in WEIGHTS], *[out_v[n] for n in WEIGHTS])
```

```python
import math

import jax
import jax.numpy as jnp
from jax import lax
from jax.experimental import pallas as pl
from jax.experimental.pallas import tpu as pltpu

F32 = jnp.float32
BF16 = jnp.bfloat16
EPS = 1e-6
N_XHEADS = 4
POOL_WINDOWS = (2, 4, 8, 16)
ADAM_LR = 0.001
ADAM_B1 = 0.9
ADAM_B2 = 0.999
ADAM_EPS = 1e-08
ADAM_WD = 0.01
ADAM_STEP = 10
N_DEV = 8
VMEM_LIMIT_V7X = 48 * 1024 * 1024
LANES = 128
SUBLANES = 8
SUB_ROWS = 256
POOL_PAD = 16
MESH = pl.DeviceIdType.MESH
ANY = pl.BlockSpec(memory_space=pl.ANY)
HBM = pl.BlockSpec(memory_space=pltpu.HBM)
SEM = pl.BlockSpec(memory_space=pltpu.SEMAPHORE)
SIDE_EFFECT = pltpu.SideEffectType.DATAFLOW_SIDE_EFFECTING

_DIMS = {
    "nt": (((1,), (1,)), ((), ())),
    "nn": (((1,), (0,)), ((), ())),
    "tn": (((0,), (0,)), ((), ())),
}


def _pick(dim, pref, mult=LANES):
    if dim <= pref:
        return dim
    for t in range(pref - pref % mult, 0, -mult):
        if dim % t == 0:
            return t
    return dim


def _params(sem):
    return pltpu.CompilerParams(dimension_semantics=sem, vmem_limit_bytes=VMEM_LIMIT_V7X)


def _tile(tm, tn, coff=0):
    return pl.BlockSpec((tm, tn), lambda i, j: (i, j + coff))


def _rowvec(tn, coff=0):
    return pl.BlockSpec((1, tn), lambda i, j: (0, j + coff))


def _out(m, n, dtype):
    return jax.ShapeDtypeStruct((m, n), dtype)


def _mm(name, form, a_list, b_list, groups, m, n, tm, tn, extras, epilogue, outs, after=None, sub=SUB_ROWS):
    na, nb, ne = len(a_list), len(b_list), len(extras)
    pins = [] if after is None else [after]
    step = tm if (sub is None or form == "tn" or tm % sub) else sub

    def a_spec(a):
        if form == "tn":
            return pl.BlockSpec((a.shape[0], tm), lambda i, j: (0, i))
        return pl.BlockSpec((tm, a.shape[1]), lambda i, j: (i, 0))

    def b_spec(b):
        if form == "nt":
            return pl.BlockSpec((tn, b.shape[1]), lambda i, j: (j, 0))
        return pl.BlockSpec((b.shape[0], tn), lambda i, j: (0, j))

    def body(*refs):
        a_refs, b_refs = refs[:na], refs[na:na + nb]
        e_refs, o_refs = refs[na + nb:na + nb + ne], refs[na + nb + ne + len(pins):]
        b_vals = {}
        for s0 in range(0, tm, step):
            rows = slice(None) if step == tm else pl.ds(s0, step)
            a_vals, accs = {}, []
            for group in groups:
                acc = None
                for ai, bi in group:
                    if ai not in a_vals:
                        a_vals[ai] = (a_refs[ai][...] if form == "tn" else a_refs[ai][rows, :]).astype(BF16)
                    if bi not in b_vals:
                        b_vals[bi] = b_refs[bi][...].astype(BF16)
                    d = lax.dot_general(a_vals[ai], b_vals[bi], _DIMS[form], preferred_element_type=F32)
                    acc = d if acc is None else acc + d
                accs.append(acc)
            res = epilogue(accs, *[e[rows, :] if e.shape[0] == tm else e[...] for e in e_refs])
            for o_ref, r in zip(o_refs, res):
                o_ref[rows, :] = r.astype(o_ref.dtype)

    out_specs = [_tile(tm, tn) if s is None else s for _, s in outs]
    res = pl.pallas_call(
        body, name=name, grid=(m // tm, n // tn),
        in_specs=[a_spec(a) for a in a_list] + [b_spec(b) for b in b_list] + [s for _, s in extras] + [ANY] * len(pins),
        out_specs=out_specs, out_shape=[o for o, _ in outs],
        compiler_params=_params(("parallel", "parallel")),
    )(*a_list, *b_list, *[e for e, _ in extras], *pins)
    return res


def _mm1(name, form, a, b, m, n, tm, tn, dtype, scale=None):
    epi = (lambda accs: (accs[0],)) if scale is None else (lambda accs: (accs[0] * scale,))
    return _mm(name, form, [a], [b], [[(0, 0)]], m, n, tm, tn, [], epi, [(_out(m, n, dtype), None)])[0]


def _rms_fwd(name, h, g):
    t, d = h.shape
    tm = _pick(t, 1024, SUBLANES)

    def body(h_ref, g_ref, n_ref):
        hv = h_ref[...]
        r = lax.rsqrt(jnp.mean(hv * hv, axis=-1, keepdims=True) + EPS)
        n_ref[...] = ((hv * r) * g_ref[...]).astype(BF16)

    return pl.pallas_call(
        body, name=name, grid=(t // tm,),
        in_specs=[pl.BlockSpec((tm, d), lambda i: (i, 0)), pl.BlockSpec((1, d), lambda i: (0, 0))],
        out_specs=pl.BlockSpec((tm, d), lambda i: (i, 0)), out_shape=_out(t, d, BF16),
        compiler_params=_params(("parallel",)),
    )(h, g)


def _rms_bwd(name, h, g, dn, dres=None):
    t, d = h.shape
    tm = _pick(t, 1024, SUBLANES)
    need_dh = dres is not None

    def body(*refs):
        if need_dh:
            h_ref, g_ref, dn_ref, dres_ref, dh_ref, dhb_ref, dg_ref = refs
        else:
            h_ref, g_ref, dn_ref, dg_ref = refs
        hv = h_ref[...]
        r = lax.rsqrt(jnp.mean(hv * hv, axis=-1, keepdims=True) + EPS)
        nh = hv * r
        dnv = dn_ref[...].astype(F32)

        @pl.when(pl.program_id(0) == 0)
        def _():
            dg_ref[...] = jnp.zeros_like(dg_ref)

        dg_ref[...] += jnp.sum(dnv * nh, axis=0, keepdims=True)
        if need_dh:
            dng = dnv * g_ref[...]
            dh = dres_ref[...] + r * (dng - nh * jnp.mean(dng * nh, axis=-1, keepdims=True))
            dh_ref[...] = dh
            dhb_ref[...] = dh.astype(BF16)

    row = pl.BlockSpec((tm, d), lambda i: (i, 0))
    vec = pl.BlockSpec((1, d), lambda i: (0, 0))
    if need_dh:
        return pl.pallas_call(
            body, name=name, grid=(t // tm,), in_specs=[row, vec, row, row], out_specs=[row, row, vec],
            out_shape=[_out(t, d, F32), _out(t, d, BF16), _out(1, d, F32)], compiler_params=_params(("arbitrary",)),
        )(h, g, dn, dres)
    return pl.pallas_call(
        body, name=name, grid=(t // tm,), in_specs=[row, vec, row], out_specs=vec,
        out_shape=_out(1, d, F32), compiler_params=_params(("arbitrary",)),
    )(h, g, dn)


def _loss_head(h, g, tgt):
    t, d = h.shape
    tm = _pick(t, 1024, SUBLANES)

    def body(h_ref, g_ref, t_ref, dh_ref, dhb_ref, dg_ref, loss_ref):
        hv = h_ref[...]
        r = lax.rsqrt(jnp.mean(hv * hv, axis=-1, keepdims=True) + EPS)
        nh = hv * r
        err = nh * g_ref[...] - t_ref[...]

        @pl.when(pl.program_id(0) == 0)
        def _():
            dg_ref[...] = jnp.zeros_like(dg_ref)
            loss_ref[...] = jnp.zeros_like(loss_ref)

        per_row = jnp.mean(err * err, axis=-1, keepdims=True)
        loss_ref[...] += 0.5 * jnp.sum(per_row, axis=0, keepdims=True)
        dy = err * (1.0 / d)
        dg_ref[...] += jnp.sum(dy * nh, axis=0, keepdims=True)
        dng = dy * g_ref[...]
        dh = r * (dng - nh * jnp.mean(dng * nh, axis=-1, keepdims=True))
        dh_ref[...] = dh
        dhb_ref[...] = dh.astype(BF16)

    row = pl.BlockSpec((tm, d), lambda i: (i, 0))
    vec = pl.BlockSpec((1, d), lambda i: (0, 0))
    return pl.pallas_call(
        body, name="loss_head", grid=(t // tm,), in_specs=[row, vec, row],
        out_specs=[row, row, vec, pl.BlockSpec((1, LANES), lambda i: (0, 0))],
        out_shape=[_out(t, d, F32), _out(t, d, BF16), _out(1, d, F32), _out(1, LANES, F32)],
        compiler_params=_params(("arbitrary",)),
    )(h, g, tgt)


def _ffn_fwd(tag, h, n, wg_t, wu_t, wd):
    t, d = h.shape
    f = wg_t.shape[0]
    tm, tn = _pick(t, 1024), _pick(f, 1408)

    def up_epi(accs):
        a, b = accs
        return a, b, (a * jax.nn.sigmoid(a)) * b

    a, b, hid = _mm(tag + "_up", "nt", [n], [wg_t, wu_t], [[(0, 0)], [(0, 1)]], t, f, tm, tn, [], up_epi,
                    [(_out(t, f, BF16), None)] * 3)
    if callable(wd):
        wd = wd(hid)
    tm2, tn2 = _pick(t, 1024), _pick(d, 512)
    h_out = _mm(tag + "_down", "nn", [hid], [wd], [[(0, 0)]], t, d, tm2, tn2, [(h, _tile(tm2, tn2))],
                lambda accs, hin: (hin + 0.5 * accs[0],), [(_out(t, d, F32), None)])[0]
    return h_out, (n, a, b, hid)


def _ffn_bwd(tag, h, g, wg_t, wu_t, wd, saved, dh, dh_bf, weights_done=None, after=None):
    n, a, b, hid = saved
    t, d = h.shape
    f = wd.shape[0]
    tm, tn = _pick(t, 1024), _pick(f, 1408)

    def hid_epi(accs, av, bv):
        dhid = 0.5 * accs[0]
        av, bv = av.astype(F32), bv.astype(F32)
        sig = jax.nn.sigmoid(av)
        da = dhid * bv * (sig * (1.0 + av * (1.0 - sig)))
        db = dhid * (av * sig)
        return da, db

    da, db = _mm(tag + "_bwd_hid", "nt", [dh_bf], [wd], [[(0, 0)]], t, f, tm, tn,
                 [(a, _tile(tm, tn)), (b, _tile(tm, tn))], hid_epi, [(_out(t, f, BF16), None)] * 2, after=after)
    tw, tnw = _pick(f, 1408), _pick(d, 512)
    d_wd = _mm1(tag + "_dwd", "tn", hid, dh_bf, f, d, tw, tnw, BF16, scale=0.5)
    d_wg = _mm1(tag + "_dwg", "tn", da, n, f, d, tw, tnw, BF16)
    d_wu = _mm1(tag + "_dwu", "tn", db, n, f, d, tw, tnw, BF16)
    pin = weights_done(d_wg, d_wu, d_wd) if weights_done is not None else None
    tm2, tn2 = _pick(t, 1024), _pick(d, 512)
    dn = _mm(tag + "_dn", "nn", [da, db], [wg_t, wu_t], [[(0, 0), (1, 1)]], t, d, tm2, tn2, [],
             lambda accs: (accs[0],), [(_out(t, d, F32), None)], after=pin)[0]
    dh_in, dh_in_bf, dg = _rms_bwd(tag + "_norm_bwd", h, g, dn, dh)
    return dh_in, dh_in_bf, dg, d_wg, d_wu, d_wd


def _window_sum(win, offsets):
    n = win.shape[0]
    acc = None
    for j in offsets:
        term = win if j == 0 else pltpu.roll(win, (-j) % n, 0)
        acc = term if acc is None else acc + term
    return acc


def _pool_counts(r0, ch, c, left, right, t):
    pos = r0 + lax.broadcasted_iota(jnp.int32, (ch, c), 0)
    return (jnp.minimum(pos + right + 1, t) - jnp.maximum(pos - left, 0)).astype(F32)


def _pool_fwd(proj, pool_w_bf, pool_scale):
    t = proj.shape[0]
    ng, c, _ = pool_w_bf.shape
    ch = _pick(t, 512, SUBLANES)
    pad = POOL_PAD

    def body(p_ref, w_ref, s_ref, pooled_ref, pm_ref, buf):
        grp = pl.program_id(0)
        buf[pl.ds(0, pad), :] = jnp.zeros((pad, c), F32)
        buf[pl.ds(pad + t, pad), :] = jnp.zeros((pad, c), F32)

        def fill(ci, carry):
            r0 = pl.multiple_of(ci * ch, SUBLANES)
            buf[pl.ds(pl.multiple_of(r0 + pad, SUBLANES), ch), :] = p_ref[pl.ds(r0, ch), :]
            return carry

        lax.fori_loop(0, t // ch, fill, 0)
        for gi, w in enumerate(POOL_WINDOWS):
            left = w // 2
            right = w - 1 - left

            @pl.when(grp == gi)
            def _(left=left, right=right):
                def chunk(ci, carry):
                    r0 = pl.multiple_of(ci * ch, SUBLANES)
                    win = buf[pl.ds(r0, ch + 2 * pad), :]
                    s = _window_sum(win, range(-left, right + 1))[pad:pad + ch]
                    pooled = s / _pool_counts(r0, ch, c, left, right, t) - win[pad:pad + ch]
                    pooled_bf = pooled.astype(BF16)
                    mixed = jnp.dot(pooled_bf, w_ref[0], preferred_element_type=F32)
                    pooled_ref[pl.ds(r0, ch), :] = pooled_bf
                    pm_ref[pl.ds(r0, ch), :] = (mixed * s_ref[...]).astype(BF16)
                    return carry

                lax.fori_loop(0, t // ch, chunk, 0)

    col = pl.BlockSpec((t, c), lambda g: (0, g))
    return pl.pallas_call(
        body, name="pool_fwd", grid=(ng,),
        in_specs=[col, pl.BlockSpec((1, c, c), lambda g: (g, 0, 0)), pl.BlockSpec((1, c), lambda g: (0, g))],
        out_specs=[col, col], out_shape=[_out(t, ng * c, BF16), _out(t, ng * c, BF16)],
        scratch_shapes=[pltpu.VMEM((t + 2 * pad, c), F32)],
        compiler_params=_params(("parallel",)),
    )(proj, pool_w_bf, pool_scale)


def _pool_bwd(pooled, dpm, pool_w_bf, pool_scale):
    t = pooled.shape[0]
    ng, c, _ = pool_w_bf.shape
    ch = _pick(t, 512, SUBLANES)
    pad = POOL_PAD

    def body(pooled_ref, dpm_ref, w_ref, s_ref, dp_ref, dw_ref, ds_ref, buf, raw):
        grp = pl.program_id(0)
        buf[pl.ds(0, pad), :] = jnp.zeros((pad, c), F32)
        buf[pl.ds(pad + t, pad), :] = jnp.zeros((pad, c), F32)
        dw_ref[...] = jnp.zeros_like(dw_ref)
        ds_ref[...] = jnp.zeros_like(ds_ref)
        for gi, w in enumerate(POOL_WINDOWS):
            left = w // 2
            right = w - 1 - left

            @pl.when(grp == gi)
            def _(left=left, right=right):
                def first(ci, carry):
                    r0 = pl.multiple_of(ci * ch, SUBLANES)
                    pv = pooled_ref[pl.ds(r0, ch), :]
                    dpm_v = dpm_ref[pl.ds(r0, ch), :]
                    mixed = jnp.dot(pv, w_ref[0], preferred_element_type=F32)
                    ds_ref[...] += jnp.sum(dpm_v * mixed, axis=0, keepdims=True)
                    dmixed = (dpm_v * s_ref[...]).astype(BF16)
                    dw_ref[0] += lax.dot_general(pv, dmixed, _DIMS["tn"], preferred_element_type=F32)
                    dpooled = lax.dot_general(dmixed, w_ref[0], _DIMS["nt"], preferred_element_type=F32)
                    raw[pl.ds(r0, ch), :] = dpooled
                    buf[pl.ds(pl.multiple_of(r0 + pad, SUBLANES), ch), :] = (
                        dpooled / _pool_counts(r0, ch, c, left, right, t))
                    return carry

                lax.fori_loop(0, t // ch, first, 0)

                def second(ci, carry):
                    r0 = pl.multiple_of(ci * ch, SUBLANES)
                    win = buf[pl.ds(r0, ch + 2 * pad), :]
                    s = _window_sum(win, range(-right, left + 1))[pad:pad + ch]
                    dp_ref[pl.ds(r0, ch), :] = (s - raw[pl.ds(r0, ch), :]).astype(BF16)
                    return carry

                lax.fori_loop(0, t // ch, second, 0)

    col = pl.BlockSpec((t, c), lambda g: (0, g))
    return pl.pallas_call(
        body, name="pool_bwd", grid=(ng,),
        in_specs=[col, col, pl.BlockSpec((1, c, c), lambda g: (g, 0, 0)), pl.BlockSpec((1, c), lambda g: (0, g))],
        out_specs=[col, pl.BlockSpec((1, c, c), lambda g: (g, 0, 0)), pl.BlockSpec((1, c), lambda g: (0, g))],
        out_shape=[_out(t, ng * c, BF16), jax.ShapeDtypeStruct((ng, c, c), F32), _out(1, ng * c, F32)],
        scratch_shapes=[pltpu.VMEM((t + 2 * pad, c), F32), pltpu.VMEM((t, c), F32)],
        compiler_params=_params(("parallel",)),
    )(pooled, dpm, pool_w_bf, pool_scale)


def _discretise(a_re, a_im, log_dt, b_re, b_im):
    dt = jnp.exp(log_dt)
    mag = jnp.exp(dt * a_re)
    ang = dt * a_im
    abr = mag * jnp.cos(ang)
    abi = mag * jnp.sin(ang)
    den = a_re * a_re + a_im * a_im
    nr = abr - 1.0
    qr = (nr * a_re + abi * a_im) / den
    qi = (abi * a_re - nr * a_im) / den
    return abr, abi, qr * b_re - qi * b_im, qr * b_im + qi * b_re


def _ssm_disc(args):
    def body(ar, ai, ld, br, bi, o1, o2, o3, o4):
        res = _discretise(ar[...], ai[...], ld[...], br[...], bi[...])
        for o, r in zip((o1, o2, o3, o4), res):
            o[...] = r

    like = lambda a: jax.ShapeDtypeStruct(a.shape, F32)
    return pl.pallas_call(
        body, name="ssm_disc", out_shape=[like(args[0]), like(args[0]), like(args[3]), like(args[3])],
    )(*args)


def _ssm_disc_bwd(args, cots):
    def body(ar, ai, ld, br, bi, c1, c2, c3, c4, o1, o2, o3, o4, o5):
        _, vjp = jax.vjp(_discretise, ar[...], ai[...], ld[...], br[...], bi[...])
        res = vjp((c1[...], c2[...], c3[...], c4[...]))
        for o, r in zip((o1, o2, o3, o4, o5), res):
            o[...] = r

    return pl.pallas_call(
        body, name="ssm_disc_bwd", out_shape=[jax.ShapeDtypeStruct(a.shape, F32) for a in args],
    )(*args, *cots)


def _cmul(pr, pi, qr, qi):
    return pr * qr - pi * qi, pr * qi + pi * qr


def _cpow(pr, pi, n):
    rr, ri = None, None
    while n:
        if n & 1:
            rr, ri = (pr, pi) if rr is None else _cmul(rr, ri, pr, pi)
        n >>= 1
        if n:
            pr, pi = _cmul(pr, pi, pr, pi)
    return rr, ri


def _segment_carry(er, ei, pr, pi, reverse):
    row = lax.broadcasted_iota(jnp.int32, er.shape, 0)
    cr, ci = jnp.zeros_like(er), jnp.zeros_like(ei)
    for _ in range(SUBLANES - 1):
        tr = er + pr * cr - pi * ci
        ti = ei + pr * ci + pi * cr
        if reverse:
            keep, shift = row < SUBLANES - 1, SUBLANES - 1
        else:
            keep, shift = row >= 1, 1
        cr = jnp.where(keep, pltpu.roll(tr, shift, 0), 0.0)
        ci = jnp.where(keep, pltpu.roll(ti, shift, 0), 0.0)
    return cr, ci


def _ssm_fwd(name, sp, b_re, b_im, c_re, c_im, ar, ai, reverse):
    t, c = sp.shape
    s = ar.shape[1]
    w = _pick(s, 512)
    ch = _pick(t, 1024, SUBLANES)
    n_ch, gpc, steps = t // ch, ch // SUBLANES, t // SUBLANES

    def body(sp_ref, bre_ref, bim_ref, cre_ref, cim_ref, ar_ref, ai_ref, xr_ref, xi_ref, y_ref, ur, ui, xbr, xbi):
        a_r = jnp.broadcast_to(ar_ref[...], (SUBLANES, w))
        a_i = jnp.broadcast_to(ai_ref[...], (SUBLANES, w))

        @pl.when(pl.program_id(0) == 0)
        def _():
            y_ref[...] = jnp.zeros_like(y_ref)

        def sweep(h0, store):
            def chunk(k, h):
                ci = n_ch - 1 - k if reverse else k
                rows = pl.ds(pl.multiple_of(ci * ch, ch), ch)
                spv = sp_ref[rows, :].astype(BF16)
                ur[...] = jnp.dot(spv, bre_ref[...], preferred_element_type=F32)
                ui[...] = jnp.dot(spv, bim_ref[...], preferred_element_type=F32)

                def group(g, hh):
                    gi = gpc - 1 - g if reverse else g
                    r0 = pl.multiple_of(gi * SUBLANES, SUBLANES)
                    hr, hi = hh
                    nr = a_r * hr - a_i * hi + ur[pl.ds(r0, SUBLANES), :]
                    ni = a_r * hi + a_i * hr + ui[pl.ds(r0, SUBLANES), :]
                    if store:
                        xbr[pl.ds(r0, SUBLANES), :] = nr
                        xbi[pl.ds(r0, SUBLANES), :] = ni
                    return nr, ni

                h = lax.fori_loop(0, gpc, group, h)
                if store:
                    xr16, xi16 = xbr[...].astype(BF16), xbi[...].astype(BF16)
                    xr_ref[rows, :] = xr16
                    xi_ref[rows, :] = xi16
                    y_ref[rows, :] += (lax.dot_general(xr16, cre_ref[...], _DIMS["nt"], preferred_element_type=F32)
                                       + lax.dot_general(xi16, cim_ref[...], _DIMS["nt"], preferred_element_type=F32))
                return h

            return lax.fori_loop(0, n_ch, chunk, h0)

        zero = jnp.zeros((SUBLANES, w), F32)
        er, ei = sweep((zero, zero), False)
        pr, pi = _cpow(ar_ref[...], ai_ref[...], steps)
        sweep(_segment_carry(er, ei, pr, pi, reverse), True)

    col = lambda i: (0, i)
    return pl.pallas_call(
        body, name=name, grid=(s // w,),
        in_specs=[pl.BlockSpec((t, c), lambda i: (0, 0))] + [pl.BlockSpec((c, w), col)] * 4
        + [pl.BlockSpec((1, w), col)] * 2,
        out_specs=[pl.BlockSpec((t, w), col), pl.BlockSpec((t, w), col), pl.BlockSpec((t, c), lambda i: (0, 0))],
        out_shape=[_out(t, s, BF16), _out(t, s, BF16), _out(t, c, F32)],
        scratch_shapes=[pltpu.VMEM((ch, w), F32)] * 4,
        compiler_params=_params(("arbitrary",)),
    )(sp, b_re, b_im, c_re, c_im, ar, ai)


def _ssm_bwd(name, dyp, c_re, c_im, xr, xi, ar, ai, reverse):
    t, c = dyp.shape
    s = ar.shape[1]
    w = _pick(s, 512)
    ch = _pick(t, 512, SUBLANES)
    n_ch, gpc, steps = t // ch, ch // SUBLANES, t // SUBLANES
    back = not reverse
    edge = 2 * SUBLANES

    def body(dy_ref, cre_ref, cim_ref, xr_ref, xi_ref, ar_ref, ai_ref, lr_ref, li_ref, dar_ref, dai_ref,
             gr, gi_, lbr, lbi, xbr, xbi):
        a_r = jnp.broadcast_to(ar_ref[...], (SUBLANES, w))
        a_i = -jnp.broadcast_to(ai_ref[...], (SUBLANES, w))
        row = lax.broadcasted_iota(jnp.int32, (SUBLANES, w), 0)

        def neighbours(ci, x_ref, buf):
            rows = pl.ds(pl.multiple_of(ci * ch, ch), ch)
            if reverse:
                buf[pl.ds(0, ch), :] = x_ref[rows, :].astype(F32)
                nxt = x_ref[pl.ds(pl.multiple_of(jnp.minimum(ci + 1, n_ch - 1) * ch, ch), edge), :].astype(F32)[:SUBLANES]
                first = x_ref[pl.ds(0, edge), :].astype(F32)[:SUBLANES]
                wrap = jnp.where(row < SUBLANES - 1, pltpu.roll(first, SUBLANES - 1, 0), 0.0)
                buf[pl.ds(ch, SUBLANES), :] = jnp.where(ci == n_ch - 1, wrap, nxt)
            else:
                buf[pl.ds(SUBLANES, ch), :] = x_ref[rows, :].astype(F32)
                prv = x_ref[pl.ds(pl.multiple_of(jnp.maximum(ci * ch - edge, 0), edge), edge), :].astype(F32)[SUBLANES:]
                last = x_ref[pl.ds(t - edge, edge), :].astype(F32)[SUBLANES:]
                wrap = jnp.where(row >= 1, pltpu.roll(last, 1, 0), 0.0)
                buf[pl.ds(0, SUBLANES), :] = jnp.where(ci == 0, wrap, prv)

        def sweep(h0, store):
            def chunk(k, carry):
                ci = n_ch - 1 - k if back else k
                rows = pl.ds(pl.multiple_of(ci * ch, ch), ch)
                dyv = dy_ref[rows, :].astype(BF16)
                gr[...] = jnp.dot(dyv, cre_ref[...], preferred_element_type=F32)
                gi_[...] = jnp.dot(dyv, cim_ref[...], preferred_element_type=F32)
                if store:
                    neighbours(ci, xr_ref, xbr)
                    neighbours(ci, xi_ref, xbi)

                def group(g, cc):
                    gidx = gpc - 1 - g if back else g
                    r0 = pl.multiple_of(gidx * SUBLANES, SUBLANES)
                    hr, hi = cc[0], cc[1]
                    nr = a_r * hr - a_i * hi + gr[pl.ds(r0, SUBLANES), :]
                    ni = a_r * hi + a_i * hr + gi_[pl.ds(r0, SUBLANES), :]
                    if not store:
                        return nr, ni
                    lbr[pl.ds(r0, SUBLANES), :] = nr
                    lbi[pl.ds(r0, SUBLANES), :] = ni
                    x0 = pl.multiple_of(r0 + SUBLANES, SUBLANES) if reverse else r0
                    xpr, xpi = xbr[pl.ds(x0, SUBLANES), :], xbi[pl.ds(x0, SUBLANES), :]
                    return nr, ni, cc[2] + nr * xpr + ni * xpi, cc[3] + ni * xpr - nr * xpi

                carry = lax.fori_loop(0, gpc, group, carry)
                if store:
                    lr_ref[rows, :] = lbr[...].astype(BF16)
                    li_ref[rows, :] = lbi[...].astype(BF16)
                return carry

            return lax.fori_loop(0, n_ch, chunk, h0)

        zero = jnp.zeros((SUBLANES, w), F32)
        er, ei = sweep((zero, zero), False)
        pr, pi = _cpow(ar_ref[...], -ai_ref[...], steps)
        cr, ci0 = _segment_carry(er, ei, pr, pi, back)
        _, _, dar, dai = sweep((cr, ci0, zero, zero), True)
        dar_ref[...] = jnp.sum(dar, axis=0, keepdims=True)
        dai_ref[...] = jnp.sum(dai, axis=0, keepdims=True)

    col = lambda i: (0, i)
    return pl.pallas_call(
        body, name=name, grid=(s // w,),
        in_specs=[pl.BlockSpec((t, c), lambda i: (0, 0)), pl.BlockSpec((c, w), col), pl.BlockSpec((c, w), col),
                  pl.BlockSpec((t, w), col), pl.BlockSpec((t, w), col), pl.BlockSpec((1, w), col), pl.BlockSpec((1, w), col)],
        out_specs=[pl.BlockSpec((t, w), col), pl.BlockSpec((t, w), col), pl.BlockSpec((1, w), col), pl.BlockSpec((1, w), col)],
        out_shape=[_out(t, s, BF16), _out(t, s, BF16), _out(1, s, F32), _out(1, s, F32)],
        scratch_shapes=[pltpu.VMEM((ch, w), F32)] * 4 + [pltpu.VMEM((ch + SUBLANES, w), F32)] * 2,
        compiler_params=_params(("parallel",)),
    )(dyp, c_re, c_im, xr, xi, ar, ai)


def _ssm_finish(y0, y1, sp, skip):
    t, c = sp.shape
    steps = t // SUBLANES
    w = _pick(c, LANES)

    def body(y0_ref, y1_ref, sp_ref, d_ref, y_ref, ys_ref):
        rows = pl.ds(pl.program_id(1), steps, stride=SUBLANES)
        y = y0_ref[rows, :] + y1_ref[rows, :] + sp_ref[rows, :] * d_ref[...]
        y_ref[...] = y
        ys_ref[...] = jax.nn.gelu(y).astype(BF16)

    whole = pl.BlockSpec((t, w), lambda j, k: (0, j))
    seg = pl.BlockSpec((steps, w), lambda j, k: (k, j))
    return pl.pallas_call(
        body, name="ssm_finish", grid=(c // w, SUBLANES),
        in_specs=[whole, whole, whole, pl.BlockSpec((1, w), lambda j, k: (0, j))], out_specs=[seg, seg],
        out_shape=[_out(t, c, F32), _out(t, c, BF16)], compiler_params=_params(("parallel", "arbitrary")),
    )(y0, y1, sp, skip)


def _to_segments(a):
    t, c = a.shape
    return a.reshape(SUBLANES, t // SUBLANES, c).transpose(1, 0, 2).reshape(t, c)


def _from_segments(a):
    t, c = a.shape
    return a.reshape(t // SUBLANES, SUBLANES, c).transpose(1, 0, 2).reshape(t, c)


def _colsum_prod(name, a, b, b_coff=0):
    t, n = a.shape
    tm = _pick(t, 1024, SUBLANES)

    def body(a_ref, b_ref, o_ref):
        @pl.when(pl.program_id(0) == 0)
        def _():
            o_ref[...] = jnp.zeros_like(o_ref)

        o_ref[...] += jnp.sum(a_ref[...].astype(F32) * b_ref[...].astype(F32), axis=0, keepdims=True)

    return pl.pallas_call(
        body, name=name, grid=(t // tm,),
        in_specs=[pl.BlockSpec((tm, n), lambda i: (i, 0)), pl.BlockSpec((tm, n), lambda i: (i, b_coff))],
        out_specs=pl.BlockSpec((1, n), lambda i: (0, 0)), out_shape=_out(1, n, F32),
        compiler_params=_params(("arbitrary",)),
    )(a, b)


def _ssm_maps(arrs, signs):
    n2, hh, p = arrs[0].shape
    g = n2 // 2

    def body(*refs):
        ins, outs = refs[:len(arrs)], refs[len(arrs):]
        for a, (a_ref, sign) in enumerate(zip(ins, signs)):
            for d in range(2):
                o_ref = outs[2 * a + d]
                o_ref[...] = jnp.zeros_like(o_ref)
                for k in range(g):
                    o_ref[pl.ds(k * hh, hh), pl.ds(k * p, p)] = (sign * a_ref[d * g + k]).astype(BF16)

    outs = pl.pallas_call(body, name="ssm_maps", out_shape=[_out(g * hh, g * p, BF16)] * (2 * len(arrs)))(*arrs)
    return [outs[2 * a:2 * a + 2] for a in range(len(arrs))]


def _softmax(qh, kh, scale):
    s = lax.dot_general(qh, kh, _DIMS["nt"], preferred_element_type=F32) * scale
    e = jnp.exp(s - jnp.max(s, axis=-1, keepdims=True))
    return e / jnp.sum(e, axis=-1, keepdims=True)


def _attn_fwd(q, kv):
    t, d = q.shape
    mm_ = kv.shape[0]
    hd = d // N_XHEADS
    scale = 1.0 / math.sqrt(hd)
    tm = _pick(t, 1024, SUBLANES)

    def body(q_ref, kv_ref, o_ref):
        for h in range(N_XHEADS):
            sl = pl.ds(h * hd, hd)
            p = _softmax(q_ref[:, sl], kv_ref[:, sl], scale)
            o_ref[:, sl] = jnp.dot(p.astype(BF16), kv_ref[:, pl.ds(d + h * hd, hd)],
                                   preferred_element_type=F32).astype(BF16)

    return pl.pallas_call(
        body, name="attn_fwd", grid=(t // tm,),
        in_specs=[pl.BlockSpec((tm, d), lambda i: (i, 0)), pl.BlockSpec((mm_, 2 * d), lambda i: (0, 0))],
        out_specs=pl.BlockSpec((tm, d), lambda i: (i, 0)), out_shape=_out(t, d, BF16),
        compiler_params=_params(("parallel",)),
    )(q, kv)


def _attn_bwd(q, kv, do):
    t, d = q.shape
    mm_ = kv.shape[0]
    hd = d // N_XHEADS
    scale = 1.0 / math.sqrt(hd)
    tm = _pick(t, 1024, SUBLANES)

    def body(q_ref, kv_ref, do_ref, dq_ref, dkv_ref):
        @pl.when(pl.program_id(0) == 0)
        def _():
            dkv_ref[...] = jnp.zeros_like(dkv_ref)

        for h in range(N_XHEADS):
            sl = pl.ds(h * hd, hd)
            vsl = pl.ds(d + h * hd, hd)
            qh, kh, doh = q_ref[:, sl], kv_ref[:, sl], do_ref[:, sl]
            p = _softmax(qh, kh, scale)
            dp = lax.dot_general(doh, kv_ref[:, vsl], _DIMS["nt"], preferred_element_type=F32)
            dkv_ref[:, vsl] += lax.dot_general(p.astype(BF16), doh, _DIMS["tn"], preferred_element_type=F32)
            ds = (p * (dp - jnp.sum(dp * p, axis=-1, keepdims=True)) * scale).astype(BF16)
            dq_ref[:, sl] = jnp.dot(ds, kh, preferred_element_type=F32).astype(BF16)
            dkv_ref[:, sl] += lax.dot_general(ds, qh, _DIMS["tn"], preferred_element_type=F32)

    row = pl.BlockSpec((tm, d), lambda i: (i, 0))
    full = pl.BlockSpec((mm_, 2 * d), lambda i: (0, 0))
    return pl.pallas_call(
        body, name="attn_bwd", grid=(t // tm,), in_specs=[row, full, row], out_specs=[row, full],
        out_shape=[_out(t, d, BF16), _out(mm_, 2 * d, F32)], compiler_params=_params(("arbitrary",)),
    )(q, kv, do)


def _ew(name, fn, ins, outs, rows_pref=256):
    r, c = ins[0].shape
    tr = _pick(r, rows_pref, SUBLANES)
    ni = len(ins)

    def body(*refs):
        res = fn(*[x[...] for x in refs[:ni]])
        for o_ref, v in zip(refs[ni:], res):
            o_ref[...] = v.astype(o_ref.dtype)

    blk = pl.BlockSpec((tr, c), lambda i: (i, 0))
    return pl.pallas_call(
        body, name=name, grid=(r // tr,), in_specs=[blk] * ni, out_specs=[blk] * len(outs),
        out_shape=[_out(r, c, dt) for dt in outs], compiler_params=_params(("parallel",)),
    )(*ins)


def _sum_slots(name, a, dtype):
    s, r, c = a.shape
    tr = _pick(r, 256, SUBLANES)

    def body(a_ref, o_ref):
        acc = a_ref[0].astype(F32)
        for k in range(1, s):
            acc = acc + a_ref[k].astype(F32)
        o_ref[...] = acc.astype(o_ref.dtype)

    return pl.pallas_call(
        body, name=name, grid=(r // tr,), in_specs=[pl.BlockSpec((s, tr, c), lambda i: (0, i, 0))],
        out_specs=pl.BlockSpec((tr, c), lambda i: (i, 0)), out_shape=_out(r, c, dtype),
        compiler_params=_params(("parallel",)),
    )(a)


def _adamw_step(wv, gv, mv, vv):
    bc1 = 1.0 - ADAM_B1 ** ADAM_STEP
    bc2 = 1.0 - ADAM_B2 ** ADAM_STEP
    m2 = ADAM_B1 * mv + (1.0 - ADAM_B1) * gv
    v2 = ADAM_B2 * vv + (1.0 - ADAM_B2) * (gv * gv)
    delta = -ADAM_LR * ((m2 / bc1) / (jnp.sqrt(v2 / bc2) + ADAM_EPS) + ADAM_WD * wv)
    return delta, m2, v2


def _adamw_group(name, items, transposed):
    k, r = items[0][0].shape
    if transposed and r % LANES != 0:
        rows = _adamw_group(name, [(w.T, g, m.T, v.T) for w, g, m, v in items], False)
        return [[a.T for a in item] for item in rows]
    tk = _pick(k, max(SUBLANES, ADAMW_STEP_WORDS // (r * len(items))), SUBLANES)
    n_out = 4 if transposed else 3

    def body(*refs):
        ins, outs = refs[:4 * len(items)], refs[4 * len(items):]
        for i in range(len(items)):
            wv, gv, mv, vv = (a[...] for a in ins[4 * i:4 * i + 4])
            if transposed:
                gv = gv.T
            res = _adamw_step(wv, gv, mv, vv) + ((gv,) if transposed else ())
            for o_ref, val in zip(outs[n_out * i:n_out * (i + 1)], res):
                o_ref[...] = val

    blk = pl.BlockSpec((tk, r), lambda j: (j, 0))
    g_blk = pl.BlockSpec((r, tk), lambda j: (0, j)) if transposed else blk
    res = pl.pallas_call(
        body, name=name, grid=(k // tk,), in_specs=[blk, g_blk, blk, blk] * len(items),
        out_specs=[blk] * (n_out * len(items)), out_shape=[pltpu.HBM((k, r), F32)] * (n_out * len(items)),
        compiler_params=_params(("parallel",)),
    )(*[pltpu.with_memory_space_constraint(a, pltpu.HBM) for item in items for a in item])
    return [list(res[n_out * i:n_out * (i + 1)]) + ([] if transposed else [items[i][1]]) for i in range(len(items))]


def _allgather(name, arrs):
    n = len(arrs)

    def body(*refs):
        ins, outs = refs[:n], refs[n:2 * n]
        send_sems, recv_sems, local_sems = refs[2 * n:]
        x, y, c = lax.axis_index("x"), lax.axis_index("y"), lax.axis_index("c")
        me, sibling = (x, y, c), (x, y, 1 - c)
        chips = [(1 - x, y), (x, 1 - y), (1 - x, 1 - y)]

        def rows(a, px, py, pc):
            r = ins[a].shape[0]
            return outs[a].at[pl.ds((4 * px + 2 * py + pc) * r, r), :]

        def copy(a, k, block, to, src=None):
            return pltpu.make_async_remote_copy(
                src_ref=rows(a, *block) if src is None else src, dst_ref=rows(a, *block),
                send_sem=send_sems.at[a, k], recv_sem=recv_sems.at[a, k], device_id=to, device_id_type=MESH)

        mine = [pltpu.make_async_copy(ins[a], rows(a, *me), local_sems.at[a]) for a in range(n)]
        for cp in mine:
            cp.start()
        first = []
        for a in range(n):
            first.append(copy(a, 0, me, sibling, src=ins[a]))
            first += [copy(a, 1 + j, me, (*chip, c), src=ins[a]) for j, chip in enumerate(chips)]
        for cp in first:
            cp.start()
        passed = []
        for j, chip in enumerate(chips):
            for a in range(n):
                copy(a, 1 + j, (*chip, c), me).wait_recv()
                cp = copy(a, 4 + j, (*chip, c), sibling)
                cp.start()
                passed.append(cp)
        for a in range(n):
            copy(a, 0, sibling, me).wait_recv()
            for j, chip in enumerate(chips):
                copy(a, 4 + j, (*chip, 1 - c), me).wait_recv()
        for cp in first + passed:
            cp.wait_send()
        for cp in mine:
            cp.wait()

    return pl.pallas_call(
        body, name=name, in_specs=[ANY] * n, out_specs=[ANY] * n,
        out_shape=[_out(N_DEV * a.shape[0], a.shape[1], a.dtype) for a in arrs],
        scratch_shapes=[pltpu.SemaphoreType.DMA((n, 7)), pltpu.SemaphoreType.DMA((n, 7)), pltpu.SemaphoreType.DMA((n,))],
    )(*arrs)


def _cores_start(name, blocks):
    n = len(blocks)
    c = blocks[0].shape[2]
    r = sum(b.shape[1] for b in blocks)

    def build(src_refs, land_refs, send_sems, recv_sems):
        x, y, cc = lax.axis_index("x"), lax.axis_index("y"), lax.axis_index("c")
        remote, off = [], 0
        for a, src in enumerate(src_refs):
            rows = pl.ds(off, src.shape[1])
            off += src.shape[1]
            for q in range(4):
                remote.append(pltpu.make_async_remote_copy(
                    src_ref=src.at[2 * q + (1 - cc)], dst_ref=land_refs[0].at[q, rows], send_sem=send_sems.at[4 * a + q],
                    recv_sem=recv_sems.at[4 * a + q], device_id=(x, y, 1 - cc), device_id_type=MESH))
        return remote, []

    return _split_start(name, [(blocks, [jax.ShapeDtypeStruct((4, r, c), blocks[0].dtype)], 4 * n, 0, build)])[0]


def _peer(k, x, y, c):
    return (1 - x if k & 4 else x, 1 - y if k & 2 else y, 1 - c if k & 1 else c)


def _split_start(name, groups, after=None):
    pins = [] if after is None else [after]
    bufs, sem_shapes, spans = [], [], []
    for srcs, land_shapes, n_remote, n_local, _ in groups:
        sems = [pltpu.SemaphoreType.DMA((n_remote,)), pltpu.SemaphoreType.DMA((n_remote,))]
        sems += [pltpu.SemaphoreType.DMA((n_local,))] if n_local else []
        spans.append((len(bufs), len(srcs), len(land_shapes), len(sem_shapes), len(sems)))
        bufs += [pltpu.with_memory_space_constraint(a, pltpu.HBM) for a in srcs]
        bufs += [pltpu.with_memory_space_constraint(lax.empty(s.shape, s.dtype), pltpu.HBM) for s in land_shapes]
        sem_shapes += sems
    n_buf, n_sem = len(bufs), len(sem_shapes)

    def body(*refs):
        buf_refs, sem_refs, token = refs[:n_buf], refs[n_buf + len(pins):n_buf + len(pins) + n_sem], refs[-1]
        for (b0, ns, nl, s0, k), group in zip(spans, groups):
            remote, local = group[4](buf_refs[b0:b0 + ns], buf_refs[b0 + ns:b0 + ns + nl], *sem_refs[s0:s0 + k])
            for cp in local + remote:
                cp.start()
        token[...] = jnp.zeros_like(token)

    outs = pl.pallas_call(
        body, name=name,
        out_shape=sem_shapes + [pltpu.HBM(b.shape, b.dtype) for b in bufs] + [jax.ShapeDtypeStruct((SUBLANES, LANES), F32)],
        in_specs=[HBM] * n_buf + [ANY] * len(pins),
        out_specs=[SEM] * n_sem + [HBM] * n_buf + [pl.BlockSpec(memory_space=pltpu.VMEM)],
        input_output_aliases={i: n_sem + i for i in range(n_buf)},
        compiler_params=pltpu.CompilerParams(has_side_effects=SIDE_EFFECT),
    )(*bufs, *pins)
    return [dict(sems=list(outs[s0:s0 + k]), bufs=list(outs[n_sem + b0:n_sem + b0 + ns + nl]), token=outs[-1],
                 build=group[4], ns=ns) for (b0, ns, nl, s0, k), group in zip(spans, groups)]


def _split_wait(name, started, after):
    ns, n_buf, n_sem = started["ns"], len(started["bufs"]), len(started["sems"])

    def body(*refs):
        src_refs, land_refs = refs[:ns], refs[ns:n_buf]
        sems = refs[n_buf:n_buf + n_sem]
        remote, local = started["build"](src_refs, land_refs, *sems)
        for cp in local:
            cp.wait()
        for cp in remote:
            cp.wait_send()
            cp.wait_recv()

    outs = pl.pallas_call(
        body, name=name, out_shape=[pltpu.HBM(b.shape, b.dtype) for b in started["bufs"]],
        in_specs=[HBM] * n_buf + [SEM] * n_sem + [ANY], out_specs=[HBM] * n_buf,
        input_output_aliases={i: i for i in range(n_buf)},
        compiler_params=pltpu.CompilerParams(has_side_effects=SIDE_EFFECT),
    )(*started["bufs"], *started["sems"], after)
    return list(outs[:ns]), list(outs[ns:])


def _gather_group(shards):
    m = len(shards)

    def build(src_refs, land_refs, send_sems, recv_sems, local_sems):
        x, y, c = lax.axis_index("x"), lax.axis_index("y"), lax.axis_index("c")
        remote, local = [], []
        for j in range(m):
            r = src_refs[j].shape[0]
            dst = land_refs[j].at[pl.ds((4 * x + 2 * y + c) * r, r), :]
            local.append(pltpu.make_async_copy(src_refs[j], dst, local_sems.at[j]))
            for k in range(1, N_DEV):
                remote.append(pltpu.make_async_remote_copy(
                    src_ref=src_refs[j], dst_ref=dst, send_sem=send_sems.at[7 * j + k - 1],
                    recv_sem=recv_sems.at[7 * j + k - 1], device_id=_peer(k, x, y, c), device_id_type=MESH))
        return remote, local

    lands = [jax.ShapeDtypeStruct((N_DEV * a.shape[0], a.shape[1]), a.dtype) for a in shards]
    return shards, lands, 7 * m, m, build


def _slots_start(name, a):
    def build(src_refs, land_refs, send_sems, recv_sems, local_sems):
        x, y, c = lax.axis_index("x"), lax.axis_index("y"), lax.axis_index("c")
        dst = land_refs[0].at[4 * x + 2 * y + c]
        local = [pltpu.make_async_copy(src_refs[0], dst, local_sems.at[0])]
        remote = [pltpu.make_async_remote_copy(
            src_ref=src_refs[0], dst_ref=dst, send_sem=send_sems.at[k - 1], recv_sem=recv_sems.at[k - 1],
            device_id=_peer(k, x, y, c), device_id_type=MESH) for k in range(1, N_DEV)]
        return remote, local

    return _split_start(name, [([a], [jax.ShapeDtypeStruct((N_DEV,) + a.shape, a.dtype)], 7, 1, build)])[0]


def _chips_start(name, p):
    _, r, c = p.shape
    nck = r // GRAD_ROW_TILE

    def build(src_refs, land_refs, send_sems, recv_sems):
        x, y, cc = lax.axis_index("x"), lax.axis_index("y"), lax.axis_index("c")
        remote = []
        for k in range(1, 4):
            px = 1 - x if k >> 1 else x
            py = 1 - y if k & 1 else y
            for j in range(nck):
                rows = pl.ds(j * GRAD_ROW_TILE, GRAD_ROW_TILE)
                remote.append(pltpu.make_async_remote_copy(
                    src_ref=src_refs[0].at[2 * px + py, rows], dst_ref=land_refs[0].at[k - 1, rows],
                    send_sem=send_sems.at[(k - 1) * nck + j], recv_sem=recv_sems.at[(k - 1) * nck + j],
                    device_id=(px, py, cc), device_id_type=MESH))
        return remote, []

    return _split_start(name, [([p], [jax.ShapeDtypeStruct((3, r, c), p.dtype)], 3 * nck, 0, build)])[0]


def _chip_sum(name, p, recv, chip):
    _, r, c = p.shape
    tr = _pick(r, 5 * GRAD_ROW_TILE, GRAD_ROW_TILE)

    def body(chip_ref, p_ref, r_ref, o_ref):
        acc = p_ref[...].astype(F32)
        for k in range(3):
            acc = acc + r_ref[k].astype(F32)
        o_ref[...] = acc

    return pl.pallas_call(
        body, name=name,
        grid_spec=pltpu.PrefetchScalarGridSpec(
            num_scalar_prefetch=1, grid=(r // tr,),
            in_specs=[pl.BlockSpec((None, tr, c), lambda i, chip_ref: (chip_ref[0], i, 0)),
                      pl.BlockSpec((3, tr, c), lambda i, chip_ref: (0, i, 0))],
            out_specs=pl.BlockSpec((tr, c), lambda i, chip_ref: (i, 0))),
        out_shape=_out(r, c, F32), compiler_params=_params(("parallel",)),
    )(chip, p, recv)


def _local_step(x, mem, tgt, wt, sm, ev=None):
    t, d = x.shape
    n_mem = mem.shape[0]
    d_pool = sm["pool_scale"].shape[1]
    d_ssm = sm["ssm_d"].shape[1]
    _, sg, sp, sh = sm["ssm_b_re"].shape
    n_state = sg * sp
    gb, gs = {}, {}

    def emit(name, **kw):
        return ev(name, **kw) if ev is not None else None

    n1 = _rms_fwd("ffn1_norm", x, sm["ffn1_norm"])
    emit("ffn1_norm_done", marker=n1)
    def ffn1_down(hid):
        emit("ffn1_up_done", marker=hid)
        return wt["ffn1_w_down"]

    h1, ffn1_saved = _ffn_fwd("ffn1", x, n1, wt["ffn1_w_gate"], wt["ffn1_w_up"], ffn1_down)
    emit("ffn1_fwd_done", marker=h1)
    u = _rms_fwd("mix_norm", h1, sm["mix_norm"])
    d_in = wt["w_in"].shape[0]
    tm, tn = _pick(t, 2048), _pick(d_in, 1408)
    proj = _mm1("in_proj", "nt", u, wt["w_in"], t, d_in, tm, tn, F32)
    off_s = d_pool // d_ssm
    off_gp = (d_pool + d_ssm)
    off_gs = off_gp + d

    pool_w_bf = sm["pool_w"].astype(BF16)
    pooled, pm = _pool_fwd(proj, pool_w_bf, sm["pool_scale"])

    by_p = lambda a: jnp.swapaxes(a, -1, -2).reshape(2 * sg, sh, sp)
    disc_args = [sm["ssm_a_re"].reshape(2 * sg, 1, sp), sm["ssm_a_im"].reshape(2 * sg, 1, sp),
                 sm["ssm_log_dt"].reshape(2 * sg, 1, 1), by_p(sm["ssm_b_re"]), by_p(sm["ssm_b_im"])]
    abr, abi, bbr, bbi = _ssm_disc(disc_args)
    abr2, abi2 = abr.reshape(2, n_state), abi.reshape(2, n_state)
    b_re, b_im, c_re, c_im = _ssm_maps(
        [bbr, bbi, sm["ssm_c_re"].reshape(2 * sg, sh, sp), sm["ssm_c_im"].reshape(2 * sg, sh, sp)], [1.0, 1.0, 1.0, -1.0])
    sp32 = _to_segments(proj[:, d_pool:d_pool + d_ssm])
    xs, y_parts = [], []
    for dr in range(2):
        xr, xi, y_part = _ssm_fwd(f"ssm_fwd{dr}", sp32, b_re[dr], b_im[dr], c_re[dr], c_im[dr], abr2[dr:dr + 1],
                                  abi2[dr:dr + 1], reverse=(dr == 1))
        xs.append((xr, xi))
        y_parts.append(y_part)
    y, ys = _ssm_finish(y_parts[0], y_parts[1], sp32, sm["ssm_d"])
    tmy = _pick(t, 1024)
    emit("mix_in_done", marker=ys)

    tmm, tnm, tnx = _pick(t, 2048), _pick(d, 256), _pick(d, 512)
    gp_spec = _tile(tmm, tnm, off_gp // tnm)
    gs_spec = _tile(tmm, tnm, off_gs // tnm)

    def merge_epi(accs, gpv, gsv):
        z_pool, val, gate = accs
        return (jax.nn.sigmoid(gpv) * z_pool + jax.nn.sigmoid(gsv) * (val * jax.nn.sigmoid(gate)),)

    merged = _mm("mix_merge", "nt", [pm, ys], [wt["w_pool_proj"], wt["w_glu_val"], wt["w_glu_gate"]],
                 [[(0, 0)], [(1, 1)], [(1, 2)]], t, d, tmm, tnm, [(proj, gp_spec), (proj, gs_spec)], merge_epi,
                 [(_out(t, d, BF16), None)])[0]
    res_epi = lambda accs, hin: (hin + accs[0],)
    h2 = _mm("mix_out", "nn", [merged], [wt["w_mix_out"]], [[(0, 0)]], t, d, tmm, tnx, [(h1, _tile(tmm, tnx))],
             res_epi, [(_out(t, d, F32), None)])[0]

    un = _rms_fwd("xattn_norm", h2, sm["xattn_norm"])
    mn = _rms_fwd("mem_norm", mem, sm["mem_norm"])
    emit("mix_done", marker=un)
    q = _mm1("xattn_q", "nn", un, wt["w_q"], t, d, tmm, tnx, BF16)
    kv = _mm1("xattn_kv", "nt", mn, wt["w_kv"], n_mem, 2 * d, n_mem, _pick(2 * d, 512), BF16)
    o = _attn_fwd(q, kv)
    h3 = _mm("xattn_out", "nn", [o], [wt["w_xo"]], [[(0, 0)]], t, d, tmm, tnx, [(h2, _tile(tmm, tnx))],
             res_epi, [(_out(t, d, F32), None)])[0]

    n2 = _rms_fwd("ffn2_norm", h3, sm["ffn2_norm"])
    emit("xattn_done", marker=n2)
    h4, ffn2_saved = _ffn_fwd("ffn2", h3, n2, wt["ffn2_w_gate"], wt["ffn2_w_up"], wt["ffn2_w_down"])

    dh4, dh4_bf, gs["final_norm"], loss = _loss_head(h4, sm["final_norm"], tgt)
    dh3, dh3_bf, gs["ffn2_norm"], gb["ffn2_w_gate"], gb["ffn2_w_up"], gb["ffn2_w_down"] = _ffn_bwd(
        "ffn2", h3, sm["ffn2_norm"], wt["ffn2_w_gate"], wt["ffn2_w_up"], wt["ffn2_w_down"], ffn2_saved, dh4, dh4_bf)

    tw = _pick(d, 1024)
    do = _mm1("xattn_do", "nt", dh3_bf, wt["w_xo"], t, d, tmm, tnx, BF16)
    gb["w_xo"] = _mm1("xattn_dwxo", "tn", o, dh3_bf, d, d, tw, tnx, BF16)
    dq, dkv = _attn_bwd(q, kv, do)
    gb["w_q"] = _mm1("xattn_dwq", "tn", un, dq, d, d, tw, tnx, BF16)
    dun = _mm1("xattn_dun", "nt", dq, wt["w_q"], t, d, tmm, tnx, F32)
    dh2, dh2_bf, gs["xattn_norm"] = _rms_bwd("xattn_norm_bwd", h2, sm["xattn_norm"], dun, dh3)
    gb["w_kv"] = _mm1("xattn_dwkv", "tn", dkv, mn, 2 * d, d, _pick(2 * d, 512), d, BF16)
    dmn = _mm1("xattn_dmn", "nn", dkv, wt["w_kv"], n_mem, d, n_mem, tnx, F32)
    gs["mem_norm"] = _rms_bwd("mem_norm_bwd", mem, sm["mem_norm"], dmn)

    gb["w_mix_out"] = _mm1("mix_dwout", "tn", merged, dh2_bf, d, d, tw, tnx, BF16)

    def merge_bwd_epi(accs, gpv, gsv):
        dmerged, z_pool, val, gate = accs
        sp_, ss_, sg_ = jax.nn.sigmoid(gpv), jax.nn.sigmoid(gsv), jax.nn.sigmoid(gate)
        glu = val * sg_
        dz_pool = dmerged * sp_
        dg_pool = dmerged * z_pool * (sp_ * (1.0 - sp_))
        dz_ssm = dmerged * ss_
        dg_ssm = dmerged * glu * (ss_ * (1.0 - ss_))
        dval = dz_ssm * sg_
        dgate = dz_ssm * glu * (1.0 - sg_)
        return dz_pool, dg_pool, dg_ssm, dval, dgate

    dz_pool, dg_pool, dg_ssm, dval, dgate = _mm(
        "mix_merge_bwd", "nt", [dh2_bf, pm, ys], [wt["w_mix_out"], wt["w_pool_proj"], wt["w_glu_val"], wt["w_glu_gate"]],
        [[(0, 0)], [(1, 1)], [(2, 2)], [(2, 3)]], t, d, tmm, tnm, [(proj, gp_spec), (proj, gs_spec)], merge_bwd_epi,
        [(_out(t, d, BF16), None)] * 5)
    gb["w_pool_proj"] = _mm1("pool_dwproj", "tn", dz_pool, pm, d, d_pool, tw, d_pool, BF16)
    gb["w_glu_val"] = _mm1("glu_dwval", "tn", dval, ys, d, d_ssm, tw, d_ssm, BF16)
    gb["w_glu_gate"] = _mm1("glu_dwgate", "tn", dgate, ys, d, d_ssm, tw, d_ssm, BF16)

    def gelu_bwd_epi(accs, yv):
        _, vjp = jax.vjp(jax.nn.gelu, yv)
        return (vjp(accs[0])[0],)

    dy = _mm("glu_dy", "nn", [dval, dgate], [wt["w_glu_val"], wt["w_glu_gate"]], [[(0, 0), (1, 1)]], t, d_ssm, tmy, d_ssm,
             [(y, _tile(tmy, d_ssm))], gelu_bwd_epi, [(_out(t, d_ssm, F32), None)])[0]
    gs["ssm_d"] = _colsum_prod("ssm_dd", dy, proj, b_coff=off_s)
    dyp = _to_segments(dy)
    d_abr, d_abi, d_bbr, d_bbi, d_cre, d_cim, lams = [], [], [], [], [], [], []
    ts = _pick(n_state, 512)

    def fold_diag(accs):
        first = pl.program_id(1) * (ts // sp)
        row_group = lax.broadcasted_iota(jnp.int32, (d_ssm, sp), 0) // sh
        folded = []
        for acc in accs:
            out = jnp.zeros((d_ssm, sp), F32)
            for k in range(ts // sp):
                out = out + jnp.where(row_group == first + k, acc[:, sp * k:sp * (k + 1)], 0.0)
            folded.append(out)
        return tuple(folded)

    for dr in range(2):
        lr, li, dar, dai = _ssm_bwd(f"ssm_bwd{dr}", dyp, c_re[dr], c_im[dr], xs[dr][0], xs[dr][1], abr2[dr:dr + 1],
                                    abi2[dr:dr + 1], reverse=(dr == 1))
        d_abr.append(dar)
        d_abi.append(dai)
        lams += [lr, li]
        maps = _mm(f"ssm_dmaps{dr}", "tn", [sp32, dyp], [lr, li, xs[dr][0], xs[dr][1]],
                   [[(0, 0)], [(0, 1)], [(1, 2)], [(1, 3)]], d_ssm, n_state, d_ssm, ts, [], fold_diag,
                   [(_out(n_state // ts * d_ssm, sp, F32), pl.BlockSpec((d_ssm, sp), lambda i, j: (j, 0)))] * 4)
        for acc, m in zip((d_bbr, d_bbi, d_cre, d_cim), maps):
            acc.append(jnp.sum(m.reshape(n_state // ts, sg, sh, sp), axis=0))
    ds = _from_segments(_mm(
        "ssm_ds", "nt", lams, [b_re[0], b_im[0], b_re[1], b_im[1]], [[(k, k) for k in range(4)]], t, d_ssm, tmy,
        d_ssm, [(dyp, _tile(tmy, d_ssm)), (sm["ssm_d"], _rowvec(d_ssm))],
        lambda accs, dyv, dv: (dyv * dv + accs[0],), [(_out(t, d_ssm, BF16), None)])[0])
    cots = [jnp.concatenate(d_abr, axis=0).reshape(2 * sg, 1, sp), jnp.concatenate(d_abi, axis=0).reshape(2 * sg, 1, sp),
            jnp.concatenate(d_bbr, axis=0), jnp.concatenate(d_bbi, axis=0)]
    d_are, d_aim, d_ldt, d_bre, d_bim = _ssm_disc_bwd(disc_args, cots)
    gs["ssm_a_re"] = d_are.reshape(2, sg, sp)
    gs["ssm_a_im"] = d_aim.reshape(2, sg, sp)
    gs["ssm_log_dt"] = d_ldt.reshape(2, sg)
    from_p = lambda a: jnp.swapaxes(a.reshape(2, sg, sh, sp), -1, -2)
    gs["ssm_b_re"], gs["ssm_b_im"] = from_p(d_bre), from_p(d_bim)
    gs["ssm_c_re"] = jnp.stack(d_cre, axis=0)
    gs["ssm_c_im"] = -jnp.stack(d_cim, axis=0)

    dpm = _mm1("pool_dpm", "nn", dz_pool, wt["w_pool_proj"], t, d_pool, tmm, _pick(d_pool, 256), F32)
    dp, gs["pool_w"], gs["pool_scale"] = _pool_bwd(pooled, dpm, pool_w_bf, sm["pool_scale"])

    w_in = wt["w_in"]
    parts = [(dp, 0, d_pool), (ds, d_pool, d_ssm), (dg_pool, off_gp, d), (dg_ssm, off_gs, d)]
    w_in_parts = [w_in[o0:o0 + width] for _, o0, width in parts]
    gb["w_in"] = jnp.concatenate(
        [_mm1(f"in_proj_dw{k}", "tn", p_[0], u, p_[2], d, _pick(p_[2], 1024), tnx, BF16) for k, p_ in enumerate(parts)], axis=0)
    pin = emit("grads_main", gb=gb)
    du = _mm("in_proj_du", "nn", [p_[0] for p_ in parts], w_in_parts, [[(k, k) for k in range(4)]], t, d, tmm, tnx, [],
             lambda accs: (accs[0],), [(_out(t, d, F32), None)], after=pin)[0]
    dh1, dh1_bf, gs["mix_norm"] = _rms_bwd("mix_norm_bwd", h1, sm["mix_norm"], du, dh2)
    pin = emit("small_early", gs=gs, loss=loss)

    def ffn1_weights_done(d_wg, d_wu, d_wd):
        gb["ffn1_w_gate"], gb["ffn1_w_up"], gb["ffn1_w_down"] = d_wg, d_wu, d_wd
        return emit("grads_ffn1", gb=gb)

    dx, _, gs["ffn1_norm"], _, _, _ = _ffn_bwd(
        "ffn1", x, sm["ffn1_norm"], wt["ffn1_w_gate"], wt["ffn1_w_up"], wt["ffn1_w_down"], ffn1_saved, dh1, dh1_bf,
        weights_done=ffn1_weights_done, after=pin)
    return loss, dx, gb, gs


WEIGHTS = ["ffn1_norm", "ffn1_w_gate", "ffn1_w_up", "ffn1_w_down", "mix_norm", "w_in", "pool_w", "pool_scale",
           "w_pool_proj", "ssm_a_re", "ssm_a_im", "ssm_log_dt", "ssm_b_re", "ssm_b_im", "ssm_c_re", "ssm_c_im", "ssm_d",
           "w_glu_val", "w_glu_gate", "w_mix_out", "xattn_norm", "mem_norm", "w_q", "w_kv", "w_xo", "ffn2_norm",
           "ffn2_w_gate", "ffn2_w_up", "ffn2_w_down", "final_norm"]
COL_SHARDED = ["ffn1_w_gate", "ffn1_w_up", "w_in", "w_pool_proj", "w_glu_val", "w_glu_gate", "w_kv", "ffn2_w_gate",
               "ffn2_w_up"]
ROW_SHARDED = ["ffn1_w_down", "w_mix_out", "w_q", "w_xo", "ffn2_w_down"]
BIG = [n for n in WEIGHTS if n in COL_SHARDED or n in ROW_SHARDED]
SMALL = [n for n in WEIGHTS if n not in BIG]
FFN1_BIG = ["ffn1_w_gate", "ffn1_w_up", "ffn1_w_down"]
MAIN_BIG = [n for n in BIG if n not in FFN1_BIG]
GATHER_PLAN = [("ffn1_up_done", ["ffn1_w_down"]), ("ffn1_fwd_done", ["w_in"]),
               ("mix_in_done", ["w_pool_proj", "w_glu_val", "w_glu_gate", "w_mix_out"]),
               ("mix_done", ["w_q", "w_kv", "w_xo"]), ("xattn_done", ["ffn2_w_gate", "ffn2_w_up", "ffn2_w_down"])]
MINOR_SWAPPED = ["ssm_b_re", "ssm_b_im"]
LATE_SMALL = "ffn1_norm"
EARLY_SMALL = [n for n in SMALL if n != LATE_SMALL]
PACK_ROWS = SUBLANES * LANES
GRAD_ROW_TILE = 256
ADAMW_STEP_WORDS = 1 << 19


def _to_rows(name, w):
    return w.T if name in COL_SHARDED else w


def _pack_small(vals):
    flat = []
    for v in vals:
        f = v.reshape(-1)
        flat.append(jnp.pad(f, (0, (-f.shape[0]) % PACK_ROWS)))
    total = sum(f.shape[0] for f in flat)
    flat.append(jnp.zeros(((-total) % (GRAD_ROW_TILE * LANES),), F32))
    return jnp.concatenate(flat).reshape(-1, LANES)


def _unpack_small(packed, shapes):
    out, row = [], 0
    for shp in shapes:
        size = math.prod(shp)
        rows = -(-size // PACK_ROWS) * SUBLANES
        out.append(packed[row:row + rows].reshape(-1)[:size].reshape(shp))
        row += rows
    return out


def kernel(x, mem, ffn1_norm, ffn1_w_gate, ffn1_w_up, ffn1_w_down, mix_norm, w_in, pool_w, pool_scale, w_pool_proj, ssm_a_re, ssm_a_im, ssm_log_dt, ssm_b_re, ssm_b_im, ssm_c_re, ssm_c_im, ssm_d, w_glu_val, w_glu_gate, w_mix_out, xattn_norm, mem_norm, w_q, w_kv, w_xo, ffn2_norm, ffn2_w_gate, ffn2_w_up, ffn2_w_down, final_norm, loss_target, m_ffn1_norm, m_ffn1_w_gate, m_ffn1_w_up, m_ffn1_w_down, m_mix_norm, m_w_in, m_pool_w, m_pool_scale, m_w_pool_proj, m_ssm_a_re, m_ssm_a_im, m_ssm_log_dt, m_ssm_b_re, m_ssm_b_im, m_ssm_c_re, m_ssm_c_im, m_ssm_d, m_w_glu_val, m_w_glu_gate, m_w_mix_out, m_xattn_norm, m_mem_norm, m_w_q, m_w_kv, m_w_xo, m_ffn2_norm, m_ffn2_w_gate, m_ffn2_w_up, m_ffn2_w_down, m_final_norm, v_ffn1_norm, v_ffn1_w_gate, v_ffn1_w_up, v_ffn1_w_down, v_mix_norm, v_w_in, v_pool_w, v_pool_scale, v_w_pool_proj, v_ssm_a_re, v_ssm_a_im, v_ssm_log_dt, v_ssm_b_re, v_ssm_b_im, v_ssm_c_re, v_ssm_c_im, v_ssm_d, v_w_glu_val, v_w_glu_gate, v_w_mix_out, v_xattn_norm, v_mem_norm, v_w_q, v_w_kv, v_w_xo, v_ffn2_norm, v_ffn2_w_gate, v_ffn2_w_up, v_ffn2_w_down, v_final_norm):
    given = dict(locals())
    wts = {n: given[n] for n in WEIGHTS}
    moms = {n: (given["m_" + n], given["v_" + n]) for n in WEIGHTS}
    x2, mem2, tgt2 = x[0], mem[0], loss_target[0]
    d = x2.shape[1]
    chip = (2 * lax.axis_index("x") + lax.axis_index("y")).astype(jnp.int32).reshape(1)

    def full_form(n, f):
        shard = wts[n][0].shape
        return f.reshape(N_DEV * shard[1], shard[0]) if n in COL_SHARDED else f.reshape(N_DEV * shard[0], shard[1])

    shards = {n: _to_rows(n, wts[n][0]).astype(BF16) for n in BIG}
    first = FFN1_BIG[:2]
    wt = {n: full_form(n, f) for n, f in zip(first, _allgather("weight_allgather_first", [shards[n] for n in first]))}
    started = _split_start("weight_gather_start", [_gather_group([shards[n] for n in names]) for _, names in GATHER_PLAN],
                           after=wt[first[0]])
    gathers = {event: (names, st) for (event, names), st in zip(GATHER_PLAN, started)}
    sm = {n: (wts[n].reshape(1, -1) if wts[n].ndim <= 2 else wts[n][0]) for n in SMALL}
    sm["ffn1_norm"] = sm["ffn1_norm"] + started[0]["token"][0, 0]

    pending = {}

    def reduce_start(tag, names, gb):
        blocks = [gb[n].reshape(N_DEV, -1, d) for n in names]
        pad_rows = (-sum(b.shape[1] for b in blocks)) % GRAD_ROW_TILE
        pad = [jnp.zeros((N_DEV, pad_rows, d), BF16)] if pad_rows else []
        started = _cores_start("grad_exchange_cores_start_" + tag, blocks + pad)
        own = jnp.concatenate([lax.dynamic_index_in_dim(b.reshape(4, 2, b.shape[1], d), lax.axis_index("c"), 1, False)
                               for b in started["bufs"][:len(blocks + pad)]], axis=1)
        _, (recv,) = _split_wait("grad_exchange_cores_wait_" + tag, started, own)
        rows_all = own.shape[1]
        pair = _ew("grad_pair_sum_" + tag, lambda a, b: (a.astype(F32) + b.astype(F32),),
                   [own.reshape(-1, d), recv.reshape(-1, d)], [BF16], rows_pref=5 * GRAD_ROW_TILE)[0]
        pair = pair.reshape(4, rows_all, d)
        pending[tag] = (pair, _chips_start("grad_exchange_chips_start_" + tag, pair), [b.shape[1] for b in blocks])
        return pending[tag][1]["token"]

    def reduce_finish(tag, after):
        _, started, rows = pending[tag]
        (pair,), (recv,) = _split_wait("grad_exchange_chips_wait_" + tag, started, after)
        return _chip_sum("grad_chip_sum_" + tag, pair, recv, chip), rows

    def ev(name, gb=None, gs=None, loss=None, marker=None):
        if name in gathers:
            names, started = gathers[name]
            for n, f in zip(names, _split_wait("weight_gather_wait_" + name, started, marker)[1]):
                wt[n] = full_form(n, f)
        elif name == "grads_main":
            return reduce_start("main", MAIN_BIG, gb)
        elif name == "small_early":
            pending["small"] = _slots_start("small_gather_start", _pack_small([gs[n] for n in EARLY_SMALL] + [loss[:, :1]]))
            return pending["small"]["token"]
        elif name == "grads_ffn1":
            return reduce_start("ffn1", FFN1_BIG, gb)
        return None

    _, dx, _, gs = _local_step(x2, mem2, tgt2, wt, sm, ev)

    grads = {}
    for tag, names in (("main", MAIN_BIG), ("ffn1", FFN1_BIG)):
        g_rows, rows = reduce_finish(tag, dx)
        off = 0
        for n, r in zip(names, rows):
            shard = wts[n].shape
            grads[n] = g_rows[off:off + r].reshape((shard[2], shard[1]) if n in COL_SHARDED else shard[1:])
            off += r
    small_sum = _sum_slots("small_sum", _split_wait("small_gather_wait", pending["small"], dx)[1][0], F32)
    late = _allgather("small_allgather_late", [gs[LATE_SMALL].reshape(-1, LANES)])[0]
    late_sum = _sum_slots("small_sum_late", late.reshape(N_DEV, -1, LANES), F32)
    vals = _unpack_small(small_sum, [wts[n].shape for n in EARLY_SMALL] + [(1, 1)])
    total_loss = vals[-1].reshape(())
    def flat(n, a):
        a = a.reshape(wts[n].shape)
        a = jnp.swapaxes(a, -1, -2) if n in MINOR_SWAPPED else a
        return a.reshape(-1, a.shape[-1])

    def unflat(n, a):
        shape = wts[n].shape
        if n in MINOR_SWAPPED:
            return jnp.swapaxes(a.reshape(shape[:-2] + (shape[-1], shape[-2])), -1, -2)
        return a.reshape(shape)

    for n, g_full in zip(EARLY_SMALL + [LATE_SMALL], vals[:-1] + [late_sum]):
        grads[n] = flat(n, g_full)

    out_g, out_d, out_m, out_v = {}, {}, {}, {}
    by_shape = {}
    for n in WEIGHTS:
        by_shape.setdefault((flat(n, wts[n]).shape, n in COL_SHARDED), []).append(n)
    for (_, transposed), names in by_shape.items():
        items = [(flat(n, wts[n]), grads[n], flat(n, moms[n][0]), flat(n, moms[n][1])) for n in names]
        for n, res in zip(names, _adamw_group("adamw_" + names[0], items, transposed)):
            out_d[n], out_m[n], out_v[n], out_g[n] = (unflat(n, a) for a in res)

    return (total_loss, dx[None], *[out_g[n] for n in WEIGHTS], *[out_d[n] for n in WEIGHTS],
            *[out_m[n] for n in WEIGHTS], *[out_v[n] for n in WEIGHTS])
```

```python
import math

import jax
import jax.numpy as jnp
from jax import lax
from jax.experimental import pallas as pl
from jax.experimental.pallas import tpu as pltpu

F32 = jnp.float32
BF16 = jnp.bfloat16
EPS = 1e-6
N_XHEADS = 4
POOL_WINDOWS = (2, 4, 8, 16)
ADAM_LR = 0.001
ADAM_B1 = 0.9
ADAM_B2 = 0.999
ADAM_EPS = 1e-08
ADAM_WD = 0.01
ADAM_STEP = 10
N_DEV = 8
VMEM_LIMIT_V7X = 48 * 1024 * 1024
LANES = 128
SUBLANES = 8
SUB_ROWS = 256
POOL_PAD = 16
MESH = pl.DeviceIdType.MESH
ANY = pl.BlockSpec(memory_space=pl.ANY)
HBM = pl.BlockSpec(memory_space=pltpu.HBM)
SEM = pl.BlockSpec(memory_space=pltpu.SEMAPHORE)
SIDE_EFFECT = pltpu.SideEffectType.DATAFLOW_SIDE_EFFECTING

_DIMS = {
    "nt": (((1,), (1,)), ((), ())),
    "nn": (((1,), (0,)), ((), ())),
    "tn": (((0,), (0,)), ((), ())),
}


def _pick(dim, pref, mult=LANES):
    if dim <= pref:
        return dim
    for t in range(pref - pref % mult, 0, -mult):
        if dim % t == 0:
            return t
    return dim


def _params(sem):
    return pltpu.CompilerParams(dimension_semantics=sem, vmem_limit_bytes=VMEM_LIMIT_V7X)


def _tile(tm, tn, coff=0):
    return pl.BlockSpec((tm, tn), lambda i, j: (i, j + coff))


def _rowvec(tn, coff=0):
    return pl.BlockSpec((1, tn), lambda i, j: (0, j + coff))


def _out(m, n, dtype):
    return jax.ShapeDtypeStruct((m, n), dtype)


def _mm(name, form, a_list, b_list, groups, m, n, tm, tn, extras, epilogue, outs, after=None, sub=SUB_ROWS):
    na, nb, ne = len(a_list), len(b_list), len(extras)
    pins = [] if after is None else [after]
    step = tm if (sub is None or form == "tn" or tm % sub) else sub

    def a_spec(a):
        if form == "tn":
            return pl.BlockSpec((a.shape[0], tm), lambda i, j: (0, i))
        return pl.BlockSpec((tm, a.shape[1]), lambda i, j: (i, 0))

    def b_spec(b):
        if form == "nt":
            return pl.BlockSpec((tn, b.shape[1]), lambda i, j: (j, 0))
        return pl.BlockSpec((b.shape[0], tn), lambda i, j: (0, j))

    def body(*refs):
        a_refs, b_refs = refs[:na], refs[na:na + nb]
        e_refs, o_refs = refs[na + nb:na + nb + ne], refs[na + nb + ne + len(pins):]
        b_vals = {}
        for s0 in range(0, tm, step):
            rows = slice(None) if step == tm else pl.ds(s0, step)
            a_vals, accs = {}, []
            for group in groups:
                acc = None
                for ai, bi in group:
                    if ai not in a_vals:
                        a_vals[ai] = (a_refs[ai][...] if form == "tn" else a_refs[ai][rows, :]).astype(BF16)
                    if bi not in b_vals:
                        b_vals[bi] = b_refs[bi][...].astype(BF16)
                    d = lax.dot_general(a_vals[ai], b_vals[bi], _DIMS[form], preferred_element_type=F32)
                    acc = d if acc is None else acc + d
                accs.append(acc)
            res = epilogue(accs, *[e[rows, :] if e.shape[0] == tm else e[...] for e in e_refs])
            for o_ref, r in zip(o_refs, res):
                o_ref[rows, :] = r.astype(o_ref.dtype)

    out_specs = [_tile(tm, tn) if s is None else s for _, s in outs]
    res = pl.pallas_call(
        body, name=name, grid=(m // tm, n // tn),
        in_specs=[a_spec(a) for a in a_list] + [b_spec(b) for b in b_list] + [s for _, s in extras] + [ANY] * len(pins),
        out_specs=out_specs, out_shape=[o for o, _ in outs],
        compiler_params=_params(("parallel", "parallel")),
    )(*a_list, *b_list, *[e for e, _ in extras], *pins)
    return res


def _mm1(name, form, a, b, m, n, tm, tn, dtype, scale=None):
    epi = (lambda accs: (accs[0],)) if scale is None else (lambda accs: (accs[0] * scale,))
    return _mm(name, form, [a], [b], [[(0, 0)]], m, n, tm, tn, [], epi, [(_out(m, n, dtype), None)])[0]


def _rms_fwd(name, h, g):
    t, d = h.shape
    tm = _pick(t, 1024, SUBLANES)

    def body(h_ref, g_ref, n_ref):
        hv = h_ref[...]
        r = lax.rsqrt(jnp.mean(hv * hv, axis=-1, keepdims=True) + EPS)
        n_ref[...] = ((hv * r) * g_ref[...]).astype(BF16)

    return pl.pallas_call(
        body, name=name, grid=(t // tm,),
        in_specs=[pl.BlockSpec((tm, d), lambda i: (i, 0)), pl.BlockSpec((1, d), lambda i: (0, 0))],
        out_specs=pl.BlockSpec((tm, d), lambda i: (i, 0)), out_shape=_out(t, d, BF16),
        compiler_params=_params(("parallel",)),
    )(h, g)


def _rms_bwd(name, h, g, dn, dres=None):
    t, d = h.shape
    tm = _pick(t, 1024, SUBLANES)
    need_dh = dres is not None

    def body(*refs):
        if need_dh:
            h_ref, g_ref, dn_ref, dres_ref, dh_ref, dhb_ref, dg_ref = refs
        else:
            h_ref, g_ref, dn_ref, dg_ref = refs
        hv = h_ref[...]
        r = lax.rsqrt(jnp.mean(hv * hv, axis=-1, keepdims=True) + EPS)
        nh = hv * r
        dnv = dn_ref[...].astype(F32)

        @pl.when(pl.program_id(0) == 0)
        def _():
            dg_ref[...] = jnp.zeros_like(dg_ref)

        dg_ref[...] += jnp.sum(dnv * nh, axis=0, keepdims=True)
        if need_dh:
            dng = dnv * g_ref[...]
            dh = dres_ref[...] + r * (dng - nh * jnp.mean(dng * nh, axis=-1, keepdims=True))
            dh_ref[...] = dh
            dhb_ref[...] = dh.astype(BF16)

    row = pl.BlockSpec((tm, d), lambda i: (i, 0))
    vec = pl.BlockSpec((1, d), lambda i: (0, 0))
    if need_dh:
        return pl.pallas_call(
            body, name=name, grid=(t // tm,), in_specs=[row, vec, row, row], out_specs=[row, row, vec],
            out_shape=[_out(t, d, F32), _out(t, d, BF16), _out(1, d, F32)], compiler_params=_params(("arbitrary",)),
        )(h, g, dn, dres)
    return pl.pallas_call(
        body, name=name, grid=(t // tm,), in_specs=[row, vec, row], out_specs=vec,
        out_shape=_out(1, d, F32), compiler_params=_params(("arbitrary",)),
    )(h, g, dn)


def _loss_head(h, g, tgt):
    t, d = h.shape
    tm = _pick(t, 1024, SUBLANES)

    def body(h_ref, g_ref, t_ref, dh_ref, dhb_ref, dg_ref, loss_ref):
        hv = h_ref[...]
        r = lax.rsqrt(jnp.mean(hv * hv, axis=-1, keepdims=True) + EPS)
        nh = hv * r
        err = nh * g_ref[...] - t_ref[...]

        @pl.when(pl.program_id(0) == 0)
        def _():
            dg_ref[...] = jnp.zeros_like(dg_ref)
            loss_ref[...] = jnp.zeros_like(loss_ref)

        per_row = jnp.mean(err * err, axis=-1, keepdims=True)
        loss_ref[...] += 0.5 * jnp.sum(per_row, axis=0, keepdims=True)
        dy = err * (1.0 / d)
        dg_ref[...] += jnp.sum(dy * nh, axis=0, keepdims=True)
        dng = dy * g_ref[...]
        dh = r * (dng - nh * jnp.mean(dng * nh, axis=-1, keepdims=True))
        dh_ref[...] = dh
        dhb_ref[...] = dh.astype(BF16)

    row = pl.BlockSpec((tm, d), lambda i: (i, 0))
    vec = pl.BlockSpec((1, d), lambda i: (0, 0))
    return pl.pallas_call(
        body, name="loss_head", grid=(t // tm,), in_specs=[row, vec, row],
        out_specs=[row, row, vec, pl.BlockSpec((1, LANES), lambda i: (0, 0))],
        out_shape=[_out(t, d, F32), _out(t, d, BF16), _out(1, d, F32), _out(1, LANES, F32)],
        compiler_params=_params(("arbitrary",)),
    )(h, g, tgt)


def _ffn_fwd(tag, h, n, wg_t, wu_t, wd):
    t, d = h.shape
    f = wg_t.shape[0]
    tm, tn = _pick(t, 1024), _pick(f, 1408)

    def up_epi(accs):
        a, b = accs
        return a, b, (a * jax.nn.sigmoid(a)) * b

    a, b, hid = _mm(tag + "_up", "nt", [n], [wg_t, wu_t], [[(0, 0)], [(0, 1)]], t, f, tm, tn, [], up_epi,
                    [(_out(t, f, BF16), None)] * 3)
    if callable(wd):
        wd = wd(hid)
    tm2, tn2 = _pick(t, 1024), _pick(d, 512)
    h_out = _mm(tag + "_down", "nn", [hid], [wd], [[(0, 0)]], t, d, tm2, tn2, [(h, _tile(tm2, tn2))],
                lambda accs, hin: (hin + 0.5 * accs[0],), [(_out(t, d, F32), None)])[0]
    return h_out, (n, a, b, hid)


def _ffn_bwd(tag, h, g, wg_t, wu_t, wd, saved, dh, dh_bf, weights_done=None, after=None):
    n, a, b, hid = saved
    t, d = h.shape
    f = wd.shape[0]
    tm, tn = _pick(t, 1024), _pick(f, 1408)

    def hid_epi(accs, av, bv):
        dhid = 0.5 * accs[0]
        av, bv = av.astype(F32), bv.astype(F32)
        sig = jax.nn.sigmoid(av)
        da = dhid * bv * (sig * (1.0 + av * (1.0 - sig)))
        db = dhid * (av * sig)
        return da, db

    da, db = _mm(tag + "_bwd_hid", "nt", [dh_bf], [wd], [[(0, 0)]], t, f, tm, tn,
                 [(a, _tile(tm, tn)), (b, _tile(tm, tn))], hid_epi, [(_out(t, f, BF16), None)] * 2, after=after)
    tw, tnw = _pick(f, 1408), _pick(d, 512)
    d_wd = _mm1(tag + "_dwd", "tn", hid, dh_bf, f, d, tw, tnw, BF16, scale=0.5)
    d_wg = _mm1(tag + "_dwg", "tn", da, n, f, d, tw, tnw, BF16)
    d_wu = _mm1(tag + "_dwu", "tn", db, n, f, d, tw, tnw, BF16)
    pin = weights_done(d_wg, d_wu, d_wd) if weights_done is not None else None
    tm2, tn2 = _pick(t, 1024), _pick(d, 512)
    dn = _mm(tag + "_dn", "nn", [da, db], [wg_t, wu_t], [[(0, 0), (1, 1)]], t, d, tm2, tn2, [],
             lambda accs: (accs[0],), [(_out(t, d, F32), None)], after=pin)[0]
    dh_in, dh_in_bf, dg = _rms_bwd(tag + "_norm_bwd", h, g, dn, dh)
    return dh_in, dh_in_bf, dg, d_wg, d_wu, d_wd


def _window_sum(win, offsets):
    n = win.shape[0]
    acc = None
    for j in offsets:
        term = win if j == 0 else pltpu.roll(win, (-j) % n, 0)
        acc = term if acc is None else acc + term
    return acc


def _pool_counts(r0, ch, c, left, right, t):
    pos = r0 + lax.broadcasted_iota(jnp.int32, (ch, c), 0)
    return (jnp.minimum(pos + right + 1, t) - jnp.maximum(pos - left, 0)).astype(F32)


def _pool_fwd(proj, pool_w_bf, pool_scale):
    t = proj.shape[0]
    ng, c, _ = pool_w_bf.shape
    ch = _pick(t, 1024, SUBLANES)
    pad = POOL_PAD

    def body(p_ref, w_ref, s_ref, pooled_ref, pm_ref, buf):
        grp = pl.program_id(0)
        buf[pl.ds(0, pad), :] = jnp.zeros((pad, c), F32)
        buf[pl.ds(pad + t, pad), :] = jnp.zeros((pad, c), F32)

        def fill(ci, carry):
            r0 = pl.multiple_of(ci * ch, SUBLANES)
            buf[pl.ds(pl.multiple_of(r0 + pad, SUBLANES), ch), :] = p_ref[pl.ds(r0, ch), :]
            return carry

        lax.fori_loop(0, t // ch, fill, 0)
        for gi, w in enumerate(POOL_WINDOWS):
            left = w // 2
            right = w - 1 - left

            @pl.when(grp == gi)
            def _(left=left, right=right):
                def chunk(ci, carry):
                    r0 = pl.multiple_of(ci * ch, SUBLANES)
                    win = buf[pl.ds(r0, ch + 2 * pad), :]
                    s = _window_sum(win, range(-left, right + 1))[pad:pad + ch]
                    pooled = s / _pool_counts(r0, ch, c, left, right, t) - win[pad:pad + ch]
                    pooled_bf = pooled.astype(BF16)
                    mixed = jnp.dot(pooled_bf, w_ref[0], preferred_element_type=F32)
                    pooled_ref[pl.ds(r0, ch), :] = pooled_bf
                    pm_ref[pl.ds(r0, ch), :] = (mixed * s_ref[...]).astype(BF16)
                    return carry

                lax.fori_loop(0, t // ch, chunk, 0)

    col = pl.BlockSpec((t, c), lambda g: (0, g))
    return pl.pallas_call(
        body, name="pool_fwd", grid=(ng,),
        in_specs=[col, pl.BlockSpec((1, c, c), lambda g: (g, 0, 0)), pl.BlockSpec((1, c), lambda g: (0, g))],
        out_specs=[col, col], out_shape=[_out(t, ng * c, BF16), _out(t, ng * c, BF16)],
        scratch_shapes=[pltpu.VMEM((t + 2 * pad, c), F32)],
        compiler_params=_params(("parallel",)),
    )(proj, pool_w_bf, pool_scale)


def _pool_bwd(pooled, dpm, pool_w_bf, pool_scale):
    t = pooled.shape[0]
    ng, c, _ = pool_w_bf.shape
    ch = _pick(t, 1024, SUBLANES)
    pad = POOL_PAD

    def body(pooled_ref, dpm_ref, w_ref, s_ref, dp_ref, dw_ref, ds_ref, buf, raw):
        grp = pl.program_id(0)
        buf[pl.ds(0, pad), :] = jnp.zeros((pad, c), F32)
        buf[pl.ds(pad + t, pad), :] = jnp.zeros((pad, c), F32)
        dw_ref[...] = jnp.zeros_like(dw_ref)
        ds_ref[...] = jnp.zeros_like(ds_ref)
        for gi, w in enumerate(POOL_WINDOWS):
            left = w // 2
            right = w - 1 - left

            @pl.when(grp == gi)
            def _(left=left, right=right):
                def first(ci, carry):
                    r0 = pl.multiple_of(ci * ch, SUBLANES)
                    pv = pooled_ref[pl.ds(r0, ch), :]
                    dpm_v = dpm_ref[pl.ds(r0, ch), :]
                    mixed = jnp.dot(pv, w_ref[0], preferred_element_type=F32)
                    ds_ref[...] += jnp.sum(dpm_v * mixed, axis=0, keepdims=True)
                    dmixed = (dpm_v * s_ref[...]).astype(BF16)
                    dw_ref[0] += lax.dot_general(pv, dmixed, _DIMS["tn"], preferred_element_type=F32)
                    dpooled = lax.dot_general(dmixed, w_ref[0], _DIMS["nt"], preferred_element_type=F32)
                    raw[pl.ds(r0, ch), :] = dpooled
                    buf[pl.ds(pl.multiple_of(r0 + pad, SUBLANES), ch), :] = (
                        dpooled / _pool_counts(r0, ch, c, left, right, t))
                    return carry

                lax.fori_loop(0, t // ch, first, 0)

                def second(ci, carry):
                    r0 = pl.multiple_of(ci * ch, SUBLANES)
                    win = buf[pl.ds(r0, ch + 2 * pad), :]
                    s = _window_sum(win, range(-right, left + 1))[pad:pad + ch]
                    dp_ref[pl.ds(r0, ch), :] = (s - raw[pl.ds(r0, ch), :]).astype(BF16)
                    return carry

                lax.fori_loop(0, t // ch, second, 0)

    col = pl.BlockSpec((t, c), lambda g: (0, g))
    return pl.pallas_call(
        body, name="pool_bwd", grid=(ng,),
        in_specs=[col, col, pl.BlockSpec((1, c, c), lambda g: (g, 0, 0)), pl.BlockSpec((1, c), lambda g: (0, g))],
        out_specs=[col, pl.BlockSpec((1, c, c), lambda g: (g, 0, 0)), pl.BlockSpec((1, c), lambda g: (0, g))],
        out_shape=[_out(t, ng * c, BF16), jax.ShapeDtypeStruct((ng, c, c), F32), _out(1, ng * c, F32)],
        scratch_shapes=[pltpu.VMEM((t + 2 * pad, c), F32), pltpu.VMEM((t, c), F32)],
        compiler_params=_params(("parallel",)),
    )(pooled, dpm, pool_w_bf, pool_scale)


def _discretise(a_re, a_im, log_dt, b_re, b_im):
    dt = jnp.exp(log_dt)
    mag = jnp.exp(dt * a_re)
    ang = dt * a_im
    abr = mag * jnp.cos(ang)
    abi = mag * jnp.sin(ang)
    den = a_re * a_re + a_im * a_im
    nr = abr - 1.0
    qr = (nr * a_re + abi * a_im) / den
    qi = (abi * a_re - nr * a_im) / den
    return abr, abi, qr * b_re - qi * b_im, qr * b_im + qi * b_re


def _ssm_disc(args):
    def body(ar, ai, ld, br, bi, o1, o2, o3, o4):
        res = _discretise(ar[...], ai[...], ld[...], br[...], bi[...])
        for o, r in zip((o1, o2, o3, o4), res):
            o[...] = r

    like = lambda a: jax.ShapeDtypeStruct(a.shape, F32)
    return pl.pallas_call(
        body, name="ssm_disc", out_shape=[like(args[0]), like(args[0]), like(args[3]), like(args[3])],
    )(*args)


def _ssm_disc_bwd(args, cots):
    def body(ar, ai, ld, br, bi, c1, c2, c3, c4, o1, o2, o3, o4, o5):
        _, vjp = jax.vjp(_discretise, ar[...], ai[...], ld[...], br[...], bi[...])
        res = vjp((c1[...], c2[...], c3[...], c4[...]))
        for o, r in zip((o1, o2, o3, o4, o5), res):
            o[...] = r

    return pl.pallas_call(
        body, name="ssm_disc_bwd", out_shape=[jax.ShapeDtypeStruct(a.shape, F32) for a in args],
    )(*args, *cots)


def _cmul(pr, pi, qr, qi):
    return pr * qr - pi * qi, pr * qi + pi * qr


def _cpow(pr, pi, n):
    rr, ri = None, None
    while n:
        if n & 1:
            rr, ri = (pr, pi) if rr is None else _cmul(rr, ri, pr, pi)
        n >>= 1
        if n:
            pr, pi = _cmul(pr, pi, pr, pi)
    return rr, ri


def _segment_carry(er, ei, pr, pi, reverse):
    row = lax.broadcasted_iota(jnp.int32, er.shape, 0)
    cr, ci = jnp.zeros_like(er), jnp.zeros_like(ei)
    for _ in range(SUBLANES - 1):
        tr = er + pr * cr - pi * ci
        ti = ei + pr * ci + pi * cr
        if reverse:
            keep, shift = row < SUBLANES - 1, SUBLANES - 1
        else:
            keep, shift = row >= 1, 1
        cr = jnp.where(keep, pltpu.roll(tr, shift, 0), 0.0)
        ci = jnp.where(keep, pltpu.roll(ti, shift, 0), 0.0)
    return cr, ci


def _ssm_fwd(name, sp, b_re, b_im, c_re, c_im, ar, ai, reverse):
    t, c = sp.shape
    s = ar.shape[1]
    w = _pick(s, 512)
    ch = _pick(t, 1024, SUBLANES)
    n_ch, gpc, steps = t // ch, ch // SUBLANES, t // SUBLANES

    def body(sp_ref, bre_ref, bim_ref, cre_ref, cim_ref, ar_ref, ai_ref, xr_ref, xi_ref, y_ref, ur, ui, xbr, xbi):
        a_r = jnp.broadcast_to(ar_ref[...], (SUBLANES, w))
        a_i = jnp.broadcast_to(ai_ref[...], (SUBLANES, w))

        @pl.when(pl.program_id(0) == 0)
        def _():
            y_ref[...] = jnp.zeros_like(y_ref)

        def sweep(h0, store):
            def chunk(k, h):
                ci = n_ch - 1 - k if reverse else k
                rows = pl.ds(pl.multiple_of(ci * ch, ch), ch)
                spv = sp_ref[rows, :].astype(BF16)
                ur[...] = jnp.dot(spv, bre_ref[...], preferred_element_type=F32)
                ui[...] = jnp.dot(spv, bim_ref[...], preferred_element_type=F32)

                def group(g, hh):
                    gi = gpc - 1 - g if reverse else g
                    r0 = pl.multiple_of(gi * SUBLANES, SUBLANES)
                    hr, hi = hh
                    nr = a_r * hr - a_i * hi + ur[pl.ds(r0, SUBLANES), :]
                    ni = a_r * hi + a_i * hr + ui[pl.ds(r0, SUBLANES), :]
                    if store:
                        xbr[pl.ds(r0, SUBLANES), :] = nr
                        xbi[pl.ds(r0, SUBLANES), :] = ni
                    return nr, ni

                h = lax.fori_loop(0, gpc, group, h)
                if store:
                    xr16, xi16 = xbr[...].astype(BF16), xbi[...].astype(BF16)
                    xr_ref[rows, :] = xr16
                    xi_ref[rows, :] = xi16
                    y_ref[rows, :] += (lax.dot_general(xr16, cre_ref[...], _DIMS["nt"], preferred_element_type=F32)
                                       + lax.dot_general(xi16, cim_ref[...], _DIMS["nt"], preferred_element_type=F32))
                return h

            return lax.fori_loop(0, n_ch, chunk, h0)

        zero = jnp.zeros((SUBLANES, w), F32)
        er, ei = sweep((zero, zero), False)
        pr, pi = _cpow(ar_ref[...], ai_ref[...], steps)
        sweep(_segment_carry(er, ei, pr, pi, reverse), True)

    col = lambda i: (0, i)
    return pl.pallas_call(
        body, name=name, grid=(s // w,),
        in_specs=[pl.BlockSpec((t, c), lambda i: (0, 0))] + [pl.BlockSpec((c, w), col)] * 4
        + [pl.BlockSpec((1, w), col)] * 2,
        out_specs=[pl.BlockSpec((t, w), col), pl.BlockSpec((t, w), col), pl.BlockSpec((t, c), lambda i: (0, 0))],
        out_shape=[_out(t, s, BF16), _out(t, s, BF16), _out(t, c, F32)],
        scratch_shapes=[pltpu.VMEM((ch, w), F32)] * 4,
        compiler_params=_params(("arbitrary",)),
    )(sp, b_re, b_im, c_re, c_im, ar, ai)


def _ssm_bwd(name, dyp, c_re, c_im, xr, xi, ar, ai, reverse):
    t, c = dyp.shape
    s = ar.shape[1]
    w = _pick(s, 512)
    ch = _pick(t, 512, SUBLANES)
    n_ch, gpc, steps = t // ch, ch // SUBLANES, t // SUBLANES
    back = not reverse
    edge = 2 * SUBLANES

    def body(dy_ref, cre_ref, cim_ref, xr_ref, xi_ref, ar_ref, ai_ref, lr_ref, li_ref, dar_ref, dai_ref,
             gr, gi_, lbr, lbi, xbr, xbi):
        a_r = jnp.broadcast_to(ar_ref[...], (SUBLANES, w))
        a_i = -jnp.broadcast_to(ai_ref[...], (SUBLANES, w))
        row = lax.broadcasted_iota(jnp.int32, (SUBLANES, w), 0)

        def neighbours(ci, x_ref, buf):
            rows = pl.ds(pl.multiple_of(ci * ch, ch), ch)
            if reverse:
                buf[pl.ds(0, ch), :] = x_ref[rows, :].astype(F32)
                nxt = x_ref[pl.ds(pl.multiple_of(jnp.minimum(ci + 1, n_ch - 1) * ch, ch), edge), :].astype(F32)[:SUBLANES]
                first = x_ref[pl.ds(0, edge), :].astype(F32)[:SUBLANES]
                wrap = jnp.where(row < SUBLANES - 1, pltpu.roll(first, SUBLANES - 1, 0), 0.0)
                buf[pl.ds(ch, SUBLANES), :] = jnp.where(ci == n_ch - 1, wrap, nxt)
            else:
                buf[pl.ds(SUBLANES, ch), :] = x_ref[rows, :].astype(F32)
                prv = x_ref[pl.ds(pl.multiple_of(jnp.maximum(ci * ch - edge, 0), edge), edge), :].astype(F32)[SUBLANES:]
                last = x_ref[pl.ds(t - edge, edge), :].astype(F32)[SUBLANES:]
                wrap = jnp.where(row >= 1, pltpu.roll(last, 1, 0), 0.0)
                buf[pl.ds(0, SUBLANES), :] = jnp.where(ci == 0, wrap, prv)

        def sweep(h0, store):
            def chunk(k, carry):
                ci = n_ch - 1 - k if back else k
                rows = pl.ds(pl.multiple_of(ci * ch, ch), ch)
                dyv = dy_ref[rows, :].astype(BF16)
                gr[...] = jnp.dot(dyv, cre_ref[...], preferred_element_type=F32)
                gi_[...] = jnp.dot(dyv, cim_ref[...], preferred_element_type=F32)
                if store:
                    neighbours(ci, xr_ref, xbr)
                    neighbours(ci, xi_ref, xbi)

                def group(g, cc):
                    gidx = gpc - 1 - g if back else g
                    r0 = pl.multiple_of(gidx * SUBLANES, SUBLANES)
                    hr, hi = cc[0], cc[1]
                    nr = a_r * hr - a_i * hi + gr[pl.ds(r0, SUBLANES), :]
                    ni = a_r * hi + a_i * hr + gi_[pl.ds(r0, SUBLANES), :]
                    if not store:
                        return nr, ni
                    lbr[pl.ds(r0, SUBLANES), :] = nr
                    lbi[pl.ds(r0, SUBLANES), :] = ni
                    x0 = pl.multiple_of(r0 + SUBLANES, SUBLANES) if reverse else r0
                    xpr, xpi = xbr[pl.ds(x0, SUBLANES), :], xbi[pl.ds(x0, SUBLANES), :]
                    return nr, ni, cc[2] + nr * xpr + ni * xpi, cc[3] + ni * xpr - nr * xpi

                carry = lax.fori_loop(0, gpc, group, carry)
                if store:
                    lr_ref[rows, :] = lbr[...].astype(BF16)
                    li_ref[rows, :] = lbi[...].astype(BF16)
                return carry

            return lax.fori_loop(0, n_ch, chunk, h0)

        zero = jnp.zeros((SUBLANES, w), F32)
        er, ei = sweep((zero, zero), False)
        pr, pi = _cpow(ar_ref[...], -ai_ref[...], steps)
        cr, ci0 = _segment_carry(er, ei, pr, pi, back)
        _, _, dar, dai = sweep((cr, ci0, zero, zero), True)
        dar_ref[...] = jnp.sum(dar, axis=0, keepdims=True)
        dai_ref[...] = jnp.sum(dai, axis=0, keepdims=True)

    col = lambda i: (0, i)
    return pl.pallas_call(
        body, name=name, grid=(s // w,),
        in_specs=[pl.BlockSpec((t, c), lambda i: (0, 0)), pl.BlockSpec((c, w), col), pl.BlockSpec((c, w), col),
                  pl.BlockSpec((t, w), col), pl.BlockSpec((t, w), col), pl.BlockSpec((1, w), col), pl.BlockSpec((1, w), col)],
        out_specs=[pl.BlockSpec((t, w), col), pl.BlockSpec((t, w), col), pl.BlockSpec((1, w), col), pl.BlockSpec((1, w), col)],
        out_shape=[_out(t, s, BF16), _out(t, s, BF16), _out(1, s, F32), _out(1, s, F32)],
        scratch_shapes=[pltpu.VMEM((ch, w), F32)] * 4 + [pltpu.VMEM((ch + SUBLANES, w), F32)] * 2,
        compiler_params=_params(("parallel",)),
    )(dyp, c_re, c_im, xr, xi, ar, ai)


def _ssm_finish(y0, y1, sp, skip):
    t, c = sp.shape
    steps = t // SUBLANES
    w = _pick(c, LANES)

    def body(y0_ref, y1_ref, sp_ref, d_ref, y_ref, ys_ref):
        rows = pl.ds(pl.program_id(1), steps, stride=SUBLANES)
        y = y0_ref[rows, :] + y1_ref[rows, :] + sp_ref[rows, :] * d_ref[...]
        y_ref[...] = y
        ys_ref[...] = jax.nn.gelu(y).astype(BF16)

    whole = pl.BlockSpec((t, w), lambda j, k: (0, j))
    seg = pl.BlockSpec((steps, w), lambda j, k: (k, j))
    return pl.pallas_call(
        body, name="ssm_finish", grid=(c // w, SUBLANES),
        in_specs=[whole, whole, whole, pl.BlockSpec((1, w), lambda j, k: (0, j))], out_specs=[seg, seg],
        out_shape=[_out(t, c, F32), _out(t, c, BF16)], compiler_params=_params(("parallel", "arbitrary")),
    )(y0, y1, sp, skip)


def _to_segments(a):
    t, c = a.shape
    return a.reshape(SUBLANES, t // SUBLANES, c).transpose(1, 0, 2).reshape(t, c)


def _from_segments(a):
    t, c = a.shape
    return a.reshape(t // SUBLANES, SUBLANES, c).transpose(1, 0, 2).reshape(t, c)


def _colsum_prod(name, a, b, b_coff=0):
    t, n = a.shape
    tm = _pick(t, 1024, SUBLANES)

    def body(a_ref, b_ref, o_ref):
        @pl.when(pl.program_id(0) == 0)
        def _():
            o_ref[...] = jnp.zeros_like(o_ref)

        o_ref[...] += jnp.sum(a_ref[...].astype(F32) * b_ref[...].astype(F32), axis=0, keepdims=True)

    return pl.pallas_call(
        body, name=name, grid=(t // tm,),
        in_specs=[pl.BlockSpec((tm, n), lambda i: (i, 0)), pl.BlockSpec((tm, n), lambda i: (i, b_coff))],
        out_specs=pl.BlockSpec((1, n), lambda i: (0, 0)), out_shape=_out(1, n, F32),
        compiler_params=_params(("arbitrary",)),
    )(a, b)


def _ssm_maps(arrs, signs):
    n2, hh, p = arrs[0].shape
    g = n2 // 2

    def body(*refs):
        ins, outs = refs[:len(arrs)], refs[len(arrs):]
        for a, (a_ref, sign) in enumerate(zip(ins, signs)):
            for d in range(2):
                o_ref = outs[2 * a + d]
                o_ref[...] = jnp.zeros_like(o_ref)
                for k in range(g):
                    o_ref[pl.ds(k * hh, hh), pl.ds(k * p, p)] = (sign * a_ref[d * g + k]).astype(BF16)

    outs = pl.pallas_call(body, name="ssm_maps", out_shape=[_out(g * hh, g * p, BF16)] * (2 * len(arrs)))(*arrs)
    return [outs[2 * a:2 * a + 2] for a in range(len(arrs))]


def _softmax(qh, kh, scale):
    s = lax.dot_general(qh, kh, _DIMS["nt"], preferred_element_type=F32) * scale
    e = jnp.exp(s - jnp.max(s, axis=-1, keepdims=True))
    return e / jnp.sum(e, axis=-1, keepdims=True)


def _attn_fwd(q, kv):
    t, d = q.shape
    mm_ = kv.shape[0]
    hd = d // N_XHEADS
    scale = 1.0 / math.sqrt(hd)
    tm = _pick(t, 1024, SUBLANES)

    def body(q_ref, kv_ref, o_ref):
        for h in range(N_XHEADS):
            sl = pl.ds(h * hd, hd)
            p = _softmax(q_ref[:, sl], kv_ref[:, sl], scale)
            o_ref[:, sl] = jnp.dot(p.astype(BF16), kv_ref[:, pl.ds(d + h * hd, hd)],
                                   preferred_element_type=F32).astype(BF16)

    return pl.pallas_call(
        body, name="attn_fwd", grid=(t // tm,),
        in_specs=[pl.BlockSpec((tm, d), lambda i: (i, 0)), pl.BlockSpec((mm_, 2 * d), lambda i: (0, 0))],
        out_specs=pl.BlockSpec((tm, d), lambda i: (i, 0)), out_shape=_out(t, d, BF16),
        compiler_params=_params(("parallel",)),
    )(q, kv)


def _attn_bwd(q, kv, do):
    t, d = q.shape
    mm_ = kv.shape[0]
    hd = d // N_XHEADS
    scale = 1.0 / math.sqrt(hd)
    tm = _pick(t, 1024, SUBLANES)

    def body(q_ref, kv_ref, do_ref, dq_ref, dkv_ref):
        @pl.when(pl.program_id(0) == 0)
        def _():
            dkv_ref[...] = jnp.zeros_like(dkv_ref)

        for h in range(N_XHEADS):
            sl = pl.ds(h * hd, hd)
            vsl = pl.ds(d + h * hd, hd)
            qh, kh, doh = q_ref[:, sl], kv_ref[:, sl], do_ref[:, sl]
            p = _softmax(qh, kh, scale)
            dp = lax.dot_general(doh, kv_ref[:, vsl], _DIMS["nt"], preferred_element_type=F32)
            dkv_ref[:, vsl] += lax.dot_general(p.astype(BF16), doh, _DIMS["tn"], preferred_element_type=F32)
            ds = (p * (dp - jnp.sum(dp * p, axis=-1, keepdims=True)) * scale).astype(BF16)
            dq_ref[:, sl] = jnp.dot(ds, kh, preferred_element_type=F32).astype(BF16)
            dkv_ref[:, sl] += lax.dot_general(ds, qh, _DIMS["tn"], preferred_element_type=F32)

    row = pl.BlockSpec((tm, d), lambda i: (i, 0))
    full = pl.BlockSpec((mm_, 2 * d), lambda i: (0, 0))
    return pl.pallas_call(
        body, name="attn_bwd", grid=(t // tm,), in_specs=[row, full, row], out_specs=[row, full],
        out_shape=[_out(t, d, BF16), _out(mm_, 2 * d, F32)], compiler_params=_params(("arbitrary",)),
    )(q, kv, do)


def _ew(name, fn, ins, outs, rows_pref=256):
    r, c = ins[0].shape
    tr = _pick(r, rows_pref, SUBLANES)
    ni = len(ins)

    def body(*refs):
        res = fn(*[x[...] for x in refs[:ni]])
        for o_ref, v in zip(refs[ni:], res):
            o_ref[...] = v.astype(o_ref.dtype)

    blk = pl.BlockSpec((tr, c), lambda i: (i, 0))
    return pl.pallas_call(
        body, name=name, grid=(r // tr,), in_specs=[blk] * ni, out_specs=[blk] * len(outs),
        out_shape=[_out(r, c, dt) for dt in outs], compiler_params=_params(("parallel",)),
    )(*ins)


def _sum_slots(name, a, dtype):
    s, r, c = a.shape
    tr = _pick(r, 256, SUBLANES)

    def body(a_ref, o_ref):
        acc = a_ref[0].astype(F32)
        for k in range(1, s):
            acc = acc + a_ref[k].astype(F32)
        o_ref[...] = acc.astype(o_ref.dtype)

    return pl.pallas_call(
        body, name=name, grid=(r // tr,), in_specs=[pl.BlockSpec((s, tr, c), lambda i: (0, i, 0))],
        out_specs=pl.BlockSpec((tr, c), lambda i: (i, 0)), out_shape=_out(r, c, dtype),
        compiler_params=_params(("parallel",)),
    )(a)


def _adamw_step(wv, gv, mv, vv):
    bc1 = 1.0 - ADAM_B1 ** ADAM_STEP
    bc2 = 1.0 - ADAM_B2 ** ADAM_STEP
    m2 = ADAM_B1 * mv + (1.0 - ADAM_B1) * gv
    v2 = ADAM_B2 * vv + (1.0 - ADAM_B2) * (gv * gv)
    delta = -ADAM_LR * ((m2 / bc1) / (jnp.sqrt(v2 / bc2) + ADAM_EPS) + ADAM_WD * wv)
    return delta, m2, v2


def _adamw_group(name, items, transposed):
    k, r = items[0][0].shape
    if transposed and r % LANES != 0:
        rows = _adamw_group(name, [(w.T, g, m.T, v.T) for w, g, m, v in items], False)
        return [[a.T for a in item] for item in rows]
    tk = _pick(k, max(SUBLANES, ADAMW_STEP_WORDS // (r * len(items))), SUBLANES)
    n_out = 4 if transposed else 3

    def body(*refs):
        ins, outs = refs[:4 * len(items)], refs[4 * len(items):]
        for i in range(len(items)):
            wv, gv, mv, vv = (a[...] for a in ins[4 * i:4 * i + 4])
            if transposed:
                gv = gv.T
            res = _adamw_step(wv, gv, mv, vv) + ((gv,) if transposed else ())
            for o_ref, val in zip(outs[n_out * i:n_out * (i + 1)], res):
                o_ref[...] = val

    blk = pl.BlockSpec((tk, r), lambda j: (j, 0))
    g_blk = pl.BlockSpec((r, tk), lambda j: (0, j)) if transposed else blk
    res = pl.pallas_call(
        body, name=name, grid=(k // tk,), in_specs=[blk, g_blk, blk, blk] * len(items),
        out_specs=[blk] * (n_out * len(items)), out_shape=[pltpu.HBM((k, r), F32)] * (n_out * len(items)),
        compiler_params=_params(("parallel",)),
    )(*[pltpu.with_memory_space_constraint(a, pltpu.HBM) for item in items for a in item])
    return [list(res[n_out * i:n_out * (i + 1)]) + ([] if transposed else [items[i][1]]) for i in range(len(items))]


def _allgather(name, arrs):
    n = len(arrs)

    def body(*refs):
        ins, outs = refs[:n], refs[n:2 * n]
        send_sems, recv_sems, local_sems = refs[2 * n:]
        x, y, c = lax.axis_index("x"), lax.axis_index("y"), lax.axis_index("c")
        me, sibling = (x, y, c), (x, y, 1 - c)
        chips = [(1 - x, y), (x, 1 - y), (1 - x, 1 - y)]

        def rows(a, px, py, pc):
            r = ins[a].shape[0]
            return outs[a].at[pl.ds((4 * px + 2 * py + pc) * r, r), :]

        def copy(a, k, block, to, src=None):
            return pltpu.make_async_remote_copy(
                src_ref=rows(a, *block) if src is None else src, dst_ref=rows(a, *block),
                send_sem=send_sems.at[a, k], recv_sem=recv_sems.at[a, k], device_id=to, device_id_type=MESH)

        mine = [pltpu.make_async_copy(ins[a], rows(a, *me), local_sems.at[a]) for a in range(n)]
        for cp in mine:
            cp.start()
        first = []
        for a in range(n):
            first.append(copy(a, 0, me, sibling, src=ins[a]))
            first += [copy(a, 1 + j, me, (*chip, c), src=ins[a]) for j, chip in enumerate(chips)]
        for cp in first:
            cp.start()
        passed = []
        for j, chip in enumerate(chips):
            for a in range(n):
                copy(a, 1 + j, (*chip, c), me).wait_recv()
                cp = copy(a, 4 + j, (*chip, c), sibling)
                cp.start()
                passed.append(cp)
        for a in range(n):
            copy(a, 0, sibling, me).wait_recv()
            for j, chip in enumerate(chips):
                copy(a, 4 + j, (*chip, 1 - c), me).wait_recv()
        for cp in first + passed:
            cp.wait_send()
        for cp in mine:
            cp.wait()

    return pl.pallas_call(
        body, name=name, in_specs=[ANY] * n, out_specs=[ANY] * n,
        out_shape=[_out(N_DEV * a.shape[0], a.shape[1], a.dtype) for a in arrs],
        scratch_shapes=[pltpu.SemaphoreType.DMA((n, 7)), pltpu.SemaphoreType.DMA((n, 7)), pltpu.SemaphoreType.DMA((n,))],
    )(*arrs)


def _cores_start(name, blocks):
    n = len(blocks)
    c = blocks[0].shape[2]
    r = sum(b.shape[1] for b in blocks)

    def build(src_refs, land_refs, send_sems, recv_sems):
        x, y, cc = lax.axis_index("x"), lax.axis_index("y"), lax.axis_index("c")
        remote, off = [], 0
        for a, src in enumerate(src_refs):
            rows = pl.ds(off, src.shape[1])
            off += src.shape[1]
            for q in range(4):
                remote.append(pltpu.make_async_remote_copy(
                    src_ref=src.at[2 * q + (1 - cc)], dst_ref=land_refs[0].at[q, rows], send_sem=send_sems.at[4 * a + q],
                    recv_sem=recv_sems.at[4 * a + q], device_id=(x, y, 1 - cc), device_id_type=MESH))
        return remote, []

    return _split_start(name, [(blocks, [jax.ShapeDtypeStruct((4, r, c), blocks[0].dtype)], 4 * n, 0, build)])[0]


def _peer(k, x, y, c):
    return (1 - x if k & 4 else x, 1 - y if k & 2 else y, 1 - c if k & 1 else c)


def _split_start(name, groups, after=None):
    pins = [] if after is None else [after]
    bufs, sem_shapes, spans = [], [], []
    for srcs, land_shapes, n_remote, n_local, _ in groups:
        sems = [pltpu.SemaphoreType.DMA((n_remote,)), pltpu.SemaphoreType.DMA((n_remote,))]
        sems += [pltpu.SemaphoreType.DMA((n_local,))] if n_local else []
        spans.append((len(bufs), len(srcs), len(land_shapes), len(sem_shapes), len(sems)))
        bufs += [pltpu.with_memory_space_constraint(a, pltpu.HBM) for a in srcs]
        bufs += [pltpu.with_memory_space_constraint(lax.empty(s.shape, s.dtype), pltpu.HBM) for s in land_shapes]
        sem_shapes += sems
    n_buf, n_sem = len(bufs), len(sem_shapes)

    def body(*refs):
        buf_refs, sem_refs, token = refs[:n_buf], refs[n_buf + len(pins):n_buf + len(pins) + n_sem], refs[-1]
        for (b0, ns, nl, s0, k), group in zip(spans, groups):
            remote, local = group[4](buf_refs[b0:b0 + ns], buf_refs[b0 + ns:b0 + ns + nl], *sem_refs[s0:s0 + k])
            for cp in local + remote:
                cp.start()
        token[...] = jnp.zeros_like(token)

    outs = pl.pallas_call(
        body, name=name,
        out_shape=sem_shapes + [pltpu.HBM(b.shape, b.dtype) for b in bufs] + [jax.ShapeDtypeStruct((SUBLANES, LANES), F32)],
        in_specs=[HBM] * n_buf + [ANY] * len(pins),
        out_specs=[SEM] * n_sem + [HBM] * n_buf + [pl.BlockSpec(memory_space=pltpu.VMEM)],
        input_output_aliases={i: n_sem + i for i in range(n_buf)},
        compiler_params=pltpu.CompilerParams(has_side_effects=SIDE_EFFECT),
    )(*bufs, *pins)
    return [dict(sems=list(outs[s0:s0 + k]), bufs=list(outs[n_sem + b0:n_sem + b0 + ns + nl]), token=outs[-1],
                 build=group[4], ns=ns) for (b0, ns, nl, s0, k), group in zip(spans, groups)]


def _split_wait(name, started, after):
    ns, n_buf, n_sem = started["ns"], len(started["bufs"]), len(started["sems"])

    def body(*refs):
        src_refs, land_refs = refs[:ns], refs[ns:n_buf]
        sems = refs[n_buf:n_buf + n_sem]
        remote, local = started["build"](src_refs, land_refs, *sems)
        for cp in local:
            cp.wait()
        for cp in remote:
            cp.wait_send()
            cp.wait_recv()

    outs = pl.pallas_call(
        body, name=name, out_shape=[pltpu.HBM(b.shape, b.dtype) for b in started["bufs"]],
        in_specs=[HBM] * n_buf + [SEM] * n_sem + [ANY], out_specs=[HBM] * n_buf,
        input_output_aliases={i: i for i in range(n_buf)},
        compiler_params=pltpu.CompilerParams(has_side_effects=SIDE_EFFECT),
    )(*started["bufs"], *started["sems"], after)
    return list(outs[:ns]), list(outs[ns:])


def _gather_group(shards):
    m = len(shards)

    def build(src_refs, land_refs, send_sems, recv_sems, local_sems):
        x, y, c = lax.axis_index("x"), lax.axis_index("y"), lax.axis_index("c")
        remote, local = [], []
        for j in range(m):
            r = src_refs[j].shape[0]
            dst = land_refs[j].at[pl.ds((4 * x + 2 * y + c) * r, r), :]
            local.append(pltpu.make_async_copy(src_refs[j], dst, local_sems.at[j]))
            for k in range(1, N_DEV):
                remote.append(pltpu.make_async_remote_copy(
                    src_ref=src_refs[j], dst_ref=dst, send_sem=send_sems.at[7 * j + k - 1],
                    recv_sem=recv_sems.at[7 * j + k - 1], device_id=_peer(k, x, y, c), device_id_type=MESH))
        return remote, local

    lands = [jax.ShapeDtypeStruct((N_DEV * a.shape[0], a.shape[1]), a.dtype) for a in shards]
    return shards, lands, 7 * m, m, build


def _slots_start(name, a):
    def build(src_refs, land_refs, send_sems, recv_sems, local_sems):
        x, y, c = lax.axis_index("x"), lax.axis_index("y"), lax.axis_index("c")
        dst = land_refs[0].at[4 * x + 2 * y + c]
        local = [pltpu.make_async_copy(src_refs[0], dst, local_sems.at[0])]
        remote = [pltpu.make_async_remote_copy(
            src_ref=src_refs[0], dst_ref=dst, send_sem=send_sems.at[k - 1], recv_sem=recv_sems.at[k - 1],
            device_id=_peer(k, x, y, c), device_id_type=MESH) for k in range(1, N_DEV)]
        return remote, local

    return _split_start(name, [([a], [jax.ShapeDtypeStruct((N_DEV,) + a.shape, a.dtype)], 7, 1, build)])[0]


def _chips_start(name, p):
    _, r, c = p.shape
    nck = r // GRAD_ROW_TILE

    def build(src_refs, land_refs, send_sems, recv_sems):
        x, y, cc = lax.axis_index("x"), lax.axis_index("y"), lax.axis_index("c")
        remote = []
        for k in range(1, 4):
            px = 1 - x if k >> 1 else x
            py = 1 - y if k & 1 else y
            for j in range(nck):
                rows = pl.ds(j * GRAD_ROW_TILE, GRAD_ROW_TILE)
                remote.append(pltpu.make_async_remote_copy(
                    src_ref=src_refs[0].at[2 * px + py, rows], dst_ref=land_refs[0].at[k - 1, rows],
                    send_sem=send_sems.at[(k - 1) * nck + j], recv_sem=recv_sems.at[(k - 1) * nck + j],
                    device_id=(px, py, cc), device_id_type=MESH))
        return remote, []

    return _split_start(name, [([p], [jax.ShapeDtypeStruct((3, r, c), p.dtype)], 3 * nck, 0, build)])[0]


def _chip_sum(name, p, recv, chip):
    _, r, c = p.shape
    tr = _pick(r, 5 * GRAD_ROW_TILE, GRAD_ROW_TILE)

    def body(chip_ref, p_ref, r_ref, o_ref):
        acc = p_ref[...].astype(F32)
        for k in range(3):
            acc = acc + r_ref[k].astype(F32)
        o_ref[...] = acc

    return pl.pallas_call(
        body, name=name,
        grid_spec=pltpu.PrefetchScalarGridSpec(
            num_scalar_prefetch=1, grid=(r // tr,),
            in_specs=[pl.BlockSpec((None, tr, c), lambda i, chip_ref: (chip_ref[0], i, 0)),
                      pl.BlockSpec((3, tr, c), lambda i, chip_ref: (0, i, 0))],
            out_specs=pl.BlockSpec((tr, c), lambda i, chip_ref: (i, 0))),
        out_shape=_out(r, c, F32), compiler_params=_params(("parallel",)),
    )(chip, p, recv)


def _local_step(x, mem, tgt, wt, sm, ev=None):
    t, d = x.shape
    n_mem = mem.shape[0]
    d_pool = sm["pool_scale"].shape[1]
    d_ssm = sm["ssm_d"].shape[1]
    _, sg, sp, sh = sm["ssm_b_re"].shape
    n_state = sg * sp
    gb, gs = {}, {}

    def emit(name, **kw):
        return ev(name, **kw) if ev is not None else None

    n1 = _rms_fwd("ffn1_norm", x, sm["ffn1_norm"])
    emit("ffn1_norm_done", marker=n1)
    def ffn1_down(hid):
        emit("ffn1_up_done", marker=hid)
        return wt["ffn1_w_down"]

    h1, ffn1_saved = _ffn_fwd("ffn1", x, n1, wt["ffn1_w_gate"], wt["ffn1_w_up"], ffn1_down)
    emit("ffn1_fwd_done", marker=h1)
    u = _rms_fwd("mix_norm", h1, sm["mix_norm"])
    d_in = wt["w_in"].shape[0]
    tm, tn = _pick(t, 2048), _pick(d_in, 1408)
    proj = _mm1("in_proj", "nt", u, wt["w_in"], t, d_in, tm, tn, F32)
    off_s = d_pool // d_ssm
    off_gp = (d_pool + d_ssm)
    off_gs = off_gp + d

    pool_w_bf = sm["pool_w"].astype(BF16)
    pooled, pm = _pool_fwd(proj, pool_w_bf, sm["pool_scale"])

    by_p = lambda a: jnp.swapaxes(a, -1, -2).reshape(2 * sg, sh, sp)
    disc_args = [sm["ssm_a_re"].reshape(2 * sg, 1, sp), sm["ssm_a_im"].reshape(2 * sg, 1, sp),
                 sm["ssm_log_dt"].reshape(2 * sg, 1, 1), by_p(sm["ssm_b_re"]), by_p(sm["ssm_b_im"])]
    abr, abi, bbr, bbi = _ssm_disc(disc_args)
    abr2, abi2 = abr.reshape(2, n_state), abi.reshape(2, n_state)
    b_re, b_im, c_re, c_im = _ssm_maps(
        [bbr, bbi, sm["ssm_c_re"].reshape(2 * sg, sh, sp), sm["ssm_c_im"].reshape(2 * sg, sh, sp)], [1.0, 1.0, 1.0, -1.0])
    sp32 = _to_segments(proj[:, d_pool:d_pool + d_ssm])
    xs, y_parts = [], []
    for dr in range(2):
        xr, xi, y_part = _ssm_fwd(f"ssm_fwd{dr}", sp32, b_re[dr], b_im[dr], c_re[dr], c_im[dr], abr2[dr:dr + 1],
                                  abi2[dr:dr + 1], reverse=(dr == 1))
        xs.append((xr, xi))
        y_parts.append(y_part)
    y, ys = _ssm_finish(y_parts[0], y_parts[1], sp32, sm["ssm_d"])
    tmy = _pick(t, 1024)
    emit("mix_in_done", marker=ys)

    tmm, tnm, tnx = _pick(t, 2048), _pick(d, 256), _pick(d, 512)
    gp_spec = _tile(tmm, tnm, off_gp // tnm)
    gs_spec = _tile(tmm, tnm, off_gs // tnm)

    def merge_epi(accs, gpv, gsv):
        z_pool, val, gate = accs
        return (jax.nn.sigmoid(gpv) * z_pool + jax.nn.sigmoid(gsv) * (val * jax.nn.sigmoid(gate)),)

    merged = _mm("mix_merge", "nt", [pm, ys], [wt["w_pool_proj"], wt["w_glu_val"], wt["w_glu_gate"]],
                 [[(0, 0)], [(1, 1)], [(1, 2)]], t, d, tmm, tnm, [(proj, gp_spec), (proj, gs_spec)], merge_epi,
                 [(_out(t, d, BF16), None)])[0]
    res_epi = lambda accs, hin: (hin + accs[0],)
    h2 = _mm("mix_out", "nn", [merged], [wt["w_mix_out"]], [[(0, 0)]], t, d, tmm, tnx, [(h1, _tile(tmm, tnx))],
             res_epi, [(_out(t, d, F32), None)])[0]

    un = _rms_fwd("xattn_norm", h2, sm["xattn_norm"])
    mn = _rms_fwd("mem_norm", mem, sm["mem_norm"])
    emit("mix_done", marker=un)
    q = _mm1("xattn_q", "nn", un, wt["w_q"], t, d, tmm, tnx, BF16)
    kv = _mm1("xattn_kv", "nt", mn, wt["w_kv"], n_mem, 2 * d, n_mem, _pick(2 * d, 512), BF16)
    o = _attn_fwd(q, kv)
    h3 = _mm("xattn_out", "nn", [o], [wt["w_xo"]], [[(0, 0)]], t, d, tmm, tnx, [(h2, _tile(tmm, tnx))],
             res_epi, [(_out(t, d, F32), None)])[0]

    n2 = _rms_fwd("ffn2_norm", h3, sm["ffn2_norm"])
    emit("xattn_done", marker=n2)
    h4, ffn2_saved = _ffn_fwd("ffn2", h3, n2, wt["ffn2_w_gate"], wt["ffn2_w_up"], wt["ffn2_w_down"])

    dh4, dh4_bf, gs["final_norm"], loss = _loss_head(h4, sm["final_norm"], tgt)
    dh3, dh3_bf, gs["ffn2_norm"], gb["ffn2_w_gate"], gb["ffn2_w_up"], gb["ffn2_w_down"] = _ffn_bwd(
        "ffn2", h3, sm["ffn2_norm"], wt["ffn2_w_gate"], wt["ffn2_w_up"], wt["ffn2_w_down"], ffn2_saved, dh4, dh4_bf)

    tw = _pick(d, 1024)
    do = _mm1("xattn_do", "nt", dh3_bf, wt["w_xo"], t, d, tmm, tnx, BF16)
    gb["w_xo"] = _mm1("xattn_dwxo", "tn", o, dh3_bf, d, d, tw, tnx, BF16)
    dq, dkv = _attn_bwd(q, kv, do)
    gb["w_q"] = _mm1("xattn_dwq", "tn", un, dq, d, d, tw, tnx, BF16)
    dun = _mm1("xattn_dun", "nt", dq, wt["w_q"], t, d, tmm, tnx, F32)
    dh2, dh2_bf, gs["xattn_norm"] = _rms_bwd("xattn_norm_bwd", h2, sm["xattn_norm"], dun, dh3)
    gb["w_kv"] = _mm1("xattn_dwkv", "tn", dkv, mn, 2 * d, d, _pick(2 * d, 512), d, BF16)
    dmn = _mm1("xattn_dmn", "nn", dkv, wt["w_kv"], n_mem, d, n_mem, tnx, F32)
    gs["mem_norm"] = _rms_bwd("mem_norm_bwd", mem, sm["mem_norm"], dmn)

    gb["w_mix_out"] = _mm1("mix_dwout", "tn", merged, dh2_bf, d, d, tw, tnx, BF16)

    def merge_bwd_epi(accs, gpv, gsv):
        dmerged, z_pool, val, gate = accs
        sp_, ss_, sg_ = jax.nn.sigmoid(gpv), jax.nn.sigmoid(gsv), jax.nn.sigmoid(gate)
        glu = val * sg_
        dz_pool = dmerged * sp_
        dg_pool = dmerged * z_pool * (sp_ * (1.0 - sp_))
        dz_ssm = dmerged * ss_
        dg_ssm = dmerged * glu * (ss_ * (1.0 - ss_))
        dval = dz_ssm * sg_
        dgate = dz_ssm * glu * (1.0 - sg_)
        return dz_pool, dg_pool, dg_ssm, dval, dgate

    dz_pool, dg_pool, dg_ssm, dval, dgate = _mm(
        "mix_merge_bwd", "nt", [dh2_bf, pm, ys], [wt["w_mix_out"], wt["w_pool_proj"], wt["w_glu_val"], wt["w_glu_gate"]],
        [[(0, 0)], [(1, 1)], [(2, 2)], [(2, 3)]], t, d, tmm, tnm, [(proj, gp_spec), (proj, gs_spec)], merge_bwd_epi,
        [(_out(t, d, BF16), None)] * 5)
    gb["w_pool_proj"] = _mm1("pool_dwproj", "tn", dz_pool, pm, d, d_pool, tw, d_pool, BF16)
    gb["w_glu_val"] = _mm1("glu_dwval", "tn", dval, ys, d, d_ssm, tw, d_ssm, BF16)
    gb["w_glu_gate"] = _mm1("glu_dwgate", "tn", dgate, ys, d, d_ssm, tw, d_ssm, BF16)

    def gelu_bwd_epi(accs, yv):
        _, vjp = jax.vjp(jax.nn.gelu, yv)
        return (vjp(accs[0])[0],)

    dy = _mm("glu_dy", "nn", [dval, dgate], [wt["w_glu_val"], wt["w_glu_gate"]], [[(0, 0), (1, 1)]], t, d_ssm, tmy, d_ssm,
             [(y, _tile(tmy, d_ssm))], gelu_bwd_epi, [(_out(t, d_ssm, F32), None)])[0]
    gs["ssm_d"] = _colsum_prod("ssm_dd", dy, proj, b_coff=off_s)
    dyp = _to_segments(dy)
    d_abr, d_abi, d_bbr, d_bbi, d_cre, d_cim, lams = [], [], [], [], [], [], []
    ts = _pick(n_state, 512)

    def fold_diag(accs):
        first = pl.program_id(1) * (ts // sp)
        row_group = lax.broadcasted_iota(jnp.int32, (d_ssm, sp), 0) // sh
        folded = []
        for acc in accs:
            out = jnp.zeros((d_ssm, sp), F32)
            for k in range(ts // sp):
                out = out + jnp.where(row_group == first + k, acc[:, sp * k:sp * (k + 1)], 0.0)
            folded.append(out)
        return tuple(folded)

    for dr in range(2):
        lr, li, dar, dai = _ssm_bwd(f"ssm_bwd{dr}", dyp, c_re[dr], c_im[dr], xs[dr][0], xs[dr][1], abr2[dr:dr + 1],
                                    abi2[dr:dr + 1], reverse=(dr == 1))
        d_abr.append(dar)
        d_abi.append(dai)
        lams += [lr, li]
        maps = _mm(f"ssm_dmaps{dr}", "tn", [sp32, dyp], [lr, li, xs[dr][0], xs[dr][1]],
                   [[(0, 0)], [(0, 1)], [(1, 2)], [(1, 3)]], d_ssm, n_state, d_ssm, ts, [], fold_diag,
                   [(_out(n_state // ts * d_ssm, sp, F32), pl.BlockSpec((d_ssm, sp), lambda i, j: (j, 0)))] * 4)
        for acc, m in zip((d_bbr, d_bbi, d_cre, d_cim), maps):
            acc.append(jnp.sum(m.reshape(n_state // ts, sg, sh, sp), axis=0))
    ds = _from_segments(_mm(
        "ssm_ds", "nt", lams, [b_re[0], b_im[0], b_re[1], b_im[1]], [[(k, k) for k in range(4)]], t, d_ssm, tmy,
        d_ssm, [(dyp, _tile(tmy, d_ssm)), (sm["ssm_d"], _rowvec(d_ssm))],
        lambda accs, dyv, dv: (dyv * dv + accs[0],), [(_out(t, d_ssm, BF16), None)])[0])
    cots = [jnp.concatenate(d_abr, axis=0).reshape(2 * sg, 1, sp), jnp.concatenate(d_abi, axis=0).reshape(2 * sg, 1, sp),
            jnp.concatenate(d_bbr, axis=0), jnp.concatenate(d_bbi, axis=0)]
    d_are, d_aim, d_ldt, d_bre, d_bim = _ssm_disc_bwd(disc_args, cots)
    gs["ssm_a_re"] = d_are.reshape(2, sg, sp)
    gs["ssm_a_im"] = d_aim.reshape(2, sg, sp)
    gs["ssm_log_dt"] = d_ldt.reshape(2, sg)
    from_p = lambda a: jnp.swapaxes(a.reshape(2, sg, sh, sp), -1, -2)
    gs["ssm_b_re"], gs["ssm_b_im"] = from_p(d_bre), from_p(d_bim)
    gs["ssm_c_re"] = jnp.stack(d_cre, axis=0)
    gs["ssm_c_im"] = -jnp.stack(d_cim, axis=0)

    dpm = _mm1("pool_dpm", "nn", dz_pool, wt["w_pool_proj"], t, d_pool, tmm, _pick(d_pool, 256), F32)
    dp, gs["pool_w"], gs["pool_scale"] = _pool_bwd(pooled, dpm, pool_w_bf, sm["pool_scale"])

    w_in = wt["w_in"]
    parts = [(dp, 0, d_pool), (ds, d_pool, d_ssm), (dg_pool, off_gp, d), (dg_ssm, off_gs, d)]
    w_in_parts = [w_in[o0:o0 + width] for _, o0, width in parts]
    gb["w_in"] = jnp.concatenate(
        [_mm1(f"in_proj_dw{k}", "tn", p_[0], u, p_[2], d, _pick(p_[2], 1024), tnx, BF16) for k, p_ in enumerate(parts)], axis=0)
    pin = emit("grads_main", gb=gb)
    du = _mm("in_proj_du", "nn", [p_[0] for p_ in parts], w_in_parts, [[(k, k) for k in range(4)]], t, d, tmm, tnx, [],
             lambda accs: (accs[0],), [(_out(t, d, F32), None)], after=pin)[0]
    dh1, dh1_bf, gs["mix_norm"] = _rms_bwd("mix_norm_bwd", h1, sm["mix_norm"], du, dh2)
    pin = emit("small_early", gs=gs, loss=loss)

    def ffn1_weights_done(d_wg, d_wu, d_wd):
        gb["ffn1_w_gate"], gb["ffn1_w_up"], gb["ffn1_w_down"] = d_wg, d_wu, d_wd
        return emit("grads_ffn1", gb=gb)

    dx, _, gs["ffn1_norm"], _, _, _ = _ffn_bwd(
        "ffn1", x, sm["ffn1_norm"], wt["ffn1_w_gate"], wt["ffn1_w_up"], wt["ffn1_w_down"], ffn1_saved, dh1, dh1_bf,
        weights_done=ffn1_weights_done, after=pin)
    return loss, dx, gb, gs


WEIGHTS = ["ffn1_norm", "ffn1_w_gate", "ffn1_w_up", "ffn1_w_down", "mix_norm", "w_in", "pool_w", "pool_scale",
           "w_pool_proj", "ssm_a_re", "ssm_a_im", "ssm_log_dt", "ssm_b_re", "ssm_b_im", "ssm_c_re", "ssm_c_im", "ssm_d",
           "w_glu_val", "w_glu_gate", "w_mix_out", "xattn_norm", "mem_norm", "w_q", "w_kv", "w_xo", "ffn2_norm",
           "ffn2_w_gate", "ffn2_w_up", "ffn2_w_down", "final_norm"]
COL_SHARDED = ["ffn1_w_gate", "ffn1_w_up", "w_in", "w_pool_proj", "w_glu_val", "w_glu_gate", "w_kv", "ffn2_w_gate",
               "ffn2_w_up"]
ROW_SHARDED = ["ffn1_w_down", "w_mix_out", "w_q", "w_xo", "ffn2_w_down"]
BIG = [n for n in WEIGHTS if n in COL_SHARDED or n in ROW_SHARDED]
SMALL = [n for n in WEIGHTS if n not in BIG]
FFN1_BIG = ["ffn1_w_gate", "ffn1_w_up", "ffn1_w_down"]
MAIN_BIG = [n for n in BIG if n not in FFN1_BIG]
GATHER_PLAN = [("ffn1_up_done", ["ffn1_w_down"]), ("ffn1_fwd_done", ["w_in"]),
               ("mix_in_done", ["w_pool_proj", "w_glu_val", "w_glu_gate", "w_mix_out"]),
               ("mix_done", ["w_q", "w_kv", "w_xo"]), ("xattn_done", ["ffn2_w_gate", "ffn2_w_up", "ffn2_w_down"])]
MINOR_SWAPPED = ["ssm_b_re", "ssm_b_im"]
LATE_SMALL = "ffn1_norm"
EARLY_SMALL = [n for n in SMALL if n != LATE_SMALL]
PACK_ROWS = SUBLANES * LANES
GRAD_ROW_TILE = 256
ADAMW_STEP_WORDS = 1 << 19


def _to_rows(name, w):
    return w.T if name in COL_SHARDED else w


def _pack_small(vals):
    flat = []
    for v in vals:
        f = v.reshape(-1)
        flat.append(jnp.pad(f, (0, (-f.shape[0]) % PACK_ROWS)))
    total = sum(f.shape[0] for f in flat)
    flat.append(jnp.zeros(((-total) % (GRAD_ROW_TILE * LANES),), F32))
    return jnp.concatenate(flat).reshape(-1, LANES)


def _unpack_small(packed, shapes):
    out, row = [], 0
    for shp in shapes:
        size = math.prod(shp)
        rows = -(-size // PACK_ROWS) * SUBLANES
        out.append(packed[row:row + rows].reshape(-1)[:size].reshape(shp))
        row += rows
    return out


def kernel(x, mem, ffn1_norm, ffn1_w_gate, ffn1_w_up, ffn1_w_down, mix_norm, w_in, pool_w, pool_scale, w_pool_proj, ssm_a_re, ssm_a_im, ssm_log_dt, ssm_b_re, ssm_b_im, ssm_c_re, ssm_c_im, ssm_d, w_glu_val, w_glu_gate, w_mix_out, xattn_norm, mem_norm, w_q, w_kv, w_xo, ffn2_norm, ffn2_w_gate, ffn2_w_up, ffn2_w_down, final_norm, loss_target, m_ffn1_norm, m_ffn1_w_gate, m_ffn1_w_up, m_ffn1_w_down, m_mix_norm, m_w_in, m_pool_w, m_pool_scale, m_w_pool_proj, m_ssm_a_re, m_ssm_a_im, m_ssm_log_dt, m_ssm_b_re, m_ssm_b_im, m_ssm_c_re, m_ssm_c_im, m_ssm_d, m_w_glu_val, m_w_glu_gate, m_w_mix_out, m_xattn_norm, m_mem_norm, m_w_q, m_w_kv, m_w_xo, m_ffn2_norm, m_ffn2_w_gate, m_ffn2_w_up, m_ffn2_w_down, m_final_norm, v_ffn1_norm, v_ffn1_w_gate, v_ffn1_w_up, v_ffn1_w_down, v_mix_norm, v_w_in, v_pool_w, v_pool_scale, v_w_pool_proj, v_ssm_a_re, v_ssm_a_im, v_ssm_log_dt, v_ssm_b_re, v_ssm_b_im, v_ssm_c_re, v_ssm_c_im, v_ssm_d, v_w_glu_val, v_w_glu_gate, v_w_mix_out, v_xattn_norm, v_mem_norm, v_w_q, v_w_kv, v_w_xo, v_ffn2_norm, v_ffn2_w_gate, v_ffn2_w_up, v_ffn2_w_down, v_final_norm):
    given = dict(locals())
    wts = {n: given[n] for n in WEIGHTS}
    moms = {n: (given["m_" + n], given["v_" + n]) for n in WEIGHTS}
    x2, mem2, tgt2 = x[0], mem[0], loss_target[0]
    d = x2.shape[1]
    chip = (2 * lax.axis_index("x") + lax.axis_index("y")).astype(jnp.int32).reshape(1)

    def full_form(n, f):
        shard = wts[n][0].shape
        return f.reshape(N_DEV * shard[1], shard[0]) if n in COL_SHARDED else f.reshape(N_DEV * shard[0], shard[1])

    shards = {n: _to_rows(n, wts[n][0]).astype(BF16) for n in BIG}
    first = FFN1_BIG[:2]
    wt = {n: full_form(n, f) for n, f in zip(first, _allgather("weight_allgather_first", [shards[n] for n in first]))}
    started = _split_start("weight_gather_start", [_gather_group([shards[n] for n in names]) for _, names in GATHER_PLAN],
                           after=wt[first[0]])
    gathers = {event: (names, st) for (event, names), st in zip(GATHER_PLAN, started)}
    sm = {n: (wts[n].reshape(1, -1) if wts[n].ndim <= 2 else wts[n][0]) for n in SMALL}
    sm["ffn1_norm"] = sm["ffn1_norm"] + started[0]["token"][0, 0]

    pending = {}

    def reduce_start(tag, names, gb):
        blocks = [gb[n].reshape(N_DEV, -1, d) for n in names]
        pad_rows = (-sum(b.shape[1] for b in blocks)) % GRAD_ROW_TILE
        pad = [jnp.zeros((N_DEV, pad_rows, d), BF16)] if pad_rows else []
        started = _cores_start("grad_exchange_cores_start_" + tag, blocks + pad)
        own = jnp.concatenate([lax.dynamic_index_in_dim(b.reshape(4, 2, b.shape[1], d), lax.axis_index("c"), 1, False)
                               for b in started["bufs"][:len(blocks + pad)]], axis=1)
        _, (recv,) = _split_wait("grad_exchange_cores_wait_" + tag, started, own)
        rows_all = own.shape[1]
        pair = _ew("grad_pair_sum_" + tag, lambda a, b: (a.astype(F32) + b.astype(F32),),
                   [own.reshape(-1, d), recv.reshape(-1, d)], [BF16], rows_pref=5 * GRAD_ROW_TILE)[0]
        pair = pair.reshape(4, rows_all, d)
        pending[tag] = (pair, _chips_start("grad_exchange_chips_start_" + tag, pair), [b.shape[1] for b in blocks])
        return pending[tag][1]["token"]

    def reduce_finish(tag, after):
        _, started, rows = pending[tag]
        (pair,), (recv,) = _split_wait("grad_exchange_chips_wait_" + tag, started, after)
        return _chip_sum("grad_chip_sum_" + tag, pair, recv, chip), rows

    def ev(name, gb=None, gs=None, loss=None, marker=None):
        if name in gathers:
            names, started = gathers[name]
            for n, f in zip(names, _split_wait("weight_gather_wait_" + name, started, marker)[1]):
                wt[n] = full_form(n, f)
        elif name == "grads_main":
            return reduce_start("main", MAIN_BIG, gb)
        elif name == "small_early":
            pending["small"] = _slots_start("small_gather_start", _pack_small([gs[n] for n in EARLY_SMALL] + [loss[:, :1]]))
            return pending["small"]["token"]
        elif name == "grads_ffn1":
            return reduce_start("ffn1", FFN1_BIG, gb)
        return None

    _, dx, _, gs = _local_step(x2, mem2, tgt2, wt, sm, ev)

    grads = {}
    for tag, names in (("main", MAIN_BIG), ("ffn1", FFN1_BIG)):
        g_rows, rows = reduce_finish(tag, dx)
        off = 0
        for n, r in zip(names, rows):
            shard = wts[n].shape
            grads[n] = g_rows[off:off + r].reshape((shard[2], shard[1]) if n in COL_SHARDED else shard[1:])
            off += r
    small_sum = _sum_slots("small_sum", _split_wait("small_gather_wait", pending["small"], dx)[1][0], F32)
    late = _allgather("small_allgather_late", [gs[LATE_SMALL].reshape(-1, LANES)])[0]
    late_sum = _sum_slots("small_sum_late", late.reshape(N_DEV, -1, LANES), F32)
    vals = _unpack_small(small_sum, [wts[n].shape for n in EARLY_SMALL] + [(1, 1)])
    total_loss = vals[-1].reshape(())
    def flat(n, a):
        a = a.reshape(wts[n].shape)
        a = jnp.swapaxes(a, -1, -2) if n in MINOR_SWAPPED else a
        return a.reshape(-1, a.shape[-1])

    def unflat(n, a):
        shape = wts[n].shape
        if n in MINOR_SWAPPED:
            return jnp.swapaxes(a.reshape(shape[:-2] + (shape[-1], shape[-2])), -1, -2)
        return a.reshape(shape)

    for n, g_full in zip(EARLY_SMALL + [LATE_SMALL], vals[:-1] + [late_sum]):
        grads[n] = flat(n, g_full)

    out_g, out_d, out_m, out_v = {}, {}, {}, {}
    by_shape = {}
    for n in WEIGHTS:
        by_shape.setdefault((flat(n, wts[n]).shape, n in COL_SHARDED), []).append(n)
    for (_, transposed), names in by_shape.items():
        items = [(flat(n, wts[n]), grads[n], flat(n, moms[n][0]), flat(n, moms[n][1])) for n in names]
        for n, res in zip(names, _adamw_group("adamw_" + names[0], items, transposed)):
            out_d[n], out_m[n], out_v[n], out_g[n] = (unflat(n, a) for a in res)

    return (total_loss, dx[None], *[out_g[n] for n in WEIGHTS], *[out_d[n] for n in WEIGHTS],
            *[out_m[n] for n in WEIGHTS], *[out_v[n] for n in WEIGHTS])
```
